```python
import math
import jax, jax.numpy as jnp
from jax import lax
import numpy as np

D_MODEL = 1024
BATCH = 8
SEQ = 4096
DEPTH = 4

N_MIXERS = 3
N_META = 16
EPS = 1e-6
S5_WIDTH = D_MODEL
S5_GROUP = 16
S5_GROUPS = S5_WIDTH // S5_GROUP
S5_STATE = 64
DT_MIN = 1e-3
DT_MAX = 1e-1
CONV_E = 2 * D_MODEL
CONV_K = 3
POOL_E = 2 * D_MODEL
POOL_WINDOWS = (2, 4, 8, 16)
POOL_GROUP = POOL_E // len(POOL_WINDOWS)

kernel_name = "hybrid_s5_shortconv_pool_interleaved"


def rmsnorm(h, g):
    hf = h.astype(jnp.float32)
    y = hf * lax.rsqrt(jnp.mean(hf * hf, axis=-1, keepdims=True) + EPS)
    return (y * g.astype(jnp.float32)).astype(h.dtype)


def s5_branch(n, w_in, lam_re, lam_im, log_dt, b_re, b_im, c_re, c_im, d_skip, w_glu, b_glu, w_out):
    f32 = jnp.float32
    bsz, L, _ = n.shape
    u, z = jnp.split(n @ w_in, 2, axis=-1)
    uf = u.astype(f32).reshape(bsz, L, S5_GROUPS, S5_GROUP)
    lr = lam_re.astype(f32)
    li = lam_im.astype(f32)
    dt = jnp.exp(log_dt.astype(f32))[:, None]
    mag = jnp.exp(lr * dt)
    ar = mag * jnp.cos(li * dt)
    ai = mag * jnp.sin(li * dt)
    den = lr * lr + li * li
    kr = ((ar - 1.0) * lr + ai * li) / den
    ki = (ai * lr - (ar - 1.0) * li) / den
    br = b_re.astype(f32)
    bi = b_im.astype(f32)
    bbr = kr[..., None] * br - ki[..., None] * bi
    bbi = kr[..., None] * bi + ki[..., None] * br
    xr = jnp.einsum("blgi,gpi->blgp", uf, bbr)
    xi = jnp.einsum("blgi,gpi->blgp", uf, bbi)
    a_r = jnp.broadcast_to(ar[None, None], (1, L, S5_GROUPS, S5_STATE))
    a_i = jnp.broadcast_to(ai[None, None], (1, L, S5_GROUPS, S5_STATE))

    def combine(e1, e2):
        a1r, a1i, b1r, b1i = e1
        a2r, a2i, b2r, b2i = e2
        return (a2r * a1r - a2i * a1i,
                a2r * a1i + a2i * a1r,
                a2r * b1r - a2i * b1i + b2r,
                a2r * b1i + a2i * b1r + b2i)

    _, _, sr, si = lax.associative_scan(combine, (a_r, a_i, xr, xi), axis=1)
    y = (jnp.einsum("blgp,gip->blgi", sr, c_re.astype(f32))
         - jnp.einsum("blgp,gip->blgi", si, c_im.astype(f32))
         + d_skip.astype(f32).reshape(S5_GROUPS, S5_GROUP) * uf)
    y = jax.nn.gelu(y.reshape(bsz, L, S5_WIDTH))
    y = y * jax.nn.sigmoid(y @ w_glu.astype(f32) + b_glu.astype(f32))
    y = y.astype(n.dtype) * jax.nn.silu(z)
    return y @ w_out


def shortconv_branch(n, w_in, conv_w, conv_b, w_out):
    bg, cg, v, z = jnp.split(n @ w_in, 4, axis=-1)
    hc = cg * v
    conv = lax.conv_general_dilated(
        hc, conv_w[:, None, :], window_strides=(1,), padding=[(CONV_K - 1, 0)],
        dimension_numbers=("NWC", "WIO", "NWC"), feature_group_count=CONV_E) + conv_b
    y = bg * conv
    return (y * jax.nn.silu(z)) @ w_out


def pool_branch(n, w_in, w_grp, b_grp, scale, w_out):
    f32 = jnp.float32
    bsz, L, _ = n.shape
    u, z = jnp.split(n @ w_in, 2, axis=-1)
    ug = u.astype(f32).reshape(bsz, L, len(POOL_WINDOWS), POOL_GROUP)
    cs = jnp.cumsum(ug, axis=1)
    t = jnp.arange(1, L + 1, dtype=f32)[:, None]
    outs = []
    for k, w in enumerate(POOL_WINDOWS):
        c = cs[:, :, k]
        lag = jnp.concatenate([jnp.zeros_like(c[:, :w]), c[:, :L - w]], axis=1)
        mixed = (c - lag) / jnp.minimum(t, float(w)) - ug[:, :, k]
        outs.append(mixed @ w_grp[k].astype(f32) + b_grp[k].astype(f32))
    y = jnp.concatenate(outs, axis=-1) * scale.astype(f32)
    y = y.astype(n.dtype) * jax.nn.silu(z)
    return y @ w_out


def _normal(key, shape, std):
    return jax.random.normal(key, shape, jnp.float32) * std


def _s5_params(key, p):
    ks = jax.random.split(key, 12)
    n_idx = jnp.arange(S5_STATE, dtype=jnp.float32)
    return {
        p + "w_in": _normal(ks[0], (D_MODEL, 2 * S5_WIDTH), D_MODEL ** -0.5),
        p + "lam_re": -0.5 + _normal(ks[1], (S5_GROUPS, S5_STATE), 0.01),
        p + "lam_im": math.pi * n_idx[None, :] + _normal(ks[2], (S5_GROUPS, S5_STATE), 0.01),
        p + "log_dt": jax.random.uniform(ks[3], (S5_GROUPS,), jnp.float32,
                                         math.log(DT_MIN), math.log(DT_MAX)),
        p + "b_re": _normal(ks[4], (S5_GROUPS, S5_STATE, S5_GROUP), (2 * S5_GROUP) ** -0.5),
        p + "b_im": _normal(ks[5], (S5_GROUPS, S5_STATE, S5_GROUP), (2 * S5_GROUP) ** -0.5),
        p + "c_re": _normal(ks[6], (S5_GROUPS, S5_GROUP, S5_STATE), (2 * S5_STATE) ** -0.5),
        p + "c_im": _normal(ks[7], (S5_GROUPS, S5_GROUP, S5_STATE), (2 * S5_STATE) ** -0.5),
        p + "d_skip": _normal(ks[8], (S5_WIDTH,), 1.0),
        p + "w_glu": _normal(ks[9], (S5_WIDTH, S5_WIDTH), S5_WIDTH ** -0.5),
        p + "b_glu": _normal(ks[10], (S5_WIDTH,), 0.01),
        p + "w_out": _normal(ks[11], (S5_WIDTH, D_MODEL), S5_WIDTH ** -0.5),
    }


def _conv_params(key, p):
    ks = jax.random.split(key, 4)
    return {
        p + "w_in": _normal(ks[0], (D_MODEL, 4 * CONV_E), D_MODEL ** -0.5),
        p + "conv_w": _normal(ks[1], (CONV_K, CONV_E), CONV_K ** -0.5),
        p + "conv_b": _normal(ks[2], (CONV_E,), 0.01),
        p + "w_out": _normal(ks[3], (CONV_E, D_MODEL), CONV_E ** -0.5),
    }


def _pool_params(key, p):
    ks = jax.random.split(key, 5)
    ng = len(POOL_WINDOWS)
    return {
        p + "w_in": _normal(ks[0], (D_MODEL, 2 * POOL_E), D_MODEL ** -0.5),
        p + "w_grp": _normal(ks[1], (ng, POOL_GROUP, POOL_GROUP), POOL_GROUP ** -0.5),
        p + "b_grp": _normal(ks[2], (ng, POOL_GROUP), 0.01),
        p + "scale": 1.0 + _normal(ks[3], (POOL_E,), 0.02),
        p + "w_out": _normal(ks[4], (POOL_E, D_MODEL), POOL_E ** -0.5),
    }


def _fwd_setup_inputs(seed: int = 0) -> dict:
    key = jax.random.key(seed)
    ks = jax.random.split(key, 3 + 2 * DEPTH)
    out = {
        "x": _normal(ks[0], (BATCH, SEQ, D_MODEL), 1.0),
        "meta_tokens": _normal(ks[1], (N_META, D_MODEL), 1.0),
    }
    builders = (_s5_params, _conv_params, _pool_params)
    for i in range(DEPTH):
        out["norm%d_g" % i] = 1.0 + _normal(ks[3 + 2 * i], (D_MODEL,), 0.02)
        out.update(builders[i % N_MIXERS](ks[4 + 2 * i], "l%d_" % i))
    out["final_g"] = 1.0 + _normal(ks[2], (D_MODEL,), 0.02)
    return out


def _fwd_reference(x, meta_tokens,
              norm0_g, l0_w_in, l0_lam_re, l0_lam_im, l0_log_dt, l0_b_re, l0_b_im, l0_c_re, l0_c_im,
              l0_d_skip, l0_w_glu, l0_b_glu, l0_w_out,
              norm1_g, l1_w_in, l1_conv_w, l1_conv_b, l1_w_out,
              norm2_g, l2_w_in, l2_w_grp, l2_b_grp, l2_scale, l2_w_out,
              norm3_g, l3_w_in, l3_lam_re, l3_lam_im, l3_log_dt, l3_b_re, l3_b_im, l3_c_re, l3_c_im,
              l3_d_skip, l3_w_glu, l3_b_glu, l3_w_out,
              final_g):
    bsz = x.shape[0]
    meta = jnp.broadcast_to(meta_tokens[None].astype(x.dtype), (bsz, N_META, D_MODEL))
    h = jnp.concatenate([meta, x], axis=1)
    layers = [
        (norm0_g, (l0_w_in, l0_lam_re, l0_lam_im, l0_log_dt, l0_b_re, l0_b_im, l0_c_re, l0_c_im,
                   l0_d_skip, l0_w_glu, l0_b_glu, l0_w_out)),
        (norm1_g, (l1_w_in, l1_conv_w, l1_conv_b, l1_w_out)),
        (norm2_g, (l2_w_in, l2_w_grp, l2_b_grp, l2_scale, l2_w_out)),
        (norm3_g, (l3_w_in, l3_lam_re, l3_lam_im, l3_log_dt, l3_b_re, l3_b_im, l3_c_re, l3_c_im,
                   l3_d_skip, l3_w_glu, l3_b_glu, l3_w_out)),
    ]
    mixers = (s5_branch, shortconv_branch, pool_branch)
    for i in range(DEPTH):
        g, params = layers[i]
        h = h + mixers[i % N_MIXERS](rmsnorm(h, g), *params)
    return rmsnorm(h[:, N_META:], final_g)


import jax as _jax
import jax.numpy as _jnp

TWIN_FORMAT = 'train_step'
FWD_PARAMS = ['x', 'meta_tokens', 'norm0_g', 'l0_w_in', 'l0_lam_re', 'l0_lam_im', 'l0_log_dt', 'l0_b_re', 'l0_b_im', 'l0_c_re', 'l0_c_im', 'l0_d_skip', 'l0_w_glu', 'l0_b_glu', 'l0_w_out', 'norm1_g', 'l1_w_in', 'l1_conv_w', 'l1_conv_b', 'l1_w_out', 'norm2_g', 'l2_w_in', 'l2_w_grp', 'l2_b_grp', 'l2_scale', 'l2_w_out', 'norm3_g', 'l3_w_in', 'l3_lam_re', 'l3_lam_im', 'l3_log_dt', 'l3_b_re', 'l3_b_im', 'l3_c_re', 'l3_c_im', 'l3_d_skip', 'l3_w_glu', 'l3_b_glu', 'l3_w_out', 'final_g']
TWIN_WEIGHTS = ['meta_tokens', 'norm0_g', 'l0_w_in', 'l0_lam_re', 'l0_lam_im', 'l0_log_dt', 'l0_b_re', 'l0_b_im', 'l0_c_re', 'l0_c_im', 'l0_d_skip', 'l0_w_glu', 'l0_b_glu', 'l0_w_out', 'norm1_g', 'l1_w_in', 'l1_conv_w', 'l1_conv_b', 'l1_w_out', 'norm2_g', 'l2_w_in', 'l2_w_grp', 'l2_b_grp', 'l2_scale', 'l2_w_out', 'norm3_g', 'l3_w_in', 'l3_lam_re', 'l3_lam_im', 'l3_log_dt', 'l3_b_re', 'l3_b_im', 'l3_c_re', 'l3_c_im', 'l3_d_skip', 'l3_w_glu', 'l3_b_glu', 'l3_w_out', 'final_g']
TWIN_DIFF_INPUT = 'x'
TWIN_INPUTS = ['x', 'meta_tokens', 'norm0_g', 'l0_w_in', 'l0_lam_re', 'l0_lam_im', 'l0_log_dt', 'l0_b_re', 'l0_b_im', 'l0_c_re', 'l0_c_im', 'l0_d_skip', 'l0_w_glu', 'l0_b_glu', 'l0_w_out', 'norm1_g', 'l1_w_in', 'l1_conv_w', 'l1_conv_b', 'l1_w_out', 'norm2_g', 'l2_w_in', 'l2_w_grp', 'l2_b_grp', 'l2_scale', 'l2_w_out', 'norm3_g', 'l3_w_in', 'l3_lam_re', 'l3_lam_im', 'l3_log_dt', 'l3_b_re', 'l3_b_im', 'l3_c_re', 'l3_c_im', 'l3_d_skip', 'l3_w_glu', 'l3_b_glu', 'l3_w_out', 'final_g', 'loss_target', 'm_meta_tokens', 'm_norm0_g', 'm_l0_w_in', 'm_l0_lam_re', 'm_l0_lam_im', 'm_l0_log_dt', 'm_l0_b_re', 'm_l0_b_im', 'm_l0_c_re', 'm_l0_c_im', 'm_l0_d_skip', 'm_l0_w_glu', 'm_l0_b_glu', 'm_l0_w_out', 'm_norm1_g', 'm_l1_w_in', 'm_l1_conv_w', 'm_l1_conv_b', 'm_l1_w_out', 'm_norm2_g', 'm_l2_w_in', 'm_l2_w_grp', 'm_l2_b_grp', 'm_l2_scale', 'm_l2_w_out', 'm_norm3_g', 'm_l3_w_in', 'm_l3_lam_re', 'm_l3_lam_im', 'm_l3_log_dt', 'm_l3_b_re', 'm_l3_b_im', 'm_l3_c_re', 'm_l3_c_im', 'm_l3_d_skip', 'm_l3_w_glu', 'm_l3_b_glu', 'm_l3_w_out', 'm_final_g', 'v_meta_tokens', 'v_norm0_g', 'v_l0_w_in', 'v_l0_lam_re', 'v_l0_lam_im', 'v_l0_log_dt', 'v_l0_b_re', 'v_l0_b_im', 'v_l0_c_re', 'v_l0_c_im', 'v_l0_d_skip', 'v_l0_w_glu', 'v_l0_b_glu', 'v_l0_w_out', 'v_norm1_g', 'v_l1_w_in', 'v_l1_conv_w', 'v_l1_conv_b', 'v_l1_w_out', 'v_norm2_g', 'v_l2_w_in', 'v_l2_w_grp', 'v_l2_b_grp', 'v_l2_scale', 'v_l2_w_out', 'v_norm3_g', 'v_l3_w_in', 'v_l3_lam_re', 'v_l3_lam_im', 'v_l3_log_dt', 'v_l3_b_re', 'v_l3_b_im', 'v_l3_c_re', 'v_l3_c_im', 'v_l3_d_skip', 'v_l3_w_glu', 'v_l3_b_glu', 'v_l3_w_out', 'v_final_g']
TWIN_OUTPUTS = ['loss', 'grad_x', 'grad_meta_tokens', 'grad_norm0_g', 'grad_l0_w_in', 'grad_l0_lam_re', 'grad_l0_lam_im', 'grad_l0_log_dt', 'grad_l0_b_re', 'grad_l0_b_im', 'grad_l0_c_re', 'grad_l0_c_im', 'grad_l0_d_skip', 'grad_l0_w_glu', 'grad_l0_b_glu', 'grad_l0_w_out', 'grad_norm1_g', 'grad_l1_w_in', 'grad_l1_conv_w', 'grad_l1_conv_b', 'grad_l1_w_out', 'grad_norm2_g', 'grad_l2_w_in', 'grad_l2_w_grp', 'grad_l2_b_grp', 'grad_l2_scale', 'grad_l2_w_out', 'grad_norm3_g', 'grad_l3_w_in', 'grad_l3_lam_re', 'grad_l3_lam_im', 'grad_l3_log_dt', 'grad_l3_b_re', 'grad_l3_b_im', 'grad_l3_c_re', 'grad_l3_c_im', 'grad_l3_d_skip', 'grad_l3_w_glu', 'grad_l3_b_glu', 'grad_l3_w_out', 'grad_final_g', 'delta_meta_tokens', 'delta_norm0_g', 'delta_l0_w_in', 'delta_l0_lam_re', 'delta_l0_lam_im', 'delta_l0_log_dt', 'delta_l0_b_re', 'delta_l0_b_im', 'delta_l0_c_re', 'delta_l0_c_im', 'delta_l0_d_skip', 'delta_l0_w_glu', 'delta_l0_b_glu', 'delta_l0_w_out', 'delta_norm1_g', 'delta_l1_w_in', 'delta_l1_conv_w', 'delta_l1_conv_b', 'delta_l1_w_out', 'delta_norm2_g', 'delta_l2_w_in', 'delta_l2_w_grp', 'delta_l2_b_grp', 'delta_l2_scale', 'delta_l2_w_out', 'delta_norm3_g', 'delta_l3_w_in', 'delta_l3_lam_re', 'delta_l3_lam_im', 'delta_l3_log_dt', 'delta_l3_b_re', 'delta_l3_b_im', 'delta_l3_c_re', 'delta_l3_c_im', 'delta_l3_d_skip', 'delta_l3_w_glu', 'delta_l3_b_glu', 'delta_l3_w_out', 'delta_final_g', 'new_m_meta_tokens', 'new_m_norm0_g', 'new_m_l0_w_in', 'new_m_l0_lam_re', 'new_m_l0_lam_im', 'new_m_l0_log_dt', 'new_m_l0_b_re', 'new_m_l0_b_im', 'new_m_l0_c_re', 'new_m_l0_c_im', 'new_m_l0_d_skip', 'new_m_l0_w_glu', 'new_m_l0_b_glu', 'new_m_l0_w_out', 'new_m_norm1_g', 'new_m_l1_w_in', 'new_m_l1_conv_w', 'new_m_l1_conv_b', 'new_m_l1_w_out', 'new_m_norm2_g', 'new_m_l2_w_in', 'new_m_l2_w_grp', 'new_m_l2_b_grp', 'new_m_l2_scale', 'new_m_l2_w_out', 'new_m_norm3_g', 'new_m_l3_w_in', 'new_m_l3_lam_re', 'new_m_l3_lam_im', 'new_m_l3_log_dt', 'new_m_l3_b_re', 'new_m_l3_b_im', 'new_m_l3_c_re', 'new_m_l3_c_im', 'new_m_l3_d_skip', 'new_m_l3_w_glu', 'new_m_l3_b_glu', 'new_m_l3_w_out', 'new_m_final_g', 'new_v_meta_tokens', 'new_v_norm0_g', 'new_v_l0_w_in', 'new_v_l0_lam_re', 'new_v_l0_lam_im', 'new_v_l0_log_dt', 'new_v_l0_b_re', 'new_v_l0_b_im', 'new_v_l0_c_re', 'new_v_l0_c_im', 'new_v_l0_d_skip', 'new_v_l0_w_glu', 'new_v_l0_b_glu', 'new_v_l0_w_out', 'new_v_norm1_g', 'new_v_l1_w_in', 'new_v_l1_conv_w', 'new_v_l1_conv_b', 'new_v_l1_w_out', 'new_v_norm2_g', 'new_v_l2_w_in', 'new_v_l2_w_grp', 'new_v_l2_b_grp', 'new_v_l2_scale', 'new_v_l2_w_out', 'new_v_norm3_g', 'new_v_l3_w_in', 'new_v_l3_lam_re', 'new_v_l3_lam_im', 'new_v_l3_log_dt', 'new_v_l3_b_re', 'new_v_l3_b_im', 'new_v_l3_c_re', 'new_v_l3_c_im', 'new_v_l3_d_skip', 'new_v_l3_w_glu', 'new_v_l3_b_glu', 'new_v_l3_w_out', 'new_v_final_g']
TWIN_LEAF_KINDS = {'loss': 'loss', 'grad_x': 'grad_x', 'grad_meta_tokens': 'grad_w', 'grad_norm0_g': 'grad_w', 'grad_l0_w_in': 'grad_w', 'grad_l0_lam_re': 'grad_w', 'grad_l0_lam_im': 'grad_w', 'grad_l0_log_dt': 'grad_w', 'grad_l0_b_re': 'grad_w', 'grad_l0_b_im': 'grad_w', 'grad_l0_c_re': 'grad_w', 'grad_l0_c_im': 'grad_w', 'grad_l0_d_skip': 'grad_w', 'grad_l0_w_glu': 'grad_w', 'grad_l0_b_glu': 'grad_w', 'grad_l0_w_out': 'grad_w', 'grad_norm1_g': 'grad_w', 'grad_l1_w_in': 'grad_w', 'grad_l1_conv_w': 'grad_w', 'grad_l1_conv_b': 'grad_w', 'grad_l1_w_out': 'grad_w', 'grad_norm2_g': 'grad_w', 'grad_l2_w_in': 'grad_w', 'grad_l2_w_grp': 'grad_w', 'grad_l2_b_grp': 'grad_w', 'grad_l2_scale': 'grad_w', 'grad_l2_w_out': 'grad_w', 'grad_norm3_g': 'grad_w', 'grad_l3_w_in': 'grad_w', 'grad_l3_lam_re': 'grad_w', 'grad_l3_lam_im': 'grad_w', 'grad_l3_log_dt': 'grad_w', 'grad_l3_b_re': 'grad_w', 'grad_l3_b_im': 'grad_w', 'grad_l3_c_re': 'grad_w', 'grad_l3_c_im': 'grad_w', 'grad_l3_d_skip': 'grad_w', 'grad_l3_w_glu': 'grad_w', 'grad_l3_b_glu': 'grad_w', 'grad_l3_w_out': 'grad_w', 'grad_final_g': 'grad_w', 'delta_meta_tokens': 'delta_w', 'delta_norm0_g': 'delta_w', 'delta_l0_w_in': 'delta_w', 'delta_l0_lam_re': 'delta_w', 'delta_l0_lam_im': 'delta_w', 'delta_l0_log_dt': 'delta_w', 'delta_l0_b_re': 'delta_w', 'delta_l0_b_im': 'delta_w', 'delta_l0_c_re': 'delta_w', 'delta_l0_c_im': 'delta_w', 'delta_l0_d_skip': 'delta_w', 'delta_l0_w_glu': 'delta_w', 'delta_l0_b_glu': 'delta_w', 'delta_l0_w_out': 'delta_w', 'delta_norm1_g': 'delta_w', 'delta_l1_w_in': 'delta_w', 'delta_l1_conv_w': 'delta_w', 'delta_l1_conv_b': 'delta_w', 'delta_l1_w_out': 'delta_w', 'delta_norm2_g': 'delta_w', 'delta_l2_w_in': 'delta_w', 'delta_l2_w_grp': 'delta_w', 'delta_l2_b_grp': 'delta_w', 'delta_l2_scale': 'delta_w', 'delta_l2_w_out': 'delta_w', 'delta_norm3_g': 'delta_w', 'delta_l3_w_in': 'delta_w', 'delta_l3_lam_re': 'delta_w', 'delta_l3_lam_im': 'delta_w', 'delta_l3_log_dt': 'delta_w', 'delta_l3_b_re': 'delta_w', 'delta_l3_b_im': 'delta_w', 'delta_l3_c_re': 'delta_w', 'delta_l3_c_im': 'delta_w', 'delta_l3_d_skip': 'delta_w', 'delta_l3_w_glu': 'delta_w', 'delta_l3_b_glu': 'delta_w', 'delta_l3_w_out': 'delta_w', 'delta_final_g': 'delta_w', 'new_m_meta_tokens': 'new_m', 'new_m_norm0_g': 'new_m', 'new_m_l0_w_in': 'new_m', 'new_m_l0_lam_re': 'new_m', 'new_m_l0_lam_im': 'new_m', 'new_m_l0_log_dt': 'new_m', 'new_m_l0_b_re': 'new_m', 'new_m_l0_b_im': 'new_m', 'new_m_l0_c_re': 'new_m', 'new_m_l0_c_im': 'new_m', 'new_m_l0_d_skip': 'new_m', 'new_m_l0_w_glu': 'new_m', 'new_m_l0_b_glu': 'new_m', 'new_m_l0_w_out': 'new_m', 'new_m_norm1_g': 'new_m', 'new_m_l1_w_in': 'new_m', 'new_m_l1_conv_w': 'new_m', 'new_m_l1_conv_b': 'new_m', 'new_m_l1_w_out': 'new_m', 'new_m_norm2_g': 'new_m', 'new_m_l2_w_in': 'new_m', 'new_m_l2_w_grp': 'new_m', 'new_m_l2_b_grp': 'new_m', 'new_m_l2_scale': 'new_m', 'new_m_l2_w_out': 'new_m', 'new_m_norm3_g': 'new_m', 'new_m_l3_w_in': 'new_m', 'new_m_l3_lam_re': 'new_m', 'new_m_l3_lam_im': 'new_m', 'new_m_l3_log_dt': 'new_m', 'new_m_l3_b_re': 'new_m', 'new_m_l3_b_im': 'new_m', 'new_m_l3_c_re': 'new_m', 'new_m_l3_c_im': 'new_m', 'new_m_l3_d_skip': 'new_m', 'new_m_l3_w_glu': 'new_m', 'new_m_l3_b_glu': 'new_m', 'new_m_l3_w_out': 'new_m', 'new_m_final_g': 'new_m', 'new_v_meta_tokens': 'new_v', 'new_v_norm0_g': 'new_v', 'new_v_l0_w_in': 'new_v', 'new_v_l0_lam_re': 'new_v', 'new_v_l0_lam_im': 'new_v', 'new_v_l0_log_dt': 'new_v', 'new_v_l0_b_re': 'new_v', 'new_v_l0_b_im': 'new_v', 'new_v_l0_c_re': 'new_v', 'new_v_l0_c_im': 'new_v', 'new_v_l0_d_skip': 'new_v', 'new_v_l0_w_glu': 'new_v', 'new_v_l0_b_glu': 'new_v', 'new_v_l0_w_out': 'new_v', 'new_v_norm1_g': 'new_v', 'new_v_l1_w_in': 'new_v', 'new_v_l1_conv_w': 'new_v', 'new_v_l1_conv_b': 'new_v', 'new_v_l1_w_out': 'new_v', 'new_v_norm2_g': 'new_v', 'new_v_l2_w_in': 'new_v', 'new_v_l2_w_grp': 'new_v', 'new_v_l2_b_grp': 'new_v', 'new_v_l2_scale': 'new_v', 'new_v_l2_w_out': 'new_v', 'new_v_norm3_g': 'new_v', 'new_v_l3_w_in': 'new_v', 'new_v_l3_lam_re': 'new_v', 'new_v_l3_lam_im': 'new_v', 'new_v_l3_log_dt': 'new_v', 'new_v_l3_b_re': 'new_v', 'new_v_l3_b_im': 'new_v', 'new_v_l3_c_re': 'new_v', 'new_v_l3_c_im': 'new_v', 'new_v_l3_d_skip': 'new_v', 'new_v_l3_w_glu': 'new_v', 'new_v_l3_b_glu': 'new_v', 'new_v_l3_w_out': 'new_v', 'new_v_final_g': 'new_v'}


def _forward(args):
    return _fwd_reference(*[args[k] for k in FWD_PARAMS])


def _output_shape():
    out = _jax.eval_shape(lambda: _forward(_fwd_setup_inputs(0)))
    return out.shape, out.dtype

N_MICROBATCH = 1
ADAM_LR = 0.001
ADAM_B1 = 0.9
ADAM_B2 = 0.999
ADAM_EPS = 1e-08
ADAM_WD = 0.01
ADAM_STEP = 10
PER_EXAMPLE_BATCH_AXIS = {'x': 0, 'loss_target': 0}
SHARED_INPUTS = []
_WEIGHT_DTYPES = {'meta_tokens': _jnp.float32, 'norm0_g': _jnp.float32, 'l0_w_in': _jnp.float32, 'l0_lam_re': _jnp.float32, 'l0_lam_im': _jnp.float32, 'l0_log_dt': _jnp.float32, 'l0_b_re': _jnp.float32, 'l0_b_im': _jnp.float32, 'l0_c_re': _jnp.float32, 'l0_c_im': _jnp.float32, 'l0_d_skip': _jnp.float32, 'l0_w_glu': _jnp.float32, 'l0_b_glu': _jnp.float32, 'l0_w_out': _jnp.float32, 'norm1_g': _jnp.float32, 'l1_w_in': _jnp.float32, 'l1_conv_w': _jnp.float32, 'l1_conv_b': _jnp.float32, 'l1_w_out': _jnp.float32, 'norm2_g': _jnp.float32, 'l2_w_in': _jnp.float32, 'l2_w_grp': _jnp.float32, 'l2_b_grp': _jnp.float32, 'l2_scale': _jnp.float32, 'l2_w_out': _jnp.float32, 'norm3_g': _jnp.float32, 'l3_w_in': _jnp.float32, 'l3_lam_re': _jnp.float32, 'l3_lam_im': _jnp.float32, 'l3_log_dt': _jnp.float32, 'l3_b_re': _jnp.float32, 'l3_b_im': _jnp.float32, 'l3_c_re': _jnp.float32, 'l3_c_im': _jnp.float32, 'l3_d_skip': _jnp.float32, 'l3_w_glu': _jnp.float32, 'l3_b_glu': _jnp.float32, 'l3_w_out': _jnp.float32, 'final_g': _jnp.float32}
MOMENT_SCALE = {'meta_tokens': 2.347543e-03, 'norm0_g': 7.955675e-02, 'l0_w_in': 5.660083e-02, 'l0_lam_re': 3.068966e-03, 'l0_lam_im': 2.768014e-03, 'l0_log_dt': 2.362331e+00, 'l0_b_re': 1.778346e-03, 'l0_b_im': 1.767152e-03, 'l0_c_re': 3.550124e-03, 'l0_c_im': 3.577773e-03, 'l0_d_skip': 6.035771e-02, 'l0_w_glu': 1.719597e-02, 'l0_b_glu': 3.078605e-02, 'l0_w_out': 5.400041e-02, 'norm1_g': 1.999278e-01, 'l1_w_in': 7.065557e-02, 'l1_conv_w': 7.327814e-02, 'l1_conv_b': 7.119716e-02, 'l1_w_out': 9.846800e-02, 'norm2_g': 1.092607e-01, 'l2_w_in': 5.207243e-02, 'l2_w_grp': 5.149129e-02, 'l2_b_grp': 5.799075e-02, 'l2_scale': 5.182771e-02, 'l2_w_out': 7.245167e-02, 'norm3_g': 4.222641e-02, 'l3_w_in': 2.968085e-02, 'l3_lam_re': 1.591094e-03, 'l3_lam_im': 1.634554e-03, 'l3_log_dt': 9.715484e-01, 'l3_b_re': 9.716849e-04, 'l3_b_im': 9.690803e-04, 'l3_c_re': 2.003158e-03, 'l3_c_im': 1.908270e-03, 'l3_d_skip': 3.157341e-02, 'l3_w_glu': 8.379649e-03, 'l3_b_glu': 1.284899e-02, 'l3_w_out': 2.852457e-02, 'final_g': 3.200758e+01}


def _to_microbatches(a, axis):
    t = _jnp.moveaxis(a, axis, 0)
    t = t.reshape((N_MICROBATCH, t.shape[0] // N_MICROBATCH) + t.shape[1:])
    return _jnp.moveaxis(t, 1, axis + 1)


def setup_inputs(seed: int = 0) -> dict:
    inp = _fwd_setup_inputs(seed)
    key = _jax.random.fold_in(_jax.random.key(seed), 7919)
    shape, _ = _output_shape()
    out = dict(inp)
    out["loss_target"] = _jax.random.normal(_jax.random.fold_in(key, 0), shape, _jnp.float32)
    for i, name in enumerate(TWIN_WEIGHTS):
        w = inp[name].astype(_jnp.float32)
        if MOMENT_SCALE is None:
            s = _jnp.sqrt(_jnp.mean(_jnp.square(w)) + 1e-30)
        else:
            s = MOMENT_SCALE[name]
        km, kv = _jax.random.split(_jax.random.fold_in(key, i + 1))
        out[name] = w
        out["m_" + name] = s * _jax.random.normal(km, w.shape, _jnp.float32)
        out["v_" + name] = (s * s) * _jax.random.uniform(kv, w.shape, _jnp.float32, 0.5, 1.5)
    if N_MICROBATCH > 1:
        for name, axis in PER_EXAMPLE_BATCH_AXIS.items():
            out[name] = _to_microbatches(out[name], axis)
    return {'x': out['x'], 'meta_tokens': out['meta_tokens'], 'norm0_g': out['norm0_g'], 'l0_w_in': out['l0_w_in'], 'l0_lam_re': out['l0_lam_re'], 'l0_lam_im': out['l0_lam_im'], 'l0_log_dt': out['l0_log_dt'], 'l0_b_re': out['l0_b_re'], 'l0_b_im': out['l0_b_im'], 'l0_c_re': out['l0_c_re'], 'l0_c_im': out['l0_c_im'], 'l0_d_skip': out['l0_d_skip'], 'l0_w_glu': out['l0_w_glu'], 'l0_b_glu': out['l0_b_glu'], 'l0_w_out': out['l0_w_out'], 'norm1_g': out['norm1_g'], 'l1_w_in': out['l1_w_in'], 'l1_conv_w': out['l1_conv_w'], 'l1_conv_b': out['l1_conv_b'], 'l1_w_out': out['l1_w_out'], 'norm2_g': out['norm2_g'], 'l2_w_in': out['l2_w_in'], 'l2_w_grp': out['l2_w_grp'], 'l2_b_grp': out['l2_b_grp'], 'l2_scale': out['l2_scale'], 'l2_w_out': out['l2_w_out'], 'norm3_g': out['norm3_g'], 'l3_w_in': out['l3_w_in'], 'l3_lam_re': out['l3_lam_re'], 'l3_lam_im': out['l3_lam_im'], 'l3_log_dt': out['l3_log_dt'], 'l3_b_re': out['l3_b_re'], 'l3_b_im': out['l3_b_im'], 'l3_c_re': out['l3_c_re'], 'l3_c_im': out['l3_c_im'], 'l3_d_skip': out['l3_d_skip'], 'l3_w_glu': out['l3_w_glu'], 'l3_b_glu': out['l3_b_glu'], 'l3_w_out': out['l3_w_out'], 'final_g': out['final_g'], 'loss_target': out['loss_target'], 'm_meta_tokens': out['m_meta_tokens'], 'm_norm0_g': out['m_norm0_g'], 'm_l0_w_in': out['m_l0_w_in'], 'm_l0_lam_re': out['m_l0_lam_re'], 'm_l0_lam_im': out['m_l0_lam_im'], 'm_l0_log_dt': out['m_l0_log_dt'], 'm_l0_b_re': out['m_l0_b_re'], 'm_l0_b_im': out['m_l0_b_im'], 'm_l0_c_re': out['m_l0_c_re'], 'm_l0_c_im': out['m_l0_c_im'], 'm_l0_d_skip': out['m_l0_d_skip'], 'm_l0_w_glu': out['m_l0_w_glu'], 'm_l0_b_glu': out['m_l0_b_glu'], 'm_l0_w_out': out['m_l0_w_out'], 'm_norm1_g': out['m_norm1_g'], 'm_l1_w_in': out['m_l1_w_in'], 'm_l1_conv_w': out['m_l1_conv_w'], 'm_l1_conv_b': out['m_l1_conv_b'], 'm_l1_w_out': out['m_l1_w_out'], 'm_norm2_g': out['m_norm2_g'], 'm_l2_w_in': out['m_l2_w_in'], 'm_l2_w_grp': out['m_l2_w_grp'], 'm_l2_b_grp': out['m_l2_b_grp'], 'm_l2_scale': out['m_l2_scale'], 'm_l2_w_out': out['m_l2_w_out'], 'm_norm3_g': out['m_norm3_g'], 'm_l3_w_in': out['m_l3_w_in'], 'm_l3_lam_re': out['m_l3_lam_re'], 'm_l3_lam_im': out['m_l3_lam_im'], 'm_l3_log_dt': out['m_l3_log_dt'], 'm_l3_b_re': out['m_l3_b_re'], 'm_l3_b_im': out['m_l3_b_im'], 'm_l3_c_re': out['m_l3_c_re'], 'm_l3_c_im': out['m_l3_c_im'], 'm_l3_d_skip': out['m_l3_d_skip'], 'm_l3_w_glu': out['m_l3_w_glu'], 'm_l3_b_glu': out['m_l3_b_glu'], 'm_l3_w_out': out['m_l3_w_out'], 'm_final_g': out['m_final_g'], 'v_meta_tokens': out['v_meta_tokens'], 'v_norm0_g': out['v_norm0_g'], 'v_l0_w_in': out['v_l0_w_in'], 'v_l0_lam_re': out['v_l0_lam_re'], 'v_l0_lam_im': out['v_l0_lam_im'], 'v_l0_log_dt': out['v_l0_log_dt'], 'v_l0_b_re': out['v_l0_b_re'], 'v_l0_b_im': out['v_l0_b_im'], 'v_l0_c_re': out['v_l0_c_re'], 'v_l0_c_im': out['v_l0_c_im'], 'v_l0_d_skip': out['v_l0_d_skip'], 'v_l0_w_glu': out['v_l0_w_glu'], 'v_l0_b_glu': out['v_l0_b_glu'], 'v_l0_w_out': out['v_l0_w_out'], 'v_norm1_g': out['v_norm1_g'], 'v_l1_w_in': out['v_l1_w_in'], 'v_l1_conv_w': out['v_l1_conv_w'], 'v_l1_conv_b': out['v_l1_conv_b'], 'v_l1_w_out': out['v_l1_w_out'], 'v_norm2_g': out['v_norm2_g'], 'v_l2_w_in': out['v_l2_w_in'], 'v_l2_w_grp': out['v_l2_w_grp'], 'v_l2_b_grp': out['v_l2_b_grp'], 'v_l2_scale': out['v_l2_scale'], 'v_l2_w_out': out['v_l2_w_out'], 'v_norm3_g': out['v_norm3_g'], 'v_l3_w_in': out['v_l3_w_in'], 'v_l3_lam_re': out['v_l3_lam_re'], 'v_l3_lam_im': out['v_l3_lam_im'], 'v_l3_log_dt': out['v_l3_log_dt'], 'v_l3_b_re': out['v_l3_b_re'], 'v_l3_b_im': out['v_l3_b_im'], 'v_l3_c_re': out['v_l3_c_re'], 'v_l3_c_im': out['v_l3_c_im'], 'v_l3_d_skip': out['v_l3_d_skip'], 'v_l3_w_glu': out['v_l3_w_glu'], 'v_l3_b_glu': out['v_l3_b_glu'], 'v_l3_w_out': out['v_l3_w_out'], 'v_final_g': out['v_final_g']}


def _loss(weights, diff, rest, loss_target):
    with _jax.named_scope("forward"):
        args = {**rest, TWIN_DIFF_INPUT: diff, **{k: w.astype(_WEIGHT_DTYPES[k]) for k, w in weights.items()}}
        y = _forward(args)
    with _jax.named_scope("loss_head"):
        err = _jnp.square(y.astype(_jnp.float32) - loss_target)
        return 0.5 * _jnp.sum(_jnp.mean(err, axis=-1)) if err.ndim else 0.5 * err


def _adamw(w, g, m, v):
    m = ADAM_B1 * m + (1.0 - ADAM_B1) * g
    v = ADAM_B2 * v + (1.0 - ADAM_B2) * _jnp.square(g)
    m_hat = m / (1.0 - ADAM_B1 ** ADAM_STEP)
    v_hat = v / (1.0 - ADAM_B2 ** ADAM_STEP)
    delta = -ADAM_LR * (m_hat / (_jnp.sqrt(v_hat) + ADAM_EPS) + ADAM_WD * w)
    return delta, m, v


def reference(x, meta_tokens, norm0_g, l0_w_in, l0_lam_re, l0_lam_im, l0_log_dt, l0_b_re, l0_b_im, l0_c_re, l0_c_im, l0_d_skip, l0_w_glu, l0_b_glu, l0_w_out, norm1_g, l1_w_in, l1_conv_w, l1_conv_b, l1_w_out, norm2_g, l2_w_in, l2_w_grp, l2_b_grp, l2_scale, l2_w_out, norm3_g, l3_w_in, l3_lam_re, l3_lam_im, l3_log_dt, l3_b_re, l3_b_im, l3_c_re, l3_c_im, l3_d_skip, l3_w_glu, l3_b_glu, l3_w_out, final_g, loss_target, m_meta_tokens, m_norm0_g, m_l0_w_in, m_l0_lam_re, m_l0_lam_im, m_l0_log_dt, m_l0_b_re, m_l0_b_im, m_l0_c_re, m_l0_c_im, m_l0_d_skip, m_l0_w_glu, m_l0_b_glu, m_l0_w_out, m_norm1_g, m_l1_w_in, m_l1_conv_w, m_l1_conv_b, m_l1_w_out, m_norm2_g, m_l2_w_in, m_l2_w_grp, m_l2_b_grp, m_l2_scale, m_l2_w_out, m_norm3_g, m_l3_w_in, m_l3_lam_re, m_l3_lam_im, m_l3_log_dt, m_l3_b_re, m_l3_b_im, m_l3_c_re, m_l3_c_im, m_l3_d_skip, m_l3_w_glu, m_l3_b_glu, m_l3_w_out, m_final_g, v_meta_tokens, v_norm0_g, v_l0_w_in, v_l0_lam_re, v_l0_lam_im, v_l0_log_dt, v_l0_b_re, v_l0_b_im, v_l0_c_re, v_l0_c_im, v_l0_d_skip, v_l0_w_glu, v_l0_b_glu, v_l0_w_out, v_norm1_g, v_l1_w_in, v_l1_conv_w, v_l1_conv_b, v_l1_w_out, v_norm2_g, v_l2_w_in, v_l2_w_grp, v_l2_b_grp, v_l2_scale, v_l2_w_out, v_norm3_g, v_l3_w_in, v_l3_lam_re, v_l3_lam_im, v_l3_log_dt, v_l3_b_re, v_l3_b_im, v_l3_c_re, v_l3_c_im, v_l3_d_skip, v_l3_w_glu, v_l3_b_glu, v_l3_w_out, v_final_g):
    given = dict(x=x, meta_tokens=meta_tokens, norm0_g=norm0_g, l0_w_in=l0_w_in, l0_lam_re=l0_lam_re, l0_lam_im=l0_lam_im, l0_log_dt=l0_log_dt, l0_b_re=l0_b_re, l0_b_im=l0_b_im, l0_c_re=l0_c_re, l0_c_im=l0_c_im, l0_d_skip=l0_d_skip, l0_w_glu=l0_w_glu, l0_b_glu=l0_b_glu, l0_w_out=l0_w_out, norm1_g=norm1_g, l1_w_in=l1_w_in, l1_conv_w=l1_conv_w, l1_conv_b=l1_conv_b, l1_w_out=l1_w_out, norm2_g=norm2_g, l2_w_in=l2_w_in, l2_w_grp=l2_w_grp, l2_b_grp=l2_b_grp, l2_scale=l2_scale, l2_w_out=l2_w_out, norm3_g=norm3_g, l3_w_in=l3_w_in, l3_lam_re=l3_lam_re, l3_lam_im=l3_lam_im, l3_log_dt=l3_log_dt, l3_b_re=l3_b_re, l3_b_im=l3_b_im, l3_c_re=l3_c_re, l3_c_im=l3_c_im, l3_d_skip=l3_d_skip, l3_w_glu=l3_w_glu, l3_b_glu=l3_b_glu, l3_w_out=l3_w_out, final_g=final_g, loss_target=loss_target, m_meta_tokens=m_meta_tokens, m_norm0_g=m_norm0_g, m_l0_w_in=m_l0_w_in, m_l0_lam_re=m_l0_lam_re, m_l0_lam_im=m_l0_lam_im, m_l0_log_dt=m_l0_log_dt, m_l0_b_re=m_l0_b_re, m_l0_b_im=m_l0_b_im, m_l0_c_re=m_l0_c_re, m_l0_c_im=m_l0_c_im, m_l0_d_skip=m_l0_d_skip, m_l0_w_glu=m_l0_w_glu, m_l0_b_glu=m_l0_b_glu, m_l0_w_out=m_l0_w_out, m_norm1_g=m_norm1_g, m_l1_w_in=m_l1_w_in, m_l1_conv_w=m_l1_conv_w, m_l1_conv_b=m_l1_conv_b, m_l1_w_out=m_l1_w_out, m_norm2_g=m_norm2_g, m_l2_w_in=m_l2_w_in, m_l2_w_grp=m_l2_w_grp, m_l2_b_grp=m_l2_b_grp, m_l2_scale=m_l2_scale, m_l2_w_out=m_l2_w_out, m_norm3_g=m_norm3_g, m_l3_w_in=m_l3_w_in, m_l3_lam_re=m_l3_lam_re, m_l3_lam_im=m_l3_lam_im, m_l3_log_dt=m_l3_log_dt, m_l3_b_re=m_l3_b_re, m_l3_b_im=m_l3_b_im, m_l3_c_re=m_l3_c_re, m_l3_c_im=m_l3_c_im, m_l3_d_skip=m_l3_d_skip, m_l3_w_glu=m_l3_w_glu, m_l3_b_glu=m_l3_b_glu, m_l3_w_out=m_l3_w_out, m_final_g=m_final_g, v_meta_tokens=v_meta_tokens, v_norm0_g=v_norm0_g, v_l0_w_in=v_l0_w_in, v_l0_lam_re=v_l0_lam_re, v_l0_lam_im=v_l0_lam_im, v_l0_log_dt=v_l0_log_dt, v_l0_b_re=v_l0_b_re, v_l0_b_im=v_l0_b_im, v_l0_c_re=v_l0_c_re, v_l0_c_im=v_l0_c_im, v_l0_d_skip=v_l0_d_skip, v_l0_w_glu=v_l0_w_glu, v_l0_b_glu=v_l0_b_glu, v_l0_w_out=v_l0_w_out, v_norm1_g=v_norm1_g, v_l1_w_in=v_l1_w_in, v_l1_conv_w=v_l1_conv_w, v_l1_conv_b=v_l1_conv_b, v_l1_w_out=v_l1_w_out, v_norm2_g=v_norm2_g, v_l2_w_in=v_l2_w_in, v_l2_w_grp=v_l2_w_grp, v_l2_b_grp=v_l2_b_grp, v_l2_scale=v_l2_scale, v_l2_w_out=v_l2_w_out, v_norm3_g=v_norm3_g, v_l3_w_in=v_l3_w_in, v_l3_lam_re=v_l3_lam_re, v_l3_lam_im=v_l3_lam_im, v_l3_log_dt=v_l3_log_dt, v_l3_b_re=v_l3_b_re, v_l3_b_im=v_l3_b_im, v_l3_c_re=v_l3_c_re, v_l3_c_im=v_l3_c_im, v_l3_d_skip=v_l3_d_skip, v_l3_w_glu=v_l3_w_glu, v_l3_b_glu=v_l3_b_glu, v_l3_w_out=v_l3_w_out, v_final_g=v_final_g)
    weights = {n: given[n] for n in TWIN_WEIGHTS}
    shared = {n: given[n] for n in SHARED_INPUTS}
    per_example = {n: given[n] for n in ['x']}
    grad_fn = _jax.value_and_grad(_loss, argnums=(0, 1))

    def one_microbatch(ex, loss_target):
        ex = dict(ex)
        diff = ex.pop(TWIN_DIFF_INPUT)
        return grad_fn(weights, diff, {**shared, **ex}, loss_target)

    if N_MICROBATCH == 1:
        loss, (grad_w, grad_x) = one_microbatch(per_example, given["loss_target"])
    else:
        def body(carry, xs):
            loss_sum, grad_sum = carry
            l_k, (gw_k, gx_k) = one_microbatch(xs[0], xs[1])
            with _jax.named_scope("update"):
                return (loss_sum + l_k, _jax.tree.map(_jnp.add, grad_sum, gw_k)), gx_k

        init = (_jnp.zeros((), _jnp.float32), _jax.tree.map(_jnp.zeros_like, weights))
        (loss, grad_w), grad_x = _jax.lax.scan(body, init, (per_example, given["loss_target"]))
    with _jax.named_scope("update"):
        delta_w, new_m, new_v = {}, {}, {}
        for n in TWIN_WEIGHTS:
            delta_w[n], new_m[n], new_v[n] = _adamw(weights[n], grad_w[n], given["m_" + n], given["v_" + n])
    return (loss, grad_x, *[grad_w[n] for n in TWIN_WEIGHTS], *[delta_w[n] for n in TWIN_WEIGHTS],
            *[new_m[n] for n in TWIN_WEIGHTS], *[new_v[n] for n in TWIN_WEIGHTS])
```

```python
import functools

import jax
import jax.numpy as jnp
from jax import lax
from jax.experimental import pallas as pl
from jax.experimental.pallas import tpu as pltpu

F32, BF16 = jnp.float32, jnp.bfloat16
MESH = pl.DeviceIdType.MESH

D_MODEL = 1024
N_META = 16
EPS = 1e-6
S5_GROUPS, S5_GROUP, S5_STATE = 64, 16, 64
LANE_BLOCK = 128
S5_BLOCKS = D_MODEL // LANE_BLOCK
GROUPS_PER_BLOCK = LANE_BLOCK // S5_GROUP
STATE_BLOCK = GROUPS_PER_BLOCK * S5_STATE
SCAN_ROWS = 256
CONV_E = 2048
POOL_E = 2048
POOL_WINDOWS = (2, 4, 8, 16)
POOL_GROUP = 512
POOL_HALO = 16
N_CHIPS = 4

ADAM_LR, ADAM_B1, ADAM_B2, ADAM_EPS, ADAM_WD, ADAM_STEP = 0.001, 0.9, 0.999, 1e-08, 0.01, 10

VMEM_LIMIT = 48 * 1024 * 1024

WEIGHTS = ['meta_tokens', 'norm0_g', 'l0_w_in', 'l0_lam_re', 'l0_lam_im', 'l0_log_dt', 'l0_b_re', 'l0_b_im',
           'l0_c_re', 'l0_c_im', 'l0_d_skip', 'l0_w_glu', 'l0_b_glu', 'l0_w_out', 'norm1_g', 'l1_w_in',
           'l1_conv_w', 'l1_conv_b', 'l1_w_out', 'norm2_g', 'l2_w_in', 'l2_w_grp', 'l2_b_grp', 'l2_scale',
           'l2_w_out', 'norm3_g', 'l3_w_in', 'l3_lam_re', 'l3_lam_im', 'l3_log_dt', 'l3_b_re', 'l3_b_im',
           'l3_c_re', 'l3_c_im', 'l3_d_skip', 'l3_w_glu', 'l3_b_glu', 'l3_w_out', 'final_g']
BIG = ['l0_w_in', 'l0_w_glu', 'l0_w_out', 'l1_w_in', 'l1_w_out', 'l2_w_in', 'l2_w_grp', 'l2_w_out',
       'l3_w_in', 'l3_w_glu', 'l3_w_out']
SMALL_SHARDED = ['meta_tokens', 'l1_conv_w', 'l2_b_grp']
SMALL = [n for n in WEIGHTS if n not in BIG]


def _params(*sem):
    return pltpu.CompilerParams(dimension_semantics=sem, vmem_limit_bytes=VMEM_LIMIT)


def _tile(n, cap, mult):
    best = None
    for t in range(mult, min(n, cap) + 1, mult):
        if n % t == 0:
            best = t
    assert best is not None, (n, cap, mult)
    return best


def mm_nn(a, b, *, name, bias=None, add=None, out_dtype=F32):
    m, k = a.shape
    s, _, ns = b.shape
    n = s * ns
    tm, tn = _tile(m, 1088, 16), _tile(ns, 512, 128)
    per = ns // tn

    def body(*refs):
        a_ref, b_ref = refs[0], refs[1]
        o_ref = refs[-1]
        acc = jnp.dot(a_ref[...].astype(BF16), b_ref[...], preferred_element_type=F32)
        pos = 2
        if bias is not None:
            acc = acc + refs[pos][...]
            pos += 1
        if add is not None:
            acc = acc + refs[pos][...]
        o_ref[...] = acc.astype(o_ref.dtype)

    in_specs = [pl.BlockSpec((tm, k), lambda i, j: (i, 0)),
                pl.BlockSpec((None, k, tn), lambda i, j: (j // per, 0, j % per))]
    args = [a, b]
    if bias is not None:
        in_specs.append(pl.BlockSpec((1, tn), lambda i, j: (0, j)))
        args.append(bias)
    if add is not None:
        in_specs.append(pl.BlockSpec((tm, tn), lambda i, j: (i, j)))
        args.append(add)
    return pl.pallas_call(
        body, name=name, grid=(m // tm, n // tn), in_specs=in_specs,
        out_specs=pl.BlockSpec((tm, tn), lambda i, j: (i, j)),
        out_shape=jax.ShapeDtypeStruct((m, n), out_dtype),
        compiler_params=_params("parallel", "parallel"))(*args)


def mm_nt(g, w, *, name):
    m, kc = g.shape
    s, nout, kcs = w.shape
    assert s * kcs == kc
    tm, tno, tk = _tile(m, 1088, 16), _tile(nout, 1024, 128), _tile(kcs, 1024, 128)
    per = kcs // tk

    def body(g_ref, w_ref, o_ref):
        kk = pl.program_id(2)
        part = lax.dot_general(g_ref[...].astype(BF16), w_ref[...], (((1,), (1,)), ((), ())),
                               preferred_element_type=F32)

        @pl.when(kk == 0)
        def _():
            o_ref[...] = part

        @pl.when(kk > 0)
        def _():
            o_ref[...] += part

    return pl.pallas_call(
        body, name=name, grid=(m // tm, nout // tno, kc // tk),
        in_specs=[pl.BlockSpec((tm, tk), lambda i, j, kk: (i, kk)),
                  pl.BlockSpec((None, tno, tk), lambda i, j, kk: (kk // per, j, kk % per))],
        out_specs=pl.BlockSpec((tm, tno), lambda i, j, kk: (i, j)),
        out_shape=jax.ShapeDtypeStruct((m, nout), F32),
        compiler_params=_params("parallel", "parallel", "arbitrary"))(g, w)


def mm_tn(a, g, shards, *, name):
    m, ka = a.shape
    n = g.shape[1]
    ns = n // shards
    tm, tka, tn = _tile(m, 1088, 16), _tile(ka, 1024, 128), _tile(ns, 1024, 128)
    per = ns // tn

    def body(a_ref, g_ref, o_ref):
        mm = pl.program_id(2)
        part = lax.dot_general(a_ref[...].astype(BF16), g_ref[...].astype(BF16), (((0,), (0,)), ((), ())),
                               preferred_element_type=F32)

        @pl.when(mm == 0)
        def _():
            o_ref[...] = part

        @pl.when(mm > 0)
        def _():
            o_ref[...] += part

    return pl.pallas_call(
        body, name=name, grid=(ka // tka, n // tn, m // tm),
        in_specs=[pl.BlockSpec((tm, tka), lambda i, j, mm: (mm, i)),
                  pl.BlockSpec((tm, tn), lambda i, j, mm: (mm, j))],
        out_specs=pl.BlockSpec((None, tka, tn), lambda i, j, mm: (j // per, i, j % per)),
        out_shape=jax.ShapeDtypeStruct((shards, ka, ns), F32),
        compiler_params=_params("parallel", "parallel", "arbitrary"))(a, g)


def grp_nn(a, w, bias, *, name):
    m = a.shape[0]
    tm = _tile(m, 1088, 16)
    ng = len(POOL_WINDOWS)

    def body(a_ref, w_ref, b_ref, o_ref):
        wj = w_ref[...].reshape(POOL_GROUP, POOL_GROUP)
        o_ref[...] = jnp.dot(a_ref[...].astype(BF16), wj, preferred_element_type=F32) + b_ref[...]

    return pl.pallas_call(
        body, name=name, grid=(m // tm, ng),
        in_specs=[pl.BlockSpec((tm, POOL_GROUP), lambda i, j: (i, j)),
                  pl.BlockSpec((N_CHIPS, None, POOL_GROUP // N_CHIPS, POOL_GROUP), lambda i, j: (0, j, 0, 0)),
                  pl.BlockSpec((1, POOL_GROUP), lambda i, j: (0, j))],
        out_specs=pl.BlockSpec((tm, POOL_GROUP), lambda i, j: (i, j)),
        out_shape=jax.ShapeDtypeStruct((m, ng * POOL_GROUP), F32),
        compiler_params=_params("parallel", "parallel"))(a, w, bias)


def grp_nt(g, w, *, name):
    m = g.shape[0]
    tm = _tile(m, 1088, 16)
    ng = len(POOL_WINDOWS)

    def body(g_ref, w_ref, o_ref):
        wj = w_ref[...].reshape(POOL_GROUP, POOL_GROUP)
        o_ref[...] = lax.dot_general(g_ref[...].astype(BF16), wj, (((1,), (1,)), ((), ())),
                                     preferred_element_type=F32)

    return pl.pallas_call(
        body, name=name, grid=(m // tm, ng),
        in_specs=[pl.BlockSpec((tm, POOL_GROUP), lambda i, j: (i, j)),
                  pl.BlockSpec((N_CHIPS, None, POOL_GROUP // N_CHIPS, POOL_GROUP), lambda i, j: (0, j, 0, 0))],
        out_specs=pl.BlockSpec((tm, POOL_GROUP), lambda i, j: (i, j)),
        out_shape=jax.ShapeDtypeStruct((m, ng * POOL_GROUP), F32),
        compiler_params=_params("parallel", "parallel"))(g, w)


def grp_tn(a, g, *, name):
    m = a.shape[0]
    tm = _tile(m, 1088, 16)
    ng = len(POOL_WINDOWS)
    rows = POOL_GROUP // N_CHIPS

    def body(a_ref, g_ref, o_ref):
        mm = pl.program_id(1)
        part = lax.dot_general(a_ref[...].astype(BF16), g_ref[...].astype(BF16), (((0,), (0,)), ((), ())),
                               preferred_element_type=F32).reshape(N_CHIPS, rows, POOL_GROUP)

        @pl.when(mm == 0)
        def _():
            o_ref[...] = part

        @pl.when(mm > 0)
        def _():
            o_ref[...] += part

    return pl.pallas_call(
        body, name=name, grid=(ng, m // tm),
        in_specs=[pl.BlockSpec((tm, POOL_GROUP), lambda j, mm: (mm, j)),
                  pl.BlockSpec((tm, POOL_GROUP), lambda j, mm: (mm, j))],
        out_specs=pl.BlockSpec((N_CHIPS, None, rows, POOL_GROUP), lambda j, mm: (0, j, 0, 0)),
        out_shape=jax.ShapeDtypeStruct((N_CHIPS, ng, rows, POOL_GROUP), F32),
        compiler_params=_params("parallel", "arbitrary"))(a, g)


def rowwise(fn, rows, bcast, outs, accs=(), *, name, aliases=None):
    m = rows[0][0].shape[0]
    tr = m // 16
    n_in = len(rows) + len(bcast)

    def body(*refs):
        vals = [r[...] for r in refs[:n_in]]
        o_vals, a_vals = fn(*vals)
        for ref, val in zip(refs[n_in:n_in + len(outs)], o_vals, strict=True):
            ref[...] = val.astype(ref.dtype)
        step = pl.program_id(0)
        for ref, val in zip(refs[n_in + len(outs):], a_vals, strict=True):
            @pl.when(step == 0)
            def _(ref=ref, val=val):
                ref[...] = val

            @pl.when(step > 0)
            def _(ref=ref, val=val):
                ref[...] += val

    in_specs = [pl.BlockSpec((tr, w), functools.partial(lambda i, cb: (i, cb), cb=cb)) for _, w, cb in rows]
    in_specs += [pl.BlockSpec(b.shape, lambda i: (0, 0)) for b in bcast]
    out_specs = [pl.BlockSpec((tr, w), functools.partial(lambda i, cb: (i, cb), cb=cb)) for _, _, w, cb in outs]
    out_specs += [pl.BlockSpec((1, w), lambda i: (0, 0)) for w in accs]
    out_shape = [jax.ShapeDtypeStruct((m, cols), dt) for cols, dt, _, _ in outs]
    out_shape += [jax.ShapeDtypeStruct((1, w), F32) for w in accs]
    res = pl.pallas_call(
        body, name=name, grid=(m // tr,), in_specs=in_specs, out_specs=out_specs, out_shape=out_shape,
        input_output_aliases=aliases or {},
        compiler_params=_params("arbitrary"))(*[r[0] for r in rows], *bcast)
    return res[:len(outs)], res[len(outs):]


def _rms(h, g):
    return h * lax.rsqrt(jnp.mean(h * h, axis=-1, keepdims=True) + EPS) * g


def _colsum(v):
    return jnp.sum(v, axis=0, keepdims=True)


def rms_fwd(h, g, *, name):
    (n,), _ = rowwise(lambda hv, gv: ([_rms(hv, gv)], []), [(h, D_MODEL, 0)], [g],
                      [(D_MODEL, BF16, D_MODEL, 0)], name=name)
    return n


def rms_bwd(h, g, dn, dh_out, *, name):
    def fn(hv, dnv, dhv, gv):
        _, vjp = jax.vjp(_rms, hv, gv)
        dh, dg = vjp(dnv)
        return [dhv + dh], [dg]

    (dh,), (dg,) = rowwise(fn, [(h, D_MODEL, 0), (dn, D_MODEL, 0), (dh_out, D_MODEL, 0)], [g],
                           [(D_MODEL, F32, D_MODEL, 0)], [D_MODEL], name=name)
    return dh, dg


def _s5_gate(y0, q, z):
    yg = jax.nn.gelu(y0)
    return yg * jax.nn.sigmoid(q) * jax.nn.silu(z)


def _pool_gate(o, z, scale):
    return o * scale * jax.nn.silu(z)


def _shift_rows(v, d, down):
    t = v.shape[0]
    row = lax.broadcasted_iota(jnp.int32, v.shape, 0)
    if down:
        return jnp.where(row >= d, pltpu.roll(v, d, 0), 0.0)
    return jnp.where(row < t - d, pltpu.roll(v, t - d, 0), 0.0)


def _scan(sr, si, ar, ai, down):
    pr, pi = ar, ai
    d = 1
    while d < sr.shape[0]:
        hr, hi = _shift_rows(sr, d, down), _shift_rows(si, d, down)
        sr, si = sr + pr * hr - pi * hi, si + pr * hi + pi * hr
        pr, pi = pr * pr - pi * pi, 2.0 * pr * pi
        d *= 2
    return sr, si


def s5_fwd(uz, bblk, cblk, ar, ai, dskip, *, name):
    lp = uz.shape[0]
    t = SCAN_ROWS
    nst = S5_GROUPS * S5_STATE

    def body(u_ref, b_ref, c_ref, ar_ref, ai_ref, d_ref, y_ref, sr_ref, si_ref, cr, ci):
        step = pl.program_id(1)

        @pl.when(step == 0)
        def _():
            cr[...] = jnp.zeros_like(cr)
            ci[...] = jnp.zeros_like(ci)

        u = u_ref[...]
        a_r, a_i = ar_ref[...], ai_ref[...]
        x = jnp.dot(u.astype(BF16), b_ref[...].astype(BF16), preferred_element_type=F32)
        xr, xi = x[:, :STATE_BLOCK], x[:, STATE_BLOCK:]
        first = lax.broadcasted_iota(jnp.int32, xr.shape, 0) == 0
        xr = xr + jnp.where(first, a_r * cr[...] - a_i * ci[...], 0.0)
        xi = xi + jnp.where(first, a_r * ci[...] + a_i * cr[...], 0.0)
        sr, si = _scan(xr, xi, a_r, a_i, True)
        sr_ref[...] = sr
        si_ref[...] = si
        cr[...] = sr[t - 1:t, :]
        ci[...] = si[t - 1:t, :]
        s = jnp.concatenate([sr, si], axis=1).astype(BF16)
        y_ref[...] = jnp.dot(s, c_ref[...].astype(BF16), preferred_element_type=F32) + d_ref[...] * u

    return pl.pallas_call(
        body, name=name, grid=(S5_BLOCKS, lp // t),
        in_specs=[pl.BlockSpec((t, LANE_BLOCK), lambda b, c: (c, b)),
                  pl.BlockSpec((None, LANE_BLOCK, 2 * STATE_BLOCK), lambda b, c: (b, 0, 0)),
                  pl.BlockSpec((None, 2 * STATE_BLOCK, LANE_BLOCK), lambda b, c: (b, 0, 0)),
                  pl.BlockSpec((1, STATE_BLOCK), lambda b, c: (0, b)),
                  pl.BlockSpec((1, STATE_BLOCK), lambda b, c: (0, b)),
                  pl.BlockSpec((1, LANE_BLOCK), lambda b, c: (0, b))],
        out_specs=[pl.BlockSpec((t, LANE_BLOCK), lambda b, c: (c, b)),
                   pl.BlockSpec((t, STATE_BLOCK), lambda b, c: (c, b)),
                   pl.BlockSpec((t, STATE_BLOCK), lambda b, c: (c, b))],
        out_shape=[jax.ShapeDtypeStruct((lp, D_MODEL), F32), jax.ShapeDtypeStruct((lp, nst), F32),
                   jax.ShapeDtypeStruct((lp, nst), F32)],
        scratch_shapes=[pltpu.VMEM((1, STATE_BLOCK), F32), pltpu.VMEM((1, STATE_BLOCK), F32)],
        compiler_params=_params("parallel", "arbitrary"))(uz, bblk, cblk, ar, ai, dskip)


def s5_bwd(dy0, uz, sr, si, bblk, cblk, ar, ai, dskip, dp, *, name):
    lp = uz.shape[0]
    t = SCAN_ROWS
    nch = lp // t
    nst = S5_GROUPS * S5_STATE

    def body(dy_ref, u_ref, sr_ref, si_ref, b_ref, c_ref, ar_ref, ai_ref, d_ref, _, du_ref, db_ref, dc_ref,
             dar_ref, dai_ref, dd_ref, cr, ci):
        step = pl.program_id(1)

        @pl.when(step == 0)
        def _():
            cr[...] = jnp.zeros_like(cr)
            ci[...] = jnp.zeros_like(ci)

        dy, u = dy_ref[...], u_ref[...]
        dyb, ub = dy.astype(BF16), u.astype(BF16)
        a_r, a_i = ar_ref[...], -ai_ref[...]
        g = lax.dot_general(dyb, c_ref[...].astype(BF16), (((1,), (1,)), ((), ())), preferred_element_type=F32)
        gr, gi = g[:, :STATE_BLOCK], g[:, STATE_BLOCK:]
        nxt_r, nxt_i = cr[...], ci[...]
        last = lax.broadcasted_iota(jnp.int32, gr.shape, 0) == t - 1
        gr = gr + jnp.where(last, a_r * nxt_r - a_i * nxt_i, 0.0)
        gi = gi + jnp.where(last, a_r * nxt_i + a_i * nxt_r, 0.0)
        lr, li = _scan(gr, gi, a_r, a_i, False)
        cr[...] = lr[0:1, :]
        ci[...] = li[0:1, :]
        nr = _shift_rows(lr, 1, False) + jnp.where(last, nxt_r, 0.0)
        ni = _shift_rows(li, 1, False) + jnp.where(last, nxt_i, 0.0)
        s_r, s_i = sr_ref[...], si_ref[...]
        dar = _colsum(s_r * nr + s_i * ni)
        dai = _colsum(s_r * ni - s_i * nr)
        lam = jnp.concatenate([lr, li], axis=1).astype(BF16)
        sb = jnp.concatenate([s_r, s_i], axis=1).astype(BF16)
        dc = lax.dot_general(sb, dyb, (((0,), (0,)), ((), ())), preferred_element_type=F32)
        db = lax.dot_general(ub, lam, (((0,), (0,)), ((), ())), preferred_element_type=F32)
        du_ref[...] = lax.dot_general(lam, b_ref[...].astype(BF16), (((1,), (1,)), ((), ())),
                                      preferred_element_type=F32) + d_ref[...] * dy
        dd = _colsum(dy * u)

        @pl.when(step == 0)
        def _():
            db_ref[...] = db
            dc_ref[...] = dc
            dar_ref[...] = dar
            dai_ref[...] = dai
            dd_ref[...] = dd

        @pl.when(step > 0)
        def _():
            db_ref[...] += db
            dc_ref[...] += dc
            dar_ref[...] += dar
            dai_ref[...] += dai
            dd_ref[...] += dd

    rev = lambda b, c: (nch - 1 - c, b)
    return pl.pallas_call(
        body, name=name, grid=(S5_BLOCKS, nch),
        in_specs=[pl.BlockSpec((t, LANE_BLOCK), rev), pl.BlockSpec((t, LANE_BLOCK), rev),
                  pl.BlockSpec((t, STATE_BLOCK), rev), pl.BlockSpec((t, STATE_BLOCK), rev),
                  pl.BlockSpec((None, LANE_BLOCK, 2 * STATE_BLOCK), lambda b, c: (b, 0, 0)),
                  pl.BlockSpec((None, 2 * STATE_BLOCK, LANE_BLOCK), lambda b, c: (b, 0, 0)),
                  pl.BlockSpec((1, STATE_BLOCK), lambda b, c: (0, b)),
                  pl.BlockSpec((1, STATE_BLOCK), lambda b, c: (0, b)),
                  pl.BlockSpec((1, LANE_BLOCK), lambda b, c: (0, b)),
                  pl.BlockSpec(memory_space=pl.ANY)],
        out_specs=[pl.BlockSpec((t, LANE_BLOCK), rev),
                   pl.BlockSpec((None, LANE_BLOCK, 2 * STATE_BLOCK), lambda b, c: (b, 0, 0)),
                   pl.BlockSpec((None, 2 * STATE_BLOCK, LANE_BLOCK), lambda b, c: (b, 0, 0)),
                   pl.BlockSpec((1, STATE_BLOCK), lambda b, c: (0, b)),
                   pl.BlockSpec((1, STATE_BLOCK), lambda b, c: (0, b)),
                   pl.BlockSpec((1, LANE_BLOCK), lambda b, c: (0, b))],
        out_shape=[jax.ShapeDtypeStruct(dp.shape, F32),
                   jax.ShapeDtypeStruct((S5_BLOCKS, LANE_BLOCK, 2 * STATE_BLOCK), F32),
                   jax.ShapeDtypeStruct((S5_BLOCKS, 2 * STATE_BLOCK, LANE_BLOCK), F32),
                   jax.ShapeDtypeStruct((1, nst), F32), jax.ShapeDtypeStruct((1, nst), F32),
                   jax.ShapeDtypeStruct((1, D_MODEL), F32)],
        scratch_shapes=[pltpu.VMEM((1, STATE_BLOCK), F32), pltpu.VMEM((1, STATE_BLOCK), F32)],
        input_output_aliases={9: 0},
        compiler_params=_params("parallel", "arbitrary"))(dy0, uz, sr, si, bblk, cblk, ar, ai, dskip, dp)


def s5_tables(lam_re, lam_im, log_dt, b_re, b_im, c_re, c_im):
    dt = jnp.exp(log_dt)[:, None]
    mag = jnp.exp(lam_re * dt)
    ar = mag * jnp.cos(lam_im * dt)
    ai = mag * jnp.sin(lam_im * dt)
    den = lam_re * lam_re + lam_im * lam_im
    kr = ((ar - 1.0) * lam_re + ai * lam_im) / den
    ki = (ai * lam_re - (ar - 1.0) * lam_im) / den
    bbr = kr[..., None] * b_re - ki[..., None] * b_im
    bbi = kr[..., None] * b_im + ki[..., None] * b_re
    eye = jnp.eye(GROUPS_PER_BLOCK, dtype=F32)

    def b_blocks(v):
        v = v.reshape(S5_BLOCKS, GROUPS_PER_BLOCK, S5_STATE, S5_GROUP)
        return jnp.einsum('bgpi,gh->bgihp', v, eye).reshape(S5_BLOCKS, LANE_BLOCK, STATE_BLOCK)

    def c_blocks(v):
        v = v.reshape(S5_BLOCKS, GROUPS_PER_BLOCK, S5_GROUP, S5_STATE)
        return jnp.einsum('bgip,gh->bgphi', v, eye).reshape(S5_BLOCKS, STATE_BLOCK, LANE_BLOCK)

    bblk = jnp.concatenate([b_blocks(bbr), b_blocks(bbi)], axis=2)
    cblk = jnp.concatenate([c_blocks(c_re), -c_blocks(c_im)], axis=1)
    nst = S5_GROUPS * S5_STATE
    return ar.reshape(1, nst), ai.reshape(1, nst), bblk, cblk


def _silu_grad(z):
    s = jax.nn.sigmoid(z)
    return s * (1.0 + z * (1.0 - s))


def conv_fwd(p, conv_w, conv_b, *, name):
    lp = p.shape[0]
    tr = lp // 16
    e = CONV_E
    hb = tr // 8

    def body(p_ref, cgh_ref, vh_ref, w_ref, b_ref, o_ref):
        i = pl.program_id(0)
        bg, cg, v, z = (p_ref[:, k * e:(k + 1) * e] for k in range(4))
        hc = cg * v
        halo = jnp.where(i > 0, cgh_ref[...] * vh_ref[...], 0.0)
        ext = jnp.concatenate([halo, hc], axis=0)
        w = w_ref[...]
        conv = (w[0:1] * pltpu.roll(ext, 2, 0)[8:] + w[1:2] * pltpu.roll(ext, 1, 0)[8:] + w[2:3] * hc
                + b_ref[...])
        o_ref[...] = (bg * conv * jax.nn.silu(z)).astype(o_ref.dtype)

    prev = lambda col: (lambda i: (jnp.maximum(i * hb - 1, 0), col))
    return pl.pallas_call(
        body, name=name, grid=(lp // tr,),
        in_specs=[pl.BlockSpec((tr, 4 * e), lambda i: (i, 0)),
                  pl.BlockSpec((8, e), prev(1)), pl.BlockSpec((8, e), prev(2)),
                  pl.BlockSpec((3, e), lambda i: (0, 0)), pl.BlockSpec((1, e), lambda i: (0, 0))],
        out_specs=pl.BlockSpec((tr, e), lambda i: (i, 0)),
        out_shape=jax.ShapeDtypeStruct((lp, e), BF16),
        compiler_params=_params("parallel"))(p, p, p, conv_w, conv_b)


def conv_bwd(dy3, p, conv_w, conv_b, *, name):
    lp = p.shape[0]
    tr = lp // 32
    e = CONV_E
    hb = tr // 8
    nt = lp // tr

    def body(dy_ref, p_ref, cgh_ref, vh_ref, dyn_ref, bgn_ref, zn_ref, w_ref, b_ref, dp_ref, dw_ref, db_ref):
        i = pl.program_id(0)
        dy = dy_ref[...]
        bg, cg, v, z = (p_ref[:, k * e:(k + 1) * e] for k in range(4))
        w = w_ref[...]
        hc = cg * v
        halo = jnp.where(i > 0, cgh_ref[...] * vh_ref[...], 0.0)
        ext = jnp.concatenate([halo, hc], axis=0)
        hc2, hc1 = pltpu.roll(ext, 2, 0)[8:], pltpu.roll(ext, 1, 0)[8:]
        yc = w[0:1] * hc2 + w[1:2] * hc1 + w[2:3] * hc + b_ref[...]
        sz = jax.nn.silu(z)
        dyc = dy * bg * sz
        dyc_n = jnp.where(i < nt - 1, dyn_ref[...] * bgn_ref[...] * jax.nn.silu(zn_ref[...]), 0.0)
        ext_n = jnp.concatenate([dyc, dyc_n], axis=0)
        n_rows = tr + 8
        dhc = (w[2:3] * dyc + w[1:2] * pltpu.roll(ext_n, n_rows - 1, 0)[:tr]
               + w[0:1] * pltpu.roll(ext_n, n_rows - 2, 0)[:tr])
        dp_ref[:, 0:e] = dy * yc * sz
        dp_ref[:, e:2 * e] = dhc * v
        dp_ref[:, 2 * e:3 * e] = dhc * cg
        dp_ref[:, 3 * e:4 * e] = dy * bg * yc * _silu_grad(z)
        dw = jnp.concatenate([_colsum(dyc * hc2), _colsum(dyc * hc1), _colsum(dyc * hc)], axis=0)
        db = _colsum(dyc)

        @pl.when(i == 0)
        def _():
            dw_ref[...] = dw
            db_ref[...] = db

        @pl.when(i > 0)
        def _():
            dw_ref[...] += dw
            db_ref[...] += db

    prev = lambda col: (lambda i: (jnp.maximum(i * hb - 1, 0), col))
    nxt = lambda col: (lambda i: (jnp.minimum((i + 1) * hb, lp // 8 - 1), col))
    return pl.pallas_call(
        body, name=name, grid=(nt,),
        in_specs=[pl.BlockSpec((tr, e), lambda i: (i, 0)), pl.BlockSpec((tr, 4 * e), lambda i: (i, 0)),
                  pl.BlockSpec((8, e), prev(1)), pl.BlockSpec((8, e), prev(2)),
                  pl.BlockSpec((8, e), nxt(0)), pl.BlockSpec((8, e), nxt(0)), pl.BlockSpec((8, e), nxt(3)),
                  pl.BlockSpec((3, e), lambda i: (0, 0)), pl.BlockSpec((1, e), lambda i: (0, 0))],
        out_specs=[pl.BlockSpec((tr, 4 * e), lambda i: (i, 0)), pl.BlockSpec((3, e), lambda i: (0, 0)),
                   pl.BlockSpec((1, e), lambda i: (0, 0))],
        out_shape=[jax.ShapeDtypeStruct((lp, 4 * e), F32), jax.ShapeDtypeStruct((3, e), F32),
                   jax.ShapeDtypeStruct((1, e), F32)],
        compiler_params=_params("arbitrary"))(dy3, p, p, p, dy3, p, p, conv_w, conv_b)


def _pool_counts(i, tr, rows, offset, w):
    t = (i * tr + offset + 1 + lax.broadcasted_iota(jnp.int32, (rows, 1), 0)).astype(F32)
    return jnp.minimum(t, float(w))


def pool_fwd(p, *, name):
    lp = p.shape[0]
    tr = lp // 16
    h = POOL_HALO
    hb = tr // h

    def body(u_ref, uh_ref, o_ref):
        i = pl.program_id(0)
        u = u_ref[...]
        ext = jnp.concatenate([jnp.where(i > 0, uh_ref[...], 0.0), u], axis=0)
        for k, w in enumerate(POOL_WINDOWS):
            cols = slice(k * POOL_GROUP, (k + 1) * POOL_GROUP)
            s = ext[:, cols]
            d = 1
            while d < w:
                s = s + pltpu.roll(s, d, 0)
                d *= 2
            o_ref[:, cols] = (s[h:] / _pool_counts(i, tr, tr, 0, w) - u[:, cols]).astype(o_ref.dtype)

    return pl.pallas_call(
        body, name=name, grid=(lp // tr,),
        in_specs=[pl.BlockSpec((tr, POOL_E), lambda i: (i, 0)),
                  pl.BlockSpec((h, POOL_E), lambda i: (jnp.maximum(i * hb - 1, 0), 0))],
        out_specs=pl.BlockSpec((tr, POOL_E), lambda i: (i, 0)),
        out_shape=jax.ShapeDtypeStruct((lp, POOL_E), BF16),
        compiler_params=_params("parallel"))(p, p)


def pool_bwd(dmix, dp, *, name):
    lp = dmix.shape[0]
    tr = lp // 16
    h = POOL_HALO
    hb = tr // h
    nt = lp // tr

    def body(dm_ref, dmn_ref, _, du_ref):
        i = pl.program_id(0)
        dm = dm_ref[...]
        dmn = jnp.where(i < nt - 1, dmn_ref[...], 0.0)
        n_rows = tr + h
        for k, w in enumerate(POOL_WINDOWS):
            cols = slice(k * POOL_GROUP, (k + 1) * POOL_GROUP)
            s = jnp.concatenate([dm[:, cols] / _pool_counts(i, tr, tr, 0, w),
                                 dmn[:, cols] / _pool_counts(i, tr, h, tr, w)], axis=0)
            d = 1
            while d < w:
                s = s + pltpu.roll(s, n_rows - d, 0)
                d *= 2
            du_ref[:, cols] = s[:tr] - dm[:, cols]

    return pl.pallas_call(
        body, name=name, grid=(nt,),
        in_specs=[pl.BlockSpec((tr, POOL_E), lambda i: (i, 0)),
                  pl.BlockSpec((h, POOL_E), lambda i: (jnp.minimum((i + 1) * hb, lp // h - 1), 0)),
                  pl.BlockSpec(memory_space=pl.ANY)],
        out_specs=pl.BlockSpec((tr, POOL_E), lambda i: (i, 0)),
        out_shape=jax.ShapeDtypeStruct(dp.shape, F32),
        input_output_aliases={2: 0},
        compiler_params=_params("parallel"))(dmix, dmix, dp)


def adamw(w, g, m, v, *, name):
    r, c = w.shape
    tr = _tile(r, max(8, (1 << 19) // c), 8)

    def body(w_ref, g_ref, m_ref, v_ref, d_ref, nm_ref, nv_ref):
        gv = g_ref[...]
        nm = ADAM_B1 * m_ref[...] + (1.0 - ADAM_B1) * gv
        nv = ADAM_B2 * v_ref[...] + (1.0 - ADAM_B2) * jnp.square(gv)
        m_hat = nm / (1.0 - ADAM_B1 ** ADAM_STEP)
        v_hat = nv / (1.0 - ADAM_B2 ** ADAM_STEP)
        d_ref[...] = -ADAM_LR * (m_hat / (jnp.sqrt(v_hat) + ADAM_EPS) + ADAM_WD * w_ref[...])
        nm_ref[...] = nm
        nv_ref[...] = nv

    spec = pl.BlockSpec((tr, c), lambda i: (i, 0))
    return pl.pallas_call(
        body, name=name, grid=(r // tr,), in_specs=[spec] * 4, out_specs=[spec] * 3,
        out_shape=[jax.ShapeDtypeStruct((r, c), F32)] * 3,
        compiler_params=_params("parallel"))(w, g, m, v)


def add_n(parts, *, name):
    if isinstance(parts, (list, tuple)):
        arrs = list(parts)
        r, c = arrs[0].shape
        specs_of = lambda tr: [pl.BlockSpec((tr, c), lambda i: (i, 0)) for _ in arrs]
    else:
        n, r, c = parts.shape
        arrs = [parts] * n
        specs_of = lambda tr: [pl.BlockSpec((None, tr, c), functools.partial(lambda i, s: (s, i, 0), s=s))
                               for s in range(n)]
    tr = _tile(r, max(8, (1 << 19) // c), 8)

    def body(*refs):
        acc = refs[0][...]
        for ref in refs[1:-1]:
            acc = acc + ref[...]
        refs[-1][...] = acc

    return pl.pallas_call(
        body, name=name, grid=(r // tr,), in_specs=specs_of(tr),
        out_specs=pl.BlockSpec((tr, c), lambda i: (i, 0)),
        out_shape=jax.ShapeDtypeStruct((r, c), F32),
        compiler_params=_params("parallel"))(*arrs)


ANY = pl.BlockSpec(memory_space=pl.ANY)


def _place():
    x, y, c = lax.axis_index("x"), lax.axis_index("y"), lax.axis_index("c")
    return x, y, c, [(1 - x, y), (x, 1 - y), (1 - x, 1 - y)]


def gather_chips(shards, *, name):
    n = len(shards)

    def body(*refs):
        ins, outs = refs[:n], refs[n:2 * n]
        send_sems, recv_sems, local_sems = refs[2 * n:]
        x, y, c, chips = _place()
        k = 2 * x + y
        copies = []
        for a in range(n):
            mine = pltpu.make_async_copy(ins[a], outs[a].at[k], local_sems.at[a])
            mine.start()
            copies.append(mine)
            for j, (px, py) in enumerate(chips):
                cp = pltpu.make_async_remote_copy(
                    src_ref=ins[a], dst_ref=outs[a].at[k], send_sem=send_sems.at[a, j],
                    recv_sem=recv_sems.at[a, j], device_id=(px, py, c), device_id_type=MESH)
                cp.start()
                copies.append(cp)
        for cp in copies:
            cp.wait()

    return pl.pallas_call(
        body, name=name, in_specs=[ANY] * n, out_specs=[ANY] * n,
        out_shape=[jax.ShapeDtypeStruct((N_CHIPS,) + s.shape, s.dtype) for s in shards],
        scratch_shapes=[pltpu.SemaphoreType.DMA((n, 3)), pltpu.SemaphoreType.DMA((n, 3)),
                        pltpu.SemaphoreType.DMA((n,))])(*shards)


def swap_halves(grads, *, name):
    n = len(grads)

    def body(*refs):
        ins, mine_out, theirs_out = refs[:n], refs[n:2 * n], refs[2 * n:3 * n]
        send_sems, recv_sems, local_sems = refs[3 * n:]
        x, y, c, _ = _place()
        copies = []
        for a in range(n):
            half = ins[a].shape[1] // 2
            keep = pltpu.make_async_copy(ins[a].at[:, pl.ds(c * half, half), :], mine_out[a], local_sems.at[a])
            keep.start()
            give = pltpu.make_async_remote_copy(
                src_ref=ins[a].at[:, pl.ds((1 - c) * half, half), :], dst_ref=theirs_out[a],
                send_sem=send_sems.at[a], recv_sem=recv_sems.at[a], device_id=(x, y, 1 - c), device_id_type=MESH)
            give.start()
            copies += [keep, give]
        for cp in copies:
            cp.wait()

    halves = [jax.ShapeDtypeStruct((g.shape[0], g.shape[1] // 2, g.shape[2]), g.dtype) for g in grads]
    res = pl.pallas_call(
        body, name=name, in_specs=[ANY] * n, out_specs=[ANY] * (2 * n), out_shape=halves + halves,
        scratch_shapes=[pltpu.SemaphoreType.DMA((n,)), pltpu.SemaphoreType.DMA((n,)), pltpu.SemaphoreType.DMA((n,))])(*grads)
    return res[:n], res[n:]


def scatter_chips(parts, *, name):
    n = len(parts)

    def body(*refs):
        ins, outs = refs[:n], refs[n:2 * n]
        send_sems, recv_sems, local_sems = refs[2 * n:]
        x, y, c, chips = _place()
        k = 2 * x + y
        copies = []
        for a in range(n):
            mine = pltpu.make_async_copy(ins[a].at[k], outs[a].at[k], local_sems.at[a])
            mine.start()
            copies.append(mine)
            for j, (px, py) in enumerate(chips):
                cp = pltpu.make_async_remote_copy(
                    src_ref=ins[a].at[2 * px + py], dst_ref=outs[a].at[k], send_sem=send_sems.at[a, j],
                    recv_sem=recv_sems.at[a, j], device_id=(px, py, c), device_id_type=MESH)
                cp.start()
                copies.append(cp)
        for cp in copies:
            cp.wait()

    return pl.pallas_call(
        body, name=name, in_specs=[ANY] * n, out_specs=[ANY] * n,
        out_shape=[jax.ShapeDtypeStruct(p.shape, p.dtype) for p in parts],
        scratch_shapes=[pltpu.SemaphoreType.DMA((n, 3)), pltpu.SemaphoreType.DMA((n, 3)),
                        pltpu.SemaphoreType.DMA((n,))])(*parts)


def join_halves(halves, *, name):
    n = len(halves)

    def body(*refs):
        ins, outs = refs[:n], refs[n:2 * n]
        send_sems, recv_sems, local_sems = refs[2 * n:]
        x, y, c, _ = _place()
        copies = []
        for a in range(n):
            mine = pltpu.make_async_copy(ins[a], outs[a].at[c], local_sems.at[a])
            mine.start()
            give = pltpu.make_async_remote_copy(
                src_ref=ins[a], dst_ref=outs[a].at[c], send_sem=send_sems.at[a], recv_sem=recv_sems.at[a],
                device_id=(x, y, 1 - c), device_id_type=MESH)
            give.start()
            copies += [mine, give]
        for cp in copies:
            cp.wait()

    return pl.pallas_call(
        body, name=name, in_specs=[ANY] * n, out_specs=[ANY] * n,
        out_shape=[jax.ShapeDtypeStruct((2,) + h.shape, h.dtype) for h in halves],
        scratch_shapes=[pltpu.SemaphoreType.DMA((n,)), pltpu.SemaphoreType.DMA((n,)), pltpu.SemaphoreType.DMA((n,))])(*halves)


def _pack(arrs, rows_mult):
    flat = jnp.concatenate([a.reshape(-1) for a in arrs])
    rows = -(-flat.shape[0] // (128 * rows_mult)) * rows_mult
    return jnp.pad(flat, (0, rows * 128 - flat.shape[0])).reshape(rows, 128)


def _unpack(packed, shapes):
    flat = packed.reshape(-1)
    out, pos = [], 0
    for s in shapes:
        size = 1
        for d in s:
            size *= d
        out.append(flat[pos:pos + size].reshape(s))
        pos += size
    return out


def _column_shard(full, k, axis):
    width = full.shape[axis] // N_CHIPS
    return lax.dynamic_slice_in_dim(full, k * width, width, axis)


def kernel(x, meta_tokens, norm0_g, l0_w_in, l0_lam_re, l0_lam_im, l0_log_dt, l0_b_re, l0_b_im, l0_c_re, l0_c_im, l0_d_skip, l0_w_glu, l0_b_glu, l0_w_out, norm1_g, l1_w_in, l1_conv_w, l1_conv_b, l1_w_out, norm2_g, l2_w_in, l2_w_grp, l2_b_grp, l2_scale, l2_w_out, norm3_g, l3_w_in, l3_lam_re, l3_lam_im, l3_log_dt, l3_b_re, l3_b_im, l3_c_re, l3_c_im, l3_d_skip, l3_w_glu, l3_b_glu, l3_w_out, final_g, loss_target, m_meta_tokens, m_norm0_g, m_l0_w_in, m_l0_lam_re, m_l0_lam_im, m_l0_log_dt, m_l0_b_re, m_l0_b_im, m_l0_c_re, m_l0_c_im, m_l0_d_skip, m_l0_w_glu, m_l0_b_glu, m_l0_w_out, m_norm1_g, m_l1_w_in, m_l1_conv_w, m_l1_conv_b, m_l1_w_out, m_norm2_g, m_l2_w_in, m_l2_w_grp, m_l2_b_grp, m_l2_scale, m_l2_w_out, m_norm3_g, m_l3_w_in, m_l3_lam_re, m_l3_lam_im, m_l3_log_dt, m_l3_b_re, m_l3_b_im, m_l3_c_re, m_l3_c_im, m_l3_d_skip, m_l3_w_glu, m_l3_b_glu, m_l3_w_out, m_final_g, v_meta_tokens, v_norm0_g, v_l0_w_in, v_l0_lam_re, v_l0_lam_im, v_l0_log_dt, v_l0_b_re, v_l0_b_im, v_l0_c_re, v_l0_c_im, v_l0_d_skip, v_l0_w_glu, v_l0_b_glu, v_l0_w_out, v_norm1_g, v_l1_w_in, v_l1_conv_w, v_l1_conv_b, v_l1_w_out, v_norm2_g, v_l2_w_in, v_l2_w_grp, v_l2_b_grp, v_l2_scale, v_l2_w_out, v_norm3_g, v_l3_w_in, v_l3_lam_re, v_l3_lam_im, v_l3_log_dt, v_l3_b_re, v_l3_b_im, v_l3_c_re, v_l3_c_im, v_l3_d_skip, v_l3_w_glu, v_l3_b_glu, v_l3_w_out, v_final_g):
    given = dict(locals())
    w = {n: given[n] for n in WEIGHTS}
    mom_m = {n: given["m_" + n] for n in WEIGHTS}
    mom_v = {n: given["v_" + n] for n in WEIGHTS}
    seq = x.shape[1]
    real = N_META + seq
    lp = -(-real // SCAN_ROWS) * SCAN_ROWS
    chip = 2 * lax.axis_index("x") + lax.axis_index("y")
    row = lambda a: a.reshape(1, -1)

    shard2d = {n: (w[n].reshape(-1, w[n].shape[-1]) if n == 'l2_w_grp' else w[n]) for n in BIG}
    names = BIG + SMALL_SHARDED
    gathered = gather_chips([shard2d[n].astype(BF16) for n in BIG] + [w[n] for n in SMALL_SHARDED],
                            name="gather_weights")
    full = dict(zip(names, gathered, strict=True))
    for n in BIG:
        if n.endswith('w_in'):
            pass
        elif n == 'l2_w_grp':
            full[n] = full[n].reshape(N_CHIPS, len(POOL_WINDOWS), POOL_GROUP // N_CHIPS, POOL_GROUP)
        else:
            full[n] = full[n].reshape(1, -1, full[n].shape[-1])
    meta_full = full['meta_tokens'].transpose(1, 0, 2).reshape(N_META, D_MODEL)
    conv_w_full = full['l1_conv_w'].transpose(1, 0, 2).reshape(3, CONV_E)
    b_grp_full = full['l2_b_grp'].transpose(1, 0, 2).reshape(len(POOL_WINDOWS), POOL_GROUP)

    h0 = jnp.concatenate([meta_full, x[0], jnp.zeros((lp - real, D_MODEL), F32)], axis=0)
    target = jnp.concatenate([jnp.zeros((N_META, D_MODEL), F32), loss_target[0],
                              jnp.zeros((lp - real, D_MODEL), F32)], axis=0)
    grads = {}
    saved = {}

    def s5_layer_fwd(i, h):
        pre = "l%d_" % i
        tables, tables_vjp = jax.vjp(s5_tables, *[w[pre + s] for s in
                                                   ('lam_re', 'lam_im', 'log_dt', 'b_re', 'b_im', 'c_re', 'c_im')])
        ar, ai, bblk, cblk = tables
        n = rms_fwd(h, row(w["norm%d_g" % i]), name=pre + "norm")
        uz = mm_nn(n, full[pre + 'w_in'], name=pre + "in")
        y0, sr, si = s5_fwd(uz, bblk, cblk, ar, ai, row(w[pre + 'd_skip']), name=pre + "scan")
        (yg,), _ = rowwise(lambda yv: ([jax.nn.gelu(yv)], []), [(y0, D_MODEL, 0)], [],
                           [(D_MODEL, BF16, D_MODEL, 0)], name=pre + "gelu")
        q = mm_nn(yg, full[pre + 'w_glu'], bias=row(w[pre + 'b_glu']), name=pre + "glu")
        (y3,), _ = rowwise(lambda yv, qv, zv: ([_s5_gate(yv, qv, zv)], []),
                           [(y0, D_MODEL, 0), (q, D_MODEL, 0), (uz, D_MODEL, 1)], [],
                           [(D_MODEL, BF16, D_MODEL, 0)], name=pre + "gate")
        h_new = mm_nn(y3, full[pre + 'w_out'], add=h, name=pre + "out")
        saved[i] = dict(h=h, n=n, uz=uz, y0=y0, sr=sr, si=si, yg=yg, q=q, y3=y3, tables=tables,
                        tables_vjp=tables_vjp)
        return h_new

    def s5_layer_bwd(i, dh):
        pre = "l%d_" % i
        s = saved[i]
        ar, ai, bblk, cblk = s['tables']
        grads[pre + 'w_out'] = mm_tn(s['y3'], dh, 1, name=pre + "d_w_out")[0]
        dy3 = mm_nt(dh, full[pre + 'w_out'], name=pre + "d_y3")

        def gate_bwd(dyv, yv, qv, zv):
            _, vjp = jax.vjp(_s5_gate, yv, qv, zv)
            dy0, dq, dz = vjp(dyv)
            return [dy0, dq, dz], [_colsum(dq)]

        (dy0_direct, dq, dp), (db_glu,) = rowwise(
            gate_bwd, [(dy3, D_MODEL, 0), (s['y0'], D_MODEL, 0), (s['q'], D_MODEL, 0), (s['uz'], D_MODEL, 1)], [],
            [(D_MODEL, F32, D_MODEL, 0), (D_MODEL, F32, D_MODEL, 0), (2 * D_MODEL, F32, D_MODEL, 1)], [D_MODEL],
            name=pre + "d_gate")
        grads[pre + 'b_glu'] = db_glu.reshape(-1)
        grads[pre + 'w_glu'] = mm_tn(s['yg'], dq, 1, name=pre + "d_w_glu")[0]
        r = mm_nt(dq, full[pre + 'w_glu'], name=pre + "d_yg")

        def gelu_bwd(dv, rv, yv):
            _, vjp = jax.vjp(jax.nn.gelu, yv)
            return [dv + vjp(rv)[0]], []

        (dy0,), _ = rowwise(gelu_bwd, [(dy0_direct, D_MODEL, 0), (r, D_MODEL, 0), (s['y0'], D_MODEL, 0)], [],
                            [(D_MODEL, F32, D_MODEL, 0)], name=pre + "d_gelu")
        dp, dbblk, dcblk, dar, dai, dd = s5_bwd(dy0, s['uz'], s['sr'], s['si'], bblk, cblk, ar, ai,
                                                row(w[pre + 'd_skip']), dp, name=pre + "d_scan")
        grads[pre + 'd_skip'] = dd.reshape(-1)
        for key, val in zip(('lam_re', 'lam_im', 'log_dt', 'b_re', 'b_im', 'c_re', 'c_im'),
                            s['tables_vjp']((dar, dai, dbblk, dcblk)), strict=True):
            grads[pre + key] = val
        grads[pre + 'w_in'] = mm_tn(s['n'], dp, N_CHIPS, name=pre + "d_w_in")
        dn = mm_nt(dp, full[pre + 'w_in'], name=pre + "d_n")
        dh_in, dg = rms_bwd(s['h'], row(w["norm%d_g" % i]), dn, dh, name=pre + "d_norm")
        grads["norm%d_g" % i] = dg.reshape(-1)
        return dh_in

    h1 = s5_layer_fwd(0, h0)

    n1 = rms_fwd(h1, row(w['norm1_g']), name="l1_norm")
    p1 = mm_nn(n1, full['l1_w_in'], name="l1_in")
    y3_1 = conv_fwd(p1, conv_w_full, row(w['l1_conv_b']), name="l1_conv")
    h2 = mm_nn(y3_1, full['l1_w_out'], add=h1, name="l1_out")

    n2 = rms_fwd(h2, row(w['norm2_g']), name="l2_norm")
    p2 = mm_nn(n2, full['l2_w_in'], name="l2_in")
    mixed = pool_fwd(p2, name="l2_pool")
    o2 = grp_nn(mixed, full['l2_w_grp'], b_grp_full.reshape(1, -1), name="l2_grp")
    (y3_2,), _ = rowwise(lambda ov, zv, sv: ([_pool_gate(ov, zv, sv)], []),
                         [(o2, POOL_E, 0), (p2, POOL_E, 1)], [row(w['l2_scale'])],
                         [(POOL_E, BF16, POOL_E, 0)], name="l2_gate")
    h3 = mm_nn(y3_2, full['l2_w_out'], add=h2, name="l2_out")

    h4 = s5_layer_fwd(3, h3)

    def loss_fn(hv, tv, gv):
        i = pl.program_id(0)
        tr = hv.shape[0]
        yf, vjp = jax.vjp(_rms, hv, gv)
        pos = i * tr + lax.broadcasted_iota(jnp.int32, (tr, 1), 0)
        diff = jnp.where((pos >= N_META) & (pos < real), yf - tv, 0.0)
        dh, dg = vjp(diff / D_MODEL)
        return [dh], [dg, _colsum(diff * diff)]

    (dh,), (d_final_g, sq) = rowwise(loss_fn, [(h4, D_MODEL, 0), (target, D_MODEL, 0)], [row(w['final_g'])],
                                     [(D_MODEL, F32, D_MODEL, 0)], [D_MODEL, D_MODEL], name="loss")
    grads['final_g'] = d_final_g.reshape(-1)
    loss = lax.psum(0.5 / D_MODEL * jnp.sum(sq), ("x", "y", "c"))

    dh = s5_layer_bwd(3, dh)

    grads['l2_w_out'] = mm_tn(y3_2, dh, 1, name="l2_d_w_out")[0]
    dy3 = mm_nt(dh, full['l2_w_out'], name="l2_d_y3")

    def pool_gate_bwd(dyv, ov, zv, sv):
        _, vjp = jax.vjp(_pool_gate, ov, zv, sv)
        do, dz, dscale = vjp(dyv)
        return [do, dz], [dscale, _colsum(do)]

    (do2, dp2), (dscale, dbgrp) = rowwise(
        pool_gate_bwd, [(dy3, POOL_E, 0), (o2, POOL_E, 0), (p2, POOL_E, 1)], [row(w['l2_scale'])],
        [(POOL_E, F32, POOL_E, 0), (2 * POOL_E, F32, POOL_E, 1)], [POOL_E, POOL_E], name="l2_d_gate")
    grads['l2_scale'] = dscale.reshape(-1)
    grads['l2_b_grp'] = dbgrp.reshape(len(POOL_WINDOWS), POOL_GROUP)
    grads['l2_w_grp'] = grp_tn(mixed, do2, name="l2_d_w_grp")
    dmix = grp_nt(do2, full['l2_w_grp'], name="l2_d_mixed")
    dp2 = pool_bwd(dmix, dp2, name="l2_d_pool")
    grads['l2_w_in'] = mm_tn(n2, dp2, N_CHIPS, name="l2_d_w_in")
    dn = mm_nt(dp2, full['l2_w_in'], name="l2_d_n")
    dh, dg = rms_bwd(h2, row(w['norm2_g']), dn, dh, name="l2_d_norm")
    grads['norm2_g'] = dg.reshape(-1)

    grads['l1_w_out'] = mm_tn(y3_1, dh, 1, name="l1_d_w_out")[0]
    dy3 = mm_nt(dh, full['l1_w_out'], name="l1_d_y3")
    dp1, dcw, dcb = conv_bwd(dy3, p1, conv_w_full, row(w['l1_conv_b']), name="l1_d_conv")
    grads['l1_conv_w'] = dcw
    grads['l1_conv_b'] = dcb.reshape(-1)
    grads['l1_w_in'] = mm_tn(n1, dp1, N_CHIPS, name="l1_d_w_in")
    dn = mm_nt(dp1, full['l1_w_in'], name="l1_d_n")
    dh, dg = rms_bwd(h1, row(w['norm1_g']), dn, dh, name="l1_d_norm")
    grads['norm1_g'] = dg.reshape(-1)

    dh = s5_layer_bwd(0, dh)
    grad_x = dh[N_META:real][None]
    grads['meta_tokens'] = dh[:N_META]

    def chip_major(n):
        g = grads[n]
        if n.endswith('w_in'):
            return g
        if n == 'l2_w_grp':
            return g.reshape(N_CHIPS, -1, POOL_GROUP)
        return g.reshape(N_CHIPS, -1, g.shape[-1])
    small_local = _pack([grads[n] for n in SMALL], 16 * N_CHIPS)
    pieces = [chip_major(n) for n in BIG] + [small_local.reshape(N_CHIPS, -1, 128)]
    mine, theirs = swap_halves(pieces, name="reduce_cores")
    pair = [add_n([a.reshape(-1, a.shape[-1]), b.reshape(-1, b.shape[-1])], name="sum_cores_%d" % i
                  ).reshape(a.shape) for i, (a, b) in enumerate(zip(mine, theirs, strict=True))]
    from_chips = scatter_chips(pair, name="reduce_chips")
    reduced_half = [add_n(p, name="sum_chips_%d" % i) for i, p in enumerate(from_chips)]
    joined = join_halves(reduced_half, name="share_cores")
    reduced = [j.reshape(-1, j.shape[-1]) for j in joined]
    g_big = {n: reduced[i].reshape(w[n].shape) for i, n in enumerate(BIG)}
    (small_all,) = gather_chips([reduced[-1]], name="gather_small")
    g_small_full = dict(zip(SMALL, _unpack(small_all, [grads[n].shape for n in SMALL]), strict=True))
    g_small = {}
    for n in SMALL:
        g = g_small_full[n]
        g_small[n] = _column_shard(g, chip, g.ndim - 1) if n in SMALL_SHARDED else g

    out_g, out_d, out_m, out_v = {}, {}, {}, {}
    for n in BIG:
        shape = w[n].shape
        as2d = lambda a: a.reshape(-1, shape[-1])
        d, nm, nv = adamw(as2d(w[n]), as2d(g_big[n]), as2d(mom_m[n]), as2d(mom_v[n]), name="adamw_" + n)
        out_g[n], out_d[n], out_m[n], out_v[n] = g_big[n], d.reshape(shape), nm.reshape(shape), nv.reshape(shape)
    packs = [_pack([src[n] for n in SMALL], 8) for src in (w, g_small, mom_m, mom_v)]
    d, nm, nv = adamw(*packs, name="adamw_small")
    shapes = [w[n].shape for n in SMALL]
    for n, dv, mv, vv in zip(SMALL, _unpack(d, shapes), _unpack(nm, shapes), _unpack(nv, shapes), strict=True):
        out_g[n], out_d[n], out_m[n], out_v[n] = g_small[n], dv, mv, vv

    return (loss, grad_x, *[out_g[n] for n in WEIGHTS], *[out_d[n] for n in WEIGHTS],
            *[out_m[n] for n in WEIGHTS], *[out_v[n] for n in WEIGHTS])
```

```python
import functools

import jax
import jax.numpy as jnp
from jax import lax
from jax.experimental import pallas as pl
from jax.experimental.pallas import tpu as pltpu

F32, BF16 = jnp.float32, jnp.bfloat16
MESH = pl.DeviceIdType.MESH

D_MODEL = 1024
N_META = 16
EPS = 1e-6
S5_GROUPS, S5_GROUP, S5_STATE = 64, 16, 64
LANE_BLOCK = 128
S5_BLOCKS = D_MODEL // LANE_BLOCK
GROUPS_PER_BLOCK = LANE_BLOCK // S5_GROUP
STATE_BLOCK = GROUPS_PER_BLOCK * S5_STATE
SCAN_ROWS = 256
CONV_E = 2048
POOL_E = 2048
POOL_WINDOWS = (2, 4, 8, 16)
POOL_GROUP = 512
POOL_HALO = 16
N_CHIPS = 4

ADAM_LR, ADAM_B1, ADAM_B2, ADAM_EPS, ADAM_WD, ADAM_STEP = 0.001, 0.9, 0.999, 1e-08, 0.01, 10

VMEM_LIMIT = 48 * 1024 * 1024

WEIGHTS = ['meta_tokens', 'norm0_g', 'l0_w_in', 'l0_lam_re', 'l0_lam_im', 'l0_log_dt', 'l0_b_re', 'l0_b_im',
           'l0_c_re', 'l0_c_im', 'l0_d_skip', 'l0_w_glu', 'l0_b_glu', 'l0_w_out', 'norm1_g', 'l1_w_in',
           'l1_conv_w', 'l1_conv_b', 'l1_w_out', 'norm2_g', 'l2_w_in', 'l2_w_grp', 'l2_b_grp', 'l2_scale',
           'l2_w_out', 'norm3_g', 'l3_w_in', 'l3_lam_re', 'l3_lam_im', 'l3_log_dt', 'l3_b_re', 'l3_b_im',
           'l3_c_re', 'l3_c_im', 'l3_d_skip', 'l3_w_glu', 'l3_b_glu', 'l3_w_out', 'final_g']
BIG = ['l0_w_in', 'l0_w_glu', 'l0_w_out', 'l1_w_in', 'l1_w_out', 'l2_w_in', 'l2_w_grp', 'l2_w_out',
       'l3_w_in', 'l3_w_glu', 'l3_w_out']
SMALL_SHARDED = ['meta_tokens', 'l1_conv_w', 'l2_b_grp']
SMALL = [n for n in WEIGHTS if n not in BIG]


def _params(*sem):
    return pltpu.CompilerParams(dimension_semantics=sem, vmem_limit_bytes=VMEM_LIMIT)


def _tile(n, cap, mult):
    best = None
    for t in range(mult, min(n, cap) + 1, mult):
        if n % t == 0:
            best = t
    assert best is not None, (n, cap, mult)
    return best


def mm_nn(a, b, *, name, bias=None, add=None, out_dtype=F32):
    m, k = a.shape
    s, _, ns = b.shape
    n = s * ns
    tm, tn = _tile(m, 1088, 16), _tile(ns, 512, 128)
    per = ns // tn

    def body(*refs):
        a_ref, b_ref = refs[0], refs[1]
        o_ref = refs[-1]
        acc = jnp.dot(a_ref[...].astype(BF16), b_ref[...], preferred_element_type=F32)
        pos = 2
        if bias is not None:
            acc = acc + refs[pos][...]
            pos += 1
        if add is not None:
            acc = acc + refs[pos][...]
        o_ref[...] = acc.astype(o_ref.dtype)

    in_specs = [pl.BlockSpec((tm, k), lambda i, j: (i, 0)),
                pl.BlockSpec((None, k, tn), lambda i, j: (j // per, 0, j % per))]
    args = [a, b]
    if bias is not None:
        in_specs.append(pl.BlockSpec((1, tn), lambda i, j: (0, j)))
        args.append(bias)
    if add is not None:
        in_specs.append(pl.BlockSpec((tm, tn), lambda i, j: (i, j)))
        args.append(add)
    return pl.pallas_call(
        body, name=name, grid=(m // tm, n // tn), in_specs=in_specs,
        out_specs=pl.BlockSpec((tm, tn), lambda i, j: (i, j)),
        out_shape=jax.ShapeDtypeStruct((m, n), out_dtype),
        compiler_params=_params("parallel", "parallel"))(*args)


def mm_nt(g, w, *, name):
    m, kc = g.shape
    s, nout, kcs = w.shape
    assert s * kcs == kc
    tm, tno, tk = _tile(m, 1088, 16), _tile(nout, 1024, 128), _tile(kcs, 1024, 128)
    per = kcs // tk

    def body(g_ref, w_ref, o_ref):
        kk = pl.program_id(2)
        part = lax.dot_general(g_ref[...].astype(BF16), w_ref[...], (((1,), (1,)), ((), ())),
                               preferred_element_type=F32)

        @pl.when(kk == 0)
        def _():
            o_ref[...] = part

        @pl.when(kk > 0)
        def _():
            o_ref[...] += part

    return pl.pallas_call(
        body, name=name, grid=(m // tm, nout // tno, kc // tk),
        in_specs=[pl.BlockSpec((tm, tk), lambda i, j, kk: (i, kk)),
                  pl.BlockSpec((None, tno, tk), lambda i, j, kk: (kk // per, j, kk % per))],
        out_specs=pl.BlockSpec((tm, tno), lambda i, j, kk: (i, j)),
        out_shape=jax.ShapeDtypeStruct((m, nout), F32),
        compiler_params=_params("parallel", "parallel", "arbitrary"))(g, w)


def mm_tn(a, g, shards, *, name):
    m, ka = a.shape
    n = g.shape[1]
    ns = n // shards
    tm, tka, tn = _tile(m, 1088, 16), _tile(ka, 1024, 128), _tile(ns, 1024, 128)
    per = ns // tn

    def body(a_ref, g_ref, o_ref):
        mm = pl.program_id(2)
        part = lax.dot_general(a_ref[...].astype(BF16), g_ref[...].astype(BF16), (((0,), (0,)), ((), ())),
                               preferred_element_type=F32)

        @pl.when(mm == 0)
        def _():
            o_ref[...] = part

        @pl.when(mm > 0)
        def _():
            o_ref[...] += part

    return pl.pallas_call(
        body, name=name, grid=(ka // tka, n // tn, m // tm),
        in_specs=[pl.BlockSpec((tm, tka), lambda i, j, mm: (mm, i)),
                  pl.BlockSpec((tm, tn), lambda i, j, mm: (mm, j))],
        out_specs=pl.BlockSpec((None, tka, tn), lambda i, j, mm: (j // per, i, j % per)),
        out_shape=jax.ShapeDtypeStruct((shards, ka, ns), F32),
        compiler_params=_params("parallel", "parallel", "arbitrary"))(a, g)


def grp_nn(a, w, bias, *, name):
    m = a.shape[0]
    tm = _tile(m, 1088, 16)
    ng = len(POOL_WINDOWS)

    def body(a_ref, w_ref, b_ref, o_ref):
        wj = w_ref[...].reshape(POOL_GROUP, POOL_GROUP)
        o_ref[...] = jnp.dot(a_ref[...].astype(BF16), wj, preferred_element_type=F32) + b_ref[...]

    return pl.pallas_call(
        body, name=name, grid=(m // tm, ng),
        in_specs=[pl.BlockSpec((tm, POOL_GROUP), lambda i, j: (i, j)),
                  pl.BlockSpec((N_CHIPS, None, POOL_GROUP // N_CHIPS, POOL_GROUP), lambda i, j: (0, j, 0, 0)),
                  pl.BlockSpec((1, POOL_GROUP), lambda i, j: (0, j))],
        out_specs=pl.BlockSpec((tm, POOL_GROUP), lambda i, j: (i, j)),
        out_shape=jax.ShapeDtypeStruct((m, ng * POOL_GROUP), F32),
        compiler_params=_params("parallel", "parallel"))(a, w, bias)


def grp_nt(g, w, *, name):
    m = g.shape[0]
    tm = _tile(m, 1088, 16)
    ng = len(POOL_WINDOWS)

    def body(g_ref, w_ref, o_ref):
        wj = w_ref[...].reshape(POOL_GROUP, POOL_GROUP)
        o_ref[...] = lax.dot_general(g_ref[...].astype(BF16), wj, (((1,), (1,)), ((), ())),
                                     preferred_element_type=F32)

    return pl.pallas_call(
        body, name=name, grid=(m // tm, ng),
        in_specs=[pl.BlockSpec((tm, POOL_GROUP), lambda i, j: (i, j)),
                  pl.BlockSpec((N_CHIPS, None, POOL_GROUP // N_CHIPS, POOL_GROUP), lambda i, j: (0, j, 0, 0))],
        out_specs=pl.BlockSpec((tm, POOL_GROUP), lambda i, j: (i, j)),
        out_shape=jax.ShapeDtypeStruct((m, ng * POOL_GROUP), F32),
        compiler_params=_params("parallel", "parallel"))(g, w)


def grp_tn(a, g, *, name):
    m = a.shape[0]
    tm = _tile(m, 1088, 16)
    ng = len(POOL_WINDOWS)
    rows = POOL_GROUP // N_CHIPS

    def body(a_ref, g_ref, o_ref):
        mm = pl.program_id(1)
        part = lax.dot_general(a_ref[...].astype(BF16), g_ref[...].astype(BF16), (((0,), (0,)), ((), ())),
                               preferred_element_type=F32).reshape(N_CHIPS, rows, POOL_GROUP)

        @pl.when(mm == 0)
        def _():
            o_ref[...] = part

        @pl.when(mm > 0)
        def _():
            o_ref[...] += part

    return pl.pallas_call(
        body, name=name, grid=(ng, m // tm),
        in_specs=[pl.BlockSpec((tm, POOL_GROUP), lambda j, mm: (mm, j)),
                  pl.BlockSpec((tm, POOL_GROUP), lambda j, mm: (mm, j))],
        out_specs=pl.BlockSpec((N_CHIPS, None, rows, POOL_GROUP), lambda j, mm: (0, j, 0, 0)),
        out_shape=jax.ShapeDtypeStruct((N_CHIPS, ng, rows, POOL_GROUP), F32),
        compiler_params=_params("parallel", "arbitrary"))(a, g)


def rowwise(fn, rows, bcast, outs, accs=(), *, name, aliases=None):
    m = rows[0][0].shape[0]
    tr = m // 16
    n_in = len(rows) + len(bcast)

    def body(*refs):
        vals = [r[...] for r in refs[:n_in]]
        o_vals, a_vals = fn(*vals)
        for ref, val in zip(refs[n_in:n_in + len(outs)], o_vals, strict=True):
            ref[...] = val.astype(ref.dtype)
        step = pl.program_id(0)
        for ref, val in zip(refs[n_in + len(outs):], a_vals, strict=True):
            @pl.when(step == 0)
            def _(ref=ref, val=val):
                ref[...] = val

            @pl.when(step > 0)
            def _(ref=ref, val=val):
                ref[...] += val

    in_specs = [pl.BlockSpec((tr, w), functools.partial(lambda i, cb: (i, cb), cb=cb)) for _, w, cb in rows]
    in_specs += [pl.BlockSpec(b.shape, lambda i: (0, 0)) for b in bcast]
    out_specs = [pl.BlockSpec((tr, w), functools.partial(lambda i, cb: (i, cb), cb=cb)) for _, _, w, cb in outs]
    out_specs += [pl.BlockSpec((1, w), lambda i: (0, 0)) for w in accs]
    out_shape = [jax.ShapeDtypeStruct((m, cols), dt) for cols, dt, _, _ in outs]
    out_shape += [jax.ShapeDtypeStruct((1, w), F32) for w in accs]
    res = pl.pallas_call(
        body, name=name, grid=(m // tr,), in_specs=in_specs, out_specs=out_specs, out_shape=out_shape,
        input_output_aliases=aliases or {},
        compiler_params=_params("arbitrary"))(*[r[0] for r in rows], *bcast)
    return res[:len(outs)], res[len(outs):]


def _rms(h, g):
    return h * lax.rsqrt(jnp.mean(h * h, axis=-1, keepdims=True) + EPS) * g


def _colsum(v):
    return jnp.sum(v, axis=0, keepdims=True)


def rms_fwd(h, g, *, name):
    (n,), _ = rowwise(lambda hv, gv: ([_rms(hv, gv)], []), [(h, D_MODEL, 0)], [g],
                      [(D_MODEL, BF16, D_MODEL, 0)], name=name)
    return n


def rms_bwd(h, g, dn, dh_out, *, name):
    def fn(hv, dnv, dhv, gv):
        _, vjp = jax.vjp(_rms, hv, gv)
        dh, dg = vjp(dnv)
        return [dhv + dh], [dg]

    (dh,), (dg,) = rowwise(fn, [(h, D_MODEL, 0), (dn, D_MODEL, 0), (dh_out, D_MODEL, 0)], [g],
                           [(D_MODEL, F32, D_MODEL, 0)], [D_MODEL], name=name)
    return dh, dg


def _s5_gate(y0, q, z):
    yg = jax.nn.gelu(y0)
    return yg * jax.nn.sigmoid(q) * jax.nn.silu(z)


def _pool_gate(o, z, scale):
    return o * scale * jax.nn.silu(z)


def _shift_rows(v, d, down):
    t = v.shape[0]
    row = lax.broadcasted_iota(jnp.int32, v.shape, 0)
    if down:
        return jnp.where(row >= d, pltpu.roll(v, d, 0), 0.0)
    return jnp.where(row < t - d, pltpu.roll(v, t - d, 0), 0.0)


def _scan(sr, si, ar, ai, down):
    pr, pi = ar, ai
    d = 1
    while d < sr.shape[0]:
        hr, hi = _shift_rows(sr, d, down), _shift_rows(si, d, down)
        sr, si = sr + pr * hr - pi * hi, si + pr * hi + pi * hr
        pr, pi = pr * pr - pi * pi, 2.0 * pr * pi
        d *= 2
    return sr, si


def s5_fwd(uz, bblk, cblk, ar, ai, dskip, *, name):
    lp = uz.shape[0]
    t = SCAN_ROWS
    nst = S5_GROUPS * S5_STATE

    def body(u_ref, b_ref, c_ref, ar_ref, ai_ref, d_ref, y_ref, sr_ref, si_ref, cr, ci):
        step = pl.program_id(1)

        @pl.when(step == 0)
        def _():
            cr[...] = jnp.zeros_like(cr)
            ci[...] = jnp.zeros_like(ci)

        u = u_ref[...]
        a_r, a_i = ar_ref[...], ai_ref[...]
        x = jnp.dot(u.astype(BF16), b_ref[...].astype(BF16), preferred_element_type=F32)
        xr, xi = x[:, :STATE_BLOCK], x[:, STATE_BLOCK:]
        first = lax.broadcasted_iota(jnp.int32, xr.shape, 0) == 0
        xr = xr + jnp.where(first, a_r * cr[...] - a_i * ci[...], 0.0)
        xi = xi + jnp.where(first, a_r * ci[...] + a_i * cr[...], 0.0)
        sr, si = _scan(xr, xi, a_r, a_i, True)
        sr_ref[...] = sr
        si_ref[...] = si
        cr[...] = sr[t - 1:t, :]
        ci[...] = si[t - 1:t, :]
        s = jnp.concatenate([sr, si], axis=1).astype(BF16)
        y_ref[...] = jnp.dot(s, c_ref[...].astype(BF16), preferred_element_type=F32) + d_ref[...] * u

    return pl.pallas_call(
        body, name=name, grid=(S5_BLOCKS, lp // t),
        in_specs=[pl.BlockSpec((t, LANE_BLOCK), lambda b, c: (c, b)),
                  pl.BlockSpec((None, LANE_BLOCK, 2 * STATE_BLOCK), lambda b, c: (b, 0, 0)),
                  pl.BlockSpec((None, 2 * STATE_BLOCK, LANE_BLOCK), lambda b, c: (b, 0, 0)),
                  pl.BlockSpec((1, STATE_BLOCK), lambda b, c: (0, b)),
                  pl.BlockSpec((1, STATE_BLOCK), lambda b, c: (0, b)),
                  pl.BlockSpec((1, LANE_BLOCK), lambda b, c: (0, b))],
        out_specs=[pl.BlockSpec((t, LANE_BLOCK), lambda b, c: (c, b)),
                   pl.BlockSpec((t, STATE_BLOCK), lambda b, c: (c, b)),
                   pl.BlockSpec((t, STATE_BLOCK), lambda b, c: (c, b))],
        out_shape=[jax.ShapeDtypeStruct((lp, D_MODEL), F32), jax.ShapeDtypeStruct((lp, nst), F32),
                   jax.ShapeDtypeStruct((lp, nst), F32)],
        scratch_shapes=[pltpu.VMEM((1, STATE_BLOCK), F32), pltpu.VMEM((1, STATE_BLOCK), F32)],
        compiler_params=_params("parallel", "arbitrary"))(uz, bblk, cblk, ar, ai, dskip)


def s5_bwd(dy0, uz, sr, si, bblk, cblk, ar, ai, dskip, dp, *, name):
    lp = uz.shape[0]
    t = SCAN_ROWS
    nch = lp // t
    nst = S5_GROUPS * S5_STATE

    def body(dy_ref, u_ref, sr_ref, si_ref, b_ref, c_ref, ar_ref, ai_ref, d_ref, _, du_ref, db_ref, dc_ref,
             dar_ref, dai_ref, dd_ref, cr, ci):
        step = pl.program_id(1)

        @pl.when(step == 0)
        def _():
            cr[...] = jnp.zeros_like(cr)
            ci[...] = jnp.zeros_like(ci)

        dy, u = dy_ref[...], u_ref[...]
        dyb, ub = dy.astype(BF16), u.astype(BF16)
        a_r, a_i = ar_ref[...], -ai_ref[...]
        g = lax.dot_general(dyb, c_ref[...].astype(BF16), (((1,), (1,)), ((), ())), preferred_element_type=F32)
        gr, gi = g[:, :STATE_BLOCK], g[:, STATE_BLOCK:]
        nxt_r, nxt_i = cr[...], ci[...]
        last = lax.broadcasted_iota(jnp.int32, gr.shape, 0) == t - 1
        gr = gr + jnp.where(last, a_r * nxt_r - a_i * nxt_i, 0.0)
        gi = gi + jnp.where(last, a_r * nxt_i + a_i * nxt_r, 0.0)
        lr, li = _scan(gr, gi, a_r, a_i, False)
        cr[...] = lr[0:1, :]
        ci[...] = li[0:1, :]
        nr = _shift_rows(lr, 1, False) + jnp.where(last, nxt_r, 0.0)
        ni = _shift_rows(li, 1, False) + jnp.where(last, nxt_i, 0.0)
        s_r, s_i = sr_ref[...], si_ref[...]
        dar = _colsum(s_r * nr + s_i * ni)
        dai = _colsum(s_r * ni - s_i * nr)
        lam = jnp.concatenate([lr, li], axis=1).astype(BF16)
        sb = jnp.concatenate([s_r, s_i], axis=1).astype(BF16)
        dc = lax.dot_general(sb, dyb, (((0,), (0,)), ((), ())), preferred_element_type=F32)
        db = lax.dot_general(ub, lam, (((0,), (0,)), ((), ())), preferred_element_type=F32)
        du_ref[...] = lax.dot_general(lam, b_ref[...].astype(BF16), (((1,), (1,)), ((), ())),
                                      preferred_element_type=F32) + d_ref[...] * dy
        dd = _colsum(dy * u)

        @pl.when(step == 0)
        def _():
            db_ref[...] = db
            dc_ref[...] = dc
            dar_ref[...] = dar
            dai_ref[...] = dai
            dd_ref[...] = dd

        @pl.when(step > 0)
        def _():
            db_ref[...] += db
            dc_ref[...] += dc
            dar_ref[...] += dar
            dai_ref[...] += dai
            dd_ref[...] += dd

    rev = lambda b, c: (nch - 1 - c, b)
    return pl.pallas_call(
        body, name=name, grid=(S5_BLOCKS, nch),
        in_specs=[pl.BlockSpec((t, LANE_BLOCK), rev), pl.BlockSpec((t, LANE_BLOCK), rev),
                  pl.BlockSpec((t, STATE_BLOCK), rev), pl.BlockSpec((t, STATE_BLOCK), rev),
                  pl.BlockSpec((None, LANE_BLOCK, 2 * STATE_BLOCK), lambda b, c: (b, 0, 0)),
                  pl.BlockSpec((None, 2 * STATE_BLOCK, LANE_BLOCK), lambda b, c: (b, 0, 0)),
                  pl.BlockSpec((1, STATE_BLOCK), lambda b, c: (0, b)),
                  pl.BlockSpec((1, STATE_BLOCK), lambda b, c: (0, b)),
                  pl.BlockSpec((1, LANE_BLOCK), lambda b, c: (0, b)),
                  pl.BlockSpec(memory_space=pl.ANY)],
        out_specs=[pl.BlockSpec((t, LANE_BLOCK), rev),
                   pl.BlockSpec((None, LANE_BLOCK, 2 * STATE_BLOCK), lambda b, c: (b, 0, 0)),
                   pl.BlockSpec((None, 2 * STATE_BLOCK, LANE_BLOCK), lambda b, c: (b, 0, 0)),
                   pl.BlockSpec((1, STATE_BLOCK), lambda b, c: (0, b)),
                   pl.BlockSpec((1, STATE_BLOCK), lambda b, c: (0, b)),
                   pl.BlockSpec((1, LANE_BLOCK), lambda b, c: (0, b))],
        out_shape=[jax.ShapeDtypeStruct(dp.shape, F32),
                   jax.ShapeDtypeStruct((S5_BLOCKS, LANE_BLOCK, 2 * STATE_BLOCK), F32),
                   jax.ShapeDtypeStruct((S5_BLOCKS, 2 * STATE_BLOCK, LANE_BLOCK), F32),
                   jax.ShapeDtypeStruct((1, nst), F32), jax.ShapeDtypeStruct((1, nst), F32),
                   jax.ShapeDtypeStruct((1, D_MODEL), F32)],
        scratch_shapes=[pltpu.VMEM((1, STATE_BLOCK), F32), pltpu.VMEM((1, STATE_BLOCK), F32)],
        input_output_aliases={9: 0},
        compiler_params=_params("parallel", "arbitrary"))(dy0, uz, sr, si, bblk, cblk, ar, ai, dskip, dp)


def s5_tables(lam_re, lam_im, log_dt, b_re, b_im, c_re, c_im):
    dt = jnp.exp(log_dt)[:, None]
    mag = jnp.exp(lam_re * dt)
    ar = mag * jnp.cos(lam_im * dt)
    ai = mag * jnp.sin(lam_im * dt)
    den = lam_re * lam_re + lam_im * lam_im
    kr = ((ar - 1.0) * lam_re + ai * lam_im) / den
    ki = (ai * lam_re - (ar - 1.0) * lam_im) / den
    bbr = kr[..., None] * b_re - ki[..., None] * b_im
    bbi = kr[..., None] * b_im + ki[..., None] * b_re
    eye = jnp.eye(GROUPS_PER_BLOCK, dtype=F32)

    def b_blocks(v):
        v = v.reshape(S5_BLOCKS, GROUPS_PER_BLOCK, S5_STATE, S5_GROUP)
        return jnp.einsum('bgpi,gh->bgihp', v, eye).reshape(S5_BLOCKS, LANE_BLOCK, STATE_BLOCK)

    def c_blocks(v):
        v = v.reshape(S5_BLOCKS, GROUPS_PER_BLOCK, S5_GROUP, S5_STATE)
        return jnp.einsum('bgip,gh->bgphi', v, eye).reshape(S5_BLOCKS, STATE_BLOCK, LANE_BLOCK)

    bblk = jnp.concatenate([b_blocks(bbr), b_blocks(bbi)], axis=2)
    cblk = jnp.concatenate([c_blocks(c_re), -c_blocks(c_im)], axis=1)
    nst = S5_GROUPS * S5_STATE
    return ar.reshape(1, nst), ai.reshape(1, nst), bblk, cblk


def _silu_grad(z):
    s = jax.nn.sigmoid(z)
    return s * (1.0 + z * (1.0 - s))


def conv_fwd(p, conv_w, conv_b, *, name):
    lp = p.shape[0]
    tr = lp // 16
    e = CONV_E
    hb = tr // 8

    def body(p_ref, cgh_ref, vh_ref, w_ref, b_ref, o_ref):
        i = pl.program_id(0)
        bg, cg, v, z = (p_ref[:, k * e:(k + 1) * e] for k in range(4))
        hc = cg * v
        halo = jnp.where(i > 0, cgh_ref[...] * vh_ref[...], 0.0)
        ext = jnp.concatenate([halo, hc], axis=0)
        w = w_ref[...]
        conv = (w[0:1] * pltpu.roll(ext, 2, 0)[8:] + w[1:2] * pltpu.roll(ext, 1, 0)[8:] + w[2:3] * hc
                + b_ref[...])
        o_ref[...] = (bg * conv * jax.nn.silu(z)).astype(o_ref.dtype)

    prev = lambda col: (lambda i: (jnp.maximum(i * hb - 1, 0), col))
    return pl.pallas_call(
        body, name=name, grid=(lp // tr,),
        in_specs=[pl.BlockSpec((tr, 4 * e), lambda i: (i, 0)),
                  pl.BlockSpec((8, e), prev(1)), pl.BlockSpec((8, e), prev(2)),
                  pl.BlockSpec((3, e), lambda i: (0, 0)), pl.BlockSpec((1, e), lambda i: (0, 0))],
        out_specs=pl.BlockSpec((tr, e), lambda i: (i, 0)),
        out_shape=jax.ShapeDtypeStruct((lp, e), BF16),
        compiler_params=_params("parallel"))(p, p, p, conv_w, conv_b)


def conv_bwd(dy3, p, conv_w, conv_b, *, name):
    lp = p.shape[0]
    tr = lp // 32
    e = CONV_E
    hb = tr // 8
    nt = lp // tr

    def body(dy_ref, p_ref, cgh_ref, vh_ref, dyn_ref, bgn_ref, zn_ref, w_ref, b_ref, dp_ref, dw_ref, db_ref):
        i = pl.program_id(0)
        dy = dy_ref[...]
        bg, cg, v, z = (p_ref[:, k * e:(k + 1) * e] for k in range(4))
        w = w_ref[...]
        hc = cg * v
        halo = jnp.where(i > 0, cgh_ref[...] * vh_ref[...], 0.0)
        ext = jnp.concatenate([halo, hc], axis=0)
        hc2, hc1 = pltpu.roll(ext, 2, 0)[8:], pltpu.roll(ext, 1, 0)[8:]
        yc = w[0:1] * hc2 + w[1:2] * hc1 + w[2:3] * hc + b_ref[...]
        sz = jax.nn.silu(z)
        dyc = dy * bg * sz
        dyc_n = jnp.where(i < nt - 1, dyn_ref[...] * bgn_ref[...] * jax.nn.silu(zn_ref[...]), 0.0)
        ext_n = jnp.concatenate([dyc, dyc_n], axis=0)
        n_rows = tr + 8
        dhc = (w[2:3] * dyc + w[1:2] * pltpu.roll(ext_n, n_rows - 1, 0)[:tr]
               + w[0:1] * pltpu.roll(ext_n, n_rows - 2, 0)[:tr])
        dp_ref[:, 0:e] = dy * yc * sz
        dp_ref[:, e:2 * e] = dhc * v
        dp_ref[:, 2 * e:3 * e] = dhc * cg
        dp_ref[:, 3 * e:4 * e] = dy * bg * yc * _silu_grad(z)
        dw = jnp.concatenate([_colsum(dyc * hc2), _colsum(dyc * hc1), _colsum(dyc * hc)], axis=0)
        db = _colsum(dyc)

        @pl.when(i == 0)
        def _():
            dw_ref[...] = dw
            db_ref[...] = db

        @pl.when(i > 0)
        def _():
            dw_ref[...] += dw
            db_ref[...] += db

    prev = lambda col: (lambda i: (jnp.maximum(i * hb - 1, 0), col))
    nxt = lambda col: (lambda i: (jnp.minimum((i + 1) * hb, lp // 8 - 1), col))
    return pl.pallas_call(
        body, name=name, grid=(nt,),
        in_specs=[pl.BlockSpec((tr, e), lambda i: (i, 0)), pl.BlockSpec((tr, 4 * e), lambda i: (i, 0)),
                  pl.BlockSpec((8, e), prev(1)), pl.BlockSpec((8, e), prev(2)),
                  pl.BlockSpec((8, e), nxt(0)), pl.BlockSpec((8, e), nxt(0)), pl.BlockSpec((8, e), nxt(3)),
                  pl.BlockSpec((3, e), lambda i: (0, 0)), pl.BlockSpec((1, e), lambda i: (0, 0))],
        out_specs=[pl.BlockSpec((tr, 4 * e), lambda i: (i, 0)), pl.BlockSpec((3, e), lambda i: (0, 0)),
                   pl.BlockSpec((1, e), lambda i: (0, 0))],
        out_shape=[jax.ShapeDtypeStruct((lp, 4 * e), F32), jax.ShapeDtypeStruct((3, e), F32),
                   jax.ShapeDtypeStruct((1, e), F32)],
        compiler_params=_params("arbitrary"))(dy3, p, p, p, dy3, p, p, conv_w, conv_b)


def _pool_counts(i, tr, rows, offset, w):
    t = (i * tr + offset + 1 + lax.broadcasted_iota(jnp.int32, (rows, 1), 0)).astype(F32)
    return jnp.minimum(t, float(w))


def pool_fwd(p, *, name):
    lp = p.shape[0]
    tr = lp // 16
    h = POOL_HALO
    hb = tr // h

    def body(u_ref, uh_ref, o_ref):
        i = pl.program_id(0)
        u = u_ref[...]
        ext = jnp.concatenate([jnp.where(i > 0, uh_ref[...], 0.0), u], axis=0)
        for k, w in enumerate(POOL_WINDOWS):
            cols = slice(k * POOL_GROUP, (k + 1) * POOL_GROUP)
            s = ext[:, cols]
            d = 1
            while d < w:
                s = s + pltpu.roll(s, d, 0)
                d *= 2
            o_ref[:, cols] = (s[h:] / _pool_counts(i, tr, tr, 0, w) - u[:, cols]).astype(o_ref.dtype)

    return pl.pallas_call(
        body, name=name, grid=(lp // tr,),
        in_specs=[pl.BlockSpec((tr, POOL_E), lambda i: (i, 0)),
                  pl.BlockSpec((h, POOL_E), lambda i: (jnp.maximum(i * hb - 1, 0), 0))],
        out_specs=pl.BlockSpec((tr, POOL_E), lambda i: (i, 0)),
        out_shape=jax.ShapeDtypeStruct((lp, POOL_E), BF16),
        compiler_params=_params("parallel"))(p, p)


def pool_bwd(dmix, dp, *, name):
    lp = dmix.shape[0]
    tr = lp // 16
    h = POOL_HALO
    hb = tr // h
    nt = lp // tr

    def body(dm_ref, dmn_ref, _, du_ref):
        i = pl.program_id(0)
        dm = dm_ref[...]
        dmn = jnp.where(i < nt - 1, dmn_ref[...], 0.0)
        n_rows = tr + h
        for k, w in enumerate(POOL_WINDOWS):
            cols = slice(k * POOL_GROUP, (k + 1) * POOL_GROUP)
            s = jnp.concatenate([dm[:, cols] / _pool_counts(i, tr, tr, 0, w),
                                 dmn[:, cols] / _pool_counts(i, tr, h, tr, w)], axis=0)
            d = 1
            while d < w:
                s = s + pltpu.roll(s, n_rows - d, 0)
                d *= 2
            du_ref[:, cols] = s[:tr] - dm[:, cols]

    return pl.pallas_call(
        body, name=name, grid=(nt,),
        in_specs=[pl.BlockSpec((tr, POOL_E), lambda i: (i, 0)),
                  pl.BlockSpec((h, POOL_E), lambda i: (jnp.minimum((i + 1) * hb, lp // h - 1), 0)),
                  pl.BlockSpec(memory_space=pl.ANY)],
        out_specs=pl.BlockSpec((tr, POOL_E), lambda i: (i, 0)),
        out_shape=jax.ShapeDtypeStruct(dp.shape, F32),
        input_output_aliases={2: 0},
        compiler_params=_params("parallel"))(dmix, dmix, dp)


def adamw(w, g, m, v, *, name):
    r, c = w.shape
    tr = _tile(r, max(8, (1 << 19) // c), 8)

    def body(w_ref, g_ref, m_ref, v_ref, d_ref, nm_ref, nv_ref):
        gv = g_ref[...]
        nm = ADAM_B1 * m_ref[...] + (1.0 - ADAM_B1) * gv
        nv = ADAM_B2 * v_ref[...] + (1.0 - ADAM_B2) * jnp.square(gv)
        m_hat = nm / (1.0 - ADAM_B1 ** ADAM_STEP)
        v_hat = nv / (1.0 - ADAM_B2 ** ADAM_STEP)
        d_ref[...] = -ADAM_LR * (m_hat / (jnp.sqrt(v_hat) + ADAM_EPS) + ADAM_WD * w_ref[...])
        nm_ref[...] = nm
        nv_ref[...] = nv

    spec = pl.BlockSpec((tr, c), lambda i: (i, 0))
    return pl.pallas_call(
        body, name=name, grid=(r // tr,), in_specs=[spec] * 4, out_specs=[spec] * 3,
        out_shape=[jax.ShapeDtypeStruct((r, c), F32)] * 3,
        compiler_params=_params("parallel"))(w, g, m, v)


def add_n(parts, *, name, out_dtype=F32):
    if isinstance(parts, (list, tuple)):
        arrs = list(parts)
        r, c = arrs[0].shape
        specs_of = lambda tr: [pl.BlockSpec((tr, c), lambda i: (i, 0)) for _ in arrs]
    else:
        n, r, c = parts.shape
        arrs = [parts] * n
        specs_of = lambda tr: [pl.BlockSpec((None, tr, c), functools.partial(lambda i, s: (s, i, 0), s=s))
                               for s in range(n)]
    tr = _tile(r, max(16, (1 << 19) // c), 16)

    def body(*refs):
        acc = refs[0][...].astype(F32)
        for ref in refs[1:-1]:
            acc = acc + ref[...].astype(F32)
        refs[-1][...] = acc.astype(refs[-1].dtype)

    return pl.pallas_call(
        body, name=name, grid=(r // tr,), in_specs=specs_of(tr),
        out_specs=pl.BlockSpec((tr, c), lambda i: (i, 0)),
        out_shape=jax.ShapeDtypeStruct((r, c), out_dtype),
        compiler_params=_params("parallel"))(*arrs)


ANY = pl.BlockSpec(memory_space=pl.ANY)


def _place():
    x, y, c = lax.axis_index("x"), lax.axis_index("y"), lax.axis_index("c")
    return x, y, c, [(1 - x, y), (x, 1 - y), (1 - x, 1 - y)]


DMA_PIECE_BYTES = 512 * 1024


def _pieces(src, dst):
    shape = src.shape
    rows, cols = shape[-2], shape[-1]
    item = jnp.dtype(src.dtype).itemsize
    unit = 32 // item
    want = max(1, (rows * cols * item) // DMA_PIECE_BYTES)
    n = 1
    if rows % unit == 0:
        n = max(d for d in range(1, rows // unit + 1) if (rows // unit) % d == 0 and d <= want)
    size = rows // n
    out = []
    for lead in (range(shape[0]) if len(shape) == 3 else [None]):
        for i in range(n):
            sel = (pl.ds(i * size, size), slice(None)) if lead is None else (lead, pl.ds(i * size, size), slice(None))
            out.append((src.at[sel], dst.at[sel]))
    return out


def _send(src, dst, send_sem, recv_sem, peer):
    for s, d in _pieces(src, dst):
        pltpu.make_async_remote_copy(src_ref=s, dst_ref=d, send_sem=send_sem, recv_sem=recv_sem,
                                     device_id=peer, device_id_type=MESH).start()
    return pltpu.make_async_remote_copy(src_ref=src, dst_ref=dst, send_sem=send_sem, recv_sem=recv_sem,
                                        device_id=peer, device_id_type=MESH)


def _copy(src, dst, sem):
    for s, d in _pieces(src, dst):
        pltpu.make_async_copy(s, d, sem).start()
    return pltpu.make_async_copy(src, dst, sem)


def gather_chips(shards, *, name):
    n = len(shards)

    def body(*refs):
        ins, outs = refs[:n], refs[n:2 * n]
        send_sems, recv_sems, local_sems = refs[2 * n:]
        x, y, c, chips = _place()
        k = 2 * x + y
        copies = []
        for a in range(n):
            copies.append(_copy(ins[a], outs[a].at[k], local_sems.at[a]))
            for j, (px, py) in enumerate(chips):
                copies.append(_send(ins[a], outs[a].at[k], send_sems.at[a, j], recv_sems.at[a, j], (px, py, c)))
        for cp in copies:
            cp.wait()

    return pl.pallas_call(
        body, name=name, in_specs=[ANY] * n, out_specs=[ANY] * n,
        out_shape=[jax.ShapeDtypeStruct((N_CHIPS,) + s.shape, s.dtype) for s in shards],
        scratch_shapes=[pltpu.SemaphoreType.DMA((n, 3)), pltpu.SemaphoreType.DMA((n, 3)),
                        pltpu.SemaphoreType.DMA((n,))])(*shards)


def gather_chips_by_halves(shards, *, name):
    n = len(shards)

    def body(*refs):
        ins, outs = refs[:n], refs[n:2 * n]
        send_sems, recv_sems, local_sems = refs[2 * n:]
        x, y, c, chips = _place()
        k = 2 * x + y
        sibling = (x, y, 1 - c)
        waits, arrivals = [], []
        for a in range(n):
            half = ins[a].shape[0] // 2
            waits.append(_copy(ins[a], outs[a].at[k], local_sems.at[a]))
            mine = pl.ds(c * half, half)
            for j, (px, py) in enumerate(chips):
                arrivals.append(_send(ins[a].at[mine, :], outs[a].at[k, mine, :], send_sems.at[a, j],
                                      recv_sems.at[a, j], (px, py, c)))
        i = 0
        for a in range(n):
            half = ins[a].shape[0] // 2
            mine = pl.ds(c * half, half)
            for j, (px, py) in enumerate(chips):
                arrivals[i].wait_recv()
                landed = outs[a].at[2 * px + py, mine, :]
                waits.append(_send(landed, landed, send_sems.at[a, 3 + j], recv_sems.at[a, 3 + j], sibling))
                i += 1
        for cp in arrivals:
            cp.wait_send()
        for cp in waits:
            cp.wait()

    return pl.pallas_call(
        body, name=name, in_specs=[ANY] * n, out_specs=[ANY] * n,
        out_shape=[jax.ShapeDtypeStruct((N_CHIPS,) + s.shape, s.dtype) for s in shards],
        scratch_shapes=[pltpu.SemaphoreType.DMA((n, 6)), pltpu.SemaphoreType.DMA((n, 6)),
                        pltpu.SemaphoreType.DMA((n,))])(*shards)


def swap_halves(grads, *, name):
    n = len(grads)

    def body(*refs):
        ins, mine_out, theirs_out = refs[:n], refs[n:2 * n], refs[2 * n:3 * n]
        send_sems, recv_sems, local_sems = refs[3 * n:]
        x, y, c, _ = _place()
        copies = []
        for a in range(n):
            half = ins[a].shape[1] // 2
            copies.append(_send(ins[a].at[:, pl.ds((1 - c) * half, half), :], theirs_out[a], send_sems.at[a],
                                recv_sems.at[a], (x, y, 1 - c)))
            copies.append(_copy(ins[a].at[:, pl.ds(c * half, half), :], mine_out[a], local_sems.at[a]))
        for cp in copies:
            cp.wait()

    halves = [jax.ShapeDtypeStruct((g.shape[0], g.shape[1] // 2, g.shape[2]), g.dtype) for g in grads]
    res = pl.pallas_call(
        body, name=name, in_specs=[ANY] * n, out_specs=[ANY] * (2 * n), out_shape=halves + halves,
        scratch_shapes=[pltpu.SemaphoreType.DMA((n,)), pltpu.SemaphoreType.DMA((n,)), pltpu.SemaphoreType.DMA((n,))])(*grads)
    return res[:n], res[n:]


def scatter_chips(parts, *, name):
    n = len(parts)

    def body(*refs):
        ins, outs = refs[:n], refs[n:2 * n]
        send_sems, recv_sems, local_sems = refs[2 * n:]
        x, y, c, chips = _place()
        k = 2 * x + y
        copies = []
        for a in range(n):
            for j, (px, py) in enumerate(chips):
                copies.append(_send(ins[a].at[2 * px + py], outs[a].at[k], send_sems.at[a, j], recv_sems.at[a, j],
                                    (px, py, c)))
            copies.append(_copy(ins[a].at[k], outs[a].at[k], local_sems.at[a]))
        for cp in copies:
            cp.wait()

    return pl.pallas_call(
        body, name=name, in_specs=[ANY] * n, out_specs=[ANY] * n,
        out_shape=[jax.ShapeDtypeStruct(p.shape, p.dtype) for p in parts],
        scratch_shapes=[pltpu.SemaphoreType.DMA((n, 3)), pltpu.SemaphoreType.DMA((n, 3)),
                        pltpu.SemaphoreType.DMA((n,))])(*parts)


def join_halves(halves, *, name):
    n = len(halves)

    def body(*refs):
        ins, outs = refs[:n], refs[n:2 * n]
        send_sems, recv_sems, local_sems = refs[2 * n:]
        x, y, c, _ = _place()
        copies = []
        for a in range(n):
            copies.append(_send(ins[a], outs[a].at[c], send_sems.at[a], recv_sems.at[a], (x, y, 1 - c)))
            copies.append(_copy(ins[a], outs[a].at[c], local_sems.at[a]))
        for cp in copies:
            cp.wait()

    return pl.pallas_call(
        body, name=name, in_specs=[ANY] * n, out_specs=[ANY] * n,
        out_shape=[jax.ShapeDtypeStruct((2,) + h.shape, h.dtype) for h in halves],
        scratch_shapes=[pltpu.SemaphoreType.DMA((n,)), pltpu.SemaphoreType.DMA((n,)), pltpu.SemaphoreType.DMA((n,))])(*halves)


def _pack(arrs, rows_mult):
    flat = jnp.concatenate([a.reshape(-1) for a in arrs])
    rows = -(-flat.shape[0] // (128 * rows_mult)) * rows_mult
    return jnp.pad(flat, (0, rows * 128 - flat.shape[0])).reshape(rows, 128)


def _unpack(packed, shapes):
    flat = packed.reshape(-1)
    out, pos = [], 0
    for s in shapes:
        size = 1
        for d in s:
            size *= d
        out.append(flat[pos:pos + size].reshape(s))
        pos += size
    return out


def _column_shard(full, k, axis):
    width = full.shape[axis] // N_CHIPS
    return lax.dynamic_slice_in_dim(full, k * width, width, axis)


def kernel(x, meta_tokens, norm0_g, l0_w_in, l0_lam_re, l0_lam_im, l0_log_dt, l0_b_re, l0_b_im, l0_c_re, l0_c_im, l0_d_skip, l0_w_glu, l0_b_glu, l0_w_out, norm1_g, l1_w_in, l1_conv_w, l1_conv_b, l1_w_out, norm2_g, l2_w_in, l2_w_grp, l2_b_grp, l2_scale, l2_w_out, norm3_g, l3_w_in, l3_lam_re, l3_lam_im, l3_log_dt, l3_b_re, l3_b_im, l3_c_re, l3_c_im, l3_d_skip, l3_w_glu, l3_b_glu, l3_w_out, final_g, loss_target, m_meta_tokens, m_norm0_g, m_l0_w_in, m_l0_lam_re, m_l0_lam_im, m_l0_log_dt, m_l0_b_re, m_l0_b_im, m_l0_c_re, m_l0_c_im, m_l0_d_skip, m_l0_w_glu, m_l0_b_glu, m_l0_w_out, m_norm1_g, m_l1_w_in, m_l1_conv_w, m_l1_conv_b, m_l1_w_out, m_norm2_g, m_l2_w_in, m_l2_w_grp, m_l2_b_grp, m_l2_scale, m_l2_w_out, m_norm3_g, m_l3_w_in, m_l3_lam_re, m_l3_lam_im, m_l3_log_dt, m_l3_b_re, m_l3_b_im, m_l3_c_re, m_l3_c_im, m_l3_d_skip, m_l3_w_glu, m_l3_b_glu, m_l3_w_out, m_final_g, v_meta_tokens, v_norm0_g, v_l0_w_in, v_l0_lam_re, v_l0_lam_im, v_l0_log_dt, v_l0_b_re, v_l0_b_im, v_l0_c_re, v_l0_c_im, v_l0_d_skip, v_l0_w_glu, v_l0_b_glu, v_l0_w_out, v_norm1_g, v_l1_w_in, v_l1_conv_w, v_l1_conv_b, v_l1_w_out, v_norm2_g, v_l2_w_in, v_l2_w_grp, v_l2_b_grp, v_l2_scale, v_l2_w_out, v_norm3_g, v_l3_w_in, v_l3_lam_re, v_l3_lam_im, v_l3_log_dt, v_l3_b_re, v_l3_b_im, v_l3_c_re, v_l3_c_im, v_l3_d_skip, v_l3_w_glu, v_l3_b_glu, v_l3_w_out, v_final_g):
    given = dict(locals())
    w = {n: given[n] for n in WEIGHTS}
    mom_m = {n: given["m_" + n] for n in WEIGHTS}
    mom_v = {n: given["v_" + n] for n in WEIGHTS}
    seq = x.shape[1]
    real = N_META + seq
    lp = -(-real // SCAN_ROWS) * SCAN_ROWS
    chip = 2 * lax.axis_index("x") + lax.axis_index("y")
    row = lambda a: a.reshape(1, -1)

    shard2d = {n: (w[n].reshape(-1, w[n].shape[-1]) if n == 'l2_w_grp' else w[n]) for n in BIG}
    full = dict(zip(BIG, gather_chips_by_halves([shard2d[n].astype(BF16) for n in BIG], name="gather_weights"),
                    strict=True))
    full.update(zip(SMALL_SHARDED, gather_chips([w[n] for n in SMALL_SHARDED], name="gather_small_shards"),
                    strict=True))
    for n in BIG:
        if n.endswith('w_in'):
            pass
        elif n == 'l2_w_grp':
            full[n] = full[n].reshape(N_CHIPS, len(POOL_WINDOWS), POOL_GROUP // N_CHIPS, POOL_GROUP)
        else:
            full[n] = full[n].reshape(1, -1, full[n].shape[-1])
    meta_full = full['meta_tokens'].transpose(1, 0, 2).reshape(N_META, D_MODEL)
    conv_w_full = full['l1_conv_w'].transpose(1, 0, 2).reshape(3, CONV_E)
    b_grp_full = full['l2_b_grp'].transpose(1, 0, 2).reshape(len(POOL_WINDOWS), POOL_GROUP)

    h0 = jnp.concatenate([meta_full, x[0], jnp.zeros((lp - real, D_MODEL), F32)], axis=0)
    target = jnp.concatenate([jnp.zeros((N_META, D_MODEL), F32), loss_target[0],
                              jnp.zeros((lp - real, D_MODEL), F32)], axis=0)
    grads = {}
    saved = {}

    def s5_layer_fwd(i, h):
        pre = "l%d_" % i
        tables, tables_vjp = jax.vjp(s5_tables, *[w[pre + s] for s in
                                                   ('lam_re', 'lam_im', 'log_dt', 'b_re', 'b_im', 'c_re', 'c_im')])
        ar, ai, bblk, cblk = tables
        n = rms_fwd(h, row(w["norm%d_g" % i]), name=pre + "norm")
        uz = mm_nn(n, full[pre + 'w_in'], name=pre + "in")
        y0, sr, si = s5_fwd(uz, bblk, cblk, ar, ai, row(w[pre + 'd_skip']), name=pre + "scan")
        (yg,), _ = rowwise(lambda yv: ([jax.nn.gelu(yv)], []), [(y0, D_MODEL, 0)], [],
                           [(D_MODEL, BF16, D_MODEL, 0)], name=pre + "gelu")
        q = mm_nn(yg, full[pre + 'w_glu'], bias=row(w[pre + 'b_glu']), name=pre + "glu")
        (y3,), _ = rowwise(lambda yv, qv, zv: ([_s5_gate(yv, qv, zv)], []),
                           [(y0, D_MODEL, 0), (q, D_MODEL, 0), (uz, D_MODEL, 1)], [],
                           [(D_MODEL, BF16, D_MODEL, 0)], name=pre + "gate")
        h_new = mm_nn(y3, full[pre + 'w_out'], add=h, name=pre + "out")
        saved[i] = dict(h=h, n=n, uz=uz, y0=y0, sr=sr, si=si, yg=yg, q=q, y3=y3, tables=tables,
                        tables_vjp=tables_vjp)
        return h_new

    def s5_layer_bwd(i, dh):
        pre = "l%d_" % i
        s = saved[i]
        ar, ai, bblk, cblk = s['tables']
        grads[pre + 'w_out'] = mm_tn(s['y3'], dh, 1, name=pre + "d_w_out")[0]
        dy3 = mm_nt(dh, full[pre + 'w_out'], name=pre + "d_y3")

        def gate_bwd(dyv, yv, qv, zv):
            _, vjp = jax.vjp(_s5_gate, yv, qv, zv)
            dy0, dq, dz = vjp(dyv)
            return [dy0, dq, dz], [_colsum(dq)]

        (dy0_direct, dq, dp), (db_glu,) = rowwise(
            gate_bwd, [(dy3, D_MODEL, 0), (s['y0'], D_MODEL, 0), (s['q'], D_MODEL, 0), (s['uz'], D_MODEL, 1)], [],
            [(D_MODEL, F32, D_MODEL, 0), (D_MODEL, F32, D_MODEL, 0), (2 * D_MODEL, F32, D_MODEL, 1)], [D_MODEL],
            name=pre + "d_gate")
        grads[pre + 'b_glu'] = db_glu.reshape(-1)
        grads[pre + 'w_glu'] = mm_tn(s['yg'], dq, 1, name=pre + "d_w_glu")[0]
        r = mm_nt(dq, full[pre + 'w_glu'], name=pre + "d_yg")

        def gelu_bwd(dv, rv, yv):
            _, vjp = jax.vjp(jax.nn.gelu, yv)
            return [dv + vjp(rv)[0]], []

        (dy0,), _ = rowwise(gelu_bwd, [(dy0_direct, D_MODEL, 0), (r, D_MODEL, 0), (s['y0'], D_MODEL, 0)], [],
                            [(D_MODEL, F32, D_MODEL, 0)], name=pre + "d_gelu")
        dp, dbblk, dcblk, dar, dai, dd = s5_bwd(dy0, s['uz'], s['sr'], s['si'], bblk, cblk, ar, ai,
                                                row(w[pre + 'd_skip']), dp, name=pre + "d_scan")
        grads[pre + 'd_skip'] = dd.reshape(-1)
        for key, val in zip(('lam_re', 'lam_im', 'log_dt', 'b_re', 'b_im', 'c_re', 'c_im'),
                            s['tables_vjp']((dar, dai, dbblk, dcblk)), strict=True):
            grads[pre + key] = val
        grads[pre + 'w_in'] = mm_tn(s['n'], dp, N_CHIPS, name=pre + "d_w_in")
        dn = mm_nt(dp, full[pre + 'w_in'], name=pre + "d_n")
        dh_in, dg = rms_bwd(s['h'], row(w["norm%d_g" % i]), dn, dh, name=pre + "d_norm")
        grads["norm%d_g" % i] = dg.reshape(-1)
        return dh_in

    h1 = s5_layer_fwd(0, h0)

    n1 = rms_fwd(h1, row(w['norm1_g']), name="l1_norm")
    p1 = mm_nn(n1, full['l1_w_in'], name="l1_in")
    y3_1 = conv_fwd(p1, conv_w_full, row(w['l1_conv_b']), name="l1_conv")
    h2 = mm_nn(y3_1, full['l1_w_out'], add=h1, name="l1_out")

    n2 = rms_fwd(h2, row(w['norm2_g']), name="l2_norm")
    p2 = mm_nn(n2, full['l2_w_in'], name="l2_in")
    mixed = pool_fwd(p2, name="l2_pool")
    o2 = grp_nn(mixed, full['l2_w_grp'], b_grp_full.reshape(1, -1), name="l2_grp")
    (y3_2,), _ = rowwise(lambda ov, zv, sv: ([_pool_gate(ov, zv, sv)], []),
                         [(o2, POOL_E, 0), (p2, POOL_E, 1)], [row(w['l2_scale'])],
                         [(POOL_E, BF16, POOL_E, 0)], name="l2_gate")
    h3 = mm_nn(y3_2, full['l2_w_out'], add=h2, name="l2_out")

    h4 = s5_layer_fwd(3, h3)

    def loss_fn(hv, tv, gv):
        i = pl.program_id(0)
        tr = hv.shape[0]
        yf, vjp = jax.vjp(_rms, hv, gv)
        pos = i * tr + lax.broadcasted_iota(jnp.int32, (tr, 1), 0)
        diff = jnp.where((pos >= N_META) & (pos < real), yf - tv, 0.0)
        dh, dg = vjp(diff / D_MODEL)
        return [dh], [dg, _colsum(diff * diff)]

    (dh,), (d_final_g, sq) = rowwise(loss_fn, [(h4, D_MODEL, 0), (target, D_MODEL, 0)], [row(w['final_g'])],
                                     [(D_MODEL, F32, D_MODEL, 0)], [D_MODEL, D_MODEL], name="loss")
    grads['final_g'] = d_final_g.reshape(-1)
    loss = lax.psum(0.5 / D_MODEL * jnp.sum(sq), ("x", "y", "c"))

    dh = s5_layer_bwd(3, dh)

    grads['l2_w_out'] = mm_tn(y3_2, dh, 1, name="l2_d_w_out")[0]
    dy3 = mm_nt(dh, full['l2_w_out'], name="l2_d_y3")

    def pool_gate_bwd(dyv, ov, zv, sv):
        _, vjp = jax.vjp(_pool_gate, ov, zv, sv)
        do, dz, dscale = vjp(dyv)
        return [do, dz], [dscale, _colsum(do)]

    (do2, dp2), (dscale, dbgrp) = rowwise(
        pool_gate_bwd, [(dy3, POOL_E, 0), (o2, POOL_E, 0), (p2, POOL_E, 1)], [row(w['l2_scale'])],
        [(POOL_E, F32, POOL_E, 0), (2 * POOL_E, F32, POOL_E, 1)], [POOL_E, POOL_E], name="l2_d_gate")
    grads['l2_scale'] = dscale.reshape(-1)
    grads['l2_b_grp'] = dbgrp.reshape(len(POOL_WINDOWS), POOL_GROUP)
    grads['l2_w_grp'] = grp_tn(mixed, do2, name="l2_d_w_grp")
    dmix = grp_nt(do2, full['l2_w_grp'], name="l2_d_mixed")
    dp2 = pool_bwd(dmix, dp2, name="l2_d_pool")
    grads['l2_w_in'] = mm_tn(n2, dp2, N_CHIPS, name="l2_d_w_in")
    dn = mm_nt(dp2, full['l2_w_in'], name="l2_d_n")
    dh, dg = rms_bwd(h2, row(w['norm2_g']), dn, dh, name="l2_d_norm")
    grads['norm2_g'] = dg.reshape(-1)

    grads['l1_w_out'] = mm_tn(y3_1, dh, 1, name="l1_d_w_out")[0]
    dy3 = mm_nt(dh, full['l1_w_out'], name="l1_d_y3")
    dp1, dcw, dcb = conv_bwd(dy3, p1, conv_w_full, row(w['l1_conv_b']), name="l1_d_conv")
    grads['l1_conv_w'] = dcw
    grads['l1_conv_b'] = dcb.reshape(-1)
    grads['l1_w_in'] = mm_tn(n1, dp1, N_CHIPS, name="l1_d_w_in")
    dn = mm_nt(dp1, full['l1_w_in'], name="l1_d_n")
    dh, dg = rms_bwd(h1, row(w['norm1_g']), dn, dh, name="l1_d_norm")
    grads['norm1_g'] = dg.reshape(-1)

    dh = s5_layer_bwd(0, dh)
    grad_x = dh[N_META:real][None]
    grads['meta_tokens'] = dh[:N_META]

    def chip_major(n):
        g = grads[n]
        if n.endswith('w_in'):
            return g
        if n == 'l2_w_grp':
            return g.reshape(N_CHIPS, -1, POOL_GROUP)
        return g.reshape(N_CHIPS, -1, g.shape[-1])
    small_local = _pack([grads[n] for n in SMALL], 32 * N_CHIPS)
    pieces = [chip_major(n) for n in BIG] + [small_local.reshape(N_CHIPS, -1, 128)]
    mine, theirs = swap_halves(pieces, name="reduce_cores")
    wire = [BF16] * len(BIG) + [F32]
    pair = [add_n([a.reshape(-1, a.shape[-1]), b.reshape(-1, b.shape[-1])], name="sum_cores_%d" % i,
                  out_dtype=dt).reshape(a.shape)
            for i, (a, b, dt) in enumerate(zip(mine, theirs, wire, strict=True))]
    from_chips = scatter_chips(pair, name="reduce_chips")
    reduced_half = [add_n(p, name="sum_chips_%d" % i) for i, p in enumerate(from_chips)]
    joined = join_halves(reduced_half, name="share_cores")
    reduced = [j.reshape(-1, j.shape[-1]) for j in joined]
    g_big = {n: reduced[i].reshape(w[n].shape) for i, n in enumerate(BIG)}
    (small_all,) = gather_chips([reduced[-1]], name="gather_small")
    g_small_full = dict(zip(SMALL, _unpack(small_all, [grads[n].shape for n in SMALL]), strict=True))
    g_small = {}
    for n in SMALL:
        g = g_small_full[n]
        g_small[n] = _column_shard(g, chip, g.ndim - 1) if n in SMALL_SHARDED else g

    out_g, out_d, out_m, out_v = {}, {}, {}, {}
    for n in BIG:
        shape = w[n].shape
        as2d = lambda a: a.reshape(-1, shape[-1])
        d, nm, nv = adamw(as2d(w[n]), as2d(g_big[n]), as2d(mom_m[n]), as2d(mom_v[n]), name="adamw_" + n)
        out_g[n], out_d[n], out_m[n], out_v[n] = g_big[n], d.reshape(shape), nm.reshape(shape), nv.reshape(shape)
    packs = [_pack([src[n] for n in SMALL], 8) for src in (w, g_small, mom_m, mom_v)]
    d, nm, nv = adamw(*packs, name="adamw_small")
    shapes = [w[n].shape for n in SMALL]
    for n, dv, mv, vv in zip(SMALL, _unpack(d, shapes), _unpack(nm, shapes), _unpack(nv, shapes), strict=True):
        out_g[n], out_d[n], out_m[n], out_v[n] = g_small[n], dv, mv, vv

    return (loss, grad_x, *[out_g[n] for n in WEIGHTS], *[out_d[n] for n in WEIGHTS],
            *[out_m[n] for n in WEIGHTS], *[out_v[n] for n in WEIGHTS])
```

```python
import functools

import jax
import jax.numpy as jnp
from jax import lax
from jax.experimental import pallas as pl
from jax.experimental.pallas import tpu as pltpu

F32, BF16 = jnp.float32, jnp.bfloat16
MESH = pl.DeviceIdType.MESH

D_MODEL = 1024
N_META = 16
EPS = 1e-6
S5_GROUPS, S5_GROUP, S5_STATE = 64, 16, 64
LANE_BLOCK = 128
S5_BLOCKS = D_MODEL // LANE_BLOCK
GROUPS_PER_BLOCK = LANE_BLOCK // S5_GROUP
STATE_BLOCK = GROUPS_PER_BLOCK * S5_STATE
SCAN_ROWS = 256
CONV_E = 2048
POOL_E = 2048
POOL_WINDOWS = (2, 4, 8, 16)
POOL_GROUP = 512
POOL_HALO = 16
N_CHIPS = 4

ADAM_LR, ADAM_B1, ADAM_B2, ADAM_EPS, ADAM_WD, ADAM_STEP = 0.001, 0.9, 0.999, 1e-08, 0.01, 10

VMEM_LIMIT = 48 * 1024 * 1024

WEIGHTS = ['meta_tokens', 'norm0_g', 'l0_w_in', 'l0_lam_re', 'l0_lam_im', 'l0_log_dt', 'l0_b_re', 'l0_b_im',
           'l0_c_re', 'l0_c_im', 'l0_d_skip', 'l0_w_glu', 'l0_b_glu', 'l0_w_out', 'norm1_g', 'l1_w_in',
           'l1_conv_w', 'l1_conv_b', 'l1_w_out', 'norm2_g', 'l2_w_in', 'l2_w_grp', 'l2_b_grp', 'l2_scale',
           'l2_w_out', 'norm3_g', 'l3_w_in', 'l3_lam_re', 'l3_lam_im', 'l3_log_dt', 'l3_b_re', 'l3_b_im',
           'l3_c_re', 'l3_c_im', 'l3_d_skip', 'l3_w_glu', 'l3_b_glu', 'l3_w_out', 'final_g']
BIG = ['l0_w_in', 'l0_w_glu', 'l0_w_out', 'l1_w_in', 'l1_w_out', 'l2_w_in', 'l2_w_grp', 'l2_w_out',
       'l3_w_in', 'l3_w_glu', 'l3_w_out']
SMALL_SHARDED = ['meta_tokens', 'l1_conv_w', 'l2_b_grp']
SMALL = [n for n in WEIGHTS if n not in BIG]


def _params(*sem):
    return pltpu.CompilerParams(dimension_semantics=sem, vmem_limit_bytes=VMEM_LIMIT)


def _tile(n, cap, mult):
    best = None
    for t in range(mult, min(n, cap) + 1, mult):
        if n % t == 0:
            best = t
    assert best is not None, (n, cap, mult)
    return best


def mm_nn(a, b, *, name, bias=None, add=None, out_dtype=F32):
    m, k = a.shape
    s, _, ns = b.shape
    n = s * ns
    tm, tn = _tile(m, 1088, 16), _tile(ns, 512, 128)
    per = ns // tn

    def body(*refs):
        a_ref, b_ref = refs[0], refs[1]
        o_ref = refs[-1]
        acc = jnp.dot(a_ref[...].astype(BF16), b_ref[...], preferred_element_type=F32)
        pos = 2
        if bias is not None:
            acc = acc + refs[pos][...]
            pos += 1
        if add is not None:
            acc = acc + refs[pos][...]
        o_ref[...] = acc.astype(o_ref.dtype)

    in_specs = [pl.BlockSpec((tm, k), lambda i, j: (i, 0)),
                pl.BlockSpec((None, k, tn), lambda i, j: (j // per, 0, j % per))]
    args = [a, b]
    if bias is not None:
        in_specs.append(pl.BlockSpec((1, tn), lambda i, j: (0, j)))
        args.append(bias)
    if add is not None:
        in_specs.append(pl.BlockSpec((tm, tn), lambda i, j: (i, j)))
        args.append(add)
    return pl.pallas_call(
        body, name=name, grid=(m // tm, n // tn), in_specs=in_specs,
        out_specs=pl.BlockSpec((tm, tn), lambda i, j: (i, j)),
        out_shape=jax.ShapeDtypeStruct((m, n), out_dtype),
        compiler_params=_params("parallel", "parallel"))(*args)


def mm_nt(g, w, *, name):
    m, kc = g.shape
    s, nout, kcs = w.shape
    assert s * kcs == kc
    tm, tno, tk = _tile(m, 1088, 16), _tile(nout, 1024, 128), _tile(kcs, 1024, 128)
    per = kcs // tk

    def body(g_ref, w_ref, o_ref):
        kk = pl.program_id(2)
        part = lax.dot_general(g_ref[...].astype(BF16), w_ref[...], (((1,), (1,)), ((), ())),
                               preferred_element_type=F32)

        @pl.when(kk == 0)
        def _():
            o_ref[...] = part

        @pl.when(kk > 0)
        def _():
            o_ref[...] += part

    return pl.pallas_call(
        body, name=name, grid=(m // tm, nout // tno, kc // tk),
        in_specs=[pl.BlockSpec((tm, tk), lambda i, j, kk: (i, kk)),
                  pl.BlockSpec((None, tno, tk), lambda i, j, kk: (kk // per, j, kk % per))],
        out_specs=pl.BlockSpec((tm, tno), lambda i, j, kk: (i, j)),
        out_shape=jax.ShapeDtypeStruct((m, nout), F32),
        compiler_params=_params("parallel", "parallel", "arbitrary"))(g, w)


def mm_tn(a, g, shards, *, name):
    m, ka = a.shape
    n = g.shape[1]
    ns = n // shards
    tm, tka, tn = _tile(m, 1088, 16), _tile(ka, 1024, 128), _tile(ns, 1024, 128)
    per = ns // tn

    def body(a_ref, g_ref, o_ref):
        mm = pl.program_id(2)
        part = lax.dot_general(a_ref[...].astype(BF16), g_ref[...].astype(BF16), (((0,), (0,)), ((), ())),
                               preferred_element_type=F32)

        @pl.when(mm == 0)
        def _():
            o_ref[...] = part

        @pl.when(mm > 0)
        def _():
            o_ref[...] += part

    return pl.pallas_call(
        body, name=name, grid=(ka // tka, n // tn, m // tm),
        in_specs=[pl.BlockSpec((tm, tka), lambda i, j, mm: (mm, i)),
                  pl.BlockSpec((tm, tn), lambda i, j, mm: (mm, j))],
        out_specs=pl.BlockSpec((None, tka, tn), lambda i, j, mm: (j // per, i, j % per)),
        out_shape=jax.ShapeDtypeStruct((shards, ka, ns), F32),
        compiler_params=_params("parallel", "parallel", "arbitrary"))(a, g)


def grp_nn(a, w, bias, *, name):
    m = a.shape[0]
    tm = _tile(m, 1088, 16)
    ng = len(POOL_WINDOWS)

    def body(a_ref, w_ref, b_ref, o_ref):
        wj = w_ref[...].reshape(POOL_GROUP, POOL_GROUP)
        o_ref[...] = jnp.dot(a_ref[...].astype(BF16), wj, preferred_element_type=F32) + b_ref[...]

    return pl.pallas_call(
        body, name=name, grid=(m // tm, ng),
        in_specs=[pl.BlockSpec((tm, POOL_GROUP), lambda i, j: (i, j)),
                  pl.BlockSpec((N_CHIPS, None, POOL_GROUP // N_CHIPS, POOL_GROUP), lambda i, j: (0, j, 0, 0)),
                  pl.BlockSpec((1, POOL_GROUP), lambda i, j: (0, j))],
        out_specs=pl.BlockSpec((tm, POOL_GROUP), lambda i, j: (i, j)),
        out_shape=jax.ShapeDtypeStruct((m, ng * POOL_GROUP), F32),
        compiler_params=_params("parallel", "parallel"))(a, w, bias)


def grp_nt(g, w, *, name):
    m = g.shape[0]
    tm = _tile(m, 1088, 16)
    ng = len(POOL_WINDOWS)

    def body(g_ref, w_ref, o_ref):
        wj = w_ref[...].reshape(POOL_GROUP, POOL_GROUP)
        o_ref[...] = lax.dot_general(g_ref[...].astype(BF16), wj, (((1,), (1,)), ((), ())),
                                     preferred_element_type=F32)

    return pl.pallas_call(
        body, name=name, grid=(m // tm, ng),
        in_specs=[pl.BlockSpec((tm, POOL_GROUP), lambda i, j: (i, j)),
                  pl.BlockSpec((N_CHIPS, None, POOL_GROUP // N_CHIPS, POOL_GROUP), lambda i, j: (0, j, 0, 0))],
        out_specs=pl.BlockSpec((tm, POOL_GROUP), lambda i, j: (i, j)),
        out_shape=jax.ShapeDtypeStruct((m, ng * POOL_GROUP), F32),
        compiler_params=_params("parallel", "parallel"))(g, w)


def grp_tn(a, g, *, name):
    m = a.shape[0]
    tm = _tile(m, 1088, 16)
    ng = len(POOL_WINDOWS)
    rows = POOL_GROUP // N_CHIPS

    def body(a_ref, g_ref, o_ref):
        mm = pl.program_id(1)
        part = lax.dot_general(a_ref[...].astype(BF16), g_ref[...].astype(BF16), (((0,), (0,)), ((), ())),
                               preferred_element_type=F32).reshape(N_CHIPS, rows, POOL_GROUP)

        @pl.when(mm == 0)
        def _():
            o_ref[...] = part

        @pl.when(mm > 0)
        def _():
            o_ref[...] += part

    return pl.pallas_call(
        body, name=name, grid=(ng, m // tm),
        in_specs=[pl.BlockSpec((tm, POOL_GROUP), lambda j, mm: (mm, j)),
                  pl.BlockSpec((tm, POOL_GROUP), lambda j, mm: (mm, j))],
        out_specs=pl.BlockSpec((N_CHIPS, None, rows, POOL_GROUP), lambda j, mm: (0, j, 0, 0)),
        out_shape=jax.ShapeDtypeStruct((N_CHIPS, ng, rows, POOL_GROUP), F32),
        compiler_params=_params("parallel", "arbitrary"))(a, g)


def rowwise(fn, rows, bcast, outs, accs=(), *, name, aliases=None):
    m = rows[0][0].shape[0]
    tr = m // 16
    n_in = len(rows) + len(bcast)

    def body(*refs):
        vals = [r[...] for r in refs[:n_in]]
        o_vals, a_vals = fn(*vals)
        for ref, val in zip(refs[n_in:n_in + len(outs)], o_vals, strict=True):
            ref[...] = val.astype(ref.dtype)
        step = pl.program_id(0)
        for ref, val in zip(refs[n_in + len(outs):], a_vals, strict=True):
            @pl.when(step == 0)
            def _(ref=ref, val=val):
                ref[...] = val

            @pl.when(step > 0)
            def _(ref=ref, val=val):
                ref[...] += val

    in_specs = [pl.BlockSpec((tr, w), functools.partial(lambda i, cb: (i, cb), cb=cb)) for _, w, cb in rows]
    in_specs += [pl.BlockSpec(b.shape, lambda i: (0, 0)) for b in bcast]
    out_specs = [pl.BlockSpec((tr, w), functools.partial(lambda i, cb: (i, cb), cb=cb)) for _, _, w, cb in outs]
    out_specs += [pl.BlockSpec((1, w), lambda i: (0, 0)) for w in accs]
    out_shape = [jax.ShapeDtypeStruct((m, cols), dt) for cols, dt, _, _ in outs]
    out_shape += [jax.ShapeDtypeStruct((1, w), F32) for w in accs]
    res = pl.pallas_call(
        body, name=name, grid=(m // tr,), in_specs=in_specs, out_specs=out_specs, out_shape=out_shape,
        input_output_aliases=aliases or {},
        compiler_params=_params("arbitrary"))(*[r[0] for r in rows], *bcast)
    return res[:len(outs)], res[len(outs):]


def _rms(h, g):
    return h * lax.rsqrt(jnp.mean(h * h, axis=-1, keepdims=True) + EPS) * g


def _colsum(v):
    return jnp.sum(v, axis=0, keepdims=True)


def rms_fwd(h, g, *, name):
    (n,), _ = rowwise(lambda hv, gv: ([_rms(hv, gv)], []), [(h, D_MODEL, 0)], [g],
                      [(D_MODEL, BF16, D_MODEL, 0)], name=name)
    return n


def rms_bwd(h, g, dn, dh_out, *, name):
    def fn(hv, dnv, dhv, gv):
        _, vjp = jax.vjp(_rms, hv, gv)
        dh, dg = vjp(dnv)
        return [dhv + dh], [dg]

    (dh,), (dg,) = rowwise(fn, [(h, D_MODEL, 0), (dn, D_MODEL, 0), (dh_out, D_MODEL, 0)], [g],
                           [(D_MODEL, F32, D_MODEL, 0)], [D_MODEL], name=name)
    return dh, dg


def _s5_gate(y0, q, z):
    yg = jax.nn.gelu(y0)
    return yg * jax.nn.sigmoid(q) * jax.nn.silu(z)


def _pool_gate(o, z, scale):
    return o * scale * jax.nn.silu(z)


def _shift_rows(v, d, down):
    t = v.shape[0]
    row = lax.broadcasted_iota(jnp.int32, v.shape, 0)
    if down:
        return jnp.where(row >= d, pltpu.roll(v, d, 0), 0.0)
    return jnp.where(row < t - d, pltpu.roll(v, t - d, 0), 0.0)


def _scan(sr, si, ar, ai, down):
    pr, pi = ar, ai
    d = 1
    while d < sr.shape[0]:
        hr, hi = _shift_rows(sr, d, down), _shift_rows(si, d, down)
        sr, si = sr + pr * hr - pi * hi, si + pr * hi + pi * hr
        pr, pi = pr * pr - pi * pi, 2.0 * pr * pi
        d *= 2
    return sr, si


def s5_fwd(uz, bblk, cblk, ar, ai, dskip, *, name):
    lp = uz.shape[0]
    t = SCAN_ROWS
    nst = S5_GROUPS * S5_STATE

    def body(u_ref, b_ref, c_ref, ar_ref, ai_ref, d_ref, y_ref, sr_ref, si_ref, cr, ci):
        step = pl.program_id(1)

        @pl.when(step == 0)
        def _():
            cr[...] = jnp.zeros_like(cr)
            ci[...] = jnp.zeros_like(ci)

        u = u_ref[...]
        a_r, a_i = ar_ref[...], ai_ref[...]
        x = jnp.dot(u.astype(BF16), b_ref[...].astype(BF16), preferred_element_type=F32)
        xr, xi = x[:, :STATE_BLOCK], x[:, STATE_BLOCK:]
        first = lax.broadcasted_iota(jnp.int32, xr.shape, 0) == 0
        xr = xr + jnp.where(first, a_r * cr[...] - a_i * ci[...], 0.0)
        xi = xi + jnp.where(first, a_r * ci[...] + a_i * cr[...], 0.0)
        sr, si = _scan(xr, xi, a_r, a_i, True)
        sr_ref[...] = sr
        si_ref[...] = si
        cr[...] = sr[t - 1:t, :]
        ci[...] = si[t - 1:t, :]
        s = jnp.concatenate([sr, si], axis=1).astype(BF16)
        y_ref[...] = jnp.dot(s, c_ref[...].astype(BF16), preferred_element_type=F32) + d_ref[...] * u

    return pl.pallas_call(
        body, name=name, grid=(S5_BLOCKS, lp // t),
        in_specs=[pl.BlockSpec((t, LANE_BLOCK), lambda b, c: (c, b)),
                  pl.BlockSpec((None, LANE_BLOCK, 2 * STATE_BLOCK), lambda b, c: (b, 0, 0)),
                  pl.BlockSpec((None, 2 * STATE_BLOCK, LANE_BLOCK), lambda b, c: (b, 0, 0)),
                  pl.BlockSpec((1, STATE_BLOCK), lambda b, c: (0, b)),
                  pl.BlockSpec((1, STATE_BLOCK), lambda b, c: (0, b)),
                  pl.BlockSpec((1, LANE_BLOCK), lambda b, c: (0, b))],
        out_specs=[pl.BlockSpec((t, LANE_BLOCK), lambda b, c: (c, b)),
                   pl.BlockSpec((t, STATE_BLOCK), lambda b, c: (c, b)),
                   pl.BlockSpec((t, STATE_BLOCK), lambda b, c: (c, b))],
        out_shape=[jax.ShapeDtypeStruct((lp, D_MODEL), F32), jax.ShapeDtypeStruct((lp, nst), F32),
                   jax.ShapeDtypeStruct((lp, nst), F32)],
        scratch_shapes=[pltpu.VMEM((1, STATE_BLOCK), F32), pltpu.VMEM((1, STATE_BLOCK), F32)],
        compiler_params=_params("parallel", "arbitrary"))(uz, bblk, cblk, ar, ai, dskip)


def s5_bwd(dy0, uz, sr, si, bblk, cblk, ar, ai, dskip, dp, *, name):
    lp = uz.shape[0]
    t = SCAN_ROWS
    nch = lp // t
    nst = S5_GROUPS * S5_STATE

    def body(dy_ref, u_ref, sr_ref, si_ref, b_ref, c_ref, ar_ref, ai_ref, d_ref, _, du_ref, db_ref, dc_ref,
             dar_ref, dai_ref, dd_ref, cr, ci):
        step = pl.program_id(1)

        @pl.when(step == 0)
        def _():
            cr[...] = jnp.zeros_like(cr)
            ci[...] = jnp.zeros_like(ci)

        dy, u = dy_ref[...], u_ref[...]
        dyb, ub = dy.astype(BF16), u.astype(BF16)
        a_r, a_i = ar_ref[...], -ai_ref[...]
        g = lax.dot_general(dyb, c_ref[...].astype(BF16), (((1,), (1,)), ((), ())), preferred_element_type=F32)
        gr, gi = g[:, :STATE_BLOCK], g[:, STATE_BLOCK:]
        nxt_r, nxt_i = cr[...], ci[...]
        last = lax.broadcasted_iota(jnp.int32, gr.shape, 0) == t - 1
        gr = gr + jnp.where(last, a_r * nxt_r - a_i * nxt_i, 0.0)
        gi = gi + jnp.where(last, a_r * nxt_i + a_i * nxt_r, 0.0)
        lr, li = _scan(gr, gi, a_r, a_i, False)
        cr[...] = lr[0:1, :]
        ci[...] = li[0:1, :]
        nr = _shift_rows(lr, 1, False) + jnp.where(last, nxt_r, 0.0)
        ni = _shift_rows(li, 1, False) + jnp.where(last, nxt_i, 0.0)
        s_r, s_i = sr_ref[...], si_ref[...]
        dar = _colsum(s_r * nr + s_i * ni)
        dai = _colsum(s_r * ni - s_i * nr)
        lam = jnp.concatenate([lr, li], axis=1).astype(BF16)
        sb = jnp.concatenate([s_r, s_i], axis=1).astype(BF16)
        dc = lax.dot_general(sb, dyb, (((0,), (0,)), ((), ())), preferred_element_type=F32)
        db = lax.dot_general(ub, lam, (((0,), (0,)), ((), ())), preferred_element_type=F32)
        du_ref[...] = lax.dot_general(lam, b_ref[...].astype(BF16), (((1,), (1,)), ((), ())),
                                      preferred_element_type=F32) + d_ref[...] * dy
        dd = _colsum(dy * u)

        @pl.when(step == 0)
        def _():
            db_ref[...] = db
            dc_ref[...] = dc
            dar_ref[...] = dar
            dai_ref[...] = dai
            dd_ref[...] = dd

        @pl.when(step > 0)
        def _():
            db_ref[...] += db
            dc_ref[...] += dc
            dar_ref[...] += dar
            dai_ref[...] += dai
            dd_ref[...] += dd

    rev = lambda b, c: (nch - 1 - c, b)
    return pl.pallas_call(
        body, name=name, grid=(S5_BLOCKS, nch),
        in_specs=[pl.BlockSpec((t, LANE_BLOCK), rev), pl.BlockSpec((t, LANE_BLOCK), rev),
                  pl.BlockSpec((t, STATE_BLOCK), rev), pl.BlockSpec((t, STATE_BLOCK), rev),
                  pl.BlockSpec((None, LANE_BLOCK, 2 * STATE_BLOCK), lambda b, c: (b, 0, 0)),
                  pl.BlockSpec((None, 2 * STATE_BLOCK, LANE_BLOCK), lambda b, c: (b, 0, 0)),
                  pl.BlockSpec((1, STATE_BLOCK), lambda b, c: (0, b)),
                  pl.BlockSpec((1, STATE_BLOCK), lambda b, c: (0, b)),
                  pl.BlockSpec((1, LANE_BLOCK), lambda b, c: (0, b)),
                  pl.BlockSpec(memory_space=pl.ANY)],
        out_specs=[pl.BlockSpec((t, LANE_BLOCK), rev),
                   pl.BlockSpec((None, LANE_BLOCK, 2 * STATE_BLOCK), lambda b, c: (b, 0, 0)),
                   pl.BlockSpec((None, 2 * STATE_BLOCK, LANE_BLOCK), lambda b, c: (b, 0, 0)),
                   pl.BlockSpec((1, STATE_BLOCK), lambda b, c: (0, b)),
                   pl.BlockSpec((1, STATE_BLOCK), lambda b, c: (0, b)),
                   pl.BlockSpec((1, LANE_BLOCK), lambda b, c: (0, b))],
        out_shape=[jax.ShapeDtypeStruct(dp.shape, F32),
                   jax.ShapeDtypeStruct((S5_BLOCKS, LANE_BLOCK, 2 * STATE_BLOCK), F32),
                   jax.ShapeDtypeStruct((S5_BLOCKS, 2 * STATE_BLOCK, LANE_BLOCK), F32),
                   jax.ShapeDtypeStruct((1, nst), F32), jax.ShapeDtypeStruct((1, nst), F32),
                   jax.ShapeDtypeStruct((1, D_MODEL), F32)],
        scratch_shapes=[pltpu.VMEM((1, STATE_BLOCK), F32), pltpu.VMEM((1, STATE_BLOCK), F32)],
        input_output_aliases={9: 0},
        compiler_params=_params("parallel", "arbitrary"))(dy0, uz, sr, si, bblk, cblk, ar, ai, dskip, dp)


def s5_tables(lam_re, lam_im, log_dt, b_re, b_im, c_re, c_im):
    dt = jnp.exp(log_dt)[:, None]
    mag = jnp.exp(lam_re * dt)
    ar = mag * jnp.cos(lam_im * dt)
    ai = mag * jnp.sin(lam_im * dt)
    den = lam_re * lam_re + lam_im * lam_im
    kr = ((ar - 1.0) * lam_re + ai * lam_im) / den
    ki = (ai * lam_re - (ar - 1.0) * lam_im) / den
    bbr = kr[..., None] * b_re - ki[..., None] * b_im
    bbi = kr[..., None] * b_im + ki[..., None] * b_re
    eye = jnp.eye(GROUPS_PER_BLOCK, dtype=F32)

    def b_blocks(v):
        v = v.reshape(S5_BLOCKS, GROUPS_PER_BLOCK, S5_STATE, S5_GROUP)
        return jnp.einsum('bgpi,gh->bgihp', v, eye).reshape(S5_BLOCKS, LANE_BLOCK, STATE_BLOCK)

    def c_blocks(v):
        v = v.reshape(S5_BLOCKS, GROUPS_PER_BLOCK, S5_GROUP, S5_STATE)
        return jnp.einsum('bgip,gh->bgphi', v, eye).reshape(S5_BLOCKS, STATE_BLOCK, LANE_BLOCK)

    bblk = jnp.concatenate([b_blocks(bbr), b_blocks(bbi)], axis=2)
    cblk = jnp.concatenate([c_blocks(c_re), -c_blocks(c_im)], axis=1)
    nst = S5_GROUPS * S5_STATE
    return ar.reshape(1, nst), ai.reshape(1, nst), bblk, cblk


def _silu_grad(z):
    s = jax.nn.sigmoid(z)
    return s * (1.0 + z * (1.0 - s))


def conv_fwd(p, conv_w, conv_b, *, name):
    lp = p.shape[0]
    tr = lp // 16
    e = CONV_E
    hb = tr // 8

    def body(p_ref, cgh_ref, vh_ref, w_ref, b_ref, o_ref):
        i = pl.program_id(0)
        bg, cg, v, z = (p_ref[:, k * e:(k + 1) * e] for k in range(4))
        hc = cg * v
        halo = jnp.where(i > 0, cgh_ref[...] * vh_ref[...], 0.0)
        ext = jnp.concatenate([halo, hc], axis=0)
        w = w_ref[...]
        conv = (w[0:1] * pltpu.roll(ext, 2, 0)[8:] + w[1:2] * pltpu.roll(ext, 1, 0)[8:] + w[2:3] * hc
                + b_ref[...])
        o_ref[...] = (bg * conv * jax.nn.silu(z)).astype(o_ref.dtype)

    prev = lambda col: (lambda i: (jnp.maximum(i * hb - 1, 0), col))
    return pl.pallas_call(
        body, name=name, grid=(lp // tr,),
        in_specs=[pl.BlockSpec((tr, 4 * e), lambda i: (i, 0)),
                  pl.BlockSpec((8, e), prev(1)), pl.BlockSpec((8, e), prev(2)),
                  pl.BlockSpec((3, e), lambda i: (0, 0)), pl.BlockSpec((1, e), lambda i: (0, 0))],
        out_specs=pl.BlockSpec((tr, e), lambda i: (i, 0)),
        out_shape=jax.ShapeDtypeStruct((lp, e), BF16),
        compiler_params=_params("parallel"))(p, p, p, conv_w, conv_b)


def conv_bwd(dy3, p, conv_w, conv_b, *, name):
    lp = p.shape[0]
    tr = lp // 32
    e = CONV_E
    hb = tr // 8
    nt = lp // tr

    def body(dy_ref, p_ref, cgh_ref, vh_ref, dyn_ref, bgn_ref, zn_ref, w_ref, b_ref, dp_ref, dw_ref, db_ref):
        i = pl.program_id(0)
        dy = dy_ref[...]
        bg, cg, v, z = (p_ref[:, k * e:(k + 1) * e] for k in range(4))
        w = w_ref[...]
        hc = cg * v
        halo = jnp.where(i > 0, cgh_ref[...] * vh_ref[...], 0.0)
        ext = jnp.concatenate([halo, hc], axis=0)
        hc2, hc1 = pltpu.roll(ext, 2, 0)[8:], pltpu.roll(ext, 1, 0)[8:]
        yc = w[0:1] * hc2 + w[1:2] * hc1 + w[2:3] * hc + b_ref[...]
        sz = jax.nn.silu(z)
        dyc = dy * bg * sz
        dyc_n = jnp.where(i < nt - 1, dyn_ref[...] * bgn_ref[...] * jax.nn.silu(zn_ref[...]), 0.0)
        ext_n = jnp.concatenate([dyc, dyc_n], axis=0)
        n_rows = tr + 8
        dhc = (w[2:3] * dyc + w[1:2] * pltpu.roll(ext_n, n_rows - 1, 0)[:tr]
               + w[0:1] * pltpu.roll(ext_n, n_rows - 2, 0)[:tr])
        dp_ref[:, 0:e] = dy * yc * sz
        dp_ref[:, e:2 * e] = dhc * v
        dp_ref[:, 2 * e:3 * e] = dhc * cg
        dp_ref[:, 3 * e:4 * e] = dy * bg * yc * _silu_grad(z)
        dw = jnp.concatenate([_colsum(dyc * hc2), _colsum(dyc * hc1), _colsum(dyc * hc)], axis=0)
        db = _colsum(dyc)

        @pl.when(i == 0)
        def _():
            dw_ref[...] = dw
            db_ref[...] = db

        @pl.when(i > 0)
        def _():
            dw_ref[...] += dw
            db_ref[...] += db

    prev = lambda col: (lambda i: (jnp.maximum(i * hb - 1, 0), col))
    nxt = lambda col: (lambda i: (jnp.minimum((i + 1) * hb, lp // 8 - 1), col))
    return pl.pallas_call(
        body, name=name, grid=(nt,),
        in_specs=[pl.BlockSpec((tr, e), lambda i: (i, 0)), pl.BlockSpec((tr, 4 * e), lambda i: (i, 0)),
                  pl.BlockSpec((8, e), prev(1)), pl.BlockSpec((8, e), prev(2)),
                  pl.BlockSpec((8, e), nxt(0)), pl.BlockSpec((8, e), nxt(0)), pl.BlockSpec((8, e), nxt(3)),
                  pl.BlockSpec((3, e), lambda i: (0, 0)), pl.BlockSpec((1, e), lambda i: (0, 0))],
        out_specs=[pl.BlockSpec((tr, 4 * e), lambda i: (i, 0)), pl.BlockSpec((3, e), lambda i: (0, 0)),
                   pl.BlockSpec((1, e), lambda i: (0, 0))],
        out_shape=[jax.ShapeDtypeStruct((lp, 4 * e), F32), jax.ShapeDtypeStruct((3, e), F32),
                   jax.ShapeDtypeStruct((1, e), F32)],
        compiler_params=_params("arbitrary"))(dy3, p, p, p, dy3, p, p, conv_w, conv_b)


def _pool_counts(i, tr, rows, offset, w):
    t = (i * tr + offset + 1 + lax.broadcasted_iota(jnp.int32, (rows, 1), 0)).astype(F32)
    return jnp.minimum(t, float(w))


def pool_fwd(p, *, name):
    lp = p.shape[0]
    tr = lp // 16
    h = POOL_HALO
    hb = tr // h

    def body(u_ref, uh_ref, o_ref):
        i = pl.program_id(0)
        u = u_ref[...]
        ext = jnp.concatenate([jnp.where(i > 0, uh_ref[...], 0.0), u], axis=0)
        for k, w in enumerate(POOL_WINDOWS):
            cols = slice(k * POOL_GROUP, (k + 1) * POOL_GROUP)
            s = ext[:, cols]
            d = 1
            while d < w:
                s = s + pltpu.roll(s, d, 0)
                d *= 2
            o_ref[:, cols] = (s[h:] / _pool_counts(i, tr, tr, 0, w) - u[:, cols]).astype(o_ref.dtype)

    return pl.pallas_call(
        body, name=name, grid=(lp // tr,),
        in_specs=[pl.BlockSpec((tr, POOL_E), lambda i: (i, 0)),
                  pl.BlockSpec((h, POOL_E), lambda i: (jnp.maximum(i * hb - 1, 0), 0))],
        out_specs=pl.BlockSpec((tr, POOL_E), lambda i: (i, 0)),
        out_shape=jax.ShapeDtypeStruct((lp, POOL_E), BF16),
        compiler_params=_params("parallel"))(p, p)


def pool_bwd(dmix, dp, *, name):
    lp = dmix.shape[0]
    tr = lp // 16
    h = POOL_HALO
    hb = tr // h
    nt = lp // tr

    def body(dm_ref, dmn_ref, _, du_ref):
        i = pl.program_id(0)
        dm = dm_ref[...]
        dmn = jnp.where(i < nt - 1, dmn_ref[...], 0.0)
        n_rows = tr + h
        for k, w in enumerate(POOL_WINDOWS):
            cols = slice(k * POOL_GROUP, (k + 1) * POOL_GROUP)
            s = jnp.concatenate([dm[:, cols] / _pool_counts(i, tr, tr, 0, w),
                                 dmn[:, cols] / _pool_counts(i, tr, h, tr, w)], axis=0)
            d = 1
            while d < w:
                s = s + pltpu.roll(s, n_rows - d, 0)
                d *= 2
            du_ref[:, cols] = s[:tr] - dm[:, cols]

    return pl.pallas_call(
        body, name=name, grid=(nt,),
        in_specs=[pl.BlockSpec((tr, POOL_E), lambda i: (i, 0)),
                  pl.BlockSpec((h, POOL_E), lambda i: (jnp.minimum((i + 1) * hb, lp // h - 1), 0)),
                  pl.BlockSpec(memory_space=pl.ANY)],
        out_specs=pl.BlockSpec((tr, POOL_E), lambda i: (i, 0)),
        out_shape=jax.ShapeDtypeStruct(dp.shape, F32),
        input_output_aliases={2: 0},
        compiler_params=_params("parallel"))(dmix, dmix, dp)


def adamw(w, g, m, v, *, name):
    r, c = w.shape
    tr = _tile(r, max(8, (1 << 19) // c), 8)

    def body(w_ref, g_ref, m_ref, v_ref, d_ref, nm_ref, nv_ref):
        gv = g_ref[...]
        nm = ADAM_B1 * m_ref[...] + (1.0 - ADAM_B1) * gv
        nv = ADAM_B2 * v_ref[...] + (1.0 - ADAM_B2) * jnp.square(gv)
        m_hat = nm / (1.0 - ADAM_B1 ** ADAM_STEP)
        v_hat = nv / (1.0 - ADAM_B2 ** ADAM_STEP)
        d_ref[...] = -ADAM_LR * (m_hat / (jnp.sqrt(v_hat) + ADAM_EPS) + ADAM_WD * w_ref[...])
        nm_ref[...] = nm
        nv_ref[...] = nv

    spec = pl.BlockSpec((tr, c), lambda i: (i, 0))
    return pl.pallas_call(
        body, name=name, grid=(r // tr,), in_specs=[spec] * 4, out_specs=[spec] * 3,
        out_shape=[jax.ShapeDtypeStruct((r, c), F32)] * 3,
        compiler_params=_params("parallel"))(w, g, m, v)


ANY = pl.BlockSpec(memory_space=pl.ANY)


def _place():
    x, y, c = lax.axis_index("x"), lax.axis_index("y"), lax.axis_index("c")
    return x, y, c, [(1 - x, y), (x, 1 - y), (1 - x, 1 - y)]


DMA_PIECE_BYTES = 512 * 1024


def _pieces(src, dst):
    shape = src.shape
    rows, cols = shape[-2], shape[-1]
    item = jnp.dtype(src.dtype).itemsize
    unit = 32 // item
    want = max(1, (rows * cols * item) // DMA_PIECE_BYTES)
    n = 1
    if rows % unit == 0:
        n = max(d for d in range(1, rows // unit + 1) if (rows // unit) % d == 0 and d <= want)
    size = rows // n
    out = []
    for lead in (range(shape[0]) if len(shape) == 3 else [None]):
        for i in range(n):
            sel = (pl.ds(i * size, size), slice(None)) if lead is None else (lead, pl.ds(i * size, size), slice(None))
            out.append((src.at[sel], dst.at[sel]))
    return out


def _send(src, dst, send_sem, recv_sem, peer):
    for s, d in _pieces(src, dst):
        pltpu.make_async_remote_copy(src_ref=s, dst_ref=d, send_sem=send_sem, recv_sem=recv_sem,
                                     device_id=peer, device_id_type=MESH).start()
    return pltpu.make_async_remote_copy(src_ref=src, dst_ref=dst, send_sem=send_sem, recv_sem=recv_sem,
                                        device_id=peer, device_id_type=MESH)


def gather_chips(shards, *, name):
    n = len(shards)

    def body(*refs):
        ins, outs = refs[:n], refs[n:2 * n]
        send_sems, recv_sems = refs[2 * n:]
        x, y, c, chips = _place()
        k = 2 * x + y
        copies = []
        for a in range(n):
            for j, peer in enumerate([(px, py, c) for px, py in chips] + [(x, y, 1 - c)]):
                copies.append(_send(ins[a], outs[a].at[k], send_sems.at[a, j], recv_sems.at[a, j], peer))
        for cp in copies:
            cp.wait()

    return pl.pallas_call(
        body, name=name, in_specs=[ANY] * n, out_specs=[ANY] * n,
        out_shape=[jax.ShapeDtypeStruct((N_CHIPS,) + s.shape, s.dtype) for s in shards],
        scratch_shapes=[pltpu.SemaphoreType.DMA((n, 4)), pltpu.SemaphoreType.DMA((n, 4))])(*shards)


def gather_chips_by_halves(shards, *, name):
    n = len(shards)

    def body(*refs):
        ins, outs = refs[:n], refs[n:2 * n]
        send_sems, recv_sems = refs[2 * n:]
        x, y, c, chips = _place()
        k = 2 * x + y
        sibling = (x, y, 1 - c)
        waits, arrivals = [], []
        for a in range(n):
            half = ins[a].shape[0] // 2
            waits.append(_send(ins[a], outs[a].at[k], send_sems.at[a, 6], recv_sems.at[a, 6], sibling))
            mine = pl.ds(c * half, half)
            for j, (px, py) in enumerate(chips):
                arrivals.append(_send(ins[a].at[mine, :], outs[a].at[k, mine, :], send_sems.at[a, j],
                                      recv_sems.at[a, j], (px, py, c)))
        i = 0
        for a in range(n):
            half = ins[a].shape[0] // 2
            mine = pl.ds(c * half, half)
            for j, (px, py) in enumerate(chips):
                arrivals[i].wait_recv()
                landed = outs[a].at[2 * px + py, mine, :]
                waits.append(_send(landed, landed, send_sems.at[a, 3 + j], recv_sems.at[a, 3 + j], sibling))
                i += 1
        for cp in arrivals:
            cp.wait_send()
        for cp in waits:
            cp.wait()

    return pl.pallas_call(
        body, name=name, in_specs=[ANY] * n, out_specs=[ANY] * n,
        out_shape=[jax.ShapeDtypeStruct((N_CHIPS,) + s.shape, s.dtype) for s in shards],
        scratch_shapes=[pltpu.SemaphoreType.DMA((n, 7)), pltpu.SemaphoreType.DMA((n, 7))])(*shards)


def swap_halves(grads, *, name):
    n = len(grads)

    def body(*refs):
        ins, outs = refs[:n], refs[n:2 * n]
        send_sems, recv_sems = refs[2 * n:]
        x, y, c, _ = _place()
        copies = []
        for a in range(n):
            half = ins[a].shape[1] // 2
            copies.append(_send(ins[a].at[:, pl.ds((1 - c) * half, half), :], outs[a], send_sems.at[a],
                                recv_sems.at[a], (x, y, 1 - c)))
        for cp in copies:
            cp.wait()

    return pl.pallas_call(
        body, name=name, in_specs=[ANY] * n, out_specs=[ANY] * n,
        out_shape=[jax.ShapeDtypeStruct((g.shape[0], g.shape[1] // 2, g.shape[2]), g.dtype) for g in grads],
        scratch_shapes=[pltpu.SemaphoreType.DMA((n,)), pltpu.SemaphoreType.DMA((n,))])(*grads)


def _chip_slot(s, k):
    rel = s ^ k
    return (rel >> 1) | ((rel & 1) << 1)


def sum_cores(g, theirs, ck, out_dtype, *, name):
    n, r, cols = g.shape
    half = r // 2
    tr = _tile(half, max(16, (1 << 19) // cols), 16)
    nb = half // tr

    def body(ck_ref, g_ref, t_ref, o_ref):
        o_ref[...] = (g_ref[...] + t_ref[...]).astype(o_ref.dtype)

    return pl.pallas_call(
        body, name=name,
        grid_spec=pltpu.PrefetchScalarGridSpec(
            num_scalar_prefetch=1, grid=(n, nb),
            in_specs=[pl.BlockSpec((None, tr, cols), lambda s, i, ck_ref: (s, ck_ref[0] * nb + i, 0)),
                      pl.BlockSpec((None, tr, cols), lambda s, i, ck_ref: (s, i, 0))],
            out_specs=pl.BlockSpec((None, tr, cols), lambda s, i, ck_ref: (_chip_slot(s, ck_ref[1]), i, 0))),
        out_shape=jax.ShapeDtypeStruct((n, half, cols), out_dtype),
        compiler_params=_params("parallel", "parallel"))(ck, g, theirs)


def scatter_chips(parts, *, name):
    n = len(parts)

    def body(*refs):
        ins, outs = refs[:n], refs[n:2 * n]
        send_sems, recv_sems = refs[2 * n:]
        x, y, c, chips = _place()
        copies = []
        for a in range(n):
            for j, (px, py) in enumerate(chips):
                copies.append(_send(ins[a].at[1 + j], outs[a].at[j], send_sems.at[a, j], recv_sems.at[a, j],
                                    (px, py, c)))
        for cp in copies:
            cp.wait()

    return pl.pallas_call(
        body, name=name, in_specs=[ANY] * n, out_specs=[ANY] * n,
        out_shape=[jax.ShapeDtypeStruct((3,) + p.shape[1:], p.dtype) for p in parts],
        scratch_shapes=[pltpu.SemaphoreType.DMA((n, 3)), pltpu.SemaphoreType.DMA((n, 3))])(*parts)


def sum_chips(own, others, ck, *, name):
    _, r, cols = own.shape
    tr = _tile(r, max(16, (1 << 19) // cols), 16)

    def body(ck_ref, a_ref, b0_ref, b1_ref, b2_ref, o_ref):
        o_ref[...] = (a_ref[...].astype(F32) + b0_ref[...].astype(F32) + b1_ref[...].astype(F32)
                      + b2_ref[...].astype(F32))

    other = lambda j: pl.BlockSpec((None, tr, cols), lambda i, ck_ref: (j, i, 0))
    return pl.pallas_call(
        body, name=name,
        grid_spec=pltpu.PrefetchScalarGridSpec(
            num_scalar_prefetch=1, grid=(r // tr,),
            in_specs=[pl.BlockSpec((None, tr, cols), lambda i, ck_ref: (0, i, 0)), other(0), other(1), other(2)],
            out_specs=pl.BlockSpec((None, tr, cols), lambda i, ck_ref: (ck_ref[0], i, 0))),
        out_shape=jax.ShapeDtypeStruct((2, r, cols), F32),
        compiler_params=_params("parallel"))(ck, own, others, others, others)


def share_halves(joined, *, name):
    n = len(joined)

    def body(*refs):
        outs = refs[n:2 * n]
        send_sems, recv_sems = refs[2 * n:]
        x, y, c, _ = _place()
        copies = [_send(outs[a].at[c], outs[a].at[c], send_sems.at[a], recv_sems.at[a], (x, y, 1 - c))
                  for a in range(n)]
        for cp in copies:
            cp.wait()

    return pl.pallas_call(
        body, name=name, in_specs=[ANY] * n, out_specs=[ANY] * n,
        out_shape=[jax.ShapeDtypeStruct(j.shape, j.dtype) for j in joined],
        input_output_aliases={a: a for a in range(n)},
        scratch_shapes=[pltpu.SemaphoreType.DMA((n,)), pltpu.SemaphoreType.DMA((n,))])(*joined)


def _pack(arrs, rows_mult):
    flat = jnp.concatenate([a.reshape(-1) for a in arrs])
    rows = -(-flat.shape[0] // (128 * rows_mult)) * rows_mult
    return jnp.pad(flat, (0, rows * 128 - flat.shape[0])).reshape(rows, 128)


def _unpack(packed, shapes):
    flat = packed.reshape(-1)
    out, pos = [], 0
    for s in shapes:
        size = 1
        for d in s:
            size *= d
        out.append(flat[pos:pos + size].reshape(s))
        pos += size
    return out


def _column_shard(full, k, axis):
    width = full.shape[axis] // N_CHIPS
    return lax.dynamic_slice_in_dim(full, k * width, width, axis)


def kernel(x, meta_tokens, norm0_g, l0_w_in, l0_lam_re, l0_lam_im, l0_log_dt, l0_b_re, l0_b_im, l0_c_re, l0_c_im, l0_d_skip, l0_w_glu, l0_b_glu, l0_w_out, norm1_g, l1_w_in, l1_conv_w, l1_conv_b, l1_w_out, norm2_g, l2_w_in, l2_w_grp, l2_b_grp, l2_scale, l2_w_out, norm3_g, l3_w_in, l3_lam_re, l3_lam_im, l3_log_dt, l3_b_re, l3_b_im, l3_c_re, l3_c_im, l3_d_skip, l3_w_glu, l3_b_glu, l3_w_out, final_g, loss_target, m_meta_tokens, m_norm0_g, m_l0_w_in, m_l0_lam_re, m_l0_lam_im, m_l0_log_dt, m_l0_b_re, m_l0_b_im, m_l0_c_re, m_l0_c_im, m_l0_d_skip, m_l0_w_glu, m_l0_b_glu, m_l0_w_out, m_norm1_g, m_l1_w_in, m_l1_conv_w, m_l1_conv_b, m_l1_w_out, m_norm2_g, m_l2_w_in, m_l2_w_grp, m_l2_b_grp, m_l2_scale, m_l2_w_out, m_norm3_g, m_l3_w_in, m_l3_lam_re, m_l3_lam_im, m_l3_log_dt, m_l3_b_re, m_l3_b_im, m_l3_c_re, m_l3_c_im, m_l3_d_skip, m_l3_w_glu, m_l3_b_glu, m_l3_w_out, m_final_g, v_meta_tokens, v_norm0_g, v_l0_w_in, v_l0_lam_re, v_l0_lam_im, v_l0_log_dt, v_l0_b_re, v_l0_b_im, v_l0_c_re, v_l0_c_im, v_l0_d_skip, v_l0_w_glu, v_l0_b_glu, v_l0_w_out, v_norm1_g, v_l1_w_in, v_l1_conv_w, v_l1_conv_b, v_l1_w_out, v_norm2_g, v_l2_w_in, v_l2_w_grp, v_l2_b_grp, v_l2_scale, v_l2_w_out, v_norm3_g, v_l3_w_in, v_l3_lam_re, v_l3_lam_im, v_l3_log_dt, v_l3_b_re, v_l3_b_im, v_l3_c_re, v_l3_c_im, v_l3_d_skip, v_l3_w_glu, v_l3_b_glu, v_l3_w_out, v_final_g):
    given = dict(locals())
    w = {n: given[n] for n in WEIGHTS}
    mom_m = {n: given["m_" + n] for n in WEIGHTS}
    mom_v = {n: given["v_" + n] for n in WEIGHTS}
    seq = x.shape[1]
    real = N_META + seq
    lp = -(-real // SCAN_ROWS) * SCAN_ROWS
    chip = 2 * lax.axis_index("x") + lax.axis_index("y")
    row = lambda a: a.reshape(1, -1)

    shard2d = {n: (w[n].reshape(-1, w[n].shape[-1]) if n == 'l2_w_grp' else w[n]) for n in BIG}
    full = dict(zip(BIG, gather_chips_by_halves([shard2d[n].astype(BF16) for n in BIG], name="gather_weights"),
                    strict=True))
    full.update(zip(SMALL_SHARDED, gather_chips([w[n] for n in SMALL_SHARDED], name="gather_small_shards"),
                    strict=True))
    for n in BIG:
        if n.endswith('w_in'):
            pass
        elif n == 'l2_w_grp':
            full[n] = full[n].reshape(N_CHIPS, len(POOL_WINDOWS), POOL_GROUP // N_CHIPS, POOL_GROUP)
        else:
            full[n] = full[n].reshape(1, -1, full[n].shape[-1])
    meta_full = full['meta_tokens'].transpose(1, 0, 2).reshape(N_META, D_MODEL)
    conv_w_full = full['l1_conv_w'].transpose(1, 0, 2).reshape(3, CONV_E)
    b_grp_full = full['l2_b_grp'].transpose(1, 0, 2).reshape(len(POOL_WINDOWS), POOL_GROUP)

    h0 = jnp.concatenate([meta_full, x[0], jnp.zeros((lp - real, D_MODEL), F32)], axis=0)
    target = jnp.concatenate([jnp.zeros((N_META, D_MODEL), F32), loss_target[0],
                              jnp.zeros((lp - real, D_MODEL), F32)], axis=0)
    grads = {}
    saved = {}

    def s5_layer_fwd(i, h):
        pre = "l%d_" % i
        tables, tables_vjp = jax.vjp(s5_tables, *[w[pre + s] for s in
                                                   ('lam_re', 'lam_im', 'log_dt', 'b_re', 'b_im', 'c_re', 'c_im')])
        ar, ai, bblk, cblk = tables
        n = rms_fwd(h, row(w["norm%d_g" % i]), name=pre + "norm")
        uz = mm_nn(n, full[pre + 'w_in'], name=pre + "in")
        y0, sr, si = s5_fwd(uz, bblk, cblk, ar, ai, row(w[pre + 'd_skip']), name=pre + "scan")
        (yg,), _ = rowwise(lambda yv: ([jax.nn.gelu(yv)], []), [(y0, D_MODEL, 0)], [],
                           [(D_MODEL, BF16, D_MODEL, 0)], name=pre + "gelu")
        q = mm_nn(yg, full[pre + 'w_glu'], bias=row(w[pre + 'b_glu']), name=pre + "glu")
        (y3,), _ = rowwise(lambda yv, qv, zv: ([_s5_gate(yv, qv, zv)], []),
                           [(y0, D_MODEL, 0), (q, D_MODEL, 0), (uz, D_MODEL, 1)], [],
                           [(D_MODEL, BF16, D_MODEL, 0)], name=pre + "gate")
        h_new = mm_nn(y3, full[pre + 'w_out'], add=h, name=pre + "out")
        saved[i] = dict(h=h, n=n, uz=uz, y0=y0, sr=sr, si=si, yg=yg, q=q, y3=y3, tables=tables,
                        tables_vjp=tables_vjp)
        return h_new

    def s5_layer_bwd(i, dh):
        pre = "l%d_" % i
        s = saved[i]
        ar, ai, bblk, cblk = s['tables']
        grads[pre + 'w_out'] = mm_tn(s['y3'], dh, 1, name=pre + "d_w_out")[0]
        dy3 = mm_nt(dh, full[pre + 'w_out'], name=pre + "d_y3")

        def gate_bwd(dyv, yv, qv, zv):
            _, vjp = jax.vjp(_s5_gate, yv, qv, zv)
            dy0, dq, dz = vjp(dyv)
            return [dy0, dq, dz], [_colsum(dq)]

        (dy0_direct, dq, dp), (db_glu,) = rowwise(
            gate_bwd, [(dy3, D_MODEL, 0), (s['y0'], D_MODEL, 0), (s['q'], D_MODEL, 0), (s['uz'], D_MODEL, 1)], [],
            [(D_MODEL, F32, D_MODEL, 0), (D_MODEL, F32, D_MODEL, 0), (2 * D_MODEL, F32, D_MODEL, 1)], [D_MODEL],
            name=pre + "d_gate")
        grads[pre + 'b_glu'] = db_glu.reshape(-1)
        grads[pre + 'w_glu'] = mm_tn(s['yg'], dq, 1, name=pre + "d_w_glu")[0]
        r = mm_nt(dq, full[pre + 'w_glu'], name=pre + "d_yg")

        def gelu_bwd(dv, rv, yv):
            _, vjp = jax.vjp(jax.nn.gelu, yv)
            return [dv + vjp(rv)[0]], []

        (dy0,), _ = rowwise(gelu_bwd, [(dy0_direct, D_MODEL, 0), (r, D_MODEL, 0), (s['y0'], D_MODEL, 0)], [],
                            [(D_MODEL, F32, D_MODEL, 0)], name=pre + "d_gelu")
        dp, dbblk, dcblk, dar, dai, dd = s5_bwd(dy0, s['uz'], s['sr'], s['si'], bblk, cblk, ar, ai,
                                                row(w[pre + 'd_skip']), dp, name=pre + "d_scan")
        grads[pre + 'd_skip'] = dd.reshape(-1)
        for key, val in zip(('lam_re', 'lam_im', 'log_dt', 'b_re', 'b_im', 'c_re', 'c_im'),
                            s['tables_vjp']((dar, dai, dbblk, dcblk)), strict=True):
            grads[pre + key] = val
        grads[pre + 'w_in'] = mm_tn(s['n'], dp, N_CHIPS, name=pre + "d_w_in")
        dn = mm_nt(dp, full[pre + 'w_in'], name=pre + "d_n")
        dh_in, dg = rms_bwd(s['h'], row(w["norm%d_g" % i]), dn, dh, name=pre + "d_norm")
        grads["norm%d_g" % i] = dg.reshape(-1)
        return dh_in

    h1 = s5_layer_fwd(0, h0)

    n1 = rms_fwd(h1, row(w['norm1_g']), name="l1_norm")
    p1 = mm_nn(n1, full['l1_w_in'], name="l1_in")
    y3_1 = conv_fwd(p1, conv_w_full, row(w['l1_conv_b']), name="l1_conv")
    h2 = mm_nn(y3_1, full['l1_w_out'], add=h1, name="l1_out")

    n2 = rms_fwd(h2, row(w['norm2_g']), name="l2_norm")
    p2 = mm_nn(n2, full['l2_w_in'], name="l2_in")
    mixed = pool_fwd(p2, name="l2_pool")
    o2 = grp_nn(mixed, full['l2_w_grp'], b_grp_full.reshape(1, -1), name="l2_grp")
    (y3_2,), _ = rowwise(lambda ov, zv, sv: ([_pool_gate(ov, zv, sv)], []),
                         [(o2, POOL_E, 0), (p2, POOL_E, 1)], [row(w['l2_scale'])],
                         [(POOL_E, BF16, POOL_E, 0)], name="l2_gate")
    h3 = mm_nn(y3_2, full['l2_w_out'], add=h2, name="l2_out")

    h4 = s5_layer_fwd(3, h3)

    def loss_fn(hv, tv, gv):
        i = pl.program_id(0)
        tr = hv.shape[0]
        yf, vjp = jax.vjp(_rms, hv, gv)
        pos = i * tr + lax.broadcasted_iota(jnp.int32, (tr, 1), 0)
        diff = jnp.where((pos >= N_META) & (pos < real), yf - tv, 0.0)
        dh, dg = vjp(diff / D_MODEL)
        return [dh], [dg, _colsum(diff * diff)]

    (dh,), (d_final_g, sq) = rowwise(loss_fn, [(h4, D_MODEL, 0), (target, D_MODEL, 0)], [row(w['final_g'])],
                                     [(D_MODEL, F32, D_MODEL, 0)], [D_MODEL, D_MODEL], name="loss")
    grads['final_g'] = d_final_g.reshape(-1)
    loss = lax.psum(0.5 / D_MODEL * jnp.sum(sq), ("x", "y", "c"))

    dh = s5_layer_bwd(3, dh)

    grads['l2_w_out'] = mm_tn(y3_2, dh, 1, name="l2_d_w_out")[0]
    dy3 = mm_nt(dh, full['l2_w_out'], name="l2_d_y3")

    def pool_gate_bwd(dyv, ov, zv, sv):
        _, vjp = jax.vjp(_pool_gate, ov, zv, sv)
        do, dz, dscale = vjp(dyv)
        return [do, dz], [dscale, _colsum(do)]

    (do2, dp2), (dscale, dbgrp) = rowwise(
        pool_gate_bwd, [(dy3, POOL_E, 0), (o2, POOL_E, 0), (p2, POOL_E, 1)], [row(w['l2_scale'])],
        [(POOL_E, F32, POOL_E, 0), (2 * POOL_E, F32, POOL_E, 1)], [POOL_E, POOL_E], name="l2_d_gate")
    grads['l2_scale'] = dscale.reshape(-1)
    grads['l2_b_grp'] = dbgrp.reshape(len(POOL_WINDOWS), POOL_GROUP)
    grads['l2_w_grp'] = grp_tn(mixed, do2, name="l2_d_w_grp")
    dmix = grp_nt(do2, full['l2_w_grp'], name="l2_d_mixed")
    dp2 = pool_bwd(dmix, dp2, name="l2_d_pool")
    grads['l2_w_in'] = mm_tn(n2, dp2, N_CHIPS, name="l2_d_w_in")
    dn = mm_nt(dp2, full['l2_w_in'], name="l2_d_n")
    dh, dg = rms_bwd(h2, row(w['norm2_g']), dn, dh, name="l2_d_norm")
    grads['norm2_g'] = dg.reshape(-1)

    grads['l1_w_out'] = mm_tn(y3_1, dh, 1, name="l1_d_w_out")[0]
    dy3 = mm_nt(dh, full['l1_w_out'], name="l1_d_y3")
    dp1, dcw, dcb = conv_bwd(dy3, p1, conv_w_full, row(w['l1_conv_b']), name="l1_d_conv")
    grads['l1_conv_w'] = dcw
    grads['l1_conv_b'] = dcb.reshape(-1)
    grads['l1_w_in'] = mm_tn(n1, dp1, N_CHIPS, name="l1_d_w_in")
    dn = mm_nt(dp1, full['l1_w_in'], name="l1_d_n")
    dh, dg = rms_bwd(h1, row(w['norm1_g']), dn, dh, name="l1_d_norm")
    grads['norm1_g'] = dg.reshape(-1)

    dh = s5_layer_bwd(0, dh)
    grad_x = dh[N_META:real][None]
    grads['meta_tokens'] = dh[:N_META]

    def chip_major(n):
        g = grads[n]
        if n.endswith('w_in'):
            return g
        if n == 'l2_w_grp':
            return g.reshape(N_CHIPS, -1, POOL_GROUP)
        return g.reshape(N_CHIPS, -1, g.shape[-1])
    small_local = _pack([grads[n] for n in SMALL], 32 * N_CHIPS)
    pieces = [chip_major(n) for n in BIG] + [small_local.reshape(N_CHIPS, -1, 128)]
    theirs = swap_halves(pieces, name="reduce_cores")
    ck = jnp.stack([lax.axis_index("c"), chip]).astype(jnp.int32)
    wire = [BF16] * len(BIG) + [F32]
    pair = [sum_cores(g, t, ck, dt, name="sum_cores_%d" % i)
            for i, (g, t, dt) in enumerate(zip(pieces, theirs, wire, strict=True))]
    from_chips = scatter_chips(pair, name="reduce_chips")
    joined = [sum_chips(p, f, ck, name="sum_chips_%d" % i)
              for i, (p, f) in enumerate(zip(pair, from_chips, strict=True))]
    joined = share_halves(joined, name="share_cores")
    reduced = [j.reshape(-1, j.shape[-1]) for j in joined]
    g_big = {n: reduced[i].reshape(w[n].shape) for i, n in enumerate(BIG)}
    (small_all,) = gather_chips([reduced[-1]], name="gather_small")
    g_small_full = dict(zip(SMALL, _unpack(small_all, [grads[n].shape for n in SMALL]), strict=True))
    g_small = {}
    for n in SMALL:
        g = g_small_full[n]
        g_small[n] = _column_shard(g, chip, g.ndim - 1) if n in SMALL_SHARDED else g

    out_g, out_d, out_m, out_v = {}, {}, {}, {}
    for n in BIG:
        shape = w[n].shape
        as2d = lambda a: a.reshape(-1, shape[-1])
        d, nm, nv = adamw(as2d(w[n]), as2d(g_big[n]), as2d(mom_m[n]), as2d(mom_v[n]), name="adamw_" + n)
        out_g[n], out_d[n], out_m[n], out_v[n] = g_big[n], d.reshape(shape), nm.reshape(shape), nv.reshape(shape)
    packs = [_pack([src[n] for n in SMALL], 8) for src in (w, g_small, mom_m, mom_v)]
    d, nm, nv = adamw(*packs, name="adamw_small")
    shapes = [w[n].shape for n in SMALL]
    for n, dv, mv, vv in zip(SMALL, _unpack(d, shapes), _unpack(nm, shapes), _unpack(nv, shapes), strict=True):
        out_g[n], out_d[n], out_m[n], out_v[n] = g_small[n], dv, mv, vv

    return (loss, grad_x, *[out_g[n] for n in WEIGHTS], *[out_d[n] for n in WEIGHTS],
            *[out_m[n] for n in WEIGHTS], *[out_v[n] for n in WEIGHTS])
```

```python
import functools

import jax
import jax.numpy as jnp
from jax import lax
from jax.experimental import pallas as pl
from jax.experimental.pallas import tpu as pltpu

F32, BF16 = jnp.float32, jnp.bfloat16
MESH = pl.DeviceIdType.MESH

D_MODEL = 1024
N_META = 16
EPS = 1e-6
S5_GROUPS, S5_GROUP, S5_STATE = 64, 16, 64
LANE_BLOCK = 128
S5_BLOCKS = D_MODEL // LANE_BLOCK
GROUPS_PER_BLOCK = LANE_BLOCK // S5_GROUP
STATE_BLOCK = GROUPS_PER_BLOCK * S5_STATE
SCAN_ROWS = 256
SUBLANES = 8
CONV_E = 2048
POOL_E = 2048
POOL_WINDOWS = (2, 4, 8, 16)
POOL_GROUP = 512
POOL_HALO = 16
N_CHIPS = 4

ADAM_LR, ADAM_B1, ADAM_B2, ADAM_EPS, ADAM_WD, ADAM_STEP = 0.001, 0.9, 0.999, 1e-08, 0.01, 10

VMEM_LIMIT = 48 * 1024 * 1024

WEIGHTS = ['meta_tokens', 'norm0_g', 'l0_w_in', 'l0_lam_re', 'l0_lam_im', 'l0_log_dt', 'l0_b_re', 'l0_b_im',
           'l0_c_re', 'l0_c_im', 'l0_d_skip', 'l0_w_glu', 'l0_b_glu', 'l0_w_out', 'norm1_g', 'l1_w_in',
           'l1_conv_w', 'l1_conv_b', 'l1_w_out', 'norm2_g', 'l2_w_in', 'l2_w_grp', 'l2_b_grp', 'l2_scale',
           'l2_w_out', 'norm3_g', 'l3_w_in', 'l3_lam_re', 'l3_lam_im', 'l3_log_dt', 'l3_b_re', 'l3_b_im',
           'l3_c_re', 'l3_c_im', 'l3_d_skip', 'l3_w_glu', 'l3_b_glu', 'l3_w_out', 'final_g']
BIG = ['l0_w_in', 'l0_w_glu', 'l0_w_out', 'l1_w_in', 'l1_w_out', 'l2_w_in', 'l2_w_grp', 'l2_w_out',
       'l3_w_in', 'l3_w_glu', 'l3_w_out']
SMALL_SHARDED = ['meta_tokens', 'l1_conv_w', 'l2_b_grp']
SMALL = [n for n in WEIGHTS if n not in BIG]


def _params(*sem):
    return pltpu.CompilerParams(dimension_semantics=sem, vmem_limit_bytes=VMEM_LIMIT)


class Job:
    def __init__(self, arrays, out_shapes, sems, fn):
        self.arrays, self.out_shapes, self.sems, self.fn = list(arrays), list(out_shapes), list(sems), fn


def _call(body, *, name, grid, in_specs, out_specs, out_shape, args, semantics, scratch=(), aliases=None, job=None):
    if job is None:
        return pl.pallas_call(
            body, name=name, grid=grid, in_specs=list(in_specs), out_specs=list(out_specs),
            out_shape=list(out_shape), scratch_shapes=list(scratch), input_output_aliases=aliases or {},
            compiler_params=_params(*semantics))(*args)
    counts = [len(in_specs), len(job.arrays), len(out_specs), len(job.out_shapes), len(scratch), len(job.sems)]

    def hosted(*refs):
        parts, pos = [], 0
        for k in counts:
            parts.append(refs[pos:pos + k])
            pos += k
        ins, job_ins, outs, job_outs, scr, job_sems = parts
        steps = [pl.program_id(d) for d in range(len(grid))]
        first = functools.reduce(jnp.logical_and, [s == 0 for s in steps])
        last = functools.reduce(jnp.logical_and, [s == g - 1 for s, g in zip(steps, grid, strict=True)])

        @pl.when(first)
        def _():
            job.fn(job_ins, job_outs, job_sems, "start")

        body(*ins, *outs, *scr)

        @pl.when(last)
        def _():
            job.fn(job_ins, job_outs, job_sems, "finish")

    any_spec = pl.BlockSpec(memory_space=pl.ANY)
    res = pl.pallas_call(
        hosted, name=name, grid=grid, in_specs=list(in_specs) + [any_spec] * len(job.arrays),
        out_specs=list(out_specs) + [any_spec] * len(job.out_shapes),
        out_shape=list(out_shape) + job.out_shapes, scratch_shapes=list(scratch) + job.sems,
        input_output_aliases=aliases or {},
        compiler_params=_params(*["arbitrary"] * len(grid)))(*args, *job.arrays)
    return res[:len(out_specs)], res[len(out_specs):]


def _tile(n, cap, mult):
    best = None
    for t in range(mult, min(n, cap) + 1, mult):
        if n % t == 0:
            best = t
    assert best is not None, (n, cap, mult)
    return best


def _with_job(res, job):
    return res[0] if job is None else (res[0][0], res[1])


def mm_nn(a, b, *, name, bias=None, add=None, out_dtype=F32, job=None):
    m, k = a.shape
    s, _, ns = b.shape
    n = s * ns
    tm, tn = _tile(m, 1088, 16), _tile(ns, 512, 128)
    per = ns // tn

    def body(*refs):
        a_ref, b_ref = refs[0], refs[1]
        o_ref = refs[-1]
        acc = jnp.dot(a_ref[...].astype(BF16), b_ref[...], preferred_element_type=F32)
        pos = 2
        if bias is not None:
            acc = acc + refs[pos][...]
            pos += 1
        if add is not None:
            acc = acc + refs[pos][...]
        o_ref[...] = acc.astype(o_ref.dtype)

    in_specs = [pl.BlockSpec((tm, k), lambda i, j: (i, 0)),
                pl.BlockSpec((None, k, tn), lambda i, j: (j // per, 0, j % per))]
    args = [a, b]
    if bias is not None:
        in_specs.append(pl.BlockSpec((1, tn), lambda i, j: (0, j)))
        args.append(bias)
    if add is not None:
        in_specs.append(pl.BlockSpec((tm, tn), lambda i, j: (i, j)))
        args.append(add)
    return _with_job(_call(
        body, name=name, grid=(m // tm, n // tn), in_specs=in_specs,
        out_specs=[pl.BlockSpec((tm, tn), lambda i, j: (i, j))],
        out_shape=[jax.ShapeDtypeStruct((m, n), out_dtype)], args=args,
        semantics=("parallel", "parallel"), job=job), job)


def mm_nt(g, w, *, name, job=None):
    m, kc = g.shape
    s, nout, kcs = w.shape
    assert s * kcs == kc
    tm, tno, tk = _tile(m, 1088, 16), _tile(nout, 1024, 128), _tile(kcs, 1024, 128)
    per = kcs // tk

    def body(g_ref, w_ref, o_ref):
        kk = pl.program_id(2)
        part = lax.dot_general(g_ref[...].astype(BF16), w_ref[...], (((1,), (1,)), ((), ())),
                               preferred_element_type=F32)

        @pl.when(kk == 0)
        def _():
            o_ref[...] = part

        @pl.when(kk > 0)
        def _():
            o_ref[...] += part

    return _with_job(_call(
        body, name=name, grid=(m // tm, nout // tno, kc // tk),
        in_specs=[pl.BlockSpec((tm, tk), lambda i, j, kk: (i, kk)),
                  pl.BlockSpec((None, tno, tk), lambda i, j, kk: (kk // per, j, kk % per))],
        out_specs=[pl.BlockSpec((tm, tno), lambda i, j, kk: (i, j))],
        out_shape=[jax.ShapeDtypeStruct((m, nout), F32)], args=(g, w),
        semantics=("parallel", "parallel", "arbitrary"), job=job), job)


def mm_tn(a, g, shards, *, name, job=None):
    m, ka = a.shape
    n = g.shape[1]
    ns = n // shards
    tm, tka, tn = _tile(m, 1088, 16), _tile(ka, 1024, 128), _tile(ns, 1024, 128)
    per = ns // tn

    def body(a_ref, g_ref, o_ref):
        mm = pl.program_id(2)
        part = lax.dot_general(a_ref[...].astype(BF16), g_ref[...].astype(BF16), (((0,), (0,)), ((), ())),
                               preferred_element_type=F32)

        @pl.when(mm == 0)
        def _():
            o_ref[...] = part

        @pl.when(mm > 0)
        def _():
            o_ref[...] += part

    return _with_job(_call(
        body, name=name, grid=(ka // tka, n // tn, m // tm),
        in_specs=[pl.BlockSpec((tm, tka), lambda i, j, mm: (mm, i)),
                  pl.BlockSpec((tm, tn), lambda i, j, mm: (mm, j))],
        out_specs=[pl.BlockSpec((None, tka, tn), lambda i, j, mm: (j // per, i, j % per))],
        out_shape=[jax.ShapeDtypeStruct((shards, ka, ns), F32)], args=(a, g),
        semantics=("parallel", "parallel", "arbitrary"), job=job), job)


def grp_nn(a, w, bias, *, name):
    m = a.shape[0]
    tm = _tile(m, 1088, 16)
    ng = len(POOL_WINDOWS)

    def body(a_ref, w_ref, b_ref, o_ref):
        wj = w_ref[...].reshape(POOL_GROUP, POOL_GROUP)
        o_ref[...] = jnp.dot(a_ref[...].astype(BF16), wj, preferred_element_type=F32) + b_ref[...]

    return pl.pallas_call(
        body, name=name, grid=(m // tm, ng),
        in_specs=[pl.BlockSpec((tm, POOL_GROUP), lambda i, j: (i, j)),
                  pl.BlockSpec((N_CHIPS, None, POOL_GROUP // N_CHIPS, POOL_GROUP), lambda i, j: (0, j, 0, 0)),
                  pl.BlockSpec((1, POOL_GROUP), lambda i, j: (0, j))],
        out_specs=pl.BlockSpec((tm, POOL_GROUP), lambda i, j: (i, j)),
        out_shape=jax.ShapeDtypeStruct((m, ng * POOL_GROUP), F32),
        compiler_params=_params("parallel", "parallel"))(a, w, bias)


def grp_nt(g, w, *, name):
    m = g.shape[0]
    tm = _tile(m, 1088, 16)
    ng = len(POOL_WINDOWS)

    def body(g_ref, w_ref, o_ref):
        wj = w_ref[...].reshape(POOL_GROUP, POOL_GROUP)
        o_ref[...] = lax.dot_general(g_ref[...].astype(BF16), wj, (((1,), (1,)), ((), ())),
                                     preferred_element_type=F32)

    return pl.pallas_call(
        body, name=name, grid=(m // tm, ng),
        in_specs=[pl.BlockSpec((tm, POOL_GROUP), lambda i, j: (i, j)),
                  pl.BlockSpec((N_CHIPS, None, POOL_GROUP // N_CHIPS, POOL_GROUP), lambda i, j: (0, j, 0, 0))],
        out_specs=pl.BlockSpec((tm, POOL_GROUP), lambda i, j: (i, j)),
        out_shape=jax.ShapeDtypeStruct((m, ng * POOL_GROUP), F32),
        compiler_params=_params("parallel", "parallel"))(g, w)


def grp_tn(a, g, *, name):
    m = a.shape[0]
    tm = _tile(m, 1088, 16)
    ng = len(POOL_WINDOWS)
    rows = POOL_GROUP // N_CHIPS

    def body(a_ref, g_ref, o_ref):
        mm = pl.program_id(1)
        part = lax.dot_general(a_ref[...].astype(BF16), g_ref[...].astype(BF16), (((0,), (0,)), ((), ())),
                               preferred_element_type=F32).reshape(N_CHIPS, rows, POOL_GROUP)

        @pl.when(mm == 0)
        def _():
            o_ref[...] = part

        @pl.when(mm > 0)
        def _():
            o_ref[...] += part

    return pl.pallas_call(
        body, name=name, grid=(ng, m // tm),
        in_specs=[pl.BlockSpec((tm, POOL_GROUP), lambda j, mm: (mm, j)),
                  pl.BlockSpec((tm, POOL_GROUP), lambda j, mm: (mm, j))],
        out_specs=pl.BlockSpec((N_CHIPS, None, rows, POOL_GROUP), lambda j, mm: (0, j, 0, 0)),
        out_shape=jax.ShapeDtypeStruct((N_CHIPS, ng, rows, POOL_GROUP), F32),
        compiler_params=_params("parallel", "arbitrary"))(a, g)


def rowwise(fn, rows, bcast, outs, accs=(), *, name, aliases=None):
    m = rows[0][0].shape[0]
    tr = m // 16
    n_in = len(rows) + len(bcast)

    def body(*refs):
        vals = [r[...] for r in refs[:n_in]]
        o_vals, a_vals = fn(*vals)
        for ref, val in zip(refs[n_in:n_in + len(outs)], o_vals, strict=True):
            ref[...] = val.astype(ref.dtype)
        step = pl.program_id(0)
        for ref, val in zip(refs[n_in + len(outs):], a_vals, strict=True):
            @pl.when(step == 0)
            def _(ref=ref, val=val):
                ref[...] = val

            @pl.when(step > 0)
            def _(ref=ref, val=val):
                ref[...] += val

    in_specs = [pl.BlockSpec((tr, w), functools.partial(lambda i, cb: (i, cb), cb=cb)) for _, w, cb in rows]
    in_specs += [pl.BlockSpec(b.shape, lambda i: (0, 0)) for b in bcast]
    out_specs = [pl.BlockSpec((tr, w), functools.partial(lambda i, cb: (i, cb), cb=cb)) for _, _, w, cb in outs]
    out_specs += [pl.BlockSpec((1, w), lambda i: (0, 0)) for w in accs]
    out_shape = [jax.ShapeDtypeStruct((m, cols), dt) for cols, dt, _, _ in outs]
    out_shape += [jax.ShapeDtypeStruct((1, w), F32) for w in accs]
    res = pl.pallas_call(
        body, name=name, grid=(m // tr,), in_specs=in_specs, out_specs=out_specs, out_shape=out_shape,
        input_output_aliases=aliases or {},
        compiler_params=_params("arbitrary"))(*[r[0] for r in rows], *bcast)
    return res[:len(outs)], res[len(outs):]


def _rms(h, g):
    return h * lax.rsqrt(jnp.mean(h * h, axis=-1, keepdims=True) + EPS) * g


def _colsum(v):
    return jnp.sum(v, axis=0, keepdims=True)


def rms_fwd(h, g, *, name):
    (n,), _ = rowwise(lambda hv, gv: ([_rms(hv, gv)], []), [(h, D_MODEL, 0)], [g],
                      [(D_MODEL, BF16, D_MODEL, 0)], name=name)
    return n


def rms_bwd(h, g, dn, dh_out, *, name):
    def fn(hv, dnv, dhv, gv):
        _, vjp = jax.vjp(_rms, hv, gv)
        dh, dg = vjp(dnv)
        return [dhv + dh], [dg]

    (dh,), (dg,) = rowwise(fn, [(h, D_MODEL, 0), (dn, D_MODEL, 0), (dh_out, D_MODEL, 0)], [g],
                           [(D_MODEL, F32, D_MODEL, 0)], [D_MODEL], name=name)
    return dh, dg


def _s5_gate(y0, q, z):
    yg = jax.nn.gelu(y0)
    return yg * jax.nn.sigmoid(q) * jax.nn.silu(z)


def _pool_gate(o, z, scale):
    return o * scale * jax.nn.silu(z)


def _shift_rows(v, d, down):
    t = v.shape[0]
    row = lax.broadcasted_iota(jnp.int32, v.shape, 0)
    if down:
        return jnp.where(row >= d, pltpu.roll(v, d, 0), 0.0)
    return jnp.where(row < t - d, pltpu.roll(v, t - d, 0), 0.0)


def _cmul(ar, ai, br, bi):
    return ar * br - ai * bi, ar * bi + ai * br


def _scan_rows(sr, si, ar, ai, down):
    pr, pi = ar, ai
    d = 1
    while d < sr.shape[0]:
        hr, hi = _shift_rows(sr, d, down), _shift_rows(si, d, down)
        sr, si = sr + pr * hr - pi * hi, si + pr * hi + pi * hr
        pr, pi = pr * pr - pi * pi, 2.0 * pr * pi
        d *= 2
    return sr, si


def _scan(xr, xi, ar, ai, cr, ci, down):
    t, n = xr.shape
    nb = t // SUBLANES
    pw = [(ar, ai)]
    for _ in range(SUBLANES - 1):
        pw.append(_cmul(*pw[-1], ar, ai))
    sub = lax.broadcasted_iota(jnp.int32, (SUBLANES, n), 0)

    def table(exps):
        tr, ti = jnp.zeros((SUBLANES, n), F32), jnp.zeros((SUBLANES, n), F32)
        for r, e in enumerate(exps):
            if e:
                tr, ti = jnp.where(sub == r, pw[e - 1][0], tr), jnp.where(sub == r, pw[e - 1][1], ti)
        return tr[None], ti[None]

    sr, si = xr.reshape(nb, SUBLANES, n), xi.reshape(nb, SUBLANES, n)
    d = 1
    while d < SUBLANES:
        mr, mi = table([d if (r >= d if down else r < SUBLANES - d) else 0 for r in range(SUBLANES)])
        turn = d if down else SUBLANES - d
        hr, hi = pltpu.roll(sr, turn, 1), pltpu.roll(si, turn, 1)
        sr, si = sr + mr * hr - mi * hi, si + mr * hi + mi * hr
        d *= 2
    edge = SUBLANES - 1 if down else 0
    tr, ti = sr[:, edge, :], si[:, edge, :]
    ends = lax.broadcasted_iota(jnp.int32, (nb, n), 0) == (0 if down else nb - 1)
    er, ei = _cmul(*pw[SUBLANES - 1], cr, ci)
    tr, ti = tr + jnp.where(ends, er, 0.0), ti + jnp.where(ends, ei, 0.0)
    pr, pi = _scan_rows(tr, ti, *pw[SUBLANES - 1], down)
    far = nb - 1 if down else 0
    new_cr, new_ci = pr[far:far + 1, :], pi[far:far + 1, :]
    inr = (_shift_rows(pr, 1, down) + jnp.where(ends, cr, 0.0))[:, None, :]
    ini = (_shift_rows(pi, 1, down) + jnp.where(ends, ci, 0.0))[:, None, :]
    qr, qi = table([r + 1 if down else SUBLANES - r for r in range(SUBLANES)])
    sr, si = sr + qr * inr - qi * ini, si + qr * ini + qi * inr
    return sr.reshape(t, n), si.reshape(t, n), new_cr, new_ci


def s5_fwd(uz, bblk, cblk, ar, ai, dskip, *, name, job=None):
    lp = uz.shape[0]
    t = SCAN_ROWS
    nst = S5_GROUPS * S5_STATE

    def body(u_ref, b_ref, c_ref, ar_ref, ai_ref, d_ref, y_ref, sr_ref, si_ref, cr, ci):
        step = pl.program_id(1)

        @pl.when(step == 0)
        def _():
            cr[...] = jnp.zeros_like(cr)
            ci[...] = jnp.zeros_like(ci)

        u = u_ref[...]
        a_r, a_i = ar_ref[...], ai_ref[...]
        x = jnp.dot(u.astype(BF16), b_ref[...].astype(BF16), preferred_element_type=F32)
        sr, si, cr[...], ci[...] = _scan(x[:, :STATE_BLOCK], x[:, STATE_BLOCK:], a_r, a_i, cr[...], ci[...], True)
        sr_ref[...] = sr
        si_ref[...] = si
        s = jnp.concatenate([sr, si], axis=1).astype(BF16)
        y_ref[...] = jnp.dot(s, c_ref[...].astype(BF16), preferred_element_type=F32) + d_ref[...] * u

    return _call(
        body, name=name, grid=(S5_BLOCKS, lp // t),
        in_specs=[pl.BlockSpec((t, LANE_BLOCK), lambda b, c: (c, b)),
                  pl.BlockSpec((None, LANE_BLOCK, 2 * STATE_BLOCK), lambda b, c: (b, 0, 0)),
                  pl.BlockSpec((None, 2 * STATE_BLOCK, LANE_BLOCK), lambda b, c: (b, 0, 0)),
                  pl.BlockSpec((1, STATE_BLOCK), lambda b, c: (0, b)),
                  pl.BlockSpec((1, STATE_BLOCK), lambda b, c: (0, b)),
                  pl.BlockSpec((1, LANE_BLOCK), lambda b, c: (0, b))],
        out_specs=[pl.BlockSpec((t, LANE_BLOCK), lambda b, c: (c, b)),
                   pl.BlockSpec((t, STATE_BLOCK), lambda b, c: (c, b)),
                   pl.BlockSpec((t, STATE_BLOCK), lambda b, c: (c, b))],
        out_shape=[jax.ShapeDtypeStruct((lp, D_MODEL), F32), jax.ShapeDtypeStruct((lp, nst), F32),
                   jax.ShapeDtypeStruct((lp, nst), F32)],
        scratch=[pltpu.VMEM((1, STATE_BLOCK), F32), pltpu.VMEM((1, STATE_BLOCK), F32)],
        args=(uz, bblk, cblk, ar, ai, dskip), semantics=("parallel", "arbitrary"), job=job)


def s5_bwd(dy0, uz, sr, si, bblk, cblk, ar, ai, dskip, dp, *, name, job=None):
    lp = uz.shape[0]
    t = SCAN_ROWS
    nch = lp // t
    nst = S5_GROUPS * S5_STATE

    def body(dy_ref, u_ref, sr_ref, si_ref, b_ref, c_ref, ar_ref, ai_ref, d_ref, _, du_ref, db_ref, dc_ref,
             dar_ref, dai_ref, dd_ref, cr, ci):
        step = pl.program_id(1)

        @pl.when(step == 0)
        def _():
            cr[...] = jnp.zeros_like(cr)
            ci[...] = jnp.zeros_like(ci)

        dy, u = dy_ref[...], u_ref[...]
        dyb, ub = dy.astype(BF16), u.astype(BF16)
        a_r, a_i = ar_ref[...], -ai_ref[...]
        g = lax.dot_general(dyb, c_ref[...].astype(BF16), (((1,), (1,)), ((), ())), preferred_element_type=F32)
        gr, gi = g[:, :STATE_BLOCK], g[:, STATE_BLOCK:]
        nxt_r, nxt_i = cr[...], ci[...]
        last = lax.broadcasted_iota(jnp.int32, gr.shape, 0) == t - 1
        lr, li, cr[...], ci[...] = _scan(gr, gi, a_r, a_i, nxt_r, nxt_i, False)
        nr = _shift_rows(lr, 1, False) + jnp.where(last, nxt_r, 0.0)
        ni = _shift_rows(li, 1, False) + jnp.where(last, nxt_i, 0.0)
        s_r, s_i = sr_ref[...], si_ref[...]
        dar = _colsum(s_r * nr + s_i * ni)
        dai = _colsum(s_r * ni - s_i * nr)
        lam = jnp.concatenate([lr, li], axis=1).astype(BF16)
        sb = jnp.concatenate([s_r, s_i], axis=1).astype(BF16)
        dc = lax.dot_general(sb, dyb, (((0,), (0,)), ((), ())), preferred_element_type=F32)
        db = lax.dot_general(ub, lam, (((0,), (0,)), ((), ())), preferred_element_type=F32)
        du_ref[...] = lax.dot_general(lam, b_ref[...].astype(BF16), (((1,), (1,)), ((), ())),
                                      preferred_element_type=F32) + d_ref[...] * dy
        dd = _colsum(dy * u)

        @pl.when(step == 0)
        def _():
            db_ref[...] = db
            dc_ref[...] = dc
            dar_ref[...] = dar
            dai_ref[...] = dai
            dd_ref[...] = dd

        @pl.when(step > 0)
        def _():
            db_ref[...] += db
            dc_ref[...] += dc
            dar_ref[...] += dar
            dai_ref[...] += dai
            dd_ref[...] += dd

    rev = lambda b, c: (nch - 1 - c, b)
    return _call(
        body, name=name, grid=(S5_BLOCKS, nch),
        in_specs=[pl.BlockSpec((t, LANE_BLOCK), rev), pl.BlockSpec((t, LANE_BLOCK), rev),
                  pl.BlockSpec((t, STATE_BLOCK), rev), pl.BlockSpec((t, STATE_BLOCK), rev),
                  pl.BlockSpec((None, LANE_BLOCK, 2 * STATE_BLOCK), lambda b, c: (b, 0, 0)),
                  pl.BlockSpec((None, 2 * STATE_BLOCK, LANE_BLOCK), lambda b, c: (b, 0, 0)),
                  pl.BlockSpec((1, STATE_BLOCK), lambda b, c: (0, b)),
                  pl.BlockSpec((1, STATE_BLOCK), lambda b, c: (0, b)),
                  pl.BlockSpec((1, LANE_BLOCK), lambda b, c: (0, b)),
                  pl.BlockSpec(memory_space=pl.ANY)],
        out_specs=[pl.BlockSpec((t, LANE_BLOCK), rev),
                   pl.BlockSpec((None, LANE_BLOCK, 2 * STATE_BLOCK), lambda b, c: (b, 0, 0)),
                   pl.BlockSpec((None, 2 * STATE_BLOCK, LANE_BLOCK), lambda b, c: (b, 0, 0)),
                   pl.BlockSpec((1, STATE_BLOCK), lambda b, c: (0, b)),
                   pl.BlockSpec((1, STATE_BLOCK), lambda b, c: (0, b)),
                   pl.BlockSpec((1, LANE_BLOCK), lambda b, c: (0, b))],
        out_shape=[jax.ShapeDtypeStruct(dp.shape, F32),
                   jax.ShapeDtypeStruct((S5_BLOCKS, LANE_BLOCK, 2 * STATE_BLOCK), F32),
                   jax.ShapeDtypeStruct((S5_BLOCKS, 2 * STATE_BLOCK, LANE_BLOCK), F32),
                   jax.ShapeDtypeStruct((1, nst), F32), jax.ShapeDtypeStruct((1, nst), F32),
                   jax.ShapeDtypeStruct((1, D_MODEL), F32)],
        scratch=[pltpu.VMEM((1, STATE_BLOCK), F32), pltpu.VMEM((1, STATE_BLOCK), F32)],
        aliases={9: 0}, args=(dy0, uz, sr, si, bblk, cblk, ar, ai, dskip, dp),
        semantics=("parallel", "arbitrary"), job=job)


def s5_tables(lam_re, lam_im, log_dt, b_re, b_im, c_re, c_im):
    dt = jnp.exp(log_dt)[:, None]
    mag = jnp.exp(lam_re * dt)
    ar = mag * jnp.cos(lam_im * dt)
    ai = mag * jnp.sin(lam_im * dt)
    den = lam_re * lam_re + lam_im * lam_im
    kr = ((ar - 1.0) * lam_re + ai * lam_im) / den
    ki = (ai * lam_re - (ar - 1.0) * lam_im) / den
    bbr = kr[..., None] * b_re - ki[..., None] * b_im
    bbi = kr[..., None] * b_im + ki[..., None] * b_re
    eye = jnp.eye(GROUPS_PER_BLOCK, dtype=F32)

    def b_blocks(v):
        v = v.reshape(S5_BLOCKS, GROUPS_PER_BLOCK, S5_STATE, S5_GROUP)
        return jnp.einsum('bgpi,gh->bgihp', v, eye).reshape(S5_BLOCKS, LANE_BLOCK, STATE_BLOCK)

    def c_blocks(v):
        v = v.reshape(S5_BLOCKS, GROUPS_PER_BLOCK, S5_GROUP, S5_STATE)
        return jnp.einsum('bgip,gh->bgphi', v, eye).reshape(S5_BLOCKS, STATE_BLOCK, LANE_BLOCK)

    bblk = jnp.concatenate([b_blocks(bbr), b_blocks(bbi)], axis=2)
    cblk = jnp.concatenate([c_blocks(c_re), -c_blocks(c_im)], axis=1)
    nst = S5_GROUPS * S5_STATE
    return ar.reshape(1, nst), ai.reshape(1, nst), bblk, cblk


def _silu_grad(z):
    s = jax.nn.sigmoid(z)
    return s * (1.0 + z * (1.0 - s))


def conv_fwd(p, conv_w, conv_b, *, name):
    lp = p.shape[0]
    tr = lp // 16
    e = CONV_E
    hb = tr // 8

    def body(p_ref, cgh_ref, vh_ref, w_ref, b_ref, o_ref):
        i = pl.program_id(0)
        bg, cg, v, z = (p_ref[:, k * e:(k + 1) * e] for k in range(4))
        hc = cg * v
        halo = jnp.where(i > 0, cgh_ref[...] * vh_ref[...], 0.0)
        ext = jnp.concatenate([halo, hc], axis=0)
        w = w_ref[...]
        conv = (w[0:1] * pltpu.roll(ext, 2, 0)[8:] + w[1:2] * pltpu.roll(ext, 1, 0)[8:] + w[2:3] * hc
                + b_ref[...])
        o_ref[...] = (bg * conv * jax.nn.silu(z)).astype(o_ref.dtype)

    prev = lambda col: (lambda i: (jnp.maximum(i * hb - 1, 0), col))
    return pl.pallas_call(
        body, name=name, grid=(lp // tr,),
        in_specs=[pl.BlockSpec((tr, 4 * e), lambda i: (i, 0)),
                  pl.BlockSpec((8, e), prev(1)), pl.BlockSpec((8, e), prev(2)),
                  pl.BlockSpec((3, e), lambda i: (0, 0)), pl.BlockSpec((1, e), lambda i: (0, 0))],
        out_specs=pl.BlockSpec((tr, e), lambda i: (i, 0)),
        out_shape=jax.ShapeDtypeStruct((lp, e), BF16),
        compiler_params=_params("parallel"))(p, p, p, conv_w, conv_b)


def conv_bwd(dy3, p, conv_w, conv_b, *, name):
    lp = p.shape[0]
    tr = lp // 32
    e = CONV_E
    hb = tr // 8
    nt = lp // tr

    def body(dy_ref, p_ref, cgh_ref, vh_ref, dyn_ref, bgn_ref, zn_ref, w_ref, b_ref, dp_ref, dw_ref, db_ref):
        i = pl.program_id(0)
        dy = dy_ref[...]
        bg, cg, v, z = (p_ref[:, k * e:(k + 1) * e] for k in range(4))
        w = w_ref[...]
        hc = cg * v
        halo = jnp.where(i > 0, cgh_ref[...] * vh_ref[...], 0.0)
        ext = jnp.concatenate([halo, hc], axis=0)
        hc2, hc1 = pltpu.roll(ext, 2, 0)[8:], pltpu.roll(ext, 1, 0)[8:]
        yc = w[0:1] * hc2 + w[1:2] * hc1 + w[2:3] * hc + b_ref[...]
        sz = jax.nn.silu(z)
        dyc = dy * bg * sz
        dyc_n = jnp.where(i < nt - 1, dyn_ref[...] * bgn_ref[...] * jax.nn.silu(zn_ref[...]), 0.0)
        ext_n = jnp.concatenate([dyc, dyc_n], axis=0)
        n_rows = tr + 8
        dhc = (w[2:3] * dyc + w[1:2] * pltpu.roll(ext_n, n_rows - 1, 0)[:tr]
               + w[0:1] * pltpu.roll(ext_n, n_rows - 2, 0)[:tr])
        dp_ref[:, 0:e] = dy * yc * sz
        dp_ref[:, e:2 * e] = dhc * v
        dp_ref[:, 2 * e:3 * e] = dhc * cg
        dp_ref[:, 3 * e:4 * e] = dy * bg * yc * _silu_grad(z)
        dw = jnp.concatenate([_colsum(dyc * hc2), _colsum(dyc * hc1), _colsum(dyc * hc)], axis=0)
        db = _colsum(dyc)

        @pl.when(i == 0)
        def _():
            dw_ref[...] = dw
            db_ref[...] = db

        @pl.when(i > 0)
        def _():
            dw_ref[...] += dw
            db_ref[...] += db

    prev = lambda col: (lambda i: (jnp.maximum(i * hb - 1, 0), col))
    nxt = lambda col: (lambda i: (jnp.minimum((i + 1) * hb, lp // 8 - 1), col))
    return pl.pallas_call(
        body, name=name, grid=(nt,),
        in_specs=[pl.BlockSpec((tr, e), lambda i: (i, 0)), pl.BlockSpec((tr, 4 * e), lambda i: (i, 0)),
                  pl.BlockSpec((8, e), prev(1)), pl.BlockSpec((8, e), prev(2)),
                  pl.BlockSpec((8, e), nxt(0)), pl.BlockSpec((8, e), nxt(0)), pl.BlockSpec((8, e), nxt(3)),
                  pl.BlockSpec((3, e), lambda i: (0, 0)), pl.BlockSpec((1, e), lambda i: (0, 0))],
        out_specs=[pl.BlockSpec((tr, 4 * e), lambda i: (i, 0)), pl.BlockSpec((3, e), lambda i: (0, 0)),
                   pl.BlockSpec((1, e), lambda i: (0, 0))],
        out_shape=[jax.ShapeDtypeStruct((lp, 4 * e), F32), jax.ShapeDtypeStruct((3, e), F32),
                   jax.ShapeDtypeStruct((1, e), F32)],
        compiler_params=_params("arbitrary"))(dy3, p, p, p, dy3, p, p, conv_w, conv_b)


def _pool_counts(i, tr, rows, offset, w):
    t = (i * tr + offset + 1 + lax.broadcasted_iota(jnp.int32, (rows, 1), 0)).astype(F32)
    return jnp.minimum(t, float(w))


def pool_fwd(p, *, name):
    lp = p.shape[0]
    tr = lp // 16
    h = POOL_HALO
    hb = tr // h

    def body(u_ref, uh_ref, o_ref):
        i = pl.program_id(0)
        u = u_ref[...]
        ext = jnp.concatenate([jnp.where(i > 0, uh_ref[...], 0.0), u], axis=0)
        for k, w in enumerate(POOL_WINDOWS):
            cols = slice(k * POOL_GROUP, (k + 1) * POOL_GROUP)
            s = ext[:, cols]
            d = 1
            while d < w:
                s = s + pltpu.roll(s, d, 0)
                d *= 2
            o_ref[:, cols] = (s[h:] / _pool_counts(i, tr, tr, 0, w) - u[:, cols]).astype(o_ref.dtype)

    return pl.pallas_call(
        body, name=name, grid=(lp // tr,),
        in_specs=[pl.BlockSpec((tr, POOL_E), lambda i: (i, 0)),
                  pl.BlockSpec((h, POOL_E), lambda i: (jnp.maximum(i * hb - 1, 0), 0))],
        out_specs=pl.BlockSpec((tr, POOL_E), lambda i: (i, 0)),
        out_shape=jax.ShapeDtypeStruct((lp, POOL_E), BF16),
        compiler_params=_params("parallel"))(p, p)


def pool_bwd(dmix, dp, *, name):
    lp = dmix.shape[0]
    tr = lp // 16
    h = POOL_HALO
    hb = tr // h
    nt = lp // tr

    def body(dm_ref, dmn_ref, _, du_ref):
        i = pl.program_id(0)
        dm = dm_ref[...]
        dmn = jnp.where(i < nt - 1, dmn_ref[...], 0.0)
        n_rows = tr + h
        for k, w in enumerate(POOL_WINDOWS):
            cols = slice(k * POOL_GROUP, (k + 1) * POOL_GROUP)
            s = jnp.concatenate([dm[:, cols] / _pool_counts(i, tr, tr, 0, w),
                                 dmn[:, cols] / _pool_counts(i, tr, h, tr, w)], axis=0)
            d = 1
            while d < w:
                s = s + pltpu.roll(s, n_rows - d, 0)
                d *= 2
            du_ref[:, cols] = s[:tr] - dm[:, cols]

    return pl.pallas_call(
        body, name=name, grid=(nt,),
        in_specs=[pl.BlockSpec((tr, POOL_E), lambda i: (i, 0)),
                  pl.BlockSpec((h, POOL_E), lambda i: (jnp.minimum((i + 1) * hb, lp // h - 1), 0)),
                  pl.BlockSpec(memory_space=pl.ANY)],
        out_specs=pl.BlockSpec((tr, POOL_E), lambda i: (i, 0)),
        out_shape=jax.ShapeDtypeStruct(dp.shape, F32),
        input_output_aliases={2: 0},
        compiler_params=_params("parallel"))(dmix, dmix, dp)


def adamw(w, g, m, v, *, name):
    r, c = w.shape
    tr = _tile(r, max(8, (1 << 19) // c), 8)

    def body(w_ref, g_ref, m_ref, v_ref, d_ref, nm_ref, nv_ref):
        gv = g_ref[...]
        nm = ADAM_B1 * m_ref[...] + (1.0 - ADAM_B1) * gv
        nv = ADAM_B2 * v_ref[...] + (1.0 - ADAM_B2) * jnp.square(gv)
        m_hat = nm / (1.0 - ADAM_B1 ** ADAM_STEP)
        v_hat = nv / (1.0 - ADAM_B2 ** ADAM_STEP)
        d_ref[...] = -ADAM_LR * (m_hat / (jnp.sqrt(v_hat) + ADAM_EPS) + ADAM_WD * w_ref[...])
        nm_ref[...] = nm
        nv_ref[...] = nv

    spec = pl.BlockSpec((tr, c), lambda i: (i, 0))
    return pl.pallas_call(
        body, name=name, grid=(r // tr,), in_specs=[spec] * 4, out_specs=[spec] * 3,
        out_shape=[jax.ShapeDtypeStruct((r, c), F32)] * 3,
        compiler_params=_params("parallel"))(w, g, m, v)


ANY = pl.BlockSpec(memory_space=pl.ANY)


def _place():
    x, y, c = lax.axis_index("x"), lax.axis_index("y"), lax.axis_index("c")
    return x, y, c, [(1 - x, y), (x, 1 - y), (1 - x, 1 - y)]


DMA_PIECE_BYTES = 512 * 1024


def _pieces(src, dst):
    shape = src.shape
    rows, cols = shape[-2], shape[-1]
    item = jnp.dtype(src.dtype).itemsize
    unit = 32 // item
    want = max(1, (rows * cols * item) // DMA_PIECE_BYTES)
    n = 1
    if rows % unit == 0:
        n = max(d for d in range(1, rows // unit + 1) if (rows // unit) % d == 0 and d <= want)
    size = rows // n
    out = []
    for lead in (range(shape[0]) if len(shape) == 3 else [None]):
        for i in range(n):
            sel = (pl.ds(i * size, size), slice(None)) if lead is None else (lead, pl.ds(i * size, size), slice(None))
            out.append((src.at[sel], dst.at[sel]))
    return out


def _send(src, dst, send_sem, recv_sem, peer, start=True):
    if start:
        for s, d in _pieces(src, dst):
            pltpu.make_async_remote_copy(src_ref=s, dst_ref=d, send_sem=send_sem, recv_sem=recv_sem,
                                         device_id=peer, device_id_type=MESH).start()
    return pltpu.make_async_remote_copy(src_ref=src, dst_ref=dst, send_sem=send_sem, recv_sem=recv_sem,
                                        device_id=peer, device_id_type=MESH)


def run_job(job, *, name):
    n_in, n_out = len(job.arrays), len(job.out_shapes)

    def body(*refs):
        job.fn(refs[:n_in], refs[n_in:n_in + n_out], refs[n_in + n_out:], "both")

    return pl.pallas_call(body, name=name, in_specs=[ANY] * n_in, out_specs=[ANY] * n_out,
                          out_shape=job.out_shapes, scratch_shapes=job.sems)(*job.arrays)


def gather_chips(shards, *, name):
    n = len(shards)

    def body(*refs):
        ins, outs = refs[:n], refs[n:2 * n]
        send_sems, recv_sems = refs[2 * n:]
        x, y, c, chips = _place()
        k = 2 * x + y
        copies = []
        for a in range(n):
            for j, peer in enumerate([(px, py, c) for px, py in chips] + [(x, y, 1 - c)]):
                copies.append(_send(ins[a], outs[a].at[k], send_sems.at[a, j], recv_sems.at[a, j], peer))
        for cp in copies:
            cp.wait()

    return pl.pallas_call(
        body, name=name, in_specs=[ANY] * n, out_specs=[ANY] * n,
        out_shape=[jax.ShapeDtypeStruct((N_CHIPS,) + s.shape, s.dtype) for s in shards],
        scratch_shapes=[pltpu.SemaphoreType.DMA((n, 4)), pltpu.SemaphoreType.DMA((n, 4))])(*shards)


def gather_by_halves_job(shards):
    n = len(shards)

    def fn(ins, outs, sems, phase):
        send_sems, recv_sems = sems
        x, y, c, chips = _place()
        k = 2 * x + y
        sibling = (x, y, 1 - c)
        begin = phase != "finish"
        waits, arrivals = [], []
        for a in range(n):
            half = ins[a].shape[0] // 2
            waits.append(_send(ins[a], outs[a].at[k], send_sems.at[a, 6], recv_sems.at[a, 6], sibling, begin))
            mine = pl.ds(c * half, half)
            for j, (px, py) in enumerate(chips):
                arrivals.append(_send(ins[a].at[mine, :], outs[a].at[k, mine, :], send_sems.at[a, j],
                                      recv_sems.at[a, j], (px, py, c), begin))
        if phase == "start":
            return
        i = 0
        for a in range(n):
            half = ins[a].shape[0] // 2
            mine = pl.ds(c * half, half)
            for j, (px, py) in enumerate(chips):
                arrivals[i].wait_recv()
                landed = outs[a].at[2 * px + py, mine, :]
                waits.append(_send(landed, landed, send_sems.at[a, 3 + j], recv_sems.at[a, 3 + j], sibling))
                i += 1
        for cp in arrivals:
            cp.wait_send()
        for cp in waits:
            cp.wait()

    return Job(shards, [jax.ShapeDtypeStruct((N_CHIPS,) + s.shape, s.dtype) for s in shards],
               [pltpu.SemaphoreType.DMA((n, 7)), pltpu.SemaphoreType.DMA((n, 7))], fn)


def swap_halves(grads, *, name):
    n = len(grads)

    def body(*refs):
        ins, outs = refs[:n], refs[n:2 * n]
        send_sems, recv_sems = refs[2 * n:]
        x, y, c, _ = _place()
        copies = []
        for a in range(n):
            half = ins[a].shape[1] // 2
            copies.append(_send(ins[a].at[:, pl.ds((1 - c) * half, half), :], outs[a], send_sems.at[a],
                                recv_sems.at[a], (x, y, 1 - c)))
        for cp in copies:
            cp.wait()

    return pl.pallas_call(
        body, name=name, in_specs=[ANY] * n, out_specs=[ANY] * n,
        out_shape=[jax.ShapeDtypeStruct((g.shape[0], g.shape[1] // 2, g.shape[2]), g.dtype) for g in grads],
        scratch_shapes=[pltpu.SemaphoreType.DMA((n,)), pltpu.SemaphoreType.DMA((n,))])(*grads)


def _chip_slot(s, k):
    rel = s ^ k
    return (rel >> 1) | ((rel & 1) << 1)


def sum_cores(g, theirs, ck, out_dtype, *, name):
    n, r, cols = g.shape
    half = r // 2
    tr = _tile(half, max(16, (1 << 19) // cols), 16)
    nb = half // tr

    def body(ck_ref, g_ref, t_ref, o_ref):
        o_ref[...] = (g_ref[...] + t_ref[...]).astype(o_ref.dtype)

    return pl.pallas_call(
        body, name=name,
        grid_spec=pltpu.PrefetchScalarGridSpec(
            num_scalar_prefetch=1, grid=(n, nb),
            in_specs=[pl.BlockSpec((None, tr, cols), lambda s, i, ck_ref: (s, ck_ref[0] * nb + i, 0)),
                      pl.BlockSpec((None, tr, cols), lambda s, i, ck_ref: (s, i, 0))],
            out_specs=pl.BlockSpec((None, tr, cols), lambda s, i, ck_ref: (_chip_slot(s, ck_ref[1]), i, 0))),
        out_shape=jax.ShapeDtypeStruct((n, half, cols), out_dtype),
        compiler_params=_params("parallel", "parallel"))(ck, g, theirs)


def scatter_job(parts):
    n = len(parts)

    def fn(ins, outs, sems, phase):
        send_sems, recv_sems = sems
        x, y, c, chips = _place()
        copies = []
        for a in range(n):
            for j, (px, py) in enumerate(chips):
                copies.append(_send(ins[a].at[1 + j], outs[a].at[j], send_sems.at[a, j], recv_sems.at[a, j],
                                    (px, py, c), phase != "finish"))
        if phase != "start":
            for cp in copies:
                cp.wait()

    return Job(parts, [jax.ShapeDtypeStruct((3,) + p.shape[1:], p.dtype) for p in parts],
               [pltpu.SemaphoreType.DMA((n, 3)), pltpu.SemaphoreType.DMA((n, 3))], fn)


def sum_chips(own, others, ck, *, name):
    _, r, cols = own.shape
    tr = _tile(r, max(16, (1 << 19) // cols), 16)

    def body(ck_ref, a_ref, b0_ref, b1_ref, b2_ref, o_ref):
        o_ref[...] = (a_ref[...].astype(F32) + b0_ref[...].astype(F32) + b1_ref[...].astype(F32)
                      + b2_ref[...].astype(F32))

    other = lambda j: pl.BlockSpec((None, tr, cols), lambda i, ck_ref: (j, i, 0))
    return pl.pallas_call(
        body, name=name,
        grid_spec=pltpu.PrefetchScalarGridSpec(
            num_scalar_prefetch=1, grid=(r // tr,),
            in_specs=[pl.BlockSpec((None, tr, cols), lambda i, ck_ref: (0, i, 0)), other(0), other(1), other(2)],
            out_specs=pl.BlockSpec((None, tr, cols), lambda i, ck_ref: (ck_ref[0], i, 0))),
        out_shape=jax.ShapeDtypeStruct((2, r, cols), F32),
        compiler_params=_params("parallel"))(ck, own, others, others, others)


def share_halves(joined, *, name):
    n = len(joined)

    def body(*refs):
        outs = refs[n:2 * n]
        send_sems, recv_sems = refs[2 * n:]
        x, y, c, _ = _place()
        copies = [_send(outs[a].at[c], outs[a].at[c], send_sems.at[a], recv_sems.at[a], (x, y, 1 - c))
                  for a in range(n)]
        for cp in copies:
            cp.wait()

    return pl.pallas_call(
        body, name=name, in_specs=[ANY] * n, out_specs=[ANY] * n,
        out_shape=[jax.ShapeDtypeStruct(j.shape, j.dtype) for j in joined],
        input_output_aliases={a: a for a in range(n)},
        scratch_shapes=[pltpu.SemaphoreType.DMA((n,)), pltpu.SemaphoreType.DMA((n,))])(*joined)


def _pack(arrs, rows_mult):
    flat = jnp.concatenate([a.reshape(-1) for a in arrs])
    rows = -(-flat.shape[0] // (128 * rows_mult)) * rows_mult
    return jnp.pad(flat, (0, rows * 128 - flat.shape[0])).reshape(rows, 128)


def _unpack(packed, shapes):
    flat = packed.reshape(-1)
    out, pos = [], 0
    for s in shapes:
        size = 1
        for d in s:
            size *= d
        out.append(flat[pos:pos + size].reshape(s))
        pos += size
    return out


def _column_shard(full, k, axis):
    width = full.shape[axis] // N_CHIPS
    return lax.dynamic_slice_in_dim(full, k * width, width, axis)


def kernel(x, meta_tokens, norm0_g, l0_w_in, l0_lam_re, l0_lam_im, l0_log_dt, l0_b_re, l0_b_im, l0_c_re, l0_c_im, l0_d_skip, l0_w_glu, l0_b_glu, l0_w_out, norm1_g, l1_w_in, l1_conv_w, l1_conv_b, l1_w_out, norm2_g, l2_w_in, l2_w_grp, l2_b_grp, l2_scale, l2_w_out, norm3_g, l3_w_in, l3_lam_re, l3_lam_im, l3_log_dt, l3_b_re, l3_b_im, l3_c_re, l3_c_im, l3_d_skip, l3_w_glu, l3_b_glu, l3_w_out, final_g, loss_target, m_meta_tokens, m_norm0_g, m_l0_w_in, m_l0_lam_re, m_l0_lam_im, m_l0_log_dt, m_l0_b_re, m_l0_b_im, m_l0_c_re, m_l0_c_im, m_l0_d_skip, m_l0_w_glu, m_l0_b_glu, m_l0_w_out, m_norm1_g, m_l1_w_in, m_l1_conv_w, m_l1_conv_b, m_l1_w_out, m_norm2_g, m_l2_w_in, m_l2_w_grp, m_l2_b_grp, m_l2_scale, m_l2_w_out, m_norm3_g, m_l3_w_in, m_l3_lam_re, m_l3_lam_im, m_l3_log_dt, m_l3_b_re, m_l3_b_im, m_l3_c_re, m_l3_c_im, m_l3_d_skip, m_l3_w_glu, m_l3_b_glu, m_l3_w_out, m_final_g, v_meta_tokens, v_norm0_g, v_l0_w_in, v_l0_lam_re, v_l0_lam_im, v_l0_log_dt, v_l0_b_re, v_l0_b_im, v_l0_c_re, v_l0_c_im, v_l0_d_skip, v_l0_w_glu, v_l0_b_glu, v_l0_w_out, v_norm1_g, v_l1_w_in, v_l1_conv_w, v_l1_conv_b, v_l1_w_out, v_norm2_g, v_l2_w_in, v_l2_w_grp, v_l2_b_grp, v_l2_scale, v_l2_w_out, v_norm3_g, v_l3_w_in, v_l3_lam_re, v_l3_lam_im, v_l3_log_dt, v_l3_b_re, v_l3_b_im, v_l3_c_re, v_l3_c_im, v_l3_d_skip, v_l3_w_glu, v_l3_b_glu, v_l3_w_out, v_final_g):
    given = dict(locals())
    w = {n: given[n] for n in WEIGHTS}
    mom_m = {n: given["m_" + n] for n in WEIGHTS}
    mom_v = {n: given["v_" + n] for n in WEIGHTS}
    seq = x.shape[1]
    real = N_META + seq
    lp = -(-real // SCAN_ROWS) * SCAN_ROWS
    chip = 2 * lax.axis_index("x") + lax.axis_index("y")
    row = lambda a: a.reshape(1, -1)

    shard_bf16 = {n: (w[n].reshape(-1, w[n].shape[-1]) if n == 'l2_w_grp' else w[n]).astype(BF16) for n in BIG}
    layer = lambda i: [n for n in BIG if n.startswith("l%d_" % i)]
    gather_layer = lambda i: gather_by_halves_job([shard_bf16[n] for n in layer(i)])
    full = {}

    def landed(i, arrays):
        for n, arr in zip(layer(i), arrays, strict=True):
            if n.endswith('w_in'):
                full[n] = arr
            elif n == 'l2_w_grp':
                full[n] = arr.reshape(N_CHIPS, len(POOL_WINDOWS), POOL_GROUP // N_CHIPS, POOL_GROUP)
            else:
                full[n] = arr.reshape(1, -1, arr.shape[-1])

    landed(0, run_job(gather_layer(0), name="gather_l0"))
    full.update(zip(SMALL_SHARDED, gather_chips([w[n] for n in SMALL_SHARDED], name="gather_small_shards"),
                    strict=True))
    meta_full = full['meta_tokens'].transpose(1, 0, 2).reshape(N_META, D_MODEL)
    conv_w_full = full['l1_conv_w'].transpose(1, 0, 2).reshape(3, CONV_E)
    b_grp_full = full['l2_b_grp'].transpose(1, 0, 2).reshape(len(POOL_WINDOWS), POOL_GROUP)

    h0 = jnp.concatenate([meta_full, x[0], jnp.zeros((lp - real, D_MODEL), F32)], axis=0)
    target = jnp.concatenate([jnp.zeros((N_META, D_MODEL), F32), loss_target[0],
                              jnp.zeros((lp - real, D_MODEL), F32)], axis=0)
    grads = {}
    saved = {}

    def s5_layer_fwd(i, h, gather=None):
        pre = "l%d_" % i
        tables, tables_vjp = jax.vjp(s5_tables, *[w[pre + s] for s in
                                                   ('lam_re', 'lam_im', 'log_dt', 'b_re', 'b_im', 'c_re', 'c_im')])
        ar, ai, bblk, cblk = tables
        n = rms_fwd(h, row(w["norm%d_g" % i]), name=pre + "norm")
        uz = mm_nn(n, full[pre + 'w_in'], name=pre + "in")
        res = s5_fwd(uz, bblk, cblk, ar, ai, row(w[pre + 'd_skip']), name=pre + "scan",
                     job=None if gather is None else gather_layer(gather))
        if gather is not None:
            res, arrived = res
            landed(gather, arrived)
        y0, sr, si = res
        (yg,), _ = rowwise(lambda yv: ([jax.nn.gelu(yv)], []), [(y0, D_MODEL, 0)], [],
                           [(D_MODEL, BF16, D_MODEL, 0)], name=pre + "gelu")
        q = mm_nn(yg, full[pre + 'w_glu'], bias=row(w[pre + 'b_glu']), name=pre + "glu")
        (y3,), _ = rowwise(lambda yv, qv, zv: ([_s5_gate(yv, qv, zv)], []),
                           [(y0, D_MODEL, 0), (q, D_MODEL, 0), (uz, D_MODEL, 1)], [],
                           [(D_MODEL, BF16, D_MODEL, 0)], name=pre + "gate")
        h_new = mm_nn(y3, full[pre + 'w_out'], add=h, name=pre + "out")
        saved[i] = dict(h=h, n=n, uz=uz, y0=y0, sr=sr, si=si, yg=yg, q=q, y3=y3, tables=tables,
                        tables_vjp=tables_vjp)
        return h_new

    def s5_layer_bwd(i, dh, scatter=None):
        pre = "l%d_" % i
        s = saved[i]
        ar, ai, bblk, cblk = s['tables']
        grads[pre + 'w_out'] = mm_tn(s['y3'], dh, 1, name=pre + "d_w_out")[0]
        dy3 = mm_nt(dh, full[pre + 'w_out'], name=pre + "d_y3")

        def gate_bwd(dyv, yv, qv, zv):
            _, vjp = jax.vjp(_s5_gate, yv, qv, zv)
            dy0, dq, dz = vjp(dyv)
            return [dy0, dq, dz], [_colsum(dq)]

        (dy0_direct, dq, dp), (db_glu,) = rowwise(
            gate_bwd, [(dy3, D_MODEL, 0), (s['y0'], D_MODEL, 0), (s['q'], D_MODEL, 0), (s['uz'], D_MODEL, 1)], [],
            [(D_MODEL, F32, D_MODEL, 0), (D_MODEL, F32, D_MODEL, 0), (2 * D_MODEL, F32, D_MODEL, 1)], [D_MODEL],
            name=pre + "d_gate")
        grads[pre + 'b_glu'] = db_glu.reshape(-1)
        grads[pre + 'w_glu'] = mm_tn(s['yg'], dq, 1, name=pre + "d_w_glu")[0]
        r = mm_nt(dq, full[pre + 'w_glu'], name=pre + "d_yg")

        def gelu_bwd(dv, rv, yv):
            _, vjp = jax.vjp(jax.nn.gelu, yv)
            return [dv + vjp(rv)[0]], []

        (dy0,), _ = rowwise(gelu_bwd, [(dy0_direct, D_MODEL, 0), (r, D_MODEL, 0), (s['y0'], D_MODEL, 0)], [],
                            [(D_MODEL, F32, D_MODEL, 0)], name=pre + "d_gelu")
        res = s5_bwd(dy0, s['uz'], s['sr'], s['si'], bblk, cblk, ar, ai, row(w[pre + 'd_skip']), dp,
                     name=pre + "d_scan", job=None if scatter is None else scatter_job(scatter_of(scatter)))
        if scatter is not None:
            res, arrived = res
            scattered(scatter, arrived)
        dp, dbblk, dcblk, dar, dai, dd = res
        grads[pre + 'd_skip'] = dd.reshape(-1)
        for key, val in zip(('lam_re', 'lam_im', 'log_dt', 'b_re', 'b_im', 'c_re', 'c_im'),
                            s['tables_vjp']((dar, dai, dbblk, dcblk)), strict=True):
            grads[pre + key] = val
        grads[pre + 'w_in'] = mm_tn(s['n'], dp, N_CHIPS, name=pre + "d_w_in")
        dn = mm_nt(dp, full[pre + 'w_in'], name=pre + "d_n")
        dh_in, dg = rms_bwd(s['h'], row(w["norm%d_g" % i]), dn, dh, name=pre + "d_norm")
        grads["norm%d_g" % i] = dg.reshape(-1)
        return dh_in

    h1 = s5_layer_fwd(0, h0, gather=1)

    n1 = rms_fwd(h1, row(w['norm1_g']), name="l1_norm")
    p1, arrived = mm_nn(n1, full['l1_w_in'], name="l1_in", job=gather_layer(2))
    landed(2, arrived)
    y3_1 = conv_fwd(p1, conv_w_full, row(w['l1_conv_b']), name="l1_conv")
    h2 = mm_nn(y3_1, full['l1_w_out'], add=h1, name="l1_out")

    n2 = rms_fwd(h2, row(w['norm2_g']), name="l2_norm")
    p2, arrived = mm_nn(n2, full['l2_w_in'], name="l2_in", job=gather_layer(3))
    landed(3, arrived)
    mixed = pool_fwd(p2, name="l2_pool")
    o2 = grp_nn(mixed, full['l2_w_grp'], b_grp_full.reshape(1, -1), name="l2_grp")
    (y3_2,), _ = rowwise(lambda ov, zv, sv: ([_pool_gate(ov, zv, sv)], []),
                         [(o2, POOL_E, 0), (p2, POOL_E, 1)], [row(w['l2_scale'])],
                         [(POOL_E, BF16, POOL_E, 0)], name="l2_gate")
    h3 = mm_nn(y3_2, full['l2_w_out'], add=h2, name="l2_out")

    h4 = s5_layer_fwd(3, h3)

    def loss_fn(hv, tv, gv):
        i = pl.program_id(0)
        tr = hv.shape[0]
        yf, vjp = jax.vjp(_rms, hv, gv)
        pos = i * tr + lax.broadcasted_iota(jnp.int32, (tr, 1), 0)
        diff = jnp.where((pos >= N_META) & (pos < real), yf - tv, 0.0)
        dh, dg = vjp(diff / D_MODEL)
        return [dh], [dg, _colsum(diff * diff)]

    (dh,), (d_final_g, sq) = rowwise(loss_fn, [(h4, D_MODEL, 0), (target, D_MODEL, 0)], [row(w['final_g'])],
                                     [(D_MODEL, F32, D_MODEL, 0)], [D_MODEL, D_MODEL], name="loss")
    grads['final_g'] = d_final_g.reshape(-1)
    loss = lax.psum(0.5 / D_MODEL * jnp.sum(sq), ("x", "y", "c"))

    ck = jnp.stack([lax.axis_index("c"), chip]).astype(jnp.int32)
    pair, from_chips = {}, {}

    def chip_major(n):
        g = grads[n]
        if n.endswith('w_in'):
            return g
        if n == 'l2_w_grp':
            return g.reshape(N_CHIPS, -1, POOL_GROUP)
        return g.reshape(N_CHIPS, -1, g.shape[-1])

    def reduce_cores(i, small=None):
        names, pieces, wire = layer(i), [chip_major(n) for n in layer(i)], [BF16] * len(layer(i))
        if small is not None:
            names, pieces, wire = names + ['small'], pieces + [small], wire + [F32]
        theirs = swap_halves(pieces, name="reduce_cores_l%d" % i)
        for n, g, t, dt in zip(names, pieces, theirs, wire, strict=True):
            pair[n] = sum_cores(g, t, ck, dt, name="sum_cores_" + n)

    scatter_of = lambda i: [pair[n] for n in layer(i)]

    def scattered(i, arrays):
        from_chips.update(zip(layer(i), arrays, strict=True))

    dh = s5_layer_bwd(3, dh)
    reduce_cores(3)

    grads['l2_w_out'] = mm_tn(y3_2, dh, 1, name="l2_d_w_out")[0]
    dy3 = mm_nt(dh, full['l2_w_out'], name="l2_d_y3")

    def pool_gate_bwd(dyv, ov, zv, sv):
        _, vjp = jax.vjp(_pool_gate, ov, zv, sv)
        do, dz, dscale = vjp(dyv)
        return [do, dz], [dscale, _colsum(do)]

    (do2, dp2), (dscale, dbgrp) = rowwise(
        pool_gate_bwd, [(dy3, POOL_E, 0), (o2, POOL_E, 0), (p2, POOL_E, 1)], [row(w['l2_scale'])],
        [(POOL_E, F32, POOL_E, 0), (2 * POOL_E, F32, POOL_E, 1)], [POOL_E, POOL_E], name="l2_d_gate")
    grads['l2_scale'] = dscale.reshape(-1)
    grads['l2_b_grp'] = dbgrp.reshape(len(POOL_WINDOWS), POOL_GROUP)
    grads['l2_w_grp'] = grp_tn(mixed, do2, name="l2_d_w_grp")
    dmix = grp_nt(do2, full['l2_w_grp'], name="l2_d_mixed")
    dp2 = pool_bwd(dmix, dp2, name="l2_d_pool")
    grads['l2_w_in'], arrived = mm_tn(n2, dp2, N_CHIPS, name="l2_d_w_in", job=scatter_job(scatter_of(3)))
    scattered(3, arrived)
    dn = mm_nt(dp2, full['l2_w_in'], name="l2_d_n")
    dh, dg = rms_bwd(h2, row(w['norm2_g']), dn, dh, name="l2_d_norm")
    grads['norm2_g'] = dg.reshape(-1)
    reduce_cores(2)

    grads['l1_w_out'] = mm_tn(y3_1, dh, 1, name="l1_d_w_out")[0]
    dy3 = mm_nt(dh, full['l1_w_out'], name="l1_d_y3")
    dp1, dcw, dcb = conv_bwd(dy3, p1, conv_w_full, row(w['l1_conv_b']), name="l1_d_conv")
    grads['l1_conv_w'] = dcw
    grads['l1_conv_b'] = dcb.reshape(-1)
    grads['l1_w_in'], arrived = mm_tn(n1, dp1, N_CHIPS, name="l1_d_w_in", job=scatter_job(scatter_of(2)))
    scattered(2, arrived)
    dn = mm_nt(dp1, full['l1_w_in'], name="l1_d_n")
    dh, dg = rms_bwd(h1, row(w['norm1_g']), dn, dh, name="l1_d_norm")
    grads['norm1_g'] = dg.reshape(-1)
    reduce_cores(1)

    dh = s5_layer_bwd(0, dh, scatter=1)
    grad_x = dh[N_META:real][None]
    grads['meta_tokens'] = dh[:N_META]

    small_local = _pack([grads[n] for n in SMALL], 32 * N_CHIPS)
    reduce_cores(0, small_local.reshape(N_CHIPS, -1, 128))
    rest = layer(0) + ['small']
    from_chips.update(zip(rest, run_job(scatter_job([pair[n] for n in rest]), name="reduce_chips_l0"), strict=True))
    joined = [sum_chips(pair[n], from_chips[n], ck, name="sum_chips_" + n) for n in BIG + ['small']]
    joined = share_halves(joined, name="share_cores")
    reduced = [j.reshape(-1, j.shape[-1]) for j in joined]
    g_big = {n: reduced[i].reshape(w[n].shape) for i, n in enumerate(BIG)}
    (small_all,) = gather_chips([reduced[-1]], name="gather_small")
    g_small_full = dict(zip(SMALL, _unpack(small_all, [grads[n].shape for n in SMALL]), strict=True))
    g_small = {}
    for n in SMALL:
        g = g_small_full[n]
        g_small[n] = _column_shard(g, chip, g.ndim - 1) if n in SMALL_SHARDED else g

    out_g, out_d, out_m, out_v = {}, {}, {}, {}
    for n in BIG:
        shape = w[n].shape
        as2d = lambda a: a.reshape(-1, shape[-1])
        d, nm, nv = adamw(as2d(w[n]), as2d(g_big[n]), as2d(mom_m[n]), as2d(mom_v[n]), name="adamw_" + n)
        out_g[n], out_d[n], out_m[n], out_v[n] = g_big[n], d.reshape(shape), nm.reshape(shape), nv.reshape(shape)
    packs = [_pack([src[n] for n in SMALL], 8) for src in (w, g_small, mom_m, mom_v)]
    d, nm, nv = adamw(*packs, name="adamw_small")
    shapes = [w[n].shape for n in SMALL]
    for n, dv, mv, vv in zip(SMALL, _unpack(d, shapes), _unpack(nm, shapes), _unpack(nv, shapes), strict=True):
        out_g[n], out_d[n], out_m[n], out_v[n] = g_small[n], dv, mv, vv

    return (loss, grad_x, *[out_g[n] for n in WEIGHTS], *[out_d[n] for n in WEIGHTS],
            *[out_m[n] for n in WEIGHTS], *[out_v[n] for n in WEIGHTS])
```

```python
import functools

import jax
import jax.numpy as jnp
from jax import lax
from jax.experimental import pallas as pl
from jax.experimental.pallas import tpu as pltpu

F32, BF16 = jnp.float32, jnp.bfloat16
MESH = pl.DeviceIdType.MESH

D_MODEL = 1024
N_META = 16
EPS = 1e-6
S5_GROUPS, S5_GROUP, S5_STATE = 64, 16, 64
LANE_BLOCK = 128
S5_BLOCKS = D_MODEL // LANE_BLOCK
GROUPS_PER_BLOCK = LANE_BLOCK // S5_GROUP
STATE_BLOCK = GROUPS_PER_BLOCK * S5_STATE
SCAN_ROWS = 256
SUBLANES = 8
CONV_E = 2048
POOL_E = 2048
POOL_WINDOWS = (2, 4, 8, 16)
POOL_GROUP = 512
POOL_HALO = 16
N_CHIPS = 4

ADAM_LR, ADAM_B1, ADAM_B2, ADAM_EPS, ADAM_WD, ADAM_STEP = 0.001, 0.9, 0.999, 1e-08, 0.01, 10

VMEM_LIMIT = 48 * 1024 * 1024

WEIGHTS = ['meta_tokens', 'norm0_g', 'l0_w_in', 'l0_lam_re', 'l0_lam_im', 'l0_log_dt', 'l0_b_re', 'l0_b_im',
           'l0_c_re', 'l0_c_im', 'l0_d_skip', 'l0_w_glu', 'l0_b_glu', 'l0_w_out', 'norm1_g', 'l1_w_in',
           'l1_conv_w', 'l1_conv_b', 'l1_w_out', 'norm2_g', 'l2_w_in', 'l2_w_grp', 'l2_b_grp', 'l2_scale',
           'l2_w_out', 'norm3_g', 'l3_w_in', 'l3_lam_re', 'l3_lam_im', 'l3_log_dt', 'l3_b_re', 'l3_b_im',
           'l3_c_re', 'l3_c_im', 'l3_d_skip', 'l3_w_glu', 'l3_b_glu', 'l3_w_out', 'final_g']
BIG = ['l0_w_in', 'l0_w_glu', 'l0_w_out', 'l1_w_in', 'l1_w_out', 'l2_w_in', 'l2_w_grp', 'l2_w_out',
       'l3_w_in', 'l3_w_glu', 'l3_w_out']
SMALL_SHARDED = ['meta_tokens', 'l1_conv_w', 'l2_b_grp']
SMALL = [n for n in WEIGHTS if n not in BIG]


def _params(*sem):
    return pltpu.CompilerParams(dimension_semantics=sem, vmem_limit_bytes=VMEM_LIMIT)


class Job:
    def __init__(self, arrays, out_shapes, sems, fn):
        self.arrays, self.out_shapes, self.sems, self.fn = list(arrays), list(out_shapes), list(sems), fn


def _call(body, *, name, grid, in_specs, out_specs, out_shape, args, semantics, scratch=(), aliases=None, job=None):
    if job is None:
        return pl.pallas_call(
            body, name=name, grid=grid, in_specs=list(in_specs), out_specs=list(out_specs),
            out_shape=list(out_shape), scratch_shapes=list(scratch), input_output_aliases=aliases or {},
            compiler_params=_params(*semantics))(*args)
    counts = [len(in_specs), len(job.arrays), len(out_specs), len(job.out_shapes), len(scratch), len(job.sems)]

    def hosted(*refs):
        parts, pos = [], 0
        for k in counts:
            parts.append(refs[pos:pos + k])
            pos += k
        ins, job_ins, outs, job_outs, scr, job_sems = parts
        steps = [pl.program_id(d) for d in range(len(grid))]
        first = functools.reduce(jnp.logical_and, [s == 0 for s in steps])
        last = functools.reduce(jnp.logical_and, [s == g - 1 for s, g in zip(steps, grid, strict=True)])

        @pl.when(first)
        def _():
            job.fn(job_ins, job_outs, job_sems, "start")

        body(*ins, *outs, *scr)

        @pl.when(last)
        def _():
            job.fn(job_ins, job_outs, job_sems, "finish")

    any_spec = pl.BlockSpec(memory_space=pl.ANY)
    res = pl.pallas_call(
        hosted, name=name, grid=grid, in_specs=list(in_specs) + [any_spec] * len(job.arrays),
        out_specs=list(out_specs) + [any_spec] * len(job.out_shapes),
        out_shape=list(out_shape) + job.out_shapes, scratch_shapes=list(scratch) + job.sems,
        input_output_aliases=aliases or {},
        compiler_params=_params(*["arbitrary"] * len(grid)))(*args, *job.arrays)
    return res[:len(out_specs)], res[len(out_specs):]


def _tile(n, cap, mult):
    best = None
    for t in range(mult, min(n, cap) + 1, mult):
        if n % t == 0:
            best = t
    assert best is not None, (n, cap, mult)
    return best


def _with_job(res, job):
    return res[0] if job is None else (res[0][0], res[1])


def mm_nn(a, b, *, name, bias=None, add=None, out_dtype=F32, job=None):
    m, k = a.shape
    s, _, ns = b.shape
    n = s * ns
    tm, tn = _tile(m, 1088, 16), _tile(ns, 512, 128)
    per = ns // tn

    def body(*refs):
        a_ref, b_ref = refs[0], refs[1]
        o_ref = refs[-1]
        acc = jnp.dot(a_ref[...].astype(BF16), b_ref[...], preferred_element_type=F32)
        pos = 2
        if bias is not None:
            acc = acc + refs[pos][...]
            pos += 1
        if add is not None:
            acc = acc + refs[pos][...]
        o_ref[...] = acc.astype(o_ref.dtype)

    in_specs = [pl.BlockSpec((tm, k), lambda i, j: (i, 0)),
                pl.BlockSpec((None, k, tn), lambda i, j: (j // per, 0, j % per))]
    args = [a, b]
    if bias is not None:
        in_specs.append(pl.BlockSpec((1, tn), lambda i, j: (0, j)))
        args.append(bias)
    if add is not None:
        in_specs.append(pl.BlockSpec((tm, tn), lambda i, j: (i, j)))
        args.append(add)
    return _with_job(_call(
        body, name=name, grid=(m // tm, n // tn), in_specs=in_specs,
        out_specs=[pl.BlockSpec((tm, tn), lambda i, j: (i, j))],
        out_shape=[jax.ShapeDtypeStruct((m, n), out_dtype)], args=args,
        semantics=("parallel", "parallel"), job=job), job)


def mm_nt(g, w, *, name, job=None):
    m, kc = g.shape
    s, nout, kcs = w.shape
    assert s * kcs == kc
    tm, tno, tk = _tile(m, 1088, 16), _tile(nout, 1024, 128), _tile(kcs, 1024, 128)
    per = kcs // tk

    def body(g_ref, w_ref, o_ref):
        kk = pl.program_id(2)
        part = lax.dot_general(g_ref[...].astype(BF16), w_ref[...], (((1,), (1,)), ((), ())),
                               preferred_element_type=F32)

        @pl.when(kk == 0)
        def _():
            o_ref[...] = part

        @pl.when(kk > 0)
        def _():
            o_ref[...] += part

    return _with_job(_call(
        body, name=name, grid=(m // tm, nout // tno, kc // tk),
        in_specs=[pl.BlockSpec((tm, tk), lambda i, j, kk: (i, kk)),
                  pl.BlockSpec((None, tno, tk), lambda i, j, kk: (kk // per, j, kk % per))],
        out_specs=[pl.BlockSpec((tm, tno), lambda i, j, kk: (i, j))],
        out_shape=[jax.ShapeDtypeStruct((m, nout), F32)], args=(g, w),
        semantics=("parallel", "parallel", "arbitrary"), job=job), job)


def mm_tn(a, g, shards, *, name, job=None):
    m, ka = a.shape
    n = g.shape[1]
    ns = n // shards
    tm, tka, tn = _tile(m, 1088, 16), _tile(ka, 1024, 128), _tile(ns, 1024, 128)
    per = ns // tn

    def body(a_ref, g_ref, o_ref):
        mm = pl.program_id(2)
        part = lax.dot_general(a_ref[...].astype(BF16), g_ref[...].astype(BF16), (((0,), (0,)), ((), ())),
                               preferred_element_type=F32)

        @pl.when(mm == 0)
        def _():
            o_ref[...] = part

        @pl.when(mm > 0)
        def _():
            o_ref[...] += part

    return _with_job(_call(
        body, name=name, grid=(ka // tka, n // tn, m // tm),
        in_specs=[pl.BlockSpec((tm, tka), lambda i, j, mm: (mm, i)),
                  pl.BlockSpec((tm, tn), lambda i, j, mm: (mm, j))],
        out_specs=[pl.BlockSpec((None, tka, tn), lambda i, j, mm: (j // per, i, j % per))],
        out_shape=[jax.ShapeDtypeStruct((shards, ka, ns), F32)], args=(a, g),
        semantics=("parallel", "parallel", "arbitrary"), job=job), job)


def grp_nn(a, w, bias, *, name):
    m = a.shape[0]
    tm = _tile(m, 1088, 16)
    ng = len(POOL_WINDOWS)

    def body(a_ref, w_ref, b_ref, o_ref):
        wj = w_ref[...].reshape(POOL_GROUP, POOL_GROUP)
        o_ref[...] = jnp.dot(a_ref[...].astype(BF16), wj, preferred_element_type=F32) + b_ref[...]

    return pl.pallas_call(
        body, name=name, grid=(m // tm, ng),
        in_specs=[pl.BlockSpec((tm, POOL_GROUP), lambda i, j: (i, j)),
                  pl.BlockSpec((N_CHIPS, None, POOL_GROUP // N_CHIPS, POOL_GROUP), lambda i, j: (0, j, 0, 0)),
                  pl.BlockSpec((1, POOL_GROUP), lambda i, j: (0, j))],
        out_specs=pl.BlockSpec((tm, POOL_GROUP), lambda i, j: (i, j)),
        out_shape=jax.ShapeDtypeStruct((m, ng * POOL_GROUP), F32),
        compiler_params=_params("parallel", "parallel"))(a, w, bias)


def grp_nt(g, w, *, name):
    m = g.shape[0]
    tm = _tile(m, 1088, 16)
    ng = len(POOL_WINDOWS)

    def body(g_ref, w_ref, o_ref):
        wj = w_ref[...].reshape(POOL_GROUP, POOL_GROUP)
        o_ref[...] = lax.dot_general(g_ref[...].astype(BF16), wj, (((1,), (1,)), ((), ())),
                                     preferred_element_type=F32)

    return pl.pallas_call(
        body, name=name, grid=(m // tm, ng),
        in_specs=[pl.BlockSpec((tm, POOL_GROUP), lambda i, j: (i, j)),
                  pl.BlockSpec((N_CHIPS, None, POOL_GROUP // N_CHIPS, POOL_GROUP), lambda i, j: (0, j, 0, 0))],
        out_specs=pl.BlockSpec((tm, POOL_GROUP), lambda i, j: (i, j)),
        out_shape=jax.ShapeDtypeStruct((m, ng * POOL_GROUP), F32),
        compiler_params=_params("parallel", "parallel"))(g, w)


def grp_tn(a, g, *, name):
    m = a.shape[0]
    tm = _tile(m, 1088, 16)
    ng = len(POOL_WINDOWS)
    rows = POOL_GROUP // N_CHIPS

    def body(a_ref, g_ref, o_ref):
        mm = pl.program_id(1)
        part = lax.dot_general(a_ref[...].astype(BF16), g_ref[...].astype(BF16), (((0,), (0,)), ((), ())),
                               preferred_element_type=F32).reshape(N_CHIPS, rows, POOL_GROUP)

        @pl.when(mm == 0)
        def _():
            o_ref[...] = part

        @pl.when(mm > 0)
        def _():
            o_ref[...] += part

    return pl.pallas_call(
        body, name=name, grid=(ng, m // tm),
        in_specs=[pl.BlockSpec((tm, POOL_GROUP), lambda j, mm: (mm, j)),
                  pl.BlockSpec((tm, POOL_GROUP), lambda j, mm: (mm, j))],
        out_specs=pl.BlockSpec((N_CHIPS, None, rows, POOL_GROUP), lambda j, mm: (0, j, 0, 0)),
        out_shape=jax.ShapeDtypeStruct((N_CHIPS, ng, rows, POOL_GROUP), F32),
        compiler_params=_params("parallel", "arbitrary"))(a, g)


def rowwise(fn, rows, bcast, outs, accs=(), *, name, aliases=None):
    m = rows[0][0].shape[0]
    tr = m // 16
    n_in = len(rows) + len(bcast)

    def body(*refs):
        vals = [r[...] for r in refs[:n_in]]
        o_vals, a_vals = fn(*vals)
        for ref, val in zip(refs[n_in:n_in + len(outs)], o_vals, strict=True):
            ref[...] = val.astype(ref.dtype)
        step = pl.program_id(0)
        for ref, val in zip(refs[n_in + len(outs):], a_vals, strict=True):
            @pl.when(step == 0)
            def _(ref=ref, val=val):
                ref[...] = val

            @pl.when(step > 0)
            def _(ref=ref, val=val):
                ref[...] += val

    in_specs = [pl.BlockSpec((tr, w), functools.partial(lambda i, cb: (i, cb), cb=cb)) for _, w, cb in rows]
    in_specs += [pl.BlockSpec(b.shape, lambda i: (0, 0)) for b in bcast]
    out_specs = [pl.BlockSpec((tr, w), functools.partial(lambda i, cb: (i, cb), cb=cb)) for _, _, w, cb in outs]
    out_specs += [pl.BlockSpec((1, w), lambda i: (0, 0)) for w in accs]
    out_shape = [jax.ShapeDtypeStruct((m, cols), dt) for cols, dt, _, _ in outs]
    out_shape += [jax.ShapeDtypeStruct((1, w), F32) for w in accs]
    res = pl.pallas_call(
        body, name=name, grid=(m // tr,), in_specs=in_specs, out_specs=out_specs, out_shape=out_shape,
        input_output_aliases=aliases or {},
        compiler_params=_params("arbitrary"))(*[r[0] for r in rows], *bcast)
    return res[:len(outs)], res[len(outs):]


def _rms(h, g):
    return h * lax.rsqrt(jnp.mean(h * h, axis=-1, keepdims=True) + EPS) * g


def _colsum(v):
    return jnp.sum(v, axis=0, keepdims=True)


def rms_fwd(h, g, *, name):
    (n,), _ = rowwise(lambda hv, gv: ([_rms(hv, gv)], []), [(h, D_MODEL, 0)], [g],
                      [(D_MODEL, BF16, D_MODEL, 0)], name=name)
    return n


def rms_bwd(h, g, dn, dh_out, *, name):
    def fn(hv, dnv, dhv, gv):
        _, vjp = jax.vjp(_rms, hv, gv)
        dh, dg = vjp(dnv)
        return [dhv + dh], [dg]

    (dh,), (dg,) = rowwise(fn, [(h, D_MODEL, 0), (dn, D_MODEL, 0), (dh_out, D_MODEL, 0)], [g],
                           [(D_MODEL, F32, D_MODEL, 0)], [D_MODEL], name=name)
    return dh, dg


def _s5_gate(y0, q, z):
    yg = jax.nn.gelu(y0)
    return yg * jax.nn.sigmoid(q) * jax.nn.silu(z)


def _pool_gate(o, z, scale):
    return o * scale * jax.nn.silu(z)


def _shift_rows(v, d, down):
    t = v.shape[0]
    row = lax.broadcasted_iota(jnp.int32, v.shape, 0)
    if down:
        return jnp.where(row >= d, pltpu.roll(v, d, 0), 0.0)
    return jnp.where(row < t - d, pltpu.roll(v, t - d, 0), 0.0)


def _cmul(ar, ai, br, bi):
    return ar * br - ai * bi, ar * bi + ai * br


def _scan(xr, xi, ar, ai, cr, ci, down):
    t, n = xr.shape
    nb = t // SUBLANES
    pw = [(ar, ai)]
    for _ in range(SUBLANES - 1):
        pw.append(_cmul(*pw[-1], ar, ai))
    sub = lax.broadcasted_iota(jnp.int32, (SUBLANES, n), 0)

    def table(exps):
        tr, ti = jnp.zeros((SUBLANES, n), F32), jnp.zeros((SUBLANES, n), F32)
        for r, e in enumerate(exps):
            if e:
                tr, ti = jnp.where(sub == r, pw[e - 1][0], tr), jnp.where(sub == r, pw[e - 1][1], ti)
        return tr[None], ti[None]

    sr, si = xr.reshape(nb, SUBLANES, n), xi.reshape(nb, SUBLANES, n)
    d = 1
    while d < SUBLANES:
        mr, mi = table([d if (r >= d if down else r < SUBLANES - d) else 0 for r in range(SUBLANES)])
        turn = d if down else SUBLANES - d
        hr, hi = pltpu.roll(sr, turn, 1), pltpu.roll(si, turn, 1)
        sr, si = sr + mr * hr - mi * hi, si + mr * hi + mi * hr
        d *= 2
    edge = SUBLANES - 1 if down else 0
    qr, qi = table([r + 1 if down else SUBLANES - r for r in range(SUBLANES)])
    qr, qi = qr[0], qi[0]
    in_r, in_i = jnp.broadcast_to(cr, (SUBLANES, n)), jnp.broadcast_to(ci, (SUBLANES, n))
    out_r, out_i = [None] * nb, [None] * nb
    for b in (range(nb) if down else reversed(range(nb))):
        out_r[b] = sr[b] + qr * in_r - qi * in_i
        out_i[b] = si[b] + qr * in_i + qi * in_r
        in_r = jnp.broadcast_to(out_r[b][edge:edge + 1, :], (SUBLANES, n))
        in_i = jnp.broadcast_to(out_i[b][edge:edge + 1, :], (SUBLANES, n))
    return jnp.concatenate(out_r, axis=0), jnp.concatenate(out_i, axis=0), in_r[0:1, :], in_i[0:1, :]


def s5_fwd(uz, bblk, cblk, ar, ai, dskip, *, name, job=None):
    lp = uz.shape[0]
    t = SCAN_ROWS
    nst = S5_GROUPS * S5_STATE

    def body(u_ref, b_ref, c_ref, ar_ref, ai_ref, d_ref, y_ref, yg_ref, sr_ref, si_ref, cr, ci):
        step = pl.program_id(1)

        @pl.when(step == 0)
        def _():
            cr[...] = jnp.zeros_like(cr)
            ci[...] = jnp.zeros_like(ci)

        u = u_ref[...]
        a_r, a_i = ar_ref[...], ai_ref[...]
        x = jnp.dot(u.astype(BF16), b_ref[...].astype(BF16), preferred_element_type=F32)
        sr, si, cr[...], ci[...] = _scan(x[:, :STATE_BLOCK], x[:, STATE_BLOCK:], a_r, a_i, cr[...], ci[...], True)
        sr_ref[...] = sr
        si_ref[...] = si
        s = jnp.concatenate([sr, si], axis=1).astype(BF16)
        y0 = jnp.dot(s, c_ref[...].astype(BF16), preferred_element_type=F32) + d_ref[...] * u
        y_ref[...] = y0
        yg_ref[...] = jax.nn.gelu(y0).astype(yg_ref.dtype)

    return _call(
        body, name=name, grid=(S5_BLOCKS, lp // t),
        in_specs=[pl.BlockSpec((t, LANE_BLOCK), lambda b, c: (c, b)),
                  pl.BlockSpec((None, LANE_BLOCK, 2 * STATE_BLOCK), lambda b, c: (b, 0, 0)),
                  pl.BlockSpec((None, 2 * STATE_BLOCK, LANE_BLOCK), lambda b, c: (b, 0, 0)),
                  pl.BlockSpec((1, STATE_BLOCK), lambda b, c: (0, b)),
                  pl.BlockSpec((1, STATE_BLOCK), lambda b, c: (0, b)),
                  pl.BlockSpec((1, LANE_BLOCK), lambda b, c: (0, b))],
        out_specs=[pl.BlockSpec((t, LANE_BLOCK), lambda b, c: (c, b)),
                   pl.BlockSpec((t, LANE_BLOCK), lambda b, c: (c, b)),
                   pl.BlockSpec((t, STATE_BLOCK), lambda b, c: (c, b)),
                   pl.BlockSpec((t, STATE_BLOCK), lambda b, c: (c, b))],
        out_shape=[jax.ShapeDtypeStruct((lp, D_MODEL), F32), jax.ShapeDtypeStruct((lp, D_MODEL), BF16),
                   jax.ShapeDtypeStruct((lp, nst), F32), jax.ShapeDtypeStruct((lp, nst), F32)],
        scratch=[pltpu.VMEM((1, STATE_BLOCK), F32), pltpu.VMEM((1, STATE_BLOCK), F32)],
        args=(uz, bblk, cblk, ar, ai, dskip), semantics=("parallel", "arbitrary"), job=job)


def s5_bwd(dy_direct, dyg, y0, uz, sr, si, bblk, cblk, ar, ai, dskip, dp, *, name, job=None):
    lp = uz.shape[0]
    t = SCAN_ROWS
    nch = lp // t
    nst = S5_GROUPS * S5_STATE

    def body(dyd_ref, dyg_ref, y0_ref, u_ref, sr_ref, si_ref, b_ref, c_ref, ar_ref, ai_ref, d_ref, _, du_ref,
             db_ref, dc_ref, dar_ref, dai_ref, dd_ref, cr, ci):
        step = pl.program_id(1)

        @pl.when(step == 0)
        def _():
            cr[...] = jnp.zeros_like(cr)
            ci[...] = jnp.zeros_like(ci)

        dy = dyd_ref[...] + jax.vjp(jax.nn.gelu, y0_ref[...])[1](dyg_ref[...])[0]
        u = u_ref[...]
        dyb, ub = dy.astype(BF16), u.astype(BF16)
        a_r, a_i = ar_ref[...], -ai_ref[...]
        g = lax.dot_general(dyb, c_ref[...].astype(BF16), (((1,), (1,)), ((), ())), preferred_element_type=F32)
        gr, gi = g[:, :STATE_BLOCK], g[:, STATE_BLOCK:]
        nxt_r, nxt_i = cr[...], ci[...]
        last = lax.broadcasted_iota(jnp.int32, gr.shape, 0) == t - 1
        lr, li, cr[...], ci[...] = _scan(gr, gi, a_r, a_i, nxt_r, nxt_i, False)
        nr = _shift_rows(lr, 1, False) + jnp.where(last, nxt_r, 0.0)
        ni = _shift_rows(li, 1, False) + jnp.where(last, nxt_i, 0.0)
        s_r, s_i = sr_ref[...], si_ref[...]
        dar = _colsum(s_r * nr + s_i * ni)
        dai = _colsum(s_r * ni - s_i * nr)
        lam = jnp.concatenate([lr, li], axis=1).astype(BF16)
        sb = jnp.concatenate([s_r, s_i], axis=1).astype(BF16)
        dc = lax.dot_general(sb, dyb, (((0,), (0,)), ((), ())), preferred_element_type=F32)
        db = lax.dot_general(ub, lam, (((0,), (0,)), ((), ())), preferred_element_type=F32)
        du_ref[...] = lax.dot_general(lam, b_ref[...].astype(BF16), (((1,), (1,)), ((), ())),
                                      preferred_element_type=F32) + d_ref[...] * dy
        dd = _colsum(dy * u)

        @pl.when(step == 0)
        def _():
            db_ref[...] = db
            dc_ref[...] = dc
            dar_ref[...] = dar
            dai_ref[...] = dai
            dd_ref[...] = dd

        @pl.when(step > 0)
        def _():
            db_ref[...] += db
            dc_ref[...] += dc
            dar_ref[...] += dar
            dai_ref[...] += dai
            dd_ref[...] += dd

    rev = lambda b, c: (nch - 1 - c, b)
    return _call(
        body, name=name, grid=(S5_BLOCKS, nch),
        in_specs=[pl.BlockSpec((t, LANE_BLOCK), rev), pl.BlockSpec((t, LANE_BLOCK), rev),
                  pl.BlockSpec((t, LANE_BLOCK), rev), pl.BlockSpec((t, LANE_BLOCK), rev),
                  pl.BlockSpec((t, STATE_BLOCK), rev), pl.BlockSpec((t, STATE_BLOCK), rev),
                  pl.BlockSpec((None, LANE_BLOCK, 2 * STATE_BLOCK), lambda b, c: (b, 0, 0)),
                  pl.BlockSpec((None, 2 * STATE_BLOCK, LANE_BLOCK), lambda b, c: (b, 0, 0)),
                  pl.BlockSpec((1, STATE_BLOCK), lambda b, c: (0, b)),
                  pl.BlockSpec((1, STATE_BLOCK), lambda b, c: (0, b)),
                  pl.BlockSpec((1, LANE_BLOCK), lambda b, c: (0, b)),
                  pl.BlockSpec(memory_space=pl.ANY)],
        out_specs=[pl.BlockSpec((t, LANE_BLOCK), rev),
                   pl.BlockSpec((None, LANE_BLOCK, 2 * STATE_BLOCK), lambda b, c: (b, 0, 0)),
                   pl.BlockSpec((None, 2 * STATE_BLOCK, LANE_BLOCK), lambda b, c: (b, 0, 0)),
                   pl.BlockSpec((1, STATE_BLOCK), lambda b, c: (0, b)),
                   pl.BlockSpec((1, STATE_BLOCK), lambda b, c: (0, b)),
                   pl.BlockSpec((1, LANE_BLOCK), lambda b, c: (0, b))],
        out_shape=[jax.ShapeDtypeStruct(dp.shape, F32),
                   jax.ShapeDtypeStruct((S5_BLOCKS, LANE_BLOCK, 2 * STATE_BLOCK), F32),
                   jax.ShapeDtypeStruct((S5_BLOCKS, 2 * STATE_BLOCK, LANE_BLOCK), F32),
                   jax.ShapeDtypeStruct((1, nst), F32), jax.ShapeDtypeStruct((1, nst), F32),
                   jax.ShapeDtypeStruct((1, D_MODEL), F32)],
        scratch=[pltpu.VMEM((1, STATE_BLOCK), F32), pltpu.VMEM((1, STATE_BLOCK), F32)],
        aliases={11: 0}, args=(dy_direct, dyg, y0, uz, sr, si, bblk, cblk, ar, ai, dskip, dp),
        semantics=("parallel", "arbitrary"), job=job)


def s5_tables(lam_re, lam_im, log_dt, b_re, b_im, c_re, c_im):
    dt = jnp.exp(log_dt)[:, None]
    mag = jnp.exp(lam_re * dt)
    ar = mag * jnp.cos(lam_im * dt)
    ai = mag * jnp.sin(lam_im * dt)
    den = lam_re * lam_re + lam_im * lam_im
    kr = ((ar - 1.0) * lam_re + ai * lam_im) / den
    ki = (ai * lam_re - (ar - 1.0) * lam_im) / den
    bbr = kr[..., None] * b_re - ki[..., None] * b_im
    bbi = kr[..., None] * b_im + ki[..., None] * b_re
    eye = jnp.eye(GROUPS_PER_BLOCK, dtype=F32)

    def b_blocks(v):
        v = v.reshape(S5_BLOCKS, GROUPS_PER_BLOCK, S5_STATE, S5_GROUP)
        return jnp.einsum('bgpi,gh->bgihp', v, eye).reshape(S5_BLOCKS, LANE_BLOCK, STATE_BLOCK)

    def c_blocks(v):
        v = v.reshape(S5_BLOCKS, GROUPS_PER_BLOCK, S5_GROUP, S5_STATE)
        return jnp.einsum('bgip,gh->bgphi', v, eye).reshape(S5_BLOCKS, STATE_BLOCK, LANE_BLOCK)

    bblk = jnp.concatenate([b_blocks(bbr), b_blocks(bbi)], axis=2)
    cblk = jnp.concatenate([c_blocks(c_re), -c_blocks(c_im)], axis=1)
    nst = S5_GROUPS * S5_STATE
    return ar.reshape(1, nst), ai.reshape(1, nst), bblk, cblk


def _silu_grad(z):
    s = jax.nn.sigmoid(z)
    return s * (1.0 + z * (1.0 - s))


def conv_fwd(p, conv_w, conv_b, *, name):
    lp = p.shape[0]
    tr = lp // 16
    e = CONV_E
    hb = tr // 8

    def body(p_ref, cgh_ref, vh_ref, w_ref, b_ref, o_ref):
        i = pl.program_id(0)
        bg, cg, v, z = (p_ref[:, k * e:(k + 1) * e] for k in range(4))
        hc = cg * v
        halo = jnp.where(i > 0, cgh_ref[...] * vh_ref[...], 0.0)
        ext = jnp.concatenate([halo, hc], axis=0)
        w = w_ref[...]
        conv = (w[0:1] * pltpu.roll(ext, 2, 0)[8:] + w[1:2] * pltpu.roll(ext, 1, 0)[8:] + w[2:3] * hc
                + b_ref[...])
        o_ref[...] = (bg * conv * jax.nn.silu(z)).astype(o_ref.dtype)

    prev = lambda col: (lambda i: (jnp.maximum(i * hb - 1, 0), col))
    return pl.pallas_call(
        body, name=name, grid=(lp // tr,),
        in_specs=[pl.BlockSpec((tr, 4 * e), lambda i: (i, 0)),
                  pl.BlockSpec((8, e), prev(1)), pl.BlockSpec((8, e), prev(2)),
                  pl.BlockSpec((3, e), lambda i: (0, 0)), pl.BlockSpec((1, e), lambda i: (0, 0))],
        out_specs=pl.BlockSpec((tr, e), lambda i: (i, 0)),
        out_shape=jax.ShapeDtypeStruct((lp, e), BF16),
        compiler_params=_params("parallel"))(p, p, p, conv_w, conv_b)


def conv_bwd(dy3, p, conv_w, conv_b, *, name):
    lp = p.shape[0]
    tr = lp // 32
    e = CONV_E
    hb = tr // 8
    nt = lp // tr

    def body(dy_ref, p_ref, cgh_ref, vh_ref, dyn_ref, bgn_ref, zn_ref, w_ref, b_ref, dp_ref, dw_ref, db_ref):
        i = pl.program_id(0)
        dy = dy_ref[...]
        bg, cg, v, z = (p_ref[:, k * e:(k + 1) * e] for k in range(4))
        w = w_ref[...]
        hc = cg * v
        halo = jnp.where(i > 0, cgh_ref[...] * vh_ref[...], 0.0)
        ext = jnp.concatenate([halo, hc], axis=0)
        hc2, hc1 = pltpu.roll(ext, 2, 0)[8:], pltpu.roll(ext, 1, 0)[8:]
        yc = w[0:1] * hc2 + w[1:2] * hc1 + w[2:3] * hc + b_ref[...]
        sz = jax.nn.silu(z)
        dyc = dy * bg * sz
        dyc_n = jnp.where(i < nt - 1, dyn_ref[...] * bgn_ref[...] * jax.nn.silu(zn_ref[...]), 0.0)
        ext_n = jnp.concatenate([dyc, dyc_n], axis=0)
        n_rows = tr + 8
        dhc = (w[2:3] * dyc + w[1:2] * pltpu.roll(ext_n, n_rows - 1, 0)[:tr]
               + w[0:1] * pltpu.roll(ext_n, n_rows - 2, 0)[:tr])
        dp_ref[:, 0:e] = dy * yc * sz
        dp_ref[:, e:2 * e] = dhc * v
        dp_ref[:, 2 * e:3 * e] = dhc * cg
        dp_ref[:, 3 * e:4 * e] = dy * bg * yc * _silu_grad(z)
        dw = jnp.concatenate([_colsum(dyc * hc2), _colsum(dyc * hc1), _colsum(dyc * hc)], axis=0)
        db = _colsum(dyc)

        @pl.when(i == 0)
        def _():
            dw_ref[...] = dw
            db_ref[...] = db

        @pl.when(i > 0)
        def _():
            dw_ref[...] += dw
            db_ref[...] += db

    prev = lambda col: (lambda i: (jnp.maximum(i * hb - 1, 0), col))
    nxt = lambda col: (lambda i: (jnp.minimum((i + 1) * hb, lp // 8 - 1), col))
    return pl.pallas_call(
        body, name=name, grid=(nt,),
        in_specs=[pl.BlockSpec((tr, e), lambda i: (i, 0)), pl.BlockSpec((tr, 4 * e), lambda i: (i, 0)),
                  pl.BlockSpec((8, e), prev(1)), pl.BlockSpec((8, e), prev(2)),
                  pl.BlockSpec((8, e), nxt(0)), pl.BlockSpec((8, e), nxt(0)), pl.BlockSpec((8, e), nxt(3)),
                  pl.BlockSpec((3, e), lambda i: (0, 0)), pl.BlockSpec((1, e), lambda i: (0, 0))],
        out_specs=[pl.BlockSpec((tr, 4 * e), lambda i: (i, 0)), pl.BlockSpec((3, e), lambda i: (0, 0)),
                   pl.BlockSpec((1, e), lambda i: (0, 0))],
        out_shape=[jax.ShapeDtypeStruct((lp, 4 * e), F32), jax.ShapeDtypeStruct((3, e), F32),
                   jax.ShapeDtypeStruct((1, e), F32)],
        compiler_params=_params("arbitrary"))(dy3, p, p, p, dy3, p, p, conv_w, conv_b)


def _pool_counts(i, tr, rows, offset, w):
    t = (i * tr + offset + 1 + lax.broadcasted_iota(jnp.int32, (rows, 1), 0)).astype(F32)
    return jnp.minimum(t, float(w))


def pool_fwd(p, *, name):
    lp = p.shape[0]
    tr = lp // 16
    h = POOL_HALO
    hb = tr // h

    def body(u_ref, uh_ref, o_ref):
        i = pl.program_id(0)
        u = u_ref[...]
        ext = jnp.concatenate([jnp.where(i > 0, uh_ref[...], 0.0), u], axis=0)
        for k, w in enumerate(POOL_WINDOWS):
            cols = slice(k * POOL_GROUP, (k + 1) * POOL_GROUP)
            s = ext[:, cols]
            d = 1
            while d < w:
                s = s + pltpu.roll(s, d, 0)
                d *= 2
            o_ref[:, cols] = (s[h:] / _pool_counts(i, tr, tr, 0, w) - u[:, cols]).astype(o_ref.dtype)

    return pl.pallas_call(
        body, name=name, grid=(lp // tr,),
        in_specs=[pl.BlockSpec((tr, POOL_E), lambda i: (i, 0)),
                  pl.BlockSpec((h, POOL_E), lambda i: (jnp.maximum(i * hb - 1, 0), 0))],
        out_specs=pl.BlockSpec((tr, POOL_E), lambda i: (i, 0)),
        out_shape=jax.ShapeDtypeStruct((lp, POOL_E), BF16),
        compiler_params=_params("parallel"))(p, p)


def pool_bwd(dmix, dp, *, name):
    lp = dmix.shape[0]
    tr = lp // 16
    h = POOL_HALO
    hb = tr // h
    nt = lp // tr

    def body(dm_ref, dmn_ref, _, du_ref):
        i = pl.program_id(0)
        dm = dm_ref[...]
        dmn = jnp.where(i < nt - 1, dmn_ref[...], 0.0)
        n_rows = tr + h
        for k, w in enumerate(POOL_WINDOWS):
            cols = slice(k * POOL_GROUP, (k + 1) * POOL_GROUP)
            s = jnp.concatenate([dm[:, cols] / _pool_counts(i, tr, tr, 0, w),
                                 dmn[:, cols] / _pool_counts(i, tr, h, tr, w)], axis=0)
            d = 1
            while d < w:
                s = s + pltpu.roll(s, n_rows - d, 0)
                d *= 2
            du_ref[:, cols] = s[:tr] - dm[:, cols]

    return pl.pallas_call(
        body, name=name, grid=(nt,),
        in_specs=[pl.BlockSpec((tr, POOL_E), lambda i: (i, 0)),
                  pl.BlockSpec((h, POOL_E), lambda i: (jnp.minimum((i + 1) * hb, lp // h - 1), 0)),
                  pl.BlockSpec(memory_space=pl.ANY)],
        out_specs=pl.BlockSpec((tr, POOL_E), lambda i: (i, 0)),
        out_shape=jax.ShapeDtypeStruct(dp.shape, F32),
        input_output_aliases={2: 0},
        compiler_params=_params("parallel"))(dmix, dmix, dp)


def adamw(w, g, m, v, *, name):
    r, c = w.shape
    tr = _tile(r, max(8, (1 << 19) // c), 8)

    def body(w_ref, g_ref, m_ref, v_ref, d_ref, nm_ref, nv_ref):
        gv = g_ref[...]
        nm = ADAM_B1 * m_ref[...] + (1.0 - ADAM_B1) * gv
        nv = ADAM_B2 * v_ref[...] + (1.0 - ADAM_B2) * jnp.square(gv)
        m_hat = nm / (1.0 - ADAM_B1 ** ADAM_STEP)
        v_hat = nv / (1.0 - ADAM_B2 ** ADAM_STEP)
        d_ref[...] = -ADAM_LR * (m_hat / (jnp.sqrt(v_hat) + ADAM_EPS) + ADAM_WD * w_ref[...])
        nm_ref[...] = nm
        nv_ref[...] = nv

    spec = pl.BlockSpec((tr, c), lambda i: (i, 0))
    return pl.pallas_call(
        body, name=name, grid=(r // tr,), in_specs=[spec] * 4, out_specs=[spec] * 3,
        out_shape=[jax.ShapeDtypeStruct((r, c), F32)] * 3,
        compiler_params=_params("parallel"))(w, g, m, v)


ANY = pl.BlockSpec(memory_space=pl.ANY)


def _place():
    x, y, c = lax.axis_index("x"), lax.axis_index("y"), lax.axis_index("c")
    return x, y, c, [(1 - x, y), (x, 1 - y), (1 - x, 1 - y)]


DMA_PIECE_BYTES = 512 * 1024


def _pieces(src, dst):
    shape = src.shape
    rows, cols = shape[-2], shape[-1]
    item = jnp.dtype(src.dtype).itemsize
    unit = 32 // item
    want = max(1, (rows * cols * item) // DMA_PIECE_BYTES)
    n = 1
    if rows % unit == 0:
        n = max(d for d in range(1, rows // unit + 1) if (rows // unit) % d == 0 and d <= want)
    size = rows // n
    out = []
    for lead in (range(shape[0]) if len(shape) == 3 else [None]):
        for i in range(n):
            sel = (pl.ds(i * size, size), slice(None)) if lead is None else (lead, pl.ds(i * size, size), slice(None))
            out.append((src.at[sel], dst.at[sel]))
    return out


def _send(src, dst, send_sem, recv_sem, peer, start=True):
    if start:
        for s, d in _pieces(src, dst):
            pltpu.make_async_remote_copy(src_ref=s, dst_ref=d, send_sem=send_sem, recv_sem=recv_sem,
                                         device_id=peer, device_id_type=MESH).start()
    return pltpu.make_async_remote_copy(src_ref=src, dst_ref=dst, send_sem=send_sem, recv_sem=recv_sem,
                                        device_id=peer, device_id_type=MESH)


def run_job(job, *, name):
    n_in, n_out = len(job.arrays), len(job.out_shapes)

    def body(*refs):
        job.fn(refs[:n_in], refs[n_in:n_in + n_out], refs[n_in + n_out:], "both")

    return pl.pallas_call(body, name=name, in_specs=[ANY] * n_in, out_specs=[ANY] * n_out,
                          out_shape=job.out_shapes, scratch_shapes=job.sems)(*job.arrays)


def gather_chips(shards, *, name):
    n = len(shards)

    def body(*refs):
        ins, outs = refs[:n], refs[n:2 * n]
        send_sems, recv_sems = refs[2 * n:]
        x, y, c, chips = _place()
        k = 2 * x + y
        copies = []
        for a in range(n):
            for j, peer in enumerate([(px, py, c) for px, py in chips] + [(x, y, 1 - c)]):
                copies.append(_send(ins[a], outs[a].at[k], send_sems.at[a, j], recv_sems.at[a, j], peer))
        for cp in copies:
            cp.wait()

    return pl.pallas_call(
        body, name=name, in_specs=[ANY] * n, out_specs=[ANY] * n,
        out_shape=[jax.ShapeDtypeStruct((N_CHIPS,) + s.shape, s.dtype) for s in shards],
        scratch_shapes=[pltpu.SemaphoreType.DMA((n, 4)), pltpu.SemaphoreType.DMA((n, 4))])(*shards)


def gather_by_halves_job(shards):
    n = len(shards)

    def fn(ins, outs, sems, phase):
        send_sems, recv_sems = sems
        x, y, c, chips = _place()
        k = 2 * x + y
        sibling = (x, y, 1 - c)
        begin = phase != "finish"
        waits, arrivals = [], []
        for a in range(n):
            half = ins[a].shape[0] // 2
            waits.append(_send(ins[a], outs[a].at[k], send_sems.at[a, 6], recv_sems.at[a, 6], sibling, begin))
            mine = pl.ds(c * half, half)
            for j, (px, py) in enumerate(chips):
                arrivals.append(_send(ins[a].at[mine, :], outs[a].at[k, mine, :], send_sems.at[a, j],
                                      recv_sems.at[a, j], (px, py, c), begin))
        if phase == "start":
            return
        i = 0
        for a in range(n):
            half = ins[a].shape[0] // 2
            mine = pl.ds(c * half, half)
            for j, (px, py) in enumerate(chips):
                arrivals[i].wait_recv()
                landed = outs[a].at[2 * px + py, mine, :]
                waits.append(_send(landed, landed, send_sems.at[a, 3 + j], recv_sems.at[a, 3 + j], sibling))
                i += 1
        for cp in arrivals:
            cp.wait_send()
        for cp in waits:
            cp.wait()

    return Job(shards, [jax.ShapeDtypeStruct((N_CHIPS,) + s.shape, s.dtype) for s in shards],
               [pltpu.SemaphoreType.DMA((n, 7)), pltpu.SemaphoreType.DMA((n, 7))], fn)


def swap_halves(grads, *, name):
    n = len(grads)

    def body(*refs):
        ins, outs = refs[:n], refs[n:2 * n]
        send_sems, recv_sems = refs[2 * n:]
        x, y, c, _ = _place()
        copies = []
        for a in range(n):
            half = ins[a].shape[1] // 2
            copies.append(_send(ins[a].at[:, pl.ds((1 - c) * half, half), :], outs[a], send_sems.at[a],
                                recv_sems.at[a], (x, y, 1 - c)))
        for cp in copies:
            cp.wait()

    return pl.pallas_call(
        body, name=name, in_specs=[ANY] * n, out_specs=[ANY] * n,
        out_shape=[jax.ShapeDtypeStruct((g.shape[0], g.shape[1] // 2, g.shape[2]), g.dtype) for g in grads],
        scratch_shapes=[pltpu.SemaphoreType.DMA((n,)), pltpu.SemaphoreType.DMA((n,))])(*grads)


def _chip_slot(s, k):
    rel = s ^ k
    return (rel >> 1) | ((rel & 1) << 1)


def sum_cores(g, theirs, ck, out_dtype, *, name):
    n, r, cols = g.shape
    half = r // 2
    tr = _tile(half, max(16, (1 << 19) // cols), 16)
    nb = half // tr

    def body(ck_ref, g_ref, t_ref, o_ref):
        o_ref[...] = (g_ref[...] + t_ref[...]).astype(o_ref.dtype)

    return pl.pallas_call(
        body, name=name,
        grid_spec=pltpu.PrefetchScalarGridSpec(
            num_scalar_prefetch=1, grid=(n, nb),
            in_specs=[pl.BlockSpec((None, tr, cols), lambda s, i, ck_ref: (s, ck_ref[0] * nb + i, 0)),
                      pl.BlockSpec((None, tr, cols), lambda s, i, ck_ref: (s, i, 0))],
            out_specs=pl.BlockSpec((None, tr, cols), lambda s, i, ck_ref: (_chip_slot(s, ck_ref[1]), i, 0))),
        out_shape=jax.ShapeDtypeStruct((n, half, cols), out_dtype),
        compiler_params=_params("parallel", "parallel"))(ck, g, theirs)


def scatter_job(parts):
    n = len(parts)

    def fn(ins, outs, sems, phase):
        send_sems, recv_sems = sems
        x, y, c, chips = _place()
        copies = []
        for a in range(n):
            for j, (px, py) in enumerate(chips):
                copies.append(_send(ins[a].at[1 + j], outs[a].at[j], send_sems.at[a, j], recv_sems.at[a, j],
                                    (px, py, c), phase != "finish"))
        if phase != "start":
            for cp in copies:
                cp.wait()

    return Job(parts, [jax.ShapeDtypeStruct((3,) + p.shape[1:], p.dtype) for p in parts],
               [pltpu.SemaphoreType.DMA((n, 3)), pltpu.SemaphoreType.DMA((n, 3))], fn)


def sum_chips(own, others, ck, *, name):
    _, r, cols = own.shape
    tr = _tile(r, max(16, (1 << 19) // cols), 16)

    def body(ck_ref, a_ref, b0_ref, b1_ref, b2_ref, o_ref):
        o_ref[...] = (a_ref[...].astype(F32) + b0_ref[...].astype(F32) + b1_ref[...].astype(F32)
                      + b2_ref[...].astype(F32))

    other = lambda j: pl.BlockSpec((None, tr, cols), lambda i, ck_ref: (j, i, 0))
    return pl.pallas_call(
        body, name=name,
        grid_spec=pltpu.PrefetchScalarGridSpec(
            num_scalar_prefetch=1, grid=(r // tr,),
            in_specs=[pl.BlockSpec((None, tr, cols), lambda i, ck_ref: (0, i, 0)), other(0), other(1), other(2)],
            out_specs=pl.BlockSpec((None, tr, cols), lambda i, ck_ref: (ck_ref[0], i, 0))),
        out_shape=jax.ShapeDtypeStruct((2, r, cols), F32),
        compiler_params=_params("parallel"))(ck, own, others, others, others)


def share_halves(joined, *, name):
    n = len(joined)

    def body(*refs):
        outs = refs[n:2 * n]
        send_sems, recv_sems = refs[2 * n:]
        x, y, c, _ = _place()
        copies = [_send(outs[a].at[c], outs[a].at[c], send_sems.at[a], recv_sems.at[a], (x, y, 1 - c))
                  for a in range(n)]
        for cp in copies:
            cp.wait()

    return pl.pallas_call(
        body, name=name, in_specs=[ANY] * n, out_specs=[ANY] * n,
        out_shape=[jax.ShapeDtypeStruct(j.shape, j.dtype) for j in joined],
        input_output_aliases={a: a for a in range(n)},
        scratch_shapes=[pltpu.SemaphoreType.DMA((n,)), pltpu.SemaphoreType.DMA((n,))])(*joined)


def _pack(arrs, rows_mult):
    flat = jnp.concatenate([a.reshape(-1) for a in arrs])
    rows = -(-flat.shape[0] // (128 * rows_mult)) * rows_mult
    return jnp.pad(flat, (0, rows * 128 - flat.shape[0])).reshape(rows, 128)


def _unpack(packed, shapes):
    flat = packed.reshape(-1)
    out, pos = [], 0
    for s in shapes:
        size = 1
        for d in s:
            size *= d
        out.append(flat[pos:pos + size].reshape(s))
        pos += size
    return out


def _column_shard(full, k, axis):
    width = full.shape[axis] // N_CHIPS
    return lax.dynamic_slice_in_dim(full, k * width, width, axis)


def kernel(x, meta_tokens, norm0_g, l0_w_in, l0_lam_re, l0_lam_im, l0_log_dt, l0_b_re, l0_b_im, l0_c_re, l0_c_im, l0_d_skip, l0_w_glu, l0_b_glu, l0_w_out, norm1_g, l1_w_in, l1_conv_w, l1_conv_b, l1_w_out, norm2_g, l2_w_in, l2_w_grp, l2_b_grp, l2_scale, l2_w_out, norm3_g, l3_w_in, l3_lam_re, l3_lam_im, l3_log_dt, l3_b_re, l3_b_im, l3_c_re, l3_c_im, l3_d_skip, l3_w_glu, l3_b_glu, l3_w_out, final_g, loss_target, m_meta_tokens, m_norm0_g, m_l0_w_in, m_l0_lam_re, m_l0_lam_im, m_l0_log_dt, m_l0_b_re, m_l0_b_im, m_l0_c_re, m_l0_c_im, m_l0_d_skip, m_l0_w_glu, m_l0_b_glu, m_l0_w_out, m_norm1_g, m_l1_w_in, m_l1_conv_w, m_l1_conv_b, m_l1_w_out, m_norm2_g, m_l2_w_in, m_l2_w_grp, m_l2_b_grp, m_l2_scale, m_l2_w_out, m_norm3_g, m_l3_w_in, m_l3_lam_re, m_l3_lam_im, m_l3_log_dt, m_l3_b_re, m_l3_b_im, m_l3_c_re, m_l3_c_im, m_l3_d_skip, m_l3_w_glu, m_l3_b_glu, m_l3_w_out, m_final_g, v_meta_tokens, v_norm0_g, v_l0_w_in, v_l0_lam_re, v_l0_lam_im, v_l0_log_dt, v_l0_b_re, v_l0_b_im, v_l0_c_re, v_l0_c_im, v_l0_d_skip, v_l0_w_glu, v_l0_b_glu, v_l0_w_out, v_norm1_g, v_l1_w_in, v_l1_conv_w, v_l1_conv_b, v_l1_w_out, v_norm2_g, v_l2_w_in, v_l2_w_grp, v_l2_b_grp, v_l2_scale, v_l2_w_out, v_norm3_g, v_l3_w_in, v_l3_lam_re, v_l3_lam_im, v_l3_log_dt, v_l3_b_re, v_l3_b_im, v_l3_c_re, v_l3_c_im, v_l3_d_skip, v_l3_w_glu, v_l3_b_glu, v_l3_w_out, v_final_g):
    given = dict(locals())
    w = {n: given[n] for n in WEIGHTS}
    mom_m = {n: given["m_" + n] for n in WEIGHTS}
    mom_v = {n: given["v_" + n] for n in WEIGHTS}
    seq = x.shape[1]
    real = N_META + seq
    lp = -(-real // SCAN_ROWS) * SCAN_ROWS
    chip = 2 * lax.axis_index("x") + lax.axis_index("y")
    row = lambda a: a.reshape(1, -1)

    shard_bf16 = {n: (w[n].reshape(-1, w[n].shape[-1]) if n == 'l2_w_grp' else w[n]).astype(BF16) for n in BIG}
    layer = lambda i: [n for n in BIG if n.startswith("l%d_" % i)]
    gather_layer = lambda i: gather_by_halves_job([shard_bf16[n] for n in layer(i)])
    full = {}

    def landed(i, arrays):
        for n, arr in zip(layer(i), arrays, strict=True):
            if n.endswith('w_in'):
                full[n] = arr
            elif n == 'l2_w_grp':
                full[n] = arr.reshape(N_CHIPS, len(POOL_WINDOWS), POOL_GROUP // N_CHIPS, POOL_GROUP)
            else:
                full[n] = arr.reshape(1, -1, arr.shape[-1])

    landed(0, run_job(gather_layer(0), name="gather_l0"))
    full.update(zip(SMALL_SHARDED, gather_chips([w[n] for n in SMALL_SHARDED], name="gather_small_shards"),
                    strict=True))
    meta_full = full['meta_tokens'].transpose(1, 0, 2).reshape(N_META, D_MODEL)
    conv_w_full = full['l1_conv_w'].transpose(1, 0, 2).reshape(3, CONV_E)
    b_grp_full = full['l2_b_grp'].transpose(1, 0, 2).reshape(len(POOL_WINDOWS), POOL_GROUP)

    h0 = jnp.concatenate([meta_full, x[0], jnp.zeros((lp - real, D_MODEL), F32)], axis=0)
    target = jnp.concatenate([jnp.zeros((N_META, D_MODEL), F32), loss_target[0],
                              jnp.zeros((lp - real, D_MODEL), F32)], axis=0)
    grads = {}
    saved = {}

    def s5_layer_fwd(i, h, gather=None):
        pre = "l%d_" % i
        tables, tables_vjp = jax.vjp(s5_tables, *[w[pre + s] for s in
                                                   ('lam_re', 'lam_im', 'log_dt', 'b_re', 'b_im', 'c_re', 'c_im')])
        ar, ai, bblk, cblk = tables
        n = rms_fwd(h, row(w["norm%d_g" % i]), name=pre + "norm")
        uz = mm_nn(n, full[pre + 'w_in'], name=pre + "in")
        res = s5_fwd(uz, bblk, cblk, ar, ai, row(w[pre + 'd_skip']), name=pre + "scan",
                     job=None if gather is None else gather_layer(gather))
        if gather is not None:
            res, arrived = res
            landed(gather, arrived)
        y0, yg, sr, si = res
        q = mm_nn(yg, full[pre + 'w_glu'], bias=row(w[pre + 'b_glu']), name=pre + "glu")
        (y3,), _ = rowwise(lambda yv, qv, zv: ([_s5_gate(yv, qv, zv)], []),
                           [(y0, D_MODEL, 0), (q, D_MODEL, 0), (uz, D_MODEL, 1)], [],
                           [(D_MODEL, BF16, D_MODEL, 0)], name=pre + "gate")
        h_new = mm_nn(y3, full[pre + 'w_out'], add=h, name=pre + "out")
        saved[i] = dict(h=h, n=n, uz=uz, y0=y0, sr=sr, si=si, yg=yg, q=q, y3=y3, tables=tables,
                        tables_vjp=tables_vjp)
        return h_new

    def s5_layer_bwd(i, dh, scatter=None):
        pre = "l%d_" % i
        s = saved[i]
        ar, ai, bblk, cblk = s['tables']
        grads[pre + 'w_out'] = mm_tn(s['y3'], dh, 1, name=pre + "d_w_out")[0]
        dy3 = mm_nt(dh, full[pre + 'w_out'], name=pre + "d_y3")

        def gate_bwd(dyv, yv, qv, zv):
            _, vjp = jax.vjp(_s5_gate, yv, qv, zv)
            dy0, dq, dz = vjp(dyv)
            return [dy0, dq, dz], [_colsum(dq)]

        (dy0_direct, dq, dp), (db_glu,) = rowwise(
            gate_bwd, [(dy3, D_MODEL, 0), (s['y0'], D_MODEL, 0), (s['q'], D_MODEL, 0), (s['uz'], D_MODEL, 1)], [],
            [(D_MODEL, F32, D_MODEL, 0), (D_MODEL, F32, D_MODEL, 0), (2 * D_MODEL, F32, D_MODEL, 1)], [D_MODEL],
            name=pre + "d_gate")
        grads[pre + 'b_glu'] = db_glu.reshape(-1)
        grads[pre + 'w_glu'] = mm_tn(s['yg'], dq, 1, name=pre + "d_w_glu")[0]
        dyg = mm_nt(dq, full[pre + 'w_glu'], name=pre + "d_yg")
        res = s5_bwd(dy0_direct, dyg, s['y0'], s['uz'], s['sr'], s['si'], bblk, cblk, ar, ai,
                     row(w[pre + 'd_skip']), dp, name=pre + "d_scan",
                     job=None if scatter is None else scatter_job(scatter_of(scatter)))
        if scatter is not None:
            res, arrived = res
            scattered(scatter, arrived)
        dp, dbblk, dcblk, dar, dai, dd = res
        grads[pre + 'd_skip'] = dd.reshape(-1)
        for key, val in zip(('lam_re', 'lam_im', 'log_dt', 'b_re', 'b_im', 'c_re', 'c_im'),
                            s['tables_vjp']((dar, dai, dbblk, dcblk)), strict=True):
            grads[pre + key] = val
        grads[pre + 'w_in'] = mm_tn(s['n'], dp, N_CHIPS, name=pre + "d_w_in")
        dn = mm_nt(dp, full[pre + 'w_in'], name=pre + "d_n")
        dh_in, dg = rms_bwd(s['h'], row(w["norm%d_g" % i]), dn, dh, name=pre + "d_norm")
        grads["norm%d_g" % i] = dg.reshape(-1)
        return dh_in

    h1 = s5_layer_fwd(0, h0, gather=1)

    n1 = rms_fwd(h1, row(w['norm1_g']), name="l1_norm")
    p1, arrived = mm_nn(n1, full['l1_w_in'], name="l1_in", job=gather_layer(2))
    landed(2, arrived)
    y3_1 = conv_fwd(p1, conv_w_full, row(w['l1_conv_b']), name="l1_conv")
    h2 = mm_nn(y3_1, full['l1_w_out'], add=h1, name="l1_out")

    n2 = rms_fwd(h2, row(w['norm2_g']), name="l2_norm")
    p2, arrived = mm_nn(n2, full['l2_w_in'], name="l2_in", job=gather_layer(3))
    landed(3, arrived)
    mixed = pool_fwd(p2, name="l2_pool")
    o2 = grp_nn(mixed, full['l2_w_grp'], b_grp_full.reshape(1, -1), name="l2_grp")
    (y3_2,), _ = rowwise(lambda ov, zv, sv: ([_pool_gate(ov, zv, sv)], []),
                         [(o2, POOL_E, 0), (p2, POOL_E, 1)], [row(w['l2_scale'])],
                         [(POOL_E, BF16, POOL_E, 0)], name="l2_gate")
    h3 = mm_nn(y3_2, full['l2_w_out'], add=h2, name="l2_out")

    h4 = s5_layer_fwd(3, h3)

    def loss_fn(hv, tv, gv):
        i = pl.program_id(0)
        tr = hv.shape[0]
        yf, vjp = jax.vjp(_rms, hv, gv)
        pos = i * tr + lax.broadcasted_iota(jnp.int32, (tr, 1), 0)
        diff = jnp.where((pos >= N_META) & (pos < real), yf - tv, 0.0)
        dh, dg = vjp(diff / D_MODEL)
        return [dh], [dg, _colsum(diff * diff)]

    (dh,), (d_final_g, sq) = rowwise(loss_fn, [(h4, D_MODEL, 0), (target, D_MODEL, 0)], [row(w['final_g'])],
                                     [(D_MODEL, F32, D_MODEL, 0)], [D_MODEL, D_MODEL], name="loss")
    grads['final_g'] = d_final_g.reshape(-1)
    loss = lax.psum(0.5 / D_MODEL * jnp.sum(sq), ("x", "y", "c"))

    ck = jnp.stack([lax.axis_index("c"), chip]).astype(jnp.int32)
    pair, from_chips = {}, {}

    def chip_major(n):
        g = grads[n]
        if n.endswith('w_in'):
            return g
        if n == 'l2_w_grp':
            return g.reshape(N_CHIPS, -1, POOL_GROUP)
        return g.reshape(N_CHIPS, -1, g.shape[-1])

    def reduce_cores(i, small=None):
        names, pieces, wire = layer(i), [chip_major(n) for n in layer(i)], [BF16] * len(layer(i))
        if small is not None:
            names, pieces, wire = names + ['small'], pieces + [small], wire + [F32]
        theirs = swap_halves(pieces, name="reduce_cores_l%d" % i)
        for n, g, t, dt in zip(names, pieces, theirs, wire, strict=True):
            pair[n] = sum_cores(g, t, ck, dt, name="sum_cores_" + n)

    scatter_of = lambda i: [pair[n] for n in layer(i)]

    def scattered(i, arrays):
        from_chips.update(zip(layer(i), arrays, strict=True))

    dh = s5_layer_bwd(3, dh)
    reduce_cores(3)

    grads['l2_w_out'] = mm_tn(y3_2, dh, 1, name="l2_d_w_out")[0]
    dy3 = mm_nt(dh, full['l2_w_out'], name="l2_d_y3")

    def pool_gate_bwd(dyv, ov, zv, sv):
        _, vjp = jax.vjp(_pool_gate, ov, zv, sv)
        do, dz, dscale = vjp(dyv)
        return [do, dz], [dscale, _colsum(do)]

    (do2, dp2), (dscale, dbgrp) = rowwise(
        pool_gate_bwd, [(dy3, POOL_E, 0), (o2, POOL_E, 0), (p2, POOL_E, 1)], [row(w['l2_scale'])],
        [(POOL_E, F32, POOL_E, 0), (2 * POOL_E, F32, POOL_E, 1)], [POOL_E, POOL_E], name="l2_d_gate")
    grads['l2_scale'] = dscale.reshape(-1)
    grads['l2_b_grp'] = dbgrp.reshape(len(POOL_WINDOWS), POOL_GROUP)
    grads['l2_w_grp'] = grp_tn(mixed, do2, name="l2_d_w_grp")
    dmix = grp_nt(do2, full['l2_w_grp'], name="l2_d_mixed")
    dp2 = pool_bwd(dmix, dp2, name="l2_d_pool")
    grads['l2_w_in'], arrived = mm_tn(n2, dp2, N_CHIPS, name="l2_d_w_in", job=scatter_job(scatter_of(3)))
    scattered(3, arrived)
    dn = mm_nt(dp2, full['l2_w_in'], name="l2_d_n")
    dh, dg = rms_bwd(h2, row(w['norm2_g']), dn, dh, name="l2_d_norm")
    grads['norm2_g'] = dg.reshape(-1)
    reduce_cores(2)

    grads['l1_w_out'] = mm_tn(y3_1, dh, 1, name="l1_d_w_out")[0]
    dy3 = mm_nt(dh, full['l1_w_out'], name="l1_d_y3")
    dp1, dcw, dcb = conv_bwd(dy3, p1, conv_w_full, row(w['l1_conv_b']), name="l1_d_conv")
    grads['l1_conv_w'] = dcw
    grads['l1_conv_b'] = dcb.reshape(-1)
    grads['l1_w_in'], arrived = mm_tn(n1, dp1, N_CHIPS, name="l1_d_w_in", job=scatter_job(scatter_of(2)))
    scattered(2, arrived)
    dn = mm_nt(dp1, full['l1_w_in'], name="l1_d_n")
    dh, dg = rms_bwd(h1, row(w['norm1_g']), dn, dh, name="l1_d_norm")
    grads['norm1_g'] = dg.reshape(-1)
    reduce_cores(1)

    dh = s5_layer_bwd(0, dh, scatter=1)
    grad_x = dh[N_META:real][None]
    grads['meta_tokens'] = dh[:N_META]

    small_local = _pack([grads[n] for n in SMALL], 32 * N_CHIPS)
    reduce_cores(0, small_local.reshape(N_CHIPS, -1, 128))
    rest = layer(0) + ['small']
    from_chips.update(zip(rest, run_job(scatter_job([pair[n] for n in rest]), name="reduce_chips_l0"), strict=True))
    joined = [sum_chips(pair[n], from_chips[n], ck, name="sum_chips_" + n) for n in BIG + ['small']]
    joined = share_halves(joined, name="share_cores")
    reduced = [j.reshape(-1, j.shape[-1]) for j in joined]
    g_big = {n: reduced[i].reshape(w[n].shape) for i, n in enumerate(BIG)}
    (small_all,) = gather_chips([reduced[-1]], name="gather_small")
    g_small_full = dict(zip(SMALL, _unpack(small_all, [grads[n].shape for n in SMALL]), strict=True))
    g_small = {}
    for n in SMALL:
        g = g_small_full[n]
        g_small[n] = _column_shard(g, chip, g.ndim - 1) if n in SMALL_SHARDED else g

    out_g, out_d, out_m, out_v = {}, {}, {}, {}
    for n in BIG:
        shape = w[n].shape
        as2d = lambda a: a.reshape(-1, shape[-1])
        d, nm, nv = adamw(as2d(w[n]), as2d(g_big[n]), as2d(mom_m[n]), as2d(mom_v[n]), name="adamw_" + n)
        out_g[n], out_d[n], out_m[n], out_v[n] = g_big[n], d.reshape(shape), nm.reshape(shape), nv.reshape(shape)
    packs = [_pack([src[n] for n in SMALL], 8) for src in (w, g_small, mom_m, mom_v)]
    d, nm, nv = adamw(*packs, name="adamw_small")
    shapes = [w[n].shape for n in SMALL]
    for n, dv, mv, vv in zip(SMALL, _unpack(d, shapes), _unpack(nm, shapes), _unpack(nv, shapes), strict=True):
        out_g[n], out_d[n], out_m[n], out_v[n] = g_small[n], dv, mv, vv

    return (loss, grad_x, *[out_g[n] for n in WEIGHTS], *[out_d[n] for n in WEIGHTS],
            *[out_m[n] for n in WEIGHTS], *[out_v[n] for n in WEIGHTS])
```

```python
import functools

import jax
import jax.numpy as jnp
from jax import lax
from jax.experimental import pallas as pl
from jax.experimental.pallas import tpu as pltpu

F32, BF16 = jnp.float32, jnp.bfloat16
MESH = pl.DeviceIdType.MESH

D_MODEL = 1024
N_META = 16
EPS = 1e-6
S5_GROUPS, S5_GROUP, S5_STATE = 64, 16, 64
LANE_BLOCK = 128
S5_BLOCKS = D_MODEL // LANE_BLOCK
GROUPS_PER_BLOCK = LANE_BLOCK // S5_GROUP
STATE_BLOCK = GROUPS_PER_BLOCK * S5_STATE
SCAN_ROWS = 256
SUBLANES = 8
CONV_E = 2048
POOL_E = 2048
POOL_WINDOWS = (2, 4, 8, 16)
POOL_GROUP = 512
POOL_HALO = 16
N_CHIPS = 4

ADAM_LR, ADAM_B1, ADAM_B2, ADAM_EPS, ADAM_WD, ADAM_STEP = 0.001, 0.9, 0.999, 1e-08, 0.01, 10

VMEM_LIMIT = 48 * 1024 * 1024
TN_OPERAND_BYTES = 28 * 1024 * 1024

WEIGHTS = ['meta_tokens', 'norm0_g', 'l0_w_in', 'l0_lam_re', 'l0_lam_im', 'l0_log_dt', 'l0_b_re', 'l0_b_im',
           'l0_c_re', 'l0_c_im', 'l0_d_skip', 'l0_w_glu', 'l0_b_glu', 'l0_w_out', 'norm1_g', 'l1_w_in',
           'l1_conv_w', 'l1_conv_b', 'l1_w_out', 'norm2_g', 'l2_w_in', 'l2_w_grp', 'l2_b_grp', 'l2_scale',
           'l2_w_out', 'norm3_g', 'l3_w_in', 'l3_lam_re', 'l3_lam_im', 'l3_log_dt', 'l3_b_re', 'l3_b_im',
           'l3_c_re', 'l3_c_im', 'l3_d_skip', 'l3_w_glu', 'l3_b_glu', 'l3_w_out', 'final_g']
BIG = ['l0_w_in', 'l0_w_glu', 'l0_w_out', 'l1_w_in', 'l1_w_out', 'l2_w_in', 'l2_w_grp', 'l2_w_out',
       'l3_w_in', 'l3_w_glu', 'l3_w_out']
SMALL_SHARDED = ['meta_tokens', 'l1_conv_w', 'l2_b_grp']
SMALL = [n for n in WEIGHTS if n not in BIG]


def _params(*sem):
    return pltpu.CompilerParams(dimension_semantics=sem, vmem_limit_bytes=VMEM_LIMIT)


class Job:
    def __init__(self, arrays, out_shapes, sems, fn):
        self.arrays, self.out_shapes, self.sems, self.fn = list(arrays), list(out_shapes), list(sems), fn


def _call(body, *, name, grid, in_specs, out_specs, out_shape, args, semantics, scratch=(), aliases=None, job=None):
    if job is None:
        return pl.pallas_call(
            body, name=name, grid=grid, in_specs=list(in_specs), out_specs=list(out_specs),
            out_shape=list(out_shape), scratch_shapes=list(scratch), input_output_aliases=aliases or {},
            compiler_params=_params(*semantics))(*args)
    counts = [len(in_specs), len(job.arrays), len(out_specs), len(job.out_shapes), len(scratch), len(job.sems)]

    def hosted(*refs):
        parts, pos = [], 0
        for k in counts:
            parts.append(refs[pos:pos + k])
            pos += k
        ins, job_ins, outs, job_outs, scr, job_sems = parts
        steps = [pl.program_id(d) for d in range(len(grid))]
        first = functools.reduce(jnp.logical_and, [s == 0 for s in steps])
        last = functools.reduce(jnp.logical_and, [s == g - 1 for s, g in zip(steps, grid, strict=True)])

        @pl.when(first)
        def _():
            job.fn(job_ins, job_outs, job_sems, "start")

        body(*ins, *outs, *scr)

        @pl.when(last)
        def _():
            job.fn(job_ins, job_outs, job_sems, "finish")

    any_spec = pl.BlockSpec(memory_space=pl.ANY)
    res = pl.pallas_call(
        hosted, name=name, grid=grid, in_specs=list(in_specs) + [any_spec] * len(job.arrays),
        out_specs=list(out_specs) + [any_spec] * len(job.out_shapes),
        out_shape=list(out_shape) + job.out_shapes, scratch_shapes=list(scratch) + job.sems,
        input_output_aliases=aliases or {},
        compiler_params=_params(*["arbitrary"] * len(grid)))(*args, *job.arrays)
    return res[:len(out_specs)], res[len(out_specs):]


def _tile(n, cap, mult):
    best = None
    for t in range(mult, min(n, cap) + 1, mult):
        if n % t == 0:
            best = t
    assert best is not None, (n, cap, mult)
    return best


def _with_job(res, job):
    return res[0] if job is None else (res[0][0], res[1])


def mm_nn(a, b, *, name, bias=None, add=None, out_dtype=F32, job=None):
    m, k = a.shape
    s, _, ns = b.shape
    n = s * ns
    tm, tn = _tile(m, 1088, 16), _tile(ns, 512, 128)
    per = ns // tn

    def body(*refs):
        a_ref, b_ref = refs[0], refs[1]
        o_ref = refs[-1]
        acc = jnp.dot(a_ref[...].astype(BF16), b_ref[...], preferred_element_type=F32)
        pos = 2
        if bias is not None:
            acc = acc + refs[pos][...]
            pos += 1
        if add is not None:
            acc = acc + refs[pos][...]
        o_ref[...] = acc.astype(o_ref.dtype)

    in_specs = [pl.BlockSpec((tm, k), lambda i, j: (i, 0)),
                pl.BlockSpec((None, k, tn), lambda i, j: (j // per, 0, j % per))]
    args = [a, b]
    if bias is not None:
        in_specs.append(pl.BlockSpec((1, tn), lambda i, j: (0, j)))
        args.append(bias)
    if add is not None:
        in_specs.append(pl.BlockSpec((tm, tn), lambda i, j: (i, j)))
        args.append(add)
    return _with_job(_call(
        body, name=name, grid=(m // tm, n // tn), in_specs=in_specs,
        out_specs=[pl.BlockSpec((tm, tn), lambda i, j: (i, j))],
        out_shape=[jax.ShapeDtypeStruct((m, n), out_dtype)], args=args,
        semantics=("parallel", "parallel"), job=job), job)


def mm_nt(g, w, *, name, job=None):
    m, kc = g.shape
    s, nout, kcs = w.shape
    assert s * kcs == kc
    tm, tno, tk = _tile(m, 1088, 16), _tile(nout, 1024, 128), _tile(kcs, 1024, 128)
    per = kcs // tk

    def body(g_ref, w_ref, o_ref):
        kk = pl.program_id(2)
        part = lax.dot_general(g_ref[...].astype(BF16), w_ref[...], (((1,), (1,)), ((), ())),
                               preferred_element_type=F32)

        @pl.when(kk == 0)
        def _():
            o_ref[...] = part

        @pl.when(kk > 0)
        def _():
            o_ref[...] += part

    return _with_job(_call(
        body, name=name, grid=(m // tm, nout // tno, kc // tk),
        in_specs=[pl.BlockSpec((tm, tk), lambda i, j, kk: (i, kk)),
                  pl.BlockSpec((None, tno, tk), lambda i, j, kk: (kk // per, j, kk % per))],
        out_specs=[pl.BlockSpec((tm, tno), lambda i, j, kk: (i, j))],
        out_shape=[jax.ShapeDtypeStruct((m, nout), F32)], args=(g, w),
        semantics=("parallel", "parallel", "arbitrary"), job=job), job)


def mm_tn(a, g, shards, *, name, job=None):
    m, ka = a.shape
    n = g.shape[1]
    ns = n // shards
    tka, tn = _tile(ka, 1024, 128), _tile(ns, 512, 128)
    row_bytes = tka * jnp.dtype(a.dtype).itemsize + tn * jnp.dtype(g.dtype).itemsize
    tm = _tile(m, TN_OPERAND_BYTES // (2 * row_bytes), 16)
    per = ns // tn

    def body(a_ref, g_ref, o_ref):
        mm = pl.program_id(2)
        part = lax.dot_general(a_ref[...].astype(BF16), g_ref[...].astype(BF16), (((0,), (0,)), ((), ())),
                               preferred_element_type=F32)

        @pl.when(mm == 0)
        def _():
            o_ref[...] = part

        @pl.when(mm > 0)
        def _():
            o_ref[...] += part

    return _with_job(_call(
        body, name=name, grid=(ka // tka, n // tn, m // tm),
        in_specs=[pl.BlockSpec((tm, tka), lambda i, j, mm: (mm, i)),
                  pl.BlockSpec((tm, tn), lambda i, j, mm: (mm, j))],
        out_specs=[pl.BlockSpec((None, tka, tn), lambda i, j, mm: (j // per, i, j % per))],
        out_shape=[jax.ShapeDtypeStruct((shards, ka, ns), F32)], args=(a, g),
        semantics=("parallel", "parallel", "arbitrary"), job=job), job)


def grp_nn(a, w, bias, *, name):
    m = a.shape[0]
    tm = _tile(m, 1088, 16)
    ng = len(POOL_WINDOWS)

    def body(a_ref, w_ref, b_ref, o_ref):
        wj = w_ref[...].reshape(POOL_GROUP, POOL_GROUP)
        o_ref[...] = jnp.dot(a_ref[...].astype(BF16), wj, preferred_element_type=F32) + b_ref[...]

    return pl.pallas_call(
        body, name=name, grid=(m // tm, ng),
        in_specs=[pl.BlockSpec((tm, POOL_GROUP), lambda i, j: (i, j)),
                  pl.BlockSpec((N_CHIPS, None, POOL_GROUP // N_CHIPS, POOL_GROUP), lambda i, j: (0, j, 0, 0)),
                  pl.BlockSpec((1, POOL_GROUP), lambda i, j: (0, j))],
        out_specs=pl.BlockSpec((tm, POOL_GROUP), lambda i, j: (i, j)),
        out_shape=jax.ShapeDtypeStruct((m, ng * POOL_GROUP), F32),
        compiler_params=_params("parallel", "parallel"))(a, w, bias)


def grp_nt(g, w, *, name):
    m = g.shape[0]
    tm = _tile(m, 1088, 16)
    ng = len(POOL_WINDOWS)

    def body(g_ref, w_ref, o_ref):
        wj = w_ref[...].reshape(POOL_GROUP, POOL_GROUP)
        o_ref[...] = lax.dot_general(g_ref[...].astype(BF16), wj, (((1,), (1,)), ((), ())),
                                     preferred_element_type=F32)

    return pl.pallas_call(
        body, name=name, grid=(m // tm, ng),
        in_specs=[pl.BlockSpec((tm, POOL_GROUP), lambda i, j: (i, j)),
                  pl.BlockSpec((N_CHIPS, None, POOL_GROUP // N_CHIPS, POOL_GROUP), lambda i, j: (0, j, 0, 0))],
        out_specs=pl.BlockSpec((tm, POOL_GROUP), lambda i, j: (i, j)),
        out_shape=jax.ShapeDtypeStruct((m, ng * POOL_GROUP), F32),
        compiler_params=_params("parallel", "parallel"))(g, w)


def grp_tn(a, g, *, name):
    m = a.shape[0]
    tm = _tile(m, 1088, 16)
    ng = len(POOL_WINDOWS)
    rows = POOL_GROUP // N_CHIPS

    def body(a_ref, g_ref, o_ref):
        mm = pl.program_id(1)
        part = lax.dot_general(a_ref[...].astype(BF16), g_ref[...].astype(BF16), (((0,), (0,)), ((), ())),
                               preferred_element_type=F32).reshape(N_CHIPS, rows, POOL_GROUP)

        @pl.when(mm == 0)
        def _():
            o_ref[...] = part

        @pl.when(mm > 0)
        def _():
            o_ref[...] += part

    return pl.pallas_call(
        body, name=name, grid=(ng, m // tm),
        in_specs=[pl.BlockSpec((tm, POOL_GROUP), lambda j, mm: (mm, j)),
                  pl.BlockSpec((tm, POOL_GROUP), lambda j, mm: (mm, j))],
        out_specs=pl.BlockSpec((N_CHIPS, None, rows, POOL_GROUP), lambda j, mm: (0, j, 0, 0)),
        out_shape=jax.ShapeDtypeStruct((N_CHIPS, ng, rows, POOL_GROUP), F32),
        compiler_params=_params("parallel", "arbitrary"))(a, g)


def rowwise(fn, rows, bcast, outs, accs=(), *, name, aliases=None):
    m = rows[0][0].shape[0]
    tr = m // 16
    n_in = len(rows) + len(bcast)

    def body(*refs):
        vals = [r[...] for r in refs[:n_in]]
        o_vals, a_vals = fn(*vals)
        for ref, val in zip(refs[n_in:n_in + len(outs)], o_vals, strict=True):
            ref[...] = val.astype(ref.dtype)
        step = pl.program_id(0)
        for ref, val in zip(refs[n_in + len(outs):], a_vals, strict=True):
            @pl.when(step == 0)
            def _(ref=ref, val=val):
                ref[...] = val

            @pl.when(step > 0)
            def _(ref=ref, val=val):
                ref[...] += val

    in_specs = [pl.BlockSpec((tr, w), functools.partial(lambda i, cb: (i, cb), cb=cb)) for _, w, cb in rows]
    in_specs += [pl.BlockSpec(b.shape, lambda i: (0, 0)) for b in bcast]
    out_specs = [pl.BlockSpec((tr, w), functools.partial(lambda i, cb: (i, cb), cb=cb)) for _, _, w, cb in outs]
    out_specs += [pl.BlockSpec((1, w), lambda i: (0, 0)) for w in accs]
    out_shape = [jax.ShapeDtypeStruct((m, cols), dt) for cols, dt, _, _ in outs]
    out_shape += [jax.ShapeDtypeStruct((1, w), F32) for w in accs]
    res = pl.pallas_call(
        body, name=name, grid=(m // tr,), in_specs=in_specs, out_specs=out_specs, out_shape=out_shape,
        input_output_aliases=aliases or {},
        compiler_params=_params("arbitrary"))(*[r[0] for r in rows], *bcast)
    return res[:len(outs)], res[len(outs):]


def _rms(h, g):
    return h * lax.rsqrt(jnp.mean(h * h, axis=-1, keepdims=True) + EPS) * g


def _colsum(v):
    return jnp.sum(v, axis=0, keepdims=True)


def rms_fwd(h, g, *, name):
    (n,), _ = rowwise(lambda hv, gv: ([_rms(hv, gv)], []), [(h, D_MODEL, 0)], [g],
                      [(D_MODEL, BF16, D_MODEL, 0)], name=name)
    return n


def rms_bwd(h, g, dn, dh_out, *, name):
    def fn(hv, dnv, dhv, gv):
        _, vjp = jax.vjp(_rms, hv, gv)
        dh, dg = vjp(dnv)
        return [dhv + dh], [dg]

    (dh,), (dg,) = rowwise(fn, [(h, D_MODEL, 0), (dn, D_MODEL, 0), (dh_out, D_MODEL, 0)], [g],
                           [(D_MODEL, F32, D_MODEL, 0)], [D_MODEL], name=name)
    return dh, dg


def _s5_gate(y0, q, z):
    yg = jax.nn.gelu(y0)
    return yg * jax.nn.sigmoid(q) * jax.nn.silu(z)


def _pool_gate(o, z, scale):
    return o * scale * jax.nn.silu(z)


def _shift_rows(v, d, down):
    t = v.shape[0]
    row = lax.broadcasted_iota(jnp.int32, v.shape, 0)
    if down:
        return jnp.where(row >= d, pltpu.roll(v, d, 0), 0.0)
    return jnp.where(row < t - d, pltpu.roll(v, t - d, 0), 0.0)


def _cmul(ar, ai, br, bi):
    return ar * br - ai * bi, ar * bi + ai * br


def _scan(xr, xi, ar, ai, cr, ci, down):
    t, n = xr.shape
    nb = t // SUBLANES
    pw = [(ar, ai)]
    for _ in range(SUBLANES - 1):
        pw.append(_cmul(*pw[-1], ar, ai))
    sub = lax.broadcasted_iota(jnp.int32, (SUBLANES, n), 0)

    def table(exps):
        tr, ti = jnp.zeros((SUBLANES, n), F32), jnp.zeros((SUBLANES, n), F32)
        for r, e in enumerate(exps):
            if e:
                tr, ti = jnp.where(sub == r, pw[e - 1][0], tr), jnp.where(sub == r, pw[e - 1][1], ti)
        return tr[None], ti[None]

    sr, si = xr.reshape(nb, SUBLANES, n), xi.reshape(nb, SUBLANES, n)
    d = 1
    while d < SUBLANES:
        mr, mi = table([d if (r >= d if down else r < SUBLANES - d) else 0 for r in range(SUBLANES)])
        turn = d if down else SUBLANES - d
        hr, hi = pltpu.roll(sr, turn, 1), pltpu.roll(si, turn, 1)
        sr, si = sr + mr * hr - mi * hi, si + mr * hi + mi * hr
        d *= 2
    edge = SUBLANES - 1 if down else 0
    qr, qi = table([r + 1 if down else SUBLANES - r for r in range(SUBLANES)])
    qr, qi = qr[0], qi[0]
    in_r, in_i = jnp.broadcast_to(cr, (SUBLANES, n)), jnp.broadcast_to(ci, (SUBLANES, n))
    out_r, out_i = [None] * nb, [None] * nb
    for b in (range(nb) if down else reversed(range(nb))):
        out_r[b] = sr[b] + qr * in_r - qi * in_i
        out_i[b] = si[b] + qr * in_i + qi * in_r
        in_r = jnp.broadcast_to(out_r[b][edge:edge + 1, :], (SUBLANES, n))
        in_i = jnp.broadcast_to(out_i[b][edge:edge + 1, :], (SUBLANES, n))
    return jnp.concatenate(out_r, axis=0), jnp.concatenate(out_i, axis=0), in_r[0:1, :], in_i[0:1, :]


def s5_fwd(uz, bblk, cblk, ar, ai, dskip, *, name, job=None):
    lp = uz.shape[0]
    t = SCAN_ROWS
    nst = S5_GROUPS * S5_STATE

    def body(u_ref, b_ref, c_ref, ar_ref, ai_ref, d_ref, y_ref, yg_ref, sr_ref, si_ref, cr, ci):
        step = pl.program_id(1)

        @pl.when(step == 0)
        def _():
            cr[...] = jnp.zeros_like(cr)
            ci[...] = jnp.zeros_like(ci)

        u = u_ref[...]
        a_r, a_i = ar_ref[...], ai_ref[...]
        x = jnp.dot(u.astype(BF16), b_ref[...].astype(BF16), preferred_element_type=F32)
        sr, si, cr[...], ci[...] = _scan(x[:, :STATE_BLOCK], x[:, STATE_BLOCK:], a_r, a_i, cr[...], ci[...], True)
        sr_ref[...] = sr
        si_ref[...] = si
        s = jnp.concatenate([sr, si], axis=1).astype(BF16)
        y0 = jnp.dot(s, c_ref[...].astype(BF16), preferred_element_type=F32) + d_ref[...] * u
        y_ref[...] = y0
        yg_ref[...] = jax.nn.gelu(y0).astype(yg_ref.dtype)

    return _call(
        body, name=name, grid=(S5_BLOCKS, lp // t),
        in_specs=[pl.BlockSpec((t, LANE_BLOCK), lambda b, c: (c, b)),
                  pl.BlockSpec((None, LANE_BLOCK, 2 * STATE_BLOCK), lambda b, c: (b, 0, 0)),
                  pl.BlockSpec((None, 2 * STATE_BLOCK, LANE_BLOCK), lambda b, c: (b, 0, 0)),
                  pl.BlockSpec((1, STATE_BLOCK), lambda b, c: (0, b)),
                  pl.BlockSpec((1, STATE_BLOCK), lambda b, c: (0, b)),
                  pl.BlockSpec((1, LANE_BLOCK), lambda b, c: (0, b))],
        out_specs=[pl.BlockSpec((t, LANE_BLOCK), lambda b, c: (c, b)),
                   pl.BlockSpec((t, LANE_BLOCK), lambda b, c: (c, b)),
                   pl.BlockSpec((t, STATE_BLOCK), lambda b, c: (c, b)),
                   pl.BlockSpec((t, STATE_BLOCK), lambda b, c: (c, b))],
        out_shape=[jax.ShapeDtypeStruct((lp, D_MODEL), F32), jax.ShapeDtypeStruct((lp, D_MODEL), BF16),
                   jax.ShapeDtypeStruct((lp, nst), F32), jax.ShapeDtypeStruct((lp, nst), F32)],
        scratch=[pltpu.VMEM((1, STATE_BLOCK), F32), pltpu.VMEM((1, STATE_BLOCK), F32)],
        args=(uz, bblk, cblk, ar, ai, dskip), semantics=("parallel", "arbitrary"), job=job)


def s5_bwd(dy_direct, dyg, y0, uz, sr, si, bblk, cblk, ar, ai, dskip, dp, *, name, job=None):
    lp = uz.shape[0]
    t = SCAN_ROWS
    nch = lp // t
    nst = S5_GROUPS * S5_STATE

    def body(dyd_ref, dyg_ref, y0_ref, u_ref, sr_ref, si_ref, b_ref, c_ref, ar_ref, ai_ref, d_ref, _, du_ref,
             db_ref, dc_ref, dar_ref, dai_ref, dd_ref, cr, ci):
        step = pl.program_id(1)

        @pl.when(step == 0)
        def _():
            cr[...] = jnp.zeros_like(cr)
            ci[...] = jnp.zeros_like(ci)

        dy = dyd_ref[...] + jax.vjp(jax.nn.gelu, y0_ref[...])[1](dyg_ref[...])[0]
        u = u_ref[...]
        dyb, ub = dy.astype(BF16), u.astype(BF16)
        a_r, a_i = ar_ref[...], -ai_ref[...]
        g = lax.dot_general(dyb, c_ref[...].astype(BF16), (((1,), (1,)), ((), ())), preferred_element_type=F32)
        gr, gi = g[:, :STATE_BLOCK], g[:, STATE_BLOCK:]
        nxt_r, nxt_i = cr[...], ci[...]
        last = lax.broadcasted_iota(jnp.int32, gr.shape, 0) == t - 1
        lr, li, cr[...], ci[...] = _scan(gr, gi, a_r, a_i, nxt_r, nxt_i, False)
        nr = _shift_rows(lr, 1, False) + jnp.where(last, nxt_r, 0.0)
        ni = _shift_rows(li, 1, False) + jnp.where(last, nxt_i, 0.0)
        s_r, s_i = sr_ref[...], si_ref[...]
        dar = _colsum(s_r * nr + s_i * ni)
        dai = _colsum(s_r * ni - s_i * nr)
        lam = jnp.concatenate([lr, li], axis=1).astype(BF16)
        sb = jnp.concatenate([s_r, s_i], axis=1).astype(BF16)
        dc = lax.dot_general(sb, dyb, (((0,), (0,)), ((), ())), preferred_element_type=F32)
        db = lax.dot_general(ub, lam, (((0,), (0,)), ((), ())), preferred_element_type=F32)
        du_ref[...] = (lax.dot_general(lam, b_ref[...].astype(BF16), (((1,), (1,)), ((), ())),
                                       preferred_element_type=F32) + d_ref[...] * dy).astype(du_ref.dtype)
        dd = _colsum(dy * u)

        @pl.when(step == 0)
        def _():
            db_ref[...] = db
            dc_ref[...] = dc
            dar_ref[...] = dar
            dai_ref[...] = dai
            dd_ref[...] = dd

        @pl.when(step > 0)
        def _():
            db_ref[...] += db
            dc_ref[...] += dc
            dar_ref[...] += dar
            dai_ref[...] += dai
            dd_ref[...] += dd

    rev = lambda b, c: (nch - 1 - c, b)
    return _call(
        body, name=name, grid=(S5_BLOCKS, nch),
        in_specs=[pl.BlockSpec((t, LANE_BLOCK), rev), pl.BlockSpec((t, LANE_BLOCK), rev),
                  pl.BlockSpec((t, LANE_BLOCK), rev), pl.BlockSpec((t, LANE_BLOCK), rev),
                  pl.BlockSpec((t, STATE_BLOCK), rev), pl.BlockSpec((t, STATE_BLOCK), rev),
                  pl.BlockSpec((None, LANE_BLOCK, 2 * STATE_BLOCK), lambda b, c: (b, 0, 0)),
                  pl.BlockSpec((None, 2 * STATE_BLOCK, LANE_BLOCK), lambda b, c: (b, 0, 0)),
                  pl.BlockSpec((1, STATE_BLOCK), lambda b, c: (0, b)),
                  pl.BlockSpec((1, STATE_BLOCK), lambda b, c: (0, b)),
                  pl.BlockSpec((1, LANE_BLOCK), lambda b, c: (0, b)),
                  pl.BlockSpec(memory_space=pl.ANY)],
        out_specs=[pl.BlockSpec((t, LANE_BLOCK), rev),
                   pl.BlockSpec((None, LANE_BLOCK, 2 * STATE_BLOCK), lambda b, c: (b, 0, 0)),
                   pl.BlockSpec((None, 2 * STATE_BLOCK, LANE_BLOCK), lambda b, c: (b, 0, 0)),
                   pl.BlockSpec((1, STATE_BLOCK), lambda b, c: (0, b)),
                   pl.BlockSpec((1, STATE_BLOCK), lambda b, c: (0, b)),
                   pl.BlockSpec((1, LANE_BLOCK), lambda b, c: (0, b))],
        out_shape=[jax.ShapeDtypeStruct(dp.shape, dp.dtype),
                   jax.ShapeDtypeStruct((S5_BLOCKS, LANE_BLOCK, 2 * STATE_BLOCK), F32),
                   jax.ShapeDtypeStruct((S5_BLOCKS, 2 * STATE_BLOCK, LANE_BLOCK), F32),
                   jax.ShapeDtypeStruct((1, nst), F32), jax.ShapeDtypeStruct((1, nst), F32),
                   jax.ShapeDtypeStruct((1, D_MODEL), F32)],
        scratch=[pltpu.VMEM((1, STATE_BLOCK), F32), pltpu.VMEM((1, STATE_BLOCK), F32)],
        aliases={11: 0}, args=(dy_direct, dyg, y0, uz, sr, si, bblk, cblk, ar, ai, dskip, dp),
        semantics=("parallel", "arbitrary"), job=job)


def s5_tables(lam_re, lam_im, log_dt, b_re, b_im, c_re, c_im):
    dt = jnp.exp(log_dt)[:, None]
    mag = jnp.exp(lam_re * dt)
    ar = mag * jnp.cos(lam_im * dt)
    ai = mag * jnp.sin(lam_im * dt)
    den = lam_re * lam_re + lam_im * lam_im
    kr = ((ar - 1.0) * lam_re + ai * lam_im) / den
    ki = (ai * lam_re - (ar - 1.0) * lam_im) / den
    bbr = kr[..., None] * b_re - ki[..., None] * b_im
    bbi = kr[..., None] * b_im + ki[..., None] * b_re
    eye = jnp.eye(GROUPS_PER_BLOCK, dtype=F32)

    def b_blocks(v):
        v = v.reshape(S5_BLOCKS, GROUPS_PER_BLOCK, S5_STATE, S5_GROUP)
        return jnp.einsum('bgpi,gh->bgihp', v, eye).reshape(S5_BLOCKS, LANE_BLOCK, STATE_BLOCK)

    def c_blocks(v):
        v = v.reshape(S5_BLOCKS, GROUPS_PER_BLOCK, S5_GROUP, S5_STATE)
        return jnp.einsum('bgip,gh->bgphi', v, eye).reshape(S5_BLOCKS, STATE_BLOCK, LANE_BLOCK)

    bblk = jnp.concatenate([b_blocks(bbr), b_blocks(bbi)], axis=2)
    cblk = jnp.concatenate([c_blocks(c_re), -c_blocks(c_im)], axis=1)
    nst = S5_GROUPS * S5_STATE
    return ar.reshape(1, nst), ai.reshape(1, nst), bblk, cblk


def _silu_grad(z):
    s = jax.nn.sigmoid(z)
    return s * (1.0 + z * (1.0 - s))


def conv_fwd(p, conv_w, conv_b, *, name):
    lp = p.shape[0]
    tr = lp // 16
    e = CONV_E
    hb = tr // 8

    def body(p_ref, cgh_ref, vh_ref, w_ref, b_ref, o_ref):
        i = pl.program_id(0)
        bg, cg, v, z = (p_ref[:, k * e:(k + 1) * e] for k in range(4))
        hc = cg * v
        halo = jnp.where(i > 0, cgh_ref[...] * vh_ref[...], 0.0)
        ext = jnp.concatenate([halo, hc], axis=0)
        w = w_ref[...]
        conv = (w[0:1] * pltpu.roll(ext, 2, 0)[8:] + w[1:2] * pltpu.roll(ext, 1, 0)[8:] + w[2:3] * hc
                + b_ref[...])
        o_ref[...] = (bg * conv * jax.nn.silu(z)).astype(o_ref.dtype)

    prev = lambda col: (lambda i: (jnp.maximum(i * hb - 1, 0), col))
    return pl.pallas_call(
        body, name=name, grid=(lp // tr,),
        in_specs=[pl.BlockSpec((tr, 4 * e), lambda i: (i, 0)),
                  pl.BlockSpec((8, e), prev(1)), pl.BlockSpec((8, e), prev(2)),
                  pl.BlockSpec((3, e), lambda i: (0, 0)), pl.BlockSpec((1, e), lambda i: (0, 0))],
        out_specs=pl.BlockSpec((tr, e), lambda i: (i, 0)),
        out_shape=jax.ShapeDtypeStruct((lp, e), BF16),
        compiler_params=_params("parallel"))(p, p, p, conv_w, conv_b)


def conv_bwd(dy3, p, conv_w, conv_b, *, name):
    lp = p.shape[0]
    tr = lp // 16
    e = CONV_E
    hb = tr // 8
    nt = lp // tr
    width = 512

    def body(dy_ref, p_ref, cgh_ref, vh_ref, dyn_ref, bgn_ref, zn_ref, w_ref, b_ref, dp_ref, dw_ref, db_ref):
        i = pl.program_id(0)
        for c0 in range(0, e, width):
            cols = slice(c0, c0 + width)
            dy = dy_ref[:, cols]
            bg, cg, v, z = (p_ref[:, k * e + c0:k * e + c0 + width] for k in range(4))
            w = w_ref[:, cols]
            hc = cg * v
            halo = jnp.where(i > 0, cgh_ref[:, cols] * vh_ref[:, cols], 0.0)
            ext = jnp.concatenate([halo, hc], axis=0)
            hc2, hc1 = pltpu.roll(ext, 2, 0)[8:], pltpu.roll(ext, 1, 0)[8:]
            yc = w[0:1] * hc2 + w[1:2] * hc1 + w[2:3] * hc + b_ref[:, cols]
            sz = jax.nn.silu(z)
            dyc = dy * bg * sz
            dyc_n = jnp.where(i < nt - 1, dyn_ref[:, cols] * bgn_ref[:, cols] * jax.nn.silu(zn_ref[:, cols]), 0.0)
            ext_n = jnp.concatenate([dyc, dyc_n], axis=0)
            n_rows = tr + 8
            dhc = (w[2:3] * dyc + w[1:2] * pltpu.roll(ext_n, n_rows - 1, 0)[:tr]
                   + w[0:1] * pltpu.roll(ext_n, n_rows - 2, 0)[:tr])
            for k, val in enumerate((dy * yc * sz, dhc * v, dhc * cg, dy * bg * yc * _silu_grad(z))):
                dp_ref[:, k * e + c0:k * e + c0 + width] = val.astype(dp_ref.dtype)
            dw = jnp.concatenate([_colsum(dyc * hc2), _colsum(dyc * hc1), _colsum(dyc * hc)], axis=0)
            db = _colsum(dyc)

            @pl.when(i == 0)
            def _(cols=cols, dw=dw, db=db):
                dw_ref[:, cols] = dw
                db_ref[:, cols] = db

            @pl.when(i > 0)
            def _(cols=cols, dw=dw, db=db):
                dw_ref[:, cols] += dw
                db_ref[:, cols] += db

    prev = lambda col: (lambda i: (jnp.maximum(i * hb - 1, 0), col))
    nxt = lambda col: (lambda i: (jnp.minimum((i + 1) * hb, lp // 8 - 1), col))
    return pl.pallas_call(
        body, name=name, grid=(nt,),
        in_specs=[pl.BlockSpec((tr, e), lambda i: (i, 0)), pl.BlockSpec((tr, 4 * e), lambda i: (i, 0)),
                  pl.BlockSpec((8, e), prev(1)), pl.BlockSpec((8, e), prev(2)),
                  pl.BlockSpec((8, e), nxt(0)), pl.BlockSpec((8, e), nxt(0)), pl.BlockSpec((8, e), nxt(3)),
                  pl.BlockSpec((3, e), lambda i: (0, 0)), pl.BlockSpec((1, e), lambda i: (0, 0))],
        out_specs=[pl.BlockSpec((tr, 4 * e), lambda i: (i, 0)), pl.BlockSpec((3, e), lambda i: (0, 0)),
                   pl.BlockSpec((1, e), lambda i: (0, 0))],
        out_shape=[jax.ShapeDtypeStruct((lp, 4 * e), BF16), jax.ShapeDtypeStruct((3, e), F32),
                   jax.ShapeDtypeStruct((1, e), F32)],
        compiler_params=_params("arbitrary"))(dy3, p, p, p, dy3, p, p, conv_w, conv_b)


def _pool_counts(i, tr, rows, offset, w):
    t = (i * tr + offset + 1 + lax.broadcasted_iota(jnp.int32, (rows, 1), 0)).astype(F32)
    return jnp.minimum(t, float(w))


def pool_fwd(p, *, name):
    lp = p.shape[0]
    tr = lp // 16
    h = POOL_HALO
    hb = tr // h

    def body(u_ref, uh_ref, o_ref):
        i = pl.program_id(0)
        u = u_ref[...]
        ext = jnp.concatenate([jnp.where(i > 0, uh_ref[...], 0.0), u], axis=0)
        for k, w in enumerate(POOL_WINDOWS):
            cols = slice(k * POOL_GROUP, (k + 1) * POOL_GROUP)
            s = ext[:, cols]
            d = 1
            while d < w:
                s = s + pltpu.roll(s, d, 0)
                d *= 2
            o_ref[:, cols] = (s[h:] / _pool_counts(i, tr, tr, 0, w) - u[:, cols]).astype(o_ref.dtype)

    return pl.pallas_call(
        body, name=name, grid=(lp // tr,),
        in_specs=[pl.BlockSpec((tr, POOL_E), lambda i: (i, 0)),
                  pl.BlockSpec((h, POOL_E), lambda i: (jnp.maximum(i * hb - 1, 0), 0))],
        out_specs=pl.BlockSpec((tr, POOL_E), lambda i: (i, 0)),
        out_shape=jax.ShapeDtypeStruct((lp, POOL_E), BF16),
        compiler_params=_params("parallel"))(p, p)


def pool_bwd(dmix, dp, *, name):
    lp = dmix.shape[0]
    tr = lp // 16
    h = POOL_HALO
    hb = tr // h
    nt = lp // tr

    def body(dm_ref, dmn_ref, _, du_ref):
        i = pl.program_id(0)
        dm = dm_ref[...]
        dmn = jnp.where(i < nt - 1, dmn_ref[...], 0.0)
        n_rows = tr + h
        for k, w in enumerate(POOL_WINDOWS):
            cols = slice(k * POOL_GROUP, (k + 1) * POOL_GROUP)
            s = jnp.concatenate([dm[:, cols] / _pool_counts(i, tr, tr, 0, w),
                                 dmn[:, cols] / _pool_counts(i, tr, h, tr, w)], axis=0)
            d = 1
            while d < w:
                s = s + pltpu.roll(s, n_rows - d, 0)
                d *= 2
            du_ref[:, cols] = (s[:tr] - dm[:, cols]).astype(du_ref.dtype)

    return pl.pallas_call(
        body, name=name, grid=(nt,),
        in_specs=[pl.BlockSpec((tr, POOL_E), lambda i: (i, 0)),
                  pl.BlockSpec((h, POOL_E), lambda i: (jnp.minimum((i + 1) * hb, lp // h - 1), 0)),
                  pl.BlockSpec(memory_space=pl.ANY)],
        out_specs=pl.BlockSpec((tr, POOL_E), lambda i: (i, 0)),
        out_shape=jax.ShapeDtypeStruct(dp.shape, dp.dtype),
        input_output_aliases={2: 0},
        compiler_params=_params("parallel"))(dmix, dmix, dp)


def adamw(w, g, m, v, *, name):
    r, c = w.shape
    tr = _tile(r, max(8, (1 << 19) // c), 8)

    def body(w_ref, g_ref, m_ref, v_ref, d_ref, nm_ref, nv_ref):
        gv = g_ref[...]
        nm = ADAM_B1 * m_ref[...] + (1.0 - ADAM_B1) * gv
        nv = ADAM_B2 * v_ref[...] + (1.0 - ADAM_B2) * jnp.square(gv)
        m_hat = nm / (1.0 - ADAM_B1 ** ADAM_STEP)
        v_hat = nv / (1.0 - ADAM_B2 ** ADAM_STEP)
        d_ref[...] = -ADAM_LR * (m_hat / (jnp.sqrt(v_hat) + ADAM_EPS) + ADAM_WD * w_ref[...])
        nm_ref[...] = nm
        nv_ref[...] = nv

    spec = pl.BlockSpec((tr, c), lambda i: (i, 0))
    return pl.pallas_call(
        body, name=name, grid=(r // tr,), in_specs=[spec] * 4, out_specs=[spec] * 3,
        out_shape=[jax.ShapeDtypeStruct((r, c), F32)] * 3,
        compiler_params=_params("parallel"))(w, g, m, v)


ANY = pl.BlockSpec(memory_space=pl.ANY)


def _place():
    x, y, c = lax.axis_index("x"), lax.axis_index("y"), lax.axis_index("c")
    return x, y, c, [(1 - x, y), (x, 1 - y), (1 - x, 1 - y)]


DMA_PIECE_BYTES = 512 * 1024


def _pieces(src, dst):
    shape = src.shape
    rows, cols = shape[-2], shape[-1]
    item = jnp.dtype(src.dtype).itemsize
    unit = 32 // item
    want = max(1, (rows * cols * item) // DMA_PIECE_BYTES)
    n = 1
    if rows % unit == 0:
        n = max(d for d in range(1, rows // unit + 1) if (rows // unit) % d == 0 and d <= want)
    size = rows // n
    out = []
    for lead in (range(shape[0]) if len(shape) == 3 else [None]):
        for i in range(n):
            sel = (pl.ds(i * size, size), slice(None)) if lead is None else (lead, pl.ds(i * size, size), slice(None))
            out.append((src.at[sel], dst.at[sel]))
    return out


def _send(src, dst, send_sem, recv_sem, peer, start=True):
    if start:
        for s, d in _pieces(src, dst):
            pltpu.make_async_remote_copy(src_ref=s, dst_ref=d, send_sem=send_sem, recv_sem=recv_sem,
                                         device_id=peer, device_id_type=MESH).start()
    return pltpu.make_async_remote_copy(src_ref=src, dst_ref=dst, send_sem=send_sem, recv_sem=recv_sem,
                                        device_id=peer, device_id_type=MESH)


def run_job(job, *, name):
    n_in, n_out = len(job.arrays), len(job.out_shapes)

    def body(*refs):
        job.fn(refs[:n_in], refs[n_in:n_in + n_out], refs[n_in + n_out:], "both")

    return pl.pallas_call(body, name=name, in_specs=[ANY] * n_in, out_specs=[ANY] * n_out,
                          out_shape=job.out_shapes, scratch_shapes=job.sems)(*job.arrays)


def gather_chips(shards, *, name):
    n = len(shards)

    def body(*refs):
        ins, outs = refs[:n], refs[n:2 * n]
        send_sems, recv_sems = refs[2 * n:]
        x, y, c, chips = _place()
        k = 2 * x + y
        copies = []
        for a in range(n):
            for j, peer in enumerate([(px, py, c) for px, py in chips] + [(x, y, 1 - c)]):
                copies.append(_send(ins[a], outs[a].at[k], send_sems.at[a, j], recv_sems.at[a, j], peer))
        for cp in copies:
            cp.wait()

    return pl.pallas_call(
        body, name=name, in_specs=[ANY] * n, out_specs=[ANY] * n,
        out_shape=[jax.ShapeDtypeStruct((N_CHIPS,) + s.shape, s.dtype) for s in shards],
        scratch_shapes=[pltpu.SemaphoreType.DMA((n, 4)), pltpu.SemaphoreType.DMA((n, 4))])(*shards)


def gather_by_halves_job(shards):
    n = len(shards)

    def fn(ins, outs, sems, phase):
        send_sems, recv_sems = sems
        x, y, c, chips = _place()
        k = 2 * x + y
        sibling = (x, y, 1 - c)
        begin = phase != "finish"
        waits, arrivals = [], []
        for a in range(n):
            half = ins[a].shape[0] // 2
            waits.append(_send(ins[a], outs[a].at[k], send_sems.at[a, 6], recv_sems.at[a, 6], sibling, begin))
            mine = pl.ds(c * half, half)
            for j, (px, py) in enumerate(chips):
                arrivals.append(_send(ins[a].at[mine, :], outs[a].at[k, mine, :], send_sems.at[a, j],
                                      recv_sems.at[a, j], (px, py, c), begin))
        if phase == "start":
            return
        i = 0
        for a in range(n):
            half = ins[a].shape[0] // 2
            mine = pl.ds(c * half, half)
            for j, (px, py) in enumerate(chips):
                arrivals[i].wait_recv()
                landed = outs[a].at[2 * px + py, mine, :]
                waits.append(_send(landed, landed, send_sems.at[a, 3 + j], recv_sems.at[a, 3 + j], sibling))
                i += 1
        for cp in arrivals:
            cp.wait_send()
        for cp in waits:
            cp.wait()

    return Job(shards, [jax.ShapeDtypeStruct((N_CHIPS,) + s.shape, s.dtype) for s in shards],
               [pltpu.SemaphoreType.DMA((n, 7)), pltpu.SemaphoreType.DMA((n, 7))], fn)


def swap_halves(grads, *, name):
    n = len(grads)

    def body(*refs):
        ins, outs = refs[:n], refs[n:2 * n]
        send_sems, recv_sems = refs[2 * n:]
        x, y, c, _ = _place()
        copies = []
        for a in range(n):
            half = ins[a].shape[1] // 2
            copies.append(_send(ins[a].at[:, pl.ds((1 - c) * half, half), :], outs[a], send_sems.at[a],
                                recv_sems.at[a], (x, y, 1 - c)))
        for cp in copies:
            cp.wait()

    return pl.pallas_call(
        body, name=name, in_specs=[ANY] * n, out_specs=[ANY] * n,
        out_shape=[jax.ShapeDtypeStruct((g.shape[0], g.shape[1] // 2, g.shape[2]), g.dtype) for g in grads],
        scratch_shapes=[pltpu.SemaphoreType.DMA((n,)), pltpu.SemaphoreType.DMA((n,))])(*grads)


def _chip_slot(s, k):
    rel = s ^ k
    return (rel >> 1) | ((rel & 1) << 1)


def sum_cores(g, theirs, ck, out_dtype, *, name):
    n, r, cols = g.shape
    half = r // 2
    tr = _tile(half, max(16, (1 << 19) // cols), 16)
    nb = half // tr

    def body(ck_ref, g_ref, t_ref, o_ref):
        o_ref[...] = (g_ref[...] + t_ref[...]).astype(o_ref.dtype)

    return pl.pallas_call(
        body, name=name,
        grid_spec=pltpu.PrefetchScalarGridSpec(
            num_scalar_prefetch=1, grid=(n, nb),
            in_specs=[pl.BlockSpec((None, tr, cols), lambda s, i, ck_ref: (s, ck_ref[0] * nb + i, 0)),
                      pl.BlockSpec((None, tr, cols), lambda s, i, ck_ref: (s, i, 0))],
            out_specs=pl.BlockSpec((None, tr, cols), lambda s, i, ck_ref: (_chip_slot(s, ck_ref[1]), i, 0))),
        out_shape=jax.ShapeDtypeStruct((n, half, cols), out_dtype),
        compiler_params=_params("parallel", "parallel"))(ck, g, theirs)


def scatter_job(parts):
    n = len(parts)

    def fn(ins, outs, sems, phase):
        send_sems, recv_sems = sems
        x, y, c, chips = _place()
        copies = []
        for a in range(n):
            for j, (px, py) in enumerate(chips):
                copies.append(_send(ins[a].at[1 + j], outs[a].at[j], send_sems.at[a, j], recv_sems.at[a, j],
                                    (px, py, c), phase != "finish"))
        if phase != "start":
            for cp in copies:
                cp.wait()

    return Job(parts, [jax.ShapeDtypeStruct((3,) + p.shape[1:], p.dtype) for p in parts],
               [pltpu.SemaphoreType.DMA((n, 3)), pltpu.SemaphoreType.DMA((n, 3))], fn)


def sum_chips(own, others, ck, *, name):
    _, r, cols = own.shape
    tr = _tile(r, max(16, (1 << 19) // cols), 16)

    def body(ck_ref, a_ref, b0_ref, b1_ref, b2_ref, o_ref):
        o_ref[...] = (a_ref[...].astype(F32) + b0_ref[...].astype(F32) + b1_ref[...].astype(F32)
                      + b2_ref[...].astype(F32))

    other = lambda j: pl.BlockSpec((None, tr, cols), lambda i, ck_ref: (j, i, 0))
    return pl.pallas_call(
        body, name=name,
        grid_spec=pltpu.PrefetchScalarGridSpec(
            num_scalar_prefetch=1, grid=(r // tr,),
            in_specs=[pl.BlockSpec((None, tr, cols), lambda i, ck_ref: (0, i, 0)), other(0), other(1), other(2)],
            out_specs=pl.BlockSpec((None, tr, cols), lambda i, ck_ref: (ck_ref[0], i, 0))),
        out_shape=jax.ShapeDtypeStruct((2, r, cols), F32),
        compiler_params=_params("parallel"))(ck, own, others, others, others)


def share_halves(joined, *, name):
    n = len(joined)

    def body(*refs):
        outs = refs[n:2 * n]
        send_sems, recv_sems = refs[2 * n:]
        x, y, c, _ = _place()
        copies = [_send(outs[a].at[c], outs[a].at[c], send_sems.at[a], recv_sems.at[a], (x, y, 1 - c))
                  for a in range(n)]
        for cp in copies:
            cp.wait()

    return pl.pallas_call(
        body, name=name, in_specs=[ANY] * n, out_specs=[ANY] * n,
        out_shape=[jax.ShapeDtypeStruct(j.shape, j.dtype) for j in joined],
        input_output_aliases={a: a for a in range(n)},
        scratch_shapes=[pltpu.SemaphoreType.DMA((n,)), pltpu.SemaphoreType.DMA((n,))])(*joined)


def _pack(arrs, rows_mult):
    flat = jnp.concatenate([a.reshape(-1) for a in arrs])
    rows = -(-flat.shape[0] // (128 * rows_mult)) * rows_mult
    return jnp.pad(flat, (0, rows * 128 - flat.shape[0])).reshape(rows, 128)


def _unpack(packed, shapes):
    flat = packed.reshape(-1)
    out, pos = [], 0
    for s in shapes:
        size = 1
        for d in s:
            size *= d
        out.append(flat[pos:pos + size].reshape(s))
        pos += size
    return out


def _column_shard(full, k, axis):
    width = full.shape[axis] // N_CHIPS
    return lax.dynamic_slice_in_dim(full, k * width, width, axis)


def kernel(x, meta_tokens, norm0_g, l0_w_in, l0_lam_re, l0_lam_im, l0_log_dt, l0_b_re, l0_b_im, l0_c_re, l0_c_im, l0_d_skip, l0_w_glu, l0_b_glu, l0_w_out, norm1_g, l1_w_in, l1_conv_w, l1_conv_b, l1_w_out, norm2_g, l2_w_in, l2_w_grp, l2_b_grp, l2_scale, l2_w_out, norm3_g, l3_w_in, l3_lam_re, l3_lam_im, l3_log_dt, l3_b_re, l3_b_im, l3_c_re, l3_c_im, l3_d_skip, l3_w_glu, l3_b_glu, l3_w_out, final_g, loss_target, m_meta_tokens, m_norm0_g, m_l0_w_in, m_l0_lam_re, m_l0_lam_im, m_l0_log_dt, m_l0_b_re, m_l0_b_im, m_l0_c_re, m_l0_c_im, m_l0_d_skip, m_l0_w_glu, m_l0_b_glu, m_l0_w_out, m_norm1_g, m_l1_w_in, m_l1_conv_w, m_l1_conv_b, m_l1_w_out, m_norm2_g, m_l2_w_in, m_l2_w_grp, m_l2_b_grp, m_l2_scale, m_l2_w_out, m_norm3_g, m_l3_w_in, m_l3_lam_re, m_l3_lam_im, m_l3_log_dt, m_l3_b_re, m_l3_b_im, m_l3_c_re, m_l3_c_im, m_l3_d_skip, m_l3_w_glu, m_l3_b_glu, m_l3_w_out, m_final_g, v_meta_tokens, v_norm0_g, v_l0_w_in, v_l0_lam_re, v_l0_lam_im, v_l0_log_dt, v_l0_b_re, v_l0_b_im, v_l0_c_re, v_l0_c_im, v_l0_d_skip, v_l0_w_glu, v_l0_b_glu, v_l0_w_out, v_norm1_g, v_l1_w_in, v_l1_conv_w, v_l1_conv_b, v_l1_w_out, v_norm2_g, v_l2_w_in, v_l2_w_grp, v_l2_b_grp, v_l2_scale, v_l2_w_out, v_norm3_g, v_l3_w_in, v_l3_lam_re, v_l3_lam_im, v_l3_log_dt, v_l3_b_re, v_l3_b_im, v_l3_c_re, v_l3_c_im, v_l3_d_skip, v_l3_w_glu, v_l3_b_glu, v_l3_w_out, v_final_g):
    given = dict(locals())
    w = {n: given[n] for n in WEIGHTS}
    mom_m = {n: given["m_" + n] for n in WEIGHTS}
    mom_v = {n: given["v_" + n] for n in WEIGHTS}
    seq = x.shape[1]
    real = N_META + seq
    lp = -(-real // SCAN_ROWS) * SCAN_ROWS
    chip = 2 * lax.axis_index("x") + lax.axis_index("y")
    row = lambda a: a.reshape(1, -1)

    shard_bf16 = {n: (w[n].reshape(-1, w[n].shape[-1]) if n == 'l2_w_grp' else w[n]).astype(BF16) for n in BIG}
    layer = lambda i: [n for n in BIG if n.startswith("l%d_" % i)]
    gather_layer = lambda i: gather_by_halves_job([shard_bf16[n] for n in layer(i)])
    full = {}

    def landed(i, arrays):
        for n, arr in zip(layer(i), arrays, strict=True):
            if n.endswith('w_in'):
                full[n] = arr
            elif n == 'l2_w_grp':
                full[n] = arr.reshape(N_CHIPS, len(POOL_WINDOWS), POOL_GROUP // N_CHIPS, POOL_GROUP)
            else:
                full[n] = arr.reshape(1, -1, arr.shape[-1])

    landed(0, run_job(gather_layer(0), name="gather_l0"))
    full.update(zip(SMALL_SHARDED, gather_chips([w[n] for n in SMALL_SHARDED], name="gather_small_shards"),
                    strict=True))
    meta_full = full['meta_tokens'].transpose(1, 0, 2).reshape(N_META, D_MODEL)
    conv_w_full = full['l1_conv_w'].transpose(1, 0, 2).reshape(3, CONV_E)
    b_grp_full = full['l2_b_grp'].transpose(1, 0, 2).reshape(len(POOL_WINDOWS), POOL_GROUP)

    h0 = jnp.concatenate([meta_full, x[0], jnp.zeros((lp - real, D_MODEL), F32)], axis=0)
    target = jnp.concatenate([jnp.zeros((N_META, D_MODEL), F32), loss_target[0],
                              jnp.zeros((lp - real, D_MODEL), F32)], axis=0)
    grads = {}
    saved = {}

    def s5_layer_fwd(i, h, gather=None):
        pre = "l%d_" % i
        tables, tables_vjp = jax.vjp(s5_tables, *[w[pre + s] for s in
                                                   ('lam_re', 'lam_im', 'log_dt', 'b_re', 'b_im', 'c_re', 'c_im')])
        ar, ai, bblk, cblk = tables
        n = rms_fwd(h, row(w["norm%d_g" % i]), name=pre + "norm")
        uz = mm_nn(n, full[pre + 'w_in'], name=pre + "in")
        res = s5_fwd(uz, bblk, cblk, ar, ai, row(w[pre + 'd_skip']), name=pre + "scan",
                     job=None if gather is None else gather_layer(gather))
        if gather is not None:
            res, arrived = res
            landed(gather, arrived)
        y0, yg, sr, si = res
        q = mm_nn(yg, full[pre + 'w_glu'], bias=row(w[pre + 'b_glu']), name=pre + "glu")
        (y3,), _ = rowwise(lambda yv, qv, zv: ([_s5_gate(yv, qv, zv)], []),
                           [(y0, D_MODEL, 0), (q, D_MODEL, 0), (uz, D_MODEL, 1)], [],
                           [(D_MODEL, BF16, D_MODEL, 0)], name=pre + "gate")
        h_new = mm_nn(y3, full[pre + 'w_out'], add=h, name=pre + "out")
        saved[i] = dict(h=h, n=n, uz=uz, y0=y0, sr=sr, si=si, yg=yg, q=q, y3=y3, tables=tables,
                        tables_vjp=tables_vjp)
        return h_new

    def s5_layer_bwd(i, dh, scatter=None):
        pre = "l%d_" % i
        s = saved[i]
        ar, ai, bblk, cblk = s['tables']
        grads[pre + 'w_out'] = mm_tn(s['y3'], dh, 1, name=pre + "d_w_out")[0]
        dy3 = mm_nt(dh, full[pre + 'w_out'], name=pre + "d_y3")

        def gate_bwd(dyv, yv, qv, zv):
            _, vjp = jax.vjp(_s5_gate, yv, qv, zv)
            dy0, dq, dz = vjp(dyv)
            return [dy0, dq, dz], [_colsum(dq)]

        (dy0_direct, dq, dp), (db_glu,) = rowwise(
            gate_bwd, [(dy3, D_MODEL, 0), (s['y0'], D_MODEL, 0), (s['q'], D_MODEL, 0), (s['uz'], D_MODEL, 1)], [],
            [(D_MODEL, F32, D_MODEL, 0), (D_MODEL, BF16, D_MODEL, 0), (2 * D_MODEL, BF16, D_MODEL, 1)], [D_MODEL],
            name=pre + "d_gate")
        grads[pre + 'b_glu'] = db_glu.reshape(-1)
        grads[pre + 'w_glu'] = mm_tn(s['yg'], dq, 1, name=pre + "d_w_glu")[0]
        dyg = mm_nt(dq, full[pre + 'w_glu'], name=pre + "d_yg")
        res = s5_bwd(dy0_direct, dyg, s['y0'], s['uz'], s['sr'], s['si'], bblk, cblk, ar, ai,
                     row(w[pre + 'd_skip']), dp, name=pre + "d_scan",
                     job=None if scatter is None else scatter_job(scatter_of(scatter)))
        if scatter is not None:
            res, arrived = res
            scattered(scatter, arrived)
        dp, dbblk, dcblk, dar, dai, dd = res
        grads[pre + 'd_skip'] = dd.reshape(-1)
        for key, val in zip(('lam_re', 'lam_im', 'log_dt', 'b_re', 'b_im', 'c_re', 'c_im'),
                            s['tables_vjp']((dar, dai, dbblk, dcblk)), strict=True):
            grads[pre + key] = val
        grads[pre + 'w_in'] = mm_tn(s['n'], dp, N_CHIPS, name=pre + "d_w_in")
        dn = mm_nt(dp, full[pre + 'w_in'], name=pre + "d_n")
        dh_in, dg = rms_bwd(s['h'], row(w["norm%d_g" % i]), dn, dh, name=pre + "d_norm")
        grads["norm%d_g" % i] = dg.reshape(-1)
        return dh_in

    h1 = s5_layer_fwd(0, h0, gather=1)

    n1 = rms_fwd(h1, row(w['norm1_g']), name="l1_norm")
    p1, arrived = mm_nn(n1, full['l1_w_in'], name="l1_in", job=gather_layer(2))
    landed(2, arrived)
    y3_1 = conv_fwd(p1, conv_w_full, row(w['l1_conv_b']), name="l1_conv")
    h2 = mm_nn(y3_1, full['l1_w_out'], add=h1, name="l1_out")

    n2 = rms_fwd(h2, row(w['norm2_g']), name="l2_norm")
    p2, arrived = mm_nn(n2, full['l2_w_in'], name="l2_in", job=gather_layer(3))
    landed(3, arrived)
    mixed = pool_fwd(p2, name="l2_pool")
    o2 = grp_nn(mixed, full['l2_w_grp'], b_grp_full.reshape(1, -1), name="l2_grp")
    (y3_2,), _ = rowwise(lambda ov, zv, sv: ([_pool_gate(ov, zv, sv)], []),
                         [(o2, POOL_E, 0), (p2, POOL_E, 1)], [row(w['l2_scale'])],
                         [(POOL_E, BF16, POOL_E, 0)], name="l2_gate")
    h3 = mm_nn(y3_2, full['l2_w_out'], add=h2, name="l2_out")

    h4 = s5_layer_fwd(3, h3)

    def loss_fn(hv, tv, gv):
        i = pl.program_id(0)
        tr = hv.shape[0]
        yf, vjp = jax.vjp(_rms, hv, gv)
        pos = i * tr + lax.broadcasted_iota(jnp.int32, (tr, 1), 0)
        diff = jnp.where((pos >= N_META) & (pos < real), yf - tv, 0.0)
        dh, dg = vjp(diff / D_MODEL)
        return [dh], [dg, _colsum(diff * diff)]

    (dh,), (d_final_g, sq) = rowwise(loss_fn, [(h4, D_MODEL, 0), (target, D_MODEL, 0)], [row(w['final_g'])],
                                     [(D_MODEL, F32, D_MODEL, 0)], [D_MODEL, D_MODEL], name="loss")
    grads['final_g'] = d_final_g.reshape(-1)
    loss = lax.psum(0.5 / D_MODEL * jnp.sum(sq), ("x", "y", "c"))

    ck = jnp.stack([lax.axis_index("c"), chip]).astype(jnp.int32)
    pair, from_chips = {}, {}

    def chip_major(n):
        g = grads[n]
        if n.endswith('w_in'):
            return g
        if n == 'l2_w_grp':
            return g.reshape(N_CHIPS, -1, POOL_GROUP)
        return g.reshape(N_CHIPS, -1, g.shape[-1])

    def reduce_cores(i, small=None):
        names, pieces, wire = layer(i), [chip_major(n) for n in layer(i)], [BF16] * len(layer(i))
        if small is not None:
            names, pieces, wire = names + ['small'], pieces + [small], wire + [F32]
        theirs = swap_halves(pieces, name="reduce_cores_l%d" % i)
        for n, g, t, dt in zip(names, pieces, theirs, wire, strict=True):
            pair[n] = sum_cores(g, t, ck, dt, name="sum_cores_" + n)

    scatter_of = lambda i: [pair[n] for n in layer(i)]

    def scattered(i, arrays):
        from_chips.update(zip(layer(i), arrays, strict=True))

    dh = s5_layer_bwd(3, dh)
    reduce_cores(3)

    grads['l2_w_out'] = mm_tn(y3_2, dh, 1, name="l2_d_w_out")[0]
    dy3 = mm_nt(dh, full['l2_w_out'], name="l2_d_y3")

    def pool_gate_bwd(dyv, ov, zv, sv):
        _, vjp = jax.vjp(_pool_gate, ov, zv, sv)
        do, dz, dscale = vjp(dyv)
        return [do, dz], [dscale, _colsum(do)]

    (do2, dp2), (dscale, dbgrp) = rowwise(
        pool_gate_bwd, [(dy3, POOL_E, 0), (o2, POOL_E, 0), (p2, POOL_E, 1)], [row(w['l2_scale'])],
        [(POOL_E, BF16, POOL_E, 0), (2 * POOL_E, BF16, POOL_E, 1)], [POOL_E, POOL_E], name="l2_d_gate")
    grads['l2_scale'] = dscale.reshape(-1)
    grads['l2_b_grp'] = dbgrp.reshape(len(POOL_WINDOWS), POOL_GROUP)
    grads['l2_w_grp'] = grp_tn(mixed, do2, name="l2_d_w_grp")
    dmix = grp_nt(do2, full['l2_w_grp'], name="l2_d_mixed")
    dp2 = pool_bwd(dmix, dp2, name="l2_d_pool")
    grads['l2_w_in'], arrived = mm_tn(n2, dp2, N_CHIPS, name="l2_d_w_in", job=scatter_job(scatter_of(3)))
    scattered(3, arrived)
    dn = mm_nt(dp2, full['l2_w_in'], name="l2_d_n")
    dh, dg = rms_bwd(h2, row(w['norm2_g']), dn, dh, name="l2_d_norm")
    grads['norm2_g'] = dg.reshape(-1)
    reduce_cores(2)

    grads['l1_w_out'] = mm_tn(y3_1, dh, 1, name="l1_d_w_out")[0]
    dy3 = mm_nt(dh, full['l1_w_out'], name="l1_d_y3")
    dp1, dcw, dcb = conv_bwd(dy3, p1, conv_w_full, row(w['l1_conv_b']), name="l1_d_conv")
    grads['l1_conv_w'] = dcw
    grads['l1_conv_b'] = dcb.reshape(-1)
    grads['l1_w_in'], arrived = mm_tn(n1, dp1, N_CHIPS, name="l1_d_w_in", job=scatter_job(scatter_of(2)))
    scattered(2, arrived)
    dn = mm_nt(dp1, full['l1_w_in'], name="l1_d_n")
    dh, dg = rms_bwd(h1, row(w['norm1_g']), dn, dh, name="l1_d_norm")
    grads['norm1_g'] = dg.reshape(-1)
    reduce_cores(1)

    dh = s5_layer_bwd(0, dh, scatter=1)
    grad_x = dh[N_META:real][None]
    grads['meta_tokens'] = dh[:N_META]

    small_local = _pack([grads[n] for n in SMALL], 32 * N_CHIPS)
    reduce_cores(0, small_local.reshape(N_CHIPS, -1, 128))
    rest = layer(0) + ['small']
    from_chips.update(zip(rest, run_job(scatter_job([pair[n] for n in rest]), name="reduce_chips_l0"), strict=True))
    joined = [sum_chips(pair[n], from_chips[n], ck, name="sum_chips_" + n) for n in BIG + ['small']]
    joined = share_halves(joined, name="share_cores")
    reduced = [j.reshape(-1, j.shape[-1]) for j in joined]
    g_big = {n: reduced[i].reshape(w[n].shape) for i, n in enumerate(BIG)}
    (small_all,) = gather_chips([reduced[-1]], name="gather_small")
    g_small_full = dict(zip(SMALL, _unpack(small_all, [grads[n].shape for n in SMALL]), strict=True))
    g_small = {}
    for n in SMALL:
        g = g_small_full[n]
        g_small[n] = _column_shard(g, chip, g.ndim - 1) if n in SMALL_SHARDED else g

    out_g, out_d, out_m, out_v = {}, {}, {}, {}
    for n in BIG:
        shape = w[n].shape
        as2d = lambda a: a.reshape(-1, shape[-1])
        d, nm, nv = adamw(as2d(w[n]), as2d(g_big[n]), as2d(mom_m[n]), as2d(mom_v[n]), name="adamw_" + n)
        out_g[n], out_d[n], out_m[n], out_v[n] = g_big[n], d.reshape(shape), nm.reshape(shape), nv.reshape(shape)
    packs = [_pack([src[n] for n in SMALL], 8) for src in (w, g_small, mom_m, mom_v)]
    d, nm, nv = adamw(*packs, name="adamw_small")
    shapes = [w[n].shape for n in SMALL]
    for n, dv, mv, vv in zip(SMALL, _unpack(d, shapes), _unpack(nm, shapes), _unpack(nv, shapes), strict=True):
        out_g[n], out_d[n], out_m[n], out_v[n] = g_small[n], dv, mv, vv

    return (loss, grad_x, *[out_g[n] for n in WEIGHTS], *[out_d[n] for n in WEIGHTS],
            *[out_m[n] for n in WEIGHTS], *[out_v[n] for n in WEIGHTS])
```

```python
import functools

import jax
import jax.numpy as jnp
from jax import lax
from jax.experimental import pallas as pl
from jax.experimental.pallas import tpu as pltpu

F32, BF16 = jnp.float32, jnp.bfloat16
MESH = pl.DeviceIdType.MESH

D_MODEL = 1024
N_META = 16
EPS = 1e-6
S5_GROUPS, S5_GROUP, S5_STATE = 64, 16, 64
LANE_BLOCK = 128
S5_BLOCKS = D_MODEL // LANE_BLOCK
GROUPS_PER_BLOCK = LANE_BLOCK // S5_GROUP
STATE_BLOCK = GROUPS_PER_BLOCK * S5_STATE
SCAN_ROWS = 256
SUBLANES = 8
CONV_E = 2048
POOL_E = 2048
POOL_WINDOWS = (2, 4, 8, 16)
POOL_GROUP = 512
POOL_HALO = 16
N_CHIPS = 4

ADAM_LR, ADAM_B1, ADAM_B2, ADAM_EPS, ADAM_WD, ADAM_STEP = 0.001, 0.9, 0.999, 1e-08, 0.01, 10

VMEM_LIMIT = 48 * 1024 * 1024
TN_OPERAND_BYTES = 28 * 1024 * 1024

WEIGHTS = ['meta_tokens', 'norm0_g', 'l0_w_in', 'l0_lam_re', 'l0_lam_im', 'l0_log_dt', 'l0_b_re', 'l0_b_im',
           'l0_c_re', 'l0_c_im', 'l0_d_skip', 'l0_w_glu', 'l0_b_glu', 'l0_w_out', 'norm1_g', 'l1_w_in',
           'l1_conv_w', 'l1_conv_b', 'l1_w_out', 'norm2_g', 'l2_w_in', 'l2_w_grp', 'l2_b_grp', 'l2_scale',
           'l2_w_out', 'norm3_g', 'l3_w_in', 'l3_lam_re', 'l3_lam_im', 'l3_log_dt', 'l3_b_re', 'l3_b_im',
           'l3_c_re', 'l3_c_im', 'l3_d_skip', 'l3_w_glu', 'l3_b_glu', 'l3_w_out', 'final_g']
BIG = ['l0_w_in', 'l0_w_glu', 'l0_w_out', 'l1_w_in', 'l1_w_out', 'l2_w_in', 'l2_w_grp', 'l2_w_out',
       'l3_w_in', 'l3_w_glu', 'l3_w_out']
SMALL_SHARDED = ['meta_tokens', 'l1_conv_w', 'l2_b_grp']
SMALL = [n for n in WEIGHTS if n not in BIG]


def _params(*sem):
    return pltpu.CompilerParams(dimension_semantics=sem, vmem_limit_bytes=VMEM_LIMIT)


class Job:
    def __init__(self, arrays, out_shapes, sems, fn):
        self.arrays, self.out_shapes, self.sems, self.fn = list(arrays), list(out_shapes), list(sems), fn


def _call(body, *, name, grid, in_specs, out_specs, out_shape, args, semantics, scratch=(), aliases=None, job=None):
    if job is None:
        return pl.pallas_call(
            body, name=name, grid=grid, in_specs=list(in_specs), out_specs=list(out_specs),
            out_shape=list(out_shape), scratch_shapes=list(scratch), input_output_aliases=aliases or {},
            compiler_params=_params(*semantics))(*args)
    counts = [len(in_specs), len(job.arrays), len(out_specs), len(job.out_shapes), len(scratch), len(job.sems)]

    def hosted(*refs):
        parts, pos = [], 0
        for k in counts:
            parts.append(refs[pos:pos + k])
            pos += k
        ins, job_ins, outs, job_outs, scr, job_sems = parts
        steps = [pl.program_id(d) for d in range(len(grid))]
        first = functools.reduce(jnp.logical_and, [s == 0 for s in steps])
        last = functools.reduce(jnp.logical_and, [s == g - 1 for s, g in zip(steps, grid, strict=True)])

        @pl.when(first)
        def _():
            job.fn(job_ins, job_outs, job_sems, "start")

        body(*ins, *outs, *scr)

        @pl.when(last)
        def _():
            job.fn(job_ins, job_outs, job_sems, "finish")

    any_spec = pl.BlockSpec(memory_space=pl.ANY)
    res = pl.pallas_call(
        hosted, name=name, grid=grid, in_specs=list(in_specs) + [any_spec] * len(job.arrays),
        out_specs=list(out_specs) + [any_spec] * len(job.out_shapes),
        out_shape=list(out_shape) + job.out_shapes, scratch_shapes=list(scratch) + job.sems,
        input_output_aliases=aliases or {},
        compiler_params=_params(*["arbitrary"] * len(grid)))(*args, *job.arrays)
    return res[:len(out_specs)], res[len(out_specs):]


def _tile(n, cap, mult):
    best = None
    for t in range(mult, min(n, cap) + 1, mult):
        if n % t == 0:
            best = t
    assert best is not None, (n, cap, mult)
    return best


def _with_job(res, job):
    return res[0] if job is None else (res[0][0], res[1])


def mm_nn(a, b, *, name, bias=None, add=None, out_dtype=F32, job=None):
    m, k = a.shape
    s, _, ns = b.shape
    n = s * ns
    tm, tn = _tile(m, 1088, 16), _tile(ns, 512, 128)
    per = ns // tn

    def body(*refs):
        a_ref, b_ref = refs[0], refs[1]
        o_ref = refs[-1]
        acc = jnp.dot(a_ref[...].astype(BF16), b_ref[...], preferred_element_type=F32)
        pos = 2
        if bias is not None:
            acc = acc + refs[pos][...]
            pos += 1
        if add is not None:
            acc = acc + refs[pos][...]
        o_ref[...] = acc.astype(o_ref.dtype)

    in_specs = [pl.BlockSpec((tm, k), lambda i, j: (i, 0)),
                pl.BlockSpec((None, k, tn), lambda i, j: (j // per, 0, j % per))]
    args = [a, b]
    if bias is not None:
        in_specs.append(pl.BlockSpec((1, tn), lambda i, j: (0, j)))
        args.append(bias)
    if add is not None:
        in_specs.append(pl.BlockSpec((tm, tn), lambda i, j: (i, j)))
        args.append(add)
    return _with_job(_call(
        body, name=name, grid=(m // tm, n // tn), in_specs=in_specs,
        out_specs=[pl.BlockSpec((tm, tn), lambda i, j: (i, j))],
        out_shape=[jax.ShapeDtypeStruct((m, n), out_dtype)], args=args,
        semantics=("parallel", "parallel"), job=job), job)


def mm_nt(g, w, *, name, job=None):
    m, kc = g.shape
    s, nout, kcs = w.shape
    assert s * kcs == kc
    tm, tno, tk = _tile(m, 1088, 16), _tile(nout, 1024, 128), _tile(kcs, 1024, 128)
    per = kcs // tk

    def body(g_ref, w_ref, o_ref):
        kk = pl.program_id(2)
        part = lax.dot_general(g_ref[...].astype(BF16), w_ref[...], (((1,), (1,)), ((), ())),
                               preferred_element_type=F32)

        @pl.when(kk == 0)
        def _():
            o_ref[...] = part

        @pl.when(kk > 0)
        def _():
            o_ref[...] += part

    return _with_job(_call(
        body, name=name, grid=(m // tm, nout // tno, kc // tk),
        in_specs=[pl.BlockSpec((tm, tk), lambda i, j, kk: (i, kk)),
                  pl.BlockSpec((None, tno, tk), lambda i, j, kk: (kk // per, j, kk % per))],
        out_specs=[pl.BlockSpec((tm, tno), lambda i, j, kk: (i, j))],
        out_shape=[jax.ShapeDtypeStruct((m, nout), F32)], args=(g, w),
        semantics=("parallel", "parallel", "arbitrary"), job=job), job)


def mm_tn(a, g, shards, *, name, job=None):
    m, ka = a.shape
    n = g.shape[1]
    ns = n // shards
    tka, tn = _tile(ka, 1024, 128), _tile(ns, 512, 128)
    row_bytes = tka * jnp.dtype(a.dtype).itemsize + tn * jnp.dtype(g.dtype).itemsize
    tm = _tile(m, TN_OPERAND_BYTES // (2 * row_bytes), 16)
    per = ns // tn

    def body(a_ref, g_ref, o_ref):
        mm = pl.program_id(2)
        part = lax.dot_general(a_ref[...].astype(BF16), g_ref[...].astype(BF16), (((0,), (0,)), ((), ())),
                               preferred_element_type=F32)

        @pl.when(mm == 0)
        def _():
            o_ref[...] = part

        @pl.when(mm > 0)
        def _():
            o_ref[...] += part

    return _with_job(_call(
        body, name=name, grid=(ka // tka, n // tn, m // tm),
        in_specs=[pl.BlockSpec((tm, tka), lambda i, j, mm: (mm, i)),
                  pl.BlockSpec((tm, tn), lambda i, j, mm: (mm, j))],
        out_specs=[pl.BlockSpec((None, tka, tn), lambda i, j, mm: (j // per, i, j % per))],
        out_shape=[jax.ShapeDtypeStruct((shards, ka, ns), F32)], args=(a, g),
        semantics=("parallel", "parallel", "arbitrary"), job=job), job)


def grp_nn(a, w, bias, *, name):
    m = a.shape[0]
    tm = _tile(m, 1088, 16)
    ng = len(POOL_WINDOWS)

    def body(a_ref, w_ref, b_ref, o_ref):
        wj = w_ref[...].reshape(POOL_GROUP, POOL_GROUP)
        o_ref[...] = jnp.dot(a_ref[...].astype(BF16), wj, preferred_element_type=F32) + b_ref[...]

    return pl.pallas_call(
        body, name=name, grid=(m // tm, ng),
        in_specs=[pl.BlockSpec((tm, POOL_GROUP), lambda i, j: (i, j)),
                  pl.BlockSpec((N_CHIPS, None, POOL_GROUP // N_CHIPS, POOL_GROUP), lambda i, j: (0, j, 0, 0)),
                  pl.BlockSpec((1, POOL_GROUP), lambda i, j: (0, j))],
        out_specs=pl.BlockSpec((tm, POOL_GROUP), lambda i, j: (i, j)),
        out_shape=jax.ShapeDtypeStruct((m, ng * POOL_GROUP), F32),
        compiler_params=_params("parallel", "parallel"))(a, w, bias)


def grp_nt(g, w, *, name):
    m = g.shape[0]
    tm = _tile(m, 1088, 16)
    ng = len(POOL_WINDOWS)

    def body(g_ref, w_ref, o_ref):
        wj = w_ref[...].reshape(POOL_GROUP, POOL_GROUP)
        o_ref[...] = lax.dot_general(g_ref[...].astype(BF16), wj, (((1,), (1,)), ((), ())),
                                     preferred_element_type=F32)

    return pl.pallas_call(
        body, name=name, grid=(m // tm, ng),
        in_specs=[pl.BlockSpec((tm, POOL_GROUP), lambda i, j: (i, j)),
                  pl.BlockSpec((N_CHIPS, None, POOL_GROUP // N_CHIPS, POOL_GROUP), lambda i, j: (0, j, 0, 0))],
        out_specs=pl.BlockSpec((tm, POOL_GROUP), lambda i, j: (i, j)),
        out_shape=jax.ShapeDtypeStruct((m, ng * POOL_GROUP), F32),
        compiler_params=_params("parallel", "parallel"))(g, w)


def grp_tn(a, g, *, name):
    m = a.shape[0]
    tm = _tile(m, 1088, 16)
    ng = len(POOL_WINDOWS)
    rows = POOL_GROUP // N_CHIPS

    def body(a_ref, g_ref, o_ref):
        mm = pl.program_id(1)
        part = lax.dot_general(a_ref[...].astype(BF16), g_ref[...].astype(BF16), (((0,), (0,)), ((), ())),
                               preferred_element_type=F32).reshape(N_CHIPS, rows, POOL_GROUP)

        @pl.when(mm == 0)
        def _():
            o_ref[...] = part

        @pl.when(mm > 0)
        def _():
            o_ref[...] += part

    return pl.pallas_call(
        body, name=name, grid=(ng, m // tm),
        in_specs=[pl.BlockSpec((tm, POOL_GROUP), lambda j, mm: (mm, j)),
                  pl.BlockSpec((tm, POOL_GROUP), lambda j, mm: (mm, j))],
        out_specs=pl.BlockSpec((N_CHIPS, None, rows, POOL_GROUP), lambda j, mm: (0, j, 0, 0)),
        out_shape=jax.ShapeDtypeStruct((N_CHIPS, ng, rows, POOL_GROUP), F32),
        compiler_params=_params("parallel", "arbitrary"))(a, g)


def rowwise(fn, rows, bcast, outs, accs=(), *, name, aliases=None):
    m = rows[0][0].shape[0]
    tr = m // 16
    n_in = len(rows) + len(bcast)

    def body(*refs):
        vals = [r[...] for r in refs[:n_in]]
        o_vals, a_vals = fn(*vals)
        for ref, val in zip(refs[n_in:n_in + len(outs)], o_vals, strict=True):
            ref[...] = val.astype(ref.dtype)
        step = pl.program_id(0)
        for ref, val in zip(refs[n_in + len(outs):], a_vals, strict=True):
            @pl.when(step == 0)
            def _(ref=ref, val=val):
                ref[...] = val

            @pl.when(step > 0)
            def _(ref=ref, val=val):
                ref[...] += val

    in_specs = [pl.BlockSpec((tr, w), functools.partial(lambda i, cb: (i, cb), cb=cb)) for _, w, cb in rows]
    in_specs += [pl.BlockSpec(b.shape, lambda i: (0, 0)) for b in bcast]
    out_specs = [pl.BlockSpec((tr, w), functools.partial(lambda i, cb: (i, cb), cb=cb)) for _, _, w, cb in outs]
    out_specs += [pl.BlockSpec((1, w), lambda i: (0, 0)) for w in accs]
    out_shape = [jax.ShapeDtypeStruct((m, cols), dt) for cols, dt, _, _ in outs]
    out_shape += [jax.ShapeDtypeStruct((1, w), F32) for w in accs]
    res = pl.pallas_call(
        body, name=name, grid=(m // tr,), in_specs=in_specs, out_specs=out_specs, out_shape=out_shape,
        input_output_aliases=aliases or {},
        compiler_params=_params("arbitrary"))(*[r[0] for r in rows], *bcast)
    return res[:len(outs)], res[len(outs):]


def _rms(h, g):
    return h * lax.rsqrt(jnp.mean(h * h, axis=-1, keepdims=True) + EPS) * g


def _colsum(v):
    return jnp.sum(v, axis=0, keepdims=True)


def rms_fwd(h, g, *, name):
    (n,), _ = rowwise(lambda hv, gv: ([_rms(hv, gv)], []), [(h, D_MODEL, 0)], [g],
                      [(D_MODEL, BF16, D_MODEL, 0)], name=name)
    return n


def rms_bwd(h, g, dn, dh_out, *, name):
    def fn(hv, dnv, dhv, gv):
        _, vjp = jax.vjp(_rms, hv, gv)
        dh, dg = vjp(dnv)
        return [dhv + dh], [dg]

    (dh,), (dg,) = rowwise(fn, [(h, D_MODEL, 0), (dn, D_MODEL, 0), (dh_out, D_MODEL, 0)], [g],
                           [(D_MODEL, F32, D_MODEL, 0)], [D_MODEL], name=name)
    return dh, dg


def _s5_gate(y0, q, z):
    yg = jax.nn.gelu(y0)
    return yg * jax.nn.sigmoid(q) * jax.nn.silu(z)


def _pool_gate(o, z, scale):
    return o * scale * jax.nn.silu(z)


def _shift_rows(v, d, down):
    t = v.shape[0]
    row = lax.broadcasted_iota(jnp.int32, v.shape, 0)
    if down:
        return jnp.where(row >= d, pltpu.roll(v, d, 0), 0.0)
    return jnp.where(row < t - d, pltpu.roll(v, t - d, 0), 0.0)


def _cmul(ar, ai, br, bi):
    return ar * br - ai * bi, ar * bi + ai * br


def _scan(xr, xi, ar, ai, cr, ci, down):
    t, n = xr.shape
    nb = t // SUBLANES
    pw = [(ar, ai)]
    for _ in range(SUBLANES - 1):
        pw.append(_cmul(*pw[-1], ar, ai))
    sub = lax.broadcasted_iota(jnp.int32, (SUBLANES, n), 0)

    def table(exps):
        tr, ti = jnp.zeros((SUBLANES, n), F32), jnp.zeros((SUBLANES, n), F32)
        for r, e in enumerate(exps):
            if e:
                tr, ti = jnp.where(sub == r, pw[e - 1][0], tr), jnp.where(sub == r, pw[e - 1][1], ti)
        return tr[None], ti[None]

    sr, si = xr.reshape(nb, SUBLANES, n), xi.reshape(nb, SUBLANES, n)
    d = 1
    while d < SUBLANES:
        mr, mi = table([d if (r >= d if down else r < SUBLANES - d) else 0 for r in range(SUBLANES)])
        turn = d if down else SUBLANES - d
        hr, hi = pltpu.roll(sr, turn, 1), pltpu.roll(si, turn, 1)
        sr, si = sr + mr * hr - mi * hi, si + mr * hi + mi * hr
        d *= 2
    edge = SUBLANES - 1 if down else 0
    qr, qi = table([r + 1 if down else SUBLANES - r for r in range(SUBLANES)])
    qr, qi = qr[0], qi[0]
    in_r, in_i = jnp.broadcast_to(cr, (SUBLANES, n)), jnp.broadcast_to(ci, (SUBLANES, n))
    out_r, out_i = [None] * nb, [None] * nb
    for b in (range(nb) if down else reversed(range(nb))):
        out_r[b] = sr[b] + qr * in_r - qi * in_i
        out_i[b] = si[b] + qr * in_i + qi * in_r
        in_r = jnp.broadcast_to(out_r[b][edge:edge + 1, :], (SUBLANES, n))
        in_i = jnp.broadcast_to(out_i[b][edge:edge + 1, :], (SUBLANES, n))
    return jnp.concatenate(out_r, axis=0), jnp.concatenate(out_i, axis=0), in_r[0:1, :], in_i[0:1, :]


def s5_fwd(uz, bblk, cblk, ar, ai, dskip, *, name, job=None):
    lp = uz.shape[0]
    t = SCAN_ROWS
    nst = S5_GROUPS * S5_STATE

    def body(u_ref, b_ref, c_ref, ar_ref, ai_ref, d_ref, y_ref, yg_ref, sr_ref, si_ref, cr, ci):
        step = pl.program_id(1)

        @pl.when(step == 0)
        def _():
            cr[...] = jnp.zeros_like(cr)
            ci[...] = jnp.zeros_like(ci)

        u = u_ref[...]
        a_r, a_i = ar_ref[...], ai_ref[...]
        x = jnp.dot(u.astype(BF16), b_ref[...].astype(BF16), preferred_element_type=F32)
        sr, si, cr[...], ci[...] = _scan(x[:, :STATE_BLOCK], x[:, STATE_BLOCK:], a_r, a_i, cr[...], ci[...], True)
        sr_ref[...] = sr
        si_ref[...] = si
        s = jnp.concatenate([sr, si], axis=1).astype(BF16)
        y0 = jnp.dot(s, c_ref[...].astype(BF16), preferred_element_type=F32) + d_ref[...] * u
        y_ref[...] = y0
        yg_ref[...] = jax.nn.gelu(y0).astype(yg_ref.dtype)

    return _call(
        body, name=name, grid=(S5_BLOCKS, lp // t),
        in_specs=[pl.BlockSpec((t, LANE_BLOCK), lambda b, c: (c, b)),
                  pl.BlockSpec((None, LANE_BLOCK, 2 * STATE_BLOCK), lambda b, c: (b, 0, 0)),
                  pl.BlockSpec((None, 2 * STATE_BLOCK, LANE_BLOCK), lambda b, c: (b, 0, 0)),
                  pl.BlockSpec((1, STATE_BLOCK), lambda b, c: (0, b)),
                  pl.BlockSpec((1, STATE_BLOCK), lambda b, c: (0, b)),
                  pl.BlockSpec((1, LANE_BLOCK), lambda b, c: (0, b))],
        out_specs=[pl.BlockSpec((t, LANE_BLOCK), lambda b, c: (c, b)),
                   pl.BlockSpec((t, LANE_BLOCK), lambda b, c: (c, b)),
                   pl.BlockSpec((t, STATE_BLOCK), lambda b, c: (c, b)),
                   pl.BlockSpec((t, STATE_BLOCK), lambda b, c: (c, b))],
        out_shape=[jax.ShapeDtypeStruct((lp, D_MODEL), F32), jax.ShapeDtypeStruct((lp, D_MODEL), BF16),
                   jax.ShapeDtypeStruct((lp, nst), F32), jax.ShapeDtypeStruct((lp, nst), F32)],
        scratch=[pltpu.VMEM((1, STATE_BLOCK), F32), pltpu.VMEM((1, STATE_BLOCK), F32)],
        args=(uz, bblk, cblk, ar, ai, dskip), semantics=("parallel", "arbitrary"), job=job)


def s5_bwd(dy_direct, dyg, y0, uz, sr, si, bblk, cblk, ar, ai, dskip, dp, *, name, job=None):
    lp = uz.shape[0]
    t = SCAN_ROWS
    nch = lp // t
    nst = S5_GROUPS * S5_STATE

    def body(dyd_ref, dyg_ref, y0_ref, u_ref, sr_ref, si_ref, b_ref, c_ref, ar_ref, ai_ref, d_ref, _, du_ref,
             db_ref, dc_ref, dar_ref, dai_ref, dd_ref, cr, ci):
        step = pl.program_id(1)

        @pl.when(step == 0)
        def _():
            cr[...] = jnp.zeros_like(cr)
            ci[...] = jnp.zeros_like(ci)

        dy = dyd_ref[...] + jax.vjp(jax.nn.gelu, y0_ref[...])[1](dyg_ref[...])[0]
        u = u_ref[...]
        dyb, ub = dy.astype(BF16), u.astype(BF16)
        a_r, a_i = ar_ref[...], -ai_ref[...]
        g = lax.dot_general(dyb, c_ref[...].astype(BF16), (((1,), (1,)), ((), ())), preferred_element_type=F32)
        gr, gi = g[:, :STATE_BLOCK], g[:, STATE_BLOCK:]
        nxt_r, nxt_i = cr[...], ci[...]
        last = lax.broadcasted_iota(jnp.int32, gr.shape, 0) == t - 1
        lr, li, cr[...], ci[...] = _scan(gr, gi, a_r, a_i, nxt_r, nxt_i, False)
        nr = _shift_rows(lr, 1, False) + jnp.where(last, nxt_r, 0.0)
        ni = _shift_rows(li, 1, False) + jnp.where(last, nxt_i, 0.0)
        s_r, s_i = sr_ref[...], si_ref[...]
        dar = _colsum(s_r * nr + s_i * ni)
        dai = _colsum(s_r * ni - s_i * nr)
        lam = jnp.concatenate([lr, li], axis=1).astype(BF16)
        sb = jnp.concatenate([s_r, s_i], axis=1).astype(BF16)
        dc = lax.dot_general(sb, dyb, (((0,), (0,)), ((), ())), preferred_element_type=F32)
        db = lax.dot_general(ub, lam, (((0,), (0,)), ((), ())), preferred_element_type=F32)
        du_ref[...] = (lax.dot_general(lam, b_ref[...].astype(BF16), (((1,), (1,)), ((), ())),
                                       preferred_element_type=F32) + d_ref[...] * dy).astype(du_ref.dtype)
        dd = _colsum(dy * u)

        @pl.when(step == 0)
        def _():
            db_ref[...] = db
            dc_ref[...] = dc
            dar_ref[...] = dar
            dai_ref[...] = dai
            dd_ref[...] = dd

        @pl.when(step > 0)
        def _():
            db_ref[...] += db
            dc_ref[...] += dc
            dar_ref[...] += dar
            dai_ref[...] += dai
            dd_ref[...] += dd

    rev = lambda b, c: (nch - 1 - c, b)
    return _call(
        body, name=name, grid=(S5_BLOCKS, nch),
        in_specs=[pl.BlockSpec((t, LANE_BLOCK), rev), pl.BlockSpec((t, LANE_BLOCK), rev),
                  pl.BlockSpec((t, LANE_BLOCK), rev), pl.BlockSpec((t, LANE_BLOCK), rev),
                  pl.BlockSpec((t, STATE_BLOCK), rev), pl.BlockSpec((t, STATE_BLOCK), rev),
                  pl.BlockSpec((None, LANE_BLOCK, 2 * STATE_BLOCK), lambda b, c: (b, 0, 0)),
                  pl.BlockSpec((None, 2 * STATE_BLOCK, LANE_BLOCK), lambda b, c: (b, 0, 0)),
                  pl.BlockSpec((1, STATE_BLOCK), lambda b, c: (0, b)),
                  pl.BlockSpec((1, STATE_BLOCK), lambda b, c: (0, b)),
                  pl.BlockSpec((1, LANE_BLOCK), lambda b, c: (0, b)),
                  pl.BlockSpec(memory_space=pl.ANY)],
        out_specs=[pl.BlockSpec((t, LANE_BLOCK), rev),
                   pl.BlockSpec((None, LANE_BLOCK, 2 * STATE_BLOCK), lambda b, c: (b, 0, 0)),
                   pl.BlockSpec((None, 2 * STATE_BLOCK, LANE_BLOCK), lambda b, c: (b, 0, 0)),
                   pl.BlockSpec((1, STATE_BLOCK), lambda b, c: (0, b)),
                   pl.BlockSpec((1, STATE_BLOCK), lambda b, c: (0, b)),
                   pl.BlockSpec((1, LANE_BLOCK), lambda b, c: (0, b))],
        out_shape=[jax.ShapeDtypeStruct(dp.shape, dp.dtype),
                   jax.ShapeDtypeStruct((S5_BLOCKS, LANE_BLOCK, 2 * STATE_BLOCK), F32),
                   jax.ShapeDtypeStruct((S5_BLOCKS, 2 * STATE_BLOCK, LANE_BLOCK), F32),
                   jax.ShapeDtypeStruct((1, nst), F32), jax.ShapeDtypeStruct((1, nst), F32),
                   jax.ShapeDtypeStruct((1, D_MODEL), F32)],
        scratch=[pltpu.VMEM((1, STATE_BLOCK), F32), pltpu.VMEM((1, STATE_BLOCK), F32)],
        aliases={11: 0}, args=(dy_direct, dyg, y0, uz, sr, si, bblk, cblk, ar, ai, dskip, dp),
        semantics=("parallel", "arbitrary"), job=job)


def s5_tables(lam_re, lam_im, log_dt, b_re, b_im, c_re, c_im):
    dt = jnp.exp(log_dt)[:, None]
    mag = jnp.exp(lam_re * dt)
    ar = mag * jnp.cos(lam_im * dt)
    ai = mag * jnp.sin(lam_im * dt)
    den = lam_re * lam_re + lam_im * lam_im
    kr = ((ar - 1.0) * lam_re + ai * lam_im) / den
    ki = (ai * lam_re - (ar - 1.0) * lam_im) / den
    bbr = kr[..., None] * b_re - ki[..., None] * b_im
    bbi = kr[..., None] * b_im + ki[..., None] * b_re
    eye = jnp.eye(GROUPS_PER_BLOCK, dtype=F32)

    def b_blocks(v):
        v = v.reshape(S5_BLOCKS, GROUPS_PER_BLOCK, S5_STATE, S5_GROUP)
        return jnp.einsum('bgpi,gh->bgihp', v, eye).reshape(S5_BLOCKS, LANE_BLOCK, STATE_BLOCK)

    def c_blocks(v):
        v = v.reshape(S5_BLOCKS, GROUPS_PER_BLOCK, S5_GROUP, S5_STATE)
        return jnp.einsum('bgip,gh->bgphi', v, eye).reshape(S5_BLOCKS, STATE_BLOCK, LANE_BLOCK)

    bblk = jnp.concatenate([b_blocks(bbr), b_blocks(bbi)], axis=2)
    cblk = jnp.concatenate([c_blocks(c_re), -c_blocks(c_im)], axis=1)
    nst = S5_GROUPS * S5_STATE
    return ar.reshape(1, nst), ai.reshape(1, nst), bblk, cblk


def _silu_grad(z):
    s = jax.nn.sigmoid(z)
    return s * (1.0 + z * (1.0 - s))


def conv_fwd(p, conv_w, conv_b, *, name):
    lp = p.shape[0]
    tr = lp // 16
    e = CONV_E
    hb = tr // 8

    def body(p_ref, cgh_ref, vh_ref, w_ref, b_ref, o_ref):
        i = pl.program_id(0)
        bg, cg, v, z = (p_ref[:, k * e:(k + 1) * e] for k in range(4))
        hc = cg * v
        halo = jnp.where(i > 0, cgh_ref[...] * vh_ref[...], 0.0)
        ext = jnp.concatenate([halo, hc], axis=0)
        w = w_ref[...]
        conv = (w[0:1] * pltpu.roll(ext, 2, 0)[8:] + w[1:2] * pltpu.roll(ext, 1, 0)[8:] + w[2:3] * hc
                + b_ref[...])
        o_ref[...] = (bg * conv * jax.nn.silu(z)).astype(o_ref.dtype)

    prev = lambda col: (lambda i: (jnp.maximum(i * hb - 1, 0), col))
    return pl.pallas_call(
        body, name=name, grid=(lp // tr,),
        in_specs=[pl.BlockSpec((tr, 4 * e), lambda i: (i, 0)),
                  pl.BlockSpec((8, e), prev(1)), pl.BlockSpec((8, e), prev(2)),
                  pl.BlockSpec((3, e), lambda i: (0, 0)), pl.BlockSpec((1, e), lambda i: (0, 0))],
        out_specs=pl.BlockSpec((tr, e), lambda i: (i, 0)),
        out_shape=jax.ShapeDtypeStruct((lp, e), BF16),
        compiler_params=_params("parallel"))(p, p, p, conv_w, conv_b)


def conv_bwd(dy3, p, conv_w, conv_b, *, name):
    lp = p.shape[0]
    tr = lp // 16
    e = CONV_E
    hb = tr // 8
    nt = lp // tr
    width = 512

    def body(dy_ref, p_ref, cgh_ref, vh_ref, dyn_ref, bgn_ref, zn_ref, w_ref, b_ref, dp_ref, dw_ref, db_ref):
        i = pl.program_id(0)
        for c0 in range(0, e, width):
            cols = slice(c0, c0 + width)
            dy = dy_ref[:, cols]
            bg, cg, v, z = (p_ref[:, k * e + c0:k * e + c0 + width] for k in range(4))
            w = w_ref[:, cols]
            hc = cg * v
            halo = jnp.where(i > 0, cgh_ref[:, cols] * vh_ref[:, cols], 0.0)
            ext = jnp.concatenate([halo, hc], axis=0)
            hc2, hc1 = pltpu.roll(ext, 2, 0)[8:], pltpu.roll(ext, 1, 0)[8:]
            yc = w[0:1] * hc2 + w[1:2] * hc1 + w[2:3] * hc + b_ref[:, cols]
            sz = jax.nn.silu(z)
            dyc = dy * bg * sz
            dyc_n = jnp.where(i < nt - 1, dyn_ref[:, cols] * bgn_ref[:, cols] * jax.nn.silu(zn_ref[:, cols]), 0.0)
            ext_n = jnp.concatenate([dyc, dyc_n], axis=0)
            n_rows = tr + 8
            dhc = (w[2:3] * dyc + w[1:2] * pltpu.roll(ext_n, n_rows - 1, 0)[:tr]
                   + w[0:1] * pltpu.roll(ext_n, n_rows - 2, 0)[:tr])
            for k, val in enumerate((dy * yc * sz, dhc * v, dhc * cg, dy * bg * yc * _silu_grad(z))):
                dp_ref[:, k * e + c0:k * e + c0 + width] = val.astype(dp_ref.dtype)
            dw = jnp.concatenate([_colsum(dyc * hc2), _colsum(dyc * hc1), _colsum(dyc * hc)], axis=0)
            db = _colsum(dyc)

            @pl.when(i == 0)
            def _(cols=cols, dw=dw, db=db):
                dw_ref[:, cols] = dw
                db_ref[:, cols] = db

            @pl.when(i > 0)
            def _(cols=cols, dw=dw, db=db):
                dw_ref[:, cols] += dw
                db_ref[:, cols] += db

    prev = lambda col: (lambda i: (jnp.maximum(i * hb - 1, 0), col))
    nxt = lambda col: (lambda i: (jnp.minimum((i + 1) * hb, lp // 8 - 1), col))
    return pl.pallas_call(
        body, name=name, grid=(nt,),
        in_specs=[pl.BlockSpec((tr, e), lambda i: (i, 0)), pl.BlockSpec((tr, 4 * e), lambda i: (i, 0)),
                  pl.BlockSpec((8, e), prev(1)), pl.BlockSpec((8, e), prev(2)),
                  pl.BlockSpec((8, e), nxt(0)), pl.BlockSpec((8, e), nxt(0)), pl.BlockSpec((8, e), nxt(3)),
                  pl.BlockSpec((3, e), lambda i: (0, 0)), pl.BlockSpec((1, e), lambda i: (0, 0))],
        out_specs=[pl.BlockSpec((tr, 4 * e), lambda i: (i, 0)), pl.BlockSpec((3, e), lambda i: (0, 0)),
                   pl.BlockSpec((1, e), lambda i: (0, 0))],
        out_shape=[jax.ShapeDtypeStruct((lp, 4 * e), BF16), jax.ShapeDtypeStruct((3, e), F32),
                   jax.ShapeDtypeStruct((1, e), F32)],
        compiler_params=_params("arbitrary"))(dy3, p, p, p, dy3, p, p, conv_w, conv_b)


def _pool_counts(i, tr, rows, offset, w):
    t = (i * tr + offset + 1 + lax.broadcasted_iota(jnp.int32, (rows, 1), 0)).astype(F32)
    return jnp.minimum(t, float(w))


def pool_fwd(p, *, name):
    lp = p.shape[0]
    tr = lp // 16
    h = POOL_HALO
    hb = tr // h

    def body(u_ref, uh_ref, o_ref):
        i = pl.program_id(0)
        u = u_ref[...]
        ext = jnp.concatenate([jnp.where(i > 0, uh_ref[...], 0.0), u], axis=0)
        for k, w in enumerate(POOL_WINDOWS):
            cols = slice(k * POOL_GROUP, (k + 1) * POOL_GROUP)
            s = ext[:, cols]
            d = 1
            while d < w:
                s = s + pltpu.roll(s, d, 0)
                d *= 2
            o_ref[:, cols] = (s[h:] / _pool_counts(i, tr, tr, 0, w) - u[:, cols]).astype(o_ref.dtype)

    return pl.pallas_call(
        body, name=name, grid=(lp // tr,),
        in_specs=[pl.BlockSpec((tr, POOL_E), lambda i: (i, 0)),
                  pl.BlockSpec((h, POOL_E), lambda i: (jnp.maximum(i * hb - 1, 0), 0))],
        out_specs=pl.BlockSpec((tr, POOL_E), lambda i: (i, 0)),
        out_shape=jax.ShapeDtypeStruct((lp, POOL_E), BF16),
        compiler_params=_params("parallel"))(p, p)


def pool_bwd(dmix, dp, *, name):
    lp = dmix.shape[0]
    tr = lp // 16
    h = POOL_HALO
    hb = tr // h
    nt = lp // tr

    def body(dm_ref, dmn_ref, _, du_ref):
        i = pl.program_id(0)
        dm = dm_ref[...]
        dmn = jnp.where(i < nt - 1, dmn_ref[...], 0.0)
        n_rows = tr + h
        for k, w in enumerate(POOL_WINDOWS):
            cols = slice(k * POOL_GROUP, (k + 1) * POOL_GROUP)
            s = jnp.concatenate([dm[:, cols] / _pool_counts(i, tr, tr, 0, w),
                                 dmn[:, cols] / _pool_counts(i, tr, h, tr, w)], axis=0)
            d = 1
            while d < w:
                s = s + pltpu.roll(s, n_rows - d, 0)
                d *= 2
            du_ref[:, cols] = (s[:tr] - dm[:, cols]).astype(du_ref.dtype)

    return pl.pallas_call(
        body, name=name, grid=(nt,),
        in_specs=[pl.BlockSpec((tr, POOL_E), lambda i: (i, 0)),
                  pl.BlockSpec((h, POOL_E), lambda i: (jnp.minimum((i + 1) * hb, lp // h - 1), 0)),
                  pl.BlockSpec(memory_space=pl.ANY)],
        out_specs=pl.BlockSpec((tr, POOL_E), lambda i: (i, 0)),
        out_shape=jax.ShapeDtypeStruct(dp.shape, dp.dtype),
        input_output_aliases={2: 0},
        compiler_params=_params("parallel"))(dmix, dmix, dp)


def _adamw_math(w, g, m, v):
    nm = ADAM_B1 * m + (1.0 - ADAM_B1) * g
    nv = ADAM_B2 * v + (1.0 - ADAM_B2) * jnp.square(g)
    m_hat = nm / (1.0 - ADAM_B1 ** ADAM_STEP)
    v_hat = nv / (1.0 - ADAM_B2 ** ADAM_STEP)
    return -ADAM_LR * (m_hat / (jnp.sqrt(v_hat) + ADAM_EPS) + ADAM_WD * w), nm, nv


def adamw(w, g, m, v, *, name):
    r, c = w.shape
    tr = _tile(r, max(8, (1 << 19) // c), 8)

    def body(w_ref, g_ref, m_ref, v_ref, d_ref, nm_ref, nv_ref):
        d_ref[...], nm_ref[...], nv_ref[...] = _adamw_math(w_ref[...], g_ref[...], m_ref[...], v_ref[...])

    spec = pl.BlockSpec((tr, c), lambda i: (i, 0))
    return pl.pallas_call(
        body, name=name, grid=(r // tr,), in_specs=[spec] * 4, out_specs=[spec] * 3,
        out_shape=[jax.ShapeDtypeStruct((r, c), F32)] * 3,
        compiler_params=_params("parallel"))(w, g, m, v)


def adamw_many(quads, *, name):
    shape = quads[0][0].shape
    steps = 8
    spec = pl.BlockSpec((shape[0] // steps,) + shape[1:], lambda i: (i, 0, 0))
    n = len(quads)

    def body(*refs):
        for q in range(n):
            w_ref, g_ref, m_ref, v_ref = refs[4 * q:4 * q + 4]
            d_ref, nm_ref, nv_ref = refs[4 * n + 3 * q:4 * n + 3 * q + 3]
            d_ref[...], nm_ref[...], nv_ref[...] = _adamw_math(w_ref[...], g_ref[...], m_ref[...], v_ref[...])

    res = pl.pallas_call(
        body, name=name, grid=(steps,), in_specs=[spec] * (4 * n), out_specs=[spec] * (3 * n),
        out_shape=[jax.ShapeDtypeStruct(shape, F32)] * (3 * n),
        compiler_params=_params("parallel"))(*[a for quad in quads for a in quad])
    return [tuple(res[3 * q:3 * q + 3]) for q in range(n)]


WIRE_WIDTH = 1024
WIRE_ROWS = 1024


def _wire_plan(shapes):
    starts, row = [], 0
    for s in shapes:
        r, c = (1, s[0]) if len(s) == 1 else s
        starts.append(row)
        row += -(-(r * max(1, c // WIRE_WIDTH)) // 8) * 8
    assert row <= WIRE_ROWS, row
    return starts


def _wire_cells(shape):
    if len(shape) == 1:
        n = shape[0]
        if n <= WIRE_WIDTH:
            return [(0, n, (slice(None),))]
        return [(h, WIRE_WIDTH, (slice(h * WIRE_WIDTH, (h + 1) * WIRE_WIDTH),)) for h in range(n // WIRE_WIDTH)]
    r, c = shape
    if c <= WIRE_WIDTH:
        return [(None, c, (slice(None), slice(None)))]
    per = c // WIRE_WIDTH
    return [(j * per + h, WIRE_WIDTH, (j, slice(h * WIRE_WIDTH, (h + 1) * WIRE_WIDTH)))
            for j in range(r) for h in range(per)]


def pack_small(arrays, *, name):
    shapes = [a.shape for a in arrays]
    starts = _wire_plan(shapes)

    def body(*refs):
        out = refs[-1]
        out[...] = jnp.zeros_like(out)
        for ref, shape, row0 in zip(refs[:-1], shapes, starts, strict=True):
            for off, cols, idx in _wire_cells(shape):
                if off is None:
                    out[row0:row0 + shape[0], 0:cols] = ref[idx]
                else:
                    out[row0 + off, 0:cols] = ref[idx]

    return pl.pallas_call(body, name=name, out_shape=jax.ShapeDtypeStruct((WIRE_ROWS, WIRE_WIDTH), F32),
                          compiler_params=pltpu.CompilerParams(vmem_limit_bytes=VMEM_LIMIT))(*arrays)


def update_small(packed, starts, triples, shard_cells, chip, *, name):
    n = len(triples)

    def body(chip_ref, p_ref, *refs):
        k = chip_ref[0]
        ins, outs = refs[:3 * n], refs[3 * n:]

        def cut(row0, rows, cols):
            c0, c1 = cols
            if isinstance(rows, int):
                return p_ref[row0 + rows, c0:c1]
            return p_ref[row0 + rows[0]:row0 + rows[1], c0:c1]

        for q in range(n):
            w_ref, m_ref, v_ref = ins[3 * q:3 * q + 3]
            g_ref, d_ref, nm_ref, nv_ref = outs[4 * q:4 * q + 4]
            shape = w_ref.shape
            if shard_cells[q] is None:
                pieces = [(cut(starts[q], (0, shape[0]) if off is None else off, (0, cols)), idx)
                          for off, cols, idx in _wire_cells(shape)]
            else:
                by_chip = [shard_cells[q](kk) for kk in range(N_CHIPS)]
                pieces = []
                for i, (_, _, idx) in enumerate(by_chip[0]):
                    g = cut(starts[q], *by_chip[0][i][:2])
                    for kk in range(1, N_CHIPS):
                        g = jnp.where(k == kk, cut(starts[q], *by_chip[kk][i][:2]), g)
                    pieces.append((g, idx))
            for g, idx in pieces:
                g_ref[idx] = g
                d_ref[idx], nm_ref[idx], nv_ref[idx] = _adamw_math(w_ref[idx], g, m_ref[idx], v_ref[idx])

    whole = lambda a: pl.BlockSpec(a.shape, lambda i, c_ref, nd=a.ndim: (0,) * nd)
    flat = [a for t in triples for a in t]
    out_shape = [jax.ShapeDtypeStruct(t[0].shape, F32) for t in triples for _ in range(4)]
    res = pl.pallas_call(
        body, name=name,
        grid_spec=pltpu.PrefetchScalarGridSpec(
            num_scalar_prefetch=1, grid=(1,),
            in_specs=[whole(packed)] + [whole(a) for a in flat],
            out_specs=[pl.BlockSpec(s.shape, lambda i, c_ref, nd=len(s.shape): (0,) * nd) for s in out_shape]),
        out_shape=out_shape,
        compiler_params=_params("arbitrary"))(chip.reshape(1).astype(jnp.int32), packed, *flat)
    return [tuple(res[4 * q:4 * q + 4]) for q in range(n)]


ANY = pl.BlockSpec(memory_space=pl.ANY)


def _place():
    x, y, c = lax.axis_index("x"), lax.axis_index("y"), lax.axis_index("c")
    return x, y, c, [(1 - x, y), (x, 1 - y), (1 - x, 1 - y)]


DMA_PIECE_BYTES = 512 * 1024


def _pieces(src, dst):
    shape = src.shape
    rows, cols = shape[-2], shape[-1]
    item = jnp.dtype(src.dtype).itemsize
    unit = 32 // item
    want = max(1, (rows * cols * item) // DMA_PIECE_BYTES)
    n = 1
    if rows % unit == 0:
        n = max(d for d in range(1, rows // unit + 1) if (rows // unit) % d == 0 and d <= want)
    size = rows // n
    out = []
    for lead in (range(shape[0]) if len(shape) == 3 else [None]):
        for i in range(n):
            sel = (pl.ds(i * size, size), slice(None)) if lead is None else (lead, pl.ds(i * size, size), slice(None))
            out.append((src.at[sel], dst.at[sel]))
    return out


def _send(src, dst, send_sem, recv_sem, peer, start=True):
    if start:
        for s, d in _pieces(src, dst):
            pltpu.make_async_remote_copy(src_ref=s, dst_ref=d, send_sem=send_sem, recv_sem=recv_sem,
                                         device_id=peer, device_id_type=MESH).start()
    return pltpu.make_async_remote_copy(src_ref=src, dst_ref=dst, send_sem=send_sem, recv_sem=recv_sem,
                                        device_id=peer, device_id_type=MESH)


def run_job(job, *, name):
    n_in, n_out = len(job.arrays), len(job.out_shapes)

    def body(*refs):
        job.fn(refs[:n_in], refs[n_in:n_in + n_out], refs[n_in + n_out:], "both")

    return pl.pallas_call(body, name=name, in_specs=[ANY] * n_in, out_specs=[ANY] * n_out,
                          out_shape=job.out_shapes, scratch_shapes=job.sems)(*job.arrays)


def gather_chips(shards, *, name):
    n = len(shards)

    def body(*refs):
        ins, outs = refs[:n], refs[n:2 * n]
        send_sems, recv_sems = refs[2 * n:]
        x, y, c, chips = _place()
        k = 2 * x + y
        copies = []
        for a in range(n):
            for j, peer in enumerate([(px, py, c) for px, py in chips] + [(x, y, 1 - c)]):
                copies.append(_send(ins[a], outs[a].at[k], send_sems.at[a, j], recv_sems.at[a, j], peer))
        for cp in copies:
            cp.wait()

    return pl.pallas_call(
        body, name=name, in_specs=[ANY] * n, out_specs=[ANY] * n,
        out_shape=[jax.ShapeDtypeStruct((N_CHIPS,) + s.shape, s.dtype) for s in shards],
        scratch_shapes=[pltpu.SemaphoreType.DMA((n, 4)), pltpu.SemaphoreType.DMA((n, 4))])(*shards)


def gather_by_halves_job(shards):
    n = len(shards)

    def fn(ins, outs, sems, phase):
        send_sems, recv_sems = sems
        x, y, c, chips = _place()
        k = 2 * x + y
        sibling = (x, y, 1 - c)
        begin = phase != "finish"
        waits, arrivals = [], []
        for a in range(n):
            half = ins[a].shape[0] // 2
            waits.append(_send(ins[a], outs[a].at[k], send_sems.at[a, 6], recv_sems.at[a, 6], sibling, begin))
            mine = pl.ds(c * half, half)
            for j, (px, py) in enumerate(chips):
                arrivals.append(_send(ins[a].at[mine, :], outs[a].at[k, mine, :], send_sems.at[a, j],
                                      recv_sems.at[a, j], (px, py, c), begin))
        if phase == "start":
            return
        i = 0
        for a in range(n):
            half = ins[a].shape[0] // 2
            mine = pl.ds(c * half, half)
            for j, (px, py) in enumerate(chips):
                arrivals[i].wait_recv()
                landed = outs[a].at[2 * px + py, mine, :]
                waits.append(_send(landed, landed, send_sems.at[a, 3 + j], recv_sems.at[a, 3 + j], sibling))
                i += 1
        for cp in arrivals:
            cp.wait_send()
        for cp in waits:
            cp.wait()

    return Job(shards, [jax.ShapeDtypeStruct((N_CHIPS,) + s.shape, s.dtype) for s in shards],
               [pltpu.SemaphoreType.DMA((n, 7)), pltpu.SemaphoreType.DMA((n, 7))], fn)


def swap_halves(grads, *, name):
    n = len(grads)

    def body(*refs):
        ins, outs = refs[:n], refs[n:2 * n]
        send_sems, recv_sems = refs[2 * n:]
        x, y, c, _ = _place()
        copies = []
        for a in range(n):
            half = ins[a].shape[1] // 2
            copies.append(_send(ins[a].at[:, pl.ds((1 - c) * half, half), :], outs[a], send_sems.at[a],
                                recv_sems.at[a], (x, y, 1 - c)))
        for cp in copies:
            cp.wait()

    return pl.pallas_call(
        body, name=name, in_specs=[ANY] * n, out_specs=[ANY] * n,
        out_shape=[jax.ShapeDtypeStruct((g.shape[0], g.shape[1] // 2, g.shape[2]), g.dtype) for g in grads],
        scratch_shapes=[pltpu.SemaphoreType.DMA((n,)), pltpu.SemaphoreType.DMA((n,))])(*grads)


def _chip_slot(s, k):
    rel = s ^ k
    return (rel >> 1) | ((rel & 1) << 1)


def sum_cores(g, theirs, ck, out_dtype, *, name):
    n, r, cols = g.shape
    half = r // 2
    tr = _tile(half, max(16, (1 << 19) // cols), 16)
    nb = half // tr

    def body(ck_ref, g_ref, t_ref, o_ref):
        o_ref[...] = (g_ref[...] + t_ref[...]).astype(o_ref.dtype)

    return pl.pallas_call(
        body, name=name,
        grid_spec=pltpu.PrefetchScalarGridSpec(
            num_scalar_prefetch=1, grid=(n, nb),
            in_specs=[pl.BlockSpec((None, tr, cols), lambda s, i, ck_ref: (s, ck_ref[0] * nb + i, 0)),
                      pl.BlockSpec((None, tr, cols), lambda s, i, ck_ref: (s, i, 0))],
            out_specs=pl.BlockSpec((None, tr, cols), lambda s, i, ck_ref: (_chip_slot(s, ck_ref[1]), i, 0))),
        out_shape=jax.ShapeDtypeStruct((n, half, cols), out_dtype),
        compiler_params=_params("parallel", "parallel"))(ck, g, theirs)


def scatter_job(parts):
    n = len(parts)

    def fn(ins, outs, sems, phase):
        send_sems, recv_sems = sems
        x, y, c, chips = _place()
        copies = []
        for a in range(n):
            for j, (px, py) in enumerate(chips):
                copies.append(_send(ins[a].at[1 + j], outs[a].at[j], send_sems.at[a, j], recv_sems.at[a, j],
                                    (px, py, c), phase != "finish"))
        if phase != "start":
            for cp in copies:
                cp.wait()

    return Job(parts, [jax.ShapeDtypeStruct((3,) + p.shape[1:], p.dtype) for p in parts],
               [pltpu.SemaphoreType.DMA((n, 3)), pltpu.SemaphoreType.DMA((n, 3))], fn)


def sum_chips(own, others, ck, *, name):
    _, r, cols = own.shape
    tr = _tile(r, max(16, (1 << 19) // cols), 16)

    def body(ck_ref, a_ref, b0_ref, b1_ref, b2_ref, o_ref):
        o_ref[...] = (a_ref[...].astype(F32) + b0_ref[...].astype(F32) + b1_ref[...].astype(F32)
                      + b2_ref[...].astype(F32))

    other = lambda j: pl.BlockSpec((None, tr, cols), lambda i, ck_ref: (j, i, 0))
    return pl.pallas_call(
        body, name=name,
        grid_spec=pltpu.PrefetchScalarGridSpec(
            num_scalar_prefetch=1, grid=(r // tr,),
            in_specs=[pl.BlockSpec((None, tr, cols), lambda i, ck_ref: (0, i, 0)), other(0), other(1), other(2)],
            out_specs=pl.BlockSpec((None, tr, cols), lambda i, ck_ref: (ck_ref[0], i, 0))),
        out_shape=jax.ShapeDtypeStruct((2, r, cols), F32),
        compiler_params=_params("parallel"))(ck, own, others, others, others)


def share_halves(joined, *, name):
    n = len(joined)

    def body(*refs):
        outs = refs[n:2 * n]
        send_sems, recv_sems = refs[2 * n:]
        x, y, c, _ = _place()
        copies = [_send(outs[a].at[c], outs[a].at[c], send_sems.at[a], recv_sems.at[a], (x, y, 1 - c))
                  for a in range(n)]
        for cp in copies:
            cp.wait()

    return pl.pallas_call(
        body, name=name, in_specs=[ANY] * n, out_specs=[ANY] * n,
        out_shape=[jax.ShapeDtypeStruct(j.shape, j.dtype) for j in joined],
        input_output_aliases={a: a for a in range(n)},
        scratch_shapes=[pltpu.SemaphoreType.DMA((n,)), pltpu.SemaphoreType.DMA((n,))])(*joined)


def kernel(x, meta_tokens, norm0_g, l0_w_in, l0_lam_re, l0_lam_im, l0_log_dt, l0_b_re, l0_b_im, l0_c_re, l0_c_im, l0_d_skip, l0_w_glu, l0_b_glu, l0_w_out, norm1_g, l1_w_in, l1_conv_w, l1_conv_b, l1_w_out, norm2_g, l2_w_in, l2_w_grp, l2_b_grp, l2_scale, l2_w_out, norm3_g, l3_w_in, l3_lam_re, l3_lam_im, l3_log_dt, l3_b_re, l3_b_im, l3_c_re, l3_c_im, l3_d_skip, l3_w_glu, l3_b_glu, l3_w_out, final_g, loss_target, m_meta_tokens, m_norm0_g, m_l0_w_in, m_l0_lam_re, m_l0_lam_im, m_l0_log_dt, m_l0_b_re, m_l0_b_im, m_l0_c_re, m_l0_c_im, m_l0_d_skip, m_l0_w_glu, m_l0_b_glu, m_l0_w_out, m_norm1_g, m_l1_w_in, m_l1_conv_w, m_l1_conv_b, m_l1_w_out, m_norm2_g, m_l2_w_in, m_l2_w_grp, m_l2_b_grp, m_l2_scale, m_l2_w_out, m_norm3_g, m_l3_w_in, m_l3_lam_re, m_l3_lam_im, m_l3_log_dt, m_l3_b_re, m_l3_b_im, m_l3_c_re, m_l3_c_im, m_l3_d_skip, m_l3_w_glu, m_l3_b_glu, m_l3_w_out, m_final_g, v_meta_tokens, v_norm0_g, v_l0_w_in, v_l0_lam_re, v_l0_lam_im, v_l0_log_dt, v_l0_b_re, v_l0_b_im, v_l0_c_re, v_l0_c_im, v_l0_d_skip, v_l0_w_glu, v_l0_b_glu, v_l0_w_out, v_norm1_g, v_l1_w_in, v_l1_conv_w, v_l1_conv_b, v_l1_w_out, v_norm2_g, v_l2_w_in, v_l2_w_grp, v_l2_b_grp, v_l2_scale, v_l2_w_out, v_norm3_g, v_l3_w_in, v_l3_lam_re, v_l3_lam_im, v_l3_log_dt, v_l3_b_re, v_l3_b_im, v_l3_c_re, v_l3_c_im, v_l3_d_skip, v_l3_w_glu, v_l3_b_glu, v_l3_w_out, v_final_g):
    given = dict(locals())
    w = {n: given[n] for n in WEIGHTS}
    mom_m = {n: given["m_" + n] for n in WEIGHTS}
    mom_v = {n: given["v_" + n] for n in WEIGHTS}
    seq = x.shape[1]
    real = N_META + seq
    lp = -(-real // SCAN_ROWS) * SCAN_ROWS
    chip = 2 * lax.axis_index("x") + lax.axis_index("y")
    row = lambda a: a.reshape(1, -1)

    shard_bf16 = {n: (w[n].reshape(-1, w[n].shape[-1]) if n == 'l2_w_grp' else w[n]).astype(BF16) for n in BIG}
    layer = lambda i: [n for n in BIG if n.startswith("l%d_" % i)]
    gather_layer = lambda i: gather_by_halves_job([shard_bf16[n] for n in layer(i)])
    full = {}

    def landed(i, arrays):
        for n, arr in zip(layer(i), arrays, strict=True):
            if n.endswith('w_in'):
                full[n] = arr
            elif n == 'l2_w_grp':
                full[n] = arr.reshape(N_CHIPS, len(POOL_WINDOWS), POOL_GROUP // N_CHIPS, POOL_GROUP)
            else:
                full[n] = arr.reshape(1, -1, arr.shape[-1])

    landed(0, run_job(gather_layer(0), name="gather_l0"))
    full.update(zip(SMALL_SHARDED, gather_chips([w[n] for n in SMALL_SHARDED], name="gather_small_shards"),
                    strict=True))
    meta_full = full['meta_tokens'].transpose(1, 0, 2).reshape(N_META, D_MODEL)
    conv_w_full = full['l1_conv_w'].transpose(1, 0, 2).reshape(3, CONV_E)
    b_grp_full = full['l2_b_grp'].transpose(1, 0, 2).reshape(len(POOL_WINDOWS), POOL_GROUP)

    h0 = jnp.concatenate([meta_full, x[0], jnp.zeros((lp - real, D_MODEL), F32)], axis=0)
    target = jnp.concatenate([jnp.zeros((N_META, D_MODEL), F32), loss_target[0],
                              jnp.zeros((lp - real, D_MODEL), F32)], axis=0)
    grads = {}
    saved = {}

    def s5_layer_fwd(i, h, gather=None):
        pre = "l%d_" % i
        tables, tables_vjp = jax.vjp(s5_tables, *[w[pre + s] for s in
                                                   ('lam_re', 'lam_im', 'log_dt', 'b_re', 'b_im', 'c_re', 'c_im')])
        ar, ai, bblk, cblk = tables
        n = rms_fwd(h, row(w["norm%d_g" % i]), name=pre + "norm")
        uz = mm_nn(n, full[pre + 'w_in'], name=pre + "in")
        res = s5_fwd(uz, bblk, cblk, ar, ai, row(w[pre + 'd_skip']), name=pre + "scan",
                     job=None if gather is None else gather_layer(gather))
        if gather is not None:
            res, arrived = res
            landed(gather, arrived)
        y0, yg, sr, si = res
        q = mm_nn(yg, full[pre + 'w_glu'], bias=row(w[pre + 'b_glu']), name=pre + "glu")
        (y3,), _ = rowwise(lambda yv, qv, zv: ([_s5_gate(yv, qv, zv)], []),
                           [(y0, D_MODEL, 0), (q, D_MODEL, 0), (uz, D_MODEL, 1)], [],
                           [(D_MODEL, BF16, D_MODEL, 0)], name=pre + "gate")
        h_new = mm_nn(y3, full[pre + 'w_out'], add=h, name=pre + "out")
        saved[i] = dict(h=h, n=n, uz=uz, y0=y0, sr=sr, si=si, yg=yg, q=q, y3=y3, tables=tables,
                        tables_vjp=tables_vjp)
        return h_new

    def s5_layer_bwd(i, dh, scatter=None):
        pre = "l%d_" % i
        s = saved[i]
        ar, ai, bblk, cblk = s['tables']
        grads[pre + 'w_out'] = mm_tn(s['y3'], dh, 1, name=pre + "d_w_out")[0]
        dy3 = mm_nt(dh, full[pre + 'w_out'], name=pre + "d_y3")

        def gate_bwd(dyv, yv, qv, zv):
            _, vjp = jax.vjp(_s5_gate, yv, qv, zv)
            dy0, dq, dz = vjp(dyv)
            return [dy0, dq, dz], [_colsum(dq)]

        (dy0_direct, dq, dp), (db_glu,) = rowwise(
            gate_bwd, [(dy3, D_MODEL, 0), (s['y0'], D_MODEL, 0), (s['q'], D_MODEL, 0), (s['uz'], D_MODEL, 1)], [],
            [(D_MODEL, F32, D_MODEL, 0), (D_MODEL, BF16, D_MODEL, 0), (2 * D_MODEL, BF16, D_MODEL, 1)], [D_MODEL],
            name=pre + "d_gate")
        grads[pre + 'b_glu'] = db_glu
        grads[pre + 'w_glu'] = mm_tn(s['yg'], dq, 1, name=pre + "d_w_glu")[0]
        dyg = mm_nt(dq, full[pre + 'w_glu'], name=pre + "d_yg")
        res = s5_bwd(dy0_direct, dyg, s['y0'], s['uz'], s['sr'], s['si'], bblk, cblk, ar, ai,
                     row(w[pre + 'd_skip']), dp, name=pre + "d_scan",
                     job=None if scatter is None else scatter_job(scatter_of(scatter)))
        if scatter is not None:
            res, arrived = res
            scattered(scatter, arrived)
        dp, dbblk, dcblk, dar, dai, dd = res
        grads[pre + 'd_skip'] = dd
        for key, val in zip(('lam_re', 'lam_im', 'log_dt', 'b_re', 'b_im', 'c_re', 'c_im'),
                            s['tables_vjp']((dar, dai, dbblk, dcblk)), strict=True):
            grads[pre + key] = val
        grads[pre + 'w_in'] = mm_tn(s['n'], dp, N_CHIPS, name=pre + "d_w_in")
        dn = mm_nt(dp, full[pre + 'w_in'], name=pre + "d_n")
        dh_in, dg = rms_bwd(s['h'], row(w["norm%d_g" % i]), dn, dh, name=pre + "d_norm")
        grads["norm%d_g" % i] = dg
        return dh_in

    h1 = s5_layer_fwd(0, h0, gather=1)

    n1 = rms_fwd(h1, row(w['norm1_g']), name="l1_norm")
    p1, arrived = mm_nn(n1, full['l1_w_in'], name="l1_in", job=gather_layer(2))
    landed(2, arrived)
    y3_1 = conv_fwd(p1, conv_w_full, row(w['l1_conv_b']), name="l1_conv")
    h2 = mm_nn(y3_1, full['l1_w_out'], add=h1, name="l1_out")

    n2 = rms_fwd(h2, row(w['norm2_g']), name="l2_norm")
    p2, arrived = mm_nn(n2, full['l2_w_in'], name="l2_in", job=gather_layer(3))
    landed(3, arrived)
    mixed = pool_fwd(p2, name="l2_pool")
    o2 = grp_nn(mixed, full['l2_w_grp'], b_grp_full.reshape(1, -1), name="l2_grp")
    (y3_2,), _ = rowwise(lambda ov, zv, sv: ([_pool_gate(ov, zv, sv)], []),
                         [(o2, POOL_E, 0), (p2, POOL_E, 1)], [row(w['l2_scale'])],
                         [(POOL_E, BF16, POOL_E, 0)], name="l2_gate")
    h3 = mm_nn(y3_2, full['l2_w_out'], add=h2, name="l2_out")

    h4 = s5_layer_fwd(3, h3)

    def loss_fn(hv, tv, gv):
        i = pl.program_id(0)
        tr = hv.shape[0]
        yf, vjp = jax.vjp(_rms, hv, gv)
        pos = i * tr + lax.broadcasted_iota(jnp.int32, (tr, 1), 0)
        diff = jnp.where((pos >= N_META) & (pos < real), yf - tv, 0.0)
        dh, dg = vjp(diff / D_MODEL)
        return [dh], [dg, _colsum(diff * diff)]

    (dh,), (d_final_g, sq) = rowwise(loss_fn, [(h4, D_MODEL, 0), (target, D_MODEL, 0)], [row(w['final_g'])],
                                     [(D_MODEL, F32, D_MODEL, 0)], [D_MODEL, D_MODEL], name="loss")
    grads['final_g'] = d_final_g
    loss = lax.psum(0.5 / D_MODEL * jnp.sum(sq), ("x", "y", "c"))

    ck = jnp.stack([lax.axis_index("c"), chip]).astype(jnp.int32)
    pair, from_chips = {}, {}

    def chip_major(n):
        g = grads[n]
        if n.endswith('w_in'):
            return g
        if n == 'l2_w_grp':
            return g.reshape(N_CHIPS, -1, POOL_GROUP)
        return g.reshape(N_CHIPS, -1, g.shape[-1])

    def reduce_cores(i, small=None):
        names, pieces, wire = layer(i), [chip_major(n) for n in layer(i)], [BF16] * len(layer(i))
        if small is not None:
            names, pieces, wire = names + ['small'], pieces + [small], wire + [F32]
        theirs = swap_halves(pieces, name="reduce_cores_l%d" % i)
        for n, g, t, dt in zip(names, pieces, theirs, wire, strict=True):
            pair[n] = sum_cores(g, t, ck, dt, name="sum_cores_" + n)

    scatter_of = lambda i: [pair[n] for n in layer(i)]

    def scattered(i, arrays):
        from_chips.update(zip(layer(i), arrays, strict=True))

    dh = s5_layer_bwd(3, dh)
    reduce_cores(3)

    grads['l2_w_out'] = mm_tn(y3_2, dh, 1, name="l2_d_w_out")[0]
    dy3 = mm_nt(dh, full['l2_w_out'], name="l2_d_y3")

    def pool_gate_bwd(dyv, ov, zv, sv):
        _, vjp = jax.vjp(_pool_gate, ov, zv, sv)
        do, dz, dscale = vjp(dyv)
        return [do, dz], [dscale, _colsum(do)]

    (do2, dp2), (dscale, dbgrp) = rowwise(
        pool_gate_bwd, [(dy3, POOL_E, 0), (o2, POOL_E, 0), (p2, POOL_E, 1)], [row(w['l2_scale'])],
        [(POOL_E, BF16, POOL_E, 0), (2 * POOL_E, BF16, POOL_E, 1)], [POOL_E, POOL_E], name="l2_d_gate")
    grads['l2_scale'] = dscale
    grads['l2_b_grp'] = dbgrp
    grads['l2_w_grp'] = grp_tn(mixed, do2, name="l2_d_w_grp")
    dmix = grp_nt(do2, full['l2_w_grp'], name="l2_d_mixed")
    dp2 = pool_bwd(dmix, dp2, name="l2_d_pool")
    grads['l2_w_in'], arrived = mm_tn(n2, dp2, N_CHIPS, name="l2_d_w_in", job=scatter_job(scatter_of(3)))
    scattered(3, arrived)
    dn = mm_nt(dp2, full['l2_w_in'], name="l2_d_n")
    dh, dg = rms_bwd(h2, row(w['norm2_g']), dn, dh, name="l2_d_norm")
    grads['norm2_g'] = dg
    reduce_cores(2)

    grads['l1_w_out'] = mm_tn(y3_1, dh, 1, name="l1_d_w_out")[0]
    dy3 = mm_nt(dh, full['l1_w_out'], name="l1_d_y3")
    dp1, dcw, dcb = conv_bwd(dy3, p1, conv_w_full, row(w['l1_conv_b']), name="l1_d_conv")
    grads['l1_conv_w'] = dcw
    grads['l1_conv_b'] = dcb
    grads['l1_w_in'], arrived = mm_tn(n1, dp1, N_CHIPS, name="l1_d_w_in", job=scatter_job(scatter_of(2)))
    scattered(2, arrived)
    dn = mm_nt(dp1, full['l1_w_in'], name="l1_d_n")
    dh, dg = rms_bwd(h1, row(w['norm1_g']), dn, dh, name="l1_d_norm")
    grads['norm1_g'] = dg
    reduce_cores(1)

    dh = s5_layer_bwd(0, dh, scatter=1)
    grad_x = dh[N_META:real][None]
    grads['meta_tokens'] = dh[:N_META]

    wide = [n for n in SMALL if w[n].ndim == 3]
    wire = [grads[n].reshape(S5_GROUPS, -1) if n in wide else grads[n] for n in SMALL]
    starts = dict(zip(SMALL, _wire_plan([a.shape for a in wire]), strict=True))
    small_local = pack_small(wire, name="pack_small")
    reduce_cores(0, small_local.reshape(N_CHIPS, -1, WIRE_WIDTH))
    rest = layer(0) + ['small']
    from_chips.update(zip(rest, run_job(scatter_job([pair[n] for n in rest]), name="reduce_chips_l0"), strict=True))
    joined = [sum_chips(pair[n], from_chips[n], ck, name="sum_chips_" + n) for n in BIG + ['small']]
    joined = share_halves(joined, name="share_cores")
    reduced = [j.reshape(-1, j.shape[-1]) for j in joined]
    g_big = {n: reduced[i].reshape(w[n].shape) for i, n in enumerate(BIG)}
    (small_all,) = gather_chips([reduced[-1]], name="gather_small")
    small_all = small_all.reshape(WIRE_ROWS, WIRE_WIDTH)

    out_g, out_d, out_m, out_v = {}, {}, {}, {}
    for n in BIG:
        shape = w[n].shape
        as2d = lambda a: a.reshape(-1, shape[-1])
        d, nm, nv = adamw(as2d(w[n]), as2d(g_big[n]), as2d(mom_m[n]), as2d(mom_v[n]), name="adamw_" + n)
        out_g[n], out_d[n], out_m[n], out_v[n] = g_big[n], d.reshape(shape), nm.reshape(shape), nv.reshape(shape)

    half_w = WIRE_WIDTH // 2
    shard_cells = {
        'meta_tokens': lambda k: [((0, N_META), (k * 256, (k + 1) * 256), (slice(None), slice(None)))],
        'l1_conv_w': lambda k: [(2 * j + k // 2, ((k % 2) * half_w, (k % 2 + 1) * half_w), (j, slice(None)))
                                for j in range(3)],
        'l2_b_grp': lambda k: [(j // 2, ((j % 2) * half_w + k * 128, (j % 2) * half_w + (k + 1) * 128),
                                (j, slice(None))) for j in range(len(POOL_WINDOWS))],
    }
    plain = [n for n in SMALL if n not in wide]
    res = update_small(small_all, [starts[n] for n in plain], [(w[n], mom_m[n], mom_v[n]) for n in plain],
                       [shard_cells.get(n) for n in plain], chip, name="adamw_small")
    for n, (g, d, nm, nv) in zip(plain, res, strict=True):
        out_g[n], out_d[n], out_m[n], out_v[n] = g, d, nm, nv
    for shape in sorted({w[n].shape for n in wide}):
        names = [n for n in wide if w[n].shape == shape]
        for n in names:
            out_g[n] = small_all[starts[n]:starts[n] + S5_GROUPS].reshape(shape)
        res = adamw_many([(w[n], out_g[n], mom_m[n], mom_v[n]) for n in names], name="adamw_%dx%dx%d" % shape)
        for n, (d, nm, nv) in zip(names, res, strict=True):
            out_d[n], out_m[n], out_v[n] = d, nm, nv

    return (loss, grad_x, *[out_g[n] for n in WEIGHTS], *[out_d[n] for n in WEIGHTS],
            *[out_m[n] for n in WEIGHTS], *[out_v[n] for n in WEIGHTS])
```

```python
import functools

import jax
import jax.numpy as jnp
from jax import lax
from jax.experimental import pallas as pl
from jax.experimental.pallas import tpu as pltpu

F32, BF16 = jnp.float32, jnp.bfloat16
MESH = pl.DeviceIdType.MESH

D_MODEL = 1024
N_META = 16
EPS = 1e-6
S5_GROUPS, S5_GROUP, S5_STATE = 64, 16, 64
LANE_BLOCK = 128
S5_BLOCKS = D_MODEL // LANE_BLOCK
GROUPS_PER_BLOCK = LANE_BLOCK // S5_GROUP
STATE_BLOCK = GROUPS_PER_BLOCK * S5_STATE
SCAN_ROWS = 256
SUBLANES = 8
CONV_E = 2048
POOL_E = 2048
POOL_WINDOWS = (2, 4, 8, 16)
POOL_GROUP = 512
POOL_HALO = 16
N_CHIPS = 4

ADAM_LR, ADAM_B1, ADAM_B2, ADAM_EPS, ADAM_WD, ADAM_STEP = 0.001, 0.9, 0.999, 1e-08, 0.01, 10

VMEM_LIMIT = 48 * 1024 * 1024
TN_OPERAND_BYTES = 28 * 1024 * 1024

WEIGHTS = ['meta_tokens', 'norm0_g', 'l0_w_in', 'l0_lam_re', 'l0_lam_im', 'l0_log_dt', 'l0_b_re', 'l0_b_im',
           'l0_c_re', 'l0_c_im', 'l0_d_skip', 'l0_w_glu', 'l0_b_glu', 'l0_w_out', 'norm1_g', 'l1_w_in',
           'l1_conv_w', 'l1_conv_b', 'l1_w_out', 'norm2_g', 'l2_w_in', 'l2_w_grp', 'l2_b_grp', 'l2_scale',
           'l2_w_out', 'norm3_g', 'l3_w_in', 'l3_lam_re', 'l3_lam_im', 'l3_log_dt', 'l3_b_re', 'l3_b_im',
           'l3_c_re', 'l3_c_im', 'l3_d_skip', 'l3_w_glu', 'l3_b_glu', 'l3_w_out', 'final_g']
BIG = ['l0_w_in', 'l0_w_glu', 'l0_w_out', 'l1_w_in', 'l1_w_out', 'l2_w_in', 'l2_w_grp', 'l2_w_out',
       'l3_w_in', 'l3_w_glu', 'l3_w_out']
SMALL_SHARDED = ['meta_tokens', 'l1_conv_w', 'l2_b_grp']
SMALL = [n for n in WEIGHTS if n not in BIG]


def _params(*sem):
    return pltpu.CompilerParams(dimension_semantics=sem, vmem_limit_bytes=VMEM_LIMIT)


class Job:
    def __init__(self, arrays, out_shapes, sems, fn):
        self.arrays, self.out_shapes, self.sems, self.fn = list(arrays), list(out_shapes), list(sems), fn


def _call(body, *, name, grid, in_specs, out_specs, out_shape, args, semantics, scratch=(), aliases=None, job=None):
    if job is None:
        return pl.pallas_call(
            body, name=name, grid=grid, in_specs=list(in_specs), out_specs=list(out_specs),
            out_shape=list(out_shape), scratch_shapes=list(scratch), input_output_aliases=aliases or {},
            compiler_params=_params(*semantics))(*args)
    counts = [len(in_specs), len(job.arrays), len(out_specs), len(job.out_shapes), len(scratch), len(job.sems)]

    def hosted(*refs):
        parts, pos = [], 0
        for k in counts:
            parts.append(refs[pos:pos + k])
            pos += k
        ins, job_ins, outs, job_outs, scr, job_sems = parts
        steps = [pl.program_id(d) for d in range(len(grid))]
        first = functools.reduce(jnp.logical_and, [s == 0 for s in steps])
        last = functools.reduce(jnp.logical_and, [s == g - 1 for s, g in zip(steps, grid, strict=True)])

        @pl.when(first)
        def _():
            job.fn(job_ins, job_outs, job_sems, "start")

        body(*ins, *outs, *scr)

        @pl.when(last)
        def _():
            job.fn(job_ins, job_outs, job_sems, "finish")

    any_spec = pl.BlockSpec(memory_space=pl.ANY)
    res = pl.pallas_call(
        hosted, name=name, grid=grid, in_specs=list(in_specs) + [any_spec] * len(job.arrays),
        out_specs=list(out_specs) + [any_spec] * len(job.out_shapes),
        out_shape=list(out_shape) + job.out_shapes, scratch_shapes=list(scratch) + job.sems,
        input_output_aliases=aliases or {},
        compiler_params=_params(*["arbitrary"] * len(grid)))(*args, *job.arrays)
    return res[:len(out_specs)], res[len(out_specs):]


def _tile(n, cap, mult):
    best = None
    for t in range(mult, min(n, cap) + 1, mult):
        if n % t == 0:
            best = t
    assert best is not None, (n, cap, mult)
    return best


def _with_job(res, job):
    return res[0] if job is None else (res[0][0], res[1])


def mm_nn(a, b, *, name, bias=None, add=None, out_dtype=F32, job=None):
    m, k = a.shape
    s, _, ns = b.shape
    n = s * ns
    tm, tn = _tile(m, 1088, 16), _tile(ns, 512, 128)
    per = ns // tn

    def body(*refs):
        a_ref, b_ref = refs[0], refs[1]
        o_ref = refs[-1]
        acc = jnp.dot(a_ref[...].astype(BF16), b_ref[...], preferred_element_type=F32)
        pos = 2
        if bias is not None:
            acc = acc + refs[pos][...]
            pos += 1
        if add is not None:
            acc = acc + refs[pos][...]
        o_ref[...] = acc.astype(o_ref.dtype)

    in_specs = [pl.BlockSpec((tm, k), lambda i, j: (i, 0)),
                pl.BlockSpec((None, k, tn), lambda i, j: (j // per, 0, j % per))]
    args = [a, b]
    if bias is not None:
        in_specs.append(pl.BlockSpec((1, tn), lambda i, j: (0, j)))
        args.append(bias)
    if add is not None:
        in_specs.append(pl.BlockSpec((tm, tn), lambda i, j: (i, j)))
        args.append(add)
    return _with_job(_call(
        body, name=name, grid=(m // tm, n // tn), in_specs=in_specs,
        out_specs=[pl.BlockSpec((tm, tn), lambda i, j: (i, j))],
        out_shape=[jax.ShapeDtypeStruct((m, n), out_dtype)], args=args,
        semantics=("parallel", "parallel"), job=job), job)


def mm_nt(g, w, *, name, job=None):
    m, kc = g.shape
    s, nout, kcs = w.shape
    assert s * kcs == kc
    tm, tno, tk = _tile(m, 1088, 16), _tile(nout, 1024, 128), _tile(kcs, 1024, 128)
    per = kcs // tk

    def body(g_ref, w_ref, o_ref):
        kk = pl.program_id(2)
        part = lax.dot_general(g_ref[...].astype(BF16), w_ref[...], (((1,), (1,)), ((), ())),
                               preferred_element_type=F32)

        @pl.when(kk == 0)
        def _():
            o_ref[...] = part

        @pl.when(kk > 0)
        def _():
            o_ref[...] += part

    return _with_job(_call(
        body, name=name, grid=(m // tm, nout // tno, kc // tk),
        in_specs=[pl.BlockSpec((tm, tk), lambda i, j, kk: (i, kk)),
                  pl.BlockSpec((None, tno, tk), lambda i, j, kk: (kk // per, j, kk % per))],
        out_specs=[pl.BlockSpec((tm, tno), lambda i, j, kk: (i, j))],
        out_shape=[jax.ShapeDtypeStruct((m, nout), F32)], args=(g, w),
        semantics=("parallel", "parallel", "arbitrary"), job=job), job)


def mm_tn(a, g, shards, *, name, job=None):
    m, ka = a.shape
    n = g.shape[1]
    ns = n // shards
    tka, tn = _tile(ka, 1024, 128), _tile(ns, 512, 128)
    row_bytes = tka * jnp.dtype(a.dtype).itemsize + tn * jnp.dtype(g.dtype).itemsize
    tm = _tile(m, TN_OPERAND_BYTES // (2 * row_bytes), 16)
    per = ns // tn

    def body(a_ref, g_ref, o_ref):
        mm = pl.program_id(2)
        part = lax.dot_general(a_ref[...].astype(BF16), g_ref[...].astype(BF16), (((0,), (0,)), ((), ())),
                               preferred_element_type=F32)

        @pl.when(mm == 0)
        def _():
            o_ref[...] = part

        @pl.when(mm > 0)
        def _():
            o_ref[...] += part

    return _with_job(_call(
        body, name=name, grid=(ka // tka, n // tn, m // tm),
        in_specs=[pl.BlockSpec((tm, tka), lambda i, j, mm: (mm, i)),
                  pl.BlockSpec((tm, tn), lambda i, j, mm: (mm, j))],
        out_specs=[pl.BlockSpec((None, tka, tn), lambda i, j, mm: (j // per, i, j % per))],
        out_shape=[jax.ShapeDtypeStruct((shards, ka, ns), F32)], args=(a, g),
        semantics=("parallel", "parallel", "arbitrary"), job=job), job)


def grp_nn(a, w, bias, *, name):
    m = a.shape[0]
    tm = _tile(m, 1088, 16)
    ng = len(POOL_WINDOWS)

    def body(a_ref, w_ref, b_ref, o_ref):
        wj = w_ref[...].reshape(POOL_GROUP, POOL_GROUP)
        o_ref[...] = jnp.dot(a_ref[...].astype(BF16), wj, preferred_element_type=F32) + b_ref[...]

    return pl.pallas_call(
        body, name=name, grid=(m // tm, ng),
        in_specs=[pl.BlockSpec((tm, POOL_GROUP), lambda i, j: (i, j)),
                  pl.BlockSpec((N_CHIPS, None, POOL_GROUP // N_CHIPS, POOL_GROUP), lambda i, j: (0, j, 0, 0)),
                  pl.BlockSpec((1, POOL_GROUP), lambda i, j: (0, j))],
        out_specs=pl.BlockSpec((tm, POOL_GROUP), lambda i, j: (i, j)),
        out_shape=jax.ShapeDtypeStruct((m, ng * POOL_GROUP), F32),
        compiler_params=_params("parallel", "parallel"))(a, w, bias)


def grp_nt(g, w, *, name):
    m = g.shape[0]
    tm = _tile(m, 1088, 16)
    ng = len(POOL_WINDOWS)

    def body(g_ref, w_ref, o_ref):
        wj = w_ref[...].reshape(POOL_GROUP, POOL_GROUP)
        o_ref[...] = lax.dot_general(g_ref[...].astype(BF16), wj, (((1,), (1,)), ((), ())),
                                     preferred_element_type=F32)

    return pl.pallas_call(
        body, name=name, grid=(m // tm, ng),
        in_specs=[pl.BlockSpec((tm, POOL_GROUP), lambda i, j: (i, j)),
                  pl.BlockSpec((N_CHIPS, None, POOL_GROUP // N_CHIPS, POOL_GROUP), lambda i, j: (0, j, 0, 0))],
        out_specs=pl.BlockSpec((tm, POOL_GROUP), lambda i, j: (i, j)),
        out_shape=jax.ShapeDtypeStruct((m, ng * POOL_GROUP), F32),
        compiler_params=_params("parallel", "parallel"))(g, w)


def grp_tn(a, g, *, name):
    m = a.shape[0]
    tm = _tile(m, 1088, 16)
    ng = len(POOL_WINDOWS)
    rows = POOL_GROUP // N_CHIPS

    def body(a_ref, g_ref, o_ref):
        mm = pl.program_id(1)
        part = lax.dot_general(a_ref[...].astype(BF16), g_ref[...].astype(BF16), (((0,), (0,)), ((), ())),
                               preferred_element_type=F32).reshape(N_CHIPS, rows, POOL_GROUP)

        @pl.when(mm == 0)
        def _():
            o_ref[...] = part

        @pl.when(mm > 0)
        def _():
            o_ref[...] += part

    return pl.pallas_call(
        body, name=name, grid=(ng, m // tm),
        in_specs=[pl.BlockSpec((tm, POOL_GROUP), lambda j, mm: (mm, j)),
                  pl.BlockSpec((tm, POOL_GROUP), lambda j, mm: (mm, j))],
        out_specs=pl.BlockSpec((N_CHIPS, None, rows, POOL_GROUP), lambda j, mm: (0, j, 0, 0)),
        out_shape=jax.ShapeDtypeStruct((N_CHIPS, ng, rows, POOL_GROUP), F32),
        compiler_params=_params("parallel", "arbitrary"))(a, g)


def rowwise(fn, rows, bcast, outs, accs=(), *, name, aliases=None):
    m = rows[0][0].shape[0]
    tr = m // 16
    n_in = len(rows) + len(bcast)

    def body(*refs):
        vals = [r[...] for r in refs[:n_in]]
        o_vals, a_vals = fn(*vals)
        for ref, val in zip(refs[n_in:n_in + len(outs)], o_vals, strict=True):
            ref[...] = val.astype(ref.dtype)
        step = pl.program_id(0)
        for ref, val in zip(refs[n_in + len(outs):], a_vals, strict=True):
            @pl.when(step == 0)
            def _(ref=ref, val=val):
                ref[...] = val

            @pl.when(step > 0)
            def _(ref=ref, val=val):
                ref[...] += val

    in_specs = [pl.BlockSpec((tr, w), functools.partial(lambda i, cb: (i, cb), cb=cb)) for _, w, cb in rows]
    in_specs += [pl.BlockSpec(b.shape, lambda i: (0, 0)) for b in bcast]
    out_specs = [pl.BlockSpec((tr, w), functools.partial(lambda i, cb: (i, cb), cb=cb)) for _, _, w, cb in outs]
    out_specs += [pl.BlockSpec((1, w), lambda i: (0, 0)) for w in accs]
    out_shape = [jax.ShapeDtypeStruct((m, cols), dt) for cols, dt, _, _ in outs]
    out_shape += [jax.ShapeDtypeStruct((1, w), F32) for w in accs]
    res = pl.pallas_call(
        body, name=name, grid=(m // tr,), in_specs=in_specs, out_specs=out_specs, out_shape=out_shape,
        input_output_aliases=aliases or {},
        compiler_params=_params("arbitrary"))(*[r[0] for r in rows], *bcast)
    return res[:len(outs)], res[len(outs):]


def _rms(h, g):
    return h * lax.rsqrt(jnp.mean(h * h, axis=-1, keepdims=True) + EPS) * g


def _colsum(v):
    return jnp.sum(v, axis=0, keepdims=True)


def rms_fwd(h, g, *, name):
    (n,), _ = rowwise(lambda hv, gv: ([_rms(hv, gv)], []), [(h, D_MODEL, 0)], [g],
                      [(D_MODEL, BF16, D_MODEL, 0)], name=name)
    return n


def rms_bwd(h, g, dn, dh_out, *, name):
    def fn(hv, dnv, dhv, gv):
        _, vjp = jax.vjp(_rms, hv, gv)
        dh, dg = vjp(dnv)
        return [dhv + dh], [dg]

    (dh,), (dg,) = rowwise(fn, [(h, D_MODEL, 0), (dn, D_MODEL, 0), (dh_out, D_MODEL, 0)], [g],
                           [(D_MODEL, F32, D_MODEL, 0)], [D_MODEL], name=name)
    return dh, dg


def _s5_gate(y0, q, z):
    yg = jax.nn.gelu(y0)
    return yg * jax.nn.sigmoid(q) * jax.nn.silu(z)


def _pool_gate(o, z, scale):
    return o * scale * jax.nn.silu(z)


def _shift_rows(v, d, down):
    t = v.shape[0]
    row = lax.broadcasted_iota(jnp.int32, v.shape, 0)
    if down:
        return jnp.where(row >= d, pltpu.roll(v, d, 0), 0.0)
    return jnp.where(row < t - d, pltpu.roll(v, t - d, 0), 0.0)


def _cmul(ar, ai, br, bi):
    return ar * br - ai * bi, ar * bi + ai * br


def _scan(xr, xi, ar, ai, cr, ci, down):
    t, n = xr.shape
    nb = t // SUBLANES
    pw = [(ar, ai)]
    for _ in range(SUBLANES - 1):
        pw.append(_cmul(*pw[-1], ar, ai))
    sub = lax.broadcasted_iota(jnp.int32, (SUBLANES, n), 0)

    def table(exps):
        tr, ti = jnp.zeros((SUBLANES, n), F32), jnp.zeros((SUBLANES, n), F32)
        for r, e in enumerate(exps):
            if e:
                tr, ti = jnp.where(sub == r, pw[e - 1][0], tr), jnp.where(sub == r, pw[e - 1][1], ti)
        return tr[None], ti[None]

    sr, si = xr.reshape(nb, SUBLANES, n), xi.reshape(nb, SUBLANES, n)
    d = 1
    while d < SUBLANES:
        mr, mi = table([d if (r >= d if down else r < SUBLANES - d) else 0 for r in range(SUBLANES)])
        turn = d if down else SUBLANES - d
        hr, hi = pltpu.roll(sr, turn, 1), pltpu.roll(si, turn, 1)
        sr, si = sr + mr * hr - mi * hi, si + mr * hi + mi * hr
        d *= 2
    edge = SUBLANES - 1 if down else 0
    qr, qi = table([r + 1 if down else SUBLANES - r for r in range(SUBLANES)])
    qr, qi = qr[0], qi[0]
    in_r, in_i = jnp.broadcast_to(cr, (SUBLANES, n)), jnp.broadcast_to(ci, (SUBLANES, n))
    out_r, out_i = [None] * nb, [None] * nb
    for b in (range(nb) if down else reversed(range(nb))):
        out_r[b] = sr[b] + qr * in_r - qi * in_i
        out_i[b] = si[b] + qr * in_i + qi * in_r
        in_r = jnp.broadcast_to(out_r[b][edge:edge + 1, :], (SUBLANES, n))
        in_i = jnp.broadcast_to(out_i[b][edge:edge + 1, :], (SUBLANES, n))
    return jnp.concatenate(out_r, axis=0), jnp.concatenate(out_i, axis=0), in_r[0:1, :], in_i[0:1, :]


def s5_fwd(uz, bblk, cblk, ar, ai, dskip, *, name, job=None):
    lp = uz.shape[0]
    t = SCAN_ROWS
    nst = S5_GROUPS * S5_STATE

    def body(u_ref, b_ref, c_ref, ar_ref, ai_ref, d_ref, y_ref, yg_ref, sr_ref, si_ref, cr, ci):
        step = pl.program_id(1)

        @pl.when(step == 0)
        def _():
            cr[...] = jnp.zeros_like(cr)
            ci[...] = jnp.zeros_like(ci)

        u = u_ref[...]
        a_r, a_i = ar_ref[...], ai_ref[...]
        x = jnp.dot(u.astype(BF16), b_ref[...].astype(BF16), preferred_element_type=F32)
        sr, si, cr[...], ci[...] = _scan(x[:, :STATE_BLOCK], x[:, STATE_BLOCK:], a_r, a_i, cr[...], ci[...], True)
        sr_ref[...] = sr
        si_ref[...] = si
        s = jnp.concatenate([sr, si], axis=1).astype(BF16)
        y0 = lax.dot_general(s, c_ref[...].astype(BF16), (((1,), (1,)), ((), ())),
                             preferred_element_type=F32) + d_ref[...] * u
        y_ref[...] = y0
        yg_ref[...] = jax.nn.gelu(y0).astype(yg_ref.dtype)

    return _call(
        body, name=name, grid=(S5_BLOCKS, lp // t),
        in_specs=[pl.BlockSpec((t, LANE_BLOCK), lambda b, c: (c, b)),
                  pl.BlockSpec((None, LANE_BLOCK, 2 * STATE_BLOCK), lambda b, c: (b, 0, 0)),
                  pl.BlockSpec((None, LANE_BLOCK, 2 * STATE_BLOCK), lambda b, c: (b, 0, 0)),
                  pl.BlockSpec((1, STATE_BLOCK), lambda b, c: (0, b)),
                  pl.BlockSpec((1, STATE_BLOCK), lambda b, c: (0, b)),
                  pl.BlockSpec((1, LANE_BLOCK), lambda b, c: (0, b))],
        out_specs=[pl.BlockSpec((t, LANE_BLOCK), lambda b, c: (c, b)),
                   pl.BlockSpec((t, LANE_BLOCK), lambda b, c: (c, b)),
                   pl.BlockSpec((t, STATE_BLOCK), lambda b, c: (c, b)),
                   pl.BlockSpec((t, STATE_BLOCK), lambda b, c: (c, b))],
        out_shape=[jax.ShapeDtypeStruct((lp, D_MODEL), F32), jax.ShapeDtypeStruct((lp, D_MODEL), BF16),
                   jax.ShapeDtypeStruct((lp, nst), F32), jax.ShapeDtypeStruct((lp, nst), F32)],
        scratch=[pltpu.VMEM((1, STATE_BLOCK), F32), pltpu.VMEM((1, STATE_BLOCK), F32)],
        args=(uz, bblk, cblk, ar, ai, dskip), semantics=("parallel", "arbitrary"), job=job)


def s5_bwd(dy_direct, dyg, y0, uz, sr, si, bblk, cblk, ar, ai, dskip, dp, *, name, job=None):
    lp = uz.shape[0]
    t = SCAN_ROWS
    nch = lp // t
    nst = S5_GROUPS * S5_STATE

    def body(dyd_ref, dyg_ref, y0_ref, u_ref, sr_ref, si_ref, b_ref, c_ref, ar_ref, ai_ref, d_ref, _, du_ref,
             db_ref, dc_ref, dar_ref, dai_ref, dd_ref, cr, ci):
        step = pl.program_id(1)

        @pl.when(step == 0)
        def _():
            cr[...] = jnp.zeros_like(cr)
            ci[...] = jnp.zeros_like(ci)

        dy = dyd_ref[...] + jax.vjp(jax.nn.gelu, y0_ref[...])[1](dyg_ref[...])[0]
        u = u_ref[...]
        dyb, ub = dy.astype(BF16), u.astype(BF16)
        a_r, a_i = ar_ref[...], -ai_ref[...]
        g = jnp.dot(dyb, c_ref[...].astype(BF16), preferred_element_type=F32)
        gr, gi = g[:, :STATE_BLOCK], g[:, STATE_BLOCK:]
        nxt_r, nxt_i = cr[...], ci[...]
        last = lax.broadcasted_iota(jnp.int32, gr.shape, 0) == t - 1
        lr, li, cr[...], ci[...] = _scan(gr, gi, a_r, a_i, nxt_r, nxt_i, False)
        nr = _shift_rows(lr, 1, False) + jnp.where(last, nxt_r, 0.0)
        ni = _shift_rows(li, 1, False) + jnp.where(last, nxt_i, 0.0)
        s_r, s_i = sr_ref[...], si_ref[...]
        dar = _colsum(s_r * nr + s_i * ni)
        dai = _colsum(s_r * ni - s_i * nr)
        lam = jnp.concatenate([lr, li], axis=1).astype(BF16)
        sb = jnp.concatenate([s_r, s_i], axis=1).astype(BF16)
        dc = lax.dot_general(dyb, sb, (((0,), (0,)), ((), ())), preferred_element_type=F32)
        db = lax.dot_general(ub, lam, (((0,), (0,)), ((), ())), preferred_element_type=F32)
        du_ref[...] = (lax.dot_general(lam, b_ref[...].astype(BF16), (((1,), (1,)), ((), ())),
                                       preferred_element_type=F32) + d_ref[...] * dy).astype(du_ref.dtype)
        dd = _colsum(dy * u)

        @pl.when(step == 0)
        def _():
            db_ref[...] = db
            dc_ref[...] = dc
            dar_ref[...] = dar
            dai_ref[...] = dai
            dd_ref[...] = dd

        @pl.when(step > 0)
        def _():
            db_ref[...] += db
            dc_ref[...] += dc
            dar_ref[...] += dar
            dai_ref[...] += dai
            dd_ref[...] += dd

    rev = lambda b, c: (nch - 1 - c, b)
    return _call(
        body, name=name, grid=(S5_BLOCKS, nch),
        in_specs=[pl.BlockSpec((t, LANE_BLOCK), rev), pl.BlockSpec((t, LANE_BLOCK), rev),
                  pl.BlockSpec((t, LANE_BLOCK), rev), pl.BlockSpec((t, LANE_BLOCK), rev),
                  pl.BlockSpec((t, STATE_BLOCK), rev), pl.BlockSpec((t, STATE_BLOCK), rev),
                  pl.BlockSpec((None, LANE_BLOCK, 2 * STATE_BLOCK), lambda b, c: (b, 0, 0)),
                  pl.BlockSpec((None, LANE_BLOCK, 2 * STATE_BLOCK), lambda b, c: (b, 0, 0)),
                  pl.BlockSpec((1, STATE_BLOCK), lambda b, c: (0, b)),
                  pl.BlockSpec((1, STATE_BLOCK), lambda b, c: (0, b)),
                  pl.BlockSpec((1, LANE_BLOCK), lambda b, c: (0, b)),
                  pl.BlockSpec(memory_space=pl.ANY)],
        out_specs=[pl.BlockSpec((t, LANE_BLOCK), rev),
                   pl.BlockSpec((None, LANE_BLOCK, 2 * STATE_BLOCK), lambda b, c: (b, 0, 0)),
                   pl.BlockSpec((None, LANE_BLOCK, 2 * STATE_BLOCK), lambda b, c: (b, 0, 0)),
                   pl.BlockSpec((1, STATE_BLOCK), lambda b, c: (0, b)),
                   pl.BlockSpec((1, STATE_BLOCK), lambda b, c: (0, b)),
                   pl.BlockSpec((1, LANE_BLOCK), lambda b, c: (0, b))],
        out_shape=[jax.ShapeDtypeStruct(dp.shape, dp.dtype),
                   jax.ShapeDtypeStruct((S5_BLOCKS, LANE_BLOCK, 2 * STATE_BLOCK), F32),
                   jax.ShapeDtypeStruct((S5_BLOCKS, LANE_BLOCK, 2 * STATE_BLOCK), F32),
                   jax.ShapeDtypeStruct((1, nst), F32), jax.ShapeDtypeStruct((1, nst), F32),
                   jax.ShapeDtypeStruct((1, D_MODEL), F32)],
        scratch=[pltpu.VMEM((1, STATE_BLOCK), F32), pltpu.VMEM((1, STATE_BLOCK), F32)],
        aliases={11: 0}, args=(dy_direct, dyg, y0, uz, sr, si, bblk, cblk, ar, ai, dskip, dp),
        semantics=("parallel", "arbitrary"), job=job)


def _s5_prep(lam_re, lam_im, log_dt, b_re, b_im, c_re, c_im):
    dt = jnp.exp(log_dt)
    mag = jnp.exp(lam_re * dt)
    ar = mag * jnp.cos(lam_im * dt)
    ai = mag * jnp.sin(lam_im * dt)
    den = lam_re * lam_re + lam_im * lam_im
    kr = ((ar - 1.0) * lam_re + ai * lam_im) / den
    ki = (ai * lam_re - (ar - 1.0) * lam_im) / den
    bbr = kr[:, None, :] * b_re - ki[:, None, :] * b_im
    bbi = kr[:, None, :] * b_im + ki[:, None, :] * b_re
    rows = S5_GROUPS * S5_GROUP
    expand = (lax.broadcasted_iota(jnp.int32, (S5_STATE, STATE_BLOCK), 1) % S5_STATE
              == lax.broadcasted_iota(jnp.int32, (S5_STATE, STATE_BLOCK), 0)).astype(F32)
    own = ((lax.broadcasted_iota(jnp.int32, (rows, STATE_BLOCK), 0) // S5_GROUP) % GROUPS_PER_BLOCK
           == lax.broadcasted_iota(jnp.int32, (rows, STATE_BLOCK), 1) // S5_STATE).astype(F32)

    def blocks(v):
        wide = jnp.dot(v.reshape(rows, S5_STATE), expand, precision=lax.Precision.HIGHEST,
                       preferred_element_type=F32)
        return (wide * own).reshape(S5_BLOCKS, LANE_BLOCK, STATE_BLOCK)

    btab = jnp.concatenate([blocks(bbr), blocks(bbi)], axis=2)
    ctab = jnp.concatenate([blocks(c_re), -blocks(c_im)], axis=2)
    return ar, ai, btab, ctab


_S5_PREP_OUT = [(S5_GROUPS, S5_STATE), (S5_GROUPS, S5_STATE), (S5_BLOCKS, LANE_BLOCK, 2 * STATE_BLOCK),
                (S5_BLOCKS, LANE_BLOCK, 2 * STATE_BLOCK)]


def s5_prep(params, *, name):
    def body(*refs):
        for ref, val in zip(refs[7:], _s5_prep(*[r[...] for r in refs[:7]]), strict=True):
            ref[...] = val

    return pl.pallas_call(body, name=name, out_shape=[jax.ShapeDtypeStruct(s, F32) for s in _S5_PREP_OUT],
                          compiler_params=pltpu.CompilerParams(vmem_limit_bytes=VMEM_LIMIT))(*params)


def s5_prep_bwd(params, cotangents, *, name):
    def body(*refs):
        grads = jax.vjp(_s5_prep, *[r[...] for r in refs[:7]])[1](tuple(r[...] for r in refs[7:11]))
        for ref, val in zip(refs[11:], grads, strict=True):
            ref[...] = val

    return pl.pallas_call(body, name=name, out_shape=[jax.ShapeDtypeStruct(p.shape, F32) for p in params],
                          compiler_params=pltpu.CompilerParams(vmem_limit_bytes=VMEM_LIMIT))(*params, *cotangents)


def _silu_grad(z):
    s = jax.nn.sigmoid(z)
    return s * (1.0 + z * (1.0 - s))


def conv_fwd(p, conv_w, conv_b, *, name):
    lp = p.shape[0]
    tr = lp // 16
    e = CONV_E
    hb = tr // 8

    def body(p_ref, cgh_ref, vh_ref, w_ref, b_ref, o_ref):
        i = pl.program_id(0)
        bg, cg, v, z = (p_ref[:, k * e:(k + 1) * e] for k in range(4))
        hc = cg * v
        halo = jnp.where(i > 0, cgh_ref[...] * vh_ref[...], 0.0)
        ext = jnp.concatenate([halo, hc], axis=0)
        w = w_ref[...]
        conv = (w[0:1] * pltpu.roll(ext, 2, 0)[8:] + w[1:2] * pltpu.roll(ext, 1, 0)[8:] + w[2:3] * hc
                + b_ref[...])
        o_ref[...] = (bg * conv * jax.nn.silu(z)).astype(o_ref.dtype)

    prev = lambda col: (lambda i: (jnp.maximum(i * hb - 1, 0), col))
    return pl.pallas_call(
        body, name=name, grid=(lp // tr,),
        in_specs=[pl.BlockSpec((tr, 4 * e), lambda i: (i, 0)),
                  pl.BlockSpec((8, e), prev(1)), pl.BlockSpec((8, e), prev(2)),
                  pl.BlockSpec((3, e), lambda i: (0, 0)), pl.BlockSpec((1, e), lambda i: (0, 0))],
        out_specs=pl.BlockSpec((tr, e), lambda i: (i, 0)),
        out_shape=jax.ShapeDtypeStruct((lp, e), BF16),
        compiler_params=_params("parallel"))(p, p, p, conv_w, conv_b)


def conv_bwd(dy3, p, conv_w, conv_b, *, name):
    lp = p.shape[0]
    tr = lp // 16
    e = CONV_E
    hb = tr // 8
    nt = lp // tr
    width = 512

    def body(dy_ref, p_ref, cgh_ref, vh_ref, dyn_ref, bgn_ref, zn_ref, w_ref, b_ref, dp_ref, dw_ref, db_ref):
        i = pl.program_id(0)
        for c0 in range(0, e, width):
            cols = slice(c0, c0 + width)
            dy = dy_ref[:, cols]
            bg, cg, v, z = (p_ref[:, k * e + c0:k * e + c0 + width] for k in range(4))
            w = w_ref[:, cols]
            hc = cg * v
            halo = jnp.where(i > 0, cgh_ref[:, cols] * vh_ref[:, cols], 0.0)
            ext = jnp.concatenate([halo, hc], axis=0)
            hc2, hc1 = pltpu.roll(ext, 2, 0)[8:], pltpu.roll(ext, 1, 0)[8:]
            yc = w[0:1] * hc2 + w[1:2] * hc1 + w[2:3] * hc + b_ref[:, cols]
            sz = jax.nn.silu(z)
            dyc = dy * bg * sz
            dyc_n = jnp.where(i < nt - 1, dyn_ref[:, cols] * bgn_ref[:, cols] * jax.nn.silu(zn_ref[:, cols]), 0.0)
            ext_n = jnp.concatenate([dyc, dyc_n], axis=0)
            n_rows = tr + 8
            dhc = (w[2:3] * dyc + w[1:2] * pltpu.roll(ext_n, n_rows - 1, 0)[:tr]
                   + w[0:1] * pltpu.roll(ext_n, n_rows - 2, 0)[:tr])
            for k, val in enumerate((dy * yc * sz, dhc * v, dhc * cg, dy * bg * yc * _silu_grad(z))):
                dp_ref[:, k * e + c0:k * e + c0 + width] = val.astype(dp_ref.dtype)
            dw = jnp.concatenate([_colsum(dyc * hc2), _colsum(dyc * hc1), _colsum(dyc * hc)], axis=0)
            db = _colsum(dyc)

            @pl.when(i == 0)
            def _(cols=cols, dw=dw, db=db):
                dw_ref[:, cols] = dw
                db_ref[:, cols] = db

            @pl.when(i > 0)
            def _(cols=cols, dw=dw, db=db):
                dw_ref[:, cols] += dw
                db_ref[:, cols] += db

    prev = lambda col: (lambda i: (jnp.maximum(i * hb - 1, 0), col))
    nxt = lambda col: (lambda i: (jnp.minimum((i + 1) * hb, lp // 8 - 1), col))
    return pl.pallas_call(
        body, name=name, grid=(nt,),
        in_specs=[pl.BlockSpec((tr, e), lambda i: (i, 0)), pl.BlockSpec((tr, 4 * e), lambda i: (i, 0)),
                  pl.BlockSpec((8, e), prev(1)), pl.BlockSpec((8, e), prev(2)),
                  pl.BlockSpec((8, e), nxt(0)), pl.BlockSpec((8, e), nxt(0)), pl.BlockSpec((8, e), nxt(3)),
                  pl.BlockSpec((3, e), lambda i: (0, 0)), pl.BlockSpec((1, e), lambda i: (0, 0))],
        out_specs=[pl.BlockSpec((tr, 4 * e), lambda i: (i, 0)), pl.BlockSpec((3, e), lambda i: (0, 0)),
                   pl.BlockSpec((1, e), lambda i: (0, 0))],
        out_shape=[jax.ShapeDtypeStruct((lp, 4 * e), BF16), jax.ShapeDtypeStruct((3, e), F32),
                   jax.ShapeDtypeStruct((1, e), F32)],
        compiler_params=_params("arbitrary"))(dy3, p, p, p, dy3, p, p, conv_w, conv_b)


def _pool_counts(i, tr, rows, offset, w):
    t = (i * tr + offset + 1 + lax.broadcasted_iota(jnp.int32, (rows, 1), 0)).astype(F32)
    return jnp.minimum(t, float(w))


def pool_fwd(p, *, name):
    lp = p.shape[0]
    tr = lp // 16
    h = POOL_HALO
    hb = tr // h

    def body(u_ref, uh_ref, o_ref):
        i = pl.program_id(0)
        u = u_ref[...]
        ext = jnp.concatenate([jnp.where(i > 0, uh_ref[...], 0.0), u], axis=0)
        for k, w in enumerate(POOL_WINDOWS):
            cols = slice(k * POOL_GROUP, (k + 1) * POOL_GROUP)
            s = ext[:, cols]
            d = 1
            while d < w:
                s = s + pltpu.roll(s, d, 0)
                d *= 2
            o_ref[:, cols] = (s[h:] / _pool_counts(i, tr, tr, 0, w) - u[:, cols]).astype(o_ref.dtype)

    return pl.pallas_call(
        body, name=name, grid=(lp // tr,),
        in_specs=[pl.BlockSpec((tr, POOL_E), lambda i: (i, 0)),
                  pl.BlockSpec((h, POOL_E), lambda i: (jnp.maximum(i * hb - 1, 0), 0))],
        out_specs=pl.BlockSpec((tr, POOL_E), lambda i: (i, 0)),
        out_shape=jax.ShapeDtypeStruct((lp, POOL_E), BF16),
        compiler_params=_params("parallel"))(p, p)


def pool_bwd(dmix, dp, *, name):
    lp = dmix.shape[0]
    tr = lp // 16
    h = POOL_HALO
    hb = tr // h
    nt = lp // tr

    def body(dm_ref, dmn_ref, _, du_ref):
        i = pl.program_id(0)
        dm = dm_ref[...]
        dmn = jnp.where(i < nt - 1, dmn_ref[...], 0.0)
        n_rows = tr + h
        for k, w in enumerate(POOL_WINDOWS):
            cols = slice(k * POOL_GROUP, (k + 1) * POOL_GROUP)
            s = jnp.concatenate([dm[:, cols] / _pool_counts(i, tr, tr, 0, w),
                                 dmn[:, cols] / _pool_counts(i, tr, h, tr, w)], axis=0)
            d = 1
            while d < w:
                s = s + pltpu.roll(s, n_rows - d, 0)
                d *= 2
            du_ref[:, cols] = (s[:tr] - dm[:, cols]).astype(du_ref.dtype)

    return pl.pallas_call(
        body, name=name, grid=(nt,),
        in_specs=[pl.BlockSpec((tr, POOL_E), lambda i: (i, 0)),
                  pl.BlockSpec((h, POOL_E), lambda i: (jnp.minimum((i + 1) * hb, lp // h - 1), 0)),
                  pl.BlockSpec(memory_space=pl.ANY)],
        out_specs=pl.BlockSpec((tr, POOL_E), lambda i: (i, 0)),
        out_shape=jax.ShapeDtypeStruct(dp.shape, dp.dtype),
        input_output_aliases={2: 0},
        compiler_params=_params("parallel"))(dmix, dmix, dp)


def _adamw_math(w, g, m, v):
    nm = ADAM_B1 * m + (1.0 - ADAM_B1) * g
    nv = ADAM_B2 * v + (1.0 - ADAM_B2) * jnp.square(g)
    m_hat = nm / (1.0 - ADAM_B1 ** ADAM_STEP)
    v_hat = nv / (1.0 - ADAM_B2 ** ADAM_STEP)
    return -ADAM_LR * (m_hat / (jnp.sqrt(v_hat) + ADAM_EPS) + ADAM_WD * w), nm, nv


def adamw(w, g, m, v, *, name):
    r, c = w.shape
    tr = _tile(r, max(8, (1 << 19) // c), 8)

    def body(w_ref, g_ref, m_ref, v_ref, d_ref, nm_ref, nv_ref):
        d_ref[...], nm_ref[...], nv_ref[...] = _adamw_math(w_ref[...], g_ref[...], m_ref[...], v_ref[...])

    spec = pl.BlockSpec((tr, c), lambda i: (i, 0))
    return pl.pallas_call(
        body, name=name, grid=(r // tr,), in_specs=[spec] * 4, out_specs=[spec] * 3,
        out_shape=[jax.ShapeDtypeStruct((r, c), F32)] * 3,
        compiler_params=_params("parallel"))(w, g, m, v)


def adamw_many(quads, *, name):
    shape = quads[0][0].shape
    steps = 8
    spec = pl.BlockSpec((shape[0] // steps,) + shape[1:], lambda i: (i, 0, 0))
    n = len(quads)

    def body(*refs):
        for q in range(n):
            w_ref, g_ref, m_ref, v_ref = refs[4 * q:4 * q + 4]
            d_ref, nm_ref, nv_ref = refs[4 * n + 3 * q:4 * n + 3 * q + 3]
            d_ref[...], nm_ref[...], nv_ref[...] = _adamw_math(w_ref[...], g_ref[...], m_ref[...], v_ref[...])

    res = pl.pallas_call(
        body, name=name, grid=(steps,), in_specs=[spec] * (4 * n), out_specs=[spec] * (3 * n),
        out_shape=[jax.ShapeDtypeStruct(shape, F32)] * (3 * n),
        compiler_params=_params("parallel"))(*[a for quad in quads for a in quad])
    return [tuple(res[3 * q:3 * q + 3]) for q in range(n)]


WIRE_WIDTH = 1024
WIRE_ROWS = 1024


def _wire_plan(shapes):
    starts, row = [], 0
    for s in shapes:
        r, c = (1, s[0]) if len(s) == 1 else s
        starts.append(row)
        row += -(-(r * max(1, c // WIRE_WIDTH)) // 8) * 8
    assert row <= WIRE_ROWS, row
    return starts


def _wire_cells(shape):
    if len(shape) == 1:
        n = shape[0]
        if n <= WIRE_WIDTH:
            return [(0, n, (slice(None),))]
        return [(h, WIRE_WIDTH, (slice(h * WIRE_WIDTH, (h + 1) * WIRE_WIDTH),)) for h in range(n // WIRE_WIDTH)]
    r, c = shape
    if c <= WIRE_WIDTH:
        return [(None, c, (slice(None), slice(None)))]
    per = c // WIRE_WIDTH
    return [(j * per + h, WIRE_WIDTH, (j, slice(h * WIRE_WIDTH, (h + 1) * WIRE_WIDTH)))
            for j in range(r) for h in range(per)]


def pack_small(arrays, *, name):
    shapes = [a.shape for a in arrays]
    starts = _wire_plan(shapes)

    def body(*refs):
        out = refs[-1]
        out[...] = jnp.zeros_like(out)
        for ref, shape, row0 in zip(refs[:-1], shapes, starts, strict=True):
            for off, cols, idx in _wire_cells(shape):
                if off is None:
                    out[row0:row0 + shape[0], 0:cols] = ref[idx]
                else:
                    out[row0 + off, 0:cols] = ref[idx]

    return pl.pallas_call(body, name=name, out_shape=jax.ShapeDtypeStruct((WIRE_ROWS, WIRE_WIDTH), F32),
                          compiler_params=pltpu.CompilerParams(vmem_limit_bytes=VMEM_LIMIT))(*arrays)


def update_small(packed, starts, triples, shard_cells, chip, *, name):
    n = len(triples)

    def body(chip_ref, p_ref, *refs):
        k = chip_ref[0]
        ins, outs = refs[:3 * n], refs[3 * n:]

        def cut(row0, rows, cols):
            c0, c1 = cols
            if isinstance(rows, int):
                return p_ref[row0 + rows, c0:c1]
            return p_ref[row0 + rows[0]:row0 + rows[1], c0:c1]

        for q in range(n):
            w_ref, m_ref, v_ref = ins[3 * q:3 * q + 3]
            g_ref, d_ref, nm_ref, nv_ref = outs[4 * q:4 * q + 4]
            shape = w_ref.shape
            if shard_cells[q] is None:
                pieces = [(cut(starts[q], (0, shape[0]) if off is None else off, (0, cols)), idx)
                          for off, cols, idx in _wire_cells(shape)]
            else:
                by_chip = [shard_cells[q](kk) for kk in range(N_CHIPS)]
                pieces = []
                for i, (_, _, idx) in enumerate(by_chip[0]):
                    g = cut(starts[q], *by_chip[0][i][:2])
                    for kk in range(1, N_CHIPS):
                        g = jnp.where(k == kk, cut(starts[q], *by_chip[kk][i][:2]), g)
                    pieces.append((g, idx))
            for g, idx in pieces:
                g_ref[idx] = g
                d_ref[idx], nm_ref[idx], nv_ref[idx] = _adamw_math(w_ref[idx], g, m_ref[idx], v_ref[idx])

    whole = lambda a: pl.BlockSpec(a.shape, lambda i, c_ref, nd=a.ndim: (0,) * nd)
    flat = [a for t in triples for a in t]
    out_shape = [jax.ShapeDtypeStruct(t[0].shape, F32) for t in triples for _ in range(4)]
    res = pl.pallas_call(
        body, name=name,
        grid_spec=pltpu.PrefetchScalarGridSpec(
            num_scalar_prefetch=1, grid=(1,),
            in_specs=[whole(packed)] + [whole(a) for a in flat],
            out_specs=[pl.BlockSpec(s.shape, lambda i, c_ref, nd=len(s.shape): (0,) * nd) for s in out_shape]),
        out_shape=out_shape,
        compiler_params=_params("arbitrary"))(chip.reshape(1).astype(jnp.int32), packed, *flat)
    return [tuple(res[4 * q:4 * q + 4]) for q in range(n)]


ANY = pl.BlockSpec(memory_space=pl.ANY)


def _place():
    x, y, c = lax.axis_index("x"), lax.axis_index("y"), lax.axis_index("c")
    return x, y, c, [(1 - x, y), (x, 1 - y), (1 - x, 1 - y)]


DMA_PIECE_BYTES = 512 * 1024


def _pieces(src, dst):
    shape = src.shape
    rows, cols = shape[-2], shape[-1]
    item = jnp.dtype(src.dtype).itemsize
    unit = 32 // item
    want = max(1, (rows * cols * item) // DMA_PIECE_BYTES)
    n = 1
    if rows % unit == 0:
        n = max(d for d in range(1, rows // unit + 1) if (rows // unit) % d == 0 and d <= want)
    size = rows // n
    out = []
    for lead in (range(shape[0]) if len(shape) == 3 else [None]):
        for i in range(n):
            sel = (pl.ds(i * size, size), slice(None)) if lead is None else (lead, pl.ds(i * size, size), slice(None))
            out.append((src.at[sel], dst.at[sel]))
    return out


def _send(src, dst, send_sem, recv_sem, peer, start=True):
    if start:
        for s, d in _pieces(src, dst):
            pltpu.make_async_remote_copy(src_ref=s, dst_ref=d, send_sem=send_sem, recv_sem=recv_sem,
                                         device_id=peer, device_id_type=MESH).start()
    return pltpu.make_async_remote_copy(src_ref=src, dst_ref=dst, send_sem=send_sem, recv_sem=recv_sem,
                                        device_id=peer, device_id_type=MESH)


def run_job(job, *, name):
    n_in, n_out = len(job.arrays), len(job.out_shapes)

    def body(*refs):
        job.fn(refs[:n_in], refs[n_in:n_in + n_out], refs[n_in + n_out:], "both")

    return pl.pallas_call(body, name=name, in_specs=[ANY] * n_in, out_specs=[ANY] * n_out,
                          out_shape=job.out_shapes, scratch_shapes=job.sems)(*job.arrays)


def gather_chips(shards, *, name):
    n = len(shards)

    def body(*refs):
        ins, outs = refs[:n], refs[n:2 * n]
        send_sems, recv_sems = refs[2 * n:]
        x, y, c, chips = _place()
        k = 2 * x + y
        copies = []
        for a in range(n):
            for j, peer in enumerate([(px, py, c) for px, py in chips] + [(x, y, 1 - c)]):
                copies.append(_send(ins[a], outs[a].at[k], send_sems.at[a, j], recv_sems.at[a, j], peer))
        for cp in copies:
            cp.wait()

    return pl.pallas_call(
        body, name=name, in_specs=[ANY] * n, out_specs=[ANY] * n,
        out_shape=[jax.ShapeDtypeStruct((N_CHIPS,) + s.shape, s.dtype) for s in shards],
        scratch_shapes=[pltpu.SemaphoreType.DMA((n, 4)), pltpu.SemaphoreType.DMA((n, 4))])(*shards)


def gather_by_halves_job(shards):
    n = len(shards)

    def fn(ins, outs, sems, phase):
        send_sems, recv_sems = sems
        x, y, c, chips = _place()
        k = 2 * x + y
        sibling = (x, y, 1 - c)
        begin = phase != "finish"
        waits, arrivals = [], []
        for a in range(n):
            half = ins[a].shape[0] // 2
            waits.append(_send(ins[a], outs[a].at[k], send_sems.at[a, 6], recv_sems.at[a, 6], sibling, begin))
            mine = pl.ds(c * half, half)
            for j, (px, py) in enumerate(chips):
                arrivals.append(_send(ins[a].at[mine, :], outs[a].at[k, mine, :], send_sems.at[a, j],
                                      recv_sems.at[a, j], (px, py, c), begin))
        if phase == "start":
            return
        i = 0
        for a in range(n):
            half = ins[a].shape[0] // 2
            mine = pl.ds(c * half, half)
            for j, (px, py) in enumerate(chips):
                arrivals[i].wait_recv()
                landed = outs[a].at[2 * px + py, mine, :]
                waits.append(_send(landed, landed, send_sems.at[a, 3 + j], recv_sems.at[a, 3 + j], sibling))
                i += 1
        for cp in arrivals:
            cp.wait_send()
        for cp in waits:
            cp.wait()

    return Job(shards, [jax.ShapeDtypeStruct((N_CHIPS,) + s.shape, s.dtype) for s in shards],
               [pltpu.SemaphoreType.DMA((n, 7)), pltpu.SemaphoreType.DMA((n, 7))], fn)


def swap_halves(grads, *, name):
    n = len(grads)

    def body(*refs):
        ins, outs = refs[:n], refs[n:2 * n]
        send_sems, recv_sems = refs[2 * n:]
        x, y, c, _ = _place()
        copies = []
        for a in range(n):
            half = ins[a].shape[1] // 2
            copies.append(_send(ins[a].at[:, pl.ds((1 - c) * half, half), :], outs[a], send_sems.at[a],
                                recv_sems.at[a], (x, y, 1 - c)))
        for cp in copies:
            cp.wait()

    return pl.pallas_call(
        body, name=name, in_specs=[ANY] * n, out_specs=[ANY] * n,
        out_shape=[jax.ShapeDtypeStruct((g.shape[0], g.shape[1] // 2, g.shape[2]), g.dtype) for g in grads],
        scratch_shapes=[pltpu.SemaphoreType.DMA((n,)), pltpu.SemaphoreType.DMA((n,))])(*grads)


def _chip_slot(s, k):
    rel = s ^ k
    return (rel >> 1) | ((rel & 1) << 1)


def sum_cores(g, theirs, ck, out_dtype, *, name):
    n, r, cols = g.shape
    half = r // 2
    tr = _tile(half, max(16, (1 << 19) // cols), 16)
    nb = half // tr

    def body(ck_ref, g_ref, t_ref, o_ref):
        o_ref[...] = (g_ref[...] + t_ref[...]).astype(o_ref.dtype)

    return pl.pallas_call(
        body, name=name,
        grid_spec=pltpu.PrefetchScalarGridSpec(
            num_scalar_prefetch=1, grid=(n, nb),
            in_specs=[pl.BlockSpec((None, tr, cols), lambda s, i, ck_ref: (s, ck_ref[0] * nb + i, 0)),
                      pl.BlockSpec((None, tr, cols), lambda s, i, ck_ref: (s, i, 0))],
            out_specs=pl.BlockSpec((None, tr, cols), lambda s, i, ck_ref: (_chip_slot(s, ck_ref[1]), i, 0))),
        out_shape=jax.ShapeDtypeStruct((n, half, cols), out_dtype),
        compiler_params=_params("parallel", "parallel"))(ck, g, theirs)


def scatter_job(parts):
    n = len(parts)

    def fn(ins, outs, sems, phase):
        send_sems, recv_sems = sems
        x, y, c, chips = _place()
        copies = []
        for a in range(n):
            for j, (px, py) in enumerate(chips):
                copies.append(_send(ins[a].at[1 + j], outs[a].at[j], send_sems.at[a, j], recv_sems.at[a, j],
                                    (px, py, c), phase != "finish"))
        if phase != "start":
            for cp in copies:
                cp.wait()

    return Job(parts, [jax.ShapeDtypeStruct((3,) + p.shape[1:], p.dtype) for p in parts],
               [pltpu.SemaphoreType.DMA((n, 3)), pltpu.SemaphoreType.DMA((n, 3))], fn)


def sum_chips(own, others, ck, *, name):
    _, r, cols = own.shape
    tr = _tile(r, max(16, (1 << 19) // cols), 16)

    def body(ck_ref, a_ref, b0_ref, b1_ref, b2_ref, o_ref):
        o_ref[...] = (a_ref[...].astype(F32) + b0_ref[...].astype(F32) + b1_ref[...].astype(F32)
                      + b2_ref[...].astype(F32))

    other = lambda j: pl.BlockSpec((None, tr, cols), lambda i, ck_ref: (j, i, 0))
    return pl.pallas_call(
        body, name=name,
        grid_spec=pltpu.PrefetchScalarGridSpec(
            num_scalar_prefetch=1, grid=(r // tr,),
            in_specs=[pl.BlockSpec((None, tr, cols), lambda i, ck_ref: (0, i, 0)), other(0), other(1), other(2)],
            out_specs=pl.BlockSpec((None, tr, cols), lambda i, ck_ref: (ck_ref[0], i, 0))),
        out_shape=jax.ShapeDtypeStruct((2, r, cols), F32),
        compiler_params=_params("parallel"))(ck, own, others, others, others)


def share_halves(joined, *, name):
    n = len(joined)

    def body(*refs):
        outs = refs[n:2 * n]
        send_sems, recv_sems = refs[2 * n:]
        x, y, c, _ = _place()
        copies = [_send(outs[a].at[c], outs[a].at[c], send_sems.at[a], recv_sems.at[a], (x, y, 1 - c))
                  for a in range(n)]
        for cp in copies:
            cp.wait()

    return pl.pallas_call(
        body, name=name, in_specs=[ANY] * n, out_specs=[ANY] * n,
        out_shape=[jax.ShapeDtypeStruct(j.shape, j.dtype) for j in joined],
        input_output_aliases={a: a for a in range(n)},
        scratch_shapes=[pltpu.SemaphoreType.DMA((n,)), pltpu.SemaphoreType.DMA((n,))])(*joined)


def kernel(x, meta_tokens, norm0_g, l0_w_in, l0_lam_re, l0_lam_im, l0_log_dt, l0_b_re, l0_b_im, l0_c_re, l0_c_im, l0_d_skip, l0_w_glu, l0_b_glu, l0_w_out, norm1_g, l1_w_in, l1_conv_w, l1_conv_b, l1_w_out, norm2_g, l2_w_in, l2_w_grp, l2_b_grp, l2_scale, l2_w_out, norm3_g, l3_w_in, l3_lam_re, l3_lam_im, l3_log_dt, l3_b_re, l3_b_im, l3_c_re, l3_c_im, l3_d_skip, l3_w_glu, l3_b_glu, l3_w_out, final_g, loss_target, m_meta_tokens, m_norm0_g, m_l0_w_in, m_l0_lam_re, m_l0_lam_im, m_l0_log_dt, m_l0_b_re, m_l0_b_im, m_l0_c_re, m_l0_c_im, m_l0_d_skip, m_l0_w_glu, m_l0_b_glu, m_l0_w_out, m_norm1_g, m_l1_w_in, m_l1_conv_w, m_l1_conv_b, m_l1_w_out, m_norm2_g, m_l2_w_in, m_l2_w_grp, m_l2_b_grp, m_l2_scale, m_l2_w_out, m_norm3_g, m_l3_w_in, m_l3_lam_re, m_l3_lam_im, m_l3_log_dt, m_l3_b_re, m_l3_b_im, m_l3_c_re, m_l3_c_im, m_l3_d_skip, m_l3_w_glu, m_l3_b_glu, m_l3_w_out, m_final_g, v_meta_tokens, v_norm0_g, v_l0_w_in, v_l0_lam_re, v_l0_lam_im, v_l0_log_dt, v_l0_b_re, v_l0_b_im, v_l0_c_re, v_l0_c_im, v_l0_d_skip, v_l0_w_glu, v_l0_b_glu, v_l0_w_out, v_norm1_g, v_l1_w_in, v_l1_conv_w, v_l1_conv_b, v_l1_w_out, v_norm2_g, v_l2_w_in, v_l2_w_grp, v_l2_b_grp, v_l2_scale, v_l2_w_out, v_norm3_g, v_l3_w_in, v_l3_lam_re, v_l3_lam_im, v_l3_log_dt, v_l3_b_re, v_l3_b_im, v_l3_c_re, v_l3_c_im, v_l3_d_skip, v_l3_w_glu, v_l3_b_glu, v_l3_w_out, v_final_g):
    given = dict(locals())
    w = {n: given[n] for n in WEIGHTS}
    mom_m = {n: given["m_" + n] for n in WEIGHTS}
    mom_v = {n: given["v_" + n] for n in WEIGHTS}
    seq = x.shape[1]
    real = N_META + seq
    lp = -(-real // SCAN_ROWS) * SCAN_ROWS
    chip = 2 * lax.axis_index("x") + lax.axis_index("y")
    row = lambda a: a.reshape(1, -1)

    shard_bf16 = {n: (w[n].reshape(-1, w[n].shape[-1]) if n == 'l2_w_grp' else w[n]).astype(BF16) for n in BIG}
    layer = lambda i: [n for n in BIG if n.startswith("l%d_" % i)]
    gather_layer = lambda i: gather_by_halves_job([shard_bf16[n] for n in layer(i)])
    full = {}

    def landed(i, arrays):
        for n, arr in zip(layer(i), arrays, strict=True):
            if n.endswith('w_in'):
                full[n] = arr
            elif n == 'l2_w_grp':
                full[n] = arr.reshape(N_CHIPS, len(POOL_WINDOWS), POOL_GROUP // N_CHIPS, POOL_GROUP)
            else:
                full[n] = arr.reshape(1, -1, arr.shape[-1])

    landed(0, run_job(gather_layer(0), name="gather_l0"))
    full.update(zip(SMALL_SHARDED, gather_chips([w[n] for n in SMALL_SHARDED], name="gather_small_shards"),
                    strict=True))
    meta_full = full['meta_tokens'].transpose(1, 0, 2).reshape(N_META, D_MODEL)
    conv_w_full = full['l1_conv_w'].transpose(1, 0, 2).reshape(3, CONV_E)
    b_grp_full = full['l2_b_grp'].transpose(1, 0, 2).reshape(len(POOL_WINDOWS), POOL_GROUP)

    h0 = jnp.concatenate([meta_full, x[0], jnp.zeros((lp - real, D_MODEL), F32)], axis=0)
    target = jnp.concatenate([jnp.zeros((N_META, D_MODEL), F32), loss_target[0],
                              jnp.zeros((lp - real, D_MODEL), F32)], axis=0)
    grads = {}
    saved = {}
    gip = lambda a: jnp.swapaxes(a, 1, 2)

    def s5_layer_fwd(i, h, gather=None):
        pre = "l%d_" % i
        prm = [w[pre + 'lam_re'], w[pre + 'lam_im'], w[pre + 'log_dt'].reshape(-1, 1), gip(w[pre + 'b_re']),
               gip(w[pre + 'b_im']), w[pre + 'c_re'], w[pre + 'c_im']]
        a_re, a_im, bblk, cblk = s5_prep(prm, name=pre + "prep")
        tables = (a_re.reshape(1, -1), a_im.reshape(1, -1), bblk, cblk)
        ar, ai = tables[:2]
        n = rms_fwd(h, row(w["norm%d_g" % i]), name=pre + "norm")
        uz = mm_nn(n, full[pre + 'w_in'], name=pre + "in")
        res = s5_fwd(uz, bblk, cblk, ar, ai, row(w[pre + 'd_skip']), name=pre + "scan",
                     job=None if gather is None else gather_layer(gather))
        if gather is not None:
            res, arrived = res
            landed(gather, arrived)
        y0, yg, sr, si = res
        q = mm_nn(yg, full[pre + 'w_glu'], bias=row(w[pre + 'b_glu']), name=pre + "glu")
        (y3,), _ = rowwise(lambda yv, qv, zv: ([_s5_gate(yv, qv, zv)], []),
                           [(y0, D_MODEL, 0), (q, D_MODEL, 0), (uz, D_MODEL, 1)], [],
                           [(D_MODEL, BF16, D_MODEL, 0)], name=pre + "gate")
        h_new = mm_nn(y3, full[pre + 'w_out'], add=h, name=pre + "out")
        saved[i] = dict(h=h, n=n, uz=uz, y0=y0, sr=sr, si=si, yg=yg, q=q, y3=y3, tables=tables, prm=prm)
        return h_new

    def s5_layer_bwd(i, dh, scatter=None):
        pre = "l%d_" % i
        s = saved[i]
        ar, ai, bblk, cblk = s['tables']
        grads[pre + 'w_out'] = mm_tn(s['y3'], dh, 1, name=pre + "d_w_out")[0]
        dy3 = mm_nt(dh, full[pre + 'w_out'], name=pre + "d_y3")

        def gate_bwd(dyv, yv, qv, zv):
            _, vjp = jax.vjp(_s5_gate, yv, qv, zv)
            dy0, dq, dz = vjp(dyv)
            return [dy0, dq, dz], [_colsum(dq)]

        (dy0_direct, dq, dp), (db_glu,) = rowwise(
            gate_bwd, [(dy3, D_MODEL, 0), (s['y0'], D_MODEL, 0), (s['q'], D_MODEL, 0), (s['uz'], D_MODEL, 1)], [],
            [(D_MODEL, F32, D_MODEL, 0), (D_MODEL, BF16, D_MODEL, 0), (2 * D_MODEL, BF16, D_MODEL, 1)], [D_MODEL],
            name=pre + "d_gate")
        grads[pre + 'b_glu'] = db_glu
        grads[pre + 'w_glu'] = mm_tn(s['yg'], dq, 1, name=pre + "d_w_glu")[0]
        dyg = mm_nt(dq, full[pre + 'w_glu'], name=pre + "d_yg")
        res = s5_bwd(dy0_direct, dyg, s['y0'], s['uz'], s['sr'], s['si'], bblk, cblk, ar, ai,
                     row(w[pre + 'd_skip']), dp, name=pre + "d_scan",
                     job=None if scatter is None else scatter_job(scatter_of(scatter)))
        if scatter is not None:
            res, arrived = res
            scattered(scatter, arrived)
        dp, dbblk, dcblk, dar, dai, dd = res
        grads[pre + 'd_skip'] = dd
        cots = (dar.reshape(S5_GROUPS, S5_STATE), dai.reshape(S5_GROUPS, S5_STATE), dbblk, dcblk)
        for key, val in zip(('lam_re', 'lam_im', 'log_dt', 'b_re', 'b_im', 'c_re', 'c_im'),
                            s5_prep_bwd(s['prm'], cots, name=pre + "d_prep"), strict=True):
            grads[pre + key] = val.reshape(-1) if key == 'log_dt' else val
        grads[pre + 'w_in'] = mm_tn(s['n'], dp, N_CHIPS, name=pre + "d_w_in")
        dn = mm_nt(dp, full[pre + 'w_in'], name=pre + "d_n")
        dh_in, dg = rms_bwd(s['h'], row(w["norm%d_g" % i]), dn, dh, name=pre + "d_norm")
        grads["norm%d_g" % i] = dg
        return dh_in

    h1 = s5_layer_fwd(0, h0, gather=1)

    n1 = rms_fwd(h1, row(w['norm1_g']), name="l1_norm")
    p1, arrived = mm_nn(n1, full['l1_w_in'], name="l1_in", job=gather_layer(2))
    landed(2, arrived)
    y3_1 = conv_fwd(p1, conv_w_full, row(w['l1_conv_b']), name="l1_conv")
    h2 = mm_nn(y3_1, full['l1_w_out'], add=h1, name="l1_out")

    n2 = rms_fwd(h2, row(w['norm2_g']), name="l2_norm")
    p2, arrived = mm_nn(n2, full['l2_w_in'], name="l2_in", job=gather_layer(3))
    landed(3, arrived)
    mixed = pool_fwd(p2, name="l2_pool")
    o2 = grp_nn(mixed, full['l2_w_grp'], b_grp_full.reshape(1, -1), name="l2_grp")
    (y3_2,), _ = rowwise(lambda ov, zv, sv: ([_pool_gate(ov, zv, sv)], []),
                         [(o2, POOL_E, 0), (p2, POOL_E, 1)], [row(w['l2_scale'])],
                         [(POOL_E, BF16, POOL_E, 0)], name="l2_gate")
    h3 = mm_nn(y3_2, full['l2_w_out'], add=h2, name="l2_out")

    h4 = s5_layer_fwd(3, h3)

    def loss_fn(hv, tv, gv):
        i = pl.program_id(0)
        tr = hv.shape[0]
        yf, vjp = jax.vjp(_rms, hv, gv)
        pos = i * tr + lax.broadcasted_iota(jnp.int32, (tr, 1), 0)
        diff = jnp.where((pos >= N_META) & (pos < real), yf - tv, 0.0)
        dh, dg = vjp(diff / D_MODEL)
        return [dh], [dg, _colsum(diff * diff)]

    (dh,), (d_final_g, sq) = rowwise(loss_fn, [(h4, D_MODEL, 0), (target, D_MODEL, 0)], [row(w['final_g'])],
                                     [(D_MODEL, F32, D_MODEL, 0)], [D_MODEL, D_MODEL], name="loss")
    grads['final_g'] = d_final_g
    loss = lax.psum(0.5 / D_MODEL * jnp.sum(sq), ("x", "y", "c"))

    ck = jnp.stack([lax.axis_index("c"), chip]).astype(jnp.int32)
    pair, from_chips = {}, {}

    def chip_major(n):
        g = grads[n]
        if n.endswith('w_in'):
            return g
        if n == 'l2_w_grp':
            return g.reshape(N_CHIPS, -1, POOL_GROUP)
        return g.reshape(N_CHIPS, -1, g.shape[-1])

    def reduce_cores(i, small=None):
        names, pieces, wire = layer(i), [chip_major(n) for n in layer(i)], [BF16] * len(layer(i))
        if small is not None:
            names, pieces, wire = names + ['small'], pieces + [small], wire + [F32]
        theirs = swap_halves(pieces, name="reduce_cores_l%d" % i)
        for n, g, t, dt in zip(names, pieces, theirs, wire, strict=True):
            pair[n] = sum_cores(g, t, ck, dt, name="sum_cores_" + n)

    scatter_of = lambda i: [pair[n] for n in layer(i)]

    def scattered(i, arrays):
        from_chips.update(zip(layer(i), arrays, strict=True))

    dh = s5_layer_bwd(3, dh)
    reduce_cores(3)

    grads['l2_w_out'] = mm_tn(y3_2, dh, 1, name="l2_d_w_out")[0]
    dy3 = mm_nt(dh, full['l2_w_out'], name="l2_d_y3")

    def pool_gate_bwd(dyv, ov, zv, sv):
        _, vjp = jax.vjp(_pool_gate, ov, zv, sv)
        do, dz, dscale = vjp(dyv)
        return [do, dz], [dscale, _colsum(do)]

    (do2, dp2), (dscale, dbgrp) = rowwise(
        pool_gate_bwd, [(dy3, POOL_E, 0), (o2, POOL_E, 0), (p2, POOL_E, 1)], [row(w['l2_scale'])],
        [(POOL_E, BF16, POOL_E, 0), (2 * POOL_E, BF16, POOL_E, 1)], [POOL_E, POOL_E], name="l2_d_gate")
    grads['l2_scale'] = dscale
    grads['l2_b_grp'] = dbgrp
    grads['l2_w_grp'] = grp_tn(mixed, do2, name="l2_d_w_grp")
    dmix = grp_nt(do2, full['l2_w_grp'], name="l2_d_mixed")
    dp2 = pool_bwd(dmix, dp2, name="l2_d_pool")
    grads['l2_w_in'], arrived = mm_tn(n2, dp2, N_CHIPS, name="l2_d_w_in", job=scatter_job(scatter_of(3)))
    scattered(3, arrived)
    dn = mm_nt(dp2, full['l2_w_in'], name="l2_d_n")
    dh, dg = rms_bwd(h2, row(w['norm2_g']), dn, dh, name="l2_d_norm")
    grads['norm2_g'] = dg
    reduce_cores(2)

    grads['l1_w_out'] = mm_tn(y3_1, dh, 1, name="l1_d_w_out")[0]
    dy3 = mm_nt(dh, full['l1_w_out'], name="l1_d_y3")
    dp1, dcw, dcb = conv_bwd(dy3, p1, conv_w_full, row(w['l1_conv_b']), name="l1_d_conv")
    grads['l1_conv_w'] = dcw
    grads['l1_conv_b'] = dcb
    grads['l1_w_in'], arrived = mm_tn(n1, dp1, N_CHIPS, name="l1_d_w_in", job=scatter_job(scatter_of(2)))
    scattered(2, arrived)
    dn = mm_nt(dp1, full['l1_w_in'], name="l1_d_n")
    dh, dg = rms_bwd(h1, row(w['norm1_g']), dn, dh, name="l1_d_norm")
    grads['norm1_g'] = dg
    reduce_cores(1)

    dh = s5_layer_bwd(0, dh, scatter=1)
    grad_x = dh[N_META:real][None]
    grads['meta_tokens'] = dh[:N_META]

    wide = [n for n in SMALL if w[n].ndim == 3]
    wire = [grads[n].reshape(S5_GROUPS, -1) if n in wide else grads[n] for n in SMALL]
    starts = dict(zip(SMALL, _wire_plan([a.shape for a in wire]), strict=True))
    small_local = pack_small(wire, name="pack_small")
    reduce_cores(0, small_local.reshape(N_CHIPS, -1, WIRE_WIDTH))
    rest = layer(0) + ['small']
    from_chips.update(zip(rest, run_job(scatter_job([pair[n] for n in rest]), name="reduce_chips_l0"), strict=True))
    joined = [sum_chips(pair[n], from_chips[n], ck, name="sum_chips_" + n) for n in BIG + ['small']]
    joined = share_halves(joined, name="share_cores")
    reduced = [j.reshape(-1, j.shape[-1]) for j in joined]
    g_big = {n: reduced[i].reshape(w[n].shape) for i, n in enumerate(BIG)}
    (small_all,) = gather_chips([reduced[-1]], name="gather_small")
    small_all = small_all.reshape(WIRE_ROWS, WIRE_WIDTH)

    out_g, out_d, out_m, out_v = {}, {}, {}, {}
    for n in BIG:
        shape = w[n].shape
        as2d = lambda a: a.reshape(-1, shape[-1])
        d, nm, nv = adamw(as2d(w[n]), as2d(g_big[n]), as2d(mom_m[n]), as2d(mom_v[n]), name="adamw_" + n)
        out_g[n], out_d[n], out_m[n], out_v[n] = g_big[n], d.reshape(shape), nm.reshape(shape), nv.reshape(shape)

    half_w = WIRE_WIDTH // 2
    shard_cells = {
        'meta_tokens': lambda k: [((0, N_META), (k * 256, (k + 1) * 256), (slice(None), slice(None)))],
        'l1_conv_w': lambda k: [(2 * j + k // 2, ((k % 2) * half_w, (k % 2 + 1) * half_w), (j, slice(None)))
                                for j in range(3)],
        'l2_b_grp': lambda k: [(j // 2, ((j % 2) * half_w + k * 128, (j % 2) * half_w + (k + 1) * 128),
                                (j, slice(None))) for j in range(len(POOL_WINDOWS))],
    }
    plain = [n for n in SMALL if n not in wide]
    res = update_small(small_all, [starts[n] for n in plain], [(w[n], mom_m[n], mom_v[n]) for n in plain],
                       [shard_cells.get(n) for n in plain], chip, name="adamw_small")
    for n, (g, d, nm, nv) in zip(plain, res, strict=True):
        out_g[n], out_d[n], out_m[n], out_v[n] = g, d, nm, nv
    as_gip = lambda n, a: gip(a) if n.endswith(('b_re', 'b_im')) else a
    g_gip = {n: small_all[starts[n]:starts[n] + S5_GROUPS].reshape(S5_GROUPS, S5_GROUP, S5_STATE) for n in wide}
    res = adamw_many([(as_gip(n, w[n]), g_gip[n], as_gip(n, mom_m[n]), as_gip(n, mom_v[n])) for n in wide],
                     name="adamw_s5_tensors")
    for n, (d, nm, nv) in zip(wide, res, strict=True):
        out_g[n], out_d[n], out_m[n], out_v[n] = as_gip(n, g_gip[n]), as_gip(n, d), as_gip(n, nm), as_gip(n, nv)

    return (loss, grad_x, *[out_g[n] for n in WEIGHTS], *[out_d[n] for n in WEIGHTS],
            *[out_m[n] for n in WEIGHTS], *[out_v[n] for n in WEIGHTS])
```

```python
import functools

import jax
import jax.numpy as jnp
from jax import lax
from jax.experimental import pallas as pl
from jax.experimental.pallas import tpu as pltpu

F32, BF16 = jnp.float32, jnp.bfloat16
MESH = pl.DeviceIdType.MESH

D_MODEL = 1024
N_META = 16
EPS = 1e-6
S5_GROUPS, S5_GROUP, S5_STATE = 64, 16, 64
LANE_BLOCK = 128
S5_BLOCKS = D_MODEL // LANE_BLOCK
GROUPS_PER_BLOCK = LANE_BLOCK // S5_GROUP
STATE_BLOCK = GROUPS_PER_BLOCK * S5_STATE
SCAN_ROWS = 256
SUBLANES = 8
CONV_E = 2048
POOL_E = 2048
POOL_WINDOWS = (2, 4, 8, 16)
POOL_GROUP = 512
POOL_HALO = 16
N_CHIPS = 4

ADAM_LR, ADAM_B1, ADAM_B2, ADAM_EPS, ADAM_WD, ADAM_STEP = 0.001, 0.9, 0.999, 1e-08, 0.01, 10

VMEM_LIMIT = 48 * 1024 * 1024
TN_OPERAND_BYTES = 28 * 1024 * 1024

WEIGHTS = ['meta_tokens', 'norm0_g', 'l0_w_in', 'l0_lam_re', 'l0_lam_im', 'l0_log_dt', 'l0_b_re', 'l0_b_im',
           'l0_c_re', 'l0_c_im', 'l0_d_skip', 'l0_w_glu', 'l0_b_glu', 'l0_w_out', 'norm1_g', 'l1_w_in',
           'l1_conv_w', 'l1_conv_b', 'l1_w_out', 'norm2_g', 'l2_w_in', 'l2_w_grp', 'l2_b_grp', 'l2_scale',
           'l2_w_out', 'norm3_g', 'l3_w_in', 'l3_lam_re', 'l3_lam_im', 'l3_log_dt', 'l3_b_re', 'l3_b_im',
           'l3_c_re', 'l3_c_im', 'l3_d_skip', 'l3_w_glu', 'l3_b_glu', 'l3_w_out', 'final_g']
BIG = ['l0_w_in', 'l0_w_glu', 'l0_w_out', 'l1_w_in', 'l1_w_out', 'l2_w_in', 'l2_w_grp', 'l2_w_out',
       'l3_w_in', 'l3_w_glu', 'l3_w_out']
SMALL_SHARDED = ['meta_tokens', 'l1_conv_w', 'l2_b_grp']
SMALL = [n for n in WEIGHTS if n not in BIG]


def _params(*sem):
    return pltpu.CompilerParams(dimension_semantics=sem, vmem_limit_bytes=VMEM_LIMIT)


class Job:
    def __init__(self, arrays, out_shapes, sems, fn, in_place=False):
        self.arrays, self.out_shapes, self.sems, self.fn = list(arrays), list(out_shapes), list(sems), fn
        self.in_place = in_place
        assert len(self.arrays) == len(self.out_shapes)

    @staticmethod
    def both(first, second):
        na, ns = len(first.arrays), len(first.sems)

        def fn(ins, outs, sems, phase):
            first.fn(ins[:na], outs[:na], sems[:ns], phase)
            second.fn(ins[na:], outs[na:], sems[ns:], phase)

        job = Job(first.arrays + second.arrays, first.out_shapes + second.out_shapes, first.sems + second.sems, fn)
        job.in_place = [first.in_place] * na + [second.in_place] * len(second.arrays)
        return job

    def aliased(self):
        flags = self.in_place if isinstance(self.in_place, list) else [self.in_place] * len(self.arrays)
        return [a for a, flag in enumerate(flags) if flag]


def _call(body, *, name, grid, in_specs, out_specs, out_shape, args, semantics, scratch=(), aliases=None, job=None):
    if job is None:
        return pl.pallas_call(
            body, name=name, grid=grid, in_specs=list(in_specs), out_specs=list(out_specs),
            out_shape=list(out_shape), scratch_shapes=list(scratch), input_output_aliases=aliases or {},
            compiler_params=_params(*semantics))(*args)
    counts = [len(in_specs), len(job.arrays), len(out_specs), len(job.out_shapes), len(scratch), len(job.sems)]

    def hosted(*refs):
        parts, pos = [], 0
        for k in counts:
            parts.append(refs[pos:pos + k])
            pos += k
        ins, job_ins, outs, job_outs, scr, job_sems = parts
        steps = [pl.program_id(d) for d in range(len(grid))]
        first = functools.reduce(jnp.logical_and, [s == 0 for s in steps])
        last = functools.reduce(jnp.logical_and, [s == g - 1 for s, g in zip(steps, grid, strict=True)])

        @pl.when(first)
        def _():
            job.fn(job_ins, job_outs, job_sems, "start")

        body(*ins, *outs, *scr)

        @pl.when(last)
        def _():
            job.fn(job_ins, job_outs, job_sems, "finish")

    any_spec = pl.BlockSpec(memory_space=pl.ANY)
    aliases = dict(aliases or {})
    aliases.update({len(in_specs) + a: len(out_specs) + a for a in job.aliased()})
    res = pl.pallas_call(
        hosted, name=name, grid=grid, in_specs=list(in_specs) + [any_spec] * len(job.arrays),
        out_specs=list(out_specs) + [any_spec] * len(job.out_shapes),
        out_shape=list(out_shape) + job.out_shapes, scratch_shapes=list(scratch) + job.sems,
        input_output_aliases=aliases,
        compiler_params=_params(*["arbitrary"] * len(grid)))(*args, *job.arrays)
    return res[:len(out_specs)], res[len(out_specs):]


def _tile(n, cap, mult):
    best = None
    for t in range(mult, min(n, cap) + 1, mult):
        if n % t == 0:
            best = t
    assert best is not None, (n, cap, mult)
    return best


def _with_job(res, job):
    return res[0] if job is None else (res[0][0], res[1])


def mm_nn(a, b, *, name, bias=None, add=None, out_dtype=F32, job=None):
    m, k = a.shape
    s, _, ns = b.shape
    n = s * ns
    tm, tn = _tile(m, 1088, 16), _tile(ns, 512, 128)
    per = ns // tn

    def body(*refs):
        a_ref, b_ref = refs[0], refs[1]
        o_ref = refs[-1]
        acc = jnp.dot(a_ref[...].astype(BF16), b_ref[...], preferred_element_type=F32)
        pos = 2
        if bias is not None:
            acc = acc + refs[pos][...]
            pos += 1
        if add is not None:
            acc = acc + refs[pos][...]
        o_ref[...] = acc.astype(o_ref.dtype)

    in_specs = [pl.BlockSpec((tm, k), lambda i, j: (i, 0)),
                pl.BlockSpec((None, k, tn), lambda i, j: (j // per, 0, j % per))]
    args = [a, b]
    if bias is not None:
        in_specs.append(pl.BlockSpec((1, tn), lambda i, j: (0, j)))
        args.append(bias)
    if add is not None:
        in_specs.append(pl.BlockSpec((tm, tn), lambda i, j: (i, j)))
        args.append(add)
    return _with_job(_call(
        body, name=name, grid=(m // tm, n // tn), in_specs=in_specs,
        out_specs=[pl.BlockSpec((tm, tn), lambda i, j: (i, j))],
        out_shape=[jax.ShapeDtypeStruct((m, n), out_dtype)], args=args,
        semantics=("parallel", "parallel"), job=job), job)


def mm_nt(g, w, *, name, job=None):
    m, kc = g.shape
    s, nout, kcs = w.shape
    assert s * kcs == kc
    tm, tno, tk = _tile(m, 1088, 16), _tile(nout, 1024, 128), _tile(kcs, 1024, 128)
    per = kcs // tk

    def body(g_ref, w_ref, o_ref):
        kk = pl.program_id(2)
        part = lax.dot_general(g_ref[...].astype(BF16), w_ref[...], (((1,), (1,)), ((), ())),
                               preferred_element_type=F32)

        @pl.when(kk == 0)
        def _():
            o_ref[...] = part

        @pl.when(kk > 0)
        def _():
            o_ref[...] += part

    return _with_job(_call(
        body, name=name, grid=(m // tm, nout // tno, kc // tk),
        in_specs=[pl.BlockSpec((tm, tk), lambda i, j, kk: (i, kk)),
                  pl.BlockSpec((None, tno, tk), lambda i, j, kk: (kk // per, j, kk % per))],
        out_specs=[pl.BlockSpec((tm, tno), lambda i, j, kk: (i, j))],
        out_shape=[jax.ShapeDtypeStruct((m, nout), F32)], args=(g, w),
        semantics=("parallel", "parallel", "arbitrary"), job=job), job)


def mm_tn(a, g, shards, *, name, job=None):
    m, ka = a.shape
    n = g.shape[1]
    ns = n // shards
    tka, tn = _tile(ka, 1024, 128), _tile(ns, 512, 128)
    row_bytes = tka * jnp.dtype(a.dtype).itemsize + tn * jnp.dtype(g.dtype).itemsize
    tm = _tile(m, TN_OPERAND_BYTES // (2 * row_bytes), 16)
    per = ns // tn

    def body(a_ref, g_ref, o_ref):
        mm = pl.program_id(2)
        part = lax.dot_general(a_ref[...].astype(BF16), g_ref[...].astype(BF16), (((0,), (0,)), ((), ())),
                               preferred_element_type=F32)

        @pl.when(mm == 0)
        def _():
            o_ref[...] = part

        @pl.when(mm > 0)
        def _():
            o_ref[...] += part

    return _with_job(_call(
        body, name=name, grid=(ka // tka, n // tn, m // tm),
        in_specs=[pl.BlockSpec((tm, tka), lambda i, j, mm: (mm, i)),
                  pl.BlockSpec((tm, tn), lambda i, j, mm: (mm, j))],
        out_specs=[pl.BlockSpec((None, tka, tn), lambda i, j, mm: (j // per, i, j % per))],
        out_shape=[jax.ShapeDtypeStruct((shards, ka, ns), F32)], args=(a, g),
        semantics=("parallel", "parallel", "arbitrary"), job=job), job)


def grp_nn(a, w, bias, *, name):
    m = a.shape[0]
    tm = _tile(m, 1088, 16)
    ng = len(POOL_WINDOWS)

    def body(a_ref, w_ref, b_ref, o_ref):
        wj = w_ref[...].reshape(POOL_GROUP, POOL_GROUP)
        o_ref[...] = jnp.dot(a_ref[...].astype(BF16), wj, preferred_element_type=F32) + b_ref[...]

    return pl.pallas_call(
        body, name=name, grid=(m // tm, ng),
        in_specs=[pl.BlockSpec((tm, POOL_GROUP), lambda i, j: (i, j)),
                  pl.BlockSpec((N_CHIPS, None, POOL_GROUP // N_CHIPS, POOL_GROUP), lambda i, j: (0, j, 0, 0)),
                  pl.BlockSpec((1, POOL_GROUP), lambda i, j: (0, j))],
        out_specs=pl.BlockSpec((tm, POOL_GROUP), lambda i, j: (i, j)),
        out_shape=jax.ShapeDtypeStruct((m, ng * POOL_GROUP), F32),
        compiler_params=_params("parallel", "parallel"))(a, w, bias)


def grp_nt(g, w, *, name):
    m = g.shape[0]
    tm = _tile(m, 1088, 16)
    ng = len(POOL_WINDOWS)

    def body(g_ref, w_ref, o_ref):
        wj = w_ref[...].reshape(POOL_GROUP, POOL_GROUP)
        o_ref[...] = lax.dot_general(g_ref[...].astype(BF16), wj, (((1,), (1,)), ((), ())),
                                     preferred_element_type=F32)

    return pl.pallas_call(
        body, name=name, grid=(m // tm, ng),
        in_specs=[pl.BlockSpec((tm, POOL_GROUP), lambda i, j: (i, j)),
                  pl.BlockSpec((N_CHIPS, None, POOL_GROUP // N_CHIPS, POOL_GROUP), lambda i, j: (0, j, 0, 0))],
        out_specs=pl.BlockSpec((tm, POOL_GROUP), lambda i, j: (i, j)),
        out_shape=jax.ShapeDtypeStruct((m, ng * POOL_GROUP), F32),
        compiler_params=_params("parallel", "parallel"))(g, w)


def grp_tn(a, g, *, name):
    m = a.shape[0]
    tm = _tile(m, 1088, 16)
    ng = len(POOL_WINDOWS)
    rows = POOL_GROUP // N_CHIPS

    def body(a_ref, g_ref, o_ref):
        mm = pl.program_id(1)
        part = lax.dot_general(a_ref[...].astype(BF16), g_ref[...].astype(BF16), (((0,), (0,)), ((), ())),
                               preferred_element_type=F32).reshape(N_CHIPS, rows, POOL_GROUP)

        @pl.when(mm == 0)
        def _():
            o_ref[...] = part

        @pl.when(mm > 0)
        def _():
            o_ref[...] += part

    return pl.pallas_call(
        body, name=name, grid=(ng, m // tm),
        in_specs=[pl.BlockSpec((tm, POOL_GROUP), lambda j, mm: (mm, j)),
                  pl.BlockSpec((tm, POOL_GROUP), lambda j, mm: (mm, j))],
        out_specs=pl.BlockSpec((N_CHIPS, None, rows, POOL_GROUP), lambda j, mm: (0, j, 0, 0)),
        out_shape=jax.ShapeDtypeStruct((N_CHIPS, ng, rows, POOL_GROUP), F32),
        compiler_params=_params("parallel", "arbitrary"))(a, g)


def rowwise(fn, rows, bcast, outs, accs=(), *, name, aliases=None):
    m = rows[0][0].shape[0]
    tr = m // 16
    n_in = len(rows) + len(bcast)

    def body(*refs):
        vals = [r[...] for r in refs[:n_in]]
        o_vals, a_vals = fn(*vals)
        for ref, val in zip(refs[n_in:n_in + len(outs)], o_vals, strict=True):
            ref[...] = val.astype(ref.dtype)
        step = pl.program_id(0)
        for ref, val in zip(refs[n_in + len(outs):], a_vals, strict=True):
            @pl.when(step == 0)
            def _(ref=ref, val=val):
                ref[...] = val

            @pl.when(step > 0)
            def _(ref=ref, val=val):
                ref[...] += val

    in_specs = [pl.BlockSpec((tr, w), functools.partial(lambda i, cb: (i, cb), cb=cb)) for _, w, cb in rows]
    in_specs += [pl.BlockSpec(b.shape, lambda i: (0, 0)) for b in bcast]
    out_specs = [pl.BlockSpec((tr, w), functools.partial(lambda i, cb: (i, cb), cb=cb)) for _, _, w, cb in outs]
    out_specs += [pl.BlockSpec((1, w), lambda i: (0, 0)) for w in accs]
    out_shape = [jax.ShapeDtypeStruct((m, cols), dt) for cols, dt, _, _ in outs]
    out_shape += [jax.ShapeDtypeStruct((1, w), F32) for w in accs]
    res = pl.pallas_call(
        body, name=name, grid=(m // tr,), in_specs=in_specs, out_specs=out_specs, out_shape=out_shape,
        input_output_aliases=aliases or {},
        compiler_params=_params("arbitrary"))(*[r[0] for r in rows], *bcast)
    return res[:len(outs)], res[len(outs):]


def _rms(h, g):
    return h * lax.rsqrt(jnp.mean(h * h, axis=-1, keepdims=True) + EPS) * g


def _colsum(v):
    return jnp.sum(v, axis=0, keepdims=True)


def rms_fwd(h, g, *, name):
    (n,), _ = rowwise(lambda hv, gv: ([_rms(hv, gv)], []), [(h, D_MODEL, 0)], [g],
                      [(D_MODEL, BF16, D_MODEL, 0)], name=name)
    return n


def rms_bwd(h, g, dn, dh_out, *, name):
    def fn(hv, dnv, dhv, gv):
        _, vjp = jax.vjp(_rms, hv, gv)
        dh, dg = vjp(dnv)
        return [dhv + dh], [dg]

    (dh,), (dg,) = rowwise(fn, [(h, D_MODEL, 0), (dn, D_MODEL, 0), (dh_out, D_MODEL, 0)], [g],
                           [(D_MODEL, F32, D_MODEL, 0)], [D_MODEL], name=name)
    return dh, dg


def _s5_gate(y0, q, z):
    yg = jax.nn.gelu(y0)
    return yg * jax.nn.sigmoid(q) * jax.nn.silu(z)


def _pool_gate(o, z, scale):
    return o * scale * jax.nn.silu(z)


def _shift_rows(v, d, down):
    t = v.shape[0]
    row = lax.broadcasted_iota(jnp.int32, v.shape, 0)
    if down:
        return jnp.where(row >= d, pltpu.roll(v, d, 0), 0.0)
    return jnp.where(row < t - d, pltpu.roll(v, t - d, 0), 0.0)


def _cmul(ar, ai, br, bi):
    return ar * br - ai * bi, ar * bi + ai * br


def _scan(xr, xi, ar, ai, cr, ci, down):
    t, n = xr.shape
    nb = t // SUBLANES
    pw = [(ar, ai)]
    for _ in range(SUBLANES - 1):
        pw.append(_cmul(*pw[-1], ar, ai))
    sub = lax.broadcasted_iota(jnp.int32, (SUBLANES, n), 0)

    def table(exps):
        tr, ti = jnp.zeros((SUBLANES, n), F32), jnp.zeros((SUBLANES, n), F32)
        for r, e in enumerate(exps):
            if e:
                tr, ti = jnp.where(sub == r, pw[e - 1][0], tr), jnp.where(sub == r, pw[e - 1][1], ti)
        return tr[None], ti[None]

    sr, si = xr.reshape(nb, SUBLANES, n), xi.reshape(nb, SUBLANES, n)
    d = 1
    while d < SUBLANES:
        mr, mi = table([d if (r >= d if down else r < SUBLANES - d) else 0 for r in range(SUBLANES)])
        turn = d if down else SUBLANES - d
        hr, hi = pltpu.roll(sr, turn, 1), pltpu.roll(si, turn, 1)
        sr, si = sr + mr * hr - mi * hi, si + mr * hi + mi * hr
        d *= 2
    edge = SUBLANES - 1 if down else 0
    qr, qi = table([r + 1 if down else SUBLANES - r for r in range(SUBLANES)])
    qr, qi = qr[0], qi[0]
    in_r, in_i = jnp.broadcast_to(cr, (SUBLANES, n)), jnp.broadcast_to(ci, (SUBLANES, n))
    out_r, out_i = [None] * nb, [None] * nb
    for b in (range(nb) if down else reversed(range(nb))):
        out_r[b] = sr[b] + qr * in_r - qi * in_i
        out_i[b] = si[b] + qr * in_i + qi * in_r
        in_r = jnp.broadcast_to(out_r[b][edge:edge + 1, :], (SUBLANES, n))
        in_i = jnp.broadcast_to(out_i[b][edge:edge + 1, :], (SUBLANES, n))
    return jnp.concatenate(out_r, axis=0), jnp.concatenate(out_i, axis=0), in_r[0:1, :], in_i[0:1, :]


def s5_fwd(uz, bblk, cblk, ar, ai, dskip, *, name, job=None):
    lp = uz.shape[0]
    t = SCAN_ROWS
    nst = S5_GROUPS * S5_STATE

    def body(u_ref, b_ref, c_ref, ar_ref, ai_ref, d_ref, y_ref, yg_ref, sr_ref, si_ref, cr, ci):
        step = pl.program_id(1)

        @pl.when(step == 0)
        def _():
            cr[...] = jnp.zeros_like(cr)
            ci[...] = jnp.zeros_like(ci)

        u = u_ref[...]
        a_r, a_i = ar_ref[...], ai_ref[...]
        x = jnp.dot(u.astype(BF16), b_ref[...].astype(BF16), preferred_element_type=F32)
        sr, si, cr[...], ci[...] = _scan(x[:, :STATE_BLOCK], x[:, STATE_BLOCK:], a_r, a_i, cr[...], ci[...], True)
        sr_ref[...] = sr
        si_ref[...] = si
        s = jnp.concatenate([sr, si], axis=1).astype(BF16)
        y0 = lax.dot_general(s, c_ref[...].astype(BF16), (((1,), (1,)), ((), ())),
                             preferred_element_type=F32) + d_ref[...] * u
        y_ref[...] = y0
        yg_ref[...] = jax.nn.gelu(y0).astype(yg_ref.dtype)

    return _call(
        body, name=name, grid=(S5_BLOCKS, lp // t),
        in_specs=[pl.BlockSpec((t, LANE_BLOCK), lambda b, c: (c, b)),
                  pl.BlockSpec((None, LANE_BLOCK, 2 * STATE_BLOCK), lambda b, c: (b, 0, 0)),
                  pl.BlockSpec((None, LANE_BLOCK, 2 * STATE_BLOCK), lambda b, c: (b, 0, 0)),
                  pl.BlockSpec((1, STATE_BLOCK), lambda b, c: (0, b)),
                  pl.BlockSpec((1, STATE_BLOCK), lambda b, c: (0, b)),
                  pl.BlockSpec((1, LANE_BLOCK), lambda b, c: (0, b))],
        out_specs=[pl.BlockSpec((t, LANE_BLOCK), lambda b, c: (c, b)),
                   pl.BlockSpec((t, LANE_BLOCK), lambda b, c: (c, b)),
                   pl.BlockSpec((t, STATE_BLOCK), lambda b, c: (c, b)),
                   pl.BlockSpec((t, STATE_BLOCK), lambda b, c: (c, b))],
        out_shape=[jax.ShapeDtypeStruct((lp, D_MODEL), F32), jax.ShapeDtypeStruct((lp, D_MODEL), BF16),
                   jax.ShapeDtypeStruct((lp, nst), F32), jax.ShapeDtypeStruct((lp, nst), F32)],
        scratch=[pltpu.VMEM((1, STATE_BLOCK), F32), pltpu.VMEM((1, STATE_BLOCK), F32)],
        args=(uz, bblk, cblk, ar, ai, dskip), semantics=("parallel", "arbitrary"), job=job)


def s5_bwd(dy_direct, dyg, y0, uz, sr, si, bblk, cblk, ar, ai, dskip, dp, *, name, job=None):
    lp = uz.shape[0]
    t = SCAN_ROWS
    nch = lp // t
    nst = S5_GROUPS * S5_STATE

    def body(dyd_ref, dyg_ref, y0_ref, u_ref, sr_ref, si_ref, b_ref, c_ref, ar_ref, ai_ref, d_ref, _, du_ref,
             db_ref, dc_ref, dar_ref, dai_ref, dd_ref, cr, ci):
        step = pl.program_id(1)

        @pl.when(step == 0)
        def _():
            cr[...] = jnp.zeros_like(cr)
            ci[...] = jnp.zeros_like(ci)

        dy = dyd_ref[...] + jax.vjp(jax.nn.gelu, y0_ref[...])[1](dyg_ref[...])[0]
        u = u_ref[...]
        dyb, ub = dy.astype(BF16), u.astype(BF16)
        a_r, a_i = ar_ref[...], -ai_ref[...]
        g = jnp.dot(dyb, c_ref[...].astype(BF16), preferred_element_type=F32)
        gr, gi = g[:, :STATE_BLOCK], g[:, STATE_BLOCK:]
        nxt_r, nxt_i = cr[...], ci[...]
        last = lax.broadcasted_iota(jnp.int32, gr.shape, 0) == t - 1
        lr, li, cr[...], ci[...] = _scan(gr, gi, a_r, a_i, nxt_r, nxt_i, False)
        nr = _shift_rows(lr, 1, False) + jnp.where(last, nxt_r, 0.0)
        ni = _shift_rows(li, 1, False) + jnp.where(last, nxt_i, 0.0)
        s_r, s_i = sr_ref[...], si_ref[...]
        dar = _colsum(s_r * nr + s_i * ni)
        dai = _colsum(s_r * ni - s_i * nr)
        lam = jnp.concatenate([lr, li], axis=1).astype(BF16)
        sb = jnp.concatenate([s_r, s_i], axis=1).astype(BF16)
        dc = lax.dot_general(dyb, sb, (((0,), (0,)), ((), ())), preferred_element_type=F32)
        db = lax.dot_general(ub, lam, (((0,), (0,)), ((), ())), preferred_element_type=F32)
        du_ref[...] = (lax.dot_general(lam, b_ref[...].astype(BF16), (((1,), (1,)), ((), ())),
                                       preferred_element_type=F32) + d_ref[...] * dy).astype(du_ref.dtype)
        dd = _colsum(dy * u)

        @pl.when(step == 0)
        def _():
            db_ref[...] = db
            dc_ref[...] = dc
            dar_ref[...] = dar
            dai_ref[...] = dai
            dd_ref[...] = dd

        @pl.when(step > 0)
        def _():
            db_ref[...] += db
            dc_ref[...] += dc
            dar_ref[...] += dar
            dai_ref[...] += dai
            dd_ref[...] += dd

    rev = lambda b, c: (nch - 1 - c, b)
    return _call(
        body, name=name, grid=(S5_BLOCKS, nch),
        in_specs=[pl.BlockSpec((t, LANE_BLOCK), rev), pl.BlockSpec((t, LANE_BLOCK), rev),
                  pl.BlockSpec((t, LANE_BLOCK), rev), pl.BlockSpec((t, LANE_BLOCK), rev),
                  pl.BlockSpec((t, STATE_BLOCK), rev), pl.BlockSpec((t, STATE_BLOCK), rev),
                  pl.BlockSpec((None, LANE_BLOCK, 2 * STATE_BLOCK), lambda b, c: (b, 0, 0)),
                  pl.BlockSpec((None, LANE_BLOCK, 2 * STATE_BLOCK), lambda b, c: (b, 0, 0)),
                  pl.BlockSpec((1, STATE_BLOCK), lambda b, c: (0, b)),
                  pl.BlockSpec((1, STATE_BLOCK), lambda b, c: (0, b)),
                  pl.BlockSpec((1, LANE_BLOCK), lambda b, c: (0, b)),
                  pl.BlockSpec(memory_space=pl.ANY)],
        out_specs=[pl.BlockSpec((t, LANE_BLOCK), rev),
                   pl.BlockSpec((None, LANE_BLOCK, 2 * STATE_BLOCK), lambda b, c: (b, 0, 0)),
                   pl.BlockSpec((None, LANE_BLOCK, 2 * STATE_BLOCK), lambda b, c: (b, 0, 0)),
                   pl.BlockSpec((1, STATE_BLOCK), lambda b, c: (0, b)),
                   pl.BlockSpec((1, STATE_BLOCK), lambda b, c: (0, b)),
                   pl.BlockSpec((1, LANE_BLOCK), lambda b, c: (0, b))],
        out_shape=[jax.ShapeDtypeStruct(dp.shape, dp.dtype),
                   jax.ShapeDtypeStruct((S5_BLOCKS, LANE_BLOCK, 2 * STATE_BLOCK), F32),
                   jax.ShapeDtypeStruct((S5_BLOCKS, LANE_BLOCK, 2 * STATE_BLOCK), F32),
                   jax.ShapeDtypeStruct((1, nst), F32), jax.ShapeDtypeStruct((1, nst), F32),
                   jax.ShapeDtypeStruct((1, D_MODEL), F32)],
        scratch=[pltpu.VMEM((1, STATE_BLOCK), F32), pltpu.VMEM((1, STATE_BLOCK), F32)],
        aliases={11: 0}, args=(dy_direct, dyg, y0, uz, sr, si, bblk, cblk, ar, ai, dskip, dp),
        semantics=("parallel", "arbitrary"), job=job)


def _s5_prep(lam_re, lam_im, log_dt, b_re, b_im, c_re, c_im):
    dt = jnp.exp(log_dt)
    mag = jnp.exp(lam_re * dt)
    ar = mag * jnp.cos(lam_im * dt)
    ai = mag * jnp.sin(lam_im * dt)
    den = lam_re * lam_re + lam_im * lam_im
    kr = ((ar - 1.0) * lam_re + ai * lam_im) / den
    ki = (ai * lam_re - (ar - 1.0) * lam_im) / den
    bbr = kr[:, None, :] * b_re - ki[:, None, :] * b_im
    bbi = kr[:, None, :] * b_im + ki[:, None, :] * b_re
    rows = S5_GROUPS * S5_GROUP
    expand = (lax.broadcasted_iota(jnp.int32, (S5_STATE, STATE_BLOCK), 1) % S5_STATE
              == lax.broadcasted_iota(jnp.int32, (S5_STATE, STATE_BLOCK), 0)).astype(F32)
    own = ((lax.broadcasted_iota(jnp.int32, (rows, STATE_BLOCK), 0) // S5_GROUP) % GROUPS_PER_BLOCK
           == lax.broadcasted_iota(jnp.int32, (rows, STATE_BLOCK), 1) // S5_STATE).astype(F32)

    def blocks(v):
        wide = jnp.dot(v.reshape(rows, S5_STATE), expand, precision=lax.Precision.HIGHEST,
                       preferred_element_type=F32)
        return (wide * own).reshape(S5_BLOCKS, LANE_BLOCK, STATE_BLOCK)

    btab = jnp.concatenate([blocks(bbr), blocks(bbi)], axis=2)
    ctab = jnp.concatenate([blocks(c_re), -blocks(c_im)], axis=2)
    return ar, ai, btab, ctab


_S5_PREP_OUT = [(S5_GROUPS, S5_STATE), (S5_GROUPS, S5_STATE), (S5_BLOCKS, LANE_BLOCK, 2 * STATE_BLOCK),
                (S5_BLOCKS, LANE_BLOCK, 2 * STATE_BLOCK)]


def s5_prep(params, *, name):
    def body(*refs):
        for ref, val in zip(refs[7:], _s5_prep(*[r[...] for r in refs[:7]]), strict=True):
            ref[...] = val

    return pl.pallas_call(body, name=name, out_shape=[jax.ShapeDtypeStruct(s, F32) for s in _S5_PREP_OUT],
                          compiler_params=pltpu.CompilerParams(vmem_limit_bytes=VMEM_LIMIT))(*params)


def s5_prep_bwd(params, cotangents, *, name):
    def body(*refs):
        grads = jax.vjp(_s5_prep, *[r[...] for r in refs[:7]])[1](tuple(r[...] for r in refs[7:11]))
        for ref, val in zip(refs[11:], grads, strict=True):
            ref[...] = val

    return pl.pallas_call(body, name=name, out_shape=[jax.ShapeDtypeStruct(p.shape, F32) for p in params],
                          compiler_params=pltpu.CompilerParams(vmem_limit_bytes=VMEM_LIMIT))(*params, *cotangents)


def _silu_grad(z):
    s = jax.nn.sigmoid(z)
    return s * (1.0 + z * (1.0 - s))


def conv_fwd(p, conv_w, conv_b, *, name):
    lp = p.shape[0]
    tr = lp // 16
    e = CONV_E
    hb = tr // 8

    def body(p_ref, cgh_ref, vh_ref, w_ref, b_ref, o_ref):
        i = pl.program_id(0)
        bg, cg, v, z = (p_ref[:, k * e:(k + 1) * e] for k in range(4))
        hc = cg * v
        halo = jnp.where(i > 0, cgh_ref[...] * vh_ref[...], 0.0)
        ext = jnp.concatenate([halo, hc], axis=0)
        w = w_ref[...]
        conv = (w[0:1] * pltpu.roll(ext, 2, 0)[8:] + w[1:2] * pltpu.roll(ext, 1, 0)[8:] + w[2:3] * hc
                + b_ref[...])
        o_ref[...] = (bg * conv * jax.nn.silu(z)).astype(o_ref.dtype)

    prev = lambda col: (lambda i: (jnp.maximum(i * hb - 1, 0), col))
    return pl.pallas_call(
        body, name=name, grid=(lp // tr,),
        in_specs=[pl.BlockSpec((tr, 4 * e), lambda i: (i, 0)),
                  pl.BlockSpec((8, e), prev(1)), pl.BlockSpec((8, e), prev(2)),
                  pl.BlockSpec((3, e), lambda i: (0, 0)), pl.BlockSpec((1, e), lambda i: (0, 0))],
        out_specs=pl.BlockSpec((tr, e), lambda i: (i, 0)),
        out_shape=jax.ShapeDtypeStruct((lp, e), BF16),
        compiler_params=_params("parallel"))(p, p, p, conv_w, conv_b)


def conv_bwd(dy3, p, conv_w, conv_b, *, name):
    lp = p.shape[0]
    tr = lp // 16
    e = CONV_E
    hb = tr // 8
    nt = lp // tr
    width = 512

    def body(dy_ref, p_ref, cgh_ref, vh_ref, dyn_ref, bgn_ref, zn_ref, w_ref, b_ref, dp_ref, dw_ref, db_ref):
        i = pl.program_id(0)
        for c0 in range(0, e, width):
            cols = slice(c0, c0 + width)
            dy = dy_ref[:, cols]
            bg, cg, v, z = (p_ref[:, k * e + c0:k * e + c0 + width] for k in range(4))
            w = w_ref[:, cols]
            hc = cg * v
            halo = jnp.where(i > 0, cgh_ref[:, cols] * vh_ref[:, cols], 0.0)
            ext = jnp.concatenate([halo, hc], axis=0)
            hc2, hc1 = pltpu.roll(ext, 2, 0)[8:], pltpu.roll(ext, 1, 0)[8:]
            yc = w[0:1] * hc2 + w[1:2] * hc1 + w[2:3] * hc + b_ref[:, cols]
            sz = jax.nn.silu(z)
            dyc = dy * bg * sz
            dyc_n = jnp.where(i < nt - 1, dyn_ref[:, cols] * bgn_ref[:, cols] * jax.nn.silu(zn_ref[:, cols]), 0.0)
            ext_n = jnp.concatenate([dyc, dyc_n], axis=0)
            n_rows = tr + 8
            dhc = (w[2:3] * dyc + w[1:2] * pltpu.roll(ext_n, n_rows - 1, 0)[:tr]
                   + w[0:1] * pltpu.roll(ext_n, n_rows - 2, 0)[:tr])
            for k, val in enumerate((dy * yc * sz, dhc * v, dhc * cg, dy * bg * yc * _silu_grad(z))):
                dp_ref[:, k * e + c0:k * e + c0 + width] = val.astype(dp_ref.dtype)
            dw = jnp.concatenate([_colsum(dyc * hc2), _colsum(dyc * hc1), _colsum(dyc * hc)], axis=0)
            db = _colsum(dyc)

            @pl.when(i == 0)
            def _(cols=cols, dw=dw, db=db):
                dw_ref[:, cols] = dw
                db_ref[:, cols] = db

            @pl.when(i > 0)
            def _(cols=cols, dw=dw, db=db):
                dw_ref[:, cols] += dw
                db_ref[:, cols] += db

    prev = lambda col: (lambda i: (jnp.maximum(i * hb - 1, 0), col))
    nxt = lambda col: (lambda i: (jnp.minimum((i + 1) * hb, lp // 8 - 1), col))
    return pl.pallas_call(
        body, name=name, grid=(nt,),
        in_specs=[pl.BlockSpec((tr, e), lambda i: (i, 0)), pl.BlockSpec((tr, 4 * e), lambda i: (i, 0)),
                  pl.BlockSpec((8, e), prev(1)), pl.BlockSpec((8, e), prev(2)),
                  pl.BlockSpec((8, e), nxt(0)), pl.BlockSpec((8, e), nxt(0)), pl.BlockSpec((8, e), nxt(3)),
                  pl.BlockSpec((3, e), lambda i: (0, 0)), pl.BlockSpec((1, e), lambda i: (0, 0))],
        out_specs=[pl.BlockSpec((tr, 4 * e), lambda i: (i, 0)), pl.BlockSpec((3, e), lambda i: (0, 0)),
                   pl.BlockSpec((1, e), lambda i: (0, 0))],
        out_shape=[jax.ShapeDtypeStruct((lp, 4 * e), BF16), jax.ShapeDtypeStruct((3, e), F32),
                   jax.ShapeDtypeStruct((1, e), F32)],
        compiler_params=_params("arbitrary"))(dy3, p, p, p, dy3, p, p, conv_w, conv_b)


def _pool_counts(i, tr, rows, offset, w):
    t = (i * tr + offset + 1 + lax.broadcasted_iota(jnp.int32, (rows, 1), 0)).astype(F32)
    return jnp.minimum(t, float(w))


def pool_fwd(p, *, name):
    lp = p.shape[0]
    tr = lp // 16
    h = POOL_HALO
    hb = tr // h

    def body(u_ref, uh_ref, o_ref):
        i = pl.program_id(0)
        u = u_ref[...]
        ext = jnp.concatenate([jnp.where(i > 0, uh_ref[...], 0.0), u], axis=0)
        for k, w in enumerate(POOL_WINDOWS):
            cols = slice(k * POOL_GROUP, (k + 1) * POOL_GROUP)
            s = ext[:, cols]
            d = 1
            while d < w:
                s = s + pltpu.roll(s, d, 0)
                d *= 2
            o_ref[:, cols] = (s[h:] / _pool_counts(i, tr, tr, 0, w) - u[:, cols]).astype(o_ref.dtype)

    return pl.pallas_call(
        body, name=name, grid=(lp // tr,),
        in_specs=[pl.BlockSpec((tr, POOL_E), lambda i: (i, 0)),
                  pl.BlockSpec((h, POOL_E), lambda i: (jnp.maximum(i * hb - 1, 0), 0))],
        out_specs=pl.BlockSpec((tr, POOL_E), lambda i: (i, 0)),
        out_shape=jax.ShapeDtypeStruct((lp, POOL_E), BF16),
        compiler_params=_params("parallel"))(p, p)


def pool_bwd(dmix, dp, *, name):
    lp = dmix.shape[0]
    tr = lp // 16
    h = POOL_HALO
    hb = tr // h
    nt = lp // tr

    def body(dm_ref, dmn_ref, _, du_ref):
        i = pl.program_id(0)
        dm = dm_ref[...]
        dmn = jnp.where(i < nt - 1, dmn_ref[...], 0.0)
        n_rows = tr + h
        for k, w in enumerate(POOL_WINDOWS):
            cols = slice(k * POOL_GROUP, (k + 1) * POOL_GROUP)
            s = jnp.concatenate([dm[:, cols] / _pool_counts(i, tr, tr, 0, w),
                                 dmn[:, cols] / _pool_counts(i, tr, h, tr, w)], axis=0)
            d = 1
            while d < w:
                s = s + pltpu.roll(s, n_rows - d, 0)
                d *= 2
            du_ref[:, cols] = (s[:tr] - dm[:, cols]).astype(du_ref.dtype)

    return pl.pallas_call(
        body, name=name, grid=(nt,),
        in_specs=[pl.BlockSpec((tr, POOL_E), lambda i: (i, 0)),
                  pl.BlockSpec((h, POOL_E), lambda i: (jnp.minimum((i + 1) * hb, lp // h - 1), 0)),
                  pl.BlockSpec(memory_space=pl.ANY)],
        out_specs=pl.BlockSpec((tr, POOL_E), lambda i: (i, 0)),
        out_shape=jax.ShapeDtypeStruct(dp.shape, dp.dtype),
        input_output_aliases={2: 0},
        compiler_params=_params("parallel"))(dmix, dmix, dp)


def _adamw_math(w, g, m, v):
    nm = ADAM_B1 * m + (1.0 - ADAM_B1) * g
    nv = ADAM_B2 * v + (1.0 - ADAM_B2) * jnp.square(g)
    m_hat = nm / (1.0 - ADAM_B1 ** ADAM_STEP)
    v_hat = nv / (1.0 - ADAM_B2 ** ADAM_STEP)
    return -ADAM_LR * (m_hat / (jnp.sqrt(v_hat) + ADAM_EPS) + ADAM_WD * w), nm, nv


def adamw(w, g, m, v, *, name):
    r, c = w.shape
    tr = _tile(r, max(8, (1 << 19) // c), 8)

    def body(w_ref, g_ref, m_ref, v_ref, d_ref, nm_ref, nv_ref):
        d_ref[...], nm_ref[...], nv_ref[...] = _adamw_math(w_ref[...], g_ref[...], m_ref[...], v_ref[...])

    spec = pl.BlockSpec((tr, c), lambda i: (i, 0))
    return pl.pallas_call(
        body, name=name, grid=(r // tr,), in_specs=[spec] * 4, out_specs=[spec] * 3,
        out_shape=[jax.ShapeDtypeStruct((r, c), F32)] * 3,
        compiler_params=_params("parallel"))(w, g, m, v)


def adamw_many(quads, *, name):
    shape = quads[0][0].shape
    steps = 8
    spec = pl.BlockSpec((shape[0] // steps,) + shape[1:], lambda i: (i, 0, 0))
    n = len(quads)

    def body(*refs):
        for q in range(n):
            w_ref, g_ref, m_ref, v_ref = refs[4 * q:4 * q + 4]
            d_ref, nm_ref, nv_ref = refs[4 * n + 3 * q:4 * n + 3 * q + 3]
            d_ref[...], nm_ref[...], nv_ref[...] = _adamw_math(w_ref[...], g_ref[...], m_ref[...], v_ref[...])

    res = pl.pallas_call(
        body, name=name, grid=(steps,), in_specs=[spec] * (4 * n), out_specs=[spec] * (3 * n),
        out_shape=[jax.ShapeDtypeStruct(shape, F32)] * (3 * n),
        compiler_params=_params("parallel"))(*[a for quad in quads for a in quad])
    return [tuple(res[3 * q:3 * q + 3]) for q in range(n)]


WIRE_WIDTH = 1024
WIRE_ROWS = 1024


def _wire_plan(shapes):
    starts, row = [], 0
    for s in shapes:
        r, c = (1, s[0]) if len(s) == 1 else s
        starts.append(row)
        row += -(-(r * max(1, c // WIRE_WIDTH)) // 8) * 8
    assert row <= WIRE_ROWS, row
    return starts


def _wire_cells(shape):
    if len(shape) == 1:
        n = shape[0]
        if n <= WIRE_WIDTH:
            return [(0, n, (slice(None),))]
        return [(h, WIRE_WIDTH, (slice(h * WIRE_WIDTH, (h + 1) * WIRE_WIDTH),)) for h in range(n // WIRE_WIDTH)]
    r, c = shape
    if c <= WIRE_WIDTH:
        return [(None, c, (slice(None), slice(None)))]
    per = c // WIRE_WIDTH
    return [(j * per + h, WIRE_WIDTH, (j, slice(h * WIRE_WIDTH, (h + 1) * WIRE_WIDTH)))
            for j in range(r) for h in range(per)]


def pack_small(arrays, *, name):
    shapes = [a.shape for a in arrays]
    starts = _wire_plan(shapes)

    def body(*refs):
        out = refs[-1]
        out[...] = jnp.zeros_like(out)
        for ref, shape, row0 in zip(refs[:-1], shapes, starts, strict=True):
            for off, cols, idx in _wire_cells(shape):
                if off is None:
                    out[row0:row0 + shape[0], 0:cols] = ref[idx]
                else:
                    out[row0 + off, 0:cols] = ref[idx]

    return pl.pallas_call(body, name=name, out_shape=jax.ShapeDtypeStruct((WIRE_ROWS, WIRE_WIDTH), F32),
                          compiler_params=pltpu.CompilerParams(vmem_limit_bytes=VMEM_LIMIT))(*arrays)


def update_small(packed, starts, triples, shard_cells, chip, *, name):
    n = len(triples)

    def body(chip_ref, p_ref, *refs):
        k = chip_ref[0]
        ins, outs = refs[:3 * n], refs[3 * n:]

        def cut(row0, rows, cols):
            c0, c1 = cols
            if isinstance(rows, int):
                return p_ref[row0 + rows, c0:c1]
            return p_ref[row0 + rows[0]:row0 + rows[1], c0:c1]

        for q in range(n):
            w_ref, m_ref, v_ref = ins[3 * q:3 * q + 3]
            g_ref, d_ref, nm_ref, nv_ref = outs[4 * q:4 * q + 4]
            shape = w_ref.shape
            if shard_cells[q] is None:
                pieces = [(cut(starts[q], (0, shape[0]) if off is None else off, (0, cols)), idx)
                          for off, cols, idx in _wire_cells(shape)]
            else:
                by_chip = [shard_cells[q](kk) for kk in range(N_CHIPS)]
                pieces = []
                for i, (_, _, idx) in enumerate(by_chip[0]):
                    g = cut(starts[q], *by_chip[0][i][:2])
                    for kk in range(1, N_CHIPS):
                        g = jnp.where(k == kk, cut(starts[q], *by_chip[kk][i][:2]), g)
                    pieces.append((g, idx))
            for g, idx in pieces:
                g_ref[idx] = g
                d_ref[idx], nm_ref[idx], nv_ref[idx] = _adamw_math(w_ref[idx], g, m_ref[idx], v_ref[idx])

    whole = lambda a: pl.BlockSpec(a.shape, lambda i, c_ref, nd=a.ndim: (0,) * nd)
    flat = [a for t in triples for a in t]
    out_shape = [jax.ShapeDtypeStruct(t[0].shape, F32) for t in triples for _ in range(4)]
    res = pl.pallas_call(
        body, name=name,
        grid_spec=pltpu.PrefetchScalarGridSpec(
            num_scalar_prefetch=1, grid=(1,),
            in_specs=[whole(packed)] + [whole(a) for a in flat],
            out_specs=[pl.BlockSpec(s.shape, lambda i, c_ref, nd=len(s.shape): (0,) * nd) for s in out_shape]),
        out_shape=out_shape,
        compiler_params=_params("arbitrary"))(chip.reshape(1).astype(jnp.int32), packed, *flat)
    return [tuple(res[4 * q:4 * q + 4]) for q in range(n)]


ANY = pl.BlockSpec(memory_space=pl.ANY)


def _place():
    x, y, c = lax.axis_index("x"), lax.axis_index("y"), lax.axis_index("c")
    return x, y, c, [(1 - x, y), (x, 1 - y), (1 - x, 1 - y)]


DMA_PIECE_BYTES = 512 * 1024


def _pieces(src, dst):
    shape = src.shape
    rows, cols = shape[-2], shape[-1]
    item = jnp.dtype(src.dtype).itemsize
    unit = 32 // item
    want = max(1, (rows * cols * item) // DMA_PIECE_BYTES)
    n = 1
    if rows % unit == 0:
        n = max(d for d in range(1, rows // unit + 1) if (rows // unit) % d == 0 and d <= want)
    size = rows // n
    out = []
    for lead in (range(shape[0]) if len(shape) == 3 else [None]):
        for i in range(n):
            sel = (pl.ds(i * size, size), slice(None)) if lead is None else (lead, pl.ds(i * size, size), slice(None))
            out.append((src.at[sel], dst.at[sel]))
    return out


def _send(src, dst, send_sem, recv_sem, peer, start=True):
    if start:
        for s, d in _pieces(src, dst):
            pltpu.make_async_remote_copy(src_ref=s, dst_ref=d, send_sem=send_sem, recv_sem=recv_sem,
                                         device_id=peer, device_id_type=MESH).start()
    return pltpu.make_async_remote_copy(src_ref=src, dst_ref=dst, send_sem=send_sem, recv_sem=recv_sem,
                                        device_id=peer, device_id_type=MESH)


def run_job(job, *, name):
    n_in, n_out = len(job.arrays), len(job.out_shapes)

    def body(*refs):
        job.fn(refs[:n_in], refs[n_in:n_in + n_out], refs[n_in + n_out:], "both")

    return pl.pallas_call(body, name=name, in_specs=[ANY] * n_in, out_specs=[ANY] * n_out,
                          out_shape=job.out_shapes, scratch_shapes=job.sems)(*job.arrays)


def gather_chips(shards, *, name):
    n = len(shards)

    def body(*refs):
        ins, outs = refs[:n], refs[n:2 * n]
        send_sems, recv_sems = refs[2 * n:]
        x, y, c, chips = _place()
        k = 2 * x + y
        copies = []
        for a in range(n):
            for j, peer in enumerate([(px, py, c) for px, py in chips] + [(x, y, 1 - c)]):
                copies.append(_send(ins[a], outs[a].at[k], send_sems.at[a, j], recv_sems.at[a, j], peer))
        for cp in copies:
            cp.wait()

    return pl.pallas_call(
        body, name=name, in_specs=[ANY] * n, out_specs=[ANY] * n,
        out_shape=[jax.ShapeDtypeStruct((N_CHIPS,) + s.shape, s.dtype) for s in shards],
        scratch_shapes=[pltpu.SemaphoreType.DMA((n, 4)), pltpu.SemaphoreType.DMA((n, 4))])(*shards)


def gather_by_halves_job(shards):
    n = len(shards)

    def fn(ins, outs, sems, phase):
        send_sems, recv_sems = sems
        x, y, c, chips = _place()
        k = 2 * x + y
        sibling = (x, y, 1 - c)
        begin = phase != "finish"
        waits, arrivals = [], []
        for a in range(n):
            half = ins[a].shape[0] // 2
            waits.append(_send(ins[a], outs[a].at[k], send_sems.at[a, 6], recv_sems.at[a, 6], sibling, begin))
            mine = pl.ds(c * half, half)
            for j, (px, py) in enumerate(chips):
                arrivals.append(_send(ins[a].at[mine, :], outs[a].at[k, mine, :], send_sems.at[a, j],
                                      recv_sems.at[a, j], (px, py, c), begin))
        if phase == "start":
            return
        i = 0
        for a in range(n):
            half = ins[a].shape[0] // 2
            mine = pl.ds(c * half, half)
            for j, (px, py) in enumerate(chips):
                arrivals[i].wait_recv()
                landed = outs[a].at[2 * px + py, mine, :]
                waits.append(_send(landed, landed, send_sems.at[a, 3 + j], recv_sems.at[a, 3 + j], sibling))
                i += 1
        for cp in arrivals:
            cp.wait_send()
        for cp in waits:
            cp.wait()

    return Job(shards, [jax.ShapeDtypeStruct((N_CHIPS,) + s.shape, s.dtype) for s in shards],
               [pltpu.SemaphoreType.DMA((n, 7)), pltpu.SemaphoreType.DMA((n, 7))], fn)


def gather_halves_job(shards):
    n = len(shards)

    def fn(ins, outs, sems, phase):
        send_sems, recv_sems = sems
        x, y, c, chips = _place()
        k = 2 * x + y
        copies = []
        for a in range(n):
            half = ins[a].shape[0] // 2
            mine = pl.ds(c * half, half)
            copies.append(_send(ins[a], outs[a].at[k], send_sems.at[a, 3], recv_sems.at[a, 3], (x, y, 1 - c),
                                phase != "finish"))
            for j, (px, py) in enumerate(chips):
                copies.append(_send(ins[a].at[mine, :], outs[a].at[k, mine, :], send_sems.at[a, j],
                                    recv_sems.at[a, j], (px, py, c), phase != "finish"))
        if phase != "start":
            for cp in copies:
                cp.wait()

    return Job(shards, [jax.ShapeDtypeStruct((N_CHIPS,) + s.shape, s.dtype) for s in shards],
               [pltpu.SemaphoreType.DMA((n, 4)), pltpu.SemaphoreType.DMA((n, 4))], fn)


def forward_halves_job(gathered):
    n = len(gathered)

    def fn(ins, outs, sems, phase):
        send_sems, recv_sems = sems
        x, y, c, chips = _place()
        copies = []
        for a in range(n):
            half = outs[a].shape[1] // 2
            for j, (px, py) in enumerate(chips):
                landed = outs[a].at[2 * px + py, pl.ds(c * half, half), :]
                copies.append(_send(landed, landed, send_sems.at[a, j], recv_sems.at[a, j], (x, y, 1 - c),
                                    phase != "finish"))
        if phase != "start":
            for cp in copies:
                cp.wait()

    return Job(gathered, [jax.ShapeDtypeStruct(g.shape, g.dtype) for g in gathered],
               [pltpu.SemaphoreType.DMA((n, 3)), pltpu.SemaphoreType.DMA((n, 3))], fn, in_place=True)


def swap_job(grads):
    n = len(grads)

    def fn(ins, outs, sems, phase):
        send_sems, recv_sems = sems
        x, y, c, _ = _place()
        copies = []
        for a in range(n):
            half = ins[a].shape[1] // 2
            copies.append(_send(ins[a].at[:, pl.ds((1 - c) * half, half), :], outs[a], send_sems.at[a],
                                recv_sems.at[a], (x, y, 1 - c), phase != "finish"))
        if phase != "start":
            for cp in copies:
                cp.wait()

    return Job(grads, [jax.ShapeDtypeStruct((g.shape[0], g.shape[1] // 2, g.shape[2]), g.dtype) for g in grads],
               [pltpu.SemaphoreType.DMA((n,)), pltpu.SemaphoreType.DMA((n,))], fn)


def _chip_slot(s, k):
    rel = s ^ k
    return (rel >> 1) | ((rel & 1) << 1)


def sum_cores(g, theirs, ck, out_dtype, *, name):
    n, r, cols = g.shape
    half = r // 2
    tr = _tile(half, max(16, (1 << 19) // cols), 16)
    nb = half // tr

    def body(ck_ref, g_ref, t_ref, o_ref):
        o_ref[...] = (g_ref[...] + t_ref[...]).astype(o_ref.dtype)

    return pl.pallas_call(
        body, name=name,
        grid_spec=pltpu.PrefetchScalarGridSpec(
            num_scalar_prefetch=1, grid=(n, nb),
            in_specs=[pl.BlockSpec((None, tr, cols), lambda s, i, ck_ref: (s, ck_ref[0] * nb + i, 0)),
                      pl.BlockSpec((None, tr, cols), lambda s, i, ck_ref: (s, i, 0))],
            out_specs=pl.BlockSpec((None, tr, cols), lambda s, i, ck_ref: (_chip_slot(s, ck_ref[1]), i, 0))),
        out_shape=jax.ShapeDtypeStruct((n, half, cols), out_dtype),
        compiler_params=_params("parallel", "parallel"))(ck, g, theirs)


def scatter_job(parts):
    n = len(parts)

    def fn(ins, outs, sems, phase):
        send_sems, recv_sems = sems
        x, y, c, chips = _place()
        copies = []
        for a in range(n):
            for j, (px, py) in enumerate(chips):
                copies.append(_send(ins[a].at[1 + j], outs[a].at[j], send_sems.at[a, j], recv_sems.at[a, j],
                                    (px, py, c), phase != "finish"))
        if phase != "start":
            for cp in copies:
                cp.wait()

    return Job(parts, [jax.ShapeDtypeStruct((3,) + p.shape[1:], p.dtype) for p in parts],
               [pltpu.SemaphoreType.DMA((n, 3)), pltpu.SemaphoreType.DMA((n, 3))], fn)


def sum_chips(own, others, ck, *, name):
    _, r, cols = own.shape
    tr = _tile(r, max(16, (1 << 19) // cols), 16)

    def body(ck_ref, a_ref, b0_ref, b1_ref, b2_ref, o_ref):
        o_ref[...] = (a_ref[...].astype(F32) + b0_ref[...].astype(F32) + b1_ref[...].astype(F32)
                      + b2_ref[...].astype(F32))

    other = lambda j: pl.BlockSpec((None, tr, cols), lambda i, ck_ref: (j, i, 0))
    return pl.pallas_call(
        body, name=name,
        grid_spec=pltpu.PrefetchScalarGridSpec(
            num_scalar_prefetch=1, grid=(r // tr,),
            in_specs=[pl.BlockSpec((None, tr, cols), lambda i, ck_ref: (0, i, 0)), other(0), other(1), other(2)],
            out_specs=pl.BlockSpec((None, tr, cols), lambda i, ck_ref: (ck_ref[0], i, 0))),
        out_shape=jax.ShapeDtypeStruct((2, r, cols), F32),
        compiler_params=_params("parallel"))(ck, own, others, others, others)


def share_halves(joined, *, name):
    n = len(joined)

    def body(*refs):
        outs = refs[n:2 * n]
        send_sems, recv_sems = refs[2 * n:]
        x, y, c, _ = _place()
        copies = [_send(outs[a].at[c], outs[a].at[c], send_sems.at[a], recv_sems.at[a], (x, y, 1 - c))
                  for a in range(n)]
        for cp in copies:
            cp.wait()

    return pl.pallas_call(
        body, name=name, in_specs=[ANY] * n, out_specs=[ANY] * n,
        out_shape=[jax.ShapeDtypeStruct(j.shape, j.dtype) for j in joined],
        input_output_aliases={a: a for a in range(n)},
        scratch_shapes=[pltpu.SemaphoreType.DMA((n,)), pltpu.SemaphoreType.DMA((n,))])(*joined)


def kernel(x, meta_tokens, norm0_g, l0_w_in, l0_lam_re, l0_lam_im, l0_log_dt, l0_b_re, l0_b_im, l0_c_re, l0_c_im, l0_d_skip, l0_w_glu, l0_b_glu, l0_w_out, norm1_g, l1_w_in, l1_conv_w, l1_conv_b, l1_w_out, norm2_g, l2_w_in, l2_w_grp, l2_b_grp, l2_scale, l2_w_out, norm3_g, l3_w_in, l3_lam_re, l3_lam_im, l3_log_dt, l3_b_re, l3_b_im, l3_c_re, l3_c_im, l3_d_skip, l3_w_glu, l3_b_glu, l3_w_out, final_g, loss_target, m_meta_tokens, m_norm0_g, m_l0_w_in, m_l0_lam_re, m_l0_lam_im, m_l0_log_dt, m_l0_b_re, m_l0_b_im, m_l0_c_re, m_l0_c_im, m_l0_d_skip, m_l0_w_glu, m_l0_b_glu, m_l0_w_out, m_norm1_g, m_l1_w_in, m_l1_conv_w, m_l1_conv_b, m_l1_w_out, m_norm2_g, m_l2_w_in, m_l2_w_grp, m_l2_b_grp, m_l2_scale, m_l2_w_out, m_norm3_g, m_l3_w_in, m_l3_lam_re, m_l3_lam_im, m_l3_log_dt, m_l3_b_re, m_l3_b_im, m_l3_c_re, m_l3_c_im, m_l3_d_skip, m_l3_w_glu, m_l3_b_glu, m_l3_w_out, m_final_g, v_meta_tokens, v_norm0_g, v_l0_w_in, v_l0_lam_re, v_l0_lam_im, v_l0_log_dt, v_l0_b_re, v_l0_b_im, v_l0_c_re, v_l0_c_im, v_l0_d_skip, v_l0_w_glu, v_l0_b_glu, v_l0_w_out, v_norm1_g, v_l1_w_in, v_l1_conv_w, v_l1_conv_b, v_l1_w_out, v_norm2_g, v_l2_w_in, v_l2_w_grp, v_l2_b_grp, v_l2_scale, v_l2_w_out, v_norm3_g, v_l3_w_in, v_l3_lam_re, v_l3_lam_im, v_l3_log_dt, v_l3_b_re, v_l3_b_im, v_l3_c_re, v_l3_c_im, v_l3_d_skip, v_l3_w_glu, v_l3_b_glu, v_l3_w_out, v_final_g):
    given = dict(locals())
    w = {n: given[n] for n in WEIGHTS}
    mom_m = {n: given["m_" + n] for n in WEIGHTS}
    mom_v = {n: given["v_" + n] for n in WEIGHTS}
    seq = x.shape[1]
    real = N_META + seq
    lp = -(-real // SCAN_ROWS) * SCAN_ROWS
    chip = 2 * lax.axis_index("x") + lax.axis_index("y")
    row = lambda a: a.reshape(1, -1)

    shard_bf16 = {n: (w[n].reshape(-1, w[n].shape[-1]) if n == 'l2_w_grp' else w[n]).astype(BF16) for n in BIG}
    layer = lambda i: [n for n in BIG if n.startswith("l%d_" % i)]
    over_ici = lambda names: gather_halves_job([shard_bf16[n] for n in names])
    full = {}

    def landed(names, arrays):
        for n, arr in zip(names, arrays, strict=True):
            if n.endswith('w_in'):
                full[n] = arr
            elif n == 'l2_w_grp':
                full[n] = arr.reshape(N_CHIPS, len(POOL_WINDOWS), POOL_GROUP // N_CHIPS, POOL_GROUP)
            else:
                full[n] = arr.reshape(1, -1, arr.shape[-1])

    landed(['l0_w_in'], run_job(gather_by_halves_job([shard_bf16['l0_w_in']]), name="gather_l0_w_in"))
    full.update(zip(SMALL_SHARDED, gather_chips([w[n] for n in SMALL_SHARDED], name="gather_small_shards"),
                    strict=True))
    meta_full = full['meta_tokens'].transpose(1, 0, 2).reshape(N_META, D_MODEL)
    conv_w_full = full['l1_conv_w'].transpose(1, 0, 2).reshape(3, CONV_E)
    b_grp_full = full['l2_b_grp'].transpose(1, 0, 2).reshape(len(POOL_WINDOWS), POOL_GROUP)

    h0 = jnp.concatenate([meta_full, x[0], jnp.zeros((lp - real, D_MODEL), F32)], axis=0)
    target = jnp.concatenate([jnp.zeros((N_META, D_MODEL), F32), loss_target[0],
                              jnp.zeros((lp - real, D_MODEL), F32)], axis=0)
    grads = {}
    saved = {}
    gip = lambda a: jnp.swapaxes(a, 1, 2)

    def s5_layer_fwd(i, h, first=False):
        pre = "l%d_" % i
        prm = [w[pre + 'lam_re'], w[pre + 'lam_im'], w[pre + 'log_dt'].reshape(-1, 1), gip(w[pre + 'b_re']),
               gip(w[pre + 'b_im']), w[pre + 'c_re'], w[pre + 'c_im']]
        a_re, a_im, bblk, cblk = s5_prep(prm, name=pre + "prep")
        tables = (a_re.reshape(1, -1), a_im.reshape(1, -1), bblk, cblk)
        ar, ai = tables[:2]
        n = rms_fwd(h, row(w["norm%d_g" % i]), name=pre + "norm")
        rest = [pre + 'w_glu', pre + 'w_out']
        if not first:
            uz = mm_nn(n, full[pre + 'w_in'], name=pre + "in")
            y0, yg, sr, si = s5_fwd(uz, bblk, cblk, ar, ai, row(w[pre + 'd_skip']), name=pre + "scan")
            q = mm_nn(yg, full[pre + 'w_glu'], bias=row(w[pre + 'b_glu']), name=pre + "glu")
        else:
            uz, halves = mm_nn(n, full[pre + 'w_in'], name=pre + "in", job=over_ici(rest))
            (y0, yg, sr, si), arrived = s5_fwd(
                uz, bblk, cblk, ar, ai, row(w[pre + 'd_skip']), name=pre + "scan",
                job=Job.both(over_ici(layer(1)), forward_halves_job(halves)))
            landed(rest, arrived[len(layer(1)):])
            q, whole = mm_nn(yg, full[pre + 'w_glu'], bias=row(w[pre + 'b_glu']), name=pre + "glu",
                             job=forward_halves_job(arrived[:len(layer(1))]))
            landed(layer(1), whole)
        (y3,), _ = rowwise(lambda yv, qv, zv: ([_s5_gate(yv, qv, zv)], []),
                           [(y0, D_MODEL, 0), (q, D_MODEL, 0), (uz, D_MODEL, 1)], [],
                           [(D_MODEL, BF16, D_MODEL, 0)], name=pre + "gate")
        h_new = mm_nn(y3, full[pre + 'w_out'], add=h, name=pre + "out")
        saved[i] = dict(h=h, n=n, uz=uz, y0=y0, sr=sr, si=si, yg=yg, q=q, y3=y3, tables=tables, prm=prm)
        return h_new

    def s5_layer_bwd(i, dh, carry=None):
        pre = "l%d_" % i
        s = saved[i]
        ar, ai, bblk, cblk = s['tables']
        if carry is None:
            grads[pre + 'w_out'] = mm_tn(s['y3'], dh, 1, name=pre + "d_w_out")[0]
        else:
            names, pieces = core_parts(carry)
            d_w_out, theirs = mm_tn(s['y3'], dh, 1, name=pre + "d_w_out", job=swap_job(pieces))
            grads[pre + 'w_out'] = d_w_out[0]
            sum_over_cores(names, pieces, theirs)
        dy3 = mm_nt(dh, full[pre + 'w_out'], name=pre + "d_y3")

        def gate_bwd(dyv, yv, qv, zv):
            _, vjp = jax.vjp(_s5_gate, yv, qv, zv)
            dy0, dq, dz = vjp(dyv)
            return [dy0, dq, dz], [_colsum(dq)]

        (dy0_direct, dq, dp), (db_glu,) = rowwise(
            gate_bwd, [(dy3, D_MODEL, 0), (s['y0'], D_MODEL, 0), (s['q'], D_MODEL, 0), (s['uz'], D_MODEL, 1)], [],
            [(D_MODEL, F32, D_MODEL, 0), (D_MODEL, BF16, D_MODEL, 0), (2 * D_MODEL, BF16, D_MODEL, 1)], [D_MODEL],
            name=pre + "d_gate")
        grads[pre + 'b_glu'] = db_glu
        grads[pre + 'w_glu'] = mm_tn(s['yg'], dq, 1, name=pre + "d_w_glu")[0]
        dyg = mm_nt(dq, full[pre + 'w_glu'], name=pre + "d_yg")
        res = s5_bwd(dy0_direct, dyg, s['y0'], s['uz'], s['sr'], s['si'], bblk, cblk, ar, ai,
                     row(w[pre + 'd_skip']), dp, name=pre + "d_scan",
                     job=None if carry is None else scatter_job(scatter_of(carry)))
        if carry is not None:
            res, arrived = res
            scattered(carry, arrived)
        dp, dbblk, dcblk, dar, dai, dd = res
        grads[pre + 'd_skip'] = dd
        cots = (dar.reshape(S5_GROUPS, S5_STATE), dai.reshape(S5_GROUPS, S5_STATE), dbblk, dcblk)
        for key, val in zip(('lam_re', 'lam_im', 'log_dt', 'b_re', 'b_im', 'c_re', 'c_im'),
                            s5_prep_bwd(s['prm'], cots, name=pre + "d_prep"), strict=True):
            grads[pre + key] = val.reshape(-1) if key == 'log_dt' else val
        grads[pre + 'w_in'] = mm_tn(s['n'], dp, N_CHIPS, name=pre + "d_w_in")
        dn = mm_nt(dp, full[pre + 'w_in'], name=pre + "d_n")
        dh_in, dg = rms_bwd(s['h'], row(w["norm%d_g" % i]), dn, dh, name=pre + "d_norm")
        grads["norm%d_g" % i] = dg
        return dh_in

    h1 = s5_layer_fwd(0, h0, first=True)

    n1 = rms_fwd(h1, row(w['norm1_g']), name="l1_norm")
    p1, halves = mm_nn(n1, full['l1_w_in'], name="l1_in", job=over_ici(layer(2)))
    y3_1 = conv_fwd(p1, conv_w_full, row(w['l1_conv_b']), name="l1_conv")
    h2, whole = mm_nn(y3_1, full['l1_w_out'], add=h1, name="l1_out", job=forward_halves_job(halves))
    landed(layer(2), whole)

    n2 = rms_fwd(h2, row(w['norm2_g']), name="l2_norm")
    p2, halves = mm_nn(n2, full['l2_w_in'], name="l2_in", job=over_ici(layer(3)))
    mixed = pool_fwd(p2, name="l2_pool")
    o2 = grp_nn(mixed, full['l2_w_grp'], b_grp_full.reshape(1, -1), name="l2_grp")
    (y3_2,), _ = rowwise(lambda ov, zv, sv: ([_pool_gate(ov, zv, sv)], []),
                         [(o2, POOL_E, 0), (p2, POOL_E, 1)], [row(w['l2_scale'])],
                         [(POOL_E, BF16, POOL_E, 0)], name="l2_gate")
    h3, whole = mm_nn(y3_2, full['l2_w_out'], add=h2, name="l2_out", job=forward_halves_job(halves))
    landed(layer(3), whole)

    h4 = s5_layer_fwd(3, h3)

    def loss_fn(hv, tv, gv):
        i = pl.program_id(0)
        tr = hv.shape[0]
        yf, vjp = jax.vjp(_rms, hv, gv)
        pos = i * tr + lax.broadcasted_iota(jnp.int32, (tr, 1), 0)
        diff = jnp.where((pos >= N_META) & (pos < real), yf - tv, 0.0)
        dh, dg = vjp(diff / D_MODEL)
        return [dh], [dg, _colsum(diff * diff)]

    (dh,), (d_final_g, sq) = rowwise(loss_fn, [(h4, D_MODEL, 0), (target, D_MODEL, 0)], [row(w['final_g'])],
                                     [(D_MODEL, F32, D_MODEL, 0)], [D_MODEL, D_MODEL], name="loss")
    grads['final_g'] = d_final_g
    loss = lax.psum(0.5 / D_MODEL * jnp.sum(sq), ("x", "y", "c"))

    ck = jnp.stack([lax.axis_index("c"), chip]).astype(jnp.int32)
    pair, from_chips = {}, {}

    def chip_major(n):
        g = grads[n]
        if n.endswith('w_in'):
            return g
        if n == 'l2_w_grp':
            return g.reshape(N_CHIPS, -1, POOL_GROUP)
        return g.reshape(N_CHIPS, -1, g.shape[-1])

    def core_parts(i, small=None):
        names, pieces = layer(i), [chip_major(n) for n in layer(i)]
        return (names + ['small'], pieces + [small]) if small is not None else (names, pieces)

    def sum_over_cores(names, pieces, theirs):
        for n, g, t in zip(names, pieces, theirs, strict=True):
            pair[n] = sum_cores(g, t, ck, F32 if n == 'small' else BF16, name="sum_cores_" + n)

    scatter_of = lambda i: [pair[n] for n in layer(i)]

    def scattered(i, arrays):
        from_chips.update(zip(layer(i), arrays, strict=True))

    dh = s5_layer_bwd(3, dh)

    names, pieces = core_parts(3)
    d_w_out, theirs = mm_tn(y3_2, dh, 1, name="l2_d_w_out", job=swap_job(pieces))
    grads['l2_w_out'] = d_w_out[0]
    sum_over_cores(names, pieces, theirs)
    dy3 = mm_nt(dh, full['l2_w_out'], name="l2_d_y3")

    def pool_gate_bwd(dyv, ov, zv, sv):
        _, vjp = jax.vjp(_pool_gate, ov, zv, sv)
        do, dz, dscale = vjp(dyv)
        return [do, dz], [dscale, _colsum(do)]

    (do2, dp2), (dscale, dbgrp) = rowwise(
        pool_gate_bwd, [(dy3, POOL_E, 0), (o2, POOL_E, 0), (p2, POOL_E, 1)], [row(w['l2_scale'])],
        [(POOL_E, BF16, POOL_E, 0), (2 * POOL_E, BF16, POOL_E, 1)], [POOL_E, POOL_E], name="l2_d_gate")
    grads['l2_scale'] = dscale
    grads['l2_b_grp'] = dbgrp
    grads['l2_w_grp'] = grp_tn(mixed, do2, name="l2_d_w_grp")
    dmix = grp_nt(do2, full['l2_w_grp'], name="l2_d_mixed")
    dp2 = pool_bwd(dmix, dp2, name="l2_d_pool")
    grads['l2_w_in'], arrived = mm_tn(n2, dp2, N_CHIPS, name="l2_d_w_in", job=scatter_job(scatter_of(3)))
    scattered(3, arrived)
    dn = mm_nt(dp2, full['l2_w_in'], name="l2_d_n")
    dh, dg = rms_bwd(h2, row(w['norm2_g']), dn, dh, name="l2_d_norm")
    grads['norm2_g'] = dg

    names, pieces = core_parts(2)
    d_w_out, theirs = mm_tn(y3_1, dh, 1, name="l1_d_w_out", job=swap_job(pieces))
    grads['l1_w_out'] = d_w_out[0]
    sum_over_cores(names, pieces, theirs)
    dy3 = mm_nt(dh, full['l1_w_out'], name="l1_d_y3")
    dp1, dcw, dcb = conv_bwd(dy3, p1, conv_w_full, row(w['l1_conv_b']), name="l1_d_conv")
    grads['l1_conv_w'] = dcw
    grads['l1_conv_b'] = dcb
    grads['l1_w_in'], arrived = mm_tn(n1, dp1, N_CHIPS, name="l1_d_w_in", job=scatter_job(scatter_of(2)))
    scattered(2, arrived)
    dn = mm_nt(dp1, full['l1_w_in'], name="l1_d_n")
    dh, dg = rms_bwd(h1, row(w['norm1_g']), dn, dh, name="l1_d_norm")
    grads['norm1_g'] = dg

    dh = s5_layer_bwd(0, dh, carry=1)
    grad_x = dh[N_META:real][None]
    grads['meta_tokens'] = dh[:N_META]

    wide = [n for n in SMALL if w[n].ndim == 3]
    wire = [grads[n].reshape(S5_GROUPS, -1) if n in wide else grads[n] for n in SMALL]
    starts = dict(zip(SMALL, _wire_plan([a.shape for a in wire]), strict=True))
    small_local = pack_small(wire, name="pack_small")
    names, pieces = core_parts(0, small_local.reshape(N_CHIPS, -1, WIRE_WIDTH))
    sum_over_cores(names, pieces, run_job(swap_job(pieces), name="reduce_cores_l0"))
    rest = layer(0) + ['small']
    from_chips.update(zip(rest, run_job(scatter_job([pair[n] for n in rest]), name="reduce_chips_l0"), strict=True))
    joined = [sum_chips(pair[n], from_chips[n], ck, name="sum_chips_" + n) for n in BIG + ['small']]
    joined = share_halves(joined, name="share_cores")
    reduced = [j.reshape(-1, j.shape[-1]) for j in joined]
    g_big = {n: reduced[i].reshape(w[n].shape) for i, n in enumerate(BIG)}
    (small_all,) = gather_chips([reduced[-1]], name="gather_small")
    small_all = small_all.reshape(WIRE_ROWS, WIRE_WIDTH)

    out_g, out_d, out_m, out_v = {}, {}, {}, {}
    for n in BIG:
        shape = w[n].shape
        as2d = lambda a: a.reshape(-1, shape[-1])
        d, nm, nv = adamw(as2d(w[n]), as2d(g_big[n]), as2d(mom_m[n]), as2d(mom_v[n]), name="adamw_" + n)
        out_g[n], out_d[n], out_m[n], out_v[n] = g_big[n], d.reshape(shape), nm.reshape(shape), nv.reshape(shape)

    half_w = WIRE_WIDTH // 2
    shard_cells = {
        'meta_tokens': lambda k: [((0, N_META), (k * 256, (k + 1) * 256), (slice(None), slice(None)))],
        'l1_conv_w': lambda k: [(2 * j + k // 2, ((k % 2) * half_w, (k % 2 + 1) * half_w), (j, slice(None)))
                                for j in range(3)],
        'l2_b_grp': lambda k: [(j // 2, ((j % 2) * half_w + k * 128, (j % 2) * half_w + (k + 1) * 128),
                                (j, slice(None))) for j in range(len(POOL_WINDOWS))],
    }
    plain = [n for n in SMALL if n not in wide]
    res = update_small(small_all, [starts[n] for n in plain], [(w[n], mom_m[n], mom_v[n]) for n in plain],
                       [shard_cells.get(n) for n in plain], chip, name="adamw_small")
    for n, (g, d, nm, nv) in zip(plain, res, strict=True):
        out_g[n], out_d[n], out_m[n], out_v[n] = g, d, nm, nv
    as_gip = lambda n, a: gip(a) if n.endswith(('b_re', 'b_im')) else a
    g_gip = {n: small_all[starts[n]:starts[n] + S5_GROUPS].reshape(S5_GROUPS, S5_GROUP, S5_STATE) for n in wide}
    res = adamw_many([(as_gip(n, w[n]), g_gip[n], as_gip(n, mom_m[n]), as_gip(n, mom_v[n])) for n in wide],
                     name="adamw_s5_tensors")
    for n, (d, nm, nv) in zip(wide, res, strict=True):
        out_g[n], out_d[n], out_m[n], out_v[n] = as_gip(n, g_gip[n]), as_gip(n, d), as_gip(n, nm), as_gip(n, nv)

    return (loss, grad_x, *[out_g[n] for n in WEIGHTS], *[out_d[n] for n in WEIGHTS],
            *[out_m[n] for n in WEIGHTS], *[out_v[n] for n in WEIGHTS])
```

```python
import functools

import jax
import jax.numpy as jnp
from jax import lax
from jax.experimental import pallas as pl
from jax.experimental.pallas import tpu as pltpu

F32, BF16 = jnp.float32, jnp.bfloat16
MESH = pl.DeviceIdType.MESH

D_MODEL = 1024
N_META = 16
EPS = 1e-6
S5_GROUPS, S5_GROUP, S5_STATE = 64, 16, 64
LANE_BLOCK = 128
S5_BLOCKS = D_MODEL // LANE_BLOCK
GROUPS_PER_BLOCK = LANE_BLOCK // S5_GROUP
STATE_BLOCK = GROUPS_PER_BLOCK * S5_STATE
SCAN_ROWS = 256
SUBLANES = 8
CONV_E = 2048
POOL_E = 2048
POOL_WINDOWS = (2, 4, 8, 16)
POOL_GROUP = 512
POOL_HALO = 16
N_CHIPS = 4

ADAM_LR, ADAM_B1, ADAM_B2, ADAM_EPS, ADAM_WD, ADAM_STEP = 0.001, 0.9, 0.999, 1e-08, 0.01, 10

VMEM_LIMIT = 48 * 1024 * 1024
TN_OPERAND_BYTES = 28 * 1024 * 1024

WEIGHTS = ['meta_tokens', 'norm0_g', 'l0_w_in', 'l0_lam_re', 'l0_lam_im', 'l0_log_dt', 'l0_b_re', 'l0_b_im',
           'l0_c_re', 'l0_c_im', 'l0_d_skip', 'l0_w_glu', 'l0_b_glu', 'l0_w_out', 'norm1_g', 'l1_w_in',
           'l1_conv_w', 'l1_conv_b', 'l1_w_out', 'norm2_g', 'l2_w_in', 'l2_w_grp', 'l2_b_grp', 'l2_scale',
           'l2_w_out', 'norm3_g', 'l3_w_in', 'l3_lam_re', 'l3_lam_im', 'l3_log_dt', 'l3_b_re', 'l3_b_im',
           'l3_c_re', 'l3_c_im', 'l3_d_skip', 'l3_w_glu', 'l3_b_glu', 'l3_w_out', 'final_g']
BIG = ['l0_w_in', 'l0_w_glu', 'l0_w_out', 'l1_w_in', 'l1_w_out', 'l2_w_in', 'l2_w_grp', 'l2_w_out',
       'l3_w_in', 'l3_w_glu', 'l3_w_out']
SMALL_SHARDED = ['meta_tokens', 'l1_conv_w', 'l2_b_grp']
SMALL = [n for n in WEIGHTS if n not in BIG]


def _params(*sem):
    return pltpu.CompilerParams(dimension_semantics=sem, vmem_limit_bytes=VMEM_LIMIT)


class Job:
    def __init__(self, arrays, out_shapes, sems, fn, in_place=False):
        self.arrays, self.out_shapes, self.sems, self.fn = list(arrays), list(out_shapes), list(sems), fn
        self.in_place = in_place
        assert len(self.arrays) == len(self.out_shapes)

    @staticmethod
    def both(first, second):
        na, ns = len(first.arrays), len(first.sems)

        def fn(ins, outs, sems, phase):
            first.fn(ins[:na], outs[:na], sems[:ns], phase)
            second.fn(ins[na:], outs[na:], sems[ns:], phase)

        job = Job(first.arrays + second.arrays, first.out_shapes + second.out_shapes, first.sems + second.sems, fn)
        job.in_place = [first.in_place] * na + [second.in_place] * len(second.arrays)
        return job

    def aliased(self):
        flags = self.in_place if isinstance(self.in_place, list) else [self.in_place] * len(self.arrays)
        return [a for a, flag in enumerate(flags) if flag]


def _call(body, *, name, grid, in_specs, out_specs, out_shape, args, semantics, scratch=(), aliases=None, job=None):
    if job is None:
        return pl.pallas_call(
            body, name=name, grid=grid, in_specs=list(in_specs), out_specs=list(out_specs),
            out_shape=list(out_shape), scratch_shapes=list(scratch), input_output_aliases=aliases or {},
            compiler_params=_params(*semantics))(*args)
    counts = [len(in_specs), len(job.arrays), len(out_specs), len(job.out_shapes), len(scratch), len(job.sems)]

    def hosted(*refs):
        parts, pos = [], 0
        for k in counts:
            parts.append(refs[pos:pos + k])
            pos += k
        ins, job_ins, outs, job_outs, scr, job_sems = parts
        steps = [pl.program_id(d) for d in range(len(grid))]
        first = functools.reduce(jnp.logical_and, [s == 0 for s in steps])
        last = functools.reduce(jnp.logical_and, [s == g - 1 for s, g in zip(steps, grid, strict=True)])

        @pl.when(first)
        def _():
            job.fn(job_ins, job_outs, job_sems, "start")

        body(*ins, *outs, *scr)

        @pl.when(last)
        def _():
            job.fn(job_ins, job_outs, job_sems, "finish")

    any_spec = pl.BlockSpec(memory_space=pl.ANY)
    aliases = dict(aliases or {})
    aliases.update({len(in_specs) + a: len(out_specs) + a for a in job.aliased()})
    res = pl.pallas_call(
        hosted, name=name, grid=grid, in_specs=list(in_specs) + [any_spec] * len(job.arrays),
        out_specs=list(out_specs) + [any_spec] * len(job.out_shapes),
        out_shape=list(out_shape) + job.out_shapes, scratch_shapes=list(scratch) + job.sems,
        input_output_aliases=aliases,
        compiler_params=_params(*["arbitrary"] * len(grid)))(*args, *job.arrays)
    return res[:len(out_specs)], res[len(out_specs):]


def _tile(n, cap, mult):
    best = None
    for t in range(mult, min(n, cap) + 1, mult):
        if n % t == 0:
            best = t
    assert best is not None, (n, cap, mult)
    return best


def _with_job(res, job):
    return res[0] if job is None else (res[0][0], res[1])


def mm_nn(a, b, *, name, bias=None, add=None, out_dtype=F32, job=None):
    m, k = a.shape
    s, _, ns = b.shape
    n = s * ns
    tm, tn = _tile(m, 1088, 16), _tile(ns, 1024, 128)
    per = ns // tn

    def body(*refs):
        a_ref, b_ref = refs[0], refs[1]
        o_ref = refs[-1]
        acc = jnp.dot(a_ref[...].astype(BF16), b_ref[...], preferred_element_type=F32)
        pos = 2
        if bias is not None:
            acc = acc + refs[pos][...]
            pos += 1
        if add is not None:
            acc = acc + refs[pos][...]
        o_ref[...] = acc.astype(o_ref.dtype)

    in_specs = [pl.BlockSpec((tm, k), lambda i, j: (i, 0)),
                pl.BlockSpec((None, k, tn), lambda i, j: (j // per, 0, j % per))]
    args = [a, b]
    if bias is not None:
        in_specs.append(pl.BlockSpec((1, tn), lambda i, j: (0, j)))
        args.append(bias)
    if add is not None:
        in_specs.append(pl.BlockSpec((tm, tn), lambda i, j: (i, j)))
        args.append(add)
    return _with_job(_call(
        body, name=name, grid=(m // tm, n // tn), in_specs=in_specs,
        out_specs=[pl.BlockSpec((tm, tn), lambda i, j: (i, j))],
        out_shape=[jax.ShapeDtypeStruct((m, n), out_dtype)], args=args,
        semantics=("parallel", "parallel"), job=job), job)


def mm_nt(g, w, *, name, job=None):
    m, kc = g.shape
    s, nout, kcs = w.shape
    assert s * kcs == kc
    tm, tno, tk = _tile(m, 1088, 16), _tile(nout, 1024, 128), _tile(kcs, 1024, 128)
    per = kcs // tk

    def body(g_ref, w_ref, o_ref):
        kk = pl.program_id(2)
        part = lax.dot_general(g_ref[...].astype(BF16), w_ref[...], (((1,), (1,)), ((), ())),
                               preferred_element_type=F32)

        @pl.when(kk == 0)
        def _():
            o_ref[...] = part

        @pl.when(kk > 0)
        def _():
            o_ref[...] += part

    return _with_job(_call(
        body, name=name, grid=(m // tm, nout // tno, kc // tk),
        in_specs=[pl.BlockSpec((tm, tk), lambda i, j, kk: (i, kk)),
                  pl.BlockSpec((None, tno, tk), lambda i, j, kk: (kk // per, j, kk % per))],
        out_specs=[pl.BlockSpec((tm, tno), lambda i, j, kk: (i, j))],
        out_shape=[jax.ShapeDtypeStruct((m, nout), F32)], args=(g, w),
        semantics=("parallel", "parallel", "arbitrary"), job=job), job)


def mm_tn(a, g, shards, *, name, job=None):
    m, ka = a.shape
    n = g.shape[1]
    ns = n // shards
    tka, tn = _tile(ka, 1024, 128), _tile(ns, 512, 128)
    row_bytes = tka * jnp.dtype(a.dtype).itemsize + tn * jnp.dtype(g.dtype).itemsize
    tm = _tile(m, TN_OPERAND_BYTES // (2 * row_bytes), 16)
    per = ns // tn

    def body(a_ref, g_ref, o_ref):
        mm = pl.program_id(2)
        part = lax.dot_general(a_ref[...].astype(BF16), g_ref[...].astype(BF16), (((0,), (0,)), ((), ())),
                               preferred_element_type=F32)

        @pl.when(mm == 0)
        def _():
            o_ref[...] = part

        @pl.when(mm > 0)
        def _():
            o_ref[...] += part

    return _with_job(_call(
        body, name=name, grid=(ka // tka, n // tn, m // tm),
        in_specs=[pl.BlockSpec((tm, tka), lambda i, j, mm: (mm, i)),
                  pl.BlockSpec((tm, tn), lambda i, j, mm: (mm, j))],
        out_specs=[pl.BlockSpec((None, tka, tn), lambda i, j, mm: (j // per, i, j % per))],
        out_shape=[jax.ShapeDtypeStruct((shards, ka, ns), F32)], args=(a, g),
        semantics=("parallel", "parallel", "arbitrary"), job=job), job)


def grp_nn(a, w, bias, *, name):
    m = a.shape[0]
    tm = _tile(m, 1088, 16)
    ng = len(POOL_WINDOWS)

    def body(a_ref, w_ref, b_ref, o_ref):
        wj = w_ref[...].reshape(POOL_GROUP, POOL_GROUP)
        o_ref[...] = jnp.dot(a_ref[...].astype(BF16), wj, preferred_element_type=F32) + b_ref[...]

    return pl.pallas_call(
        body, name=name, grid=(m // tm, ng),
        in_specs=[pl.BlockSpec((tm, POOL_GROUP), lambda i, j: (i, j)),
                  pl.BlockSpec((N_CHIPS, None, POOL_GROUP // N_CHIPS, POOL_GROUP), lambda i, j: (0, j, 0, 0)),
                  pl.BlockSpec((1, POOL_GROUP), lambda i, j: (0, j))],
        out_specs=pl.BlockSpec((tm, POOL_GROUP), lambda i, j: (i, j)),
        out_shape=jax.ShapeDtypeStruct((m, ng * POOL_GROUP), F32),
        compiler_params=_params("parallel", "parallel"))(a, w, bias)


def grp_nt(g, w, *, name):
    m = g.shape[0]
    tm = _tile(m, 1088, 16)
    ng = len(POOL_WINDOWS)

    def body(g_ref, w_ref, o_ref):
        wj = w_ref[...].reshape(POOL_GROUP, POOL_GROUP)
        o_ref[...] = lax.dot_general(g_ref[...].astype(BF16), wj, (((1,), (1,)), ((), ())),
                                     preferred_element_type=F32)

    return pl.pallas_call(
        body, name=name, grid=(m // tm, ng),
        in_specs=[pl.BlockSpec((tm, POOL_GROUP), lambda i, j: (i, j)),
                  pl.BlockSpec((N_CHIPS, None, POOL_GROUP // N_CHIPS, POOL_GROUP), lambda i, j: (0, j, 0, 0))],
        out_specs=pl.BlockSpec((tm, POOL_GROUP), lambda i, j: (i, j)),
        out_shape=jax.ShapeDtypeStruct((m, ng * POOL_GROUP), F32),
        compiler_params=_params("parallel", "parallel"))(g, w)


def grp_tn(a, g, *, name):
    m = a.shape[0]
    tm = _tile(m, 1088, 16)
    ng = len(POOL_WINDOWS)
    rows = POOL_GROUP // N_CHIPS

    def body(a_ref, g_ref, o_ref):
        mm = pl.program_id(1)
        part = lax.dot_general(a_ref[...].astype(BF16), g_ref[...].astype(BF16), (((0,), (0,)), ((), ())),
                               preferred_element_type=F32).reshape(N_CHIPS, rows, POOL_GROUP)

        @pl.when(mm == 0)
        def _():
            o_ref[...] = part

        @pl.when(mm > 0)
        def _():
            o_ref[...] += part

    return pl.pallas_call(
        body, name=name, grid=(ng, m // tm),
        in_specs=[pl.BlockSpec((tm, POOL_GROUP), lambda j, mm: (mm, j)),
                  pl.BlockSpec((tm, POOL_GROUP), lambda j, mm: (mm, j))],
        out_specs=pl.BlockSpec((N_CHIPS, None, rows, POOL_GROUP), lambda j, mm: (0, j, 0, 0)),
        out_shape=jax.ShapeDtypeStruct((N_CHIPS, ng, rows, POOL_GROUP), F32),
        compiler_params=_params("parallel", "arbitrary"))(a, g)


def rowwise(fn, rows, bcast, outs, accs=(), *, name, aliases=None):
    m = rows[0][0].shape[0]
    tr = m // 16
    n_in = len(rows) + len(bcast)

    def body(*refs):
        vals = [r[...] for r in refs[:n_in]]
        o_vals, a_vals = fn(*vals)
        for ref, val in zip(refs[n_in:n_in + len(outs)], o_vals, strict=True):
            ref[...] = val.astype(ref.dtype)
        step = pl.program_id(0)
        for ref, val in zip(refs[n_in + len(outs):], a_vals, strict=True):
            @pl.when(step == 0)
            def _(ref=ref, val=val):
                ref[...] = val

            @pl.when(step > 0)
            def _(ref=ref, val=val):
                ref[...] += val

    in_specs = [pl.BlockSpec((tr, w), functools.partial(lambda i, cb: (i, cb), cb=cb)) for _, w, cb in rows]
    in_specs += [pl.BlockSpec(b.shape, lambda i: (0, 0)) for b in bcast]
    out_specs = [pl.BlockSpec((tr, w), functools.partial(lambda i, cb: (i, cb), cb=cb)) for _, _, w, cb in outs]
    out_specs += [pl.BlockSpec((1, w), lambda i: (0, 0)) for w in accs]
    out_shape = [jax.ShapeDtypeStruct((m, cols), dt) for cols, dt, _, _ in outs]
    out_shape += [jax.ShapeDtypeStruct((1, w), F32) for w in accs]
    res = pl.pallas_call(
        body, name=name, grid=(m // tr,), in_specs=in_specs, out_specs=out_specs, out_shape=out_shape,
        input_output_aliases=aliases or {},
        compiler_params=_params("arbitrary"))(*[r[0] for r in rows], *bcast)
    return res[:len(outs)], res[len(outs):]


def _rms(h, g):
    return h * lax.rsqrt(jnp.mean(h * h, axis=-1, keepdims=True) + EPS) * g


def _colsum(v):
    return jnp.sum(v, axis=0, keepdims=True)


def rms_fwd(h, g, *, name):
    (n,), _ = rowwise(lambda hv, gv: ([_rms(hv, gv)], []), [(h, D_MODEL, 0)], [g],
                      [(D_MODEL, BF16, D_MODEL, 0)], name=name)
    return n


def rms_bwd(h, g, dn, dh_out, *, name):
    def fn(hv, dnv, dhv, gv):
        _, vjp = jax.vjp(_rms, hv, gv)
        dh, dg = vjp(dnv)
        return [dhv + dh], [dg]

    (dh,), (dg,) = rowwise(fn, [(h, D_MODEL, 0), (dn, D_MODEL, 0), (dh_out, D_MODEL, 0)], [g],
                           [(D_MODEL, F32, D_MODEL, 0)], [D_MODEL], name=name)
    return dh, dg


def _s5_gate(y0, q, z):
    yg = jax.nn.gelu(y0)
    return yg * jax.nn.sigmoid(q) * jax.nn.silu(z)


def _pool_gate(o, z, scale):
    return o * scale * jax.nn.silu(z)


def _shift_rows(v, d, down):
    t = v.shape[0]
    row = lax.broadcasted_iota(jnp.int32, v.shape, 0)
    if down:
        return jnp.where(row >= d, pltpu.roll(v, d, 0), 0.0)
    return jnp.where(row < t - d, pltpu.roll(v, t - d, 0), 0.0)


def _cmul(ar, ai, br, bi):
    return ar * br - ai * bi, ar * bi + ai * br


def _scan(xr, xi, ar, ai, cr, ci, down):
    t, n = xr.shape
    nb = t // SUBLANES
    pw = [(ar, ai)]
    for _ in range(SUBLANES - 1):
        pw.append(_cmul(*pw[-1], ar, ai))
    sub = lax.broadcasted_iota(jnp.int32, (SUBLANES, n), 0)

    def table(exps):
        tr, ti = jnp.zeros((SUBLANES, n), F32), jnp.zeros((SUBLANES, n), F32)
        for r, e in enumerate(exps):
            if e:
                tr, ti = jnp.where(sub == r, pw[e - 1][0], tr), jnp.where(sub == r, pw[e - 1][1], ti)
        return tr[None], ti[None]

    sr, si = xr.reshape(nb, SUBLANES, n), xi.reshape(nb, SUBLANES, n)
    d = 1
    while d < SUBLANES:
        mr, mi = table([d if (r >= d if down else r < SUBLANES - d) else 0 for r in range(SUBLANES)])
        turn = d if down else SUBLANES - d
        hr, hi = pltpu.roll(sr, turn, 1), pltpu.roll(si, turn, 1)
        sr, si = sr + mr * hr - mi * hi, si + mr * hi + mi * hr
        d *= 2
    edge = SUBLANES - 1 if down else 0
    qr, qi = table([r + 1 if down else SUBLANES - r for r in range(SUBLANES)])
    qr, qi = qr[0], qi[0]
    in_r, in_i = jnp.broadcast_to(cr, (SUBLANES, n)), jnp.broadcast_to(ci, (SUBLANES, n))
    out_r, out_i = [None] * nb, [None] * nb
    for b in (range(nb) if down else reversed(range(nb))):
        out_r[b] = sr[b] + qr * in_r - qi * in_i
        out_i[b] = si[b] + qr * in_i + qi * in_r
        in_r = jnp.broadcast_to(out_r[b][edge:edge + 1, :], (SUBLANES, n))
        in_i = jnp.broadcast_to(out_i[b][edge:edge + 1, :], (SUBLANES, n))
    return jnp.concatenate(out_r, axis=0), jnp.concatenate(out_i, axis=0), in_r[0:1, :], in_i[0:1, :]


def s5_fwd(uz, bblk, cblk, ar, ai, dskip, *, name, job=None):
    lp = uz.shape[0]
    t = SCAN_ROWS
    nst = S5_GROUPS * S5_STATE

    def body(u_ref, b_ref, c_ref, ar_ref, ai_ref, d_ref, y_ref, yg_ref, sr_ref, si_ref, cr, ci):
        step = pl.program_id(1)

        @pl.when(step == 0)
        def _():
            cr[...] = jnp.zeros_like(cr)
            ci[...] = jnp.zeros_like(ci)

        u = u_ref[...]
        a_r, a_i = ar_ref[...], ai_ref[...]
        x = jnp.dot(u.astype(BF16), b_ref[...].astype(BF16), preferred_element_type=F32)
        sr, si, cr[...], ci[...] = _scan(x[:, :STATE_BLOCK], x[:, STATE_BLOCK:], a_r, a_i, cr[...], ci[...], True)
        sr_ref[...] = sr
        si_ref[...] = si
        s = jnp.concatenate([sr, si], axis=1).astype(BF16)
        y0 = lax.dot_general(s, c_ref[...].astype(BF16), (((1,), (1,)), ((), ())),
                             preferred_element_type=F32) + d_ref[...] * u
        y_ref[...] = y0
        yg_ref[...] = jax.nn.gelu(y0).astype(yg_ref.dtype)

    return _call(
        body, name=name, grid=(S5_BLOCKS, lp // t),
        in_specs=[pl.BlockSpec((t, LANE_BLOCK), lambda b, c: (c, b)),
                  pl.BlockSpec((None, LANE_BLOCK, 2 * STATE_BLOCK), lambda b, c: (b, 0, 0)),
                  pl.BlockSpec((None, LANE_BLOCK, 2 * STATE_BLOCK), lambda b, c: (b, 0, 0)),
                  pl.BlockSpec((1, STATE_BLOCK), lambda b, c: (0, b)),
                  pl.BlockSpec((1, STATE_BLOCK), lambda b, c: (0, b)),
                  pl.BlockSpec((1, LANE_BLOCK), lambda b, c: (0, b))],
        out_specs=[pl.BlockSpec((t, LANE_BLOCK), lambda b, c: (c, b)),
                   pl.BlockSpec((t, LANE_BLOCK), lambda b, c: (c, b)),
                   pl.BlockSpec((t, STATE_BLOCK), lambda b, c: (c, b)),
                   pl.BlockSpec((t, STATE_BLOCK), lambda b, c: (c, b))],
        out_shape=[jax.ShapeDtypeStruct((lp, D_MODEL), F32), jax.ShapeDtypeStruct((lp, D_MODEL), BF16),
                   jax.ShapeDtypeStruct((lp, nst), F32), jax.ShapeDtypeStruct((lp, nst), F32)],
        scratch=[pltpu.VMEM((1, STATE_BLOCK), F32), pltpu.VMEM((1, STATE_BLOCK), F32)],
        args=(uz, bblk, cblk, ar, ai, dskip), semantics=("parallel", "arbitrary"), job=job)


def s5_bwd(dy_direct, dyg, y0, uz, sr, si, bblk, cblk, ar, ai, dskip, dp, *, name, job=None):
    lp = uz.shape[0]
    t = SCAN_ROWS
    nch = lp // t
    nst = S5_GROUPS * S5_STATE

    def body(dyd_ref, dyg_ref, y0_ref, u_ref, sr_ref, si_ref, b_ref, c_ref, ar_ref, ai_ref, d_ref, _, du_ref,
             db_ref, dc_ref, dar_ref, dai_ref, dd_ref, cr, ci):
        step = pl.program_id(1)

        @pl.when(step == 0)
        def _():
            cr[...] = jnp.zeros_like(cr)
            ci[...] = jnp.zeros_like(ci)

        dy = dyd_ref[...] + jax.vjp(jax.nn.gelu, y0_ref[...])[1](dyg_ref[...])[0]
        u = u_ref[...]
        dyb, ub = dy.astype(BF16), u.astype(BF16)
        a_r, a_i = ar_ref[...], -ai_ref[...]
        g = jnp.dot(dyb, c_ref[...].astype(BF16), preferred_element_type=F32)
        gr, gi = g[:, :STATE_BLOCK], g[:, STATE_BLOCK:]
        nxt_r, nxt_i = cr[...], ci[...]
        last = lax.broadcasted_iota(jnp.int32, gr.shape, 0) == t - 1
        lr, li, cr[...], ci[...] = _scan(gr, gi, a_r, a_i, nxt_r, nxt_i, False)
        nr = _shift_rows(lr, 1, False) + jnp.where(last, nxt_r, 0.0)
        ni = _shift_rows(li, 1, False) + jnp.where(last, nxt_i, 0.0)
        s_r, s_i = sr_ref[...], si_ref[...]
        dar = _colsum(s_r * nr + s_i * ni)
        dai = _colsum(s_r * ni - s_i * nr)
        lam = jnp.concatenate([lr, li], axis=1).astype(BF16)
        sb = jnp.concatenate([s_r, s_i], axis=1).astype(BF16)
        dc = lax.dot_general(dyb, sb, (((0,), (0,)), ((), ())), preferred_element_type=F32)
        db = lax.dot_general(ub, lam, (((0,), (0,)), ((), ())), preferred_element_type=F32)
        du_ref[...] = (lax.dot_general(lam, b_ref[...].astype(BF16), (((1,), (1,)), ((), ())),
                                       preferred_element_type=F32) + d_ref[...] * dy).astype(du_ref.dtype)
        dd = _colsum(dy * u)

        @pl.when(step == 0)
        def _():
            db_ref[...] = db
            dc_ref[...] = dc
            dar_ref[...] = dar
            dai_ref[...] = dai
            dd_ref[...] = dd

        @pl.when(step > 0)
        def _():
            db_ref[...] += db
            dc_ref[...] += dc
            dar_ref[...] += dar
            dai_ref[...] += dai
            dd_ref[...] += dd

    rev = lambda b, c: (nch - 1 - c, b)
    return _call(
        body, name=name, grid=(S5_BLOCKS, nch),
        in_specs=[pl.BlockSpec((t, LANE_BLOCK), rev), pl.BlockSpec((t, LANE_BLOCK), rev),
                  pl.BlockSpec((t, LANE_BLOCK), rev), pl.BlockSpec((t, LANE_BLOCK), rev),
                  pl.BlockSpec((t, STATE_BLOCK), rev), pl.BlockSpec((t, STATE_BLOCK), rev),
                  pl.BlockSpec((None, LANE_BLOCK, 2 * STATE_BLOCK), lambda b, c: (b, 0, 0)),
                  pl.BlockSpec((None, LANE_BLOCK, 2 * STATE_BLOCK), lambda b, c: (b, 0, 0)),
                  pl.BlockSpec((1, STATE_BLOCK), lambda b, c: (0, b)),
                  pl.BlockSpec((1, STATE_BLOCK), lambda b, c: (0, b)),
                  pl.BlockSpec((1, LANE_BLOCK), lambda b, c: (0, b)),
                  pl.BlockSpec(memory_space=pl.ANY)],
        out_specs=[pl.BlockSpec((t, LANE_BLOCK), rev),
                   pl.BlockSpec((None, LANE_BLOCK, 2 * STATE_BLOCK), lambda b, c: (b, 0, 0)),
                   pl.BlockSpec((None, LANE_BLOCK, 2 * STATE_BLOCK), lambda b, c: (b, 0, 0)),
                   pl.BlockSpec((1, STATE_BLOCK), lambda b, c: (0, b)),
                   pl.BlockSpec((1, STATE_BLOCK), lambda b, c: (0, b)),
                   pl.BlockSpec((1, LANE_BLOCK), lambda b, c: (0, b))],
        out_shape=[jax.ShapeDtypeStruct(dp.shape, dp.dtype),
                   jax.ShapeDtypeStruct((S5_BLOCKS, LANE_BLOCK, 2 * STATE_BLOCK), F32),
                   jax.ShapeDtypeStruct((S5_BLOCKS, LANE_BLOCK, 2 * STATE_BLOCK), F32),
                   jax.ShapeDtypeStruct((1, nst), F32), jax.ShapeDtypeStruct((1, nst), F32),
                   jax.ShapeDtypeStruct((1, D_MODEL), F32)],
        scratch=[pltpu.VMEM((1, STATE_BLOCK), F32), pltpu.VMEM((1, STATE_BLOCK), F32)],
        aliases={11: 0}, args=(dy_direct, dyg, y0, uz, sr, si, bblk, cblk, ar, ai, dskip, dp),
        semantics=("parallel", "arbitrary"), job=job)


def _s5_prep(lam_re, lam_im, log_dt, b_re, b_im, c_re, c_im):
    dt = jnp.exp(log_dt)
    mag = jnp.exp(lam_re * dt)
    ar = mag * jnp.cos(lam_im * dt)
    ai = mag * jnp.sin(lam_im * dt)
    den = lam_re * lam_re + lam_im * lam_im
    kr = ((ar - 1.0) * lam_re + ai * lam_im) / den
    ki = (ai * lam_re - (ar - 1.0) * lam_im) / den
    bbr = kr[:, None, :] * b_re - ki[:, None, :] * b_im
    bbi = kr[:, None, :] * b_im + ki[:, None, :] * b_re
    rows = S5_GROUPS * S5_GROUP
    expand = (lax.broadcasted_iota(jnp.int32, (S5_STATE, STATE_BLOCK), 1) % S5_STATE
              == lax.broadcasted_iota(jnp.int32, (S5_STATE, STATE_BLOCK), 0)).astype(F32)
    own = ((lax.broadcasted_iota(jnp.int32, (rows, STATE_BLOCK), 0) // S5_GROUP) % GROUPS_PER_BLOCK
           == lax.broadcasted_iota(jnp.int32, (rows, STATE_BLOCK), 1) // S5_STATE).astype(F32)

    def blocks(v):
        wide = jnp.dot(v.reshape(rows, S5_STATE), expand, precision=lax.Precision.HIGHEST,
                       preferred_element_type=F32)
        return (wide * own).reshape(S5_BLOCKS, LANE_BLOCK, STATE_BLOCK)

    btab = jnp.concatenate([blocks(bbr), blocks(bbi)], axis=2)
    ctab = jnp.concatenate([blocks(c_re), -blocks(c_im)], axis=2)
    return ar, ai, btab, ctab


_S5_PREP_OUT = [(S5_GROUPS, S5_STATE), (S5_GROUPS, S5_STATE), (S5_BLOCKS, LANE_BLOCK, 2 * STATE_BLOCK),
                (S5_BLOCKS, LANE_BLOCK, 2 * STATE_BLOCK)]


def s5_prep(params, *, name):
    def body(*refs):
        for ref, val in zip(refs[7:], _s5_prep(*[r[...] for r in refs[:7]]), strict=True):
            ref[...] = val

    return pl.pallas_call(body, name=name, out_shape=[jax.ShapeDtypeStruct(s, F32) for s in _S5_PREP_OUT],
                          compiler_params=pltpu.CompilerParams(vmem_limit_bytes=VMEM_LIMIT))(*params)


def s5_prep_bwd(params, cotangents, *, name):
    def body(*refs):
        grads = jax.vjp(_s5_prep, *[r[...] for r in refs[:7]])[1](tuple(r[...] for r in refs[7:11]))
        for ref, val in zip(refs[11:], grads, strict=True):
            ref[...] = val

    return pl.pallas_call(body, name=name, out_shape=[jax.ShapeDtypeStruct(p.shape, F32) for p in params],
                          compiler_params=pltpu.CompilerParams(vmem_limit_bytes=VMEM_LIMIT))(*params, *cotangents)


def _silu_grad(z):
    s = jax.nn.sigmoid(z)
    return s * (1.0 + z * (1.0 - s))


def conv_fwd(p, conv_w, conv_b, *, name):
    lp = p.shape[0]
    tr = lp // 16
    e = CONV_E
    hb = tr // 8

    def body(p_ref, cgh_ref, vh_ref, w_ref, b_ref, o_ref):
        i = pl.program_id(0)
        bg, cg, v, z = (p_ref[:, k * e:(k + 1) * e] for k in range(4))
        hc = cg * v
        halo = jnp.where(i > 0, cgh_ref[...] * vh_ref[...], 0.0)
        ext = jnp.concatenate([halo, hc], axis=0)
        w = w_ref[...]
        conv = (w[0:1] * pltpu.roll(ext, 2, 0)[8:] + w[1:2] * pltpu.roll(ext, 1, 0)[8:] + w[2:3] * hc
                + b_ref[...])
        o_ref[...] = (bg * conv * jax.nn.silu(z)).astype(o_ref.dtype)

    prev = lambda col: (lambda i: (jnp.maximum(i * hb - 1, 0), col))
    return pl.pallas_call(
        body, name=name, grid=(lp // tr,),
        in_specs=[pl.BlockSpec((tr, 4 * e), lambda i: (i, 0)),
                  pl.BlockSpec((8, e), prev(1)), pl.BlockSpec((8, e), prev(2)),
                  pl.BlockSpec((3, e), lambda i: (0, 0)), pl.BlockSpec((1, e), lambda i: (0, 0))],
        out_specs=pl.BlockSpec((tr, e), lambda i: (i, 0)),
        out_shape=jax.ShapeDtypeStruct((lp, e), BF16),
        compiler_params=_params("parallel"))(p, p, p, conv_w, conv_b)


def conv_bwd(dy3, p, conv_w, conv_b, *, name):
    lp = p.shape[0]
    tr = lp // 16
    e = CONV_E
    hb = tr // 8
    nt = lp // tr
    width = 512

    def body(dy_ref, p_ref, cgh_ref, vh_ref, dyn_ref, bgn_ref, zn_ref, w_ref, b_ref, dp_ref, dw_ref, db_ref):
        i = pl.program_id(0)
        for c0 in range(0, e, width):
            cols = slice(c0, c0 + width)
            dy = dy_ref[:, cols]
            bg, cg, v, z = (p_ref[:, k * e + c0:k * e + c0 + width] for k in range(4))
            w = w_ref[:, cols]
            hc = cg * v
            halo = jnp.where(i > 0, cgh_ref[:, cols] * vh_ref[:, cols], 0.0)
            ext = jnp.concatenate([halo, hc], axis=0)
            hc2, hc1 = pltpu.roll(ext, 2, 0)[8:], pltpu.roll(ext, 1, 0)[8:]
            yc = w[0:1] * hc2 + w[1:2] * hc1 + w[2:3] * hc + b_ref[:, cols]
            sz = jax.nn.silu(z)
            dyc = dy * bg * sz
            dyc_n = jnp.where(i < nt - 1, dyn_ref[:, cols] * bgn_ref[:, cols] * jax.nn.silu(zn_ref[:, cols]), 0.0)
            ext_n = jnp.concatenate([dyc, dyc_n], axis=0)
            n_rows = tr + 8
            dhc = (w[2:3] * dyc + w[1:2] * pltpu.roll(ext_n, n_rows - 1, 0)[:tr]
                   + w[0:1] * pltpu.roll(ext_n, n_rows - 2, 0)[:tr])
            for k, val in enumerate((dy * yc * sz, dhc * v, dhc * cg, dy * bg * yc * _silu_grad(z))):
                dp_ref[:, k * e + c0:k * e + c0 + width] = val.astype(dp_ref.dtype)
            dw = jnp.concatenate([_colsum(dyc * hc2), _colsum(dyc * hc1), _colsum(dyc * hc)], axis=0)
            db = _colsum(dyc)

            @pl.when(i == 0)
            def _(cols=cols, dw=dw, db=db):
                dw_ref[:, cols] = dw
                db_ref[:, cols] = db

            @pl.when(i > 0)
            def _(cols=cols, dw=dw, db=db):
                dw_ref[:, cols] += dw
                db_ref[:, cols] += db

    prev = lambda col: (lambda i: (jnp.maximum(i * hb - 1, 0), col))
    nxt = lambda col: (lambda i: (jnp.minimum((i + 1) * hb, lp // 8 - 1), col))
    return pl.pallas_call(
        body, name=name, grid=(nt,),
        in_specs=[pl.BlockSpec((tr, e), lambda i: (i, 0)), pl.BlockSpec((tr, 4 * e), lambda i: (i, 0)),
                  pl.BlockSpec((8, e), prev(1)), pl.BlockSpec((8, e), prev(2)),
                  pl.BlockSpec((8, e), nxt(0)), pl.BlockSpec((8, e), nxt(0)), pl.BlockSpec((8, e), nxt(3)),
                  pl.BlockSpec((3, e), lambda i: (0, 0)), pl.BlockSpec((1, e), lambda i: (0, 0))],
        out_specs=[pl.BlockSpec((tr, 4 * e), lambda i: (i, 0)), pl.BlockSpec((3, e), lambda i: (0, 0)),
                   pl.BlockSpec((1, e), lambda i: (0, 0))],
        out_shape=[jax.ShapeDtypeStruct((lp, 4 * e), BF16), jax.ShapeDtypeStruct((3, e), F32),
                   jax.ShapeDtypeStruct((1, e), F32)],
        compiler_params=_params("arbitrary"))(dy3, p, p, p, dy3, p, p, conv_w, conv_b)


def _pool_counts(i, tr, rows, offset, w):
    t = (i * tr + offset + 1 + lax.broadcasted_iota(jnp.int32, (rows, 1), 0)).astype(F32)
    return jnp.minimum(t, float(w))


def pool_fwd(p, *, name):
    lp = p.shape[0]
    tr = lp // 16
    h = POOL_HALO
    hb = tr // h

    def body(u_ref, uh_ref, o_ref):
        i = pl.program_id(0)
        u = u_ref[...]
        ext = jnp.concatenate([jnp.where(i > 0, uh_ref[...], 0.0), u], axis=0)
        for k, w in enumerate(POOL_WINDOWS):
            cols = slice(k * POOL_GROUP, (k + 1) * POOL_GROUP)
            s = ext[:, cols]
            d = 1
            while d < w:
                s = s + pltpu.roll(s, d, 0)
                d *= 2
            o_ref[:, cols] = (s[h:] / _pool_counts(i, tr, tr, 0, w) - u[:, cols]).astype(o_ref.dtype)

    return pl.pallas_call(
        body, name=name, grid=(lp // tr,),
        in_specs=[pl.BlockSpec((tr, POOL_E), lambda i: (i, 0)),
                  pl.BlockSpec((h, POOL_E), lambda i: (jnp.maximum(i * hb - 1, 0), 0))],
        out_specs=pl.BlockSpec((tr, POOL_E), lambda i: (i, 0)),
        out_shape=jax.ShapeDtypeStruct((lp, POOL_E), BF16),
        compiler_params=_params("parallel"))(p, p)


def pool_bwd(dmix, dp, *, name):
    lp = dmix.shape[0]
    tr = lp // 16
    h = POOL_HALO
    hb = tr // h
    nt = lp // tr

    def body(dm_ref, dmn_ref, _, du_ref):
        i = pl.program_id(0)
        dm = dm_ref[...]
        dmn = jnp.where(i < nt - 1, dmn_ref[...], 0.0)
        n_rows = tr + h
        for k, w in enumerate(POOL_WINDOWS):
            cols = slice(k * POOL_GROUP, (k + 1) * POOL_GROUP)
            s = jnp.concatenate([dm[:, cols] / _pool_counts(i, tr, tr, 0, w),
                                 dmn[:, cols] / _pool_counts(i, tr, h, tr, w)], axis=0)
            d = 1
            while d < w:
                s = s + pltpu.roll(s, n_rows - d, 0)
                d *= 2
            du_ref[:, cols] = (s[:tr] - dm[:, cols]).astype(du_ref.dtype)

    return pl.pallas_call(
        body, name=name, grid=(nt,),
        in_specs=[pl.BlockSpec((tr, POOL_E), lambda i: (i, 0)),
                  pl.BlockSpec((h, POOL_E), lambda i: (jnp.minimum((i + 1) * hb, lp // h - 1), 0)),
                  pl.BlockSpec(memory_space=pl.ANY)],
        out_specs=pl.BlockSpec((tr, POOL_E), lambda i: (i, 0)),
        out_shape=jax.ShapeDtypeStruct(dp.shape, dp.dtype),
        input_output_aliases={2: 0},
        compiler_params=_params("parallel"))(dmix, dmix, dp)


def _adamw_math(w, g, m, v):
    nm = ADAM_B1 * m + (1.0 - ADAM_B1) * g
    nv = ADAM_B2 * v + (1.0 - ADAM_B2) * jnp.square(g)
    m_hat = nm / (1.0 - ADAM_B1 ** ADAM_STEP)
    v_hat = nv / (1.0 - ADAM_B2 ** ADAM_STEP)
    return -ADAM_LR * (m_hat / (jnp.sqrt(v_hat) + ADAM_EPS) + ADAM_WD * w), nm, nv


def adamw(w, g, m, v, *, name):
    r, c = w.shape
    tr = _tile(r, max(8, (1 << 19) // c), 8)

    def body(w_ref, g_ref, m_ref, v_ref, d_ref, nm_ref, nv_ref):
        d_ref[...], nm_ref[...], nv_ref[...] = _adamw_math(w_ref[...], g_ref[...], m_ref[...], v_ref[...])

    spec = pl.BlockSpec((tr, c), lambda i: (i, 0))
    return pl.pallas_call(
        body, name=name, grid=(r // tr,), in_specs=[spec] * 4, out_specs=[spec] * 3,
        out_shape=[jax.ShapeDtypeStruct((r, c), F32)] * 3,
        compiler_params=_params("parallel"))(w, g, m, v)


def adamw_many(quads, *, name):
    shape = quads[0][0].shape
    steps = 8
    spec = pl.BlockSpec((shape[0] // steps,) + shape[1:], lambda i: (i, 0, 0))
    n = len(quads)

    def body(*refs):
        for q in range(n):
            w_ref, g_ref, m_ref, v_ref = refs[4 * q:4 * q + 4]
            d_ref, nm_ref, nv_ref = refs[4 * n + 3 * q:4 * n + 3 * q + 3]
            d_ref[...], nm_ref[...], nv_ref[...] = _adamw_math(w_ref[...], g_ref[...], m_ref[...], v_ref[...])

    res = pl.pallas_call(
        body, name=name, grid=(steps,), in_specs=[spec] * (4 * n), out_specs=[spec] * (3 * n),
        out_shape=[jax.ShapeDtypeStruct(shape, F32)] * (3 * n),
        compiler_params=_params("parallel"))(*[a for quad in quads for a in quad])
    return [tuple(res[3 * q:3 * q + 3]) for q in range(n)]


WIRE_WIDTH = 1024
WIRE_ROWS = 1024


def _wire_plan(shapes):
    starts, row = [], 0
    for s in shapes:
        r, c = (1, s[0]) if len(s) == 1 else s
        starts.append(row)
        row += -(-(r * max(1, c // WIRE_WIDTH)) // 8) * 8
    assert row <= WIRE_ROWS, row
    return starts


def _wire_cells(shape):
    if len(shape) == 1:
        n = shape[0]
        if n <= WIRE_WIDTH:
            return [(0, n, (slice(None),))]
        return [(h, WIRE_WIDTH, (slice(h * WIRE_WIDTH, (h + 1) * WIRE_WIDTH),)) for h in range(n // WIRE_WIDTH)]
    r, c = shape
    if c <= WIRE_WIDTH:
        return [(None, c, (slice(None), slice(None)))]
    per = c // WIRE_WIDTH
    return [(j * per + h, WIRE_WIDTH, (j, slice(h * WIRE_WIDTH, (h + 1) * WIRE_WIDTH)))
            for j in range(r) for h in range(per)]


def pack_small(arrays, *, name):
    shapes = [a.shape for a in arrays]
    starts = _wire_plan(shapes)

    def body(*refs):
        out = refs[-1]
        out[...] = jnp.zeros_like(out)
        for ref, shape, row0 in zip(refs[:-1], shapes, starts, strict=True):
            for off, cols, idx in _wire_cells(shape):
                if off is None:
                    out[row0:row0 + shape[0], 0:cols] = ref[idx]
                else:
                    out[row0 + off, 0:cols] = ref[idx]

    return pl.pallas_call(body, name=name, out_shape=jax.ShapeDtypeStruct((WIRE_ROWS, WIRE_WIDTH), F32),
                          compiler_params=pltpu.CompilerParams(vmem_limit_bytes=VMEM_LIMIT))(*arrays)


def update_small(packed, starts, triples, shard_cells, chip, *, name):
    n = len(triples)

    def body(chip_ref, p_ref, *refs):
        k = chip_ref[0]
        ins, outs = refs[:3 * n], refs[3 * n:]

        def cut(row0, rows, cols):
            c0, c1 = cols
            if isinstance(rows, int):
                return p_ref[row0 + rows, c0:c1]
            return p_ref[row0 + rows[0]:row0 + rows[1], c0:c1]

        for q in range(n):
            w_ref, m_ref, v_ref = ins[3 * q:3 * q + 3]
            g_ref, d_ref, nm_ref, nv_ref = outs[4 * q:4 * q + 4]
            shape = w_ref.shape
            if shard_cells[q] is None:
                pieces = [(cut(starts[q], (0, shape[0]) if off is None else off, (0, cols)), idx)
                          for off, cols, idx in _wire_cells(shape)]
            else:
                by_chip = [shard_cells[q](kk) for kk in range(N_CHIPS)]
                pieces = []
                for i, (_, _, idx) in enumerate(by_chip[0]):
                    g = cut(starts[q], *by_chip[0][i][:2])
                    for kk in range(1, N_CHIPS):
                        g = jnp.where(k == kk, cut(starts[q], *by_chip[kk][i][:2]), g)
                    pieces.append((g, idx))
            for g, idx in pieces:
                g_ref[idx] = g
                d_ref[idx], nm_ref[idx], nv_ref[idx] = _adamw_math(w_ref[idx], g, m_ref[idx], v_ref[idx])

    whole = lambda a: pl.BlockSpec(a.shape, lambda i, c_ref, nd=a.ndim: (0,) * nd)
    flat = [a for t in triples for a in t]
    out_shape = [jax.ShapeDtypeStruct(t[0].shape, F32) for t in triples for _ in range(4)]
    res = pl.pallas_call(
        body, name=name,
        grid_spec=pltpu.PrefetchScalarGridSpec(
            num_scalar_prefetch=1, grid=(1,),
            in_specs=[whole(packed)] + [whole(a) for a in flat],
            out_specs=[pl.BlockSpec(s.shape, lambda i, c_ref, nd=len(s.shape): (0,) * nd) for s in out_shape]),
        out_shape=out_shape,
        compiler_params=_params("arbitrary"))(chip.reshape(1).astype(jnp.int32), packed, *flat)
    return [tuple(res[4 * q:4 * q + 4]) for q in range(n)]


ANY = pl.BlockSpec(memory_space=pl.ANY)


def _place():
    x, y, c = lax.axis_index("x"), lax.axis_index("y"), lax.axis_index("c")
    return x, y, c, [(1 - x, y), (x, 1 - y), (1 - x, 1 - y)]


DMA_PIECE_BYTES = 512 * 1024


def _pieces(src, dst):
    shape = src.shape
    rows, cols = shape[-2], shape[-1]
    item = jnp.dtype(src.dtype).itemsize
    unit = 32 // item
    want = max(1, (rows * cols * item) // DMA_PIECE_BYTES)
    n = 1
    if rows % unit == 0:
        n = max(d for d in range(1, rows // unit + 1) if (rows // unit) % d == 0 and d <= want)
    size = rows // n
    out = []
    for lead in (range(shape[0]) if len(shape) == 3 else [None]):
        for i in range(n):
            sel = (pl.ds(i * size, size), slice(None)) if lead is None else (lead, pl.ds(i * size, size), slice(None))
            out.append((src.at[sel], dst.at[sel]))
    return out


def _send(src, dst, send_sem, recv_sem, peer, start=True):
    if start:
        for s, d in _pieces(src, dst):
            pltpu.make_async_remote_copy(src_ref=s, dst_ref=d, send_sem=send_sem, recv_sem=recv_sem,
                                         device_id=peer, device_id_type=MESH).start()
    return pltpu.make_async_remote_copy(src_ref=src, dst_ref=dst, send_sem=send_sem, recv_sem=recv_sem,
                                        device_id=peer, device_id_type=MESH)


def run_job(job, *, name):
    n_in, n_out = len(job.arrays), len(job.out_shapes)

    def body(*refs):
        job.fn(refs[:n_in], refs[n_in:n_in + n_out], refs[n_in + n_out:], "both")

    return pl.pallas_call(body, name=name, in_specs=[ANY] * n_in, out_specs=[ANY] * n_out,
                          out_shape=job.out_shapes, scratch_shapes=job.sems)(*job.arrays)


def gather_chips(shards, *, name):
    n = len(shards)

    def body(*refs):
        ins, outs = refs[:n], refs[n:2 * n]
        send_sems, recv_sems = refs[2 * n:]
        x, y, c, chips = _place()
        k = 2 * x + y
        copies = []
        for a in range(n):
            for j, peer in enumerate([(px, py, c) for px, py in chips] + [(x, y, 1 - c)]):
                copies.append(_send(ins[a], outs[a].at[k], send_sems.at[a, j], recv_sems.at[a, j], peer))
        for cp in copies:
            cp.wait()

    return pl.pallas_call(
        body, name=name, in_specs=[ANY] * n, out_specs=[ANY] * n,
        out_shape=[jax.ShapeDtypeStruct((N_CHIPS,) + s.shape, s.dtype) for s in shards],
        scratch_shapes=[pltpu.SemaphoreType.DMA((n, 4)), pltpu.SemaphoreType.DMA((n, 4))])(*shards)


def gather_by_halves_job(shards):
    n = len(shards)

    def fn(ins, outs, sems, phase):
        send_sems, recv_sems = sems
        x, y, c, chips = _place()
        k = 2 * x + y
        sibling = (x, y, 1 - c)
        begin = phase != "finish"
        waits, arrivals = [], []
        for a in range(n):
            half = ins[a].shape[0] // 2
            waits.append(_send(ins[a], outs[a].at[k], send_sems.at[a, 6], recv_sems.at[a, 6], sibling, begin))
            mine = pl.ds(c * half, half)
            for j, (px, py) in enumerate(chips):
                arrivals.append(_send(ins[a].at[mine, :], outs[a].at[k, mine, :], send_sems.at[a, j],
                                      recv_sems.at[a, j], (px, py, c), begin))
        if phase == "start":
            return
        i = 0
        for a in range(n):
            half = ins[a].shape[0] // 2
            mine = pl.ds(c * half, half)
            for j, (px, py) in enumerate(chips):
                arrivals[i].wait_recv()
                landed = outs[a].at[2 * px + py, mine, :]
                waits.append(_send(landed, landed, send_sems.at[a, 3 + j], recv_sems.at[a, 3 + j], sibling))
                i += 1
        for cp in arrivals:
            cp.wait_send()
        for cp in waits:
            cp.wait()

    return Job(shards, [jax.ShapeDtypeStruct((N_CHIPS,) + s.shape, s.dtype) for s in shards],
               [pltpu.SemaphoreType.DMA((n, 7)), pltpu.SemaphoreType.DMA((n, 7))], fn)


def gather_halves_job(shards):
    n = len(shards)

    def fn(ins, outs, sems, phase):
        send_sems, recv_sems = sems
        x, y, c, chips = _place()
        k = 2 * x + y
        copies = []
        for a in range(n):
            half = ins[a].shape[0] // 2
            mine = pl.ds(c * half, half)
            copies.append(_send(ins[a], outs[a].at[k], send_sems.at[a, 3], recv_sems.at[a, 3], (x, y, 1 - c),
                                phase != "finish"))
            for j, (px, py) in enumerate(chips):
                copies.append(_send(ins[a].at[mine, :], outs[a].at[k, mine, :], send_sems.at[a, j],
                                    recv_sems.at[a, j], (px, py, c), phase != "finish"))
        if phase != "start":
            for cp in copies:
                cp.wait()

    return Job(shards, [jax.ShapeDtypeStruct((N_CHIPS,) + s.shape, s.dtype) for s in shards],
               [pltpu.SemaphoreType.DMA((n, 4)), pltpu.SemaphoreType.DMA((n, 4))], fn)


def forward_halves_job(gathered):
    n = len(gathered)

    def fn(ins, outs, sems, phase):
        send_sems, recv_sems = sems
        x, y, c, chips = _place()
        copies = []
        for a in range(n):
            half = outs[a].shape[1] // 2
            for j, (px, py) in enumerate(chips):
                landed = outs[a].at[2 * px + py, pl.ds(c * half, half), :]
                copies.append(_send(landed, landed, send_sems.at[a, j], recv_sems.at[a, j], (x, y, 1 - c),
                                    phase != "finish"))
        if phase != "start":
            for cp in copies:
                cp.wait()

    return Job(gathered, [jax.ShapeDtypeStruct(g.shape, g.dtype) for g in gathered],
               [pltpu.SemaphoreType.DMA((n, 3)), pltpu.SemaphoreType.DMA((n, 3))], fn, in_place=True)


def swap_job(grads):
    n = len(grads)

    def fn(ins, outs, sems, phase):
        send_sems, recv_sems = sems
        x, y, c, _ = _place()
        copies = []
        for a in range(n):
            half = ins[a].shape[1] // 2
            copies.append(_send(ins[a].at[:, pl.ds((1 - c) * half, half), :], outs[a], send_sems.at[a],
                                recv_sems.at[a], (x, y, 1 - c), phase != "finish"))
        if phase != "start":
            for cp in copies:
                cp.wait()

    return Job(grads, [jax.ShapeDtypeStruct((g.shape[0], g.shape[1] // 2, g.shape[2]), g.dtype) for g in grads],
               [pltpu.SemaphoreType.DMA((n,)), pltpu.SemaphoreType.DMA((n,))], fn)


def _chip_slot(s, k):
    rel = s ^ k
    return (rel >> 1) | ((rel & 1) << 1)


def sum_cores(g, theirs, ck, out_dtype, *, name):
    n, r, cols = g.shape
    half = r // 2
    tr = _tile(half, max(16, (1 << 19) // cols), 16)
    nb = half // tr

    def body(ck_ref, g_ref, t_ref, o_ref):
        o_ref[...] = (g_ref[...] + t_ref[...]).astype(o_ref.dtype)

    return pl.pallas_call(
        body, name=name,
        grid_spec=pltpu.PrefetchScalarGridSpec(
            num_scalar_prefetch=1, grid=(n, nb),
            in_specs=[pl.BlockSpec((None, tr, cols), lambda s, i, ck_ref: (s, ck_ref[0] * nb + i, 0)),
                      pl.BlockSpec((None, tr, cols), lambda s, i, ck_ref: (s, i, 0))],
            out_specs=pl.BlockSpec((None, tr, cols), lambda s, i, ck_ref: (_chip_slot(s, ck_ref[1]), i, 0))),
        out_shape=jax.ShapeDtypeStruct((n, half, cols), out_dtype),
        compiler_params=_params("parallel", "parallel"))(ck, g, theirs)


def scatter_job(parts):
    n = len(parts)

    def fn(ins, outs, sems, phase):
        send_sems, recv_sems = sems
        x, y, c, chips = _place()
        copies = []
        for a in range(n):
            for j, (px, py) in enumerate(chips):
                copies.append(_send(ins[a].at[1 + j], outs[a].at[j], send_sems.at[a, j], recv_sems.at[a, j],
                                    (px, py, c), phase != "finish"))
        if phase != "start":
            for cp in copies:
                cp.wait()

    return Job(parts, [jax.ShapeDtypeStruct((3,) + p.shape[1:], p.dtype) for p in parts],
               [pltpu.SemaphoreType.DMA((n, 3)), pltpu.SemaphoreType.DMA((n, 3))], fn)


def sum_chips(own, others, ck, *, name):
    _, r, cols = own.shape
    tr = _tile(r, max(16, (1 << 19) // cols), 16)

    def body(ck_ref, a_ref, b0_ref, b1_ref, b2_ref, o_ref):
        o_ref[...] = (a_ref[...].astype(F32) + b0_ref[...].astype(F32) + b1_ref[...].astype(F32)
                      + b2_ref[...].astype(F32))

    other = lambda j: pl.BlockSpec((None, tr, cols), lambda i, ck_ref: (j, i, 0))
    return pl.pallas_call(
        body, name=name,
        grid_spec=pltpu.PrefetchScalarGridSpec(
            num_scalar_prefetch=1, grid=(r // tr,),
            in_specs=[pl.BlockSpec((None, tr, cols), lambda i, ck_ref: (0, i, 0)), other(0), other(1), other(2)],
            out_specs=pl.BlockSpec((None, tr, cols), lambda i, ck_ref: (ck_ref[0], i, 0))),
        out_shape=jax.ShapeDtypeStruct((2, r, cols), F32),
        compiler_params=_params("parallel"))(ck, own, others, others, others)


def share_halves(joined, *, name):
    n = len(joined)

    def body(*refs):
        outs = refs[n:2 * n]
        send_sems, recv_sems = refs[2 * n:]
        x, y, c, _ = _place()
        copies = [_send(outs[a].at[c], outs[a].at[c], send_sems.at[a], recv_sems.at[a], (x, y, 1 - c))
                  for a in range(n)]
        for cp in copies:
            cp.wait()

    return pl.pallas_call(
        body, name=name, in_specs=[ANY] * n, out_specs=[ANY] * n,
        out_shape=[jax.ShapeDtypeStruct(j.shape, j.dtype) for j in joined],
        input_output_aliases={a: a for a in range(n)},
        scratch_shapes=[pltpu.SemaphoreType.DMA((n,)), pltpu.SemaphoreType.DMA((n,))])(*joined)


def kernel(x, meta_tokens, norm0_g, l0_w_in, l0_lam_re, l0_lam_im, l0_log_dt, l0_b_re, l0_b_im, l0_c_re, l0_c_im, l0_d_skip, l0_w_glu, l0_b_glu, l0_w_out, norm1_g, l1_w_in, l1_conv_w, l1_conv_b, l1_w_out, norm2_g, l2_w_in, l2_w_grp, l2_b_grp, l2_scale, l2_w_out, norm3_g, l3_w_in, l3_lam_re, l3_lam_im, l3_log_dt, l3_b_re, l3_b_im, l3_c_re, l3_c_im, l3_d_skip, l3_w_glu, l3_b_glu, l3_w_out, final_g, loss_target, m_meta_tokens, m_norm0_g, m_l0_w_in, m_l0_lam_re, m_l0_lam_im, m_l0_log_dt, m_l0_b_re, m_l0_b_im, m_l0_c_re, m_l0_c_im, m_l0_d_skip, m_l0_w_glu, m_l0_b_glu, m_l0_w_out, m_norm1_g, m_l1_w_in, m_l1_conv_w, m_l1_conv_b, m_l1_w_out, m_norm2_g, m_l2_w_in, m_l2_w_grp, m_l2_b_grp, m_l2_scale, m_l2_w_out, m_norm3_g, m_l3_w_in, m_l3_lam_re, m_l3_lam_im, m_l3_log_dt, m_l3_b_re, m_l3_b_im, m_l3_c_re, m_l3_c_im, m_l3_d_skip, m_l3_w_glu, m_l3_b_glu, m_l3_w_out, m_final_g, v_meta_tokens, v_norm0_g, v_l0_w_in, v_l0_lam_re, v_l0_lam_im, v_l0_log_dt, v_l0_b_re, v_l0_b_im, v_l0_c_re, v_l0_c_im, v_l0_d_skip, v_l0_w_glu, v_l0_b_glu, v_l0_w_out, v_norm1_g, v_l1_w_in, v_l1_conv_w, v_l1_conv_b, v_l1_w_out, v_norm2_g, v_l2_w_in, v_l2_w_grp, v_l2_b_grp, v_l2_scale, v_l2_w_out, v_norm3_g, v_l3_w_in, v_l3_lam_re, v_l3_lam_im, v_l3_log_dt, v_l3_b_re, v_l3_b_im, v_l3_c_re, v_l3_c_im, v_l3_d_skip, v_l3_w_glu, v_l3_b_glu, v_l3_w_out, v_final_g):
    given = dict(locals())
    w = {n: given[n] for n in WEIGHTS}
    mom_m = {n: given["m_" + n] for n in WEIGHTS}
    mom_v = {n: given["v_" + n] for n in WEIGHTS}
    seq = x.shape[1]
    real = N_META + seq
    lp = -(-real // SCAN_ROWS) * SCAN_ROWS
    chip = 2 * lax.axis_index("x") + lax.axis_index("y")
    row = lambda a: a.reshape(1, -1)

    shard_bf16 = {n: (w[n].reshape(-1, w[n].shape[-1]) if n == 'l2_w_grp' else w[n]).astype(BF16) for n in BIG}
    layer = lambda i: [n for n in BIG if n.startswith("l%d_" % i)]
    over_ici = lambda names: gather_halves_job([shard_bf16[n] for n in names])
    full = {}

    def landed(names, arrays):
        for n, arr in zip(names, arrays, strict=True):
            if n.endswith('w_in'):
                full[n] = arr
            elif n == 'l2_w_grp':
                full[n] = arr.reshape(N_CHIPS, len(POOL_WINDOWS), POOL_GROUP // N_CHIPS, POOL_GROUP)
            else:
                full[n] = arr.reshape(1, -1, arr.shape[-1])

    landed(['l0_w_in'], run_job(gather_by_halves_job([shard_bf16['l0_w_in']]), name="gather_l0_w_in"))
    full.update(zip(SMALL_SHARDED, gather_chips([w[n] for n in SMALL_SHARDED], name="gather_small_shards"),
                    strict=True))
    meta_full = full['meta_tokens'].transpose(1, 0, 2).reshape(N_META, D_MODEL)
    conv_w_full = full['l1_conv_w'].transpose(1, 0, 2).reshape(3, CONV_E)
    b_grp_full = full['l2_b_grp'].transpose(1, 0, 2).reshape(len(POOL_WINDOWS), POOL_GROUP)

    h0 = jnp.concatenate([meta_full, x[0], jnp.zeros((lp - real, D_MODEL), F32)], axis=0)
    target = jnp.concatenate([jnp.zeros((N_META, D_MODEL), F32), loss_target[0],
                              jnp.zeros((lp - real, D_MODEL), F32)], axis=0)
    grads = {}
    saved = {}
    gip = lambda a: jnp.swapaxes(a, 1, 2)

    def s5_layer_fwd(i, h, first=False):
        pre = "l%d_" % i
        prm = [w[pre + 'lam_re'], w[pre + 'lam_im'], w[pre + 'log_dt'].reshape(-1, 1), gip(w[pre + 'b_re']),
               gip(w[pre + 'b_im']), w[pre + 'c_re'], w[pre + 'c_im']]
        a_re, a_im, bblk, cblk = s5_prep(prm, name=pre + "prep")
        tables = (a_re.reshape(1, -1), a_im.reshape(1, -1), bblk, cblk)
        ar, ai = tables[:2]
        n = rms_fwd(h, row(w["norm%d_g" % i]), name=pre + "norm")
        rest = [pre + 'w_glu', pre + 'w_out']
        if not first:
            uz = mm_nn(n, full[pre + 'w_in'], name=pre + "in")
            y0, yg, sr, si = s5_fwd(uz, bblk, cblk, ar, ai, row(w[pre + 'd_skip']), name=pre + "scan")
            q = mm_nn(yg, full[pre + 'w_glu'], bias=row(w[pre + 'b_glu']), name=pre + "glu")
        else:
            uz, halves = mm_nn(n, full[pre + 'w_in'], name=pre + "in", job=over_ici(rest))
            (y0, yg, sr, si), arrived = s5_fwd(
                uz, bblk, cblk, ar, ai, row(w[pre + 'd_skip']), name=pre + "scan",
                job=Job.both(over_ici(layer(1)), forward_halves_job(halves)))
            landed(rest, arrived[len(layer(1)):])
            q, whole = mm_nn(yg, full[pre + 'w_glu'], bias=row(w[pre + 'b_glu']), name=pre + "glu",
                             job=forward_halves_job(arrived[:len(layer(1))]))
            landed(layer(1), whole)
        (y3,), _ = rowwise(lambda yv, qv, zv: ([_s5_gate(yv, qv, zv)], []),
                           [(y0, D_MODEL, 0), (q, D_MODEL, 0), (uz, D_MODEL, 1)], [],
                           [(D_MODEL, BF16, D_MODEL, 0)], name=pre + "gate")
        h_new = mm_nn(y3, full[pre + 'w_out'], add=h, name=pre + "out")
        saved[i] = dict(h=h, n=n, uz=uz, y0=y0, sr=sr, si=si, yg=yg, q=q, y3=y3, tables=tables, prm=prm)
        return h_new

    def s5_layer_bwd(i, dh, carry=None):
        pre = "l%d_" % i
        s = saved[i]
        ar, ai, bblk, cblk = s['tables']
        if carry is None:
            grads[pre + 'w_out'] = mm_tn(s['y3'], dh, 1, name=pre + "d_w_out")[0]
        else:
            names, pieces = core_parts(carry)
            d_w_out, theirs = mm_tn(s['y3'], dh, 1, name=pre + "d_w_out", job=swap_job(pieces))
            grads[pre + 'w_out'] = d_w_out[0]
            sum_over_cores(names, pieces, theirs)
        dy3 = mm_nt(dh, full[pre + 'w_out'], name=pre + "d_y3")

        def gate_bwd(dyv, yv, qv, zv):
            _, vjp = jax.vjp(_s5_gate, yv, qv, zv)
            dy0, dq, dz = vjp(dyv)
            return [dy0, dq, dz], [_colsum(dq)]

        (dy0_direct, dq, dp), (db_glu,) = rowwise(
            gate_bwd, [(dy3, D_MODEL, 0), (s['y0'], D_MODEL, 0), (s['q'], D_MODEL, 0), (s['uz'], D_MODEL, 1)], [],
            [(D_MODEL, F32, D_MODEL, 0), (D_MODEL, BF16, D_MODEL, 0), (2 * D_MODEL, BF16, D_MODEL, 1)], [D_MODEL],
            name=pre + "d_gate")
        grads[pre + 'b_glu'] = db_glu
        grads[pre + 'w_glu'] = mm_tn(s['yg'], dq, 1, name=pre + "d_w_glu")[0]
        dyg = mm_nt(dq, full[pre + 'w_glu'], name=pre + "d_yg")
        res = s5_bwd(dy0_direct, dyg, s['y0'], s['uz'], s['sr'], s['si'], bblk, cblk, ar, ai,
                     row(w[pre + 'd_skip']), dp, name=pre + "d_scan",
                     job=None if carry is None else scatter_job(scatter_of(carry)))
        if carry is not None:
            res, arrived = res
            scattered(carry, arrived)
        dp, dbblk, dcblk, dar, dai, dd = res
        grads[pre + 'd_skip'] = dd
        cots = (dar.reshape(S5_GROUPS, S5_STATE), dai.reshape(S5_GROUPS, S5_STATE), dbblk, dcblk)
        for key, val in zip(('lam_re', 'lam_im', 'log_dt', 'b_re', 'b_im', 'c_re', 'c_im'),
                            s5_prep_bwd(s['prm'], cots, name=pre + "d_prep"), strict=True):
            grads[pre + key] = val.reshape(-1) if key == 'log_dt' else val
        grads[pre + 'w_in'] = mm_tn(s['n'], dp, N_CHIPS, name=pre + "d_w_in")
        dn = mm_nt(dp, full[pre + 'w_in'], name=pre + "d_n")
        dh_in, dg = rms_bwd(s['h'], row(w["norm%d_g" % i]), dn, dh, name=pre + "d_norm")
        grads["norm%d_g" % i] = dg
        return dh_in

    h1 = s5_layer_fwd(0, h0, first=True)

    n1 = rms_fwd(h1, row(w['norm1_g']), name="l1_norm")
    p1, halves = mm_nn(n1, full['l1_w_in'], name="l1_in", job=over_ici(layer(2)))
    y3_1 = conv_fwd(p1, conv_w_full, row(w['l1_conv_b']), name="l1_conv")
    h2, whole = mm_nn(y3_1, full['l1_w_out'], add=h1, name="l1_out", job=forward_halves_job(halves))
    landed(layer(2), whole)

    n2 = rms_fwd(h2, row(w['norm2_g']), name="l2_norm")
    p2, halves = mm_nn(n2, full['l2_w_in'], name="l2_in", job=over_ici(layer(3)))
    mixed = pool_fwd(p2, name="l2_pool")
    o2 = grp_nn(mixed, full['l2_w_grp'], b_grp_full.reshape(1, -1), name="l2_grp")
    (y3_2,), _ = rowwise(lambda ov, zv, sv: ([_pool_gate(ov, zv, sv)], []),
                         [(o2, POOL_E, 0), (p2, POOL_E, 1)], [row(w['l2_scale'])],
                         [(POOL_E, BF16, POOL_E, 0)], name="l2_gate")
    h3, whole = mm_nn(y3_2, full['l2_w_out'], add=h2, name="l2_out", job=forward_halves_job(halves))
    landed(layer(3), whole)

    h4 = s5_layer_fwd(3, h3)

    def loss_fn(hv, tv, gv):
        i = pl.program_id(0)
        tr = hv.shape[0]
        yf, vjp = jax.vjp(_rms, hv, gv)
        pos = i * tr + lax.broadcasted_iota(jnp.int32, (tr, 1), 0)
        diff = jnp.where((pos >= N_META) & (pos < real), yf - tv, 0.0)
        dh, dg = vjp(diff / D_MODEL)
        return [dh], [dg, _colsum(diff * diff)]

    (dh,), (d_final_g, sq) = rowwise(loss_fn, [(h4, D_MODEL, 0), (target, D_MODEL, 0)], [row(w['final_g'])],
                                     [(D_MODEL, F32, D_MODEL, 0)], [D_MODEL, D_MODEL], name="loss")
    grads['final_g'] = d_final_g
    loss = lax.psum(0.5 / D_MODEL * jnp.sum(sq), ("x", "y", "c"))

    ck = jnp.stack([lax.axis_index("c"), chip]).astype(jnp.int32)
    pair, from_chips = {}, {}

    def chip_major(n):
        g = grads[n]
        if n.endswith('w_in'):
            return g
        if n == 'l2_w_grp':
            return g.reshape(N_CHIPS, -1, POOL_GROUP)
        return g.reshape(N_CHIPS, -1, g.shape[-1])

    def core_parts(i, small=None):
        names, pieces = layer(i), [chip_major(n) for n in layer(i)]
        return (names + ['small'], pieces + [small]) if small is not None else (names, pieces)

    def sum_over_cores(names, pieces, theirs):
        for n, g, t in zip(names, pieces, theirs, strict=True):
            pair[n] = sum_cores(g, t, ck, F32 if n == 'small' else BF16, name="sum_cores_" + n)

    scatter_of = lambda i: [pair[n] for n in layer(i)]

    def scattered(i, arrays):
        from_chips.update(zip(layer(i), arrays, strict=True))

    dh = s5_layer_bwd(3, dh)

    names, pieces = core_parts(3)
    d_w_out, theirs = mm_tn(y3_2, dh, 1, name="l2_d_w_out", job=swap_job(pieces))
    grads['l2_w_out'] = d_w_out[0]
    sum_over_cores(names, pieces, theirs)
    dy3 = mm_nt(dh, full['l2_w_out'], name="l2_d_y3")

    def pool_gate_bwd(dyv, ov, zv, sv):
        _, vjp = jax.vjp(_pool_gate, ov, zv, sv)
        do, dz, dscale = vjp(dyv)
        return [do, dz], [dscale, _colsum(do)]

    (do2, dp2), (dscale, dbgrp) = rowwise(
        pool_gate_bwd, [(dy3, POOL_E, 0), (o2, POOL_E, 0), (p2, POOL_E, 1)], [row(w['l2_scale'])],
        [(POOL_E, BF16, POOL_E, 0), (2 * POOL_E, BF16, POOL_E, 1)], [POOL_E, POOL_E], name="l2_d_gate")
    grads['l2_scale'] = dscale
    grads['l2_b_grp'] = dbgrp
    grads['l2_w_grp'] = grp_tn(mixed, do2, name="l2_d_w_grp")
    dmix = grp_nt(do2, full['l2_w_grp'], name="l2_d_mixed")
    dp2 = pool_bwd(dmix, dp2, name="l2_d_pool")
    grads['l2_w_in'], arrived = mm_tn(n2, dp2, N_CHIPS, name="l2_d_w_in", job=scatter_job(scatter_of(3)))
    scattered(3, arrived)
    dn = mm_nt(dp2, full['l2_w_in'], name="l2_d_n")
    dh, dg = rms_bwd(h2, row(w['norm2_g']), dn, dh, name="l2_d_norm")
    grads['norm2_g'] = dg

    names, pieces = core_parts(2)
    d_w_out, theirs = mm_tn(y3_1, dh, 1, name="l1_d_w_out", job=swap_job(pieces))
    grads['l1_w_out'] = d_w_out[0]
    sum_over_cores(names, pieces, theirs)
    dy3 = mm_nt(dh, full['l1_w_out'], name="l1_d_y3")
    dp1, dcw, dcb = conv_bwd(dy3, p1, conv_w_full, row(w['l1_conv_b']), name="l1_d_conv")
    grads['l1_conv_w'] = dcw
    grads['l1_conv_b'] = dcb
    grads['l1_w_in'], arrived = mm_tn(n1, dp1, N_CHIPS, name="l1_d_w_in", job=scatter_job(scatter_of(2)))
    scattered(2, arrived)
    dn = mm_nt(dp1, full['l1_w_in'], name="l1_d_n")
    dh, dg = rms_bwd(h1, row(w['norm1_g']), dn, dh, name="l1_d_norm")
    grads['norm1_g'] = dg

    dh = s5_layer_bwd(0, dh, carry=1)
    grad_x = dh[N_META:real][None]
    grads['meta_tokens'] = dh[:N_META]

    wide = [n for n in SMALL if w[n].ndim == 3]
    wire = [grads[n].reshape(S5_GROUPS, -1) if n in wide else grads[n] for n in SMALL]
    starts = dict(zip(SMALL, _wire_plan([a.shape for a in wire]), strict=True))
    small_local = pack_small(wire, name="pack_small")
    names, pieces = core_parts(0, small_local.reshape(N_CHIPS, -1, WIRE_WIDTH))
    sum_over_cores(names, pieces, run_job(swap_job(pieces), name="reduce_cores_l0"))
    rest = layer(0) + ['small']
    from_chips.update(zip(rest, run_job(scatter_job([pair[n] for n in rest]), name="reduce_chips_l0"), strict=True))
    joined = [sum_chips(pair[n], from_chips[n], ck, name="sum_chips_" + n) for n in BIG + ['small']]
    joined = share_halves(joined, name="share_cores")
    reduced = [j.reshape(-1, j.shape[-1]) for j in joined]
    g_big = {n: reduced[i].reshape(w[n].shape) for i, n in enumerate(BIG)}
    (small_all,) = run_job(gather_by_halves_job([reduced[-1]]), name="gather_small")
    small_all = small_all.reshape(WIRE_ROWS, WIRE_WIDTH)

    out_g, out_d, out_m, out_v = {}, {}, {}, {}
    for n in BIG:
        shape = w[n].shape
        as2d = lambda a: a.reshape(-1, shape[-1])
        d, nm, nv = adamw(as2d(w[n]), as2d(g_big[n]), as2d(mom_m[n]), as2d(mom_v[n]), name="adamw_" + n)
        out_g[n], out_d[n], out_m[n], out_v[n] = g_big[n], d.reshape(shape), nm.reshape(shape), nv.reshape(shape)

    half_w = WIRE_WIDTH // 2
    shard_cells = {
        'meta_tokens': lambda k: [((0, N_META), (k * 256, (k + 1) * 256), (slice(None), slice(None)))],
        'l1_conv_w': lambda k: [(2 * j + k // 2, ((k % 2) * half_w, (k % 2 + 1) * half_w), (j, slice(None)))
                                for j in range(3)],
        'l2_b_grp': lambda k: [(j // 2, ((j % 2) * half_w + k * 128, (j % 2) * half_w + (k + 1) * 128),
                                (j, slice(None))) for j in range(len(POOL_WINDOWS))],
    }
    plain = [n for n in SMALL if n not in wide]
    res = update_small(small_all, [starts[n] for n in plain], [(w[n], mom_m[n], mom_v[n]) for n in plain],
                       [shard_cells.get(n) for n in plain], chip, name="adamw_small")
    for n, (g, d, nm, nv) in zip(plain, res, strict=True):
        out_g[n], out_d[n], out_m[n], out_v[n] = g, d, nm, nv
    as_gip = lambda n, a: gip(a) if n.endswith(('b_re', 'b_im')) else a
    g_gip = {n: small_all[starts[n]:starts[n] + S5_GROUPS].reshape(S5_GROUPS, S5_GROUP, S5_STATE) for n in wide}
    res = adamw_many([(as_gip(n, w[n]), g_gip[n], as_gip(n, mom_m[n]), as_gip(n, mom_v[n])) for n in wide],
                     name="adamw_s5_tensors")
    for n, (d, nm, nv) in zip(wide, res, strict=True):
        out_g[n], out_d[n], out_m[n], out_v[n] = as_gip(n, g_gip[n]), as_gip(n, d), as_gip(n, nm), as_gip(n, nv)

    return (loss, grad_x, *[out_g[n] for n in WEIGHTS], *[out_d[n] for n in WEIGHTS],
            *[out_m[n] for n in WEIGHTS], *[out_v[n] for n in WEIGHTS])
```

```python
import functools

import jax
import jax.numpy as jnp
from jax import lax
from jax.experimental import pallas as pl
from jax.experimental.pallas import tpu as pltpu

F32, BF16 = jnp.float32, jnp.bfloat16
MESH = pl.DeviceIdType.MESH

D_MODEL = 1024
N_META = 16
EPS = 1e-6
S5_GROUPS, S5_GROUP, S5_STATE = 64, 16, 64
LANE_BLOCK = 128
S5_BLOCKS = D_MODEL // LANE_BLOCK
GROUPS_PER_BLOCK = LANE_BLOCK // S5_GROUP
STATE_BLOCK = GROUPS_PER_BLOCK * S5_STATE
SCAN_ROWS = 256
SUBLANES = 8
CONV_E = 2048
POOL_E = 2048
POOL_WINDOWS = (2, 4, 8, 16)
POOL_GROUP = 512
POOL_HALO = 16
N_CHIPS = 4

ADAM_LR, ADAM_B1, ADAM_B2, ADAM_EPS, ADAM_WD, ADAM_STEP = 0.001, 0.9, 0.999, 1e-08, 0.01, 10

VMEM_LIMIT = 48 * 1024 * 1024
TN_OPERAND_BYTES = 28 * 1024 * 1024

WEIGHTS = ['meta_tokens', 'norm0_g', 'l0_w_in', 'l0_lam_re', 'l0_lam_im', 'l0_log_dt', 'l0_b_re', 'l0_b_im',
           'l0_c_re', 'l0_c_im', 'l0_d_skip', 'l0_w_glu', 'l0_b_glu', 'l0_w_out', 'norm1_g', 'l1_w_in',
           'l1_conv_w', 'l1_conv_b', 'l1_w_out', 'norm2_g', 'l2_w_in', 'l2_w_grp', 'l2_b_grp', 'l2_scale',
           'l2_w_out', 'norm3_g', 'l3_w_in', 'l3_lam_re', 'l3_lam_im', 'l3_log_dt', 'l3_b_re', 'l3_b_im',
           'l3_c_re', 'l3_c_im', 'l3_d_skip', 'l3_w_glu', 'l3_b_glu', 'l3_w_out', 'final_g']
BIG = ['l0_w_in', 'l0_w_glu', 'l0_w_out', 'l1_w_in', 'l1_w_out', 'l2_w_in', 'l2_w_grp', 'l2_w_out',
       'l3_w_in', 'l3_w_glu', 'l3_w_out']
SMALL_SHARDED = ['meta_tokens', 'l1_conv_w', 'l2_b_grp']
SMALL = [n for n in WEIGHTS if n not in BIG]


def _params(*sem):
    return pltpu.CompilerParams(dimension_semantics=sem, vmem_limit_bytes=VMEM_LIMIT)


class Job:
    def __init__(self, arrays, out_shapes, sems, fn, in_place=False):
        self.arrays, self.out_shapes, self.sems, self.fn = list(arrays), list(out_shapes), list(sems), fn
        self.in_place = in_place
        assert len(self.arrays) == len(self.out_shapes)

    @staticmethod
    def both(first, second):
        na, ns = len(first.arrays), len(first.sems)

        def fn(ins, outs, sems, phase):
            first.fn(ins[:na], outs[:na], sems[:ns], phase)
            second.fn(ins[na:], outs[na:], sems[ns:], phase)

        job = Job(first.arrays + second.arrays, first.out_shapes + second.out_shapes, first.sems + second.sems, fn)
        job.in_place = [first.in_place] * na + [second.in_place] * len(second.arrays)
        return job

    def aliased(self):
        flags = self.in_place if isinstance(self.in_place, list) else [self.in_place] * len(self.arrays)
        return [a for a, flag in enumerate(flags) if flag]


def _call(body, *, name, grid, in_specs, out_specs, out_shape, args, semantics, scratch=(), aliases=None, job=None):
    if job is None:
        return pl.pallas_call(
            body, name=name, grid=grid, in_specs=list(in_specs), out_specs=list(out_specs),
            out_shape=list(out_shape), scratch_shapes=list(scratch), input_output_aliases=aliases or {},
            compiler_params=_params(*semantics))(*args)
    counts = [len(in_specs), len(job.arrays), len(out_specs), len(job.out_shapes), len(scratch), len(job.sems)]

    def hosted(*refs):
        parts, pos = [], 0
        for k in counts:
            parts.append(refs[pos:pos + k])
            pos += k
        ins, job_ins, outs, job_outs, scr, job_sems = parts
        steps = [pl.program_id(d) for d in range(len(grid))]
        first = functools.reduce(jnp.logical_and, [s == 0 for s in steps])
        last = functools.reduce(jnp.logical_and, [s == g - 1 for s, g in zip(steps, grid, strict=True)])

        @pl.when(first)
        def _():
            job.fn(job_ins, job_outs, job_sems, "start")

        body(*ins, *outs, *scr)

        @pl.when(last)
        def _():
            job.fn(job_ins, job_outs, job_sems, "finish")

    any_spec = pl.BlockSpec(memory_space=pl.ANY)
    aliases = dict(aliases or {})
    aliases.update({len(in_specs) + a: len(out_specs) + a for a in job.aliased()})
    res = pl.pallas_call(
        hosted, name=name, grid=grid, in_specs=list(in_specs) + [any_spec] * len(job.arrays),
        out_specs=list(out_specs) + [any_spec] * len(job.out_shapes),
        out_shape=list(out_shape) + job.out_shapes, scratch_shapes=list(scratch) + job.sems,
        input_output_aliases=aliases,
        compiler_params=_params(*["arbitrary"] * len(grid)))(*args, *job.arrays)
    return res[:len(out_specs)], res[len(out_specs):]


def _tile(n, cap, mult):
    best = None
    for t in range(mult, min(n, cap) + 1, mult):
        if n % t == 0:
            best = t
    assert best is not None, (n, cap, mult)
    return best


def _with_job(res, job):
    return res[0] if job is None else (res[0][0], res[1])


def mm_nn(a, b, *, name, bias=None, add=None, out_dtype=F32, job=None):
    m, k = a.shape
    s, _, ns = b.shape
    n = s * ns
    tm, tn = _tile(m, 1088, 16), _tile(ns, 1024, 128)
    per = ns // tn

    def body(*refs):
        a_ref, b_ref = refs[0], refs[1]
        o_ref = refs[-1]
        acc = jnp.dot(a_ref[...].astype(BF16), b_ref[...], preferred_element_type=F32)
        pos = 2
        if bias is not None:
            acc = acc + refs[pos][...]
            pos += 1
        if add is not None:
            acc = acc + refs[pos][...]
        o_ref[...] = acc.astype(o_ref.dtype)

    in_specs = [pl.BlockSpec((tm, k), lambda i, j: (i, 0)),
                pl.BlockSpec((None, k, tn), lambda i, j: (j // per, 0, j % per))]
    args = [a, b]
    if bias is not None:
        in_specs.append(pl.BlockSpec((1, tn), lambda i, j: (0, j)))
        args.append(bias)
    if add is not None:
        in_specs.append(pl.BlockSpec((tm, tn), lambda i, j: (i, j)))
        args.append(add)
    return _with_job(_call(
        body, name=name, grid=(m // tm, n // tn), in_specs=in_specs,
        out_specs=[pl.BlockSpec((tm, tn), lambda i, j: (i, j))],
        out_shape=[jax.ShapeDtypeStruct((m, n), out_dtype)], args=args,
        semantics=("parallel", "parallel"), job=job), job)


def norm_mm_nn(h, gain, b, *, name, job=None):
    m, k = h.shape
    s, _, ns = b.shape
    n = s * ns
    tm, tn = _tile(m, 1088, 16), _tile(ns, 1024, 128)
    per = ns // tn

    def body(h_ref, g_ref, b_ref, o_ref, n_ref):
        @pl.when(pl.program_id(1) == 0)
        def _():
            n_ref[...] = _rms(h_ref[...], g_ref[...]).astype(n_ref.dtype)

        o_ref[...] = jnp.dot(n_ref[...], b_ref[...], preferred_element_type=F32)

    res = _call(
        body, name=name, grid=(m // tm, n // tn),
        in_specs=[pl.BlockSpec((tm, k), lambda i, j: (i, 0)), pl.BlockSpec((1, k), lambda i, j: (0, 0)),
                  pl.BlockSpec((None, k, tn), lambda i, j: (j // per, 0, j % per))],
        out_specs=[pl.BlockSpec((tm, tn), lambda i, j: (i, j)), pl.BlockSpec((tm, k), lambda i, j: (i, 0))],
        out_shape=[jax.ShapeDtypeStruct((m, n), F32), jax.ShapeDtypeStruct((m, k), BF16)], args=(h, gain, b),
        semantics=("parallel", "arbitrary"), job=job)
    return (res[0], res[1]) if job is None else ((res[0][0], res[0][1]), res[1])


def mm_nt_norm_bwd(g, w, h, gain, dh_out, *, name):
    m, kc = g.shape
    s, nout, kcs = w.shape
    assert s * kcs == kc
    tm, tk = _tile(m, 1088, 16), _tile(kcs, 1024, 128)
    per = kcs // tk
    nk = kc // tk

    def body(g_ref, w_ref, h_ref, gain_ref, dh_ref, o_ref, dg_ref):
        i, kk = pl.program_id(0), pl.program_id(1)
        part = lax.dot_general(g_ref[...].astype(BF16), w_ref[...], (((1,), (1,)), ((), ())),
                               preferred_element_type=F32)

        @pl.when(kk == 0)
        def _():
            o_ref[...] = part

        @pl.when(kk > 0)
        def _():
            o_ref[...] += part

        @pl.when(kk == nk - 1)
        def _():
            dh, dg = jax.vjp(_rms, h_ref[...], gain_ref[...])[1](o_ref[...])
            o_ref[...] = dh_ref[...] + dh

            @pl.when(i == 0)
            def _():
                dg_ref[...] = dg

            @pl.when(i > 0)
            def _():
                dg_ref[...] += dg

    row_tile = pl.BlockSpec((tm, nout), lambda i, kk: (i, 0))
    return pl.pallas_call(
        body, name=name, grid=(m // tm, nk),
        in_specs=[pl.BlockSpec((tm, tk), lambda i, kk: (i, kk)),
                  pl.BlockSpec((None, nout, tk), lambda i, kk: (kk // per, 0, kk % per)),
                  row_tile, pl.BlockSpec((1, nout), lambda i, kk: (0, 0)), row_tile],
        out_specs=[row_tile, pl.BlockSpec((1, nout), lambda i, kk: (0, 0))],
        out_shape=[jax.ShapeDtypeStruct((m, nout), F32), jax.ShapeDtypeStruct((1, nout), F32)],
        compiler_params=_params("arbitrary", "arbitrary"))(g, w, h, gain, dh_out)


def mm_nt(g, w, *, name, job=None):
    m, kc = g.shape
    s, nout, kcs = w.shape
    assert s * kcs == kc
    tm, tno, tk = _tile(m, 2176, 16), _tile(nout, 1024, 128), _tile(kcs, 1024, 128)
    per = kcs // tk

    def body(g_ref, w_ref, o_ref):
        kk = pl.program_id(2)
        part = lax.dot_general(g_ref[...].astype(BF16), w_ref[...], (((1,), (1,)), ((), ())),
                               preferred_element_type=F32)

        @pl.when(kk == 0)
        def _():
            o_ref[...] = part

        @pl.when(kk > 0)
        def _():
            o_ref[...] += part

    return _with_job(_call(
        body, name=name, grid=(m // tm, nout // tno, kc // tk),
        in_specs=[pl.BlockSpec((tm, tk), lambda i, j, kk: (i, kk)),
                  pl.BlockSpec((None, tno, tk), lambda i, j, kk: (kk // per, j, kk % per))],
        out_specs=[pl.BlockSpec((tm, tno), lambda i, j, kk: (i, j))],
        out_shape=[jax.ShapeDtypeStruct((m, nout), F32)], args=(g, w),
        semantics=("parallel", "parallel", "arbitrary"), job=job), job)


def mm_tn(a, g, shards, *, name, job=None):
    m, ka = a.shape
    n = g.shape[1]
    ns = n // shards
    tka, tn = _tile(ka, 1024, 128), _tile(ns, 512, 128)
    row_bytes = tka * jnp.dtype(a.dtype).itemsize + tn * jnp.dtype(g.dtype).itemsize
    tm = _tile(m, TN_OPERAND_BYTES // (2 * row_bytes), 16)
    per = ns // tn

    def body(a_ref, g_ref, o_ref):
        mm = pl.program_id(2)
        part = lax.dot_general(a_ref[...].astype(BF16), g_ref[...].astype(BF16), (((0,), (0,)), ((), ())),
                               preferred_element_type=F32)

        @pl.when(mm == 0)
        def _():
            o_ref[...] = part

        @pl.when(mm > 0)
        def _():
            o_ref[...] += part

    return _with_job(_call(
        body, name=name, grid=(ka // tka, n // tn, m // tm),
        in_specs=[pl.BlockSpec((tm, tka), lambda i, j, mm: (mm, i)),
                  pl.BlockSpec((tm, tn), lambda i, j, mm: (mm, j))],
        out_specs=[pl.BlockSpec((None, tka, tn), lambda i, j, mm: (j // per, i, j % per))],
        out_shape=[jax.ShapeDtypeStruct((shards, ka, ns), F32)], args=(a, g),
        semantics=("parallel", "parallel", "arbitrary"), job=job), job)


def grp_nn(a, w, bias, *, name):
    m = a.shape[0]
    tm = _tile(m, 1088, 16)
    ng = len(POOL_WINDOWS)

    def body(a_ref, w_ref, b_ref, o_ref):
        wj = w_ref[...].reshape(POOL_GROUP, POOL_GROUP)
        o_ref[...] = jnp.dot(a_ref[...].astype(BF16), wj, preferred_element_type=F32) + b_ref[...]

    return pl.pallas_call(
        body, name=name, grid=(m // tm, ng),
        in_specs=[pl.BlockSpec((tm, POOL_GROUP), lambda i, j: (i, j)),
                  pl.BlockSpec((N_CHIPS, None, POOL_GROUP // N_CHIPS, POOL_GROUP), lambda i, j: (0, j, 0, 0)),
                  pl.BlockSpec((1, POOL_GROUP), lambda i, j: (0, j))],
        out_specs=pl.BlockSpec((tm, POOL_GROUP), lambda i, j: (i, j)),
        out_shape=jax.ShapeDtypeStruct((m, ng * POOL_GROUP), F32),
        compiler_params=_params("parallel", "parallel"))(a, w, bias)


def grp_nt(g, w, *, name):
    m = g.shape[0]
    tm = _tile(m, 1088, 16)
    ng = len(POOL_WINDOWS)

    def body(g_ref, w_ref, o_ref):
        wj = w_ref[...].reshape(POOL_GROUP, POOL_GROUP)
        o_ref[...] = lax.dot_general(g_ref[...].astype(BF16), wj, (((1,), (1,)), ((), ())),
                                     preferred_element_type=F32)

    return pl.pallas_call(
        body, name=name, grid=(m // tm, ng),
        in_specs=[pl.BlockSpec((tm, POOL_GROUP), lambda i, j: (i, j)),
                  pl.BlockSpec((N_CHIPS, None, POOL_GROUP // N_CHIPS, POOL_GROUP), lambda i, j: (0, j, 0, 0))],
        out_specs=pl.BlockSpec((tm, POOL_GROUP), lambda i, j: (i, j)),
        out_shape=jax.ShapeDtypeStruct((m, ng * POOL_GROUP), F32),
        compiler_params=_params("parallel", "parallel"))(g, w)


def grp_tn(a, g, *, name):
    m = a.shape[0]
    tm = _tile(m, 1088, 16)
    ng = len(POOL_WINDOWS)
    rows = POOL_GROUP // N_CHIPS

    def body(a_ref, g_ref, o_ref):
        mm = pl.program_id(1)
        part = lax.dot_general(a_ref[...].astype(BF16), g_ref[...].astype(BF16), (((0,), (0,)), ((), ())),
                               preferred_element_type=F32).reshape(N_CHIPS, rows, POOL_GROUP)

        @pl.when(mm == 0)
        def _():
            o_ref[...] = part

        @pl.when(mm > 0)
        def _():
            o_ref[...] += part

    return pl.pallas_call(
        body, name=name, grid=(ng, m // tm),
        in_specs=[pl.BlockSpec((tm, POOL_GROUP), lambda j, mm: (mm, j)),
                  pl.BlockSpec((tm, POOL_GROUP), lambda j, mm: (mm, j))],
        out_specs=pl.BlockSpec((N_CHIPS, None, rows, POOL_GROUP), lambda j, mm: (0, j, 0, 0)),
        out_shape=jax.ShapeDtypeStruct((N_CHIPS, ng, rows, POOL_GROUP), F32),
        compiler_params=_params("parallel", "arbitrary"))(a, g)


def rowwise(fn, rows, bcast, outs, accs=(), *, name, aliases=None):
    m = rows[0][0].shape[0]
    tr = m // 16
    n_in = len(rows) + len(bcast)

    def body(*refs):
        vals = [r[...] for r in refs[:n_in]]
        o_vals, a_vals = fn(*vals)
        for ref, val in zip(refs[n_in:n_in + len(outs)], o_vals, strict=True):
            ref[...] = val.astype(ref.dtype)
        step = pl.program_id(0)
        for ref, val in zip(refs[n_in + len(outs):], a_vals, strict=True):
            @pl.when(step == 0)
            def _(ref=ref, val=val):
                ref[...] = val

            @pl.when(step > 0)
            def _(ref=ref, val=val):
                ref[...] += val

    in_specs = [pl.BlockSpec((tr, w), functools.partial(lambda i, cb: (i, cb), cb=cb)) for _, w, cb in rows]
    in_specs += [pl.BlockSpec(b.shape, lambda i: (0, 0)) for b in bcast]
    out_specs = [pl.BlockSpec((tr, w), functools.partial(lambda i, cb: (i, cb), cb=cb)) for _, _, w, cb in outs]
    out_specs += [pl.BlockSpec((1, w), lambda i: (0, 0)) for w in accs]
    out_shape = [jax.ShapeDtypeStruct((m, cols), dt) for cols, dt, _, _ in outs]
    out_shape += [jax.ShapeDtypeStruct((1, w), F32) for w in accs]
    res = pl.pallas_call(
        body, name=name, grid=(m // tr,), in_specs=in_specs, out_specs=out_specs, out_shape=out_shape,
        input_output_aliases=aliases or {},
        compiler_params=_params("arbitrary"))(*[r[0] for r in rows], *bcast)
    return res[:len(outs)], res[len(outs):]


def _rms(h, g):
    return h * lax.rsqrt(jnp.mean(h * h, axis=-1, keepdims=True) + EPS) * g


def _colsum(v):
    return jnp.sum(v, axis=0, keepdims=True)


def _s5_gate(y0, q, z):
    yg = jax.nn.gelu(y0)
    return yg * jax.nn.sigmoid(q) * jax.nn.silu(z)


def _pool_gate(o, z, scale):
    return o * scale * jax.nn.silu(z)


def _shift_rows(v, d, down):
    t = v.shape[0]
    row = lax.broadcasted_iota(jnp.int32, v.shape, 0)
    if down:
        return jnp.where(row >= d, pltpu.roll(v, d, 0), 0.0)
    return jnp.where(row < t - d, pltpu.roll(v, t - d, 0), 0.0)


def _cmul(ar, ai, br, bi):
    return ar * br - ai * bi, ar * bi + ai * br


def _scan(xr, xi, ar, ai, cr, ci, down):
    t, n = xr.shape
    nb = t // SUBLANES
    pw = [(ar, ai)]
    for _ in range(SUBLANES - 1):
        pw.append(_cmul(*pw[-1], ar, ai))
    sub = lax.broadcasted_iota(jnp.int32, (SUBLANES, n), 0)

    def table(exps):
        tr, ti = jnp.zeros((SUBLANES, n), F32), jnp.zeros((SUBLANES, n), F32)
        for r, e in enumerate(exps):
            if e:
                tr, ti = jnp.where(sub == r, pw[e - 1][0], tr), jnp.where(sub == r, pw[e - 1][1], ti)
        return tr[None], ti[None]

    sr, si = xr.reshape(nb, SUBLANES, n), xi.reshape(nb, SUBLANES, n)
    d = 1
    while d < SUBLANES:
        mr, mi = table([d if (r >= d if down else r < SUBLANES - d) else 0 for r in range(SUBLANES)])
        turn = d if down else SUBLANES - d
        hr, hi = pltpu.roll(sr, turn, 1), pltpu.roll(si, turn, 1)
        sr, si = sr + mr * hr - mi * hi, si + mr * hi + mi * hr
        d *= 2
    edge = SUBLANES - 1 if down else 0
    qr, qi = table([r + 1 if down else SUBLANES - r for r in range(SUBLANES)])
    qr, qi = qr[0], qi[0]
    in_r, in_i = jnp.broadcast_to(cr, (SUBLANES, n)), jnp.broadcast_to(ci, (SUBLANES, n))
    out_r, out_i = [None] * nb, [None] * nb
    for b in (range(nb) if down else reversed(range(nb))):
        out_r[b] = sr[b] + qr * in_r - qi * in_i
        out_i[b] = si[b] + qr * in_i + qi * in_r
        in_r = jnp.broadcast_to(out_r[b][edge:edge + 1, :], (SUBLANES, n))
        in_i = jnp.broadcast_to(out_i[b][edge:edge + 1, :], (SUBLANES, n))
    return jnp.concatenate(out_r, axis=0), jnp.concatenate(out_i, axis=0), in_r[0:1, :], in_i[0:1, :]


def s5_fwd(uz, bblk, cblk, ar, ai, dskip, *, name, job=None):
    lp = uz.shape[0]
    t = SCAN_ROWS
    nst = S5_GROUPS * S5_STATE

    def body(u_ref, b_ref, c_ref, ar_ref, ai_ref, d_ref, y_ref, yg_ref, sr_ref, si_ref, cr, ci):
        step = pl.program_id(1)

        @pl.when(step == 0)
        def _():
            cr[...] = jnp.zeros_like(cr)
            ci[...] = jnp.zeros_like(ci)

        u = u_ref[...]
        a_r, a_i = ar_ref[...], ai_ref[...]
        x = jnp.dot(u.astype(BF16), b_ref[...].astype(BF16), preferred_element_type=F32)
        sr, si, cr[...], ci[...] = _scan(x[:, :STATE_BLOCK], x[:, STATE_BLOCK:], a_r, a_i, cr[...], ci[...], True)
        sr_ref[...] = sr
        si_ref[...] = si
        s = jnp.concatenate([sr, si], axis=1).astype(BF16)
        y0 = lax.dot_general(s, c_ref[...].astype(BF16), (((1,), (1,)), ((), ())),
                             preferred_element_type=F32) + d_ref[...] * u
        y_ref[...] = y0
        yg_ref[...] = jax.nn.gelu(y0).astype(yg_ref.dtype)

    return _call(
        body, name=name, grid=(S5_BLOCKS, lp // t),
        in_specs=[pl.BlockSpec((t, LANE_BLOCK), lambda b, c: (c, b)),
                  pl.BlockSpec((None, LANE_BLOCK, 2 * STATE_BLOCK), lambda b, c: (b, 0, 0)),
                  pl.BlockSpec((None, LANE_BLOCK, 2 * STATE_BLOCK), lambda b, c: (b, 0, 0)),
                  pl.BlockSpec((1, STATE_BLOCK), lambda b, c: (0, b)),
                  pl.BlockSpec((1, STATE_BLOCK), lambda b, c: (0, b)),
                  pl.BlockSpec((1, LANE_BLOCK), lambda b, c: (0, b))],
        out_specs=[pl.BlockSpec((t, LANE_BLOCK), lambda b, c: (c, b)),
                   pl.BlockSpec((t, LANE_BLOCK), lambda b, c: (c, b)),
                   pl.BlockSpec((t, STATE_BLOCK), lambda b, c: (c, b)),
                   pl.BlockSpec((t, STATE_BLOCK), lambda b, c: (c, b))],
        out_shape=[jax.ShapeDtypeStruct((lp, D_MODEL), F32), jax.ShapeDtypeStruct((lp, D_MODEL), BF16),
                   jax.ShapeDtypeStruct((lp, nst), F32), jax.ShapeDtypeStruct((lp, nst), F32)],
        scratch=[pltpu.VMEM((1, STATE_BLOCK), F32), pltpu.VMEM((1, STATE_BLOCK), F32)],
        args=(uz, bblk, cblk, ar, ai, dskip), semantics=("parallel", "arbitrary"), job=job)


def s5_bwd(dy_direct, dyg, y0, uz, sr, si, bblk, cblk, ar, ai, dskip, dp, *, name, job=None):
    lp = uz.shape[0]
    t = SCAN_ROWS
    nch = lp // t
    nst = S5_GROUPS * S5_STATE

    def body(dyd_ref, dyg_ref, y0_ref, u_ref, sr_ref, si_ref, b_ref, c_ref, ar_ref, ai_ref, d_ref, _, du_ref,
             db_ref, dc_ref, dar_ref, dai_ref, dd_ref, cr, ci):
        step = pl.program_id(1)

        @pl.when(step == 0)
        def _():
            cr[...] = jnp.zeros_like(cr)
            ci[...] = jnp.zeros_like(ci)

        dy = dyd_ref[...] + jax.vjp(jax.nn.gelu, y0_ref[...])[1](dyg_ref[...])[0]
        u = u_ref[...]
        dyb, ub = dy.astype(BF16), u.astype(BF16)
        a_r, a_i = ar_ref[...], -ai_ref[...]
        g = jnp.dot(dyb, c_ref[...].astype(BF16), preferred_element_type=F32)
        gr, gi = g[:, :STATE_BLOCK], g[:, STATE_BLOCK:]
        nxt_r, nxt_i = cr[...], ci[...]
        last = lax.broadcasted_iota(jnp.int32, gr.shape, 0) == t - 1
        lr, li, cr[...], ci[...] = _scan(gr, gi, a_r, a_i, nxt_r, nxt_i, False)
        nr = _shift_rows(lr, 1, False) + jnp.where(last, nxt_r, 0.0)
        ni = _shift_rows(li, 1, False) + jnp.where(last, nxt_i, 0.0)
        s_r, s_i = sr_ref[...], si_ref[...]
        dar = _colsum(s_r * nr + s_i * ni)
        dai = _colsum(s_r * ni - s_i * nr)
        lam = jnp.concatenate([lr, li], axis=1).astype(BF16)
        sb = jnp.concatenate([s_r, s_i], axis=1).astype(BF16)
        dc = lax.dot_general(dyb, sb, (((0,), (0,)), ((), ())), preferred_element_type=F32)
        db = lax.dot_general(ub, lam, (((0,), (0,)), ((), ())), preferred_element_type=F32)
        du_ref[...] = (lax.dot_general(lam, b_ref[...].astype(BF16), (((1,), (1,)), ((), ())),
                                       preferred_element_type=F32) + d_ref[...] * dy).astype(du_ref.dtype)
        dd = _colsum(dy * u)

        @pl.when(step == 0)
        def _():
            db_ref[...] = db
            dc_ref[...] = dc
            dar_ref[...] = dar
            dai_ref[...] = dai
            dd_ref[...] = dd

        @pl.when(step > 0)
        def _():
            db_ref[...] += db
            dc_ref[...] += dc
            dar_ref[...] += dar
            dai_ref[...] += dai
            dd_ref[...] += dd

    rev = lambda b, c: (nch - 1 - c, b)
    return _call(
        body, name=name, grid=(S5_BLOCKS, nch),
        in_specs=[pl.BlockSpec((t, LANE_BLOCK), rev), pl.BlockSpec((t, LANE_BLOCK), rev),
                  pl.BlockSpec((t, LANE_BLOCK), rev), pl.BlockSpec((t, LANE_BLOCK), rev),
                  pl.BlockSpec((t, STATE_BLOCK), rev), pl.BlockSpec((t, STATE_BLOCK), rev),
                  pl.BlockSpec((None, LANE_BLOCK, 2 * STATE_BLOCK), lambda b, c: (b, 0, 0)),
                  pl.BlockSpec((None, LANE_BLOCK, 2 * STATE_BLOCK), lambda b, c: (b, 0, 0)),
                  pl.BlockSpec((1, STATE_BLOCK), lambda b, c: (0, b)),
                  pl.BlockSpec((1, STATE_BLOCK), lambda b, c: (0, b)),
                  pl.BlockSpec((1, LANE_BLOCK), lambda b, c: (0, b)),
                  pl.BlockSpec(memory_space=pl.ANY)],
        out_specs=[pl.BlockSpec((t, LANE_BLOCK), rev),
                   pl.BlockSpec((None, LANE_BLOCK, 2 * STATE_BLOCK), lambda b, c: (b, 0, 0)),
                   pl.BlockSpec((None, LANE_BLOCK, 2 * STATE_BLOCK), lambda b, c: (b, 0, 0)),
                   pl.BlockSpec((1, STATE_BLOCK), lambda b, c: (0, b)),
                   pl.BlockSpec((1, STATE_BLOCK), lambda b, c: (0, b)),
                   pl.BlockSpec((1, LANE_BLOCK), lambda b, c: (0, b))],
        out_shape=[jax.ShapeDtypeStruct(dp.shape, dp.dtype),
                   jax.ShapeDtypeStruct((S5_BLOCKS, LANE_BLOCK, 2 * STATE_BLOCK), F32),
                   jax.ShapeDtypeStruct((S5_BLOCKS, LANE_BLOCK, 2 * STATE_BLOCK), F32),
                   jax.ShapeDtypeStruct((1, nst), F32), jax.ShapeDtypeStruct((1, nst), F32),
                   jax.ShapeDtypeStruct((1, D_MODEL), F32)],
        scratch=[pltpu.VMEM((1, STATE_BLOCK), F32), pltpu.VMEM((1, STATE_BLOCK), F32)],
        aliases={11: 0}, args=(dy_direct, dyg, y0, uz, sr, si, bblk, cblk, ar, ai, dskip, dp),
        semantics=("parallel", "arbitrary"), job=job)


def _s5_prep(lam_re, lam_im, log_dt, b_re, b_im, c_re, c_im):
    dt = jnp.exp(log_dt)
    mag = jnp.exp(lam_re * dt)
    ar = mag * jnp.cos(lam_im * dt)
    ai = mag * jnp.sin(lam_im * dt)
    den = lam_re * lam_re + lam_im * lam_im
    kr = ((ar - 1.0) * lam_re + ai * lam_im) / den
    ki = (ai * lam_re - (ar - 1.0) * lam_im) / den
    bbr = kr[:, None, :] * b_re - ki[:, None, :] * b_im
    bbi = kr[:, None, :] * b_im + ki[:, None, :] * b_re
    rows = S5_GROUPS * S5_GROUP
    expand = (lax.broadcasted_iota(jnp.int32, (S5_STATE, STATE_BLOCK), 1) % S5_STATE
              == lax.broadcasted_iota(jnp.int32, (S5_STATE, STATE_BLOCK), 0)).astype(F32)
    own = ((lax.broadcasted_iota(jnp.int32, (rows, STATE_BLOCK), 0) // S5_GROUP) % GROUPS_PER_BLOCK
           == lax.broadcasted_iota(jnp.int32, (rows, STATE_BLOCK), 1) // S5_STATE).astype(F32)

    def blocks(v):
        wide = jnp.dot(v.reshape(rows, S5_STATE), expand, precision=lax.Precision.HIGHEST,
                       preferred_element_type=F32)
        return (wide * own).reshape(S5_BLOCKS, LANE_BLOCK, STATE_BLOCK)

    btab = jnp.concatenate([blocks(bbr), blocks(bbi)], axis=2)
    ctab = jnp.concatenate([blocks(c_re), -blocks(c_im)], axis=2)
    return ar, ai, btab, ctab


_S5_PREP_OUT = [(S5_GROUPS, S5_STATE), (S5_GROUPS, S5_STATE), (S5_BLOCKS, LANE_BLOCK, 2 * STATE_BLOCK),
                (S5_BLOCKS, LANE_BLOCK, 2 * STATE_BLOCK)]


def s5_prep(params, *, name):
    def body(*refs):
        for ref, val in zip(refs[7:], _s5_prep(*[r[...] for r in refs[:7]]), strict=True):
            ref[...] = val

    return pl.pallas_call(body, name=name, out_shape=[jax.ShapeDtypeStruct(s, F32) for s in _S5_PREP_OUT],
                          compiler_params=pltpu.CompilerParams(vmem_limit_bytes=VMEM_LIMIT))(*params)


def s5_prep_bwd(params, cotangents, *, name):
    def body(*refs):
        grads = jax.vjp(_s5_prep, *[r[...] for r in refs[:7]])[1](tuple(r[...] for r in refs[7:11]))
        for ref, val in zip(refs[11:], grads, strict=True):
            ref[...] = val

    return pl.pallas_call(body, name=name, out_shape=[jax.ShapeDtypeStruct(p.shape, F32) for p in params],
                          compiler_params=pltpu.CompilerParams(vmem_limit_bytes=VMEM_LIMIT))(*params, *cotangents)


def _silu_grad(z):
    s = jax.nn.sigmoid(z)
    return s * (1.0 + z * (1.0 - s))


def conv_fwd(p, conv_w, conv_b, *, name):
    lp = p.shape[0]
    tr = lp // 16
    e = CONV_E
    hb = tr // 8

    def body(p_ref, cgh_ref, vh_ref, w_ref, b_ref, o_ref):
        i = pl.program_id(0)
        bg, cg, v, z = (p_ref[:, k * e:(k + 1) * e] for k in range(4))
        hc = cg * v
        halo = jnp.where(i > 0, cgh_ref[...] * vh_ref[...], 0.0)
        ext = jnp.concatenate([halo, hc], axis=0)
        w = w_ref[...]
        conv = (w[0:1] * pltpu.roll(ext, 2, 0)[8:] + w[1:2] * pltpu.roll(ext, 1, 0)[8:] + w[2:3] * hc
                + b_ref[...])
        o_ref[...] = (bg * conv * jax.nn.silu(z)).astype(o_ref.dtype)

    prev = lambda col: (lambda i: (jnp.maximum(i * hb - 1, 0), col))
    return pl.pallas_call(
        body, name=name, grid=(lp // tr,),
        in_specs=[pl.BlockSpec((tr, 4 * e), lambda i: (i, 0)),
                  pl.BlockSpec((8, e), prev(1)), pl.BlockSpec((8, e), prev(2)),
                  pl.BlockSpec((3, e), lambda i: (0, 0)), pl.BlockSpec((1, e), lambda i: (0, 0))],
        out_specs=pl.BlockSpec((tr, e), lambda i: (i, 0)),
        out_shape=jax.ShapeDtypeStruct((lp, e), BF16),
        compiler_params=_params("parallel"))(p, p, p, conv_w, conv_b)


def conv_bwd(dy3, p, conv_w, conv_b, *, name):
    lp = p.shape[0]
    tr = lp // 16
    e = CONV_E
    hb = tr // 8
    nt = lp // tr
    width = 512

    def body(dy_ref, p_ref, cgh_ref, vh_ref, dyn_ref, bgn_ref, zn_ref, w_ref, b_ref, dp_ref, dw_ref, db_ref):
        i = pl.program_id(0)
        for c0 in range(0, e, width):
            cols = slice(c0, c0 + width)
            dy = dy_ref[:, cols]
            bg, cg, v, z = (p_ref[:, k * e + c0:k * e + c0 + width] for k in range(4))
            w = w_ref[:, cols]
            hc = cg * v
            halo = jnp.where(i > 0, cgh_ref[:, cols] * vh_ref[:, cols], 0.0)
            ext = jnp.concatenate([halo, hc], axis=0)
            hc2, hc1 = pltpu.roll(ext, 2, 0)[8:], pltpu.roll(ext, 1, 0)[8:]
            yc = w[0:1] * hc2 + w[1:2] * hc1 + w[2:3] * hc + b_ref[:, cols]
            sz = jax.nn.silu(z)
            dyc = dy * bg * sz
            dyc_n = jnp.where(i < nt - 1, dyn_ref[:, cols] * bgn_ref[:, cols] * jax.nn.silu(zn_ref[:, cols]), 0.0)
            ext_n = jnp.concatenate([dyc, dyc_n], axis=0)
            n_rows = tr + 8
            dhc = (w[2:3] * dyc + w[1:2] * pltpu.roll(ext_n, n_rows - 1, 0)[:tr]
                   + w[0:1] * pltpu.roll(ext_n, n_rows - 2, 0)[:tr])
            for k, val in enumerate((dy * yc * sz, dhc * v, dhc * cg, dy * bg * yc * _silu_grad(z))):
                dp_ref[:, k * e + c0:k * e + c0 + width] = val.astype(dp_ref.dtype)
            dw = jnp.concatenate([_colsum(dyc * hc2), _colsum(dyc * hc1), _colsum(dyc * hc)], axis=0)
            db = _colsum(dyc)

            @pl.when(i == 0)
            def _(cols=cols, dw=dw, db=db):
                dw_ref[:, cols] = dw
                db_ref[:, cols] = db

            @pl.when(i > 0)
            def _(cols=cols, dw=dw, db=db):
                dw_ref[:, cols] += dw
                db_ref[:, cols] += db

    prev = lambda col: (lambda i: (jnp.maximum(i * hb - 1, 0), col))
    nxt = lambda col: (lambda i: (jnp.minimum((i + 1) * hb, lp // 8 - 1), col))
    return pl.pallas_call(
        body, name=name, grid=(nt,),
        in_specs=[pl.BlockSpec((tr, e), lambda i: (i, 0)), pl.BlockSpec((tr, 4 * e), lambda i: (i, 0)),
                  pl.BlockSpec((8, e), prev(1)), pl.BlockSpec((8, e), prev(2)),
                  pl.BlockSpec((8, e), nxt(0)), pl.BlockSpec((8, e), nxt(0)), pl.BlockSpec((8, e), nxt(3)),
                  pl.BlockSpec((3, e), lambda i: (0, 0)), pl.BlockSpec((1, e), lambda i: (0, 0))],
        out_specs=[pl.BlockSpec((tr, 4 * e), lambda i: (i, 0)), pl.BlockSpec((3, e), lambda i: (0, 0)),
                   pl.BlockSpec((1, e), lambda i: (0, 0))],
        out_shape=[jax.ShapeDtypeStruct((lp, 4 * e), BF16), jax.ShapeDtypeStruct((3, e), F32),
                   jax.ShapeDtypeStruct((1, e), F32)],
        compiler_params=_params("arbitrary"))(dy3, p, p, p, dy3, p, p, conv_w, conv_b)


def _pool_counts(i, tr, rows, offset, w):
    t = (i * tr + offset + 1 + lax.broadcasted_iota(jnp.int32, (rows, 1), 0)).astype(F32)
    return jnp.minimum(t, float(w))


def pool_fwd(p, *, name):
    lp = p.shape[0]
    tr = lp // 16
    h = POOL_HALO
    hb = tr // h

    def body(u_ref, uh_ref, o_ref):
        i = pl.program_id(0)
        u = u_ref[...]
        ext = jnp.concatenate([jnp.where(i > 0, uh_ref[...], 0.0), u], axis=0)
        for k, w in enumerate(POOL_WINDOWS):
            cols = slice(k * POOL_GROUP, (k + 1) * POOL_GROUP)
            s = ext[:, cols]
            d = 1
            while d < w:
                s = s + pltpu.roll(s, d, 0)
                d *= 2
            o_ref[:, cols] = (s[h:] / _pool_counts(i, tr, tr, 0, w) - u[:, cols]).astype(o_ref.dtype)

    return pl.pallas_call(
        body, name=name, grid=(lp // tr,),
        in_specs=[pl.BlockSpec((tr, POOL_E), lambda i: (i, 0)),
                  pl.BlockSpec((h, POOL_E), lambda i: (jnp.maximum(i * hb - 1, 0), 0))],
        out_specs=pl.BlockSpec((tr, POOL_E), lambda i: (i, 0)),
        out_shape=jax.ShapeDtypeStruct((lp, POOL_E), BF16),
        compiler_params=_params("parallel"))(p, p)


def pool_bwd(dmix, dp, *, name):
    lp = dmix.shape[0]
    tr = lp // 16
    h = POOL_HALO
    hb = tr // h
    nt = lp // tr

    def body(dm_ref, dmn_ref, _, du_ref):
        i = pl.program_id(0)
        dm = dm_ref[...]
        dmn = jnp.where(i < nt - 1, dmn_ref[...], 0.0)
        n_rows = tr + h
        for k, w in enumerate(POOL_WINDOWS):
            cols = slice(k * POOL_GROUP, (k + 1) * POOL_GROUP)
            s = jnp.concatenate([dm[:, cols] / _pool_counts(i, tr, tr, 0, w),
                                 dmn[:, cols] / _pool_counts(i, tr, h, tr, w)], axis=0)
            d = 1
            while d < w:
                s = s + pltpu.roll(s, n_rows - d, 0)
                d *= 2
            du_ref[:, cols] = (s[:tr] - dm[:, cols]).astype(du_ref.dtype)

    return pl.pallas_call(
        body, name=name, grid=(nt,),
        in_specs=[pl.BlockSpec((tr, POOL_E), lambda i: (i, 0)),
                  pl.BlockSpec((h, POOL_E), lambda i: (jnp.minimum((i + 1) * hb, lp // h - 1), 0)),
                  pl.BlockSpec(memory_space=pl.ANY)],
        out_specs=pl.BlockSpec((tr, POOL_E), lambda i: (i, 0)),
        out_shape=jax.ShapeDtypeStruct(dp.shape, dp.dtype),
        input_output_aliases={2: 0},
        compiler_params=_params("parallel"))(dmix, dmix, dp)


def _adamw_math(w, g, m, v):
    nm = ADAM_B1 * m + (1.0 - ADAM_B1) * g
    nv = ADAM_B2 * v + (1.0 - ADAM_B2) * jnp.square(g)
    m_hat = nm / (1.0 - ADAM_B1 ** ADAM_STEP)
    v_hat = nv / (1.0 - ADAM_B2 ** ADAM_STEP)
    return -ADAM_LR * (m_hat / (jnp.sqrt(v_hat) + ADAM_EPS) + ADAM_WD * w), nm, nv


def adamw(w, g, m, v, *, name):
    r, c = w.shape
    tr = _tile(r, max(8, (1 << 19) // c), 8)

    def body(w_ref, g_ref, m_ref, v_ref, d_ref, nm_ref, nv_ref):
        d_ref[...], nm_ref[...], nv_ref[...] = _adamw_math(w_ref[...], g_ref[...], m_ref[...], v_ref[...])

    spec = pl.BlockSpec((tr, c), lambda i: (i, 0))
    return pl.pallas_call(
        body, name=name, grid=(r // tr,), in_specs=[spec] * 4, out_specs=[spec] * 3,
        out_shape=[jax.ShapeDtypeStruct((r, c), F32)] * 3,
        compiler_params=_params("parallel"))(w, g, m, v)


def adamw_many(quads, *, name):
    shape = quads[0][0].shape
    steps = 8
    spec = pl.BlockSpec((shape[0] // steps,) + shape[1:], lambda i: (i, 0, 0))
    n = len(quads)

    def body(*refs):
        for q in range(n):
            w_ref, g_ref, m_ref, v_ref = refs[4 * q:4 * q + 4]
            d_ref, nm_ref, nv_ref = refs[4 * n + 3 * q:4 * n + 3 * q + 3]
            d_ref[...], nm_ref[...], nv_ref[...] = _adamw_math(w_ref[...], g_ref[...], m_ref[...], v_ref[...])

    res = pl.pallas_call(
        body, name=name, grid=(steps,), in_specs=[spec] * (4 * n), out_specs=[spec] * (3 * n),
        out_shape=[jax.ShapeDtypeStruct(shape, F32)] * (3 * n),
        compiler_params=_params("parallel"))(*[a for quad in quads for a in quad])
    return [tuple(res[3 * q:3 * q + 3]) for q in range(n)]


WIRE_WIDTH = 1024
WIRE_ROWS = 1024


def _wire_plan(shapes):
    starts, row = [], 0
    for s in shapes:
        r, c = (1, s[0]) if len(s) == 1 else s
        starts.append(row)
        row += -(-(r * max(1, c // WIRE_WIDTH)) // 8) * 8
    assert row <= WIRE_ROWS, row
    return starts


def _wire_cells(shape):
    if len(shape) == 1:
        n = shape[0]
        if n <= WIRE_WIDTH:
            return [(0, n, (slice(None),))]
        return [(h, WIRE_WIDTH, (slice(h * WIRE_WIDTH, (h + 1) * WIRE_WIDTH),)) for h in range(n // WIRE_WIDTH)]
    r, c = shape
    if c <= WIRE_WIDTH:
        return [(None, c, (slice(None), slice(None)))]
    per = c // WIRE_WIDTH
    return [(j * per + h, WIRE_WIDTH, (j, slice(h * WIRE_WIDTH, (h + 1) * WIRE_WIDTH)))
            for j in range(r) for h in range(per)]


def pack_small(arrays, *, name):
    shapes = [a.shape for a in arrays]
    starts = _wire_plan(shapes)

    def body(*refs):
        out = refs[-1]
        out[...] = jnp.zeros_like(out)
        for ref, shape, row0 in zip(refs[:-1], shapes, starts, strict=True):
            for off, cols, idx in _wire_cells(shape):
                if off is None:
                    out[row0:row0 + shape[0], 0:cols] = ref[idx]
                else:
                    out[row0 + off, 0:cols] = ref[idx]

    return pl.pallas_call(body, name=name, out_shape=jax.ShapeDtypeStruct((WIRE_ROWS, WIRE_WIDTH), F32),
                          compiler_params=pltpu.CompilerParams(vmem_limit_bytes=VMEM_LIMIT))(*arrays)


def update_small(packed, starts, triples, shard_cells, chip, *, name):
    n = len(triples)

    def body(chip_ref, p_ref, *refs):
        k = chip_ref[0]
        ins, outs = refs[:3 * n], refs[3 * n:]

        def cut(row0, rows, cols):
            c0, c1 = cols
            if isinstance(rows, int):
                return p_ref[row0 + rows, c0:c1]
            return p_ref[row0 + rows[0]:row0 + rows[1], c0:c1]

        for q in range(n):
            w_ref, m_ref, v_ref = ins[3 * q:3 * q + 3]
            g_ref, d_ref, nm_ref, nv_ref = outs[4 * q:4 * q + 4]
            shape = w_ref.shape
            if shard_cells[q] is None:
                pieces = [(cut(starts[q], (0, shape[0]) if off is None else off, (0, cols)), idx)
                          for off, cols, idx in _wire_cells(shape)]
            else:
                by_chip = [shard_cells[q](kk) for kk in range(N_CHIPS)]
                pieces = []
                for i, (_, _, idx) in enumerate(by_chip[0]):
                    g = cut(starts[q], *by_chip[0][i][:2])
                    for kk in range(1, N_CHIPS):
                        g = jnp.where(k == kk, cut(starts[q], *by_chip[kk][i][:2]), g)
                    pieces.append((g, idx))
            for g, idx in pieces:
                g_ref[idx] = g
                d_ref[idx], nm_ref[idx], nv_ref[idx] = _adamw_math(w_ref[idx], g, m_ref[idx], v_ref[idx])

    whole = lambda a: pl.BlockSpec(a.shape, lambda i, c_ref, nd=a.ndim: (0,) * nd)
    flat = [a for t in triples for a in t]
    out_shape = [jax.ShapeDtypeStruct(t[0].shape, F32) for t in triples for _ in range(4)]
    res = pl.pallas_call(
        body, name=name,
        grid_spec=pltpu.PrefetchScalarGridSpec(
            num_scalar_prefetch=1, grid=(1,),
            in_specs=[whole(packed)] + [whole(a) for a in flat],
            out_specs=[pl.BlockSpec(s.shape, lambda i, c_ref, nd=len(s.shape): (0,) * nd) for s in out_shape]),
        out_shape=out_shape,
        compiler_params=_params("arbitrary"))(chip.reshape(1).astype(jnp.int32), packed, *flat)
    return [tuple(res[4 * q:4 * q + 4]) for q in range(n)]


ANY = pl.BlockSpec(memory_space=pl.ANY)


def _place():
    x, y, c = lax.axis_index("x"), lax.axis_index("y"), lax.axis_index("c")
    return x, y, c, [(1 - x, y), (x, 1 - y), (1 - x, 1 - y)]


DMA_PIECE_BYTES = 512 * 1024


def _pieces(src, dst):
    shape = src.shape
    rows, cols = shape[-2], shape[-1]
    item = jnp.dtype(src.dtype).itemsize
    unit = 32 // item
    want = max(1, (rows * cols * item) // DMA_PIECE_BYTES)
    n = 1
    if rows % unit == 0:
        n = max(d for d in range(1, rows // unit + 1) if (rows // unit) % d == 0 and d <= want)
    size = rows // n
    out = []
    for lead in (range(shape[0]) if len(shape) == 3 else [None]):
        for i in range(n):
            sel = (pl.ds(i * size, size), slice(None)) if lead is None else (lead, pl.ds(i * size, size), slice(None))
            out.append((src.at[sel], dst.at[sel]))
    return out


def _send(src, dst, send_sem, recv_sem, peer, start=True):
    if start:
        for s, d in _pieces(src, dst):
            pltpu.make_async_remote_copy(src_ref=s, dst_ref=d, send_sem=send_sem, recv_sem=recv_sem,
                                         device_id=peer, device_id_type=MESH).start()
    return pltpu.make_async_remote_copy(src_ref=src, dst_ref=dst, send_sem=send_sem, recv_sem=recv_sem,
                                        device_id=peer, device_id_type=MESH)


def run_job(job, *, name):
    n_in, n_out = len(job.arrays), len(job.out_shapes)

    def body(*refs):
        job.fn(refs[:n_in], refs[n_in:n_in + n_out], refs[n_in + n_out:], "both")

    return pl.pallas_call(body, name=name, in_specs=[ANY] * n_in, out_specs=[ANY] * n_out,
                          out_shape=job.out_shapes, scratch_shapes=job.sems)(*job.arrays)


def gather_chips(shards, *, name):
    n = len(shards)

    def body(*refs):
        ins, outs = refs[:n], refs[n:2 * n]
        send_sems, recv_sems = refs[2 * n:]
        x, y, c, chips = _place()
        k = 2 * x + y
        copies = []
        for a in range(n):
            for j, peer in enumerate([(px, py, c) for px, py in chips] + [(x, y, 1 - c)]):
                copies.append(_send(ins[a], outs[a].at[k], send_sems.at[a, j], recv_sems.at[a, j], peer))
        for cp in copies:
            cp.wait()

    return pl.pallas_call(
        body, name=name, in_specs=[ANY] * n, out_specs=[ANY] * n,
        out_shape=[jax.ShapeDtypeStruct((N_CHIPS,) + s.shape, s.dtype) for s in shards],
        scratch_shapes=[pltpu.SemaphoreType.DMA((n, 4)), pltpu.SemaphoreType.DMA((n, 4))])(*shards)


def gather_by_halves_job(shards):
    n = len(shards)

    def fn(ins, outs, sems, phase):
        send_sems, recv_sems = sems
        x, y, c, chips = _place()
        k = 2 * x + y
        sibling = (x, y, 1 - c)
        begin = phase != "finish"
        waits, arrivals = [], []
        for a in range(n):
            half = ins[a].shape[0] // 2
            waits.append(_send(ins[a], outs[a].at[k], send_sems.at[a, 6], recv_sems.at[a, 6], sibling, begin))
            mine = pl.ds(c * half, half)
            for j, (px, py) in enumerate(chips):
                arrivals.append(_send(ins[a].at[mine, :], outs[a].at[k, mine, :], send_sems.at[a, j],
                                      recv_sems.at[a, j], (px, py, c), begin))
        if phase == "start":
            return
        i = 0
        for a in range(n):
            half = ins[a].shape[0] // 2
            mine = pl.ds(c * half, half)
            for j, (px, py) in enumerate(chips):
                arrivals[i].wait_recv()
                landed = outs[a].at[2 * px + py, mine, :]
                waits.append(_send(landed, landed, send_sems.at[a, 3 + j], recv_sems.at[a, 3 + j], sibling))
                i += 1
        for cp in arrivals:
            cp.wait_send()
        for cp in waits:
            cp.wait()

    return Job(shards, [jax.ShapeDtypeStruct((N_CHIPS,) + s.shape, s.dtype) for s in shards],
               [pltpu.SemaphoreType.DMA((n, 7)), pltpu.SemaphoreType.DMA((n, 7))], fn)


def gather_halves_job(shards):
    n = len(shards)

    def fn(ins, outs, sems, phase):
        send_sems, recv_sems = sems
        x, y, c, chips = _place()
        k = 2 * x + y
        copies = []
        for a in range(n):
            half = ins[a].shape[0] // 2
            mine = pl.ds(c * half, half)
            copies.append(_send(ins[a], outs[a].at[k], send_sems.at[a, 3], recv_sems.at[a, 3], (x, y, 1 - c),
                                phase != "finish"))
            for j, (px, py) in enumerate(chips):
                copies.append(_send(ins[a].at[mine, :], outs[a].at[k, mine, :], send_sems.at[a, j],
                                    recv_sems.at[a, j], (px, py, c), phase != "finish"))
        if phase != "start":
            for cp in copies:
                cp.wait()

    return Job(shards, [jax.ShapeDtypeStruct((N_CHIPS,) + s.shape, s.dtype) for s in shards],
               [pltpu.SemaphoreType.DMA((n, 4)), pltpu.SemaphoreType.DMA((n, 4))], fn)


def forward_halves_job(gathered):
    n = len(gathered)

    def fn(ins, outs, sems, phase):
        send_sems, recv_sems = sems
        x, y, c, chips = _place()
        copies = []
        for a in range(n):
            half = outs[a].shape[1] // 2
            for j, (px, py) in enumerate(chips):
                landed = outs[a].at[2 * px + py, pl.ds(c * half, half), :]
                copies.append(_send(landed, landed, send_sems.at[a, j], recv_sems.at[a, j], (x, y, 1 - c),
                                    phase != "finish"))
        if phase != "start":
            for cp in copies:
                cp.wait()

    return Job(gathered, [jax.ShapeDtypeStruct(g.shape, g.dtype) for g in gathered],
               [pltpu.SemaphoreType.DMA((n, 3)), pltpu.SemaphoreType.DMA((n, 3))], fn, in_place=True)


def swap_job(grads):
    n = len(grads)

    def fn(ins, outs, sems, phase):
        send_sems, recv_sems = sems
        x, y, c, _ = _place()
        copies = []
        for a in range(n):
            half = ins[a].shape[1] // 2
            copies.append(_send(ins[a].at[:, pl.ds((1 - c) * half, half), :], outs[a], send_sems.at[a],
                                recv_sems.at[a], (x, y, 1 - c), phase != "finish"))
        if phase != "start":
            for cp in copies:
                cp.wait()

    return Job(grads, [jax.ShapeDtypeStruct((g.shape[0], g.shape[1] // 2, g.shape[2]), g.dtype) for g in grads],
               [pltpu.SemaphoreType.DMA((n,)), pltpu.SemaphoreType.DMA((n,))], fn)


def _chip_slot(s, k):
    rel = s ^ k
    return (rel >> 1) | ((rel & 1) << 1)


def sum_cores(g, theirs, ck, out_dtype, *, name):
    n, r, cols = g.shape
    half = r // 2
    tr = _tile(half, max(16, (1 << 19) // cols), 16)
    nb = half // tr

    def body(ck_ref, g_ref, t_ref, o_ref):
        o_ref[...] = (g_ref[...] + t_ref[...]).astype(o_ref.dtype)

    return pl.pallas_call(
        body, name=name,
        grid_spec=pltpu.PrefetchScalarGridSpec(
            num_scalar_prefetch=1, grid=(n, nb),
            in_specs=[pl.BlockSpec((None, tr, cols), lambda s, i, ck_ref: (s, ck_ref[0] * nb + i, 0)),
                      pl.BlockSpec((None, tr, cols), lambda s, i, ck_ref: (s, i, 0))],
            out_specs=pl.BlockSpec((None, tr, cols), lambda s, i, ck_ref: (_chip_slot(s, ck_ref[1]), i, 0))),
        out_shape=jax.ShapeDtypeStruct((n, half, cols), out_dtype),
        compiler_params=_params("parallel", "parallel"))(ck, g, theirs)


def scatter_job(parts):
    n = len(parts)

    def fn(ins, outs, sems, phase):
        send_sems, recv_sems = sems
        x, y, c, chips = _place()
        copies = []
        for a in range(n):
            for j, (px, py) in enumerate(chips):
                copies.append(_send(ins[a].at[1 + j], outs[a].at[j], send_sems.at[a, j], recv_sems.at[a, j],
                                    (px, py, c), phase != "finish"))
        if phase != "start":
            for cp in copies:
                cp.wait()

    return Job(parts, [jax.ShapeDtypeStruct((3,) + p.shape[1:], p.dtype) for p in parts],
               [pltpu.SemaphoreType.DMA((n, 3)), pltpu.SemaphoreType.DMA((n, 3))], fn)


def sum_chips(own, others, ck, *, name):
    _, r, cols = own.shape
    tr = _tile(r, max(16, (1 << 19) // cols), 16)

    def body(ck_ref, a_ref, b0_ref, b1_ref, b2_ref, o_ref):
        o_ref[...] = (a_ref[...].astype(F32) + b0_ref[...].astype(F32) + b1_ref[...].astype(F32)
                      + b2_ref[...].astype(F32))

    other = lambda j: pl.BlockSpec((None, tr, cols), lambda i, ck_ref: (j, i, 0))
    return pl.pallas_call(
        body, name=name,
        grid_spec=pltpu.PrefetchScalarGridSpec(
            num_scalar_prefetch=1, grid=(r // tr,),
            in_specs=[pl.BlockSpec((None, tr, cols), lambda i, ck_ref: (0, i, 0)), other(0), other(1), other(2)],
            out_specs=pl.BlockSpec((None, tr, cols), lambda i, ck_ref: (ck_ref[0], i, 0))),
        out_shape=jax.ShapeDtypeStruct((2, r, cols), F32),
        compiler_params=_params("parallel"))(ck, own, others, others, others)


def share_halves(joined, *, name):
    n = len(joined)

    def body(*refs):
        outs = refs[n:2 * n]
        send_sems, recv_sems = refs[2 * n:]
        x, y, c, _ = _place()
        copies = [_send(outs[a].at[c], outs[a].at[c], send_sems.at[a], recv_sems.at[a], (x, y, 1 - c))
                  for a in range(n)]
        for cp in copies:
            cp.wait()

    return pl.pallas_call(
        body, name=name, in_specs=[ANY] * n, out_specs=[ANY] * n,
        out_shape=[jax.ShapeDtypeStruct(j.shape, j.dtype) for j in joined],
        input_output_aliases={a: a for a in range(n)},
        scratch_shapes=[pltpu.SemaphoreType.DMA((n,)), pltpu.SemaphoreType.DMA((n,))])(*joined)


def kernel(x, meta_tokens, norm0_g, l0_w_in, l0_lam_re, l0_lam_im, l0_log_dt, l0_b_re, l0_b_im, l0_c_re, l0_c_im, l0_d_skip, l0_w_glu, l0_b_glu, l0_w_out, norm1_g, l1_w_in, l1_conv_w, l1_conv_b, l1_w_out, norm2_g, l2_w_in, l2_w_grp, l2_b_grp, l2_scale, l2_w_out, norm3_g, l3_w_in, l3_lam_re, l3_lam_im, l3_log_dt, l3_b_re, l3_b_im, l3_c_re, l3_c_im, l3_d_skip, l3_w_glu, l3_b_glu, l3_w_out, final_g, loss_target, m_meta_tokens, m_norm0_g, m_l0_w_in, m_l0_lam_re, m_l0_lam_im, m_l0_log_dt, m_l0_b_re, m_l0_b_im, m_l0_c_re, m_l0_c_im, m_l0_d_skip, m_l0_w_glu, m_l0_b_glu, m_l0_w_out, m_norm1_g, m_l1_w_in, m_l1_conv_w, m_l1_conv_b, m_l1_w_out, m_norm2_g, m_l2_w_in, m_l2_w_grp, m_l2_b_grp, m_l2_scale, m_l2_w_out, m_norm3_g, m_l3_w_in, m_l3_lam_re, m_l3_lam_im, m_l3_log_dt, m_l3_b_re, m_l3_b_im, m_l3_c_re, m_l3_c_im, m_l3_d_skip, m_l3_w_glu, m_l3_b_glu, m_l3_w_out, m_final_g, v_meta_tokens, v_norm0_g, v_l0_w_in, v_l0_lam_re, v_l0_lam_im, v_l0_log_dt, v_l0_b_re, v_l0_b_im, v_l0_c_re, v_l0_c_im, v_l0_d_skip, v_l0_w_glu, v_l0_b_glu, v_l0_w_out, v_norm1_g, v_l1_w_in, v_l1_conv_w, v_l1_conv_b, v_l1_w_out, v_norm2_g, v_l2_w_in, v_l2_w_grp, v_l2_b_grp, v_l2_scale, v_l2_w_out, v_norm3_g, v_l3_w_in, v_l3_lam_re, v_l3_lam_im, v_l3_log_dt, v_l3_b_re, v_l3_b_im, v_l3_c_re, v_l3_c_im, v_l3_d_skip, v_l3_w_glu, v_l3_b_glu, v_l3_w_out, v_final_g):
    given = dict(locals())
    w = {n: given[n] for n in WEIGHTS}
    mom_m = {n: given["m_" + n] for n in WEIGHTS}
    mom_v = {n: given["v_" + n] for n in WEIGHTS}
    seq = x.shape[1]
    real = N_META + seq
    lp = -(-real // SCAN_ROWS) * SCAN_ROWS
    chip = 2 * lax.axis_index("x") + lax.axis_index("y")
    row = lambda a: a.reshape(1, -1)

    shard_bf16 = {n: (w[n].reshape(-1, w[n].shape[-1]) if n == 'l2_w_grp' else w[n]).astype(BF16) for n in BIG}
    layer = lambda i: [n for n in BIG if n.startswith("l%d_" % i)]
    over_ici = lambda names: gather_halves_job([shard_bf16[n] for n in names])
    full = {}

    def landed(names, arrays):
        for n, arr in zip(names, arrays, strict=True):
            if n.endswith('w_in'):
                full[n] = arr
            elif n == 'l2_w_grp':
                full[n] = arr.reshape(N_CHIPS, len(POOL_WINDOWS), POOL_GROUP // N_CHIPS, POOL_GROUP)
            else:
                full[n] = arr.reshape(1, -1, arr.shape[-1])

    landed(['l0_w_in'], run_job(gather_by_halves_job([shard_bf16['l0_w_in']]), name="gather_l0_w_in"))
    full.update(zip(SMALL_SHARDED, gather_chips([w[n] for n in SMALL_SHARDED], name="gather_small_shards"),
                    strict=True))
    meta_full = full['meta_tokens'].transpose(1, 0, 2).reshape(N_META, D_MODEL)
    conv_w_full = full['l1_conv_w'].transpose(1, 0, 2).reshape(3, CONV_E)
    b_grp_full = full['l2_b_grp'].transpose(1, 0, 2).reshape(len(POOL_WINDOWS), POOL_GROUP)

    h0 = jnp.concatenate([meta_full, x[0], jnp.zeros((lp - real, D_MODEL), F32)], axis=0)
    target = jnp.concatenate([jnp.zeros((N_META, D_MODEL), F32), loss_target[0],
                              jnp.zeros((lp - real, D_MODEL), F32)], axis=0)
    grads = {}
    saved = {}
    gip = lambda a: jnp.swapaxes(a, 1, 2)

    def s5_layer_fwd(i, h, first=False):
        pre = "l%d_" % i
        prm = [w[pre + 'lam_re'], w[pre + 'lam_im'], w[pre + 'log_dt'].reshape(-1, 1), gip(w[pre + 'b_re']),
               gip(w[pre + 'b_im']), w[pre + 'c_re'], w[pre + 'c_im']]
        a_re, a_im, bblk, cblk = s5_prep(prm, name=pre + "prep")
        tables = (a_re.reshape(1, -1), a_im.reshape(1, -1), bblk, cblk)
        ar, ai = tables[:2]
        gain = row(w["norm%d_g" % i])
        rest = [pre + 'w_glu', pre + 'w_out']
        if not first:
            uz, n = norm_mm_nn(h, gain, full[pre + 'w_in'], name=pre + "in")
            y0, yg, sr, si = s5_fwd(uz, bblk, cblk, ar, ai, row(w[pre + 'd_skip']), name=pre + "scan")
            q = mm_nn(yg, full[pre + 'w_glu'], bias=row(w[pre + 'b_glu']), name=pre + "glu")
        else:
            (uz, n), halves = norm_mm_nn(h, gain, full[pre + 'w_in'], name=pre + "in", job=over_ici(rest))
            (y0, yg, sr, si), arrived = s5_fwd(
                uz, bblk, cblk, ar, ai, row(w[pre + 'd_skip']), name=pre + "scan",
                job=Job.both(over_ici(layer(1)), forward_halves_job(halves)))
            landed(rest, arrived[len(layer(1)):])
            q, whole = mm_nn(yg, full[pre + 'w_glu'], bias=row(w[pre + 'b_glu']), name=pre + "glu",
                             job=forward_halves_job(arrived[:len(layer(1))]))
            landed(layer(1), whole)
        (y3,), _ = rowwise(lambda yv, qv, zv: ([_s5_gate(yv, qv, zv)], []),
                           [(y0, D_MODEL, 0), (q, D_MODEL, 0), (uz, D_MODEL, 1)], [],
                           [(D_MODEL, BF16, D_MODEL, 0)], name=pre + "gate")
        h_new = mm_nn(y3, full[pre + 'w_out'], add=h, name=pre + "out")
        saved[i] = dict(h=h, n=n, uz=uz, y0=y0, sr=sr, si=si, yg=yg, q=q, y3=y3, tables=tables, prm=prm)
        return h_new

    def s5_layer_bwd(i, dh, carry=None):
        pre = "l%d_" % i
        s = saved[i]
        ar, ai, bblk, cblk = s['tables']
        if carry is None:
            grads[pre + 'w_out'] = mm_tn(s['y3'], dh, 1, name=pre + "d_w_out")[0]
        else:
            names, pieces = core_parts(carry)
            d_w_out, theirs = mm_tn(s['y3'], dh, 1, name=pre + "d_w_out", job=swap_job(pieces))
            grads[pre + 'w_out'] = d_w_out[0]
            sum_over_cores(names, pieces, theirs)
        dy3 = mm_nt(dh, full[pre + 'w_out'], name=pre + "d_y3")

        def gate_bwd(dyv, yv, qv, zv):
            _, vjp = jax.vjp(_s5_gate, yv, qv, zv)
            dy0, dq, dz = vjp(dyv)
            return [dy0, dq, dz], [_colsum(dq)]

        (dy0_direct, dq, dp), (db_glu,) = rowwise(
            gate_bwd, [(dy3, D_MODEL, 0), (s['y0'], D_MODEL, 0), (s['q'], D_MODEL, 0), (s['uz'], D_MODEL, 1)], [],
            [(D_MODEL, F32, D_MODEL, 0), (D_MODEL, BF16, D_MODEL, 0), (2 * D_MODEL, BF16, D_MODEL, 1)], [D_MODEL],
            name=pre + "d_gate")
        grads[pre + 'b_glu'] = db_glu
        grads[pre + 'w_glu'] = mm_tn(s['yg'], dq, 1, name=pre + "d_w_glu")[0]
        dyg = mm_nt(dq, full[pre + 'w_glu'], name=pre + "d_yg")
        res = s5_bwd(dy0_direct, dyg, s['y0'], s['uz'], s['sr'], s['si'], bblk, cblk, ar, ai,
                     row(w[pre + 'd_skip']), dp, name=pre + "d_scan",
                     job=None if carry is None else scatter_job(scatter_of(carry)))
        if carry is not None:
            res, arrived = res
            scattered(carry, arrived)
        dp, dbblk, dcblk, dar, dai, dd = res
        grads[pre + 'd_skip'] = dd
        cots = (dar.reshape(S5_GROUPS, S5_STATE), dai.reshape(S5_GROUPS, S5_STATE), dbblk, dcblk)
        for key, val in zip(('lam_re', 'lam_im', 'log_dt', 'b_re', 'b_im', 'c_re', 'c_im'),
                            s5_prep_bwd(s['prm'], cots, name=pre + "d_prep"), strict=True):
            grads[pre + key] = val.reshape(-1) if key == 'log_dt' else val
        grads[pre + 'w_in'] = mm_tn(s['n'], dp, N_CHIPS, name=pre + "d_w_in")
        dh_in, dg = mm_nt_norm_bwd(dp, full[pre + 'w_in'], s['h'], row(w["norm%d_g" % i]), dh, name=pre + "d_n")
        grads["norm%d_g" % i] = dg
        return dh_in

    h1 = s5_layer_fwd(0, h0, first=True)

    (p1, n1), halves = norm_mm_nn(h1, row(w['norm1_g']), full['l1_w_in'], name="l1_in", job=over_ici(layer(2)))
    y3_1 = conv_fwd(p1, conv_w_full, row(w['l1_conv_b']), name="l1_conv")
    h2, whole = mm_nn(y3_1, full['l1_w_out'], add=h1, name="l1_out", job=forward_halves_job(halves))
    landed(layer(2), whole)

    (p2, n2), halves = norm_mm_nn(h2, row(w['norm2_g']), full['l2_w_in'], name="l2_in", job=over_ici(layer(3)))
    mixed = pool_fwd(p2, name="l2_pool")
    o2 = grp_nn(mixed, full['l2_w_grp'], b_grp_full.reshape(1, -1), name="l2_grp")
    (y3_2,), _ = rowwise(lambda ov, zv, sv: ([_pool_gate(ov, zv, sv)], []),
                         [(o2, POOL_E, 0), (p2, POOL_E, 1)], [row(w['l2_scale'])],
                         [(POOL_E, BF16, POOL_E, 0)], name="l2_gate")
    h3, whole = mm_nn(y3_2, full['l2_w_out'], add=h2, name="l2_out", job=forward_halves_job(halves))
    landed(layer(3), whole)

    h4 = s5_layer_fwd(3, h3)

    def loss_fn(hv, tv, gv):
        i = pl.program_id(0)
        tr = hv.shape[0]
        yf, vjp = jax.vjp(_rms, hv, gv)
        pos = i * tr + lax.broadcasted_iota(jnp.int32, (tr, 1), 0)
        diff = jnp.where((pos >= N_META) & (pos < real), yf - tv, 0.0)
        dh, dg = vjp(diff / D_MODEL)
        return [dh], [dg, _colsum(diff * diff)]

    (dh,), (d_final_g, sq) = rowwise(loss_fn, [(h4, D_MODEL, 0), (target, D_MODEL, 0)], [row(w['final_g'])],
                                     [(D_MODEL, F32, D_MODEL, 0)], [D_MODEL, D_MODEL], name="loss")
    grads['final_g'] = d_final_g
    loss = lax.psum(0.5 / D_MODEL * jnp.sum(sq), ("x", "y", "c"))

    ck = jnp.stack([lax.axis_index("c"), chip]).astype(jnp.int32)
    pair, from_chips = {}, {}

    def chip_major(n):
        g = grads[n]
        if n.endswith('w_in'):
            return g
        if n == 'l2_w_grp':
            return g.reshape(N_CHIPS, -1, POOL_GROUP)
        return g.reshape(N_CHIPS, -1, g.shape[-1])

    def core_parts(i, small=None):
        names, pieces = layer(i), [chip_major(n) for n in layer(i)]
        return (names + ['small'], pieces + [small]) if small is not None else (names, pieces)

    def sum_over_cores(names, pieces, theirs):
        for n, g, t in zip(names, pieces, theirs, strict=True):
            pair[n] = sum_cores(g, t, ck, F32 if n == 'small' else BF16, name="sum_cores_" + n)

    scatter_of = lambda i: [pair[n] for n in layer(i)]

    def scattered(i, arrays):
        from_chips.update(zip(layer(i), arrays, strict=True))

    dh = s5_layer_bwd(3, dh)

    names, pieces = core_parts(3)
    d_w_out, theirs = mm_tn(y3_2, dh, 1, name="l2_d_w_out", job=swap_job(pieces))
    grads['l2_w_out'] = d_w_out[0]
    sum_over_cores(names, pieces, theirs)
    dy3 = mm_nt(dh, full['l2_w_out'], name="l2_d_y3")

    def pool_gate_bwd(dyv, ov, zv, sv):
        _, vjp = jax.vjp(_pool_gate, ov, zv, sv)
        do, dz, dscale = vjp(dyv)
        return [do, dz], [dscale, _colsum(do)]

    (do2, dp2), (dscale, dbgrp) = rowwise(
        pool_gate_bwd, [(dy3, POOL_E, 0), (o2, POOL_E, 0), (p2, POOL_E, 1)], [row(w['l2_scale'])],
        [(POOL_E, BF16, POOL_E, 0), (2 * POOL_E, BF16, POOL_E, 1)], [POOL_E, POOL_E], name="l2_d_gate")
    grads['l2_scale'] = dscale
    grads['l2_b_grp'] = dbgrp
    grads['l2_w_grp'] = grp_tn(mixed, do2, name="l2_d_w_grp")
    dmix = grp_nt(do2, full['l2_w_grp'], name="l2_d_mixed")
    dp2 = pool_bwd(dmix, dp2, name="l2_d_pool")
    grads['l2_w_in'], arrived = mm_tn(n2, dp2, N_CHIPS, name="l2_d_w_in", job=scatter_job(scatter_of(3)))
    scattered(3, arrived)
    dh, dg = mm_nt_norm_bwd(dp2, full['l2_w_in'], h2, row(w['norm2_g']), dh, name="l2_d_n")
    grads['norm2_g'] = dg

    names, pieces = core_parts(2)
    d_w_out, theirs = mm_tn(y3_1, dh, 1, name="l1_d_w_out", job=swap_job(pieces))
    grads['l1_w_out'] = d_w_out[0]
    sum_over_cores(names, pieces, theirs)
    dy3 = mm_nt(dh, full['l1_w_out'], name="l1_d_y3")
    dp1, dcw, dcb = conv_bwd(dy3, p1, conv_w_full, row(w['l1_conv_b']), name="l1_d_conv")
    grads['l1_conv_w'] = dcw
    grads['l1_conv_b'] = dcb
    grads['l1_w_in'], arrived = mm_tn(n1, dp1, N_CHIPS, name="l1_d_w_in", job=scatter_job(scatter_of(2)))
    scattered(2, arrived)
    dh, dg = mm_nt_norm_bwd(dp1, full['l1_w_in'], h1, row(w['norm1_g']), dh, name="l1_d_n")
    grads['norm1_g'] = dg

    dh = s5_layer_bwd(0, dh, carry=1)
    grad_x = dh[N_META:real][None]
    grads['meta_tokens'] = dh[:N_META]

    wide = [n for n in SMALL if w[n].ndim == 3]
    wire = [grads[n].reshape(S5_GROUPS, -1) if n in wide else grads[n] for n in SMALL]
    starts = dict(zip(SMALL, _wire_plan([a.shape for a in wire]), strict=True))
    small_local = pack_small(wire, name="pack_small")
    names, pieces = core_parts(0, small_local.reshape(N_CHIPS, -1, WIRE_WIDTH))
    sum_over_cores(names, pieces, run_job(swap_job(pieces), name="reduce_cores_l0"))
    rest = layer(0) + ['small']
    from_chips.update(zip(rest, run_job(scatter_job([pair[n] for n in rest]), name="reduce_chips_l0"), strict=True))
    joined = [sum_chips(pair[n], from_chips[n], ck, name="sum_chips_" + n) for n in BIG + ['small']]
    joined = share_halves(joined, name="share_cores")
    reduced = [j.reshape(-1, j.shape[-1]) for j in joined]
    g_big = {n: reduced[i].reshape(w[n].shape) for i, n in enumerate(BIG)}
    (small_all,) = run_job(gather_by_halves_job([reduced[-1]]), name="gather_small")
    small_all = small_all.reshape(WIRE_ROWS, WIRE_WIDTH)

    out_g, out_d, out_m, out_v = {}, {}, {}, {}
    for n in BIG:
        shape = w[n].shape
        as2d = lambda a: a.reshape(-1, shape[-1])
        d, nm, nv = adamw(as2d(w[n]), as2d(g_big[n]), as2d(mom_m[n]), as2d(mom_v[n]), name="adamw_" + n)
        out_g[n], out_d[n], out_m[n], out_v[n] = g_big[n], d.reshape(shape), nm.reshape(shape), nv.reshape(shape)

    half_w = WIRE_WIDTH // 2
    shard_cells = {
        'meta_tokens': lambda k: [((0, N_META), (k * 256, (k + 1) * 256), (slice(None), slice(None)))],
        'l1_conv_w': lambda k: [(2 * j + k // 2, ((k % 2) * half_w, (k % 2 + 1) * half_w), (j, slice(None)))
                                for j in range(3)],
        'l2_b_grp': lambda k: [(j // 2, ((j % 2) * half_w + k * 128, (j % 2) * half_w + (k + 1) * 128),
                                (j, slice(None))) for j in range(len(POOL_WINDOWS))],
    }
    plain = [n for n in SMALL if n not in wide]
    res = update_small(small_all, [starts[n] for n in plain], [(w[n], mom_m[n], mom_v[n]) for n in plain],
                       [shard_cells.get(n) for n in plain], chip, name="adamw_small")
    for n, (g, d, nm, nv) in zip(plain, res, strict=True):
        out_g[n], out_d[n], out_m[n], out_v[n] = g, d, nm, nv
    as_gip = lambda n, a: gip(a) if n.endswith(('b_re', 'b_im')) else a
    g_gip = {n: small_all[starts[n]:starts[n] + S5_GROUPS].reshape(S5_GROUPS, S5_GROUP, S5_STATE) for n in wide}
    res = adamw_many([(as_gip(n, w[n]), g_gip[n], as_gip(n, mom_m[n]), as_gip(n, mom_v[n])) for n in wide],
                     name="adamw_s5_tensors")
    for n, (d, nm, nv) in zip(wide, res, strict=True):
        out_g[n], out_d[n], out_m[n], out_v[n] = as_gip(n, g_gip[n]), as_gip(n, d), as_gip(n, nm), as_gip(n, nv)

    return (loss, grad_x, *[out_g[n] for n in WEIGHTS], *[out_d[n] for n in WEIGHTS],
            *[out_m[n] for n in WEIGHTS], *[out_v[n] for n in WEIGHTS])
```

```python
import functools

import jax
import jax.numpy as jnp
from jax import lax
from jax.experimental import pallas as pl
from jax.experimental.pallas import tpu as pltpu

F32, BF16 = jnp.float32, jnp.bfloat16
MESH = pl.DeviceIdType.MESH

D_MODEL = 1024
N_META = 16
EPS = 1e-6
S5_GROUPS, S5_GROUP, S5_STATE = 64, 16, 64
LANE_BLOCK = 128
S5_BLOCKS = D_MODEL // LANE_BLOCK
GROUPS_PER_BLOCK = LANE_BLOCK // S5_GROUP
STATE_BLOCK = GROUPS_PER_BLOCK * S5_STATE
SCAN_ROWS = 256
SUBLANES = 8
CONV_E = 2048
POOL_E = 2048
POOL_WINDOWS = (2, 4, 8, 16)
POOL_GROUP = 512
POOL_HALO = 16
N_CHIPS = 4

ADAM_LR, ADAM_B1, ADAM_B2, ADAM_EPS, ADAM_WD, ADAM_STEP = 0.001, 0.9, 0.999, 1e-08, 0.01, 10

VMEM_LIMIT = 48 * 1024 * 1024
TN_OPERAND_BYTES = 28 * 1024 * 1024

WEIGHTS = ['meta_tokens', 'norm0_g', 'l0_w_in', 'l0_lam_re', 'l0_lam_im', 'l0_log_dt', 'l0_b_re', 'l0_b_im',
           'l0_c_re', 'l0_c_im', 'l0_d_skip', 'l0_w_glu', 'l0_b_glu', 'l0_w_out', 'norm1_g', 'l1_w_in',
           'l1_conv_w', 'l1_conv_b', 'l1_w_out', 'norm2_g', 'l2_w_in', 'l2_w_grp', 'l2_b_grp', 'l2_scale',
           'l2_w_out', 'norm3_g', 'l3_w_in', 'l3_lam_re', 'l3_lam_im', 'l3_log_dt', 'l3_b_re', 'l3_b_im',
           'l3_c_re', 'l3_c_im', 'l3_d_skip', 'l3_w_glu', 'l3_b_glu', 'l3_w_out', 'final_g']
BIG = ['l0_w_in', 'l0_w_glu', 'l0_w_out', 'l1_w_in', 'l1_w_out', 'l2_w_in', 'l2_w_grp', 'l2_w_out',
       'l3_w_in', 'l3_w_glu', 'l3_w_out']
SMALL_SHARDED = ['meta_tokens', 'l1_conv_w', 'l2_b_grp']
SMALL = [n for n in WEIGHTS if n not in BIG]


def _params(*sem):
    return pltpu.CompilerParams(dimension_semantics=sem, vmem_limit_bytes=VMEM_LIMIT)


class Job:
    def __init__(self, arrays, out_shapes, sems, fn, in_place=False):
        self.arrays, self.out_shapes, self.sems, self.fn = list(arrays), list(out_shapes), list(sems), fn
        self.in_place = in_place
        assert len(self.arrays) == len(self.out_shapes)

    @staticmethod
    def both(first, second):
        na, ns = len(first.arrays), len(first.sems)

        def fn(ins, outs, sems, phase):
            first.fn(ins[:na], outs[:na], sems[:ns], phase)
            second.fn(ins[na:], outs[na:], sems[ns:], phase)

        job = Job(first.arrays + second.arrays, first.out_shapes + second.out_shapes, first.sems + second.sems, fn)
        job.in_place = [first.in_place] * na + [second.in_place] * len(second.arrays)
        return job

    def aliased(self):
        flags = self.in_place if isinstance(self.in_place, list) else [self.in_place] * len(self.arrays)
        return [a for a, flag in enumerate(flags) if flag]


def _call(body, *, name, grid, in_specs, out_specs, out_shape, args, semantics, scratch=(), aliases=None, job=None):
    if job is None:
        return pl.pallas_call(
            body, name=name, grid=grid, in_specs=list(in_specs), out_specs=list(out_specs),
            out_shape=list(out_shape), scratch_shapes=list(scratch), input_output_aliases=aliases or {},
            compiler_params=_params(*semantics))(*args)
    counts = [len(in_specs), len(job.arrays), len(out_specs), len(job.out_shapes), len(scratch), len(job.sems)]

    def hosted(*refs):
        parts, pos = [], 0
        for k in counts:
            parts.append(refs[pos:pos + k])
            pos += k
        ins, job_ins, outs, job_outs, scr, job_sems = parts
        steps = [pl.program_id(d) for d in range(len(grid))]
        first = functools.reduce(jnp.logical_and, [s == 0 for s in steps])
        last = functools.reduce(jnp.logical_and, [s == g - 1 for s, g in zip(steps, grid, strict=True)])

        @pl.when(first)
        def _():
            job.fn(job_ins, job_outs, job_sems, "start")

        body(*ins, *outs, *scr)

        @pl.when(last)
        def _():
            job.fn(job_ins, job_outs, job_sems, "finish")

    any_spec = pl.BlockSpec(memory_space=pl.ANY)
    aliases = dict(aliases or {})
    aliases.update({len(in_specs) + a: len(out_specs) + a for a in job.aliased()})
    res = pl.pallas_call(
        hosted, name=name, grid=grid, in_specs=list(in_specs) + [any_spec] * len(job.arrays),
        out_specs=list(out_specs) + [any_spec] * len(job.out_shapes),
        out_shape=list(out_shape) + job.out_shapes, scratch_shapes=list(scratch) + job.sems,
        input_output_aliases=aliases,
        compiler_params=_params(*["arbitrary"] * len(grid)))(*args, *job.arrays)
    return res[:len(out_specs)], res[len(out_specs):]


def _tile(n, cap, mult):
    best = None
    for t in range(mult, min(n, cap) + 1, mult):
        if n % t == 0:
            best = t
    assert best is not None, (n, cap, mult)
    return best


def _with_job(res, job):
    return res[0] if job is None else (res[0][0], res[1])


def mm_nn(a, b, *, name, bias=None, add=None, out_dtype=F32, job=None):
    m, k = a.shape
    s, _, ns = b.shape
    n = s * ns
    tm, tn = _tile(m, 1088, 16), _tile(ns, 1024, 128)
    per = ns // tn

    def body(*refs):
        a_ref, b_ref = refs[0], refs[1]
        o_ref = refs[-1]
        acc = jnp.dot(a_ref[...].astype(BF16), b_ref[...], preferred_element_type=F32)
        pos = 2
        if bias is not None:
            acc = acc + refs[pos][...]
            pos += 1
        if add is not None:
            acc = acc + refs[pos][...]
        o_ref[...] = acc.astype(o_ref.dtype)

    in_specs = [pl.BlockSpec((tm, k), lambda i, j: (i, 0)),
                pl.BlockSpec((None, k, tn), lambda i, j: (j // per, 0, j % per))]
    args = [a, b]
    if bias is not None:
        in_specs.append(pl.BlockSpec((1, tn), lambda i, j: (0, j)))
        args.append(bias)
    if add is not None:
        in_specs.append(pl.BlockSpec((tm, tn), lambda i, j: (i, j)))
        args.append(add)
    return _with_job(_call(
        body, name=name, grid=(m // tm, n // tn), in_specs=in_specs,
        out_specs=[pl.BlockSpec((tm, tn), lambda i, j: (i, j))],
        out_shape=[jax.ShapeDtypeStruct((m, n), out_dtype)], args=args,
        semantics=("parallel", "parallel"), job=job), job)


def norm_mm_nn(h, gain, b, *, name, job=None):
    m, k = h.shape
    s, _, ns = b.shape
    n = s * ns
    tm, tn = _tile(m, 1088, 16), _tile(ns, 1024, 128)
    per = ns // tn

    def body(h_ref, g_ref, b_ref, o_ref, n_ref):
        @pl.when(pl.program_id(1) == 0)
        def _():
            n_ref[...] = _rms(h_ref[...], g_ref[...]).astype(n_ref.dtype)

        o_ref[...] = jnp.dot(n_ref[...], b_ref[...], preferred_element_type=F32)

    res = _call(
        body, name=name, grid=(m // tm, n // tn),
        in_specs=[pl.BlockSpec((tm, k), lambda i, j: (i, 0)), pl.BlockSpec((1, k), lambda i, j: (0, 0)),
                  pl.BlockSpec((None, k, tn), lambda i, j: (j // per, 0, j % per))],
        out_specs=[pl.BlockSpec((tm, tn), lambda i, j: (i, j)), pl.BlockSpec((tm, k), lambda i, j: (i, 0))],
        out_shape=[jax.ShapeDtypeStruct((m, n), F32), jax.ShapeDtypeStruct((m, k), BF16)], args=(h, gain, b),
        semantics=("parallel", "arbitrary"), job=job)
    return (res[0], res[1]) if job is None else ((res[0][0], res[0][1]), res[1])


def glu_gate(yg, w_glu, b_glu, y0, uz, *, name, job=None):
    m, k = yg.shape
    n = w_glu.shape[2]
    tm = _tile(m, 1088, 16)

    def body(a_ref, w_ref, b_ref, y0_ref, z_ref, q_ref, o_ref):
        q = jnp.dot(a_ref[...], w_ref[...], preferred_element_type=F32) + b_ref[...]
        q_ref[...] = q
        o_ref[...] = _s5_gate(y0_ref[...], q, z_ref[...]).astype(o_ref.dtype)

    tile = pl.BlockSpec((tm, n), lambda i: (i, 0))
    res = _call(
        body, name=name, grid=(m // tm,),
        in_specs=[pl.BlockSpec((tm, k), lambda i: (i, 0)), pl.BlockSpec((None, k, n), lambda i: (0, 0, 0)),
                  pl.BlockSpec((1, n), lambda i: (0, 0)), tile, pl.BlockSpec((tm, n), lambda i: (i, 1))],
        out_specs=[tile, tile],
        out_shape=[jax.ShapeDtypeStruct((m, n), F32), jax.ShapeDtypeStruct((m, n), BF16)],
        args=(yg, w_glu, b_glu, y0, uz), semantics=("parallel",), job=job)
    return (res[0], res[1]) if job is None else ((res[0][0], res[0][1]), res[1])


def mm_nt_norm_bwd(g, w, h, gain, dh_out, *, name, job=None):
    m, kc = g.shape
    s, nout, kcs = w.shape
    assert s * kcs == kc
    tm, tk = _tile(m, 1088, 16), _tile(kcs, 1024, 128)
    per = kcs // tk
    nk = kc // tk

    def body(g_ref, w_ref, h_ref, gain_ref, dh_ref, o_ref, dg_ref):
        i, kk = pl.program_id(0), pl.program_id(1)
        part = lax.dot_general(g_ref[...].astype(BF16), w_ref[...], (((1,), (1,)), ((), ())),
                               preferred_element_type=F32)

        @pl.when(kk == 0)
        def _():
            o_ref[...] = part

        @pl.when(kk > 0)
        def _():
            o_ref[...] += part

        @pl.when(kk == nk - 1)
        def _():
            dh, dg = jax.vjp(_rms, h_ref[...], gain_ref[...])[1](o_ref[...])
            o_ref[...] = dh_ref[...] + dh

            @pl.when(i == 0)
            def _():
                dg_ref[...] = dg

            @pl.when(i > 0)
            def _():
                dg_ref[...] += dg

    row_tile = pl.BlockSpec((tm, nout), lambda i, kk: (i, 0))
    res = _call(
        body, name=name, grid=(m // tm, nk),
        in_specs=[pl.BlockSpec((tm, tk), lambda i, kk: (i, kk)),
                  pl.BlockSpec((None, nout, tk), lambda i, kk: (kk // per, 0, kk % per)),
                  row_tile, pl.BlockSpec((1, nout), lambda i, kk: (0, 0)), row_tile],
        out_specs=[row_tile, pl.BlockSpec((1, nout), lambda i, kk: (0, 0))],
        out_shape=[jax.ShapeDtypeStruct((m, nout), F32), jax.ShapeDtypeStruct((1, nout), F32)],
        args=(g, w, h, gain, dh_out), semantics=("arbitrary", "arbitrary"), job=job)
    return (res[0], res[1]) if job is None else ((res[0][0], res[0][1]), res[1])


def mm_nt(g, w, *, name, job=None):
    m, kc = g.shape
    s, nout, kcs = w.shape
    assert s * kcs == kc
    tm, tno, tk = _tile(m, 2176, 16), _tile(nout, 1024, 128), _tile(kcs, 1024, 128)
    per = kcs // tk

    def body(g_ref, w_ref, o_ref):
        kk = pl.program_id(2)
        part = lax.dot_general(g_ref[...].astype(BF16), w_ref[...], (((1,), (1,)), ((), ())),
                               preferred_element_type=F32)

        @pl.when(kk == 0)
        def _():
            o_ref[...] = part

        @pl.when(kk > 0)
        def _():
            o_ref[...] += part

    return _with_job(_call(
        body, name=name, grid=(m // tm, nout // tno, kc // tk),
        in_specs=[pl.BlockSpec((tm, tk), lambda i, j, kk: (i, kk)),
                  pl.BlockSpec((None, tno, tk), lambda i, j, kk: (kk // per, j, kk % per))],
        out_specs=[pl.BlockSpec((tm, tno), lambda i, j, kk: (i, j))],
        out_shape=[jax.ShapeDtypeStruct((m, nout), F32)], args=(g, w),
        semantics=("parallel", "parallel", "arbitrary"), job=job), job)


def mm_tn(a, g, shards, *, name, job=None):
    m, ka = a.shape
    n = g.shape[1]
    ns = n // shards
    tka, tn = _tile(ka, 1024, 128), _tile(ns, 512, 128)
    row_bytes = tka * jnp.dtype(a.dtype).itemsize + tn * jnp.dtype(g.dtype).itemsize
    tm = _tile(m, TN_OPERAND_BYTES // (2 * row_bytes), 16)
    per = ns // tn

    def body(a_ref, g_ref, o_ref):
        mm = pl.program_id(2)
        part = lax.dot_general(a_ref[...].astype(BF16), g_ref[...].astype(BF16), (((0,), (0,)), ((), ())),
                               preferred_element_type=F32)

        @pl.when(mm == 0)
        def _():
            o_ref[...] = part

        @pl.when(mm > 0)
        def _():
            o_ref[...] += part

    return _with_job(_call(
        body, name=name, grid=(ka // tka, n // tn, m // tm),
        in_specs=[pl.BlockSpec((tm, tka), lambda i, j, mm: (mm, i)),
                  pl.BlockSpec((tm, tn), lambda i, j, mm: (mm, j))],
        out_specs=[pl.BlockSpec((None, tka, tn), lambda i, j, mm: (j // per, i, j % per))],
        out_shape=[jax.ShapeDtypeStruct((shards, ka, ns), F32)], args=(a, g),
        semantics=("parallel", "parallel", "arbitrary"), job=job), job)


def grp_nn(a, w, bias, *, name):
    m = a.shape[0]
    tm = _tile(m, 1088, 16)
    ng = len(POOL_WINDOWS)

    def body(a_ref, w_ref, b_ref, o_ref):
        wj = w_ref[...].reshape(POOL_GROUP, POOL_GROUP)
        o_ref[...] = jnp.dot(a_ref[...].astype(BF16), wj, preferred_element_type=F32) + b_ref[...]

    return pl.pallas_call(
        body, name=name, grid=(m // tm, ng),
        in_specs=[pl.BlockSpec((tm, POOL_GROUP), lambda i, j: (i, j)),
                  pl.BlockSpec((N_CHIPS, None, POOL_GROUP // N_CHIPS, POOL_GROUP), lambda i, j: (0, j, 0, 0)),
                  pl.BlockSpec((1, POOL_GROUP), lambda i, j: (0, j))],
        out_specs=pl.BlockSpec((tm, POOL_GROUP), lambda i, j: (i, j)),
        out_shape=jax.ShapeDtypeStruct((m, ng * POOL_GROUP), F32),
        compiler_params=_params("parallel", "parallel"))(a, w, bias)


def grp_nt(g, w, *, name):
    m = g.shape[0]
    tm = _tile(m, 1088, 16)
    ng = len(POOL_WINDOWS)

    def body(g_ref, w_ref, o_ref):
        wj = w_ref[...].reshape(POOL_GROUP, POOL_GROUP)
        o_ref[...] = lax.dot_general(g_ref[...].astype(BF16), wj, (((1,), (1,)), ((), ())),
                                     preferred_element_type=F32)

    return pl.pallas_call(
        body, name=name, grid=(m // tm, ng),
        in_specs=[pl.BlockSpec((tm, POOL_GROUP), lambda i, j: (i, j)),
                  pl.BlockSpec((N_CHIPS, None, POOL_GROUP // N_CHIPS, POOL_GROUP), lambda i, j: (0, j, 0, 0))],
        out_specs=pl.BlockSpec((tm, POOL_GROUP), lambda i, j: (i, j)),
        out_shape=jax.ShapeDtypeStruct((m, ng * POOL_GROUP), F32),
        compiler_params=_params("parallel", "parallel"))(g, w)


def grp_tn(a, g, *, name):
    m = a.shape[0]
    tm = _tile(m, 1088, 16)
    ng = len(POOL_WINDOWS)
    rows = POOL_GROUP // N_CHIPS

    def body(a_ref, g_ref, o_ref):
        mm = pl.program_id(1)
        part = lax.dot_general(a_ref[...].astype(BF16), g_ref[...].astype(BF16), (((0,), (0,)), ((), ())),
                               preferred_element_type=F32).reshape(N_CHIPS, rows, POOL_GROUP)

        @pl.when(mm == 0)
        def _():
            o_ref[...] = part

        @pl.when(mm > 0)
        def _():
            o_ref[...] += part

    return pl.pallas_call(
        body, name=name, grid=(ng, m // tm),
        in_specs=[pl.BlockSpec((tm, POOL_GROUP), lambda j, mm: (mm, j)),
                  pl.BlockSpec((tm, POOL_GROUP), lambda j, mm: (mm, j))],
        out_specs=pl.BlockSpec((N_CHIPS, None, rows, POOL_GROUP), lambda j, mm: (0, j, 0, 0)),
        out_shape=jax.ShapeDtypeStruct((N_CHIPS, ng, rows, POOL_GROUP), F32),
        compiler_params=_params("parallel", "arbitrary"))(a, g)


def rowwise(fn, rows, bcast, outs, accs=(), *, name, aliases=None):
    m = rows[0][0].shape[0]
    tr = m // 16
    n_in = len(rows) + len(bcast)

    def body(*refs):
        vals = [r[...] for r in refs[:n_in]]
        o_vals, a_vals = fn(*vals)
        for ref, val in zip(refs[n_in:n_in + len(outs)], o_vals, strict=True):
            ref[...] = val.astype(ref.dtype)
        step = pl.program_id(0)
        for ref, val in zip(refs[n_in + len(outs):], a_vals, strict=True):
            @pl.when(step == 0)
            def _(ref=ref, val=val):
                ref[...] = val

            @pl.when(step > 0)
            def _(ref=ref, val=val):
                ref[...] += val

    in_specs = [pl.BlockSpec((tr, w), functools.partial(lambda i, cb: (i, cb), cb=cb)) for _, w, cb in rows]
    in_specs += [pl.BlockSpec(b.shape, lambda i: (0, 0)) for b in bcast]
    out_specs = [pl.BlockSpec((tr, w), functools.partial(lambda i, cb: (i, cb), cb=cb)) for _, _, w, cb in outs]
    out_specs += [pl.BlockSpec((1, w), lambda i: (0, 0)) for w in accs]
    out_shape = [jax.ShapeDtypeStruct((m, cols), dt) for cols, dt, _, _ in outs]
    out_shape += [jax.ShapeDtypeStruct((1, w), F32) for w in accs]
    res = pl.pallas_call(
        body, name=name, grid=(m // tr,), in_specs=in_specs, out_specs=out_specs, out_shape=out_shape,
        input_output_aliases=aliases or {},
        compiler_params=_params("arbitrary"))(*[r[0] for r in rows], *bcast)
    return res[:len(outs)], res[len(outs):]


def _rms(h, g):
    return h * lax.rsqrt(jnp.mean(h * h, axis=-1, keepdims=True) + EPS) * g


def _colsum(v):
    return jnp.sum(v, axis=0, keepdims=True)


def _s5_gate(y0, q, z):
    yg = jax.nn.gelu(y0)
    return yg * jax.nn.sigmoid(q) * jax.nn.silu(z)


def _pool_gate(o, z, scale):
    return o * scale * jax.nn.silu(z)


def _shift_rows(v, d, down):
    t = v.shape[0]
    row = lax.broadcasted_iota(jnp.int32, v.shape, 0)
    if down:
        return jnp.where(row >= d, pltpu.roll(v, d, 0), 0.0)
    return jnp.where(row < t - d, pltpu.roll(v, t - d, 0), 0.0)


def _cmul(ar, ai, br, bi):
    return ar * br - ai * bi, ar * bi + ai * br


def _scan(xr, xi, ar, ai, cr, ci, down):
    t, n = xr.shape
    nb = t // SUBLANES
    pw = [(ar, ai)]
    for _ in range(SUBLANES - 1):
        pw.append(_cmul(*pw[-1], ar, ai))
    sub = lax.broadcasted_iota(jnp.int32, (SUBLANES, n), 0)

    def table(exps):
        tr, ti = jnp.zeros((SUBLANES, n), F32), jnp.zeros((SUBLANES, n), F32)
        for r, e in enumerate(exps):
            if e:
                tr, ti = jnp.where(sub == r, pw[e - 1][0], tr), jnp.where(sub == r, pw[e - 1][1], ti)
        return tr[None], ti[None]

    sr, si = xr.reshape(nb, SUBLANES, n), xi.reshape(nb, SUBLANES, n)
    d = 1
    while d < SUBLANES:
        mr, mi = table([d if (r >= d if down else r < SUBLANES - d) else 0 for r in range(SUBLANES)])
        turn = d if down else SUBLANES - d
        hr, hi = pltpu.roll(sr, turn, 1), pltpu.roll(si, turn, 1)
        sr, si = sr + mr * hr - mi * hi, si + mr * hi + mi * hr
        d *= 2
    edge = SUBLANES - 1 if down else 0
    qr, qi = table([r + 1 if down else SUBLANES - r for r in range(SUBLANES)])
    qr, qi = qr[0], qi[0]
    in_r, in_i = jnp.broadcast_to(cr, (SUBLANES, n)), jnp.broadcast_to(ci, (SUBLANES, n))
    out_r, out_i = [None] * nb, [None] * nb
    for b in (range(nb) if down else reversed(range(nb))):
        out_r[b] = sr[b] + qr * in_r - qi * in_i
        out_i[b] = si[b] + qr * in_i + qi * in_r
        in_r = jnp.broadcast_to(out_r[b][edge:edge + 1, :], (SUBLANES, n))
        in_i = jnp.broadcast_to(out_i[b][edge:edge + 1, :], (SUBLANES, n))
    return jnp.concatenate(out_r, axis=0), jnp.concatenate(out_i, axis=0), in_r[0:1, :], in_i[0:1, :]


def s5_fwd(uz, bblk, cblk, ar, ai, dskip, *, name, job=None):
    lp = uz.shape[0]
    t = SCAN_ROWS
    nst = S5_GROUPS * S5_STATE

    def body(u_ref, b_ref, c_ref, ar_ref, ai_ref, d_ref, y_ref, yg_ref, sr_ref, si_ref, cr, ci):
        step = pl.program_id(1)

        @pl.when(step == 0)
        def _():
            cr[...] = jnp.zeros_like(cr)
            ci[...] = jnp.zeros_like(ci)

        u = u_ref[...]
        a_r, a_i = ar_ref[...], ai_ref[...]
        x = jnp.dot(u.astype(BF16), b_ref[...].astype(BF16), preferred_element_type=F32)
        sr, si, cr[...], ci[...] = _scan(x[:, :STATE_BLOCK], x[:, STATE_BLOCK:], a_r, a_i, cr[...], ci[...], True)
        sr_ref[...] = sr
        si_ref[...] = si
        s = jnp.concatenate([sr, si], axis=1).astype(BF16)
        y0 = lax.dot_general(s, c_ref[...].astype(BF16), (((1,), (1,)), ((), ())),
                             preferred_element_type=F32) + d_ref[...] * u
        y_ref[...] = y0
        yg_ref[...] = jax.nn.gelu(y0).astype(yg_ref.dtype)

    return _call(
        body, name=name, grid=(S5_BLOCKS, lp // t),
        in_specs=[pl.BlockSpec((t, LANE_BLOCK), lambda b, c: (c, b)),
                  pl.BlockSpec((None, LANE_BLOCK, 2 * STATE_BLOCK), lambda b, c: (b, 0, 0)),
                  pl.BlockSpec((None, LANE_BLOCK, 2 * STATE_BLOCK), lambda b, c: (b, 0, 0)),
                  pl.BlockSpec((1, STATE_BLOCK), lambda b, c: (0, b)),
                  pl.BlockSpec((1, STATE_BLOCK), lambda b, c: (0, b)),
                  pl.BlockSpec((1, LANE_BLOCK), lambda b, c: (0, b))],
        out_specs=[pl.BlockSpec((t, LANE_BLOCK), lambda b, c: (c, b)),
                   pl.BlockSpec((t, LANE_BLOCK), lambda b, c: (c, b)),
                   pl.BlockSpec((t, STATE_BLOCK), lambda b, c: (c, b)),
                   pl.BlockSpec((t, STATE_BLOCK), lambda b, c: (c, b))],
        out_shape=[jax.ShapeDtypeStruct((lp, D_MODEL), F32), jax.ShapeDtypeStruct((lp, D_MODEL), BF16),
                   jax.ShapeDtypeStruct((lp, nst), F32), jax.ShapeDtypeStruct((lp, nst), F32)],
        scratch=[pltpu.VMEM((1, STATE_BLOCK), F32), pltpu.VMEM((1, STATE_BLOCK), F32)],
        args=(uz, bblk, cblk, ar, ai, dskip), semantics=("parallel", "arbitrary"), job=job)


def s5_bwd(dy_direct, dyg, y0, uz, sr, si, bblk, cblk, ar, ai, dskip, dp, *, name, job=None):
    lp = uz.shape[0]
    t = SCAN_ROWS
    nch = lp // t
    nst = S5_GROUPS * S5_STATE

    def body(dyd_ref, dyg_ref, y0_ref, u_ref, sr_ref, si_ref, b_ref, c_ref, ar_ref, ai_ref, d_ref, _, du_ref,
             db_ref, dc_ref, dar_ref, dai_ref, dd_ref, cr, ci):
        step = pl.program_id(1)

        @pl.when(step == 0)
        def _():
            cr[...] = jnp.zeros_like(cr)
            ci[...] = jnp.zeros_like(ci)

        dy = dyd_ref[...] + jax.vjp(jax.nn.gelu, y0_ref[...])[1](dyg_ref[...])[0]
        u = u_ref[...]
        dyb, ub = dy.astype(BF16), u.astype(BF16)
        a_r, a_i = ar_ref[...], -ai_ref[...]
        g = jnp.dot(dyb, c_ref[...].astype(BF16), preferred_element_type=F32)
        gr, gi = g[:, :STATE_BLOCK], g[:, STATE_BLOCK:]
        nxt_r, nxt_i = cr[...], ci[...]
        last = lax.broadcasted_iota(jnp.int32, gr.shape, 0) == t - 1
        lr, li, cr[...], ci[...] = _scan(gr, gi, a_r, a_i, nxt_r, nxt_i, False)
        nr = _shift_rows(lr, 1, False) + jnp.where(last, nxt_r, 0.0)
        ni = _shift_rows(li, 1, False) + jnp.where(last, nxt_i, 0.0)
        s_r, s_i = sr_ref[...], si_ref[...]
        dar = _colsum(s_r * nr + s_i * ni)
        dai = _colsum(s_r * ni - s_i * nr)
        lam = jnp.concatenate([lr, li], axis=1).astype(BF16)
        sb = jnp.concatenate([s_r, s_i], axis=1).astype(BF16)
        dc = lax.dot_general(dyb, sb, (((0,), (0,)), ((), ())), preferred_element_type=F32)
        db = lax.dot_general(ub, lam, (((0,), (0,)), ((), ())), preferred_element_type=F32)
        du_ref[...] = (lax.dot_general(lam, b_ref[...].astype(BF16), (((1,), (1,)), ((), ())),
                                       preferred_element_type=F32) + d_ref[...] * dy).astype(du_ref.dtype)
        dd = _colsum(dy * u)

        @pl.when(step == 0)
        def _():
            db_ref[...] = db
            dc_ref[...] = dc
            dar_ref[...] = dar
            dai_ref[...] = dai
            dd_ref[...] = dd

        @pl.when(step > 0)
        def _():
            db_ref[...] += db
            dc_ref[...] += dc
            dar_ref[...] += dar
            dai_ref[...] += dai
            dd_ref[...] += dd

    rev = lambda b, c: (nch - 1 - c, b)
    return _call(
        body, name=name, grid=(S5_BLOCKS, nch),
        in_specs=[pl.BlockSpec((t, LANE_BLOCK), rev), pl.BlockSpec((t, LANE_BLOCK), rev),
                  pl.BlockSpec((t, LANE_BLOCK), rev), pl.BlockSpec((t, LANE_BLOCK), rev),
                  pl.BlockSpec((t, STATE_BLOCK), rev), pl.BlockSpec((t, STATE_BLOCK), rev),
                  pl.BlockSpec((None, LANE_BLOCK, 2 * STATE_BLOCK), lambda b, c: (b, 0, 0)),
                  pl.BlockSpec((None, LANE_BLOCK, 2 * STATE_BLOCK), lambda b, c: (b, 0, 0)),
                  pl.BlockSpec((1, STATE_BLOCK), lambda b, c: (0, b)),
                  pl.BlockSpec((1, STATE_BLOCK), lambda b, c: (0, b)),
                  pl.BlockSpec((1, LANE_BLOCK), lambda b, c: (0, b)),
                  pl.BlockSpec(memory_space=pl.ANY)],
        out_specs=[pl.BlockSpec((t, LANE_BLOCK), rev),
                   pl.BlockSpec((None, LANE_BLOCK, 2 * STATE_BLOCK), lambda b, c: (b, 0, 0)),
                   pl.BlockSpec((None, LANE_BLOCK, 2 * STATE_BLOCK), lambda b, c: (b, 0, 0)),
                   pl.BlockSpec((1, STATE_BLOCK), lambda b, c: (0, b)),
                   pl.BlockSpec((1, STATE_BLOCK), lambda b, c: (0, b)),
                   pl.BlockSpec((1, LANE_BLOCK), lambda b, c: (0, b))],
        out_shape=[jax.ShapeDtypeStruct(dp.shape, dp.dtype),
                   jax.ShapeDtypeStruct((S5_BLOCKS, LANE_BLOCK, 2 * STATE_BLOCK), F32),
                   jax.ShapeDtypeStruct((S5_BLOCKS, LANE_BLOCK, 2 * STATE_BLOCK), F32),
                   jax.ShapeDtypeStruct((1, nst), F32), jax.ShapeDtypeStruct((1, nst), F32),
                   jax.ShapeDtypeStruct((1, D_MODEL), F32)],
        scratch=[pltpu.VMEM((1, STATE_BLOCK), F32), pltpu.VMEM((1, STATE_BLOCK), F32)],
        aliases={11: 0}, args=(dy_direct, dyg, y0, uz, sr, si, bblk, cblk, ar, ai, dskip, dp),
        semantics=("parallel", "arbitrary"), job=job)


def _s5_prep(lam_re, lam_im, log_dt, b_re, b_im, c_re, c_im):
    dt = jnp.exp(log_dt)
    mag = jnp.exp(lam_re * dt)
    ar = mag * jnp.cos(lam_im * dt)
    ai = mag * jnp.sin(lam_im * dt)
    den = lam_re * lam_re + lam_im * lam_im
    kr = ((ar - 1.0) * lam_re + ai * lam_im) / den
    ki = (ai * lam_re - (ar - 1.0) * lam_im) / den
    bbr = kr[:, None, :] * b_re - ki[:, None, :] * b_im
    bbi = kr[:, None, :] * b_im + ki[:, None, :] * b_re
    rows = S5_GROUPS * S5_GROUP
    expand = (lax.broadcasted_iota(jnp.int32, (S5_STATE, STATE_BLOCK), 1) % S5_STATE
              == lax.broadcasted_iota(jnp.int32, (S5_STATE, STATE_BLOCK), 0)).astype(F32)
    own = ((lax.broadcasted_iota(jnp.int32, (rows, STATE_BLOCK), 0) // S5_GROUP) % GROUPS_PER_BLOCK
           == lax.broadcasted_iota(jnp.int32, (rows, STATE_BLOCK), 1) // S5_STATE).astype(F32)

    def blocks(v):
        wide = jnp.dot(v.reshape(rows, S5_STATE), expand, precision=lax.Precision.HIGHEST,
                       preferred_element_type=F32)
        return (wide * own).reshape(S5_BLOCKS, LANE_BLOCK, STATE_BLOCK)

    btab = jnp.concatenate([blocks(bbr), blocks(bbi)], axis=2)
    ctab = jnp.concatenate([blocks(c_re), -blocks(c_im)], axis=2)
    return ar, ai, btab, ctab


_S5_PREP_OUT = [(S5_GROUPS, S5_STATE), (S5_GROUPS, S5_STATE), (S5_BLOCKS, LANE_BLOCK, 2 * STATE_BLOCK),
                (S5_BLOCKS, LANE_BLOCK, 2 * STATE_BLOCK)]


def s5_prep(params, *, name):
    def body(*refs):
        for ref, val in zip(refs[7:], _s5_prep(*[r[...] for r in refs[:7]]), strict=True):
            ref[...] = val

    return pl.pallas_call(body, name=name, out_shape=[jax.ShapeDtypeStruct(s, F32) for s in _S5_PREP_OUT],
                          compiler_params=pltpu.CompilerParams(vmem_limit_bytes=VMEM_LIMIT))(*params)


def s5_prep_bwd(params, cotangents, *, name):
    def body(*refs):
        grads = jax.vjp(_s5_prep, *[r[...] for r in refs[:7]])[1](tuple(r[...] for r in refs[7:11]))
        for ref, val in zip(refs[11:], grads, strict=True):
            ref[...] = val

    return pl.pallas_call(body, name=name, out_shape=[jax.ShapeDtypeStruct(p.shape, F32) for p in params],
                          compiler_params=pltpu.CompilerParams(vmem_limit_bytes=VMEM_LIMIT))(*params, *cotangents)


def _silu_grad(z):
    s = jax.nn.sigmoid(z)
    return s * (1.0 + z * (1.0 - s))


def conv_fwd(p, conv_w, conv_b, *, name):
    lp = p.shape[0]
    tr = lp // 16
    e = CONV_E
    hb = tr // 8

    def body(p_ref, cgh_ref, vh_ref, w_ref, b_ref, o_ref):
        i = pl.program_id(0)
        bg, cg, v, z = (p_ref[:, k * e:(k + 1) * e] for k in range(4))
        hc = cg * v
        halo = jnp.where(i > 0, cgh_ref[...] * vh_ref[...], 0.0)
        ext = jnp.concatenate([halo, hc], axis=0)
        w = w_ref[...]
        conv = (w[0:1] * pltpu.roll(ext, 2, 0)[8:] + w[1:2] * pltpu.roll(ext, 1, 0)[8:] + w[2:3] * hc
                + b_ref[...])
        o_ref[...] = (bg * conv * jax.nn.silu(z)).astype(o_ref.dtype)

    prev = lambda col: (lambda i: (jnp.maximum(i * hb - 1, 0), col))
    return pl.pallas_call(
        body, name=name, grid=(lp // tr,),
        in_specs=[pl.BlockSpec((tr, 4 * e), lambda i: (i, 0)),
                  pl.BlockSpec((8, e), prev(1)), pl.BlockSpec((8, e), prev(2)),
                  pl.BlockSpec((3, e), lambda i: (0, 0)), pl.BlockSpec((1, e), lambda i: (0, 0))],
        out_specs=pl.BlockSpec((tr, e), lambda i: (i, 0)),
        out_shape=jax.ShapeDtypeStruct((lp, e), BF16),
        compiler_params=_params("parallel"))(p, p, p, conv_w, conv_b)


def conv_bwd(dy3, p, conv_w, conv_b, *, name):
    lp = p.shape[0]
    tr = lp // 16
    e = CONV_E
    hb = tr // 8
    nt = lp // tr
    width = 512

    def body(dy_ref, p_ref, cgh_ref, vh_ref, dyn_ref, bgn_ref, zn_ref, w_ref, b_ref, dp_ref, dw_ref, db_ref):
        i = pl.program_id(0)
        for c0 in range(0, e, width):
            cols = slice(c0, c0 + width)
            dy = dy_ref[:, cols]
            bg, cg, v, z = (p_ref[:, k * e + c0:k * e + c0 + width] for k in range(4))
            w = w_ref[:, cols]
            hc = cg * v
            halo = jnp.where(i > 0, cgh_ref[:, cols] * vh_ref[:, cols], 0.0)
            ext = jnp.concatenate([halo, hc], axis=0)
            hc2, hc1 = pltpu.roll(ext, 2, 0)[8:], pltpu.roll(ext, 1, 0)[8:]
            yc = w[0:1] * hc2 + w[1:2] * hc1 + w[2:3] * hc + b_ref[:, cols]
            sz = jax.nn.silu(z)
            dyc = dy * bg * sz
            dyc_n = jnp.where(i < nt - 1, dyn_ref[:, cols] * bgn_ref[:, cols] * jax.nn.silu(zn_ref[:, cols]), 0.0)
            ext_n = jnp.concatenate([dyc, dyc_n], axis=0)
            n_rows = tr + 8
            dhc = (w[2:3] * dyc + w[1:2] * pltpu.roll(ext_n, n_rows - 1, 0)[:tr]
                   + w[0:1] * pltpu.roll(ext_n, n_rows - 2, 0)[:tr])
            for k, val in enumerate((dy * yc * sz, dhc * v, dhc * cg, dy * bg * yc * _silu_grad(z))):
                dp_ref[:, k * e + c0:k * e + c0 + width] = val.astype(dp_ref.dtype)
            dw = jnp.concatenate([_colsum(dyc * hc2), _colsum(dyc * hc1), _colsum(dyc * hc)], axis=0)
            db = _colsum(dyc)

            @pl.when(i == 0)
            def _(cols=cols, dw=dw, db=db):
                dw_ref[:, cols] = dw
                db_ref[:, cols] = db

            @pl.when(i > 0)
            def _(cols=cols, dw=dw, db=db):
                dw_ref[:, cols] += dw
                db_ref[:, cols] += db

    prev = lambda col: (lambda i: (jnp.maximum(i * hb - 1, 0), col))
    nxt = lambda col: (lambda i: (jnp.minimum((i + 1) * hb, lp // 8 - 1), col))
    return pl.pallas_call(
        body, name=name, grid=(nt,),
        in_specs=[pl.BlockSpec((tr, e), lambda i: (i, 0)), pl.BlockSpec((tr, 4 * e), lambda i: (i, 0)),
                  pl.BlockSpec((8, e), prev(1)), pl.BlockSpec((8, e), prev(2)),
                  pl.BlockSpec((8, e), nxt(0)), pl.BlockSpec((8, e), nxt(0)), pl.BlockSpec((8, e), nxt(3)),
                  pl.BlockSpec((3, e), lambda i: (0, 0)), pl.BlockSpec((1, e), lambda i: (0, 0))],
        out_specs=[pl.BlockSpec((tr, 4 * e), lambda i: (i, 0)), pl.BlockSpec((3, e), lambda i: (0, 0)),
                   pl.BlockSpec((1, e), lambda i: (0, 0))],
        out_shape=[jax.ShapeDtypeStruct((lp, 4 * e), BF16), jax.ShapeDtypeStruct((3, e), F32),
                   jax.ShapeDtypeStruct((1, e), F32)],
        compiler_params=_params("arbitrary"))(dy3, p, p, p, dy3, p, p, conv_w, conv_b)


def _pool_counts(i, tr, rows, offset, w):
    t = (i * tr + offset + 1 + lax.broadcasted_iota(jnp.int32, (rows, 1), 0)).astype(F32)
    return jnp.minimum(t, float(w))


def pool_fwd(p, *, name):
    lp = p.shape[0]
    tr = lp // 16
    h = POOL_HALO
    hb = tr // h

    def body(u_ref, uh_ref, o_ref):
        i = pl.program_id(0)
        u = u_ref[...]
        ext = jnp.concatenate([jnp.where(i > 0, uh_ref[...], 0.0), u], axis=0)
        for k, w in enumerate(POOL_WINDOWS):
            cols = slice(k * POOL_GROUP, (k + 1) * POOL_GROUP)
            s = ext[:, cols]
            d = 1
            while d < w:
                s = s + pltpu.roll(s, d, 0)
                d *= 2
            o_ref[:, cols] = (s[h:] / _pool_counts(i, tr, tr, 0, w) - u[:, cols]).astype(o_ref.dtype)

    return pl.pallas_call(
        body, name=name, grid=(lp // tr,),
        in_specs=[pl.BlockSpec((tr, POOL_E), lambda i: (i, 0)),
                  pl.BlockSpec((h, POOL_E), lambda i: (jnp.maximum(i * hb - 1, 0), 0))],
        out_specs=pl.BlockSpec((tr, POOL_E), lambda i: (i, 0)),
        out_shape=jax.ShapeDtypeStruct((lp, POOL_E), BF16),
        compiler_params=_params("parallel"))(p, p)


def pool_bwd(dmix, dp, *, name):
    lp = dmix.shape[0]
    tr = lp // 16
    h = POOL_HALO
    hb = tr // h
    nt = lp // tr

    def body(dm_ref, dmn_ref, _, du_ref):
        i = pl.program_id(0)
        dm = dm_ref[...]
        dmn = jnp.where(i < nt - 1, dmn_ref[...], 0.0)
        n_rows = tr + h
        for k, w in enumerate(POOL_WINDOWS):
            cols = slice(k * POOL_GROUP, (k + 1) * POOL_GROUP)
            s = jnp.concatenate([dm[:, cols] / _pool_counts(i, tr, tr, 0, w),
                                 dmn[:, cols] / _pool_counts(i, tr, h, tr, w)], axis=0)
            d = 1
            while d < w:
                s = s + pltpu.roll(s, n_rows - d, 0)
                d *= 2
            du_ref[:, cols] = (s[:tr] - dm[:, cols]).astype(du_ref.dtype)

    return pl.pallas_call(
        body, name=name, grid=(nt,),
        in_specs=[pl.BlockSpec((tr, POOL_E), lambda i: (i, 0)),
                  pl.BlockSpec((h, POOL_E), lambda i: (jnp.minimum((i + 1) * hb, lp // h - 1), 0)),
                  pl.BlockSpec(memory_space=pl.ANY)],
        out_specs=pl.BlockSpec((tr, POOL_E), lambda i: (i, 0)),
        out_shape=jax.ShapeDtypeStruct(dp.shape, dp.dtype),
        input_output_aliases={2: 0},
        compiler_params=_params("parallel"))(dmix, dmix, dp)


def _adamw_math(w, g, m, v):
    nm = ADAM_B1 * m + (1.0 - ADAM_B1) * g
    nv = ADAM_B2 * v + (1.0 - ADAM_B2) * jnp.square(g)
    m_hat = nm / (1.0 - ADAM_B1 ** ADAM_STEP)
    v_hat = nv / (1.0 - ADAM_B2 ** ADAM_STEP)
    return -ADAM_LR * (m_hat / (jnp.sqrt(v_hat) + ADAM_EPS) + ADAM_WD * w), nm, nv


def adamw(w, g, m, v, *, name):
    r, c = w.shape
    tr = _tile(r, max(8, (1 << 19) // c), 8)

    def body(w_ref, g_ref, m_ref, v_ref, d_ref, nm_ref, nv_ref):
        d_ref[...], nm_ref[...], nv_ref[...] = _adamw_math(w_ref[...], g_ref[...], m_ref[...], v_ref[...])

    spec = pl.BlockSpec((tr, c), lambda i: (i, 0))
    return pl.pallas_call(
        body, name=name, grid=(r // tr,), in_specs=[spec] * 4, out_specs=[spec] * 3,
        out_shape=[jax.ShapeDtypeStruct((r, c), F32)] * 3,
        compiler_params=_params("parallel"))(w, g, m, v)


def adamw_many(quads, *, name):
    shape = quads[0][0].shape
    steps = 8
    spec = pl.BlockSpec((shape[0] // steps,) + shape[1:], lambda i: (i, 0, 0))
    n = len(quads)

    def body(*refs):
        for q in range(n):
            w_ref, g_ref, m_ref, v_ref = refs[4 * q:4 * q + 4]
            d_ref, nm_ref, nv_ref = refs[4 * n + 3 * q:4 * n + 3 * q + 3]
            d_ref[...], nm_ref[...], nv_ref[...] = _adamw_math(w_ref[...], g_ref[...], m_ref[...], v_ref[...])

    res = pl.pallas_call(
        body, name=name, grid=(steps,), in_specs=[spec] * (4 * n), out_specs=[spec] * (3 * n),
        out_shape=[jax.ShapeDtypeStruct(shape, F32)] * (3 * n),
        compiler_params=_params("parallel"))(*[a for quad in quads for a in quad])
    return [tuple(res[3 * q:3 * q + 3]) for q in range(n)]


WIRE_WIDTH = 1024
WIRE_ROWS = 1024


def _wire_plan(shapes):
    starts, row = [], 0
    for s in shapes:
        r, c = (1, s[0]) if len(s) == 1 else s
        starts.append(row)
        row += -(-(r * max(1, c // WIRE_WIDTH)) // 8) * 8
    assert row <= WIRE_ROWS, row
    return starts


def _wire_cells(shape):
    if len(shape) == 1:
        n = shape[0]
        if n <= WIRE_WIDTH:
            return [(0, n, (slice(None),))]
        return [(h, WIRE_WIDTH, (slice(h * WIRE_WIDTH, (h + 1) * WIRE_WIDTH),)) for h in range(n // WIRE_WIDTH)]
    r, c = shape
    if c <= WIRE_WIDTH:
        return [(None, c, (slice(None), slice(None)))]
    per = c // WIRE_WIDTH
    return [(j * per + h, WIRE_WIDTH, (j, slice(h * WIRE_WIDTH, (h + 1) * WIRE_WIDTH)))
            for j in range(r) for h in range(per)]


def pack_small(arrays, *, name):
    shapes = [a.shape for a in arrays]
    starts = _wire_plan(shapes)

    def body(*refs):
        out = refs[-1]
        out[...] = jnp.zeros_like(out)
        for ref, shape, row0 in zip(refs[:-1], shapes, starts, strict=True):
            for off, cols, idx in _wire_cells(shape):
                if off is None:
                    out[row0:row0 + shape[0], 0:cols] = ref[idx]
                else:
                    out[row0 + off, 0:cols] = ref[idx]

    return pl.pallas_call(body, name=name, out_shape=jax.ShapeDtypeStruct((WIRE_ROWS, WIRE_WIDTH), F32),
                          compiler_params=pltpu.CompilerParams(vmem_limit_bytes=VMEM_LIMIT))(*arrays)


def update_small(packed, starts, triples, shard_cells, chip, *, name):
    n = len(triples)

    def body(chip_ref, p_ref, *refs):
        k = chip_ref[0]
        ins, outs = refs[:3 * n], refs[3 * n:]

        def cut(row0, rows, cols):
            c0, c1 = cols
            if isinstance(rows, int):
                return p_ref[row0 + rows, c0:c1]
            return p_ref[row0 + rows[0]:row0 + rows[1], c0:c1]

        for q in range(n):
            w_ref, m_ref, v_ref = ins[3 * q:3 * q + 3]
            g_ref, d_ref, nm_ref, nv_ref = outs[4 * q:4 * q + 4]
            shape = w_ref.shape
            if shard_cells[q] is None:
                pieces = [(cut(starts[q], (0, shape[0]) if off is None else off, (0, cols)), idx)
                          for off, cols, idx in _wire_cells(shape)]
            else:
                by_chip = [shard_cells[q](kk) for kk in range(N_CHIPS)]
                pieces = []
                for i, (_, _, idx) in enumerate(by_chip[0]):
                    g = cut(starts[q], *by_chip[0][i][:2])
                    for kk in range(1, N_CHIPS):
                        g = jnp.where(k == kk, cut(starts[q], *by_chip[kk][i][:2]), g)
                    pieces.append((g, idx))
            for g, idx in pieces:
                g_ref[idx] = g
                d_ref[idx], nm_ref[idx], nv_ref[idx] = _adamw_math(w_ref[idx], g, m_ref[idx], v_ref[idx])

    whole = lambda a: pl.BlockSpec(a.shape, lambda i, c_ref, nd=a.ndim: (0,) * nd)
    flat = [a for t in triples for a in t]
    out_shape = [jax.ShapeDtypeStruct(t[0].shape, F32) for t in triples for _ in range(4)]
    res = pl.pallas_call(
        body, name=name,
        grid_spec=pltpu.PrefetchScalarGridSpec(
            num_scalar_prefetch=1, grid=(1,),
            in_specs=[whole(packed)] + [whole(a) for a in flat],
            out_specs=[pl.BlockSpec(s.shape, lambda i, c_ref, nd=len(s.shape): (0,) * nd) for s in out_shape]),
        out_shape=out_shape,
        compiler_params=_params("arbitrary"))(chip.reshape(1).astype(jnp.int32), packed, *flat)
    return [tuple(res[4 * q:4 * q + 4]) for q in range(n)]


ANY = pl.BlockSpec(memory_space=pl.ANY)


def _place():
    x, y, c = lax.axis_index("x"), lax.axis_index("y"), lax.axis_index("c")
    return x, y, c, [(1 - x, y), (x, 1 - y), (1 - x, 1 - y)]


DMA_PIECE_BYTES = 512 * 1024


def _pieces(src, dst):
    shape = src.shape
    rows, cols = shape[-2], shape[-1]
    item = jnp.dtype(src.dtype).itemsize
    unit = 32 // item
    want = max(1, (rows * cols * item) // DMA_PIECE_BYTES)
    n = 1
    if rows % unit == 0:
        n = max(d for d in range(1, rows // unit + 1) if (rows // unit) % d == 0 and d <= want)
    size = rows // n
    out = []
    for lead in (range(shape[0]) if len(shape) == 3 else [None]):
        for i in range(n):
            sel = (pl.ds(i * size, size), slice(None)) if lead is None else (lead, pl.ds(i * size, size), slice(None))
            out.append((src.at[sel], dst.at[sel]))
    return out


def _send(src, dst, send_sem, recv_sem, peer, start=True):
    if start:
        for s, d in _pieces(src, dst):
            pltpu.make_async_remote_copy(src_ref=s, dst_ref=d, send_sem=send_sem, recv_sem=recv_sem,
                                         device_id=peer, device_id_type=MESH).start()
    return pltpu.make_async_remote_copy(src_ref=src, dst_ref=dst, send_sem=send_sem, recv_sem=recv_sem,
                                        device_id=peer, device_id_type=MESH)


def run_job(job, *, name):
    n_in, n_out = len(job.arrays), len(job.out_shapes)

    def body(*refs):
        job.fn(refs[:n_in], refs[n_in:n_in + n_out], refs[n_in + n_out:], "both")

    return pl.pallas_call(body, name=name, in_specs=[ANY] * n_in, out_specs=[ANY] * n_out,
                          out_shape=job.out_shapes, scratch_shapes=job.sems)(*job.arrays)


def gather_chips(shards, *, name):
    n = len(shards)

    def body(*refs):
        ins, outs = refs[:n], refs[n:2 * n]
        send_sems, recv_sems = refs[2 * n:]
        x, y, c, chips = _place()
        k = 2 * x + y
        copies = []
        for a in range(n):
            for j, peer in enumerate([(px, py, c) for px, py in chips] + [(x, y, 1 - c)]):
                copies.append(_send(ins[a], outs[a].at[k], send_sems.at[a, j], recv_sems.at[a, j], peer))
        for cp in copies:
            cp.wait()

    return pl.pallas_call(
        body, name=name, in_specs=[ANY] * n, out_specs=[ANY] * n,
        out_shape=[jax.ShapeDtypeStruct((N_CHIPS,) + s.shape, s.dtype) for s in shards],
        scratch_shapes=[pltpu.SemaphoreType.DMA((n, 4)), pltpu.SemaphoreType.DMA((n, 4))])(*shards)


def gather_by_halves_job(shards):
    n = len(shards)

    def fn(ins, outs, sems, phase):
        send_sems, recv_sems = sems
        x, y, c, chips = _place()
        k = 2 * x + y
        sibling = (x, y, 1 - c)
        begin = phase != "finish"
        waits, arrivals = [], []
        for a in range(n):
            half = ins[a].shape[0] // 2
            waits.append(_send(ins[a], outs[a].at[k], send_sems.at[a, 6], recv_sems.at[a, 6], sibling, begin))
            mine = pl.ds(c * half, half)
            for j, (px, py) in enumerate(chips):
                arrivals.append(_send(ins[a].at[mine, :], outs[a].at[k, mine, :], send_sems.at[a, j],
                                      recv_sems.at[a, j], (px, py, c), begin))
        if phase == "start":
            return
        i = 0
        for a in range(n):
            half = ins[a].shape[0] // 2
            mine = pl.ds(c * half, half)
            for j, (px, py) in enumerate(chips):
                arrivals[i].wait_recv()
                landed = outs[a].at[2 * px + py, mine, :]
                waits.append(_send(landed, landed, send_sems.at[a, 3 + j], recv_sems.at[a, 3 + j], sibling))
                i += 1
        for cp in arrivals:
            cp.wait_send()
        for cp in waits:
            cp.wait()

    return Job(shards, [jax.ShapeDtypeStruct((N_CHIPS,) + s.shape, s.dtype) for s in shards],
               [pltpu.SemaphoreType.DMA((n, 7)), pltpu.SemaphoreType.DMA((n, 7))], fn)


def gather_halves_job(shards):
    n = len(shards)

    def fn(ins, outs, sems, phase):
        send_sems, recv_sems = sems
        x, y, c, chips = _place()
        k = 2 * x + y
        copies = []
        for a in range(n):
            half = ins[a].shape[0] // 2
            mine = pl.ds(c * half, half)
            copies.append(_send(ins[a], outs[a].at[k], send_sems.at[a, 3], recv_sems.at[a, 3], (x, y, 1 - c),
                                phase != "finish"))
            for j, (px, py) in enumerate(chips):
                copies.append(_send(ins[a].at[mine, :], outs[a].at[k, mine, :], send_sems.at[a, j],
                                    recv_sems.at[a, j], (px, py, c), phase != "finish"))
        if phase != "start":
            for cp in copies:
                cp.wait()

    return Job(shards, [jax.ShapeDtypeStruct((N_CHIPS,) + s.shape, s.dtype) for s in shards],
               [pltpu.SemaphoreType.DMA((n, 4)), pltpu.SemaphoreType.DMA((n, 4))], fn)


def forward_halves_job(gathered):
    n = len(gathered)

    def fn(ins, outs, sems, phase):
        send_sems, recv_sems = sems
        x, y, c, chips = _place()
        copies = []
        for a in range(n):
            half = outs[a].shape[1] // 2
            for j, (px, py) in enumerate(chips):
                landed = outs[a].at[2 * px + py, pl.ds(c * half, half), :]
                copies.append(_send(landed, landed, send_sems.at[a, j], recv_sems.at[a, j], (x, y, 1 - c),
                                    phase != "finish"))
        if phase != "start":
            for cp in copies:
                cp.wait()

    return Job(gathered, [jax.ShapeDtypeStruct(g.shape, g.dtype) for g in gathered],
               [pltpu.SemaphoreType.DMA((n, 3)), pltpu.SemaphoreType.DMA((n, 3))], fn, in_place=True)


def swap_job(grads):
    n = len(grads)

    def fn(ins, outs, sems, phase):
        send_sems, recv_sems = sems
        x, y, c, _ = _place()
        copies = []
        for a in range(n):
            half = ins[a].shape[1] // 2
            copies.append(_send(ins[a].at[:, pl.ds((1 - c) * half, half), :], outs[a], send_sems.at[a],
                                recv_sems.at[a], (x, y, 1 - c), phase != "finish"))
        if phase != "start":
            for cp in copies:
                cp.wait()

    return Job(grads, [jax.ShapeDtypeStruct((g.shape[0], g.shape[1] // 2, g.shape[2]), g.dtype) for g in grads],
               [pltpu.SemaphoreType.DMA((n,)), pltpu.SemaphoreType.DMA((n,))], fn)


def _chip_slot(s, k):
    rel = s ^ k
    return (rel >> 1) | ((rel & 1) << 1)


def sum_cores(g, theirs, ck, out_dtype, *, name):
    n, r, cols = g.shape
    half = r // 2
    tr = _tile(half, max(16, (1 << 19) // cols), 16)
    nb = half // tr

    def body(ck_ref, g_ref, t_ref, o_ref):
        o_ref[...] = (g_ref[...] + t_ref[...]).astype(o_ref.dtype)

    return pl.pallas_call(
        body, name=name,
        grid_spec=pltpu.PrefetchScalarGridSpec(
            num_scalar_prefetch=1, grid=(n, nb),
            in_specs=[pl.BlockSpec((None, tr, cols), lambda s, i, ck_ref: (s, ck_ref[0] * nb + i, 0)),
                      pl.BlockSpec((None, tr, cols), lambda s, i, ck_ref: (s, i, 0))],
            out_specs=pl.BlockSpec((None, tr, cols), lambda s, i, ck_ref: (_chip_slot(s, ck_ref[1]), i, 0))),
        out_shape=jax.ShapeDtypeStruct((n, half, cols), out_dtype),
        compiler_params=_params("parallel", "parallel"))(ck, g, theirs)


def scatter_job(parts):
    n = len(parts)

    def fn(ins, outs, sems, phase):
        send_sems, recv_sems = sems
        x, y, c, chips = _place()
        copies = []
        for a in range(n):
            for j, (px, py) in enumerate(chips):
                copies.append(_send(ins[a].at[1 + j], outs[a].at[j], send_sems.at[a, j], recv_sems.at[a, j],
                                    (px, py, c), phase != "finish"))
        if phase != "start":
            for cp in copies:
                cp.wait()

    return Job(parts, [jax.ShapeDtypeStruct((3,) + p.shape[1:], p.dtype) for p in parts],
               [pltpu.SemaphoreType.DMA((n, 3)), pltpu.SemaphoreType.DMA((n, 3))], fn)


def sum_chips(own, others, ck, *, name):
    _, r, cols = own.shape
    tr = _tile(r, max(16, (1 << 19) // cols), 16)

    def body(ck_ref, a_ref, b0_ref, b1_ref, b2_ref, o_ref):
        o_ref[...] = (a_ref[...].astype(F32) + b0_ref[...].astype(F32) + b1_ref[...].astype(F32)
                      + b2_ref[...].astype(F32))

    other = lambda j: pl.BlockSpec((None, tr, cols), lambda i, ck_ref: (j, i, 0))
    return pl.pallas_call(
        body, name=name,
        grid_spec=pltpu.PrefetchScalarGridSpec(
            num_scalar_prefetch=1, grid=(r // tr,),
            in_specs=[pl.BlockSpec((None, tr, cols), lambda i, ck_ref: (0, i, 0)), other(0), other(1), other(2)],
            out_specs=pl.BlockSpec((None, tr, cols), lambda i, ck_ref: (ck_ref[0], i, 0))),
        out_shape=jax.ShapeDtypeStruct((2, r, cols), F32),
        compiler_params=_params("parallel"))(ck, own, others, others, others)


def share_halves(joined, *, name):
    n = len(joined)

    def body(*refs):
        outs = refs[n:2 * n]
        send_sems, recv_sems = refs[2 * n:]
        x, y, c, _ = _place()
        copies = [_send(outs[a].at[c], outs[a].at[c], send_sems.at[a], recv_sems.at[a], (x, y, 1 - c))
                  for a in range(n)]
        for cp in copies:
            cp.wait()

    return pl.pallas_call(
        body, name=name, in_specs=[ANY] * n, out_specs=[ANY] * n,
        out_shape=[jax.ShapeDtypeStruct(j.shape, j.dtype) for j in joined],
        input_output_aliases={a: a for a in range(n)},
        scratch_shapes=[pltpu.SemaphoreType.DMA((n,)), pltpu.SemaphoreType.DMA((n,))])(*joined)


def kernel(x, meta_tokens, norm0_g, l0_w_in, l0_lam_re, l0_lam_im, l0_log_dt, l0_b_re, l0_b_im, l0_c_re, l0_c_im, l0_d_skip, l0_w_glu, l0_b_glu, l0_w_out, norm1_g, l1_w_in, l1_conv_w, l1_conv_b, l1_w_out, norm2_g, l2_w_in, l2_w_grp, l2_b_grp, l2_scale, l2_w_out, norm3_g, l3_w_in, l3_lam_re, l3_lam_im, l3_log_dt, l3_b_re, l3_b_im, l3_c_re, l3_c_im, l3_d_skip, l3_w_glu, l3_b_glu, l3_w_out, final_g, loss_target, m_meta_tokens, m_norm0_g, m_l0_w_in, m_l0_lam_re, m_l0_lam_im, m_l0_log_dt, m_l0_b_re, m_l0_b_im, m_l0_c_re, m_l0_c_im, m_l0_d_skip, m_l0_w_glu, m_l0_b_glu, m_l0_w_out, m_norm1_g, m_l1_w_in, m_l1_conv_w, m_l1_conv_b, m_l1_w_out, m_norm2_g, m_l2_w_in, m_l2_w_grp, m_l2_b_grp, m_l2_scale, m_l2_w_out, m_norm3_g, m_l3_w_in, m_l3_lam_re, m_l3_lam_im, m_l3_log_dt, m_l3_b_re, m_l3_b_im, m_l3_c_re, m_l3_c_im, m_l3_d_skip, m_l3_w_glu, m_l3_b_glu, m_l3_w_out, m_final_g, v_meta_tokens, v_norm0_g, v_l0_w_in, v_l0_lam_re, v_l0_lam_im, v_l0_log_dt, v_l0_b_re, v_l0_b_im, v_l0_c_re, v_l0_c_im, v_l0_d_skip, v_l0_w_glu, v_l0_b_glu, v_l0_w_out, v_norm1_g, v_l1_w_in, v_l1_conv_w, v_l1_conv_b, v_l1_w_out, v_norm2_g, v_l2_w_in, v_l2_w_grp, v_l2_b_grp, v_l2_scale, v_l2_w_out, v_norm3_g, v_l3_w_in, v_l3_lam_re, v_l3_lam_im, v_l3_log_dt, v_l3_b_re, v_l3_b_im, v_l3_c_re, v_l3_c_im, v_l3_d_skip, v_l3_w_glu, v_l3_b_glu, v_l3_w_out, v_final_g):
    given = dict(locals())
    w = {n: given[n] for n in WEIGHTS}
    mom_m = {n: given["m_" + n] for n in WEIGHTS}
    mom_v = {n: given["v_" + n] for n in WEIGHTS}
    seq = x.shape[1]
    real = N_META + seq
    lp = -(-real // SCAN_ROWS) * SCAN_ROWS
    chip = 2 * lax.axis_index("x") + lax.axis_index("y")
    row = lambda a: a.reshape(1, -1)

    shard_bf16 = {n: (w[n].reshape(-1, w[n].shape[-1]) if n == 'l2_w_grp' else w[n]).astype(BF16) for n in BIG}
    layer = lambda i: [n for n in BIG if n.startswith("l%d_" % i)]
    over_ici = lambda names: gather_halves_job([shard_bf16[n] for n in names])
    full = {}

    def landed(names, arrays):
        for n, arr in zip(names, arrays, strict=True):
            if n.endswith('w_in'):
                full[n] = arr
            elif n == 'l2_w_grp':
                full[n] = arr.reshape(N_CHIPS, len(POOL_WINDOWS), POOL_GROUP // N_CHIPS, POOL_GROUP)
            else:
                full[n] = arr.reshape(1, -1, arr.shape[-1])

    landed(['l0_w_in'], run_job(gather_by_halves_job([shard_bf16['l0_w_in']]), name="gather_l0_w_in"))
    full.update(zip(SMALL_SHARDED, gather_chips([w[n] for n in SMALL_SHARDED], name="gather_small_shards"),
                    strict=True))
    meta_full = full['meta_tokens'].transpose(1, 0, 2).reshape(N_META, D_MODEL)
    conv_w_full = full['l1_conv_w'].transpose(1, 0, 2).reshape(3, CONV_E)
    b_grp_full = full['l2_b_grp'].transpose(1, 0, 2).reshape(len(POOL_WINDOWS), POOL_GROUP)

    h0 = jnp.concatenate([meta_full, x[0], jnp.zeros((lp - real, D_MODEL), F32)], axis=0)
    target = jnp.concatenate([jnp.zeros((N_META, D_MODEL), F32), loss_target[0],
                              jnp.zeros((lp - real, D_MODEL), F32)], axis=0)
    grads = {}
    saved = {}
    gip = lambda a: jnp.swapaxes(a, 1, 2)

    def s5_layer_fwd(i, h, first=False):
        pre = "l%d_" % i
        prm = [w[pre + 'lam_re'], w[pre + 'lam_im'], w[pre + 'log_dt'].reshape(-1, 1), gip(w[pre + 'b_re']),
               gip(w[pre + 'b_im']), w[pre + 'c_re'], w[pre + 'c_im']]
        a_re, a_im, bblk, cblk = s5_prep(prm, name=pre + "prep")
        tables = (a_re.reshape(1, -1), a_im.reshape(1, -1), bblk, cblk)
        ar, ai = tables[:2]
        gain = row(w["norm%d_g" % i])
        rest = [pre + 'w_glu', pre + 'w_out']
        if not first:
            uz, n = norm_mm_nn(h, gain, full[pre + 'w_in'], name=pre + "in")
            y0, yg, sr, si = s5_fwd(uz, bblk, cblk, ar, ai, row(w[pre + 'd_skip']), name=pre + "scan")
            q, y3 = glu_gate(yg, full[pre + 'w_glu'], row(w[pre + 'b_glu']), y0, uz, name=pre + "glu")
        else:
            (uz, n), halves = norm_mm_nn(h, gain, full[pre + 'w_in'], name=pre + "in", job=over_ici(rest))
            (y0, yg, sr, si), arrived = s5_fwd(
                uz, bblk, cblk, ar, ai, row(w[pre + 'd_skip']), name=pre + "scan",
                job=Job.both(over_ici(layer(1)), forward_halves_job(halves)))
            landed(rest, arrived[len(layer(1)):])
            (q, y3), whole = glu_gate(yg, full[pre + 'w_glu'], row(w[pre + 'b_glu']), y0, uz, name=pre + "glu",
                                      job=forward_halves_job(arrived[:len(layer(1))]))
            landed(layer(1), whole)
        h_new = mm_nn(y3, full[pre + 'w_out'], add=h, name=pre + "out")
        saved[i] = dict(h=h, n=n, uz=uz, y0=y0, sr=sr, si=si, yg=yg, q=q, y3=y3, tables=tables, prm=prm)
        return h_new

    def s5_layer_bwd(i, dh, carry=None):
        pre = "l%d_" % i
        s = saved[i]
        ar, ai, bblk, cblk = s['tables']
        grads[pre + 'w_out'] = mm_tn(s['y3'], dh, 1, name=pre + "d_w_out")[0]
        dy3 = mm_nt(dh, full[pre + 'w_out'], name=pre + "d_y3")

        def gate_bwd(dyv, yv, qv, zv):
            _, vjp = jax.vjp(_s5_gate, yv, qv, zv)
            dy0, dq, dz = vjp(dyv)
            return [dy0, dq, dz], [_colsum(dq)]

        (dy0_direct, dq, dp), (db_glu,) = rowwise(
            gate_bwd, [(dy3, D_MODEL, 0), (s['y0'], D_MODEL, 0), (s['q'], D_MODEL, 0), (s['uz'], D_MODEL, 1)], [],
            [(D_MODEL, F32, D_MODEL, 0), (D_MODEL, BF16, D_MODEL, 0), (2 * D_MODEL, BF16, D_MODEL, 1)], [D_MODEL],
            name=pre + "d_gate")
        grads[pre + 'b_glu'] = db_glu
        grads[pre + 'w_glu'] = mm_tn(s['yg'], dq, 1, name=pre + "d_w_glu")[0]
        dyg = mm_nt(dq, full[pre + 'w_glu'], name=pre + "d_yg")
        res = s5_bwd(dy0_direct, dyg, s['y0'], s['uz'], s['sr'], s['si'], bblk, cblk, ar, ai,
                     row(w[pre + 'd_skip']), dp, name=pre + "d_scan",
                     job=None if carry is None else scatter_job(scatter_of(carry)))
        if carry is not None:
            res, arrived = res
            scattered(carry, arrived)
        dp, dbblk, dcblk, dar, dai, dd = res
        grads[pre + 'd_skip'] = dd
        cots = (dar.reshape(S5_GROUPS, S5_STATE), dai.reshape(S5_GROUPS, S5_STATE), dbblk, dcblk)
        for key, val in zip(('lam_re', 'lam_im', 'log_dt', 'b_re', 'b_im', 'c_re', 'c_im'),
                            s5_prep_bwd(s['prm'], cots, name=pre + "d_prep"), strict=True):
            grads[pre + key] = val.reshape(-1) if key == 'log_dt' else val
        grads[pre + 'w_in'] = mm_tn(s['n'], dp, N_CHIPS, name=pre + "d_w_in")
        names, pieces = core_parts(i)
        (dh_in, dg), theirs = mm_nt_norm_bwd(dp, full[pre + 'w_in'], s['h'], row(w["norm%d_g" % i]), dh,
                                             name=pre + "d_n", job=swap_job(pieces))
        sum_over_cores(names, pieces, theirs)
        grads["norm%d_g" % i] = dg
        return dh_in

    h1 = s5_layer_fwd(0, h0, first=True)

    (p1, n1), halves = norm_mm_nn(h1, row(w['norm1_g']), full['l1_w_in'], name="l1_in", job=over_ici(layer(2)))
    y3_1 = conv_fwd(p1, conv_w_full, row(w['l1_conv_b']), name="l1_conv")
    h2, whole = mm_nn(y3_1, full['l1_w_out'], add=h1, name="l1_out", job=forward_halves_job(halves))
    landed(layer(2), whole)

    (p2, n2), halves = norm_mm_nn(h2, row(w['norm2_g']), full['l2_w_in'], name="l2_in", job=over_ici(layer(3)))
    mixed = pool_fwd(p2, name="l2_pool")
    o2 = grp_nn(mixed, full['l2_w_grp'], b_grp_full.reshape(1, -1), name="l2_grp")
    (y3_2,), _ = rowwise(lambda ov, zv, sv: ([_pool_gate(ov, zv, sv)], []),
                         [(o2, POOL_E, 0), (p2, POOL_E, 1)], [row(w['l2_scale'])],
                         [(POOL_E, BF16, POOL_E, 0)], name="l2_gate")
    h3, whole = mm_nn(y3_2, full['l2_w_out'], add=h2, name="l2_out", job=forward_halves_job(halves))
    landed(layer(3), whole)

    h4 = s5_layer_fwd(3, h3)

    def loss_fn(hv, tv, gv):
        i = pl.program_id(0)
        tr = hv.shape[0]
        yf, vjp = jax.vjp(_rms, hv, gv)
        pos = i * tr + lax.broadcasted_iota(jnp.int32, (tr, 1), 0)
        diff = jnp.where((pos >= N_META) & (pos < real), yf - tv, 0.0)
        dh, dg = vjp(diff / D_MODEL)
        return [dh], [dg, _colsum(diff * diff)]

    (dh,), (d_final_g, sq) = rowwise(loss_fn, [(h4, D_MODEL, 0), (target, D_MODEL, 0)], [row(w['final_g'])],
                                     [(D_MODEL, F32, D_MODEL, 0)], [D_MODEL, D_MODEL], name="loss")
    grads['final_g'] = d_final_g
    loss = lax.psum(0.5 / D_MODEL * jnp.sum(sq), ("x", "y", "c"))

    ck = jnp.stack([lax.axis_index("c"), chip]).astype(jnp.int32)
    pair, from_chips = {}, {}

    def chip_major(n):
        g = grads[n]
        if n.endswith('w_in'):
            return g
        if n == 'l2_w_grp':
            return g.reshape(N_CHIPS, -1, POOL_GROUP)
        return g.reshape(N_CHIPS, -1, g.shape[-1])

    def core_parts(i):
        return layer(i), [chip_major(n) for n in layer(i)]

    def sum_over_cores(names, pieces, theirs):
        for n, g, t in zip(names, pieces, theirs, strict=True):
            pair[n] = sum_cores(g, t, ck, F32 if n == 'small' else BF16, name="sum_cores_" + n)

    scatter_of = lambda i: [pair[n] for n in layer(i)]

    def scattered(i, arrays):
        from_chips.update(zip(layer(i), arrays, strict=True))

    dh = s5_layer_bwd(3, dh)

    grads['l2_w_out'] = mm_tn(y3_2, dh, 1, name="l2_d_w_out")[0]
    dy3 = mm_nt(dh, full['l2_w_out'], name="l2_d_y3")

    def pool_gate_bwd(dyv, ov, zv, sv):
        _, vjp = jax.vjp(_pool_gate, ov, zv, sv)
        do, dz, dscale = vjp(dyv)
        return [do, dz], [dscale, _colsum(do)]

    (do2, dp2), (dscale, dbgrp) = rowwise(
        pool_gate_bwd, [(dy3, POOL_E, 0), (o2, POOL_E, 0), (p2, POOL_E, 1)], [row(w['l2_scale'])],
        [(POOL_E, BF16, POOL_E, 0), (2 * POOL_E, BF16, POOL_E, 1)], [POOL_E, POOL_E], name="l2_d_gate")
    grads['l2_scale'] = dscale
    grads['l2_b_grp'] = dbgrp
    grads['l2_w_grp'] = grp_tn(mixed, do2, name="l2_d_w_grp")
    dmix = grp_nt(do2, full['l2_w_grp'], name="l2_d_mixed")
    dp2 = pool_bwd(dmix, dp2, name="l2_d_pool")
    grads['l2_w_in'], arrived = mm_tn(n2, dp2, N_CHIPS, name="l2_d_w_in", job=scatter_job(scatter_of(3)))
    scattered(3, arrived)
    names, pieces = core_parts(2)
    (dh, dg), theirs = mm_nt_norm_bwd(dp2, full['l2_w_in'], h2, row(w['norm2_g']), dh, name="l2_d_n",
                                      job=swap_job(pieces))
    sum_over_cores(names, pieces, theirs)
    grads['norm2_g'] = dg

    grads['l1_w_out'] = mm_tn(y3_1, dh, 1, name="l1_d_w_out")[0]
    dy3 = mm_nt(dh, full['l1_w_out'], name="l1_d_y3")
    dp1, dcw, dcb = conv_bwd(dy3, p1, conv_w_full, row(w['l1_conv_b']), name="l1_d_conv")
    grads['l1_conv_w'] = dcw
    grads['l1_conv_b'] = dcb
    grads['l1_w_in'], arrived = mm_tn(n1, dp1, N_CHIPS, name="l1_d_w_in", job=scatter_job(scatter_of(2)))
    scattered(2, arrived)
    names, pieces = core_parts(1)
    (dh, dg), theirs = mm_nt_norm_bwd(dp1, full['l1_w_in'], h1, row(w['norm1_g']), dh, name="l1_d_n",
                                      job=swap_job(pieces))
    sum_over_cores(names, pieces, theirs)
    grads['norm1_g'] = dg

    dh = s5_layer_bwd(0, dh, carry=1)
    grad_x = dh[N_META:real][None]
    grads['meta_tokens'] = dh[:N_META]

    wide = [n for n in SMALL if w[n].ndim == 3]
    wire = [grads[n].reshape(S5_GROUPS, -1) if n in wide else grads[n] for n in SMALL]
    starts = dict(zip(SMALL, _wire_plan([a.shape for a in wire]), strict=True))
    small_local = pack_small(wire, name="pack_small")
    pieces = [small_local.reshape(N_CHIPS, -1, WIRE_WIDTH)]
    sum_over_cores(['small'], pieces, run_job(swap_job(pieces), name="reduce_cores_small"))
    rest = layer(0) + ['small']
    from_chips.update(zip(rest, run_job(scatter_job([pair[n] for n in rest]), name="reduce_chips_l0"), strict=True))
    joined = [sum_chips(pair[n], from_chips[n], ck, name="sum_chips_" + n) for n in BIG + ['small']]
    joined = share_halves(joined, name="share_cores")
    reduced = [j.reshape(-1, j.shape[-1]) for j in joined]
    g_big = {n: reduced[i].reshape(w[n].shape) for i, n in enumerate(BIG)}
    (small_all,) = run_job(gather_by_halves_job([reduced[-1]]), name="gather_small")
    small_all = small_all.reshape(WIRE_ROWS, WIRE_WIDTH)

    out_g, out_d, out_m, out_v = {}, {}, {}, {}
    for n in BIG:
        shape = w[n].shape
        as2d = lambda a: a.reshape(-1, shape[-1])
        d, nm, nv = adamw(as2d(w[n]), as2d(g_big[n]), as2d(mom_m[n]), as2d(mom_v[n]), name="adamw_" + n)
        out_g[n], out_d[n], out_m[n], out_v[n] = g_big[n], d.reshape(shape), nm.reshape(shape), nv.reshape(shape)

    half_w = WIRE_WIDTH // 2
    shard_cells = {
        'meta_tokens': lambda k: [((0, N_META), (k * 256, (k + 1) * 256), (slice(None), slice(None)))],
        'l1_conv_w': lambda k: [(2 * j + k // 2, ((k % 2) * half_w, (k % 2 + 1) * half_w), (j, slice(None)))
                                for j in range(3)],
        'l2_b_grp': lambda k: [(j // 2, ((j % 2) * half_w + k * 128, (j % 2) * half_w + (k + 1) * 128),
                                (j, slice(None))) for j in range(len(POOL_WINDOWS))],
    }
    plain = [n for n in SMALL if n not in wide]
    res = update_small(small_all, [starts[n] for n in plain], [(w[n], mom_m[n], mom_v[n]) for n in plain],
                       [shard_cells.get(n) for n in plain], chip, name="adamw_small")
    for n, (g, d, nm, nv) in zip(plain, res, strict=True):
        out_g[n], out_d[n], out_m[n], out_v[n] = g, d, nm, nv
    as_gip = lambda n, a: gip(a) if n.endswith(('b_re', 'b_im')) else a
    g_gip = {n: small_all[starts[n]:starts[n] + S5_GROUPS].reshape(S5_GROUPS, S5_GROUP, S5_STATE) for n in wide}
    res = adamw_many([(as_gip(n, w[n]), g_gip[n], as_gip(n, mom_m[n]), as_gip(n, mom_v[n])) for n in wide],
                     name="adamw_s5_tensors")
    for n, (d, nm, nv) in zip(wide, res, strict=True):
        out_g[n], out_d[n], out_m[n], out_v[n] = as_gip(n, g_gip[n]), as_gip(n, d), as_gip(n, nm), as_gip(n, nv)

    return (loss, grad_x, *[out_g[n] for n in WEIGHTS], *[out_d[n] for n in WEIGHTS],
            *[out_m[n] for n in WEIGHTS], *[out_v[n] for n in WEIGHTS])
```

```python
import functools

import jax
import jax.numpy as jnp
from jax import lax
from jax.experimental import pallas as pl
from jax.experimental.pallas import tpu as pltpu

F32, BF16 = jnp.float32, jnp.bfloat16
MESH = pl.DeviceIdType.MESH

D_MODEL = 1024
N_META = 16
EPS = 1e-6
S5_GROUPS, S5_GROUP, S5_STATE = 64, 16, 64
LANE_BLOCK = 128
S5_BLOCKS = D_MODEL // LANE_BLOCK
GROUPS_PER_BLOCK = LANE_BLOCK // S5_GROUP
STATE_BLOCK = GROUPS_PER_BLOCK * S5_STATE
SCAN_ROWS = 256
SUBLANES = 8
CONV_E = 2048
POOL_E = 2048
POOL_WINDOWS = (2, 4, 8, 16)
POOL_GROUP = 512
POOL_HALO = 16
N_CHIPS = 4

ADAM_LR, ADAM_B1, ADAM_B2, ADAM_EPS, ADAM_WD, ADAM_STEP = 0.001, 0.9, 0.999, 1e-08, 0.01, 10

VMEM_LIMIT = 48 * 1024 * 1024
TN_OPERAND_BYTES = 28 * 1024 * 1024

WEIGHTS = ['meta_tokens', 'norm0_g', 'l0_w_in', 'l0_lam_re', 'l0_lam_im', 'l0_log_dt', 'l0_b_re', 'l0_b_im',
           'l0_c_re', 'l0_c_im', 'l0_d_skip', 'l0_w_glu', 'l0_b_glu', 'l0_w_out', 'norm1_g', 'l1_w_in',
           'l1_conv_w', 'l1_conv_b', 'l1_w_out', 'norm2_g', 'l2_w_in', 'l2_w_grp', 'l2_b_grp', 'l2_scale',
           'l2_w_out', 'norm3_g', 'l3_w_in', 'l3_lam_re', 'l3_lam_im', 'l3_log_dt', 'l3_b_re', 'l3_b_im',
           'l3_c_re', 'l3_c_im', 'l3_d_skip', 'l3_w_glu', 'l3_b_glu', 'l3_w_out', 'final_g']
BIG = ['l0_w_in', 'l0_w_glu', 'l0_w_out', 'l1_w_in', 'l1_w_out', 'l2_w_in', 'l2_w_grp', 'l2_w_out',
       'l3_w_in', 'l3_w_glu', 'l3_w_out']
SMALL_SHARDED = ['meta_tokens', 'l1_conv_w', 'l2_b_grp']
SMALL = [n for n in WEIGHTS if n not in BIG]


def _params(*sem):
    return pltpu.CompilerParams(dimension_semantics=sem, vmem_limit_bytes=VMEM_LIMIT)


class Job:
    def __init__(self, arrays, out_shapes, sems, fn, in_place=False):
        self.arrays, self.out_shapes, self.sems, self.fn = list(arrays), list(out_shapes), list(sems), fn
        self.in_place = in_place
        assert len(self.arrays) == len(self.out_shapes)

    @staticmethod
    def both(first, second):
        na, ns = len(first.arrays), len(first.sems)

        def fn(ins, outs, sems, phase):
            first.fn(ins[:na], outs[:na], sems[:ns], phase)
            second.fn(ins[na:], outs[na:], sems[ns:], phase)

        job = Job(first.arrays + second.arrays, first.out_shapes + second.out_shapes, first.sems + second.sems, fn)
        job.in_place = [first.in_place] * na + [second.in_place] * len(second.arrays)
        return job

    def aliased(self):
        flags = self.in_place if isinstance(self.in_place, list) else [self.in_place] * len(self.arrays)
        return [a for a, flag in enumerate(flags) if flag]


def _call(body, *, name, grid, in_specs, out_specs, out_shape, args, semantics, scratch=(), aliases=None, job=None):
    if job is None:
        return pl.pallas_call(
            body, name=name, grid=grid, in_specs=list(in_specs), out_specs=list(out_specs),
            out_shape=list(out_shape), scratch_shapes=list(scratch), input_output_aliases=aliases or {},
            compiler_params=_params(*semantics))(*args)
    counts = [len(in_specs), len(job.arrays), len(out_specs), len(job.out_shapes), len(scratch), len(job.sems)]

    def hosted(*refs):
        parts, pos = [], 0
        for k in counts:
            parts.append(refs[pos:pos + k])
            pos += k
        ins, job_ins, outs, job_outs, scr, job_sems = parts
        steps = [pl.program_id(d) for d in range(len(grid))]
        first = functools.reduce(jnp.logical_and, [s == 0 for s in steps])
        last = functools.reduce(jnp.logical_and, [s == g - 1 for s, g in zip(steps, grid, strict=True)])

        @pl.when(first)
        def _():
            job.fn(job_ins, job_outs, job_sems, "start")

        body(*ins, *outs, *scr)

        @pl.when(last)
        def _():
            job.fn(job_ins, job_outs, job_sems, "finish")

    any_spec = pl.BlockSpec(memory_space=pl.ANY)
    aliases = dict(aliases or {})
    aliases.update({len(in_specs) + a: len(out_specs) + a for a in job.aliased()})
    res = pl.pallas_call(
        hosted, name=name, grid=grid, in_specs=list(in_specs) + [any_spec] * len(job.arrays),
        out_specs=list(out_specs) + [any_spec] * len(job.out_shapes),
        out_shape=list(out_shape) + job.out_shapes, scratch_shapes=list(scratch) + job.sems,
        input_output_aliases=aliases,
        compiler_params=_params(*["arbitrary"] * len(grid)))(*args, *job.arrays)
    return res[:len(out_specs)], res[len(out_specs):]


def _tile(n, cap, mult):
    best = None
    for t in range(mult, min(n, cap) + 1, mult):
        if n % t == 0:
            best = t
    assert best is not None, (n, cap, mult)
    return best


def _with_job(res, job):
    return res[0] if job is None else (res[0][0], res[1])


def mm_nn(a, b, *, name, bias=None, add=None, out_dtype=F32, job=None):
    m, k = a.shape
    s, _, ns = b.shape
    n = s * ns
    tm, tn = _tile(m, 1088, 16), _tile(ns, 1024, 128)
    per = ns // tn

    def body(*refs):
        a_ref, b_ref = refs[0], refs[1]
        o_ref = refs[-1]
        acc = jnp.dot(a_ref[...].astype(BF16), b_ref[...], preferred_element_type=F32)
        pos = 2
        if bias is not None:
            acc = acc + refs[pos][...]
            pos += 1
        if add is not None:
            acc = acc + refs[pos][...]
        o_ref[...] = acc.astype(o_ref.dtype)

    in_specs = [pl.BlockSpec((tm, k), lambda i, j: (i, 0)),
                pl.BlockSpec((None, k, tn), lambda i, j: (j // per, 0, j % per))]
    args = [a, b]
    if bias is not None:
        in_specs.append(pl.BlockSpec((1, tn), lambda i, j: (0, j)))
        args.append(bias)
    if add is not None:
        in_specs.append(pl.BlockSpec((tm, tn), lambda i, j: (i, j)))
        args.append(add)
    return _with_job(_call(
        body, name=name, grid=(m // tm, n // tn), in_specs=in_specs,
        out_specs=[pl.BlockSpec((tm, tn), lambda i, j: (i, j))],
        out_shape=[jax.ShapeDtypeStruct((m, n), out_dtype)], args=args,
        semantics=("parallel", "parallel"), job=job), job)


def norm_mm_nn(h, gain, b, *, name, job=None):
    m, k = h.shape
    s, _, ns = b.shape
    n = s * ns
    tm, tn = _tile(m, 1088, 16), _tile(ns, 1024, 128)
    per = ns // tn

    def body(h_ref, g_ref, b_ref, o_ref, n_ref):
        @pl.when(pl.program_id(1) == 0)
        def _():
            n_ref[...] = _rms(h_ref[...], g_ref[...]).astype(n_ref.dtype)

        o_ref[...] = jnp.dot(n_ref[...], b_ref[...], preferred_element_type=F32)

    res = _call(
        body, name=name, grid=(m // tm, n // tn),
        in_specs=[pl.BlockSpec((tm, k), lambda i, j: (i, 0)), pl.BlockSpec((1, k), lambda i, j: (0, 0)),
                  pl.BlockSpec((None, k, tn), lambda i, j: (j // per, 0, j % per))],
        out_specs=[pl.BlockSpec((tm, tn), lambda i, j: (i, j)), pl.BlockSpec((tm, k), lambda i, j: (i, 0))],
        out_shape=[jax.ShapeDtypeStruct((m, n), F32), jax.ShapeDtypeStruct((m, k), BF16)], args=(h, gain, b),
        semantics=("parallel", "arbitrary"), job=job)
    return (res[0], res[1]) if job is None else ((res[0][0], res[0][1]), res[1])


def glu_gate(yg, w_glu, b_glu, y0, uz, *, name, job=None):
    m, k = yg.shape
    n = w_glu.shape[2]
    tm = _tile(m, 1088, 16)

    def body(a_ref, w_ref, b_ref, y0_ref, z_ref, q_ref, o_ref):
        q = jnp.dot(a_ref[...], w_ref[...], preferred_element_type=F32) + b_ref[...]
        q_ref[...] = q
        o_ref[...] = _s5_gate(y0_ref[...], q, z_ref[...]).astype(o_ref.dtype)

    tile = pl.BlockSpec((tm, n), lambda i: (i, 0))
    res = _call(
        body, name=name, grid=(m // tm,),
        in_specs=[pl.BlockSpec((tm, k), lambda i: (i, 0)), pl.BlockSpec((None, k, n), lambda i: (0, 0, 0)),
                  pl.BlockSpec((1, n), lambda i: (0, 0)), tile, pl.BlockSpec((tm, n), lambda i: (i, 1))],
        out_specs=[tile, tile],
        out_shape=[jax.ShapeDtypeStruct((m, n), F32), jax.ShapeDtypeStruct((m, n), BF16)],
        args=(yg, w_glu, b_glu, y0, uz), semantics=("parallel",), job=job)
    return (res[0], res[1]) if job is None else ((res[0][0], res[0][1]), res[1])


def mm_nt_norm_bwd(g, w, h, gain, dh_out, *, name, job=None):
    m, kc = g.shape
    s, nout, kcs = w.shape
    assert s * kcs == kc
    tm, tk = _tile(m, 1088, 16), _tile(kcs, 1024, 128)
    per = kcs // tk
    nk = kc // tk

    def body(g_ref, w_ref, h_ref, gain_ref, dh_ref, o_ref, dg_ref):
        i, kk = pl.program_id(0), pl.program_id(1)
        part = lax.dot_general(g_ref[...].astype(BF16), w_ref[...], (((1,), (1,)), ((), ())),
                               preferred_element_type=F32)

        @pl.when(kk == 0)
        def _():
            o_ref[...] = part

        @pl.when(kk > 0)
        def _():
            o_ref[...] += part

        @pl.when(kk == nk - 1)
        def _():
            dh, dg = jax.vjp(_rms, h_ref[...], gain_ref[...])[1](o_ref[...])
            o_ref[...] = dh_ref[...] + dh

            @pl.when(i == 0)
            def _():
                dg_ref[...] = dg

            @pl.when(i > 0)
            def _():
                dg_ref[...] += dg

    row_tile = pl.BlockSpec((tm, nout), lambda i, kk: (i, 0))
    res = _call(
        body, name=name, grid=(m // tm, nk),
        in_specs=[pl.BlockSpec((tm, tk), lambda i, kk: (i, kk)),
                  pl.BlockSpec((None, nout, tk), lambda i, kk: (kk // per, 0, kk % per)),
                  row_tile, pl.BlockSpec((1, nout), lambda i, kk: (0, 0)), row_tile],
        out_specs=[row_tile, pl.BlockSpec((1, nout), lambda i, kk: (0, 0))],
        out_shape=[jax.ShapeDtypeStruct((m, nout), F32), jax.ShapeDtypeStruct((1, nout), F32)],
        args=(g, w, h, gain, dh_out), semantics=("arbitrary", "arbitrary"), job=job)
    return (res[0], res[1]) if job is None else ((res[0][0], res[0][1]), res[1])


def mm_nt(g, w, *, name, job=None):
    m, kc = g.shape
    s, nout, kcs = w.shape
    assert s * kcs == kc
    tm, tno, tk = _tile(m, 2176, 16), _tile(nout, 1024, 128), _tile(kcs, 1024, 128)
    per = kcs // tk

    def body(g_ref, w_ref, o_ref):
        kk = pl.program_id(2)
        part = lax.dot_general(g_ref[...].astype(BF16), w_ref[...], (((1,), (1,)), ((), ())),
                               preferred_element_type=F32)

        @pl.when(kk == 0)
        def _():
            o_ref[...] = part

        @pl.when(kk > 0)
        def _():
            o_ref[...] += part

    return _with_job(_call(
        body, name=name, grid=(m // tm, nout // tno, kc // tk),
        in_specs=[pl.BlockSpec((tm, tk), lambda i, j, kk: (i, kk)),
                  pl.BlockSpec((None, tno, tk), lambda i, j, kk: (kk // per, j, kk % per))],
        out_specs=[pl.BlockSpec((tm, tno), lambda i, j, kk: (i, j))],
        out_shape=[jax.ShapeDtypeStruct((m, nout), F32)], args=(g, w),
        semantics=("parallel", "parallel", "arbitrary"), job=job), job)


def mm_tn(a, g, shards, *, name, job=None):
    m, ka = a.shape
    n = g.shape[1]
    ns = n // shards
    tka, tn = _tile(ka, 1024, 128), _tile(ns, 512, 128)
    row_bytes = tka * jnp.dtype(a.dtype).itemsize + tn * jnp.dtype(g.dtype).itemsize
    tm = _tile(m, TN_OPERAND_BYTES // (2 * row_bytes), 16)
    per = ns // tn

    def body(a_ref, g_ref, o_ref):
        mm = pl.program_id(2)
        part = lax.dot_general(a_ref[...].astype(BF16), g_ref[...].astype(BF16), (((0,), (0,)), ((), ())),
                               preferred_element_type=F32)

        @pl.when(mm == 0)
        def _():
            o_ref[...] = part

        @pl.when(mm > 0)
        def _():
            o_ref[...] += part

    return _with_job(_call(
        body, name=name, grid=(ka // tka, n // tn, m // tm),
        in_specs=[pl.BlockSpec((tm, tka), lambda i, j, mm: (mm, i)),
                  pl.BlockSpec((tm, tn), lambda i, j, mm: (mm, j))],
        out_specs=[pl.BlockSpec((None, tka, tn), lambda i, j, mm: (j // per, i, j % per))],
        out_shape=[jax.ShapeDtypeStruct((shards, ka, ns), F32)], args=(a, g),
        semantics=("parallel", "parallel", "arbitrary"), job=job), job)


def grp_nn(a, w, bias, *, name):
    m = a.shape[0]
    tm = _tile(m, 1088, 16)
    ng = len(POOL_WINDOWS)

    def body(a_ref, w_ref, b_ref, o_ref):
        wj = w_ref[...].reshape(POOL_GROUP, POOL_GROUP)
        o_ref[...] = jnp.dot(a_ref[...].astype(BF16), wj, preferred_element_type=F32) + b_ref[...]

    return pl.pallas_call(
        body, name=name, grid=(m // tm, ng),
        in_specs=[pl.BlockSpec((tm, POOL_GROUP), lambda i, j: (i, j)),
                  pl.BlockSpec((N_CHIPS, None, POOL_GROUP // N_CHIPS, POOL_GROUP), lambda i, j: (0, j, 0, 0)),
                  pl.BlockSpec((1, POOL_GROUP), lambda i, j: (0, j))],
        out_specs=pl.BlockSpec((tm, POOL_GROUP), lambda i, j: (i, j)),
        out_shape=jax.ShapeDtypeStruct((m, ng * POOL_GROUP), F32),
        compiler_params=_params("parallel", "parallel"))(a, w, bias)


def grp_nt(g, w, *, name):
    m = g.shape[0]
    tm = _tile(m, 1088, 16)
    ng = len(POOL_WINDOWS)

    def body(g_ref, w_ref, o_ref):
        wj = w_ref[...].reshape(POOL_GROUP, POOL_GROUP)
        o_ref[...] = lax.dot_general(g_ref[...].astype(BF16), wj, (((1,), (1,)), ((), ())),
                                     preferred_element_type=F32)

    return pl.pallas_call(
        body, name=name, grid=(m // tm, ng),
        in_specs=[pl.BlockSpec((tm, POOL_GROUP), lambda i, j: (i, j)),
                  pl.BlockSpec((N_CHIPS, None, POOL_GROUP // N_CHIPS, POOL_GROUP), lambda i, j: (0, j, 0, 0))],
        out_specs=pl.BlockSpec((tm, POOL_GROUP), lambda i, j: (i, j)),
        out_shape=jax.ShapeDtypeStruct((m, ng * POOL_GROUP), F32),
        compiler_params=_params("parallel", "parallel"))(g, w)


def grp_tn(a, g, *, name):
    m = a.shape[0]
    tm = _tile(m, 1088, 16)
    ng = len(POOL_WINDOWS)
    rows = POOL_GROUP // N_CHIPS

    def body(a_ref, g_ref, o_ref):
        mm = pl.program_id(1)
        part = lax.dot_general(a_ref[...].astype(BF16), g_ref[...].astype(BF16), (((0,), (0,)), ((), ())),
                               preferred_element_type=F32).reshape(N_CHIPS, rows, POOL_GROUP)

        @pl.when(mm == 0)
        def _():
            o_ref[...] = part

        @pl.when(mm > 0)
        def _():
            o_ref[...] += part

    return pl.pallas_call(
        body, name=name, grid=(ng, m // tm),
        in_specs=[pl.BlockSpec((tm, POOL_GROUP), lambda j, mm: (mm, j)),
                  pl.BlockSpec((tm, POOL_GROUP), lambda j, mm: (mm, j))],
        out_specs=pl.BlockSpec((N_CHIPS, None, rows, POOL_GROUP), lambda j, mm: (0, j, 0, 0)),
        out_shape=jax.ShapeDtypeStruct((N_CHIPS, ng, rows, POOL_GROUP), F32),
        compiler_params=_params("parallel", "arbitrary"))(a, g)


def rowwise(fn, rows, bcast, outs, accs=(), *, name, aliases=None):
    m = rows[0][0].shape[0]
    tr = m // 16
    n_in = len(rows) + len(bcast)

    def body(*refs):
        vals = [r[...] for r in refs[:n_in]]
        o_vals, a_vals = fn(*vals)
        for ref, val in zip(refs[n_in:n_in + len(outs)], o_vals, strict=True):
            ref[...] = val.astype(ref.dtype)
        step = pl.program_id(0)
        for ref, val in zip(refs[n_in + len(outs):], a_vals, strict=True):
            @pl.when(step == 0)
            def _(ref=ref, val=val):
                ref[...] = val

            @pl.when(step > 0)
            def _(ref=ref, val=val):
                ref[...] += val

    in_specs = [pl.BlockSpec((tr, w), functools.partial(lambda i, cb: (i, cb), cb=cb)) for _, w, cb in rows]
    in_specs += [pl.BlockSpec(b.shape, lambda i: (0, 0)) for b in bcast]
    out_specs = [pl.BlockSpec((tr, w), functools.partial(lambda i, cb: (i, cb), cb=cb)) for _, _, w, cb in outs]
    out_specs += [pl.BlockSpec((1, w), lambda i: (0, 0)) for w in accs]
    out_shape = [jax.ShapeDtypeStruct((m, cols), dt) for cols, dt, _, _ in outs]
    out_shape += [jax.ShapeDtypeStruct((1, w), F32) for w in accs]
    res = pl.pallas_call(
        body, name=name, grid=(m // tr,), in_specs=in_specs, out_specs=out_specs, out_shape=out_shape,
        input_output_aliases=aliases or {},
        compiler_params=_params("arbitrary"))(*[r[0] for r in rows], *bcast)
    return res[:len(outs)], res[len(outs):]


def _rms(h, g):
    return h * lax.rsqrt(jnp.mean(h * h, axis=-1, keepdims=True) + EPS) * g


def _colsum(v):
    return jnp.sum(v, axis=0, keepdims=True)


def _s5_gate(y0, q, z):
    yg = jax.nn.gelu(y0)
    return yg * jax.nn.sigmoid(q) * jax.nn.silu(z)


def _pool_gate(o, z, scale):
    return o * scale * jax.nn.silu(z)


def _shift_rows(v, d, down):
    t = v.shape[0]
    row = lax.broadcasted_iota(jnp.int32, v.shape, 0)
    if down:
        return jnp.where(row >= d, pltpu.roll(v, d, 0), 0.0)
    return jnp.where(row < t - d, pltpu.roll(v, t - d, 0), 0.0)


def _cmul(ar, ai, br, bi):
    return ar * br - ai * bi, ar * bi + ai * br


def _scan(xr, xi, ar, ai, cr, ci, down):
    t, n = xr.shape
    nb = t // SUBLANES
    pw = [(ar, ai)]
    for _ in range(SUBLANES - 1):
        pw.append(_cmul(*pw[-1], ar, ai))
    sub = lax.broadcasted_iota(jnp.int32, (SUBLANES, n), 0)

    def table(exps):
        tr, ti = jnp.zeros((SUBLANES, n), F32), jnp.zeros((SUBLANES, n), F32)
        for r, e in enumerate(exps):
            if e:
                tr, ti = jnp.where(sub == r, pw[e - 1][0], tr), jnp.where(sub == r, pw[e - 1][1], ti)
        return tr[None], ti[None]

    sr, si = xr.reshape(nb, SUBLANES, n), xi.reshape(nb, SUBLANES, n)
    d = 1
    while d < SUBLANES:
        mr, mi = table([d if (r >= d if down else r < SUBLANES - d) else 0 for r in range(SUBLANES)])
        turn = d if down else SUBLANES - d
        hr, hi = pltpu.roll(sr, turn, 1), pltpu.roll(si, turn, 1)
        sr, si = sr + mr * hr - mi * hi, si + mr * hi + mi * hr
        d *= 2
    edge = SUBLANES - 1 if down else 0
    qr, qi = table([r + 1 if down else SUBLANES - r for r in range(SUBLANES)])
    qr, qi = qr[0], qi[0]
    in_r, in_i = jnp.broadcast_to(cr, (SUBLANES, n)), jnp.broadcast_to(ci, (SUBLANES, n))
    out_r, out_i = [None] * nb, [None] * nb
    for b in (range(nb) if down else reversed(range(nb))):
        out_r[b] = sr[b] + qr * in_r - qi * in_i
        out_i[b] = si[b] + qr * in_i + qi * in_r
        in_r = jnp.broadcast_to(out_r[b][edge:edge + 1, :], (SUBLANES, n))
        in_i = jnp.broadcast_to(out_i[b][edge:edge + 1, :], (SUBLANES, n))
    return jnp.concatenate(out_r, axis=0), jnp.concatenate(out_i, axis=0), in_r[0:1, :], in_i[0:1, :]


def s5_fwd(uz, bblk, cblk, ar, ai, dskip, *, name, job=None):
    lp = uz.shape[0]
    t = SCAN_ROWS
    nst = S5_GROUPS * S5_STATE

    def body(u_ref, b_ref, c_ref, ar_ref, ai_ref, d_ref, y_ref, yg_ref, sr_ref, si_ref, cr, ci):
        step = pl.program_id(1)

        @pl.when(step == 0)
        def _():
            cr[...] = jnp.zeros_like(cr)
            ci[...] = jnp.zeros_like(ci)

        u = u_ref[...]
        a_r, a_i = ar_ref[...], ai_ref[...]
        x = jnp.dot(u.astype(BF16), b_ref[...].astype(BF16), preferred_element_type=F32)
        sr, si, cr[...], ci[...] = _scan(x[:, :STATE_BLOCK], x[:, STATE_BLOCK:], a_r, a_i, cr[...], ci[...], True)
        sr_ref[...] = sr
        si_ref[...] = si
        s = jnp.concatenate([sr, si], axis=1).astype(BF16)
        y0 = lax.dot_general(s, c_ref[...].astype(BF16), (((1,), (1,)), ((), ())),
                             preferred_element_type=F32) + d_ref[...] * u
        y_ref[...] = y0
        yg_ref[...] = jax.nn.gelu(y0).astype(yg_ref.dtype)

    return _call(
        body, name=name, grid=(S5_BLOCKS, lp // t),
        in_specs=[pl.BlockSpec((t, LANE_BLOCK), lambda b, c: (c, b)),
                  pl.BlockSpec((None, LANE_BLOCK, 2 * STATE_BLOCK), lambda b, c: (b, 0, 0)),
                  pl.BlockSpec((None, LANE_BLOCK, 2 * STATE_BLOCK), lambda b, c: (b, 0, 0)),
                  pl.BlockSpec((1, STATE_BLOCK), lambda b, c: (0, b)),
                  pl.BlockSpec((1, STATE_BLOCK), lambda b, c: (0, b)),
                  pl.BlockSpec((1, LANE_BLOCK), lambda b, c: (0, b))],
        out_specs=[pl.BlockSpec((t, LANE_BLOCK), lambda b, c: (c, b)),
                   pl.BlockSpec((t, LANE_BLOCK), lambda b, c: (c, b)),
                   pl.BlockSpec((t, STATE_BLOCK), lambda b, c: (c, b)),
                   pl.BlockSpec((t, STATE_BLOCK), lambda b, c: (c, b))],
        out_shape=[jax.ShapeDtypeStruct((lp, D_MODEL), F32), jax.ShapeDtypeStruct((lp, D_MODEL), BF16),
                   jax.ShapeDtypeStruct((lp, nst), F32), jax.ShapeDtypeStruct((lp, nst), F32)],
        scratch=[pltpu.VMEM((1, STATE_BLOCK), F32), pltpu.VMEM((1, STATE_BLOCK), F32)],
        args=(uz, bblk, cblk, ar, ai, dskip), semantics=("parallel", "arbitrary"), job=job)


def s5_bwd(dy_direct, dyg, y0, uz, sr, si, bblk, cblk, ar, ai, dskip, dp, *, name, job=None):
    lp = uz.shape[0]
    t = SCAN_ROWS
    nch = lp // t
    nst = S5_GROUPS * S5_STATE

    def body(dyd_ref, dyg_ref, y0_ref, u_ref, sr_ref, si_ref, b_ref, c_ref, ar_ref, ai_ref, d_ref, _, du_ref,
             db_ref, dc_ref, dar_ref, dai_ref, dd_ref, cr, ci):
        step = pl.program_id(1)

        @pl.when(step == 0)
        def _():
            cr[...] = jnp.zeros_like(cr)
            ci[...] = jnp.zeros_like(ci)

        dy = dyd_ref[...] + jax.vjp(jax.nn.gelu, y0_ref[...])[1](dyg_ref[...])[0]
        u = u_ref[...]
        dyb, ub = dy.astype(BF16), u.astype(BF16)
        a_r, a_i = ar_ref[...], -ai_ref[...]
        g = jnp.dot(dyb, c_ref[...].astype(BF16), preferred_element_type=F32)
        gr, gi = g[:, :STATE_BLOCK], g[:, STATE_BLOCK:]
        nxt_r, nxt_i = cr[...], ci[...]
        last = lax.broadcasted_iota(jnp.int32, gr.shape, 0) == t - 1
        lr, li, cr[...], ci[...] = _scan(gr, gi, a_r, a_i, nxt_r, nxt_i, False)
        nr = _shift_rows(lr, 1, False) + jnp.where(last, nxt_r, 0.0)
        ni = _shift_rows(li, 1, False) + jnp.where(last, nxt_i, 0.0)
        s_r, s_i = sr_ref[...], si_ref[...]
        dar = _colsum(s_r * nr + s_i * ni)
        dai = _colsum(s_r * ni - s_i * nr)
        lam = jnp.concatenate([lr, li], axis=1).astype(BF16)
        sb = jnp.concatenate([s_r, s_i], axis=1).astype(BF16)
        dc = lax.dot_general(dyb, sb, (((0,), (0,)), ((), ())), preferred_element_type=F32)
        db = lax.dot_general(ub, lam, (((0,), (0,)), ((), ())), preferred_element_type=F32)
        du_ref[...] = (lax.dot_general(lam, b_ref[...].astype(BF16), (((1,), (1,)), ((), ())),
                                       preferred_element_type=F32) + d_ref[...] * dy).astype(du_ref.dtype)
        dd = _colsum(dy * u)

        @pl.when(step == 0)
        def _():
            db_ref[...] = db
            dc_ref[...] = dc
            dar_ref[...] = dar
            dai_ref[...] = dai
            dd_ref[...] = dd

        @pl.when(step > 0)
        def _():
            db_ref[...] += db
            dc_ref[...] += dc
            dar_ref[...] += dar
            dai_ref[...] += dai
            dd_ref[...] += dd

    rev = lambda b, c: (nch - 1 - c, b)
    return _call(
        body, name=name, grid=(S5_BLOCKS, nch),
        in_specs=[pl.BlockSpec((t, LANE_BLOCK), rev), pl.BlockSpec((t, LANE_BLOCK), rev),
                  pl.BlockSpec((t, LANE_BLOCK), rev), pl.BlockSpec((t, LANE_BLOCK), rev),
                  pl.BlockSpec((t, STATE_BLOCK), rev), pl.BlockSpec((t, STATE_BLOCK), rev),
                  pl.BlockSpec((None, LANE_BLOCK, 2 * STATE_BLOCK), lambda b, c: (b, 0, 0)),
                  pl.BlockSpec((None, LANE_BLOCK, 2 * STATE_BLOCK), lambda b, c: (b, 0, 0)),
                  pl.BlockSpec((1, STATE_BLOCK), lambda b, c: (0, b)),
                  pl.BlockSpec((1, STATE_BLOCK), lambda b, c: (0, b)),
                  pl.BlockSpec((1, LANE_BLOCK), lambda b, c: (0, b)),
                  pl.BlockSpec(memory_space=pl.ANY)],
        out_specs=[pl.BlockSpec((t, LANE_BLOCK), rev),
                   pl.BlockSpec((None, LANE_BLOCK, 2 * STATE_BLOCK), lambda b, c: (b, 0, 0)),
                   pl.BlockSpec((None, LANE_BLOCK, 2 * STATE_BLOCK), lambda b, c: (b, 0, 0)),
                   pl.BlockSpec((1, STATE_BLOCK), lambda b, c: (0, b)),
                   pl.BlockSpec((1, STATE_BLOCK), lambda b, c: (0, b)),
                   pl.BlockSpec((1, LANE_BLOCK), lambda b, c: (0, b))],
        out_shape=[jax.ShapeDtypeStruct(dp.shape, dp.dtype),
                   jax.ShapeDtypeStruct((S5_BLOCKS, LANE_BLOCK, 2 * STATE_BLOCK), F32),
                   jax.ShapeDtypeStruct((S5_BLOCKS, LANE_BLOCK, 2 * STATE_BLOCK), F32),
                   jax.ShapeDtypeStruct((1, nst), F32), jax.ShapeDtypeStruct((1, nst), F32),
                   jax.ShapeDtypeStruct((1, D_MODEL), F32)],
        scratch=[pltpu.VMEM((1, STATE_BLOCK), F32), pltpu.VMEM((1, STATE_BLOCK), F32)],
        aliases={11: 0}, args=(dy_direct, dyg, y0, uz, sr, si, bblk, cblk, ar, ai, dskip, dp),
        semantics=("parallel", "arbitrary"), job=job)


def _s5_prep(lam_re, lam_im, log_dt, b_re, b_im, c_re, c_im):
    dt = jnp.exp(log_dt)
    mag = jnp.exp(lam_re * dt)
    ar = mag * jnp.cos(lam_im * dt)
    ai = mag * jnp.sin(lam_im * dt)
    den = lam_re * lam_re + lam_im * lam_im
    kr = ((ar - 1.0) * lam_re + ai * lam_im) / den
    ki = (ai * lam_re - (ar - 1.0) * lam_im) / den
    bbr = kr[:, None, :] * b_re - ki[:, None, :] * b_im
    bbi = kr[:, None, :] * b_im + ki[:, None, :] * b_re
    rows = S5_GROUPS * S5_GROUP
    expand = (lax.broadcasted_iota(jnp.int32, (S5_STATE, STATE_BLOCK), 1) % S5_STATE
              == lax.broadcasted_iota(jnp.int32, (S5_STATE, STATE_BLOCK), 0)).astype(F32)
    own = ((lax.broadcasted_iota(jnp.int32, (rows, STATE_BLOCK), 0) // S5_GROUP) % GROUPS_PER_BLOCK
           == lax.broadcasted_iota(jnp.int32, (rows, STATE_BLOCK), 1) // S5_STATE).astype(F32)

    def blocks(v):
        wide = jnp.dot(v.reshape(rows, S5_STATE), expand, precision=lax.Precision.HIGHEST,
                       preferred_element_type=F32)
        return (wide * own).reshape(S5_BLOCKS, LANE_BLOCK, STATE_BLOCK)

    btab = jnp.concatenate([blocks(bbr), blocks(bbi)], axis=2)
    ctab = jnp.concatenate([blocks(c_re), -blocks(c_im)], axis=2)
    return ar, ai, btab, ctab


_S5_PREP_OUT = [(S5_GROUPS, S5_STATE), (S5_GROUPS, S5_STATE), (S5_BLOCKS, LANE_BLOCK, 2 * STATE_BLOCK),
                (S5_BLOCKS, LANE_BLOCK, 2 * STATE_BLOCK)]


def s5_prep(params, *, name):
    def body(*refs):
        for ref, val in zip(refs[7:], _s5_prep(*[r[...] for r in refs[:7]]), strict=True):
            ref[...] = val

    return pl.pallas_call(body, name=name, out_shape=[jax.ShapeDtypeStruct(s, F32) for s in _S5_PREP_OUT],
                          compiler_params=pltpu.CompilerParams(vmem_limit_bytes=VMEM_LIMIT))(*params)


def s5_prep_bwd(params, cotangents, *, name):
    def body(*refs):
        grads = jax.vjp(_s5_prep, *[r[...] for r in refs[:7]])[1](tuple(r[...] for r in refs[7:11]))
        for ref, val in zip(refs[11:], grads, strict=True):
            ref[...] = val

    return pl.pallas_call(body, name=name, out_shape=[jax.ShapeDtypeStruct(p.shape, F32) for p in params],
                          compiler_params=pltpu.CompilerParams(vmem_limit_bytes=VMEM_LIMIT))(*params, *cotangents)


def _silu_grad(z):
    s = jax.nn.sigmoid(z)
    return s * (1.0 + z * (1.0 - s))


def conv_fwd(p, conv_w, conv_b, *, name):
    lp = p.shape[0]
    tr = lp // 16
    e = CONV_E
    hb = tr // 8

    def body(p_ref, cgh_ref, vh_ref, w_ref, b_ref, o_ref):
        i = pl.program_id(0)
        bg, cg, v, z = (p_ref[:, k * e:(k + 1) * e] for k in range(4))
        hc = cg * v
        halo = jnp.where(i > 0, cgh_ref[...] * vh_ref[...], 0.0)
        ext = jnp.concatenate([halo, hc], axis=0)
        w = w_ref[...]
        conv = (w[0:1] * pltpu.roll(ext, 2, 0)[8:] + w[1:2] * pltpu.roll(ext, 1, 0)[8:] + w[2:3] * hc
                + b_ref[...])
        o_ref[...] = (bg * conv * jax.nn.silu(z)).astype(o_ref.dtype)

    prev = lambda col: (lambda i: (jnp.maximum(i * hb - 1, 0), col))
    return pl.pallas_call(
        body, name=name, grid=(lp // tr,),
        in_specs=[pl.BlockSpec((tr, 4 * e), lambda i: (i, 0)),
                  pl.BlockSpec((8, e), prev(1)), pl.BlockSpec((8, e), prev(2)),
                  pl.BlockSpec((3, e), lambda i: (0, 0)), pl.BlockSpec((1, e), lambda i: (0, 0))],
        out_specs=pl.BlockSpec((tr, e), lambda i: (i, 0)),
        out_shape=jax.ShapeDtypeStruct((lp, e), BF16),
        compiler_params=_params("parallel"))(p, p, p, conv_w, conv_b)


def conv_bwd(dy3, p, conv_w, conv_b, *, name):
    lp = p.shape[0]
    tr = lp // 16
    e = CONV_E
    hb = tr // 8
    nt = lp // tr
    width = 512

    def body(dy_ref, p_ref, cgh_ref, vh_ref, dyn_ref, bgn_ref, zn_ref, w_ref, b_ref, dp_ref, dw_ref, db_ref):
        i = pl.program_id(0)
        for c0 in range(0, e, width):
            cols = slice(c0, c0 + width)
            dy = dy_ref[:, cols]
            bg, cg, v, z = (p_ref[:, k * e + c0:k * e + c0 + width] for k in range(4))
            w = w_ref[:, cols]
            hc = cg * v
            halo = jnp.where(i > 0, cgh_ref[:, cols] * vh_ref[:, cols], 0.0)
            ext = jnp.concatenate([halo, hc], axis=0)
            hc2, hc1 = pltpu.roll(ext, 2, 0)[8:], pltpu.roll(ext, 1, 0)[8:]
            yc = w[0:1] * hc2 + w[1:2] * hc1 + w[2:3] * hc + b_ref[:, cols]
            sz = jax.nn.silu(z)
            dyc = dy * bg * sz
            dyc_n = jnp.where(i < nt - 1, dyn_ref[:, cols] * bgn_ref[:, cols] * jax.nn.silu(zn_ref[:, cols]), 0.0)
            ext_n = jnp.concatenate([dyc, dyc_n], axis=0)
            n_rows = tr + 8
            dhc = (w[2:3] * dyc + w[1:2] * pltpu.roll(ext_n, n_rows - 1, 0)[:tr]
                   + w[0:1] * pltpu.roll(ext_n, n_rows - 2, 0)[:tr])
            for k, val in enumerate((dy * yc * sz, dhc * v, dhc * cg, dy * bg * yc * _silu_grad(z))):
                dp_ref[:, k * e + c0:k * e + c0 + width] = val.astype(dp_ref.dtype)
            dw = jnp.concatenate([_colsum(dyc * hc2), _colsum(dyc * hc1), _colsum(dyc * hc)], axis=0)
            db = _colsum(dyc)

            @pl.when(i == 0)
            def _(cols=cols, dw=dw, db=db):
                dw_ref[:, cols] = dw
                db_ref[:, cols] = db

            @pl.when(i > 0)
            def _(cols=cols, dw=dw, db=db):
                dw_ref[:, cols] += dw
                db_ref[:, cols] += db

    prev = lambda col: (lambda i: (jnp.maximum(i * hb - 1, 0), col))
    nxt = lambda col: (lambda i: (jnp.minimum((i + 1) * hb, lp // 8 - 1), col))
    return pl.pallas_call(
        body, name=name, grid=(nt,),
        in_specs=[pl.BlockSpec((tr, e), lambda i: (i, 0)), pl.BlockSpec((tr, 4 * e), lambda i: (i, 0)),
                  pl.BlockSpec((8, e), prev(1)), pl.BlockSpec((8, e), prev(2)),
                  pl.BlockSpec((8, e), nxt(0)), pl.BlockSpec((8, e), nxt(0)), pl.BlockSpec((8, e), nxt(3)),
                  pl.BlockSpec((3, e), lambda i: (0, 0)), pl.BlockSpec((1, e), lambda i: (0, 0))],
        out_specs=[pl.BlockSpec((tr, 4 * e), lambda i: (i, 0)), pl.BlockSpec((3, e), lambda i: (0, 0)),
                   pl.BlockSpec((1, e), lambda i: (0, 0))],
        out_shape=[jax.ShapeDtypeStruct((lp, 4 * e), BF16), jax.ShapeDtypeStruct((3, e), F32),
                   jax.ShapeDtypeStruct((1, e), F32)],
        compiler_params=_params("arbitrary"))(dy3, p, p, p, dy3, p, p, conv_w, conv_b)


def _pool_counts(i, tr, rows, offset, w):
    t = (i * tr + offset + 1 + lax.broadcasted_iota(jnp.int32, (rows, 1), 0)).astype(F32)
    return jnp.minimum(t, float(w))


def pool_fwd(p, *, name):
    lp = p.shape[0]
    tr = lp // 16
    h = POOL_HALO
    hb = tr // h

    def body(u_ref, uh_ref, o_ref):
        i = pl.program_id(0)
        u = u_ref[...]
        ext = jnp.concatenate([jnp.where(i > 0, uh_ref[...], 0.0), u], axis=0)
        for k, w in enumerate(POOL_WINDOWS):
            cols = slice(k * POOL_GROUP, (k + 1) * POOL_GROUP)
            s = ext[:, cols]
            d = 1
            while d < w:
                s = s + pltpu.roll(s, d, 0)
                d *= 2
            o_ref[:, cols] = (s[h:] / _pool_counts(i, tr, tr, 0, w) - u[:, cols]).astype(o_ref.dtype)

    return pl.pallas_call(
        body, name=name, grid=(lp // tr,),
        in_specs=[pl.BlockSpec((tr, POOL_E), lambda i: (i, 0)),
                  pl.BlockSpec((h, POOL_E), lambda i: (jnp.maximum(i * hb - 1, 0), 0))],
        out_specs=pl.BlockSpec((tr, POOL_E), lambda i: (i, 0)),
        out_shape=jax.ShapeDtypeStruct((lp, POOL_E), BF16),
        compiler_params=_params("parallel"))(p, p)


def pool_bwd(dmix, dp, *, name):
    lp = dmix.shape[0]
    tr = lp // 16
    h = POOL_HALO
    hb = tr // h
    nt = lp // tr

    def body(dm_ref, dmn_ref, _, du_ref):
        i = pl.program_id(0)
        dm = dm_ref[...]
        dmn = jnp.where(i < nt - 1, dmn_ref[...], 0.0)
        n_rows = tr + h
        for k, w in enumerate(POOL_WINDOWS):
            cols = slice(k * POOL_GROUP, (k + 1) * POOL_GROUP)
            s = jnp.concatenate([dm[:, cols] / _pool_counts(i, tr, tr, 0, w),
                                 dmn[:, cols] / _pool_counts(i, tr, h, tr, w)], axis=0)
            d = 1
            while d < w:
                s = s + pltpu.roll(s, n_rows - d, 0)
                d *= 2
            du_ref[:, cols] = (s[:tr] - dm[:, cols]).astype(du_ref.dtype)

    return pl.pallas_call(
        body, name=name, grid=(nt,),
        in_specs=[pl.BlockSpec((tr, POOL_E), lambda i: (i, 0)),
                  pl.BlockSpec((h, POOL_E), lambda i: (jnp.minimum((i + 1) * hb, lp // h - 1), 0)),
                  pl.BlockSpec(memory_space=pl.ANY)],
        out_specs=pl.BlockSpec((tr, POOL_E), lambda i: (i, 0)),
        out_shape=jax.ShapeDtypeStruct(dp.shape, dp.dtype),
        input_output_aliases={2: 0},
        compiler_params=_params("parallel"))(dmix, dmix, dp)


def _adamw_math(w, g, m, v):
    nm = ADAM_B1 * m + (1.0 - ADAM_B1) * g
    nv = ADAM_B2 * v + (1.0 - ADAM_B2) * jnp.square(g)
    m_hat = nm / (1.0 - ADAM_B1 ** ADAM_STEP)
    v_hat = nv / (1.0 - ADAM_B2 ** ADAM_STEP)
    return -ADAM_LR * (m_hat / (jnp.sqrt(v_hat) + ADAM_EPS) + ADAM_WD * w), nm, nv


def adamw_many(quads, *, name, steps=8, job=None):
    n = len(quads)

    def spec(shape):
        return pl.BlockSpec((shape[0] // steps,) + shape[1:], lambda i, nd=len(shape): (i,) + (0,) * (nd - 1))

    def body(*refs):
        for q in range(n):
            w_ref, g_ref, m_ref, v_ref = refs[4 * q:4 * q + 4]
            d_ref, nm_ref, nv_ref = refs[4 * n + 3 * q:4 * n + 3 * q + 3]
            d_ref[...], nm_ref[...], nv_ref[...] = _adamw_math(w_ref[...], g_ref[...], m_ref[...], v_ref[...])

    res = _call(
        body, name=name, grid=(steps,), in_specs=[spec(quad[0].shape) for quad in quads for _ in range(4)],
        out_specs=[spec(quad[0].shape) for quad in quads for _ in range(3)],
        out_shape=[jax.ShapeDtypeStruct(quad[0].shape, F32) for quad in quads for _ in range(3)],
        args=[a for quad in quads for a in quad], semantics=("parallel",), job=job)
    outs = res if job is None else res[0]
    triples = [tuple(outs[3 * q:3 * q + 3]) for q in range(n)]
    return triples if job is None else (triples, res[1])


WIRE_WIDTH = 1024
WIRE_ROWS = 1024


def _wire_plan(shapes):
    starts, row = [], 0
    for s in shapes:
        r, c = (1, s[0]) if len(s) == 1 else s
        starts.append(row)
        row += -(-(r * max(1, c // WIRE_WIDTH)) // 8) * 8
    assert row <= WIRE_ROWS, row
    return starts


def _wire_cells(shape):
    if len(shape) == 1:
        n = shape[0]
        if n <= WIRE_WIDTH:
            return [(0, n, (slice(None),))]
        return [(h, WIRE_WIDTH, (slice(h * WIRE_WIDTH, (h + 1) * WIRE_WIDTH),)) for h in range(n // WIRE_WIDTH)]
    r, c = shape
    if c <= WIRE_WIDTH:
        return [(None, c, (slice(None), slice(None)))]
    per = c // WIRE_WIDTH
    return [(j * per + h, WIRE_WIDTH, (j, slice(h * WIRE_WIDTH, (h + 1) * WIRE_WIDTH)))
            for j in range(r) for h in range(per)]


def pack_small(arrays, *, name):
    shapes = [a.shape for a in arrays]
    starts = _wire_plan(shapes)

    def body(*refs):
        out = refs[-1]
        out[...] = jnp.zeros_like(out)
        for ref, shape, row0 in zip(refs[:-1], shapes, starts, strict=True):
            for off, cols, idx in _wire_cells(shape):
                if off is None:
                    out[row0:row0 + shape[0], 0:cols] = ref[idx]
                else:
                    out[row0 + off, 0:cols] = ref[idx]

    return pl.pallas_call(body, name=name, out_shape=jax.ShapeDtypeStruct((WIRE_ROWS, WIRE_WIDTH), F32),
                          compiler_params=pltpu.CompilerParams(vmem_limit_bytes=VMEM_LIMIT))(*arrays)


def update_small(packed, starts, triples, shard_cells, chip, *, name):
    n = len(triples)

    def body(chip_ref, p_ref, *refs):
        k = chip_ref[0]
        ins, outs = refs[:3 * n], refs[3 * n:]

        def cut(row0, rows, cols):
            c0, c1 = cols
            if isinstance(rows, int):
                return p_ref[row0 + rows, c0:c1]
            return p_ref[row0 + rows[0]:row0 + rows[1], c0:c1]

        for q in range(n):
            w_ref, m_ref, v_ref = ins[3 * q:3 * q + 3]
            g_ref, d_ref, nm_ref, nv_ref = outs[4 * q:4 * q + 4]
            shape = w_ref.shape
            if shard_cells[q] is None:
                pieces = [(cut(starts[q], (0, shape[0]) if off is None else off, (0, cols)), idx)
                          for off, cols, idx in _wire_cells(shape)]
            else:
                by_chip = [shard_cells[q](kk) for kk in range(N_CHIPS)]
                pieces = []
                for i, (_, _, idx) in enumerate(by_chip[0]):
                    g = cut(starts[q], *by_chip[0][i][:2])
                    for kk in range(1, N_CHIPS):
                        g = jnp.where(k == kk, cut(starts[q], *by_chip[kk][i][:2]), g)
                    pieces.append((g, idx))
            for g, idx in pieces:
                g_ref[idx] = g
                d_ref[idx], nm_ref[idx], nv_ref[idx] = _adamw_math(w_ref[idx], g, m_ref[idx], v_ref[idx])

    whole = lambda a: pl.BlockSpec(a.shape, lambda i, c_ref, nd=a.ndim: (0,) * nd)
    flat = [a for t in triples for a in t]
    out_shape = [jax.ShapeDtypeStruct(t[0].shape, F32) for t in triples for _ in range(4)]
    res = pl.pallas_call(
        body, name=name,
        grid_spec=pltpu.PrefetchScalarGridSpec(
            num_scalar_prefetch=1, grid=(1,),
            in_specs=[whole(packed)] + [whole(a) for a in flat],
            out_specs=[pl.BlockSpec(s.shape, lambda i, c_ref, nd=len(s.shape): (0,) * nd) for s in out_shape]),
        out_shape=out_shape,
        compiler_params=_params("arbitrary"))(chip.reshape(1).astype(jnp.int32), packed, *flat)
    return [tuple(res[4 * q:4 * q + 4]) for q in range(n)]


ANY = pl.BlockSpec(memory_space=pl.ANY)


def _place():
    x, y, c = lax.axis_index("x"), lax.axis_index("y"), lax.axis_index("c")
    return x, y, c, [(1 - x, y), (x, 1 - y), (1 - x, 1 - y)]


DMA_PIECE_BYTES = 512 * 1024


def _pieces(src, dst):
    shape = src.shape
    rows, cols = shape[-2], shape[-1]
    item = jnp.dtype(src.dtype).itemsize
    unit = 32 // item
    want = max(1, (rows * cols * item) // DMA_PIECE_BYTES)
    n = 1
    if rows % unit == 0:
        n = max(d for d in range(1, rows // unit + 1) if (rows // unit) % d == 0 and d <= want)
    size = rows // n
    out = []
    for lead in (range(shape[0]) if len(shape) == 3 else [None]):
        for i in range(n):
            sel = (pl.ds(i * size, size), slice(None)) if lead is None else (lead, pl.ds(i * size, size), slice(None))
            out.append((src.at[sel], dst.at[sel]))
    return out


def _send(src, dst, send_sem, recv_sem, peer, start=True):
    if start:
        for s, d in _pieces(src, dst):
            pltpu.make_async_remote_copy(src_ref=s, dst_ref=d, send_sem=send_sem, recv_sem=recv_sem,
                                         device_id=peer, device_id_type=MESH).start()
    return pltpu.make_async_remote_copy(src_ref=src, dst_ref=dst, send_sem=send_sem, recv_sem=recv_sem,
                                        device_id=peer, device_id_type=MESH)


def run_job(job, *, name):
    n_in, n_out = len(job.arrays), len(job.out_shapes)

    def body(*refs):
        job.fn(refs[:n_in], refs[n_in:n_in + n_out], refs[n_in + n_out:], "both")

    return pl.pallas_call(body, name=name, in_specs=[ANY] * n_in, out_specs=[ANY] * n_out,
                          out_shape=job.out_shapes, scratch_shapes=job.sems,
                          input_output_aliases={a: a for a in job.aliased()})(*job.arrays)


def gather_chips(shards, *, name):
    n = len(shards)

    def body(*refs):
        ins, outs = refs[:n], refs[n:2 * n]
        send_sems, recv_sems = refs[2 * n:]
        x, y, c, chips = _place()
        k = 2 * x + y
        copies = []
        for a in range(n):
            for j, peer in enumerate([(px, py, c) for px, py in chips] + [(x, y, 1 - c)]):
                copies.append(_send(ins[a], outs[a].at[k], send_sems.at[a, j], recv_sems.at[a, j], peer))
        for cp in copies:
            cp.wait()

    return pl.pallas_call(
        body, name=name, in_specs=[ANY] * n, out_specs=[ANY] * n,
        out_shape=[jax.ShapeDtypeStruct((N_CHIPS,) + s.shape, s.dtype) for s in shards],
        scratch_shapes=[pltpu.SemaphoreType.DMA((n, 4)), pltpu.SemaphoreType.DMA((n, 4))])(*shards)


def gather_by_halves_job(shards):
    n = len(shards)

    def fn(ins, outs, sems, phase):
        send_sems, recv_sems = sems
        x, y, c, chips = _place()
        k = 2 * x + y
        sibling = (x, y, 1 - c)
        begin = phase != "finish"
        waits, arrivals = [], []
        for a in range(n):
            half = ins[a].shape[0] // 2
            waits.append(_send(ins[a], outs[a].at[k], send_sems.at[a, 6], recv_sems.at[a, 6], sibling, begin))
            mine = pl.ds(c * half, half)
            for j, (px, py) in enumerate(chips):
                arrivals.append(_send(ins[a].at[mine, :], outs[a].at[k, mine, :], send_sems.at[a, j],
                                      recv_sems.at[a, j], (px, py, c), begin))
        if phase == "start":
            return
        i = 0
        for a in range(n):
            half = ins[a].shape[0] // 2
            mine = pl.ds(c * half, half)
            for j, (px, py) in enumerate(chips):
                arrivals[i].wait_recv()
                landed = outs[a].at[2 * px + py, mine, :]
                waits.append(_send(landed, landed, send_sems.at[a, 3 + j], recv_sems.at[a, 3 + j], sibling))
                i += 1
        for cp in arrivals:
            cp.wait_send()
        for cp in waits:
            cp.wait()

    return Job(shards, [jax.ShapeDtypeStruct((N_CHIPS,) + s.shape, s.dtype) for s in shards],
               [pltpu.SemaphoreType.DMA((n, 7)), pltpu.SemaphoreType.DMA((n, 7))], fn)


def gather_halves_job(shards):
    n = len(shards)

    def fn(ins, outs, sems, phase):
        send_sems, recv_sems = sems
        x, y, c, chips = _place()
        k = 2 * x + y
        copies = []
        for a in range(n):
            half = ins[a].shape[0] // 2
            mine = pl.ds(c * half, half)
            copies.append(_send(ins[a], outs[a].at[k], send_sems.at[a, 3], recv_sems.at[a, 3], (x, y, 1 - c),
                                phase != "finish"))
            for j, (px, py) in enumerate(chips):
                copies.append(_send(ins[a].at[mine, :], outs[a].at[k, mine, :], send_sems.at[a, j],
                                    recv_sems.at[a, j], (px, py, c), phase != "finish"))
        if phase != "start":
            for cp in copies:
                cp.wait()

    return Job(shards, [jax.ShapeDtypeStruct((N_CHIPS,) + s.shape, s.dtype) for s in shards],
               [pltpu.SemaphoreType.DMA((n, 4)), pltpu.SemaphoreType.DMA((n, 4))], fn)


def forward_halves_job(gathered):
    n = len(gathered)

    def fn(ins, outs, sems, phase):
        send_sems, recv_sems = sems
        x, y, c, chips = _place()
        copies = []
        for a in range(n):
            half = outs[a].shape[1] // 2
            for j, (px, py) in enumerate(chips):
                landed = outs[a].at[2 * px + py, pl.ds(c * half, half), :]
                copies.append(_send(landed, landed, send_sems.at[a, j], recv_sems.at[a, j], (x, y, 1 - c),
                                    phase != "finish"))
        if phase != "start":
            for cp in copies:
                cp.wait()

    return Job(gathered, [jax.ShapeDtypeStruct(g.shape, g.dtype) for g in gathered],
               [pltpu.SemaphoreType.DMA((n, 3)), pltpu.SemaphoreType.DMA((n, 3))], fn, in_place=True)


def swap_job(grads):
    n = len(grads)

    def fn(ins, outs, sems, phase):
        send_sems, recv_sems = sems
        x, y, c, _ = _place()
        copies = []
        for a in range(n):
            half = ins[a].shape[1] // 2
            copies.append(_send(ins[a].at[:, pl.ds((1 - c) * half, half), :], outs[a], send_sems.at[a],
                                recv_sems.at[a], (x, y, 1 - c), phase != "finish"))
        if phase != "start":
            for cp in copies:
                cp.wait()

    return Job(grads, [jax.ShapeDtypeStruct((g.shape[0], g.shape[1] // 2, g.shape[2]), g.dtype) for g in grads],
               [pltpu.SemaphoreType.DMA((n,)), pltpu.SemaphoreType.DMA((n,))], fn)


def _chip_slot(s, k):
    rel = s ^ k
    return (rel >> 1) | ((rel & 1) << 1)


def sum_cores(g, theirs, ck, out_dtype, *, name):
    n, r, cols = g.shape
    half = r // 2
    tr = _tile(half, max(16, (1 << 19) // cols), 16)
    nb = half // tr

    def body(ck_ref, g_ref, t_ref, o_ref):
        o_ref[...] = (g_ref[...] + t_ref[...]).astype(o_ref.dtype)

    return pl.pallas_call(
        body, name=name,
        grid_spec=pltpu.PrefetchScalarGridSpec(
            num_scalar_prefetch=1, grid=(n, nb),
            in_specs=[pl.BlockSpec((None, tr, cols), lambda s, i, ck_ref: (s, ck_ref[0] * nb + i, 0)),
                      pl.BlockSpec((None, tr, cols), lambda s, i, ck_ref: (s, i, 0))],
            out_specs=pl.BlockSpec((None, tr, cols), lambda s, i, ck_ref: (_chip_slot(s, ck_ref[1]), i, 0))),
        out_shape=jax.ShapeDtypeStruct((n, half, cols), out_dtype),
        compiler_params=_params("parallel", "parallel"))(ck, g, theirs)


def scatter_job(parts):
    n = len(parts)

    def fn(ins, outs, sems, phase):
        send_sems, recv_sems = sems
        x, y, c, chips = _place()
        copies = []
        for a in range(n):
            for j, (px, py) in enumerate(chips):
                copies.append(_send(ins[a].at[1 + j], outs[a].at[j], send_sems.at[a, j], recv_sems.at[a, j],
                                    (px, py, c), phase != "finish"))
        if phase != "start":
            for cp in copies:
                cp.wait()

    return Job(parts, [jax.ShapeDtypeStruct((3,) + p.shape[1:], p.dtype) for p in parts],
               [pltpu.SemaphoreType.DMA((n, 3)), pltpu.SemaphoreType.DMA((n, 3))], fn)


def sum_chips(own, others, ck, *, name):
    _, r, cols = own.shape
    tr = _tile(r, max(16, (1 << 19) // cols), 16)

    def body(ck_ref, a_ref, b0_ref, b1_ref, b2_ref, o_ref):
        o_ref[...] = (a_ref[...].astype(F32) + b0_ref[...].astype(F32) + b1_ref[...].astype(F32)
                      + b2_ref[...].astype(F32))

    other = lambda j: pl.BlockSpec((None, tr, cols), lambda i, ck_ref: (j, i, 0))
    return pl.pallas_call(
        body, name=name,
        grid_spec=pltpu.PrefetchScalarGridSpec(
            num_scalar_prefetch=1, grid=(r // tr,),
            in_specs=[pl.BlockSpec((None, tr, cols), lambda i, ck_ref: (0, i, 0)), other(0), other(1), other(2)],
            out_specs=pl.BlockSpec((None, tr, cols), lambda i, ck_ref: (ck_ref[0], i, 0))),
        out_shape=jax.ShapeDtypeStruct((2, r, cols), F32),
        compiler_params=_params("parallel"))(ck, own, others, others, others)


def share_job(joined):
    n = len(joined)

    def fn(ins, outs, sems, phase):
        send_sems, recv_sems = sems
        x, y, c, _ = _place()
        copies = [_send(outs[a].at[c], outs[a].at[c], send_sems.at[a], recv_sems.at[a], (x, y, 1 - c),
                        phase != "finish") for a in range(n)]
        if phase != "start":
            for cp in copies:
                cp.wait()

    return Job(joined, [jax.ShapeDtypeStruct(j.shape, j.dtype) for j in joined],
               [pltpu.SemaphoreType.DMA((n,)), pltpu.SemaphoreType.DMA((n,))], fn, in_place=True)


def kernel(x, meta_tokens, norm0_g, l0_w_in, l0_lam_re, l0_lam_im, l0_log_dt, l0_b_re, l0_b_im, l0_c_re, l0_c_im, l0_d_skip, l0_w_glu, l0_b_glu, l0_w_out, norm1_g, l1_w_in, l1_conv_w, l1_conv_b, l1_w_out, norm2_g, l2_w_in, l2_w_grp, l2_b_grp, l2_scale, l2_w_out, norm3_g, l3_w_in, l3_lam_re, l3_lam_im, l3_log_dt, l3_b_re, l3_b_im, l3_c_re, l3_c_im, l3_d_skip, l3_w_glu, l3_b_glu, l3_w_out, final_g, loss_target, m_meta_tokens, m_norm0_g, m_l0_w_in, m_l0_lam_re, m_l0_lam_im, m_l0_log_dt, m_l0_b_re, m_l0_b_im, m_l0_c_re, m_l0_c_im, m_l0_d_skip, m_l0_w_glu, m_l0_b_glu, m_l0_w_out, m_norm1_g, m_l1_w_in, m_l1_conv_w, m_l1_conv_b, m_l1_w_out, m_norm2_g, m_l2_w_in, m_l2_w_grp, m_l2_b_grp, m_l2_scale, m_l2_w_out, m_norm3_g, m_l3_w_in, m_l3_lam_re, m_l3_lam_im, m_l3_log_dt, m_l3_b_re, m_l3_b_im, m_l3_c_re, m_l3_c_im, m_l3_d_skip, m_l3_w_glu, m_l3_b_glu, m_l3_w_out, m_final_g, v_meta_tokens, v_norm0_g, v_l0_w_in, v_l0_lam_re, v_l0_lam_im, v_l0_log_dt, v_l0_b_re, v_l0_b_im, v_l0_c_re, v_l0_c_im, v_l0_d_skip, v_l0_w_glu, v_l0_b_glu, v_l0_w_out, v_norm1_g, v_l1_w_in, v_l1_conv_w, v_l1_conv_b, v_l1_w_out, v_norm2_g, v_l2_w_in, v_l2_w_grp, v_l2_b_grp, v_l2_scale, v_l2_w_out, v_norm3_g, v_l3_w_in, v_l3_lam_re, v_l3_lam_im, v_l3_log_dt, v_l3_b_re, v_l3_b_im, v_l3_c_re, v_l3_c_im, v_l3_d_skip, v_l3_w_glu, v_l3_b_glu, v_l3_w_out, v_final_g):
    given = dict(locals())
    w = {n: given[n] for n in WEIGHTS}
    mom_m = {n: given["m_" + n] for n in WEIGHTS}
    mom_v = {n: given["v_" + n] for n in WEIGHTS}
    seq = x.shape[1]
    real = N_META + seq
    lp = -(-real // SCAN_ROWS) * SCAN_ROWS
    chip = 2 * lax.axis_index("x") + lax.axis_index("y")
    row = lambda a: a.reshape(1, -1)

    shard_bf16 = {n: (w[n].reshape(-1, w[n].shape[-1]) if n == 'l2_w_grp' else w[n]).astype(BF16) for n in BIG}
    layer = lambda i: [n for n in BIG if n.startswith("l%d_" % i)]
    over_ici = lambda names: gather_halves_job([shard_bf16[n] for n in names])
    full = {}

    def landed(names, arrays):
        for n, arr in zip(names, arrays, strict=True):
            if n.endswith('w_in'):
                full[n] = arr
            elif n == 'l2_w_grp':
                full[n] = arr.reshape(N_CHIPS, len(POOL_WINDOWS), POOL_GROUP // N_CHIPS, POOL_GROUP)
            else:
                full[n] = arr.reshape(1, -1, arr.shape[-1])

    landed(['l0_w_in'], run_job(gather_by_halves_job([shard_bf16['l0_w_in']]), name="gather_l0_w_in"))
    full.update(zip(SMALL_SHARDED, gather_chips([w[n] for n in SMALL_SHARDED], name="gather_small_shards"),
                    strict=True))
    meta_full = full['meta_tokens'].transpose(1, 0, 2).reshape(N_META, D_MODEL)
    conv_w_full = full['l1_conv_w'].transpose(1, 0, 2).reshape(3, CONV_E)
    b_grp_full = full['l2_b_grp'].transpose(1, 0, 2).reshape(len(POOL_WINDOWS), POOL_GROUP)

    h0 = jnp.concatenate([meta_full, x[0], jnp.zeros((lp - real, D_MODEL), F32)], axis=0)
    target = jnp.concatenate([jnp.zeros((N_META, D_MODEL), F32), loss_target[0],
                              jnp.zeros((lp - real, D_MODEL), F32)], axis=0)
    grads = {}
    saved = {}
    gip = lambda a: jnp.swapaxes(a, 1, 2)

    def s5_layer_fwd(i, h, first=False):
        pre = "l%d_" % i
        prm = [w[pre + 'lam_re'], w[pre + 'lam_im'], w[pre + 'log_dt'].reshape(-1, 1), gip(w[pre + 'b_re']),
               gip(w[pre + 'b_im']), w[pre + 'c_re'], w[pre + 'c_im']]
        a_re, a_im, bblk, cblk = s5_prep(prm, name=pre + "prep")
        tables = (a_re.reshape(1, -1), a_im.reshape(1, -1), bblk, cblk)
        ar, ai = tables[:2]
        gain = row(w["norm%d_g" % i])
        rest = [pre + 'w_glu', pre + 'w_out']
        if not first:
            uz, n = norm_mm_nn(h, gain, full[pre + 'w_in'], name=pre + "in")
            y0, yg, sr, si = s5_fwd(uz, bblk, cblk, ar, ai, row(w[pre + 'd_skip']), name=pre + "scan")
            q, y3 = glu_gate(yg, full[pre + 'w_glu'], row(w[pre + 'b_glu']), y0, uz, name=pre + "glu")
        else:
            (uz, n), halves = norm_mm_nn(h, gain, full[pre + 'w_in'], name=pre + "in", job=over_ici(rest))
            (y0, yg, sr, si), arrived = s5_fwd(
                uz, bblk, cblk, ar, ai, row(w[pre + 'd_skip']), name=pre + "scan",
                job=Job.both(over_ici(layer(1)), forward_halves_job(halves)))
            landed(rest, arrived[len(layer(1)):])
            (q, y3), whole = glu_gate(yg, full[pre + 'w_glu'], row(w[pre + 'b_glu']), y0, uz, name=pre + "glu",
                                      job=forward_halves_job(arrived[:len(layer(1))]))
            landed(layer(1), whole)
        h_new = mm_nn(y3, full[pre + 'w_out'], add=h, name=pre + "out")
        saved[i] = dict(h=h, n=n, uz=uz, y0=y0, sr=sr, si=si, yg=yg, q=q, y3=y3, tables=tables, prm=prm)
        return h_new

    def s5_layer_bwd(i, dh, carry=None):
        pre = "l%d_" % i
        s = saved[i]
        ar, ai, bblk, cblk = s['tables']
        grads[pre + 'w_out'] = mm_tn(s['y3'], dh, 1, name=pre + "d_w_out")[0]
        dy3 = mm_nt(dh, full[pre + 'w_out'], name=pre + "d_y3")

        def gate_bwd(dyv, yv, qv, zv):
            _, vjp = jax.vjp(_s5_gate, yv, qv, zv)
            dy0, dq, dz = vjp(dyv)
            return [dy0, dq, dz], [_colsum(dq)]

        (dy0_direct, dq, dp), (db_glu,) = rowwise(
            gate_bwd, [(dy3, D_MODEL, 0), (s['y0'], D_MODEL, 0), (s['q'], D_MODEL, 0), (s['uz'], D_MODEL, 1)], [],
            [(D_MODEL, F32, D_MODEL, 0), (D_MODEL, BF16, D_MODEL, 0), (2 * D_MODEL, BF16, D_MODEL, 1)], [D_MODEL],
            name=pre + "d_gate")
        grads[pre + 'b_glu'] = db_glu
        grads[pre + 'w_glu'] = mm_tn(s['yg'], dq, 1, name=pre + "d_w_glu")[0]
        dyg = mm_nt(dq, full[pre + 'w_glu'], name=pre + "d_yg")
        res = s5_bwd(dy0_direct, dyg, s['y0'], s['uz'], s['sr'], s['si'], bblk, cblk, ar, ai,
                     row(w[pre + 'd_skip']), dp, name=pre + "d_scan",
                     job=None if carry is None else scatter_job(scatter_of(carry)))
        if carry is not None:
            res, arrived = res
            scattered(carry, arrived)
        dp, dbblk, dcblk, dar, dai, dd = res
        grads[pre + 'd_skip'] = dd
        cots = (dar.reshape(S5_GROUPS, S5_STATE), dai.reshape(S5_GROUPS, S5_STATE), dbblk, dcblk)
        for key, val in zip(('lam_re', 'lam_im', 'log_dt', 'b_re', 'b_im', 'c_re', 'c_im'),
                            s5_prep_bwd(s['prm'], cots, name=pre + "d_prep"), strict=True):
            grads[pre + key] = val.reshape(-1) if key == 'log_dt' else val
        if carry is None:
            grads[pre + 'w_in'] = mm_tn(s['n'], dp, N_CHIPS, name=pre + "d_w_in")
        else:
            later = [n for n in BIG if not n.startswith(pre)]
            sums = [sum_chips(pair[n], from_chips[n], ck, name="sum_chips_" + n) for n in later]
            grads[pre + 'w_in'], shared = mm_tn(s['n'], dp, N_CHIPS, name=pre + "d_w_in", job=share_job(sums))
            joined.update(zip(later, shared, strict=True))
        names, pieces = core_parts(i)
        (dh_in, dg), theirs = mm_nt_norm_bwd(dp, full[pre + 'w_in'], s['h'], row(w["norm%d_g" % i]), dh,
                                             name=pre + "d_n", job=swap_job(pieces))
        sum_over_cores(names, pieces, theirs)
        grads["norm%d_g" % i] = dg
        return dh_in

    h1 = s5_layer_fwd(0, h0, first=True)

    (p1, n1), halves = norm_mm_nn(h1, row(w['norm1_g']), full['l1_w_in'], name="l1_in", job=over_ici(layer(2)))
    y3_1 = conv_fwd(p1, conv_w_full, row(w['l1_conv_b']), name="l1_conv")
    h2, whole = mm_nn(y3_1, full['l1_w_out'], add=h1, name="l1_out", job=forward_halves_job(halves))
    landed(layer(2), whole)

    (p2, n2), halves = norm_mm_nn(h2, row(w['norm2_g']), full['l2_w_in'], name="l2_in", job=over_ici(layer(3)))
    mixed = pool_fwd(p2, name="l2_pool")
    o2 = grp_nn(mixed, full['l2_w_grp'], b_grp_full.reshape(1, -1), name="l2_grp")
    (y3_2,), _ = rowwise(lambda ov, zv, sv: ([_pool_gate(ov, zv, sv)], []),
                         [(o2, POOL_E, 0), (p2, POOL_E, 1)], [row(w['l2_scale'])],
                         [(POOL_E, BF16, POOL_E, 0)], name="l2_gate")
    h3, whole = mm_nn(y3_2, full['l2_w_out'], add=h2, name="l2_out", job=forward_halves_job(halves))
    landed(layer(3), whole)

    h4 = s5_layer_fwd(3, h3)

    def loss_fn(hv, tv, gv):
        i = pl.program_id(0)
        tr = hv.shape[0]
        yf, vjp = jax.vjp(_rms, hv, gv)
        pos = i * tr + lax.broadcasted_iota(jnp.int32, (tr, 1), 0)
        diff = jnp.where((pos >= N_META) & (pos < real), yf - tv, 0.0)
        dh, dg = vjp(diff / D_MODEL)
        return [dh], [dg, _colsum(diff * diff)]

    (dh,), (d_final_g, sq) = rowwise(loss_fn, [(h4, D_MODEL, 0), (target, D_MODEL, 0)], [row(w['final_g'])],
                                     [(D_MODEL, F32, D_MODEL, 0)], [D_MODEL, D_MODEL], name="loss")
    grads['final_g'] = d_final_g
    loss = lax.psum(0.5 / D_MODEL * jnp.sum(sq), ("x", "y", "c"))

    ck = jnp.stack([lax.axis_index("c"), chip]).astype(jnp.int32)
    pair, from_chips, joined = {}, {}, {}

    def chip_major(n):
        g = grads[n]
        if n.endswith('w_in'):
            return g
        if n == 'l2_w_grp':
            return g.reshape(N_CHIPS, -1, POOL_GROUP)
        return g.reshape(N_CHIPS, -1, g.shape[-1])

    def core_parts(i):
        return layer(i), [chip_major(n) for n in layer(i)]

    def sum_over_cores(names, pieces, theirs):
        for n, g, t in zip(names, pieces, theirs, strict=True):
            pair[n] = sum_cores(g, t, ck, F32 if n == 'small' else BF16, name="sum_cores_" + n)

    scatter_of = lambda i: [pair[n] for n in layer(i)]

    def scattered(i, arrays):
        from_chips.update(zip(layer(i), arrays, strict=True))

    dh = s5_layer_bwd(3, dh)

    grads['l2_w_out'] = mm_tn(y3_2, dh, 1, name="l2_d_w_out")[0]
    dy3 = mm_nt(dh, full['l2_w_out'], name="l2_d_y3")

    def pool_gate_bwd(dyv, ov, zv, sv):
        _, vjp = jax.vjp(_pool_gate, ov, zv, sv)
        do, dz, dscale = vjp(dyv)
        return [do, dz], [dscale, _colsum(do)]

    (do2, dp2), (dscale, dbgrp) = rowwise(
        pool_gate_bwd, [(dy3, POOL_E, 0), (o2, POOL_E, 0), (p2, POOL_E, 1)], [row(w['l2_scale'])],
        [(POOL_E, BF16, POOL_E, 0), (2 * POOL_E, BF16, POOL_E, 1)], [POOL_E, POOL_E], name="l2_d_gate")
    grads['l2_scale'] = dscale
    grads['l2_b_grp'] = dbgrp
    grads['l2_w_grp'] = grp_tn(mixed, do2, name="l2_d_w_grp")
    dmix = grp_nt(do2, full['l2_w_grp'], name="l2_d_mixed")
    dp2 = pool_bwd(dmix, dp2, name="l2_d_pool")
    grads['l2_w_in'], arrived = mm_tn(n2, dp2, N_CHIPS, name="l2_d_w_in", job=scatter_job(scatter_of(3)))
    scattered(3, arrived)
    names, pieces = core_parts(2)
    (dh, dg), theirs = mm_nt_norm_bwd(dp2, full['l2_w_in'], h2, row(w['norm2_g']), dh, name="l2_d_n",
                                      job=swap_job(pieces))
    sum_over_cores(names, pieces, theirs)
    grads['norm2_g'] = dg

    grads['l1_w_out'] = mm_tn(y3_1, dh, 1, name="l1_d_w_out")[0]
    dy3 = mm_nt(dh, full['l1_w_out'], name="l1_d_y3")
    dp1, dcw, dcb = conv_bwd(dy3, p1, conv_w_full, row(w['l1_conv_b']), name="l1_d_conv")
    grads['l1_conv_w'] = dcw
    grads['l1_conv_b'] = dcb
    grads['l1_w_in'], arrived = mm_tn(n1, dp1, N_CHIPS, name="l1_d_w_in", job=scatter_job(scatter_of(2)))
    scattered(2, arrived)
    names, pieces = core_parts(1)
    (dh, dg), theirs = mm_nt_norm_bwd(dp1, full['l1_w_in'], h1, row(w['norm1_g']), dh, name="l1_d_n",
                                      job=swap_job(pieces))
    sum_over_cores(names, pieces, theirs)
    grads['norm1_g'] = dg

    dh = s5_layer_bwd(0, dh, carry=1)
    grad_x = dh[N_META:real][None]
    grads['meta_tokens'] = dh[:N_META]

    wide = [n for n in SMALL if w[n].ndim == 3]
    wire = [grads[n].reshape(S5_GROUPS, -1) if n in wide else grads[n] for n in SMALL]
    starts = dict(zip(SMALL, _wire_plan([a.shape for a in wire]), strict=True))
    small_local = pack_small(wire, name="pack_small")
    pieces = [small_local.reshape(N_CHIPS, -1, WIRE_WIDTH)]
    sum_over_cores(['small'], pieces, run_job(swap_job(pieces), name="reduce_cores_small"))

    out_g, out_d, out_m, out_v = {}, {}, {}, {}
    as2d = lambda a: a.reshape(-1, a.shape[-1])

    def update_big(names, job=None):
        for n in names:
            out_g[n] = joined[n].reshape(w[n].shape)
        res = adamw_many([(as2d(w[n]), as2d(out_g[n]), as2d(mom_m[n]), as2d(mom_v[n])) for n in names],
                         name="adamw_" + names[0][:2] + "_on", job=job)
        triples, carried = res if job is not None else (res, None)
        for n, (d, nm, nv) in zip(names, triples, strict=True):
            out_d[n], out_m[n], out_v[n] = d.reshape(w[n].shape), nm.reshape(w[n].shape), nv.reshape(w[n].shape)
        return carried

    rest = layer(0) + ['small']
    later = [n for n in BIG if n not in rest]
    from_chips.update(zip(rest, update_big(later, scatter_job([pair[n] for n in rest])), strict=True))
    sums = [sum_chips(pair[n], from_chips[n], ck, name="sum_chips_" + n) for n in rest]
    joined.update(zip(rest, run_job(share_job(sums), name="share_cores"), strict=True))
    update_big(layer(0))
    (small_all,) = run_job(gather_by_halves_job([joined['small'].reshape(-1, WIRE_WIDTH)]), name="gather_small")
    small_all = small_all.reshape(WIRE_ROWS, WIRE_WIDTH)

    half_w = WIRE_WIDTH // 2
    shard_cells = {
        'meta_tokens': lambda k: [((0, N_META), (k * 256, (k + 1) * 256), (slice(None), slice(None)))],
        'l1_conv_w': lambda k: [(2 * j + k // 2, ((k % 2) * half_w, (k % 2 + 1) * half_w), (j, slice(None)))
                                for j in range(3)],
        'l2_b_grp': lambda k: [(j // 2, ((j % 2) * half_w + k * 128, (j % 2) * half_w + (k + 1) * 128),
                                (j, slice(None))) for j in range(len(POOL_WINDOWS))],
    }
    plain = [n for n in SMALL if n not in wide]
    res = update_small(small_all, [starts[n] for n in plain], [(w[n], mom_m[n], mom_v[n]) for n in plain],
                       [shard_cells.get(n) for n in plain], chip, name="adamw_small")
    for n, (g, d, nm, nv) in zip(plain, res, strict=True):
        out_g[n], out_d[n], out_m[n], out_v[n] = g, d, nm, nv
    as_gip = lambda n, a: gip(a) if n.endswith(('b_re', 'b_im')) else a
    g_gip = {n: small_all[starts[n]:starts[n] + S5_GROUPS].reshape(S5_GROUPS, S5_GROUP, S5_STATE) for n in wide}
    res = adamw_many([(as_gip(n, w[n]), g_gip[n], as_gip(n, mom_m[n]), as_gip(n, mom_v[n])) for n in wide],
                     name="adamw_s5_tensors")
    for n, (d, nm, nv) in zip(wide, res, strict=True):
        out_g[n], out_d[n], out_m[n], out_v[n] = as_gip(n, g_gip[n]), as_gip(n, d), as_gip(n, nm), as_gip(n, nv)

    return (loss, grad_x, *[out_g[n] for n in WEIGHTS], *[out_d[n] for n in WEIGHTS],
            *[out_m[n] for n in WEIGHTS], *[out_v[n] for n in WEIGHTS])
```

```python
import functools

import jax
import jax.numpy as jnp
from jax import lax
from jax.experimental import pallas as pl
from jax.experimental.pallas import tpu as pltpu

F32, BF16 = jnp.float32, jnp.bfloat16
MESH = pl.DeviceIdType.MESH

D_MODEL = 1024
N_META = 16
EPS = 1e-6
S5_GROUPS, S5_GROUP, S5_STATE = 64, 16, 64
LANE_BLOCK = 128
S5_BLOCKS = D_MODEL // LANE_BLOCK
GROUPS_PER_BLOCK = LANE_BLOCK // S5_GROUP
STATE_BLOCK = GROUPS_PER_BLOCK * S5_STATE
SCAN_ROWS = 256
SUBLANES = 8
CONV_E = 2048
POOL_E = 2048
POOL_WINDOWS = (2, 4, 8, 16)
POOL_GROUP = 512
POOL_HALO = 16
N_CHIPS = 4

ADAM_LR, ADAM_B1, ADAM_B2, ADAM_EPS, ADAM_WD, ADAM_STEP = 0.001, 0.9, 0.999, 1e-08, 0.01, 10

VMEM_LIMIT = 48 * 1024 * 1024
TN_OPERAND_BYTES = 28 * 1024 * 1024

WEIGHTS = ['meta_tokens', 'norm0_g', 'l0_w_in', 'l0_lam_re', 'l0_lam_im', 'l0_log_dt', 'l0_b_re', 'l0_b_im',
           'l0_c_re', 'l0_c_im', 'l0_d_skip', 'l0_w_glu', 'l0_b_glu', 'l0_w_out', 'norm1_g', 'l1_w_in',
           'l1_conv_w', 'l1_conv_b', 'l1_w_out', 'norm2_g', 'l2_w_in', 'l2_w_grp', 'l2_b_grp', 'l2_scale',
           'l2_w_out', 'norm3_g', 'l3_w_in', 'l3_lam_re', 'l3_lam_im', 'l3_log_dt', 'l3_b_re', 'l3_b_im',
           'l3_c_re', 'l3_c_im', 'l3_d_skip', 'l3_w_glu', 'l3_b_glu', 'l3_w_out', 'final_g']
BIG = ['l0_w_in', 'l0_w_glu', 'l0_w_out', 'l1_w_in', 'l1_w_out', 'l2_w_in', 'l2_w_grp', 'l2_w_out',
       'l3_w_in', 'l3_w_glu', 'l3_w_out']
SMALL_SHARDED = ['meta_tokens', 'l1_conv_w', 'l2_b_grp']
SMALL = [n for n in WEIGHTS if n not in BIG]


def _params(*sem):
    return pltpu.CompilerParams(dimension_semantics=sem, vmem_limit_bytes=VMEM_LIMIT)


class Job:
    def __init__(self, arrays, out_shapes, sems, fn, in_place=False):
        self.arrays, self.out_shapes, self.sems, self.fn = list(arrays), list(out_shapes), list(sems), fn
        self.in_place = in_place
        assert len(self.arrays) == len(self.out_shapes)

    @staticmethod
    def both(first, second):
        na, ns = len(first.arrays), len(first.sems)

        def fn(ins, outs, sems, phase):
            first.fn(ins[:na], outs[:na], sems[:ns], phase)
            second.fn(ins[na:], outs[na:], sems[ns:], phase)

        job = Job(first.arrays + second.arrays, first.out_shapes + second.out_shapes, first.sems + second.sems, fn)
        job.in_place = [first.in_place] * na + [second.in_place] * len(second.arrays)
        return job

    def aliased(self):
        flags = self.in_place if isinstance(self.in_place, list) else [self.in_place] * len(self.arrays)
        return [a for a, flag in enumerate(flags) if flag]


def _call(body, *, name, grid, in_specs, out_specs, out_shape, args, semantics, scratch=(), aliases=None, job=None):
    if job is None:
        return pl.pallas_call(
            body, name=name, grid=grid, in_specs=list(in_specs), out_specs=list(out_specs),
            out_shape=list(out_shape), scratch_shapes=list(scratch), input_output_aliases=aliases or {},
            compiler_params=_params(*semantics))(*args)
    counts = [len(in_specs), len(job.arrays), len(out_specs), len(job.out_shapes), len(scratch), len(job.sems)]

    def hosted(*refs):
        parts, pos = [], 0
        for k in counts:
            parts.append(refs[pos:pos + k])
            pos += k
        ins, job_ins, outs, job_outs, scr, job_sems = parts
        steps = [pl.program_id(d) for d in range(len(grid))]
        first = functools.reduce(jnp.logical_and, [s == 0 for s in steps])
        last = functools.reduce(jnp.logical_and, [s == g - 1 for s, g in zip(steps, grid, strict=True)])

        @pl.when(first)
        def _():
            job.fn(job_ins, job_outs, job_sems, "start")

        body(*ins, *outs, *scr)

        @pl.when(last)
        def _():
            job.fn(job_ins, job_outs, job_sems, "finish")

    any_spec = pl.BlockSpec(memory_space=pl.ANY)
    aliases = dict(aliases or {})
    aliases.update({len(in_specs) + a: len(out_specs) + a for a in job.aliased()})
    res = pl.pallas_call(
        hosted, name=name, grid=grid, in_specs=list(in_specs) + [any_spec] * len(job.arrays),
        out_specs=list(out_specs) + [any_spec] * len(job.out_shapes),
        out_shape=list(out_shape) + job.out_shapes, scratch_shapes=list(scratch) + job.sems,
        input_output_aliases=aliases,
        compiler_params=_params(*["arbitrary"] * len(grid)))(*args, *job.arrays)
    return res[:len(out_specs)], res[len(out_specs):]


def _tile(n, cap, mult):
    best = None
    for t in range(mult, min(n, cap) + 1, mult):
        if n % t == 0:
            best = t
    assert best is not None, (n, cap, mult)
    return best


def _with_job(res, job):
    return res[0] if job is None else (res[0][0], res[1])


def mm_nn(a, b, *, name, bias=None, add=None, out_dtype=F32, job=None):
    m, k = a.shape
    s, _, ns = b.shape
    n = s * ns
    tm, tn = _tile(m, 1088, 16), _tile(ns, 1024, 128)
    per = ns // tn

    def body(*refs):
        a_ref, b_ref = refs[0], refs[1]
        o_ref = refs[-1]
        acc = jnp.dot(a_ref[...].astype(BF16), b_ref[...], preferred_element_type=F32)
        pos = 2
        if bias is not None:
            acc = acc + refs[pos][...]
            pos += 1
        if add is not None:
            acc = acc + refs[pos][...]
        o_ref[...] = acc.astype(o_ref.dtype)

    in_specs = [pl.BlockSpec((tm, k), lambda i, j: (i, 0)),
                pl.BlockSpec((None, k, tn), lambda i, j: (j // per, 0, j % per))]
    args = [a, b]
    if bias is not None:
        in_specs.append(pl.BlockSpec((1, tn), lambda i, j: (0, j)))
        args.append(bias)
    if add is not None:
        in_specs.append(pl.BlockSpec((tm, tn), lambda i, j: (i, j)))
        args.append(add)
    return _with_job(_call(
        body, name=name, grid=(m // tm, n // tn), in_specs=in_specs,
        out_specs=[pl.BlockSpec((tm, tn), lambda i, j: (i, j))],
        out_shape=[jax.ShapeDtypeStruct((m, n), out_dtype)], args=args,
        semantics=("parallel", "parallel"), job=job), job)


def norm_mm_nn(h, gain, b, *, name, job=None):
    m, k = h.shape
    s, _, ns = b.shape
    n = s * ns
    tm, tn = _tile(m, 1088, 16), _tile(ns, 1024, 128)
    per = ns // tn

    def body(h_ref, g_ref, b_ref, o_ref, n_ref):
        @pl.when(pl.program_id(1) == 0)
        def _():
            n_ref[...] = _rms(h_ref[...], g_ref[...]).astype(n_ref.dtype)

        o_ref[...] = jnp.dot(n_ref[...], b_ref[...], preferred_element_type=F32)

    res = _call(
        body, name=name, grid=(m // tm, n // tn),
        in_specs=[pl.BlockSpec((tm, k), lambda i, j: (i, 0)), pl.BlockSpec((1, k), lambda i, j: (0, 0)),
                  pl.BlockSpec((None, k, tn), lambda i, j: (j // per, 0, j % per))],
        out_specs=[pl.BlockSpec((tm, tn), lambda i, j: (i, j)), pl.BlockSpec((tm, k), lambda i, j: (i, 0))],
        out_shape=[jax.ShapeDtypeStruct((m, n), F32), jax.ShapeDtypeStruct((m, k), BF16)], args=(h, gain, b),
        semantics=("parallel", "arbitrary"), job=job)
    return (res[0], res[1]) if job is None else ((res[0][0], res[0][1]), res[1])


def glu_gate(yg, w_glu, b_glu, y0, uz, *, name, job=None):
    m, k = yg.shape
    n = w_glu.shape[2]
    tm = _tile(m, 1088, 16)

    def body(a_ref, w_ref, b_ref, y0_ref, z_ref, q_ref, o_ref):
        q = jnp.dot(a_ref[...], w_ref[...], preferred_element_type=F32) + b_ref[...]
        q_ref[...] = q
        o_ref[...] = _s5_gate(y0_ref[...], q, z_ref[...]).astype(o_ref.dtype)

    tile = pl.BlockSpec((tm, n), lambda i: (i, 0))
    res = _call(
        body, name=name, grid=(m // tm,),
        in_specs=[pl.BlockSpec((tm, k), lambda i: (i, 0)), pl.BlockSpec((None, k, n), lambda i: (0, 0, 0)),
                  pl.BlockSpec((1, n), lambda i: (0, 0)), tile, pl.BlockSpec((tm, n), lambda i: (i, 1))],
        out_specs=[tile, tile],
        out_shape=[jax.ShapeDtypeStruct((m, n), F32), jax.ShapeDtypeStruct((m, n), BF16)],
        args=(yg, w_glu, b_glu, y0, uz), semantics=("parallel",), job=job)
    return (res[0], res[1]) if job is None else ((res[0][0], res[0][1]), res[1])


def s5_gate_bwd(dh, w_out, y0, q, uz, *, name):
    m, d = dh.shape
    e = w_out.shape[1]
    tm = _tile(m, 544, 16)

    def body(dh_ref, w_ref, y0_ref, q_ref, z_ref, dy0_ref, dq_ref, dz_ref, db_ref):
        i = pl.program_id(0)
        dy3 = lax.dot_general(dh_ref[...].astype(BF16), w_ref[...], (((1,), (1,)), ((), ())),
                              preferred_element_type=F32)
        dy0, dq, dz = jax.vjp(_s5_gate, y0_ref[...], q_ref[...], z_ref[...])[1](dy3)
        dy0_ref[...] = dy0
        dq_ref[...] = dq.astype(dq_ref.dtype)
        dz_ref[...] = dz.astype(dz_ref.dtype)
        db = _colsum(dq)

        @pl.when(i == 0)
        def _():
            db_ref[...] = db

        @pl.when(i > 0)
        def _():
            db_ref[...] += db

    tile = pl.BlockSpec((tm, e), lambda i: (i, 0))
    right = pl.BlockSpec((tm, e), lambda i: (i, 1))
    return pl.pallas_call(
        body, name=name, grid=(m // tm,),
        in_specs=[pl.BlockSpec((tm, d), lambda i: (i, 0)), pl.BlockSpec((None, e, d), lambda i: (0, 0, 0)),
                  tile, tile, right],
        out_specs=[tile, tile, right, pl.BlockSpec((1, e), lambda i: (0, 0))],
        out_shape=[jax.ShapeDtypeStruct((m, e), F32), jax.ShapeDtypeStruct((m, e), BF16),
                   jax.ShapeDtypeStruct((m, 2 * e), BF16), jax.ShapeDtypeStruct((1, e), F32)],
        compiler_params=_params("arbitrary"))(dh, w_out, y0, q, uz)


def pool_gate_bwd(dh, w_out, o, p, scale, *, name):
    m, d = dh.shape
    e = w_out.shape[1]
    tm, te = _tile(m, 1088, 16), 1024
    nj = e // te

    def body(dh_ref, w_ref, o_ref, z_ref, s_ref, do_ref, dz_ref, ds_ref, db_ref):
        i = pl.program_id(1)
        dy3 = lax.dot_general(dh_ref[...].astype(BF16), w_ref[...], (((1,), (1,)), ((), ())),
                              preferred_element_type=F32)
        do, dz, ds = jax.vjp(_pool_gate, o_ref[...], z_ref[...], s_ref[...])[1](dy3)
        do_ref[...] = do.astype(do_ref.dtype)
        dz_ref[...] = dz.astype(dz_ref.dtype)
        db = _colsum(do)

        @pl.when(i == 0)
        def _():
            ds_ref[...] = ds
            db_ref[...] = db

        @pl.when(i > 0)
        def _():
            ds_ref[...] += ds
            db_ref[...] += db

    tile = pl.BlockSpec((tm, te), lambda j, i: (i, j))
    right = pl.BlockSpec((tm, te), lambda j, i: (i, nj + j))
    vec = pl.BlockSpec((1, te), lambda j, i: (0, j))
    return pl.pallas_call(
        body, name=name, grid=(nj, m // tm),
        in_specs=[pl.BlockSpec((tm, d), lambda j, i: (i, 0)), pl.BlockSpec((None, te, d), lambda j, i: (0, j, 0)),
                  tile, right, vec],
        out_specs=[tile, right, vec, vec],
        out_shape=[jax.ShapeDtypeStruct((m, e), BF16), jax.ShapeDtypeStruct((m, 2 * e), BF16),
                   jax.ShapeDtypeStruct((1, e), F32), jax.ShapeDtypeStruct((1, e), F32)],
        compiler_params=_params("parallel", "arbitrary"))(dh, w_out, o, p, scale)


def mm_nt_norm_bwd(g, w, h, gain, dh_out, *, name, job=None):
    m, kc = g.shape
    s, nout, kcs = w.shape
    assert s * kcs == kc
    tm, tk = _tile(m, 1088, 16), _tile(kcs, 1024, 128)
    per = kcs // tk
    nk = kc // tk

    def body(g_ref, w_ref, h_ref, gain_ref, dh_ref, o_ref, dg_ref):
        i, kk = pl.program_id(0), pl.program_id(1)
        part = lax.dot_general(g_ref[...].astype(BF16), w_ref[...], (((1,), (1,)), ((), ())),
                               preferred_element_type=F32)

        @pl.when(kk == 0)
        def _():
            o_ref[...] = part

        @pl.when(kk > 0)
        def _():
            o_ref[...] += part

        @pl.when(kk == nk - 1)
        def _():
            dh, dg = jax.vjp(_rms, h_ref[...], gain_ref[...])[1](o_ref[...])
            o_ref[...] = dh_ref[...] + dh

            @pl.when(i == 0)
            def _():
                dg_ref[...] = dg

            @pl.when(i > 0)
            def _():
                dg_ref[...] += dg

    row_tile = pl.BlockSpec((tm, nout), lambda i, kk: (i, 0))
    res = _call(
        body, name=name, grid=(m // tm, nk),
        in_specs=[pl.BlockSpec((tm, tk), lambda i, kk: (i, kk)),
                  pl.BlockSpec((None, nout, tk), lambda i, kk: (kk // per, 0, kk % per)),
                  row_tile, pl.BlockSpec((1, nout), lambda i, kk: (0, 0)), row_tile],
        out_specs=[row_tile, pl.BlockSpec((1, nout), lambda i, kk: (0, 0))],
        out_shape=[jax.ShapeDtypeStruct((m, nout), F32), jax.ShapeDtypeStruct((1, nout), F32)],
        args=(g, w, h, gain, dh_out), semantics=("arbitrary", "arbitrary"), job=job)
    return (res[0], res[1]) if job is None else ((res[0][0], res[0][1]), res[1])


def mm_nt(g, w, *, name, job=None):
    m, kc = g.shape
    s, nout, kcs = w.shape
    assert s * kcs == kc
    tm, tno, tk = _tile(m, 2176, 16), _tile(nout, 1024, 128), _tile(kcs, 1024, 128)
    per = kcs // tk

    def body(g_ref, w_ref, o_ref):
        kk = pl.program_id(2)
        part = lax.dot_general(g_ref[...].astype(BF16), w_ref[...], (((1,), (1,)), ((), ())),
                               preferred_element_type=F32)

        @pl.when(kk == 0)
        def _():
            o_ref[...] = part

        @pl.when(kk > 0)
        def _():
            o_ref[...] += part

    return _with_job(_call(
        body, name=name, grid=(m // tm, nout // tno, kc // tk),
        in_specs=[pl.BlockSpec((tm, tk), lambda i, j, kk: (i, kk)),
                  pl.BlockSpec((None, tno, tk), lambda i, j, kk: (kk // per, j, kk % per))],
        out_specs=[pl.BlockSpec((tm, tno), lambda i, j, kk: (i, j))],
        out_shape=[jax.ShapeDtypeStruct((m, nout), F32)], args=(g, w),
        semantics=("parallel", "parallel", "arbitrary"), job=job), job)


def mm_tn(a, g, shards, *, name, job=None):
    m, ka = a.shape
    n = g.shape[1]
    ns = n // shards
    tka, tn = _tile(ka, 1024, 128), _tile(ns, 512, 128)
    row_bytes = tka * jnp.dtype(a.dtype).itemsize + tn * jnp.dtype(g.dtype).itemsize
    tm = _tile(m, TN_OPERAND_BYTES // (2 * row_bytes), 16)
    per = ns // tn

    def body(a_ref, g_ref, o_ref):
        mm = pl.program_id(2)
        part = lax.dot_general(a_ref[...].astype(BF16), g_ref[...].astype(BF16), (((0,), (0,)), ((), ())),
                               preferred_element_type=F32)

        @pl.when(mm == 0)
        def _():
            o_ref[...] = part

        @pl.when(mm > 0)
        def _():
            o_ref[...] += part

    return _with_job(_call(
        body, name=name, grid=(ka // tka, n // tn, m // tm),
        in_specs=[pl.BlockSpec((tm, tka), lambda i, j, mm: (mm, i)),
                  pl.BlockSpec((tm, tn), lambda i, j, mm: (mm, j))],
        out_specs=[pl.BlockSpec((None, tka, tn), lambda i, j, mm: (j // per, i, j % per))],
        out_shape=[jax.ShapeDtypeStruct((shards, ka, ns), F32)], args=(a, g),
        semantics=("parallel", "parallel", "arbitrary"), job=job), job)


def grp_nn_gate(a, w, bias, p, scale, *, name):
    m = a.shape[0]
    tm = _tile(m, 1088, 16)
    ng = len(POOL_WINDOWS)

    def body(a_ref, w_ref, b_ref, z_ref, s_ref, o_ref, y_ref):
        wj = w_ref[...].reshape(POOL_GROUP, POOL_GROUP)
        o = jnp.dot(a_ref[...].astype(BF16), wj, preferred_element_type=F32) + b_ref[...]
        o_ref[...] = o
        y_ref[...] = _pool_gate(o, z_ref[...], s_ref[...]).astype(y_ref.dtype)

    tile = pl.BlockSpec((tm, POOL_GROUP), lambda i, j: (i, j))
    vec = pl.BlockSpec((1, POOL_GROUP), lambda i, j: (0, j))
    return pl.pallas_call(
        body, name=name, grid=(m // tm, ng),
        in_specs=[tile,
                  pl.BlockSpec((N_CHIPS, None, POOL_GROUP // N_CHIPS, POOL_GROUP), lambda i, j: (0, j, 0, 0)),
                  vec, pl.BlockSpec((tm, POOL_GROUP), lambda i, j: (i, ng + j)), vec],
        out_specs=[tile, tile],
        out_shape=[jax.ShapeDtypeStruct((m, ng * POOL_GROUP), F32), jax.ShapeDtypeStruct((m, ng * POOL_GROUP), BF16)],
        compiler_params=_params("parallel", "parallel"))(a, w, bias, p, scale)


def grp_nt(g, w, *, name):
    m = g.shape[0]
    tm = _tile(m, 1088, 16)
    ng = len(POOL_WINDOWS)

    def body(g_ref, w_ref, o_ref):
        wj = w_ref[...].reshape(POOL_GROUP, POOL_GROUP)
        o_ref[...] = lax.dot_general(g_ref[...].astype(BF16), wj, (((1,), (1,)), ((), ())),
                                     preferred_element_type=F32)

    return pl.pallas_call(
        body, name=name, grid=(m // tm, ng),
        in_specs=[pl.BlockSpec((tm, POOL_GROUP), lambda i, j: (i, j)),
                  pl.BlockSpec((N_CHIPS, None, POOL_GROUP // N_CHIPS, POOL_GROUP), lambda i, j: (0, j, 0, 0))],
        out_specs=pl.BlockSpec((tm, POOL_GROUP), lambda i, j: (i, j)),
        out_shape=jax.ShapeDtypeStruct((m, ng * POOL_GROUP), F32),
        compiler_params=_params("parallel", "parallel"))(g, w)


def grp_tn(a, g, *, name):
    m = a.shape[0]
    tm = _tile(m, 1088, 16)
    ng = len(POOL_WINDOWS)
    rows = POOL_GROUP // N_CHIPS

    def body(a_ref, g_ref, o_ref):
        mm = pl.program_id(1)
        part = lax.dot_general(a_ref[...].astype(BF16), g_ref[...].astype(BF16), (((0,), (0,)), ((), ())),
                               preferred_element_type=F32).reshape(N_CHIPS, rows, POOL_GROUP)

        @pl.when(mm == 0)
        def _():
            o_ref[...] = part

        @pl.when(mm > 0)
        def _():
            o_ref[...] += part

    return pl.pallas_call(
        body, name=name, grid=(ng, m // tm),
        in_specs=[pl.BlockSpec((tm, POOL_GROUP), lambda j, mm: (mm, j)),
                  pl.BlockSpec((tm, POOL_GROUP), lambda j, mm: (mm, j))],
        out_specs=pl.BlockSpec((N_CHIPS, None, rows, POOL_GROUP), lambda j, mm: (0, j, 0, 0)),
        out_shape=jax.ShapeDtypeStruct((N_CHIPS, ng, rows, POOL_GROUP), F32),
        compiler_params=_params("parallel", "arbitrary"))(a, g)


def rowwise(fn, rows, bcast, outs, accs=(), *, name, aliases=None):
    m = rows[0][0].shape[0]
    tr = m // 16
    n_in = len(rows) + len(bcast)

    def body(*refs):
        vals = [r[...] for r in refs[:n_in]]
        o_vals, a_vals = fn(*vals)
        for ref, val in zip(refs[n_in:n_in + len(outs)], o_vals, strict=True):
            ref[...] = val.astype(ref.dtype)
        step = pl.program_id(0)
        for ref, val in zip(refs[n_in + len(outs):], a_vals, strict=True):
            @pl.when(step == 0)
            def _(ref=ref, val=val):
                ref[...] = val

            @pl.when(step > 0)
            def _(ref=ref, val=val):
                ref[...] += val

    in_specs = [pl.BlockSpec((tr, w), functools.partial(lambda i, cb: (i, cb), cb=cb)) for _, w, cb in rows]
    in_specs += [pl.BlockSpec(b.shape, lambda i: (0, 0)) for b in bcast]
    out_specs = [pl.BlockSpec((tr, w), functools.partial(lambda i, cb: (i, cb), cb=cb)) for _, _, w, cb in outs]
    out_specs += [pl.BlockSpec((1, w), lambda i: (0, 0)) for w in accs]
    out_shape = [jax.ShapeDtypeStruct((m, cols), dt) for cols, dt, _, _ in outs]
    out_shape += [jax.ShapeDtypeStruct((1, w), F32) for w in accs]
    res = pl.pallas_call(
        body, name=name, grid=(m // tr,), in_specs=in_specs, out_specs=out_specs, out_shape=out_shape,
        input_output_aliases=aliases or {},
        compiler_params=_params("arbitrary"))(*[r[0] for r in rows], *bcast)
    return res[:len(outs)], res[len(outs):]


def _rms(h, g):
    return h * lax.rsqrt(jnp.mean(h * h, axis=-1, keepdims=True) + EPS) * g


def _colsum(v):
    return jnp.sum(v, axis=0, keepdims=True)


def _s5_gate(y0, q, z):
    yg = jax.nn.gelu(y0)
    return yg * jax.nn.sigmoid(q) * jax.nn.silu(z)


def _pool_gate(o, z, scale):
    return o * scale * jax.nn.silu(z)


def _shift_rows(v, d, down):
    t = v.shape[0]
    row = lax.broadcasted_iota(jnp.int32, v.shape, 0)
    if down:
        return jnp.where(row >= d, pltpu.roll(v, d, 0), 0.0)
    return jnp.where(row < t - d, pltpu.roll(v, t - d, 0), 0.0)


def _cmul(ar, ai, br, bi):
    return ar * br - ai * bi, ar * bi + ai * br


def _scan(xr, xi, ar, ai, cr, ci, down):
    t, n = xr.shape
    nb = t // SUBLANES
    pw = [(ar, ai)]
    for _ in range(SUBLANES - 1):
        pw.append(_cmul(*pw[-1], ar, ai))
    sub = lax.broadcasted_iota(jnp.int32, (SUBLANES, n), 0)

    def table(exps):
        tr, ti = jnp.zeros((SUBLANES, n), F32), jnp.zeros((SUBLANES, n), F32)
        for r, e in enumerate(exps):
            if e:
                tr, ti = jnp.where(sub == r, pw[e - 1][0], tr), jnp.where(sub == r, pw[e - 1][1], ti)
        return tr[None], ti[None]

    sr, si = xr.reshape(nb, SUBLANES, n), xi.reshape(nb, SUBLANES, n)
    d = 1
    while d < SUBLANES:
        mr, mi = table([d if (r >= d if down else r < SUBLANES - d) else 0 for r in range(SUBLANES)])
        turn = d if down else SUBLANES - d
        hr, hi = pltpu.roll(sr, turn, 1), pltpu.roll(si, turn, 1)
        sr, si = sr + mr * hr - mi * hi, si + mr * hi + mi * hr
        d *= 2
    edge = SUBLANES - 1 if down else 0
    qr, qi = table([r + 1 if down else SUBLANES - r for r in range(SUBLANES)])
    qr, qi = qr[0], qi[0]
    in_r, in_i = jnp.broadcast_to(cr, (SUBLANES, n)), jnp.broadcast_to(ci, (SUBLANES, n))
    out_r, out_i = [None] * nb, [None] * nb
    for b in (range(nb) if down else reversed(range(nb))):
        out_r[b] = sr[b] + qr * in_r - qi * in_i
        out_i[b] = si[b] + qr * in_i + qi * in_r
        in_r = jnp.broadcast_to(out_r[b][edge:edge + 1, :], (SUBLANES, n))
        in_i = jnp.broadcast_to(out_i[b][edge:edge + 1, :], (SUBLANES, n))
    return jnp.concatenate(out_r, axis=0), jnp.concatenate(out_i, axis=0), in_r[0:1, :], in_i[0:1, :]


def s5_fwd(uz, bblk, cblk, ar, ai, dskip, *, name, job=None):
    lp = uz.shape[0]
    t = SCAN_ROWS
    nst = S5_GROUPS * S5_STATE

    def body(u_ref, b_ref, c_ref, ar_ref, ai_ref, d_ref, y_ref, yg_ref, sr_ref, si_ref, cr, ci):
        step = pl.program_id(1)

        @pl.when(step == 0)
        def _():
            cr[...] = jnp.zeros_like(cr)
            ci[...] = jnp.zeros_like(ci)

        u = u_ref[...]
        a_r, a_i = ar_ref[...], ai_ref[...]
        x = jnp.dot(u.astype(BF16), b_ref[...].astype(BF16), preferred_element_type=F32)
        sr, si, cr[...], ci[...] = _scan(x[:, :STATE_BLOCK], x[:, STATE_BLOCK:], a_r, a_i, cr[...], ci[...], True)
        sr_ref[...] = sr
        si_ref[...] = si
        s = jnp.concatenate([sr, si], axis=1).astype(BF16)
        y0 = lax.dot_general(s, c_ref[...].astype(BF16), (((1,), (1,)), ((), ())),
                             preferred_element_type=F32) + d_ref[...] * u
        y_ref[...] = y0
        yg_ref[...] = jax.nn.gelu(y0).astype(yg_ref.dtype)

    return _call(
        body, name=name, grid=(S5_BLOCKS, lp // t),
        in_specs=[pl.BlockSpec((t, LANE_BLOCK), lambda b, c: (c, b)),
                  pl.BlockSpec((None, LANE_BLOCK, 2 * STATE_BLOCK), lambda b, c: (b, 0, 0)),
                  pl.BlockSpec((None, LANE_BLOCK, 2 * STATE_BLOCK), lambda b, c: (b, 0, 0)),
                  pl.BlockSpec((1, STATE_BLOCK), lambda b, c: (0, b)),
                  pl.BlockSpec((1, STATE_BLOCK), lambda b, c: (0, b)),
                  pl.BlockSpec((1, LANE_BLOCK), lambda b, c: (0, b))],
        out_specs=[pl.BlockSpec((t, LANE_BLOCK), lambda b, c: (c, b)),
                   pl.BlockSpec((t, LANE_BLOCK), lambda b, c: (c, b)),
                   pl.BlockSpec((t, STATE_BLOCK), lambda b, c: (c, b)),
                   pl.BlockSpec((t, STATE_BLOCK), lambda b, c: (c, b))],
        out_shape=[jax.ShapeDtypeStruct((lp, D_MODEL), F32), jax.ShapeDtypeStruct((lp, D_MODEL), BF16),
                   jax.ShapeDtypeStruct((lp, nst), F32), jax.ShapeDtypeStruct((lp, nst), F32)],
        scratch=[pltpu.VMEM((1, STATE_BLOCK), F32), pltpu.VMEM((1, STATE_BLOCK), F32)],
        args=(uz, bblk, cblk, ar, ai, dskip), semantics=("parallel", "arbitrary"), job=job)


def s5_bwd(dy_direct, dyg, y0, uz, sr, si, bblk, cblk, ar, ai, dskip, dp, *, name, job=None):
    lp = uz.shape[0]
    t = SCAN_ROWS
    nch = lp // t
    nst = S5_GROUPS * S5_STATE

    def body(dyd_ref, dyg_ref, y0_ref, u_ref, sr_ref, si_ref, b_ref, c_ref, ar_ref, ai_ref, d_ref, _, du_ref,
             db_ref, dc_ref, dar_ref, dai_ref, dd_ref, cr, ci):
        step = pl.program_id(1)

        @pl.when(step == 0)
        def _():
            cr[...] = jnp.zeros_like(cr)
            ci[...] = jnp.zeros_like(ci)

        dy = dyd_ref[...] + jax.vjp(jax.nn.gelu, y0_ref[...])[1](dyg_ref[...])[0]
        u = u_ref[...]
        dyb, ub = dy.astype(BF16), u.astype(BF16)
        a_r, a_i = ar_ref[...], -ai_ref[...]
        g = jnp.dot(dyb, c_ref[...].astype(BF16), preferred_element_type=F32)
        gr, gi = g[:, :STATE_BLOCK], g[:, STATE_BLOCK:]
        nxt_r, nxt_i = cr[...], ci[...]
        last = lax.broadcasted_iota(jnp.int32, gr.shape, 0) == t - 1
        lr, li, cr[...], ci[...] = _scan(gr, gi, a_r, a_i, nxt_r, nxt_i, False)
        nr = _shift_rows(lr, 1, False) + jnp.where(last, nxt_r, 0.0)
        ni = _shift_rows(li, 1, False) + jnp.where(last, nxt_i, 0.0)
        s_r, s_i = sr_ref[...], si_ref[...]
        dar = _colsum(s_r * nr + s_i * ni)
        dai = _colsum(s_r * ni - s_i * nr)
        lam = jnp.concatenate([lr, li], axis=1).astype(BF16)
        sb = jnp.concatenate([s_r, s_i], axis=1).astype(BF16)
        dc = lax.dot_general(dyb, sb, (((0,), (0,)), ((), ())), preferred_element_type=F32)
        db = lax.dot_general(ub, lam, (((0,), (0,)), ((), ())), preferred_element_type=F32)
        du_ref[...] = (lax.dot_general(lam, b_ref[...].astype(BF16), (((1,), (1,)), ((), ())),
                                       preferred_element_type=F32) + d_ref[...] * dy).astype(du_ref.dtype)
        dd = _colsum(dy * u)

        @pl.when(step == 0)
        def _():
            db_ref[...] = db
            dc_ref[...] = dc
            dar_ref[...] = dar
            dai_ref[...] = dai
            dd_ref[...] = dd

        @pl.when(step > 0)
        def _():
            db_ref[...] += db
            dc_ref[...] += dc
            dar_ref[...] += dar
            dai_ref[...] += dai
            dd_ref[...] += dd

    rev = lambda b, c: (nch - 1 - c, b)
    return _call(
        body, name=name, grid=(S5_BLOCKS, nch),
        in_specs=[pl.BlockSpec((t, LANE_BLOCK), rev), pl.BlockSpec((t, LANE_BLOCK), rev),
                  pl.BlockSpec((t, LANE_BLOCK), rev), pl.BlockSpec((t, LANE_BLOCK), rev),
                  pl.BlockSpec((t, STATE_BLOCK), rev), pl.BlockSpec((t, STATE_BLOCK), rev),
                  pl.BlockSpec((None, LANE_BLOCK, 2 * STATE_BLOCK), lambda b, c: (b, 0, 0)),
                  pl.BlockSpec((None, LANE_BLOCK, 2 * STATE_BLOCK), lambda b, c: (b, 0, 0)),
                  pl.BlockSpec((1, STATE_BLOCK), lambda b, c: (0, b)),
                  pl.BlockSpec((1, STATE_BLOCK), lambda b, c: (0, b)),
                  pl.BlockSpec((1, LANE_BLOCK), lambda b, c: (0, b)),
                  pl.BlockSpec(memory_space=pl.ANY)],
        out_specs=[pl.BlockSpec((t, LANE_BLOCK), rev),
                   pl.BlockSpec((None, LANE_BLOCK, 2 * STATE_BLOCK), lambda b, c: (b, 0, 0)),
                   pl.BlockSpec((None, LANE_BLOCK, 2 * STATE_BLOCK), lambda b, c: (b, 0, 0)),
                   pl.BlockSpec((1, STATE_BLOCK), lambda b, c: (0, b)),
                   pl.BlockSpec((1, STATE_BLOCK), lambda b, c: (0, b)),
                   pl.BlockSpec((1, LANE_BLOCK), lambda b, c: (0, b))],
        out_shape=[jax.ShapeDtypeStruct(dp.shape, dp.dtype),
                   jax.ShapeDtypeStruct((S5_BLOCKS, LANE_BLOCK, 2 * STATE_BLOCK), F32),
                   jax.ShapeDtypeStruct((S5_BLOCKS, LANE_BLOCK, 2 * STATE_BLOCK), F32),
                   jax.ShapeDtypeStruct((1, nst), F32), jax.ShapeDtypeStruct((1, nst), F32),
                   jax.ShapeDtypeStruct((1, D_MODEL), F32)],
        scratch=[pltpu.VMEM((1, STATE_BLOCK), F32), pltpu.VMEM((1, STATE_BLOCK), F32)],
        aliases={11: 0}, args=(dy_direct, dyg, y0, uz, sr, si, bblk, cblk, ar, ai, dskip, dp),
        semantics=("parallel", "arbitrary"), job=job)


def _s5_prep(lam_re, lam_im, log_dt, b_re, b_im, c_re, c_im):
    dt = jnp.exp(log_dt)
    mag = jnp.exp(lam_re * dt)
    ar = mag * jnp.cos(lam_im * dt)
    ai = mag * jnp.sin(lam_im * dt)
    den = lam_re * lam_re + lam_im * lam_im
    kr = ((ar - 1.0) * lam_re + ai * lam_im) / den
    ki = (ai * lam_re - (ar - 1.0) * lam_im) / den
    bbr = kr[:, None, :] * b_re - ki[:, None, :] * b_im
    bbi = kr[:, None, :] * b_im + ki[:, None, :] * b_re
    rows = S5_GROUPS * S5_GROUP
    expand = (lax.broadcasted_iota(jnp.int32, (S5_STATE, STATE_BLOCK), 1) % S5_STATE
              == lax.broadcasted_iota(jnp.int32, (S5_STATE, STATE_BLOCK), 0)).astype(F32)
    own = ((lax.broadcasted_iota(jnp.int32, (rows, STATE_BLOCK), 0) // S5_GROUP) % GROUPS_PER_BLOCK
           == lax.broadcasted_iota(jnp.int32, (rows, STATE_BLOCK), 1) // S5_STATE).astype(F32)

    def blocks(v):
        wide = jnp.dot(v.reshape(rows, S5_STATE), expand, precision=lax.Precision.HIGHEST,
                       preferred_element_type=F32)
        return (wide * own).reshape(S5_BLOCKS, LANE_BLOCK, STATE_BLOCK)

    btab = jnp.concatenate([blocks(bbr), blocks(bbi)], axis=2)
    ctab = jnp.concatenate([blocks(c_re), -blocks(c_im)], axis=2)
    return ar, ai, btab, ctab


_S5_PREP_OUT = [(S5_GROUPS, S5_STATE), (S5_GROUPS, S5_STATE), (S5_BLOCKS, LANE_BLOCK, 2 * STATE_BLOCK),
                (S5_BLOCKS, LANE_BLOCK, 2 * STATE_BLOCK)]


def s5_prep(params, *, name):
    def body(*refs):
        for ref, val in zip(refs[7:], _s5_prep(*[r[...] for r in refs[:7]]), strict=True):
            ref[...] = val

    return pl.pallas_call(body, name=name, out_shape=[jax.ShapeDtypeStruct(s, F32) for s in _S5_PREP_OUT],
                          compiler_params=pltpu.CompilerParams(vmem_limit_bytes=VMEM_LIMIT))(*params)


def s5_prep_bwd(params, cotangents, *, name):
    def body(*refs):
        grads = jax.vjp(_s5_prep, *[r[...] for r in refs[:7]])[1](tuple(r[...] for r in refs[7:11]))
        for ref, val in zip(refs[11:], grads, strict=True):
            ref[...] = val

    return pl.pallas_call(body, name=name, out_shape=[jax.ShapeDtypeStruct(p.shape, F32) for p in params],
                          compiler_params=pltpu.CompilerParams(vmem_limit_bytes=VMEM_LIMIT))(*params, *cotangents)


def _silu_grad(z):
    s = jax.nn.sigmoid(z)
    return s * (1.0 + z * (1.0 - s))


def conv_fwd(p, conv_w, conv_b, *, name):
    lp = p.shape[0]
    tr = lp // 16
    e = CONV_E
    hb = tr // 8

    def body(p_ref, cgh_ref, vh_ref, w_ref, b_ref, o_ref):
        i = pl.program_id(0)
        bg, cg, v, z = (p_ref[:, k * e:(k + 1) * e] for k in range(4))
        hc = cg * v
        halo = jnp.where(i > 0, cgh_ref[...] * vh_ref[...], 0.0)
        ext = jnp.concatenate([halo, hc], axis=0)
        w = w_ref[...]
        conv = (w[0:1] * pltpu.roll(ext, 2, 0)[8:] + w[1:2] * pltpu.roll(ext, 1, 0)[8:] + w[2:3] * hc
                + b_ref[...])
        o_ref[...] = (bg * conv * jax.nn.silu(z)).astype(o_ref.dtype)

    prev = lambda col: (lambda i: (jnp.maximum(i * hb - 1, 0), col))
    return pl.pallas_call(
        body, name=name, grid=(lp // tr,),
        in_specs=[pl.BlockSpec((tr, 4 * e), lambda i: (i, 0)),
                  pl.BlockSpec((8, e), prev(1)), pl.BlockSpec((8, e), prev(2)),
                  pl.BlockSpec((3, e), lambda i: (0, 0)), pl.BlockSpec((1, e), lambda i: (0, 0))],
        out_specs=pl.BlockSpec((tr, e), lambda i: (i, 0)),
        out_shape=jax.ShapeDtypeStruct((lp, e), BF16),
        compiler_params=_params("parallel"))(p, p, p, conv_w, conv_b)


def conv_bwd(dy3, p, conv_w, conv_b, *, name):
    lp = p.shape[0]
    tr = lp // 16
    e = CONV_E
    hb = tr // 8
    nt = lp // tr
    width = 512

    def body(dy_ref, p_ref, cgh_ref, vh_ref, dyn_ref, bgn_ref, zn_ref, w_ref, b_ref, dp_ref, dw_ref, db_ref):
        i = pl.program_id(0)
        for c0 in range(0, e, width):
            cols = slice(c0, c0 + width)
            dy = dy_ref[:, cols]
            bg, cg, v, z = (p_ref[:, k * e + c0:k * e + c0 + width] for k in range(4))
            w = w_ref[:, cols]
            hc = cg * v
            halo = jnp.where(i > 0, cgh_ref[:, cols] * vh_ref[:, cols], 0.0)
            ext = jnp.concatenate([halo, hc], axis=0)
            hc2, hc1 = pltpu.roll(ext, 2, 0)[8:], pltpu.roll(ext, 1, 0)[8:]
            yc = w[0:1] * hc2 + w[1:2] * hc1 + w[2:3] * hc + b_ref[:, cols]
            sz = jax.nn.silu(z)
            dyc = dy * bg * sz
            dyc_n = jnp.where(i < nt - 1, dyn_ref[:, cols] * bgn_ref[:, cols] * jax.nn.silu(zn_ref[:, cols]), 0.0)
            ext_n = jnp.concatenate([dyc, dyc_n], axis=0)
            n_rows = tr + 8
            dhc = (w[2:3] * dyc + w[1:2] * pltpu.roll(ext_n, n_rows - 1, 0)[:tr]
                   + w[0:1] * pltpu.roll(ext_n, n_rows - 2, 0)[:tr])
            for k, val in enumerate((dy * yc * sz, dhc * v, dhc * cg, dy * bg * yc * _silu_grad(z))):
                dp_ref[:, k * e + c0:k * e + c0 + width] = val.astype(dp_ref.dtype)
            dw = jnp.concatenate([_colsum(dyc * hc2), _colsum(dyc * hc1), _colsum(dyc * hc)], axis=0)
            db = _colsum(dyc)

            @pl.when(i == 0)
            def _(cols=cols, dw=dw, db=db):
                dw_ref[:, cols] = dw
                db_ref[:, cols] = db

            @pl.when(i > 0)
            def _(cols=cols, dw=dw, db=db):
                dw_ref[:, cols] += dw
                db_ref[:, cols] += db

    prev = lambda col: (lambda i: (jnp.maximum(i * hb - 1, 0), col))
    nxt = lambda col: (lambda i: (jnp.minimum((i + 1) * hb, lp // 8 - 1), col))
    return pl.pallas_call(
        body, name=name, grid=(nt,),
        in_specs=[pl.BlockSpec((tr, e), lambda i: (i, 0)), pl.BlockSpec((tr, 4 * e), lambda i: (i, 0)),
                  pl.BlockSpec((8, e), prev(1)), pl.BlockSpec((8, e), prev(2)),
                  pl.BlockSpec((8, e), nxt(0)), pl.BlockSpec((8, e), nxt(0)), pl.BlockSpec((8, e), nxt(3)),
                  pl.BlockSpec((3, e), lambda i: (0, 0)), pl.BlockSpec((1, e), lambda i: (0, 0))],
        out_specs=[pl.BlockSpec((tr, 4 * e), lambda i: (i, 0)), pl.BlockSpec((3, e), lambda i: (0, 0)),
                   pl.BlockSpec((1, e), lambda i: (0, 0))],
        out_shape=[jax.ShapeDtypeStruct((lp, 4 * e), BF16), jax.ShapeDtypeStruct((3, e), F32),
                   jax.ShapeDtypeStruct((1, e), F32)],
        compiler_params=_params("arbitrary"))(dy3, p, p, p, dy3, p, p, conv_w, conv_b)


def _pool_counts(i, tr, rows, offset, w):
    t = (i * tr + offset + 1 + lax.broadcasted_iota(jnp.int32, (rows, 1), 0)).astype(F32)
    return jnp.minimum(t, float(w))


def pool_fwd(p, *, name):
    lp = p.shape[0]
    tr = lp // 16
    h = POOL_HALO
    hb = tr // h

    def body(u_ref, uh_ref, o_ref):
        i = pl.program_id(0)
        u = u_ref[...]
        ext = jnp.concatenate([jnp.where(i > 0, uh_ref[...], 0.0), u], axis=0)
        for k, w in enumerate(POOL_WINDOWS):
            cols = slice(k * POOL_GROUP, (k + 1) * POOL_GROUP)
            s = ext[:, cols]
            d = 1
            while d < w:
                s = s + pltpu.roll(s, d, 0)
                d *= 2
            o_ref[:, cols] = (s[h:] / _pool_counts(i, tr, tr, 0, w) - u[:, cols]).astype(o_ref.dtype)

    return pl.pallas_call(
        body, name=name, grid=(lp // tr,),
        in_specs=[pl.BlockSpec((tr, POOL_E), lambda i: (i, 0)),
                  pl.BlockSpec((h, POOL_E), lambda i: (jnp.maximum(i * hb - 1, 0), 0))],
        out_specs=pl.BlockSpec((tr, POOL_E), lambda i: (i, 0)),
        out_shape=jax.ShapeDtypeStruct((lp, POOL_E), BF16),
        compiler_params=_params("parallel"))(p, p)


def pool_bwd(dmix, dp, *, name):
    lp = dmix.shape[0]
    tr = lp // 16
    h = POOL_HALO
    hb = tr // h
    nt = lp // tr

    def body(dm_ref, dmn_ref, _, du_ref):
        i = pl.program_id(0)
        dm = dm_ref[...]
        dmn = jnp.where(i < nt - 1, dmn_ref[...], 0.0)
        n_rows = tr + h
        for k, w in enumerate(POOL_WINDOWS):
            cols = slice(k * POOL_GROUP, (k + 1) * POOL_GROUP)
            s = jnp.concatenate([dm[:, cols] / _pool_counts(i, tr, tr, 0, w),
                                 dmn[:, cols] / _pool_counts(i, tr, h, tr, w)], axis=0)
            d = 1
            while d < w:
                s = s + pltpu.roll(s, n_rows - d, 0)
                d *= 2
            du_ref[:, cols] = (s[:tr] - dm[:, cols]).astype(du_ref.dtype)

    return pl.pallas_call(
        body, name=name, grid=(nt,),
        in_specs=[pl.BlockSpec((tr, POOL_E), lambda i: (i, 0)),
                  pl.BlockSpec((h, POOL_E), lambda i: (jnp.minimum((i + 1) * hb, lp // h - 1), 0)),
                  pl.BlockSpec(memory_space=pl.ANY)],
        out_specs=pl.BlockSpec((tr, POOL_E), lambda i: (i, 0)),
        out_shape=jax.ShapeDtypeStruct(dp.shape, dp.dtype),
        input_output_aliases={2: 0},
        compiler_params=_params("parallel"))(dmix, dmix, dp)


def _adamw_math(w, g, m, v):
    nm = ADAM_B1 * m + (1.0 - ADAM_B1) * g
    nv = ADAM_B2 * v + (1.0 - ADAM_B2) * jnp.square(g)
    m_hat = nm / (1.0 - ADAM_B1 ** ADAM_STEP)
    v_hat = nv / (1.0 - ADAM_B2 ** ADAM_STEP)
    return -ADAM_LR * (m_hat / (jnp.sqrt(v_hat) + ADAM_EPS) + ADAM_WD * w), nm, nv


def adamw_many(quads, *, name, steps=8, job=None):
    n = len(quads)

    def spec(shape):
        return pl.BlockSpec((shape[0] // steps,) + shape[1:], lambda i, nd=len(shape): (i,) + (0,) * (nd - 1))

    def body(*refs):
        for q in range(n):
            w_ref, g_ref, m_ref, v_ref = refs[4 * q:4 * q + 4]
            d_ref, nm_ref, nv_ref = refs[4 * n + 3 * q:4 * n + 3 * q + 3]
            d_ref[...], nm_ref[...], nv_ref[...] = _adamw_math(w_ref[...], g_ref[...], m_ref[...], v_ref[...])

    res = _call(
        body, name=name, grid=(steps,), in_specs=[spec(quad[0].shape) for quad in quads for _ in range(4)],
        out_specs=[spec(quad[0].shape) for quad in quads for _ in range(3)],
        out_shape=[jax.ShapeDtypeStruct(quad[0].shape, F32) for quad in quads for _ in range(3)],
        args=[a for quad in quads for a in quad], semantics=("parallel",), job=job)
    outs = res if job is None else res[0]
    triples = [tuple(outs[3 * q:3 * q + 3]) for q in range(n)]
    return triples if job is None else (triples, res[1])


WIRE_WIDTH = 1024
WIRE_ROWS = 1024


def _wire_plan(shapes):
    starts, row = [], 0
    for s in shapes:
        r, c = (1, s[0]) if len(s) == 1 else s
        starts.append(row)
        row += -(-(r * max(1, c // WIRE_WIDTH)) // 8) * 8
    assert row <= WIRE_ROWS, row
    return starts


def _wire_cells(shape):
    if len(shape) == 1:
        n = shape[0]
        if n <= WIRE_WIDTH:
            return [(0, n, (slice(None),))]
        return [(h, WIRE_WIDTH, (slice(h * WIRE_WIDTH, (h + 1) * WIRE_WIDTH),)) for h in range(n // WIRE_WIDTH)]
    r, c = shape
    if c <= WIRE_WIDTH:
        return [(None, c, (slice(None), slice(None)))]
    per = c // WIRE_WIDTH
    return [(j * per + h, WIRE_WIDTH, (j, slice(h * WIRE_WIDTH, (h + 1) * WIRE_WIDTH)))
            for j in range(r) for h in range(per)]


def pack_small(arrays, *, name):
    shapes = [a.shape for a in arrays]
    starts = _wire_plan(shapes)

    def body(*refs):
        out = refs[-1]
        out[...] = jnp.zeros_like(out)
        for ref, shape, row0 in zip(refs[:-1], shapes, starts, strict=True):
            for off, cols, idx in _wire_cells(shape):
                if off is None:
                    out[row0:row0 + shape[0], 0:cols] = ref[idx]
                else:
                    out[row0 + off, 0:cols] = ref[idx]

    return pl.pallas_call(body, name=name, out_shape=jax.ShapeDtypeStruct((WIRE_ROWS, WIRE_WIDTH), F32),
                          compiler_params=pltpu.CompilerParams(vmem_limit_bytes=VMEM_LIMIT))(*arrays)


def update_small(packed, starts, triples, shard_cells, chip, *, name):
    n = len(triples)

    def body(chip_ref, p_ref, *refs):
        k = chip_ref[0]
        ins, outs = refs[:3 * n], refs[3 * n:]

        def cut(row0, rows, cols):
            c0, c1 = cols
            if isinstance(rows, int):
                return p_ref[row0 + rows, c0:c1]
            return p_ref[row0 + rows[0]:row0 + rows[1], c0:c1]

        for q in range(n):
            w_ref, m_ref, v_ref = ins[3 * q:3 * q + 3]
            g_ref, d_ref, nm_ref, nv_ref = outs[4 * q:4 * q + 4]
            shape = w_ref.shape
            if shard_cells[q] is None:
                pieces = [(cut(starts[q], (0, shape[0]) if off is None else off, (0, cols)), idx)
                          for off, cols, idx in _wire_cells(shape)]
            else:
                by_chip = [shard_cells[q](kk) for kk in range(N_CHIPS)]
                pieces = []
                for i, (_, _, idx) in enumerate(by_chip[0]):
                    g = cut(starts[q], *by_chip[0][i][:2])
                    for kk in range(1, N_CHIPS):
                        g = jnp.where(k == kk, cut(starts[q], *by_chip[kk][i][:2]), g)
                    pieces.append((g, idx))
            for g, idx in pieces:
                g_ref[idx] = g
                d_ref[idx], nm_ref[idx], nv_ref[idx] = _adamw_math(w_ref[idx], g, m_ref[idx], v_ref[idx])

    whole = lambda a: pl.BlockSpec(a.shape, lambda i, c_ref, nd=a.ndim: (0,) * nd)
    flat = [a for t in triples for a in t]
    out_shape = [jax.ShapeDtypeStruct(t[0].shape, F32) for t in triples for _ in range(4)]
    res = pl.pallas_call(
        body, name=name,
        grid_spec=pltpu.PrefetchScalarGridSpec(
            num_scalar_prefetch=1, grid=(1,),
            in_specs=[whole(packed)] + [whole(a) for a in flat],
            out_specs=[pl.BlockSpec(s.shape, lambda i, c_ref, nd=len(s.shape): (0,) * nd) for s in out_shape]),
        out_shape=out_shape,
        compiler_params=_params("arbitrary"))(chip.reshape(1).astype(jnp.int32), packed, *flat)
    return [tuple(res[4 * q:4 * q + 4]) for q in range(n)]


ANY = pl.BlockSpec(memory_space=pl.ANY)


def _place():
    x, y, c = lax.axis_index("x"), lax.axis_index("y"), lax.axis_index("c")
    return x, y, c, [(1 - x, y), (x, 1 - y), (1 - x, 1 - y)]


DMA_PIECE_BYTES = 512 * 1024


def _pieces(src, dst):
    shape = src.shape
    rows, cols = shape[-2], shape[-1]
    item = jnp.dtype(src.dtype).itemsize
    unit = 32 // item
    want = max(1, (rows * cols * item) // DMA_PIECE_BYTES)
    n = 1
    if rows % unit == 0:
        n = max(d for d in range(1, rows // unit + 1) if (rows // unit) % d == 0 and d <= want)
    size = rows // n
    out = []
    for lead in (range(shape[0]) if len(shape) == 3 else [None]):
        for i in range(n):
            sel = (pl.ds(i * size, size), slice(None)) if lead is None else (lead, pl.ds(i * size, size), slice(None))
            out.append((src.at[sel], dst.at[sel]))
    return out


def _send(src, dst, send_sem, recv_sem, peer, start=True):
    if start:
        for s, d in _pieces(src, dst):
            pltpu.make_async_remote_copy(src_ref=s, dst_ref=d, send_sem=send_sem, recv_sem=recv_sem,
                                         device_id=peer, device_id_type=MESH).start()
    return pltpu.make_async_remote_copy(src_ref=src, dst_ref=dst, send_sem=send_sem, recv_sem=recv_sem,
                                        device_id=peer, device_id_type=MESH)


def run_job(job, *, name):
    n_in, n_out = len(job.arrays), len(job.out_shapes)

    def body(*refs):
        job.fn(refs[:n_in], refs[n_in:n_in + n_out], refs[n_in + n_out:], "both")

    return pl.pallas_call(body, name=name, in_specs=[ANY] * n_in, out_specs=[ANY] * n_out,
                          out_shape=job.out_shapes, scratch_shapes=job.sems,
                          input_output_aliases={a: a for a in job.aliased()})(*job.arrays)


def gather_chips(shards, *, name):
    n = len(shards)

    def body(*refs):
        ins, outs = refs[:n], refs[n:2 * n]
        send_sems, recv_sems = refs[2 * n:]
        x, y, c, chips = _place()
        k = 2 * x + y
        copies = []
        for a in range(n):
            for j, peer in enumerate([(px, py, c) for px, py in chips] + [(x, y, 1 - c)]):
                copies.append(_send(ins[a], outs[a].at[k], send_sems.at[a, j], recv_sems.at[a, j], peer))
        for cp in copies:
            cp.wait()

    return pl.pallas_call(
        body, name=name, in_specs=[ANY] * n, out_specs=[ANY] * n,
        out_shape=[jax.ShapeDtypeStruct((N_CHIPS,) + s.shape, s.dtype) for s in shards],
        scratch_shapes=[pltpu.SemaphoreType.DMA((n, 4)), pltpu.SemaphoreType.DMA((n, 4))])(*shards)


def gather_by_halves_job(shards):
    n = len(shards)

    def fn(ins, outs, sems, phase):
        send_sems, recv_sems = sems
        x, y, c, chips = _place()
        k = 2 * x + y
        sibling = (x, y, 1 - c)
        begin = phase != "finish"
        waits, arrivals = [], []
        for a in range(n):
            half = ins[a].shape[0] // 2
            waits.append(_send(ins[a], outs[a].at[k], send_sems.at[a, 6], recv_sems.at[a, 6], sibling, begin))
            mine = pl.ds(c * half, half)
            for j, (px, py) in enumerate(chips):
                arrivals.append(_send(ins[a].at[mine, :], outs[a].at[k, mine, :], send_sems.at[a, j],
                                      recv_sems.at[a, j], (px, py, c), begin))
        if phase == "start":
            return
        i = 0
        for a in range(n):
            half = ins[a].shape[0] // 2
            mine = pl.ds(c * half, half)
            for j, (px, py) in enumerate(chips):
                arrivals[i].wait_recv()
                landed = outs[a].at[2 * px + py, mine, :]
                waits.append(_send(landed, landed, send_sems.at[a, 3 + j], recv_sems.at[a, 3 + j], sibling))
                i += 1
        for cp in arrivals:
            cp.wait_send()
        for cp in waits:
            cp.wait()

    return Job(shards, [jax.ShapeDtypeStruct((N_CHIPS,) + s.shape, s.dtype) for s in shards],
               [pltpu.SemaphoreType.DMA((n, 7)), pltpu.SemaphoreType.DMA((n, 7))], fn)


def gather_halves_job(shards):
    n = len(shards)

    def fn(ins, outs, sems, phase):
        send_sems, recv_sems = sems
        x, y, c, chips = _place()
        k = 2 * x + y
        copies = []
        for a in range(n):
            half = ins[a].shape[0] // 2
            mine = pl.ds(c * half, half)
            copies.append(_send(ins[a], outs[a].at[k], send_sems.at[a, 3], recv_sems.at[a, 3], (x, y, 1 - c),
                                phase != "finish"))
            for j, (px, py) in enumerate(chips):
                copies.append(_send(ins[a].at[mine, :], outs[a].at[k, mine, :], send_sems.at[a, j],
                                    recv_sems.at[a, j], (px, py, c), phase != "finish"))
        if phase != "start":
            for cp in copies:
                cp.wait()

    return Job(shards, [jax.ShapeDtypeStruct((N_CHIPS,) + s.shape, s.dtype) for s in shards],
               [pltpu.SemaphoreType.DMA((n, 4)), pltpu.SemaphoreType.DMA((n, 4))], fn)


def forward_halves_job(gathered):
    n = len(gathered)

    def fn(ins, outs, sems, phase):
        send_sems, recv_sems = sems
        x, y, c, chips = _place()
        copies = []
        for a in range(n):
            half = outs[a].shape[1] // 2
            for j, (px, py) in enumerate(chips):
                landed = outs[a].at[2 * px + py, pl.ds(c * half, half), :]
                copies.append(_send(landed, landed, send_sems.at[a, j], recv_sems.at[a, j], (x, y, 1 - c),
                                    phase != "finish"))
        if phase != "start":
            for cp in copies:
                cp.wait()

    return Job(gathered, [jax.ShapeDtypeStruct(g.shape, g.dtype) for g in gathered],
               [pltpu.SemaphoreType.DMA((n, 3)), pltpu.SemaphoreType.DMA((n, 3))], fn, in_place=True)


def swap_job(grads):
    n = len(grads)

    def fn(ins, outs, sems, phase):
        send_sems, recv_sems = sems
        x, y, c, _ = _place()
        copies = []
        for a in range(n):
            half = ins[a].shape[1] // 2
            copies.append(_send(ins[a].at[:, pl.ds((1 - c) * half, half), :], outs[a], send_sems.at[a],
                                recv_sems.at[a], (x, y, 1 - c), phase != "finish"))
        if phase != "start":
            for cp in copies:
                cp.wait()

    return Job(grads, [jax.ShapeDtypeStruct((g.shape[0], g.shape[1] // 2, g.shape[2]), g.dtype) for g in grads],
               [pltpu.SemaphoreType.DMA((n,)), pltpu.SemaphoreType.DMA((n,))], fn)


def _chip_slot(s, k):
    rel = s ^ k
    return (rel >> 1) | ((rel & 1) << 1)


def sum_cores(g, theirs, ck, out_dtype, *, name):
    n, r, cols = g.shape
    half = r // 2
    tr = _tile(half, max(16, (1 << 19) // cols), 16)
    nb = half // tr

    def body(ck_ref, g_ref, t_ref, o_ref):
        o_ref[...] = (g_ref[...] + t_ref[...]).astype(o_ref.dtype)

    return pl.pallas_call(
        body, name=name,
        grid_spec=pltpu.PrefetchScalarGridSpec(
            num_scalar_prefetch=1, grid=(n, nb),
            in_specs=[pl.BlockSpec((None, tr, cols), lambda s, i, ck_ref: (s, ck_ref[0] * nb + i, 0)),
                      pl.BlockSpec((None, tr, cols), lambda s, i, ck_ref: (s, i, 0))],
            out_specs=pl.BlockSpec((None, tr, cols), lambda s, i, ck_ref: (_chip_slot(s, ck_ref[1]), i, 0))),
        out_shape=jax.ShapeDtypeStruct((n, half, cols), out_dtype),
        compiler_params=_params("parallel", "parallel"))(ck, g, theirs)


def scatter_job(parts):
    n = len(parts)

    def fn(ins, outs, sems, phase):
        send_sems, recv_sems = sems
        x, y, c, chips = _place()
        copies = []
        for a in range(n):
            for j, (px, py) in enumerate(chips):
                copies.append(_send(ins[a].at[1 + j], outs[a].at[j], send_sems.at[a, j], recv_sems.at[a, j],
                                    (px, py, c), phase != "finish"))
        if phase != "start":
            for cp in copies:
                cp.wait()

    return Job(parts, [jax.ShapeDtypeStruct((3,) + p.shape[1:], p.dtype) for p in parts],
               [pltpu.SemaphoreType.DMA((n, 3)), pltpu.SemaphoreType.DMA((n, 3))], fn)


def sum_chips(own, others, ck, *, name):
    _, r, cols = own.shape
    tr = _tile(r, max(16, (1 << 19) // cols), 16)

    def body(ck_ref, a_ref, b0_ref, b1_ref, b2_ref, o_ref):
        o_ref[...] = (a_ref[...].astype(F32) + b0_ref[...].astype(F32) + b1_ref[...].astype(F32)
                      + b2_ref[...].astype(F32))

    other = lambda j: pl.BlockSpec((None, tr, cols), lambda i, ck_ref: (j, i, 0))
    return pl.pallas_call(
        body, name=name,
        grid_spec=pltpu.PrefetchScalarGridSpec(
            num_scalar_prefetch=1, grid=(r // tr,),
            in_specs=[pl.BlockSpec((None, tr, cols), lambda i, ck_ref: (0, i, 0)), other(0), other(1), other(2)],
            out_specs=pl.BlockSpec((None, tr, cols), lambda i, ck_ref: (ck_ref[0], i, 0))),
        out_shape=jax.ShapeDtypeStruct((2, r, cols), F32),
        compiler_params=_params("parallel"))(ck, own, others, others, others)


def share_job(joined):
    n = len(joined)

    def fn(ins, outs, sems, phase):
        send_sems, recv_sems = sems
        x, y, c, _ = _place()
        copies = [_send(outs[a].at[c], outs[a].at[c], send_sems.at[a], recv_sems.at[a], (x, y, 1 - c),
                        phase != "finish") for a in range(n)]
        if phase != "start":
            for cp in copies:
                cp.wait()

    return Job(joined, [jax.ShapeDtypeStruct(j.shape, j.dtype) for j in joined],
               [pltpu.SemaphoreType.DMA((n,)), pltpu.SemaphoreType.DMA((n,))], fn, in_place=True)


def kernel(x, meta_tokens, norm0_g, l0_w_in, l0_lam_re, l0_lam_im, l0_log_dt, l0_b_re, l0_b_im, l0_c_re, l0_c_im, l0_d_skip, l0_w_glu, l0_b_glu, l0_w_out, norm1_g, l1_w_in, l1_conv_w, l1_conv_b, l1_w_out, norm2_g, l2_w_in, l2_w_grp, l2_b_grp, l2_scale, l2_w_out, norm3_g, l3_w_in, l3_lam_re, l3_lam_im, l3_log_dt, l3_b_re, l3_b_im, l3_c_re, l3_c_im, l3_d_skip, l3_w_glu, l3_b_glu, l3_w_out, final_g, loss_target, m_meta_tokens, m_norm0_g, m_l0_w_in, m_l0_lam_re, m_l0_lam_im, m_l0_log_dt, m_l0_b_re, m_l0_b_im, m_l0_c_re, m_l0_c_im, m_l0_d_skip, m_l0_w_glu, m_l0_b_glu, m_l0_w_out, m_norm1_g, m_l1_w_in, m_l1_conv_w, m_l1_conv_b, m_l1_w_out, m_norm2_g, m_l2_w_in, m_l2_w_grp, m_l2_b_grp, m_l2_scale, m_l2_w_out, m_norm3_g, m_l3_w_in, m_l3_lam_re, m_l3_lam_im, m_l3_log_dt, m_l3_b_re, m_l3_b_im, m_l3_c_re, m_l3_c_im, m_l3_d_skip, m_l3_w_glu, m_l3_b_glu, m_l3_w_out, m_final_g, v_meta_tokens, v_norm0_g, v_l0_w_in, v_l0_lam_re, v_l0_lam_im, v_l0_log_dt, v_l0_b_re, v_l0_b_im, v_l0_c_re, v_l0_c_im, v_l0_d_skip, v_l0_w_glu, v_l0_b_glu, v_l0_w_out, v_norm1_g, v_l1_w_in, v_l1_conv_w, v_l1_conv_b, v_l1_w_out, v_norm2_g, v_l2_w_in, v_l2_w_grp, v_l2_b_grp, v_l2_scale, v_l2_w_out, v_norm3_g, v_l3_w_in, v_l3_lam_re, v_l3_lam_im, v_l3_log_dt, v_l3_b_re, v_l3_b_im, v_l3_c_re, v_l3_c_im, v_l3_d_skip, v_l3_w_glu, v_l3_b_glu, v_l3_w_out, v_final_g):
    given = dict(locals())
    w = {n: given[n] for n in WEIGHTS}
    mom_m = {n: given["m_" + n] for n in WEIGHTS}
    mom_v = {n: given["v_" + n] for n in WEIGHTS}
    seq = x.shape[1]
    real = N_META + seq
    lp = -(-real // SCAN_ROWS) * SCAN_ROWS
    chip = 2 * lax.axis_index("x") + lax.axis_index("y")
    row = lambda a: a.reshape(1, -1)

    shard_bf16 = {n: (w[n].reshape(-1, w[n].shape[-1]) if n == 'l2_w_grp' else w[n]).astype(BF16) for n in BIG}
    layer = lambda i: [n for n in BIG if n.startswith("l%d_" % i)]
    over_ici = lambda names: gather_halves_job([shard_bf16[n] for n in names])
    full = {}

    def landed(names, arrays):
        for n, arr in zip(names, arrays, strict=True):
            if n.endswith('w_in'):
                full[n] = arr
            elif n == 'l2_w_grp':
                full[n] = arr.reshape(N_CHIPS, len(POOL_WINDOWS), POOL_GROUP // N_CHIPS, POOL_GROUP)
            else:
                full[n] = arr.reshape(1, -1, arr.shape[-1])

    landed(['l0_w_in'], run_job(gather_by_halves_job([shard_bf16['l0_w_in']]), name="gather_l0_w_in"))
    full.update(zip(SMALL_SHARDED, gather_chips([w[n] for n in SMALL_SHARDED], name="gather_small_shards"),
                    strict=True))
    meta_full = full['meta_tokens'].transpose(1, 0, 2).reshape(N_META, D_MODEL)
    conv_w_full = full['l1_conv_w'].transpose(1, 0, 2).reshape(3, CONV_E)
    b_grp_full = full['l2_b_grp'].transpose(1, 0, 2).reshape(len(POOL_WINDOWS), POOL_GROUP)

    h0 = jnp.concatenate([meta_full, x[0], jnp.zeros((lp - real, D_MODEL), F32)], axis=0)
    target = jnp.concatenate([jnp.zeros((N_META, D_MODEL), F32), loss_target[0],
                              jnp.zeros((lp - real, D_MODEL), F32)], axis=0)
    grads = {}
    saved = {}
    gip = lambda a: jnp.swapaxes(a, 1, 2)

    def s5_layer_fwd(i, h, first=False):
        pre = "l%d_" % i
        prm = [w[pre + 'lam_re'], w[pre + 'lam_im'], w[pre + 'log_dt'].reshape(-1, 1), gip(w[pre + 'b_re']),
               gip(w[pre + 'b_im']), w[pre + 'c_re'], w[pre + 'c_im']]
        a_re, a_im, bblk, cblk = s5_prep(prm, name=pre + "prep")
        tables = (a_re.reshape(1, -1), a_im.reshape(1, -1), bblk, cblk)
        ar, ai = tables[:2]
        gain = row(w["norm%d_g" % i])
        rest = [pre + 'w_glu', pre + 'w_out']
        if not first:
            uz, n = norm_mm_nn(h, gain, full[pre + 'w_in'], name=pre + "in")
            y0, yg, sr, si = s5_fwd(uz, bblk, cblk, ar, ai, row(w[pre + 'd_skip']), name=pre + "scan")
            q, y3 = glu_gate(yg, full[pre + 'w_glu'], row(w[pre + 'b_glu']), y0, uz, name=pre + "glu")
        else:
            (uz, n), halves = norm_mm_nn(h, gain, full[pre + 'w_in'], name=pre + "in", job=over_ici(rest))
            (y0, yg, sr, si), arrived = s5_fwd(
                uz, bblk, cblk, ar, ai, row(w[pre + 'd_skip']), name=pre + "scan",
                job=Job.both(over_ici(layer(1)), forward_halves_job(halves)))
            landed(rest, arrived[len(layer(1)):])
            (q, y3), whole = glu_gate(yg, full[pre + 'w_glu'], row(w[pre + 'b_glu']), y0, uz, name=pre + "glu",
                                      job=forward_halves_job(arrived[:len(layer(1))]))
            landed(layer(1), whole)
        h_new = mm_nn(y3, full[pre + 'w_out'], add=h, name=pre + "out")
        saved[i] = dict(h=h, n=n, uz=uz, y0=y0, sr=sr, si=si, yg=yg, q=q, y3=y3, tables=tables, prm=prm)
        return h_new

    def s5_layer_bwd(i, dh, carry=None):
        pre = "l%d_" % i
        s = saved[i]
        ar, ai, bblk, cblk = s['tables']
        grads[pre + 'w_out'] = mm_tn(s['y3'], dh, 1, name=pre + "d_w_out")[0]
        dy0_direct, dq, dp, db_glu = s5_gate_bwd(dh, full[pre + 'w_out'], s['y0'], s['q'], s['uz'],
                                                 name=pre + "d_y3")
        grads[pre + 'b_glu'] = db_glu
        grads[pre + 'w_glu'] = mm_tn(s['yg'], dq, 1, name=pre + "d_w_glu")[0]
        dyg = mm_nt(dq, full[pre + 'w_glu'], name=pre + "d_yg")
        res = s5_bwd(dy0_direct, dyg, s['y0'], s['uz'], s['sr'], s['si'], bblk, cblk, ar, ai,
                     row(w[pre + 'd_skip']), dp, name=pre + "d_scan",
                     job=None if carry is None else scatter_job(scatter_of(carry)))
        if carry is not None:
            res, arrived = res
            scattered(carry, arrived)
        dp, dbblk, dcblk, dar, dai, dd = res
        grads[pre + 'd_skip'] = dd
        cots = (dar.reshape(S5_GROUPS, S5_STATE), dai.reshape(S5_GROUPS, S5_STATE), dbblk, dcblk)
        for key, val in zip(('lam_re', 'lam_im', 'log_dt', 'b_re', 'b_im', 'c_re', 'c_im'),
                            s5_prep_bwd(s['prm'], cots, name=pre + "d_prep"), strict=True):
            grads[pre + key] = val.reshape(-1) if key == 'log_dt' else val
        if carry is None:
            grads[pre + 'w_in'] = mm_tn(s['n'], dp, N_CHIPS, name=pre + "d_w_in")
        else:
            later = [n for n in BIG if not n.startswith(pre)]
            sums = [sum_chips(pair[n], from_chips[n], ck, name="sum_chips_" + n) for n in later]
            grads[pre + 'w_in'], shared = mm_tn(s['n'], dp, N_CHIPS, name=pre + "d_w_in", job=share_job(sums))
            joined.update(zip(later, shared, strict=True))
        names, pieces = core_parts(i)
        (dh_in, dg), theirs = mm_nt_norm_bwd(dp, full[pre + 'w_in'], s['h'], row(w["norm%d_g" % i]), dh,
                                             name=pre + "d_n", job=swap_job(pieces))
        sum_over_cores(names, pieces, theirs)
        grads["norm%d_g" % i] = dg
        return dh_in

    h1 = s5_layer_fwd(0, h0, first=True)

    (p1, n1), halves = norm_mm_nn(h1, row(w['norm1_g']), full['l1_w_in'], name="l1_in", job=over_ici(layer(2)))
    y3_1 = conv_fwd(p1, conv_w_full, row(w['l1_conv_b']), name="l1_conv")
    h2, whole = mm_nn(y3_1, full['l1_w_out'], add=h1, name="l1_out", job=forward_halves_job(halves))
    landed(layer(2), whole)

    (p2, n2), halves = norm_mm_nn(h2, row(w['norm2_g']), full['l2_w_in'], name="l2_in", job=over_ici(layer(3)))
    mixed = pool_fwd(p2, name="l2_pool")
    o2, y3_2 = grp_nn_gate(mixed, full['l2_w_grp'], b_grp_full.reshape(1, -1), p2, row(w['l2_scale']), name="l2_grp")
    h3, whole = mm_nn(y3_2, full['l2_w_out'], add=h2, name="l2_out", job=forward_halves_job(halves))
    landed(layer(3), whole)

    h4 = s5_layer_fwd(3, h3)

    def loss_fn(hv, tv, gv):
        i = pl.program_id(0)
        tr = hv.shape[0]
        yf, vjp = jax.vjp(_rms, hv, gv)
        pos = i * tr + lax.broadcasted_iota(jnp.int32, (tr, 1), 0)
        diff = jnp.where((pos >= N_META) & (pos < real), yf - tv, 0.0)
        dh, dg = vjp(diff / D_MODEL)
        return [dh], [dg, _colsum(diff * diff)]

    (dh,), (d_final_g, sq) = rowwise(loss_fn, [(h4, D_MODEL, 0), (target, D_MODEL, 0)], [row(w['final_g'])],
                                     [(D_MODEL, F32, D_MODEL, 0)], [D_MODEL, D_MODEL], name="loss")
    grads['final_g'] = d_final_g
    loss = lax.psum(0.5 / D_MODEL * jnp.sum(sq), ("x", "y", "c"))

    ck = jnp.stack([lax.axis_index("c"), chip]).astype(jnp.int32)
    pair, from_chips, joined = {}, {}, {}

    def chip_major(n):
        g = grads[n]
        if n.endswith('w_in'):
            return g
        if n == 'l2_w_grp':
            return g.reshape(N_CHIPS, -1, POOL_GROUP)
        return g.reshape(N_CHIPS, -1, g.shape[-1])

    def core_parts(i):
        return layer(i), [chip_major(n) for n in layer(i)]

    def sum_over_cores(names, pieces, theirs):
        for n, g, t in zip(names, pieces, theirs, strict=True):
            pair[n] = sum_cores(g, t, ck, F32 if n == 'small' else BF16, name="sum_cores_" + n)

    scatter_of = lambda i: [pair[n] for n in layer(i)]

    def scattered(i, arrays):
        from_chips.update(zip(layer(i), arrays, strict=True))

    dh = s5_layer_bwd(3, dh)

    grads['l2_w_out'] = mm_tn(y3_2, dh, 1, name="l2_d_w_out")[0]
    do2, dp2, dscale, dbgrp = pool_gate_bwd(dh, full['l2_w_out'], o2, p2, row(w['l2_scale']), name="l2_d_y3")
    grads['l2_scale'] = dscale
    grads['l2_b_grp'] = dbgrp
    grads['l2_w_grp'] = grp_tn(mixed, do2, name="l2_d_w_grp")
    dmix = grp_nt(do2, full['l2_w_grp'], name="l2_d_mixed")
    dp2 = pool_bwd(dmix, dp2, name="l2_d_pool")
    grads['l2_w_in'], arrived = mm_tn(n2, dp2, N_CHIPS, name="l2_d_w_in", job=scatter_job(scatter_of(3)))
    scattered(3, arrived)
    names, pieces = core_parts(2)
    (dh, dg), theirs = mm_nt_norm_bwd(dp2, full['l2_w_in'], h2, row(w['norm2_g']), dh, name="l2_d_n",
                                      job=swap_job(pieces))
    sum_over_cores(names, pieces, theirs)
    grads['norm2_g'] = dg

    grads['l1_w_out'] = mm_tn(y3_1, dh, 1, name="l1_d_w_out")[0]
    dy3 = mm_nt(dh, full['l1_w_out'], name="l1_d_y3")
    dp1, dcw, dcb = conv_bwd(dy3, p1, conv_w_full, row(w['l1_conv_b']), name="l1_d_conv")
    grads['l1_conv_w'] = dcw
    grads['l1_conv_b'] = dcb
    grads['l1_w_in'], arrived = mm_tn(n1, dp1, N_CHIPS, name="l1_d_w_in", job=scatter_job(scatter_of(2)))
    scattered(2, arrived)
    names, pieces = core_parts(1)
    (dh, dg), theirs = mm_nt_norm_bwd(dp1, full['l1_w_in'], h1, row(w['norm1_g']), dh, name="l1_d_n",
                                      job=swap_job(pieces))
    sum_over_cores(names, pieces, theirs)
    grads['norm1_g'] = dg

    dh = s5_layer_bwd(0, dh, carry=1)
    grad_x = dh[N_META:real][None]
    grads['meta_tokens'] = dh[:N_META]

    wide = [n for n in SMALL if w[n].ndim == 3]
    wire = [grads[n].reshape(S5_GROUPS, -1) if n in wide else grads[n] for n in SMALL]
    starts = dict(zip(SMALL, _wire_plan([a.shape for a in wire]), strict=True))
    small_local = pack_small(wire, name="pack_small")
    pieces = [small_local.reshape(N_CHIPS, -1, WIRE_WIDTH)]
    sum_over_cores(['small'], pieces, run_job(swap_job(pieces), name="reduce_cores_small"))

    out_g, out_d, out_m, out_v = {}, {}, {}, {}
    as2d = lambda a: a.reshape(-1, a.shape[-1])

    def update_big(names, job=None):
        for n in names:
            out_g[n] = joined[n].reshape(w[n].shape)
        res = adamw_many([(as2d(w[n]), as2d(out_g[n]), as2d(mom_m[n]), as2d(mom_v[n])) for n in names],
                         name="adamw_" + names[0][:2] + "_on", job=job)
        triples, carried = res if job is not None else (res, None)
        for n, (d, nm, nv) in zip(names, triples, strict=True):
            out_d[n], out_m[n], out_v[n] = d.reshape(w[n].shape), nm.reshape(w[n].shape), nv.reshape(w[n].shape)
        return carried

    rest = layer(0) + ['small']
    later = [n for n in BIG if n not in rest]
    from_chips.update(zip(rest, update_big(later, scatter_job([pair[n] for n in rest])), strict=True))
    sums = [sum_chips(pair[n], from_chips[n], ck, name="sum_chips_" + n) for n in rest]
    joined.update(zip(rest, run_job(share_job(sums), name="share_cores"), strict=True))
    update_big(layer(0))
    (small_all,) = run_job(gather_by_halves_job([joined['small'].reshape(-1, WIRE_WIDTH)]), name="gather_small")
    small_all = small_all.reshape(WIRE_ROWS, WIRE_WIDTH)

    half_w = WIRE_WIDTH // 2
    shard_cells = {
        'meta_tokens': lambda k: [((0, N_META), (k * 256, (k + 1) * 256), (slice(None), slice(None)))],
        'l1_conv_w': lambda k: [(2 * j + k // 2, ((k % 2) * half_w, (k % 2 + 1) * half_w), (j, slice(None)))
                                for j in range(3)],
        'l2_b_grp': lambda k: [(j // 2, ((j % 2) * half_w + k * 128, (j % 2) * half_w + (k + 1) * 128),
                                (j, slice(None))) for j in range(len(POOL_WINDOWS))],
    }
    plain = [n for n in SMALL if n not in wide]
    res = update_small(small_all, [starts[n] for n in plain], [(w[n], mom_m[n], mom_v[n]) for n in plain],
                       [shard_cells.get(n) for n in plain], chip, name="adamw_small")
    for n, (g, d, nm, nv) in zip(plain, res, strict=True):
        out_g[n], out_d[n], out_m[n], out_v[n] = g, d, nm, nv
    as_gip = lambda n, a: gip(a) if n.endswith(('b_re', 'b_im')) else a
    g_gip = {n: small_all[starts[n]:starts[n] + S5_GROUPS].reshape(S5_GROUPS, S5_GROUP, S5_STATE) for n in wide}
    res = adamw_many([(as_gip(n, w[n]), g_gip[n], as_gip(n, mom_m[n]), as_gip(n, mom_v[n])) for n in wide],
                     name="adamw_s5_tensors")
    for n, (d, nm, nv) in zip(wide, res, strict=True):
        out_g[n], out_d[n], out_m[n], out_v[n] = as_gip(n, g_gip[n]), as_gip(n, d), as_gip(n, nm), as_gip(n, nv)

    return (loss, grad_x, *[out_g[n] for n in WEIGHTS], *[out_d[n] for n in WEIGHTS],
            *[out_m[n] for n in WEIGHTS], *[out_v[n] for n in WEIGHTS])
```

```python
import functools

import jax
import jax.numpy as jnp
from jax import lax
from jax.experimental import pallas as pl
from jax.experimental.pallas import tpu as pltpu

F32, BF16 = jnp.float32, jnp.bfloat16
MESH = pl.DeviceIdType.MESH

D_MODEL = 1024
N_META = 16
EPS = 1e-6
S5_GROUPS, S5_GROUP, S5_STATE = 64, 16, 64
LANE_BLOCK = 128
S5_BLOCKS = D_MODEL // LANE_BLOCK
GROUPS_PER_BLOCK = LANE_BLOCK // S5_GROUP
STATE_BLOCK = GROUPS_PER_BLOCK * S5_STATE
SCAN_ROWS = 256
SUBLANES = 8
CONV_E = 2048
POOL_E = 2048
POOL_WINDOWS = (2, 4, 8, 16)
POOL_GROUP = 512
POOL_HALO = 16
N_CHIPS = 4

ADAM_LR, ADAM_B1, ADAM_B2, ADAM_EPS, ADAM_WD, ADAM_STEP = 0.001, 0.9, 0.999, 1e-08, 0.01, 10

VMEM_LIMIT = 48 * 1024 * 1024
TN_OPERAND_BYTES = 28 * 1024 * 1024

WEIGHTS = ['meta_tokens', 'norm0_g', 'l0_w_in', 'l0_lam_re', 'l0_lam_im', 'l0_log_dt', 'l0_b_re', 'l0_b_im',
           'l0_c_re', 'l0_c_im', 'l0_d_skip', 'l0_w_glu', 'l0_b_glu', 'l0_w_out', 'norm1_g', 'l1_w_in',
           'l1_conv_w', 'l1_conv_b', 'l1_w_out', 'norm2_g', 'l2_w_in', 'l2_w_grp', 'l2_b_grp', 'l2_scale',
           'l2_w_out', 'norm3_g', 'l3_w_in', 'l3_lam_re', 'l3_lam_im', 'l3_log_dt', 'l3_b_re', 'l3_b_im',
           'l3_c_re', 'l3_c_im', 'l3_d_skip', 'l3_w_glu', 'l3_b_glu', 'l3_w_out', 'final_g']
BIG = ['l0_w_in', 'l0_w_glu', 'l0_w_out', 'l1_w_in', 'l1_w_out', 'l2_w_in', 'l2_w_grp', 'l2_w_out',
       'l3_w_in', 'l3_w_glu', 'l3_w_out']
SMALL_SHARDED = ['meta_tokens', 'l1_conv_w', 'l2_b_grp']
SMALL = [n for n in WEIGHTS if n not in BIG]


def _params(*sem):
    return pltpu.CompilerParams(dimension_semantics=sem, vmem_limit_bytes=VMEM_LIMIT)


class Job:
    def __init__(self, arrays, out_shapes, sems, fn, in_place=False):
        self.arrays, self.out_shapes, self.sems, self.fn = list(arrays), list(out_shapes), list(sems), fn
        self.in_place = in_place
        assert len(self.arrays) == len(self.out_shapes)

    @staticmethod
    def both(first, second):
        na, ns = len(first.arrays), len(first.sems)

        def fn(ins, outs, sems, phase):
            first.fn(ins[:na], outs[:na], sems[:ns], phase)
            second.fn(ins[na:], outs[na:], sems[ns:], phase)

        job = Job(first.arrays + second.arrays, first.out_shapes + second.out_shapes, first.sems + second.sems, fn)
        job.in_place = [first.in_place] * na + [second.in_place] * len(second.arrays)
        return job

    def aliased(self):
        flags = self.in_place if isinstance(self.in_place, list) else [self.in_place] * len(self.arrays)
        return [a for a, flag in enumerate(flags) if flag]


def _call(body, *, name, grid, in_specs, out_specs, out_shape, args, semantics, scratch=(), aliases=None, job=None):
    if job is None:
        return pl.pallas_call(
            body, name=name, grid=grid, in_specs=list(in_specs), out_specs=list(out_specs),
            out_shape=list(out_shape), scratch_shapes=list(scratch), input_output_aliases=aliases or {},
            compiler_params=_params(*semantics))(*args)
    counts = [len(in_specs), len(job.arrays), len(out_specs), len(job.out_shapes), len(scratch), len(job.sems)]

    def hosted(*refs):
        parts, pos = [], 0
        for k in counts:
            parts.append(refs[pos:pos + k])
            pos += k
        ins, job_ins, outs, job_outs, scr, job_sems = parts
        steps = [pl.program_id(d) for d in range(len(grid))]
        first = functools.reduce(jnp.logical_and, [s == 0 for s in steps])
        last = functools.reduce(jnp.logical_and, [s == g - 1 for s, g in zip(steps, grid, strict=True)])

        @pl.when(first)
        def _():
            job.fn(job_ins, job_outs, job_sems, "start")

        body(*ins, *outs, *scr)

        @pl.when(last)
        def _():
            job.fn(job_ins, job_outs, job_sems, "finish")

    any_spec = pl.BlockSpec(memory_space=pl.ANY)
    aliases = dict(aliases or {})
    aliases.update({len(in_specs) + a: len(out_specs) + a for a in job.aliased()})
    res = pl.pallas_call(
        hosted, name=name, grid=grid, in_specs=list(in_specs) + [any_spec] * len(job.arrays),
        out_specs=list(out_specs) + [any_spec] * len(job.out_shapes),
        out_shape=list(out_shape) + job.out_shapes, scratch_shapes=list(scratch) + job.sems,
        input_output_aliases=aliases,
        compiler_params=_params(*["arbitrary"] * len(grid)))(*args, *job.arrays)
    return res[:len(out_specs)], res[len(out_specs):]


def _tile(n, cap, mult):
    best = None
    for t in range(mult, min(n, cap) + 1, mult):
        if n % t == 0:
            best = t
    assert best is not None, (n, cap, mult)
    return best


def _with_job(res, job):
    return res[0] if job is None else (res[0][0], res[1])


def mm_nn(a, b, *, name, bias=None, add=None, out_dtype=F32, job=None):
    m, k = a.shape
    s, _, ns = b.shape
    n = s * ns
    tm, tn = _tile(m, 1088, 16), _tile(ns, 1024, 128)
    per = ns // tn

    def body(*refs):
        a_ref, b_ref = refs[0], refs[1]
        o_ref = refs[-1]
        acc = jnp.dot(a_ref[...].astype(BF16), b_ref[...], preferred_element_type=F32)
        pos = 2
        if bias is not None:
            acc = acc + refs[pos][...]
            pos += 1
        if add is not None:
            acc = acc + refs[pos][...]
        o_ref[...] = acc.astype(o_ref.dtype)

    in_specs = [pl.BlockSpec((tm, k), lambda i, j: (i, 0)),
                pl.BlockSpec((None, k, tn), lambda i, j: (j // per, 0, j % per))]
    args = [a, b]
    if bias is not None:
        in_specs.append(pl.BlockSpec((1, tn), lambda i, j: (0, j)))
        args.append(bias)
    if add is not None:
        in_specs.append(pl.BlockSpec((tm, tn), lambda i, j: (i, j)))
        args.append(add)
    return _with_job(_call(
        body, name=name, grid=(m // tm, n // tn), in_specs=in_specs,
        out_specs=[pl.BlockSpec((tm, tn), lambda i, j: (i, j))],
        out_shape=[jax.ShapeDtypeStruct((m, n), out_dtype)], args=args,
        semantics=("parallel", "parallel"), job=job), job)


def norm_mm_nn(h, gain, b, *, name, job=None):
    m, k = h.shape
    s, _, ns = b.shape
    n = s * ns
    tm, tn = _tile(m, 1088, 16), _tile(ns, 1024, 128)
    per = ns // tn

    def body(h_ref, g_ref, b_ref, o_ref, n_ref):
        @pl.when(pl.program_id(1) == 0)
        def _():
            n_ref[...] = _rms(h_ref[...], g_ref[...]).astype(n_ref.dtype)

        o_ref[...] = jnp.dot(n_ref[...], b_ref[...], preferred_element_type=F32)

    res = _call(
        body, name=name, grid=(m // tm, n // tn),
        in_specs=[pl.BlockSpec((tm, k), lambda i, j: (i, 0)), pl.BlockSpec((1, k), lambda i, j: (0, 0)),
                  pl.BlockSpec((None, k, tn), lambda i, j: (j // per, 0, j % per))],
        out_specs=[pl.BlockSpec((tm, tn), lambda i, j: (i, j)), pl.BlockSpec((tm, k), lambda i, j: (i, 0))],
        out_shape=[jax.ShapeDtypeStruct((m, n), F32), jax.ShapeDtypeStruct((m, k), BF16)], args=(h, gain, b),
        semantics=("parallel", "arbitrary"), job=job)
    return (res[0], res[1]) if job is None else ((res[0][0], res[0][1]), res[1])


def glu_gate(yg, w_glu, b_glu, y0, uz, *, name, job=None):
    m, k = yg.shape
    n = w_glu.shape[2]
    tm = _tile(m, 1088, 16)

    def body(a_ref, w_ref, b_ref, y0_ref, z_ref, q_ref, o_ref):
        q = jnp.dot(a_ref[...], w_ref[...], preferred_element_type=F32) + b_ref[...]
        q_ref[...] = q
        o_ref[...] = _s5_gate(y0_ref[...], q, z_ref[...]).astype(o_ref.dtype)

    tile = pl.BlockSpec((tm, n), lambda i: (i, 0))
    res = _call(
        body, name=name, grid=(m // tm,),
        in_specs=[pl.BlockSpec((tm, k), lambda i: (i, 0)), pl.BlockSpec((None, k, n), lambda i: (0, 0, 0)),
                  pl.BlockSpec((1, n), lambda i: (0, 0)), tile, pl.BlockSpec((tm, n), lambda i: (i, 1))],
        out_specs=[tile, tile],
        out_shape=[jax.ShapeDtypeStruct((m, n), F32), jax.ShapeDtypeStruct((m, n), BF16)],
        args=(yg, w_glu, b_glu, y0, uz), semantics=("parallel",), job=job)
    return (res[0], res[1]) if job is None else ((res[0][0], res[0][1]), res[1])


def s5_gate_bwd(dh, w_out, y0, q, uz, *, name):
    m, d = dh.shape
    e = w_out.shape[1]
    tm = _tile(m, 544, 16)

    def body(dh_ref, w_ref, y0_ref, q_ref, z_ref, dy0_ref, dq_ref, dz_ref, db_ref):
        i = pl.program_id(0)
        dy3 = lax.dot_general(dh_ref[...].astype(BF16), w_ref[...], (((1,), (1,)), ((), ())),
                              preferred_element_type=F32)
        dy0, dq, dz = jax.vjp(_s5_gate, y0_ref[...], q_ref[...], z_ref[...])[1](dy3)
        dy0_ref[...] = dy0
        dq_ref[...] = dq.astype(dq_ref.dtype)
        dz_ref[...] = dz.astype(dz_ref.dtype)
        db = _colsum(dq)

        @pl.when(i == 0)
        def _():
            db_ref[...] = db

        @pl.when(i > 0)
        def _():
            db_ref[...] += db

    tile = pl.BlockSpec((tm, e), lambda i: (i, 0))
    right = pl.BlockSpec((tm, e), lambda i: (i, 1))
    return pl.pallas_call(
        body, name=name, grid=(m // tm,),
        in_specs=[pl.BlockSpec((tm, d), lambda i: (i, 0)), pl.BlockSpec((None, e, d), lambda i: (0, 0, 0)),
                  tile, tile, right],
        out_specs=[tile, tile, right, pl.BlockSpec((1, e), lambda i: (0, 0))],
        out_shape=[jax.ShapeDtypeStruct((m, e), F32), jax.ShapeDtypeStruct((m, e), BF16),
                   jax.ShapeDtypeStruct((m, 2 * e), BF16), jax.ShapeDtypeStruct((1, e), F32)],
        compiler_params=_params("arbitrary"))(dh, w_out, y0, q, uz)


def pool_gate_bwd(dh, w_out, o, p, scale, *, name):
    m, d = dh.shape
    e = w_out.shape[1]
    tm, te = _tile(m, 1088, 16), 1024
    nj = e // te

    def body(dh_ref, w_ref, o_ref, z_ref, s_ref, do_ref, dz_ref, ds_ref, db_ref):
        i = pl.program_id(1)
        dy3 = lax.dot_general(dh_ref[...].astype(BF16), w_ref[...], (((1,), (1,)), ((), ())),
                              preferred_element_type=F32)
        do, dz, ds = jax.vjp(_pool_gate, o_ref[...], z_ref[...], s_ref[...])[1](dy3)
        do_ref[...] = do.astype(do_ref.dtype)
        dz_ref[...] = dz.astype(dz_ref.dtype)
        db = _colsum(do)

        @pl.when(i == 0)
        def _():
            ds_ref[...] = ds
            db_ref[...] = db

        @pl.when(i > 0)
        def _():
            ds_ref[...] += ds
            db_ref[...] += db

    tile = pl.BlockSpec((tm, te), lambda j, i: (i, j))
    right = pl.BlockSpec((tm, te), lambda j, i: (i, nj + j))
    vec = pl.BlockSpec((1, te), lambda j, i: (0, j))
    return pl.pallas_call(
        body, name=name, grid=(nj, m // tm),
        in_specs=[pl.BlockSpec((tm, d), lambda j, i: (i, 0)), pl.BlockSpec((None, te, d), lambda j, i: (0, j, 0)),
                  tile, right, vec],
        out_specs=[tile, right, vec, vec],
        out_shape=[jax.ShapeDtypeStruct((m, e), BF16), jax.ShapeDtypeStruct((m, 2 * e), BF16),
                   jax.ShapeDtypeStruct((1, e), F32), jax.ShapeDtypeStruct((1, e), F32)],
        compiler_params=_params("parallel", "arbitrary"))(dh, w_out, o, p, scale)


def mm_nt_norm_bwd(g, w, h, gain, dh_out, *, name, job=None):
    m, kc = g.shape
    s, nout, kcs = w.shape
    assert s * kcs == kc
    tm, tk = _tile(m, 1088, 16), _tile(kcs, 1024, 128)
    per = kcs // tk
    nk = kc // tk

    def body(g_ref, w_ref, h_ref, gain_ref, dh_ref, o_ref, dg_ref):
        i, kk = pl.program_id(0), pl.program_id(1)
        part = lax.dot_general(g_ref[...].astype(BF16), w_ref[...], (((1,), (1,)), ((), ())),
                               preferred_element_type=F32)

        @pl.when(kk == 0)
        def _():
            o_ref[...] = part

        @pl.when(kk > 0)
        def _():
            o_ref[...] += part

        @pl.when(kk == nk - 1)
        def _():
            dh, dg = jax.vjp(_rms, h_ref[...], gain_ref[...])[1](o_ref[...])
            o_ref[...] = dh_ref[...] + dh

            @pl.when(i == 0)
            def _():
                dg_ref[...] = dg

            @pl.when(i > 0)
            def _():
                dg_ref[...] += dg

    row_tile = pl.BlockSpec((tm, nout), lambda i, kk: (i, 0))
    res = _call(
        body, name=name, grid=(m // tm, nk),
        in_specs=[pl.BlockSpec((tm, tk), lambda i, kk: (i, kk)),
                  pl.BlockSpec((None, nout, tk), lambda i, kk: (kk // per, 0, kk % per)),
                  row_tile, pl.BlockSpec((1, nout), lambda i, kk: (0, 0)), row_tile],
        out_specs=[row_tile, pl.BlockSpec((1, nout), lambda i, kk: (0, 0))],
        out_shape=[jax.ShapeDtypeStruct((m, nout), F32), jax.ShapeDtypeStruct((1, nout), F32)],
        args=(g, w, h, gain, dh_out), semantics=("arbitrary", "arbitrary"), job=job)
    return (res[0], res[1]) if job is None else ((res[0][0], res[0][1]), res[1])


def mm_nt(g, w, *, name, job=None):
    m, kc = g.shape
    s, nout, kcs = w.shape
    assert s * kcs == kc
    tm, tno, tk = _tile(m, 2176, 16), _tile(nout, 1024, 128), _tile(kcs, 1024, 128)
    per = kcs // tk

    def body(g_ref, w_ref, o_ref):
        kk = pl.program_id(2)
        part = lax.dot_general(g_ref[...].astype(BF16), w_ref[...], (((1,), (1,)), ((), ())),
                               preferred_element_type=F32)

        @pl.when(kk == 0)
        def _():
            o_ref[...] = part

        @pl.when(kk > 0)
        def _():
            o_ref[...] += part

    return _with_job(_call(
        body, name=name, grid=(m // tm, nout // tno, kc // tk),
        in_specs=[pl.BlockSpec((tm, tk), lambda i, j, kk: (i, kk)),
                  pl.BlockSpec((None, tno, tk), lambda i, j, kk: (kk // per, j, kk % per))],
        out_specs=[pl.BlockSpec((tm, tno), lambda i, j, kk: (i, j))],
        out_shape=[jax.ShapeDtypeStruct((m, nout), F32)], args=(g, w),
        semantics=("parallel", "parallel", "arbitrary"), job=job), job)


def mm_tn(a, g, shards, *, name, job=None):
    m, ka = a.shape
    n = g.shape[1]
    ns = n // shards
    tka, tn = _tile(ka, 1024, 128), _tile(ns, 512, 128)
    row_bytes = tka * jnp.dtype(a.dtype).itemsize + tn * jnp.dtype(g.dtype).itemsize
    tm = _tile(m, TN_OPERAND_BYTES // (2 * row_bytes), 16)
    per = ns // tn

    def body(a_ref, g_ref, o_ref):
        mm = pl.program_id(2)
        part = lax.dot_general(a_ref[...].astype(BF16), g_ref[...].astype(BF16), (((0,), (0,)), ((), ())),
                               preferred_element_type=F32)

        @pl.when(mm == 0)
        def _():
            o_ref[...] = part

        @pl.when(mm > 0)
        def _():
            o_ref[...] += part

    return _with_job(_call(
        body, name=name, grid=(ka // tka, n // tn, m // tm),
        in_specs=[pl.BlockSpec((tm, tka), lambda i, j, mm: (mm, i)),
                  pl.BlockSpec((tm, tn), lambda i, j, mm: (mm, j))],
        out_specs=[pl.BlockSpec((None, tka, tn), lambda i, j, mm: (j // per, i, j % per))],
        out_shape=[jax.ShapeDtypeStruct((shards, ka, ns), F32)], args=(a, g),
        semantics=("parallel", "parallel", "arbitrary"), job=job), job)


def grp_nn_gate(a, w, bias, p, scale, *, name):
    m = a.shape[0]
    tm = _tile(m, 1088, 16)
    ng = len(POOL_WINDOWS)

    def body(a_ref, w_ref, b_ref, z_ref, s_ref, o_ref, y_ref):
        wj = w_ref[...].reshape(POOL_GROUP, POOL_GROUP)
        o = jnp.dot(a_ref[...].astype(BF16), wj, preferred_element_type=F32) + b_ref[...]
        o_ref[...] = o
        y_ref[...] = _pool_gate(o, z_ref[...], s_ref[...]).astype(y_ref.dtype)

    tile = pl.BlockSpec((tm, POOL_GROUP), lambda i, j: (i, j))
    vec = pl.BlockSpec((1, POOL_GROUP), lambda i, j: (0, j))
    return pl.pallas_call(
        body, name=name, grid=(m // tm, ng),
        in_specs=[tile,
                  pl.BlockSpec((N_CHIPS, None, POOL_GROUP // N_CHIPS, POOL_GROUP), lambda i, j: (0, j, 0, 0)),
                  vec, pl.BlockSpec((tm, POOL_GROUP), lambda i, j: (i, ng + j)), vec],
        out_specs=[tile, tile],
        out_shape=[jax.ShapeDtypeStruct((m, ng * POOL_GROUP), F32), jax.ShapeDtypeStruct((m, ng * POOL_GROUP), BF16)],
        compiler_params=_params("parallel", "parallel"))(a, w, bias, p, scale)


def grp_nt(g, w, *, name):
    m = g.shape[0]
    tm = _tile(m, 1088, 16)
    ng = len(POOL_WINDOWS)

    def body(g_ref, w_ref, o_ref):
        wj = w_ref[...].reshape(POOL_GROUP, POOL_GROUP)
        o_ref[...] = lax.dot_general(g_ref[...].astype(BF16), wj, (((1,), (1,)), ((), ())),
                                     preferred_element_type=F32)

    return pl.pallas_call(
        body, name=name, grid=(m // tm, ng),
        in_specs=[pl.BlockSpec((tm, POOL_GROUP), lambda i, j: (i, j)),
                  pl.BlockSpec((N_CHIPS, None, POOL_GROUP // N_CHIPS, POOL_GROUP), lambda i, j: (0, j, 0, 0))],
        out_specs=pl.BlockSpec((tm, POOL_GROUP), lambda i, j: (i, j)),
        out_shape=jax.ShapeDtypeStruct((m, ng * POOL_GROUP), F32),
        compiler_params=_params("parallel", "parallel"))(g, w)


def grp_tn(a, g, *, name):
    m = a.shape[0]
    tm = _tile(m, 1088, 16)
    ng = len(POOL_WINDOWS)
    rows = POOL_GROUP // N_CHIPS

    def body(a_ref, g_ref, o_ref):
        mm = pl.program_id(1)
        part = lax.dot_general(a_ref[...].astype(BF16), g_ref[...].astype(BF16), (((0,), (0,)), ((), ())),
                               preferred_element_type=F32).reshape(N_CHIPS, rows, POOL_GROUP)

        @pl.when(mm == 0)
        def _():
            o_ref[...] = part

        @pl.when(mm > 0)
        def _():
            o_ref[...] += part

    return pl.pallas_call(
        body, name=name, grid=(ng, m // tm),
        in_specs=[pl.BlockSpec((tm, POOL_GROUP), lambda j, mm: (mm, j)),
                  pl.BlockSpec((tm, POOL_GROUP), lambda j, mm: (mm, j))],
        out_specs=pl.BlockSpec((N_CHIPS, None, rows, POOL_GROUP), lambda j, mm: (0, j, 0, 0)),
        out_shape=jax.ShapeDtypeStruct((N_CHIPS, ng, rows, POOL_GROUP), F32),
        compiler_params=_params("parallel", "arbitrary"))(a, g)


def rowwise(fn, rows, bcast, outs, accs=(), *, name, aliases=None):
    m = rows[0][0].shape[0]
    tr = m // 16
    n_in = len(rows) + len(bcast)

    def body(*refs):
        vals = [r[...] for r in refs[:n_in]]
        o_vals, a_vals = fn(*vals)
        for ref, val in zip(refs[n_in:n_in + len(outs)], o_vals, strict=True):
            ref[...] = val.astype(ref.dtype)
        step = pl.program_id(0)
        for ref, val in zip(refs[n_in + len(outs):], a_vals, strict=True):
            @pl.when(step == 0)
            def _(ref=ref, val=val):
                ref[...] = val

            @pl.when(step > 0)
            def _(ref=ref, val=val):
                ref[...] += val

    in_specs = [pl.BlockSpec((tr, w), functools.partial(lambda i, cb: (i, cb), cb=cb)) for _, w, cb in rows]
    in_specs += [pl.BlockSpec(b.shape, lambda i: (0, 0)) for b in bcast]
    out_specs = [pl.BlockSpec((tr, w), functools.partial(lambda i, cb: (i, cb), cb=cb)) for _, _, w, cb in outs]
    out_specs += [pl.BlockSpec((1, w), lambda i: (0, 0)) for w in accs]
    out_shape = [jax.ShapeDtypeStruct((m, cols), dt) for cols, dt, _, _ in outs]
    out_shape += [jax.ShapeDtypeStruct((1, w), F32) for w in accs]
    res = pl.pallas_call(
        body, name=name, grid=(m // tr,), in_specs=in_specs, out_specs=out_specs, out_shape=out_shape,
        input_output_aliases=aliases or {},
        compiler_params=_params("arbitrary"))(*[r[0] for r in rows], *bcast)
    return res[:len(outs)], res[len(outs):]


def _rms(h, g):
    return h * lax.rsqrt(jnp.mean(h * h, axis=-1, keepdims=True) + EPS) * g


def _colsum(v):
    return jnp.sum(v, axis=0, keepdims=True)


def _s5_gate(y0, q, z):
    yg = jax.nn.gelu(y0)
    return yg * jax.nn.sigmoid(q) * jax.nn.silu(z)


def _pool_gate(o, z, scale):
    return o * scale * jax.nn.silu(z)


def _shift_rows(v, d, down):
    t = v.shape[0]
    row = lax.broadcasted_iota(jnp.int32, v.shape, 0)
    if down:
        return jnp.where(row >= d, pltpu.roll(v, d, 0), 0.0)
    return jnp.where(row < t - d, pltpu.roll(v, t - d, 0), 0.0)


def _cmul(ar, ai, br, bi):
    return ar * br - ai * bi, ar * bi + ai * br


def _scan(xr, xi, ar, ai, cr, ci, down):
    t, n = xr.shape
    nb = t // SUBLANES
    pw = [(ar, ai)]
    for _ in range(SUBLANES - 1):
        pw.append(_cmul(*pw[-1], ar, ai))
    sub = lax.broadcasted_iota(jnp.int32, (SUBLANES, n), 0)

    def table(exps):
        tr, ti = jnp.zeros((SUBLANES, n), F32), jnp.zeros((SUBLANES, n), F32)
        for r, e in enumerate(exps):
            if e:
                tr, ti = jnp.where(sub == r, pw[e - 1][0], tr), jnp.where(sub == r, pw[e - 1][1], ti)
        return tr[None], ti[None]

    sr, si = xr.reshape(nb, SUBLANES, n), xi.reshape(nb, SUBLANES, n)
    d = 1
    while d < SUBLANES:
        mr, mi = table([d if (r >= d if down else r < SUBLANES - d) else 0 for r in range(SUBLANES)])
        turn = d if down else SUBLANES - d
        hr, hi = pltpu.roll(sr, turn, 1), pltpu.roll(si, turn, 1)
        sr, si = sr + mr * hr - mi * hi, si + mr * hi + mi * hr
        d *= 2
    edge = SUBLANES - 1 if down else 0
    qr, qi = table([r + 1 if down else SUBLANES - r for r in range(SUBLANES)])
    qr, qi = qr[0], qi[0]
    in_r, in_i = jnp.broadcast_to(cr, (SUBLANES, n)), jnp.broadcast_to(ci, (SUBLANES, n))
    out_r, out_i = [None] * nb, [None] * nb
    for b in (range(nb) if down else reversed(range(nb))):
        out_r[b] = sr[b] + qr * in_r - qi * in_i
        out_i[b] = si[b] + qr * in_i + qi * in_r
        in_r = jnp.broadcast_to(out_r[b][edge:edge + 1, :], (SUBLANES, n))
        in_i = jnp.broadcast_to(out_i[b][edge:edge + 1, :], (SUBLANES, n))
    return jnp.concatenate(out_r, axis=0), jnp.concatenate(out_i, axis=0), in_r[0:1, :], in_i[0:1, :]


def s5_fwd(uz, bblk, cblk, ar, ai, dskip, *, name, job=None):
    lp = uz.shape[0]
    t = SCAN_ROWS
    nst = S5_GROUPS * S5_STATE

    def body(u_ref, b_ref, c_ref, ar_ref, ai_ref, d_ref, y_ref, yg_ref, sr_ref, si_ref, cr, ci):
        step = pl.program_id(1)

        @pl.when(step == 0)
        def _():
            cr[...] = jnp.zeros_like(cr)
            ci[...] = jnp.zeros_like(ci)

        u = u_ref[...]
        a_r, a_i = ar_ref[...], ai_ref[...]
        x = jnp.dot(u.astype(BF16), b_ref[...].astype(BF16), preferred_element_type=F32)
        sr, si, cr[...], ci[...] = _scan(x[:, :STATE_BLOCK], x[:, STATE_BLOCK:], a_r, a_i, cr[...], ci[...], True)
        sr_ref[...] = sr
        si_ref[...] = si
        s = jnp.concatenate([sr, si], axis=1).astype(BF16)
        y0 = lax.dot_general(s, c_ref[...].astype(BF16), (((1,), (1,)), ((), ())),
                             preferred_element_type=F32) + d_ref[...] * u
        y_ref[...] = y0
        yg_ref[...] = jax.nn.gelu(y0).astype(yg_ref.dtype)

    return _call(
        body, name=name, grid=(S5_BLOCKS, lp // t),
        in_specs=[pl.BlockSpec((t, LANE_BLOCK), lambda b, c: (c, b)),
                  pl.BlockSpec((None, LANE_BLOCK, 2 * STATE_BLOCK), lambda b, c: (b, 0, 0)),
                  pl.BlockSpec((None, LANE_BLOCK, 2 * STATE_BLOCK), lambda b, c: (b, 0, 0)),
                  pl.BlockSpec((1, STATE_BLOCK), lambda b, c: (0, b)),
                  pl.BlockSpec((1, STATE_BLOCK), lambda b, c: (0, b)),
                  pl.BlockSpec((1, LANE_BLOCK), lambda b, c: (0, b))],
        out_specs=[pl.BlockSpec((t, LANE_BLOCK), lambda b, c: (c, b)),
                   pl.BlockSpec((t, LANE_BLOCK), lambda b, c: (c, b)),
                   pl.BlockSpec((t, STATE_BLOCK), lambda b, c: (c, b)),
                   pl.BlockSpec((t, STATE_BLOCK), lambda b, c: (c, b))],
        out_shape=[jax.ShapeDtypeStruct((lp, D_MODEL), F32), jax.ShapeDtypeStruct((lp, D_MODEL), BF16),
                   jax.ShapeDtypeStruct((lp, nst), F32), jax.ShapeDtypeStruct((lp, nst), F32)],
        scratch=[pltpu.VMEM((1, STATE_BLOCK), F32), pltpu.VMEM((1, STATE_BLOCK), F32)],
        args=(uz, bblk, cblk, ar, ai, dskip), semantics=("parallel", "arbitrary"), job=job)


def s5_bwd(dy_direct, dyg, y0, uz, sr, si, bblk, cblk, ar, ai, dskip, dp, *, name, job=None):
    lp = uz.shape[0]
    t = SCAN_ROWS
    nch = lp // t
    nst = S5_GROUPS * S5_STATE

    def body(dyd_ref, dyg_ref, y0_ref, u_ref, sr_ref, si_ref, b_ref, c_ref, ar_ref, ai_ref, d_ref, _, du_ref,
             db_ref, dc_ref, dar_ref, dai_ref, dd_ref, cr, ci):
        step = pl.program_id(1)

        @pl.when(step == 0)
        def _():
            cr[...] = jnp.zeros_like(cr)
            ci[...] = jnp.zeros_like(ci)

        dy = dyd_ref[...] + jax.vjp(jax.nn.gelu, y0_ref[...])[1](dyg_ref[...])[0]
        u = u_ref[...]
        dyb, ub = dy.astype(BF16), u.astype(BF16)
        a_r, a_i = ar_ref[...], -ai_ref[...]
        g = jnp.dot(dyb, c_ref[...].astype(BF16), preferred_element_type=F32)
        gr, gi = g[:, :STATE_BLOCK], g[:, STATE_BLOCK:]
        nxt_r, nxt_i = cr[...], ci[...]
        last = lax.broadcasted_iota(jnp.int32, gr.shape, 0) == t - 1
        lr, li, cr[...], ci[...] = _scan(gr, gi, a_r, a_i, nxt_r, nxt_i, False)
        nr = _shift_rows(lr, 1, False) + jnp.where(last, nxt_r, 0.0)
        ni = _shift_rows(li, 1, False) + jnp.where(last, nxt_i, 0.0)
        s_r, s_i = sr_ref[...], si_ref[...]
        dar = _colsum(s_r * nr + s_i * ni)
        dai = _colsum(s_r * ni - s_i * nr)
        lam = jnp.concatenate([lr, li], axis=1).astype(BF16)
        sb = jnp.concatenate([s_r, s_i], axis=1).astype(BF16)
        dc = lax.dot_general(dyb, sb, (((0,), (0,)), ((), ())), preferred_element_type=F32)
        db = lax.dot_general(ub, lam, (((0,), (0,)), ((), ())), preferred_element_type=F32)
        du_ref[...] = (lax.dot_general(lam, b_ref[...].astype(BF16), (((1,), (1,)), ((), ())),
                                       preferred_element_type=F32) + d_ref[...] * dy).astype(du_ref.dtype)
        dd = _colsum(dy * u)

        @pl.when(step == 0)
        def _():
            db_ref[...] = db
            dc_ref[...] = dc
            dar_ref[...] = dar
            dai_ref[...] = dai
            dd_ref[...] = dd

        @pl.when(step > 0)
        def _():
            db_ref[...] += db
            dc_ref[...] += dc
            dar_ref[...] += dar
            dai_ref[...] += dai
            dd_ref[...] += dd

    rev = lambda b, c: (nch - 1 - c, b)
    return _call(
        body, name=name, grid=(S5_BLOCKS, nch),
        in_specs=[pl.BlockSpec((t, LANE_BLOCK), rev), pl.BlockSpec((t, LANE_BLOCK), rev),
                  pl.BlockSpec((t, LANE_BLOCK), rev), pl.BlockSpec((t, LANE_BLOCK), rev),
                  pl.BlockSpec((t, STATE_BLOCK), rev), pl.BlockSpec((t, STATE_BLOCK), rev),
                  pl.BlockSpec((None, LANE_BLOCK, 2 * STATE_BLOCK), lambda b, c: (b, 0, 0)),
                  pl.BlockSpec((None, LANE_BLOCK, 2 * STATE_BLOCK), lambda b, c: (b, 0, 0)),
                  pl.BlockSpec((1, STATE_BLOCK), lambda b, c: (0, b)),
                  pl.BlockSpec((1, STATE_BLOCK), lambda b, c: (0, b)),
                  pl.BlockSpec((1, LANE_BLOCK), lambda b, c: (0, b)),
                  pl.BlockSpec(memory_space=pl.ANY)],
        out_specs=[pl.BlockSpec((t, LANE_BLOCK), rev),
                   pl.BlockSpec((None, LANE_BLOCK, 2 * STATE_BLOCK), lambda b, c: (b, 0, 0)),
                   pl.BlockSpec((None, LANE_BLOCK, 2 * STATE_BLOCK), lambda b, c: (b, 0, 0)),
                   pl.BlockSpec((1, STATE_BLOCK), lambda b, c: (0, b)),
                   pl.BlockSpec((1, STATE_BLOCK), lambda b, c: (0, b)),
                   pl.BlockSpec((1, LANE_BLOCK), lambda b, c: (0, b))],
        out_shape=[jax.ShapeDtypeStruct(dp.shape, dp.dtype),
                   jax.ShapeDtypeStruct((S5_BLOCKS, LANE_BLOCK, 2 * STATE_BLOCK), F32),
                   jax.ShapeDtypeStruct((S5_BLOCKS, LANE_BLOCK, 2 * STATE_BLOCK), F32),
                   jax.ShapeDtypeStruct((1, nst), F32), jax.ShapeDtypeStruct((1, nst), F32),
                   jax.ShapeDtypeStruct((1, D_MODEL), F32)],
        scratch=[pltpu.VMEM((1, STATE_BLOCK), F32), pltpu.VMEM((1, STATE_BLOCK), F32)],
        aliases={11: 0}, args=(dy_direct, dyg, y0, uz, sr, si, bblk, cblk, ar, ai, dskip, dp),
        semantics=("parallel", "arbitrary"), job=job)


def _s5_prep(lam_re, lam_im, log_dt, b_re, b_im, c_re, c_im):
    dt = jnp.exp(log_dt)
    mag = jnp.exp(lam_re * dt)
    ar = mag * jnp.cos(lam_im * dt)
    ai = mag * jnp.sin(lam_im * dt)
    den = lam_re * lam_re + lam_im * lam_im
    kr = ((ar - 1.0) * lam_re + ai * lam_im) / den
    ki = (ai * lam_re - (ar - 1.0) * lam_im) / den
    bbr = kr[:, None, :] * b_re - ki[:, None, :] * b_im
    bbi = kr[:, None, :] * b_im + ki[:, None, :] * b_re
    rows = S5_GROUPS * S5_GROUP
    expand = (lax.broadcasted_iota(jnp.int32, (S5_STATE, STATE_BLOCK), 1) % S5_STATE
              == lax.broadcasted_iota(jnp.int32, (S5_STATE, STATE_BLOCK), 0)).astype(F32)
    own = ((lax.broadcasted_iota(jnp.int32, (rows, STATE_BLOCK), 0) // S5_GROUP) % GROUPS_PER_BLOCK
           == lax.broadcasted_iota(jnp.int32, (rows, STATE_BLOCK), 1) // S5_STATE).astype(F32)

    def blocks(v):
        wide = jnp.dot(v.reshape(rows, S5_STATE), expand, precision=lax.Precision.HIGHEST,
                       preferred_element_type=F32)
        return (wide * own).reshape(S5_BLOCKS, LANE_BLOCK, STATE_BLOCK)

    btab = jnp.concatenate([blocks(bbr), blocks(bbi)], axis=2)
    ctab = jnp.concatenate([blocks(c_re), -blocks(c_im)], axis=2)
    return ar, ai, btab, ctab


_S5_PREP_OUT = [(S5_GROUPS, S5_STATE), (S5_GROUPS, S5_STATE), (S5_BLOCKS, LANE_BLOCK, 2 * STATE_BLOCK),
                (S5_BLOCKS, LANE_BLOCK, 2 * STATE_BLOCK)]


def s5_prep(params, *, name):
    def body(*refs):
        for ref, val in zip(refs[7:], _s5_prep(*[r[...] for r in refs[:7]]), strict=True):
            ref[...] = val

    return pl.pallas_call(body, name=name, out_shape=[jax.ShapeDtypeStruct(s, F32) for s in _S5_PREP_OUT],
                          compiler_params=pltpu.CompilerParams(vmem_limit_bytes=VMEM_LIMIT))(*params)


def s5_prep_bwd(params, cotangents, *, name):
    def body(*refs):
        grads = jax.vjp(_s5_prep, *[r[...] for r in refs[:7]])[1](tuple(r[...] for r in refs[7:11]))
        for ref, val in zip(refs[11:], grads, strict=True):
            ref[...] = val

    return pl.pallas_call(body, name=name, out_shape=[jax.ShapeDtypeStruct(p.shape, F32) for p in params],
                          compiler_params=pltpu.CompilerParams(vmem_limit_bytes=VMEM_LIMIT))(*params, *cotangents)


def _silu_grad(z):
    s = jax.nn.sigmoid(z)
    return s * (1.0 + z * (1.0 - s))


def conv_fwd(p, conv_w, conv_b, *, name):
    lp = p.shape[0]
    tr = lp // 16
    e = CONV_E
    hb = tr // 8

    def body(p_ref, cgh_ref, vh_ref, w_ref, b_ref, o_ref):
        i = pl.program_id(0)
        bg, cg, v, z = (p_ref[:, k * e:(k + 1) * e] for k in range(4))
        hc = cg * v
        halo = jnp.where(i > 0, cgh_ref[...] * vh_ref[...], 0.0)
        ext = jnp.concatenate([halo, hc], axis=0)
        w = w_ref[...]
        conv = (w[0:1] * pltpu.roll(ext, 2, 0)[8:] + w[1:2] * pltpu.roll(ext, 1, 0)[8:] + w[2:3] * hc
                + b_ref[...])
        o_ref[...] = (bg * conv * jax.nn.silu(z)).astype(o_ref.dtype)

    prev = lambda col: (lambda i: (jnp.maximum(i * hb - 1, 0), col))
    return pl.pallas_call(
        body, name=name, grid=(lp // tr,),
        in_specs=[pl.BlockSpec((tr, 4 * e), lambda i: (i, 0)),
                  pl.BlockSpec((8, e), prev(1)), pl.BlockSpec((8, e), prev(2)),
                  pl.BlockSpec((3, e), lambda i: (0, 0)), pl.BlockSpec((1, e), lambda i: (0, 0))],
        out_specs=pl.BlockSpec((tr, e), lambda i: (i, 0)),
        out_shape=jax.ShapeDtypeStruct((lp, e), BF16),
        compiler_params=_params("parallel"))(p, p, p, conv_w, conv_b)


def conv_bwd(dh, w_out, p, conv_w, conv_b, *, name):
    lp = p.shape[0]
    tr = lp // 16
    e = CONV_E
    d = dh.shape[1]
    hb = tr // 8
    nt = lp // tr
    width = 512

    def body(dh_ref, p_ref, cgh_ref, vh_ref, dhn_ref, bgn_ref, zn_ref, wo_ref, w_ref, b_ref, dp_ref, dw_ref, db_ref):
        i = pl.program_id(0)
        rows, rows_n = dh_ref[...].astype(BF16), dhn_ref[...].astype(BF16)
        for c0 in range(0, e, width):
            cols = slice(c0, c0 + width)
            wo = wo_ref[cols, :]
            dy = lax.dot_general(rows, wo, (((1,), (1,)), ((), ())), preferred_element_type=F32)
            dy_n = lax.dot_general(rows_n, wo, (((1,), (1,)), ((), ())), preferred_element_type=F32)
            bg, cg, v, z = (p_ref[:, k * e + c0:k * e + c0 + width] for k in range(4))
            w = w_ref[:, cols]
            hc = cg * v
            halo = jnp.where(i > 0, cgh_ref[:, cols] * vh_ref[:, cols], 0.0)
            ext = jnp.concatenate([halo, hc], axis=0)
            hc2, hc1 = pltpu.roll(ext, 2, 0)[8:], pltpu.roll(ext, 1, 0)[8:]
            yc = w[0:1] * hc2 + w[1:2] * hc1 + w[2:3] * hc + b_ref[:, cols]
            sz = jax.nn.silu(z)
            dyc = dy * bg * sz
            dyc_n = jnp.where(i < nt - 1, dy_n * bgn_ref[:, cols] * jax.nn.silu(zn_ref[:, cols]), 0.0)
            ext_n = jnp.concatenate([dyc, dyc_n], axis=0)
            n_rows = tr + 8
            dhc = (w[2:3] * dyc + w[1:2] * pltpu.roll(ext_n, n_rows - 1, 0)[:tr]
                   + w[0:1] * pltpu.roll(ext_n, n_rows - 2, 0)[:tr])
            for k, val in enumerate((dy * yc * sz, dhc * v, dhc * cg, dy * bg * yc * _silu_grad(z))):
                dp_ref[:, k * e + c0:k * e + c0 + width] = val.astype(dp_ref.dtype)
            dw = jnp.concatenate([_colsum(dyc * hc2), _colsum(dyc * hc1), _colsum(dyc * hc)], axis=0)
            db = _colsum(dyc)

            @pl.when(i == 0)
            def _(cols=cols, dw=dw, db=db):
                dw_ref[:, cols] = dw
                db_ref[:, cols] = db

            @pl.when(i > 0)
            def _(cols=cols, dw=dw, db=db):
                dw_ref[:, cols] += dw
                db_ref[:, cols] += db

    prev = lambda col: (lambda i: (jnp.maximum(i * hb - 1, 0), col))
    nxt = lambda col: (lambda i: (jnp.minimum((i + 1) * hb, lp // 8 - 1), col))
    return pl.pallas_call(
        body, name=name, grid=(nt,),
        in_specs=[pl.BlockSpec((tr, d), lambda i: (i, 0)), pl.BlockSpec((tr, 4 * e), lambda i: (i, 0)),
                  pl.BlockSpec((8, e), prev(1)), pl.BlockSpec((8, e), prev(2)),
                  pl.BlockSpec((8, d), nxt(0)), pl.BlockSpec((8, e), nxt(0)), pl.BlockSpec((8, e), nxt(3)),
                  pl.BlockSpec((None, e, d), lambda i: (0, 0, 0)),
                  pl.BlockSpec((3, e), lambda i: (0, 0)), pl.BlockSpec((1, e), lambda i: (0, 0))],
        out_specs=[pl.BlockSpec((tr, 4 * e), lambda i: (i, 0)), pl.BlockSpec((3, e), lambda i: (0, 0)),
                   pl.BlockSpec((1, e), lambda i: (0, 0))],
        out_shape=[jax.ShapeDtypeStruct((lp, 4 * e), BF16), jax.ShapeDtypeStruct((3, e), F32),
                   jax.ShapeDtypeStruct((1, e), F32)],
        compiler_params=_params("arbitrary"))(dh, p, p, p, dh, p, p, w_out, conv_w, conv_b)


def _pool_counts(i, tr, rows, offset, w):
    t = (i * tr + offset + 1 + lax.broadcasted_iota(jnp.int32, (rows, 1), 0)).astype(F32)
    return jnp.minimum(t, float(w))


def pool_fwd(p, *, name):
    lp = p.shape[0]
    tr = lp // 16
    h = POOL_HALO
    hb = tr // h

    def body(u_ref, uh_ref, o_ref):
        i = pl.program_id(0)
        u = u_ref[...]
        ext = jnp.concatenate([jnp.where(i > 0, uh_ref[...], 0.0), u], axis=0)
        for k, w in enumerate(POOL_WINDOWS):
            cols = slice(k * POOL_GROUP, (k + 1) * POOL_GROUP)
            s = ext[:, cols]
            d = 1
            while d < w:
                s = s + pltpu.roll(s, d, 0)
                d *= 2
            o_ref[:, cols] = (s[h:] / _pool_counts(i, tr, tr, 0, w) - u[:, cols]).astype(o_ref.dtype)

    return pl.pallas_call(
        body, name=name, grid=(lp // tr,),
        in_specs=[pl.BlockSpec((tr, POOL_E), lambda i: (i, 0)),
                  pl.BlockSpec((h, POOL_E), lambda i: (jnp.maximum(i * hb - 1, 0), 0))],
        out_specs=pl.BlockSpec((tr, POOL_E), lambda i: (i, 0)),
        out_shape=jax.ShapeDtypeStruct((lp, POOL_E), BF16),
        compiler_params=_params("parallel"))(p, p)


def pool_bwd(dmix, dp, *, name):
    lp = dmix.shape[0]
    tr = lp // 16
    h = POOL_HALO
    hb = tr // h
    nt = lp // tr

    def body(dm_ref, dmn_ref, _, du_ref):
        i = pl.program_id(0)
        dm = dm_ref[...]
        dmn = jnp.where(i < nt - 1, dmn_ref[...], 0.0)
        n_rows = tr + h
        for k, w in enumerate(POOL_WINDOWS):
            cols = slice(k * POOL_GROUP, (k + 1) * POOL_GROUP)
            s = jnp.concatenate([dm[:, cols] / _pool_counts(i, tr, tr, 0, w),
                                 dmn[:, cols] / _pool_counts(i, tr, h, tr, w)], axis=0)
            d = 1
            while d < w:
                s = s + pltpu.roll(s, n_rows - d, 0)
                d *= 2
            du_ref[:, cols] = (s[:tr] - dm[:, cols]).astype(du_ref.dtype)

    return pl.pallas_call(
        body, name=name, grid=(nt,),
        in_specs=[pl.BlockSpec((tr, POOL_E), lambda i: (i, 0)),
                  pl.BlockSpec((h, POOL_E), lambda i: (jnp.minimum((i + 1) * hb, lp // h - 1), 0)),
                  pl.BlockSpec(memory_space=pl.ANY)],
        out_specs=pl.BlockSpec((tr, POOL_E), lambda i: (i, 0)),
        out_shape=jax.ShapeDtypeStruct(dp.shape, dp.dtype),
        input_output_aliases={2: 0},
        compiler_params=_params("parallel"))(dmix, dmix, dp)


def _adamw_math(w, g, m, v):
    nm = ADAM_B1 * m + (1.0 - ADAM_B1) * g
    nv = ADAM_B2 * v + (1.0 - ADAM_B2) * jnp.square(g)
    m_hat = nm / (1.0 - ADAM_B1 ** ADAM_STEP)
    v_hat = nv / (1.0 - ADAM_B2 ** ADAM_STEP)
    return -ADAM_LR * (m_hat / (jnp.sqrt(v_hat) + ADAM_EPS) + ADAM_WD * w), nm, nv


def adamw_many(quads, *, name, steps=8, job=None):
    n = len(quads)

    def spec(shape):
        return pl.BlockSpec((shape[0] // steps,) + shape[1:], lambda i, nd=len(shape): (i,) + (0,) * (nd - 1))

    def body(*refs):
        for q in range(n):
            w_ref, g_ref, m_ref, v_ref = refs[4 * q:4 * q + 4]
            d_ref, nm_ref, nv_ref = refs[4 * n + 3 * q:4 * n + 3 * q + 3]
            d_ref[...], nm_ref[...], nv_ref[...] = _adamw_math(w_ref[...], g_ref[...], m_ref[...], v_ref[...])

    res = _call(
        body, name=name, grid=(steps,), in_specs=[spec(quad[0].shape) for quad in quads for _ in range(4)],
        out_specs=[spec(quad[0].shape) for quad in quads for _ in range(3)],
        out_shape=[jax.ShapeDtypeStruct(quad[0].shape, F32) for quad in quads for _ in range(3)],
        args=[a for quad in quads for a in quad], semantics=("parallel",), job=job)
    outs = res if job is None else res[0]
    triples = [tuple(outs[3 * q:3 * q + 3]) for q in range(n)]
    return triples if job is None else (triples, res[1])


WIRE_WIDTH = 1024
WIRE_ROWS = 1024


def _wire_plan(shapes):
    starts, row = [], 0
    for s in shapes:
        r, c = (1, s[0]) if len(s) == 1 else s
        starts.append(row)
        row += -(-(r * max(1, c // WIRE_WIDTH)) // 8) * 8
    assert row <= WIRE_ROWS, row
    return starts


def _wire_cells(shape):
    if len(shape) == 1:
        n = shape[0]
        if n <= WIRE_WIDTH:
            return [(0, n, (slice(None),))]
        return [(h, WIRE_WIDTH, (slice(h * WIRE_WIDTH, (h + 1) * WIRE_WIDTH),)) for h in range(n // WIRE_WIDTH)]
    r, c = shape
    if c <= WIRE_WIDTH:
        return [(None, c, (slice(None), slice(None)))]
    per = c // WIRE_WIDTH
    return [(j * per + h, WIRE_WIDTH, (j, slice(h * WIRE_WIDTH, (h + 1) * WIRE_WIDTH)))
            for j in range(r) for h in range(per)]


def pack_small(arrays, *, name):
    shapes = [a.shape for a in arrays]
    starts = _wire_plan(shapes)

    def body(*refs):
        out = refs[-1]
        out[...] = jnp.zeros_like(out)
        for ref, shape, row0 in zip(refs[:-1], shapes, starts, strict=True):
            for off, cols, idx in _wire_cells(shape):
                if off is None:
                    out[row0:row0 + shape[0], 0:cols] = ref[idx]
                else:
                    out[row0 + off, 0:cols] = ref[idx]

    return pl.pallas_call(body, name=name, out_shape=jax.ShapeDtypeStruct((WIRE_ROWS, WIRE_WIDTH), F32),
                          compiler_params=pltpu.CompilerParams(vmem_limit_bytes=VMEM_LIMIT))(*arrays)


def update_small(packed, starts, triples, shard_cells, chip, *, name):
    n = len(triples)

    def body(chip_ref, p_ref, *refs):
        k = chip_ref[0]
        ins, outs = refs[:3 * n], refs[3 * n:]

        def cut(row0, rows, cols):
            c0, c1 = cols
            if isinstance(rows, int):
                return p_ref[row0 + rows, c0:c1]
            return p_ref[row0 + rows[0]:row0 + rows[1], c0:c1]

        for q in range(n):
            w_ref, m_ref, v_ref = ins[3 * q:3 * q + 3]
            g_ref, d_ref, nm_ref, nv_ref = outs[4 * q:4 * q + 4]
            shape = w_ref.shape
            if shard_cells[q] is None:
                pieces = [(cut(starts[q], (0, shape[0]) if off is None else off, (0, cols)), idx)
                          for off, cols, idx in _wire_cells(shape)]
            else:
                by_chip = [shard_cells[q](kk) for kk in range(N_CHIPS)]
                pieces = []
                for i, (_, _, idx) in enumerate(by_chip[0]):
                    g = cut(starts[q], *by_chip[0][i][:2])
                    for kk in range(1, N_CHIPS):
                        g = jnp.where(k == kk, cut(starts[q], *by_chip[kk][i][:2]), g)
                    pieces.append((g, idx))
            for g, idx in pieces:
                g_ref[idx] = g
                d_ref[idx], nm_ref[idx], nv_ref[idx] = _adamw_math(w_ref[idx], g, m_ref[idx], v_ref[idx])

    whole = lambda a: pl.BlockSpec(a.shape, lambda i, c_ref, nd=a.ndim: (0,) * nd)
    flat = [a for t in triples for a in t]
    out_shape = [jax.ShapeDtypeStruct(t[0].shape, F32) for t in triples for _ in range(4)]
    res = pl.pallas_call(
        body, name=name,
        grid_spec=pltpu.PrefetchScalarGridSpec(
            num_scalar_prefetch=1, grid=(1,),
            in_specs=[whole(packed)] + [whole(a) for a in flat],
            out_specs=[pl.BlockSpec(s.shape, lambda i, c_ref, nd=len(s.shape): (0,) * nd) for s in out_shape]),
        out_shape=out_shape,
        compiler_params=_params("arbitrary"))(chip.reshape(1).astype(jnp.int32), packed, *flat)
    return [tuple(res[4 * q:4 * q + 4]) for q in range(n)]


ANY = pl.BlockSpec(memory_space=pl.ANY)


def _place():
    x, y, c = lax.axis_index("x"), lax.axis_index("y"), lax.axis_index("c")
    return x, y, c, [(1 - x, y), (x, 1 - y), (1 - x, 1 - y)]


DMA_PIECE_BYTES = 512 * 1024


def _pieces(src, dst):
    shape = src.shape
    rows, cols = shape[-2], shape[-1]
    item = jnp.dtype(src.dtype).itemsize
    unit = 32 // item
    want = max(1, (rows * cols * item) // DMA_PIECE_BYTES)
    n = 1
    if rows % unit == 0:
        n = max(d for d in range(1, rows // unit + 1) if (rows // unit) % d == 0 and d <= want)
    size = rows // n
    out = []
    for lead in (range(shape[0]) if len(shape) == 3 else [None]):
        for i in range(n):
            sel = (pl.ds(i * size, size), slice(None)) if lead is None else (lead, pl.ds(i * size, size), slice(None))
            out.append((src.at[sel], dst.at[sel]))
    return out


def _send(src, dst, send_sem, recv_sem, peer, start=True):
    if start:
        for s, d in _pieces(src, dst):
            pltpu.make_async_remote_copy(src_ref=s, dst_ref=d, send_sem=send_sem, recv_sem=recv_sem,
                                         device_id=peer, device_id_type=MESH).start()
    return pltpu.make_async_remote_copy(src_ref=src, dst_ref=dst, send_sem=send_sem, recv_sem=recv_sem,
                                        device_id=peer, device_id_type=MESH)


def run_job(job, *, name):
    n_in, n_out = len(job.arrays), len(job.out_shapes)

    def body(*refs):
        job.fn(refs[:n_in], refs[n_in:n_in + n_out], refs[n_in + n_out:], "both")

    return pl.pallas_call(body, name=name, in_specs=[ANY] * n_in, out_specs=[ANY] * n_out,
                          out_shape=job.out_shapes, scratch_shapes=job.sems,
                          input_output_aliases={a: a for a in job.aliased()})(*job.arrays)


def gather_chips(shards, *, name):
    n = len(shards)

    def body(*refs):
        ins, outs = refs[:n], refs[n:2 * n]
        send_sems, recv_sems = refs[2 * n:]
        x, y, c, chips = _place()
        k = 2 * x + y
        copies = []
        for a in range(n):
            for j, peer in enumerate([(px, py, c) for px, py in chips] + [(x, y, 1 - c)]):
                copies.append(_send(ins[a], outs[a].at[k], send_sems.at[a, j], recv_sems.at[a, j], peer))
        for cp in copies:
            cp.wait()

    return pl.pallas_call(
        body, name=name, in_specs=[ANY] * n, out_specs=[ANY] * n,
        out_shape=[jax.ShapeDtypeStruct((N_CHIPS,) + s.shape, s.dtype) for s in shards],
        scratch_shapes=[pltpu.SemaphoreType.DMA((n, 4)), pltpu.SemaphoreType.DMA((n, 4))])(*shards)


def gather_by_halves_job(shards):
    n = len(shards)

    def fn(ins, outs, sems, phase):
        send_sems, recv_sems = sems
        x, y, c, chips = _place()
        k = 2 * x + y
        sibling = (x, y, 1 - c)
        begin = phase != "finish"
        waits, arrivals = [], []
        for a in range(n):
            half = ins[a].shape[0] // 2
            waits.append(_send(ins[a], outs[a].at[k], send_sems.at[a, 6], recv_sems.at[a, 6], sibling, begin))
            mine = pl.ds(c * half, half)
            for j, (px, py) in enumerate(chips):
                arrivals.append(_send(ins[a].at[mine, :], outs[a].at[k, mine, :], send_sems.at[a, j],
                                      recv_sems.at[a, j], (px, py, c), begin))
        if phase == "start":
            return
        i = 0
        for a in range(n):
            half = ins[a].shape[0] // 2
            mine = pl.ds(c * half, half)
            for j, (px, py) in enumerate(chips):
                arrivals[i].wait_recv()
                landed = outs[a].at[2 * px + py, mine, :]
                waits.append(_send(landed, landed, send_sems.at[a, 3 + j], recv_sems.at[a, 3 + j], sibling))
                i += 1
        for cp in arrivals:
            cp.wait_send()
        for cp in waits:
            cp.wait()

    return Job(shards, [jax.ShapeDtypeStruct((N_CHIPS,) + s.shape, s.dtype) for s in shards],
               [pltpu.SemaphoreType.DMA((n, 7)), pltpu.SemaphoreType.DMA((n, 7))], fn)


def gather_halves_job(shards):
    n = len(shards)

    def fn(ins, outs, sems, phase):
        send_sems, recv_sems = sems
        x, y, c, chips = _place()
        k = 2 * x + y
        copies = []
        for a in range(n):
            half = ins[a].shape[0] // 2
            mine = pl.ds(c * half, half)
            copies.append(_send(ins[a], outs[a].at[k], send_sems.at[a, 3], recv_sems.at[a, 3], (x, y, 1 - c),
                                phase != "finish"))
            for j, (px, py) in enumerate(chips):
                copies.append(_send(ins[a].at[mine, :], outs[a].at[k, mine, :], send_sems.at[a, j],
                                    recv_sems.at[a, j], (px, py, c), phase != "finish"))
        if phase != "start":
            for cp in copies:
                cp.wait()

    return Job(shards, [jax.ShapeDtypeStruct((N_CHIPS,) + s.shape, s.dtype) for s in shards],
               [pltpu.SemaphoreType.DMA((n, 4)), pltpu.SemaphoreType.DMA((n, 4))], fn)


def forward_halves_job(gathered):
    n = len(gathered)

    def fn(ins, outs, sems, phase):
        send_sems, recv_sems = sems
        x, y, c, chips = _place()
        copies = []
        for a in range(n):
            half = outs[a].shape[1] // 2
            for j, (px, py) in enumerate(chips):
                landed = outs[a].at[2 * px + py, pl.ds(c * half, half), :]
                copies.append(_send(landed, landed, send_sems.at[a, j], recv_sems.at[a, j], (x, y, 1 - c),
                                    phase != "finish"))
        if phase != "start":
            for cp in copies:
                cp.wait()

    return Job(gathered, [jax.ShapeDtypeStruct(g.shape, g.dtype) for g in gathered],
               [pltpu.SemaphoreType.DMA((n, 3)), pltpu.SemaphoreType.DMA((n, 3))], fn, in_place=True)


def swap_job(grads):
    n = len(grads)

    def fn(ins, outs, sems, phase):
        send_sems, recv_sems = sems
        x, y, c, _ = _place()
        copies = []
        for a in range(n):
            half = ins[a].shape[1] // 2
            copies.append(_send(ins[a].at[:, pl.ds((1 - c) * half, half), :], outs[a], send_sems.at[a],
                                recv_sems.at[a], (x, y, 1 - c), phase != "finish"))
        if phase != "start":
            for cp in copies:
                cp.wait()

    return Job(grads, [jax.ShapeDtypeStruct((g.shape[0], g.shape[1] // 2, g.shape[2]), g.dtype) for g in grads],
               [pltpu.SemaphoreType.DMA((n,)), pltpu.SemaphoreType.DMA((n,))], fn)


def _chip_slot(s, k):
    rel = s ^ k
    return (rel >> 1) | ((rel & 1) << 1)


def sum_cores(g, theirs, ck, out_dtype, *, name):
    n, r, cols = g.shape
    half = r // 2
    tr = _tile(half, max(16, (1 << 19) // cols), 16)
    nb = half // tr

    def body(ck_ref, g_ref, t_ref, o_ref):
        o_ref[...] = (g_ref[...] + t_ref[...]).astype(o_ref.dtype)

    return pl.pallas_call(
        body, name=name,
        grid_spec=pltpu.PrefetchScalarGridSpec(
            num_scalar_prefetch=1, grid=(n, nb),
            in_specs=[pl.BlockSpec((None, tr, cols), lambda s, i, ck_ref: (s, ck_ref[0] * nb + i, 0)),
                      pl.BlockSpec((None, tr, cols), lambda s, i, ck_ref: (s, i, 0))],
            out_specs=pl.BlockSpec((None, tr, cols), lambda s, i, ck_ref: (_chip_slot(s, ck_ref[1]), i, 0))),
        out_shape=jax.ShapeDtypeStruct((n, half, cols), out_dtype),
        compiler_params=_params("parallel", "parallel"))(ck, g, theirs)


def scatter_job(parts):
    n = len(parts)

    def fn(ins, outs, sems, phase):
        send_sems, recv_sems = sems
        x, y, c, chips = _place()
        copies = []
        for a in range(n):
            for j, (px, py) in enumerate(chips):
                copies.append(_send(ins[a].at[1 + j], outs[a].at[j], send_sems.at[a, j], recv_sems.at[a, j],
                                    (px, py, c), phase != "finish"))
        if phase != "start":
            for cp in copies:
                cp.wait()

    return Job(parts, [jax.ShapeDtypeStruct((3,) + p.shape[1:], p.dtype) for p in parts],
               [pltpu.SemaphoreType.DMA((n, 3)), pltpu.SemaphoreType.DMA((n, 3))], fn)


def sum_chips(own, others, ck, *, name):
    _, r, cols = own.shape
    tr = _tile(r, max(16, (1 << 19) // cols), 16)

    def body(ck_ref, a_ref, b0_ref, b1_ref, b2_ref, o_ref):
        o_ref[...] = (a_ref[...].astype(F32) + b0_ref[...].astype(F32) + b1_ref[...].astype(F32)
                      + b2_ref[...].astype(F32))

    other = lambda j: pl.BlockSpec((None, tr, cols), lambda i, ck_ref: (j, i, 0))
    return pl.pallas_call(
        body, name=name,
        grid_spec=pltpu.PrefetchScalarGridSpec(
            num_scalar_prefetch=1, grid=(r // tr,),
            in_specs=[pl.BlockSpec((None, tr, cols), lambda i, ck_ref: (0, i, 0)), other(0), other(1), other(2)],
            out_specs=pl.BlockSpec((None, tr, cols), lambda i, ck_ref: (ck_ref[0], i, 0))),
        out_shape=jax.ShapeDtypeStruct((2, r, cols), F32),
        compiler_params=_params("parallel"))(ck, own, others, others, others)


def share_job(joined):
    n = len(joined)

    def fn(ins, outs, sems, phase):
        send_sems, recv_sems = sems
        x, y, c, _ = _place()
        copies = [_send(outs[a].at[c], outs[a].at[c], send_sems.at[a], recv_sems.at[a], (x, y, 1 - c),
                        phase != "finish") for a in range(n)]
        if phase != "start":
            for cp in copies:
                cp.wait()

    return Job(joined, [jax.ShapeDtypeStruct(j.shape, j.dtype) for j in joined],
               [pltpu.SemaphoreType.DMA((n,)), pltpu.SemaphoreType.DMA((n,))], fn, in_place=True)


def kernel(x, meta_tokens, norm0_g, l0_w_in, l0_lam_re, l0_lam_im, l0_log_dt, l0_b_re, l0_b_im, l0_c_re, l0_c_im, l0_d_skip, l0_w_glu, l0_b_glu, l0_w_out, norm1_g, l1_w_in, l1_conv_w, l1_conv_b, l1_w_out, norm2_g, l2_w_in, l2_w_grp, l2_b_grp, l2_scale, l2_w_out, norm3_g, l3_w_in, l3_lam_re, l3_lam_im, l3_log_dt, l3_b_re, l3_b_im, l3_c_re, l3_c_im, l3_d_skip, l3_w_glu, l3_b_glu, l3_w_out, final_g, loss_target, m_meta_tokens, m_norm0_g, m_l0_w_in, m_l0_lam_re, m_l0_lam_im, m_l0_log_dt, m_l0_b_re, m_l0_b_im, m_l0_c_re, m_l0_c_im, m_l0_d_skip, m_l0_w_glu, m_l0_b_glu, m_l0_w_out, m_norm1_g, m_l1_w_in, m_l1_conv_w, m_l1_conv_b, m_l1_w_out, m_norm2_g, m_l2_w_in, m_l2_w_grp, m_l2_b_grp, m_l2_scale, m_l2_w_out, m_norm3_g, m_l3_w_in, m_l3_lam_re, m_l3_lam_im, m_l3_log_dt, m_l3_b_re, m_l3_b_im, m_l3_c_re, m_l3_c_im, m_l3_d_skip, m_l3_w_glu, m_l3_b_glu, m_l3_w_out, m_final_g, v_meta_tokens, v_norm0_g, v_l0_w_in, v_l0_lam_re, v_l0_lam_im, v_l0_log_dt, v_l0_b_re, v_l0_b_im, v_l0_c_re, v_l0_c_im, v_l0_d_skip, v_l0_w_glu, v_l0_b_glu, v_l0_w_out, v_norm1_g, v_l1_w_in, v_l1_conv_w, v_l1_conv_b, v_l1_w_out, v_norm2_g, v_l2_w_in, v_l2_w_grp, v_l2_b_grp, v_l2_scale, v_l2_w_out, v_norm3_g, v_l3_w_in, v_l3_lam_re, v_l3_lam_im, v_l3_log_dt, v_l3_b_re, v_l3_b_im, v_l3_c_re, v_l3_c_im, v_l3_d_skip, v_l3_w_glu, v_l3_b_glu, v_l3_w_out, v_final_g):
    given = dict(locals())
    w = {n: given[n] for n in WEIGHTS}
    mom_m = {n: given["m_" + n] for n in WEIGHTS}
    mom_v = {n: given["v_" + n] for n in WEIGHTS}
    seq = x.shape[1]
    real = N_META + seq
    lp = -(-real // SCAN_ROWS) * SCAN_ROWS
    chip = 2 * lax.axis_index("x") + lax.axis_index("y")
    row = lambda a: a.reshape(1, -1)

    shard_bf16 = {n: (w[n].reshape(-1, w[n].shape[-1]) if n == 'l2_w_grp' else w[n]).astype(BF16) for n in BIG}
    layer = lambda i: [n for n in BIG if n.startswith("l%d_" % i)]
    over_ici = lambda names: gather_halves_job([shard_bf16[n] for n in names])
    full = {}

    def landed(names, arrays):
        for n, arr in zip(names, arrays, strict=True):
            if n.endswith('w_in'):
                full[n] = arr
            elif n == 'l2_w_grp':
                full[n] = arr.reshape(N_CHIPS, len(POOL_WINDOWS), POOL_GROUP // N_CHIPS, POOL_GROUP)
            else:
                full[n] = arr.reshape(1, -1, arr.shape[-1])

    landed(['l0_w_in'], run_job(gather_by_halves_job([shard_bf16['l0_w_in']]), name="gather_l0_w_in"))
    full.update(zip(SMALL_SHARDED, gather_chips([w[n] for n in SMALL_SHARDED], name="gather_small_shards"),
                    strict=True))
    meta_full = full['meta_tokens'].transpose(1, 0, 2).reshape(N_META, D_MODEL)
    conv_w_full = full['l1_conv_w'].transpose(1, 0, 2).reshape(3, CONV_E)
    b_grp_full = full['l2_b_grp'].transpose(1, 0, 2).reshape(len(POOL_WINDOWS), POOL_GROUP)

    h0 = jnp.concatenate([meta_full, x[0], jnp.zeros((lp - real, D_MODEL), F32)], axis=0)
    target = jnp.concatenate([jnp.zeros((N_META, D_MODEL), F32), loss_target[0],
                              jnp.zeros((lp - real, D_MODEL), F32)], axis=0)
    grads = {}
    saved = {}
    gip = lambda a: jnp.swapaxes(a, 1, 2)

    def s5_layer_fwd(i, h, first=False):
        pre = "l%d_" % i
        prm = [w[pre + 'lam_re'], w[pre + 'lam_im'], w[pre + 'log_dt'].reshape(-1, 1), gip(w[pre + 'b_re']),
               gip(w[pre + 'b_im']), w[pre + 'c_re'], w[pre + 'c_im']]
        a_re, a_im, bblk, cblk = s5_prep(prm, name=pre + "prep")
        tables = (a_re.reshape(1, -1), a_im.reshape(1, -1), bblk, cblk)
        ar, ai = tables[:2]
        gain = row(w["norm%d_g" % i])
        rest = [pre + 'w_glu', pre + 'w_out']
        if not first:
            uz, n = norm_mm_nn(h, gain, full[pre + 'w_in'], name=pre + "in")
            y0, yg, sr, si = s5_fwd(uz, bblk, cblk, ar, ai, row(w[pre + 'd_skip']), name=pre + "scan")
            q, y3 = glu_gate(yg, full[pre + 'w_glu'], row(w[pre + 'b_glu']), y0, uz, name=pre + "glu")
        else:
            (uz, n), halves = norm_mm_nn(h, gain, full[pre + 'w_in'], name=pre + "in", job=over_ici(rest))
            (y0, yg, sr, si), arrived = s5_fwd(
                uz, bblk, cblk, ar, ai, row(w[pre + 'd_skip']), name=pre + "scan",
                job=Job.both(over_ici(layer(1)), forward_halves_job(halves)))
            landed(rest, arrived[len(layer(1)):])
            (q, y3), whole = glu_gate(yg, full[pre + 'w_glu'], row(w[pre + 'b_glu']), y0, uz, name=pre + "glu",
                                      job=forward_halves_job(arrived[:len(layer(1))]))
            landed(layer(1), whole)
        h_new = mm_nn(y3, full[pre + 'w_out'], add=h, name=pre + "out")
        saved[i] = dict(h=h, n=n, uz=uz, y0=y0, sr=sr, si=si, yg=yg, q=q, y3=y3, tables=tables, prm=prm)
        return h_new

    def s5_layer_bwd(i, dh, carry=None):
        pre = "l%d_" % i
        s = saved[i]
        ar, ai, bblk, cblk = s['tables']
        grads[pre + 'w_out'] = mm_tn(s['y3'], dh, 1, name=pre + "d_w_out")[0]
        dy0_direct, dq, dp, db_glu = s5_gate_bwd(dh, full[pre + 'w_out'], s['y0'], s['q'], s['uz'],
                                                 name=pre + "d_y3")
        grads[pre + 'b_glu'] = db_glu
        grads[pre + 'w_glu'] = mm_tn(s['yg'], dq, 1, name=pre + "d_w_glu")[0]
        early = [] if carry is None else [pre + 'w_out', pre + 'w_glu']
        if carry is None:
            dyg = mm_nt(dq, full[pre + 'w_glu'], name=pre + "d_yg")
            res = s5_bwd(dy0_direct, dyg, s['y0'], s['uz'], s['sr'], s['si'], bblk, cblk, ar, ai,
                         row(w[pre + 'd_skip']), dp, name=pre + "d_scan")
        else:
            pieces = [chip_major(n) for n in early]
            dyg, theirs = mm_nt(dq, full[pre + 'w_glu'], name=pre + "d_yg", job=swap_job(pieces))
            sum_over_cores(early, pieces, theirs)
            res, arrived = s5_bwd(dy0_direct, dyg, s['y0'], s['uz'], s['sr'], s['si'], bblk, cblk, ar, ai,
                                  row(w[pre + 'd_skip']), dp, name=pre + "d_scan",
                                  job=scatter_job(scatter_of(carry) + [pair[n] for n in early]))
            from_chips.update(zip(layer(carry) + early, arrived, strict=True))
        dp, dbblk, dcblk, dar, dai, dd = res
        grads[pre + 'd_skip'] = dd
        cots = (dar.reshape(S5_GROUPS, S5_STATE), dai.reshape(S5_GROUPS, S5_STATE), dbblk, dcblk)
        for key, val in zip(('lam_re', 'lam_im', 'log_dt', 'b_re', 'b_im', 'c_re', 'c_im'),
                            s5_prep_bwd(s['prm'], cots, name=pre + "d_prep"), strict=True):
            grads[pre + key] = val.reshape(-1) if key == 'log_dt' else val
        if carry is None:
            grads[pre + 'w_in'] = mm_tn(s['n'], dp, N_CHIPS, name=pre + "d_w_in")
        else:
            later = [n for n in BIG if not n.startswith(pre)] + early
            sums = [sum_chips(pair[n], from_chips[n], ck, name="sum_chips_" + n) for n in later]
            grads[pre + 'w_in'], shared = mm_tn(s['n'], dp, N_CHIPS, name=pre + "d_w_in", job=share_job(sums))
            joined.update(zip(later, shared, strict=True))
        names = [n for n in layer(i) if n not in early]
        pieces = [chip_major(n) for n in names]
        (dh_in, dg), theirs = mm_nt_norm_bwd(dp, full[pre + 'w_in'], s['h'], row(w["norm%d_g" % i]), dh,
                                             name=pre + "d_n", job=swap_job(pieces))
        sum_over_cores(names, pieces, theirs)
        grads["norm%d_g" % i] = dg
        return dh_in

    h1 = s5_layer_fwd(0, h0, first=True)

    (p1, n1), halves = norm_mm_nn(h1, row(w['norm1_g']), full['l1_w_in'], name="l1_in", job=over_ici(layer(2)))
    y3_1 = conv_fwd(p1, conv_w_full, row(w['l1_conv_b']), name="l1_conv")
    h2, whole = mm_nn(y3_1, full['l1_w_out'], add=h1, name="l1_out", job=forward_halves_job(halves))
    landed(layer(2), whole)

    (p2, n2), halves = norm_mm_nn(h2, row(w['norm2_g']), full['l2_w_in'], name="l2_in", job=over_ici(layer(3)))
    mixed = pool_fwd(p2, name="l2_pool")
    o2, y3_2 = grp_nn_gate(mixed, full['l2_w_grp'], b_grp_full.reshape(1, -1), p2, row(w['l2_scale']), name="l2_grp")
    h3, whole = mm_nn(y3_2, full['l2_w_out'], add=h2, name="l2_out", job=forward_halves_job(halves))
    landed(layer(3), whole)

    h4 = s5_layer_fwd(3, h3)

    def loss_fn(hv, tv, gv):
        i = pl.program_id(0)
        tr = hv.shape[0]
        yf, vjp = jax.vjp(_rms, hv, gv)
        pos = i * tr + lax.broadcasted_iota(jnp.int32, (tr, 1), 0)
        diff = jnp.where((pos >= N_META) & (pos < real), yf - tv, 0.0)
        dh, dg = vjp(diff / D_MODEL)
        return [dh], [dg, _colsum(diff * diff)]

    (dh,), (d_final_g, sq) = rowwise(loss_fn, [(h4, D_MODEL, 0), (target, D_MODEL, 0)], [row(w['final_g'])],
                                     [(D_MODEL, F32, D_MODEL, 0)], [D_MODEL, D_MODEL], name="loss")
    grads['final_g'] = d_final_g
    loss = lax.psum(0.5 / D_MODEL * jnp.sum(sq), ("x", "y", "c"))

    ck = jnp.stack([lax.axis_index("c"), chip]).astype(jnp.int32)
    pair, from_chips, joined = {}, {}, {}

    def chip_major(n):
        g = grads[n]
        if n.endswith('w_in'):
            return g
        if n == 'l2_w_grp':
            return g.reshape(N_CHIPS, -1, POOL_GROUP)
        return g.reshape(N_CHIPS, -1, g.shape[-1])

    def core_parts(i):
        return layer(i), [chip_major(n) for n in layer(i)]

    def sum_over_cores(names, pieces, theirs):
        for n, g, t in zip(names, pieces, theirs, strict=True):
            pair[n] = sum_cores(g, t, ck, F32 if n == 'small' else BF16, name="sum_cores_" + n)

    scatter_of = lambda i: [pair[n] for n in layer(i)]

    def scattered(i, arrays):
        from_chips.update(zip(layer(i), arrays, strict=True))

    dh = s5_layer_bwd(3, dh)

    grads['l2_w_out'] = mm_tn(y3_2, dh, 1, name="l2_d_w_out")[0]
    do2, dp2, dscale, dbgrp = pool_gate_bwd(dh, full['l2_w_out'], o2, p2, row(w['l2_scale']), name="l2_d_y3")
    grads['l2_scale'] = dscale
    grads['l2_b_grp'] = dbgrp
    grads['l2_w_grp'] = grp_tn(mixed, do2, name="l2_d_w_grp")
    dmix = grp_nt(do2, full['l2_w_grp'], name="l2_d_mixed")
    dp2 = pool_bwd(dmix, dp2, name="l2_d_pool")
    grads['l2_w_in'], arrived = mm_tn(n2, dp2, N_CHIPS, name="l2_d_w_in", job=scatter_job(scatter_of(3)))
    scattered(3, arrived)
    names, pieces = core_parts(2)
    (dh, dg), theirs = mm_nt_norm_bwd(dp2, full['l2_w_in'], h2, row(w['norm2_g']), dh, name="l2_d_n",
                                      job=swap_job(pieces))
    sum_over_cores(names, pieces, theirs)
    grads['norm2_g'] = dg

    grads['l1_w_out'] = mm_tn(y3_1, dh, 1, name="l1_d_w_out")[0]
    dp1, dcw, dcb = conv_bwd(dh, full['l1_w_out'], p1, conv_w_full, row(w['l1_conv_b']), name="l1_d_conv")
    grads['l1_conv_w'] = dcw
    grads['l1_conv_b'] = dcb
    grads['l1_w_in'], arrived = mm_tn(n1, dp1, N_CHIPS, name="l1_d_w_in", job=scatter_job(scatter_of(2)))
    scattered(2, arrived)
    names, pieces = core_parts(1)
    (dh, dg), theirs = mm_nt_norm_bwd(dp1, full['l1_w_in'], h1, row(w['norm1_g']), dh, name="l1_d_n",
                                      job=swap_job(pieces))
    sum_over_cores(names, pieces, theirs)
    grads['norm1_g'] = dg

    dh = s5_layer_bwd(0, dh, carry=1)
    grad_x = dh[N_META:real][None]
    grads['meta_tokens'] = dh[:N_META]

    wide = [n for n in SMALL if w[n].ndim == 3]
    wire = [grads[n].reshape(S5_GROUPS, -1) if n in wide else grads[n] for n in SMALL]
    starts = dict(zip(SMALL, _wire_plan([a.shape for a in wire]), strict=True))
    small_local = pack_small(wire, name="pack_small")
    pieces = [small_local.reshape(N_CHIPS, -1, WIRE_WIDTH)]
    sum_over_cores(['small'], pieces, run_job(swap_job(pieces), name="reduce_cores_small"))

    out_g, out_d, out_m, out_v = {}, {}, {}, {}
    as2d = lambda a: a.reshape(-1, a.shape[-1])

    def update_big(names, job=None):
        for n in names:
            out_g[n] = joined[n].reshape(w[n].shape)
        res = adamw_many([(as2d(w[n]), as2d(out_g[n]), as2d(mom_m[n]), as2d(mom_v[n])) for n in names],
                         name="adamw_" + names[0][:2] + "_on", job=job)
        triples, carried = res if job is not None else (res, None)
        for n, (d, nm, nv) in zip(names, triples, strict=True):
            out_d[n], out_m[n], out_v[n] = d.reshape(w[n].shape), nm.reshape(w[n].shape), nv.reshape(w[n].shape)
        return carried

    rest = ['l0_w_in', 'small']
    later = [n for n in BIG if n not in rest]
    from_chips.update(zip(rest, update_big(later, scatter_job([pair[n] for n in rest])), strict=True))
    sums = [sum_chips(pair[n], from_chips[n], ck, name="sum_chips_" + n) for n in rest]
    joined.update(zip(rest, run_job(share_job(sums), name="share_cores"), strict=True))
    update_big(['l0_w_in'])
    (small_all,) = run_job(gather_by_halves_job([joined['small'].reshape(-1, WIRE_WIDTH)]), name="gather_small")
    small_all = small_all.reshape(WIRE_ROWS, WIRE_WIDTH)

    half_w = WIRE_WIDTH // 2
    shard_cells = {
        'meta_tokens': lambda k: [((0, N_META), (k * 256, (k + 1) * 256), (slice(None), slice(None)))],
        'l1_conv_w': lambda k: [(2 * j + k // 2, ((k % 2) * half_w, (k % 2 + 1) * half_w), (j, slice(None)))
                                for j in range(3)],
        'l2_b_grp': lambda k: [(j // 2, ((j % 2) * half_w + k * 128, (j % 2) * half_w + (k + 1) * 128),
                                (j, slice(None))) for j in range(len(POOL_WINDOWS))],
    }
    plain = [n for n in SMALL if n not in wide]
    res = update_small(small_all, [starts[n] for n in plain], [(w[n], mom_m[n], mom_v[n]) for n in plain],
                       [shard_cells.get(n) for n in plain], chip, name="adamw_small")
    for n, (g, d, nm, nv) in zip(plain, res, strict=True):
        out_g[n], out_d[n], out_m[n], out_v[n] = g, d, nm, nv
    as_gip = lambda n, a: gip(a) if n.endswith(('b_re', 'b_im')) else a
    g_gip = {n: small_all[starts[n]:starts[n] + S5_GROUPS].reshape(S5_GROUPS, S5_GROUP, S5_STATE) for n in wide}
    res = adamw_many([(as_gip(n, w[n]), g_gip[n], as_gip(n, mom_m[n]), as_gip(n, mom_v[n])) for n in wide],
                     name="adamw_s5_tensors")
    for n, (d, nm, nv) in zip(wide, res, strict=True):
        out_g[n], out_d[n], out_m[n], out_v[n] = as_gip(n, g_gip[n]), as_gip(n, d), as_gip(n, nm), as_gip(n, nv)

    return (loss, grad_x, *[out_g[n] for n in WEIGHTS], *[out_d[n] for n in WEIGHTS],
            *[out_m[n] for n in WEIGHTS], *[out_v[n] for n in WEIGHTS])
```

```python
import functools

import jax
import jax.numpy as jnp
from jax import lax
from jax.experimental import pallas as pl
from jax.experimental.pallas import tpu as pltpu

F32, BF16 = jnp.float32, jnp.bfloat16
MESH = pl.DeviceIdType.MESH

D_MODEL = 1024
N_META = 16
EPS = 1e-6
S5_GROUPS, S5_GROUP, S5_STATE = 64, 16, 64
LANE_BLOCK = 128
S5_BLOCKS = D_MODEL // LANE_BLOCK
GROUPS_PER_BLOCK = LANE_BLOCK // S5_GROUP
STATE_BLOCK = GROUPS_PER_BLOCK * S5_STATE
SCAN_ROWS = 256
SUBLANES = 8
CONV_E = 2048
POOL_E = 2048
POOL_WINDOWS = (2, 4, 8, 16)
POOL_GROUP = 512
POOL_HALO = 16
N_CHIPS = 4

ADAM_LR, ADAM_B1, ADAM_B2, ADAM_EPS, ADAM_WD, ADAM_STEP = 0.001, 0.9, 0.999, 1e-08, 0.01, 10

VMEM_LIMIT = 48 * 1024 * 1024
TN_OPERAND_BYTES = 28 * 1024 * 1024

WEIGHTS = ['meta_tokens', 'norm0_g', 'l0_w_in', 'l0_lam_re', 'l0_lam_im', 'l0_log_dt', 'l0_b_re', 'l0_b_im',
           'l0_c_re', 'l0_c_im', 'l0_d_skip', 'l0_w_glu', 'l0_b_glu', 'l0_w_out', 'norm1_g', 'l1_w_in',
           'l1_conv_w', 'l1_conv_b', 'l1_w_out', 'norm2_g', 'l2_w_in', 'l2_w_grp', 'l2_b_grp', 'l2_scale',
           'l2_w_out', 'norm3_g', 'l3_w_in', 'l3_lam_re', 'l3_lam_im', 'l3_log_dt', 'l3_b_re', 'l3_b_im',
           'l3_c_re', 'l3_c_im', 'l3_d_skip', 'l3_w_glu', 'l3_b_glu', 'l3_w_out', 'final_g']
BIG = ['l0_w_in', 'l0_w_glu', 'l0_w_out', 'l1_w_in', 'l1_w_out', 'l2_w_in', 'l2_w_grp', 'l2_w_out',
       'l3_w_in', 'l3_w_glu', 'l3_w_out']
SMALL_SHARDED = ['meta_tokens', 'l1_conv_w', 'l2_b_grp']
SMALL = [n for n in WEIGHTS if n not in BIG]


def _params(*sem):
    return pltpu.CompilerParams(dimension_semantics=sem, vmem_limit_bytes=VMEM_LIMIT)


class Job:
    def __init__(self, arrays, out_shapes, sems, fn, in_place=False):
        self.arrays, self.out_shapes, self.sems, self.fn = list(arrays), list(out_shapes), list(sems), fn
        self.in_place = in_place
        assert len(self.arrays) == len(self.out_shapes)

    @staticmethod
    def both(first, second):
        na, ns = len(first.arrays), len(first.sems)

        def fn(ins, outs, sems, phase):
            first.fn(ins[:na], outs[:na], sems[:ns], phase)
            second.fn(ins[na:], outs[na:], sems[ns:], phase)

        job = Job(first.arrays + second.arrays, first.out_shapes + second.out_shapes, first.sems + second.sems, fn)
        job.in_place = [first.in_place] * na + [second.in_place] * len(second.arrays)
        return job

    def aliased(self):
        flags = self.in_place if isinstance(self.in_place, list) else [self.in_place] * len(self.arrays)
        return [a for a, flag in enumerate(flags) if flag]


def _call(body, *, name, grid, in_specs, out_specs, out_shape, args, semantics, scratch=(), aliases=None, job=None):
    if job is None:
        return pl.pallas_call(
            body, name=name, grid=grid, in_specs=list(in_specs), out_specs=list(out_specs),
            out_shape=list(out_shape), scratch_shapes=list(scratch), input_output_aliases=aliases or {},
            compiler_params=_params(*semantics))(*args)
    counts = [len(in_specs), len(job.arrays), len(out_specs), len(job.out_shapes), len(scratch), len(job.sems)]

    def hosted(*refs):
        parts, pos = [], 0
        for k in counts:
            parts.append(refs[pos:pos + k])
            pos += k
        ins, job_ins, outs, job_outs, scr, job_sems = parts
        steps = [pl.program_id(d) for d in range(len(grid))]
        first = functools.reduce(jnp.logical_and, [s == 0 for s in steps])
        last = functools.reduce(jnp.logical_and, [s == g - 1 for s, g in zip(steps, grid, strict=True)])

        @pl.when(first)
        def _():
            job.fn(job_ins, job_outs, job_sems, "start")

        body(*ins, *outs, *scr)

        @pl.when(last)
        def _():
            job.fn(job_ins, job_outs, job_sems, "finish")

    any_spec = pl.BlockSpec(memory_space=pl.ANY)
    aliases = dict(aliases or {})
    aliases.update({len(in_specs) + a: len(out_specs) + a for a in job.aliased()})
    res = pl.pallas_call(
        hosted, name=name, grid=grid, in_specs=list(in_specs) + [any_spec] * len(job.arrays),
        out_specs=list(out_specs) + [any_spec] * len(job.out_shapes),
        out_shape=list(out_shape) + job.out_shapes, scratch_shapes=list(scratch) + job.sems,
        input_output_aliases=aliases,
        compiler_params=_params(*["arbitrary"] * len(grid)))(*args, *job.arrays)
    return res[:len(out_specs)], res[len(out_specs):]


def _tile(n, cap, mult):
    best = None
    for t in range(mult, min(n, cap) + 1, mult):
        if n % t == 0:
            best = t
    assert best is not None, (n, cap, mult)
    return best


def _with_job(res, job):
    return res[0] if job is None else (res[0][0], res[1])


def mm_nn(a, b, *, name, bias=None, add=None, out_dtype=F32, job=None):
    m, k = a.shape
    s, _, ns = b.shape
    n = s * ns
    tm, tn = _tile(m, 1088, 16), _tile(ns, 1024, 128)
    per = ns // tn

    def body(*refs):
        a_ref, b_ref = refs[0], refs[1]
        o_ref = refs[-1]
        acc = jnp.dot(a_ref[...].astype(BF16), b_ref[...], preferred_element_type=F32)
        pos = 2
        if bias is not None:
            acc = acc + refs[pos][...]
            pos += 1
        if add is not None:
            acc = acc + refs[pos][...]
        o_ref[...] = acc.astype(o_ref.dtype)

    in_specs = [pl.BlockSpec((tm, k), lambda i, j: (i, 0)),
                pl.BlockSpec((None, k, tn), lambda i, j: (j // per, 0, j % per))]
    args = [a, b]
    if bias is not None:
        in_specs.append(pl.BlockSpec((1, tn), lambda i, j: (0, j)))
        args.append(bias)
    if add is not None:
        in_specs.append(pl.BlockSpec((tm, tn), lambda i, j: (i, j)))
        args.append(add)
    return _with_job(_call(
        body, name=name, grid=(m // tm, n // tn), in_specs=in_specs,
        out_specs=[pl.BlockSpec((tm, tn), lambda i, j: (i, j))],
        out_shape=[jax.ShapeDtypeStruct((m, n), out_dtype)], args=args,
        semantics=("parallel", "parallel"), job=job), job)


def norm_mm_nn(h, gain, b, *, name, job=None):
    m, k = h.shape
    s, _, ns = b.shape
    n = s * ns
    tm, tn = _tile(m, 1088, 16), _tile(ns, 1024, 128)
    per = ns // tn

    def body(h_ref, g_ref, b_ref, o_ref, n_ref):
        @pl.when(pl.program_id(1) == 0)
        def _():
            n_ref[...] = _rms(h_ref[...], g_ref[...]).astype(n_ref.dtype)

        o_ref[...] = jnp.dot(n_ref[...], b_ref[...], preferred_element_type=F32)

    res = _call(
        body, name=name, grid=(m // tm, n // tn),
        in_specs=[pl.BlockSpec((tm, k), lambda i, j: (i, 0)), pl.BlockSpec((1, k), lambda i, j: (0, 0)),
                  pl.BlockSpec((None, k, tn), lambda i, j: (j // per, 0, j % per))],
        out_specs=[pl.BlockSpec((tm, tn), lambda i, j: (i, j)), pl.BlockSpec((tm, k), lambda i, j: (i, 0))],
        out_shape=[jax.ShapeDtypeStruct((m, n), F32), jax.ShapeDtypeStruct((m, k), BF16)], args=(h, gain, b),
        semantics=("parallel", "arbitrary"), job=job)
    return (res[0], res[1]) if job is None else ((res[0][0], res[0][1]), res[1])


def glu_gate(yg, w_glu, b_glu, y0, uz, *, name, job=None):
    m, k = yg.shape
    n = w_glu.shape[2]
    tm = _tile(m, 1088, 16)

    def body(a_ref, w_ref, b_ref, y0_ref, z_ref, q_ref, o_ref):
        q = jnp.dot(a_ref[...], w_ref[...], preferred_element_type=F32) + b_ref[...]
        q_ref[...] = q
        o_ref[...] = _s5_gate(y0_ref[...], q, z_ref[...]).astype(o_ref.dtype)

    tile = pl.BlockSpec((tm, n), lambda i: (i, 0))
    res = _call(
        body, name=name, grid=(m // tm,),
        in_specs=[pl.BlockSpec((tm, k), lambda i: (i, 0)), pl.BlockSpec((None, k, n), lambda i: (0, 0, 0)),
                  pl.BlockSpec((1, n), lambda i: (0, 0)), tile, pl.BlockSpec((tm, n), lambda i: (i, 1))],
        out_specs=[tile, tile],
        out_shape=[jax.ShapeDtypeStruct((m, n), F32), jax.ShapeDtypeStruct((m, n), BF16)],
        args=(yg, w_glu, b_glu, y0, uz), semantics=("parallel",), job=job)
    return (res[0], res[1]) if job is None else ((res[0][0], res[0][1]), res[1])


def s5_gate_bwd(dh, w_out, y0, q, uz, *, name):
    m, d = dh.shape
    e = w_out.shape[1]
    tm = _tile(m, 544, 16)

    def body(dh_ref, w_ref, y0_ref, q_ref, z_ref, dy0_ref, dq_ref, dz_ref, db_ref):
        i = pl.program_id(0)
        dy3 = lax.dot_general(dh_ref[...].astype(BF16), w_ref[...], (((1,), (1,)), ((), ())),
                              preferred_element_type=F32)
        dy0, dq, dz = jax.vjp(_s5_gate, y0_ref[...], q_ref[...], z_ref[...])[1](dy3)
        dy0_ref[...] = dy0
        dq_ref[...] = dq.astype(dq_ref.dtype)
        dz_ref[...] = dz.astype(dz_ref.dtype)
        db = _colsum(dq)

        @pl.when(i == 0)
        def _():
            db_ref[...] = db

        @pl.when(i > 0)
        def _():
            db_ref[...] += db

    tile = pl.BlockSpec((tm, e), lambda i: (i, 0))
    right = pl.BlockSpec((tm, e), lambda i: (i, 1))
    return pl.pallas_call(
        body, name=name, grid=(m // tm,),
        in_specs=[pl.BlockSpec((tm, d), lambda i: (i, 0)), pl.BlockSpec((None, e, d), lambda i: (0, 0, 0)),
                  tile, tile, right],
        out_specs=[tile, tile, right, pl.BlockSpec((1, e), lambda i: (0, 0))],
        out_shape=[jax.ShapeDtypeStruct((m, e), F32), jax.ShapeDtypeStruct((m, e), BF16),
                   jax.ShapeDtypeStruct((m, 2 * e), BF16), jax.ShapeDtypeStruct((1, e), F32)],
        compiler_params=_params("arbitrary"))(dh, w_out, y0, q, uz)


def pool_gate_bwd(dh, w_out, o, p, scale, *, name):
    m, d = dh.shape
    e = w_out.shape[1]
    tm, te = _tile(m, 1088, 16), 1024
    nj = e // te

    def body(dh_ref, w_ref, o_ref, z_ref, s_ref, do_ref, dz_ref, ds_ref, db_ref):
        i = pl.program_id(1)
        dy3 = lax.dot_general(dh_ref[...].astype(BF16), w_ref[...], (((1,), (1,)), ((), ())),
                              preferred_element_type=F32)
        do, dz, ds = jax.vjp(_pool_gate, o_ref[...], z_ref[...], s_ref[...])[1](dy3)
        do_ref[...] = do.astype(do_ref.dtype)
        dz_ref[...] = dz.astype(dz_ref.dtype)
        db = _colsum(do)

        @pl.when(i == 0)
        def _():
            ds_ref[...] = ds
            db_ref[...] = db

        @pl.when(i > 0)
        def _():
            ds_ref[...] += ds
            db_ref[...] += db

    tile = pl.BlockSpec((tm, te), lambda j, i: (i, j))
    right = pl.BlockSpec((tm, te), lambda j, i: (i, nj + j))
    vec = pl.BlockSpec((1, te), lambda j, i: (0, j))
    return pl.pallas_call(
        body, name=name, grid=(nj, m // tm),
        in_specs=[pl.BlockSpec((tm, d), lambda j, i: (i, 0)), pl.BlockSpec((None, te, d), lambda j, i: (0, j, 0)),
                  tile, right, vec],
        out_specs=[tile, right, vec, vec],
        out_shape=[jax.ShapeDtypeStruct((m, e), BF16), jax.ShapeDtypeStruct((m, 2 * e), BF16),
                   jax.ShapeDtypeStruct((1, e), F32), jax.ShapeDtypeStruct((1, e), F32)],
        compiler_params=_params("parallel", "arbitrary"))(dh, w_out, o, p, scale)


def mm_nt_norm_bwd(g, w, h, gain, dh_out, *, name, job=None):
    m, kc = g.shape
    s, nout, kcs = w.shape
    assert s * kcs == kc
    tm, tk = _tile(m, 1088, 16), _tile(kcs, 1024, 128)
    per = kcs // tk
    nk = kc // tk

    def body(g_ref, w_ref, h_ref, gain_ref, dh_ref, o_ref, dg_ref):
        i, kk = pl.program_id(0), pl.program_id(1)
        part = lax.dot_general(g_ref[...].astype(BF16), w_ref[...], (((1,), (1,)), ((), ())),
                               preferred_element_type=F32)

        @pl.when(kk == 0)
        def _():
            o_ref[...] = part

        @pl.when(kk > 0)
        def _():
            o_ref[...] += part

        @pl.when(kk == nk - 1)
        def _():
            dh, dg = jax.vjp(_rms, h_ref[...], gain_ref[...])[1](o_ref[...])
            o_ref[...] = dh_ref[...] + dh

            @pl.when(i == 0)
            def _():
                dg_ref[...] = dg

            @pl.when(i > 0)
            def _():
                dg_ref[...] += dg

    row_tile = pl.BlockSpec((tm, nout), lambda i, kk: (i, 0))
    res = _call(
        body, name=name, grid=(m // tm, nk),
        in_specs=[pl.BlockSpec((tm, tk), lambda i, kk: (i, kk)),
                  pl.BlockSpec((None, nout, tk), lambda i, kk: (kk // per, 0, kk % per)),
                  row_tile, pl.BlockSpec((1, nout), lambda i, kk: (0, 0)), row_tile],
        out_specs=[row_tile, pl.BlockSpec((1, nout), lambda i, kk: (0, 0))],
        out_shape=[jax.ShapeDtypeStruct((m, nout), F32), jax.ShapeDtypeStruct((1, nout), F32)],
        args=(g, w, h, gain, dh_out), semantics=("arbitrary", "arbitrary"), job=job)
    return (res[0], res[1]) if job is None else ((res[0][0], res[0][1]), res[1])


def mm_nt(g, w, *, name, job=None):
    m, kc = g.shape
    s, nout, kcs = w.shape
    assert s * kcs == kc
    tm, tno, tk = _tile(m, 2176, 16), _tile(nout, 1024, 128), _tile(kcs, 1024, 128)
    per = kcs // tk

    def body(g_ref, w_ref, o_ref):
        kk = pl.program_id(2)
        part = lax.dot_general(g_ref[...].astype(BF16), w_ref[...], (((1,), (1,)), ((), ())),
                               preferred_element_type=F32)

        @pl.when(kk == 0)
        def _():
            o_ref[...] = part

        @pl.when(kk > 0)
        def _():
            o_ref[...] += part

    return _with_job(_call(
        body, name=name, grid=(m // tm, nout // tno, kc // tk),
        in_specs=[pl.BlockSpec((tm, tk), lambda i, j, kk: (i, kk)),
                  pl.BlockSpec((None, tno, tk), lambda i, j, kk: (kk // per, j, kk % per))],
        out_specs=[pl.BlockSpec((tm, tno), lambda i, j, kk: (i, j))],
        out_shape=[jax.ShapeDtypeStruct((m, nout), F32)], args=(g, w),
        semantics=("parallel", "parallel", "arbitrary"), job=job), job)


def mm_tn(a, g, shards, *, name, job=None):
    m, ka = a.shape
    n = g.shape[1]
    ns = n // shards
    tka, tn = _tile(ka, 1024, 128), _tile(ns, 512, 128)
    row_bytes = tka * jnp.dtype(a.dtype).itemsize + tn * jnp.dtype(g.dtype).itemsize
    tm = _tile(m, TN_OPERAND_BYTES // (2 * row_bytes), 16)
    per = ns // tn

    def body(a_ref, g_ref, o_ref):
        mm = pl.program_id(2)
        part = lax.dot_general(a_ref[...].astype(BF16), g_ref[...].astype(BF16), (((0,), (0,)), ((), ())),
                               preferred_element_type=F32)

        @pl.when(mm == 0)
        def _():
            o_ref[...] = part

        @pl.when(mm > 0)
        def _():
            o_ref[...] += part

    return _with_job(_call(
        body, name=name, grid=(ka // tka, n // tn, m // tm),
        in_specs=[pl.BlockSpec((tm, tka), lambda i, j, mm: (mm, i)),
                  pl.BlockSpec((tm, tn), lambda i, j, mm: (mm, j))],
        out_specs=[pl.BlockSpec((None, tka, tn), lambda i, j, mm: (j // per, i, j % per))],
        out_shape=[jax.ShapeDtypeStruct((shards, ka, ns), F32)], args=(a, g),
        semantics=("parallel", "parallel", "arbitrary"), job=job), job)


def grp_nn_gate(a, w, bias, p, scale, *, name):
    m = a.shape[0]
    tm = _tile(m, 1088, 16)
    ng = len(POOL_WINDOWS)

    def body(a_ref, w_ref, b_ref, z_ref, s_ref, o_ref, y_ref):
        wj = w_ref[...].reshape(POOL_GROUP, POOL_GROUP)
        o = jnp.dot(a_ref[...].astype(BF16), wj, preferred_element_type=F32) + b_ref[...]
        o_ref[...] = o
        y_ref[...] = _pool_gate(o, z_ref[...], s_ref[...]).astype(y_ref.dtype)

    tile = pl.BlockSpec((tm, POOL_GROUP), lambda i, j: (i, j))
    vec = pl.BlockSpec((1, POOL_GROUP), lambda i, j: (0, j))
    return pl.pallas_call(
        body, name=name, grid=(m // tm, ng),
        in_specs=[tile,
                  pl.BlockSpec((N_CHIPS, None, POOL_GROUP // N_CHIPS, POOL_GROUP), lambda i, j: (0, j, 0, 0)),
                  vec, pl.BlockSpec((tm, POOL_GROUP), lambda i, j: (i, ng + j)), vec],
        out_specs=[tile, tile],
        out_shape=[jax.ShapeDtypeStruct((m, ng * POOL_GROUP), F32), jax.ShapeDtypeStruct((m, ng * POOL_GROUP), BF16)],
        compiler_params=_params("parallel", "parallel"))(a, w, bias, p, scale)


def grp_nt(g, w, *, name):
    m = g.shape[0]
    tm = _tile(m, 1088, 16)
    ng = len(POOL_WINDOWS)

    def body(g_ref, w_ref, o_ref):
        wj = w_ref[...].reshape(POOL_GROUP, POOL_GROUP)
        o_ref[...] = lax.dot_general(g_ref[...].astype(BF16), wj, (((1,), (1,)), ((), ())),
                                     preferred_element_type=F32)

    return pl.pallas_call(
        body, name=name, grid=(m // tm, ng),
        in_specs=[pl.BlockSpec((tm, POOL_GROUP), lambda i, j: (i, j)),
                  pl.BlockSpec((N_CHIPS, None, POOL_GROUP // N_CHIPS, POOL_GROUP), lambda i, j: (0, j, 0, 0))],
        out_specs=pl.BlockSpec((tm, POOL_GROUP), lambda i, j: (i, j)),
        out_shape=jax.ShapeDtypeStruct((m, ng * POOL_GROUP), F32),
        compiler_params=_params("parallel", "parallel"))(g, w)


def grp_tn(a, g, *, name):
    m = a.shape[0]
    tm = _tile(m, 1088, 16)
    ng = len(POOL_WINDOWS)
    rows = POOL_GROUP // N_CHIPS

    def body(a_ref, g_ref, o_ref):
        mm = pl.program_id(1)
        part = lax.dot_general(a_ref[...].astype(BF16), g_ref[...].astype(BF16), (((0,), (0,)), ((), ())),
                               preferred_element_type=F32).reshape(N_CHIPS, rows, POOL_GROUP)

        @pl.when(mm == 0)
        def _():
            o_ref[...] = part

        @pl.when(mm > 0)
        def _():
            o_ref[...] += part

    return pl.pallas_call(
        body, name=name, grid=(ng, m // tm),
        in_specs=[pl.BlockSpec((tm, POOL_GROUP), lambda j, mm: (mm, j)),
                  pl.BlockSpec((tm, POOL_GROUP), lambda j, mm: (mm, j))],
        out_specs=pl.BlockSpec((N_CHIPS, None, rows, POOL_GROUP), lambda j, mm: (0, j, 0, 0)),
        out_shape=jax.ShapeDtypeStruct((N_CHIPS, ng, rows, POOL_GROUP), F32),
        compiler_params=_params("parallel", "arbitrary"))(a, g)


def final_loss(h, target, gain, first, last, *, name):
    m, d = h.shape
    tr = m // 16

    def body(h_ref, t_ref, g_ref, dh_ref, dg_ref, sq_ref):
        i = pl.program_id(0)
        y, vjp = jax.vjp(_rms, h_ref[...], g_ref[...])
        pos = i * tr + lax.broadcasted_iota(jnp.int32, (tr, 1), 0)
        diff = jnp.where((pos >= first) & (pos < last), y - t_ref[...], 0.0)
        dh_ref[...], dg = vjp(diff / d)
        sq = _colsum(diff * diff)

        @pl.when(i == 0)
        def _():
            dg_ref[...] = dg
            sq_ref[...] = sq

        @pl.when(i > 0)
        def _():
            dg_ref[...] += dg
            sq_ref[...] += sq

    tile, vec = pl.BlockSpec((tr, d), lambda i: (i, 0)), pl.BlockSpec((1, d), lambda i: (0, 0))
    return pl.pallas_call(
        body, name=name, grid=(m // tr,), in_specs=[tile, tile, vec], out_specs=[tile, vec, vec],
        out_shape=[jax.ShapeDtypeStruct((m, d), F32), jax.ShapeDtypeStruct((1, d), F32),
                   jax.ShapeDtypeStruct((1, d), F32)],
        compiler_params=_params("arbitrary"))(h, target, gain)


def _rms(h, g):
    return h * lax.rsqrt(jnp.mean(h * h, axis=-1, keepdims=True) + EPS) * g


def _colsum(v):
    return jnp.sum(v, axis=0, keepdims=True)


def _s5_gate(y0, q, z):
    yg = jax.nn.gelu(y0)
    return yg * jax.nn.sigmoid(q) * jax.nn.silu(z)


def _pool_gate(o, z, scale):
    return o * scale * jax.nn.silu(z)


def _shift_rows(v, d, down):
    t = v.shape[0]
    row = lax.broadcasted_iota(jnp.int32, v.shape, 0)
    if down:
        return jnp.where(row >= d, pltpu.roll(v, d, 0), 0.0)
    return jnp.where(row < t - d, pltpu.roll(v, t - d, 0), 0.0)


def _cmul(ar, ai, br, bi):
    return ar * br - ai * bi, ar * bi + ai * br


def _scan(xr, xi, ar, ai, cr, ci, down):
    t, n = xr.shape
    nb = t // SUBLANES
    pw = [(ar, ai)]
    for _ in range(SUBLANES - 1):
        pw.append(_cmul(*pw[-1], ar, ai))
    sub = lax.broadcasted_iota(jnp.int32, (SUBLANES, n), 0)

    def table(exps):
        tr, ti = jnp.zeros((SUBLANES, n), F32), jnp.zeros((SUBLANES, n), F32)
        for r, e in enumerate(exps):
            if e:
                tr, ti = jnp.where(sub == r, pw[e - 1][0], tr), jnp.where(sub == r, pw[e - 1][1], ti)
        return tr[None], ti[None]

    sr, si = xr.reshape(nb, SUBLANES, n), xi.reshape(nb, SUBLANES, n)
    d = 1
    while d < SUBLANES:
        mr, mi = table([d if (r >= d if down else r < SUBLANES - d) else 0 for r in range(SUBLANES)])
        turn = d if down else SUBLANES - d
        hr, hi = pltpu.roll(sr, turn, 1), pltpu.roll(si, turn, 1)
        sr, si = sr + mr * hr - mi * hi, si + mr * hi + mi * hr
        d *= 2
    edge = SUBLANES - 1 if down else 0
    qr, qi = table([r + 1 if down else SUBLANES - r for r in range(SUBLANES)])
    qr, qi = qr[0], qi[0]
    in_r, in_i = jnp.broadcast_to(cr, (SUBLANES, n)), jnp.broadcast_to(ci, (SUBLANES, n))
    out_r, out_i = [None] * nb, [None] * nb
    for b in (range(nb) if down else reversed(range(nb))):
        out_r[b] = sr[b] + qr * in_r - qi * in_i
        out_i[b] = si[b] + qr * in_i + qi * in_r
        in_r = jnp.broadcast_to(out_r[b][edge:edge + 1, :], (SUBLANES, n))
        in_i = jnp.broadcast_to(out_i[b][edge:edge + 1, :], (SUBLANES, n))
    return jnp.concatenate(out_r, axis=0), jnp.concatenate(out_i, axis=0), in_r[0:1, :], in_i[0:1, :]


def s5_fwd(uz, bblk, cblk, ar, ai, dskip, *, name, job=None):
    lp = uz.shape[0]
    t = SCAN_ROWS
    nst = S5_GROUPS * S5_STATE

    def body(u_ref, b_ref, c_ref, ar_ref, ai_ref, d_ref, y_ref, yg_ref, sr_ref, si_ref, cr, ci):
        step = pl.program_id(1)

        @pl.when(step == 0)
        def _():
            cr[...] = jnp.zeros_like(cr)
            ci[...] = jnp.zeros_like(ci)

        u = u_ref[...]
        a_r, a_i = ar_ref[...], ai_ref[...]
        x = jnp.dot(u.astype(BF16), b_ref[...].astype(BF16), preferred_element_type=F32)
        sr, si, cr[...], ci[...] = _scan(x[:, :STATE_BLOCK], x[:, STATE_BLOCK:], a_r, a_i, cr[...], ci[...], True)
        sr_ref[...] = sr
        si_ref[...] = si
        s = jnp.concatenate([sr, si], axis=1).astype(BF16)
        y0 = lax.dot_general(s, c_ref[...].astype(BF16), (((1,), (1,)), ((), ())),
                             preferred_element_type=F32) + d_ref[...] * u
        y_ref[...] = y0
        yg_ref[...] = jax.nn.gelu(y0).astype(yg_ref.dtype)

    return _call(
        body, name=name, grid=(S5_BLOCKS, lp // t),
        in_specs=[pl.BlockSpec((t, LANE_BLOCK), lambda b, c: (c, b)),
                  pl.BlockSpec((None, LANE_BLOCK, 2 * STATE_BLOCK), lambda b, c: (b, 0, 0)),
                  pl.BlockSpec((None, LANE_BLOCK, 2 * STATE_BLOCK), lambda b, c: (b, 0, 0)),
                  pl.BlockSpec((1, STATE_BLOCK), lambda b, c: (0, b)),
                  pl.BlockSpec((1, STATE_BLOCK), lambda b, c: (0, b)),
                  pl.BlockSpec((1, LANE_BLOCK), lambda b, c: (0, b))],
        out_specs=[pl.BlockSpec((t, LANE_BLOCK), lambda b, c: (c, b)),
                   pl.BlockSpec((t, LANE_BLOCK), lambda b, c: (c, b)),
                   pl.BlockSpec((t, STATE_BLOCK), lambda b, c: (c, b)),
                   pl.BlockSpec((t, STATE_BLOCK), lambda b, c: (c, b))],
        out_shape=[jax.ShapeDtypeStruct((lp, D_MODEL), F32), jax.ShapeDtypeStruct((lp, D_MODEL), BF16),
                   jax.ShapeDtypeStruct((lp, nst), F32), jax.ShapeDtypeStruct((lp, nst), F32)],
        scratch=[pltpu.VMEM((1, STATE_BLOCK), F32), pltpu.VMEM((1, STATE_BLOCK), F32)],
        args=(uz, bblk, cblk, ar, ai, dskip), semantics=("parallel", "arbitrary"), job=job)


def s5_bwd(dy_direct, dyg, y0, uz, sr, si, bblk, cblk, ar, ai, dskip, dp, *, name, job=None):
    lp = uz.shape[0]
    t = SCAN_ROWS
    nch = lp // t
    nst = S5_GROUPS * S5_STATE

    def body(dyd_ref, dyg_ref, y0_ref, u_ref, sr_ref, si_ref, b_ref, c_ref, ar_ref, ai_ref, d_ref, _, du_ref,
             db_ref, dc_ref, dar_ref, dai_ref, dd_ref, cr, ci):
        step = pl.program_id(1)

        @pl.when(step == 0)
        def _():
            cr[...] = jnp.zeros_like(cr)
            ci[...] = jnp.zeros_like(ci)

        dy = dyd_ref[...] + jax.vjp(jax.nn.gelu, y0_ref[...])[1](dyg_ref[...])[0]
        u = u_ref[...]
        dyb, ub = dy.astype(BF16), u.astype(BF16)
        a_r, a_i = ar_ref[...], -ai_ref[...]
        g = jnp.dot(dyb, c_ref[...].astype(BF16), preferred_element_type=F32)
        gr, gi = g[:, :STATE_BLOCK], g[:, STATE_BLOCK:]
        nxt_r, nxt_i = cr[...], ci[...]
        last = lax.broadcasted_iota(jnp.int32, gr.shape, 0) == t - 1
        lr, li, cr[...], ci[...] = _scan(gr, gi, a_r, a_i, nxt_r, nxt_i, False)
        nr = _shift_rows(lr, 1, False) + jnp.where(last, nxt_r, 0.0)
        ni = _shift_rows(li, 1, False) + jnp.where(last, nxt_i, 0.0)
        s_r, s_i = sr_ref[...], si_ref[...]
        dar = _colsum(s_r * nr + s_i * ni)
        dai = _colsum(s_r * ni - s_i * nr)
        lam = jnp.concatenate([lr, li], axis=1).astype(BF16)
        sb = jnp.concatenate([s_r, s_i], axis=1).astype(BF16)
        dc = lax.dot_general(dyb, sb, (((0,), (0,)), ((), ())), preferred_element_type=F32)
        db = lax.dot_general(ub, lam, (((0,), (0,)), ((), ())), preferred_element_type=F32)
        du_ref[...] = (lax.dot_general(lam, b_ref[...].astype(BF16), (((1,), (1,)), ((), ())),
                                       preferred_element_type=F32) + d_ref[...] * dy).astype(du_ref.dtype)
        dd = _colsum(dy * u)

        @pl.when(step == 0)
        def _():
            db_ref[...] = db
            dc_ref[...] = dc
            dar_ref[...] = dar
            dai_ref[...] = dai
            dd_ref[...] = dd

        @pl.when(step > 0)
        def _():
            db_ref[...] += db
            dc_ref[...] += dc
            dar_ref[...] += dar
            dai_ref[...] += dai
            dd_ref[...] += dd

    rev = lambda b, c: (nch - 1 - c, b)
    return _call(
        body, name=name, grid=(S5_BLOCKS, nch),
        in_specs=[pl.BlockSpec((t, LANE_BLOCK), rev), pl.BlockSpec((t, LANE_BLOCK), rev),
                  pl.BlockSpec((t, LANE_BLOCK), rev), pl.BlockSpec((t, LANE_BLOCK), rev),
                  pl.BlockSpec((t, STATE_BLOCK), rev), pl.BlockSpec((t, STATE_BLOCK), rev),
                  pl.BlockSpec((None, LANE_BLOCK, 2 * STATE_BLOCK), lambda b, c: (b, 0, 0)),
                  pl.BlockSpec((None, LANE_BLOCK, 2 * STATE_BLOCK), lambda b, c: (b, 0, 0)),
                  pl.BlockSpec((1, STATE_BLOCK), lambda b, c: (0, b)),
                  pl.BlockSpec((1, STATE_BLOCK), lambda b, c: (0, b)),
                  pl.BlockSpec((1, LANE_BLOCK), lambda b, c: (0, b)),
                  pl.BlockSpec(memory_space=pl.ANY)],
        out_specs=[pl.BlockSpec((t, LANE_BLOCK), rev),
                   pl.BlockSpec((None, LANE_BLOCK, 2 * STATE_BLOCK), lambda b, c: (b, 0, 0)),
                   pl.BlockSpec((None, LANE_BLOCK, 2 * STATE_BLOCK), lambda b, c: (b, 0, 0)),
                   pl.BlockSpec((1, STATE_BLOCK), lambda b, c: (0, b)),
                   pl.BlockSpec((1, STATE_BLOCK), lambda b, c: (0, b)),
                   pl.BlockSpec((1, LANE_BLOCK), lambda b, c: (0, b))],
        out_shape=[jax.ShapeDtypeStruct(dp.shape, dp.dtype),
                   jax.ShapeDtypeStruct((S5_BLOCKS, LANE_BLOCK, 2 * STATE_BLOCK), F32),
                   jax.ShapeDtypeStruct((S5_BLOCKS, LANE_BLOCK, 2 * STATE_BLOCK), F32),
                   jax.ShapeDtypeStruct((1, nst), F32), jax.ShapeDtypeStruct((1, nst), F32),
                   jax.ShapeDtypeStruct((1, D_MODEL), F32)],
        scratch=[pltpu.VMEM((1, STATE_BLOCK), F32), pltpu.VMEM((1, STATE_BLOCK), F32)],
        aliases={11: 0}, args=(dy_direct, dyg, y0, uz, sr, si, bblk, cblk, ar, ai, dskip, dp),
        semantics=("parallel", "arbitrary"), job=job)


def _s5_prep(lam_re, lam_im, log_dt, b_re, b_im, c_re, c_im):
    dt = jnp.exp(log_dt)
    mag = jnp.exp(lam_re * dt)
    ar = mag * jnp.cos(lam_im * dt)
    ai = mag * jnp.sin(lam_im * dt)
    den = lam_re * lam_re + lam_im * lam_im
    kr = ((ar - 1.0) * lam_re + ai * lam_im) / den
    ki = (ai * lam_re - (ar - 1.0) * lam_im) / den
    bbr = kr[:, None, :] * b_re - ki[:, None, :] * b_im
    bbi = kr[:, None, :] * b_im + ki[:, None, :] * b_re
    rows = S5_GROUPS * S5_GROUP
    expand = (lax.broadcasted_iota(jnp.int32, (S5_STATE, STATE_BLOCK), 1) % S5_STATE
              == lax.broadcasted_iota(jnp.int32, (S5_STATE, STATE_BLOCK), 0)).astype(F32)
    own = ((lax.broadcasted_iota(jnp.int32, (rows, STATE_BLOCK), 0) // S5_GROUP) % GROUPS_PER_BLOCK
           == lax.broadcasted_iota(jnp.int32, (rows, STATE_BLOCK), 1) // S5_STATE).astype(F32)

    def blocks(v):
        wide = jnp.dot(v.reshape(rows, S5_STATE), expand, precision=lax.Precision.HIGHEST,
                       preferred_element_type=F32)
        return (wide * own).reshape(S5_BLOCKS, LANE_BLOCK, STATE_BLOCK)

    btab = jnp.concatenate([blocks(bbr), blocks(bbi)], axis=2)
    ctab = jnp.concatenate([blocks(c_re), -blocks(c_im)], axis=2)
    return ar, ai, btab, ctab


_S5_PREP_OUT = [(S5_GROUPS, S5_STATE), (S5_GROUPS, S5_STATE), (S5_BLOCKS, LANE_BLOCK, 2 * STATE_BLOCK),
                (S5_BLOCKS, LANE_BLOCK, 2 * STATE_BLOCK)]


def s5_prep(params, *, name):
    def body(*refs):
        for ref, val in zip(refs[7:], _s5_prep(*[r[...] for r in refs[:7]]), strict=True):
            ref[...] = val

    return pl.pallas_call(body, name=name, out_shape=[jax.ShapeDtypeStruct(s, F32) for s in _S5_PREP_OUT],
                          compiler_params=pltpu.CompilerParams(vmem_limit_bytes=VMEM_LIMIT))(*params)


def s5_prep_bwd(params, cotangents, *, name):
    def body(*refs):
        grads = jax.vjp(_s5_prep, *[r[...] for r in refs[:7]])[1](tuple(r[...] for r in refs[7:11]))
        for ref, val in zip(refs[11:], grads, strict=True):
            ref[...] = val

    return pl.pallas_call(body, name=name, out_shape=[jax.ShapeDtypeStruct(p.shape, F32) for p in params],
                          compiler_params=pltpu.CompilerParams(vmem_limit_bytes=VMEM_LIMIT))(*params, *cotangents)


def _silu_grad(z):
    s = jax.nn.sigmoid(z)
    return s * (1.0 + z * (1.0 - s))


def conv_fwd(p, conv_w, conv_b, *, name):
    lp = p.shape[0]
    tr = lp // 16
    e = CONV_E
    hb = tr // 8

    def body(p_ref, cgh_ref, vh_ref, w_ref, b_ref, o_ref):
        i = pl.program_id(0)
        bg, cg, v, z = (p_ref[:, k * e:(k + 1) * e] for k in range(4))
        hc = cg * v
        halo = jnp.where(i > 0, cgh_ref[...] * vh_ref[...], 0.0)
        ext = jnp.concatenate([halo, hc], axis=0)
        w = w_ref[...]
        conv = (w[0:1] * pltpu.roll(ext, 2, 0)[8:] + w[1:2] * pltpu.roll(ext, 1, 0)[8:] + w[2:3] * hc
                + b_ref[...])
        o_ref[...] = (bg * conv * jax.nn.silu(z)).astype(o_ref.dtype)

    prev = lambda col: (lambda i: (jnp.maximum(i * hb - 1, 0), col))
    return pl.pallas_call(
        body, name=name, grid=(lp // tr,),
        in_specs=[pl.BlockSpec((tr, 4 * e), lambda i: (i, 0)),
                  pl.BlockSpec((8, e), prev(1)), pl.BlockSpec((8, e), prev(2)),
                  pl.BlockSpec((3, e), lambda i: (0, 0)), pl.BlockSpec((1, e), lambda i: (0, 0))],
        out_specs=pl.BlockSpec((tr, e), lambda i: (i, 0)),
        out_shape=jax.ShapeDtypeStruct((lp, e), BF16),
        compiler_params=_params("parallel"))(p, p, p, conv_w, conv_b)


def conv_bwd(dh, w_out, p, conv_w, conv_b, *, name):
    lp = p.shape[0]
    tr = lp // 16
    e = CONV_E
    d = dh.shape[1]
    hb = tr // 8
    nt = lp // tr
    width = 512

    def body(dh_ref, p_ref, cgh_ref, vh_ref, dhn_ref, bgn_ref, zn_ref, wo_ref, w_ref, b_ref, dp_ref, dw_ref, db_ref):
        i = pl.program_id(0)
        rows, rows_n = dh_ref[...].astype(BF16), dhn_ref[...].astype(BF16)
        for c0 in range(0, e, width):
            cols = slice(c0, c0 + width)
            wo = wo_ref[cols, :]
            dy = lax.dot_general(rows, wo, (((1,), (1,)), ((), ())), preferred_element_type=F32)
            dy_n = lax.dot_general(rows_n, wo, (((1,), (1,)), ((), ())), preferred_element_type=F32)
            bg, cg, v, z = (p_ref[:, k * e + c0:k * e + c0 + width] for k in range(4))
            w = w_ref[:, cols]
            hc = cg * v
            halo = jnp.where(i > 0, cgh_ref[:, cols] * vh_ref[:, cols], 0.0)
            ext = jnp.concatenate([halo, hc], axis=0)
            hc2, hc1 = pltpu.roll(ext, 2, 0)[8:], pltpu.roll(ext, 1, 0)[8:]
            yc = w[0:1] * hc2 + w[1:2] * hc1 + w[2:3] * hc + b_ref[:, cols]
            sz = jax.nn.silu(z)
            dyc = dy * bg * sz
            dyc_n = jnp.where(i < nt - 1, dy_n * bgn_ref[:, cols] * jax.nn.silu(zn_ref[:, cols]), 0.0)
            ext_n = jnp.concatenate([dyc, dyc_n], axis=0)
            n_rows = tr + 8
            dhc = (w[2:3] * dyc + w[1:2] * pltpu.roll(ext_n, n_rows - 1, 0)[:tr]
                   + w[0:1] * pltpu.roll(ext_n, n_rows - 2, 0)[:tr])
            for k, val in enumerate((dy * yc * sz, dhc * v, dhc * cg, dy * bg * yc * _silu_grad(z))):
                dp_ref[:, k * e + c0:k * e + c0 + width] = val.astype(dp_ref.dtype)
            dw = jnp.concatenate([_colsum(dyc * hc2), _colsum(dyc * hc1), _colsum(dyc * hc)], axis=0)
            db = _colsum(dyc)

            @pl.when(i == 0)
            def _(cols=cols, dw=dw, db=db):
                dw_ref[:, cols] = dw
                db_ref[:, cols] = db

            @pl.when(i > 0)
            def _(cols=cols, dw=dw, db=db):
                dw_ref[:, cols] += dw
                db_ref[:, cols] += db

    prev = lambda col: (lambda i: (jnp.maximum(i * hb - 1, 0), col))
    nxt = lambda col: (lambda i: (jnp.minimum((i + 1) * hb, lp // 8 - 1), col))
    return pl.pallas_call(
        body, name=name, grid=(nt,),
        in_specs=[pl.BlockSpec((tr, d), lambda i: (i, 0)), pl.BlockSpec((tr, 4 * e), lambda i: (i, 0)),
                  pl.BlockSpec((8, e), prev(1)), pl.BlockSpec((8, e), prev(2)),
                  pl.BlockSpec((8, d), nxt(0)), pl.BlockSpec((8, e), nxt(0)), pl.BlockSpec((8, e), nxt(3)),
                  pl.BlockSpec((None, e, d), lambda i: (0, 0, 0)),
                  pl.BlockSpec((3, e), lambda i: (0, 0)), pl.BlockSpec((1, e), lambda i: (0, 0))],
        out_specs=[pl.BlockSpec((tr, 4 * e), lambda i: (i, 0)), pl.BlockSpec((3, e), lambda i: (0, 0)),
                   pl.BlockSpec((1, e), lambda i: (0, 0))],
        out_shape=[jax.ShapeDtypeStruct((lp, 4 * e), BF16), jax.ShapeDtypeStruct((3, e), F32),
                   jax.ShapeDtypeStruct((1, e), F32)],
        compiler_params=_params("arbitrary"))(dh, p, p, p, dh, p, p, w_out, conv_w, conv_b)


def _pool_counts(i, tr, rows, offset, w):
    t = (i * tr + offset + 1 + lax.broadcasted_iota(jnp.int32, (rows, 1), 0)).astype(F32)
    return jnp.minimum(t, float(w))


def pool_fwd(p, *, name):
    lp = p.shape[0]
    tr = lp // 16
    h = POOL_HALO
    hb = tr // h

    def body(u_ref, uh_ref, o_ref):
        i = pl.program_id(0)
        u = u_ref[...]
        ext = jnp.concatenate([jnp.where(i > 0, uh_ref[...], 0.0), u], axis=0)
        for k, w in enumerate(POOL_WINDOWS):
            cols = slice(k * POOL_GROUP, (k + 1) * POOL_GROUP)
            s = ext[:, cols]
            d = 1
            while d < w:
                s = s + pltpu.roll(s, d, 0)
                d *= 2
            o_ref[:, cols] = (s[h:] / _pool_counts(i, tr, tr, 0, w) - u[:, cols]).astype(o_ref.dtype)

    return pl.pallas_call(
        body, name=name, grid=(lp // tr,),
        in_specs=[pl.BlockSpec((tr, POOL_E), lambda i: (i, 0)),
                  pl.BlockSpec((h, POOL_E), lambda i: (jnp.maximum(i * hb - 1, 0), 0))],
        out_specs=pl.BlockSpec((tr, POOL_E), lambda i: (i, 0)),
        out_shape=jax.ShapeDtypeStruct((lp, POOL_E), BF16),
        compiler_params=_params("parallel"))(p, p)


def pool_bwd(dmix, dp, *, name):
    lp = dmix.shape[0]
    tr = lp // 16
    h = POOL_HALO
    hb = tr // h
    nt = lp // tr

    def body(dm_ref, dmn_ref, _, du_ref):
        i = pl.program_id(0)
        dm = dm_ref[...]
        dmn = jnp.where(i < nt - 1, dmn_ref[...], 0.0)
        n_rows = tr + h
        for k, w in enumerate(POOL_WINDOWS):
            cols = slice(k * POOL_GROUP, (k + 1) * POOL_GROUP)
            s = jnp.concatenate([dm[:, cols] / _pool_counts(i, tr, tr, 0, w),
                                 dmn[:, cols] / _pool_counts(i, tr, h, tr, w)], axis=0)
            d = 1
            while d < w:
                s = s + pltpu.roll(s, n_rows - d, 0)
                d *= 2
            du_ref[:, cols] = (s[:tr] - dm[:, cols]).astype(du_ref.dtype)

    return pl.pallas_call(
        body, name=name, grid=(nt,),
        in_specs=[pl.BlockSpec((tr, POOL_E), lambda i: (i, 0)),
                  pl.BlockSpec((h, POOL_E), lambda i: (jnp.minimum((i + 1) * hb, lp // h - 1), 0)),
                  pl.BlockSpec(memory_space=pl.ANY)],
        out_specs=pl.BlockSpec((tr, POOL_E), lambda i: (i, 0)),
        out_shape=jax.ShapeDtypeStruct(dp.shape, dp.dtype),
        input_output_aliases={2: 0},
        compiler_params=_params("parallel"))(dmix, dmix, dp)


def _adamw_math(w, g, m, v):
    nm = ADAM_B1 * m + (1.0 - ADAM_B1) * g
    nv = ADAM_B2 * v + (1.0 - ADAM_B2) * jnp.square(g)
    m_hat = nm / (1.0 - ADAM_B1 ** ADAM_STEP)
    v_hat = nv / (1.0 - ADAM_B2 ** ADAM_STEP)
    return -ADAM_LR * (m_hat / (jnp.sqrt(v_hat) + ADAM_EPS) + ADAM_WD * w), nm, nv


def adamw_many(quads, *, name, steps=8, job=None):
    n = len(quads)

    def spec(shape):
        return pl.BlockSpec((shape[0] // steps,) + shape[1:], lambda i, nd=len(shape): (i,) + (0,) * (nd - 1))

    def body(*refs):
        for q in range(n):
            w_ref, g_ref, m_ref, v_ref = refs[4 * q:4 * q + 4]
            d_ref, nm_ref, nv_ref = refs[4 * n + 3 * q:4 * n + 3 * q + 3]
            d_ref[...], nm_ref[...], nv_ref[...] = _adamw_math(w_ref[...], g_ref[...], m_ref[...], v_ref[...])

    res = _call(
        body, name=name, grid=(steps,), in_specs=[spec(quad[0].shape) for quad in quads for _ in range(4)],
        out_specs=[spec(quad[0].shape) for quad in quads for _ in range(3)],
        out_shape=[jax.ShapeDtypeStruct(quad[0].shape, F32) for quad in quads for _ in range(3)],
        args=[a for quad in quads for a in quad], semantics=("parallel",), job=job)
    outs = res if job is None else res[0]
    triples = [tuple(outs[3 * q:3 * q + 3]) for q in range(n)]
    return triples if job is None else (triples, res[1])


WIRE_WIDTH = 1024
WIRE_ROWS = 1024


def _wire_plan(shapes):
    starts, row = [], 0
    for s in shapes:
        r, c = (1, s[0]) if len(s) == 1 else s
        starts.append(row)
        row += -(-(r * max(1, c // WIRE_WIDTH)) // 8) * 8
    assert row <= WIRE_ROWS, row
    return starts


def _wire_cells(shape):
    if len(shape) == 1:
        n = shape[0]
        if n <= WIRE_WIDTH:
            return [(0, n, (slice(None),))]
        return [(h, WIRE_WIDTH, (slice(h * WIRE_WIDTH, (h + 1) * WIRE_WIDTH),)) for h in range(n // WIRE_WIDTH)]
    r, c = shape
    if c <= WIRE_WIDTH:
        return [(None, c, (slice(None), slice(None)))]
    per = c // WIRE_WIDTH
    return [(j * per + h, WIRE_WIDTH, (j, slice(h * WIRE_WIDTH, (h + 1) * WIRE_WIDTH)))
            for j in range(r) for h in range(per)]


def pack_small(arrays, *, name):
    shapes = [a.shape for a in arrays]
    starts = _wire_plan(shapes)

    def body(*refs):
        out = refs[-1]
        out[...] = jnp.zeros_like(out)
        for ref, shape, row0 in zip(refs[:-1], shapes, starts, strict=True):
            for off, cols, idx in _wire_cells(shape):
                if off is None:
                    out[row0:row0 + shape[0], 0:cols] = ref[idx]
                else:
                    out[row0 + off, 0:cols] = ref[idx]

    return pl.pallas_call(body, name=name, out_shape=jax.ShapeDtypeStruct((WIRE_ROWS, WIRE_WIDTH), F32),
                          compiler_params=pltpu.CompilerParams(vmem_limit_bytes=VMEM_LIMIT))(*arrays)


def update_small(packed, starts, triples, shard_cells, chip, *, name):
    n = len(triples)

    def body(chip_ref, p_ref, *refs):
        k = chip_ref[0]
        ins, outs = refs[:3 * n], refs[3 * n:]

        def cut(row0, rows, cols):
            c0, c1 = cols
            if isinstance(rows, int):
                return p_ref[row0 + rows, c0:c1]
            return p_ref[row0 + rows[0]:row0 + rows[1], c0:c1]

        for q in range(n):
            w_ref, m_ref, v_ref = ins[3 * q:3 * q + 3]
            g_ref, d_ref, nm_ref, nv_ref = outs[4 * q:4 * q + 4]
            shape = w_ref.shape
            if shard_cells[q] is None:
                pieces = [(cut(starts[q], (0, shape[0]) if off is None else off, (0, cols)), idx)
                          for off, cols, idx in _wire_cells(shape)]
            else:
                by_chip = [shard_cells[q](kk) for kk in range(N_CHIPS)]
                pieces = []
                for i, (_, _, idx) in enumerate(by_chip[0]):
                    g = cut(starts[q], *by_chip[0][i][:2])
                    for kk in range(1, N_CHIPS):
                        g = jnp.where(k == kk, cut(starts[q], *by_chip[kk][i][:2]), g)
                    pieces.append((g, idx))
            for g, idx in pieces:
                g_ref[idx] = g
                d_ref[idx], nm_ref[idx], nv_ref[idx] = _adamw_math(w_ref[idx], g, m_ref[idx], v_ref[idx])

    whole = lambda a: pl.BlockSpec(a.shape, lambda i, c_ref, nd=a.ndim: (0,) * nd)
    flat = [a for t in triples for a in t]
    out_shape = [jax.ShapeDtypeStruct(t[0].shape, F32) for t in triples for _ in range(4)]
    res = pl.pallas_call(
        body, name=name,
        grid_spec=pltpu.PrefetchScalarGridSpec(
            num_scalar_prefetch=1, grid=(1,),
            in_specs=[whole(packed)] + [whole(a) for a in flat],
            out_specs=[pl.BlockSpec(s.shape, lambda i, c_ref, nd=len(s.shape): (0,) * nd) for s in out_shape]),
        out_shape=out_shape,
        compiler_params=_params("arbitrary"))(chip.reshape(1).astype(jnp.int32), packed, *flat)
    return [tuple(res[4 * q:4 * q + 4]) for q in range(n)]


ANY = pl.BlockSpec(memory_space=pl.ANY)


def _place():
    x, y, c = lax.axis_index("x"), lax.axis_index("y"), lax.axis_index("c")
    return x, y, c, [(1 - x, y), (x, 1 - y), (1 - x, 1 - y)]


DMA_PIECE_BYTES = 512 * 1024


def _pieces(src, dst):
    shape = src.shape
    rows, cols = shape[-2], shape[-1]
    item = jnp.dtype(src.dtype).itemsize
    unit = 32 // item
    want = max(1, (rows * cols * item) // DMA_PIECE_BYTES)
    n = 1
    if rows % unit == 0:
        n = max(d for d in range(1, rows // unit + 1) if (rows // unit) % d == 0 and d <= want)
    size = rows // n
    out = []
    for lead in (range(shape[0]) if len(shape) == 3 else [None]):
        for i in range(n):
            sel = (pl.ds(i * size, size), slice(None)) if lead is None else (lead, pl.ds(i * size, size), slice(None))
            out.append((src.at[sel], dst.at[sel]))
    return out


def _send(src, dst, send_sem, recv_sem, peer, start=True):
    if start:
        for s, d in _pieces(src, dst):
            pltpu.make_async_remote_copy(src_ref=s, dst_ref=d, send_sem=send_sem, recv_sem=recv_sem,
                                         device_id=peer, device_id_type=MESH).start()
    return pltpu.make_async_remote_copy(src_ref=src, dst_ref=dst, send_sem=send_sem, recv_sem=recv_sem,
                                        device_id=peer, device_id_type=MESH)


def run_job(job, *, name):
    n_in, n_out = len(job.arrays), len(job.out_shapes)

    def body(*refs):
        job.fn(refs[:n_in], refs[n_in:n_in + n_out], refs[n_in + n_out:], "both")

    return pl.pallas_call(body, name=name, in_specs=[ANY] * n_in, out_specs=[ANY] * n_out,
                          out_shape=job.out_shapes, scratch_shapes=job.sems,
                          input_output_aliases={a: a for a in job.aliased()})(*job.arrays)


def gather_chips(shards, *, name):
    n = len(shards)

    def body(*refs):
        ins, outs = refs[:n], refs[n:2 * n]
        send_sems, recv_sems = refs[2 * n:]
        x, y, c, chips = _place()
        k = 2 * x + y
        copies = []
        for a in range(n):
            for j, peer in enumerate([(px, py, c) for px, py in chips] + [(x, y, 1 - c)]):
                copies.append(_send(ins[a], outs[a].at[k], send_sems.at[a, j], recv_sems.at[a, j], peer))
        for cp in copies:
            cp.wait()

    return pl.pallas_call(
        body, name=name, in_specs=[ANY] * n, out_specs=[ANY] * n,
        out_shape=[jax.ShapeDtypeStruct((N_CHIPS,) + s.shape, s.dtype) for s in shards],
        scratch_shapes=[pltpu.SemaphoreType.DMA((n, 4)), pltpu.SemaphoreType.DMA((n, 4))])(*shards)


def gather_by_halves_job(shards):
    n = len(shards)

    def fn(ins, outs, sems, phase):
        send_sems, recv_sems = sems
        x, y, c, chips = _place()
        k = 2 * x + y
        sibling = (x, y, 1 - c)
        begin = phase != "finish"
        waits, arrivals = [], []
        for a in range(n):
            half = ins[a].shape[0] // 2
            waits.append(_send(ins[a], outs[a].at[k], send_sems.at[a, 6], recv_sems.at[a, 6], sibling, begin))
            mine = pl.ds(c * half, half)
            for j, (px, py) in enumerate(chips):
                arrivals.append(_send(ins[a].at[mine, :], outs[a].at[k, mine, :], send_sems.at[a, j],
                                      recv_sems.at[a, j], (px, py, c), begin))
        if phase == "start":
            return
        i = 0
        for a in range(n):
            half = ins[a].shape[0] // 2
            mine = pl.ds(c * half, half)
            for j, (px, py) in enumerate(chips):
                arrivals[i].wait_recv()
                landed = outs[a].at[2 * px + py, mine, :]
                waits.append(_send(landed, landed, send_sems.at[a, 3 + j], recv_sems.at[a, 3 + j], sibling))
                i += 1
        for cp in arrivals:
            cp.wait_send()
        for cp in waits:
            cp.wait()

    return Job(shards, [jax.ShapeDtypeStruct((N_CHIPS,) + s.shape, s.dtype) for s in shards],
               [pltpu.SemaphoreType.DMA((n, 7)), pltpu.SemaphoreType.DMA((n, 7))], fn)


def gather_halves_job(shards):
    n = len(shards)

    def fn(ins, outs, sems, phase):
        send_sems, recv_sems = sems
        x, y, c, chips = _place()
        k = 2 * x + y
        copies = []
        for a in range(n):
            half = ins[a].shape[0] // 2
            mine = pl.ds(c * half, half)
            copies.append(_send(ins[a], outs[a].at[k], send_sems.at[a, 3], recv_sems.at[a, 3], (x, y, 1 - c),
                                phase != "finish"))
            for j, (px, py) in enumerate(chips):
                copies.append(_send(ins[a].at[mine, :], outs[a].at[k, mine, :], send_sems.at[a, j],
                                    recv_sems.at[a, j], (px, py, c), phase != "finish"))
        if phase != "start":
            for cp in copies:
                cp.wait()

    return Job(shards, [jax.ShapeDtypeStruct((N_CHIPS,) + s.shape, s.dtype) for s in shards],
               [pltpu.SemaphoreType.DMA((n, 4)), pltpu.SemaphoreType.DMA((n, 4))], fn)


def forward_halves_job(gathered):
    n = len(gathered)

    def fn(ins, outs, sems, phase):
        send_sems, recv_sems = sems
        x, y, c, chips = _place()
        copies = []
        for a in range(n):
            half = outs[a].shape[1] // 2
            for j, (px, py) in enumerate(chips):
                landed = outs[a].at[2 * px + py, pl.ds(c * half, half), :]
                copies.append(_send(landed, landed, send_sems.at[a, j], recv_sems.at[a, j], (x, y, 1 - c),
                                    phase != "finish"))
        if phase != "start":
            for cp in copies:
                cp.wait()

    return Job(gathered, [jax.ShapeDtypeStruct(g.shape, g.dtype) for g in gathered],
               [pltpu.SemaphoreType.DMA((n, 3)), pltpu.SemaphoreType.DMA((n, 3))], fn, in_place=True)


def swap_job(grads):
    n = len(grads)

    def fn(ins, outs, sems, phase):
        send_sems, recv_sems = sems
        x, y, c, _ = _place()
        copies = []
        for a in range(n):
            half = ins[a].shape[1] // 2
            copies.append(_send(ins[a].at[:, pl.ds((1 - c) * half, half), :], outs[a], send_sems.at[a],
                                recv_sems.at[a], (x, y, 1 - c), phase != "finish"))
        if phase != "start":
            for cp in copies:
                cp.wait()

    return Job(grads, [jax.ShapeDtypeStruct((g.shape[0], g.shape[1] // 2, g.shape[2]), g.dtype) for g in grads],
               [pltpu.SemaphoreType.DMA((n,)), pltpu.SemaphoreType.DMA((n,))], fn)


def _chip_slot(s, k):
    rel = s ^ k
    return (rel >> 1) | ((rel & 1) << 1)


def sum_cores(g, theirs, ck, out_dtype, *, name):
    n, r, cols = g.shape
    half = r // 2
    tr = _tile(half, max(16, (1 << 19) // cols), 16)
    nb = half // tr

    def body(ck_ref, g_ref, t_ref, o_ref):
        o_ref[...] = (g_ref[...] + t_ref[...]).astype(o_ref.dtype)

    return pl.pallas_call(
        body, name=name,
        grid_spec=pltpu.PrefetchScalarGridSpec(
            num_scalar_prefetch=1, grid=(n, nb),
            in_specs=[pl.BlockSpec((None, tr, cols), lambda s, i, ck_ref: (s, ck_ref[0] * nb + i, 0)),
                      pl.BlockSpec((None, tr, cols), lambda s, i, ck_ref: (s, i, 0))],
            out_specs=pl.BlockSpec((None, tr, cols), lambda s, i, ck_ref: (_chip_slot(s, ck_ref[1]), i, 0))),
        out_shape=jax.ShapeDtypeStruct((n, half, cols), out_dtype),
        compiler_params=_params("parallel", "parallel"))(ck, g, theirs)


def scatter_job(parts):
    n = len(parts)

    def fn(ins, outs, sems, phase):
        send_sems, recv_sems = sems
        x, y, c, chips = _place()
        copies = []
        for a in range(n):
            for j, (px, py) in enumerate(chips):
                copies.append(_send(ins[a].at[1 + j], outs[a].at[j], send_sems.at[a, j], recv_sems.at[a, j],
                                    (px, py, c), phase != "finish"))
        if phase != "start":
            for cp in copies:
                cp.wait()

    return Job(parts, [jax.ShapeDtypeStruct((3,) + p.shape[1:], p.dtype) for p in parts],
               [pltpu.SemaphoreType.DMA((n, 3)), pltpu.SemaphoreType.DMA((n, 3))], fn)


def sum_chips(own, others, ck, *, name):
    _, r, cols = own.shape
    tr = _tile(r, max(16, (1 << 19) // cols), 16)

    def body(ck_ref, a_ref, b0_ref, b1_ref, b2_ref, o_ref):
        o_ref[...] = (a_ref[...].astype(F32) + b0_ref[...].astype(F32) + b1_ref[...].astype(F32)
                      + b2_ref[...].astype(F32))

    other = lambda j: pl.BlockSpec((None, tr, cols), lambda i, ck_ref: (j, i, 0))
    return pl.pallas_call(
        body, name=name,
        grid_spec=pltpu.PrefetchScalarGridSpec(
            num_scalar_prefetch=1, grid=(r // tr,),
            in_specs=[pl.BlockSpec((None, tr, cols), lambda i, ck_ref: (0, i, 0)), other(0), other(1), other(2)],
            out_specs=pl.BlockSpec((None, tr, cols), lambda i, ck_ref: (ck_ref[0], i, 0))),
        out_shape=jax.ShapeDtypeStruct((2, r, cols), F32),
        compiler_params=_params("parallel"))(ck, own, others, others, others)


def share_job(joined):
    n = len(joined)

    def fn(ins, outs, sems, phase):
        send_sems, recv_sems = sems
        x, y, c, _ = _place()
        copies = [_send(outs[a].at[c], outs[a].at[c], send_sems.at[a], recv_sems.at[a], (x, y, 1 - c),
                        phase != "finish") for a in range(n)]
        if phase != "start":
            for cp in copies:
                cp.wait()

    return Job(joined, [jax.ShapeDtypeStruct(j.shape, j.dtype) for j in joined],
               [pltpu.SemaphoreType.DMA((n,)), pltpu.SemaphoreType.DMA((n,))], fn, in_place=True)


def kernel(x, meta_tokens, norm0_g, l0_w_in, l0_lam_re, l0_lam_im, l0_log_dt, l0_b_re, l0_b_im, l0_c_re, l0_c_im, l0_d_skip, l0_w_glu, l0_b_glu, l0_w_out, norm1_g, l1_w_in, l1_conv_w, l1_conv_b, l1_w_out, norm2_g, l2_w_in, l2_w_grp, l2_b_grp, l2_scale, l2_w_out, norm3_g, l3_w_in, l3_lam_re, l3_lam_im, l3_log_dt, l3_b_re, l3_b_im, l3_c_re, l3_c_im, l3_d_skip, l3_w_glu, l3_b_glu, l3_w_out, final_g, loss_target, m_meta_tokens, m_norm0_g, m_l0_w_in, m_l0_lam_re, m_l0_lam_im, m_l0_log_dt, m_l0_b_re, m_l0_b_im, m_l0_c_re, m_l0_c_im, m_l0_d_skip, m_l0_w_glu, m_l0_b_glu, m_l0_w_out, m_norm1_g, m_l1_w_in, m_l1_conv_w, m_l1_conv_b, m_l1_w_out, m_norm2_g, m_l2_w_in, m_l2_w_grp, m_l2_b_grp, m_l2_scale, m_l2_w_out, m_norm3_g, m_l3_w_in, m_l3_lam_re, m_l3_lam_im, m_l3_log_dt, m_l3_b_re, m_l3_b_im, m_l3_c_re, m_l3_c_im, m_l3_d_skip, m_l3_w_glu, m_l3_b_glu, m_l3_w_out, m_final_g, v_meta_tokens, v_norm0_g, v_l0_w_in, v_l0_lam_re, v_l0_lam_im, v_l0_log_dt, v_l0_b_re, v_l0_b_im, v_l0_c_re, v_l0_c_im, v_l0_d_skip, v_l0_w_glu, v_l0_b_glu, v_l0_w_out, v_norm1_g, v_l1_w_in, v_l1_conv_w, v_l1_conv_b, v_l1_w_out, v_norm2_g, v_l2_w_in, v_l2_w_grp, v_l2_b_grp, v_l2_scale, v_l2_w_out, v_norm3_g, v_l3_w_in, v_l3_lam_re, v_l3_lam_im, v_l3_log_dt, v_l3_b_re, v_l3_b_im, v_l3_c_re, v_l3_c_im, v_l3_d_skip, v_l3_w_glu, v_l3_b_glu, v_l3_w_out, v_final_g):
    given = dict(locals())
    w = {n: given[n] for n in WEIGHTS}
    mom_m = {n: given["m_" + n] for n in WEIGHTS}
    mom_v = {n: given["v_" + n] for n in WEIGHTS}
    seq = x.shape[1]
    real = N_META + seq
    lp = -(-real // SCAN_ROWS) * SCAN_ROWS
    chip = 2 * lax.axis_index("x") + lax.axis_index("y")
    row = lambda a: a.reshape(1, -1)

    shard_bf16 = {n: (w[n].reshape(-1, w[n].shape[-1]) if n == 'l2_w_grp' else w[n]).astype(BF16) for n in BIG}
    layer = lambda i: [n for n in BIG if n.startswith("l%d_" % i)]
    over_ici = lambda names: gather_halves_job([shard_bf16[n] for n in names])
    full = {}

    def landed(names, arrays):
        for n, arr in zip(names, arrays, strict=True):
            if n.endswith('w_in'):
                full[n] = arr
            elif n == 'l2_w_grp':
                full[n] = arr.reshape(N_CHIPS, len(POOL_WINDOWS), POOL_GROUP // N_CHIPS, POOL_GROUP)
            else:
                full[n] = arr.reshape(1, -1, arr.shape[-1])

    landed(['l0_w_in'], run_job(gather_by_halves_job([shard_bf16['l0_w_in']]), name="gather_l0_w_in"))
    full.update(zip(SMALL_SHARDED, gather_chips([w[n] for n in SMALL_SHARDED], name="gather_small_shards"),
                    strict=True))
    meta_full = full['meta_tokens'].transpose(1, 0, 2).reshape(N_META, D_MODEL)
    conv_w_full = full['l1_conv_w'].transpose(1, 0, 2).reshape(3, CONV_E)
    b_grp_full = full['l2_b_grp'].transpose(1, 0, 2).reshape(len(POOL_WINDOWS), POOL_GROUP)

    h0 = jnp.concatenate([meta_full, x[0], jnp.zeros((lp - real, D_MODEL), F32)], axis=0)
    target = jnp.concatenate([jnp.zeros((N_META, D_MODEL), F32), loss_target[0],
                              jnp.zeros((lp - real, D_MODEL), F32)], axis=0)
    grads = {}
    saved = {}
    gip = lambda a: jnp.swapaxes(a, 1, 2)

    def s5_layer_fwd(i, h, first=False):
        pre = "l%d_" % i
        prm = [w[pre + 'lam_re'], w[pre + 'lam_im'], w[pre + 'log_dt'].reshape(-1, 1), gip(w[pre + 'b_re']),
               gip(w[pre + 'b_im']), w[pre + 'c_re'], w[pre + 'c_im']]
        a_re, a_im, bblk, cblk = s5_prep(prm, name=pre + "prep")
        tables = (a_re.reshape(1, -1), a_im.reshape(1, -1), bblk, cblk)
        ar, ai = tables[:2]
        gain = row(w["norm%d_g" % i])
        rest = [pre + 'w_glu', pre + 'w_out']
        if not first:
            uz, n = norm_mm_nn(h, gain, full[pre + 'w_in'], name=pre + "in")
            y0, yg, sr, si = s5_fwd(uz, bblk, cblk, ar, ai, row(w[pre + 'd_skip']), name=pre + "scan")
            q, y3 = glu_gate(yg, full[pre + 'w_glu'], row(w[pre + 'b_glu']), y0, uz, name=pre + "glu")
        else:
            (uz, n), halves = norm_mm_nn(h, gain, full[pre + 'w_in'], name=pre + "in", job=over_ici(rest))
            (y0, yg, sr, si), arrived = s5_fwd(
                uz, bblk, cblk, ar, ai, row(w[pre + 'd_skip']), name=pre + "scan",
                job=Job.both(over_ici(layer(1)), forward_halves_job(halves)))
            landed(rest, arrived[len(layer(1)):])
            (q, y3), whole = glu_gate(yg, full[pre + 'w_glu'], row(w[pre + 'b_glu']), y0, uz, name=pre + "glu",
                                      job=forward_halves_job(arrived[:len(layer(1))]))
            landed(layer(1), whole)
        h_new = mm_nn(y3, full[pre + 'w_out'], add=h, name=pre + "out")
        saved[i] = dict(h=h, n=n, uz=uz, y0=y0, sr=sr, si=si, yg=yg, q=q, y3=y3, tables=tables, prm=prm)
        return h_new

    def s5_layer_bwd(i, dh, carry=None):
        pre = "l%d_" % i
        s = saved[i]
        ar, ai, bblk, cblk = s['tables']
        grads[pre + 'w_out'] = mm_tn(s['y3'], dh, 1, name=pre + "d_w_out")[0]
        dy0_direct, dq, dp, db_glu = s5_gate_bwd(dh, full[pre + 'w_out'], s['y0'], s['q'], s['uz'],
                                                 name=pre + "d_y3")
        grads[pre + 'b_glu'] = db_glu
        grads[pre + 'w_glu'] = mm_tn(s['yg'], dq, 1, name=pre + "d_w_glu")[0]
        early = [] if carry is None else [pre + 'w_out', pre + 'w_glu']
        if carry is None:
            dyg = mm_nt(dq, full[pre + 'w_glu'], name=pre + "d_yg")
            res = s5_bwd(dy0_direct, dyg, s['y0'], s['uz'], s['sr'], s['si'], bblk, cblk, ar, ai,
                         row(w[pre + 'd_skip']), dp, name=pre + "d_scan")
        else:
            pieces = [chip_major(n) for n in early]
            dyg, theirs = mm_nt(dq, full[pre + 'w_glu'], name=pre + "d_yg", job=swap_job(pieces))
            sum_over_cores(early, pieces, theirs)
            res, arrived = s5_bwd(dy0_direct, dyg, s['y0'], s['uz'], s['sr'], s['si'], bblk, cblk, ar, ai,
                                  row(w[pre + 'd_skip']), dp, name=pre + "d_scan",
                                  job=scatter_job(scatter_of(carry) + [pair[n] for n in early]))
            from_chips.update(zip(layer(carry) + early, arrived, strict=True))
        dp, dbblk, dcblk, dar, dai, dd = res
        grads[pre + 'd_skip'] = dd
        cots = (dar.reshape(S5_GROUPS, S5_STATE), dai.reshape(S5_GROUPS, S5_STATE), dbblk, dcblk)
        for key, val in zip(('lam_re', 'lam_im', 'log_dt', 'b_re', 'b_im', 'c_re', 'c_im'),
                            s5_prep_bwd(s['prm'], cots, name=pre + "d_prep"), strict=True):
            grads[pre + key] = val.reshape(-1) if key == 'log_dt' else val
        if carry is None:
            grads[pre + 'w_in'] = mm_tn(s['n'], dp, N_CHIPS, name=pre + "d_w_in")
        else:
            later = [n for n in BIG if not n.startswith(pre)] + early
            sums = [sum_chips(pair[n], from_chips[n], ck, name="sum_chips_" + n) for n in later]
            grads[pre + 'w_in'], shared = mm_tn(s['n'], dp, N_CHIPS, name=pre + "d_w_in", job=share_job(sums))
            joined.update(zip(later, shared, strict=True))
        names = [n for n in layer(i) if n not in early]
        pieces = [chip_major(n) for n in names]
        (dh_in, dg), theirs = mm_nt_norm_bwd(dp, full[pre + 'w_in'], s['h'], row(w["norm%d_g" % i]), dh,
                                             name=pre + "d_n", job=swap_job(pieces))
        sum_over_cores(names, pieces, theirs)
        grads["norm%d_g" % i] = dg
        return dh_in

    h1 = s5_layer_fwd(0, h0, first=True)

    (p1, n1), halves = norm_mm_nn(h1, row(w['norm1_g']), full['l1_w_in'], name="l1_in", job=over_ici(layer(2)))
    y3_1 = conv_fwd(p1, conv_w_full, row(w['l1_conv_b']), name="l1_conv")
    h2, whole = mm_nn(y3_1, full['l1_w_out'], add=h1, name="l1_out", job=forward_halves_job(halves))
    landed(layer(2), whole)

    (p2, n2), halves = norm_mm_nn(h2, row(w['norm2_g']), full['l2_w_in'], name="l2_in", job=over_ici(layer(3)))
    mixed = pool_fwd(p2, name="l2_pool")
    o2, y3_2 = grp_nn_gate(mixed, full['l2_w_grp'], b_grp_full.reshape(1, -1), p2, row(w['l2_scale']), name="l2_grp")
    h3, whole = mm_nn(y3_2, full['l2_w_out'], add=h2, name="l2_out", job=forward_halves_job(halves))
    landed(layer(3), whole)

    h4 = s5_layer_fwd(3, h3)

    dh, grads['final_g'], sq = final_loss(h4, target, row(w['final_g']), N_META, real, name="loss")
    loss = lax.psum(0.5 / D_MODEL * jnp.sum(sq), ("x", "y", "c"))

    ck = jnp.stack([lax.axis_index("c"), chip]).astype(jnp.int32)
    pair, from_chips, joined = {}, {}, {}

    def chip_major(n):
        g = grads[n]
        if n.endswith('w_in'):
            return g
        if n == 'l2_w_grp':
            return g.reshape(N_CHIPS, -1, POOL_GROUP)
        return g.reshape(N_CHIPS, -1, g.shape[-1])

    def core_parts(i):
        return layer(i), [chip_major(n) for n in layer(i)]

    def sum_over_cores(names, pieces, theirs):
        for n, g, t in zip(names, pieces, theirs, strict=True):
            pair[n] = sum_cores(g, t, ck, F32 if n == 'small' else BF16, name="sum_cores_" + n)

    scatter_of = lambda i: [pair[n] for n in layer(i)]

    def scattered(i, arrays):
        from_chips.update(zip(layer(i), arrays, strict=True))

    dh = s5_layer_bwd(3, dh)

    grads['l2_w_out'] = mm_tn(y3_2, dh, 1, name="l2_d_w_out")[0]
    do2, dp2, dscale, dbgrp = pool_gate_bwd(dh, full['l2_w_out'], o2, p2, row(w['l2_scale']), name="l2_d_y3")
    grads['l2_scale'] = dscale
    grads['l2_b_grp'] = dbgrp
    grads['l2_w_grp'] = grp_tn(mixed, do2, name="l2_d_w_grp")
    dmix = grp_nt(do2, full['l2_w_grp'], name="l2_d_mixed")
    dp2 = pool_bwd(dmix, dp2, name="l2_d_pool")
    grads['l2_w_in'], arrived = mm_tn(n2, dp2, N_CHIPS, name="l2_d_w_in", job=scatter_job(scatter_of(3)))
    scattered(3, arrived)
    names, pieces = core_parts(2)
    (dh, dg), theirs = mm_nt_norm_bwd(dp2, full['l2_w_in'], h2, row(w['norm2_g']), dh, name="l2_d_n",
                                      job=swap_job(pieces))
    sum_over_cores(names, pieces, theirs)
    grads['norm2_g'] = dg

    grads['l1_w_out'] = mm_tn(y3_1, dh, 1, name="l1_d_w_out")[0]
    dp1, dcw, dcb = conv_bwd(dh, full['l1_w_out'], p1, conv_w_full, row(w['l1_conv_b']), name="l1_d_conv")
    grads['l1_conv_w'] = dcw
    grads['l1_conv_b'] = dcb
    grads['l1_w_in'], arrived = mm_tn(n1, dp1, N_CHIPS, name="l1_d_w_in", job=scatter_job(scatter_of(2)))
    scattered(2, arrived)
    names, pieces = core_parts(1)
    (dh, dg), theirs = mm_nt_norm_bwd(dp1, full['l1_w_in'], h1, row(w['norm1_g']), dh, name="l1_d_n",
                                      job=swap_job(pieces))
    sum_over_cores(names, pieces, theirs)
    grads['norm1_g'] = dg

    dh = s5_layer_bwd(0, dh, carry=1)
    grad_x = dh[N_META:real][None]
    grads['meta_tokens'] = dh[:N_META]

    wide = [n for n in SMALL if w[n].ndim == 3]
    wire = [grads[n].reshape(S5_GROUPS, -1) if n in wide else grads[n] for n in SMALL]
    starts = dict(zip(SMALL, _wire_plan([a.shape for a in wire]), strict=True))
    small_local = pack_small(wire, name="pack_small")
    pieces = [small_local.reshape(N_CHIPS, -1, WIRE_WIDTH)]
    sum_over_cores(['small'], pieces, run_job(swap_job(pieces), name="reduce_cores_small"))

    out_g, out_d, out_m, out_v = {}, {}, {}, {}
    as2d = lambda a: a.reshape(-1, a.shape[-1])

    def update_big(names, name, job=None):
        for n in names:
            out_g[n] = joined[n].reshape(w[n].shape)
        res = adamw_many([(as2d(w[n]), as2d(out_g[n]), as2d(mom_m[n]), as2d(mom_v[n])) for n in names],
                         name=name, job=job)
        triples, carried = res if job is not None else (res, None)
        for n, (d, nm, nv) in zip(names, triples, strict=True):
            out_d[n], out_m[n], out_v[n] = d.reshape(w[n].shape), nm.reshape(w[n].shape), nv.reshape(w[n].shape)
        return carried

    rest = ['l0_w_in', 'small']
    later = [n for n in BIG if n not in rest]
    from_chips.update(zip(rest, update_big(later, "adamw_big", scatter_job([pair[n] for n in rest])), strict=True))
    sums = [sum_chips(pair[n], from_chips[n], ck, name="sum_chips_" + n) for n in rest]
    joined.update(zip(rest, run_job(share_job(sums), name="share_cores"), strict=True))
    update_big(['l0_w_in'], "adamw_l0_w_in")
    (small_all,) = run_job(gather_by_halves_job([joined['small'].reshape(-1, WIRE_WIDTH)]), name="gather_small")
    small_all = small_all.reshape(WIRE_ROWS, WIRE_WIDTH)

    half_w = WIRE_WIDTH // 2
    shard_cells = {
        'meta_tokens': lambda k: [((0, N_META), (k * 256, (k + 1) * 256), (slice(None), slice(None)))],
        'l1_conv_w': lambda k: [(2 * j + k // 2, ((k % 2) * half_w, (k % 2 + 1) * half_w), (j, slice(None)))
                                for j in range(3)],
        'l2_b_grp': lambda k: [(j // 2, ((j % 2) * half_w + k * 128, (j % 2) * half_w + (k + 1) * 128),
                                (j, slice(None))) for j in range(len(POOL_WINDOWS))],
    }
    plain = [n for n in SMALL if n not in wide]
    res = update_small(small_all, [starts[n] for n in plain], [(w[n], mom_m[n], mom_v[n]) for n in plain],
                       [shard_cells.get(n) for n in plain], chip, name="adamw_small")
    for n, (g, d, nm, nv) in zip(plain, res, strict=True):
        out_g[n], out_d[n], out_m[n], out_v[n] = g, d, nm, nv
    as_gip = lambda n, a: gip(a) if n.endswith(('b_re', 'b_im')) else a
    g_gip = {n: small_all[starts[n]:starts[n] + S5_GROUPS].reshape(S5_GROUPS, S5_GROUP, S5_STATE) for n in wide}
    res = adamw_many([(as_gip(n, w[n]), g_gip[n], as_gip(n, mom_m[n]), as_gip(n, mom_v[n])) for n in wide],
                     name="adamw_s5_tensors")
    for n, (d, nm, nv) in zip(wide, res, strict=True):
        out_g[n], out_d[n], out_m[n], out_v[n] = as_gip(n, g_gip[n]), as_gip(n, d), as_gip(n, nm), as_gip(n, nv)

    return (loss, grad_x, *[out_g[n] for n in WEIGHTS], *[out_d[n] for n in WEIGHTS],
            *[out_m[n] for n in WEIGHTS], *[out_v[n] for n in WEIGHTS])
```

```python
import functools

import jax
import jax.numpy as jnp
from jax import lax
from jax.experimental import pallas as pl
from jax.experimental.pallas import tpu as pltpu

F32, BF16 = jnp.float32, jnp.bfloat16
MESH = pl.DeviceIdType.MESH

D_MODEL = 1024
N_META = 16
EPS = 1e-6
S5_GROUPS, S5_GROUP, S5_STATE = 64, 16, 64
LANE_BLOCK = 128
S5_BLOCKS = D_MODEL // LANE_BLOCK
GROUPS_PER_BLOCK = LANE_BLOCK // S5_GROUP
STATE_BLOCK = GROUPS_PER_BLOCK * S5_STATE
SCAN_ROWS = 256
SCAN_CHUNK = 544
SUBLANES = 8
CONV_E = 2048
POOL_E = 2048
POOL_WINDOWS = (2, 4, 8, 16)
POOL_GROUP = 512
POOL_HALO = 16
N_CHIPS = 4

ADAM_LR, ADAM_B1, ADAM_B2, ADAM_EPS, ADAM_WD, ADAM_STEP = 0.001, 0.9, 0.999, 1e-08, 0.01, 10

VMEM_LIMIT = 48 * 1024 * 1024
TN_OPERAND_BYTES = 28 * 1024 * 1024

WEIGHTS = ['meta_tokens', 'norm0_g', 'l0_w_in', 'l0_lam_re', 'l0_lam_im', 'l0_log_dt', 'l0_b_re', 'l0_b_im',
           'l0_c_re', 'l0_c_im', 'l0_d_skip', 'l0_w_glu', 'l0_b_glu', 'l0_w_out', 'norm1_g', 'l1_w_in',
           'l1_conv_w', 'l1_conv_b', 'l1_w_out', 'norm2_g', 'l2_w_in', 'l2_w_grp', 'l2_b_grp', 'l2_scale',
           'l2_w_out', 'norm3_g', 'l3_w_in', 'l3_lam_re', 'l3_lam_im', 'l3_log_dt', 'l3_b_re', 'l3_b_im',
           'l3_c_re', 'l3_c_im', 'l3_d_skip', 'l3_w_glu', 'l3_b_glu', 'l3_w_out', 'final_g']
BIG = ['l0_w_in', 'l0_w_glu', 'l0_w_out', 'l1_w_in', 'l1_w_out', 'l2_w_in', 'l2_w_grp', 'l2_w_out',
       'l3_w_in', 'l3_w_glu', 'l3_w_out']
SMALL_SHARDED = ['meta_tokens', 'l1_conv_w', 'l2_b_grp']
SMALL = [n for n in WEIGHTS if n not in BIG]


def _params(*sem):
    return pltpu.CompilerParams(dimension_semantics=sem, vmem_limit_bytes=VMEM_LIMIT)


class Job:
    def __init__(self, arrays, out_shapes, sems, fn, in_place=False):
        self.arrays, self.out_shapes, self.sems, self.fn = list(arrays), list(out_shapes), list(sems), fn
        self.in_place = in_place
        assert len(self.arrays) == len(self.out_shapes)

    @staticmethod
    def both(first, second):
        na, ns = len(first.arrays), len(first.sems)

        def fn(ins, outs, sems, phase):
            first.fn(ins[:na], outs[:na], sems[:ns], phase)
            second.fn(ins[na:], outs[na:], sems[ns:], phase)

        job = Job(first.arrays + second.arrays, first.out_shapes + second.out_shapes, first.sems + second.sems, fn)
        job.in_place = [first.in_place] * na + [second.in_place] * len(second.arrays)
        return job

    def aliased(self):
        flags = self.in_place if isinstance(self.in_place, list) else [self.in_place] * len(self.arrays)
        return [a for a, flag in enumerate(flags) if flag]


def _call(body, *, name, grid, in_specs, out_specs, out_shape, args, semantics, scratch=(), aliases=None, job=None):
    if job is None:
        return pl.pallas_call(
            body, name=name, grid=grid, in_specs=list(in_specs), out_specs=list(out_specs),
            out_shape=list(out_shape), scratch_shapes=list(scratch), input_output_aliases=aliases or {},
            compiler_params=_params(*semantics))(*args)
    counts = [len(in_specs), len(job.arrays), len(out_specs), len(job.out_shapes), len(scratch), len(job.sems)]

    def hosted(*refs):
        parts, pos = [], 0
        for k in counts:
            parts.append(refs[pos:pos + k])
            pos += k
        ins, job_ins, outs, job_outs, scr, job_sems = parts
        steps = [pl.program_id(d) for d in range(len(grid))]
        first = functools.reduce(jnp.logical_and, [s == 0 for s in steps])
        last = functools.reduce(jnp.logical_and, [s == g - 1 for s, g in zip(steps, grid, strict=True)])

        @pl.when(first)
        def _():
            job.fn(job_ins, job_outs, job_sems, "start")

        body(*ins, *outs, *scr)

        @pl.when(last)
        def _():
            job.fn(job_ins, job_outs, job_sems, "finish")

    any_spec = pl.BlockSpec(memory_space=pl.ANY)
    aliases = dict(aliases or {})
    aliases.update({len(in_specs) + a: len(out_specs) + a for a in job.aliased()})
    res = pl.pallas_call(
        hosted, name=name, grid=grid, in_specs=list(in_specs) + [any_spec] * len(job.arrays),
        out_specs=list(out_specs) + [any_spec] * len(job.out_shapes),
        out_shape=list(out_shape) + job.out_shapes, scratch_shapes=list(scratch) + job.sems,
        input_output_aliases=aliases,
        compiler_params=_params(*["arbitrary"] * len(grid)))(*args, *job.arrays)
    return res[:len(out_specs)], res[len(out_specs):]


def _tile(n, cap, mult):
    best = None
    for t in range(mult, min(n, cap) + 1, mult):
        if n % t == 0:
            best = t
    assert best is not None, (n, cap, mult)
    return best


def _with_job(res, job):
    return res[0] if job is None else (res[0][0], res[1])


def mm_nn(a, b, *, name, bias=None, add=None, out_dtype=F32, job=None):
    m, k = a.shape
    s, _, ns = b.shape
    n = s * ns
    tm, tn = _tile(m, 1088, 16), _tile(ns, 1024, 128)
    per = ns // tn

    def body(*refs):
        a_ref, b_ref = refs[0], refs[1]
        o_ref = refs[-1]
        acc = jnp.dot(a_ref[...].astype(BF16), b_ref[...], preferred_element_type=F32)
        pos = 2
        if bias is not None:
            acc = acc + refs[pos][...]
            pos += 1
        if add is not None:
            acc = acc + refs[pos][...]
        o_ref[...] = acc.astype(o_ref.dtype)

    in_specs = [pl.BlockSpec((tm, k), lambda i, j: (i, 0)),
                pl.BlockSpec((None, k, tn), lambda i, j: (j // per, 0, j % per))]
    args = [a, b]
    if bias is not None:
        in_specs.append(pl.BlockSpec((1, tn), lambda i, j: (0, j)))
        args.append(bias)
    if add is not None:
        in_specs.append(pl.BlockSpec((tm, tn), lambda i, j: (i, j)))
        args.append(add)
    return _with_job(_call(
        body, name=name, grid=(m // tm, n // tn), in_specs=in_specs,
        out_specs=[pl.BlockSpec((tm, tn), lambda i, j: (i, j))],
        out_shape=[jax.ShapeDtypeStruct((m, n), out_dtype)], args=args,
        semantics=("parallel", "parallel"), job=job), job)


def norm_mm_nn(h, gain, b, *, name, job=None):
    m, k = h.shape
    s, _, ns = b.shape
    n = s * ns
    tm, tn = _tile(m, 1088, 16), _tile(ns, 1024, 128)
    per = ns // tn

    def body(h_ref, g_ref, b_ref, o_ref, n_ref):
        @pl.when(pl.program_id(1) == 0)
        def _():
            n_ref[...] = _rms(h_ref[...], g_ref[...]).astype(n_ref.dtype)

        o_ref[...] = jnp.dot(n_ref[...], b_ref[...], preferred_element_type=F32)

    res = _call(
        body, name=name, grid=(m // tm, n // tn),
        in_specs=[pl.BlockSpec((tm, k), lambda i, j: (i, 0)), pl.BlockSpec((1, k), lambda i, j: (0, 0)),
                  pl.BlockSpec((None, k, tn), lambda i, j: (j // per, 0, j % per))],
        out_specs=[pl.BlockSpec((tm, tn), lambda i, j: (i, j)), pl.BlockSpec((tm, k), lambda i, j: (i, 0))],
        out_shape=[jax.ShapeDtypeStruct((m, n), F32), jax.ShapeDtypeStruct((m, k), BF16)], args=(h, gain, b),
        semantics=("parallel", "arbitrary"), job=job)
    return (res[0], res[1]) if job is None else ((res[0][0], res[0][1]), res[1])


def glu_gate(yg, w_glu, b_glu, y0, uz, *, name, job=None):
    m, k = yg.shape
    n = w_glu.shape[2]
    tm = _tile(m, 1088, 16)

    def body(a_ref, w_ref, b_ref, y0_ref, z_ref, q_ref, o_ref):
        q = jnp.dot(a_ref[...], w_ref[...], preferred_element_type=F32) + b_ref[...]
        q_ref[...] = q
        o_ref[...] = _s5_gate(y0_ref[...], q, z_ref[...]).astype(o_ref.dtype)

    tile = pl.BlockSpec((tm, n), lambda i: (i, 0))
    res = _call(
        body, name=name, grid=(m // tm,),
        in_specs=[pl.BlockSpec((tm, k), lambda i: (i, 0)), pl.BlockSpec((None, k, n), lambda i: (0, 0, 0)),
                  pl.BlockSpec((1, n), lambda i: (0, 0)), tile, pl.BlockSpec((tm, n), lambda i: (i, 1))],
        out_specs=[tile, tile],
        out_shape=[jax.ShapeDtypeStruct((m, n), F32), jax.ShapeDtypeStruct((m, n), BF16)],
        args=(yg, w_glu, b_glu, y0, uz), semantics=("parallel",), job=job)
    return (res[0], res[1]) if job is None else ((res[0][0], res[0][1]), res[1])


def s5_gate_bwd(dh, w_out, y0, q, uz, *, name):
    m, d = dh.shape
    e = w_out.shape[1]
    tm = _tile(m, 544, 16)

    def body(dh_ref, w_ref, y0_ref, q_ref, z_ref, dy0_ref, dq_ref, dz_ref, db_ref):
        i = pl.program_id(0)
        dy3 = lax.dot_general(dh_ref[...].astype(BF16), w_ref[...], (((1,), (1,)), ((), ())),
                              preferred_element_type=F32)
        dy0, dq, dz = jax.vjp(_s5_gate, y0_ref[...], q_ref[...], z_ref[...])[1](dy3)
        dy0_ref[...] = dy0
        dq_ref[...] = dq.astype(dq_ref.dtype)
        dz_ref[...] = dz.astype(dz_ref.dtype)
        db = _colsum(dq)

        @pl.when(i == 0)
        def _():
            db_ref[...] = db

        @pl.when(i > 0)
        def _():
            db_ref[...] += db

    tile = pl.BlockSpec((tm, e), lambda i: (i, 0))
    right = pl.BlockSpec((tm, e), lambda i: (i, 1))
    return pl.pallas_call(
        body, name=name, grid=(m // tm,),
        in_specs=[pl.BlockSpec((tm, d), lambda i: (i, 0)), pl.BlockSpec((None, e, d), lambda i: (0, 0, 0)),
                  tile, tile, right],
        out_specs=[tile, tile, right, pl.BlockSpec((1, e), lambda i: (0, 0))],
        out_shape=[jax.ShapeDtypeStruct((m, e), F32), jax.ShapeDtypeStruct((m, e), BF16),
                   jax.ShapeDtypeStruct((m, 2 * e), BF16), jax.ShapeDtypeStruct((1, e), F32)],
        compiler_params=_params("arbitrary"))(dh, w_out, y0, q, uz)


def pool_gate_bwd(dh, w_out, o, p, scale, *, name):
    m, d = dh.shape
    e = w_out.shape[1]
    tm, te = _tile(m, 1088, 16), 1024
    nj = e // te

    def body(dh_ref, w_ref, o_ref, z_ref, s_ref, do_ref, dz_ref, ds_ref, db_ref):
        i = pl.program_id(1)
        dy3 = lax.dot_general(dh_ref[...].astype(BF16), w_ref[...], (((1,), (1,)), ((), ())),
                              preferred_element_type=F32)
        do, dz, ds = jax.vjp(_pool_gate, o_ref[...], z_ref[...], s_ref[...])[1](dy3)
        do_ref[...] = do.astype(do_ref.dtype)
        dz_ref[...] = dz.astype(dz_ref.dtype)
        db = _colsum(do)

        @pl.when(i == 0)
        def _():
            ds_ref[...] = ds
            db_ref[...] = db

        @pl.when(i > 0)
        def _():
            ds_ref[...] += ds
            db_ref[...] += db

    tile = pl.BlockSpec((tm, te), lambda j, i: (i, j))
    right = pl.BlockSpec((tm, te), lambda j, i: (i, nj + j))
    vec = pl.BlockSpec((1, te), lambda j, i: (0, j))
    return pl.pallas_call(
        body, name=name, grid=(nj, m // tm),
        in_specs=[pl.BlockSpec((tm, d), lambda j, i: (i, 0)), pl.BlockSpec((None, te, d), lambda j, i: (0, j, 0)),
                  tile, right, vec],
        out_specs=[tile, right, vec, vec],
        out_shape=[jax.ShapeDtypeStruct((m, e), BF16), jax.ShapeDtypeStruct((m, 2 * e), BF16),
                   jax.ShapeDtypeStruct((1, e), F32), jax.ShapeDtypeStruct((1, e), F32)],
        compiler_params=_params("parallel", "arbitrary"))(dh, w_out, o, p, scale)


def mm_nt_norm_bwd(g, w, h, gain, dh_out, *, name, job=None):
    m, kc = g.shape
    s, nout, kcs = w.shape
    assert s * kcs == kc
    tm, tk = _tile(m, 1088, 16), _tile(kcs, 1024, 128)
    per = kcs // tk
    nk = kc // tk

    def body(g_ref, w_ref, h_ref, gain_ref, dh_ref, o_ref, dg_ref):
        i, kk = pl.program_id(0), pl.program_id(1)
        part = lax.dot_general(g_ref[...].astype(BF16), w_ref[...], (((1,), (1,)), ((), ())),
                               preferred_element_type=F32)

        @pl.when(kk == 0)
        def _():
            o_ref[...] = part

        @pl.when(kk > 0)
        def _():
            o_ref[...] += part

        @pl.when(kk == nk - 1)
        def _():
            dh, dg = jax.vjp(_rms, h_ref[...], gain_ref[...])[1](o_ref[...])
            o_ref[...] = dh_ref[...] + dh

            @pl.when(i == 0)
            def _():
                dg_ref[...] = dg

            @pl.when(i > 0)
            def _():
                dg_ref[...] += dg

    row_tile = pl.BlockSpec((tm, nout), lambda i, kk: (i, 0))
    res = _call(
        body, name=name, grid=(m // tm, nk),
        in_specs=[pl.BlockSpec((tm, tk), lambda i, kk: (i, kk)),
                  pl.BlockSpec((None, nout, tk), lambda i, kk: (kk // per, 0, kk % per)),
                  row_tile, pl.BlockSpec((1, nout), lambda i, kk: (0, 0)), row_tile],
        out_specs=[row_tile, pl.BlockSpec((1, nout), lambda i, kk: (0, 0))],
        out_shape=[jax.ShapeDtypeStruct((m, nout), F32), jax.ShapeDtypeStruct((1, nout), F32)],
        args=(g, w, h, gain, dh_out), semantics=("arbitrary", "arbitrary"), job=job)
    return (res[0], res[1]) if job is None else ((res[0][0], res[0][1]), res[1])


def mm_nt(g, w, *, name, job=None):
    m, kc = g.shape
    s, nout, kcs = w.shape
    assert s * kcs == kc
    tm, tno, tk = _tile(m, 2176, 16), _tile(nout, 1024, 128), _tile(kcs, 1024, 128)
    per = kcs // tk

    def body(g_ref, w_ref, o_ref):
        kk = pl.program_id(2)
        part = lax.dot_general(g_ref[...].astype(BF16), w_ref[...], (((1,), (1,)), ((), ())),
                               preferred_element_type=F32)

        @pl.when(kk == 0)
        def _():
            o_ref[...] = part

        @pl.when(kk > 0)
        def _():
            o_ref[...] += part

    return _with_job(_call(
        body, name=name, grid=(m // tm, nout // tno, kc // tk),
        in_specs=[pl.BlockSpec((tm, tk), lambda i, j, kk: (i, kk)),
                  pl.BlockSpec((None, tno, tk), lambda i, j, kk: (kk // per, j, kk % per))],
        out_specs=[pl.BlockSpec((tm, tno), lambda i, j, kk: (i, j))],
        out_shape=[jax.ShapeDtypeStruct((m, nout), F32)], args=(g, w),
        semantics=("parallel", "parallel", "arbitrary"), job=job), job)


def mm_tn(a, g, shards, *, name, job=None):
    m, ka = a.shape
    n = g.shape[1]
    ns = n // shards
    tka, tn = _tile(ka, 1024, 128), _tile(ns, 512, 128)
    row_bytes = tka * jnp.dtype(a.dtype).itemsize + tn * jnp.dtype(g.dtype).itemsize
    tm = _tile(m, TN_OPERAND_BYTES // (2 * row_bytes), 16)
    per = ns // tn

    def body(a_ref, g_ref, o_ref):
        mm = pl.program_id(2)
        part = lax.dot_general(a_ref[...].astype(BF16), g_ref[...].astype(BF16), (((0,), (0,)), ((), ())),
                               preferred_element_type=F32)

        @pl.when(mm == 0)
        def _():
            o_ref[...] = part

        @pl.when(mm > 0)
        def _():
            o_ref[...] += part

    return _with_job(_call(
        body, name=name, grid=(ka // tka, n // tn, m // tm),
        in_specs=[pl.BlockSpec((tm, tka), lambda i, j, mm: (mm, i)),
                  pl.BlockSpec((tm, tn), lambda i, j, mm: (mm, j))],
        out_specs=[pl.BlockSpec((None, tka, tn), lambda i, j, mm: (j // per, i, j % per))],
        out_shape=[jax.ShapeDtypeStruct((shards, ka, ns), F32)], args=(a, g),
        semantics=("parallel", "parallel", "arbitrary"), job=job), job)


def grp_nn_gate(a, w, bias, p, scale, *, name):
    m = a.shape[0]
    tm = _tile(m, 1088, 16)
    ng = len(POOL_WINDOWS)

    def body(a_ref, w_ref, b_ref, z_ref, s_ref, o_ref, y_ref):
        wj = w_ref[...].reshape(POOL_GROUP, POOL_GROUP)
        o = jnp.dot(a_ref[...].astype(BF16), wj, preferred_element_type=F32) + b_ref[...]
        o_ref[...] = o
        y_ref[...] = _pool_gate(o, z_ref[...], s_ref[...]).astype(y_ref.dtype)

    tile = pl.BlockSpec((tm, POOL_GROUP), lambda i, j: (i, j))
    vec = pl.BlockSpec((1, POOL_GROUP), lambda i, j: (0, j))
    return pl.pallas_call(
        body, name=name, grid=(m // tm, ng),
        in_specs=[tile,
                  pl.BlockSpec((N_CHIPS, None, POOL_GROUP // N_CHIPS, POOL_GROUP), lambda i, j: (0, j, 0, 0)),
                  vec, pl.BlockSpec((tm, POOL_GROUP), lambda i, j: (i, ng + j)), vec],
        out_specs=[tile, tile],
        out_shape=[jax.ShapeDtypeStruct((m, ng * POOL_GROUP), F32), jax.ShapeDtypeStruct((m, ng * POOL_GROUP), BF16)],
        compiler_params=_params("parallel", "parallel"))(a, w, bias, p, scale)


def grp_nt(g, w, *, name):
    m = g.shape[0]
    tm = _tile(m, 1088, 16)
    ng = len(POOL_WINDOWS)

    def body(g_ref, w_ref, o_ref):
        wj = w_ref[...].reshape(POOL_GROUP, POOL_GROUP)
        o_ref[...] = lax.dot_general(g_ref[...].astype(BF16), wj, (((1,), (1,)), ((), ())),
                                     preferred_element_type=F32)

    return pl.pallas_call(
        body, name=name, grid=(m // tm, ng),
        in_specs=[pl.BlockSpec((tm, POOL_GROUP), lambda i, j: (i, j)),
                  pl.BlockSpec((N_CHIPS, None, POOL_GROUP // N_CHIPS, POOL_GROUP), lambda i, j: (0, j, 0, 0))],
        out_specs=pl.BlockSpec((tm, POOL_GROUP), lambda i, j: (i, j)),
        out_shape=jax.ShapeDtypeStruct((m, ng * POOL_GROUP), F32),
        compiler_params=_params("parallel", "parallel"))(g, w)


def grp_tn(a, g, *, name):
    m = a.shape[0]
    tm = _tile(m, 1088, 16)
    ng = len(POOL_WINDOWS)
    rows = POOL_GROUP // N_CHIPS

    def body(a_ref, g_ref, o_ref):
        mm = pl.program_id(1)
        part = lax.dot_general(a_ref[...].astype(BF16), g_ref[...].astype(BF16), (((0,), (0,)), ((), ())),
                               preferred_element_type=F32).reshape(N_CHIPS, rows, POOL_GROUP)

        @pl.when(mm == 0)
        def _():
            o_ref[...] = part

        @pl.when(mm > 0)
        def _():
            o_ref[...] += part

    return pl.pallas_call(
        body, name=name, grid=(ng, m // tm),
        in_specs=[pl.BlockSpec((tm, POOL_GROUP), lambda j, mm: (mm, j)),
                  pl.BlockSpec((tm, POOL_GROUP), lambda j, mm: (mm, j))],
        out_specs=pl.BlockSpec((N_CHIPS, None, rows, POOL_GROUP), lambda j, mm: (0, j, 0, 0)),
        out_shape=jax.ShapeDtypeStruct((N_CHIPS, ng, rows, POOL_GROUP), F32),
        compiler_params=_params("parallel", "arbitrary"))(a, g)


def final_loss(h, target, gain, first, last, *, name):
    m, d = h.shape
    tr = m // 16

    def body(h_ref, t_ref, g_ref, dh_ref, dg_ref, sq_ref):
        i = pl.program_id(0)
        y, vjp = jax.vjp(_rms, h_ref[...], g_ref[...])
        pos = i * tr + lax.broadcasted_iota(jnp.int32, (tr, 1), 0)
        diff = jnp.where((pos >= first) & (pos < last), y - t_ref[...], 0.0)
        dh_ref[...], dg = vjp(diff / d)
        sq = _colsum(diff * diff)

        @pl.when(i == 0)
        def _():
            dg_ref[...] = dg
            sq_ref[...] = sq

        @pl.when(i > 0)
        def _():
            dg_ref[...] += dg
            sq_ref[...] += sq

    tile, vec = pl.BlockSpec((tr, d), lambda i: (i, 0)), pl.BlockSpec((1, d), lambda i: (0, 0))
    return pl.pallas_call(
        body, name=name, grid=(m // tr,), in_specs=[tile, tile, vec], out_specs=[tile, vec, vec],
        out_shape=[jax.ShapeDtypeStruct((m, d), F32), jax.ShapeDtypeStruct((1, d), F32),
                   jax.ShapeDtypeStruct((1, d), F32)],
        compiler_params=_params("arbitrary"))(h, target, gain)


def _rms(h, g):
    return h * lax.rsqrt(jnp.mean(h * h, axis=-1, keepdims=True) + EPS) * g


def _colsum(v):
    return jnp.sum(v, axis=0, keepdims=True)


def _s5_gate(y0, q, z):
    yg = jax.nn.gelu(y0)
    return yg * jax.nn.sigmoid(q) * jax.nn.silu(z)


def _pool_gate(o, z, scale):
    return o * scale * jax.nn.silu(z)


def _shift_rows(v, d, down):
    t = v.shape[0]
    row = lax.broadcasted_iota(jnp.int32, v.shape, 0)
    if down:
        return jnp.where(row >= d, pltpu.roll(v, d, 0), 0.0)
    return jnp.where(row < t - d, pltpu.roll(v, t - d, 0), 0.0)


def _cmul(ar, ai, br, bi):
    return ar * br - ai * bi, ar * bi + ai * br


def _scan(xr, xi, ar, ai, cr, ci, down):
    t, n = xr.shape
    nb = t // SUBLANES
    pw = [(ar, ai)]
    for _ in range(SUBLANES - 1):
        pw.append(_cmul(*pw[-1], ar, ai))
    sub = lax.broadcasted_iota(jnp.int32, (SUBLANES, n), 0)

    def table(exps):
        tr, ti = jnp.zeros((SUBLANES, n), F32), jnp.zeros((SUBLANES, n), F32)
        for r, e in enumerate(exps):
            if e:
                tr, ti = jnp.where(sub == r, pw[e - 1][0], tr), jnp.where(sub == r, pw[e - 1][1], ti)
        return tr[None], ti[None]

    sr, si = xr.reshape(nb, SUBLANES, n), xi.reshape(nb, SUBLANES, n)
    d = 1
    while d < SUBLANES:
        mr, mi = table([d if (r >= d if down else r < SUBLANES - d) else 0 for r in range(SUBLANES)])
        turn = d if down else SUBLANES - d
        hr, hi = pltpu.roll(sr, turn, 1), pltpu.roll(si, turn, 1)
        sr, si = sr + mr * hr - mi * hi, si + mr * hi + mi * hr
        d *= 2
    edge = SUBLANES - 1 if down else 0
    qr, qi = table([r + 1 if down else SUBLANES - r for r in range(SUBLANES)])
    qr, qi = qr[0], qi[0]
    in_r, in_i = jnp.broadcast_to(cr, (SUBLANES, n)), jnp.broadcast_to(ci, (SUBLANES, n))
    out_r, out_i = [None] * nb, [None] * nb
    for b in (range(nb) if down else reversed(range(nb))):
        out_r[b] = sr[b] + qr * in_r - qi * in_i
        out_i[b] = si[b] + qr * in_i + qi * in_r
        in_r = jnp.broadcast_to(out_r[b][edge:edge + 1, :], (SUBLANES, n))
        in_i = jnp.broadcast_to(out_i[b][edge:edge + 1, :], (SUBLANES, n))
    return jnp.concatenate(out_r, axis=0), jnp.concatenate(out_i, axis=0), in_r[0:1, :], in_i[0:1, :]


def s5_fwd(uz, bblk, cblk, ar, ai, dskip, *, name, job=None):
    lp = uz.shape[0]
    t = _tile(lp, SCAN_CHUNK, 16)
    nst = S5_GROUPS * S5_STATE

    def body(u_ref, b_ref, c_ref, ar_ref, ai_ref, d_ref, y_ref, yg_ref, sr_ref, si_ref, cr, ci):
        step = pl.program_id(1)

        @pl.when(step == 0)
        def _():
            cr[...] = jnp.zeros_like(cr)
            ci[...] = jnp.zeros_like(ci)

        u = u_ref[...]
        a_r, a_i = ar_ref[...], ai_ref[...]
        x = jnp.dot(u.astype(BF16), b_ref[...].astype(BF16), preferred_element_type=F32)
        sr, si, cr[...], ci[...] = _scan(x[:, :STATE_BLOCK], x[:, STATE_BLOCK:], a_r, a_i, cr[...], ci[...], True)
        sr_ref[...] = sr
        si_ref[...] = si
        s = jnp.concatenate([sr, si], axis=1).astype(BF16)
        y0 = lax.dot_general(s, c_ref[...].astype(BF16), (((1,), (1,)), ((), ())),
                             preferred_element_type=F32) + d_ref[...] * u
        y_ref[...] = y0
        yg_ref[...] = jax.nn.gelu(y0).astype(yg_ref.dtype)

    return _call(
        body, name=name, grid=(S5_BLOCKS, lp // t),
        in_specs=[pl.BlockSpec((t, LANE_BLOCK), lambda b, c: (c, b)),
                  pl.BlockSpec((None, LANE_BLOCK, 2 * STATE_BLOCK), lambda b, c: (b, 0, 0)),
                  pl.BlockSpec((None, LANE_BLOCK, 2 * STATE_BLOCK), lambda b, c: (b, 0, 0)),
                  pl.BlockSpec((1, STATE_BLOCK), lambda b, c: (0, b)),
                  pl.BlockSpec((1, STATE_BLOCK), lambda b, c: (0, b)),
                  pl.BlockSpec((1, LANE_BLOCK), lambda b, c: (0, b))],
        out_specs=[pl.BlockSpec((t, LANE_BLOCK), lambda b, c: (c, b)),
                   pl.BlockSpec((t, LANE_BLOCK), lambda b, c: (c, b)),
                   pl.BlockSpec((t, STATE_BLOCK), lambda b, c: (c, b)),
                   pl.BlockSpec((t, STATE_BLOCK), lambda b, c: (c, b))],
        out_shape=[jax.ShapeDtypeStruct((lp, D_MODEL), F32), jax.ShapeDtypeStruct((lp, D_MODEL), BF16),
                   jax.ShapeDtypeStruct((lp, nst), F32), jax.ShapeDtypeStruct((lp, nst), F32)],
        scratch=[pltpu.VMEM((1, STATE_BLOCK), F32), pltpu.VMEM((1, STATE_BLOCK), F32)],
        args=(uz, bblk, cblk, ar, ai, dskip), semantics=("parallel", "arbitrary"), job=job)


def s5_bwd(dy_direct, dyg, y0, uz, sr, si, bblk, cblk, ar, ai, dskip, dp, *, name, job=None):
    lp = uz.shape[0]
    t = _tile(lp, SCAN_CHUNK, 16)
    nch = lp // t
    nst = S5_GROUPS * S5_STATE

    def body(dyd_ref, dyg_ref, y0_ref, u_ref, sr_ref, si_ref, b_ref, c_ref, ar_ref, ai_ref, d_ref, _, du_ref,
             db_ref, dc_ref, dar_ref, dai_ref, dd_ref, cr, ci):
        step = pl.program_id(1)

        @pl.when(step == 0)
        def _():
            cr[...] = jnp.zeros_like(cr)
            ci[...] = jnp.zeros_like(ci)

        dy = dyd_ref[...] + jax.vjp(jax.nn.gelu, y0_ref[...])[1](dyg_ref[...])[0]
        u = u_ref[...]
        dyb, ub = dy.astype(BF16), u.astype(BF16)
        a_r, a_i = ar_ref[...], -ai_ref[...]
        g = jnp.dot(dyb, c_ref[...].astype(BF16), preferred_element_type=F32)
        gr, gi = g[:, :STATE_BLOCK], g[:, STATE_BLOCK:]
        nxt_r, nxt_i = cr[...], ci[...]
        last = lax.broadcasted_iota(jnp.int32, gr.shape, 0) == t - 1
        lr, li, cr[...], ci[...] = _scan(gr, gi, a_r, a_i, nxt_r, nxt_i, False)
        nr = _shift_rows(lr, 1, False) + jnp.where(last, nxt_r, 0.0)
        ni = _shift_rows(li, 1, False) + jnp.where(last, nxt_i, 0.0)
        s_r, s_i = sr_ref[...], si_ref[...]
        dar = _colsum(s_r * nr + s_i * ni)
        dai = _colsum(s_r * ni - s_i * nr)
        lam = jnp.concatenate([lr, li], axis=1).astype(BF16)
        sb = jnp.concatenate([s_r, s_i], axis=1).astype(BF16)
        dc = lax.dot_general(dyb, sb, (((0,), (0,)), ((), ())), preferred_element_type=F32)
        db = lax.dot_general(ub, lam, (((0,), (0,)), ((), ())), preferred_element_type=F32)
        du_ref[...] = (lax.dot_general(lam, b_ref[...].astype(BF16), (((1,), (1,)), ((), ())),
                                       preferred_element_type=F32) + d_ref[...] * dy).astype(du_ref.dtype)
        dd = _colsum(dy * u)

        @pl.when(step == 0)
        def _():
            db_ref[...] = db
            dc_ref[...] = dc
            dar_ref[...] = dar
            dai_ref[...] = dai
            dd_ref[...] = dd

        @pl.when(step > 0)
        def _():
            db_ref[...] += db
            dc_ref[...] += dc
            dar_ref[...] += dar
            dai_ref[...] += dai
            dd_ref[...] += dd

    rev = lambda b, c: (nch - 1 - c, b)
    return _call(
        body, name=name, grid=(S5_BLOCKS, nch),
        in_specs=[pl.BlockSpec((t, LANE_BLOCK), rev), pl.BlockSpec((t, LANE_BLOCK), rev),
                  pl.BlockSpec((t, LANE_BLOCK), rev), pl.BlockSpec((t, LANE_BLOCK), rev),
                  pl.BlockSpec((t, STATE_BLOCK), rev), pl.BlockSpec((t, STATE_BLOCK), rev),
                  pl.BlockSpec((None, LANE_BLOCK, 2 * STATE_BLOCK), lambda b, c: (b, 0, 0)),
                  pl.BlockSpec((None, LANE_BLOCK, 2 * STATE_BLOCK), lambda b, c: (b, 0, 0)),
                  pl.BlockSpec((1, STATE_BLOCK), lambda b, c: (0, b)),
                  pl.BlockSpec((1, STATE_BLOCK), lambda b, c: (0, b)),
                  pl.BlockSpec((1, LANE_BLOCK), lambda b, c: (0, b)),
                  pl.BlockSpec(memory_space=pl.ANY)],
        out_specs=[pl.BlockSpec((t, LANE_BLOCK), rev),
                   pl.BlockSpec((None, LANE_BLOCK, 2 * STATE_BLOCK), lambda b, c: (b, 0, 0)),
                   pl.BlockSpec((None, LANE_BLOCK, 2 * STATE_BLOCK), lambda b, c: (b, 0, 0)),
                   pl.BlockSpec((1, STATE_BLOCK), lambda b, c: (0, b)),
                   pl.BlockSpec((1, STATE_BLOCK), lambda b, c: (0, b)),
                   pl.BlockSpec((1, LANE_BLOCK), lambda b, c: (0, b))],
        out_shape=[jax.ShapeDtypeStruct(dp.shape, dp.dtype),
                   jax.ShapeDtypeStruct((S5_BLOCKS, LANE_BLOCK, 2 * STATE_BLOCK), F32),
                   jax.ShapeDtypeStruct((S5_BLOCKS, LANE_BLOCK, 2 * STATE_BLOCK), F32),
                   jax.ShapeDtypeStruct((1, nst), F32), jax.ShapeDtypeStruct((1, nst), F32),
                   jax.ShapeDtypeStruct((1, D_MODEL), F32)],
        scratch=[pltpu.VMEM((1, STATE_BLOCK), F32), pltpu.VMEM((1, STATE_BLOCK), F32)],
        aliases={11: 0}, args=(dy_direct, dyg, y0, uz, sr, si, bblk, cblk, ar, ai, dskip, dp),
        semantics=("parallel", "arbitrary"), job=job)


def _s5_prep(lam_re, lam_im, log_dt, b_re, b_im, c_re, c_im):
    dt = jnp.exp(log_dt)
    mag = jnp.exp(lam_re * dt)
    ar = mag * jnp.cos(lam_im * dt)
    ai = mag * jnp.sin(lam_im * dt)
    den = lam_re * lam_re + lam_im * lam_im
    kr = ((ar - 1.0) * lam_re + ai * lam_im) / den
    ki = (ai * lam_re - (ar - 1.0) * lam_im) / den
    bbr = kr[:, None, :] * b_re - ki[:, None, :] * b_im
    bbi = kr[:, None, :] * b_im + ki[:, None, :] * b_re
    rows = S5_GROUPS * S5_GROUP
    expand = (lax.broadcasted_iota(jnp.int32, (S5_STATE, STATE_BLOCK), 1) % S5_STATE
              == lax.broadcasted_iota(jnp.int32, (S5_STATE, STATE_BLOCK), 0)).astype(F32)
    own = ((lax.broadcasted_iota(jnp.int32, (rows, STATE_BLOCK), 0) // S5_GROUP) % GROUPS_PER_BLOCK
           == lax.broadcasted_iota(jnp.int32, (rows, STATE_BLOCK), 1) // S5_STATE).astype(F32)

    def blocks(v):
        wide = jnp.dot(v.reshape(rows, S5_STATE), expand, precision=lax.Precision.HIGHEST,
                       preferred_element_type=F32)
        return (wide * own).reshape(S5_BLOCKS, LANE_BLOCK, STATE_BLOCK)

    btab = jnp.concatenate([blocks(bbr), blocks(bbi)], axis=2)
    ctab = jnp.concatenate([blocks(c_re), -blocks(c_im)], axis=2)
    return ar, ai, btab, ctab


_S5_PREP_OUT = [(S5_GROUPS, S5_STATE), (S5_GROUPS, S5_STATE), (S5_BLOCKS, LANE_BLOCK, 2 * STATE_BLOCK),
                (S5_BLOCKS, LANE_BLOCK, 2 * STATE_BLOCK)]


def s5_prep(params, *, name):
    def body(*refs):
        for ref, val in zip(refs[7:], _s5_prep(*[r[...] for r in refs[:7]]), strict=True):
            ref[...] = val

    return pl.pallas_call(body, name=name, out_shape=[jax.ShapeDtypeStruct(s, F32) for s in _S5_PREP_OUT],
                          compiler_params=pltpu.CompilerParams(vmem_limit_bytes=VMEM_LIMIT))(*params)


def s5_prep_bwd(params, cotangents, *, name):
    def body(*refs):
        grads = jax.vjp(_s5_prep, *[r[...] for r in refs[:7]])[1](tuple(r[...] for r in refs[7:11]))
        for ref, val in zip(refs[11:], grads, strict=True):
            ref[...] = val

    return pl.pallas_call(body, name=name, out_shape=[jax.ShapeDtypeStruct(p.shape, F32) for p in params],
                          compiler_params=pltpu.CompilerParams(vmem_limit_bytes=VMEM_LIMIT))(*params, *cotangents)


def _silu_grad(z):
    s = jax.nn.sigmoid(z)
    return s * (1.0 + z * (1.0 - s))


def conv_fwd(p, conv_w, conv_b, *, name):
    lp = p.shape[0]
    tr = lp // 16
    e = CONV_E
    hb = tr // 8

    def body(p_ref, cgh_ref, vh_ref, w_ref, b_ref, o_ref):
        i = pl.program_id(0)
        bg, cg, v, z = (p_ref[:, k * e:(k + 1) * e] for k in range(4))
        hc = cg * v
        halo = jnp.where(i > 0, cgh_ref[...] * vh_ref[...], 0.0)
        ext = jnp.concatenate([halo, hc], axis=0)
        w = w_ref[...]
        conv = (w[0:1] * pltpu.roll(ext, 2, 0)[8:] + w[1:2] * pltpu.roll(ext, 1, 0)[8:] + w[2:3] * hc
                + b_ref[...])
        o_ref[...] = (bg * conv * jax.nn.silu(z)).astype(o_ref.dtype)

    prev = lambda col: (lambda i: (jnp.maximum(i * hb - 1, 0), col))
    return pl.pallas_call(
        body, name=name, grid=(lp // tr,),
        in_specs=[pl.BlockSpec((tr, 4 * e), lambda i: (i, 0)),
                  pl.BlockSpec((8, e), prev(1)), pl.BlockSpec((8, e), prev(2)),
                  pl.BlockSpec((3, e), lambda i: (0, 0)), pl.BlockSpec((1, e), lambda i: (0, 0))],
        out_specs=pl.BlockSpec((tr, e), lambda i: (i, 0)),
        out_shape=jax.ShapeDtypeStruct((lp, e), BF16),
        compiler_params=_params("parallel"))(p, p, p, conv_w, conv_b)


def conv_bwd(dh, w_out, p, conv_w, conv_b, *, name):
    lp = p.shape[0]
    tr = lp // 16
    e = CONV_E
    d = dh.shape[1]
    hb = tr // 8
    nt = lp // tr
    width = 512

    def body(dh_ref, p_ref, cgh_ref, vh_ref, dhn_ref, bgn_ref, zn_ref, wo_ref, w_ref, b_ref, dp_ref, dw_ref, db_ref):
        i = pl.program_id(0)
        rows, rows_n = dh_ref[...].astype(BF16), dhn_ref[...].astype(BF16)
        for c0 in range(0, e, width):
            cols = slice(c0, c0 + width)
            wo = wo_ref[cols, :]
            dy = lax.dot_general(rows, wo, (((1,), (1,)), ((), ())), preferred_element_type=F32)
            dy_n = lax.dot_general(rows_n, wo, (((1,), (1,)), ((), ())), preferred_element_type=F32)
            bg, cg, v, z = (p_ref[:, k * e + c0:k * e + c0 + width] for k in range(4))
            w = w_ref[:, cols]
            hc = cg * v
            halo = jnp.where(i > 0, cgh_ref[:, cols] * vh_ref[:, cols], 0.0)
            ext = jnp.concatenate([halo, hc], axis=0)
            hc2, hc1 = pltpu.roll(ext, 2, 0)[8:], pltpu.roll(ext, 1, 0)[8:]
            yc = w[0:1] * hc2 + w[1:2] * hc1 + w[2:3] * hc + b_ref[:, cols]
            sz = jax.nn.silu(z)
            dyc = dy * bg * sz
            dyc_n = jnp.where(i < nt - 1, dy_n * bgn_ref[:, cols] * jax.nn.silu(zn_ref[:, cols]), 0.0)
            ext_n = jnp.concatenate([dyc, dyc_n], axis=0)
            n_rows = tr + 8
            dhc = (w[2:3] * dyc + w[1:2] * pltpu.roll(ext_n, n_rows - 1, 0)[:tr]
                   + w[0:1] * pltpu.roll(ext_n, n_rows - 2, 0)[:tr])
            for k, val in enumerate((dy * yc * sz, dhc * v, dhc * cg, dy * bg * yc * _silu_grad(z))):
                dp_ref[:, k * e + c0:k * e + c0 + width] = val.astype(dp_ref.dtype)
            dw = jnp.concatenate([_colsum(dyc * hc2), _colsum(dyc * hc1), _colsum(dyc * hc)], axis=0)
            db = _colsum(dyc)

            @pl.when(i == 0)
            def _(cols=cols, dw=dw, db=db):
                dw_ref[:, cols] = dw
                db_ref[:, cols] = db

            @pl.when(i > 0)
            def _(cols=cols, dw=dw, db=db):
                dw_ref[:, cols] += dw
                db_ref[:, cols] += db

    prev = lambda col: (lambda i: (jnp.maximum(i * hb - 1, 0), col))
    nxt = lambda col: (lambda i: (jnp.minimum((i + 1) * hb, lp // 8 - 1), col))
    return pl.pallas_call(
        body, name=name, grid=(nt,),
        in_specs=[pl.BlockSpec((tr, d), lambda i: (i, 0)), pl.BlockSpec((tr, 4 * e), lambda i: (i, 0)),
                  pl.BlockSpec((8, e), prev(1)), pl.BlockSpec((8, e), prev(2)),
                  pl.BlockSpec((8, d), nxt(0)), pl.BlockSpec((8, e), nxt(0)), pl.BlockSpec((8, e), nxt(3)),
                  pl.BlockSpec((None, e, d), lambda i: (0, 0, 0)),
                  pl.BlockSpec((3, e), lambda i: (0, 0)), pl.BlockSpec((1, e), lambda i: (0, 0))],
        out_specs=[pl.BlockSpec((tr, 4 * e), lambda i: (i, 0)), pl.BlockSpec((3, e), lambda i: (0, 0)),
                   pl.BlockSpec((1, e), lambda i: (0, 0))],
        out_shape=[jax.ShapeDtypeStruct((lp, 4 * e), BF16), jax.ShapeDtypeStruct((3, e), F32),
                   jax.ShapeDtypeStruct((1, e), F32)],
        compiler_params=_params("arbitrary"))(dh, p, p, p, dh, p, p, w_out, conv_w, conv_b)


def _pool_counts(i, tr, rows, offset, w):
    t = (i * tr + offset + 1 + lax.broadcasted_iota(jnp.int32, (rows, 1), 0)).astype(F32)
    return jnp.minimum(t, float(w))


def pool_fwd(p, *, name):
    lp = p.shape[0]
    tr = lp // 16
    h = POOL_HALO
    hb = tr // h

    def body(u_ref, uh_ref, o_ref):
        i = pl.program_id(0)
        u = u_ref[...]
        ext = jnp.concatenate([jnp.where(i > 0, uh_ref[...], 0.0), u], axis=0)
        for k, w in enumerate(POOL_WINDOWS):
            cols = slice(k * POOL_GROUP, (k + 1) * POOL_GROUP)
            s = ext[:, cols]
            d = 1
            while d < w:
                s = s + pltpu.roll(s, d, 0)
                d *= 2
            o_ref[:, cols] = (s[h:] / _pool_counts(i, tr, tr, 0, w) - u[:, cols]).astype(o_ref.dtype)

    return pl.pallas_call(
        body, name=name, grid=(lp // tr,),
        in_specs=[pl.BlockSpec((tr, POOL_E), lambda i: (i, 0)),
                  pl.BlockSpec((h, POOL_E), lambda i: (jnp.maximum(i * hb - 1, 0), 0))],
        out_specs=pl.BlockSpec((tr, POOL_E), lambda i: (i, 0)),
        out_shape=jax.ShapeDtypeStruct((lp, POOL_E), BF16),
        compiler_params=_params("parallel"))(p, p)


def pool_bwd(dmix, dp, *, name):
    lp = dmix.shape[0]
    tr = lp // 16
    h = POOL_HALO
    hb = tr // h
    nt = lp // tr

    def body(dm_ref, dmn_ref, _, du_ref):
        i = pl.program_id(0)
        dm = dm_ref[...]
        dmn = jnp.where(i < nt - 1, dmn_ref[...], 0.0)
        n_rows = tr + h
        for k, w in enumerate(POOL_WINDOWS):
            cols = slice(k * POOL_GROUP, (k + 1) * POOL_GROUP)
            s = jnp.concatenate([dm[:, cols] / _pool_counts(i, tr, tr, 0, w),
                                 dmn[:, cols] / _pool_counts(i, tr, h, tr, w)], axis=0)
            d = 1
            while d < w:
                s = s + pltpu.roll(s, n_rows - d, 0)
                d *= 2
            du_ref[:, cols] = (s[:tr] - dm[:, cols]).astype(du_ref.dtype)

    return pl.pallas_call(
        body, name=name, grid=(nt,),
        in_specs=[pl.BlockSpec((tr, POOL_E), lambda i: (i, 0)),
                  pl.BlockSpec((h, POOL_E), lambda i: (jnp.minimum((i + 1) * hb, lp // h - 1), 0)),
                  pl.BlockSpec(memory_space=pl.ANY)],
        out_specs=pl.BlockSpec((tr, POOL_E), lambda i: (i, 0)),
        out_shape=jax.ShapeDtypeStruct(dp.shape, dp.dtype),
        input_output_aliases={2: 0},
        compiler_params=_params("parallel"))(dmix, dmix, dp)


def _adamw_math(w, g, m, v):
    nm = ADAM_B1 * m + (1.0 - ADAM_B1) * g
    nv = ADAM_B2 * v + (1.0 - ADAM_B2) * jnp.square(g)
    m_hat = nm / (1.0 - ADAM_B1 ** ADAM_STEP)
    v_hat = nv / (1.0 - ADAM_B2 ** ADAM_STEP)
    return -ADAM_LR * (m_hat / (jnp.sqrt(v_hat) + ADAM_EPS) + ADAM_WD * w), nm, nv


def adamw_many(quads, *, name, steps=8, job=None):
    n = len(quads)

    def spec(shape):
        return pl.BlockSpec((shape[0] // steps,) + shape[1:], lambda i, nd=len(shape): (i,) + (0,) * (nd - 1))

    def body(*refs):
        for q in range(n):
            w_ref, g_ref, m_ref, v_ref = refs[4 * q:4 * q + 4]
            d_ref, nm_ref, nv_ref = refs[4 * n + 3 * q:4 * n + 3 * q + 3]
            d_ref[...], nm_ref[...], nv_ref[...] = _adamw_math(w_ref[...], g_ref[...], m_ref[...], v_ref[...])

    res = _call(
        body, name=name, grid=(steps,), in_specs=[spec(quad[0].shape) for quad in quads for _ in range(4)],
        out_specs=[spec(quad[0].shape) for quad in quads for _ in range(3)],
        out_shape=[jax.ShapeDtypeStruct(quad[0].shape, F32) for quad in quads for _ in range(3)],
        args=[a for quad in quads for a in quad], semantics=("parallel",), job=job)
    outs = res if job is None else res[0]
    triples = [tuple(outs[3 * q:3 * q + 3]) for q in range(n)]
    return triples if job is None else (triples, res[1])


WIRE_WIDTH = 1024
WIRE_ROWS = 1024


def _wire_plan(shapes):
    starts, row = [], 0
    for s in shapes:
        r, c = (1, s[0]) if len(s) == 1 else s
        starts.append(row)
        row += -(-(r * max(1, c // WIRE_WIDTH)) // 8) * 8
    assert row <= WIRE_ROWS, row
    return starts


def _wire_cells(shape):
    if len(shape) == 1:
        n = shape[0]
        if n <= WIRE_WIDTH:
            return [(0, n, (slice(None),))]
        return [(h, WIRE_WIDTH, (slice(h * WIRE_WIDTH, (h + 1) * WIRE_WIDTH),)) for h in range(n // WIRE_WIDTH)]
    r, c = shape
    if c <= WIRE_WIDTH:
        return [(None, c, (slice(None), slice(None)))]
    per = c // WIRE_WIDTH
    return [(j * per + h, WIRE_WIDTH, (j, slice(h * WIRE_WIDTH, (h + 1) * WIRE_WIDTH)))
            for j in range(r) for h in range(per)]


def pack_small(arrays, *, name):
    shapes = [a.shape for a in arrays]
    starts = _wire_plan(shapes)

    def body(*refs):
        out = refs[-1]
        out[...] = jnp.zeros_like(out)
        for ref, shape, row0 in zip(refs[:-1], shapes, starts, strict=True):
            for off, cols, idx in _wire_cells(shape):
                if off is None:
                    out[row0:row0 + shape[0], 0:cols] = ref[idx]
                else:
                    out[row0 + off, 0:cols] = ref[idx]

    return pl.pallas_call(body, name=name, out_shape=jax.ShapeDtypeStruct((WIRE_ROWS, WIRE_WIDTH), F32),
                          compiler_params=pltpu.CompilerParams(vmem_limit_bytes=VMEM_LIMIT))(*arrays)


def update_small(packed, starts, triples, shard_cells, chip, *, name):
    n = len(triples)

    def body(chip_ref, p_ref, *refs):
        k = chip_ref[0]
        ins, outs = refs[:3 * n], refs[3 * n:]

        def cut(row0, rows, cols):
            c0, c1 = cols
            if isinstance(rows, int):
                return p_ref[row0 + rows, c0:c1]
            return p_ref[row0 + rows[0]:row0 + rows[1], c0:c1]

        for q in range(n):
            w_ref, m_ref, v_ref = ins[3 * q:3 * q + 3]
            g_ref, d_ref, nm_ref, nv_ref = outs[4 * q:4 * q + 4]
            shape = w_ref.shape
            if shard_cells[q] is None:
                pieces = [(cut(starts[q], (0, shape[0]) if off is None else off, (0, cols)), idx)
                          for off, cols, idx in _wire_cells(shape)]
            else:
                by_chip = [shard_cells[q](kk) for kk in range(N_CHIPS)]
                pieces = []
                for i, (_, _, idx) in enumerate(by_chip[0]):
                    g = cut(starts[q], *by_chip[0][i][:2])
                    for kk in range(1, N_CHIPS):
                        g = jnp.where(k == kk, cut(starts[q], *by_chip[kk][i][:2]), g)
                    pieces.append((g, idx))
            for g, idx in pieces:
                g_ref[idx] = g
                d_ref[idx], nm_ref[idx], nv_ref[idx] = _adamw_math(w_ref[idx], g, m_ref[idx], v_ref[idx])

    whole = lambda a: pl.BlockSpec(a.shape, lambda i, c_ref, nd=a.ndim: (0,) * nd)
    flat = [a for t in triples for a in t]
    out_shape = [jax.ShapeDtypeStruct(t[0].shape, F32) for t in triples for _ in range(4)]
    res = pl.pallas_call(
        body, name=name,
        grid_spec=pltpu.PrefetchScalarGridSpec(
            num_scalar_prefetch=1, grid=(1,),
            in_specs=[whole(packed)] + [whole(a) for a in flat],
            out_specs=[pl.BlockSpec(s.shape, lambda i, c_ref, nd=len(s.shape): (0,) * nd) for s in out_shape]),
        out_shape=out_shape,
        compiler_params=_params("arbitrary"))(chip.reshape(1).astype(jnp.int32), packed, *flat)
    return [tuple(res[4 * q:4 * q + 4]) for q in range(n)]


ANY = pl.BlockSpec(memory_space=pl.ANY)


def _place():
    x, y, c = lax.axis_index("x"), lax.axis_index("y"), lax.axis_index("c")
    return x, y, c, [(1 - x, y), (x, 1 - y), (1 - x, 1 - y)]


DMA_PIECE_BYTES = 512 * 1024


def _pieces(src, dst):
    shape = src.shape
    rows, cols = shape[-2], shape[-1]
    item = jnp.dtype(src.dtype).itemsize
    unit = 32 // item
    want = max(1, (rows * cols * item) // DMA_PIECE_BYTES)
    n = 1
    if rows % unit == 0:
        n = max(d for d in range(1, rows // unit + 1) if (rows // unit) % d == 0 and d <= want)
    size = rows // n
    out = []
    for lead in (range(shape[0]) if len(shape) == 3 else [None]):
        for i in range(n):
            sel = (pl.ds(i * size, size), slice(None)) if lead is None else (lead, pl.ds(i * size, size), slice(None))
            out.append((src.at[sel], dst.at[sel]))
    return out


def _send(src, dst, send_sem, recv_sem, peer, start=True):
    if start:
        for s, d in _pieces(src, dst):
            pltpu.make_async_remote_copy(src_ref=s, dst_ref=d, send_sem=send_sem, recv_sem=recv_sem,
                                         device_id=peer, device_id_type=MESH).start()
    return pltpu.make_async_remote_copy(src_ref=src, dst_ref=dst, send_sem=send_sem, recv_sem=recv_sem,
                                        device_id=peer, device_id_type=MESH)


def run_job(job, *, name):
    n_in, n_out = len(job.arrays), len(job.out_shapes)

    def body(*refs):
        job.fn(refs[:n_in], refs[n_in:n_in + n_out], refs[n_in + n_out:], "both")

    return pl.pallas_call(body, name=name, in_specs=[ANY] * n_in, out_specs=[ANY] * n_out,
                          out_shape=job.out_shapes, scratch_shapes=job.sems,
                          input_output_aliases={a: a for a in job.aliased()})(*job.arrays)


def gather_chips(shards, *, name):
    n = len(shards)

    def body(*refs):
        ins, outs = refs[:n], refs[n:2 * n]
        send_sems, recv_sems = refs[2 * n:]
        x, y, c, chips = _place()
        k = 2 * x + y
        copies = []
        for a in range(n):
            for j, peer in enumerate([(px, py, c) for px, py in chips] + [(x, y, 1 - c)]):
                copies.append(_send(ins[a], outs[a].at[k], send_sems.at[a, j], recv_sems.at[a, j], peer))
        for cp in copies:
            cp.wait()

    return pl.pallas_call(
        body, name=name, in_specs=[ANY] * n, out_specs=[ANY] * n,
        out_shape=[jax.ShapeDtypeStruct((N_CHIPS,) + s.shape, s.dtype) for s in shards],
        scratch_shapes=[pltpu.SemaphoreType.DMA((n, 4)), pltpu.SemaphoreType.DMA((n, 4))])(*shards)


def gather_by_halves_job(shards):
    n = len(shards)

    def fn(ins, outs, sems, phase):
        send_sems, recv_sems = sems
        x, y, c, chips = _place()
        k = 2 * x + y
        sibling = (x, y, 1 - c)
        begin = phase != "finish"
        waits, arrivals = [], []
        for a in range(n):
            half = ins[a].shape[0] // 2
            waits.append(_send(ins[a], outs[a].at[k], send_sems.at[a, 6], recv_sems.at[a, 6], sibling, begin))
            mine = pl.ds(c * half, half)
            for j, (px, py) in enumerate(chips):
                arrivals.append(_send(ins[a].at[mine, :], outs[a].at[k, mine, :], send_sems.at[a, j],
                                      recv_sems.at[a, j], (px, py, c), begin))
        if phase == "start":
            return
        i = 0
        for a in range(n):
            half = ins[a].shape[0] // 2
            mine = pl.ds(c * half, half)
            for j, (px, py) in enumerate(chips):
                arrivals[i].wait_recv()
                landed = outs[a].at[2 * px + py, mine, :]
                waits.append(_send(landed, landed, send_sems.at[a, 3 + j], recv_sems.at[a, 3 + j], sibling))
                i += 1
        for cp in arrivals:
            cp.wait_send()
        for cp in waits:
            cp.wait()

    return Job(shards, [jax.ShapeDtypeStruct((N_CHIPS,) + s.shape, s.dtype) for s in shards],
               [pltpu.SemaphoreType.DMA((n, 7)), pltpu.SemaphoreType.DMA((n, 7))], fn)


def gather_halves_job(shards):
    n = len(shards)

    def fn(ins, outs, sems, phase):
        send_sems, recv_sems = sems
        x, y, c, chips = _place()
        k = 2 * x + y
        copies = []
        for a in range(n):
            half = ins[a].shape[0] // 2
            mine = pl.ds(c * half, half)
            copies.append(_send(ins[a], outs[a].at[k], send_sems.at[a, 3], recv_sems.at[a, 3], (x, y, 1 - c),
                                phase != "finish"))
            for j, (px, py) in enumerate(chips):
                copies.append(_send(ins[a].at[mine, :], outs[a].at[k, mine, :], send_sems.at[a, j],
                                    recv_sems.at[a, j], (px, py, c), phase != "finish"))
        if phase != "start":
            for cp in copies:
                cp.wait()

    return Job(shards, [jax.ShapeDtypeStruct((N_CHIPS,) + s.shape, s.dtype) for s in shards],
               [pltpu.SemaphoreType.DMA((n, 4)), pltpu.SemaphoreType.DMA((n, 4))], fn)


def forward_halves_job(gathered):
    n = len(gathered)

    def fn(ins, outs, sems, phase):
        send_sems, recv_sems = sems
        x, y, c, chips = _place()
        copies = []
        for a in range(n):
            half = outs[a].shape[1] // 2
            for j, (px, py) in enumerate(chips):
                landed = outs[a].at[2 * px + py, pl.ds(c * half, half), :]
                copies.append(_send(landed, landed, send_sems.at[a, j], recv_sems.at[a, j], (x, y, 1 - c),
                                    phase != "finish"))
        if phase != "start":
            for cp in copies:
                cp.wait()

    return Job(gathered, [jax.ShapeDtypeStruct(g.shape, g.dtype) for g in gathered],
               [pltpu.SemaphoreType.DMA((n, 3)), pltpu.SemaphoreType.DMA((n, 3))], fn, in_place=True)


def swap_job(grads):
    n = len(grads)

    def fn(ins, outs, sems, phase):
        send_sems, recv_sems = sems
        x, y, c, _ = _place()
        copies = []
        for a in range(n):
            half = ins[a].shape[1] // 2
            copies.append(_send(ins[a].at[:, pl.ds((1 - c) * half, half), :], outs[a], send_sems.at[a],
                                recv_sems.at[a], (x, y, 1 - c), phase != "finish"))
        if phase != "start":
            for cp in copies:
                cp.wait()

    return Job(grads, [jax.ShapeDtypeStruct((g.shape[0], g.shape[1] // 2, g.shape[2]), g.dtype) for g in grads],
               [pltpu.SemaphoreType.DMA((n,)), pltpu.SemaphoreType.DMA((n,))], fn)


def _chip_slot(s, k):
    rel = s ^ k
    return (rel >> 1) | ((rel & 1) << 1)


def sum_cores(g, theirs, ck, out_dtype, *, name):
    n, r, cols = g.shape
    half = r // 2
    tr = _tile(half, max(16, (1 << 19) // cols), 16)
    nb = half // tr

    def body(ck_ref, g_ref, t_ref, o_ref):
        o_ref[...] = (g_ref[...] + t_ref[...]).astype(o_ref.dtype)

    return pl.pallas_call(
        body, name=name,
        grid_spec=pltpu.PrefetchScalarGridSpec(
            num_scalar_prefetch=1, grid=(n, nb),
            in_specs=[pl.BlockSpec((None, tr, cols), lambda s, i, ck_ref: (s, ck_ref[0] * nb + i, 0)),
                      pl.BlockSpec((None, tr, cols), lambda s, i, ck_ref: (s, i, 0))],
            out_specs=pl.BlockSpec((None, tr, cols), lambda s, i, ck_ref: (_chip_slot(s, ck_ref[1]), i, 0))),
        out_shape=jax.ShapeDtypeStruct((n, half, cols), out_dtype),
        compiler_params=_params("parallel", "parallel"))(ck, g, theirs)


def scatter_job(parts):
    n = len(parts)

    def fn(ins, outs, sems, phase):
        send_sems, recv_sems = sems
        x, y, c, chips = _place()
        copies = []
        for a in range(n):
            for j, (px, py) in enumerate(chips):
                copies.append(_send(ins[a].at[1 + j], outs[a].at[j], send_sems.at[a, j], recv_sems.at[a, j],
                                    (px, py, c), phase != "finish"))
        if phase != "start":
            for cp in copies:
                cp.wait()

    return Job(parts, [jax.ShapeDtypeStruct((3,) + p.shape[1:], p.dtype) for p in parts],
               [pltpu.SemaphoreType.DMA((n, 3)), pltpu.SemaphoreType.DMA((n, 3))], fn)


def sum_chips(own, others, ck, *, name):
    _, r, cols = own.shape
    tr = _tile(r, max(16, (1 << 19) // cols), 16)

    def body(ck_ref, a_ref, b0_ref, b1_ref, b2_ref, o_ref):
        o_ref[...] = (a_ref[...].astype(F32) + b0_ref[...].astype(F32) + b1_ref[...].astype(F32)
                      + b2_ref[...].astype(F32))

    other = lambda j: pl.BlockSpec((None, tr, cols), lambda i, ck_ref: (j, i, 0))
    return pl.pallas_call(
        body, name=name,
        grid_spec=pltpu.PrefetchScalarGridSpec(
            num_scalar_prefetch=1, grid=(r // tr,),
            in_specs=[pl.BlockSpec((None, tr, cols), lambda i, ck_ref: (0, i, 0)), other(0), other(1), other(2)],
            out_specs=pl.BlockSpec((None, tr, cols), lambda i, ck_ref: (ck_ref[0], i, 0))),
        out_shape=jax.ShapeDtypeStruct((2, r, cols), F32),
        compiler_params=_params("parallel"))(ck, own, others, others, others)


def share_job(joined):
    n = len(joined)

    def fn(ins, outs, sems, phase):
        send_sems, recv_sems = sems
        x, y, c, _ = _place()
        copies = [_send(outs[a].at[c], outs[a].at[c], send_sems.at[a], recv_sems.at[a], (x, y, 1 - c),
                        phase != "finish") for a in range(n)]
        if phase != "start":
            for cp in copies:
                cp.wait()

    return Job(joined, [jax.ShapeDtypeStruct(j.shape, j.dtype) for j in joined],
               [pltpu.SemaphoreType.DMA((n,)), pltpu.SemaphoreType.DMA((n,))], fn, in_place=True)


def kernel(x, meta_tokens, norm0_g, l0_w_in, l0_lam_re, l0_lam_im, l0_log_dt, l0_b_re, l0_b_im, l0_c_re, l0_c_im, l0_d_skip, l0_w_glu, l0_b_glu, l0_w_out, norm1_g, l1_w_in, l1_conv_w, l1_conv_b, l1_w_out, norm2_g, l2_w_in, l2_w_grp, l2_b_grp, l2_scale, l2_w_out, norm3_g, l3_w_in, l3_lam_re, l3_lam_im, l3_log_dt, l3_b_re, l3_b_im, l3_c_re, l3_c_im, l3_d_skip, l3_w_glu, l3_b_glu, l3_w_out, final_g, loss_target, m_meta_tokens, m_norm0_g, m_l0_w_in, m_l0_lam_re, m_l0_lam_im, m_l0_log_dt, m_l0_b_re, m_l0_b_im, m_l0_c_re, m_l0_c_im, m_l0_d_skip, m_l0_w_glu, m_l0_b_glu, m_l0_w_out, m_norm1_g, m_l1_w_in, m_l1_conv_w, m_l1_conv_b, m_l1_w_out, m_norm2_g, m_l2_w_in, m_l2_w_grp, m_l2_b_grp, m_l2_scale, m_l2_w_out, m_norm3_g, m_l3_w_in, m_l3_lam_re, m_l3_lam_im, m_l3_log_dt, m_l3_b_re, m_l3_b_im, m_l3_c_re, m_l3_c_im, m_l3_d_skip, m_l3_w_glu, m_l3_b_glu, m_l3_w_out, m_final_g, v_meta_tokens, v_norm0_g, v_l0_w_in, v_l0_lam_re, v_l0_lam_im, v_l0_log_dt, v_l0_b_re, v_l0_b_im, v_l0_c_re, v_l0_c_im, v_l0_d_skip, v_l0_w_glu, v_l0_b_glu, v_l0_w_out, v_norm1_g, v_l1_w_in, v_l1_conv_w, v_l1_conv_b, v_l1_w_out, v_norm2_g, v_l2_w_in, v_l2_w_grp, v_l2_b_grp, v_l2_scale, v_l2_w_out, v_norm3_g, v_l3_w_in, v_l3_lam_re, v_l3_lam_im, v_l3_log_dt, v_l3_b_re, v_l3_b_im, v_l3_c_re, v_l3_c_im, v_l3_d_skip, v_l3_w_glu, v_l3_b_glu, v_l3_w_out, v_final_g):
    given = dict(locals())
    w = {n: given[n] for n in WEIGHTS}
    mom_m = {n: given["m_" + n] for n in WEIGHTS}
    mom_v = {n: given["v_" + n] for n in WEIGHTS}
    seq = x.shape[1]
    real = N_META + seq
    lp = -(-real // SCAN_ROWS) * SCAN_ROWS
    chip = 2 * lax.axis_index("x") + lax.axis_index("y")
    row = lambda a: a.reshape(1, -1)

    shard_bf16 = {n: (w[n].reshape(-1, w[n].shape[-1]) if n == 'l2_w_grp' else w[n]).astype(BF16) for n in BIG}
    layer = lambda i: [n for n in BIG if n.startswith("l%d_" % i)]
    over_ici = lambda names: gather_halves_job([shard_bf16[n] for n in names])
    full = {}

    def landed(names, arrays):
        for n, arr in zip(names, arrays, strict=True):
            if n.endswith('w_in'):
                full[n] = arr
            elif n == 'l2_w_grp':
                full[n] = arr.reshape(N_CHIPS, len(POOL_WINDOWS), POOL_GROUP // N_CHIPS, POOL_GROUP)
            else:
                full[n] = arr.reshape(1, -1, arr.shape[-1])

    landed(['l0_w_in'], run_job(gather_by_halves_job([shard_bf16['l0_w_in']]), name="gather_l0_w_in"))
    full.update(zip(SMALL_SHARDED, gather_chips([w[n] for n in SMALL_SHARDED], name="gather_small_shards"),
                    strict=True))
    meta_full = full['meta_tokens'].transpose(1, 0, 2).reshape(N_META, D_MODEL)
    conv_w_full = full['l1_conv_w'].transpose(1, 0, 2).reshape(3, CONV_E)
    b_grp_full = full['l2_b_grp'].transpose(1, 0, 2).reshape(len(POOL_WINDOWS), POOL_GROUP)

    h0 = jnp.concatenate([meta_full, x[0], jnp.zeros((lp - real, D_MODEL), F32)], axis=0)
    target = jnp.concatenate([jnp.zeros((N_META, D_MODEL), F32), loss_target[0],
                              jnp.zeros((lp - real, D_MODEL), F32)], axis=0)
    grads = {}
    saved = {}
    gip = lambda a: jnp.swapaxes(a, 1, 2)

    def s5_layer_fwd(i, h, first=False):
        pre = "l%d_" % i
        prm = [w[pre + 'lam_re'], w[pre + 'lam_im'], w[pre + 'log_dt'].reshape(-1, 1), gip(w[pre + 'b_re']),
               gip(w[pre + 'b_im']), w[pre + 'c_re'], w[pre + 'c_im']]
        a_re, a_im, bblk, cblk = s5_prep(prm, name=pre + "prep")
        tables = (a_re.reshape(1, -1), a_im.reshape(1, -1), bblk, cblk)
        ar, ai = tables[:2]
        gain = row(w["norm%d_g" % i])
        rest = [pre + 'w_glu', pre + 'w_out']
        if not first:
            uz, n = norm_mm_nn(h, gain, full[pre + 'w_in'], name=pre + "in")
            y0, yg, sr, si = s5_fwd(uz, bblk, cblk, ar, ai, row(w[pre + 'd_skip']), name=pre + "scan")
            q, y3 = glu_gate(yg, full[pre + 'w_glu'], row(w[pre + 'b_glu']), y0, uz, name=pre + "glu")
        else:
            (uz, n), halves = norm_mm_nn(h, gain, full[pre + 'w_in'], name=pre + "in", job=over_ici(rest))
            (y0, yg, sr, si), arrived = s5_fwd(
                uz, bblk, cblk, ar, ai, row(w[pre + 'd_skip']), name=pre + "scan",
                job=Job.both(over_ici(layer(1)), forward_halves_job(halves)))
            landed(rest, arrived[len(layer(1)):])
            (q, y3), whole = glu_gate(yg, full[pre + 'w_glu'], row(w[pre + 'b_glu']), y0, uz, name=pre + "glu",
                                      job=forward_halves_job(arrived[:len(layer(1))]))
            landed(layer(1), whole)
        h_new = mm_nn(y3, full[pre + 'w_out'], add=h, name=pre + "out")
        saved[i] = dict(h=h, n=n, uz=uz, y0=y0, sr=sr, si=si, yg=yg, q=q, y3=y3, tables=tables, prm=prm)
        return h_new

    def s5_layer_bwd(i, dh, carry=None):
        pre = "l%d_" % i
        s = saved[i]
        ar, ai, bblk, cblk = s['tables']
        grads[pre + 'w_out'] = mm_tn(s['y3'], dh, 1, name=pre + "d_w_out")[0]
        dy0_direct, dq, dp, db_glu = s5_gate_bwd(dh, full[pre + 'w_out'], s['y0'], s['q'], s['uz'],
                                                 name=pre + "d_y3")
        grads[pre + 'b_glu'] = db_glu
        grads[pre + 'w_glu'] = mm_tn(s['yg'], dq, 1, name=pre + "d_w_glu")[0]
        early = [] if carry is None else [pre + 'w_out', pre + 'w_glu']
        if carry is None:
            dyg = mm_nt(dq, full[pre + 'w_glu'], name=pre + "d_yg")
            res = s5_bwd(dy0_direct, dyg, s['y0'], s['uz'], s['sr'], s['si'], bblk, cblk, ar, ai,
                         row(w[pre + 'd_skip']), dp, name=pre + "d_scan")
        else:
            pieces = [chip_major(n) for n in early]
            dyg, theirs = mm_nt(dq, full[pre + 'w_glu'], name=pre + "d_yg", job=swap_job(pieces))
            sum_over_cores(early, pieces, theirs)
            res, arrived = s5_bwd(dy0_direct, dyg, s['y0'], s['uz'], s['sr'], s['si'], bblk, cblk, ar, ai,
                                  row(w[pre + 'd_skip']), dp, name=pre + "d_scan",
                                  job=scatter_job(scatter_of(carry) + [pair[n] for n in early]))
            from_chips.update(zip(layer(carry) + early, arrived, strict=True))
        dp, dbblk, dcblk, dar, dai, dd = res
        grads[pre + 'd_skip'] = dd
        cots = (dar.reshape(S5_GROUPS, S5_STATE), dai.reshape(S5_GROUPS, S5_STATE), dbblk, dcblk)
        for key, val in zip(('lam_re', 'lam_im', 'log_dt', 'b_re', 'b_im', 'c_re', 'c_im'),
                            s5_prep_bwd(s['prm'], cots, name=pre + "d_prep"), strict=True):
            grads[pre + key] = val.reshape(-1) if key == 'log_dt' else val
        if carry is None:
            grads[pre + 'w_in'] = mm_tn(s['n'], dp, N_CHIPS, name=pre + "d_w_in")
        else:
            later = [n for n in BIG if not n.startswith(pre)] + early
            sums = [sum_chips(pair[n], from_chips[n], ck, name="sum_chips_" + n) for n in later]
            grads[pre + 'w_in'], shared = mm_tn(s['n'], dp, N_CHIPS, name=pre + "d_w_in", job=share_job(sums))
            joined.update(zip(later, shared, strict=True))
        names = [n for n in layer(i) if n not in early]
        pieces = [chip_major(n) for n in names]
        (dh_in, dg), theirs = mm_nt_norm_bwd(dp, full[pre + 'w_in'], s['h'], row(w["norm%d_g" % i]), dh,
                                             name=pre + "d_n", job=swap_job(pieces))
        sum_over_cores(names, pieces, theirs)
        grads["norm%d_g" % i] = dg
        return dh_in

    h1 = s5_layer_fwd(0, h0, first=True)

    (p1, n1), halves = norm_mm_nn(h1, row(w['norm1_g']), full['l1_w_in'], name="l1_in", job=over_ici(layer(2)))
    y3_1 = conv_fwd(p1, conv_w_full, row(w['l1_conv_b']), name="l1_conv")
    h2, whole = mm_nn(y3_1, full['l1_w_out'], add=h1, name="l1_out", job=forward_halves_job(halves))
    landed(layer(2), whole)

    (p2, n2), halves = norm_mm_nn(h2, row(w['norm2_g']), full['l2_w_in'], name="l2_in", job=over_ici(layer(3)))
    mixed = pool_fwd(p2, name="l2_pool")
    o2, y3_2 = grp_nn_gate(mixed, full['l2_w_grp'], b_grp_full.reshape(1, -1), p2, row(w['l2_scale']), name="l2_grp")
    h3, whole = mm_nn(y3_2, full['l2_w_out'], add=h2, name="l2_out", job=forward_halves_job(halves))
    landed(layer(3), whole)

    h4 = s5_layer_fwd(3, h3)

    dh, grads['final_g'], sq = final_loss(h4, target, row(w['final_g']), N_META, real, name="loss")
    loss = lax.psum(0.5 / D_MODEL * jnp.sum(sq), ("x", "y", "c"))

    ck = jnp.stack([lax.axis_index("c"), chip]).astype(jnp.int32)
    pair, from_chips, joined = {}, {}, {}

    def chip_major(n):
        g = grads[n]
        if n.endswith('w_in'):
            return g
        if n == 'l2_w_grp':
            return g.reshape(N_CHIPS, -1, POOL_GROUP)
        return g.reshape(N_CHIPS, -1, g.shape[-1])

    def core_parts(i):
        return layer(i), [chip_major(n) for n in layer(i)]

    def sum_over_cores(names, pieces, theirs):
        for n, g, t in zip(names, pieces, theirs, strict=True):
            pair[n] = sum_cores(g, t, ck, F32 if n == 'small' else BF16, name="sum_cores_" + n)

    scatter_of = lambda i: [pair[n] for n in layer(i)]

    def scattered(i, arrays):
        from_chips.update(zip(layer(i), arrays, strict=True))

    dh = s5_layer_bwd(3, dh)

    grads['l2_w_out'] = mm_tn(y3_2, dh, 1, name="l2_d_w_out")[0]
    do2, dp2, dscale, dbgrp = pool_gate_bwd(dh, full['l2_w_out'], o2, p2, row(w['l2_scale']), name="l2_d_y3")
    grads['l2_scale'] = dscale
    grads['l2_b_grp'] = dbgrp
    grads['l2_w_grp'] = grp_tn(mixed, do2, name="l2_d_w_grp")
    dmix = grp_nt(do2, full['l2_w_grp'], name="l2_d_mixed")
    dp2 = pool_bwd(dmix, dp2, name="l2_d_pool")
    grads['l2_w_in'], arrived = mm_tn(n2, dp2, N_CHIPS, name="l2_d_w_in", job=scatter_job(scatter_of(3)))
    scattered(3, arrived)
    names, pieces = core_parts(2)
    (dh, dg), theirs = mm_nt_norm_bwd(dp2, full['l2_w_in'], h2, row(w['norm2_g']), dh, name="l2_d_n",
                                      job=swap_job(pieces))
    sum_over_cores(names, pieces, theirs)
    grads['norm2_g'] = dg

    grads['l1_w_out'] = mm_tn(y3_1, dh, 1, name="l1_d_w_out")[0]
    dp1, dcw, dcb = conv_bwd(dh, full['l1_w_out'], p1, conv_w_full, row(w['l1_conv_b']), name="l1_d_conv")
    grads['l1_conv_w'] = dcw
    grads['l1_conv_b'] = dcb
    grads['l1_w_in'], arrived = mm_tn(n1, dp1, N_CHIPS, name="l1_d_w_in", job=scatter_job(scatter_of(2)))
    scattered(2, arrived)
    names, pieces = core_parts(1)
    (dh, dg), theirs = mm_nt_norm_bwd(dp1, full['l1_w_in'], h1, row(w['norm1_g']), dh, name="l1_d_n",
                                      job=swap_job(pieces))
    sum_over_cores(names, pieces, theirs)
    grads['norm1_g'] = dg

    dh = s5_layer_bwd(0, dh, carry=1)
    grad_x = dh[N_META:real][None]
    grads['meta_tokens'] = dh[:N_META]

    wide = [n for n in SMALL if w[n].ndim == 3]
    wire = [grads[n].reshape(S5_GROUPS, -1) if n in wide else grads[n] for n in SMALL]
    starts = dict(zip(SMALL, _wire_plan([a.shape for a in wire]), strict=True))
    small_local = pack_small(wire, name="pack_small")
    pieces = [small_local.reshape(N_CHIPS, -1, WIRE_WIDTH)]
    sum_over_cores(['small'], pieces, run_job(swap_job(pieces), name="reduce_cores_small"))

    out_g, out_d, out_m, out_v = {}, {}, {}, {}
    as2d = lambda a: a.reshape(-1, a.shape[-1])

    def update_big(names, name, job=None):
        for n in names:
            out_g[n] = joined[n].reshape(w[n].shape)
        res = adamw_many([(as2d(w[n]), as2d(out_g[n]), as2d(mom_m[n]), as2d(mom_v[n])) for n in names],
                         name=name, job=job)
        triples, carried = res if job is not None else (res, None)
        for n, (d, nm, nv) in zip(names, triples, strict=True):
            out_d[n], out_m[n], out_v[n] = d.reshape(w[n].shape), nm.reshape(w[n].shape), nv.reshape(w[n].shape)
        return carried

    rest = ['l0_w_in', 'small']
    later = [n for n in BIG if n not in rest]
    from_chips.update(zip(rest, update_big(later, "adamw_big", scatter_job([pair[n] for n in rest])), strict=True))
    sums = [sum_chips(pair[n], from_chips[n], ck, name="sum_chips_" + n) for n in rest]
    joined.update(zip(rest, run_job(share_job(sums), name="share_cores"), strict=True))
    update_big(['l0_w_in'], "adamw_l0_w_in")
    (small_all,) = run_job(gather_by_halves_job([joined['small'].reshape(-1, WIRE_WIDTH)]), name="gather_small")
    small_all = small_all.reshape(WIRE_ROWS, WIRE_WIDTH)

    half_w = WIRE_WIDTH // 2
    shard_cells = {
        'meta_tokens': lambda k: [((0, N_META), (k * 256, (k + 1) * 256), (slice(None), slice(None)))],
        'l1_conv_w': lambda k: [(2 * j + k // 2, ((k % 2) * half_w, (k % 2 + 1) * half_w), (j, slice(None)))
                                for j in range(3)],
        'l2_b_grp': lambda k: [(j // 2, ((j % 2) * half_w + k * 128, (j % 2) * half_w + (k + 1) * 128),
                                (j, slice(None))) for j in range(len(POOL_WINDOWS))],
    }
    plain = [n for n in SMALL if n not in wide]
    res = update_small(small_all, [starts[n] for n in plain], [(w[n], mom_m[n], mom_v[n]) for n in plain],
                       [shard_cells.get(n) for n in plain], chip, name="adamw_small")
    for n, (g, d, nm, nv) in zip(plain, res, strict=True):
        out_g[n], out_d[n], out_m[n], out_v[n] = g, d, nm, nv
    as_gip = lambda n, a: gip(a) if n.endswith(('b_re', 'b_im')) else a
    g_gip = {n: small_all[starts[n]:starts[n] + S5_GROUPS].reshape(S5_GROUPS, S5_GROUP, S5_STATE) for n in wide}
    res = adamw_many([(as_gip(n, w[n]), g_gip[n], as_gip(n, mom_m[n]), as_gip(n, mom_v[n])) for n in wide],
                     name="adamw_s5_tensors")
    for n, (d, nm, nv) in zip(wide, res, strict=True):
        out_g[n], out_d[n], out_m[n], out_v[n] = as_gip(n, g_gip[n]), as_gip(n, d), as_gip(n, nm), as_gip(n, nv)

    return (loss, grad_x, *[out_g[n] for n in WEIGHTS], *[out_d[n] for n in WEIGHTS],
            *[out_m[n] for n in WEIGHTS], *[out_v[n] for n in WEIGHTS])
```

```python
import functools

import jax
import jax.numpy as jnp
from jax import lax
from jax.experimental import pallas as pl
from jax.experimental.pallas import tpu as pltpu

F32, BF16 = jnp.float32, jnp.bfloat16
MESH = pl.DeviceIdType.MESH

D_MODEL = 1024
N_META = 16
EPS = 1e-6
S5_GROUPS, S5_GROUP, S5_STATE = 64, 16, 64
LANE_BLOCK = 128
S5_BLOCKS = D_MODEL // LANE_BLOCK
GROUPS_PER_BLOCK = LANE_BLOCK // S5_GROUP
STATE_BLOCK = GROUPS_PER_BLOCK * S5_STATE
SCAN_ROWS = 256
SCAN_CHUNK = 1088
SUBLANES = 8
CONV_E = 2048
POOL_E = 2048
POOL_WINDOWS = (2, 4, 8, 16)
POOL_GROUP = 512
POOL_HALO = 16
N_CHIPS = 4

ADAM_LR, ADAM_B1, ADAM_B2, ADAM_EPS, ADAM_WD, ADAM_STEP = 0.001, 0.9, 0.999, 1e-08, 0.01, 10

VMEM_LIMIT = 48 * 1024 * 1024
TN_OPERAND_BYTES = 28 * 1024 * 1024

WEIGHTS = ['meta_tokens', 'norm0_g', 'l0_w_in', 'l0_lam_re', 'l0_lam_im', 'l0_log_dt', 'l0_b_re', 'l0_b_im',
           'l0_c_re', 'l0_c_im', 'l0_d_skip', 'l0_w_glu', 'l0_b_glu', 'l0_w_out', 'norm1_g', 'l1_w_in',
           'l1_conv_w', 'l1_conv_b', 'l1_w_out', 'norm2_g', 'l2_w_in', 'l2_w_grp', 'l2_b_grp', 'l2_scale',
           'l2_w_out', 'norm3_g', 'l3_w_in', 'l3_lam_re', 'l3_lam_im', 'l3_log_dt', 'l3_b_re', 'l3_b_im',
           'l3_c_re', 'l3_c_im', 'l3_d_skip', 'l3_w_glu', 'l3_b_glu', 'l3_w_out', 'final_g']
BIG = ['l0_w_in', 'l0_w_glu', 'l0_w_out', 'l1_w_in', 'l1_w_out', 'l2_w_in', 'l2_w_grp', 'l2_w_out',
       'l3_w_in', 'l3_w_glu', 'l3_w_out']
SMALL_SHARDED = ['meta_tokens', 'l1_conv_w', 'l2_b_grp']
SMALL = [n for n in WEIGHTS if n not in BIG]


def _params(*sem):
    return pltpu.CompilerParams(dimension_semantics=sem, vmem_limit_bytes=VMEM_LIMIT)


class Job:
    def __init__(self, arrays, out_shapes, sems, fn, in_place=False):
        self.arrays, self.out_shapes, self.sems, self.fn = list(arrays), list(out_shapes), list(sems), fn
        self.in_place = in_place
        assert len(self.arrays) == len(self.out_shapes)

    @staticmethod
    def both(first, second):
        na, ns = len(first.arrays), len(first.sems)

        def fn(ins, outs, sems, phase):
            first.fn(ins[:na], outs[:na], sems[:ns], phase)
            second.fn(ins[na:], outs[na:], sems[ns:], phase)

        job = Job(first.arrays + second.arrays, first.out_shapes + second.out_shapes, first.sems + second.sems, fn)
        job.in_place = [first.in_place] * na + [second.in_place] * len(second.arrays)
        return job

    def aliased(self):
        flags = self.in_place if isinstance(self.in_place, list) else [self.in_place] * len(self.arrays)
        return [a for a, flag in enumerate(flags) if flag]


def _call(body, *, name, grid, in_specs, out_specs, out_shape, args, semantics, scratch=(), aliases=None, job=None):
    if job is None:
        return pl.pallas_call(
            body, name=name, grid=grid, in_specs=list(in_specs), out_specs=list(out_specs),
            out_shape=list(out_shape), scratch_shapes=list(scratch), input_output_aliases=aliases or {},
            compiler_params=_params(*semantics))(*args)
    counts = [len(in_specs), len(job.arrays), len(out_specs), len(job.out_shapes), len(scratch), len(job.sems)]

    def hosted(*refs):
        parts, pos = [], 0
        for k in counts:
            parts.append(refs[pos:pos + k])
            pos += k
        ins, job_ins, outs, job_outs, scr, job_sems = parts
        steps = [pl.program_id(d) for d in range(len(grid))]
        first = functools.reduce(jnp.logical_and, [s == 0 for s in steps])
        last = functools.reduce(jnp.logical_and, [s == g - 1 for s, g in zip(steps, grid, strict=True)])

        @pl.when(first)
        def _():
            job.fn(job_ins, job_outs, job_sems, "start")

        body(*ins, *outs, *scr)

        @pl.when(last)
        def _():
            job.fn(job_ins, job_outs, job_sems, "finish")

    any_spec = pl.BlockSpec(memory_space=pl.ANY)
    aliases = dict(aliases or {})
    aliases.update({len(in_specs) + a: len(out_specs) + a for a in job.aliased()})
    res = pl.pallas_call(
        hosted, name=name, grid=grid, in_specs=list(in_specs) + [any_spec] * len(job.arrays),
        out_specs=list(out_specs) + [any_spec] * len(job.out_shapes),
        out_shape=list(out_shape) + job.out_shapes, scratch_shapes=list(scratch) + job.sems,
        input_output_aliases=aliases,
        compiler_params=_params(*["arbitrary"] * len(grid)))(*args, *job.arrays)
    return res[:len(out_specs)], res[len(out_specs):]


def _tile(n, cap, mult):
    best = None
    for t in range(mult, min(n, cap) + 1, mult):
        if n % t == 0:
            best = t
    assert best is not None, (n, cap, mult)
    return best


def _with_job(res, job):
    return res[0] if job is None else (res[0][0], res[1])


def mm_nn(a, b, *, name, bias=None, add=None, out_dtype=F32, job=None):
    m, k = a.shape
    s, _, ns = b.shape
    n = s * ns
    tm, tn = _tile(m, 1088, 16), _tile(ns, 1024, 128)
    per = ns // tn

    def body(*refs):
        a_ref, b_ref = refs[0], refs[1]
        o_ref = refs[-1]
        acc = jnp.dot(a_ref[...].astype(BF16), b_ref[...], preferred_element_type=F32)
        pos = 2
        if bias is not None:
            acc = acc + refs[pos][...]
            pos += 1
        if add is not None:
            acc = acc + refs[pos][...]
        o_ref[...] = acc.astype(o_ref.dtype)

    in_specs = [pl.BlockSpec((tm, k), lambda i, j: (i, 0)),
                pl.BlockSpec((None, k, tn), lambda i, j: (j // per, 0, j % per))]
    args = [a, b]
    if bias is not None:
        in_specs.append(pl.BlockSpec((1, tn), lambda i, j: (0, j)))
        args.append(bias)
    if add is not None:
        in_specs.append(pl.BlockSpec((tm, tn), lambda i, j: (i, j)))
        args.append(add)
    return _with_job(_call(
        body, name=name, grid=(m // tm, n // tn), in_specs=in_specs,
        out_specs=[pl.BlockSpec((tm, tn), lambda i, j: (i, j))],
        out_shape=[jax.ShapeDtypeStruct((m, n), out_dtype)], args=args,
        semantics=("parallel", "parallel"), job=job), job)


def norm_mm_nn(h, gain, b, *, name, job=None):
    m, k = h.shape
    s, _, ns = b.shape
    n = s * ns
    tm, tn = _tile(m, 1088, 16), _tile(ns, 1024, 128)
    per = ns // tn

    def body(h_ref, g_ref, b_ref, o_ref, n_ref):
        @pl.when(pl.program_id(1) == 0)
        def _():
            n_ref[...] = _rms(h_ref[...], g_ref[...]).astype(n_ref.dtype)

        o_ref[...] = jnp.dot(n_ref[...], b_ref[...], preferred_element_type=F32)

    res = _call(
        body, name=name, grid=(m // tm, n // tn),
        in_specs=[pl.BlockSpec((tm, k), lambda i, j: (i, 0)), pl.BlockSpec((1, k), lambda i, j: (0, 0)),
                  pl.BlockSpec((None, k, tn), lambda i, j: (j // per, 0, j % per))],
        out_specs=[pl.BlockSpec((tm, tn), lambda i, j: (i, j)), pl.BlockSpec((tm, k), lambda i, j: (i, 0))],
        out_shape=[jax.ShapeDtypeStruct((m, n), F32), jax.ShapeDtypeStruct((m, k), BF16)], args=(h, gain, b),
        semantics=("parallel", "arbitrary"), job=job)
    return (res[0], res[1]) if job is None else ((res[0][0], res[0][1]), res[1])


def glu_gate(yg, w_glu, b_glu, y0, uz, *, name, job=None):
    m, k = yg.shape
    n = w_glu.shape[2]
    tm = _tile(m, 1088, 16)

    def body(a_ref, w_ref, b_ref, y0_ref, z_ref, q_ref, o_ref):
        q = jnp.dot(a_ref[...], w_ref[...], preferred_element_type=F32) + b_ref[...]
        q_ref[...] = q
        o_ref[...] = _s5_gate(y0_ref[...], q, z_ref[...]).astype(o_ref.dtype)

    tile = pl.BlockSpec((tm, n), lambda i: (i, 0))
    res = _call(
        body, name=name, grid=(m // tm,),
        in_specs=[pl.BlockSpec((tm, k), lambda i: (i, 0)), pl.BlockSpec((None, k, n), lambda i: (0, 0, 0)),
                  pl.BlockSpec((1, n), lambda i: (0, 0)), tile, pl.BlockSpec((tm, n), lambda i: (i, 1))],
        out_specs=[tile, tile],
        out_shape=[jax.ShapeDtypeStruct((m, n), F32), jax.ShapeDtypeStruct((m, n), BF16)],
        args=(yg, w_glu, b_glu, y0, uz), semantics=("parallel",), job=job)
    return (res[0], res[1]) if job is None else ((res[0][0], res[0][1]), res[1])


def s5_gate_bwd(dh, w_out, y0, q, uz, *, name):
    m, d = dh.shape
    e = w_out.shape[1]
    tm = _tile(m, 544, 16)

    def body(dh_ref, w_ref, y0_ref, q_ref, z_ref, dy0_ref, dq_ref, dz_ref, db_ref):
        i = pl.program_id(0)
        dy3 = lax.dot_general(dh_ref[...].astype(BF16), w_ref[...], (((1,), (1,)), ((), ())),
                              preferred_element_type=F32)
        dy0, dq, dz = jax.vjp(_s5_gate, y0_ref[...], q_ref[...], z_ref[...])[1](dy3)
        dy0_ref[...] = dy0
        dq_ref[...] = dq.astype(dq_ref.dtype)
        dz_ref[...] = dz.astype(dz_ref.dtype)
        db = _colsum(dq)

        @pl.when(i == 0)
        def _():
            db_ref[...] = db

        @pl.when(i > 0)
        def _():
            db_ref[...] += db

    tile = pl.BlockSpec((tm, e), lambda i: (i, 0))
    right = pl.BlockSpec((tm, e), lambda i: (i, 1))
    return pl.pallas_call(
        body, name=name, grid=(m // tm,),
        in_specs=[pl.BlockSpec((tm, d), lambda i: (i, 0)), pl.BlockSpec((None, e, d), lambda i: (0, 0, 0)),
                  tile, tile, right],
        out_specs=[tile, tile, right, pl.BlockSpec((1, e), lambda i: (0, 0))],
        out_shape=[jax.ShapeDtypeStruct((m, e), F32), jax.ShapeDtypeStruct((m, e), BF16),
                   jax.ShapeDtypeStruct((m, 2 * e), BF16), jax.ShapeDtypeStruct((1, e), F32)],
        compiler_params=_params("arbitrary"))(dh, w_out, y0, q, uz)


def pool_gate_bwd(dh, w_out, o, p, scale, *, name):
    m, d = dh.shape
    e = w_out.shape[1]
    tm, te = _tile(m, 1088, 16), 1024
    nj = e // te

    def body(dh_ref, w_ref, o_ref, z_ref, s_ref, do_ref, dz_ref, ds_ref, db_ref):
        i = pl.program_id(1)
        dy3 = lax.dot_general(dh_ref[...].astype(BF16), w_ref[...], (((1,), (1,)), ((), ())),
                              preferred_element_type=F32)
        do, dz, ds = jax.vjp(_pool_gate, o_ref[...], z_ref[...], s_ref[...])[1](dy3)
        do_ref[...] = do.astype(do_ref.dtype)
        dz_ref[...] = dz.astype(dz_ref.dtype)
        db = _colsum(do)

        @pl.when(i == 0)
        def _():
            ds_ref[...] = ds
            db_ref[...] = db

        @pl.when(i > 0)
        def _():
            ds_ref[...] += ds
            db_ref[...] += db

    tile = pl.BlockSpec((tm, te), lambda j, i: (i, j))
    right = pl.BlockSpec((tm, te), lambda j, i: (i, nj + j))
    vec = pl.BlockSpec((1, te), lambda j, i: (0, j))
    return pl.pallas_call(
        body, name=name, grid=(nj, m // tm),
        in_specs=[pl.BlockSpec((tm, d), lambda j, i: (i, 0)), pl.BlockSpec((None, te, d), lambda j, i: (0, j, 0)),
                  tile, right, vec],
        out_specs=[tile, right, vec, vec],
        out_shape=[jax.ShapeDtypeStruct((m, e), BF16), jax.ShapeDtypeStruct((m, 2 * e), BF16),
                   jax.ShapeDtypeStruct((1, e), F32), jax.ShapeDtypeStruct((1, e), F32)],
        compiler_params=_params("parallel", "arbitrary"))(dh, w_out, o, p, scale)


def mm_nt_norm_bwd(g, w, h, gain, dh_out, *, name, job=None):
    m, kc = g.shape
    s, nout, kcs = w.shape
    assert s * kcs == kc
    tm, tk = _tile(m, 1088, 16), _tile(kcs, 1024, 128)
    per = kcs // tk
    nk = kc // tk

    def body(g_ref, w_ref, h_ref, gain_ref, dh_ref, o_ref, dg_ref):
        i, kk = pl.program_id(0), pl.program_id(1)
        part = lax.dot_general(g_ref[...].astype(BF16), w_ref[...], (((1,), (1,)), ((), ())),
                               preferred_element_type=F32)

        @pl.when(kk == 0)
        def _():
            o_ref[...] = part

        @pl.when(kk > 0)
        def _():
            o_ref[...] += part

        @pl.when(kk == nk - 1)
        def _():
            dh, dg = jax.vjp(_rms, h_ref[...], gain_ref[...])[1](o_ref[...])
            o_ref[...] = dh_ref[...] + dh

            @pl.when(i == 0)
            def _():
                dg_ref[...] = dg

            @pl.when(i > 0)
            def _():
                dg_ref[...] += dg

    row_tile = pl.BlockSpec((tm, nout), lambda i, kk: (i, 0))
    res = _call(
        body, name=name, grid=(m // tm, nk),
        in_specs=[pl.BlockSpec((tm, tk), lambda i, kk: (i, kk)),
                  pl.BlockSpec((None, nout, tk), lambda i, kk: (kk // per, 0, kk % per)),
                  row_tile, pl.BlockSpec((1, nout), lambda i, kk: (0, 0)), row_tile],
        out_specs=[row_tile, pl.BlockSpec((1, nout), lambda i, kk: (0, 0))],
        out_shape=[jax.ShapeDtypeStruct((m, nout), F32), jax.ShapeDtypeStruct((1, nout), F32)],
        args=(g, w, h, gain, dh_out), semantics=("arbitrary", "arbitrary"), job=job)
    return (res[0], res[1]) if job is None else ((res[0][0], res[0][1]), res[1])


def mm_nt(g, w, *, name, job=None):
    m, kc = g.shape
    s, nout, kcs = w.shape
    assert s * kcs == kc
    tm, tno, tk = _tile(m, 2176, 16), _tile(nout, 1024, 128), _tile(kcs, 1024, 128)
    per = kcs // tk

    def body(g_ref, w_ref, o_ref):
        kk = pl.program_id(2)
        part = lax.dot_general(g_ref[...].astype(BF16), w_ref[...], (((1,), (1,)), ((), ())),
                               preferred_element_type=F32)

        @pl.when(kk == 0)
        def _():
            o_ref[...] = part

        @pl.when(kk > 0)
        def _():
            o_ref[...] += part

    return _with_job(_call(
        body, name=name, grid=(m // tm, nout // tno, kc // tk),
        in_specs=[pl.BlockSpec((tm, tk), lambda i, j, kk: (i, kk)),
                  pl.BlockSpec((None, tno, tk), lambda i, j, kk: (kk // per, j, kk % per))],
        out_specs=[pl.BlockSpec((tm, tno), lambda i, j, kk: (i, j))],
        out_shape=[jax.ShapeDtypeStruct((m, nout), F32)], args=(g, w),
        semantics=("parallel", "parallel", "arbitrary"), job=job), job)


def mm_tn(a, g, shards, *, name, job=None):
    m, ka = a.shape
    n = g.shape[1]
    ns = n // shards
    tka, tn = _tile(ka, 1024, 128), _tile(ns, 512, 128)
    row_bytes = tka * jnp.dtype(a.dtype).itemsize + tn * jnp.dtype(g.dtype).itemsize
    tm = _tile(m, TN_OPERAND_BYTES // (2 * row_bytes), 16)
    per = ns // tn

    def body(a_ref, g_ref, o_ref):
        mm = pl.program_id(2)
        part = lax.dot_general(a_ref[...].astype(BF16), g_ref[...].astype(BF16), (((0,), (0,)), ((), ())),
                               preferred_element_type=F32)

        @pl.when(mm == 0)
        def _():
            o_ref[...] = part

        @pl.when(mm > 0)
        def _():
            o_ref[...] += part

    return _with_job(_call(
        body, name=name, grid=(ka // tka, n // tn, m // tm),
        in_specs=[pl.BlockSpec((tm, tka), lambda i, j, mm: (mm, i)),
                  pl.BlockSpec((tm, tn), lambda i, j, mm: (mm, j))],
        out_specs=[pl.BlockSpec((None, tka, tn), lambda i, j, mm: (j // per, i, j % per))],
        out_shape=[jax.ShapeDtypeStruct((shards, ka, ns), F32)], args=(a, g),
        semantics=("parallel", "parallel", "arbitrary"), job=job), job)


def grp_nn_gate(a, w, bias, p, scale, *, name):
    m = a.shape[0]
    tm = _tile(m, 1088, 16)
    ng = len(POOL_WINDOWS)

    def body(a_ref, w_ref, b_ref, z_ref, s_ref, o_ref, y_ref):
        wj = w_ref[...].reshape(POOL_GROUP, POOL_GROUP)
        o = jnp.dot(a_ref[...].astype(BF16), wj, preferred_element_type=F32) + b_ref[...]
        o_ref[...] = o
        y_ref[...] = _pool_gate(o, z_ref[...], s_ref[...]).astype(y_ref.dtype)

    tile = pl.BlockSpec((tm, POOL_GROUP), lambda i, j: (i, j))
    vec = pl.BlockSpec((1, POOL_GROUP), lambda i, j: (0, j))
    return pl.pallas_call(
        body, name=name, grid=(m // tm, ng),
        in_specs=[tile,
                  pl.BlockSpec((N_CHIPS, None, POOL_GROUP // N_CHIPS, POOL_GROUP), lambda i, j: (0, j, 0, 0)),
                  vec, pl.BlockSpec((tm, POOL_GROUP), lambda i, j: (i, ng + j)), vec],
        out_specs=[tile, tile],
        out_shape=[jax.ShapeDtypeStruct((m, ng * POOL_GROUP), F32), jax.ShapeDtypeStruct((m, ng * POOL_GROUP), BF16)],
        compiler_params=_params("parallel", "parallel"))(a, w, bias, p, scale)


def grp_nt(g, w, *, name):
    m = g.shape[0]
    tm = _tile(m, 1088, 16)
    ng = len(POOL_WINDOWS)

    def body(g_ref, w_ref, o_ref):
        wj = w_ref[...].reshape(POOL_GROUP, POOL_GROUP)
        o_ref[...] = lax.dot_general(g_ref[...].astype(BF16), wj, (((1,), (1,)), ((), ())),
                                     preferred_element_type=F32)

    return pl.pallas_call(
        body, name=name, grid=(m // tm, ng),
        in_specs=[pl.BlockSpec((tm, POOL_GROUP), lambda i, j: (i, j)),
                  pl.BlockSpec((N_CHIPS, None, POOL_GROUP // N_CHIPS, POOL_GROUP), lambda i, j: (0, j, 0, 0))],
        out_specs=pl.BlockSpec((tm, POOL_GROUP), lambda i, j: (i, j)),
        out_shape=jax.ShapeDtypeStruct((m, ng * POOL_GROUP), F32),
        compiler_params=_params("parallel", "parallel"))(g, w)


def grp_tn(a, g, *, name):
    m = a.shape[0]
    tm = _tile(m, 1088, 16)
    ng = len(POOL_WINDOWS)
    rows = POOL_GROUP // N_CHIPS

    def body(a_ref, g_ref, o_ref):
        mm = pl.program_id(1)
        part = lax.dot_general(a_ref[...].astype(BF16), g_ref[...].astype(BF16), (((0,), (0,)), ((), ())),
                               preferred_element_type=F32).reshape(N_CHIPS, rows, POOL_GROUP)

        @pl.when(mm == 0)
        def _():
            o_ref[...] = part

        @pl.when(mm > 0)
        def _():
            o_ref[...] += part

    return pl.pallas_call(
        body, name=name, grid=(ng, m // tm),
        in_specs=[pl.BlockSpec((tm, POOL_GROUP), lambda j, mm: (mm, j)),
                  pl.BlockSpec((tm, POOL_GROUP), lambda j, mm: (mm, j))],
        out_specs=pl.BlockSpec((N_CHIPS, None, rows, POOL_GROUP), lambda j, mm: (0, j, 0, 0)),
        out_shape=jax.ShapeDtypeStruct((N_CHIPS, ng, rows, POOL_GROUP), F32),
        compiler_params=_params("parallel", "arbitrary"))(a, g)


def final_loss(h, target, gain, first, last, *, name):
    m, d = h.shape
    tr = m // 16

    def body(h_ref, t_ref, g_ref, dh_ref, dg_ref, sq_ref):
        i = pl.program_id(0)
        y, vjp = jax.vjp(_rms, h_ref[...], g_ref[...])
        pos = i * tr + lax.broadcasted_iota(jnp.int32, (tr, 1), 0)
        diff = jnp.where((pos >= first) & (pos < last), y - t_ref[...], 0.0)
        dh_ref[...], dg = vjp(diff / d)
        sq = _colsum(diff * diff)

        @pl.when(i == 0)
        def _():
            dg_ref[...] = dg
            sq_ref[...] = sq

        @pl.when(i > 0)
        def _():
            dg_ref[...] += dg
            sq_ref[...] += sq

    tile, vec = pl.BlockSpec((tr, d), lambda i: (i, 0)), pl.BlockSpec((1, d), lambda i: (0, 0))
    return pl.pallas_call(
        body, name=name, grid=(m // tr,), in_specs=[tile, tile, vec], out_specs=[tile, vec, vec],
        out_shape=[jax.ShapeDtypeStruct((m, d), F32), jax.ShapeDtypeStruct((1, d), F32),
                   jax.ShapeDtypeStruct((1, d), F32)],
        compiler_params=_params("arbitrary"))(h, target, gain)


def _rms(h, g):
    return h * lax.rsqrt(jnp.mean(h * h, axis=-1, keepdims=True) + EPS) * g


def _colsum(v):
    return jnp.sum(v, axis=0, keepdims=True)


def _s5_gate(y0, q, z):
    yg = jax.nn.gelu(y0)
    return yg * jax.nn.sigmoid(q) * jax.nn.silu(z)


def _pool_gate(o, z, scale):
    return o * scale * jax.nn.silu(z)


def _shift_rows(v, d, down):
    t = v.shape[0]
    row = lax.broadcasted_iota(jnp.int32, v.shape, 0)
    if down:
        return jnp.where(row >= d, pltpu.roll(v, d, 0), 0.0)
    return jnp.where(row < t - d, pltpu.roll(v, t - d, 0), 0.0)


def _cmul(ar, ai, br, bi):
    return ar * br - ai * bi, ar * bi + ai * br


def _scan(xr, xi, ar, ai, cr, ci, down):
    t, n = xr.shape
    nb = t // SUBLANES
    pw = [(ar, ai)]
    for _ in range(SUBLANES - 1):
        pw.append(_cmul(*pw[-1], ar, ai))
    sub = lax.broadcasted_iota(jnp.int32, (SUBLANES, n), 0)

    def table(exps):
        tr, ti = jnp.zeros((SUBLANES, n), F32), jnp.zeros((SUBLANES, n), F32)
        for r, e in enumerate(exps):
            if e:
                tr, ti = jnp.where(sub == r, pw[e - 1][0], tr), jnp.where(sub == r, pw[e - 1][1], ti)
        return tr[None], ti[None]

    sr, si = xr.reshape(nb, SUBLANES, n), xi.reshape(nb, SUBLANES, n)
    d = 1
    while d < SUBLANES:
        mr, mi = table([d if (r >= d if down else r < SUBLANES - d) else 0 for r in range(SUBLANES)])
        turn = d if down else SUBLANES - d
        hr, hi = pltpu.roll(sr, turn, 1), pltpu.roll(si, turn, 1)
        sr, si = sr + mr * hr - mi * hi, si + mr * hi + mi * hr
        d *= 2
    edge = SUBLANES - 1 if down else 0
    qr, qi = table([r + 1 if down else SUBLANES - r for r in range(SUBLANES)])
    qr, qi = qr[0], qi[0]
    in_r, in_i = jnp.broadcast_to(cr, (SUBLANES, n)), jnp.broadcast_to(ci, (SUBLANES, n))
    out_r, out_i = [None] * nb, [None] * nb
    for b in (range(nb) if down else reversed(range(nb))):
        out_r[b] = sr[b] + qr * in_r - qi * in_i
        out_i[b] = si[b] + qr * in_i + qi * in_r
        in_r = jnp.broadcast_to(out_r[b][edge:edge + 1, :], (SUBLANES, n))
        in_i = jnp.broadcast_to(out_i[b][edge:edge + 1, :], (SUBLANES, n))
    return jnp.concatenate(out_r, axis=0), jnp.concatenate(out_i, axis=0), in_r[0:1, :], in_i[0:1, :]


def s5_fwd(uz, bblk, cblk, ar, ai, dskip, *, name, job=None):
    lp = uz.shape[0]
    t = _tile(lp, SCAN_CHUNK, 16)
    nst = S5_GROUPS * S5_STATE

    def body(u_ref, b_ref, c_ref, ar_ref, ai_ref, d_ref, y_ref, yg_ref, sr_ref, si_ref, cr, ci):
        step = pl.program_id(1)

        @pl.when(step == 0)
        def _():
            cr[...] = jnp.zeros_like(cr)
            ci[...] = jnp.zeros_like(ci)

        u = u_ref[...]
        a_r, a_i = ar_ref[...], ai_ref[...]
        x = jnp.dot(u.astype(BF16), b_ref[...].astype(BF16), preferred_element_type=F32)
        sr, si, cr[...], ci[...] = _scan(x[:, :STATE_BLOCK], x[:, STATE_BLOCK:], a_r, a_i, cr[...], ci[...], True)
        sr_ref[...] = sr
        si_ref[...] = si
        s = jnp.concatenate([sr, si], axis=1).astype(BF16)
        y0 = lax.dot_general(s, c_ref[...].astype(BF16), (((1,), (1,)), ((), ())),
                             preferred_element_type=F32) + d_ref[...] * u
        y_ref[...] = y0
        yg_ref[...] = jax.nn.gelu(y0).astype(yg_ref.dtype)

    return _call(
        body, name=name, grid=(S5_BLOCKS, lp // t),
        in_specs=[pl.BlockSpec((t, LANE_BLOCK), lambda b, c: (c, b)),
                  pl.BlockSpec((None, LANE_BLOCK, 2 * STATE_BLOCK), lambda b, c: (b, 0, 0)),
                  pl.BlockSpec((None, LANE_BLOCK, 2 * STATE_BLOCK), lambda b, c: (b, 0, 0)),
                  pl.BlockSpec((1, STATE_BLOCK), lambda b, c: (0, b)),
                  pl.BlockSpec((1, STATE_BLOCK), lambda b, c: (0, b)),
                  pl.BlockSpec((1, LANE_BLOCK), lambda b, c: (0, b))],
        out_specs=[pl.BlockSpec((t, LANE_BLOCK), lambda b, c: (c, b)),
                   pl.BlockSpec((t, LANE_BLOCK), lambda b, c: (c, b)),
                   pl.BlockSpec((t, STATE_BLOCK), lambda b, c: (c, b)),
                   pl.BlockSpec((t, STATE_BLOCK), lambda b, c: (c, b))],
        out_shape=[jax.ShapeDtypeStruct((lp, D_MODEL), F32), jax.ShapeDtypeStruct((lp, D_MODEL), BF16),
                   jax.ShapeDtypeStruct((lp, nst), F32), jax.ShapeDtypeStruct((lp, nst), F32)],
        scratch=[pltpu.VMEM((1, STATE_BLOCK), F32), pltpu.VMEM((1, STATE_BLOCK), F32)],
        args=(uz, bblk, cblk, ar, ai, dskip), semantics=("parallel", "arbitrary"), job=job)


def s5_bwd(dy_direct, dyg, y0, uz, sr, si, bblk, cblk, ar, ai, dskip, dp, *, name, job=None):
    lp = uz.shape[0]
    t = _tile(lp, SCAN_CHUNK, 16)
    nch = lp // t
    nst = S5_GROUPS * S5_STATE

    def body(dyd_ref, dyg_ref, y0_ref, u_ref, sr_ref, si_ref, b_ref, c_ref, ar_ref, ai_ref, d_ref, _, du_ref,
             db_ref, dc_ref, dar_ref, dai_ref, dd_ref, cr, ci):
        step = pl.program_id(1)

        @pl.when(step == 0)
        def _():
            cr[...] = jnp.zeros_like(cr)
            ci[...] = jnp.zeros_like(ci)

        dy = dyd_ref[...] + jax.vjp(jax.nn.gelu, y0_ref[...])[1](dyg_ref[...])[0]
        u = u_ref[...]
        dyb, ub = dy.astype(BF16), u.astype(BF16)
        a_r, a_i = ar_ref[...], -ai_ref[...]
        g = jnp.dot(dyb, c_ref[...].astype(BF16), preferred_element_type=F32)
        gr, gi = g[:, :STATE_BLOCK], g[:, STATE_BLOCK:]
        nxt_r, nxt_i = cr[...], ci[...]
        last = lax.broadcasted_iota(jnp.int32, gr.shape, 0) == t - 1
        lr, li, cr[...], ci[...] = _scan(gr, gi, a_r, a_i, nxt_r, nxt_i, False)
        nr = _shift_rows(lr, 1, False) + jnp.where(last, nxt_r, 0.0)
        ni = _shift_rows(li, 1, False) + jnp.where(last, nxt_i, 0.0)
        s_r, s_i = sr_ref[...], si_ref[...]
        dar = _colsum(s_r * nr + s_i * ni)
        dai = _colsum(s_r * ni - s_i * nr)
        lam = jnp.concatenate([lr, li], axis=1).astype(BF16)
        sb = jnp.concatenate([s_r, s_i], axis=1).astype(BF16)
        dc = lax.dot_general(dyb, sb, (((0,), (0,)), ((), ())), preferred_element_type=F32)
        db = lax.dot_general(ub, lam, (((0,), (0,)), ((), ())), preferred_element_type=F32)
        du_ref[...] = (lax.dot_general(lam, b_ref[...].astype(BF16), (((1,), (1,)), ((), ())),
                                       preferred_element_type=F32) + d_ref[...] * dy).astype(du_ref.dtype)
        dd = _colsum(dy * u)

        @pl.when(step == 0)
        def _():
            db_ref[...] = db
            dc_ref[...] = dc
            dar_ref[...] = dar
            dai_ref[...] = dai
            dd_ref[...] = dd

        @pl.when(step > 0)
        def _():
            db_ref[...] += db
            dc_ref[...] += dc
            dar_ref[...] += dar
            dai_ref[...] += dai
            dd_ref[...] += dd

    rev = lambda b, c: (nch - 1 - c, b)
    return _call(
        body, name=name, grid=(S5_BLOCKS, nch),
        in_specs=[pl.BlockSpec((t, LANE_BLOCK), rev), pl.BlockSpec((t, LANE_BLOCK), rev),
                  pl.BlockSpec((t, LANE_BLOCK), rev), pl.BlockSpec((t, LANE_BLOCK), rev),
                  pl.BlockSpec((t, STATE_BLOCK), rev), pl.BlockSpec((t, STATE_BLOCK), rev),
                  pl.BlockSpec((None, LANE_BLOCK, 2 * STATE_BLOCK), lambda b, c: (b, 0, 0)),
                  pl.BlockSpec((None, LANE_BLOCK, 2 * STATE_BLOCK), lambda b, c: (b, 0, 0)),
                  pl.BlockSpec((1, STATE_BLOCK), lambda b, c: (0, b)),
                  pl.BlockSpec((1, STATE_BLOCK), lambda b, c: (0, b)),
                  pl.BlockSpec((1, LANE_BLOCK), lambda b, c: (0, b)),
                  pl.BlockSpec(memory_space=pl.ANY)],
        out_specs=[pl.BlockSpec((t, LANE_BLOCK), rev),
                   pl.BlockSpec((None, LANE_BLOCK, 2 * STATE_BLOCK), lambda b, c: (b, 0, 0)),
                   pl.BlockSpec((None, LANE_BLOCK, 2 * STATE_BLOCK), lambda b, c: (b, 0, 0)),
                   pl.BlockSpec((1, STATE_BLOCK), lambda b, c: (0, b)),
                   pl.BlockSpec((1, STATE_BLOCK), lambda b, c: (0, b)),
                   pl.BlockSpec((1, LANE_BLOCK), lambda b, c: (0, b))],
        out_shape=[jax.ShapeDtypeStruct(dp.shape, dp.dtype),
                   jax.ShapeDtypeStruct((S5_BLOCKS, LANE_BLOCK, 2 * STATE_BLOCK), F32),
                   jax.ShapeDtypeStruct((S5_BLOCKS, LANE_BLOCK, 2 * STATE_BLOCK), F32),
                   jax.ShapeDtypeStruct((1, nst), F32), jax.ShapeDtypeStruct((1, nst), F32),
                   jax.ShapeDtypeStruct((1, D_MODEL), F32)],
        scratch=[pltpu.VMEM((1, STATE_BLOCK), F32), pltpu.VMEM((1, STATE_BLOCK), F32)],
        aliases={11: 0}, args=(dy_direct, dyg, y0, uz, sr, si, bblk, cblk, ar, ai, dskip, dp),
        semantics=("parallel", "arbitrary"), job=job)


def _s5_prep(lam_re, lam_im, log_dt, b_re, b_im, c_re, c_im):
    dt = jnp.exp(log_dt)
    mag = jnp.exp(lam_re * dt)
    ar = mag * jnp.cos(lam_im * dt)
    ai = mag * jnp.sin(lam_im * dt)
    den = lam_re * lam_re + lam_im * lam_im
    kr = ((ar - 1.0) * lam_re + ai * lam_im) / den
    ki = (ai * lam_re - (ar - 1.0) * lam_im) / den
    bbr = kr[:, None, :] * b_re - ki[:, None, :] * b_im
    bbi = kr[:, None, :] * b_im + ki[:, None, :] * b_re
    rows = S5_GROUPS * S5_GROUP
    expand = (lax.broadcasted_iota(jnp.int32, (S5_STATE, STATE_BLOCK), 1) % S5_STATE
              == lax.broadcasted_iota(jnp.int32, (S5_STATE, STATE_BLOCK), 0)).astype(F32)
    own = ((lax.broadcasted_iota(jnp.int32, (rows, STATE_BLOCK), 0) // S5_GROUP) % GROUPS_PER_BLOCK
           == lax.broadcasted_iota(jnp.int32, (rows, STATE_BLOCK), 1) // S5_STATE).astype(F32)

    def blocks(v):
        wide = jnp.dot(v.reshape(rows, S5_STATE), expand, precision=lax.Precision.HIGHEST,
                       preferred_element_type=F32)
        return (wide * own).reshape(S5_BLOCKS, LANE_BLOCK, STATE_BLOCK)

    btab = jnp.concatenate([blocks(bbr), blocks(bbi)], axis=2)
    ctab = jnp.concatenate([blocks(c_re), -blocks(c_im)], axis=2)
    return ar, ai, btab, ctab


_S5_PREP_OUT = [(S5_GROUPS, S5_STATE), (S5_GROUPS, S5_STATE), (S5_BLOCKS, LANE_BLOCK, 2 * STATE_BLOCK),
                (S5_BLOCKS, LANE_BLOCK, 2 * STATE_BLOCK)]


def s5_prep(params, *, name):
    def body(*refs):
        for ref, val in zip(refs[7:], _s5_prep(*[r[...] for r in refs[:7]]), strict=True):
            ref[...] = val

    return pl.pallas_call(body, name=name, out_shape=[jax.ShapeDtypeStruct(s, F32) for s in _S5_PREP_OUT],
                          compiler_params=pltpu.CompilerParams(vmem_limit_bytes=VMEM_LIMIT))(*params)


def s5_prep_bwd(params, cotangents, *, name):
    def body(*refs):
        grads = jax.vjp(_s5_prep, *[r[...] for r in refs[:7]])[1](tuple(r[...] for r in refs[7:11]))
        for ref, val in zip(refs[11:], grads, strict=True):
            ref[...] = val

    return pl.pallas_call(body, name=name, out_shape=[jax.ShapeDtypeStruct(p.shape, F32) for p in params],
                          compiler_params=pltpu.CompilerParams(vmem_limit_bytes=VMEM_LIMIT))(*params, *cotangents)


def _silu_grad(z):
    s = jax.nn.sigmoid(z)
    return s * (1.0 + z * (1.0 - s))


def conv_fwd(p, conv_w, conv_b, *, name):
    lp = p.shape[0]
    tr = lp // 16
    e = CONV_E
    hb = tr // 8

    def body(p_ref, cgh_ref, vh_ref, w_ref, b_ref, o_ref):
        i = pl.program_id(0)
        bg, cg, v, z = (p_ref[:, k * e:(k + 1) * e] for k in range(4))
        hc = cg * v
        halo = jnp.where(i > 0, cgh_ref[...] * vh_ref[...], 0.0)
        ext = jnp.concatenate([halo, hc], axis=0)
        w = w_ref[...]
        conv = (w[0:1] * pltpu.roll(ext, 2, 0)[8:] + w[1:2] * pltpu.roll(ext, 1, 0)[8:] + w[2:3] * hc
                + b_ref[...])
        o_ref[...] = (bg * conv * jax.nn.silu(z)).astype(o_ref.dtype)

    prev = lambda col: (lambda i: (jnp.maximum(i * hb - 1, 0), col))
    return pl.pallas_call(
        body, name=name, grid=(lp // tr,),
        in_specs=[pl.BlockSpec((tr, 4 * e), lambda i: (i, 0)),
                  pl.BlockSpec((8, e), prev(1)), pl.BlockSpec((8, e), prev(2)),
                  pl.BlockSpec((3, e), lambda i: (0, 0)), pl.BlockSpec((1, e), lambda i: (0, 0))],
        out_specs=pl.BlockSpec((tr, e), lambda i: (i, 0)),
        out_shape=jax.ShapeDtypeStruct((lp, e), BF16),
        compiler_params=_params("parallel"))(p, p, p, conv_w, conv_b)


def conv_bwd(dh, w_out, p, conv_w, conv_b, *, name):
    lp = p.shape[0]
    tr = lp // 16
    e = CONV_E
    d = dh.shape[1]
    hb = tr // 8
    nt = lp // tr
    width = 512

    def body(dh_ref, p_ref, cgh_ref, vh_ref, dhn_ref, bgn_ref, zn_ref, wo_ref, w_ref, b_ref, dp_ref, dw_ref, db_ref):
        i = pl.program_id(0)
        rows, rows_n = dh_ref[...].astype(BF16), dhn_ref[...].astype(BF16)
        for c0 in range(0, e, width):
            cols = slice(c0, c0 + width)
            wo = wo_ref[cols, :]
            dy = lax.dot_general(rows, wo, (((1,), (1,)), ((), ())), preferred_element_type=F32)
            dy_n = lax.dot_general(rows_n, wo, (((1,), (1,)), ((), ())), preferred_element_type=F32)
            bg, cg, v, z = (p_ref[:, k * e + c0:k * e + c0 + width] for k in range(4))
            w = w_ref[:, cols]
            hc = cg * v
            halo = jnp.where(i > 0, cgh_ref[:, cols] * vh_ref[:, cols], 0.0)
            ext = jnp.concatenate([halo, hc], axis=0)
            hc2, hc1 = pltpu.roll(ext, 2, 0)[8:], pltpu.roll(ext, 1, 0)[8:]
            yc = w[0:1] * hc2 + w[1:2] * hc1 + w[2:3] * hc + b_ref[:, cols]
            sz = jax.nn.silu(z)
            dyc = dy * bg * sz
            dyc_n = jnp.where(i < nt - 1, dy_n * bgn_ref[:, cols] * jax.nn.silu(zn_ref[:, cols]), 0.0)
            ext_n = jnp.concatenate([dyc, dyc_n], axis=0)
            n_rows = tr + 8
            dhc = (w[2:3] * dyc + w[1:2] * pltpu.roll(ext_n, n_rows - 1, 0)[:tr]
                   + w[0:1] * pltpu.roll(ext_n, n_rows - 2, 0)[:tr])
            for k, val in enumerate((dy * yc * sz, dhc * v, dhc * cg, dy * bg * yc * _silu_grad(z))):
                dp_ref[:, k * e + c0:k * e + c0 + width] = val.astype(dp_ref.dtype)
            dw = jnp.concatenate([_colsum(dyc * hc2), _colsum(dyc * hc1), _colsum(dyc * hc)], axis=0)
            db = _colsum(dyc)

            @pl.when(i == 0)
            def _(cols=cols, dw=dw, db=db):
                dw_ref[:, cols] = dw
                db_ref[:, cols] = db

            @pl.when(i > 0)
            def _(cols=cols, dw=dw, db=db):
                dw_ref[:, cols] += dw
                db_ref[:, cols] += db

    prev = lambda col: (lambda i: (jnp.maximum(i * hb - 1, 0), col))
    nxt = lambda col: (lambda i: (jnp.minimum((i + 1) * hb, lp // 8 - 1), col))
    return pl.pallas_call(
        body, name=name, grid=(nt,),
        in_specs=[pl.BlockSpec((tr, d), lambda i: (i, 0)), pl.BlockSpec((tr, 4 * e), lambda i: (i, 0)),
                  pl.BlockSpec((8, e), prev(1)), pl.BlockSpec((8, e), prev(2)),
                  pl.BlockSpec((8, d), nxt(0)), pl.BlockSpec((8, e), nxt(0)), pl.BlockSpec((8, e), nxt(3)),
                  pl.BlockSpec((None, e, d), lambda i: (0, 0, 0)),
                  pl.BlockSpec((3, e), lambda i: (0, 0)), pl.BlockSpec((1, e), lambda i: (0, 0))],
        out_specs=[pl.BlockSpec((tr, 4 * e), lambda i: (i, 0)), pl.BlockSpec((3, e), lambda i: (0, 0)),
                   pl.BlockSpec((1, e), lambda i: (0, 0))],
        out_shape=[jax.ShapeDtypeStruct((lp, 4 * e), BF16), jax.ShapeDtypeStruct((3, e), F32),
                   jax.ShapeDtypeStruct((1, e), F32)],
        compiler_params=_params("arbitrary"))(dh, p, p, p, dh, p, p, w_out, conv_w, conv_b)


def _pool_counts(i, tr, rows, offset, w):
    t = (i * tr + offset + 1 + lax.broadcasted_iota(jnp.int32, (rows, 1), 0)).astype(F32)
    return jnp.minimum(t, float(w))


def pool_fwd(p, *, name):
    lp = p.shape[0]
    tr = lp // 16
    h = POOL_HALO
    hb = tr // h

    def body(u_ref, uh_ref, o_ref):
        i = pl.program_id(0)
        u = u_ref[...]
        ext = jnp.concatenate([jnp.where(i > 0, uh_ref[...], 0.0), u], axis=0)
        for k, w in enumerate(POOL_WINDOWS):
            cols = slice(k * POOL_GROUP, (k + 1) * POOL_GROUP)
            s = ext[:, cols]
            d = 1
            while d < w:
                s = s + pltpu.roll(s, d, 0)
                d *= 2
            o_ref[:, cols] = (s[h:] / _pool_counts(i, tr, tr, 0, w) - u[:, cols]).astype(o_ref.dtype)

    return pl.pallas_call(
        body, name=name, grid=(lp // tr,),
        in_specs=[pl.BlockSpec((tr, POOL_E), lambda i: (i, 0)),
                  pl.BlockSpec((h, POOL_E), lambda i: (jnp.maximum(i * hb - 1, 0), 0))],
        out_specs=pl.BlockSpec((tr, POOL_E), lambda i: (i, 0)),
        out_shape=jax.ShapeDtypeStruct((lp, POOL_E), BF16),
        compiler_params=_params("parallel"))(p, p)


def pool_bwd(dmix, dp, *, name):
    lp = dmix.shape[0]
    tr = lp // 16
    h = POOL_HALO
    hb = tr // h
    nt = lp // tr

    def body(dm_ref, dmn_ref, _, du_ref):
        i = pl.program_id(0)
        dm = dm_ref[...]
        dmn = jnp.where(i < nt - 1, dmn_ref[...], 0.0)
        n_rows = tr + h
        for k, w in enumerate(POOL_WINDOWS):
            cols = slice(k * POOL_GROUP, (k + 1) * POOL_GROUP)
            s = jnp.concatenate([dm[:, cols] / _pool_counts(i, tr, tr, 0, w),
                                 dmn[:, cols] / _pool_counts(i, tr, h, tr, w)], axis=0)
            d = 1
            while d < w:
                s = s + pltpu.roll(s, n_rows - d, 0)
                d *= 2
            du_ref[:, cols] = (s[:tr] - dm[:, cols]).astype(du_ref.dtype)

    return pl.pallas_call(
        body, name=name, grid=(nt,),
        in_specs=[pl.BlockSpec((tr, POOL_E), lambda i: (i, 0)),
                  pl.BlockSpec((h, POOL_E), lambda i: (jnp.minimum((i + 1) * hb, lp // h - 1), 0)),
                  pl.BlockSpec(memory_space=pl.ANY)],
        out_specs=pl.BlockSpec((tr, POOL_E), lambda i: (i, 0)),
        out_shape=jax.ShapeDtypeStruct(dp.shape, dp.dtype),
        input_output_aliases={2: 0},
        compiler_params=_params("parallel"))(dmix, dmix, dp)


def _adamw_math(w, g, m, v):
    nm = ADAM_B1 * m + (1.0 - ADAM_B1) * g
    nv = ADAM_B2 * v + (1.0 - ADAM_B2) * jnp.square(g)
    m_hat = nm / (1.0 - ADAM_B1 ** ADAM_STEP)
    v_hat = nv / (1.0 - ADAM_B2 ** ADAM_STEP)
    return -ADAM_LR * (m_hat / (jnp.sqrt(v_hat) + ADAM_EPS) + ADAM_WD * w), nm, nv


def adamw_many(quads, *, name, steps=8, job=None):
    n = len(quads)

    def spec(shape):
        return pl.BlockSpec((shape[0] // steps,) + shape[1:], lambda i, nd=len(shape): (i,) + (0,) * (nd - 1))

    def body(*refs):
        for q in range(n):
            w_ref, g_ref, m_ref, v_ref = refs[4 * q:4 * q + 4]
            d_ref, nm_ref, nv_ref = refs[4 * n + 3 * q:4 * n + 3 * q + 3]
            d_ref[...], nm_ref[...], nv_ref[...] = _adamw_math(w_ref[...], g_ref[...], m_ref[...], v_ref[...])

    res = _call(
        body, name=name, grid=(steps,), in_specs=[spec(quad[0].shape) for quad in quads for _ in range(4)],
        out_specs=[spec(quad[0].shape) for quad in quads for _ in range(3)],
        out_shape=[jax.ShapeDtypeStruct(quad[0].shape, F32) for quad in quads for _ in range(3)],
        args=[a for quad in quads for a in quad], semantics=("parallel",), job=job)
    outs = res if job is None else res[0]
    triples = [tuple(outs[3 * q:3 * q + 3]) for q in range(n)]
    return triples if job is None else (triples, res[1])


WIRE_WIDTH = 1024
WIRE_ROWS = 1024


def _wire_plan(shapes):
    starts, row = [], 0
    for s in shapes:
        r, c = (1, s[0]) if len(s) == 1 else s
        starts.append(row)
        row += -(-(r * max(1, c // WIRE_WIDTH)) // 8) * 8
    assert row <= WIRE_ROWS, row
    return starts


def _wire_cells(shape):
    if len(shape) == 1:
        n = shape[0]
        if n <= WIRE_WIDTH:
            return [(0, n, (slice(None),))]
        return [(h, WIRE_WIDTH, (slice(h * WIRE_WIDTH, (h + 1) * WIRE_WIDTH),)) for h in range(n // WIRE_WIDTH)]
    r, c = shape
    if c <= WIRE_WIDTH:
        return [(None, c, (slice(None), slice(None)))]
    per = c // WIRE_WIDTH
    return [(j * per + h, WIRE_WIDTH, (j, slice(h * WIRE_WIDTH, (h + 1) * WIRE_WIDTH)))
            for j in range(r) for h in range(per)]


def pack_small(arrays, *, name):
    shapes = [a.shape for a in arrays]
    starts = _wire_plan(shapes)

    def body(*refs):
        out = refs[-1]
        out[...] = jnp.zeros_like(out)
        for ref, shape, row0 in zip(refs[:-1], shapes, starts, strict=True):
            for off, cols, idx in _wire_cells(shape):
                if off is None:
                    out[row0:row0 + shape[0], 0:cols] = ref[idx]
                else:
                    out[row0 + off, 0:cols] = ref[idx]

    return pl.pallas_call(body, name=name, out_shape=jax.ShapeDtypeStruct((WIRE_ROWS, WIRE_WIDTH), F32),
                          compiler_params=pltpu.CompilerParams(vmem_limit_bytes=VMEM_LIMIT))(*arrays)


def update_small(packed, starts, triples, shard_cells, chip, *, name):
    n = len(triples)

    def body(chip_ref, p_ref, *refs):
        k = chip_ref[0]
        ins, outs = refs[:3 * n], refs[3 * n:]

        def cut(row0, rows, cols):
            c0, c1 = cols
            if isinstance(rows, int):
                return p_ref[row0 + rows, c0:c1]
            return p_ref[row0 + rows[0]:row0 + rows[1], c0:c1]

        for q in range(n):
            w_ref, m_ref, v_ref = ins[3 * q:3 * q + 3]
            g_ref, d_ref, nm_ref, nv_ref = outs[4 * q:4 * q + 4]
            shape = w_ref.shape
            if shard_cells[q] is None:
                pieces = [(cut(starts[q], (0, shape[0]) if off is None else off, (0, cols)), idx)
                          for off, cols, idx in _wire_cells(shape)]
            else:
                by_chip = [shard_cells[q](kk) for kk in range(N_CHIPS)]
                pieces = []
                for i, (_, _, idx) in enumerate(by_chip[0]):
                    g = cut(starts[q], *by_chip[0][i][:2])
                    for kk in range(1, N_CHIPS):
                        g = jnp.where(k == kk, cut(starts[q], *by_chip[kk][i][:2]), g)
                    pieces.append((g, idx))
            for g, idx in pieces:
                g_ref[idx] = g
                d_ref[idx], nm_ref[idx], nv_ref[idx] = _adamw_math(w_ref[idx], g, m_ref[idx], v_ref[idx])

    whole = lambda a: pl.BlockSpec(a.shape, lambda i, c_ref, nd=a.ndim: (0,) * nd)
    flat = [a for t in triples for a in t]
    out_shape = [jax.ShapeDtypeStruct(t[0].shape, F32) for t in triples for _ in range(4)]
    res = pl.pallas_call(
        body, name=name,
        grid_spec=pltpu.PrefetchScalarGridSpec(
            num_scalar_prefetch=1, grid=(1,),
            in_specs=[whole(packed)] + [whole(a) for a in flat],
            out_specs=[pl.BlockSpec(s.shape, lambda i, c_ref, nd=len(s.shape): (0,) * nd) for s in out_shape]),
        out_shape=out_shape,
        compiler_params=_params("arbitrary"))(chip.reshape(1).astype(jnp.int32), packed, *flat)
    return [tuple(res[4 * q:4 * q + 4]) for q in range(n)]


ANY = pl.BlockSpec(memory_space=pl.ANY)


def _place():
    x, y, c = lax.axis_index("x"), lax.axis_index("y"), lax.axis_index("c")
    return x, y, c, [(1 - x, y), (x, 1 - y), (1 - x, 1 - y)]


DMA_PIECE_BYTES = 512 * 1024


def _pieces(src, dst):
    shape = src.shape
    rows, cols = shape[-2], shape[-1]
    item = jnp.dtype(src.dtype).itemsize
    unit = 32 // item
    want = max(1, (rows * cols * item) // DMA_PIECE_BYTES)
    n = 1
    if rows % unit == 0:
        n = max(d for d in range(1, rows // unit + 1) if (rows // unit) % d == 0 and d <= want)
    size = rows // n
    out = []
    for lead in (range(shape[0]) if len(shape) == 3 else [None]):
        for i in range(n):
            sel = (pl.ds(i * size, size), slice(None)) if lead is None else (lead, pl.ds(i * size, size), slice(None))
            out.append((src.at[sel], dst.at[sel]))
    return out


def _send(src, dst, send_sem, recv_sem, peer, start=True):
    if start:
        for s, d in _pieces(src, dst):
            pltpu.make_async_remote_copy(src_ref=s, dst_ref=d, send_sem=send_sem, recv_sem=recv_sem,
                                         device_id=peer, device_id_type=MESH).start()
    return pltpu.make_async_remote_copy(src_ref=src, dst_ref=dst, send_sem=send_sem, recv_sem=recv_sem,
                                        device_id=peer, device_id_type=MESH)


def run_job(job, *, name):
    n_in, n_out = len(job.arrays), len(job.out_shapes)

    def body(*refs):
        job.fn(refs[:n_in], refs[n_in:n_in + n_out], refs[n_in + n_out:], "both")

    return pl.pallas_call(body, name=name, in_specs=[ANY] * n_in, out_specs=[ANY] * n_out,
                          out_shape=job.out_shapes, scratch_shapes=job.sems,
                          input_output_aliases={a: a for a in job.aliased()})(*job.arrays)


def gather_chips(shards, *, name):
    n = len(shards)

    def body(*refs):
        ins, outs = refs[:n], refs[n:2 * n]
        send_sems, recv_sems = refs[2 * n:]
        x, y, c, chips = _place()
        k = 2 * x + y
        copies = []
        for a in range(n):
            for j, peer in enumerate([(px, py, c) for px, py in chips] + [(x, y, 1 - c)]):
                copies.append(_send(ins[a], outs[a].at[k], send_sems.at[a, j], recv_sems.at[a, j], peer))
        for cp in copies:
            cp.wait()

    return pl.pallas_call(
        body, name=name, in_specs=[ANY] * n, out_specs=[ANY] * n,
        out_shape=[jax.ShapeDtypeStruct((N_CHIPS,) + s.shape, s.dtype) for s in shards],
        scratch_shapes=[pltpu.SemaphoreType.DMA((n, 4)), pltpu.SemaphoreType.DMA((n, 4))])(*shards)


def gather_by_halves_job(shards):
    n = len(shards)

    def fn(ins, outs, sems, phase):
        send_sems, recv_sems = sems
        x, y, c, chips = _place()
        k = 2 * x + y
        sibling = (x, y, 1 - c)
        begin = phase != "finish"
        waits, arrivals = [], []
        for a in range(n):
            half = ins[a].shape[0] // 2
            waits.append(_send(ins[a], outs[a].at[k], send_sems.at[a, 6], recv_sems.at[a, 6], sibling, begin))
            mine = pl.ds(c * half, half)
            for j, (px, py) in enumerate(chips):
                arrivals.append(_send(ins[a].at[mine, :], outs[a].at[k, mine, :], send_sems.at[a, j],
                                      recv_sems.at[a, j], (px, py, c), begin))
        if phase == "start":
            return
        i = 0
        for a in range(n):
            half = ins[a].shape[0] // 2
            mine = pl.ds(c * half, half)
            for j, (px, py) in enumerate(chips):
                arrivals[i].wait_recv()
                landed = outs[a].at[2 * px + py, mine, :]
                waits.append(_send(landed, landed, send_sems.at[a, 3 + j], recv_sems.at[a, 3 + j], sibling))
                i += 1
        for cp in arrivals:
            cp.wait_send()
        for cp in waits:
            cp.wait()

    return Job(shards, [jax.ShapeDtypeStruct((N_CHIPS,) + s.shape, s.dtype) for s in shards],
               [pltpu.SemaphoreType.DMA((n, 7)), pltpu.SemaphoreType.DMA((n, 7))], fn)


def gather_halves_job(shards):
    n = len(shards)

    def fn(ins, outs, sems, phase):
        send_sems, recv_sems = sems
        x, y, c, chips = _place()
        k = 2 * x + y
        copies = []
        for a in range(n):
            half = ins[a].shape[0] // 2
            mine = pl.ds(c * half, half)
            copies.append(_send(ins[a], outs[a].at[k], send_sems.at[a, 3], recv_sems.at[a, 3], (x, y, 1 - c),
                                phase != "finish"))
            for j, (px, py) in enumerate(chips):
                copies.append(_send(ins[a].at[mine, :], outs[a].at[k, mine, :], send_sems.at[a, j],
                                    recv_sems.at[a, j], (px, py, c), phase != "finish"))
        if phase != "start":
            for cp in copies:
                cp.wait()

    return Job(shards, [jax.ShapeDtypeStruct((N_CHIPS,) + s.shape, s.dtype) for s in shards],
               [pltpu.SemaphoreType.DMA((n, 4)), pltpu.SemaphoreType.DMA((n, 4))], fn)


def forward_halves_job(gathered):
    n = len(gathered)

    def fn(ins, outs, sems, phase):
        send_sems, recv_sems = sems
        x, y, c, chips = _place()
        copies = []
        for a in range(n):
            half = outs[a].shape[1] // 2
            for j, (px, py) in enumerate(chips):
                landed = outs[a].at[2 * px + py, pl.ds(c * half, half), :]
                copies.append(_send(landed, landed, send_sems.at[a, j], recv_sems.at[a, j], (x, y, 1 - c),
                                    phase != "finish"))
        if phase != "start":
            for cp in copies:
                cp.wait()

    return Job(gathered, [jax.ShapeDtypeStruct(g.shape, g.dtype) for g in gathered],
               [pltpu.SemaphoreType.DMA((n, 3)), pltpu.SemaphoreType.DMA((n, 3))], fn, in_place=True)


def swap_job(grads):
    n = len(grads)

    def fn(ins, outs, sems, phase):
        send_sems, recv_sems = sems
        x, y, c, _ = _place()
        copies = []
        for a in range(n):
            half = ins[a].shape[1] // 2
            copies.append(_send(ins[a].at[:, pl.ds((1 - c) * half, half), :], outs[a], send_sems.at[a],
                                recv_sems.at[a], (x, y, 1 - c), phase != "finish"))
        if phase != "start":
            for cp in copies:
                cp.wait()

    return Job(grads, [jax.ShapeDtypeStruct((g.shape[0], g.shape[1] // 2, g.shape[2]), g.dtype) for g in grads],
               [pltpu.SemaphoreType.DMA((n,)), pltpu.SemaphoreType.DMA((n,))], fn)


def _chip_slot(s, k):
    rel = s ^ k
    return (rel >> 1) | ((rel & 1) << 1)


def sum_cores(g, theirs, ck, out_dtype, *, name):
    n, r, cols = g.shape
    half = r // 2
    tr = _tile(half, max(16, (1 << 19) // cols), 16)
    nb = half // tr

    def body(ck_ref, g_ref, t_ref, o_ref):
        o_ref[...] = (g_ref[...] + t_ref[...]).astype(o_ref.dtype)

    return pl.pallas_call(
        body, name=name,
        grid_spec=pltpu.PrefetchScalarGridSpec(
            num_scalar_prefetch=1, grid=(n, nb),
            in_specs=[pl.BlockSpec((None, tr, cols), lambda s, i, ck_ref: (s, ck_ref[0] * nb + i, 0)),
                      pl.BlockSpec((None, tr, cols), lambda s, i, ck_ref: (s, i, 0))],
            out_specs=pl.BlockSpec((None, tr, cols), lambda s, i, ck_ref: (_chip_slot(s, ck_ref[1]), i, 0))),
        out_shape=jax.ShapeDtypeStruct((n, half, cols), out_dtype),
        compiler_params=_params("parallel", "parallel"))(ck, g, theirs)


def scatter_job(parts):
    n = len(parts)

    def fn(ins, outs, sems, phase):
        send_sems, recv_sems = sems
        x, y, c, chips = _place()
        copies = []
        for a in range(n):
            for j, (px, py) in enumerate(chips):
                copies.append(_send(ins[a].at[1 + j], outs[a].at[j], send_sems.at[a, j], recv_sems.at[a, j],
                                    (px, py, c), phase != "finish"))
        if phase != "start":
            for cp in copies:
                cp.wait()

    return Job(parts, [jax.ShapeDtypeStruct((3,) + p.shape[1:], p.dtype) for p in parts],
               [pltpu.SemaphoreType.DMA((n, 3)), pltpu.SemaphoreType.DMA((n, 3))], fn)


def sum_chips(own, others, ck, *, name):
    _, r, cols = own.shape
    tr = _tile(r, max(16, (1 << 19) // cols), 16)

    def body(ck_ref, a_ref, b0_ref, b1_ref, b2_ref, o_ref):
        o_ref[...] = (a_ref[...].astype(F32) + b0_ref[...].astype(F32) + b1_ref[...].astype(F32)
                      + b2_ref[...].astype(F32))

    other = lambda j: pl.BlockSpec((None, tr, cols), lambda i, ck_ref: (j, i, 0))
    return pl.pallas_call(
        body, name=name,
        grid_spec=pltpu.PrefetchScalarGridSpec(
            num_scalar_prefetch=1, grid=(r // tr,),
            in_specs=[pl.BlockSpec((None, tr, cols), lambda i, ck_ref: (0, i, 0)), other(0), other(1), other(2)],
            out_specs=pl.BlockSpec((None, tr, cols), lambda i, ck_ref: (ck_ref[0], i, 0))),
        out_shape=jax.ShapeDtypeStruct((2, r, cols), F32),
        compiler_params=_params("parallel"))(ck, own, others, others, others)


def share_job(joined):
    n = len(joined)

    def fn(ins, outs, sems, phase):
        send_sems, recv_sems = sems
        x, y, c, _ = _place()
        copies = [_send(outs[a].at[c], outs[a].at[c], send_sems.at[a], recv_sems.at[a], (x, y, 1 - c),
                        phase != "finish") for a in range(n)]
        if phase != "start":
            for cp in copies:
                cp.wait()

    return Job(joined, [jax.ShapeDtypeStruct(j.shape, j.dtype) for j in joined],
               [pltpu.SemaphoreType.DMA((n,)), pltpu.SemaphoreType.DMA((n,))], fn, in_place=True)


def kernel(x, meta_tokens, norm0_g, l0_w_in, l0_lam_re, l0_lam_im, l0_log_dt, l0_b_re, l0_b_im, l0_c_re, l0_c_im, l0_d_skip, l0_w_glu, l0_b_glu, l0_w_out, norm1_g, l1_w_in, l1_conv_w, l1_conv_b, l1_w_out, norm2_g, l2_w_in, l2_w_grp, l2_b_grp, l2_scale, l2_w_out, norm3_g, l3_w_in, l3_lam_re, l3_lam_im, l3_log_dt, l3_b_re, l3_b_im, l3_c_re, l3_c_im, l3_d_skip, l3_w_glu, l3_b_glu, l3_w_out, final_g, loss_target, m_meta_tokens, m_norm0_g, m_l0_w_in, m_l0_lam_re, m_l0_lam_im, m_l0_log_dt, m_l0_b_re, m_l0_b_im, m_l0_c_re, m_l0_c_im, m_l0_d_skip, m_l0_w_glu, m_l0_b_glu, m_l0_w_out, m_norm1_g, m_l1_w_in, m_l1_conv_w, m_l1_conv_b, m_l1_w_out, m_norm2_g, m_l2_w_in, m_l2_w_grp, m_l2_b_grp, m_l2_scale, m_l2_w_out, m_norm3_g, m_l3_w_in, m_l3_lam_re, m_l3_lam_im, m_l3_log_dt, m_l3_b_re, m_l3_b_im, m_l3_c_re, m_l3_c_im, m_l3_d_skip, m_l3_w_glu, m_l3_b_glu, m_l3_w_out, m_final_g, v_meta_tokens, v_norm0_g, v_l0_w_in, v_l0_lam_re, v_l0_lam_im, v_l0_log_dt, v_l0_b_re, v_l0_b_im, v_l0_c_re, v_l0_c_im, v_l0_d_skip, v_l0_w_glu, v_l0_b_glu, v_l0_w_out, v_norm1_g, v_l1_w_in, v_l1_conv_w, v_l1_conv_b, v_l1_w_out, v_norm2_g, v_l2_w_in, v_l2_w_grp, v_l2_b_grp, v_l2_scale, v_l2_w_out, v_norm3_g, v_l3_w_in, v_l3_lam_re, v_l3_lam_im, v_l3_log_dt, v_l3_b_re, v_l3_b_im, v_l3_c_re, v_l3_c_im, v_l3_d_skip, v_l3_w_glu, v_l3_b_glu, v_l3_w_out, v_final_g):
    given = dict(locals())
    w = {n: given[n] for n in WEIGHTS}
    mom_m = {n: given["m_" + n] for n in WEIGHTS}
    mom_v = {n: given["v_" + n] for n in WEIGHTS}
    seq = x.shape[1]
    real = N_META + seq
    lp = -(-real // SCAN_ROWS) * SCAN_ROWS
    chip = 2 * lax.axis_index("x") + lax.axis_index("y")
    row = lambda a: a.reshape(1, -1)

    shard_bf16 = {n: (w[n].reshape(-1, w[n].shape[-1]) if n == 'l2_w_grp' else w[n]).astype(BF16) for n in BIG}
    layer = lambda i: [n for n in BIG if n.startswith("l%d_" % i)]
    over_ici = lambda names: gather_halves_job([shard_bf16[n] for n in names])
    full = {}

    def landed(names, arrays):
        for n, arr in zip(names, arrays, strict=True):
            if n.endswith('w_in'):
                full[n] = arr
            elif n == 'l2_w_grp':
                full[n] = arr.reshape(N_CHIPS, len(POOL_WINDOWS), POOL_GROUP // N_CHIPS, POOL_GROUP)
            else:
                full[n] = arr.reshape(1, -1, arr.shape[-1])

    landed(['l0_w_in'], run_job(gather_by_halves_job([shard_bf16['l0_w_in']]), name="gather_l0_w_in"))
    full.update(zip(SMALL_SHARDED, gather_chips([w[n] for n in SMALL_SHARDED], name="gather_small_shards"),
                    strict=True))
    meta_full = full['meta_tokens'].transpose(1, 0, 2).reshape(N_META, D_MODEL)
    conv_w_full = full['l1_conv_w'].transpose(1, 0, 2).reshape(3, CONV_E)
    b_grp_full = full['l2_b_grp'].transpose(1, 0, 2).reshape(len(POOL_WINDOWS), POOL_GROUP)

    h0 = jnp.concatenate([meta_full, x[0], jnp.zeros((lp - real, D_MODEL), F32)], axis=0)
    target = jnp.concatenate([jnp.zeros((N_META, D_MODEL), F32), loss_target[0],
                              jnp.zeros((lp - real, D_MODEL), F32)], axis=0)
    grads = {}
    saved = {}
    gip = lambda a: jnp.swapaxes(a, 1, 2)

    def s5_layer_fwd(i, h, first=False):
        pre = "l%d_" % i
        prm = [w[pre + 'lam_re'], w[pre + 'lam_im'], w[pre + 'log_dt'].reshape(-1, 1), gip(w[pre + 'b_re']),
               gip(w[pre + 'b_im']), w[pre + 'c_re'], w[pre + 'c_im']]
        a_re, a_im, bblk, cblk = s5_prep(prm, name=pre + "prep")
        tables = (a_re.reshape(1, -1), a_im.reshape(1, -1), bblk, cblk)
        ar, ai = tables[:2]
        gain = row(w["norm%d_g" % i])
        rest = [pre + 'w_glu', pre + 'w_out']
        if not first:
            uz, n = norm_mm_nn(h, gain, full[pre + 'w_in'], name=pre + "in")
            y0, yg, sr, si = s5_fwd(uz, bblk, cblk, ar, ai, row(w[pre + 'd_skip']), name=pre + "scan")
            q, y3 = glu_gate(yg, full[pre + 'w_glu'], row(w[pre + 'b_glu']), y0, uz, name=pre + "glu")
        else:
            (uz, n), halves = norm_mm_nn(h, gain, full[pre + 'w_in'], name=pre + "in", job=over_ici(rest))
            (y0, yg, sr, si), arrived = s5_fwd(
                uz, bblk, cblk, ar, ai, row(w[pre + 'd_skip']), name=pre + "scan",
                job=Job.both(over_ici(layer(1)), forward_halves_job(halves)))
            landed(rest, arrived[len(layer(1)):])
            (q, y3), whole = glu_gate(yg, full[pre + 'w_glu'], row(w[pre + 'b_glu']), y0, uz, name=pre + "glu",
                                      job=forward_halves_job(arrived[:len(layer(1))]))
            landed(layer(1), whole)
        h_new = mm_nn(y3, full[pre + 'w_out'], add=h, name=pre + "out")
        saved[i] = dict(h=h, n=n, uz=uz, y0=y0, sr=sr, si=si, yg=yg, q=q, y3=y3, tables=tables, prm=prm)
        return h_new

    def s5_layer_bwd(i, dh, carry=None):
        pre = "l%d_" % i
        s = saved[i]
        ar, ai, bblk, cblk = s['tables']
        grads[pre + 'w_out'] = mm_tn(s['y3'], dh, 1, name=pre + "d_w_out")[0]
        dy0_direct, dq, dp, db_glu = s5_gate_bwd(dh, full[pre + 'w_out'], s['y0'], s['q'], s['uz'],
                                                 name=pre + "d_y3")
        grads[pre + 'b_glu'] = db_glu
        grads[pre + 'w_glu'] = mm_tn(s['yg'], dq, 1, name=pre + "d_w_glu")[0]
        early = [] if carry is None else [pre + 'w_out', pre + 'w_glu']
        if carry is None:
            dyg = mm_nt(dq, full[pre + 'w_glu'], name=pre + "d_yg")
            res = s5_bwd(dy0_direct, dyg, s['y0'], s['uz'], s['sr'], s['si'], bblk, cblk, ar, ai,
                         row(w[pre + 'd_skip']), dp, name=pre + "d_scan")
        else:
            pieces = [chip_major(n) for n in early]
            dyg, theirs = mm_nt(dq, full[pre + 'w_glu'], name=pre + "d_yg", job=swap_job(pieces))
            sum_over_cores(early, pieces, theirs)
            res, arrived = s5_bwd(dy0_direct, dyg, s['y0'], s['uz'], s['sr'], s['si'], bblk, cblk, ar, ai,
                                  row(w[pre + 'd_skip']), dp, name=pre + "d_scan",
                                  job=scatter_job(scatter_of(carry) + [pair[n] for n in early]))
            from_chips.update(zip(layer(carry) + early, arrived, strict=True))
        dp, dbblk, dcblk, dar, dai, dd = res
        grads[pre + 'd_skip'] = dd
        cots = (dar.reshape(S5_GROUPS, S5_STATE), dai.reshape(S5_GROUPS, S5_STATE), dbblk, dcblk)
        for key, val in zip(('lam_re', 'lam_im', 'log_dt', 'b_re', 'b_im', 'c_re', 'c_im'),
                            s5_prep_bwd(s['prm'], cots, name=pre + "d_prep"), strict=True):
            grads[pre + key] = val.reshape(-1) if key == 'log_dt' else val
        if carry is None:
            grads[pre + 'w_in'] = mm_tn(s['n'], dp, N_CHIPS, name=pre + "d_w_in")
        else:
            later = [n for n in BIG if not n.startswith(pre)] + early
            sums = [sum_chips(pair[n], from_chips[n], ck, name="sum_chips_" + n) for n in later]
            grads[pre + 'w_in'], shared = mm_tn(s['n'], dp, N_CHIPS, name=pre + "d_w_in", job=share_job(sums))
            joined.update(zip(later, shared, strict=True))
        names = [n for n in layer(i) if n not in early]
        pieces = [chip_major(n) for n in names]
        (dh_in, dg), theirs = mm_nt_norm_bwd(dp, full[pre + 'w_in'], s['h'], row(w["norm%d_g" % i]), dh,
                                             name=pre + "d_n", job=swap_job(pieces))
        sum_over_cores(names, pieces, theirs)
        grads["norm%d_g" % i] = dg
        return dh_in

    h1 = s5_layer_fwd(0, h0, first=True)

    (p1, n1), halves = norm_mm_nn(h1, row(w['norm1_g']), full['l1_w_in'], name="l1_in", job=over_ici(layer(2)))
    y3_1 = conv_fwd(p1, conv_w_full, row(w['l1_conv_b']), name="l1_conv")
    h2, whole = mm_nn(y3_1, full['l1_w_out'], add=h1, name="l1_out", job=forward_halves_job(halves))
    landed(layer(2), whole)

    (p2, n2), halves = norm_mm_nn(h2, row(w['norm2_g']), full['l2_w_in'], name="l2_in", job=over_ici(layer(3)))
    mixed = pool_fwd(p2, name="l2_pool")
    o2, y3_2 = grp_nn_gate(mixed, full['l2_w_grp'], b_grp_full.reshape(1, -1), p2, row(w['l2_scale']), name="l2_grp")
    h3, whole = mm_nn(y3_2, full['l2_w_out'], add=h2, name="l2_out", job=forward_halves_job(halves))
    landed(layer(3), whole)

    h4 = s5_layer_fwd(3, h3)

    dh, grads['final_g'], sq = final_loss(h4, target, row(w['final_g']), N_META, real, name="loss")
    loss = lax.psum(0.5 / D_MODEL * jnp.sum(sq), ("x", "y", "c"))

    ck = jnp.stack([lax.axis_index("c"), chip]).astype(jnp.int32)
    pair, from_chips, joined = {}, {}, {}

    def chip_major(n):
        g = grads[n]
        if n.endswith('w_in'):
            return g
        if n == 'l2_w_grp':
            return g.reshape(N_CHIPS, -1, POOL_GROUP)
        return g.reshape(N_CHIPS, -1, g.shape[-1])

    def core_parts(i):
        return layer(i), [chip_major(n) for n in layer(i)]

    def sum_over_cores(names, pieces, theirs):
        for n, g, t in zip(names, pieces, theirs, strict=True):
            pair[n] = sum_cores(g, t, ck, F32 if n == 'small' else BF16, name="sum_cores_" + n)

    scatter_of = lambda i: [pair[n] for n in layer(i)]

    def scattered(i, arrays):
        from_chips.update(zip(layer(i), arrays, strict=True))

    dh = s5_layer_bwd(3, dh)

    grads['l2_w_out'] = mm_tn(y3_2, dh, 1, name="l2_d_w_out")[0]
    do2, dp2, dscale, dbgrp = pool_gate_bwd(dh, full['l2_w_out'], o2, p2, row(w['l2_scale']), name="l2_d_y3")
    grads['l2_scale'] = dscale
    grads['l2_b_grp'] = dbgrp
    grads['l2_w_grp'] = grp_tn(mixed, do2, name="l2_d_w_grp")
    dmix = grp_nt(do2, full['l2_w_grp'], name="l2_d_mixed")
    dp2 = pool_bwd(dmix, dp2, name="l2_d_pool")
    grads['l2_w_in'], arrived = mm_tn(n2, dp2, N_CHIPS, name="l2_d_w_in", job=scatter_job(scatter_of(3)))
    scattered(3, arrived)
    names, pieces = core_parts(2)
    (dh, dg), theirs = mm_nt_norm_bwd(dp2, full['l2_w_in'], h2, row(w['norm2_g']), dh, name="l2_d_n",
                                      job=swap_job(pieces))
    sum_over_cores(names, pieces, theirs)
    grads['norm2_g'] = dg

    grads['l1_w_out'] = mm_tn(y3_1, dh, 1, name="l1_d_w_out")[0]
    dp1, dcw, dcb = conv_bwd(dh, full['l1_w_out'], p1, conv_w_full, row(w['l1_conv_b']), name="l1_d_conv")
    grads['l1_conv_w'] = dcw
    grads['l1_conv_b'] = dcb
    grads['l1_w_in'], arrived = mm_tn(n1, dp1, N_CHIPS, name="l1_d_w_in", job=scatter_job(scatter_of(2)))
    scattered(2, arrived)
    names, pieces = core_parts(1)
    (dh, dg), theirs = mm_nt_norm_bwd(dp1, full['l1_w_in'], h1, row(w['norm1_g']), dh, name="l1_d_n",
                                      job=swap_job(pieces))
    sum_over_cores(names, pieces, theirs)
    grads['norm1_g'] = dg

    dh = s5_layer_bwd(0, dh, carry=1)
    grad_x = dh[N_META:real][None]
    grads['meta_tokens'] = dh[:N_META]

    wide = [n for n in SMALL if w[n].ndim == 3]
    wire = [grads[n].reshape(S5_GROUPS, -1) if n in wide else grads[n] for n in SMALL]
    starts = dict(zip(SMALL, _wire_plan([a.shape for a in wire]), strict=True))
    small_local = pack_small(wire, name="pack_small")
    pieces = [small_local.reshape(N_CHIPS, -1, WIRE_WIDTH)]
    sum_over_cores(['small'], pieces, run_job(swap_job(pieces), name="reduce_cores_small"))

    out_g, out_d, out_m, out_v = {}, {}, {}, {}
    as2d = lambda a: a.reshape(-1, a.shape[-1])

    def update_big(names, name, job=None):
        for n in names:
            out_g[n] = joined[n].reshape(w[n].shape)
        res = adamw_many([(as2d(w[n]), as2d(out_g[n]), as2d(mom_m[n]), as2d(mom_v[n])) for n in names],
                         name=name, job=job)
        triples, carried = res if job is not None else (res, None)
        for n, (d, nm, nv) in zip(names, triples, strict=True):
            out_d[n], out_m[n], out_v[n] = d.reshape(w[n].shape), nm.reshape(w[n].shape), nv.reshape(w[n].shape)
        return carried

    rest = ['l0_w_in', 'small']
    later = [n for n in BIG if n not in rest]
    from_chips.update(zip(rest, update_big(later, "adamw_big", scatter_job([pair[n] for n in rest])), strict=True))
    sums = [sum_chips(pair[n], from_chips[n], ck, name="sum_chips_" + n) for n in rest]
    joined.update(zip(rest, run_job(share_job(sums), name="share_cores"), strict=True))
    update_big(['l0_w_in'], "adamw_l0_w_in")
    (small_all,) = run_job(gather_by_halves_job([joined['small'].reshape(-1, WIRE_WIDTH)]), name="gather_small")
    small_all = small_all.reshape(WIRE_ROWS, WIRE_WIDTH)

    half_w = WIRE_WIDTH // 2
    shard_cells = {
        'meta_tokens': lambda k: [((0, N_META), (k * 256, (k + 1) * 256), (slice(None), slice(None)))],
        'l1_conv_w': lambda k: [(2 * j + k // 2, ((k % 2) * half_w, (k % 2 + 1) * half_w), (j, slice(None)))
                                for j in range(3)],
        'l2_b_grp': lambda k: [(j // 2, ((j % 2) * half_w + k * 128, (j % 2) * half_w + (k + 1) * 128),
                                (j, slice(None))) for j in range(len(POOL_WINDOWS))],
    }
    plain = [n for n in SMALL if n not in wide]
    res = update_small(small_all, [starts[n] for n in plain], [(w[n], mom_m[n], mom_v[n]) for n in plain],
                       [shard_cells.get(n) for n in plain], chip, name="adamw_small")
    for n, (g, d, nm, nv) in zip(plain, res, strict=True):
        out_g[n], out_d[n], out_m[n], out_v[n] = g, d, nm, nv
    as_gip = lambda n, a: gip(a) if n.endswith(('b_re', 'b_im')) else a
    g_gip = {n: small_all[starts[n]:starts[n] + S5_GROUPS].reshape(S5_GROUPS, S5_GROUP, S5_STATE) for n in wide}
    res = adamw_many([(as_gip(n, w[n]), g_gip[n], as_gip(n, mom_m[n]), as_gip(n, mom_v[n])) for n in wide],
                     name="adamw_s5_tensors")
    for n, (d, nm, nv) in zip(wide, res, strict=True):
        out_g[n], out_d[n], out_m[n], out_v[n] = as_gip(n, g_gip[n]), as_gip(n, d), as_gip(n, nm), as_gip(n, nv)

    return (loss, grad_x, *[out_g[n] for n in WEIGHTS], *[out_d[n] for n in WEIGHTS],
            *[out_m[n] for n in WEIGHTS], *[out_v[n] for n in WEIGHTS])
```

```python
import functools

import jax
import jax.numpy as jnp
from jax import lax
from jax.experimental import pallas as pl
from jax.experimental.pallas import tpu as pltpu

F32, BF16 = jnp.float32, jnp.bfloat16
MESH = pl.DeviceIdType.MESH

D_MODEL = 1024
N_META = 16
EPS = 1e-6
S5_GROUPS, S5_GROUP, S5_STATE = 64, 16, 64
LANE_BLOCK = 128
S5_BLOCKS = D_MODEL // LANE_BLOCK
GROUPS_PER_BLOCK = LANE_BLOCK // S5_GROUP
STATE_BLOCK = GROUPS_PER_BLOCK * S5_STATE
SCAN_ROWS = 256
SCAN_CHUNK = 2176
SUBLANES = 8
CONV_E = 2048
POOL_E = 2048
POOL_WINDOWS = (2, 4, 8, 16)
POOL_GROUP = 512
POOL_HALO = 16
N_CHIPS = 4

ADAM_LR, ADAM_B1, ADAM_B2, ADAM_EPS, ADAM_WD, ADAM_STEP = 0.001, 0.9, 0.999, 1e-08, 0.01, 10

VMEM_LIMIT = 56 * 1024 * 1024
TN_OPERAND_BYTES = 28 * 1024 * 1024

WEIGHTS = ['meta_tokens', 'norm0_g', 'l0_w_in', 'l0_lam_re', 'l0_lam_im', 'l0_log_dt', 'l0_b_re', 'l0_b_im',
           'l0_c_re', 'l0_c_im', 'l0_d_skip', 'l0_w_glu', 'l0_b_glu', 'l0_w_out', 'norm1_g', 'l1_w_in',
           'l1_conv_w', 'l1_conv_b', 'l1_w_out', 'norm2_g', 'l2_w_in', 'l2_w_grp', 'l2_b_grp', 'l2_scale',
           'l2_w_out', 'norm3_g', 'l3_w_in', 'l3_lam_re', 'l3_lam_im', 'l3_log_dt', 'l3_b_re', 'l3_b_im',
           'l3_c_re', 'l3_c_im', 'l3_d_skip', 'l3_w_glu', 'l3_b_glu', 'l3_w_out', 'final_g']
BIG = ['l0_w_in', 'l0_w_glu', 'l0_w_out', 'l1_w_in', 'l1_w_out', 'l2_w_in', 'l2_w_grp', 'l2_w_out',
       'l3_w_in', 'l3_w_glu', 'l3_w_out']
SMALL_SHARDED = ['meta_tokens', 'l1_conv_w', 'l2_b_grp']
SMALL = [n for n in WEIGHTS if n not in BIG]


def _params(*sem):
    return pltpu.CompilerParams(dimension_semantics=sem, vmem_limit_bytes=VMEM_LIMIT)


class Job:
    def __init__(self, arrays, out_shapes, sems, fn, in_place=False):
        self.arrays, self.out_shapes, self.sems, self.fn = list(arrays), list(out_shapes), list(sems), fn
        self.in_place = in_place
        assert len(self.arrays) == len(self.out_shapes)

    @staticmethod
    def both(first, second):
        na, ns = len(first.arrays), len(first.sems)

        def fn(ins, outs, sems, phase):
            first.fn(ins[:na], outs[:na], sems[:ns], phase)
            second.fn(ins[na:], outs[na:], sems[ns:], phase)

        job = Job(first.arrays + second.arrays, first.out_shapes + second.out_shapes, first.sems + second.sems, fn)
        job.in_place = [first.in_place] * na + [second.in_place] * len(second.arrays)
        return job

    def aliased(self):
        flags = self.in_place if isinstance(self.in_place, list) else [self.in_place] * len(self.arrays)
        return [a for a, flag in enumerate(flags) if flag]


def _call(body, *, name, grid, in_specs, out_specs, out_shape, args, semantics, scratch=(), aliases=None, job=None):
    if job is None:
        return pl.pallas_call(
            body, name=name, grid=grid, in_specs=list(in_specs), out_specs=list(out_specs),
            out_shape=list(out_shape), scratch_shapes=list(scratch), input_output_aliases=aliases or {},
            compiler_params=_params(*semantics))(*args)
    counts = [len(in_specs), len(job.arrays), len(out_specs), len(job.out_shapes), len(scratch), len(job.sems)]

    def hosted(*refs):
        parts, pos = [], 0
        for k in counts:
            parts.append(refs[pos:pos + k])
            pos += k
        ins, job_ins, outs, job_outs, scr, job_sems = parts
        steps = [pl.program_id(d) for d in range(len(grid))]
        first = functools.reduce(jnp.logical_and, [s == 0 for s in steps])
        last = functools.reduce(jnp.logical_and, [s == g - 1 for s, g in zip(steps, grid, strict=True)])

        @pl.when(first)
        def _():
            job.fn(job_ins, job_outs, job_sems, "start")

        body(*ins, *outs, *scr)

        @pl.when(last)
        def _():
            job.fn(job_ins, job_outs, job_sems, "finish")

    any_spec = pl.BlockSpec(memory_space=pl.ANY)
    aliases = dict(aliases or {})
    aliases.update({len(in_specs) + a: len(out_specs) + a for a in job.aliased()})
    res = pl.pallas_call(
        hosted, name=name, grid=grid, in_specs=list(in_specs) + [any_spec] * len(job.arrays),
        out_specs=list(out_specs) + [any_spec] * len(job.out_shapes),
        out_shape=list(out_shape) + job.out_shapes, scratch_shapes=list(scratch) + job.sems,
        input_output_aliases=aliases,
        compiler_params=_params(*["arbitrary"] * len(grid)))(*args, *job.arrays)
    return res[:len(out_specs)], res[len(out_specs):]


def _tile(n, cap, mult):
    best = None
    for t in range(mult, min(n, cap) + 1, mult):
        if n % t == 0:
            best = t
    assert best is not None, (n, cap, mult)
    return best


def _with_job(res, job):
    return res[0] if job is None else (res[0][0], res[1])


def mm_nn(a, b, *, name, bias=None, add=None, out_dtype=F32, job=None):
    m, k = a.shape
    s, _, ns = b.shape
    n = s * ns
    tm, tn = _tile(m, 1088, 16), _tile(ns, 1024, 128)
    per = ns // tn

    def body(*refs):
        a_ref, b_ref = refs[0], refs[1]
        o_ref = refs[-1]
        acc = jnp.dot(a_ref[...].astype(BF16), b_ref[...], preferred_element_type=F32)
        pos = 2
        if bias is not None:
            acc = acc + refs[pos][...]
            pos += 1
        if add is not None:
            acc = acc + refs[pos][...]
        o_ref[...] = acc.astype(o_ref.dtype)

    in_specs = [pl.BlockSpec((tm, k), lambda i, j: (i, 0)),
                pl.BlockSpec((None, k, tn), lambda i, j: (j // per, 0, j % per))]
    args = [a, b]
    if bias is not None:
        in_specs.append(pl.BlockSpec((1, tn), lambda i, j: (0, j)))
        args.append(bias)
    if add is not None:
        in_specs.append(pl.BlockSpec((tm, tn), lambda i, j: (i, j)))
        args.append(add)
    return _with_job(_call(
        body, name=name, grid=(m // tm, n // tn), in_specs=in_specs,
        out_specs=[pl.BlockSpec((tm, tn), lambda i, j: (i, j))],
        out_shape=[jax.ShapeDtypeStruct((m, n), out_dtype)], args=args,
        semantics=("parallel", "parallel"), job=job), job)


def norm_mm_nn(h, gain, b, *, name, job=None):
    m, k = h.shape
    s, _, ns = b.shape
    n = s * ns
    tm, tn = _tile(m, 1088, 16), _tile(ns, 1024, 128)
    per = ns // tn

    def body(h_ref, g_ref, b_ref, o_ref, n_ref):
        @pl.when(pl.program_id(1) == 0)
        def _():
            n_ref[...] = _rms(h_ref[...], g_ref[...]).astype(n_ref.dtype)

        o_ref[...] = jnp.dot(n_ref[...], b_ref[...], preferred_element_type=F32)

    res = _call(
        body, name=name, grid=(m // tm, n // tn),
        in_specs=[pl.BlockSpec((tm, k), lambda i, j: (i, 0)), pl.BlockSpec((1, k), lambda i, j: (0, 0)),
                  pl.BlockSpec((None, k, tn), lambda i, j: (j // per, 0, j % per))],
        out_specs=[pl.BlockSpec((tm, tn), lambda i, j: (i, j)), pl.BlockSpec((tm, k), lambda i, j: (i, 0))],
        out_shape=[jax.ShapeDtypeStruct((m, n), F32), jax.ShapeDtypeStruct((m, k), BF16)], args=(h, gain, b),
        semantics=("parallel", "arbitrary"), job=job)
    return (res[0], res[1]) if job is None else ((res[0][0], res[0][1]), res[1])


def glu_gate(yg, w_glu, b_glu, y0, uz, *, name, job=None):
    m, k = yg.shape
    n = w_glu.shape[2]
    tm = _tile(m, 1088, 16)

    def body(a_ref, w_ref, b_ref, y0_ref, z_ref, q_ref, o_ref):
        q = jnp.dot(a_ref[...], w_ref[...], preferred_element_type=F32) + b_ref[...]
        q_ref[...] = q
        o_ref[...] = _s5_gate(y0_ref[...], q, z_ref[...]).astype(o_ref.dtype)

    tile = pl.BlockSpec((tm, n), lambda i: (i, 0))
    res = _call(
        body, name=name, grid=(m // tm,),
        in_specs=[pl.BlockSpec((tm, k), lambda i: (i, 0)), pl.BlockSpec((None, k, n), lambda i: (0, 0, 0)),
                  pl.BlockSpec((1, n), lambda i: (0, 0)), tile, pl.BlockSpec((tm, n), lambda i: (i, 1))],
        out_specs=[tile, tile],
        out_shape=[jax.ShapeDtypeStruct((m, n), F32), jax.ShapeDtypeStruct((m, n), BF16)],
        args=(yg, w_glu, b_glu, y0, uz), semantics=("parallel",), job=job)
    return (res[0], res[1]) if job is None else ((res[0][0], res[0][1]), res[1])


def s5_gate_bwd(dh, w_out, y0, q, uz, *, name):
    m, d = dh.shape
    e = w_out.shape[1]
    tm = _tile(m, 544, 16)

    def body(dh_ref, w_ref, y0_ref, q_ref, z_ref, dy0_ref, dq_ref, dz_ref, db_ref):
        i = pl.program_id(0)
        dy3 = lax.dot_general(dh_ref[...].astype(BF16), w_ref[...], (((1,), (1,)), ((), ())),
                              preferred_element_type=F32)
        dy0, dq, dz = jax.vjp(_s5_gate, y0_ref[...], q_ref[...], z_ref[...])[1](dy3)
        dy0_ref[...] = dy0
        dq_ref[...] = dq.astype(dq_ref.dtype)
        dz_ref[...] = dz.astype(dz_ref.dtype)
        db = _colsum(dq)

        @pl.when(i == 0)
        def _():
            db_ref[...] = db

        @pl.when(i > 0)
        def _():
            db_ref[...] += db

    tile = pl.BlockSpec((tm, e), lambda i: (i, 0))
    right = pl.BlockSpec((tm, e), lambda i: (i, 1))
    return pl.pallas_call(
        body, name=name, grid=(m // tm,),
        in_specs=[pl.BlockSpec((tm, d), lambda i: (i, 0)), pl.BlockSpec((None, e, d), lambda i: (0, 0, 0)),
                  tile, tile, right],
        out_specs=[tile, tile, right, pl.BlockSpec((1, e), lambda i: (0, 0))],
        out_shape=[jax.ShapeDtypeStruct((m, e), F32), jax.ShapeDtypeStruct((m, e), BF16),
                   jax.ShapeDtypeStruct((m, 2 * e), BF16), jax.ShapeDtypeStruct((1, e), F32)],
        compiler_params=_params("arbitrary"))(dh, w_out, y0, q, uz)


def pool_gate_bwd(dh, w_out, o, p, scale, *, name):
    m, d = dh.shape
    e = w_out.shape[1]
    tm, te = _tile(m, 1088, 16), 1024
    nj = e // te

    def body(dh_ref, w_ref, o_ref, z_ref, s_ref, do_ref, dz_ref, ds_ref, db_ref):
        i = pl.program_id(1)
        dy3 = lax.dot_general(dh_ref[...].astype(BF16), w_ref[...], (((1,), (1,)), ((), ())),
                              preferred_element_type=F32)
        do, dz, ds = jax.vjp(_pool_gate, o_ref[...], z_ref[...], s_ref[...])[1](dy3)
        do_ref[...] = do.astype(do_ref.dtype)
        dz_ref[...] = dz.astype(dz_ref.dtype)
        db = _colsum(do)

        @pl.when(i == 0)
        def _():
            ds_ref[...] = ds
            db_ref[...] = db

        @pl.when(i > 0)
        def _():
            ds_ref[...] += ds
            db_ref[...] += db

    tile = pl.BlockSpec((tm, te), lambda j, i: (i, j))
    right = pl.BlockSpec((tm, te), lambda j, i: (i, nj + j))
    vec = pl.BlockSpec((1, te), lambda j, i: (0, j))
    return pl.pallas_call(
        body, name=name, grid=(nj, m // tm),
        in_specs=[pl.BlockSpec((tm, d), lambda j, i: (i, 0)), pl.BlockSpec((None, te, d), lambda j, i: (0, j, 0)),
                  tile, right, vec],
        out_specs=[tile, right, vec, vec],
        out_shape=[jax.ShapeDtypeStruct((m, e), BF16), jax.ShapeDtypeStruct((m, 2 * e), BF16),
                   jax.ShapeDtypeStruct((1, e), F32), jax.ShapeDtypeStruct((1, e), F32)],
        compiler_params=_params("parallel", "arbitrary"))(dh, w_out, o, p, scale)


def mm_nt_norm_bwd(g, w, h, gain, dh_out, *, name, job=None):
    m, kc = g.shape
    s, nout, kcs = w.shape
    assert s * kcs == kc
    tm, tk = _tile(m, 1088, 16), _tile(kcs, 1024, 128)
    per = kcs // tk
    nk = kc // tk

    def body(g_ref, w_ref, h_ref, gain_ref, dh_ref, o_ref, dg_ref):
        i, kk = pl.program_id(0), pl.program_id(1)
        part = lax.dot_general(g_ref[...].astype(BF16), w_ref[...], (((1,), (1,)), ((), ())),
                               preferred_element_type=F32)

        @pl.when(kk == 0)
        def _():
            o_ref[...] = part

        @pl.when(kk > 0)
        def _():
            o_ref[...] += part

        @pl.when(kk == nk - 1)
        def _():
            dh, dg = jax.vjp(_rms, h_ref[...], gain_ref[...])[1](o_ref[...])
            o_ref[...] = dh_ref[...] + dh

            @pl.when(i == 0)
            def _():
                dg_ref[...] = dg

            @pl.when(i > 0)
            def _():
                dg_ref[...] += dg

    row_tile = pl.BlockSpec((tm, nout), lambda i, kk: (i, 0))
    res = _call(
        body, name=name, grid=(m // tm, nk),
        in_specs=[pl.BlockSpec((tm, tk), lambda i, kk: (i, kk)),
                  pl.BlockSpec((None, nout, tk), lambda i, kk: (kk // per, 0, kk % per)),
                  row_tile, pl.BlockSpec((1, nout), lambda i, kk: (0, 0)), row_tile],
        out_specs=[row_tile, pl.BlockSpec((1, nout), lambda i, kk: (0, 0))],
        out_shape=[jax.ShapeDtypeStruct((m, nout), F32), jax.ShapeDtypeStruct((1, nout), F32)],
        args=(g, w, h, gain, dh_out), semantics=("arbitrary", "arbitrary"), job=job)
    return (res[0], res[1]) if job is None else ((res[0][0], res[0][1]), res[1])


def mm_nt(g, w, *, name, job=None):
    m, kc = g.shape
    s, nout, kcs = w.shape
    assert s * kcs == kc
    tm, tno, tk = _tile(m, 2176, 16), _tile(nout, 1024, 128), _tile(kcs, 1024, 128)
    per = kcs // tk

    def body(g_ref, w_ref, o_ref):
        kk = pl.program_id(2)
        part = lax.dot_general(g_ref[...].astype(BF16), w_ref[...], (((1,), (1,)), ((), ())),
                               preferred_element_type=F32)

        @pl.when(kk == 0)
        def _():
            o_ref[...] = part

        @pl.when(kk > 0)
        def _():
            o_ref[...] += part

    return _with_job(_call(
        body, name=name, grid=(m // tm, nout // tno, kc // tk),
        in_specs=[pl.BlockSpec((tm, tk), lambda i, j, kk: (i, kk)),
                  pl.BlockSpec((None, tno, tk), lambda i, j, kk: (kk // per, j, kk % per))],
        out_specs=[pl.BlockSpec((tm, tno), lambda i, j, kk: (i, j))],
        out_shape=[jax.ShapeDtypeStruct((m, nout), F32)], args=(g, w),
        semantics=("parallel", "parallel", "arbitrary"), job=job), job)


def mm_tn(a, g, shards, *, name, job=None):
    m, ka = a.shape
    n = g.shape[1]
    ns = n // shards
    tka, tn = _tile(ka, 1024, 128), _tile(ns, 512, 128)
    row_bytes = tka * jnp.dtype(a.dtype).itemsize + tn * jnp.dtype(g.dtype).itemsize
    tm = _tile(m, TN_OPERAND_BYTES // (2 * row_bytes), 16)
    per = ns // tn

    def body(a_ref, g_ref, o_ref):
        mm = pl.program_id(2)
        part = lax.dot_general(a_ref[...].astype(BF16), g_ref[...].astype(BF16), (((0,), (0,)), ((), ())),
                               preferred_element_type=F32)

        @pl.when(mm == 0)
        def _():
            o_ref[...] = part

        @pl.when(mm > 0)
        def _():
            o_ref[...] += part

    return _with_job(_call(
        body, name=name, grid=(ka // tka, n // tn, m // tm),
        in_specs=[pl.BlockSpec((tm, tka), lambda i, j, mm: (mm, i)),
                  pl.BlockSpec((tm, tn), lambda i, j, mm: (mm, j))],
        out_specs=[pl.BlockSpec((None, tka, tn), lambda i, j, mm: (j // per, i, j % per))],
        out_shape=[jax.ShapeDtypeStruct((shards, ka, ns), F32)], args=(a, g),
        semantics=("parallel", "parallel", "arbitrary"), job=job), job)


def grp_nn_gate(a, w, bias, p, scale, *, name):
    m = a.shape[0]
    tm = _tile(m, 1088, 16)
    ng = len(POOL_WINDOWS)

    def body(a_ref, w_ref, b_ref, z_ref, s_ref, o_ref, y_ref):
        wj = w_ref[...].reshape(POOL_GROUP, POOL_GROUP)
        o = jnp.dot(a_ref[...].astype(BF16), wj, preferred_element_type=F32) + b_ref[...]
        o_ref[...] = o
        y_ref[...] = _pool_gate(o, z_ref[...], s_ref[...]).astype(y_ref.dtype)

    tile = pl.BlockSpec((tm, POOL_GROUP), lambda i, j: (i, j))
    vec = pl.BlockSpec((1, POOL_GROUP), lambda i, j: (0, j))
    return pl.pallas_call(
        body, name=name, grid=(m // tm, ng),
        in_specs=[tile,
                  pl.BlockSpec((N_CHIPS, None, POOL_GROUP // N_CHIPS, POOL_GROUP), lambda i, j: (0, j, 0, 0)),
                  vec, pl.BlockSpec((tm, POOL_GROUP), lambda i, j: (i, ng + j)), vec],
        out_specs=[tile, tile],
        out_shape=[jax.ShapeDtypeStruct((m, ng * POOL_GROUP), F32), jax.ShapeDtypeStruct((m, ng * POOL_GROUP), BF16)],
        compiler_params=_params("parallel", "parallel"))(a, w, bias, p, scale)


def grp_nt(g, w, *, name):
    m = g.shape[0]
    tm = _tile(m, 1088, 16)
    ng = len(POOL_WINDOWS)

    def body(g_ref, w_ref, o_ref):
        wj = w_ref[...].reshape(POOL_GROUP, POOL_GROUP)
        o_ref[...] = lax.dot_general(g_ref[...].astype(BF16), wj, (((1,), (1,)), ((), ())),
                                     preferred_element_type=F32)

    return pl.pallas_call(
        body, name=name, grid=(m // tm, ng),
        in_specs=[pl.BlockSpec((tm, POOL_GROUP), lambda i, j: (i, j)),
                  pl.BlockSpec((N_CHIPS, None, POOL_GROUP // N_CHIPS, POOL_GROUP), lambda i, j: (0, j, 0, 0))],
        out_specs=pl.BlockSpec((tm, POOL_GROUP), lambda i, j: (i, j)),
        out_shape=jax.ShapeDtypeStruct((m, ng * POOL_GROUP), F32),
        compiler_params=_params("parallel", "parallel"))(g, w)


def grp_tn(a, g, *, name):
    m = a.shape[0]
    tm = _tile(m, 1088, 16)
    ng = len(POOL_WINDOWS)
    rows = POOL_GROUP // N_CHIPS

    def body(a_ref, g_ref, o_ref):
        mm = pl.program_id(1)
        part = lax.dot_general(a_ref[...].astype(BF16), g_ref[...].astype(BF16), (((0,), (0,)), ((), ())),
                               preferred_element_type=F32).reshape(N_CHIPS, rows, POOL_GROUP)

        @pl.when(mm == 0)
        def _():
            o_ref[...] = part

        @pl.when(mm > 0)
        def _():
            o_ref[...] += part

    return pl.pallas_call(
        body, name=name, grid=(ng, m // tm),
        in_specs=[pl.BlockSpec((tm, POOL_GROUP), lambda j, mm: (mm, j)),
                  pl.BlockSpec((tm, POOL_GROUP), lambda j, mm: (mm, j))],
        out_specs=pl.BlockSpec((N_CHIPS, None, rows, POOL_GROUP), lambda j, mm: (0, j, 0, 0)),
        out_shape=jax.ShapeDtypeStruct((N_CHIPS, ng, rows, POOL_GROUP), F32),
        compiler_params=_params("parallel", "arbitrary"))(a, g)


def final_loss(h, target, gain, first, last, *, name):
    m, d = h.shape
    tr = m // 16

    def body(h_ref, t_ref, g_ref, dh_ref, dg_ref, sq_ref):
        i = pl.program_id(0)
        y, vjp = jax.vjp(_rms, h_ref[...], g_ref[...])
        pos = i * tr + lax.broadcasted_iota(jnp.int32, (tr, 1), 0)
        diff = jnp.where((pos >= first) & (pos < last), y - t_ref[...], 0.0)
        dh_ref[...], dg = vjp(diff / d)
        sq = _colsum(diff * diff)

        @pl.when(i == 0)
        def _():
            dg_ref[...] = dg
            sq_ref[...] = sq

        @pl.when(i > 0)
        def _():
            dg_ref[...] += dg
            sq_ref[...] += sq

    tile, vec = pl.BlockSpec((tr, d), lambda i: (i, 0)), pl.BlockSpec((1, d), lambda i: (0, 0))
    return pl.pallas_call(
        body, name=name, grid=(m // tr,), in_specs=[tile, tile, vec], out_specs=[tile, vec, vec],
        out_shape=[jax.ShapeDtypeStruct((m, d), F32), jax.ShapeDtypeStruct((1, d), F32),
                   jax.ShapeDtypeStruct((1, d), F32)],
        compiler_params=_params("arbitrary"))(h, target, gain)


def _rms(h, g):
    return h * lax.rsqrt(jnp.mean(h * h, axis=-1, keepdims=True) + EPS) * g


def _colsum(v):
    return jnp.sum(v, axis=0, keepdims=True)


def _s5_gate(y0, q, z):
    yg = jax.nn.gelu(y0)
    return yg * jax.nn.sigmoid(q) * jax.nn.silu(z)


def _pool_gate(o, z, scale):
    return o * scale * jax.nn.silu(z)


def _shift_rows(v, d, down):
    t = v.shape[0]
    row = lax.broadcasted_iota(jnp.int32, v.shape, 0)
    if down:
        return jnp.where(row >= d, pltpu.roll(v, d, 0), 0.0)
    return jnp.where(row < t - d, pltpu.roll(v, t - d, 0), 0.0)


def _cmul(ar, ai, br, bi):
    return ar * br - ai * bi, ar * bi + ai * br


def _scan(xr, xi, ar, ai, cr, ci, down):
    t, n = xr.shape
    nb = t // SUBLANES
    pw = [(ar, ai)]
    for _ in range(SUBLANES - 1):
        pw.append(_cmul(*pw[-1], ar, ai))
    sub = lax.broadcasted_iota(jnp.int32, (SUBLANES, n), 0)

    def table(exps):
        tr, ti = jnp.zeros((SUBLANES, n), F32), jnp.zeros((SUBLANES, n), F32)
        for r, e in enumerate(exps):
            if e:
                tr, ti = jnp.where(sub == r, pw[e - 1][0], tr), jnp.where(sub == r, pw[e - 1][1], ti)
        return tr[None], ti[None]

    sr, si = xr.reshape(nb, SUBLANES, n), xi.reshape(nb, SUBLANES, n)
    d = 1
    while d < SUBLANES:
        mr, mi = table([d if (r >= d if down else r < SUBLANES - d) else 0 for r in range(SUBLANES)])
        turn = d if down else SUBLANES - d
        hr, hi = pltpu.roll(sr, turn, 1), pltpu.roll(si, turn, 1)
        sr, si = sr + mr * hr - mi * hi, si + mr * hi + mi * hr
        d *= 2
    edge = SUBLANES - 1 if down else 0
    qr, qi = table([r + 1 if down else SUBLANES - r for r in range(SUBLANES)])
    qr, qi = qr[0], qi[0]
    in_r, in_i = jnp.broadcast_to(cr, (SUBLANES, n)), jnp.broadcast_to(ci, (SUBLANES, n))
    out_r, out_i = [None] * nb, [None] * nb
    for b in (range(nb) if down else reversed(range(nb))):
        out_r[b] = sr[b] + qr * in_r - qi * in_i
        out_i[b] = si[b] + qr * in_i + qi * in_r
        in_r = jnp.broadcast_to(out_r[b][edge:edge + 1, :], (SUBLANES, n))
        in_i = jnp.broadcast_to(out_i[b][edge:edge + 1, :], (SUBLANES, n))
    return jnp.concatenate(out_r, axis=0), jnp.concatenate(out_i, axis=0), in_r[0:1, :], in_i[0:1, :]


def s5_fwd(uz, bblk, cblk, ar, ai, dskip, *, name, job=None):
    lp = uz.shape[0]
    t = _tile(lp, SCAN_CHUNK, 16)
    nst = S5_GROUPS * S5_STATE

    def body(u_ref, b_ref, c_ref, ar_ref, ai_ref, d_ref, y_ref, yg_ref, sr_ref, si_ref, cr, ci):
        step = pl.program_id(1)

        @pl.when(step == 0)
        def _():
            cr[...] = jnp.zeros_like(cr)
            ci[...] = jnp.zeros_like(ci)

        u = u_ref[...]
        a_r, a_i = ar_ref[...], ai_ref[...]
        x = jnp.dot(u.astype(BF16), b_ref[...].astype(BF16), preferred_element_type=F32)
        sr, si, cr[...], ci[...] = _scan(x[:, :STATE_BLOCK], x[:, STATE_BLOCK:], a_r, a_i, cr[...], ci[...], True)
        sr_ref[...] = sr
        si_ref[...] = si
        s = jnp.concatenate([sr, si], axis=1).astype(BF16)
        y0 = lax.dot_general(s, c_ref[...].astype(BF16), (((1,), (1,)), ((), ())),
                             preferred_element_type=F32) + d_ref[...] * u
        y_ref[...] = y0
        yg_ref[...] = jax.nn.gelu(y0).astype(yg_ref.dtype)

    return _call(
        body, name=name, grid=(S5_BLOCKS, lp // t),
        in_specs=[pl.BlockSpec((t, LANE_BLOCK), lambda b, c: (c, b)),
                  pl.BlockSpec((None, LANE_BLOCK, 2 * STATE_BLOCK), lambda b, c: (b, 0, 0)),
                  pl.BlockSpec((None, LANE_BLOCK, 2 * STATE_BLOCK), lambda b, c: (b, 0, 0)),
                  pl.BlockSpec((1, STATE_BLOCK), lambda b, c: (0, b)),
                  pl.BlockSpec((1, STATE_BLOCK), lambda b, c: (0, b)),
                  pl.BlockSpec((1, LANE_BLOCK), lambda b, c: (0, b))],
        out_specs=[pl.BlockSpec((t, LANE_BLOCK), lambda b, c: (c, b)),
                   pl.BlockSpec((t, LANE_BLOCK), lambda b, c: (c, b)),
                   pl.BlockSpec((t, STATE_BLOCK), lambda b, c: (c, b)),
                   pl.BlockSpec((t, STATE_BLOCK), lambda b, c: (c, b))],
        out_shape=[jax.ShapeDtypeStruct((lp, D_MODEL), F32), jax.ShapeDtypeStruct((lp, D_MODEL), BF16),
                   jax.ShapeDtypeStruct((lp, nst), F32), jax.ShapeDtypeStruct((lp, nst), F32)],
        scratch=[pltpu.VMEM((1, STATE_BLOCK), F32), pltpu.VMEM((1, STATE_BLOCK), F32)],
        args=(uz, bblk, cblk, ar, ai, dskip), semantics=("parallel", "arbitrary"), job=job)


def s5_bwd(dy_direct, dyg, y0, uz, sr, si, bblk, cblk, ar, ai, dskip, dp, *, name, job=None):
    lp = uz.shape[0]
    t = _tile(lp, SCAN_CHUNK, 16)
    nch = lp // t
    nst = S5_GROUPS * S5_STATE

    def body(dyd_ref, dyg_ref, y0_ref, u_ref, sr_ref, si_ref, b_ref, c_ref, ar_ref, ai_ref, d_ref, _, du_ref,
             db_ref, dc_ref, dar_ref, dai_ref, dd_ref, cr, ci):
        step = pl.program_id(1)

        @pl.when(step == 0)
        def _():
            cr[...] = jnp.zeros_like(cr)
            ci[...] = jnp.zeros_like(ci)

        dy = dyd_ref[...] + jax.vjp(jax.nn.gelu, y0_ref[...])[1](dyg_ref[...])[0]
        u = u_ref[...]
        dyb, ub = dy.astype(BF16), u.astype(BF16)
        a_r, a_i = ar_ref[...], -ai_ref[...]
        g = jnp.dot(dyb, c_ref[...].astype(BF16), preferred_element_type=F32)
        gr, gi = g[:, :STATE_BLOCK], g[:, STATE_BLOCK:]
        nxt_r, nxt_i = cr[...], ci[...]
        last = lax.broadcasted_iota(jnp.int32, gr.shape, 0) == t - 1
        lr, li, cr[...], ci[...] = _scan(gr, gi, a_r, a_i, nxt_r, nxt_i, False)
        nr = _shift_rows(lr, 1, False) + jnp.where(last, nxt_r, 0.0)
        ni = _shift_rows(li, 1, False) + jnp.where(last, nxt_i, 0.0)
        s_r, s_i = sr_ref[...], si_ref[...]
        dar = _colsum(s_r * nr + s_i * ni)
        dai = _colsum(s_r * ni - s_i * nr)
        lam = jnp.concatenate([lr, li], axis=1).astype(BF16)
        sb = jnp.concatenate([s_r, s_i], axis=1).astype(BF16)
        dc = lax.dot_general(dyb, sb, (((0,), (0,)), ((), ())), preferred_element_type=F32)
        db = lax.dot_general(ub, lam, (((0,), (0,)), ((), ())), preferred_element_type=F32)
        du_ref[...] = (lax.dot_general(lam, b_ref[...].astype(BF16), (((1,), (1,)), ((), ())),
                                       preferred_element_type=F32) + d_ref[...] * dy).astype(du_ref.dtype)
        dd = _colsum(dy * u)

        @pl.when(step == 0)
        def _():
            db_ref[...] = db
            dc_ref[...] = dc
            dar_ref[...] = dar
            dai_ref[...] = dai
            dd_ref[...] = dd

        @pl.when(step > 0)
        def _():
            db_ref[...] += db
            dc_ref[...] += dc
            dar_ref[...] += dar
            dai_ref[...] += dai
            dd_ref[...] += dd

    rev = lambda b, c: (nch - 1 - c, b)
    return _call(
        body, name=name, grid=(S5_BLOCKS, nch),
        in_specs=[pl.BlockSpec((t, LANE_BLOCK), rev), pl.BlockSpec((t, LANE_BLOCK), rev),
                  pl.BlockSpec((t, LANE_BLOCK), rev), pl.BlockSpec((t, LANE_BLOCK), rev),
                  pl.BlockSpec((t, STATE_BLOCK), rev), pl.BlockSpec((t, STATE_BLOCK), rev),
                  pl.BlockSpec((None, LANE_BLOCK, 2 * STATE_BLOCK), lambda b, c: (b, 0, 0)),
                  pl.BlockSpec((None, LANE_BLOCK, 2 * STATE_BLOCK), lambda b, c: (b, 0, 0)),
                  pl.BlockSpec((1, STATE_BLOCK), lambda b, c: (0, b)),
                  pl.BlockSpec((1, STATE_BLOCK), lambda b, c: (0, b)),
                  pl.BlockSpec((1, LANE_BLOCK), lambda b, c: (0, b)),
                  pl.BlockSpec(memory_space=pl.ANY)],
        out_specs=[pl.BlockSpec((t, LANE_BLOCK), rev),
                   pl.BlockSpec((None, LANE_BLOCK, 2 * STATE_BLOCK), lambda b, c: (b, 0, 0)),
                   pl.BlockSpec((None, LANE_BLOCK, 2 * STATE_BLOCK), lambda b, c: (b, 0, 0)),
                   pl.BlockSpec((1, STATE_BLOCK), lambda b, c: (0, b)),
                   pl.BlockSpec((1, STATE_BLOCK), lambda b, c: (0, b)),
                   pl.BlockSpec((1, LANE_BLOCK), lambda b, c: (0, b))],
        out_shape=[jax.ShapeDtypeStruct(dp.shape, dp.dtype),
                   jax.ShapeDtypeStruct((S5_BLOCKS, LANE_BLOCK, 2 * STATE_BLOCK), F32),
                   jax.ShapeDtypeStruct((S5_BLOCKS, LANE_BLOCK, 2 * STATE_BLOCK), F32),
                   jax.ShapeDtypeStruct((1, nst), F32), jax.ShapeDtypeStruct((1, nst), F32),
                   jax.ShapeDtypeStruct((1, D_MODEL), F32)],
        scratch=[pltpu.VMEM((1, STATE_BLOCK), F32), pltpu.VMEM((1, STATE_BLOCK), F32)],
        aliases={11: 0}, args=(dy_direct, dyg, y0, uz, sr, si, bblk, cblk, ar, ai, dskip, dp),
        semantics=("parallel", "arbitrary"), job=job)


def _s5_prep(lam_re, lam_im, log_dt, b_re, b_im, c_re, c_im):
    dt = jnp.exp(log_dt)
    mag = jnp.exp(lam_re * dt)
    ar = mag * jnp.cos(lam_im * dt)
    ai = mag * jnp.sin(lam_im * dt)
    den = lam_re * lam_re + lam_im * lam_im
    kr = ((ar - 1.0) * lam_re + ai * lam_im) / den
    ki = (ai * lam_re - (ar - 1.0) * lam_im) / den
    bbr = kr[:, None, :] * b_re - ki[:, None, :] * b_im
    bbi = kr[:, None, :] * b_im + ki[:, None, :] * b_re
    rows = S5_GROUPS * S5_GROUP
    expand = (lax.broadcasted_iota(jnp.int32, (S5_STATE, STATE_BLOCK), 1) % S5_STATE
              == lax.broadcasted_iota(jnp.int32, (S5_STATE, STATE_BLOCK), 0)).astype(F32)
    own = ((lax.broadcasted_iota(jnp.int32, (rows, STATE_BLOCK), 0) // S5_GROUP) % GROUPS_PER_BLOCK
           == lax.broadcasted_iota(jnp.int32, (rows, STATE_BLOCK), 1) // S5_STATE).astype(F32)

    def blocks(v):
        wide = jnp.dot(v.reshape(rows, S5_STATE), expand, precision=lax.Precision.HIGHEST,
                       preferred_element_type=F32)
        return (wide * own).reshape(S5_BLOCKS, LANE_BLOCK, STATE_BLOCK)

    btab = jnp.concatenate([blocks(bbr), blocks(bbi)], axis=2)
    ctab = jnp.concatenate([blocks(c_re), -blocks(c_im)], axis=2)
    return ar, ai, btab, ctab


_S5_PREP_OUT = [(S5_GROUPS, S5_STATE), (S5_GROUPS, S5_STATE), (S5_BLOCKS, LANE_BLOCK, 2 * STATE_BLOCK),
                (S5_BLOCKS, LANE_BLOCK, 2 * STATE_BLOCK)]


def s5_prep(params, *, name):
    def body(*refs):
        for ref, val in zip(refs[7:], _s5_prep(*[r[...] for r in refs[:7]]), strict=True):
            ref[...] = val

    return pl.pallas_call(body, name=name, out_shape=[jax.ShapeDtypeStruct(s, F32) for s in _S5_PREP_OUT],
                          compiler_params=pltpu.CompilerParams(vmem_limit_bytes=VMEM_LIMIT))(*params)


def s5_prep_bwd(params, cotangents, *, name):
    def body(*refs):
        grads = jax.vjp(_s5_prep, *[r[...] for r in refs[:7]])[1](tuple(r[...] for r in refs[7:11]))
        for ref, val in zip(refs[11:], grads, strict=True):
            ref[...] = val

    return pl.pallas_call(body, name=name, out_shape=[jax.ShapeDtypeStruct(p.shape, F32) for p in params],
                          compiler_params=pltpu.CompilerParams(vmem_limit_bytes=VMEM_LIMIT))(*params, *cotangents)


def _silu_grad(z):
    s = jax.nn.sigmoid(z)
    return s * (1.0 + z * (1.0 - s))


def conv_fwd(p, conv_w, conv_b, *, name):
    lp = p.shape[0]
    tr = lp // 16
    e = CONV_E
    hb = tr // 8

    def body(p_ref, cgh_ref, vh_ref, w_ref, b_ref, o_ref):
        i = pl.program_id(0)
        bg, cg, v, z = (p_ref[:, k * e:(k + 1) * e] for k in range(4))
        hc = cg * v
        halo = jnp.where(i > 0, cgh_ref[...] * vh_ref[...], 0.0)
        ext = jnp.concatenate([halo, hc], axis=0)
        w = w_ref[...]
        conv = (w[0:1] * pltpu.roll(ext, 2, 0)[8:] + w[1:2] * pltpu.roll(ext, 1, 0)[8:] + w[2:3] * hc
                + b_ref[...])
        o_ref[...] = (bg * conv * jax.nn.silu(z)).astype(o_ref.dtype)

    prev = lambda col: (lambda i: (jnp.maximum(i * hb - 1, 0), col))
    return pl.pallas_call(
        body, name=name, grid=(lp // tr,),
        in_specs=[pl.BlockSpec((tr, 4 * e), lambda i: (i, 0)),
                  pl.BlockSpec((8, e), prev(1)), pl.BlockSpec((8, e), prev(2)),
                  pl.BlockSpec((3, e), lambda i: (0, 0)), pl.BlockSpec((1, e), lambda i: (0, 0))],
        out_specs=pl.BlockSpec((tr, e), lambda i: (i, 0)),
        out_shape=jax.ShapeDtypeStruct((lp, e), BF16),
        compiler_params=_params("parallel"))(p, p, p, conv_w, conv_b)


def conv_bwd(dh, w_out, p, conv_w, conv_b, *, name):
    lp = p.shape[0]
    tr = lp // 16
    e = CONV_E
    d = dh.shape[1]
    hb = tr // 8
    nt = lp // tr
    width = 512

    def body(dh_ref, p_ref, cgh_ref, vh_ref, dhn_ref, bgn_ref, zn_ref, wo_ref, w_ref, b_ref, dp_ref, dw_ref, db_ref):
        i = pl.program_id(0)
        rows, rows_n = dh_ref[...].astype(BF16), dhn_ref[...].astype(BF16)
        for c0 in range(0, e, width):
            cols = slice(c0, c0 + width)
            wo = wo_ref[cols, :]
            dy = lax.dot_general(rows, wo, (((1,), (1,)), ((), ())), preferred_element_type=F32)
            dy_n = lax.dot_general(rows_n, wo, (((1,), (1,)), ((), ())), preferred_element_type=F32)
            bg, cg, v, z = (p_ref[:, k * e + c0:k * e + c0 + width] for k in range(4))
            w = w_ref[:, cols]
            hc = cg * v
            halo = jnp.where(i > 0, cgh_ref[:, cols] * vh_ref[:, cols], 0.0)
            ext = jnp.concatenate([halo, hc], axis=0)
            hc2, hc1 = pltpu.roll(ext, 2, 0)[8:], pltpu.roll(ext, 1, 0)[8:]
            yc = w[0:1] * hc2 + w[1:2] * hc1 + w[2:3] * hc + b_ref[:, cols]
            sz = jax.nn.silu(z)
            dyc = dy * bg * sz
            dyc_n = jnp.where(i < nt - 1, dy_n * bgn_ref[:, cols] * jax.nn.silu(zn_ref[:, cols]), 0.0)
            ext_n = jnp.concatenate([dyc, dyc_n], axis=0)
            n_rows = tr + 8
            dhc = (w[2:3] * dyc + w[1:2] * pltpu.roll(ext_n, n_rows - 1, 0)[:tr]
                   + w[0:1] * pltpu.roll(ext_n, n_rows - 2, 0)[:tr])
            for k, val in enumerate((dy * yc * sz, dhc * v, dhc * cg, dy * bg * yc * _silu_grad(z))):
                dp_ref[:, k * e + c0:k * e + c0 + width] = val.astype(dp_ref.dtype)
            dw = jnp.concatenate([_colsum(dyc * hc2), _colsum(dyc * hc1), _colsum(dyc * hc)], axis=0)
            db = _colsum(dyc)

            @pl.when(i == 0)
            def _(cols=cols, dw=dw, db=db):
                dw_ref[:, cols] = dw
                db_ref[:, cols] = db

            @pl.when(i > 0)
            def _(cols=cols, dw=dw, db=db):
                dw_ref[:, cols] += dw
                db_ref[:, cols] += db

    prev = lambda col: (lambda i: (jnp.maximum(i * hb - 1, 0), col))
    nxt = lambda col: (lambda i: (jnp.minimum((i + 1) * hb, lp // 8 - 1), col))
    return pl.pallas_call(
        body, name=name, grid=(nt,),
        in_specs=[pl.BlockSpec((tr, d), lambda i: (i, 0)), pl.BlockSpec((tr, 4 * e), lambda i: (i, 0)),
                  pl.BlockSpec((8, e), prev(1)), pl.BlockSpec((8, e), prev(2)),
                  pl.BlockSpec((8, d), nxt(0)), pl.BlockSpec((8, e), nxt(0)), pl.BlockSpec((8, e), nxt(3)),
                  pl.BlockSpec((None, e, d), lambda i: (0, 0, 0)),
                  pl.BlockSpec((3, e), lambda i: (0, 0)), pl.BlockSpec((1, e), lambda i: (0, 0))],
        out_specs=[pl.BlockSpec((tr, 4 * e), lambda i: (i, 0)), pl.BlockSpec((3, e), lambda i: (0, 0)),
                   pl.BlockSpec((1, e), lambda i: (0, 0))],
        out_shape=[jax.ShapeDtypeStruct((lp, 4 * e), BF16), jax.ShapeDtypeStruct((3, e), F32),
                   jax.ShapeDtypeStruct((1, e), F32)],
        compiler_params=_params("arbitrary"))(dh, p, p, p, dh, p, p, w_out, conv_w, conv_b)


def _pool_counts(i, tr, rows, offset, w):
    t = (i * tr + offset + 1 + lax.broadcasted_iota(jnp.int32, (rows, 1), 0)).astype(F32)
    return jnp.minimum(t, float(w))


def pool_fwd(p, *, name):
    lp = p.shape[0]
    tr = lp // 16
    h = POOL_HALO
    hb = tr // h

    def body(u_ref, uh_ref, o_ref):
        i = pl.program_id(0)
        u = u_ref[...]
        ext = jnp.concatenate([jnp.where(i > 0, uh_ref[...], 0.0), u], axis=0)
        for k, w in enumerate(POOL_WINDOWS):
            cols = slice(k * POOL_GROUP, (k + 1) * POOL_GROUP)
            s = ext[:, cols]
            d = 1
            while d < w:
                s = s + pltpu.roll(s, d, 0)
                d *= 2
            o_ref[:, cols] = (s[h:] / _pool_counts(i, tr, tr, 0, w) - u[:, cols]).astype(o_ref.dtype)

    return pl.pallas_call(
        body, name=name, grid=(lp // tr,),
        in_specs=[pl.BlockSpec((tr, POOL_E), lambda i: (i, 0)),
                  pl.BlockSpec((h, POOL_E), lambda i: (jnp.maximum(i * hb - 1, 0), 0))],
        out_specs=pl.BlockSpec((tr, POOL_E), lambda i: (i, 0)),
        out_shape=jax.ShapeDtypeStruct((lp, POOL_E), BF16),
        compiler_params=_params("parallel"))(p, p)


def pool_bwd(dmix, dp, *, name):
    lp = dmix.shape[0]
    tr = lp // 16
    h = POOL_HALO
    hb = tr // h
    nt = lp // tr

    def body(dm_ref, dmn_ref, _, du_ref):
        i = pl.program_id(0)
        dm = dm_ref[...]
        dmn = jnp.where(i < nt - 1, dmn_ref[...], 0.0)
        n_rows = tr + h
        for k, w in enumerate(POOL_WINDOWS):
            cols = slice(k * POOL_GROUP, (k + 1) * POOL_GROUP)
            s = jnp.concatenate([dm[:, cols] / _pool_counts(i, tr, tr, 0, w),
                                 dmn[:, cols] / _pool_counts(i, tr, h, tr, w)], axis=0)
            d = 1
            while d < w:
                s = s + pltpu.roll(s, n_rows - d, 0)
                d *= 2
            du_ref[:, cols] = (s[:tr] - dm[:, cols]).astype(du_ref.dtype)

    return pl.pallas_call(
        body, name=name, grid=(nt,),
        in_specs=[pl.BlockSpec((tr, POOL_E), lambda i: (i, 0)),
                  pl.BlockSpec((h, POOL_E), lambda i: (jnp.minimum((i + 1) * hb, lp // h - 1), 0)),
                  pl.BlockSpec(memory_space=pl.ANY)],
        out_specs=pl.BlockSpec((tr, POOL_E), lambda i: (i, 0)),
        out_shape=jax.ShapeDtypeStruct(dp.shape, dp.dtype),
        input_output_aliases={2: 0},
        compiler_params=_params("parallel"))(dmix, dmix, dp)


def _adamw_math(w, g, m, v):
    nm = ADAM_B1 * m + (1.0 - ADAM_B1) * g
    nv = ADAM_B2 * v + (1.0 - ADAM_B2) * jnp.square(g)
    m_hat = nm / (1.0 - ADAM_B1 ** ADAM_STEP)
    v_hat = nv / (1.0 - ADAM_B2 ** ADAM_STEP)
    return -ADAM_LR * (m_hat / (jnp.sqrt(v_hat) + ADAM_EPS) + ADAM_WD * w), nm, nv


def adamw_many(quads, *, name, steps=8, job=None):
    n = len(quads)

    def spec(shape):
        return pl.BlockSpec((shape[0] // steps,) + shape[1:], lambda i, nd=len(shape): (i,) + (0,) * (nd - 1))

    def body(*refs):
        for q in range(n):
            w_ref, g_ref, m_ref, v_ref = refs[4 * q:4 * q + 4]
            d_ref, nm_ref, nv_ref = refs[4 * n + 3 * q:4 * n + 3 * q + 3]
            d_ref[...], nm_ref[...], nv_ref[...] = _adamw_math(w_ref[...], g_ref[...], m_ref[...], v_ref[...])

    res = _call(
        body, name=name, grid=(steps,), in_specs=[spec(quad[0].shape) for quad in quads for _ in range(4)],
        out_specs=[spec(quad[0].shape) for quad in quads for _ in range(3)],
        out_shape=[jax.ShapeDtypeStruct(quad[0].shape, F32) for quad in quads for _ in range(3)],
        args=[a for quad in quads for a in quad], semantics=("parallel",), job=job)
    outs = res if job is None else res[0]
    triples = [tuple(outs[3 * q:3 * q + 3]) for q in range(n)]
    return triples if job is None else (triples, res[1])


WIRE_WIDTH = 1024
WIRE_ROWS = 1024


def _wire_plan(shapes):
    starts, row = [], 0
    for s in shapes:
        r, c = (1, s[0]) if len(s) == 1 else s
        starts.append(row)
        row += -(-(r * max(1, c // WIRE_WIDTH)) // 8) * 8
    assert row <= WIRE_ROWS, row
    return starts


def _wire_cells(shape):
    if len(shape) == 1:
        n = shape[0]
        if n <= WIRE_WIDTH:
            return [(0, n, (slice(None),))]
        return [(h, WIRE_WIDTH, (slice(h * WIRE_WIDTH, (h + 1) * WIRE_WIDTH),)) for h in range(n // WIRE_WIDTH)]
    r, c = shape
    if c <= WIRE_WIDTH:
        return [(None, c, (slice(None), slice(None)))]
    per = c // WIRE_WIDTH
    return [(j * per + h, WIRE_WIDTH, (j, slice(h * WIRE_WIDTH, (h + 1) * WIRE_WIDTH)))
            for j in range(r) for h in range(per)]


def pack_small(arrays, *, name):
    shapes = [a.shape for a in arrays]
    starts = _wire_plan(shapes)

    def body(*refs):
        out = refs[-1]
        out[...] = jnp.zeros_like(out)
        for ref, shape, row0 in zip(refs[:-1], shapes, starts, strict=True):
            for off, cols, idx in _wire_cells(shape):
                if off is None:
                    out[row0:row0 + shape[0], 0:cols] = ref[idx]
                else:
                    out[row0 + off, 0:cols] = ref[idx]

    return pl.pallas_call(body, name=name, out_shape=jax.ShapeDtypeStruct((WIRE_ROWS, WIRE_WIDTH), F32),
                          compiler_params=pltpu.CompilerParams(vmem_limit_bytes=VMEM_LIMIT))(*arrays)


def update_small(packed, starts, triples, shard_cells, chip, *, name):
    n = len(triples)

    def body(chip_ref, p_ref, *refs):
        k = chip_ref[0]
        ins, outs = refs[:3 * n], refs[3 * n:]

        def cut(row0, rows, cols):
            c0, c1 = cols
            if isinstance(rows, int):
                return p_ref[row0 + rows, c0:c1]
            return p_ref[row0 + rows[0]:row0 + rows[1], c0:c1]

        for q in range(n):
            w_ref, m_ref, v_ref = ins[3 * q:3 * q + 3]
            g_ref, d_ref, nm_ref, nv_ref = outs[4 * q:4 * q + 4]
            shape = w_ref.shape
            if shard_cells[q] is None:
                pieces = [(cut(starts[q], (0, shape[0]) if off is None else off, (0, cols)), idx)
                          for off, cols, idx in _wire_cells(shape)]
            else:
                by_chip = [shard_cells[q](kk) for kk in range(N_CHIPS)]
                pieces = []
                for i, (_, _, idx) in enumerate(by_chip[0]):
                    g = cut(starts[q], *by_chip[0][i][:2])
                    for kk in range(1, N_CHIPS):
                        g = jnp.where(k == kk, cut(starts[q], *by_chip[kk][i][:2]), g)
                    pieces.append((g, idx))
            for g, idx in pieces:
                g_ref[idx] = g
                d_ref[idx], nm_ref[idx], nv_ref[idx] = _adamw_math(w_ref[idx], g, m_ref[idx], v_ref[idx])

    whole = lambda a: pl.BlockSpec(a.shape, lambda i, c_ref, nd=a.ndim: (0,) * nd)
    flat = [a for t in triples for a in t]
    out_shape = [jax.ShapeDtypeStruct(t[0].shape, F32) for t in triples for _ in range(4)]
    res = pl.pallas_call(
        body, name=name,
        grid_spec=pltpu.PrefetchScalarGridSpec(
            num_scalar_prefetch=1, grid=(1,),
            in_specs=[whole(packed)] + [whole(a) for a in flat],
            out_specs=[pl.BlockSpec(s.shape, lambda i, c_ref, nd=len(s.shape): (0,) * nd) for s in out_shape]),
        out_shape=out_shape,
        compiler_params=_params("arbitrary"))(chip.reshape(1).astype(jnp.int32), packed, *flat)
    return [tuple(res[4 * q:4 * q + 4]) for q in range(n)]


ANY = pl.BlockSpec(memory_space=pl.ANY)


def _place():
    x, y, c = lax.axis_index("x"), lax.axis_index("y"), lax.axis_index("c")
    return x, y, c, [(1 - x, y), (x, 1 - y), (1 - x, 1 - y)]


DMA_PIECE_BYTES = 512 * 1024


def _pieces(src, dst):
    shape = src.shape
    rows, cols = shape[-2], shape[-1]
    item = jnp.dtype(src.dtype).itemsize
    unit = 32 // item
    want = max(1, (rows * cols * item) // DMA_PIECE_BYTES)
    n = 1
    if rows % unit == 0:
        n = max(d for d in range(1, rows // unit + 1) if (rows // unit) % d == 0 and d <= want)
    size = rows // n
    out = []
    for lead in (range(shape[0]) if len(shape) == 3 else [None]):
        for i in range(n):
            sel = (pl.ds(i * size, size), slice(None)) if lead is None else (lead, pl.ds(i * size, size), slice(None))
            out.append((src.at[sel], dst.at[sel]))
    return out


def _send(src, dst, send_sem, recv_sem, peer, start=True):
    if start:
        for s, d in _pieces(src, dst):
            pltpu.make_async_remote_copy(src_ref=s, dst_ref=d, send_sem=send_sem, recv_sem=recv_sem,
                                         device_id=peer, device_id_type=MESH).start()
    return pltpu.make_async_remote_copy(src_ref=src, dst_ref=dst, send_sem=send_sem, recv_sem=recv_sem,
                                        device_id=peer, device_id_type=MESH)


def run_job(job, *, name):
    n_in, n_out = len(job.arrays), len(job.out_shapes)

    def body(*refs):
        job.fn(refs[:n_in], refs[n_in:n_in + n_out], refs[n_in + n_out:], "both")

    return pl.pallas_call(body, name=name, in_specs=[ANY] * n_in, out_specs=[ANY] * n_out,
                          out_shape=job.out_shapes, scratch_shapes=job.sems,
                          input_output_aliases={a: a for a in job.aliased()})(*job.arrays)


def gather_chips(shards, *, name):
    n = len(shards)

    def body(*refs):
        ins, outs = refs[:n], refs[n:2 * n]
        send_sems, recv_sems = refs[2 * n:]
        x, y, c, chips = _place()
        k = 2 * x + y
        copies = []
        for a in range(n):
            for j, peer in enumerate([(px, py, c) for px, py in chips] + [(x, y, 1 - c)]):
                copies.append(_send(ins[a], outs[a].at[k], send_sems.at[a, j], recv_sems.at[a, j], peer))
        for cp in copies:
            cp.wait()

    return pl.pallas_call(
        body, name=name, in_specs=[ANY] * n, out_specs=[ANY] * n,
        out_shape=[jax.ShapeDtypeStruct((N_CHIPS,) + s.shape, s.dtype) for s in shards],
        scratch_shapes=[pltpu.SemaphoreType.DMA((n, 4)), pltpu.SemaphoreType.DMA((n, 4))])(*shards)


def gather_by_halves_job(shards):
    n = len(shards)

    def fn(ins, outs, sems, phase):
        send_sems, recv_sems = sems
        x, y, c, chips = _place()
        k = 2 * x + y
        sibling = (x, y, 1 - c)
        begin = phase != "finish"
        waits, arrivals = [], []
        for a in range(n):
            half = ins[a].shape[0] // 2
            waits.append(_send(ins[a], outs[a].at[k], send_sems.at[a, 6], recv_sems.at[a, 6], sibling, begin))
            mine = pl.ds(c * half, half)
            for j, (px, py) in enumerate(chips):
                arrivals.append(_send(ins[a].at[mine, :], outs[a].at[k, mine, :], send_sems.at[a, j],
                                      recv_sems.at[a, j], (px, py, c), begin))
        if phase == "start":
            return
        i = 0
        for a in range(n):
            half = ins[a].shape[0] // 2
            mine = pl.ds(c * half, half)
            for j, (px, py) in enumerate(chips):
                arrivals[i].wait_recv()
                landed = outs[a].at[2 * px + py, mine, :]
                waits.append(_send(landed, landed, send_sems.at[a, 3 + j], recv_sems.at[a, 3 + j], sibling))
                i += 1
        for cp in arrivals:
            cp.wait_send()
        for cp in waits:
            cp.wait()

    return Job(shards, [jax.ShapeDtypeStruct((N_CHIPS,) + s.shape, s.dtype) for s in shards],
               [pltpu.SemaphoreType.DMA((n, 7)), pltpu.SemaphoreType.DMA((n, 7))], fn)


def gather_halves_job(shards):
    n = len(shards)

    def fn(ins, outs, sems, phase):
        send_sems, recv_sems = sems
        x, y, c, chips = _place()
        k = 2 * x + y
        copies = []
        for a in range(n):
            half = ins[a].shape[0] // 2
            mine = pl.ds(c * half, half)
            copies.append(_send(ins[a], outs[a].at[k], send_sems.at[a, 3], recv_sems.at[a, 3], (x, y, 1 - c),
                                phase != "finish"))
            for j, (px, py) in enumerate(chips):
                copies.append(_send(ins[a].at[mine, :], outs[a].at[k, mine, :], send_sems.at[a, j],
                                    recv_sems.at[a, j], (px, py, c), phase != "finish"))
        if phase != "start":
            for cp in copies:
                cp.wait()

    return Job(shards, [jax.ShapeDtypeStruct((N_CHIPS,) + s.shape, s.dtype) for s in shards],
               [pltpu.SemaphoreType.DMA((n, 4)), pltpu.SemaphoreType.DMA((n, 4))], fn)


def forward_halves_job(gathered):
    n = len(gathered)

    def fn(ins, outs, sems, phase):
        send_sems, recv_sems = sems
        x, y, c, chips = _place()
        copies = []
        for a in range(n):
            half = outs[a].shape[1] // 2
            for j, (px, py) in enumerate(chips):
                landed = outs[a].at[2 * px + py, pl.ds(c * half, half), :]
                copies.append(_send(landed, landed, send_sems.at[a, j], recv_sems.at[a, j], (x, y, 1 - c),
                                    phase != "finish"))
        if phase != "start":
            for cp in copies:
                cp.wait()

    return Job(gathered, [jax.ShapeDtypeStruct(g.shape, g.dtype) for g in gathered],
               [pltpu.SemaphoreType.DMA((n, 3)), pltpu.SemaphoreType.DMA((n, 3))], fn, in_place=True)


def swap_job(grads):
    n = len(grads)

    def fn(ins, outs, sems, phase):
        send_sems, recv_sems = sems
        x, y, c, _ = _place()
        copies = []
        for a in range(n):
            half = ins[a].shape[1] // 2
            copies.append(_send(ins[a].at[:, pl.ds((1 - c) * half, half), :], outs[a], send_sems.at[a],
                                recv_sems.at[a], (x, y, 1 - c), phase != "finish"))
        if phase != "start":
            for cp in copies:
                cp.wait()

    return Job(grads, [jax.ShapeDtypeStruct((g.shape[0], g.shape[1] // 2, g.shape[2]), g.dtype) for g in grads],
               [pltpu.SemaphoreType.DMA((n,)), pltpu.SemaphoreType.DMA((n,))], fn)


def _chip_slot(s, k):
    rel = s ^ k
    return (rel >> 1) | ((rel & 1) << 1)


def sum_cores(g, theirs, ck, out_dtype, *, name):
    n, r, cols = g.shape
    half = r // 2
    tr = _tile(half, max(16, (1 << 19) // cols), 16)
    nb = half // tr

    def body(ck_ref, g_ref, t_ref, o_ref):
        o_ref[...] = (g_ref[...] + t_ref[...]).astype(o_ref.dtype)

    return pl.pallas_call(
        body, name=name,
        grid_spec=pltpu.PrefetchScalarGridSpec(
            num_scalar_prefetch=1, grid=(n, nb),
            in_specs=[pl.BlockSpec((None, tr, cols), lambda s, i, ck_ref: (s, ck_ref[0] * nb + i, 0)),
                      pl.BlockSpec((None, tr, cols), lambda s, i, ck_ref: (s, i, 0))],
            out_specs=pl.BlockSpec((None, tr, cols), lambda s, i, ck_ref: (_chip_slot(s, ck_ref[1]), i, 0))),
        out_shape=jax.ShapeDtypeStruct((n, half, cols), out_dtype),
        compiler_params=_params("parallel", "parallel"))(ck, g, theirs)


def scatter_job(parts):
    n = len(parts)

    def fn(ins, outs, sems, phase):
        send_sems, recv_sems = sems
        x, y, c, chips = _place()
        copies = []
        for a in range(n):
            for j, (px, py) in enumerate(chips):
                copies.append(_send(ins[a].at[1 + j], outs[a].at[j], send_sems.at[a, j], recv_sems.at[a, j],
                                    (px, py, c), phase != "finish"))
        if phase != "start":
            for cp in copies:
                cp.wait()

    return Job(parts, [jax.ShapeDtypeStruct((3,) + p.shape[1:], p.dtype) for p in parts],
               [pltpu.SemaphoreType.DMA((n, 3)), pltpu.SemaphoreType.DMA((n, 3))], fn)


def sum_chips(own, others, ck, *, name):
    _, r, cols = own.shape
    tr = _tile(r, max(16, (1 << 19) // cols), 16)

    def body(ck_ref, a_ref, b0_ref, b1_ref, b2_ref, o_ref):
        o_ref[...] = (a_ref[...].astype(F32) + b0_ref[...].astype(F32) + b1_ref[...].astype(F32)
                      + b2_ref[...].astype(F32))

    other = lambda j: pl.BlockSpec((None, tr, cols), lambda i, ck_ref: (j, i, 0))
    return pl.pallas_call(
        body, name=name,
        grid_spec=pltpu.PrefetchScalarGridSpec(
            num_scalar_prefetch=1, grid=(r // tr,),
            in_specs=[pl.BlockSpec((None, tr, cols), lambda i, ck_ref: (0, i, 0)), other(0), other(1), other(2)],
            out_specs=pl.BlockSpec((None, tr, cols), lambda i, ck_ref: (ck_ref[0], i, 0))),
        out_shape=jax.ShapeDtypeStruct((2, r, cols), F32),
        compiler_params=_params("parallel"))(ck, own, others, others, others)


def share_job(joined):
    n = len(joined)

    def fn(ins, outs, sems, phase):
        send_sems, recv_sems = sems
        x, y, c, _ = _place()
        copies = [_send(outs[a].at[c], outs[a].at[c], send_sems.at[a], recv_sems.at[a], (x, y, 1 - c),
                        phase != "finish") for a in range(n)]
        if phase != "start":
            for cp in copies:
                cp.wait()

    return Job(joined, [jax.ShapeDtypeStruct(j.shape, j.dtype) for j in joined],
               [pltpu.SemaphoreType.DMA((n,)), pltpu.SemaphoreType.DMA((n,))], fn, in_place=True)


def kernel(x, meta_tokens, norm0_g, l0_w_in, l0_lam_re, l0_lam_im, l0_log_dt, l0_b_re, l0_b_im, l0_c_re, l0_c_im, l0_d_skip, l0_w_glu, l0_b_glu, l0_w_out, norm1_g, l1_w_in, l1_conv_w, l1_conv_b, l1_w_out, norm2_g, l2_w_in, l2_w_grp, l2_b_grp, l2_scale, l2_w_out, norm3_g, l3_w_in, l3_lam_re, l3_lam_im, l3_log_dt, l3_b_re, l3_b_im, l3_c_re, l3_c_im, l3_d_skip, l3_w_glu, l3_b_glu, l3_w_out, final_g, loss_target, m_meta_tokens, m_norm0_g, m_l0_w_in, m_l0_lam_re, m_l0_lam_im, m_l0_log_dt, m_l0_b_re, m_l0_b_im, m_l0_c_re, m_l0_c_im, m_l0_d_skip, m_l0_w_glu, m_l0_b_glu, m_l0_w_out, m_norm1_g, m_l1_w_in, m_l1_conv_w, m_l1_conv_b, m_l1_w_out, m_norm2_g, m_l2_w_in, m_l2_w_grp, m_l2_b_grp, m_l2_scale, m_l2_w_out, m_norm3_g, m_l3_w_in, m_l3_lam_re, m_l3_lam_im, m_l3_log_dt, m_l3_b_re, m_l3_b_im, m_l3_c_re, m_l3_c_im, m_l3_d_skip, m_l3_w_glu, m_l3_b_glu, m_l3_w_out, m_final_g, v_meta_tokens, v_norm0_g, v_l0_w_in, v_l0_lam_re, v_l0_lam_im, v_l0_log_dt, v_l0_b_re, v_l0_b_im, v_l0_c_re, v_l0_c_im, v_l0_d_skip, v_l0_w_glu, v_l0_b_glu, v_l0_w_out, v_norm1_g, v_l1_w_in, v_l1_conv_w, v_l1_conv_b, v_l1_w_out, v_norm2_g, v_l2_w_in, v_l2_w_grp, v_l2_b_grp, v_l2_scale, v_l2_w_out, v_norm3_g, v_l3_w_in, v_l3_lam_re, v_l3_lam_im, v_l3_log_dt, v_l3_b_re, v_l3_b_im, v_l3_c_re, v_l3_c_im, v_l3_d_skip, v_l3_w_glu, v_l3_b_glu, v_l3_w_out, v_final_g):
    given = dict(locals())
    w = {n: given[n] for n in WEIGHTS}
    mom_m = {n: given["m_" + n] for n in WEIGHTS}
    mom_v = {n: given["v_" + n] for n in WEIGHTS}
    seq = x.shape[1]
    real = N_META + seq
    lp = -(-real // SCAN_ROWS) * SCAN_ROWS
    chip = 2 * lax.axis_index("x") + lax.axis_index("y")
    row = lambda a: a.reshape(1, -1)

    shard_bf16 = {n: (w[n].reshape(-1, w[n].shape[-1]) if n == 'l2_w_grp' else w[n]).astype(BF16) for n in BIG}
    layer = lambda i: [n for n in BIG if n.startswith("l%d_" % i)]
    over_ici = lambda names: gather_halves_job([shard_bf16[n] for n in names])
    full = {}

    def landed(names, arrays):
        for n, arr in zip(names, arrays, strict=True):
            if n.endswith('w_in'):
                full[n] = arr
            elif n == 'l2_w_grp':
                full[n] = arr.reshape(N_CHIPS, len(POOL_WINDOWS), POOL_GROUP // N_CHIPS, POOL_GROUP)
            else:
                full[n] = arr.reshape(1, -1, arr.shape[-1])

    landed(['l0_w_in'], run_job(gather_by_halves_job([shard_bf16['l0_w_in']]), name="gather_l0_w_in"))
    full.update(zip(SMALL_SHARDED, gather_chips([w[n] for n in SMALL_SHARDED], name="gather_small_shards"),
                    strict=True))
    meta_full = full['meta_tokens'].transpose(1, 0, 2).reshape(N_META, D_MODEL)
    conv_w_full = full['l1_conv_w'].transpose(1, 0, 2).reshape(3, CONV_E)
    b_grp_full = full['l2_b_grp'].transpose(1, 0, 2).reshape(len(POOL_WINDOWS), POOL_GROUP)

    h0 = jnp.concatenate([meta_full, x[0], jnp.zeros((lp - real, D_MODEL), F32)], axis=0)
    target = jnp.concatenate([jnp.zeros((N_META, D_MODEL), F32), loss_target[0],
                              jnp.zeros((lp - real, D_MODEL), F32)], axis=0)
    grads = {}
    saved = {}
    gip = lambda a: jnp.swapaxes(a, 1, 2)

    def s5_layer_fwd(i, h, first=False):
        pre = "l%d_" % i
        prm = [w[pre + 'lam_re'], w[pre + 'lam_im'], w[pre + 'log_dt'].reshape(-1, 1), gip(w[pre + 'b_re']),
               gip(w[pre + 'b_im']), w[pre + 'c_re'], w[pre + 'c_im']]
        a_re, a_im, bblk, cblk = s5_prep(prm, name=pre + "prep")
        tables = (a_re.reshape(1, -1), a_im.reshape(1, -1), bblk, cblk)
        ar, ai = tables[:2]
        gain = row(w["norm%d_g" % i])
        rest = [pre + 'w_glu', pre + 'w_out']
        if not first:
            uz, n = norm_mm_nn(h, gain, full[pre + 'w_in'], name=pre + "in")
            y0, yg, sr, si = s5_fwd(uz, bblk, cblk, ar, ai, row(w[pre + 'd_skip']), name=pre + "scan")
            q, y3 = glu_gate(yg, full[pre + 'w_glu'], row(w[pre + 'b_glu']), y0, uz, name=pre + "glu")
        else:
            (uz, n), halves = norm_mm_nn(h, gain, full[pre + 'w_in'], name=pre + "in", job=over_ici(rest))
            (y0, yg, sr, si), arrived = s5_fwd(
                uz, bblk, cblk, ar, ai, row(w[pre + 'd_skip']), name=pre + "scan",
                job=Job.both(over_ici(layer(1)), forward_halves_job(halves)))
            landed(rest, arrived[len(layer(1)):])
            (q, y3), whole = glu_gate(yg, full[pre + 'w_glu'], row(w[pre + 'b_glu']), y0, uz, name=pre + "glu",
                                      job=forward_halves_job(arrived[:len(layer(1))]))
            landed(layer(1), whole)
        h_new = mm_nn(y3, full[pre + 'w_out'], add=h, name=pre + "out")
        saved[i] = dict(h=h, n=n, uz=uz, y0=y0, sr=sr, si=si, yg=yg, q=q, y3=y3, tables=tables, prm=prm)
        return h_new

    def s5_layer_bwd(i, dh, carry=None):
        pre = "l%d_" % i
        s = saved[i]
        ar, ai, bblk, cblk = s['tables']
        grads[pre + 'w_out'] = mm_tn(s['y3'], dh, 1, name=pre + "d_w_out")[0]
        dy0_direct, dq, dp, db_glu = s5_gate_bwd(dh, full[pre + 'w_out'], s['y0'], s['q'], s['uz'],
                                                 name=pre + "d_y3")
        grads[pre + 'b_glu'] = db_glu
        grads[pre + 'w_glu'] = mm_tn(s['yg'], dq, 1, name=pre + "d_w_glu")[0]
        early = [] if carry is None else [pre + 'w_out', pre + 'w_glu']
        if carry is None:
            dyg = mm_nt(dq, full[pre + 'w_glu'], name=pre + "d_yg")
            res = s5_bwd(dy0_direct, dyg, s['y0'], s['uz'], s['sr'], s['si'], bblk, cblk, ar, ai,
                         row(w[pre + 'd_skip']), dp, name=pre + "d_scan")
        else:
            pieces = [chip_major(n) for n in early]
            dyg, theirs = mm_nt(dq, full[pre + 'w_glu'], name=pre + "d_yg", job=swap_job(pieces))
            sum_over_cores(early, pieces, theirs)
            res, arrived = s5_bwd(dy0_direct, dyg, s['y0'], s['uz'], s['sr'], s['si'], bblk, cblk, ar, ai,
                                  row(w[pre + 'd_skip']), dp, name=pre + "d_scan",
                                  job=scatter_job(scatter_of(carry) + [pair[n] for n in early]))
            from_chips.update(zip(layer(carry) + early, arrived, strict=True))
        dp, dbblk, dcblk, dar, dai, dd = res
        grads[pre + 'd_skip'] = dd
        cots = (dar.reshape(S5_GROUPS, S5_STATE), dai.reshape(S5_GROUPS, S5_STATE), dbblk, dcblk)
        for key, val in zip(('lam_re', 'lam_im', 'log_dt', 'b_re', 'b_im', 'c_re', 'c_im'),
                            s5_prep_bwd(s['prm'], cots, name=pre + "d_prep"), strict=True):
            grads[pre + key] = val.reshape(-1) if key == 'log_dt' else val
        if carry is None:
            grads[pre + 'w_in'] = mm_tn(s['n'], dp, N_CHIPS, name=pre + "d_w_in")
        else:
            later = [n for n in BIG if not n.startswith(pre)] + early
            sums = [sum_chips(pair[n], from_chips[n], ck, name="sum_chips_" + n) for n in later]
            grads[pre + 'w_in'], shared = mm_tn(s['n'], dp, N_CHIPS, name=pre + "d_w_in", job=share_job(sums))
            joined.update(zip(later, shared, strict=True))
        names = [n for n in layer(i) if n not in early]
        pieces = [chip_major(n) for n in names]
        (dh_in, dg), theirs = mm_nt_norm_bwd(dp, full[pre + 'w_in'], s['h'], row(w["norm%d_g" % i]), dh,
                                             name=pre + "d_n", job=swap_job(pieces))
        sum_over_cores(names, pieces, theirs)
        grads["norm%d_g" % i] = dg
        return dh_in

    h1 = s5_layer_fwd(0, h0, first=True)

    (p1, n1), halves = norm_mm_nn(h1, row(w['norm1_g']), full['l1_w_in'], name="l1_in", job=over_ici(layer(2)))
    y3_1 = conv_fwd(p1, conv_w_full, row(w['l1_conv_b']), name="l1_conv")
    h2, whole = mm_nn(y3_1, full['l1_w_out'], add=h1, name="l1_out", job=forward_halves_job(halves))
    landed(layer(2), whole)

    (p2, n2), halves = norm_mm_nn(h2, row(w['norm2_g']), full['l2_w_in'], name="l2_in", job=over_ici(layer(3)))
    mixed = pool_fwd(p2, name="l2_pool")
    o2, y3_2 = grp_nn_gate(mixed, full['l2_w_grp'], b_grp_full.reshape(1, -1), p2, row(w['l2_scale']), name="l2_grp")
    h3, whole = mm_nn(y3_2, full['l2_w_out'], add=h2, name="l2_out", job=forward_halves_job(halves))
    landed(layer(3), whole)

    h4 = s5_layer_fwd(3, h3)

    dh, grads['final_g'], sq = final_loss(h4, target, row(w['final_g']), N_META, real, name="loss")
    loss = lax.psum(0.5 / D_MODEL * jnp.sum(sq), ("x", "y", "c"))

    ck = jnp.stack([lax.axis_index("c"), chip]).astype(jnp.int32)
    pair, from_chips, joined = {}, {}, {}

    def chip_major(n):
        g = grads[n]
        if n.endswith('w_in'):
            return g
        if n == 'l2_w_grp':
            return g.reshape(N_CHIPS, -1, POOL_GROUP)
        return g.reshape(N_CHIPS, -1, g.shape[-1])

    def core_parts(i):
        return layer(i), [chip_major(n) for n in layer(i)]

    def sum_over_cores(names, pieces, theirs):
        for n, g, t in zip(names, pieces, theirs, strict=True):
            pair[n] = sum_cores(g, t, ck, F32 if n == 'small' else BF16, name="sum_cores_" + n)

    scatter_of = lambda i: [pair[n] for n in layer(i)]

    def scattered(i, arrays):
        from_chips.update(zip(layer(i), arrays, strict=True))

    dh = s5_layer_bwd(3, dh)

    grads['l2_w_out'] = mm_tn(y3_2, dh, 1, name="l2_d_w_out")[0]
    do2, dp2, dscale, dbgrp = pool_gate_bwd(dh, full['l2_w_out'], o2, p2, row(w['l2_scale']), name="l2_d_y3")
    grads['l2_scale'] = dscale
    grads['l2_b_grp'] = dbgrp
    grads['l2_w_grp'] = grp_tn(mixed, do2, name="l2_d_w_grp")
    dmix = grp_nt(do2, full['l2_w_grp'], name="l2_d_mixed")
    dp2 = pool_bwd(dmix, dp2, name="l2_d_pool")
    grads['l2_w_in'], arrived = mm_tn(n2, dp2, N_CHIPS, name="l2_d_w_in", job=scatter_job(scatter_of(3)))
    scattered(3, arrived)
    names, pieces = core_parts(2)
    (dh, dg), theirs = mm_nt_norm_bwd(dp2, full['l2_w_in'], h2, row(w['norm2_g']), dh, name="l2_d_n",
                                      job=swap_job(pieces))
    sum_over_cores(names, pieces, theirs)
    grads['norm2_g'] = dg

    grads['l1_w_out'] = mm_tn(y3_1, dh, 1, name="l1_d_w_out")[0]
    dp1, dcw, dcb = conv_bwd(dh, full['l1_w_out'], p1, conv_w_full, row(w['l1_conv_b']), name="l1_d_conv")
    grads['l1_conv_w'] = dcw
    grads['l1_conv_b'] = dcb
    grads['l1_w_in'], arrived = mm_tn(n1, dp1, N_CHIPS, name="l1_d_w_in", job=scatter_job(scatter_of(2)))
    scattered(2, arrived)
    names, pieces = core_parts(1)
    (dh, dg), theirs = mm_nt_norm_bwd(dp1, full['l1_w_in'], h1, row(w['norm1_g']), dh, name="l1_d_n",
                                      job=swap_job(pieces))
    sum_over_cores(names, pieces, theirs)
    grads['norm1_g'] = dg

    dh = s5_layer_bwd(0, dh, carry=1)
    grad_x = dh[N_META:real][None]
    grads['meta_tokens'] = dh[:N_META]

    wide = [n for n in SMALL if w[n].ndim == 3]
    wire = [grads[n].reshape(S5_GROUPS, -1) if n in wide else grads[n] for n in SMALL]
    starts = dict(zip(SMALL, _wire_plan([a.shape for a in wire]), strict=True))
    small_local = pack_small(wire, name="pack_small")
    pieces = [small_local.reshape(N_CHIPS, -1, WIRE_WIDTH)]
    sum_over_cores(['small'], pieces, run_job(swap_job(pieces), name="reduce_cores_small"))

    out_g, out_d, out_m, out_v = {}, {}, {}, {}
    as2d = lambda a: a.reshape(-1, a.shape[-1])

    def update_big(names, name, job=None):
        for n in names:
            out_g[n] = joined[n].reshape(w[n].shape)
        res = adamw_many([(as2d(w[n]), as2d(out_g[n]), as2d(mom_m[n]), as2d(mom_v[n])) for n in names],
                         name=name, job=job)
        triples, carried = res if job is not None else (res, None)
        for n, (d, nm, nv) in zip(names, triples, strict=True):
            out_d[n], out_m[n], out_v[n] = d.reshape(w[n].shape), nm.reshape(w[n].shape), nv.reshape(w[n].shape)
        return carried

    rest = ['l0_w_in', 'small']
    later = [n for n in BIG if n not in rest]
    from_chips.update(zip(rest, update_big(later, "adamw_big", scatter_job([pair[n] for n in rest])), strict=True))
    sums = [sum_chips(pair[n], from_chips[n], ck, name="sum_chips_" + n) for n in rest]
    joined.update(zip(rest, run_job(share_job(sums), name="share_cores"), strict=True))
    update_big(['l0_w_in'], "adamw_l0_w_in")
    (small_all,) = run_job(gather_by_halves_job([joined['small'].reshape(-1, WIRE_WIDTH)]), name="gather_small")
    small_all = small_all.reshape(WIRE_ROWS, WIRE_WIDTH)

    half_w = WIRE_WIDTH // 2
    shard_cells = {
        'meta_tokens': lambda k: [((0, N_META), (k * 256, (k + 1) * 256), (slice(None), slice(None)))],
        'l1_conv_w': lambda k: [(2 * j + k // 2, ((k % 2) * half_w, (k % 2 + 1) * half_w), (j, slice(None)))
                                for j in range(3)],
        'l2_b_grp': lambda k: [(j // 2, ((j % 2) * half_w + k * 128, (j % 2) * half_w + (k + 1) * 128),
                                (j, slice(None))) for j in range(len(POOL_WINDOWS))],
    }
    plain = [n for n in SMALL if n not in wide]
    res = update_small(small_all, [starts[n] for n in plain], [(w[n], mom_m[n], mom_v[n]) for n in plain],
                       [shard_cells.get(n) for n in plain], chip, name="adamw_small")
    for n, (g, d, nm, nv) in zip(plain, res, strict=True):
        out_g[n], out_d[n], out_m[n], out_v[n] = g, d, nm, nv
    as_gip = lambda n, a: gip(a) if n.endswith(('b_re', 'b_im')) else a
    g_gip = {n: small_all[starts[n]:starts[n] + S5_GROUPS].reshape(S5_GROUPS, S5_GROUP, S5_STATE) for n in wide}
    res = adamw_many([(as_gip(n, w[n]), g_gip[n], as_gip(n, mom_m[n]), as_gip(n, mom_v[n])) for n in wide],
                     name="adamw_s5_tensors")
    for n, (d, nm, nv) in zip(wide, res, strict=True):
        out_g[n], out_d[n], out_m[n], out_v[n] = as_gip(n, g_gip[n]), as_gip(n, d), as_gip(n, nm), as_gip(n, nv)

    return (loss, grad_x, *[out_g[n] for n in WEIGHTS], *[out_d[n] for n in WEIGHTS],
            *[out_m[n] for n in WEIGHTS], *[out_v[n] for n in WEIGHTS])
```

```python
import functools

import jax
import jax.numpy as jnp
from jax import lax
from jax.experimental import pallas as pl
from jax.experimental.pallas import tpu as pltpu

F32, BF16 = jnp.float32, jnp.bfloat16
MESH = pl.DeviceIdType.MESH

D_MODEL = 1024
N_META = 16
EPS = 1e-6
S5_GROUPS, S5_GROUP, S5_STATE = 64, 16, 64
LANE_BLOCK = 128
S5_BLOCKS = D_MODEL // LANE_BLOCK
GROUPS_PER_BLOCK = LANE_BLOCK // S5_GROUP
STATE_BLOCK = GROUPS_PER_BLOCK * S5_STATE
SCAN_ROWS = 256
SCAN_CHUNK = 1088
SUBLANES = 8
CONV_E = 2048
POOL_E = 2048
POOL_WINDOWS = (2, 4, 8, 16)
POOL_GROUP = 512
POOL_HALO = 16
N_CHIPS = 4

ADAM_LR, ADAM_B1, ADAM_B2, ADAM_EPS, ADAM_WD, ADAM_STEP = 0.001, 0.9, 0.999, 1e-08, 0.01, 10

VMEM_LIMIT = 48 * 1024 * 1024
TN_OPERAND_BYTES = 28 * 1024 * 1024

WEIGHTS = ['meta_tokens', 'norm0_g', 'l0_w_in', 'l0_lam_re', 'l0_lam_im', 'l0_log_dt', 'l0_b_re', 'l0_b_im',
           'l0_c_re', 'l0_c_im', 'l0_d_skip', 'l0_w_glu', 'l0_b_glu', 'l0_w_out', 'norm1_g', 'l1_w_in',
           'l1_conv_w', 'l1_conv_b', 'l1_w_out', 'norm2_g', 'l2_w_in', 'l2_w_grp', 'l2_b_grp', 'l2_scale',
           'l2_w_out', 'norm3_g', 'l3_w_in', 'l3_lam_re', 'l3_lam_im', 'l3_log_dt', 'l3_b_re', 'l3_b_im',
           'l3_c_re', 'l3_c_im', 'l3_d_skip', 'l3_w_glu', 'l3_b_glu', 'l3_w_out', 'final_g']
BIG = ['l0_w_in', 'l0_w_glu', 'l0_w_out', 'l1_w_in', 'l1_w_out', 'l2_w_in', 'l2_w_grp', 'l2_w_out',
       'l3_w_in', 'l3_w_glu', 'l3_w_out']
SMALL_SHARDED = ['meta_tokens', 'l1_conv_w', 'l2_b_grp']
SMALL = [n for n in WEIGHTS if n not in BIG]


def _params(*sem):
    return pltpu.CompilerParams(dimension_semantics=sem, vmem_limit_bytes=VMEM_LIMIT)


class Job:
    def __init__(self, arrays, out_shapes, sems, fn, in_place=False):
        self.arrays, self.out_shapes, self.sems, self.fn = list(arrays), list(out_shapes), list(sems), fn
        self.in_place = in_place
        assert len(self.arrays) == len(self.out_shapes)

    @staticmethod
    def both(first, second):
        na, ns = len(first.arrays), len(first.sems)

        def fn(ins, outs, sems, phase):
            first.fn(ins[:na], outs[:na], sems[:ns], phase)
            second.fn(ins[na:], outs[na:], sems[ns:], phase)

        job = Job(first.arrays + second.arrays, first.out_shapes + second.out_shapes, first.sems + second.sems, fn)
        job.in_place = [first.in_place] * na + [second.in_place] * len(second.arrays)
        return job

    def aliased(self):
        flags = self.in_place if isinstance(self.in_place, list) else [self.in_place] * len(self.arrays)
        return [a for a, flag in enumerate(flags) if flag]


def _call(body, *, name, grid, in_specs, out_specs, out_shape, args, semantics, scratch=(), aliases=None, job=None):
    if job is None:
        return pl.pallas_call(
            body, name=name, grid=grid, in_specs=list(in_specs), out_specs=list(out_specs),
            out_shape=list(out_shape), scratch_shapes=list(scratch), input_output_aliases=aliases or {},
            compiler_params=_params(*semantics))(*args)
    counts = [len(in_specs), len(job.arrays), len(out_specs), len(job.out_shapes), len(scratch), len(job.sems)]

    def hosted(*refs):
        parts, pos = [], 0
        for k in counts:
            parts.append(refs[pos:pos + k])
            pos += k
        ins, job_ins, outs, job_outs, scr, job_sems = parts
        steps = [pl.program_id(d) for d in range(len(grid))]
        first = functools.reduce(jnp.logical_and, [s == 0 for s in steps])
        last = functools.reduce(jnp.logical_and, [s == g - 1 for s, g in zip(steps, grid, strict=True)])

        @pl.when(first)
        def _():
            job.fn(job_ins, job_outs, job_sems, "start")

        body(*ins, *outs, *scr)

        @pl.when(last)
        def _():
            job.fn(job_ins, job_outs, job_sems, "finish")

    any_spec = pl.BlockSpec(memory_space=pl.ANY)
    aliases = dict(aliases or {})
    aliases.update({len(in_specs) + a: len(out_specs) + a for a in job.aliased()})
    res = pl.pallas_call(
        hosted, name=name, grid=grid, in_specs=list(in_specs) + [any_spec] * len(job.arrays),
        out_specs=list(out_specs) + [any_spec] * len(job.out_shapes),
        out_shape=list(out_shape) + job.out_shapes, scratch_shapes=list(scratch) + job.sems,
        input_output_aliases=aliases,
        compiler_params=_params(*["arbitrary"] * len(grid)))(*args, *job.arrays)
    return res[:len(out_specs)], res[len(out_specs):]


def _tile(n, cap, mult):
    best = None
    for t in range(mult, min(n, cap) + 1, mult):
        if n % t == 0:
            best = t
    assert best is not None, (n, cap, mult)
    return best


def _with_job(res, job):
    return res[0] if job is None else (res[0][0], res[1])


def mm_nn(a, b, *, name, bias=None, add=None, out_dtype=F32, job=None):
    m, k = a.shape
    s, _, ns = b.shape
    n = s * ns
    tm, tn = _tile(m, 1088, 16), _tile(ns, 1024, 128)
    per = ns // tn

    def body(*refs):
        a_ref, b_ref = refs[0], refs[1]
        o_ref = refs[-1]
        acc = jnp.dot(a_ref[...].astype(BF16), b_ref[...], preferred_element_type=F32)
        pos = 2
        if bias is not None:
            acc = acc + refs[pos][...]
            pos += 1
        if add is not None:
            acc = acc + refs[pos][...]
        o_ref[...] = acc.astype(o_ref.dtype)

    in_specs = [pl.BlockSpec((tm, k), lambda i, j: (i, 0)),
                pl.BlockSpec((None, k, tn), lambda i, j: (j // per, 0, j % per))]
    args = [a, b]
    if bias is not None:
        in_specs.append(pl.BlockSpec((1, tn), lambda i, j: (0, j)))
        args.append(bias)
    if add is not None:
        in_specs.append(pl.BlockSpec((tm, tn), lambda i, j: (i, j)))
        args.append(add)
    return _with_job(_call(
        body, name=name, grid=(m // tm, n // tn), in_specs=in_specs,
        out_specs=[pl.BlockSpec((tm, tn), lambda i, j: (i, j))],
        out_shape=[jax.ShapeDtypeStruct((m, n), out_dtype)], args=args,
        semantics=("parallel", "parallel"), job=job), job)


def norm_mm_nn(h, gain, b, *, name, job=None):
    m, k = h.shape
    s, _, ns = b.shape
    n = s * ns
    tm, tn = _tile(m, 1088, 16), _tile(ns, 1024, 128)
    per = ns // tn

    def body(h_ref, g_ref, b_ref, o_ref, n_ref):
        @pl.when(pl.program_id(1) == 0)
        def _():
            n_ref[...] = _rms(h_ref[...], g_ref[...]).astype(n_ref.dtype)

        o_ref[...] = jnp.dot(n_ref[...], b_ref[...], preferred_element_type=F32)

    res = _call(
        body, name=name, grid=(m // tm, n // tn),
        in_specs=[pl.BlockSpec((tm, k), lambda i, j: (i, 0)), pl.BlockSpec((1, k), lambda i, j: (0, 0)),
                  pl.BlockSpec((None, k, tn), lambda i, j: (j // per, 0, j % per))],
        out_specs=[pl.BlockSpec((tm, tn), lambda i, j: (i, j)), pl.BlockSpec((tm, k), lambda i, j: (i, 0))],
        out_shape=[jax.ShapeDtypeStruct((m, n), F32), jax.ShapeDtypeStruct((m, k), BF16)], args=(h, gain, b),
        semantics=("parallel", "arbitrary"), job=job)
    return (res[0], res[1]) if job is None else ((res[0][0], res[0][1]), res[1])


def glu_gate(yg, w_glu, b_glu, y0, uz, *, name, job=None):
    m, k = yg.shape
    n = w_glu.shape[2]
    tm = _tile(m, 1088, 16)

    def body(a_ref, w_ref, b_ref, y0_ref, z_ref, q_ref, o_ref):
        q = jnp.dot(a_ref[...], w_ref[...], preferred_element_type=F32) + b_ref[...]
        q_ref[...] = q
        o_ref[...] = _s5_gate(y0_ref[...], q, z_ref[...]).astype(o_ref.dtype)

    tile = pl.BlockSpec((tm, n), lambda i: (i, 0))
    res = _call(
        body, name=name, grid=(m // tm,),
        in_specs=[pl.BlockSpec((tm, k), lambda i: (i, 0)), pl.BlockSpec((None, k, n), lambda i: (0, 0, 0)),
                  pl.BlockSpec((1, n), lambda i: (0, 0)), tile, pl.BlockSpec((tm, n), lambda i: (i, 1))],
        out_specs=[tile, tile],
        out_shape=[jax.ShapeDtypeStruct((m, n), F32), jax.ShapeDtypeStruct((m, n), BF16)],
        args=(yg, w_glu, b_glu, y0, uz), semantics=("parallel",), job=job)
    return (res[0], res[1]) if job is None else ((res[0][0], res[0][1]), res[1])


def s5_gate_bwd(dh, w_out, y0, q, uz, *, name):
    m, d = dh.shape
    e = w_out.shape[1]
    tm = _tile(m, 544, 16)

    def body(dh_ref, w_ref, y0_ref, q_ref, z_ref, dy0_ref, dq_ref, dz_ref, db_ref):
        i = pl.program_id(0)
        dy3 = lax.dot_general(dh_ref[...].astype(BF16), w_ref[...], (((1,), (1,)), ((), ())),
                              preferred_element_type=F32)
        dy0, dq, dz = jax.vjp(_s5_gate, y0_ref[...], q_ref[...], z_ref[...])[1](dy3)
        dy0_ref[...] = dy0
        dq_ref[...] = dq.astype(dq_ref.dtype)
        dz_ref[...] = dz.astype(dz_ref.dtype)
        db = _colsum(dq)

        @pl.when(i == 0)
        def _():
            db_ref[...] = db

        @pl.when(i > 0)
        def _():
            db_ref[...] += db

    tile = pl.BlockSpec((tm, e), lambda i: (i, 0))
    right = pl.BlockSpec((tm, e), lambda i: (i, 1))
    return pl.pallas_call(
        body, name=name, grid=(m // tm,),
        in_specs=[pl.BlockSpec((tm, d), lambda i: (i, 0)), pl.BlockSpec((None, e, d), lambda i: (0, 0, 0)),
                  tile, tile, right],
        out_specs=[tile, tile, right, pl.BlockSpec((1, e), lambda i: (0, 0))],
        out_shape=[jax.ShapeDtypeStruct((m, e), F32), jax.ShapeDtypeStruct((m, e), BF16),
                   jax.ShapeDtypeStruct((m, 2 * e), BF16), jax.ShapeDtypeStruct((1, e), F32)],
        compiler_params=_params("arbitrary"))(dh, w_out, y0, q, uz)


def pool_gate_bwd(dh, w_out, o, p, scale, *, name):
    m, d = dh.shape
    e = w_out.shape[1]
    tm, te = _tile(m, 1088, 16), 1024
    nj = e // te

    def body(dh_ref, w_ref, o_ref, z_ref, s_ref, do_ref, dz_ref, ds_ref, db_ref):
        i = pl.program_id(1)
        dy3 = lax.dot_general(dh_ref[...].astype(BF16), w_ref[...], (((1,), (1,)), ((), ())),
                              preferred_element_type=F32)
        do, dz, ds = jax.vjp(_pool_gate, o_ref[...], z_ref[...], s_ref[...])[1](dy3)
        do_ref[...] = do.astype(do_ref.dtype)
        dz_ref[...] = dz.astype(dz_ref.dtype)
        db = _colsum(do)

        @pl.when(i == 0)
        def _():
            ds_ref[...] = ds
            db_ref[...] = db

        @pl.when(i > 0)
        def _():
            ds_ref[...] += ds
            db_ref[...] += db

    tile = pl.BlockSpec((tm, te), lambda j, i: (i, j))
    right = pl.BlockSpec((tm, te), lambda j, i: (i, nj + j))
    vec = pl.BlockSpec((1, te), lambda j, i: (0, j))
    return pl.pallas_call(
        body, name=name, grid=(nj, m // tm),
        in_specs=[pl.BlockSpec((tm, d), lambda j, i: (i, 0)), pl.BlockSpec((None, te, d), lambda j, i: (0, j, 0)),
                  tile, right, vec],
        out_specs=[tile, right, vec, vec],
        out_shape=[jax.ShapeDtypeStruct((m, e), BF16), jax.ShapeDtypeStruct((m, 2 * e), BF16),
                   jax.ShapeDtypeStruct((1, e), F32), jax.ShapeDtypeStruct((1, e), F32)],
        compiler_params=_params("parallel", "arbitrary"))(dh, w_out, o, p, scale)


def mm_nt_norm_bwd(g, w, h, gain, dh_out, *, name, job=None):
    m, kc = g.shape
    s, nout, kcs = w.shape
    assert s * kcs == kc
    tm, tk = _tile(m, 1088, 16), _tile(kcs, 1024, 128)
    per = kcs // tk
    nk = kc // tk

    def body(g_ref, w_ref, h_ref, gain_ref, dh_ref, o_ref, dg_ref):
        i, kk = pl.program_id(0), pl.program_id(1)
        part = lax.dot_general(g_ref[...].astype(BF16), w_ref[...], (((1,), (1,)), ((), ())),
                               preferred_element_type=F32)

        @pl.when(kk == 0)
        def _():
            o_ref[...] = part

        @pl.when(kk > 0)
        def _():
            o_ref[...] += part

        @pl.when(kk == nk - 1)
        def _():
            dh, dg = jax.vjp(_rms, h_ref[...], gain_ref[...])[1](o_ref[...])
            o_ref[...] = dh_ref[...] + dh

            @pl.when(i == 0)
            def _():
                dg_ref[...] = dg

            @pl.when(i > 0)
            def _():
                dg_ref[...] += dg

    row_tile = pl.BlockSpec((tm, nout), lambda i, kk: (i, 0))
    res = _call(
        body, name=name, grid=(m // tm, nk),
        in_specs=[pl.BlockSpec((tm, tk), lambda i, kk: (i, kk)),
                  pl.BlockSpec((None, nout, tk), lambda i, kk: (kk // per, 0, kk % per)),
                  row_tile, pl.BlockSpec((1, nout), lambda i, kk: (0, 0)), row_tile],
        out_specs=[row_tile, pl.BlockSpec((1, nout), lambda i, kk: (0, 0))],
        out_shape=[jax.ShapeDtypeStruct((m, nout), F32), jax.ShapeDtypeStruct((1, nout), F32)],
        args=(g, w, h, gain, dh_out), semantics=("arbitrary", "arbitrary"), job=job)
    return (res[0], res[1]) if job is None else ((res[0][0], res[0][1]), res[1])


def mm_nt(g, w, *, name, job=None):
    m, kc = g.shape
    s, nout, kcs = w.shape
    assert s * kcs == kc
    tm, tno, tk = _tile(m, 2176, 16), _tile(nout, 1024, 128), _tile(kcs, 1024, 128)
    per = kcs // tk

    def body(g_ref, w_ref, o_ref):
        kk = pl.program_id(2)
        part = lax.dot_general(g_ref[...].astype(BF16), w_ref[...], (((1,), (1,)), ((), ())),
                               preferred_element_type=F32)

        @pl.when(kk == 0)
        def _():
            o_ref[...] = part

        @pl.when(kk > 0)
        def _():
            o_ref[...] += part

    return _with_job(_call(
        body, name=name, grid=(m // tm, nout // tno, kc // tk),
        in_specs=[pl.BlockSpec((tm, tk), lambda i, j, kk: (i, kk)),
                  pl.BlockSpec((None, tno, tk), lambda i, j, kk: (kk // per, j, kk % per))],
        out_specs=[pl.BlockSpec((tm, tno), lambda i, j, kk: (i, j))],
        out_shape=[jax.ShapeDtypeStruct((m, nout), F32)], args=(g, w),
        semantics=("parallel", "parallel", "arbitrary"), job=job), job)


def mm_tn(a, g, shards, *, name, job=None):
    m, ka = a.shape
    n = g.shape[1]
    ns = n // shards
    tka, tn = _tile(ka, 1024, 128), _tile(ns, 512, 128)
    row_bytes = tka * jnp.dtype(a.dtype).itemsize + tn * jnp.dtype(g.dtype).itemsize
    tm = _tile(m, TN_OPERAND_BYTES // (2 * row_bytes), 16)
    per = ns // tn

    def body(a_ref, g_ref, o_ref):
        mm = pl.program_id(2)
        part = lax.dot_general(a_ref[...].astype(BF16), g_ref[...].astype(BF16), (((0,), (0,)), ((), ())),
                               preferred_element_type=F32)

        @pl.when(mm == 0)
        def _():
            o_ref[...] = part

        @pl.when(mm > 0)
        def _():
            o_ref[...] += part

    return _with_job(_call(
        body, name=name, grid=(ka // tka, n // tn, m // tm),
        in_specs=[pl.BlockSpec((tm, tka), lambda i, j, mm: (mm, i)),
                  pl.BlockSpec((tm, tn), lambda i, j, mm: (mm, j))],
        out_specs=[pl.BlockSpec((None, tka, tn), lambda i, j, mm: (j // per, i, j % per))],
        out_shape=[jax.ShapeDtypeStruct((shards, ka, ns), F32)], args=(a, g),
        semantics=("parallel", "parallel", "arbitrary"), job=job), job)


def grp_nn_gate(a, w, bias, p, scale, *, name):
    m = a.shape[0]
    tm = _tile(m, 1088, 16)
    ng = len(POOL_WINDOWS)

    def body(a_ref, w_ref, b_ref, z_ref, s_ref, o_ref, y_ref):
        wj = w_ref[...].reshape(POOL_GROUP, POOL_GROUP)
        o = jnp.dot(a_ref[...].astype(BF16), wj, preferred_element_type=F32) + b_ref[...]
        o_ref[...] = o
        y_ref[...] = _pool_gate(o, z_ref[...], s_ref[...]).astype(y_ref.dtype)

    tile = pl.BlockSpec((tm, POOL_GROUP), lambda i, j: (i, j))
    vec = pl.BlockSpec((1, POOL_GROUP), lambda i, j: (0, j))
    return pl.pallas_call(
        body, name=name, grid=(m // tm, ng),
        in_specs=[tile,
                  pl.BlockSpec((N_CHIPS, None, POOL_GROUP // N_CHIPS, POOL_GROUP), lambda i, j: (0, j, 0, 0)),
                  vec, pl.BlockSpec((tm, POOL_GROUP), lambda i, j: (i, ng + j)), vec],
        out_specs=[tile, tile],
        out_shape=[jax.ShapeDtypeStruct((m, ng * POOL_GROUP), F32), jax.ShapeDtypeStruct((m, ng * POOL_GROUP), BF16)],
        compiler_params=_params("parallel", "parallel"))(a, w, bias, p, scale)


def grp_nt(g, w, *, name):
    m = g.shape[0]
    tm = _tile(m, 1088, 16)
    ng = len(POOL_WINDOWS)

    def body(g_ref, w_ref, o_ref):
        wj = w_ref[...].reshape(POOL_GROUP, POOL_GROUP)
        o_ref[...] = lax.dot_general(g_ref[...].astype(BF16), wj, (((1,), (1,)), ((), ())),
                                     preferred_element_type=F32)

    return pl.pallas_call(
        body, name=name, grid=(m // tm, ng),
        in_specs=[pl.BlockSpec((tm, POOL_GROUP), lambda i, j: (i, j)),
                  pl.BlockSpec((N_CHIPS, None, POOL_GROUP // N_CHIPS, POOL_GROUP), lambda i, j: (0, j, 0, 0))],
        out_specs=pl.BlockSpec((tm, POOL_GROUP), lambda i, j: (i, j)),
        out_shape=jax.ShapeDtypeStruct((m, ng * POOL_GROUP), F32),
        compiler_params=_params("parallel", "parallel"))(g, w)


def grp_tn(a, g, *, name):
    m = a.shape[0]
    tm = _tile(m, 1088, 16)
    ng = len(POOL_WINDOWS)
    rows = POOL_GROUP // N_CHIPS

    def body(a_ref, g_ref, o_ref):
        mm = pl.program_id(1)
        part = lax.dot_general(a_ref[...].astype(BF16), g_ref[...].astype(BF16), (((0,), (0,)), ((), ())),
                               preferred_element_type=F32).reshape(N_CHIPS, rows, POOL_GROUP)

        @pl.when(mm == 0)
        def _():
            o_ref[...] = part

        @pl.when(mm > 0)
        def _():
            o_ref[...] += part

    return pl.pallas_call(
        body, name=name, grid=(ng, m // tm),
        in_specs=[pl.BlockSpec((tm, POOL_GROUP), lambda j, mm: (mm, j)),
                  pl.BlockSpec((tm, POOL_GROUP), lambda j, mm: (mm, j))],
        out_specs=pl.BlockSpec((N_CHIPS, None, rows, POOL_GROUP), lambda j, mm: (0, j, 0, 0)),
        out_shape=jax.ShapeDtypeStruct((N_CHIPS, ng, rows, POOL_GROUP), F32),
        compiler_params=_params("parallel", "arbitrary"))(a, g)


def final_loss(h, target, gain, first, last, *, name):
    m, d = h.shape
    tr = m // 16

    def body(h_ref, t_ref, g_ref, dh_ref, dg_ref, sq_ref):
        i = pl.program_id(0)
        y, vjp = jax.vjp(_rms, h_ref[...], g_ref[...])
        pos = i * tr + lax.broadcasted_iota(jnp.int32, (tr, 1), 0)
        diff = jnp.where((pos >= first) & (pos < last), y - t_ref[...], 0.0)
        dh_ref[...], dg = vjp(diff / d)
        sq = _colsum(diff * diff)

        @pl.when(i == 0)
        def _():
            dg_ref[...] = dg
            sq_ref[...] = sq

        @pl.when(i > 0)
        def _():
            dg_ref[...] += dg
            sq_ref[...] += sq

    tile, vec = pl.BlockSpec((tr, d), lambda i: (i, 0)), pl.BlockSpec((1, d), lambda i: (0, 0))
    return pl.pallas_call(
        body, name=name, grid=(m // tr,), in_specs=[tile, tile, vec], out_specs=[tile, vec, vec],
        out_shape=[jax.ShapeDtypeStruct((m, d), F32), jax.ShapeDtypeStruct((1, d), F32),
                   jax.ShapeDtypeStruct((1, d), F32)],
        compiler_params=_params("arbitrary"))(h, target, gain)


def _rms(h, g):
    return h * lax.rsqrt(jnp.mean(h * h, axis=-1, keepdims=True) + EPS) * g


def _colsum(v):
    return jnp.sum(v, axis=0, keepdims=True)


def _s5_gate(y0, q, z):
    yg = jax.nn.gelu(y0)
    return yg * jax.nn.sigmoid(q) * jax.nn.silu(z)


def _pool_gate(o, z, scale):
    return o * scale * jax.nn.silu(z)


def _shift_rows(v, d, down):
    t = v.shape[0]
    row = lax.broadcasted_iota(jnp.int32, v.shape, 0)
    if down:
        return jnp.where(row >= d, pltpu.roll(v, d, 0), 0.0)
    return jnp.where(row < t - d, pltpu.roll(v, t - d, 0), 0.0)


def _cmul(ar, ai, br, bi):
    return ar * br - ai * bi, ar * bi + ai * br


def _scan(xr, xi, ar, ai, cr, ci, down):
    t, n = xr.shape
    nb = t // SUBLANES
    pw = [(ar, ai)]
    for _ in range(SUBLANES - 1):
        pw.append(_cmul(*pw[-1], ar, ai))
    sub = lax.broadcasted_iota(jnp.int32, (SUBLANES, n), 0)

    def table(exps):
        tr, ti = jnp.zeros((SUBLANES, n), F32), jnp.zeros((SUBLANES, n), F32)
        for r, e in enumerate(exps):
            if e:
                tr, ti = jnp.where(sub == r, pw[e - 1][0], tr), jnp.where(sub == r, pw[e - 1][1], ti)
        return tr[None], ti[None]

    sr, si = xr.reshape(nb, SUBLANES, n), xi.reshape(nb, SUBLANES, n)
    d = 1
    while d < SUBLANES:
        mr, mi = table([d if (r >= d if down else r < SUBLANES - d) else 0 for r in range(SUBLANES)])
        turn = d if down else SUBLANES - d
        hr, hi = pltpu.roll(sr, turn, 1), pltpu.roll(si, turn, 1)
        sr, si = sr + mr * hr - mi * hi, si + mr * hi + mi * hr
        d *= 2
    edge = SUBLANES - 1 if down else 0
    qr, qi = table([r + 1 if down else SUBLANES - r for r in range(SUBLANES)])
    qr, qi = qr[0], qi[0]
    in_r, in_i = jnp.broadcast_to(cr, (SUBLANES, n)), jnp.broadcast_to(ci, (SUBLANES, n))
    out_r, out_i = [None] * nb, [None] * nb
    for b in (range(nb) if down else reversed(range(nb))):
        out_r[b] = sr[b] + qr * in_r - qi * in_i
        out_i[b] = si[b] + qr * in_i + qi * in_r
        in_r = jnp.broadcast_to(out_r[b][edge:edge + 1, :], (SUBLANES, n))
        in_i = jnp.broadcast_to(out_i[b][edge:edge + 1, :], (SUBLANES, n))
    return jnp.concatenate(out_r, axis=0), jnp.concatenate(out_i, axis=0), in_r[0:1, :], in_i[0:1, :]


def s5_fwd(uz, bblk, cblk, ar, ai, dskip, *, name, job=None):
    lp = uz.shape[0]
    t = _tile(lp, SCAN_CHUNK, 16)
    nst = S5_GROUPS * S5_STATE

    def body(u_ref, b_ref, c_ref, ar_ref, ai_ref, d_ref, y_ref, yg_ref, sr_ref, si_ref, cr, ci):
        step = pl.program_id(1)

        @pl.when(step == 0)
        def _():
            cr[...] = jnp.zeros_like(cr)
            ci[...] = jnp.zeros_like(ci)

        u = u_ref[...]
        a_r, a_i = ar_ref[...], ai_ref[...]
        x = jnp.dot(u.astype(BF16), b_ref[...].astype(BF16), preferred_element_type=F32)
        sr, si, cr[...], ci[...] = _scan(x[:, :STATE_BLOCK], x[:, STATE_BLOCK:], a_r, a_i, cr[...], ci[...], True)
        sr_ref[...] = sr
        si_ref[...] = si
        s = jnp.concatenate([sr, si], axis=1).astype(BF16)
        y0 = lax.dot_general(s, c_ref[...].astype(BF16), (((1,), (1,)), ((), ())),
                             preferred_element_type=F32) + d_ref[...] * u
        y_ref[...] = y0
        yg_ref[...] = jax.nn.gelu(y0).astype(yg_ref.dtype)

    return _call(
        body, name=name, grid=(S5_BLOCKS, lp // t),
        in_specs=[pl.BlockSpec((t, LANE_BLOCK), lambda b, c: (c, b)),
                  pl.BlockSpec((None, LANE_BLOCK, 2 * STATE_BLOCK), lambda b, c: (b, 0, 0)),
                  pl.BlockSpec((None, LANE_BLOCK, 2 * STATE_BLOCK), lambda b, c: (b, 0, 0)),
                  pl.BlockSpec((1, STATE_BLOCK), lambda b, c: (0, b)),
                  pl.BlockSpec((1, STATE_BLOCK), lambda b, c: (0, b)),
                  pl.BlockSpec((1, LANE_BLOCK), lambda b, c: (0, b))],
        out_specs=[pl.BlockSpec((t, LANE_BLOCK), lambda b, c: (c, b)),
                   pl.BlockSpec((t, LANE_BLOCK), lambda b, c: (c, b)),
                   pl.BlockSpec((t, STATE_BLOCK), lambda b, c: (c, b)),
                   pl.BlockSpec((t, STATE_BLOCK), lambda b, c: (c, b))],
        out_shape=[jax.ShapeDtypeStruct((lp, D_MODEL), F32), jax.ShapeDtypeStruct((lp, D_MODEL), BF16),
                   jax.ShapeDtypeStruct((lp, nst), F32), jax.ShapeDtypeStruct((lp, nst), F32)],
        scratch=[pltpu.VMEM((1, STATE_BLOCK), F32), pltpu.VMEM((1, STATE_BLOCK), F32)],
        args=(uz, bblk, cblk, ar, ai, dskip), semantics=("parallel", "arbitrary"), job=job)


def s5_bwd(dy_direct, dyg, y0, uz, sr, si, bblk, cblk, ar, ai, dskip, dp, *, name, job=None):
    lp = uz.shape[0]
    t = _tile(lp, SCAN_CHUNK, 16)
    nch = lp // t
    nst = S5_GROUPS * S5_STATE

    def body(dyd_ref, dyg_ref, y0_ref, u_ref, sr_ref, si_ref, b_ref, c_ref, ar_ref, ai_ref, d_ref, _, du_ref,
             db_ref, dc_ref, dar_ref, dai_ref, dd_ref, cr, ci):
        step = pl.program_id(1)

        @pl.when(step == 0)
        def _():
            cr[...] = jnp.zeros_like(cr)
            ci[...] = jnp.zeros_like(ci)

        dy = dyd_ref[...] + jax.vjp(jax.nn.gelu, y0_ref[...])[1](dyg_ref[...])[0]
        u = u_ref[...]
        dyb, ub = dy.astype(BF16), u.astype(BF16)
        a_r, a_i = ar_ref[...], -ai_ref[...]
        g = jnp.dot(dyb, c_ref[...].astype(BF16), preferred_element_type=F32)
        gr, gi = g[:, :STATE_BLOCK], g[:, STATE_BLOCK:]
        nxt_r, nxt_i = cr[...], ci[...]
        last = lax.broadcasted_iota(jnp.int32, gr.shape, 0) == t - 1
        lr, li, cr[...], ci[...] = _scan(gr, gi, a_r, a_i, nxt_r, nxt_i, False)
        nr = _shift_rows(lr, 1, False) + jnp.where(last, nxt_r, 0.0)
        ni = _shift_rows(li, 1, False) + jnp.where(last, nxt_i, 0.0)
        s_r, s_i = sr_ref[...], si_ref[...]
        dar = _colsum(s_r * nr + s_i * ni)
        dai = _colsum(s_r * ni - s_i * nr)
        lam = jnp.concatenate([lr, li], axis=1).astype(BF16)
        sb = jnp.concatenate([s_r, s_i], axis=1).astype(BF16)
        dc = lax.dot_general(dyb, sb, (((0,), (0,)), ((), ())), preferred_element_type=F32)
        db = lax.dot_general(ub, lam, (((0,), (0,)), ((), ())), preferred_element_type=F32)
        du_ref[...] = (lax.dot_general(lam, b_ref[...].astype(BF16), (((1,), (1,)), ((), ())),
                                       preferred_element_type=F32) + d_ref[...] * dy).astype(du_ref.dtype)
        dd = _colsum(dy * u)

        @pl.when(step == 0)
        def _():
            db_ref[...] = db
            dc_ref[...] = dc
            dar_ref[...] = dar
            dai_ref[...] = dai
            dd_ref[...] = dd

        @pl.when(step > 0)
        def _():
            db_ref[...] += db
            dc_ref[...] += dc
            dar_ref[...] += dar
            dai_ref[...] += dai
            dd_ref[...] += dd

    rev = lambda b, c: (nch - 1 - c, b)
    return _call(
        body, name=name, grid=(S5_BLOCKS, nch),
        in_specs=[pl.BlockSpec((t, LANE_BLOCK), rev), pl.BlockSpec((t, LANE_BLOCK), rev),
                  pl.BlockSpec((t, LANE_BLOCK), rev), pl.BlockSpec((t, LANE_BLOCK), rev),
                  pl.BlockSpec((t, STATE_BLOCK), rev), pl.BlockSpec((t, STATE_BLOCK), rev),
                  pl.BlockSpec((None, LANE_BLOCK, 2 * STATE_BLOCK), lambda b, c: (b, 0, 0)),
                  pl.BlockSpec((None, LANE_BLOCK, 2 * STATE_BLOCK), lambda b, c: (b, 0, 0)),
                  pl.BlockSpec((1, STATE_BLOCK), lambda b, c: (0, b)),
                  pl.BlockSpec((1, STATE_BLOCK), lambda b, c: (0, b)),
                  pl.BlockSpec((1, LANE_BLOCK), lambda b, c: (0, b)),
                  pl.BlockSpec(memory_space=pl.ANY)],
        out_specs=[pl.BlockSpec((t, LANE_BLOCK), rev),
                   pl.BlockSpec((None, LANE_BLOCK, 2 * STATE_BLOCK), lambda b, c: (b, 0, 0)),
                   pl.BlockSpec((None, LANE_BLOCK, 2 * STATE_BLOCK), lambda b, c: (b, 0, 0)),
                   pl.BlockSpec((1, STATE_BLOCK), lambda b, c: (0, b)),
                   pl.BlockSpec((1, STATE_BLOCK), lambda b, c: (0, b)),
                   pl.BlockSpec((1, LANE_BLOCK), lambda b, c: (0, b))],
        out_shape=[jax.ShapeDtypeStruct(dp.shape, dp.dtype),
                   jax.ShapeDtypeStruct((S5_BLOCKS, LANE_BLOCK, 2 * STATE_BLOCK), F32),
                   jax.ShapeDtypeStruct((S5_BLOCKS, LANE_BLOCK, 2 * STATE_BLOCK), F32),
                   jax.ShapeDtypeStruct((1, nst), F32), jax.ShapeDtypeStruct((1, nst), F32),
                   jax.ShapeDtypeStruct((1, D_MODEL), F32)],
        scratch=[pltpu.VMEM((1, STATE_BLOCK), F32), pltpu.VMEM((1, STATE_BLOCK), F32)],
        aliases={11: 0}, args=(dy_direct, dyg, y0, uz, sr, si, bblk, cblk, ar, ai, dskip, dp),
        semantics=("parallel", "arbitrary"), job=job)


def _s5_prep(lam_re, lam_im, log_dt, b_re, b_im, c_re, c_im):
    dt = jnp.exp(log_dt)
    mag = jnp.exp(lam_re * dt)
    ar = mag * jnp.cos(lam_im * dt)
    ai = mag * jnp.sin(lam_im * dt)
    den = lam_re * lam_re + lam_im * lam_im
    kr = ((ar - 1.0) * lam_re + ai * lam_im) / den
    ki = (ai * lam_re - (ar - 1.0) * lam_im) / den
    bbr = kr[:, None, :] * b_re - ki[:, None, :] * b_im
    bbi = kr[:, None, :] * b_im + ki[:, None, :] * b_re
    rows = S5_GROUPS * S5_GROUP
    expand = (lax.broadcasted_iota(jnp.int32, (S5_STATE, STATE_BLOCK), 1) % S5_STATE
              == lax.broadcasted_iota(jnp.int32, (S5_STATE, STATE_BLOCK), 0)).astype(F32)
    own = ((lax.broadcasted_iota(jnp.int32, (rows, STATE_BLOCK), 0) // S5_GROUP) % GROUPS_PER_BLOCK
           == lax.broadcasted_iota(jnp.int32, (rows, STATE_BLOCK), 1) // S5_STATE).astype(F32)

    def blocks(v):
        wide = jnp.dot(v.reshape(rows, S5_STATE), expand, precision=lax.Precision.HIGHEST,
                       preferred_element_type=F32)
        return (wide * own).reshape(S5_BLOCKS, LANE_BLOCK, STATE_BLOCK)

    btab = jnp.concatenate([blocks(bbr), blocks(bbi)], axis=2)
    ctab = jnp.concatenate([blocks(c_re), -blocks(c_im)], axis=2)
    return ar, ai, btab, ctab


_S5_PREP_OUT = [(S5_GROUPS, S5_STATE), (S5_GROUPS, S5_STATE), (S5_BLOCKS, LANE_BLOCK, 2 * STATE_BLOCK),
                (S5_BLOCKS, LANE_BLOCK, 2 * STATE_BLOCK)]


def s5_prep(params, *, name):
    def body(*refs):
        for ref, val in zip(refs[7:], _s5_prep(*[r[...] for r in refs[:7]]), strict=True):
            ref[...] = val

    return pl.pallas_call(body, name=name, out_shape=[jax.ShapeDtypeStruct(s, F32) for s in _S5_PREP_OUT],
                          compiler_params=pltpu.CompilerParams(vmem_limit_bytes=VMEM_LIMIT))(*params)


def s5_prep_bwd(params, cotangents, *, name):
    def body(*refs):
        grads = jax.vjp(_s5_prep, *[r[...] for r in refs[:7]])[1](tuple(r[...] for r in refs[7:11]))
        for ref, val in zip(refs[11:], grads, strict=True):
            ref[...] = val

    return pl.pallas_call(body, name=name, out_shape=[jax.ShapeDtypeStruct(p.shape, F32) for p in params],
                          compiler_params=pltpu.CompilerParams(vmem_limit_bytes=VMEM_LIMIT))(*params, *cotangents)


def _silu_grad(z):
    s = jax.nn.sigmoid(z)
    return s * (1.0 + z * (1.0 - s))


def conv_fwd(p, conv_w, conv_b, *, name):
    lp = p.shape[0]
    tr = lp // 16
    e = CONV_E
    hb = tr // 8

    def body(p_ref, cgh_ref, vh_ref, w_ref, b_ref, o_ref):
        i = pl.program_id(0)
        bg, cg, v, z = (p_ref[:, k * e:(k + 1) * e] for k in range(4))
        hc = cg * v
        halo = jnp.where(i > 0, cgh_ref[...] * vh_ref[...], 0.0)
        ext = jnp.concatenate([halo, hc], axis=0)
        w = w_ref[...]
        conv = (w[0:1] * pltpu.roll(ext, 2, 0)[8:] + w[1:2] * pltpu.roll(ext, 1, 0)[8:] + w[2:3] * hc
                + b_ref[...])
        o_ref[...] = (bg * conv * jax.nn.silu(z)).astype(o_ref.dtype)

    prev = lambda col: (lambda i: (jnp.maximum(i * hb - 1, 0), col))
    return pl.pallas_call(
        body, name=name, grid=(lp // tr,),
        in_specs=[pl.BlockSpec((tr, 4 * e), lambda i: (i, 0)),
                  pl.BlockSpec((8, e), prev(1)), pl.BlockSpec((8, e), prev(2)),
                  pl.BlockSpec((3, e), lambda i: (0, 0)), pl.BlockSpec((1, e), lambda i: (0, 0))],
        out_specs=pl.BlockSpec((tr, e), lambda i: (i, 0)),
        out_shape=jax.ShapeDtypeStruct((lp, e), BF16),
        compiler_params=_params("parallel"))(p, p, p, conv_w, conv_b)


def conv_bwd(dh, w_out, p, conv_w, conv_b, *, name):
    lp = p.shape[0]
    tr = lp // 16
    e = CONV_E
    d = dh.shape[1]
    hb = tr // 8
    nt = lp // tr
    width = 512

    def body(dh_ref, p_ref, cgh_ref, vh_ref, dhn_ref, bgn_ref, zn_ref, wo_ref, w_ref, b_ref, dp_ref, dw_ref, db_ref):
        i = pl.program_id(0)
        rows, rows_n = dh_ref[...].astype(BF16), dhn_ref[...].astype(BF16)
        for c0 in range(0, e, width):
            cols = slice(c0, c0 + width)
            wo = wo_ref[cols, :]
            dy = lax.dot_general(rows, wo, (((1,), (1,)), ((), ())), preferred_element_type=F32)
            dy_n = lax.dot_general(rows_n, wo, (((1,), (1,)), ((), ())), preferred_element_type=F32)
            bg, cg, v, z = (p_ref[:, k * e + c0:k * e + c0 + width] for k in range(4))
            w = w_ref[:, cols]
            hc = cg * v
            halo = jnp.where(i > 0, cgh_ref[:, cols] * vh_ref[:, cols], 0.0)
            ext = jnp.concatenate([halo, hc], axis=0)
            hc2, hc1 = pltpu.roll(ext, 2, 0)[8:], pltpu.roll(ext, 1, 0)[8:]
            yc = w[0:1] * hc2 + w[1:2] * hc1 + w[2:3] * hc + b_ref[:, cols]
            sz = jax.nn.silu(z)
            dyc = dy * bg * sz
            dyc_n = jnp.where(i < nt - 1, dy_n * bgn_ref[:, cols] * jax.nn.silu(zn_ref[:, cols]), 0.0)
            ext_n = jnp.concatenate([dyc, dyc_n], axis=0)
            n_rows = tr + 8
            dhc = (w[2:3] * dyc + w[1:2] * pltpu.roll(ext_n, n_rows - 1, 0)[:tr]
                   + w[0:1] * pltpu.roll(ext_n, n_rows - 2, 0)[:tr])
            for k, val in enumerate((dy * yc * sz, dhc * v, dhc * cg, dy * bg * yc * _silu_grad(z))):
                dp_ref[:, k * e + c0:k * e + c0 + width] = val.astype(dp_ref.dtype)
            dw = jnp.concatenate([_colsum(dyc * hc2), _colsum(dyc * hc1), _colsum(dyc * hc)], axis=0)
            db = _colsum(dyc)

            @pl.when(i == 0)
            def _(cols=cols, dw=dw, db=db):
                dw_ref[:, cols] = dw
                db_ref[:, cols] = db

            @pl.when(i > 0)
            def _(cols=cols, dw=dw, db=db):
                dw_ref[:, cols] += dw
                db_ref[:, cols] += db

    prev = lambda col: (lambda i: (jnp.maximum(i * hb - 1, 0), col))
    nxt = lambda col: (lambda i: (jnp.minimum((i + 1) * hb, lp // 8 - 1), col))
    return pl.pallas_call(
        body, name=name, grid=(nt,),
        in_specs=[pl.BlockSpec((tr, d), lambda i: (i, 0)), pl.BlockSpec((tr, 4 * e), lambda i: (i, 0)),
                  pl.BlockSpec((8, e), prev(1)), pl.BlockSpec((8, e), prev(2)),
                  pl.BlockSpec((8, d), nxt(0)), pl.BlockSpec((8, e), nxt(0)), pl.BlockSpec((8, e), nxt(3)),
                  pl.BlockSpec((None, e, d), lambda i: (0, 0, 0)),
                  pl.BlockSpec((3, e), lambda i: (0, 0)), pl.BlockSpec((1, e), lambda i: (0, 0))],
        out_specs=[pl.BlockSpec((tr, 4 * e), lambda i: (i, 0)), pl.BlockSpec((3, e), lambda i: (0, 0)),
                   pl.BlockSpec((1, e), lambda i: (0, 0))],
        out_shape=[jax.ShapeDtypeStruct((lp, 4 * e), BF16), jax.ShapeDtypeStruct((3, e), F32),
                   jax.ShapeDtypeStruct((1, e), F32)],
        compiler_params=_params("arbitrary"))(dh, p, p, p, dh, p, p, w_out, conv_w, conv_b)


def _pool_counts(i, tr, rows, offset, w):
    t = (i * tr + offset + 1 + lax.broadcasted_iota(jnp.int32, (rows, 1), 0)).astype(F32)
    return jnp.minimum(t, float(w))


def pool_fwd(p, *, name):
    lp = p.shape[0]
    tr = lp // 16
    h = POOL_HALO
    hb = tr // h

    def body(u_ref, uh_ref, o_ref):
        i = pl.program_id(0)
        u = u_ref[...]
        ext = jnp.concatenate([jnp.where(i > 0, uh_ref[...], 0.0), u], axis=0)
        for k, w in enumerate(POOL_WINDOWS):
            cols = slice(k * POOL_GROUP, (k + 1) * POOL_GROUP)
            s = ext[:, cols]
            d = 1
            while d < w:
                s = s + pltpu.roll(s, d, 0)
                d *= 2
            o_ref[:, cols] = (s[h:] / _pool_counts(i, tr, tr, 0, w) - u[:, cols]).astype(o_ref.dtype)

    return pl.pallas_call(
        body, name=name, grid=(lp // tr,),
        in_specs=[pl.BlockSpec((tr, POOL_E), lambda i: (i, 0)),
                  pl.BlockSpec((h, POOL_E), lambda i: (jnp.maximum(i * hb - 1, 0), 0))],
        out_specs=pl.BlockSpec((tr, POOL_E), lambda i: (i, 0)),
        out_shape=jax.ShapeDtypeStruct((lp, POOL_E), BF16),
        compiler_params=_params("parallel"))(p, p)


def pool_bwd(dmix, dp, *, name):
    lp = dmix.shape[0]
    tr = lp // 16
    h = POOL_HALO
    hb = tr // h
    nt = lp // tr

    def body(dm_ref, dmn_ref, _, du_ref):
        i = pl.program_id(0)
        dm = dm_ref[...]
        dmn = jnp.where(i < nt - 1, dmn_ref[...], 0.0)
        n_rows = tr + h
        for k, w in enumerate(POOL_WINDOWS):
            cols = slice(k * POOL_GROUP, (k + 1) * POOL_GROUP)
            s = jnp.concatenate([dm[:, cols] / _pool_counts(i, tr, tr, 0, w),
                                 dmn[:, cols] / _pool_counts(i, tr, h, tr, w)], axis=0)
            d = 1
            while d < w:
                s = s + pltpu.roll(s, n_rows - d, 0)
                d *= 2
            du_ref[:, cols] = (s[:tr] - dm[:, cols]).astype(du_ref.dtype)

    return pl.pallas_call(
        body, name=name, grid=(nt,),
        in_specs=[pl.BlockSpec((tr, POOL_E), lambda i: (i, 0)),
                  pl.BlockSpec((h, POOL_E), lambda i: (jnp.minimum((i + 1) * hb, lp // h - 1), 0)),
                  pl.BlockSpec(memory_space=pl.ANY)],
        out_specs=pl.BlockSpec((tr, POOL_E), lambda i: (i, 0)),
        out_shape=jax.ShapeDtypeStruct(dp.shape, dp.dtype),
        input_output_aliases={2: 0},
        compiler_params=_params("parallel"))(dmix, dmix, dp)


def _adamw_math(w, g, m, v):
    nm = ADAM_B1 * m + (1.0 - ADAM_B1) * g
    nv = ADAM_B2 * v + (1.0 - ADAM_B2) * jnp.square(g)
    m_hat = nm / (1.0 - ADAM_B1 ** ADAM_STEP)
    v_hat = nv / (1.0 - ADAM_B2 ** ADAM_STEP)
    return -ADAM_LR * (m_hat / (jnp.sqrt(v_hat) + ADAM_EPS) + ADAM_WD * w), nm, nv


def adamw_many(quads, *, name, steps=8, job=None):
    n = len(quads)

    def spec(shape):
        return pl.BlockSpec((shape[0] // steps,) + shape[1:], lambda i, nd=len(shape): (i,) + (0,) * (nd - 1))

    def body(*refs):
        for q in range(n):
            w_ref, g_ref, m_ref, v_ref = refs[4 * q:4 * q + 4]
            d_ref, nm_ref, nv_ref = refs[4 * n + 3 * q:4 * n + 3 * q + 3]
            d_ref[...], nm_ref[...], nv_ref[...] = _adamw_math(w_ref[...], g_ref[...], m_ref[...], v_ref[...])

    res = _call(
        body, name=name, grid=(steps,), in_specs=[spec(quad[0].shape) for quad in quads for _ in range(4)],
        out_specs=[spec(quad[0].shape) for quad in quads for _ in range(3)],
        out_shape=[jax.ShapeDtypeStruct(quad[0].shape, F32) for quad in quads for _ in range(3)],
        args=[a for quad in quads for a in quad], semantics=("parallel",), job=job)
    outs = res if job is None else res[0]
    triples = [tuple(outs[3 * q:3 * q + 3]) for q in range(n)]
    return triples if job is None else (triples, res[1])


WIRE_WIDTH = 1024
WIRE_ROWS = 1024


def _wire_plan(shapes):
    starts, row = [], 0
    for s in shapes:
        r, c = (1, s[0]) if len(s) == 1 else s
        starts.append(row)
        row += -(-(r * max(1, c // WIRE_WIDTH)) // 8) * 8
    assert row <= WIRE_ROWS, row
    return starts


def _wire_cells(shape):
    if len(shape) == 1:
        n = shape[0]
        if n <= WIRE_WIDTH:
            return [(0, n, (slice(None),))]
        return [(h, WIRE_WIDTH, (slice(h * WIRE_WIDTH, (h + 1) * WIRE_WIDTH),)) for h in range(n // WIRE_WIDTH)]
    r, c = shape
    if c <= WIRE_WIDTH:
        return [(None, c, (slice(None), slice(None)))]
    per = c // WIRE_WIDTH
    return [(j * per + h, WIRE_WIDTH, (j, slice(h * WIRE_WIDTH, (h + 1) * WIRE_WIDTH)))
            for j in range(r) for h in range(per)]


def pack_small(arrays, *, name):
    shapes = [a.shape for a in arrays]
    starts = _wire_plan(shapes)

    def body(*refs):
        out = refs[-1]
        out[...] = jnp.zeros_like(out)
        for ref, shape, row0 in zip(refs[:-1], shapes, starts, strict=True):
            for off, cols, idx in _wire_cells(shape):
                if off is None:
                    out[row0:row0 + shape[0], 0:cols] = ref[idx]
                else:
                    out[row0 + off, 0:cols] = ref[idx]

    return pl.pallas_call(body, name=name, out_shape=jax.ShapeDtypeStruct((WIRE_ROWS, WIRE_WIDTH), F32),
                          compiler_params=pltpu.CompilerParams(vmem_limit_bytes=VMEM_LIMIT))(*arrays)


def update_small(packed, starts, triples, shard_cells, chip, *, name):
    n = len(triples)

    def body(chip_ref, p_ref, *refs):
        k = chip_ref[0]
        ins, outs = refs[:3 * n], refs[3 * n:]

        def cut(row0, rows, cols):
            c0, c1 = cols
            if isinstance(rows, int):
                return p_ref[row0 + rows, c0:c1]
            return p_ref[row0 + rows[0]:row0 + rows[1], c0:c1]

        for q in range(n):
            w_ref, m_ref, v_ref = ins[3 * q:3 * q + 3]
            g_ref, d_ref, nm_ref, nv_ref = outs[4 * q:4 * q + 4]
            shape = w_ref.shape
            if shard_cells[q] is None:
                pieces = [(cut(starts[q], (0, shape[0]) if off is None else off, (0, cols)), idx)
                          for off, cols, idx in _wire_cells(shape)]
            else:
                by_chip = [shard_cells[q](kk) for kk in range(N_CHIPS)]
                pieces = []
                for i, (_, _, idx) in enumerate(by_chip[0]):
                    g = cut(starts[q], *by_chip[0][i][:2])
                    for kk in range(1, N_CHIPS):
                        g = jnp.where(k == kk, cut(starts[q], *by_chip[kk][i][:2]), g)
                    pieces.append((g, idx))
            for g, idx in pieces:
                g_ref[idx] = g
                d_ref[idx], nm_ref[idx], nv_ref[idx] = _adamw_math(w_ref[idx], g, m_ref[idx], v_ref[idx])

    whole = lambda a: pl.BlockSpec(a.shape, lambda i, c_ref, nd=a.ndim: (0,) * nd)
    flat = [a for t in triples for a in t]
    out_shape = [jax.ShapeDtypeStruct(t[0].shape, F32) for t in triples for _ in range(4)]
    res = pl.pallas_call(
        body, name=name,
        grid_spec=pltpu.PrefetchScalarGridSpec(
            num_scalar_prefetch=1, grid=(1,),
            in_specs=[whole(packed)] + [whole(a) for a in flat],
            out_specs=[pl.BlockSpec(s.shape, lambda i, c_ref, nd=len(s.shape): (0,) * nd) for s in out_shape]),
        out_shape=out_shape,
        compiler_params=_params("arbitrary"))(chip.reshape(1).astype(jnp.int32), packed, *flat)
    return [tuple(res[4 * q:4 * q + 4]) for q in range(n)]


ANY = pl.BlockSpec(memory_space=pl.ANY)


def _place():
    x, y, c = lax.axis_index("x"), lax.axis_index("y"), lax.axis_index("c")
    return x, y, c, [(1 - x, y), (x, 1 - y), (1 - x, 1 - y)]


DMA_PIECE_BYTES = 2048 * 1024


def _pieces(src, dst):
    shape = src.shape
    rows, cols = shape[-2], shape[-1]
    item = jnp.dtype(src.dtype).itemsize
    unit = 32 // item
    want = max(1, (rows * cols * item) // DMA_PIECE_BYTES)
    n = 1
    if rows % unit == 0:
        n = max(d for d in range(1, rows // unit + 1) if (rows // unit) % d == 0 and d <= want)
    size = rows // n
    out = []
    for lead in (range(shape[0]) if len(shape) == 3 else [None]):
        for i in range(n):
            sel = (pl.ds(i * size, size), slice(None)) if lead is None else (lead, pl.ds(i * size, size), slice(None))
            out.append((src.at[sel], dst.at[sel]))
    return out


def _send(src, dst, send_sem, recv_sem, peer, start=True):
    if start:
        for s, d in _pieces(src, dst):
            pltpu.make_async_remote_copy(src_ref=s, dst_ref=d, send_sem=send_sem, recv_sem=recv_sem,
                                         device_id=peer, device_id_type=MESH).start()
    return pltpu.make_async_remote_copy(src_ref=src, dst_ref=dst, send_sem=send_sem, recv_sem=recv_sem,
                                        device_id=peer, device_id_type=MESH)


def run_job(job, *, name):
    n_in, n_out = len(job.arrays), len(job.out_shapes)

    def body(*refs):
        job.fn(refs[:n_in], refs[n_in:n_in + n_out], refs[n_in + n_out:], "both")

    return pl.pallas_call(body, name=name, in_specs=[ANY] * n_in, out_specs=[ANY] * n_out,
                          out_shape=job.out_shapes, scratch_shapes=job.sems,
                          input_output_aliases={a: a for a in job.aliased()})(*job.arrays)


def gather_chips(shards, *, name):
    n = len(shards)

    def body(*refs):
        ins, outs = refs[:n], refs[n:2 * n]
        send_sems, recv_sems = refs[2 * n:]
        x, y, c, chips = _place()
        k = 2 * x + y
        copies = []
        for a in range(n):
            for j, peer in enumerate([(px, py, c) for px, py in chips] + [(x, y, 1 - c)]):
                copies.append(_send(ins[a], outs[a].at[k], send_sems.at[a, j], recv_sems.at[a, j], peer))
        for cp in copies:
            cp.wait()

    return pl.pallas_call(
        body, name=name, in_specs=[ANY] * n, out_specs=[ANY] * n,
        out_shape=[jax.ShapeDtypeStruct((N_CHIPS,) + s.shape, s.dtype) for s in shards],
        scratch_shapes=[pltpu.SemaphoreType.DMA((n, 4)), pltpu.SemaphoreType.DMA((n, 4))])(*shards)


def gather_by_halves_job(shards):
    n = len(shards)

    def fn(ins, outs, sems, phase):
        send_sems, recv_sems = sems
        x, y, c, chips = _place()
        k = 2 * x + y
        sibling = (x, y, 1 - c)
        begin = phase != "finish"
        waits, arrivals = [], []
        for a in range(n):
            half = ins[a].shape[0] // 2
            waits.append(_send(ins[a], outs[a].at[k], send_sems.at[a, 6], recv_sems.at[a, 6], sibling, begin))
            mine = pl.ds(c * half, half)
            for j, (px, py) in enumerate(chips):
                arrivals.append(_send(ins[a].at[mine, :], outs[a].at[k, mine, :], send_sems.at[a, j],
                                      recv_sems.at[a, j], (px, py, c), begin))
        if phase == "start":
            return
        i = 0
        for a in range(n):
            half = ins[a].shape[0] // 2
            mine = pl.ds(c * half, half)
            for j, (px, py) in enumerate(chips):
                arrivals[i].wait_recv()
                landed = outs[a].at[2 * px + py, mine, :]
                waits.append(_send(landed, landed, send_sems.at[a, 3 + j], recv_sems.at[a, 3 + j], sibling))
                i += 1
        for cp in arrivals:
            cp.wait_send()
        for cp in waits:
            cp.wait()

    return Job(shards, [jax.ShapeDtypeStruct((N_CHIPS,) + s.shape, s.dtype) for s in shards],
               [pltpu.SemaphoreType.DMA((n, 7)), pltpu.SemaphoreType.DMA((n, 7))], fn)


def gather_halves_job(shards):
    n = len(shards)

    def fn(ins, outs, sems, phase):
        send_sems, recv_sems = sems
        x, y, c, chips = _place()
        k = 2 * x + y
        copies = []
        for a in range(n):
            half = ins[a].shape[0] // 2
            mine = pl.ds(c * half, half)
            copies.append(_send(ins[a], outs[a].at[k], send_sems.at[a, 3], recv_sems.at[a, 3], (x, y, 1 - c),
                                phase != "finish"))
            for j, (px, py) in enumerate(chips):
                copies.append(_send(ins[a].at[mine, :], outs[a].at[k, mine, :], send_sems.at[a, j],
                                    recv_sems.at[a, j], (px, py, c), phase != "finish"))
        if phase != "start":
            for cp in copies:
                cp.wait()

    return Job(shards, [jax.ShapeDtypeStruct((N_CHIPS,) + s.shape, s.dtype) for s in shards],
               [pltpu.SemaphoreType.DMA((n, 4)), pltpu.SemaphoreType.DMA((n, 4))], fn)


def forward_halves_job(gathered):
    n = len(gathered)

    def fn(ins, outs, sems, phase):
        send_sems, recv_sems = sems
        x, y, c, chips = _place()
        copies = []
        for a in range(n):
            half = outs[a].shape[1] // 2
            for j, (px, py) in enumerate(chips):
                landed = outs[a].at[2 * px + py, pl.ds(c * half, half), :]
                copies.append(_send(landed, landed, send_sems.at[a, j], recv_sems.at[a, j], (x, y, 1 - c),
                                    phase != "finish"))
        if phase != "start":
            for cp in copies:
                cp.wait()

    return Job(gathered, [jax.ShapeDtypeStruct(g.shape, g.dtype) for g in gathered],
               [pltpu.SemaphoreType.DMA((n, 3)), pltpu.SemaphoreType.DMA((n, 3))], fn, in_place=True)


def swap_job(grads):
    n = len(grads)

    def fn(ins, outs, sems, phase):
        send_sems, recv_sems = sems
        x, y, c, _ = _place()
        copies = []
        for a in range(n):
            half = ins[a].shape[1] // 2
            copies.append(_send(ins[a].at[:, pl.ds((1 - c) * half, half), :], outs[a], send_sems.at[a],
                                recv_sems.at[a], (x, y, 1 - c), phase != "finish"))
        if phase != "start":
            for cp in copies:
                cp.wait()

    return Job(grads, [jax.ShapeDtypeStruct((g.shape[0], g.shape[1] // 2, g.shape[2]), g.dtype) for g in grads],
               [pltpu.SemaphoreType.DMA((n,)), pltpu.SemaphoreType.DMA((n,))], fn)


def _chip_slot(s, k):
    rel = s ^ k
    return (rel >> 1) | ((rel & 1) << 1)


def sum_cores(g, theirs, ck, out_dtype, *, name):
    n, r, cols = g.shape
    half = r // 2
    tr = _tile(half, max(16, (1 << 19) // cols), 16)
    nb = half // tr

    def body(ck_ref, g_ref, t_ref, o_ref):
        o_ref[...] = (g_ref[...] + t_ref[...]).astype(o_ref.dtype)

    return pl.pallas_call(
        body, name=name,
        grid_spec=pltpu.PrefetchScalarGridSpec(
            num_scalar_prefetch=1, grid=(n, nb),
            in_specs=[pl.BlockSpec((None, tr, cols), lambda s, i, ck_ref: (s, ck_ref[0] * nb + i, 0)),
                      pl.BlockSpec((None, tr, cols), lambda s, i, ck_ref: (s, i, 0))],
            out_specs=pl.BlockSpec((None, tr, cols), lambda s, i, ck_ref: (_chip_slot(s, ck_ref[1]), i, 0))),
        out_shape=jax.ShapeDtypeStruct((n, half, cols), out_dtype),
        compiler_params=_params("parallel", "parallel"))(ck, g, theirs)


def scatter_job(parts):
    n = len(parts)

    def fn(ins, outs, sems, phase):
        send_sems, recv_sems = sems
        x, y, c, chips = _place()
        copies = []
        for a in range(n):
            for j, (px, py) in enumerate(chips):
                copies.append(_send(ins[a].at[1 + j], outs[a].at[j], send_sems.at[a, j], recv_sems.at[a, j],
                                    (px, py, c), phase != "finish"))
        if phase != "start":
            for cp in copies:
                cp.wait()

    return Job(parts, [jax.ShapeDtypeStruct((3,) + p.shape[1:], p.dtype) for p in parts],
               [pltpu.SemaphoreType.DMA((n, 3)), pltpu.SemaphoreType.DMA((n, 3))], fn)


def sum_chips(own, others, ck, *, name):
    _, r, cols = own.shape
    tr = _tile(r, max(16, (1 << 19) // cols), 16)

    def body(ck_ref, a_ref, b0_ref, b1_ref, b2_ref, o_ref):
        o_ref[...] = (a_ref[...].astype(F32) + b0_ref[...].astype(F32) + b1_ref[...].astype(F32)
                      + b2_ref[...].astype(F32))

    other = lambda j: pl.BlockSpec((None, tr, cols), lambda i, ck_ref: (j, i, 0))
    return pl.pallas_call(
        body, name=name,
        grid_spec=pltpu.PrefetchScalarGridSpec(
            num_scalar_prefetch=1, grid=(r // tr,),
            in_specs=[pl.BlockSpec((None, tr, cols), lambda i, ck_ref: (0, i, 0)), other(0), other(1), other(2)],
            out_specs=pl.BlockSpec((None, tr, cols), lambda i, ck_ref: (ck_ref[0], i, 0))),
        out_shape=jax.ShapeDtypeStruct((2, r, cols), F32),
        compiler_params=_params("parallel"))(ck, own, others, others, others)


def share_job(joined):
    n = len(joined)

    def fn(ins, outs, sems, phase):
        send_sems, recv_sems = sems
        x, y, c, _ = _place()
        copies = [_send(outs[a].at[c], outs[a].at[c], send_sems.at[a], recv_sems.at[a], (x, y, 1 - c),
                        phase != "finish") for a in range(n)]
        if phase != "start":
            for cp in copies:
                cp.wait()

    return Job(joined, [jax.ShapeDtypeStruct(j.shape, j.dtype) for j in joined],
               [pltpu.SemaphoreType.DMA((n,)), pltpu.SemaphoreType.DMA((n,))], fn, in_place=True)


def kernel(x, meta_tokens, norm0_g, l0_w_in, l0_lam_re, l0_lam_im, l0_log_dt, l0_b_re, l0_b_im, l0_c_re, l0_c_im, l0_d_skip, l0_w_glu, l0_b_glu, l0_w_out, norm1_g, l1_w_in, l1_conv_w, l1_conv_b, l1_w_out, norm2_g, l2_w_in, l2_w_grp, l2_b_grp, l2_scale, l2_w_out, norm3_g, l3_w_in, l3_lam_re, l3_lam_im, l3_log_dt, l3_b_re, l3_b_im, l3_c_re, l3_c_im, l3_d_skip, l3_w_glu, l3_b_glu, l3_w_out, final_g, loss_target, m_meta_tokens, m_norm0_g, m_l0_w_in, m_l0_lam_re, m_l0_lam_im, m_l0_log_dt, m_l0_b_re, m_l0_b_im, m_l0_c_re, m_l0_c_im, m_l0_d_skip, m_l0_w_glu, m_l0_b_glu, m_l0_w_out, m_norm1_g, m_l1_w_in, m_l1_conv_w, m_l1_conv_b, m_l1_w_out, m_norm2_g, m_l2_w_in, m_l2_w_grp, m_l2_b_grp, m_l2_scale, m_l2_w_out, m_norm3_g, m_l3_w_in, m_l3_lam_re, m_l3_lam_im, m_l3_log_dt, m_l3_b_re, m_l3_b_im, m_l3_c_re, m_l3_c_im, m_l3_d_skip, m_l3_w_glu, m_l3_b_glu, m_l3_w_out, m_final_g, v_meta_tokens, v_norm0_g, v_l0_w_in, v_l0_lam_re, v_l0_lam_im, v_l0_log_dt, v_l0_b_re, v_l0_b_im, v_l0_c_re, v_l0_c_im, v_l0_d_skip, v_l0_w_glu, v_l0_b_glu, v_l0_w_out, v_norm1_g, v_l1_w_in, v_l1_conv_w, v_l1_conv_b, v_l1_w_out, v_norm2_g, v_l2_w_in, v_l2_w_grp, v_l2_b_grp, v_l2_scale, v_l2_w_out, v_norm3_g, v_l3_w_in, v_l3_lam_re, v_l3_lam_im, v_l3_log_dt, v_l3_b_re, v_l3_b_im, v_l3_c_re, v_l3_c_im, v_l3_d_skip, v_l3_w_glu, v_l3_b_glu, v_l3_w_out, v_final_g):
    given = dict(locals())
    w = {n: given[n] for n in WEIGHTS}
    mom_m = {n: given["m_" + n] for n in WEIGHTS}
    mom_v = {n: given["v_" + n] for n in WEIGHTS}
    seq = x.shape[1]
    real = N_META + seq
    lp = -(-real // SCAN_ROWS) * SCAN_ROWS
    chip = 2 * lax.axis_index("x") + lax.axis_index("y")
    row = lambda a: a.reshape(1, -1)

    shard_bf16 = {n: (w[n].reshape(-1, w[n].shape[-1]) if n == 'l2_w_grp' else w[n]).astype(BF16) for n in BIG}
    layer = lambda i: [n for n in BIG if n.startswith("l%d_" % i)]
    over_ici = lambda names: gather_halves_job([shard_bf16[n] for n in names])
    full = {}

    def landed(names, arrays):
        for n, arr in zip(names, arrays, strict=True):
            if n.endswith('w_in'):
                full[n] = arr
            elif n == 'l2_w_grp':
                full[n] = arr.reshape(N_CHIPS, len(POOL_WINDOWS), POOL_GROUP // N_CHIPS, POOL_GROUP)
            else:
                full[n] = arr.reshape(1, -1, arr.shape[-1])

    landed(['l0_w_in'], run_job(gather_by_halves_job([shard_bf16['l0_w_in']]), name="gather_l0_w_in"))
    full.update(zip(SMALL_SHARDED, gather_chips([w[n] for n in SMALL_SHARDED], name="gather_small_shards"),
                    strict=True))
    meta_full = full['meta_tokens'].transpose(1, 0, 2).reshape(N_META, D_MODEL)
    conv_w_full = full['l1_conv_w'].transpose(1, 0, 2).reshape(3, CONV_E)
    b_grp_full = full['l2_b_grp'].transpose(1, 0, 2).reshape(len(POOL_WINDOWS), POOL_GROUP)

    h0 = jnp.concatenate([meta_full, x[0], jnp.zeros((lp - real, D_MODEL), F32)], axis=0)
    target = jnp.concatenate([jnp.zeros((N_META, D_MODEL), F32), loss_target[0],
                              jnp.zeros((lp - real, D_MODEL), F32)], axis=0)
    grads = {}
    saved = {}
    gip = lambda a: jnp.swapaxes(a, 1, 2)

    def s5_layer_fwd(i, h, first=False):
        pre = "l%d_" % i
        prm = [w[pre + 'lam_re'], w[pre + 'lam_im'], w[pre + 'log_dt'].reshape(-1, 1), gip(w[pre + 'b_re']),
               gip(w[pre + 'b_im']), w[pre + 'c_re'], w[pre + 'c_im']]
        a_re, a_im, bblk, cblk = s5_prep(prm, name=pre + "prep")
        tables = (a_re.reshape(1, -1), a_im.reshape(1, -1), bblk, cblk)
        ar, ai = tables[:2]
        gain = row(w["norm%d_g" % i])
        rest = [pre + 'w_glu', pre + 'w_out']
        if not first:
            uz, n = norm_mm_nn(h, gain, full[pre + 'w_in'], name=pre + "in")
            y0, yg, sr, si = s5_fwd(uz, bblk, cblk, ar, ai, row(w[pre + 'd_skip']), name=pre + "scan")
            q, y3 = glu_gate(yg, full[pre + 'w_glu'], row(w[pre + 'b_glu']), y0, uz, name=pre + "glu")
        else:
            (uz, n), halves = norm_mm_nn(h, gain, full[pre + 'w_in'], name=pre + "in", job=over_ici(rest))
            (y0, yg, sr, si), arrived = s5_fwd(
                uz, bblk, cblk, ar, ai, row(w[pre + 'd_skip']), name=pre + "scan",
                job=Job.both(over_ici(layer(1)), forward_halves_job(halves)))
            landed(rest, arrived[len(layer(1)):])
            (q, y3), whole = glu_gate(yg, full[pre + 'w_glu'], row(w[pre + 'b_glu']), y0, uz, name=pre + "glu",
                                      job=forward_halves_job(arrived[:len(layer(1))]))
            landed(layer(1), whole)
        h_new = mm_nn(y3, full[pre + 'w_out'], add=h, name=pre + "out")
        saved[i] = dict(h=h, n=n, uz=uz, y0=y0, sr=sr, si=si, yg=yg, q=q, y3=y3, tables=tables, prm=prm)
        return h_new

    def s5_layer_bwd(i, dh, carry=None):
        pre = "l%d_" % i
        s = saved[i]
        ar, ai, bblk, cblk = s['tables']
        grads[pre + 'w_out'] = mm_tn(s['y3'], dh, 1, name=pre + "d_w_out")[0]
        dy0_direct, dq, dp, db_glu = s5_gate_bwd(dh, full[pre + 'w_out'], s['y0'], s['q'], s['uz'],
                                                 name=pre + "d_y3")
        grads[pre + 'b_glu'] = db_glu
        grads[pre + 'w_glu'] = mm_tn(s['yg'], dq, 1, name=pre + "d_w_glu")[0]
        early = [] if carry is None else [pre + 'w_out', pre + 'w_glu']
        if carry is None:
            dyg = mm_nt(dq, full[pre + 'w_glu'], name=pre + "d_yg")
            res = s5_bwd(dy0_direct, dyg, s['y0'], s['uz'], s['sr'], s['si'], bblk, cblk, ar, ai,
                         row(w[pre + 'd_skip']), dp, name=pre + "d_scan")
        else:
            pieces = [chip_major(n) for n in early]
            dyg, theirs = mm_nt(dq, full[pre + 'w_glu'], name=pre + "d_yg", job=swap_job(pieces))
            sum_over_cores(early, pieces, theirs)
            res, arrived = s5_bwd(dy0_direct, dyg, s['y0'], s['uz'], s['sr'], s['si'], bblk, cblk, ar, ai,
                                  row(w[pre + 'd_skip']), dp, name=pre + "d_scan",
                                  job=scatter_job(scatter_of(carry) + [pair[n] for n in early]))
            from_chips.update(zip(layer(carry) + early, arrived, strict=True))
        dp, dbblk, dcblk, dar, dai, dd = res
        grads[pre + 'd_skip'] = dd
        cots = (dar.reshape(S5_GROUPS, S5_STATE), dai.reshape(S5_GROUPS, S5_STATE), dbblk, dcblk)
        for key, val in zip(('lam_re', 'lam_im', 'log_dt', 'b_re', 'b_im', 'c_re', 'c_im'),
                            s5_prep_bwd(s['prm'], cots, name=pre + "d_prep"), strict=True):
            grads[pre + key] = val.reshape(-1) if key == 'log_dt' else val
        if carry is None:
            grads[pre + 'w_in'] = mm_tn(s['n'], dp, N_CHIPS, name=pre + "d_w_in")
        else:
            later = [n for n in BIG if not n.startswith(pre)] + early
            sums = [sum_chips(pair[n], from_chips[n], ck, name="sum_chips_" + n) for n in later]
            grads[pre + 'w_in'], shared = mm_tn(s['n'], dp, N_CHIPS, name=pre + "d_w_in", job=share_job(sums))
            joined.update(zip(later, shared, strict=True))
        names = [n for n in layer(i) if n not in early]
        pieces = [chip_major(n) for n in names]
        (dh_in, dg), theirs = mm_nt_norm_bwd(dp, full[pre + 'w_in'], s['h'], row(w["norm%d_g" % i]), dh,
                                             name=pre + "d_n", job=swap_job(pieces))
        sum_over_cores(names, pieces, theirs)
        grads["norm%d_g" % i] = dg
        return dh_in

    h1 = s5_layer_fwd(0, h0, first=True)

    (p1, n1), halves = norm_mm_nn(h1, row(w['norm1_g']), full['l1_w_in'], name="l1_in", job=over_ici(layer(2)))
    y3_1 = conv_fwd(p1, conv_w_full, row(w['l1_conv_b']), name="l1_conv")
    h2, whole = mm_nn(y3_1, full['l1_w_out'], add=h1, name="l1_out", job=forward_halves_job(halves))
    landed(layer(2), whole)

    (p2, n2), halves = norm_mm_nn(h2, row(w['norm2_g']), full['l2_w_in'], name="l2_in", job=over_ici(layer(3)))
    mixed = pool_fwd(p2, name="l2_pool")
    o2, y3_2 = grp_nn_gate(mixed, full['l2_w_grp'], b_grp_full.reshape(1, -1), p2, row(w['l2_scale']), name="l2_grp")
    h3, whole = mm_nn(y3_2, full['l2_w_out'], add=h2, name="l2_out", job=forward_halves_job(halves))
    landed(layer(3), whole)

    h4 = s5_layer_fwd(3, h3)

    dh, grads['final_g'], sq = final_loss(h4, target, row(w['final_g']), N_META, real, name="loss")
    loss = lax.psum(0.5 / D_MODEL * jnp.sum(sq), ("x", "y", "c"))

    ck = jnp.stack([lax.axis_index("c"), chip]).astype(jnp.int32)
    pair, from_chips, joined = {}, {}, {}

    def chip_major(n):
        g = grads[n]
        if n.endswith('w_in'):
            return g
        if n == 'l2_w_grp':
            return g.reshape(N_CHIPS, -1, POOL_GROUP)
        return g.reshape(N_CHIPS, -1, g.shape[-1])

    def core_parts(i):
        return layer(i), [chip_major(n) for n in layer(i)]

    def sum_over_cores(names, pieces, theirs):
        for n, g, t in zip(names, pieces, theirs, strict=True):
            pair[n] = sum_cores(g, t, ck, F32 if n == 'small' else BF16, name="sum_cores_" + n)

    scatter_of = lambda i: [pair[n] for n in layer(i)]

    def scattered(i, arrays):
        from_chips.update(zip(layer(i), arrays, strict=True))

    dh = s5_layer_bwd(3, dh)

    grads['l2_w_out'] = mm_tn(y3_2, dh, 1, name="l2_d_w_out")[0]
    do2, dp2, dscale, dbgrp = pool_gate_bwd(dh, full['l2_w_out'], o2, p2, row(w['l2_scale']), name="l2_d_y3")
    grads['l2_scale'] = dscale
    grads['l2_b_grp'] = dbgrp
    grads['l2_w_grp'] = grp_tn(mixed, do2, name="l2_d_w_grp")
    dmix = grp_nt(do2, full['l2_w_grp'], name="l2_d_mixed")
    dp2 = pool_bwd(dmix, dp2, name="l2_d_pool")
    grads['l2_w_in'], arrived = mm_tn(n2, dp2, N_CHIPS, name="l2_d_w_in", job=scatter_job(scatter_of(3)))
    scattered(3, arrived)
    names, pieces = core_parts(2)
    (dh, dg), theirs = mm_nt_norm_bwd(dp2, full['l2_w_in'], h2, row(w['norm2_g']), dh, name="l2_d_n",
                                      job=swap_job(pieces))
    sum_over_cores(names, pieces, theirs)
    grads['norm2_g'] = dg

    grads['l1_w_out'] = mm_tn(y3_1, dh, 1, name="l1_d_w_out")[0]
    dp1, dcw, dcb = conv_bwd(dh, full['l1_w_out'], p1, conv_w_full, row(w['l1_conv_b']), name="l1_d_conv")
    grads['l1_conv_w'] = dcw
    grads['l1_conv_b'] = dcb
    grads['l1_w_in'], arrived = mm_tn(n1, dp1, N_CHIPS, name="l1_d_w_in", job=scatter_job(scatter_of(2)))
    scattered(2, arrived)
    names, pieces = core_parts(1)
    (dh, dg), theirs = mm_nt_norm_bwd(dp1, full['l1_w_in'], h1, row(w['norm1_g']), dh, name="l1_d_n",
                                      job=swap_job(pieces))
    sum_over_cores(names, pieces, theirs)
    grads['norm1_g'] = dg

    dh = s5_layer_bwd(0, dh, carry=1)
    grad_x = dh[N_META:real][None]
    grads['meta_tokens'] = dh[:N_META]

    wide = [n for n in SMALL if w[n].ndim == 3]
    wire = [grads[n].reshape(S5_GROUPS, -1) if n in wide else grads[n] for n in SMALL]
    starts = dict(zip(SMALL, _wire_plan([a.shape for a in wire]), strict=True))
    small_local = pack_small(wire, name="pack_small")
    pieces = [small_local.reshape(N_CHIPS, -1, WIRE_WIDTH)]
    sum_over_cores(['small'], pieces, run_job(swap_job(pieces), name="reduce_cores_small"))

    out_g, out_d, out_m, out_v = {}, {}, {}, {}
    as2d = lambda a: a.reshape(-1, a.shape[-1])

    def update_big(names, name, job=None):
        for n in names:
            out_g[n] = joined[n].reshape(w[n].shape)
        res = adamw_many([(as2d(w[n]), as2d(out_g[n]), as2d(mom_m[n]), as2d(mom_v[n])) for n in names],
                         name=name, job=job)
        triples, carried = res if job is not None else (res, None)
        for n, (d, nm, nv) in zip(names, triples, strict=True):
            out_d[n], out_m[n], out_v[n] = d.reshape(w[n].shape), nm.reshape(w[n].shape), nv.reshape(w[n].shape)
        return carried

    rest = ['l0_w_in', 'small']
    later = [n for n in BIG if n not in rest]
    from_chips.update(zip(rest, update_big(later, "adamw_big", scatter_job([pair[n] for n in rest])), strict=True))
    sums = [sum_chips(pair[n], from_chips[n], ck, name="sum_chips_" + n) for n in rest]
    joined.update(zip(rest, run_job(share_job(sums), name="share_cores"), strict=True))
    update_big(['l0_w_in'], "adamw_l0_w_in")
    (small_all,) = run_job(gather_by_halves_job([joined['small'].reshape(-1, WIRE_WIDTH)]), name="gather_small")
    small_all = small_all.reshape(WIRE_ROWS, WIRE_WIDTH)

    half_w = WIRE_WIDTH // 2
    shard_cells = {
        'meta_tokens': lambda k: [((0, N_META), (k * 256, (k + 1) * 256), (slice(None), slice(None)))],
        'l1_conv_w': lambda k: [(2 * j + k // 2, ((k % 2) * half_w, (k % 2 + 1) * half_w), (j, slice(None)))
                                for j in range(3)],
        'l2_b_grp': lambda k: [(j // 2, ((j % 2) * half_w + k * 128, (j % 2) * half_w + (k + 1) * 128),
                                (j, slice(None))) for j in range(len(POOL_WINDOWS))],
    }
    plain = [n for n in SMALL if n not in wide]
    res = update_small(small_all, [starts[n] for n in plain], [(w[n], mom_m[n], mom_v[n]) for n in plain],
                       [shard_cells.get(n) for n in plain], chip, name="adamw_small")
    for n, (g, d, nm, nv) in zip(plain, res, strict=True):
        out_g[n], out_d[n], out_m[n], out_v[n] = g, d, nm, nv
    as_gip = lambda n, a: gip(a) if n.endswith(('b_re', 'b_im')) else a
    g_gip = {n: small_all[starts[n]:starts[n] + S5_GROUPS].reshape(S5_GROUPS, S5_GROUP, S5_STATE) for n in wide}
    res = adamw_many([(as_gip(n, w[n]), g_gip[n], as_gip(n, mom_m[n]), as_gip(n, mom_v[n])) for n in wide],
                     name="adamw_s5_tensors")
    for n, (d, nm, nv) in zip(wide, res, strict=True):
        out_g[n], out_d[n], out_m[n], out_v[n] = as_gip(n, g_gip[n]), as_gip(n, d), as_gip(n, nm), as_gip(n, nv)

    return (loss, grad_x, *[out_g[n] for n in WEIGHTS], *[out_d[n] for n in WEIGHTS],
            *[out_m[n] for n in WEIGHTS], *[out_v[n] for n in WEIGHTS])
```

```python
import functools

import jax
import jax.numpy as jnp
from jax import lax
from jax.experimental import pallas as pl
from jax.experimental.pallas import tpu as pltpu

F32, BF16 = jnp.float32, jnp.bfloat16
MESH = pl.DeviceIdType.MESH

D_MODEL = 1024
N_META = 16
EPS = 1e-6
S5_GROUPS, S5_GROUP, S5_STATE = 64, 16, 64
LANE_BLOCK = 128
S5_BLOCKS = D_MODEL // LANE_BLOCK
GROUPS_PER_BLOCK = LANE_BLOCK // S5_GROUP
STATE_BLOCK = GROUPS_PER_BLOCK * S5_STATE
SCAN_ROWS = 256
SCAN_CHUNK = 1088
SUBLANES = 8
CONV_E = 2048
POOL_E = 2048
POOL_WINDOWS = (2, 4, 8, 16)
POOL_GROUP = 512
POOL_HALO = 16
N_CHIPS = 4

ADAM_LR, ADAM_B1, ADAM_B2, ADAM_EPS, ADAM_WD, ADAM_STEP = 0.001, 0.9, 0.999, 1e-08, 0.01, 10

VMEM_LIMIT = 48 * 1024 * 1024
TN_OPERAND_BYTES = 28 * 1024 * 1024

WEIGHTS = ['meta_tokens', 'norm0_g', 'l0_w_in', 'l0_lam_re', 'l0_lam_im', 'l0_log_dt', 'l0_b_re', 'l0_b_im',
           'l0_c_re', 'l0_c_im', 'l0_d_skip', 'l0_w_glu', 'l0_b_glu', 'l0_w_out', 'norm1_g', 'l1_w_in',
           'l1_conv_w', 'l1_conv_b', 'l1_w_out', 'norm2_g', 'l2_w_in', 'l2_w_grp', 'l2_b_grp', 'l2_scale',
           'l2_w_out', 'norm3_g', 'l3_w_in', 'l3_lam_re', 'l3_lam_im', 'l3_log_dt', 'l3_b_re', 'l3_b_im',
           'l3_c_re', 'l3_c_im', 'l3_d_skip', 'l3_w_glu', 'l3_b_glu', 'l3_w_out', 'final_g']
BIG = ['l0_w_in', 'l0_w_glu', 'l0_w_out', 'l1_w_in', 'l1_w_out', 'l2_w_in', 'l2_w_grp', 'l2_w_out',
       'l3_w_in', 'l3_w_glu', 'l3_w_out']
SMALL_SHARDED = ['meta_tokens', 'l1_conv_w', 'l2_b_grp']
SMALL = [n for n in WEIGHTS if n not in BIG]


def _params(*sem):
    return pltpu.CompilerParams(dimension_semantics=sem, vmem_limit_bytes=VMEM_LIMIT)


class Job:
    def __init__(self, arrays, out_shapes, sems, fn, in_place=False):
        self.arrays, self.out_shapes, self.sems, self.fn = list(arrays), list(out_shapes), list(sems), fn
        self.in_place = in_place
        assert len(self.arrays) == len(self.out_shapes)

    @staticmethod
    def both(first, second):
        na, ns = len(first.arrays), len(first.sems)

        def fn(ins, outs, sems, phase):
            first.fn(ins[:na], outs[:na], sems[:ns], phase)
            second.fn(ins[na:], outs[na:], sems[ns:], phase)

        job = Job(first.arrays + second.arrays, first.out_shapes + second.out_shapes, first.sems + second.sems, fn)
        job.in_place = [first.in_place] * na + [second.in_place] * len(second.arrays)
        return job

    def aliased(self):
        flags = self.in_place if isinstance(self.in_place, list) else [self.in_place] * len(self.arrays)
        return [a for a, flag in enumerate(flags) if flag]


def _call(body, *, name, grid, in_specs, out_specs, out_shape, args, semantics, scratch=(), aliases=None, job=None):
    if job is None:
        return pl.pallas_call(
            body, name=name, grid=grid, in_specs=list(in_specs), out_specs=list(out_specs),
            out_shape=list(out_shape), scratch_shapes=list(scratch), input_output_aliases=aliases or {},
            compiler_params=_params(*semantics))(*args)
    counts = [len(in_specs), len(job.arrays), len(out_specs), len(job.out_shapes), len(scratch), len(job.sems)]

    def hosted(*refs):
        parts, pos = [], 0
        for k in counts:
            parts.append(refs[pos:pos + k])
            pos += k
        ins, job_ins, outs, job_outs, scr, job_sems = parts
        steps = [pl.program_id(d) for d in range(len(grid))]
        first = functools.reduce(jnp.logical_and, [s == 0 for s in steps])
        last = functools.reduce(jnp.logical_and, [s == g - 1 for s, g in zip(steps, grid, strict=True)])

        @pl.when(first)
        def _():
            job.fn(job_ins, job_outs, job_sems, "start")

        body(*ins, *outs, *scr)

        @pl.when(last)
        def _():
            job.fn(job_ins, job_outs, job_sems, "finish")

    any_spec = pl.BlockSpec(memory_space=pl.ANY)
    aliases = dict(aliases or {})
    aliases.update({len(in_specs) + a: len(out_specs) + a for a in job.aliased()})
    res = pl.pallas_call(
        hosted, name=name, grid=grid, in_specs=list(in_specs) + [any_spec] * len(job.arrays),
        out_specs=list(out_specs) + [any_spec] * len(job.out_shapes),
        out_shape=list(out_shape) + job.out_shapes, scratch_shapes=list(scratch) + job.sems,
        input_output_aliases=aliases,
        compiler_params=_params(*["arbitrary"] * len(grid)))(*args, *job.arrays)
    return res[:len(out_specs)], res[len(out_specs):]


def _tile(n, cap, mult):
    best = None
    for t in range(mult, min(n, cap) + 1, mult):
        if n % t == 0:
            best = t
    assert best is not None, (n, cap, mult)
    return best


def _with_job(res, job):
    return res[0] if job is None else (res[0][0], res[1])


def mm_nn(a, b, *, name, bias=None, add=None, out_dtype=F32, job=None):
    m, k = a.shape
    s, _, ns = b.shape
    n = s * ns
    tm, tn = _tile(m, 1088, 16), _tile(ns, 1024, 128)
    per = ns // tn

    def body(*refs):
        a_ref, b_ref = refs[0], refs[1]
        o_ref = refs[-1]
        acc = jnp.dot(a_ref[...].astype(BF16), b_ref[...], preferred_element_type=F32)
        pos = 2
        if bias is not None:
            acc = acc + refs[pos][...]
            pos += 1
        if add is not None:
            acc = acc + refs[pos][...]
        o_ref[...] = acc.astype(o_ref.dtype)

    in_specs = [pl.BlockSpec((tm, k), lambda i, j: (i, 0)),
                pl.BlockSpec((None, k, tn), lambda i, j: (j // per, 0, j % per))]
    args = [a, b]
    if bias is not None:
        in_specs.append(pl.BlockSpec((1, tn), lambda i, j: (0, j)))
        args.append(bias)
    if add is not None:
        in_specs.append(pl.BlockSpec((tm, tn), lambda i, j: (i, j)))
        args.append(add)
    return _with_job(_call(
        body, name=name, grid=(m // tm, n // tn), in_specs=in_specs,
        out_specs=[pl.BlockSpec((tm, tn), lambda i, j: (i, j))],
        out_shape=[jax.ShapeDtypeStruct((m, n), out_dtype)], args=args,
        semantics=("parallel", "parallel"), job=job), job)


def norm_mm_nn(h, gain, b, *, name, job=None):
    m, k = h.shape
    s, _, ns = b.shape
    n = s * ns
    tm, tn = _tile(m, 1088, 16), _tile(ns, 1024, 128)
    per = ns // tn

    def body(h_ref, g_ref, b_ref, o_ref, n_ref):
        @pl.when(pl.program_id(1) == 0)
        def _():
            n_ref[...] = _rms(h_ref[...], g_ref[...]).astype(n_ref.dtype)

        o_ref[...] = jnp.dot(n_ref[...], b_ref[...], preferred_element_type=F32)

    res = _call(
        body, name=name, grid=(m // tm, n // tn),
        in_specs=[pl.BlockSpec((tm, k), lambda i, j: (i, 0)), pl.BlockSpec((1, k), lambda i, j: (0, 0)),
                  pl.BlockSpec((None, k, tn), lambda i, j: (j // per, 0, j % per))],
        out_specs=[pl.BlockSpec((tm, tn), lambda i, j: (i, j)), pl.BlockSpec((tm, k), lambda i, j: (i, 0))],
        out_shape=[jax.ShapeDtypeStruct((m, n), F32), jax.ShapeDtypeStruct((m, k), BF16)], args=(h, gain, b),
        semantics=("parallel", "arbitrary"), job=job)
    return (res[0], res[1]) if job is None else ((res[0][0], res[0][1]), res[1])


def glu_gate(yg, w_glu, b_glu, y0, uz, *, name, job=None):
    m, k = yg.shape
    n = w_glu.shape[2]
    tm = _tile(m, 1088, 16)

    def body(a_ref, w_ref, b_ref, y0_ref, z_ref, q_ref, o_ref):
        q = jnp.dot(a_ref[...], w_ref[...], preferred_element_type=F32) + b_ref[...]
        q_ref[...] = q
        o_ref[...] = _s5_gate(y0_ref[...], q, z_ref[...]).astype(o_ref.dtype)

    tile = pl.BlockSpec((tm, n), lambda i: (i, 0))
    res = _call(
        body, name=name, grid=(m // tm,),
        in_specs=[pl.BlockSpec((tm, k), lambda i: (i, 0)), pl.BlockSpec((None, k, n), lambda i: (0, 0, 0)),
                  pl.BlockSpec((1, n), lambda i: (0, 0)), tile, pl.BlockSpec((tm, n), lambda i: (i, 1))],
        out_specs=[tile, tile],
        out_shape=[jax.ShapeDtypeStruct((m, n), F32), jax.ShapeDtypeStruct((m, n), BF16)],
        args=(yg, w_glu, b_glu, y0, uz), semantics=("parallel",), job=job)
    return (res[0], res[1]) if job is None else ((res[0][0], res[0][1]), res[1])


def s5_gate_bwd(dh, w_out, y0, q, uz, *, name):
    m, d = dh.shape
    e = w_out.shape[1]
    tm = _tile(m, 544, 16)

    def body(dh_ref, w_ref, y0_ref, q_ref, z_ref, dy0_ref, dq_ref, dz_ref, db_ref):
        i = pl.program_id(0)
        dy3 = lax.dot_general(dh_ref[...].astype(BF16), w_ref[...], (((1,), (1,)), ((), ())),
                              preferred_element_type=F32)
        dy0, dq, dz = jax.vjp(_s5_gate, y0_ref[...], q_ref[...], z_ref[...])[1](dy3)
        dy0_ref[...] = dy0
        dq_ref[...] = dq.astype(dq_ref.dtype)
        dz_ref[...] = dz.astype(dz_ref.dtype)
        db = _colsum(dq)

        @pl.when(i == 0)
        def _():
            db_ref[...] = db

        @pl.when(i > 0)
        def _():
            db_ref[...] += db

    tile = pl.BlockSpec((tm, e), lambda i: (i, 0))
    right = pl.BlockSpec((tm, e), lambda i: (i, 1))
    return pl.pallas_call(
        body, name=name, grid=(m // tm,),
        in_specs=[pl.BlockSpec((tm, d), lambda i: (i, 0)), pl.BlockSpec((None, e, d), lambda i: (0, 0, 0)),
                  tile, tile, right],
        out_specs=[tile, tile, right, pl.BlockSpec((1, e), lambda i: (0, 0))],
        out_shape=[jax.ShapeDtypeStruct((m, e), F32), jax.ShapeDtypeStruct((m, e), BF16),
                   jax.ShapeDtypeStruct((m, 2 * e), BF16), jax.ShapeDtypeStruct((1, e), F32)],
        compiler_params=_params("arbitrary"))(dh, w_out, y0, q, uz)


def pool_gate_bwd(dh, w_out, o, p, scale, *, name):
    m, d = dh.shape
    e = w_out.shape[1]
    tm, te = _tile(m, 1088, 16), 1024
    nj = e // te

    def body(dh_ref, w_ref, o_ref, z_ref, s_ref, do_ref, dz_ref, ds_ref, db_ref):
        i = pl.program_id(1)
        dy3 = lax.dot_general(dh_ref[...].astype(BF16), w_ref[...], (((1,), (1,)), ((), ())),
                              preferred_element_type=F32)
        do, dz, ds = jax.vjp(_pool_gate, o_ref[...], z_ref[...], s_ref[...])[1](dy3)
        do_ref[...] = do.astype(do_ref.dtype)
        dz_ref[...] = dz.astype(dz_ref.dtype)
        db = _colsum(do)

        @pl.when(i == 0)
        def _():
            ds_ref[...] = ds
            db_ref[...] = db

        @pl.when(i > 0)
        def _():
            ds_ref[...] += ds
            db_ref[...] += db

    tile = pl.BlockSpec((tm, te), lambda j, i: (i, j))
    right = pl.BlockSpec((tm, te), lambda j, i: (i, nj + j))
    vec = pl.BlockSpec((1, te), lambda j, i: (0, j))
    return pl.pallas_call(
        body, name=name, grid=(nj, m // tm),
        in_specs=[pl.BlockSpec((tm, d), lambda j, i: (i, 0)), pl.BlockSpec((None, te, d), lambda j, i: (0, j, 0)),
                  tile, right, vec],
        out_specs=[tile, right, vec, vec],
        out_shape=[jax.ShapeDtypeStruct((m, e), BF16), jax.ShapeDtypeStruct((m, 2 * e), BF16),
                   jax.ShapeDtypeStruct((1, e), F32), jax.ShapeDtypeStruct((1, e), F32)],
        compiler_params=_params("parallel", "arbitrary"))(dh, w_out, o, p, scale)


def mm_nt_norm_bwd(g, w, h, gain, dh_out, *, name, job=None):
    m, kc = g.shape
    s, nout, kcs = w.shape
    assert s * kcs == kc
    tm, tk = _tile(m, 1088, 16), _tile(kcs, 1024, 128)
    per = kcs // tk
    nk = kc // tk

    def body(g_ref, w_ref, h_ref, gain_ref, dh_ref, o_ref, dg_ref):
        i, kk = pl.program_id(0), pl.program_id(1)
        part = lax.dot_general(g_ref[...].astype(BF16), w_ref[...], (((1,), (1,)), ((), ())),
                               preferred_element_type=F32)

        @pl.when(kk == 0)
        def _():
            o_ref[...] = part

        @pl.when(kk > 0)
        def _():
            o_ref[...] += part

        @pl.when(kk == nk - 1)
        def _():
            dh, dg = jax.vjp(_rms, h_ref[...], gain_ref[...])[1](o_ref[...])
            o_ref[...] = dh_ref[...] + dh

            @pl.when(i == 0)
            def _():
                dg_ref[...] = dg

            @pl.when(i > 0)
            def _():
                dg_ref[...] += dg

    row_tile = pl.BlockSpec((tm, nout), lambda i, kk: (i, 0))
    res = _call(
        body, name=name, grid=(m // tm, nk),
        in_specs=[pl.BlockSpec((tm, tk), lambda i, kk: (i, kk)),
                  pl.BlockSpec((None, nout, tk), lambda i, kk: (kk // per, 0, kk % per)),
                  row_tile, pl.BlockSpec((1, nout), lambda i, kk: (0, 0)), row_tile],
        out_specs=[row_tile, pl.BlockSpec((1, nout), lambda i, kk: (0, 0))],
        out_shape=[jax.ShapeDtypeStruct((m, nout), F32), jax.ShapeDtypeStruct((1, nout), F32)],
        args=(g, w, h, gain, dh_out), semantics=("arbitrary", "arbitrary"), job=job)
    return (res[0], res[1]) if job is None else ((res[0][0], res[0][1]), res[1])


def mm_nt(g, w, *, name, job=None):
    m, kc = g.shape
    s, nout, kcs = w.shape
    assert s * kcs == kc
    tm, tno, tk = _tile(m, 2176, 16), _tile(nout, 1024, 128), _tile(kcs, 1024, 128)
    per = kcs // tk

    def body(g_ref, w_ref, o_ref):
        kk = pl.program_id(2)
        part = lax.dot_general(g_ref[...].astype(BF16), w_ref[...], (((1,), (1,)), ((), ())),
                               preferred_element_type=F32)

        @pl.when(kk == 0)
        def _():
            o_ref[...] = part

        @pl.when(kk > 0)
        def _():
            o_ref[...] += part

    return _with_job(_call(
        body, name=name, grid=(m // tm, nout // tno, kc // tk),
        in_specs=[pl.BlockSpec((tm, tk), lambda i, j, kk: (i, kk)),
                  pl.BlockSpec((None, tno, tk), lambda i, j, kk: (kk // per, j, kk % per))],
        out_specs=[pl.BlockSpec((tm, tno), lambda i, j, kk: (i, j))],
        out_shape=[jax.ShapeDtypeStruct((m, nout), F32)], args=(g, w),
        semantics=("parallel", "parallel", "arbitrary"), job=job), job)


def mm_tn(a, g, shards, *, name, job=None):
    m, ka = a.shape
    n = g.shape[1]
    ns = n // shards
    tka, tn = _tile(ka, 1024, 128), _tile(ns, 512, 128)
    row_bytes = tka * jnp.dtype(a.dtype).itemsize + tn * jnp.dtype(g.dtype).itemsize
    tm = _tile(m, TN_OPERAND_BYTES // (2 * row_bytes), 16)
    per = ns // tn

    def body(a_ref, g_ref, o_ref):
        mm = pl.program_id(2)
        part = lax.dot_general(a_ref[...].astype(BF16), g_ref[...].astype(BF16), (((0,), (0,)), ((), ())),
                               preferred_element_type=F32)

        @pl.when(mm == 0)
        def _():
            o_ref[...] = part

        @pl.when(mm > 0)
        def _():
            o_ref[...] += part

    return _with_job(_call(
        body, name=name, grid=(ka // tka, n // tn, m // tm),
        in_specs=[pl.BlockSpec((tm, tka), lambda i, j, mm: (mm, i)),
                  pl.BlockSpec((tm, tn), lambda i, j, mm: (mm, j))],
        out_specs=[pl.BlockSpec((None, tka, tn), lambda i, j, mm: (j // per, i, j % per))],
        out_shape=[jax.ShapeDtypeStruct((shards, ka, ns), F32)], args=(a, g),
        semantics=("parallel", "parallel", "arbitrary"), job=job), job)


def grp_nn_gate(a, w, bias, p, scale, *, name):
    m = a.shape[0]
    tm = _tile(m, 1088, 16)
    ng = len(POOL_WINDOWS)

    def body(a_ref, w_ref, b_ref, z_ref, s_ref, o_ref, y_ref):
        wj = w_ref[...].reshape(POOL_GROUP, POOL_GROUP)
        o = jnp.dot(a_ref[...].astype(BF16), wj, preferred_element_type=F32) + b_ref[...]
        o_ref[...] = o
        y_ref[...] = _pool_gate(o, z_ref[...], s_ref[...]).astype(y_ref.dtype)

    tile = pl.BlockSpec((tm, POOL_GROUP), lambda i, j: (i, j))
    vec = pl.BlockSpec((1, POOL_GROUP), lambda i, j: (0, j))
    return pl.pallas_call(
        body, name=name, grid=(m // tm, ng),
        in_specs=[tile,
                  pl.BlockSpec((N_CHIPS, None, POOL_GROUP // N_CHIPS, POOL_GROUP), lambda i, j: (0, j, 0, 0)),
                  vec, pl.BlockSpec((tm, POOL_GROUP), lambda i, j: (i, ng + j)), vec],
        out_specs=[tile, tile],
        out_shape=[jax.ShapeDtypeStruct((m, ng * POOL_GROUP), F32), jax.ShapeDtypeStruct((m, ng * POOL_GROUP), BF16)],
        compiler_params=_params("parallel", "parallel"))(a, w, bias, p, scale)


def grp_nt(g, w, *, name):
    m = g.shape[0]
    tm = _tile(m, 1088, 16)
    ng = len(POOL_WINDOWS)

    def body(g_ref, w_ref, o_ref):
        wj = w_ref[...].reshape(POOL_GROUP, POOL_GROUP)
        o_ref[...] = lax.dot_general(g_ref[...].astype(BF16), wj, (((1,), (1,)), ((), ())),
                                     preferred_element_type=F32)

    return pl.pallas_call(
        body, name=name, grid=(m // tm, ng),
        in_specs=[pl.BlockSpec((tm, POOL_GROUP), lambda i, j: (i, j)),
                  pl.BlockSpec((N_CHIPS, None, POOL_GROUP // N_CHIPS, POOL_GROUP), lambda i, j: (0, j, 0, 0))],
        out_specs=pl.BlockSpec((tm, POOL_GROUP), lambda i, j: (i, j)),
        out_shape=jax.ShapeDtypeStruct((m, ng * POOL_GROUP), F32),
        compiler_params=_params("parallel", "parallel"))(g, w)


def grp_tn(a, g, *, name):
    m = a.shape[0]
    tm = _tile(m, 1088, 16)
    ng = len(POOL_WINDOWS)
    rows = POOL_GROUP // N_CHIPS

    def body(a_ref, g_ref, o_ref):
        mm = pl.program_id(1)
        part = lax.dot_general(a_ref[...].astype(BF16), g_ref[...].astype(BF16), (((0,), (0,)), ((), ())),
                               preferred_element_type=F32).reshape(N_CHIPS, rows, POOL_GROUP)

        @pl.when(mm == 0)
        def _():
            o_ref[...] = part

        @pl.when(mm > 0)
        def _():
            o_ref[...] += part

    return pl.pallas_call(
        body, name=name, grid=(ng, m // tm),
        in_specs=[pl.BlockSpec((tm, POOL_GROUP), lambda j, mm: (mm, j)),
                  pl.BlockSpec((tm, POOL_GROUP), lambda j, mm: (mm, j))],
        out_specs=pl.BlockSpec((N_CHIPS, None, rows, POOL_GROUP), lambda j, mm: (0, j, 0, 0)),
        out_shape=jax.ShapeDtypeStruct((N_CHIPS, ng, rows, POOL_GROUP), F32),
        compiler_params=_params("parallel", "arbitrary"))(a, g)


def final_loss(h, target, gain, first, last, *, name):
    m, d = h.shape
    tr = m // 16

    def body(h_ref, t_ref, g_ref, dh_ref, dg_ref, sq_ref):
        i = pl.program_id(0)
        y, vjp = jax.vjp(_rms, h_ref[...], g_ref[...])
        pos = i * tr + lax.broadcasted_iota(jnp.int32, (tr, 1), 0)
        diff = jnp.where((pos >= first) & (pos < last), y - t_ref[...], 0.0)
        dh_ref[...], dg = vjp(diff / d)
        sq = _colsum(diff * diff)

        @pl.when(i == 0)
        def _():
            dg_ref[...] = dg
            sq_ref[...] = sq

        @pl.when(i > 0)
        def _():
            dg_ref[...] += dg
            sq_ref[...] += sq

    tile, vec = pl.BlockSpec((tr, d), lambda i: (i, 0)), pl.BlockSpec((1, d), lambda i: (0, 0))
    return pl.pallas_call(
        body, name=name, grid=(m // tr,), in_specs=[tile, tile, vec], out_specs=[tile, vec, vec],
        out_shape=[jax.ShapeDtypeStruct((m, d), F32), jax.ShapeDtypeStruct((1, d), F32),
                   jax.ShapeDtypeStruct((1, d), F32)],
        compiler_params=_params("arbitrary"))(h, target, gain)


def _rms(h, g):
    return h * lax.rsqrt(jnp.mean(h * h, axis=-1, keepdims=True) + EPS) * g


def _colsum(v):
    return jnp.sum(v, axis=0, keepdims=True)


def _s5_gate(y0, q, z):
    yg = jax.nn.gelu(y0)
    return yg * jax.nn.sigmoid(q) * jax.nn.silu(z)


def _pool_gate(o, z, scale):
    return o * scale * jax.nn.silu(z)


def _shift_rows(v, d, down):
    t = v.shape[0]
    row = lax.broadcasted_iota(jnp.int32, v.shape, 0)
    if down:
        return jnp.where(row >= d, pltpu.roll(v, d, 0), 0.0)
    return jnp.where(row < t - d, pltpu.roll(v, t - d, 0), 0.0)


def _cmul(ar, ai, br, bi):
    return ar * br - ai * bi, ar * bi + ai * br


def _scan(xr, xi, ar, ai, cr, ci, down):
    t, n = xr.shape
    nb = t // SUBLANES
    pw = [(ar, ai)]
    for _ in range(SUBLANES - 1):
        pw.append(_cmul(*pw[-1], ar, ai))
    sub = lax.broadcasted_iota(jnp.int32, (SUBLANES, n), 0)

    def table(exps):
        tr, ti = jnp.zeros((SUBLANES, n), F32), jnp.zeros((SUBLANES, n), F32)
        for r, e in enumerate(exps):
            if e:
                tr, ti = jnp.where(sub == r, pw[e - 1][0], tr), jnp.where(sub == r, pw[e - 1][1], ti)
        return tr[None], ti[None]

    sr, si = xr.reshape(nb, SUBLANES, n), xi.reshape(nb, SUBLANES, n)
    d = 1
    while d < SUBLANES:
        mr, mi = table([d if (r >= d if down else r < SUBLANES - d) else 0 for r in range(SUBLANES)])
        turn = d if down else SUBLANES - d
        hr, hi = pltpu.roll(sr, turn, 1), pltpu.roll(si, turn, 1)
        sr, si = sr + mr * hr - mi * hi, si + mr * hi + mi * hr
        d *= 2
    edge = SUBLANES - 1 if down else 0
    qr, qi = table([r + 1 if down else SUBLANES - r for r in range(SUBLANES)])
    qr, qi = qr[0], qi[0]
    in_r, in_i = jnp.broadcast_to(cr, (SUBLANES, n)), jnp.broadcast_to(ci, (SUBLANES, n))
    out_r, out_i = [None] * nb, [None] * nb
    for b in (range(nb) if down else reversed(range(nb))):
        out_r[b] = sr[b] + qr * in_r - qi * in_i
        out_i[b] = si[b] + qr * in_i + qi * in_r
        in_r = jnp.broadcast_to(out_r[b][edge:edge + 1, :], (SUBLANES, n))
        in_i = jnp.broadcast_to(out_i[b][edge:edge + 1, :], (SUBLANES, n))
    return jnp.concatenate(out_r, axis=0), jnp.concatenate(out_i, axis=0), in_r[0:1, :], in_i[0:1, :]


def s5_fwd(uz, bblk, cblk, ar, ai, dskip, *, name, job=None):
    lp = uz.shape[0]
    t = _tile(lp, SCAN_CHUNK, 16)
    nst = S5_GROUPS * S5_STATE

    def body(u_ref, b_ref, c_ref, ar_ref, ai_ref, d_ref, y_ref, yg_ref, sr_ref, si_ref, cr, ci):
        step = pl.program_id(1)

        @pl.when(step == 0)
        def _():
            cr[...] = jnp.zeros_like(cr)
            ci[...] = jnp.zeros_like(ci)

        u = u_ref[...]
        a_r, a_i = ar_ref[...], ai_ref[...]
        x = jnp.dot(u.astype(BF16), b_ref[...].astype(BF16), preferred_element_type=F32)
        sr, si, cr[...], ci[...] = _scan(x[:, :STATE_BLOCK], x[:, STATE_BLOCK:], a_r, a_i, cr[...], ci[...], True)
        sr_ref[...] = sr
        si_ref[...] = si
        s = jnp.concatenate([sr, si], axis=1).astype(BF16)
        y0 = lax.dot_general(s, c_ref[...].astype(BF16), (((1,), (1,)), ((), ())),
                             preferred_element_type=F32) + d_ref[...] * u
        y_ref[...] = y0
        yg_ref[...] = jax.nn.gelu(y0).astype(yg_ref.dtype)

    return _call(
        body, name=name, grid=(S5_BLOCKS, lp // t),
        in_specs=[pl.BlockSpec((t, LANE_BLOCK), lambda b, c: (c, b)),
                  pl.BlockSpec((None, LANE_BLOCK, 2 * STATE_BLOCK), lambda b, c: (b, 0, 0)),
                  pl.BlockSpec((None, LANE_BLOCK, 2 * STATE_BLOCK), lambda b, c: (b, 0, 0)),
                  pl.BlockSpec((1, STATE_BLOCK), lambda b, c: (0, b)),
                  pl.BlockSpec((1, STATE_BLOCK), lambda b, c: (0, b)),
                  pl.BlockSpec((1, LANE_BLOCK), lambda b, c: (0, b))],
        out_specs=[pl.BlockSpec((t, LANE_BLOCK), lambda b, c: (c, b)),
                   pl.BlockSpec((t, LANE_BLOCK), lambda b, c: (c, b)),
                   pl.BlockSpec((t, STATE_BLOCK), lambda b, c: (c, b)),
                   pl.BlockSpec((t, STATE_BLOCK), lambda b, c: (c, b))],
        out_shape=[jax.ShapeDtypeStruct((lp, D_MODEL), F32), jax.ShapeDtypeStruct((lp, D_MODEL), BF16),
                   jax.ShapeDtypeStruct((lp, nst), F32), jax.ShapeDtypeStruct((lp, nst), F32)],
        scratch=[pltpu.VMEM((1, STATE_BLOCK), F32), pltpu.VMEM((1, STATE_BLOCK), F32)],
        args=(uz, bblk, cblk, ar, ai, dskip), semantics=("parallel", "arbitrary"), job=job)


def s5_bwd(dy_direct, dyg, y0, uz, sr, si, bblk, cblk, ar, ai, dskip, dp, *, name, job=None):
    lp = uz.shape[0]
    t = _tile(lp, SCAN_CHUNK, 16)
    nch = lp // t
    nst = S5_GROUPS * S5_STATE

    def body(dyd_ref, dyg_ref, y0_ref, u_ref, sr_ref, si_ref, b_ref, c_ref, ar_ref, ai_ref, d_ref, _, du_ref,
             db_ref, dc_ref, dar_ref, dai_ref, dd_ref, cr, ci):
        step = pl.program_id(1)

        @pl.when(step == 0)
        def _():
            cr[...] = jnp.zeros_like(cr)
            ci[...] = jnp.zeros_like(ci)

        dy = dyd_ref[...] + jax.vjp(jax.nn.gelu, y0_ref[...])[1](dyg_ref[...])[0]
        u = u_ref[...]
        dyb, ub = dy.astype(BF16), u.astype(BF16)
        a_r, a_i = ar_ref[...], -ai_ref[...]
        g = jnp.dot(dyb, c_ref[...].astype(BF16), preferred_element_type=F32)
        gr, gi = g[:, :STATE_BLOCK], g[:, STATE_BLOCK:]
        nxt_r, nxt_i = cr[...], ci[...]
        last = lax.broadcasted_iota(jnp.int32, gr.shape, 0) == t - 1
        lr, li, cr[...], ci[...] = _scan(gr, gi, a_r, a_i, nxt_r, nxt_i, False)
        nr = _shift_rows(lr, 1, False) + jnp.where(last, nxt_r, 0.0)
        ni = _shift_rows(li, 1, False) + jnp.where(last, nxt_i, 0.0)
        s_r, s_i = sr_ref[...], si_ref[...]
        dar = _colsum(s_r * nr + s_i * ni)
        dai = _colsum(s_r * ni - s_i * nr)
        lam = jnp.concatenate([lr, li], axis=1).astype(BF16)
        sb = jnp.concatenate([s_r, s_i], axis=1).astype(BF16)
        dc = lax.dot_general(dyb, sb, (((0,), (0,)), ((), ())), preferred_element_type=F32)
        db = lax.dot_general(ub, lam, (((0,), (0,)), ((), ())), preferred_element_type=F32)
        du_ref[...] = (lax.dot_general(lam, b_ref[...].astype(BF16), (((1,), (1,)), ((), ())),
                                       preferred_element_type=F32) + d_ref[...] * dy).astype(du_ref.dtype)
        dd = _colsum(dy * u)

        @pl.when(step == 0)
        def _():
            db_ref[...] = db
            dc_ref[...] = dc
            dar_ref[...] = dar
            dai_ref[...] = dai
            dd_ref[...] = dd

        @pl.when(step > 0)
        def _():
            db_ref[...] += db
            dc_ref[...] += dc
            dar_ref[...] += dar
            dai_ref[...] += dai
            dd_ref[...] += dd

    rev = lambda b, c: (nch - 1 - c, b)
    return _call(
        body, name=name, grid=(S5_BLOCKS, nch),
        in_specs=[pl.BlockSpec((t, LANE_BLOCK), rev), pl.BlockSpec((t, LANE_BLOCK), rev),
                  pl.BlockSpec((t, LANE_BLOCK), rev), pl.BlockSpec((t, LANE_BLOCK), rev),
                  pl.BlockSpec((t, STATE_BLOCK), rev), pl.BlockSpec((t, STATE_BLOCK), rev),
                  pl.BlockSpec((None, LANE_BLOCK, 2 * STATE_BLOCK), lambda b, c: (b, 0, 0)),
                  pl.BlockSpec((None, LANE_BLOCK, 2 * STATE_BLOCK), lambda b, c: (b, 0, 0)),
                  pl.BlockSpec((1, STATE_BLOCK), lambda b, c: (0, b)),
                  pl.BlockSpec((1, STATE_BLOCK), lambda b, c: (0, b)),
                  pl.BlockSpec((1, LANE_BLOCK), lambda b, c: (0, b)),
                  pl.BlockSpec(memory_space=pl.ANY)],
        out_specs=[pl.BlockSpec((t, LANE_BLOCK), rev),
                   pl.BlockSpec((None, LANE_BLOCK, 2 * STATE_BLOCK), lambda b, c: (b, 0, 0)),
                   pl.BlockSpec((None, LANE_BLOCK, 2 * STATE_BLOCK), lambda b, c: (b, 0, 0)),
                   pl.BlockSpec((1, STATE_BLOCK), lambda b, c: (0, b)),
                   pl.BlockSpec((1, STATE_BLOCK), lambda b, c: (0, b)),
                   pl.BlockSpec((1, LANE_BLOCK), lambda b, c: (0, b))],
        out_shape=[jax.ShapeDtypeStruct(dp.shape, dp.dtype),
                   jax.ShapeDtypeStruct((S5_BLOCKS, LANE_BLOCK, 2 * STATE_BLOCK), F32),
                   jax.ShapeDtypeStruct((S5_BLOCKS, LANE_BLOCK, 2 * STATE_BLOCK), F32),
                   jax.ShapeDtypeStruct((1, nst), F32), jax.ShapeDtypeStruct((1, nst), F32),
                   jax.ShapeDtypeStruct((1, D_MODEL), F32)],
        scratch=[pltpu.VMEM((1, STATE_BLOCK), F32), pltpu.VMEM((1, STATE_BLOCK), F32)],
        aliases={11: 0}, args=(dy_direct, dyg, y0, uz, sr, si, bblk, cblk, ar, ai, dskip, dp),
        semantics=("parallel", "arbitrary"), job=job)


def _s5_prep(lam_re, lam_im, log_dt, b_re, b_im, c_re, c_im):
    dt = jnp.exp(log_dt)
    mag = jnp.exp(lam_re * dt)
    ar = mag * jnp.cos(lam_im * dt)
    ai = mag * jnp.sin(lam_im * dt)
    den = lam_re * lam_re + lam_im * lam_im
    kr = ((ar - 1.0) * lam_re + ai * lam_im) / den
    ki = (ai * lam_re - (ar - 1.0) * lam_im) / den
    bbr = kr[:, None, :] * b_re - ki[:, None, :] * b_im
    bbi = kr[:, None, :] * b_im + ki[:, None, :] * b_re
    rows = S5_GROUPS * S5_GROUP
    expand = (lax.broadcasted_iota(jnp.int32, (S5_STATE, STATE_BLOCK), 1) % S5_STATE
              == lax.broadcasted_iota(jnp.int32, (S5_STATE, STATE_BLOCK), 0)).astype(F32)
    own = ((lax.broadcasted_iota(jnp.int32, (rows, STATE_BLOCK), 0) // S5_GROUP) % GROUPS_PER_BLOCK
           == lax.broadcasted_iota(jnp.int32, (rows, STATE_BLOCK), 1) // S5_STATE).astype(F32)

    def blocks(v):
        wide = jnp.dot(v.reshape(rows, S5_STATE), expand, precision=lax.Precision.HIGHEST,
                       preferred_element_type=F32)
        return (wide * own).reshape(S5_BLOCKS, LANE_BLOCK, STATE_BLOCK)

    btab = jnp.concatenate([blocks(bbr), blocks(bbi)], axis=2)
    ctab = jnp.concatenate([blocks(c_re), -blocks(c_im)], axis=2)
    return ar, ai, btab, ctab


_S5_PREP_OUT = [(S5_GROUPS, S5_STATE), (S5_GROUPS, S5_STATE), (S5_BLOCKS, LANE_BLOCK, 2 * STATE_BLOCK),
                (S5_BLOCKS, LANE_BLOCK, 2 * STATE_BLOCK)]


def s5_prep(params, *, name):
    def body(*refs):
        for ref, val in zip(refs[7:], _s5_prep(*[r[...] for r in refs[:7]]), strict=True):
            ref[...] = val

    return pl.pallas_call(body, name=name, out_shape=[jax.ShapeDtypeStruct(s, F32) for s in _S5_PREP_OUT],
                          compiler_params=pltpu.CompilerParams(vmem_limit_bytes=VMEM_LIMIT))(*params)


def s5_prep_bwd(params, cotangents, *, name):
    def body(*refs):
        grads = jax.vjp(_s5_prep, *[r[...] for r in refs[:7]])[1](tuple(r[...] for r in refs[7:11]))
        for ref, val in zip(refs[11:], grads, strict=True):
            ref[...] = val

    return pl.pallas_call(body, name=name, out_shape=[jax.ShapeDtypeStruct(p.shape, F32) for p in params],
                          compiler_params=pltpu.CompilerParams(vmem_limit_bytes=VMEM_LIMIT))(*params, *cotangents)


def _silu_grad(z):
    s = jax.nn.sigmoid(z)
    return s * (1.0 + z * (1.0 - s))


def conv_fwd(p, conv_w, conv_b, *, name):
    lp = p.shape[0]
    tr = lp // 16
    e = CONV_E
    hb = tr // 8

    def body(p_ref, cgh_ref, vh_ref, w_ref, b_ref, o_ref):
        i = pl.program_id(0)
        bg, cg, v, z = (p_ref[:, k * e:(k + 1) * e] for k in range(4))
        hc = cg * v
        halo = jnp.where(i > 0, cgh_ref[...] * vh_ref[...], 0.0)
        ext = jnp.concatenate([halo, hc], axis=0)
        w = w_ref[...]
        conv = (w[0:1] * pltpu.roll(ext, 2, 0)[8:] + w[1:2] * pltpu.roll(ext, 1, 0)[8:] + w[2:3] * hc
                + b_ref[...])
        o_ref[...] = (bg * conv * jax.nn.silu(z)).astype(o_ref.dtype)

    prev = lambda col: (lambda i: (jnp.maximum(i * hb - 1, 0), col))
    return pl.pallas_call(
        body, name=name, grid=(lp // tr,),
        in_specs=[pl.BlockSpec((tr, 4 * e), lambda i: (i, 0)),
                  pl.BlockSpec((8, e), prev(1)), pl.BlockSpec((8, e), prev(2)),
                  pl.BlockSpec((3, e), lambda i: (0, 0)), pl.BlockSpec((1, e), lambda i: (0, 0))],
        out_specs=pl.BlockSpec((tr, e), lambda i: (i, 0)),
        out_shape=jax.ShapeDtypeStruct((lp, e), BF16),
        compiler_params=_params("parallel"))(p, p, p, conv_w, conv_b)


def conv_bwd(dh, w_out, p, conv_w, conv_b, *, name):
    lp = p.shape[0]
    tr = lp // 16
    e = CONV_E
    d = dh.shape[1]
    hb = tr // 8
    nt = lp // tr
    width = 512

    def body(dh_ref, p_ref, cgh_ref, vh_ref, dhn_ref, bgn_ref, zn_ref, wo_ref, w_ref, b_ref, dp_ref, dw_ref, db_ref):
        i = pl.program_id(0)
        rows, rows_n = dh_ref[...].astype(BF16), dhn_ref[...].astype(BF16)
        for c0 in range(0, e, width):
            cols = slice(c0, c0 + width)
            wo = wo_ref[cols, :]
            dy = lax.dot_general(rows, wo, (((1,), (1,)), ((), ())), preferred_element_type=F32)
            dy_n = lax.dot_general(rows_n, wo, (((1,), (1,)), ((), ())), preferred_element_type=F32)
            bg, cg, v, z = (p_ref[:, k * e + c0:k * e + c0 + width] for k in range(4))
            w = w_ref[:, cols]
            hc = cg * v
            halo = jnp.where(i > 0, cgh_ref[:, cols] * vh_ref[:, cols], 0.0)
            ext = jnp.concatenate([halo, hc], axis=0)
            hc2, hc1 = pltpu.roll(ext, 2, 0)[8:], pltpu.roll(ext, 1, 0)[8:]
            yc = w[0:1] * hc2 + w[1:2] * hc1 + w[2:3] * hc + b_ref[:, cols]
            sz = jax.nn.silu(z)
            dyc = dy * bg * sz
            dyc_n = jnp.where(i < nt - 1, dy_n * bgn_ref[:, cols] * jax.nn.silu(zn_ref[:, cols]), 0.0)
            ext_n = jnp.concatenate([dyc, dyc_n], axis=0)
            n_rows = tr + 8
            dhc = (w[2:3] * dyc + w[1:2] * pltpu.roll(ext_n, n_rows - 1, 0)[:tr]
                   + w[0:1] * pltpu.roll(ext_n, n_rows - 2, 0)[:tr])
            for k, val in enumerate((dy * yc * sz, dhc * v, dhc * cg, dy * bg * yc * _silu_grad(z))):
                dp_ref[:, k * e + c0:k * e + c0 + width] = val.astype(dp_ref.dtype)
            dw = jnp.concatenate([_colsum(dyc * hc2), _colsum(dyc * hc1), _colsum(dyc * hc)], axis=0)
            db = _colsum(dyc)

            @pl.when(i == 0)
            def _(cols=cols, dw=dw, db=db):
                dw_ref[:, cols] = dw
                db_ref[:, cols] = db

            @pl.when(i > 0)
            def _(cols=cols, dw=dw, db=db):
                dw_ref[:, cols] += dw
                db_ref[:, cols] += db

    prev = lambda col: (lambda i: (jnp.maximum(i * hb - 1, 0), col))
    nxt = lambda col: (lambda i: (jnp.minimum((i + 1) * hb, lp // 8 - 1), col))
    return pl.pallas_call(
        body, name=name, grid=(nt,),
        in_specs=[pl.BlockSpec((tr, d), lambda i: (i, 0)), pl.BlockSpec((tr, 4 * e), lambda i: (i, 0)),
                  pl.BlockSpec((8, e), prev(1)), pl.BlockSpec((8, e), prev(2)),
                  pl.BlockSpec((8, d), nxt(0)), pl.BlockSpec((8, e), nxt(0)), pl.BlockSpec((8, e), nxt(3)),
                  pl.BlockSpec((None, e, d), lambda i: (0, 0, 0)),
                  pl.BlockSpec((3, e), lambda i: (0, 0)), pl.BlockSpec((1, e), lambda i: (0, 0))],
        out_specs=[pl.BlockSpec((tr, 4 * e), lambda i: (i, 0)), pl.BlockSpec((3, e), lambda i: (0, 0)),
                   pl.BlockSpec((1, e), lambda i: (0, 0))],
        out_shape=[jax.ShapeDtypeStruct((lp, 4 * e), BF16), jax.ShapeDtypeStruct((3, e), F32),
                   jax.ShapeDtypeStruct((1, e), F32)],
        compiler_params=_params("arbitrary"))(dh, p, p, p, dh, p, p, w_out, conv_w, conv_b)


def _pool_counts(i, tr, rows, offset, w):
    t = (i * tr + offset + 1 + lax.broadcasted_iota(jnp.int32, (rows, 1), 0)).astype(F32)
    return jnp.minimum(t, float(w))


def pool_fwd(p, *, name):
    lp = p.shape[0]
    tr = lp // 16
    h = POOL_HALO
    hb = tr // h

    def body(u_ref, uh_ref, o_ref):
        i = pl.program_id(0)
        u = u_ref[...]
        ext = jnp.concatenate([jnp.where(i > 0, uh_ref[...], 0.0), u], axis=0)
        for k, w in enumerate(POOL_WINDOWS):
            cols = slice(k * POOL_GROUP, (k + 1) * POOL_GROUP)
            s = ext[:, cols]
            d = 1
            while d < w:
                s = s + pltpu.roll(s, d, 0)
                d *= 2
            o_ref[:, cols] = (s[h:] / _pool_counts(i, tr, tr, 0, w) - u[:, cols]).astype(o_ref.dtype)

    return pl.pallas_call(
        body, name=name, grid=(lp // tr,),
        in_specs=[pl.BlockSpec((tr, POOL_E), lambda i: (i, 0)),
                  pl.BlockSpec((h, POOL_E), lambda i: (jnp.maximum(i * hb - 1, 0), 0))],
        out_specs=pl.BlockSpec((tr, POOL_E), lambda i: (i, 0)),
        out_shape=jax.ShapeDtypeStruct((lp, POOL_E), BF16),
        compiler_params=_params("parallel"))(p, p)


def pool_bwd(dmix, dp, *, name):
    lp = dmix.shape[0]
    tr = lp // 16
    h = POOL_HALO
    hb = tr // h
    nt = lp // tr

    def body(dm_ref, dmn_ref, _, du_ref):
        i = pl.program_id(0)
        dm = dm_ref[...]
        dmn = jnp.where(i < nt - 1, dmn_ref[...], 0.0)
        n_rows = tr + h
        for k, w in enumerate(POOL_WINDOWS):
            cols = slice(k * POOL_GROUP, (k + 1) * POOL_GROUP)
            s = jnp.concatenate([dm[:, cols] / _pool_counts(i, tr, tr, 0, w),
                                 dmn[:, cols] / _pool_counts(i, tr, h, tr, w)], axis=0)
            d = 1
            while d < w:
                s = s + pltpu.roll(s, n_rows - d, 0)
                d *= 2
            du_ref[:, cols] = (s[:tr] - dm[:, cols]).astype(du_ref.dtype)

    return pl.pallas_call(
        body, name=name, grid=(nt,),
        in_specs=[pl.BlockSpec((tr, POOL_E), lambda i: (i, 0)),
                  pl.BlockSpec((h, POOL_E), lambda i: (jnp.minimum((i + 1) * hb, lp // h - 1), 0)),
                  pl.BlockSpec(memory_space=pl.ANY)],
        out_specs=pl.BlockSpec((tr, POOL_E), lambda i: (i, 0)),
        out_shape=jax.ShapeDtypeStruct(dp.shape, dp.dtype),
        input_output_aliases={2: 0},
        compiler_params=_params("parallel"))(dmix, dmix, dp)


def _adamw_math(w, g, m, v):
    nm = ADAM_B1 * m + (1.0 - ADAM_B1) * g
    nv = ADAM_B2 * v + (1.0 - ADAM_B2) * jnp.square(g)
    m_hat = nm / (1.0 - ADAM_B1 ** ADAM_STEP)
    v_hat = nv / (1.0 - ADAM_B2 ** ADAM_STEP)
    return -ADAM_LR * (m_hat / (jnp.sqrt(v_hat) + ADAM_EPS) + ADAM_WD * w), nm, nv


def adamw_many(quads, *, name, steps=8, job=None):
    n = len(quads)

    def spec(shape):
        return pl.BlockSpec((shape[0] // steps,) + shape[1:], lambda i, nd=len(shape): (i,) + (0,) * (nd - 1))

    def body(*refs):
        for q in range(n):
            w_ref, g_ref, m_ref, v_ref = refs[4 * q:4 * q + 4]
            d_ref, nm_ref, nv_ref = refs[4 * n + 3 * q:4 * n + 3 * q + 3]
            d_ref[...], nm_ref[...], nv_ref[...] = _adamw_math(w_ref[...], g_ref[...], m_ref[...], v_ref[...])

    res = _call(
        body, name=name, grid=(steps,), in_specs=[spec(quad[0].shape) for quad in quads for _ in range(4)],
        out_specs=[spec(quad[0].shape) for quad in quads for _ in range(3)],
        out_shape=[jax.ShapeDtypeStruct(quad[0].shape, F32) for quad in quads for _ in range(3)],
        args=[a for quad in quads for a in quad], semantics=("parallel",), job=job)
    outs = res if job is None else res[0]
    triples = [tuple(outs[3 * q:3 * q + 3]) for q in range(n)]
    return triples if job is None else (triples, res[1])


WIRE_WIDTH = 1024
WIRE_ROWS = 1024


def _wire_plan(shapes):
    starts, row = [], 0
    for s in shapes:
        r, c = (1, s[0]) if len(s) == 1 else s
        starts.append(row)
        row += -(-(r * max(1, c // WIRE_WIDTH)) // 8) * 8
    assert row <= WIRE_ROWS, row
    return starts


def _wire_cells(shape):
    if len(shape) == 1:
        n = shape[0]
        if n <= WIRE_WIDTH:
            return [(0, n, (slice(None),))]
        return [(h, WIRE_WIDTH, (slice(h * WIRE_WIDTH, (h + 1) * WIRE_WIDTH),)) for h in range(n // WIRE_WIDTH)]
    r, c = shape
    if c <= WIRE_WIDTH:
        return [(None, c, (slice(None), slice(None)))]
    per = c // WIRE_WIDTH
    return [(j * per + h, WIRE_WIDTH, (j, slice(h * WIRE_WIDTH, (h + 1) * WIRE_WIDTH)))
            for j in range(r) for h in range(per)]


def pack_small(arrays, *, name):
    shapes = [a.shape for a in arrays]
    starts = _wire_plan(shapes)

    def body(*refs):
        out = refs[-1]
        out[...] = jnp.zeros_like(out)
        for ref, shape, row0 in zip(refs[:-1], shapes, starts, strict=True):
            for off, cols, idx in _wire_cells(shape):
                if off is None:
                    out[row0:row0 + shape[0], 0:cols] = ref[idx]
                else:
                    out[row0 + off, 0:cols] = ref[idx]

    return pl.pallas_call(body, name=name, out_shape=jax.ShapeDtypeStruct((WIRE_ROWS, WIRE_WIDTH), F32),
                          compiler_params=pltpu.CompilerParams(vmem_limit_bytes=VMEM_LIMIT))(*arrays)


def update_small(packed, starts, triples, shard_cells, chip, *, name):
    n = len(triples)

    def body(chip_ref, p_ref, *refs):
        k = chip_ref[0]
        ins, outs = refs[:3 * n], refs[3 * n:]

        def cut(row0, rows, cols):
            c0, c1 = cols
            if isinstance(rows, int):
                return p_ref[row0 + rows, c0:c1]
            return p_ref[row0 + rows[0]:row0 + rows[1], c0:c1]

        for q in range(n):
            w_ref, m_ref, v_ref = ins[3 * q:3 * q + 3]
            g_ref, d_ref, nm_ref, nv_ref = outs[4 * q:4 * q + 4]
            shape = w_ref.shape
            if shard_cells[q] is None:
                pieces = [(cut(starts[q], (0, shape[0]) if off is None else off, (0, cols)), idx)
                          for off, cols, idx in _wire_cells(shape)]
            else:
                by_chip = [shard_cells[q](kk) for kk in range(N_CHIPS)]
                pieces = []
                for i, (_, _, idx) in enumerate(by_chip[0]):
                    g = cut(starts[q], *by_chip[0][i][:2])
                    for kk in range(1, N_CHIPS):
                        g = jnp.where(k == kk, cut(starts[q], *by_chip[kk][i][:2]), g)
                    pieces.append((g, idx))
            for g, idx in pieces:
                g_ref[idx] = g
                d_ref[idx], nm_ref[idx], nv_ref[idx] = _adamw_math(w_ref[idx], g, m_ref[idx], v_ref[idx])

    whole = lambda a: pl.BlockSpec(a.shape, lambda i, c_ref, nd=a.ndim: (0,) * nd)
    flat = [a for t in triples for a in t]
    out_shape = [jax.ShapeDtypeStruct(t[0].shape, F32) for t in triples for _ in range(4)]
    res = pl.pallas_call(
        body, name=name,
        grid_spec=pltpu.PrefetchScalarGridSpec(
            num_scalar_prefetch=1, grid=(1,),
            in_specs=[whole(packed)] + [whole(a) for a in flat],
            out_specs=[pl.BlockSpec(s.shape, lambda i, c_ref, nd=len(s.shape): (0,) * nd) for s in out_shape]),
        out_shape=out_shape,
        compiler_params=_params("arbitrary"))(chip.reshape(1).astype(jnp.int32), packed, *flat)
    return [tuple(res[4 * q:4 * q + 4]) for q in range(n)]


ANY = pl.BlockSpec(memory_space=pl.ANY)


def _place():
    x, y, c = lax.axis_index("x"), lax.axis_index("y"), lax.axis_index("c")
    return x, y, c, [(1 - x, y), (x, 1 - y), (1 - x, 1 - y)]


DMA_PIECE_BYTES = 8192 * 1024


def _pieces(src, dst):
    shape = src.shape
    rows, cols = shape[-2], shape[-1]
    item = jnp.dtype(src.dtype).itemsize
    unit = 32 // item
    want = max(1, (rows * cols * item) // DMA_PIECE_BYTES)
    n = 1
    if rows % unit == 0:
        n = max(d for d in range(1, rows // unit + 1) if (rows // unit) % d == 0 and d <= want)
    size = rows // n
    out = []
    for lead in (range(shape[0]) if len(shape) == 3 else [None]):
        for i in range(n):
            sel = (pl.ds(i * size, size), slice(None)) if lead is None else (lead, pl.ds(i * size, size), slice(None))
            out.append((src.at[sel], dst.at[sel]))
    return out


def _send(src, dst, send_sem, recv_sem, peer, start=True):
    if start:
        for s, d in _pieces(src, dst):
            pltpu.make_async_remote_copy(src_ref=s, dst_ref=d, send_sem=send_sem, recv_sem=recv_sem,
                                         device_id=peer, device_id_type=MESH).start()
    return pltpu.make_async_remote_copy(src_ref=src, dst_ref=dst, send_sem=send_sem, recv_sem=recv_sem,
                                        device_id=peer, device_id_type=MESH)


def run_job(job, *, name):
    n_in, n_out = len(job.arrays), len(job.out_shapes)

    def body(*refs):
        job.fn(refs[:n_in], refs[n_in:n_in + n_out], refs[n_in + n_out:], "both")

    return pl.pallas_call(body, name=name, in_specs=[ANY] * n_in, out_specs=[ANY] * n_out,
                          out_shape=job.out_shapes, scratch_shapes=job.sems,
                          input_output_aliases={a: a for a in job.aliased()})(*job.arrays)


def gather_chips(shards, *, name):
    n = len(shards)

    def body(*refs):
        ins, outs = refs[:n], refs[n:2 * n]
        send_sems, recv_sems = refs[2 * n:]
        x, y, c, chips = _place()
        k = 2 * x + y
        copies = []
        for a in range(n):
            for j, peer in enumerate([(px, py, c) for px, py in chips] + [(x, y, 1 - c)]):
                copies.append(_send(ins[a], outs[a].at[k], send_sems.at[a, j], recv_sems.at[a, j], peer))
        for cp in copies:
            cp.wait()

    return pl.pallas_call(
        body, name=name, in_specs=[ANY] * n, out_specs=[ANY] * n,
        out_shape=[jax.ShapeDtypeStruct((N_CHIPS,) + s.shape, s.dtype) for s in shards],
        scratch_shapes=[pltpu.SemaphoreType.DMA((n, 4)), pltpu.SemaphoreType.DMA((n, 4))])(*shards)


def gather_by_halves_job(shards):
    n = len(shards)

    def fn(ins, outs, sems, phase):
        send_sems, recv_sems = sems
        x, y, c, chips = _place()
        k = 2 * x + y
        sibling = (x, y, 1 - c)
        begin = phase != "finish"
        waits, arrivals = [], []
        for a in range(n):
            half = ins[a].shape[0] // 2
            waits.append(_send(ins[a], outs[a].at[k], send_sems.at[a, 6], recv_sems.at[a, 6], sibling, begin))
            mine = pl.ds(c * half, half)
            for j, (px, py) in enumerate(chips):
                arrivals.append(_send(ins[a].at[mine, :], outs[a].at[k, mine, :], send_sems.at[a, j],
                                      recv_sems.at[a, j], (px, py, c), begin))
        if phase == "start":
            return
        i = 0
        for a in range(n):
            half = ins[a].shape[0] // 2
            mine = pl.ds(c * half, half)
            for j, (px, py) in enumerate(chips):
                arrivals[i].wait_recv()
                landed = outs[a].at[2 * px + py, mine, :]
                waits.append(_send(landed, landed, send_sems.at[a, 3 + j], recv_sems.at[a, 3 + j], sibling))
                i += 1
        for cp in arrivals:
            cp.wait_send()
        for cp in waits:
            cp.wait()

    return Job(shards, [jax.ShapeDtypeStruct((N_CHIPS,) + s.shape, s.dtype) for s in shards],
               [pltpu.SemaphoreType.DMA((n, 7)), pltpu.SemaphoreType.DMA((n, 7))], fn)


def gather_halves_job(shards):
    n = len(shards)

    def fn(ins, outs, sems, phase):
        send_sems, recv_sems = sems
        x, y, c, chips = _place()
        k = 2 * x + y
        copies = []
        for a in range(n):
            half = ins[a].shape[0] // 2
            mine = pl.ds(c * half, half)
            copies.append(_send(ins[a], outs[a].at[k], send_sems.at[a, 3], recv_sems.at[a, 3], (x, y, 1 - c),
                                phase != "finish"))
            for j, (px, py) in enumerate(chips):
                copies.append(_send(ins[a].at[mine, :], outs[a].at[k, mine, :], send_sems.at[a, j],
                                    recv_sems.at[a, j], (px, py, c), phase != "finish"))
        if phase != "start":
            for cp in copies:
                cp.wait()

    return Job(shards, [jax.ShapeDtypeStruct((N_CHIPS,) + s.shape, s.dtype) for s in shards],
               [pltpu.SemaphoreType.DMA((n, 4)), pltpu.SemaphoreType.DMA((n, 4))], fn)


def forward_halves_job(gathered):
    n = len(gathered)

    def fn(ins, outs, sems, phase):
        send_sems, recv_sems = sems
        x, y, c, chips = _place()
        copies = []
        for a in range(n):
            half = outs[a].shape[1] // 2
            for j, (px, py) in enumerate(chips):
                landed = outs[a].at[2 * px + py, pl.ds(c * half, half), :]
                copies.append(_send(landed, landed, send_sems.at[a, j], recv_sems.at[a, j], (x, y, 1 - c),
                                    phase != "finish"))
        if phase != "start":
            for cp in copies:
                cp.wait()

    return Job(gathered, [jax.ShapeDtypeStruct(g.shape, g.dtype) for g in gathered],
               [pltpu.SemaphoreType.DMA((n, 3)), pltpu.SemaphoreType.DMA((n, 3))], fn, in_place=True)


def swap_job(grads):
    n = len(grads)

    def fn(ins, outs, sems, phase):
        send_sems, recv_sems = sems
        x, y, c, _ = _place()
        copies = []
        for a in range(n):
            half = ins[a].shape[1] // 2
            copies.append(_send(ins[a].at[:, pl.ds((1 - c) * half, half), :], outs[a], send_sems.at[a],
                                recv_sems.at[a], (x, y, 1 - c), phase != "finish"))
        if phase != "start":
            for cp in copies:
                cp.wait()

    return Job(grads, [jax.ShapeDtypeStruct((g.shape[0], g.shape[1] // 2, g.shape[2]), g.dtype) for g in grads],
               [pltpu.SemaphoreType.DMA((n,)), pltpu.SemaphoreType.DMA((n,))], fn)


def _chip_slot(s, k):
    rel = s ^ k
    return (rel >> 1) | ((rel & 1) << 1)


def sum_cores(g, theirs, ck, out_dtype, *, name):
    n, r, cols = g.shape
    half = r // 2
    tr = _tile(half, max(16, (1 << 19) // cols), 16)
    nb = half // tr

    def body(ck_ref, g_ref, t_ref, o_ref):
        o_ref[...] = (g_ref[...] + t_ref[...]).astype(o_ref.dtype)

    return pl.pallas_call(
        body, name=name,
        grid_spec=pltpu.PrefetchScalarGridSpec(
            num_scalar_prefetch=1, grid=(n, nb),
            in_specs=[pl.BlockSpec((None, tr, cols), lambda s, i, ck_ref: (s, ck_ref[0] * nb + i, 0)),
                      pl.BlockSpec((None, tr, cols), lambda s, i, ck_ref: (s, i, 0))],
            out_specs=pl.BlockSpec((None, tr, cols), lambda s, i, ck_ref: (_chip_slot(s, ck_ref[1]), i, 0))),
        out_shape=jax.ShapeDtypeStruct((n, half, cols), out_dtype),
        compiler_params=_params("parallel", "parallel"))(ck, g, theirs)


def scatter_job(parts):
    n = len(parts)

    def fn(ins, outs, sems, phase):
        send_sems, recv_sems = sems
        x, y, c, chips = _place()
        copies = []
        for a in range(n):
            for j, (px, py) in enumerate(chips):
                copies.append(_send(ins[a].at[1 + j], outs[a].at[j], send_sems.at[a, j], recv_sems.at[a, j],
                                    (px, py, c), phase != "finish"))
        if phase != "start":
            for cp in copies:
                cp.wait()

    return Job(parts, [jax.ShapeDtypeStruct((3,) + p.shape[1:], p.dtype) for p in parts],
               [pltpu.SemaphoreType.DMA((n, 3)), pltpu.SemaphoreType.DMA((n, 3))], fn)


def sum_chips(own, others, ck, *, name):
    _, r, cols = own.shape
    tr = _tile(r, max(16, (1 << 19) // cols), 16)

    def body(ck_ref, a_ref, b0_ref, b1_ref, b2_ref, o_ref):
        o_ref[...] = (a_ref[...].astype(F32) + b0_ref[...].astype(F32) + b1_ref[...].astype(F32)
                      + b2_ref[...].astype(F32))

    other = lambda j: pl.BlockSpec((None, tr, cols), lambda i, ck_ref: (j, i, 0))
    return pl.pallas_call(
        body, name=name,
        grid_spec=pltpu.PrefetchScalarGridSpec(
            num_scalar_prefetch=1, grid=(r // tr,),
            in_specs=[pl.BlockSpec((None, tr, cols), lambda i, ck_ref: (0, i, 0)), other(0), other(1), other(2)],
            out_specs=pl.BlockSpec((None, tr, cols), lambda i, ck_ref: (ck_ref[0], i, 0))),
        out_shape=jax.ShapeDtypeStruct((2, r, cols), F32),
        compiler_params=_params("parallel"))(ck, own, others, others, others)


def share_job(joined):
    n = len(joined)

    def fn(ins, outs, sems, phase):
        send_sems, recv_sems = sems
        x, y, c, _ = _place()
        copies = [_send(outs[a].at[c], outs[a].at[c], send_sems.at[a], recv_sems.at[a], (x, y, 1 - c),
                        phase != "finish") for a in range(n)]
        if phase != "start":
            for cp in copies:
                cp.wait()

    return Job(joined, [jax.ShapeDtypeStruct(j.shape, j.dtype) for j in joined],
               [pltpu.SemaphoreType.DMA((n,)), pltpu.SemaphoreType.DMA((n,))], fn, in_place=True)


def kernel(x, meta_tokens, norm0_g, l0_w_in, l0_lam_re, l0_lam_im, l0_log_dt, l0_b_re, l0_b_im, l0_c_re, l0_c_im, l0_d_skip, l0_w_glu, l0_b_glu, l0_w_out, norm1_g, l1_w_in, l1_conv_w, l1_conv_b, l1_w_out, norm2_g, l2_w_in, l2_w_grp, l2_b_grp, l2_scale, l2_w_out, norm3_g, l3_w_in, l3_lam_re, l3_lam_im, l3_log_dt, l3_b_re, l3_b_im, l3_c_re, l3_c_im, l3_d_skip, l3_w_glu, l3_b_glu, l3_w_out, final_g, loss_target, m_meta_tokens, m_norm0_g, m_l0_w_in, m_l0_lam_re, m_l0_lam_im, m_l0_log_dt, m_l0_b_re, m_l0_b_im, m_l0_c_re, m_l0_c_im, m_l0_d_skip, m_l0_w_glu, m_l0_b_glu, m_l0_w_out, m_norm1_g, m_l1_w_in, m_l1_conv_w, m_l1_conv_b, m_l1_w_out, m_norm2_g, m_l2_w_in, m_l2_w_grp, m_l2_b_grp, m_l2_scale, m_l2_w_out, m_norm3_g, m_l3_w_in, m_l3_lam_re, m_l3_lam_im, m_l3_log_dt, m_l3_b_re, m_l3_b_im, m_l3_c_re, m_l3_c_im, m_l3_d_skip, m_l3_w_glu, m_l3_b_glu, m_l3_w_out, m_final_g, v_meta_tokens, v_norm0_g, v_l0_w_in, v_l0_lam_re, v_l0_lam_im, v_l0_log_dt, v_l0_b_re, v_l0_b_im, v_l0_c_re, v_l0_c_im, v_l0_d_skip, v_l0_w_glu, v_l0_b_glu, v_l0_w_out, v_norm1_g, v_l1_w_in, v_l1_conv_w, v_l1_conv_b, v_l1_w_out, v_norm2_g, v_l2_w_in, v_l2_w_grp, v_l2_b_grp, v_l2_scale, v_l2_w_out, v_norm3_g, v_l3_w_in, v_l3_lam_re, v_l3_lam_im, v_l3_log_dt, v_l3_b_re, v_l3_b_im, v_l3_c_re, v_l3_c_im, v_l3_d_skip, v_l3_w_glu, v_l3_b_glu, v_l3_w_out, v_final_g):
    given = dict(locals())
    w = {n: given[n] for n in WEIGHTS}
    mom_m = {n: given["m_" + n] for n in WEIGHTS}
    mom_v = {n: given["v_" + n] for n in WEIGHTS}
    seq = x.shape[1]
    real = N_META + seq
    lp = -(-real // SCAN_ROWS) * SCAN_ROWS
    chip = 2 * lax.axis_index("x") + lax.axis_index("y")
    row = lambda a: a.reshape(1, -1)

    shard_bf16 = {n: (w[n].reshape(-1, w[n].shape[-1]) if n == 'l2_w_grp' else w[n]).astype(BF16) for n in BIG}
    layer = lambda i: [n for n in BIG if n.startswith("l%d_" % i)]
    over_ici = lambda names: gather_halves_job([shard_bf16[n] for n in names])
    full = {}

    def landed(names, arrays):
        for n, arr in zip(names, arrays, strict=True):
            if n.endswith('w_in'):
                full[n] = arr
            elif n == 'l2_w_grp':
                full[n] = arr.reshape(N_CHIPS, len(POOL_WINDOWS), POOL_GROUP // N_CHIPS, POOL_GROUP)
            else:
                full[n] = arr.reshape(1, -1, arr.shape[-1])

    landed(['l0_w_in'], run_job(gather_by_halves_job([shard_bf16['l0_w_in']]), name="gather_l0_w_in"))
    full.update(zip(SMALL_SHARDED, gather_chips([w[n] for n in SMALL_SHARDED], name="gather_small_shards"),
                    strict=True))
    meta_full = full['meta_tokens'].transpose(1, 0, 2).reshape(N_META, D_MODEL)
    conv_w_full = full['l1_conv_w'].transpose(1, 0, 2).reshape(3, CONV_E)
    b_grp_full = full['l2_b_grp'].transpose(1, 0, 2).reshape(len(POOL_WINDOWS), POOL_GROUP)

    h0 = jnp.concatenate([meta_full, x[0], jnp.zeros((lp - real, D_MODEL), F32)], axis=0)
    target = jnp.concatenate([jnp.zeros((N_META, D_MODEL), F32), loss_target[0],
                              jnp.zeros((lp - real, D_MODEL), F32)], axis=0)
    grads = {}
    saved = {}
    gip = lambda a: jnp.swapaxes(a, 1, 2)

    def s5_layer_fwd(i, h, first=False):
        pre = "l%d_" % i
        prm = [w[pre + 'lam_re'], w[pre + 'lam_im'], w[pre + 'log_dt'].reshape(-1, 1), gip(w[pre + 'b_re']),
               gip(w[pre + 'b_im']), w[pre + 'c_re'], w[pre + 'c_im']]
        a_re, a_im, bblk, cblk = s5_prep(prm, name=pre + "prep")
        tables = (a_re.reshape(1, -1), a_im.reshape(1, -1), bblk, cblk)
        ar, ai = tables[:2]
        gain = row(w["norm%d_g" % i])
        rest = [pre + 'w_glu', pre + 'w_out']
        if not first:
            uz, n = norm_mm_nn(h, gain, full[pre + 'w_in'], name=pre + "in")
            y0, yg, sr, si = s5_fwd(uz, bblk, cblk, ar, ai, row(w[pre + 'd_skip']), name=pre + "scan")
            q, y3 = glu_gate(yg, full[pre + 'w_glu'], row(w[pre + 'b_glu']), y0, uz, name=pre + "glu")
        else:
            (uz, n), halves = norm_mm_nn(h, gain, full[pre + 'w_in'], name=pre + "in", job=over_ici(rest))
            (y0, yg, sr, si), arrived = s5_fwd(
                uz, bblk, cblk, ar, ai, row(w[pre + 'd_skip']), name=pre + "scan",
                job=Job.both(over_ici(layer(1)), forward_halves_job(halves)))
            landed(rest, arrived[len(layer(1)):])
            (q, y3), whole = glu_gate(yg, full[pre + 'w_glu'], row(w[pre + 'b_glu']), y0, uz, name=pre + "glu",
                                      job=forward_halves_job(arrived[:len(layer(1))]))
            landed(layer(1), whole)
        h_new = mm_nn(y3, full[pre + 'w_out'], add=h, name=pre + "out")
        saved[i] = dict(h=h, n=n, uz=uz, y0=y0, sr=sr, si=si, yg=yg, q=q, y3=y3, tables=tables, prm=prm)
        return h_new

    def s5_layer_bwd(i, dh, carry=None):
        pre = "l%d_" % i
        s = saved[i]
        ar, ai, bblk, cblk = s['tables']
        grads[pre + 'w_out'] = mm_tn(s['y3'], dh, 1, name=pre + "d_w_out")[0]
        dy0_direct, dq, dp, db_glu = s5_gate_bwd(dh, full[pre + 'w_out'], s['y0'], s['q'], s['uz'],
                                                 name=pre + "d_y3")
        grads[pre + 'b_glu'] = db_glu
        grads[pre + 'w_glu'] = mm_tn(s['yg'], dq, 1, name=pre + "d_w_glu")[0]
        early = [] if carry is None else [pre + 'w_out', pre + 'w_glu']
        if carry is None:
            dyg = mm_nt(dq, full[pre + 'w_glu'], name=pre + "d_yg")
            res = s5_bwd(dy0_direct, dyg, s['y0'], s['uz'], s['sr'], s['si'], bblk, cblk, ar, ai,
                         row(w[pre + 'd_skip']), dp, name=pre + "d_scan")
        else:
            pieces = [chip_major(n) for n in early]
            dyg, theirs = mm_nt(dq, full[pre + 'w_glu'], name=pre + "d_yg", job=swap_job(pieces))
            sum_over_cores(early, pieces, theirs)
            res, arrived = s5_bwd(dy0_direct, dyg, s['y0'], s['uz'], s['sr'], s['si'], bblk, cblk, ar, ai,
                                  row(w[pre + 'd_skip']), dp, name=pre + "d_scan",
                                  job=scatter_job(scatter_of(carry) + [pair[n] for n in early]))
            from_chips.update(zip(layer(carry) + early, arrived, strict=True))
        dp, dbblk, dcblk, dar, dai, dd = res
        grads[pre + 'd_skip'] = dd
        cots = (dar.reshape(S5_GROUPS, S5_STATE), dai.reshape(S5_GROUPS, S5_STATE), dbblk, dcblk)
        for key, val in zip(('lam_re', 'lam_im', 'log_dt', 'b_re', 'b_im', 'c_re', 'c_im'),
                            s5_prep_bwd(s['prm'], cots, name=pre + "d_prep"), strict=True):
            grads[pre + key] = val.reshape(-1) if key == 'log_dt' else val
        if carry is None:
            grads[pre + 'w_in'] = mm_tn(s['n'], dp, N_CHIPS, name=pre + "d_w_in")
        else:
            later = [n for n in BIG if not n.startswith(pre)] + early
            sums = [sum_chips(pair[n], from_chips[n], ck, name="sum_chips_" + n) for n in later]
            grads[pre + 'w_in'], shared = mm_tn(s['n'], dp, N_CHIPS, name=pre + "d_w_in", job=share_job(sums))
            joined.update(zip(later, shared, strict=True))
        names = [n for n in layer(i) if n not in early]
        pieces = [chip_major(n) for n in names]
        (dh_in, dg), theirs = mm_nt_norm_bwd(dp, full[pre + 'w_in'], s['h'], row(w["norm%d_g" % i]), dh,
                                             name=pre + "d_n", job=swap_job(pieces))
        sum_over_cores(names, pieces, theirs)
        grads["norm%d_g" % i] = dg
        return dh_in

    h1 = s5_layer_fwd(0, h0, first=True)

    (p1, n1), halves = norm_mm_nn(h1, row(w['norm1_g']), full['l1_w_in'], name="l1_in", job=over_ici(layer(2)))
    y3_1 = conv_fwd(p1, conv_w_full, row(w['l1_conv_b']), name="l1_conv")
    h2, whole = mm_nn(y3_1, full['l1_w_out'], add=h1, name="l1_out", job=forward_halves_job(halves))
    landed(layer(2), whole)

    (p2, n2), halves = norm_mm_nn(h2, row(w['norm2_g']), full['l2_w_in'], name="l2_in", job=over_ici(layer(3)))
    mixed = pool_fwd(p2, name="l2_pool")
    o2, y3_2 = grp_nn_gate(mixed, full['l2_w_grp'], b_grp_full.reshape(1, -1), p2, row(w['l2_scale']), name="l2_grp")
    h3, whole = mm_nn(y3_2, full['l2_w_out'], add=h2, name="l2_out", job=forward_halves_job(halves))
    landed(layer(3), whole)

    h4 = s5_layer_fwd(3, h3)

    dh, grads['final_g'], sq = final_loss(h4, target, row(w['final_g']), N_META, real, name="loss")
    loss = lax.psum(0.5 / D_MODEL * jnp.sum(sq), ("x", "y", "c"))

    ck = jnp.stack([lax.axis_index("c"), chip]).astype(jnp.int32)
    pair, from_chips, joined = {}, {}, {}

    def chip_major(n):
        g = grads[n]
        if n.endswith('w_in'):
            return g
        if n == 'l2_w_grp':
            return g.reshape(N_CHIPS, -1, POOL_GROUP)
        return g.reshape(N_CHIPS, -1, g.shape[-1])

    def core_parts(i):
        return layer(i), [chip_major(n) for n in layer(i)]

    def sum_over_cores(names, pieces, theirs):
        for n, g, t in zip(names, pieces, theirs, strict=True):
            pair[n] = sum_cores(g, t, ck, F32 if n == 'small' else BF16, name="sum_cores_" + n)

    scatter_of = lambda i: [pair[n] for n in layer(i)]

    def scattered(i, arrays):
        from_chips.update(zip(layer(i), arrays, strict=True))

    dh = s5_layer_bwd(3, dh)

    grads['l2_w_out'] = mm_tn(y3_2, dh, 1, name="l2_d_w_out")[0]
    do2, dp2, dscale, dbgrp = pool_gate_bwd(dh, full['l2_w_out'], o2, p2, row(w['l2_scale']), name="l2_d_y3")
    grads['l2_scale'] = dscale
    grads['l2_b_grp'] = dbgrp
    grads['l2_w_grp'] = grp_tn(mixed, do2, name="l2_d_w_grp")
    dmix = grp_nt(do2, full['l2_w_grp'], name="l2_d_mixed")
    dp2 = pool_bwd(dmix, dp2, name="l2_d_pool")
    grads['l2_w_in'], arrived = mm_tn(n2, dp2, N_CHIPS, name="l2_d_w_in", job=scatter_job(scatter_of(3)))
    scattered(3, arrived)
    names, pieces = core_parts(2)
    (dh, dg), theirs = mm_nt_norm_bwd(dp2, full['l2_w_in'], h2, row(w['norm2_g']), dh, name="l2_d_n",
                                      job=swap_job(pieces))
    sum_over_cores(names, pieces, theirs)
    grads['norm2_g'] = dg

    grads['l1_w_out'] = mm_tn(y3_1, dh, 1, name="l1_d_w_out")[0]
    dp1, dcw, dcb = conv_bwd(dh, full['l1_w_out'], p1, conv_w_full, row(w['l1_conv_b']), name="l1_d_conv")
    grads['l1_conv_w'] = dcw
    grads['l1_conv_b'] = dcb
    grads['l1_w_in'], arrived = mm_tn(n1, dp1, N_CHIPS, name="l1_d_w_in", job=scatter_job(scatter_of(2)))
    scattered(2, arrived)
    names, pieces = core_parts(1)
    (dh, dg), theirs = mm_nt_norm_bwd(dp1, full['l1_w_in'], h1, row(w['norm1_g']), dh, name="l1_d_n",
                                      job=swap_job(pieces))
    sum_over_cores(names, pieces, theirs)
    grads['norm1_g'] = dg

    dh = s5_layer_bwd(0, dh, carry=1)
    grad_x = dh[N_META:real][None]
    grads['meta_tokens'] = dh[:N_META]

    wide = [n for n in SMALL if w[n].ndim == 3]
    wire = [grads[n].reshape(S5_GROUPS, -1) if n in wide else grads[n] for n in SMALL]
    starts = dict(zip(SMALL, _wire_plan([a.shape for a in wire]), strict=True))
    small_local = pack_small(wire, name="pack_small")
    pieces = [small_local.reshape(N_CHIPS, -1, WIRE_WIDTH)]
    sum_over_cores(['small'], pieces, run_job(swap_job(pieces), name="reduce_cores_small"))

    out_g, out_d, out_m, out_v = {}, {}, {}, {}
    as2d = lambda a: a.reshape(-1, a.shape[-1])

    def update_big(names, name, job=None):
        for n in names:
            out_g[n] = joined[n].reshape(w[n].shape)
        res = adamw_many([(as2d(w[n]), as2d(out_g[n]), as2d(mom_m[n]), as2d(mom_v[n])) for n in names],
                         name=name, job=job)
        triples, carried = res if job is not None else (res, None)
        for n, (d, nm, nv) in zip(names, triples, strict=True):
            out_d[n], out_m[n], out_v[n] = d.reshape(w[n].shape), nm.reshape(w[n].shape), nv.reshape(w[n].shape)
        return carried

    rest = ['l0_w_in', 'small']
    later = [n for n in BIG if n not in rest]
    from_chips.update(zip(rest, update_big(later, "adamw_big", scatter_job([pair[n] for n in rest])), strict=True))
    sums = [sum_chips(pair[n], from_chips[n], ck, name="sum_chips_" + n) for n in rest]
    joined.update(zip(rest, run_job(share_job(sums), name="share_cores"), strict=True))
    update_big(['l0_w_in'], "adamw_l0_w_in")
    (small_all,) = run_job(gather_by_halves_job([joined['small'].reshape(-1, WIRE_WIDTH)]), name="gather_small")
    small_all = small_all.reshape(WIRE_ROWS, WIRE_WIDTH)

    half_w = WIRE_WIDTH // 2
    shard_cells = {
        'meta_tokens': lambda k: [((0, N_META), (k * 256, (k + 1) * 256), (slice(None), slice(None)))],
        'l1_conv_w': lambda k: [(2 * j + k // 2, ((k % 2) * half_w, (k % 2 + 1) * half_w), (j, slice(None)))
                                for j in range(3)],
        'l2_b_grp': lambda k: [(j // 2, ((j % 2) * half_w + k * 128, (j % 2) * half_w + (k + 1) * 128),
                                (j, slice(None))) for j in range(len(POOL_WINDOWS))],
    }
    plain = [n for n in SMALL if n not in wide]
    res = update_small(small_all, [starts[n] for n in plain], [(w[n], mom_m[n], mom_v[n]) for n in plain],
                       [shard_cells.get(n) for n in plain], chip, name="adamw_small")
    for n, (g, d, nm, nv) in zip(plain, res, strict=True):
        out_g[n], out_d[n], out_m[n], out_v[n] = g, d, nm, nv
    as_gip = lambda n, a: gip(a) if n.endswith(('b_re', 'b_im')) else a
    g_gip = {n: small_all[starts[n]:starts[n] + S5_GROUPS].reshape(S5_GROUPS, S5_GROUP, S5_STATE) for n in wide}
    res = adamw_many([(as_gip(n, w[n]), g_gip[n], as_gip(n, mom_m[n]), as_gip(n, mom_v[n])) for n in wide],
                     name="adamw_s5_tensors")
    for n, (d, nm, nv) in zip(wide, res, strict=True):
        out_g[n], out_d[n], out_m[n], out_v[n] = as_gip(n, g_gip[n]), as_gip(n, d), as_gip(n, nm), as_gip(n, nv)

    return (loss, grad_x, *[out_g[n] for n in WEIGHTS], *[out_d[n] for n in WEIGHTS],
            *[out_m[n] for n in WEIGHTS], *[out_v[n] for n in WEIGHTS])
```

```python
import functools

import jax
import jax.numpy as jnp
from jax import lax
from jax.experimental import pallas as pl
from jax.experimental.pallas import tpu as pltpu

F32, BF16 = jnp.float32, jnp.bfloat16
MESH = pl.DeviceIdType.MESH

D_MODEL = 1024
N_META = 16
EPS = 1e-6
S5_GROUPS, S5_GROUP, S5_STATE = 64, 16, 64
LANE_BLOCK = 128
S5_BLOCKS = D_MODEL // LANE_BLOCK
GROUPS_PER_BLOCK = LANE_BLOCK // S5_GROUP
STATE_BLOCK = GROUPS_PER_BLOCK * S5_STATE
SCAN_ROWS = 256
SCAN_CHUNK = 1088
SUBLANES = 8
CONV_E = 2048
POOL_E = 2048
POOL_WINDOWS = (2, 4, 8, 16)
POOL_GROUP = 512
POOL_HALO = 16
N_CHIPS = 4

ADAM_LR, ADAM_B1, ADAM_B2, ADAM_EPS, ADAM_WD, ADAM_STEP = 0.001, 0.9, 0.999, 1e-08, 0.01, 10

VMEM_LIMIT = 48 * 1024 * 1024
TN_OPERAND_BYTES = 28 * 1024 * 1024

WEIGHTS = ['meta_tokens', 'norm0_g', 'l0_w_in', 'l0_lam_re', 'l0_lam_im', 'l0_log_dt', 'l0_b_re', 'l0_b_im',
           'l0_c_re', 'l0_c_im', 'l0_d_skip', 'l0_w_glu', 'l0_b_glu', 'l0_w_out', 'norm1_g', 'l1_w_in',
           'l1_conv_w', 'l1_conv_b', 'l1_w_out', 'norm2_g', 'l2_w_in', 'l2_w_grp', 'l2_b_grp', 'l2_scale',
           'l2_w_out', 'norm3_g', 'l3_w_in', 'l3_lam_re', 'l3_lam_im', 'l3_log_dt', 'l3_b_re', 'l3_b_im',
           'l3_c_re', 'l3_c_im', 'l3_d_skip', 'l3_w_glu', 'l3_b_glu', 'l3_w_out', 'final_g']
BIG = ['l0_w_in', 'l0_w_glu', 'l0_w_out', 'l1_w_in', 'l1_w_out', 'l2_w_in', 'l2_w_grp', 'l2_w_out',
       'l3_w_in', 'l3_w_glu', 'l3_w_out']
SMALL_SHARDED = ['meta_tokens', 'l1_conv_w', 'l2_b_grp']
SMALL = [n for n in WEIGHTS if n not in BIG]


def _params(*sem):
    return pltpu.CompilerParams(dimension_semantics=sem, vmem_limit_bytes=VMEM_LIMIT)


class Job:
    def __init__(self, arrays, out_shapes, sems, fn, in_place=False):
        self.arrays, self.out_shapes, self.sems, self.fn = list(arrays), list(out_shapes), list(sems), fn
        self.in_place = in_place
        assert len(self.arrays) == len(self.out_shapes)

    @staticmethod
    def both(first, second):
        na, ns = len(first.arrays), len(first.sems)

        def fn(ins, outs, sems, phase):
            first.fn(ins[:na], outs[:na], sems[:ns], phase)
            second.fn(ins[na:], outs[na:], sems[ns:], phase)

        job = Job(first.arrays + second.arrays, first.out_shapes + second.out_shapes, first.sems + second.sems, fn)
        job.in_place = [first.in_place] * na + [second.in_place] * len(second.arrays)
        return job

    def aliased(self):
        flags = self.in_place if isinstance(self.in_place, list) else [self.in_place] * len(self.arrays)
        return [a for a, flag in enumerate(flags) if flag]


def _call(body, *, name, grid, in_specs, out_specs, out_shape, args, semantics, scratch=(), aliases=None, job=None):
    if job is None:
        return pl.pallas_call(
            body, name=name, grid=grid, in_specs=list(in_specs), out_specs=list(out_specs),
            out_shape=list(out_shape), scratch_shapes=list(scratch), input_output_aliases=aliases or {},
            compiler_params=_params(*semantics))(*args)
    counts = [len(in_specs), len(job.arrays), len(out_specs), len(job.out_shapes), len(scratch), len(job.sems)]

    def hosted(*refs):
        parts, pos = [], 0
        for k in counts:
            parts.append(refs[pos:pos + k])
            pos += k
        ins, job_ins, outs, job_outs, scr, job_sems = parts
        steps = [pl.program_id(d) for d in range(len(grid))]
        first = functools.reduce(jnp.logical_and, [s == 0 for s in steps])
        last = functools.reduce(jnp.logical_and, [s == g - 1 for s, g in zip(steps, grid, strict=True)])

        @pl.when(first)
        def _():
            job.fn(job_ins, job_outs, job_sems, "start")

        body(*ins, *outs, *scr)

        @pl.when(last)
        def _():
            job.fn(job_ins, job_outs, job_sems, "finish")

    any_spec = pl.BlockSpec(memory_space=pl.ANY)
    aliases = dict(aliases or {})
    aliases.update({len(in_specs) + a: len(out_specs) + a for a in job.aliased()})
    res = pl.pallas_call(
        hosted, name=name, grid=grid, in_specs=list(in_specs) + [any_spec] * len(job.arrays),
        out_specs=list(out_specs) + [any_spec] * len(job.out_shapes),
        out_shape=list(out_shape) + job.out_shapes, scratch_shapes=list(scratch) + job.sems,
        input_output_aliases=aliases,
        compiler_params=_params(*["arbitrary"] * len(grid)))(*args, *job.arrays)
    return res[:len(out_specs)], res[len(out_specs):]


def _tile(n, cap, mult):
    best = None
    for t in range(mult, min(n, cap) + 1, mult):
        if n % t == 0:
            best = t
    assert best is not None, (n, cap, mult)
    return best


def _with_job(res, job):
    return res[0] if job is None else (res[0][0], res[1])


def mm_nn(a, b, *, name, bias=None, add=None, out_dtype=F32, job=None):
    m, k = a.shape
    s, _, ns = b.shape
    n = s * ns
    tm, tn = _tile(m, 1088, 16), _tile(ns, 1024, 128)
    per = ns // tn

    def body(*refs):
        a_ref, b_ref = refs[0], refs[1]
        o_ref = refs[-1]
        acc = jnp.dot(a_ref[...].astype(BF16), b_ref[...], preferred_element_type=F32)
        pos = 2
        if bias is not None:
            acc = acc + refs[pos][...]
            pos += 1
        if add is not None:
            acc = acc + refs[pos][...]
        o_ref[...] = acc.astype(o_ref.dtype)

    in_specs = [pl.BlockSpec((tm, k), lambda i, j: (i, 0)),
                pl.BlockSpec((None, k, tn), lambda i, j: (j // per, 0, j % per))]
    args = [a, b]
    if bias is not None:
        in_specs.append(pl.BlockSpec((1, tn), lambda i, j: (0, j)))
        args.append(bias)
    if add is not None:
        in_specs.append(pl.BlockSpec((tm, tn), lambda i, j: (i, j)))
        args.append(add)
    return _with_job(_call(
        body, name=name, grid=(m // tm, n // tn), in_specs=in_specs,
        out_specs=[pl.BlockSpec((tm, tn), lambda i, j: (i, j))],
        out_shape=[jax.ShapeDtypeStruct((m, n), out_dtype)], args=args,
        semantics=("parallel", "parallel"), job=job), job)


def norm_mm_nn(h, gain, b, *, name, job=None):
    m, k = h.shape
    s, _, ns = b.shape
    n = s * ns
    tm, tn = _tile(m, 1088, 16), _tile(ns, 1024, 128)
    per = ns // tn

    def body(h_ref, g_ref, b_ref, o_ref, n_ref):
        @pl.when(pl.program_id(1) == 0)
        def _():
            n_ref[...] = _rms(h_ref[...], g_ref[...]).astype(n_ref.dtype)

        o_ref[...] = jnp.dot(n_ref[...], b_ref[...], preferred_element_type=F32)

    res = _call(
        body, name=name, grid=(m // tm, n // tn),
        in_specs=[pl.BlockSpec((tm, k), lambda i, j: (i, 0)), pl.BlockSpec((1, k), lambda i, j: (0, 0)),
                  pl.BlockSpec((None, k, tn), lambda i, j: (j // per, 0, j % per))],
        out_specs=[pl.BlockSpec((tm, tn), lambda i, j: (i, j)), pl.BlockSpec((tm, k), lambda i, j: (i, 0))],
        out_shape=[jax.ShapeDtypeStruct((m, n), F32), jax.ShapeDtypeStruct((m, k), BF16)], args=(h, gain, b),
        semantics=("parallel", "arbitrary"), job=job)
    return (res[0], res[1]) if job is None else ((res[0][0], res[0][1]), res[1])


def glu_gate(yg, w_glu, b_glu, y0, uz, *, name, job=None):
    m, k = yg.shape
    n = w_glu.shape[2]
    tm = _tile(m, 1088, 16)

    def body(a_ref, w_ref, b_ref, y0_ref, z_ref, q_ref, o_ref):
        q = jnp.dot(a_ref[...], w_ref[...], preferred_element_type=F32) + b_ref[...]
        q_ref[...] = q
        o_ref[...] = _s5_gate(y0_ref[...], q, z_ref[...]).astype(o_ref.dtype)

    tile = pl.BlockSpec((tm, n), lambda i: (i, 0))
    res = _call(
        body, name=name, grid=(m // tm,),
        in_specs=[pl.BlockSpec((tm, k), lambda i: (i, 0)), pl.BlockSpec((None, k, n), lambda i: (0, 0, 0)),
                  pl.BlockSpec((1, n), lambda i: (0, 0)), tile, pl.BlockSpec((tm, n), lambda i: (i, 1))],
        out_specs=[tile, tile],
        out_shape=[jax.ShapeDtypeStruct((m, n), F32), jax.ShapeDtypeStruct((m, n), BF16)],
        args=(yg, w_glu, b_glu, y0, uz), semantics=("parallel",), job=job)
    return (res[0], res[1]) if job is None else ((res[0][0], res[0][1]), res[1])


def s5_gate_bwd(dh, w_out, y0, q, uz, *, name):
    m, d = dh.shape
    e = w_out.shape[1]
    tm = _tile(m, 544, 16)

    def body(dh_ref, w_ref, y0_ref, q_ref, z_ref, dy0_ref, dq_ref, dz_ref, db_ref):
        i = pl.program_id(0)
        dy3 = lax.dot_general(dh_ref[...].astype(BF16), w_ref[...], (((1,), (1,)), ((), ())),
                              preferred_element_type=F32)
        dy0, dq, dz = jax.vjp(_s5_gate, y0_ref[...], q_ref[...], z_ref[...])[1](dy3)
        dy0_ref[...] = dy0
        dq_ref[...] = dq.astype(dq_ref.dtype)
        dz_ref[...] = dz.astype(dz_ref.dtype)
        db = _colsum(dq)

        @pl.when(i == 0)
        def _():
            db_ref[...] = db

        @pl.when(i > 0)
        def _():
            db_ref[...] += db

    tile = pl.BlockSpec((tm, e), lambda i: (i, 0))
    right = pl.BlockSpec((tm, e), lambda i: (i, 1))
    return pl.pallas_call(
        body, name=name, grid=(m // tm,),
        in_specs=[pl.BlockSpec((tm, d), lambda i: (i, 0)), pl.BlockSpec((None, e, d), lambda i: (0, 0, 0)),
                  tile, tile, right],
        out_specs=[tile, tile, right, pl.BlockSpec((1, e), lambda i: (0, 0))],
        out_shape=[jax.ShapeDtypeStruct((m, e), F32), jax.ShapeDtypeStruct((m, e), BF16),
                   jax.ShapeDtypeStruct((m, 2 * e), BF16), jax.ShapeDtypeStruct((1, e), F32)],
        compiler_params=_params("arbitrary"))(dh, w_out, y0, q, uz)


def pool_gate_bwd(dh, w_out, o, p, scale, *, name):
    m, d = dh.shape
    e = w_out.shape[1]
    tm, te = _tile(m, 1088, 16), 1024
    nj = e // te

    def body(dh_ref, w_ref, o_ref, z_ref, s_ref, do_ref, dz_ref, ds_ref, db_ref):
        i = pl.program_id(1)
        dy3 = lax.dot_general(dh_ref[...].astype(BF16), w_ref[...], (((1,), (1,)), ((), ())),
                              preferred_element_type=F32)
        do, dz, ds = jax.vjp(_pool_gate, o_ref[...], z_ref[...], s_ref[...])[1](dy3)
        do_ref[...] = do.astype(do_ref.dtype)
        dz_ref[...] = dz.astype(dz_ref.dtype)
        db = _colsum(do)

        @pl.when(i == 0)
        def _():
            ds_ref[...] = ds
            db_ref[...] = db

        @pl.when(i > 0)
        def _():
            ds_ref[...] += ds
            db_ref[...] += db

    tile = pl.BlockSpec((tm, te), lambda j, i: (i, j))
    right = pl.BlockSpec((tm, te), lambda j, i: (i, nj + j))
    vec = pl.BlockSpec((1, te), lambda j, i: (0, j))
    return pl.pallas_call(
        body, name=name, grid=(nj, m // tm),
        in_specs=[pl.BlockSpec((tm, d), lambda j, i: (i, 0)), pl.BlockSpec((None, te, d), lambda j, i: (0, j, 0)),
                  tile, right, vec],
        out_specs=[tile, right, vec, vec],
        out_shape=[jax.ShapeDtypeStruct((m, e), BF16), jax.ShapeDtypeStruct((m, 2 * e), BF16),
                   jax.ShapeDtypeStruct((1, e), F32), jax.ShapeDtypeStruct((1, e), F32)],
        compiler_params=_params("parallel", "arbitrary"))(dh, w_out, o, p, scale)


def mm_nt_norm_bwd(g, w, h, gain, dh_out, *, name, job=None):
    m, kc = g.shape
    s, nout, kcs = w.shape
    assert s * kcs == kc
    tm, tk = _tile(m, 1088, 16), _tile(kcs, 1024, 128)
    per = kcs // tk
    nk = kc // tk

    def body(g_ref, w_ref, h_ref, gain_ref, dh_ref, o_ref, dg_ref):
        i, kk = pl.program_id(0), pl.program_id(1)
        part = lax.dot_general(g_ref[...].astype(BF16), w_ref[...], (((1,), (1,)), ((), ())),
                               preferred_element_type=F32)

        @pl.when(kk == 0)
        def _():
            o_ref[...] = part

        @pl.when(kk > 0)
        def _():
            o_ref[...] += part

        @pl.when(kk == nk - 1)
        def _():
            dh, dg = jax.vjp(_rms, h_ref[...], gain_ref[...])[1](o_ref[...])
            o_ref[...] = dh_ref[...] + dh

            @pl.when(i == 0)
            def _():
                dg_ref[...] = dg

            @pl.when(i > 0)
            def _():
                dg_ref[...] += dg

    row_tile = pl.BlockSpec((tm, nout), lambda i, kk: (i, 0))
    res = _call(
        body, name=name, grid=(m // tm, nk),
        in_specs=[pl.BlockSpec((tm, tk), lambda i, kk: (i, kk)),
                  pl.BlockSpec((None, nout, tk), lambda i, kk: (kk // per, 0, kk % per)),
                  row_tile, pl.BlockSpec((1, nout), lambda i, kk: (0, 0)), row_tile],
        out_specs=[row_tile, pl.BlockSpec((1, nout), lambda i, kk: (0, 0))],
        out_shape=[jax.ShapeDtypeStruct((m, nout), F32), jax.ShapeDtypeStruct((1, nout), F32)],
        args=(g, w, h, gain, dh_out), semantics=("arbitrary", "arbitrary"), job=job)
    return (res[0], res[1]) if job is None else ((res[0][0], res[0][1]), res[1])


def mm_nt(g, w, *, name, job=None):
    m, kc = g.shape
    s, nout, kcs = w.shape
    assert s * kcs == kc
    tm, tno, tk = _tile(m, 2176, 16), _tile(nout, 1024, 128), _tile(kcs, 1024, 128)
    per = kcs // tk

    def body(g_ref, w_ref, o_ref):
        kk = pl.program_id(2)
        part = lax.dot_general(g_ref[...].astype(BF16), w_ref[...], (((1,), (1,)), ((), ())),
                               preferred_element_type=F32)

        @pl.when(kk == 0)
        def _():
            o_ref[...] = part

        @pl.when(kk > 0)
        def _():
            o_ref[...] += part

    return _with_job(_call(
        body, name=name, grid=(m // tm, nout // tno, kc // tk),
        in_specs=[pl.BlockSpec((tm, tk), lambda i, j, kk: (i, kk)),
                  pl.BlockSpec((None, tno, tk), lambda i, j, kk: (kk // per, j, kk % per))],
        out_specs=[pl.BlockSpec((tm, tno), lambda i, j, kk: (i, j))],
        out_shape=[jax.ShapeDtypeStruct((m, nout), F32)], args=(g, w),
        semantics=("parallel", "parallel", "arbitrary"), job=job), job)


def mm_tn(a, g, shards, *, name, job=None):
    m, ka = a.shape
    n = g.shape[1]
    ns = n // shards
    tka, tn = _tile(ka, 1024, 128), _tile(ns, 512, 128)
    row_bytes = tka * jnp.dtype(a.dtype).itemsize + tn * jnp.dtype(g.dtype).itemsize
    tm = _tile(m, TN_OPERAND_BYTES // (2 * row_bytes), 16)
    per = ns // tn

    def body(a_ref, g_ref, o_ref):
        mm = pl.program_id(2)
        part = lax.dot_general(a_ref[...].astype(BF16), g_ref[...].astype(BF16), (((0,), (0,)), ((), ())),
                               preferred_element_type=F32)

        @pl.when(mm == 0)
        def _():
            o_ref[...] = part

        @pl.when(mm > 0)
        def _():
            o_ref[...] += part

    return _with_job(_call(
        body, name=name, grid=(ka // tka, n // tn, m // tm),
        in_specs=[pl.BlockSpec((tm, tka), lambda i, j, mm: (mm, i)),
                  pl.BlockSpec((tm, tn), lambda i, j, mm: (mm, j))],
        out_specs=[pl.BlockSpec((None, tka, tn), lambda i, j, mm: (j // per, i, j % per))],
        out_shape=[jax.ShapeDtypeStruct((shards, ka, ns), F32)], args=(a, g),
        semantics=("parallel", "parallel", "arbitrary"), job=job), job)


def grp_nn_gate(a, w, bias, p, scale, *, name):
    m = a.shape[0]
    tm = _tile(m, 1088, 16)
    ng = len(POOL_WINDOWS)

    def body(a_ref, w_ref, b_ref, z_ref, s_ref, o_ref, y_ref):
        wj = w_ref[...].reshape(POOL_GROUP, POOL_GROUP)
        o = jnp.dot(a_ref[...].astype(BF16), wj, preferred_element_type=F32) + b_ref[...]
        o_ref[...] = o
        y_ref[...] = _pool_gate(o, z_ref[...], s_ref[...]).astype(y_ref.dtype)

    tile = pl.BlockSpec((tm, POOL_GROUP), lambda i, j: (i, j))
    vec = pl.BlockSpec((1, POOL_GROUP), lambda i, j: (0, j))
    return pl.pallas_call(
        body, name=name, grid=(m // tm, ng),
        in_specs=[tile,
                  pl.BlockSpec((N_CHIPS, None, POOL_GROUP // N_CHIPS, POOL_GROUP), lambda i, j: (0, j, 0, 0)),
                  vec, pl.BlockSpec((tm, POOL_GROUP), lambda i, j: (i, ng + j)), vec],
        out_specs=[tile, tile],
        out_shape=[jax.ShapeDtypeStruct((m, ng * POOL_GROUP), F32), jax.ShapeDtypeStruct((m, ng * POOL_GROUP), BF16)],
        compiler_params=_params("parallel", "parallel"))(a, w, bias, p, scale)


def grp_nt(g, w, *, name):
    m = g.shape[0]
    tm = _tile(m, 1088, 16)
    ng = len(POOL_WINDOWS)

    def body(g_ref, w_ref, o_ref):
        wj = w_ref[...].reshape(POOL_GROUP, POOL_GROUP)
        o_ref[...] = lax.dot_general(g_ref[...].astype(BF16), wj, (((1,), (1,)), ((), ())),
                                     preferred_element_type=F32)

    return pl.pallas_call(
        body, name=name, grid=(m // tm, ng),
        in_specs=[pl.BlockSpec((tm, POOL_GROUP), lambda i, j: (i, j)),
                  pl.BlockSpec((N_CHIPS, None, POOL_GROUP // N_CHIPS, POOL_GROUP), lambda i, j: (0, j, 0, 0))],
        out_specs=pl.BlockSpec((tm, POOL_GROUP), lambda i, j: (i, j)),
        out_shape=jax.ShapeDtypeStruct((m, ng * POOL_GROUP), F32),
        compiler_params=_params("parallel", "parallel"))(g, w)


def grp_tn(a, g, *, name):
    m = a.shape[0]
    tm = _tile(m, 1088, 16)
    ng = len(POOL_WINDOWS)
    rows = POOL_GROUP // N_CHIPS

    def body(a_ref, g_ref, o_ref):
        mm = pl.program_id(1)
        part = lax.dot_general(a_ref[...].astype(BF16), g_ref[...].astype(BF16), (((0,), (0,)), ((), ())),
                               preferred_element_type=F32).reshape(N_CHIPS, rows, POOL_GROUP)

        @pl.when(mm == 0)
        def _():
            o_ref[...] = part

        @pl.when(mm > 0)
        def _():
            o_ref[...] += part

    return pl.pallas_call(
        body, name=name, grid=(ng, m // tm),
        in_specs=[pl.BlockSpec((tm, POOL_GROUP), lambda j, mm: (mm, j)),
                  pl.BlockSpec((tm, POOL_GROUP), lambda j, mm: (mm, j))],
        out_specs=pl.BlockSpec((N_CHIPS, None, rows, POOL_GROUP), lambda j, mm: (0, j, 0, 0)),
        out_shape=jax.ShapeDtypeStruct((N_CHIPS, ng, rows, POOL_GROUP), F32),
        compiler_params=_params("parallel", "arbitrary"))(a, g)


def final_loss(h, target, gain, first, last, *, name):
    m, d = h.shape
    tr = m // 16

    def body(h_ref, t_ref, g_ref, dh_ref, dg_ref, sq_ref):
        i = pl.program_id(0)
        y, vjp = jax.vjp(_rms, h_ref[...], g_ref[...])
        pos = i * tr + lax.broadcasted_iota(jnp.int32, (tr, 1), 0)
        diff = jnp.where((pos >= first) & (pos < last), y - t_ref[...], 0.0)
        dh_ref[...], dg = vjp(diff / d)
        sq = _colsum(diff * diff)

        @pl.when(i == 0)
        def _():
            dg_ref[...] = dg
            sq_ref[...] = sq

        @pl.when(i > 0)
        def _():
            dg_ref[...] += dg
            sq_ref[...] += sq

    tile, vec = pl.BlockSpec((tr, d), lambda i: (i, 0)), pl.BlockSpec((1, d), lambda i: (0, 0))
    return pl.pallas_call(
        body, name=name, grid=(m // tr,), in_specs=[tile, tile, vec], out_specs=[tile, vec, vec],
        out_shape=[jax.ShapeDtypeStruct((m, d), F32), jax.ShapeDtypeStruct((1, d), F32),
                   jax.ShapeDtypeStruct((1, d), F32)],
        compiler_params=_params("arbitrary"))(h, target, gain)


def _rms(h, g):
    return h * lax.rsqrt(jnp.mean(h * h, axis=-1, keepdims=True) + EPS) * g


def _colsum(v):
    return jnp.sum(v, axis=0, keepdims=True)


def _s5_gate(y0, q, z):
    yg = jax.nn.gelu(y0)
    return yg * jax.nn.sigmoid(q) * jax.nn.silu(z)


def _pool_gate(o, z, scale):
    return o * scale * jax.nn.silu(z)


def _shift_rows(v, d, down):
    t = v.shape[0]
    row = lax.broadcasted_iota(jnp.int32, v.shape, 0)
    if down:
        return jnp.where(row >= d, pltpu.roll(v, d, 0), 0.0)
    return jnp.where(row < t - d, pltpu.roll(v, t - d, 0), 0.0)


def _cmul(ar, ai, br, bi):
    return ar * br - ai * bi, ar * bi + ai * br


def _scan(xr, xi, ar, ai, cr, ci, down):
    t, n = xr.shape
    nb = t // SUBLANES
    pw = [(ar, ai)]
    for _ in range(SUBLANES - 1):
        pw.append(_cmul(*pw[-1], ar, ai))
    sub = lax.broadcasted_iota(jnp.int32, (SUBLANES, n), 0)

    def table(exps):
        tr, ti = jnp.zeros((SUBLANES, n), F32), jnp.zeros((SUBLANES, n), F32)
        for r, e in enumerate(exps):
            if e:
                tr, ti = jnp.where(sub == r, pw[e - 1][0], tr), jnp.where(sub == r, pw[e - 1][1], ti)
        return tr[None], ti[None]

    sr, si = xr.reshape(nb, SUBLANES, n), xi.reshape(nb, SUBLANES, n)
    d = 1
    while d < SUBLANES:
        mr, mi = table([d if (r >= d if down else r < SUBLANES - d) else 0 for r in range(SUBLANES)])
        turn = d if down else SUBLANES - d
        hr, hi = pltpu.roll(sr, turn, 1), pltpu.roll(si, turn, 1)
        sr, si = sr + mr * hr - mi * hi, si + mr * hi + mi * hr
        d *= 2
    edge = SUBLANES - 1 if down else 0
    qr, qi = table([r + 1 if down else SUBLANES - r for r in range(SUBLANES)])
    qr, qi = qr[0], qi[0]
    in_r, in_i = jnp.broadcast_to(cr, (SUBLANES, n)), jnp.broadcast_to(ci, (SUBLANES, n))
    out_r, out_i = [None] * nb, [None] * nb
    for b in (range(nb) if down else reversed(range(nb))):
        out_r[b] = sr[b] + qr * in_r - qi * in_i
        out_i[b] = si[b] + qr * in_i + qi * in_r
        in_r = jnp.broadcast_to(out_r[b][edge:edge + 1, :], (SUBLANES, n))
        in_i = jnp.broadcast_to(out_i[b][edge:edge + 1, :], (SUBLANES, n))
    return jnp.concatenate(out_r, axis=0), jnp.concatenate(out_i, axis=0), in_r[0:1, :], in_i[0:1, :]


def s5_fwd(uz, bblk, cblk, ar, ai, dskip, *, name, job=None):
    lp = uz.shape[0]
    t = _tile(lp, SCAN_CHUNK, 16)
    nst = S5_GROUPS * S5_STATE

    def body(u_ref, b_ref, c_ref, ar_ref, ai_ref, d_ref, y_ref, yg_ref, sr_ref, si_ref, cr, ci):
        step = pl.program_id(1)

        @pl.when(step == 0)
        def _():
            cr[...] = jnp.zeros_like(cr)
            ci[...] = jnp.zeros_like(ci)

        u = u_ref[...]
        a_r, a_i = ar_ref[...], ai_ref[...]
        x = jnp.dot(u.astype(BF16), b_ref[...].astype(BF16), preferred_element_type=F32)
        sr, si, cr[...], ci[...] = _scan(x[:, :STATE_BLOCK], x[:, STATE_BLOCK:], a_r, a_i, cr[...], ci[...], True)
        sr_ref[...] = sr
        si_ref[...] = si
        s = jnp.concatenate([sr, si], axis=1).astype(BF16)
        y0 = lax.dot_general(s, c_ref[...].astype(BF16), (((1,), (1,)), ((), ())),
                             preferred_element_type=F32) + d_ref[...] * u
        y_ref[...] = y0
        yg_ref[...] = jax.nn.gelu(y0).astype(yg_ref.dtype)

    return _call(
        body, name=name, grid=(S5_BLOCKS, lp // t),
        in_specs=[pl.BlockSpec((t, LANE_BLOCK), lambda b, c: (c, b)),
                  pl.BlockSpec((None, LANE_BLOCK, 2 * STATE_BLOCK), lambda b, c: (b, 0, 0)),
                  pl.BlockSpec((None, LANE_BLOCK, 2 * STATE_BLOCK), lambda b, c: (b, 0, 0)),
                  pl.BlockSpec((1, STATE_BLOCK), lambda b, c: (0, b)),
                  pl.BlockSpec((1, STATE_BLOCK), lambda b, c: (0, b)),
                  pl.BlockSpec((1, LANE_BLOCK), lambda b, c: (0, b))],
        out_specs=[pl.BlockSpec((t, LANE_BLOCK), lambda b, c: (c, b)),
                   pl.BlockSpec((t, LANE_BLOCK), lambda b, c: (c, b)),
                   pl.BlockSpec((t, STATE_BLOCK), lambda b, c: (c, b)),
                   pl.BlockSpec((t, STATE_BLOCK), lambda b, c: (c, b))],
        out_shape=[jax.ShapeDtypeStruct((lp, D_MODEL), F32), jax.ShapeDtypeStruct((lp, D_MODEL), BF16),
                   jax.ShapeDtypeStruct((lp, nst), F32), jax.ShapeDtypeStruct((lp, nst), F32)],
        scratch=[pltpu.VMEM((1, STATE_BLOCK), F32), pltpu.VMEM((1, STATE_BLOCK), F32)],
        args=(uz, bblk, cblk, ar, ai, dskip), semantics=("parallel", "arbitrary"), job=job)


def s5_bwd(dy_direct, dyg, y0, uz, sr, si, bblk, cblk, ar, ai, dskip, dp, *, name, job=None):
    lp = uz.shape[0]
    t = _tile(lp, SCAN_CHUNK, 16)
    nch = lp // t
    nst = S5_GROUPS * S5_STATE

    def body(dyd_ref, dyg_ref, y0_ref, u_ref, sr_ref, si_ref, b_ref, c_ref, ar_ref, ai_ref, d_ref, _, du_ref,
             db_ref, dc_ref, dar_ref, dai_ref, dd_ref, cr, ci):
        step = pl.program_id(1)

        @pl.when(step == 0)
        def _():
            cr[...] = jnp.zeros_like(cr)
            ci[...] = jnp.zeros_like(ci)

        dy = dyd_ref[...] + jax.vjp(jax.nn.gelu, y0_ref[...])[1](dyg_ref[...])[0]
        u = u_ref[...]
        dyb, ub = dy.astype(BF16), u.astype(BF16)
        a_r, a_i = ar_ref[...], -ai_ref[...]
        g = jnp.dot(dyb, c_ref[...].astype(BF16), preferred_element_type=F32)
        gr, gi = g[:, :STATE_BLOCK], g[:, STATE_BLOCK:]
        nxt_r, nxt_i = cr[...], ci[...]
        last = lax.broadcasted_iota(jnp.int32, gr.shape, 0) == t - 1
        lr, li, cr[...], ci[...] = _scan(gr, gi, a_r, a_i, nxt_r, nxt_i, False)
        nr = _shift_rows(lr, 1, False) + jnp.where(last, nxt_r, 0.0)
        ni = _shift_rows(li, 1, False) + jnp.where(last, nxt_i, 0.0)
        s_r, s_i = sr_ref[...], si_ref[...]
        dar = _colsum(s_r * nr + s_i * ni)
        dai = _colsum(s_r * ni - s_i * nr)
        lam = jnp.concatenate([lr, li], axis=1).astype(BF16)
        sb = jnp.concatenate([s_r, s_i], axis=1).astype(BF16)
        dc = lax.dot_general(dyb, sb, (((0,), (0,)), ((), ())), preferred_element_type=F32)
        db = lax.dot_general(ub, lam, (((0,), (0,)), ((), ())), preferred_element_type=F32)
        du_ref[...] = (lax.dot_general(lam, b_ref[...].astype(BF16), (((1,), (1,)), ((), ())),
                                       preferred_element_type=F32) + d_ref[...] * dy).astype(du_ref.dtype)
        dd = _colsum(dy * u)

        @pl.when(step == 0)
        def _():
            db_ref[...] = db
            dc_ref[...] = dc
            dar_ref[...] = dar
            dai_ref[...] = dai
            dd_ref[...] = dd

        @pl.when(step > 0)
        def _():
            db_ref[...] += db
            dc_ref[...] += dc
            dar_ref[...] += dar
            dai_ref[...] += dai
            dd_ref[...] += dd

    rev = lambda b, c: (nch - 1 - c, b)
    return _call(
        body, name=name, grid=(S5_BLOCKS, nch),
        in_specs=[pl.BlockSpec((t, LANE_BLOCK), rev), pl.BlockSpec((t, LANE_BLOCK), rev),
                  pl.BlockSpec((t, LANE_BLOCK), rev), pl.BlockSpec((t, LANE_BLOCK), rev),
                  pl.BlockSpec((t, STATE_BLOCK), rev), pl.BlockSpec((t, STATE_BLOCK), rev),
                  pl.BlockSpec((None, LANE_BLOCK, 2 * STATE_BLOCK), lambda b, c: (b, 0, 0)),
                  pl.BlockSpec((None, LANE_BLOCK, 2 * STATE_BLOCK), lambda b, c: (b, 0, 0)),
                  pl.BlockSpec((1, STATE_BLOCK), lambda b, c: (0, b)),
                  pl.BlockSpec((1, STATE_BLOCK), lambda b, c: (0, b)),
                  pl.BlockSpec((1, LANE_BLOCK), lambda b, c: (0, b)),
                  pl.BlockSpec(memory_space=pl.ANY)],
        out_specs=[pl.BlockSpec((t, LANE_BLOCK), rev),
                   pl.BlockSpec((None, LANE_BLOCK, 2 * STATE_BLOCK), lambda b, c: (b, 0, 0)),
                   pl.BlockSpec((None, LANE_BLOCK, 2 * STATE_BLOCK), lambda b, c: (b, 0, 0)),
                   pl.BlockSpec((1, STATE_BLOCK), lambda b, c: (0, b)),
                   pl.BlockSpec((1, STATE_BLOCK), lambda b, c: (0, b)),
                   pl.BlockSpec((1, LANE_BLOCK), lambda b, c: (0, b))],
        out_shape=[jax.ShapeDtypeStruct(dp.shape, dp.dtype),
                   jax.ShapeDtypeStruct((S5_BLOCKS, LANE_BLOCK, 2 * STATE_BLOCK), F32),
                   jax.ShapeDtypeStruct((S5_BLOCKS, LANE_BLOCK, 2 * STATE_BLOCK), F32),
                   jax.ShapeDtypeStruct((1, nst), F32), jax.ShapeDtypeStruct((1, nst), F32),
                   jax.ShapeDtypeStruct((1, D_MODEL), F32)],
        scratch=[pltpu.VMEM((1, STATE_BLOCK), F32), pltpu.VMEM((1, STATE_BLOCK), F32)],
        aliases={11: 0}, args=(dy_direct, dyg, y0, uz, sr, si, bblk, cblk, ar, ai, dskip, dp),
        semantics=("parallel", "arbitrary"), job=job)


def _s5_prep(lam_re, lam_im, log_dt, b_re, b_im, c_re, c_im):
    dt = jnp.exp(log_dt)
    mag = jnp.exp(lam_re * dt)
    ar = mag * jnp.cos(lam_im * dt)
    ai = mag * jnp.sin(lam_im * dt)
    den = lam_re * lam_re + lam_im * lam_im
    kr = ((ar - 1.0) * lam_re + ai * lam_im) / den
    ki = (ai * lam_re - (ar - 1.0) * lam_im) / den
    bbr = kr[:, None, :] * b_re - ki[:, None, :] * b_im
    bbi = kr[:, None, :] * b_im + ki[:, None, :] * b_re
    rows = S5_GROUPS * S5_GROUP
    expand = (lax.broadcasted_iota(jnp.int32, (S5_STATE, STATE_BLOCK), 1) % S5_STATE
              == lax.broadcasted_iota(jnp.int32, (S5_STATE, STATE_BLOCK), 0)).astype(F32)
    own = ((lax.broadcasted_iota(jnp.int32, (rows, STATE_BLOCK), 0) // S5_GROUP) % GROUPS_PER_BLOCK
           == lax.broadcasted_iota(jnp.int32, (rows, STATE_BLOCK), 1) // S5_STATE).astype(F32)

    def blocks(v):
        wide = jnp.dot(v.reshape(rows, S5_STATE), expand, precision=lax.Precision.HIGHEST,
                       preferred_element_type=F32)
        return (wide * own).reshape(S5_BLOCKS, LANE_BLOCK, STATE_BLOCK)

    btab = jnp.concatenate([blocks(bbr), blocks(bbi)], axis=2)
    ctab = jnp.concatenate([blocks(c_re), -blocks(c_im)], axis=2)
    return ar, ai, btab, ctab


_S5_PREP_OUT = [(S5_GROUPS, S5_STATE), (S5_GROUPS, S5_STATE), (S5_BLOCKS, LANE_BLOCK, 2 * STATE_BLOCK),
                (S5_BLOCKS, LANE_BLOCK, 2 * STATE_BLOCK)]


def s5_prep(params, *, name, job=None):
    def body(*refs):
        for ref, val in zip(refs[7:], _s5_prep(*[r[...] for r in refs[:7]]), strict=True):
            ref[...] = val

    whole = lambda shape: pl.BlockSpec(shape, lambda i, nd=len(shape): (0,) * nd)
    return _call(body, name=name, grid=(1,), in_specs=[whole(p.shape) for p in params],
                 out_specs=[whole(s) for s in _S5_PREP_OUT],
                 out_shape=[jax.ShapeDtypeStruct(s, F32) for s in _S5_PREP_OUT], args=list(params),
                 semantics=("arbitrary",), job=job)


def s5_prep_bwd(params, cotangents, *, name):
    def body(*refs):
        grads = jax.vjp(_s5_prep, *[r[...] for r in refs[:7]])[1](tuple(r[...] for r in refs[7:11]))
        for ref, val in zip(refs[11:], grads, strict=True):
            ref[...] = val

    return pl.pallas_call(body, name=name, out_shape=[jax.ShapeDtypeStruct(p.shape, F32) for p in params],
                          compiler_params=pltpu.CompilerParams(vmem_limit_bytes=VMEM_LIMIT))(*params, *cotangents)


def _silu_grad(z):
    s = jax.nn.sigmoid(z)
    return s * (1.0 + z * (1.0 - s))


def conv_fwd(p, conv_w, conv_b, *, name):
    lp = p.shape[0]
    tr = lp // 16
    e = CONV_E
    hb = tr // 8

    def body(p_ref, cgh_ref, vh_ref, w_ref, b_ref, o_ref):
        i = pl.program_id(0)
        bg, cg, v, z = (p_ref[:, k * e:(k + 1) * e] for k in range(4))
        hc = cg * v
        halo = jnp.where(i > 0, cgh_ref[...] * vh_ref[...], 0.0)
        ext = jnp.concatenate([halo, hc], axis=0)
        w = w_ref[...]
        conv = (w[0:1] * pltpu.roll(ext, 2, 0)[8:] + w[1:2] * pltpu.roll(ext, 1, 0)[8:] + w[2:3] * hc
                + b_ref[...])
        o_ref[...] = (bg * conv * jax.nn.silu(z)).astype(o_ref.dtype)

    prev = lambda col: (lambda i: (jnp.maximum(i * hb - 1, 0), col))
    return pl.pallas_call(
        body, name=name, grid=(lp // tr,),
        in_specs=[pl.BlockSpec((tr, 4 * e), lambda i: (i, 0)),
                  pl.BlockSpec((8, e), prev(1)), pl.BlockSpec((8, e), prev(2)),
                  pl.BlockSpec((3, e), lambda i: (0, 0)), pl.BlockSpec((1, e), lambda i: (0, 0))],
        out_specs=pl.BlockSpec((tr, e), lambda i: (i, 0)),
        out_shape=jax.ShapeDtypeStruct((lp, e), BF16),
        compiler_params=_params("parallel"))(p, p, p, conv_w, conv_b)


def conv_bwd(dh, w_out, p, conv_w, conv_b, *, name):
    lp = p.shape[0]
    tr = lp // 16
    e = CONV_E
    d = dh.shape[1]
    hb = tr // 8
    nt = lp // tr
    width = 512

    def body(dh_ref, p_ref, cgh_ref, vh_ref, dhn_ref, bgn_ref, zn_ref, wo_ref, w_ref, b_ref, dp_ref, dw_ref, db_ref):
        i = pl.program_id(0)
        rows, rows_n = dh_ref[...].astype(BF16), dhn_ref[...].astype(BF16)
        for c0 in range(0, e, width):
            cols = slice(c0, c0 + width)
            wo = wo_ref[cols, :]
            dy = lax.dot_general(rows, wo, (((1,), (1,)), ((), ())), preferred_element_type=F32)
            dy_n = lax.dot_general(rows_n, wo, (((1,), (1,)), ((), ())), preferred_element_type=F32)
            bg, cg, v, z = (p_ref[:, k * e + c0:k * e + c0 + width] for k in range(4))
            w = w_ref[:, cols]
            hc = cg * v
            halo = jnp.where(i > 0, cgh_ref[:, cols] * vh_ref[:, cols], 0.0)
            ext = jnp.concatenate([halo, hc], axis=0)
            hc2, hc1 = pltpu.roll(ext, 2, 0)[8:], pltpu.roll(ext, 1, 0)[8:]
            yc = w[0:1] * hc2 + w[1:2] * hc1 + w[2:3] * hc + b_ref[:, cols]
            sz = jax.nn.silu(z)
            dyc = dy * bg * sz
            dyc_n = jnp.where(i < nt - 1, dy_n * bgn_ref[:, cols] * jax.nn.silu(zn_ref[:, cols]), 0.0)
            ext_n = jnp.concatenate([dyc, dyc_n], axis=0)
            n_rows = tr + 8
            dhc = (w[2:3] * dyc + w[1:2] * pltpu.roll(ext_n, n_rows - 1, 0)[:tr]
                   + w[0:1] * pltpu.roll(ext_n, n_rows - 2, 0)[:tr])
            for k, val in enumerate((dy * yc * sz, dhc * v, dhc * cg, dy * bg * yc * _silu_grad(z))):
                dp_ref[:, k * e + c0:k * e + c0 + width] = val.astype(dp_ref.dtype)
            dw = jnp.concatenate([_colsum(dyc * hc2), _colsum(dyc * hc1), _colsum(dyc * hc)], axis=0)
            db = _colsum(dyc)

            @pl.when(i == 0)
            def _(cols=cols, dw=dw, db=db):
                dw_ref[:, cols] = dw
                db_ref[:, cols] = db

            @pl.when(i > 0)
            def _(cols=cols, dw=dw, db=db):
                dw_ref[:, cols] += dw
                db_ref[:, cols] += db

    prev = lambda col: (lambda i: (jnp.maximum(i * hb - 1, 0), col))
    nxt = lambda col: (lambda i: (jnp.minimum((i + 1) * hb, lp // 8 - 1), col))
    return pl.pallas_call(
        body, name=name, grid=(nt,),
        in_specs=[pl.BlockSpec((tr, d), lambda i: (i, 0)), pl.BlockSpec((tr, 4 * e), lambda i: (i, 0)),
                  pl.BlockSpec((8, e), prev(1)), pl.BlockSpec((8, e), prev(2)),
                  pl.BlockSpec((8, d), nxt(0)), pl.BlockSpec((8, e), nxt(0)), pl.BlockSpec((8, e), nxt(3)),
                  pl.BlockSpec((None, e, d), lambda i: (0, 0, 0)),
                  pl.BlockSpec((3, e), lambda i: (0, 0)), pl.BlockSpec((1, e), lambda i: (0, 0))],
        out_specs=[pl.BlockSpec((tr, 4 * e), lambda i: (i, 0)), pl.BlockSpec((3, e), lambda i: (0, 0)),
                   pl.BlockSpec((1, e), lambda i: (0, 0))],
        out_shape=[jax.ShapeDtypeStruct((lp, 4 * e), BF16), jax.ShapeDtypeStruct((3, e), F32),
                   jax.ShapeDtypeStruct((1, e), F32)],
        compiler_params=_params("arbitrary"))(dh, p, p, p, dh, p, p, w_out, conv_w, conv_b)


def _pool_counts(i, tr, rows, offset, w):
    t = (i * tr + offset + 1 + lax.broadcasted_iota(jnp.int32, (rows, 1), 0)).astype(F32)
    return jnp.minimum(t, float(w))


def pool_fwd(p, *, name):
    lp = p.shape[0]
    tr = lp // 16
    h = POOL_HALO
    hb = tr // h

    def body(u_ref, uh_ref, o_ref):
        i = pl.program_id(0)
        u = u_ref[...]
        ext = jnp.concatenate([jnp.where(i > 0, uh_ref[...], 0.0), u], axis=0)
        for k, w in enumerate(POOL_WINDOWS):
            cols = slice(k * POOL_GROUP, (k + 1) * POOL_GROUP)
            s = ext[:, cols]
            d = 1
            while d < w:
                s = s + pltpu.roll(s, d, 0)
                d *= 2
            o_ref[:, cols] = (s[h:] / _pool_counts(i, tr, tr, 0, w) - u[:, cols]).astype(o_ref.dtype)

    return pl.pallas_call(
        body, name=name, grid=(lp // tr,),
        in_specs=[pl.BlockSpec((tr, POOL_E), lambda i: (i, 0)),
                  pl.BlockSpec((h, POOL_E), lambda i: (jnp.maximum(i * hb - 1, 0), 0))],
        out_specs=pl.BlockSpec((tr, POOL_E), lambda i: (i, 0)),
        out_shape=jax.ShapeDtypeStruct((lp, POOL_E), BF16),
        compiler_params=_params("parallel"))(p, p)


def pool_bwd(dmix, dp, *, name):
    lp = dmix.shape[0]
    tr = lp // 16
    h = POOL_HALO
    hb = tr // h
    nt = lp // tr

    def body(dm_ref, dmn_ref, _, du_ref):
        i = pl.program_id(0)
        dm = dm_ref[...]
        dmn = jnp.where(i < nt - 1, dmn_ref[...], 0.0)
        n_rows = tr + h
        for k, w in enumerate(POOL_WINDOWS):
            cols = slice(k * POOL_GROUP, (k + 1) * POOL_GROUP)
            s = jnp.concatenate([dm[:, cols] / _pool_counts(i, tr, tr, 0, w),
                                 dmn[:, cols] / _pool_counts(i, tr, h, tr, w)], axis=0)
            d = 1
            while d < w:
                s = s + pltpu.roll(s, n_rows - d, 0)
                d *= 2
            du_ref[:, cols] = (s[:tr] - dm[:, cols]).astype(du_ref.dtype)

    return pl.pallas_call(
        body, name=name, grid=(nt,),
        in_specs=[pl.BlockSpec((tr, POOL_E), lambda i: (i, 0)),
                  pl.BlockSpec((h, POOL_E), lambda i: (jnp.minimum((i + 1) * hb, lp // h - 1), 0)),
                  pl.BlockSpec(memory_space=pl.ANY)],
        out_specs=pl.BlockSpec((tr, POOL_E), lambda i: (i, 0)),
        out_shape=jax.ShapeDtypeStruct(dp.shape, dp.dtype),
        input_output_aliases={2: 0},
        compiler_params=_params("parallel"))(dmix, dmix, dp)


def _adamw_math(w, g, m, v):
    nm = ADAM_B1 * m + (1.0 - ADAM_B1) * g
    nv = ADAM_B2 * v + (1.0 - ADAM_B2) * jnp.square(g)
    m_hat = nm / (1.0 - ADAM_B1 ** ADAM_STEP)
    v_hat = nv / (1.0 - ADAM_B2 ** ADAM_STEP)
    return -ADAM_LR * (m_hat / (jnp.sqrt(v_hat) + ADAM_EPS) + ADAM_WD * w), nm, nv


def adamw_many(quads, *, name, steps=8, job=None):
    n = len(quads)

    def spec(shape):
        return pl.BlockSpec((shape[0] // steps,) + shape[1:], lambda i, nd=len(shape): (i,) + (0,) * (nd - 1))

    def body(*refs):
        for q in range(n):
            w_ref, g_ref, m_ref, v_ref = refs[4 * q:4 * q + 4]
            d_ref, nm_ref, nv_ref = refs[4 * n + 3 * q:4 * n + 3 * q + 3]
            d_ref[...], nm_ref[...], nv_ref[...] = _adamw_math(w_ref[...], g_ref[...], m_ref[...], v_ref[...])

    res = _call(
        body, name=name, grid=(steps,), in_specs=[spec(quad[0].shape) for quad in quads for _ in range(4)],
        out_specs=[spec(quad[0].shape) for quad in quads for _ in range(3)],
        out_shape=[jax.ShapeDtypeStruct(quad[0].shape, F32) for quad in quads for _ in range(3)],
        args=[a for quad in quads for a in quad], semantics=("parallel",), job=job)
    outs = res if job is None else res[0]
    triples = [tuple(outs[3 * q:3 * q + 3]) for q in range(n)]
    return triples if job is None else (triples, res[1])


WIRE_WIDTH = 1024
WIRE_ROWS = 1024


def _wire_plan(shapes):
    starts, row = [], 0
    for s in shapes:
        r, c = (1, s[0]) if len(s) == 1 else s
        starts.append(row)
        row += -(-(r * max(1, c // WIRE_WIDTH)) // 8) * 8
    assert row <= WIRE_ROWS, row
    return starts


def _wire_cells(shape):
    if len(shape) == 1:
        n = shape[0]
        if n <= WIRE_WIDTH:
            return [(0, n, (slice(None),))]
        return [(h, WIRE_WIDTH, (slice(h * WIRE_WIDTH, (h + 1) * WIRE_WIDTH),)) for h in range(n // WIRE_WIDTH)]
    r, c = shape
    if c <= WIRE_WIDTH:
        return [(None, c, (slice(None), slice(None)))]
    per = c // WIRE_WIDTH
    return [(j * per + h, WIRE_WIDTH, (j, slice(h * WIRE_WIDTH, (h + 1) * WIRE_WIDTH)))
            for j in range(r) for h in range(per)]


def pack_small(arrays, *, name):
    shapes = [a.shape for a in arrays]
    starts = _wire_plan(shapes)

    def body(*refs):
        out = refs[-1]
        out[...] = jnp.zeros_like(out)
        for ref, shape, row0 in zip(refs[:-1], shapes, starts, strict=True):
            for off, cols, idx in _wire_cells(shape):
                if off is None:
                    out[row0:row0 + shape[0], 0:cols] = ref[idx]
                else:
                    out[row0 + off, 0:cols] = ref[idx]

    return pl.pallas_call(body, name=name, out_shape=jax.ShapeDtypeStruct((WIRE_ROWS, WIRE_WIDTH), F32),
                          compiler_params=pltpu.CompilerParams(vmem_limit_bytes=VMEM_LIMIT))(*arrays)


def update_small(packed, starts, triples, shard_cells, chip, *, name):
    n = len(triples)

    def body(chip_ref, p_ref, *refs):
        k = chip_ref[0]
        ins, outs = refs[:3 * n], refs[3 * n:]

        def cut(row0, rows, cols):
            c0, c1 = cols
            if isinstance(rows, int):
                return p_ref[row0 + rows, c0:c1]
            return p_ref[row0 + rows[0]:row0 + rows[1], c0:c1]

        for q in range(n):
            w_ref, m_ref, v_ref = ins[3 * q:3 * q + 3]
            g_ref, d_ref, nm_ref, nv_ref = outs[4 * q:4 * q + 4]
            shape = w_ref.shape
            if shard_cells[q] is None:
                pieces = [(cut(starts[q], (0, shape[0]) if off is None else off, (0, cols)), idx)
                          for off, cols, idx in _wire_cells(shape)]
            else:
                by_chip = [shard_cells[q](kk) for kk in range(N_CHIPS)]
                pieces = []
                for i, (_, _, idx) in enumerate(by_chip[0]):
                    g = cut(starts[q], *by_chip[0][i][:2])
                    for kk in range(1, N_CHIPS):
                        g = jnp.where(k == kk, cut(starts[q], *by_chip[kk][i][:2]), g)
                    pieces.append((g, idx))
            for g, idx in pieces:
                g_ref[idx] = g
                d_ref[idx], nm_ref[idx], nv_ref[idx] = _adamw_math(w_ref[idx], g, m_ref[idx], v_ref[idx])

    whole = lambda a: pl.BlockSpec(a.shape, lambda i, c_ref, nd=a.ndim: (0,) * nd)
    flat = [a for t in triples for a in t]
    out_shape = [jax.ShapeDtypeStruct(t[0].shape, F32) for t in triples for _ in range(4)]
    res = pl.pallas_call(
        body, name=name,
        grid_spec=pltpu.PrefetchScalarGridSpec(
            num_scalar_prefetch=1, grid=(1,),
            in_specs=[whole(packed)] + [whole(a) for a in flat],
            out_specs=[pl.BlockSpec(s.shape, lambda i, c_ref, nd=len(s.shape): (0,) * nd) for s in out_shape]),
        out_shape=out_shape,
        compiler_params=_params("arbitrary"))(chip.reshape(1).astype(jnp.int32), packed, *flat)
    return [tuple(res[4 * q:4 * q + 4]) for q in range(n)]


ANY = pl.BlockSpec(memory_space=pl.ANY)


def _place():
    x, y, c = lax.axis_index("x"), lax.axis_index("y"), lax.axis_index("c")
    return x, y, c, [(1 - x, y), (x, 1 - y), (1 - x, 1 - y)]


DMA_PIECE_BYTES = 8192 * 1024


def _pieces(src, dst):
    shape = src.shape
    rows, cols = shape[-2], shape[-1]
    item = jnp.dtype(src.dtype).itemsize
    unit = 32 // item
    want = max(1, (rows * cols * item) // DMA_PIECE_BYTES)
    n = 1
    if rows % unit == 0:
        n = max(d for d in range(1, rows // unit + 1) if (rows // unit) % d == 0 and d <= want)
    size = rows // n
    out = []
    for lead in (range(shape[0]) if len(shape) == 3 else [None]):
        for i in range(n):
            sel = (pl.ds(i * size, size), slice(None)) if lead is None else (lead, pl.ds(i * size, size), slice(None))
            out.append((src.at[sel], dst.at[sel]))
    return out


def _send(src, dst, send_sem, recv_sem, peer, start=True):
    if start:
        for s, d in _pieces(src, dst):
            pltpu.make_async_remote_copy(src_ref=s, dst_ref=d, send_sem=send_sem, recv_sem=recv_sem,
                                         device_id=peer, device_id_type=MESH).start()
    return pltpu.make_async_remote_copy(src_ref=src, dst_ref=dst, send_sem=send_sem, recv_sem=recv_sem,
                                        device_id=peer, device_id_type=MESH)


def run_job(job, *, name):
    n_in, n_out = len(job.arrays), len(job.out_shapes)

    def body(*refs):
        job.fn(refs[:n_in], refs[n_in:n_in + n_out], refs[n_in + n_out:], "both")

    return pl.pallas_call(body, name=name, in_specs=[ANY] * n_in, out_specs=[ANY] * n_out,
                          out_shape=job.out_shapes, scratch_shapes=job.sems,
                          input_output_aliases={a: a for a in job.aliased()})(*job.arrays)


def gather_chips(shards, *, name):
    n = len(shards)

    def body(*refs):
        ins, outs = refs[:n], refs[n:2 * n]
        send_sems, recv_sems = refs[2 * n:]
        x, y, c, chips = _place()
        k = 2 * x + y
        copies = []
        for a in range(n):
            for j, peer in enumerate([(px, py, c) for px, py in chips] + [(x, y, 1 - c)]):
                copies.append(_send(ins[a], outs[a].at[k], send_sems.at[a, j], recv_sems.at[a, j], peer))
        for cp in copies:
            cp.wait()

    return pl.pallas_call(
        body, name=name, in_specs=[ANY] * n, out_specs=[ANY] * n,
        out_shape=[jax.ShapeDtypeStruct((N_CHIPS,) + s.shape, s.dtype) for s in shards],
        scratch_shapes=[pltpu.SemaphoreType.DMA((n, 4)), pltpu.SemaphoreType.DMA((n, 4))])(*shards)


def gather_by_halves_job(shards):
    n = len(shards)

    def fn(ins, outs, sems, phase):
        send_sems, recv_sems = sems
        x, y, c, chips = _place()
        k = 2 * x + y
        sibling = (x, y, 1 - c)
        begin = phase != "finish"
        waits, arrivals = [], []
        for a in range(n):
            half = ins[a].shape[0] // 2
            waits.append(_send(ins[a], outs[a].at[k], send_sems.at[a, 6], recv_sems.at[a, 6], sibling, begin))
            mine = pl.ds(c * half, half)
            for j, (px, py) in enumerate(chips):
                arrivals.append(_send(ins[a].at[mine, :], outs[a].at[k, mine, :], send_sems.at[a, j],
                                      recv_sems.at[a, j], (px, py, c), begin))
        if phase == "start":
            return
        i = 0
        for a in range(n):
            half = ins[a].shape[0] // 2
            mine = pl.ds(c * half, half)
            for j, (px, py) in enumerate(chips):
                arrivals[i].wait_recv()
                landed = outs[a].at[2 * px + py, mine, :]
                waits.append(_send(landed, landed, send_sems.at[a, 3 + j], recv_sems.at[a, 3 + j], sibling))
                i += 1
        for cp in arrivals:
            cp.wait_send()
        for cp in waits:
            cp.wait()

    return Job(shards, [jax.ShapeDtypeStruct((N_CHIPS,) + s.shape, s.dtype) for s in shards],
               [pltpu.SemaphoreType.DMA((n, 7)), pltpu.SemaphoreType.DMA((n, 7))], fn)


def gather_halves_job(shards):
    n = len(shards)

    def fn(ins, outs, sems, phase):
        send_sems, recv_sems = sems
        x, y, c, chips = _place()
        k = 2 * x + y
        copies = []
        for a in range(n):
            half = ins[a].shape[0] // 2
            mine = pl.ds(c * half, half)
            copies.append(_send(ins[a], outs[a].at[k], send_sems.at[a, 3], recv_sems.at[a, 3], (x, y, 1 - c),
                                phase != "finish"))
            for j, (px, py) in enumerate(chips):
                copies.append(_send(ins[a].at[mine, :], outs[a].at[k, mine, :], send_sems.at[a, j],
                                    recv_sems.at[a, j], (px, py, c), phase != "finish"))
        if phase != "start":
            for cp in copies:
                cp.wait()

    return Job(shards, [jax.ShapeDtypeStruct((N_CHIPS,) + s.shape, s.dtype) for s in shards],
               [pltpu.SemaphoreType.DMA((n, 4)), pltpu.SemaphoreType.DMA((n, 4))], fn)


def forward_halves_job(gathered):
    n = len(gathered)

    def fn(ins, outs, sems, phase):
        send_sems, recv_sems = sems
        x, y, c, chips = _place()
        copies = []
        for a in range(n):
            half = outs[a].shape[1] // 2
            for j, (px, py) in enumerate(chips):
                landed = outs[a].at[2 * px + py, pl.ds(c * half, half), :]
                copies.append(_send(landed, landed, send_sems.at[a, j], recv_sems.at[a, j], (x, y, 1 - c),
                                    phase != "finish"))
        if phase != "start":
            for cp in copies:
                cp.wait()

    return Job(gathered, [jax.ShapeDtypeStruct(g.shape, g.dtype) for g in gathered],
               [pltpu.SemaphoreType.DMA((n, 3)), pltpu.SemaphoreType.DMA((n, 3))], fn, in_place=True)


def swap_job(grads):
    n = len(grads)

    def fn(ins, outs, sems, phase):
        send_sems, recv_sems = sems
        x, y, c, _ = _place()
        copies = []
        for a in range(n):
            half = ins[a].shape[1] // 2
            copies.append(_send(ins[a].at[:, pl.ds((1 - c) * half, half), :], outs[a], send_sems.at[a],
                                recv_sems.at[a], (x, y, 1 - c), phase != "finish"))
        if phase != "start":
            for cp in copies:
                cp.wait()

    return Job(grads, [jax.ShapeDtypeStruct((g.shape[0], g.shape[1] // 2, g.shape[2]), g.dtype) for g in grads],
               [pltpu.SemaphoreType.DMA((n,)), pltpu.SemaphoreType.DMA((n,))], fn)


def _chip_slot(s, k):
    rel = s ^ k
    return (rel >> 1) | ((rel & 1) << 1)


def sum_cores(g, theirs, ck, out_dtype, *, name):
    n, r, cols = g.shape
    half = r // 2
    tr = _tile(half, max(16, (1 << 19) // cols), 16)
    nb = half // tr

    def body(ck_ref, g_ref, t_ref, o_ref):
        o_ref[...] = (g_ref[...] + t_ref[...]).astype(o_ref.dtype)

    return pl.pallas_call(
        body, name=name,
        grid_spec=pltpu.PrefetchScalarGridSpec(
            num_scalar_prefetch=1, grid=(n, nb),
            in_specs=[pl.BlockSpec((None, tr, cols), lambda s, i, ck_ref: (s, ck_ref[0] * nb + i, 0)),
                      pl.BlockSpec((None, tr, cols), lambda s, i, ck_ref: (s, i, 0))],
            out_specs=pl.BlockSpec((None, tr, cols), lambda s, i, ck_ref: (_chip_slot(s, ck_ref[1]), i, 0))),
        out_shape=jax.ShapeDtypeStruct((n, half, cols), out_dtype),
        compiler_params=_params("parallel", "parallel"))(ck, g, theirs)


def scatter_job(parts):
    n = len(parts)

    def fn(ins, outs, sems, phase):
        send_sems, recv_sems = sems
        x, y, c, chips = _place()
        copies = []
        for a in range(n):
            for j, (px, py) in enumerate(chips):
                copies.append(_send(ins[a].at[1 + j], outs[a].at[j], send_sems.at[a, j], recv_sems.at[a, j],
                                    (px, py, c), phase != "finish"))
        if phase != "start":
            for cp in copies:
                cp.wait()

    return Job(parts, [jax.ShapeDtypeStruct((3,) + p.shape[1:], p.dtype) for p in parts],
               [pltpu.SemaphoreType.DMA((n, 3)), pltpu.SemaphoreType.DMA((n, 3))], fn)


def sum_chips(own, others, ck, *, name):
    _, r, cols = own.shape
    tr = _tile(r, max(16, (1 << 19) // cols), 16)

    def body(ck_ref, a_ref, b0_ref, b1_ref, b2_ref, o_ref):
        o_ref[...] = (a_ref[...].astype(F32) + b0_ref[...].astype(F32) + b1_ref[...].astype(F32)
                      + b2_ref[...].astype(F32))

    other = lambda j: pl.BlockSpec((None, tr, cols), lambda i, ck_ref: (j, i, 0))
    return pl.pallas_call(
        body, name=name,
        grid_spec=pltpu.PrefetchScalarGridSpec(
            num_scalar_prefetch=1, grid=(r // tr,),
            in_specs=[pl.BlockSpec((None, tr, cols), lambda i, ck_ref: (0, i, 0)), other(0), other(1), other(2)],
            out_specs=pl.BlockSpec((None, tr, cols), lambda i, ck_ref: (ck_ref[0], i, 0))),
        out_shape=jax.ShapeDtypeStruct((2, r, cols), F32),
        compiler_params=_params("parallel"))(ck, own, others, others, others)


def share_job(joined):
    n = len(joined)

    def fn(ins, outs, sems, phase):
        send_sems, recv_sems = sems
        x, y, c, _ = _place()
        copies = [_send(outs[a].at[c], outs[a].at[c], send_sems.at[a], recv_sems.at[a], (x, y, 1 - c),
                        phase != "finish") for a in range(n)]
        if phase != "start":
            for cp in copies:
                cp.wait()

    return Job(joined, [jax.ShapeDtypeStruct(j.shape, j.dtype) for j in joined],
               [pltpu.SemaphoreType.DMA((n,)), pltpu.SemaphoreType.DMA((n,))], fn, in_place=True)


def kernel(x, meta_tokens, norm0_g, l0_w_in, l0_lam_re, l0_lam_im, l0_log_dt, l0_b_re, l0_b_im, l0_c_re, l0_c_im, l0_d_skip, l0_w_glu, l0_b_glu, l0_w_out, norm1_g, l1_w_in, l1_conv_w, l1_conv_b, l1_w_out, norm2_g, l2_w_in, l2_w_grp, l2_b_grp, l2_scale, l2_w_out, norm3_g, l3_w_in, l3_lam_re, l3_lam_im, l3_log_dt, l3_b_re, l3_b_im, l3_c_re, l3_c_im, l3_d_skip, l3_w_glu, l3_b_glu, l3_w_out, final_g, loss_target, m_meta_tokens, m_norm0_g, m_l0_w_in, m_l0_lam_re, m_l0_lam_im, m_l0_log_dt, m_l0_b_re, m_l0_b_im, m_l0_c_re, m_l0_c_im, m_l0_d_skip, m_l0_w_glu, m_l0_b_glu, m_l0_w_out, m_norm1_g, m_l1_w_in, m_l1_conv_w, m_l1_conv_b, m_l1_w_out, m_norm2_g, m_l2_w_in, m_l2_w_grp, m_l2_b_grp, m_l2_scale, m_l2_w_out, m_norm3_g, m_l3_w_in, m_l3_lam_re, m_l3_lam_im, m_l3_log_dt, m_l3_b_re, m_l3_b_im, m_l3_c_re, m_l3_c_im, m_l3_d_skip, m_l3_w_glu, m_l3_b_glu, m_l3_w_out, m_final_g, v_meta_tokens, v_norm0_g, v_l0_w_in, v_l0_lam_re, v_l0_lam_im, v_l0_log_dt, v_l0_b_re, v_l0_b_im, v_l0_c_re, v_l0_c_im, v_l0_d_skip, v_l0_w_glu, v_l0_b_glu, v_l0_w_out, v_norm1_g, v_l1_w_in, v_l1_conv_w, v_l1_conv_b, v_l1_w_out, v_norm2_g, v_l2_w_in, v_l2_w_grp, v_l2_b_grp, v_l2_scale, v_l2_w_out, v_norm3_g, v_l3_w_in, v_l3_lam_re, v_l3_lam_im, v_l3_log_dt, v_l3_b_re, v_l3_b_im, v_l3_c_re, v_l3_c_im, v_l3_d_skip, v_l3_w_glu, v_l3_b_glu, v_l3_w_out, v_final_g):
    given = dict(locals())
    w = {n: given[n] for n in WEIGHTS}
    mom_m = {n: given["m_" + n] for n in WEIGHTS}
    mom_v = {n: given["v_" + n] for n in WEIGHTS}
    seq = x.shape[1]
    real = N_META + seq
    lp = -(-real // SCAN_ROWS) * SCAN_ROWS
    chip = 2 * lax.axis_index("x") + lax.axis_index("y")
    row = lambda a: a.reshape(1, -1)

    shard_bf16 = {n: (w[n].reshape(-1, w[n].shape[-1]) if n == 'l2_w_grp' else w[n]).astype(BF16) for n in BIG}
    layer = lambda i: [n for n in BIG if n.startswith("l%d_" % i)]
    over_ici = lambda names: gather_halves_job([shard_bf16[n] for n in names])
    full = {}

    def landed(names, arrays):
        for n, arr in zip(names, arrays, strict=True):
            if n.endswith('w_in'):
                full[n] = arr
            elif n == 'l2_w_grp':
                full[n] = arr.reshape(N_CHIPS, len(POOL_WINDOWS), POOL_GROUP // N_CHIPS, POOL_GROUP)
            else:
                full[n] = arr.reshape(1, -1, arr.shape[-1])

    gip = lambda a: jnp.swapaxes(a, 1, 2)
    prm_of = lambda pre: [w[pre + 'lam_re'], w[pre + 'lam_im'], w[pre + 'log_dt'].reshape(-1, 1),
                          gip(w[pre + 'b_re']), gip(w[pre + 'b_im']), w[pre + 'c_re'], w[pre + 'c_im']]
    prepared = {}
    prepared[3], halves = s5_prep(prm_of('l3_'), name="l3_prep", job=over_ici(['l0_w_in']))
    prepared[0], whole = s5_prep(prm_of('l0_'), name="l0_prep", job=forward_halves_job(halves))
    landed(['l0_w_in'], whole)
    full.update(zip(SMALL_SHARDED, gather_chips([w[n] for n in SMALL_SHARDED], name="gather_small_shards"),
                    strict=True))
    meta_full = full['meta_tokens'].transpose(1, 0, 2).reshape(N_META, D_MODEL)
    conv_w_full = full['l1_conv_w'].transpose(1, 0, 2).reshape(3, CONV_E)
    b_grp_full = full['l2_b_grp'].transpose(1, 0, 2).reshape(len(POOL_WINDOWS), POOL_GROUP)

    h0 = jnp.concatenate([meta_full, x[0], jnp.zeros((lp - real, D_MODEL), F32)], axis=0)
    target = jnp.concatenate([jnp.zeros((N_META, D_MODEL), F32), loss_target[0],
                              jnp.zeros((lp - real, D_MODEL), F32)], axis=0)
    grads = {}
    saved = {}

    def s5_layer_fwd(i, h, first=False):
        pre = "l%d_" % i
        prm = prm_of(pre)
        a_re, a_im, bblk, cblk = prepared[i]
        tables = (a_re.reshape(1, -1), a_im.reshape(1, -1), bblk, cblk)
        ar, ai = tables[:2]
        gain = row(w["norm%d_g" % i])
        rest = [pre + 'w_glu', pre + 'w_out']
        if not first:
            uz, n = norm_mm_nn(h, gain, full[pre + 'w_in'], name=pre + "in")
            y0, yg, sr, si = s5_fwd(uz, bblk, cblk, ar, ai, row(w[pre + 'd_skip']), name=pre + "scan")
            q, y3 = glu_gate(yg, full[pre + 'w_glu'], row(w[pre + 'b_glu']), y0, uz, name=pre + "glu")
        else:
            (uz, n), halves = norm_mm_nn(h, gain, full[pre + 'w_in'], name=pre + "in", job=over_ici(rest))
            (y0, yg, sr, si), arrived = s5_fwd(
                uz, bblk, cblk, ar, ai, row(w[pre + 'd_skip']), name=pre + "scan",
                job=Job.both(over_ici(layer(1)), forward_halves_job(halves)))
            landed(rest, arrived[len(layer(1)):])
            (q, y3), whole = glu_gate(yg, full[pre + 'w_glu'], row(w[pre + 'b_glu']), y0, uz, name=pre + "glu",
                                      job=forward_halves_job(arrived[:len(layer(1))]))
            landed(layer(1), whole)
        h_new = mm_nn(y3, full[pre + 'w_out'], add=h, name=pre + "out")
        saved[i] = dict(h=h, n=n, uz=uz, y0=y0, sr=sr, si=si, yg=yg, q=q, y3=y3, tables=tables, prm=prm)
        return h_new

    def s5_layer_bwd(i, dh, carry=None):
        pre = "l%d_" % i
        s = saved[i]
        ar, ai, bblk, cblk = s['tables']
        grads[pre + 'w_out'] = mm_tn(s['y3'], dh, 1, name=pre + "d_w_out")[0]
        dy0_direct, dq, dp, db_glu = s5_gate_bwd(dh, full[pre + 'w_out'], s['y0'], s['q'], s['uz'],
                                                 name=pre + "d_y3")
        grads[pre + 'b_glu'] = db_glu
        grads[pre + 'w_glu'] = mm_tn(s['yg'], dq, 1, name=pre + "d_w_glu")[0]
        early = [] if carry is None else [pre + 'w_out', pre + 'w_glu']
        if carry is None:
            dyg = mm_nt(dq, full[pre + 'w_glu'], name=pre + "d_yg")
            res = s5_bwd(dy0_direct, dyg, s['y0'], s['uz'], s['sr'], s['si'], bblk, cblk, ar, ai,
                         row(w[pre + 'd_skip']), dp, name=pre + "d_scan")
        else:
            pieces = [chip_major(n) for n in early]
            dyg, theirs = mm_nt(dq, full[pre + 'w_glu'], name=pre + "d_yg", job=swap_job(pieces))
            sum_over_cores(early, pieces, theirs)
            res, arrived = s5_bwd(dy0_direct, dyg, s['y0'], s['uz'], s['sr'], s['si'], bblk, cblk, ar, ai,
                                  row(w[pre + 'd_skip']), dp, name=pre + "d_scan",
                                  job=scatter_job(scatter_of(carry) + [pair[n] for n in early]))
            from_chips.update(zip(layer(carry) + early, arrived, strict=True))
        dp, dbblk, dcblk, dar, dai, dd = res
        grads[pre + 'd_skip'] = dd
        cots = (dar.reshape(S5_GROUPS, S5_STATE), dai.reshape(S5_GROUPS, S5_STATE), dbblk, dcblk)
        for key, val in zip(('lam_re', 'lam_im', 'log_dt', 'b_re', 'b_im', 'c_re', 'c_im'),
                            s5_prep_bwd(s['prm'], cots, name=pre + "d_prep"), strict=True):
            grads[pre + key] = val.reshape(-1) if key == 'log_dt' else val
        if carry is None:
            grads[pre + 'w_in'] = mm_tn(s['n'], dp, N_CHIPS, name=pre + "d_w_in")
        else:
            later = [n for n in BIG if not n.startswith(pre)] + early
            sums = [sum_chips(pair[n], from_chips[n], ck, name="sum_chips_" + n) for n in later]
            grads[pre + 'w_in'], shared = mm_tn(s['n'], dp, N_CHIPS, name=pre + "d_w_in", job=share_job(sums))
            joined.update(zip(later, shared, strict=True))
        names = [n for n in layer(i) if n not in early]
        pieces = [chip_major(n) for n in names]
        (dh_in, dg), theirs = mm_nt_norm_bwd(dp, full[pre + 'w_in'], s['h'], row(w["norm%d_g" % i]), dh,
                                             name=pre + "d_n", job=swap_job(pieces))
        sum_over_cores(names, pieces, theirs)
        grads["norm%d_g" % i] = dg
        return dh_in

    h1 = s5_layer_fwd(0, h0, first=True)

    (p1, n1), halves = norm_mm_nn(h1, row(w['norm1_g']), full['l1_w_in'], name="l1_in", job=over_ici(layer(2)))
    y3_1 = conv_fwd(p1, conv_w_full, row(w['l1_conv_b']), name="l1_conv")
    h2, whole = mm_nn(y3_1, full['l1_w_out'], add=h1, name="l1_out", job=forward_halves_job(halves))
    landed(layer(2), whole)

    (p2, n2), halves = norm_mm_nn(h2, row(w['norm2_g']), full['l2_w_in'], name="l2_in", job=over_ici(layer(3)))
    mixed = pool_fwd(p2, name="l2_pool")
    o2, y3_2 = grp_nn_gate(mixed, full['l2_w_grp'], b_grp_full.reshape(1, -1), p2, row(w['l2_scale']), name="l2_grp")
    h3, whole = mm_nn(y3_2, full['l2_w_out'], add=h2, name="l2_out", job=forward_halves_job(halves))
    landed(layer(3), whole)

    h4 = s5_layer_fwd(3, h3)

    dh, grads['final_g'], sq = final_loss(h4, target, row(w['final_g']), N_META, real, name="loss")
    loss = lax.psum(0.5 / D_MODEL * jnp.sum(sq), ("x", "y", "c"))

    ck = jnp.stack([lax.axis_index("c"), chip]).astype(jnp.int32)
    pair, from_chips, joined = {}, {}, {}

    def chip_major(n):
        g = grads[n]
        if n.endswith('w_in'):
            return g
        if n == 'l2_w_grp':
            return g.reshape(N_CHIPS, -1, POOL_GROUP)
        return g.reshape(N_CHIPS, -1, g.shape[-1])

    def core_parts(i):
        return layer(i), [chip_major(n) for n in layer(i)]

    def sum_over_cores(names, pieces, theirs):
        for n, g, t in zip(names, pieces, theirs, strict=True):
            pair[n] = sum_cores(g, t, ck, F32 if n == 'small' else BF16, name="sum_cores_" + n)

    scatter_of = lambda i: [pair[n] for n in layer(i)]

    def scattered(i, arrays):
        from_chips.update(zip(layer(i), arrays, strict=True))

    dh = s5_layer_bwd(3, dh)

    grads['l2_w_out'] = mm_tn(y3_2, dh, 1, name="l2_d_w_out")[0]
    do2, dp2, dscale, dbgrp = pool_gate_bwd(dh, full['l2_w_out'], o2, p2, row(w['l2_scale']), name="l2_d_y3")
    grads['l2_scale'] = dscale
    grads['l2_b_grp'] = dbgrp
    grads['l2_w_grp'] = grp_tn(mixed, do2, name="l2_d_w_grp")
    dmix = grp_nt(do2, full['l2_w_grp'], name="l2_d_mixed")
    dp2 = pool_bwd(dmix, dp2, name="l2_d_pool")
    grads['l2_w_in'], arrived = mm_tn(n2, dp2, N_CHIPS, name="l2_d_w_in", job=scatter_job(scatter_of(3)))
    scattered(3, arrived)
    names, pieces = core_parts(2)
    (dh, dg), theirs = mm_nt_norm_bwd(dp2, full['l2_w_in'], h2, row(w['norm2_g']), dh, name="l2_d_n",
                                      job=swap_job(pieces))
    sum_over_cores(names, pieces, theirs)
    grads['norm2_g'] = dg

    grads['l1_w_out'] = mm_tn(y3_1, dh, 1, name="l1_d_w_out")[0]
    dp1, dcw, dcb = conv_bwd(dh, full['l1_w_out'], p1, conv_w_full, row(w['l1_conv_b']), name="l1_d_conv")
    grads['l1_conv_w'] = dcw
    grads['l1_conv_b'] = dcb
    grads['l1_w_in'], arrived = mm_tn(n1, dp1, N_CHIPS, name="l1_d_w_in", job=scatter_job(scatter_of(2)))
    scattered(2, arrived)
    names, pieces = core_parts(1)
    (dh, dg), theirs = mm_nt_norm_bwd(dp1, full['l1_w_in'], h1, row(w['norm1_g']), dh, name="l1_d_n",
                                      job=swap_job(pieces))
    sum_over_cores(names, pieces, theirs)
    grads['norm1_g'] = dg

    dh = s5_layer_bwd(0, dh, carry=1)
    grad_x = dh[N_META:real][None]
    grads['meta_tokens'] = dh[:N_META]

    wide = [n for n in SMALL if w[n].ndim == 3]
    wire = [grads[n].reshape(S5_GROUPS, -1) if n in wide else grads[n] for n in SMALL]
    starts = dict(zip(SMALL, _wire_plan([a.shape for a in wire]), strict=True))
    small_local = pack_small(wire, name="pack_small")
    pieces = [small_local.reshape(N_CHIPS, -1, WIRE_WIDTH)]
    sum_over_cores(['small'], pieces, run_job(swap_job(pieces), name="reduce_cores_small"))

    out_g, out_d, out_m, out_v = {}, {}, {}, {}
    as2d = lambda a: a.reshape(-1, a.shape[-1])

    def update_big(names, name, job=None):
        for n in names:
            out_g[n] = joined[n].reshape(w[n].shape)
        res = adamw_many([(as2d(w[n]), as2d(out_g[n]), as2d(mom_m[n]), as2d(mom_v[n])) for n in names],
                         name=name, job=job)
        triples, carried = res if job is not None else (res, None)
        for n, (d, nm, nv) in zip(names, triples, strict=True):
            out_d[n], out_m[n], out_v[n] = d.reshape(w[n].shape), nm.reshape(w[n].shape), nv.reshape(w[n].shape)
        return carried

    rest = ['l0_w_in', 'small']
    later = [n for n in BIG if n not in rest]
    from_chips.update(zip(rest, update_big(later, "adamw_big", scatter_job([pair[n] for n in rest])), strict=True))
    sums = [sum_chips(pair[n], from_chips[n], ck, name="sum_chips_" + n) for n in rest]
    joined.update(zip(rest, run_job(share_job(sums), name="share_cores"), strict=True))
    update_big(['l0_w_in'], "adamw_l0_w_in")
    (small_all,) = run_job(gather_by_halves_job([joined['small'].reshape(-1, WIRE_WIDTH)]), name="gather_small")
    small_all = small_all.reshape(WIRE_ROWS, WIRE_WIDTH)

    half_w = WIRE_WIDTH // 2
    shard_cells = {
        'meta_tokens': lambda k: [((0, N_META), (k * 256, (k + 1) * 256), (slice(None), slice(None)))],
        'l1_conv_w': lambda k: [(2 * j + k // 2, ((k % 2) * half_w, (k % 2 + 1) * half_w), (j, slice(None)))
                                for j in range(3)],
        'l2_b_grp': lambda k: [(j // 2, ((j % 2) * half_w + k * 128, (j % 2) * half_w + (k + 1) * 128),
                                (j, slice(None))) for j in range(len(POOL_WINDOWS))],
    }
    plain = [n for n in SMALL if n not in wide]
    res = update_small(small_all, [starts[n] for n in plain], [(w[n], mom_m[n], mom_v[n]) for n in plain],
                       [shard_cells.get(n) for n in plain], chip, name="adamw_small")
    for n, (g, d, nm, nv) in zip(plain, res, strict=True):
        out_g[n], out_d[n], out_m[n], out_v[n] = g, d, nm, nv
    as_gip = lambda n, a: gip(a) if n.endswith(('b_re', 'b_im')) else a
    g_gip = {n: small_all[starts[n]:starts[n] + S5_GROUPS].reshape(S5_GROUPS, S5_GROUP, S5_STATE) for n in wide}
    res = adamw_many([(as_gip(n, w[n]), g_gip[n], as_gip(n, mom_m[n]), as_gip(n, mom_v[n])) for n in wide],
                     name="adamw_s5_tensors")
    for n, (d, nm, nv) in zip(wide, res, strict=True):
        out_g[n], out_d[n], out_m[n], out_v[n] = as_gip(n, g_gip[n]), as_gip(n, d), as_gip(n, nm), as_gip(n, nv)

    return (loss, grad_x, *[out_g[n] for n in WEIGHTS], *[out_d[n] for n in WEIGHTS],
            *[out_m[n] for n in WEIGHTS], *[out_v[n] for n in WEIGHTS])
```
